```python
import jax, jax.numpy as jnp
from jax import lax
import numpy as np

D_MODEL = 1024
BATCH = 8
SEQ = 4096
DEPTH = 1

SSM_EXPAND = 2
SSM_D_INNER = SSM_EXPAND * D_MODEL
SSM_HEAD_DIM = 64
SSM_N_HEADS = SSM_D_INNER // SSM_HEAD_DIM
SSM_N_GROUPS = 8
SSM_D_STATE = 128
SSM_CHUNK = 128
SSM_CONV_DIM = SSM_D_INNER + 2 * SSM_N_GROUPS * SSM_D_STATE
GDN_HEAD_K = 128
GDN_HEAD_V = 128
GDN_N_QK_HEADS = D_MODEL // GDN_HEAD_K
GDN_N_V_HEADS = 2 * GDN_N_QK_HEADS
GDN_KEY_DIM = GDN_N_QK_HEADS * GDN_HEAD_K
GDN_VAL_DIM = GDN_N_V_HEADS * GDN_HEAD_V
GDN_CHUNK = 64
GDN_CONV_DIM = 2 * GDN_KEY_DIM + GDN_VAL_DIM
CONV_K = 4
MLP_HIDDEN = 4 * D_MODEL
EPS = 1e-6
IN_SPLIT_SIZES = (SSM_D_INNER, SSM_CONV_DIM, SSM_N_HEADS, GDN_CONV_DIM, GDN_VAL_DIM,
                  GDN_N_V_HEADS, GDN_N_V_HEADS, D_MODEL, D_MODEL)
IN_PROJ_DIM = sum(IN_SPLIT_SIZES)

kernel_name = "hybrid_ssd_gdn_sandwich_adaln_block"


def rmsnorm(x, w):
    xf = x.astype(jnp.float32)
    y = xf * lax.rsqrt(jnp.mean(xf * xf, axis=-1, keepdims=True) + EPS)
    return (y * w.astype(jnp.float32)).astype(x.dtype)


def l2norm(x):
    return x * lax.rsqrt(jnp.sum(x * x, axis=-1, keepdims=True) + EPS)


def causal_depthwise_conv(x, w):
    return lax.conv_general_dilated(
        x, w[:, None, :].astype(x.dtype), window_strides=(1,), padding=[(CONV_K - 1, 0)],
        dimension_numbers=('NWC', 'WIO', 'NWC'), feature_group_count=x.shape[-1])


def ssd_chunked_scan(xh, dt, A, Bm, Cm):
    Bsz, S, H, P = xh.shape
    G, N = Bm.shape[-2:]
    hg = H // G
    L = SSM_CHUNK
    nc = S // L
    xdt = jnp.moveaxis((xh * dt[..., None]).reshape(Bsz, nc, L, G, hg, P), 1, 0)
    a = jnp.moveaxis((dt * A).reshape(Bsz, nc, L, G, hg), 1, 0)
    Bc = jnp.moveaxis(Bm.reshape(Bsz, nc, L, G, N), 1, 0)
    Cc = jnp.moveaxis(Cm.reshape(Bsz, nc, L, G, N), 1, 0)
    causal = jnp.tril(jnp.ones((L, L), dtype=bool))[None, :, :, None, None]

    def step(state, inp):
        xc, ac, bc, cc = inp
        acum = jnp.cumsum(ac, axis=1)
        seg = acum[:, :, None] - acum[:, None, :]
        decay = jnp.exp(jnp.where(causal, seg, -jnp.inf))
        cb = jnp.einsum('blgn,bsgn->blsg', cc, bc)
        y_diag = jnp.einsum('blsg,blsgh,bsghp->blghp', cb, decay, xc)
        y_off = jnp.einsum('blgn,bghpn->blghp', cc, state) * jnp.exp(acum)[..., None]
        a_last = acum[:, -1]
        w_s = jnp.exp(a_last[:, None] - acum)
        new_state = state * jnp.exp(a_last)[..., None, None] + jnp.einsum(
            'bsgn,bsgh,bsghp->bghpn', bc, w_s, xc)
        return new_state, y_diag + y_off

    state0 = jnp.zeros((Bsz, G, hg, P, N), dtype=jnp.float32)
    _, y = lax.scan(step, state0, (xdt, a, Bc, Cc))
    return jnp.moveaxis(y, 0, 1).reshape(Bsz, S, H, P)


def gated_delta_rule_chunked(q, k, v, g, beta):
    Bsz, S, H, dk = q.shape
    dv = v.shape[-1]
    L = GDN_CHUNK
    nc = S // L
    q = q * (dk ** -0.5)

    def chunks(t):
        t = t.reshape((Bsz, nc, L, H) + t.shape[3:])
        return jnp.moveaxis(t, (1, 3), (0, 2))

    causal = jnp.tril(jnp.ones((L, L), dtype=bool))
    strict = jnp.tril(jnp.ones((L, L), dtype=bool), -1)
    eye = jnp.eye(L, dtype=jnp.float32)

    def step(state, inp):
        qc, kc, vc, gc, bc = inp
        gcum = jnp.cumsum(gc, axis=-1)
        dmat = jnp.exp(jnp.where(causal, gcum[..., :, None] - gcum[..., None, :], -jnp.inf))
        kb = kc * bc[..., None]
        a_low = jnp.where(strict, jnp.einsum('bhid,bhjd->bhij', kb, kc) * dmat, 0.0)
        rhs = jnp.concatenate([vc * bc[..., None], kb * jnp.exp(gcum)[..., None]], axis=-1)
        sol = lax.linalg.triangular_solve(eye + a_low, rhs, left_side=True, lower=True,
                                          unit_diagonal=True)
        u, w = sol[..., :dv], sol[..., dv:]
        attn = jnp.einsum('bhid,bhjd->bhij', qc, kc) * dmat
        v_new = u - jnp.einsum('bhlk,bhkv->bhlv', w, state)
        o = jnp.einsum('bhlk,bhkv->bhlv', qc * jnp.exp(gcum)[..., None], state) + jnp.einsum(
            'bhij,bhjv->bhiv', attn, v_new)
        g_last = gcum[..., -1]
        k_dec = kc * jnp.exp(g_last[..., None] - gcum)[..., None]
        new_state = state * jnp.exp(g_last)[..., None, None] + jnp.einsum(
            'bhlk,bhlv->bhkv', k_dec, v_new)
        return new_state, o

    state0 = jnp.zeros((Bsz, H, dk, dv), dtype=jnp.float32)
    _, o = lax.scan(step, state0, (chunks(q), chunks(k), chunks(v), chunks(g), chunks(beta)))
    return jnp.moveaxis(o, (0, 2), (1, 3)).reshape(Bsz, S, H, dv)


def mamba2_branch(z, xbc, dt_raw, conv_w, conv_b, dt_bias, A_log, d_skip, norm_w):
    f32 = jnp.float32
    Bsz, S, _ = xbc.shape
    xbc = jax.nn.silu(causal_depthwise_conv(xbc, conv_w) + conv_b)
    xs, Bm, Cm = jnp.split(xbc, [SSM_D_INNER, SSM_D_INNER + SSM_N_GROUPS * SSM_D_STATE], axis=-1)
    xh = xs.reshape(Bsz, S, SSM_N_HEADS, SSM_HEAD_DIM).astype(f32)
    dt = jax.nn.softplus(dt_raw.astype(f32) + dt_bias.astype(f32))
    A = -jnp.exp(A_log.astype(f32))
    y = ssd_chunked_scan(xh, dt, A,
                         Bm.reshape(Bsz, S, SSM_N_GROUPS, SSM_D_STATE).astype(f32),
                         Cm.reshape(Bsz, S, SSM_N_GROUPS, SSM_D_STATE).astype(f32))
    y = y + d_skip.astype(f32)[:, None] * xh
    y = y.reshape(Bsz, S, SSM_D_INNER) * jax.nn.silu(z.astype(f32))
    y = rmsnorm(y.reshape(Bsz, S, SSM_N_GROUPS, -1), norm_w.reshape(SSM_N_GROUPS, -1))
    return y.reshape(Bsz, S, SSM_D_INNER).astype(z.dtype)


def gated_deltanet_branch(qkv, z, b, a, conv_w, dt_bias, A_log, norm_w):
    f32 = jnp.float32
    Bsz, S, _ = qkv.shape
    qkv = jax.nn.silu(causal_depthwise_conv(qkv, conv_w))
    q, k, v = jnp.split(qkv, [GDN_KEY_DIM, 2 * GDN_KEY_DIM], axis=-1)
    rep = GDN_N_V_HEADS // GDN_N_QK_HEADS
    q = jnp.repeat(l2norm(q.reshape(Bsz, S, GDN_N_QK_HEADS, GDN_HEAD_K).astype(f32)), rep, axis=2)
    k = jnp.repeat(l2norm(k.reshape(Bsz, S, GDN_N_QK_HEADS, GDN_HEAD_K).astype(f32)), rep, axis=2)
    v = v.reshape(Bsz, S, GDN_N_V_HEADS, GDN_HEAD_V).astype(f32)
    beta = jax.nn.sigmoid(b.astype(f32))
    g = -jnp.exp(A_log.astype(f32)) * jax.nn.softplus(a.astype(f32) + dt_bias.astype(f32))
    o = gated_delta_rule_chunked(q, k, v, g, beta)
    o = rmsnorm(o, norm_w) * jax.nn.silu(z.reshape(Bsz, S, GDN_N_V_HEADS, GDN_HEAD_V).astype(f32))
    return o.reshape(Bsz, S, GDN_VAL_DIM).astype(z.dtype)


def _fwd_setup_inputs(seed: int = 0) -> dict:
    key = jax.random.key(seed)
    ks = jax.random.split(key, 26)
    f32 = jnp.float32
    nrm = lambda k, shape, scale: jax.random.normal(k, shape, f32) * scale
    gain = lambda k, shape: 1.0 + 0.02 * jax.random.normal(k, shape, f32)

    def inv_softplus_dt(k, shape):
        dt = jnp.exp(jax.random.uniform(k, shape, f32, np.log(1e-3), np.log(1e-1)))
        return dt + jnp.log(-jnp.expm1(-dt))

    Dm = D_MODEL
    return {
        'x': jax.random.normal(ks[0], (BATCH, SEQ, Dm), f32),
        'c': jax.random.normal(ks[1], (BATCH, Dm), f32),
        'w_ada': nrm(ks[2], (DEPTH, Dm, 6 * Dm), 0.5 * Dm ** -0.5),
        'b_ada': nrm(ks[3], (DEPTH, 6 * Dm), 0.02),
        'norm_mix_pre': gain(ks[4], (DEPTH, Dm)),
        'norm_mix_post': gain(ks[5], (DEPTH, Dm)),
        'w_in': nrm(ks[6], (DEPTH, Dm, IN_PROJ_DIM), Dm ** -0.5),
        'ssm_conv_w': nrm(ks[7], (DEPTH, CONV_K, SSM_CONV_DIM), CONV_K ** -0.5),
        'ssm_conv_b': nrm(ks[8], (DEPTH, SSM_CONV_DIM), 0.02),
        'ssm_dt_bias': inv_softplus_dt(ks[9], (DEPTH, SSM_N_HEADS)),
        'ssm_A_log': jnp.log(jax.random.uniform(ks[10], (DEPTH, SSM_N_HEADS), f32, 1.0, 16.0)),
        'ssm_D': gain(ks[11], (DEPTH, SSM_N_HEADS)),
        'ssm_norm_w': gain(ks[12], (DEPTH, SSM_D_INNER)),
        'gdn_conv_w': nrm(ks[13], (DEPTH, CONV_K, GDN_CONV_DIM), CONV_K ** -0.5),
        'gdn_dt_bias': inv_softplus_dt(ks[14], (DEPTH, GDN_N_V_HEADS)),
        'gdn_A_log': jnp.log(jax.random.uniform(ks[15], (DEPTH, GDN_N_V_HEADS), f32, 1.0, 16.0)),
        'gdn_norm_w': gain(ks[16], (DEPTH, GDN_HEAD_V)),
        'w_ssm_up': nrm(ks[17], (DEPTH, SSM_D_INNER, Dm), SSM_D_INNER ** -0.5),
        'w_gdn_up': nrm(ks[18], (DEPTH, GDN_VAL_DIM, Dm), GDN_VAL_DIM ** -0.5),
        'w_out': nrm(ks[19], (DEPTH, Dm, Dm), Dm ** -0.5),
        'norm_mlp_pre': gain(ks[20], (DEPTH, Dm)),
        'norm_mlp_post': gain(ks[21], (DEPTH, Dm)),
        'w_mlp_up': nrm(ks[22], (DEPTH, Dm, MLP_HIDDEN), Dm ** -0.5),
        'w_mlp_down': nrm(ks[23], (DEPTH, MLP_HIDDEN, Dm), MLP_HIDDEN ** -0.5),
    }


def _fwd_reference(x, c, w_ada, b_ada, norm_mix_pre, norm_mix_post, w_in, ssm_conv_w, ssm_conv_b,
              ssm_dt_bias, ssm_A_log, ssm_D, ssm_norm_w, gdn_conv_w, gdn_dt_bias, gdn_A_log,
              gdn_norm_w, w_ssm_up, w_gdn_up, w_out, norm_mlp_pre, norm_mlp_post, w_mlp_up,
              w_mlp_down):
    offsets = [int(o) for o in np.cumsum(IN_SPLIT_SIZES)[:-1]]
    c_act = jax.nn.silu(c)
    for l in range(DEPTH):
        mod = c_act @ w_ada[l] + b_ada[l]
        sh1, sc1, g1, sh2, sc2, g2 = [m[:, None, :] for m in jnp.split(mod, 6, axis=-1)]

        h = rmsnorm(x, norm_mix_pre[l]) * (1.0 + sc1) + sh1
        proj = h @ w_in[l]
        (z_ssm, xbc, dt_raw, qkv, z_gdn, b_gdn, a_gdn,
         gate_ssm, gate_gdn) = jnp.split(proj, offsets, axis=-1)
        y_ssm = mamba2_branch(z_ssm, xbc, dt_raw, ssm_conv_w[l], ssm_conv_b[l], ssm_dt_bias[l],
                              ssm_A_log[l], ssm_D[l], ssm_norm_w[l]) @ w_ssm_up[l]
        y_gdn = gated_deltanet_branch(qkv, z_gdn, b_gdn, a_gdn, gdn_conv_w[l], gdn_dt_bias[l],
                                      gdn_A_log[l], gdn_norm_w[l]) @ w_gdn_up[l]
        merged = jax.nn.sigmoid(gate_ssm) * y_ssm + jax.nn.sigmoid(gate_gdn) * y_gdn
        x = x + g1 * rmsnorm(merged @ w_out[l], norm_mix_post[l])

        h = rmsnorm(x, norm_mlp_pre[l]) * (1.0 + sc2) + sh2
        y = jnp.square(jax.nn.relu(h @ w_mlp_up[l])) @ w_mlp_down[l]
        x = x + g2 * rmsnorm(y, norm_mlp_post[l])
    return x


import jax as _jax
import jax.numpy as _jnp

TWIN_FORMAT = 'train_step'
FWD_PARAMS = ['x', 'c', 'w_ada', 'b_ada', 'norm_mix_pre', 'norm_mix_post', 'w_in', 'ssm_conv_w', 'ssm_conv_b', 'ssm_dt_bias', 'ssm_A_log', 'ssm_D', 'ssm_norm_w', 'gdn_conv_w', 'gdn_dt_bias', 'gdn_A_log', 'gdn_norm_w', 'w_ssm_up', 'w_gdn_up', 'w_out', 'norm_mlp_pre', 'norm_mlp_post', 'w_mlp_up', 'w_mlp_down']
TWIN_WEIGHTS = ['w_ada', 'b_ada', 'norm_mix_pre', 'norm_mix_post', 'w_in', 'ssm_conv_w', 'ssm_conv_b', 'ssm_dt_bias', 'ssm_A_log', 'ssm_D', 'ssm_norm_w', 'gdn_conv_w', 'gdn_dt_bias', 'gdn_A_log', 'gdn_norm_w', 'w_ssm_up', 'w_gdn_up', 'w_out', 'norm_mlp_pre', 'norm_mlp_post', 'w_mlp_up', 'w_mlp_down']
TWIN_DIFF_INPUT = 'x'
TWIN_INPUTS = ['x', 'c', 'w_ada', 'b_ada', 'norm_mix_pre', 'norm_mix_post', 'w_in', 'ssm_conv_w', 'ssm_conv_b', 'ssm_dt_bias', 'ssm_A_log', 'ssm_D', 'ssm_norm_w', 'gdn_conv_w', 'gdn_dt_bias', 'gdn_A_log', 'gdn_norm_w', 'w_ssm_up', 'w_gdn_up', 'w_out', 'norm_mlp_pre', 'norm_mlp_post', 'w_mlp_up', 'w_mlp_down', 'loss_target', 'm_w_ada', 'm_b_ada', 'm_norm_mix_pre', 'm_norm_mix_post', 'm_w_in', 'm_ssm_conv_w', 'm_ssm_conv_b', 'm_ssm_dt_bias', 'm_ssm_A_log', 'm_ssm_D', 'm_ssm_norm_w', 'm_gdn_conv_w', 'm_gdn_dt_bias', 'm_gdn_A_log', 'm_gdn_norm_w', 'm_w_ssm_up', 'm_w_gdn_up', 'm_w_out', 'm_norm_mlp_pre', 'm_norm_mlp_post', 'm_w_mlp_up', 'm_w_mlp_down', 'v_w_ada', 'v_b_ada', 'v_norm_mix_pre', 'v_norm_mix_post', 'v_w_in', 'v_ssm_conv_w', 'v_ssm_conv_b', 'v_ssm_dt_bias', 'v_ssm_A_log', 'v_ssm_D', 'v_ssm_norm_w', 'v_gdn_conv_w', 'v_gdn_dt_bias', 'v_gdn_A_log', 'v_gdn_norm_w', 'v_w_ssm_up', 'v_w_gdn_up', 'v_w_out', 'v_norm_mlp_pre', 'v_norm_mlp_post', 'v_w_mlp_up', 'v_w_mlp_down']
TWIN_OUTPUTS = ['loss', 'grad_x', 'grad_w_ada', 'grad_b_ada', 'grad_norm_mix_pre', 'grad_norm_mix_post', 'grad_w_in', 'grad_ssm_conv_w', 'grad_ssm_conv_b', 'grad_ssm_dt_bias', 'grad_ssm_A_log', 'grad_ssm_D', 'grad_ssm_norm_w', 'grad_gdn_conv_w', 'grad_gdn_dt_bias', 'grad_gdn_A_log', 'grad_gdn_norm_w', 'grad_w_ssm_up', 'grad_w_gdn_up', 'grad_w_out', 'grad_norm_mlp_pre', 'grad_norm_mlp_post', 'grad_w_mlp_up', 'grad_w_mlp_down', 'delta_w_ada', 'delta_b_ada', 'delta_norm_mix_pre', 'delta_norm_mix_post', 'delta_w_in', 'delta_ssm_conv_w', 'delta_ssm_conv_b', 'delta_ssm_dt_bias', 'delta_ssm_A_log', 'delta_ssm_D', 'delta_ssm_norm_w', 'delta_gdn_conv_w', 'delta_gdn_dt_bias', 'delta_gdn_A_log', 'delta_gdn_norm_w', 'delta_w_ssm_up', 'delta_w_gdn_up', 'delta_w_out', 'delta_norm_mlp_pre', 'delta_norm_mlp_post', 'delta_w_mlp_up', 'delta_w_mlp_down', 'new_m_w_ada', 'new_m_b_ada', 'new_m_norm_mix_pre', 'new_m_norm_mix_post', 'new_m_w_in', 'new_m_ssm_conv_w', 'new_m_ssm_conv_b', 'new_m_ssm_dt_bias', 'new_m_ssm_A_log', 'new_m_ssm_D', 'new_m_ssm_norm_w', 'new_m_gdn_conv_w', 'new_m_gdn_dt_bias', 'new_m_gdn_A_log', 'new_m_gdn_norm_w', 'new_m_w_ssm_up', 'new_m_w_gdn_up', 'new_m_w_out', 'new_m_norm_mlp_pre', 'new_m_norm_mlp_post', 'new_m_w_mlp_up', 'new_m_w_mlp_down', 'new_v_w_ada', 'new_v_b_ada', 'new_v_norm_mix_pre', 'new_v_norm_mix_post', 'new_v_w_in', 'new_v_ssm_conv_w', 'new_v_ssm_conv_b', 'new_v_ssm_dt_bias', 'new_v_ssm_A_log', 'new_v_ssm_D', 'new_v_ssm_norm_w', 'new_v_gdn_conv_w', 'new_v_gdn_dt_bias', 'new_v_gdn_A_log', 'new_v_gdn_norm_w', 'new_v_w_ssm_up', 'new_v_w_gdn_up', 'new_v_w_out', 'new_v_norm_mlp_pre', 'new_v_norm_mlp_post', 'new_v_w_mlp_up', 'new_v_w_mlp_down']
TWIN_LEAF_KINDS = {'loss': 'loss', 'grad_x': 'grad_x', 'grad_w_ada': 'grad_w', 'grad_b_ada': 'grad_w', 'grad_norm_mix_pre': 'grad_w', 'grad_norm_mix_post': 'grad_w', 'grad_w_in': 'grad_w', 'grad_ssm_conv_w': 'grad_w', 'grad_ssm_conv_b': 'grad_w', 'grad_ssm_dt_bias': 'grad_w', 'grad_ssm_A_log': 'grad_w', 'grad_ssm_D': 'grad_w', 'grad_ssm_norm_w': 'grad_w', 'grad_gdn_conv_w': 'grad_w', 'grad_gdn_dt_bias': 'grad_w', 'grad_gdn_A_log': 'grad_w', 'grad_gdn_norm_w': 'grad_w', 'grad_w_ssm_up': 'grad_w', 'grad_w_gdn_up': 'grad_w', 'grad_w_out': 'grad_w', 'grad_norm_mlp_pre': 'grad_w', 'grad_norm_mlp_post': 'grad_w', 'grad_w_mlp_up': 'grad_w', 'grad_w_mlp_down': 'grad_w', 'delta_w_ada': 'delta_w', 'delta_b_ada': 'delta_w', 'delta_norm_mix_pre': 'delta_w', 'delta_norm_mix_post': 'delta_w', 'delta_w_in': 'delta_w', 'delta_ssm_conv_w': 'delta_w', 'delta_ssm_conv_b': 'delta_w', 'delta_ssm_dt_bias': 'delta_w', 'delta_ssm_A_log': 'delta_w', 'delta_ssm_D': 'delta_w', 'delta_ssm_norm_w': 'delta_w', 'delta_gdn_conv_w': 'delta_w', 'delta_gdn_dt_bias': 'delta_w', 'delta_gdn_A_log': 'delta_w', 'delta_gdn_norm_w': 'delta_w', 'delta_w_ssm_up': 'delta_w', 'delta_w_gdn_up': 'delta_w', 'delta_w_out': 'delta_w', 'delta_norm_mlp_pre': 'delta_w', 'delta_norm_mlp_post': 'delta_w', 'delta_w_mlp_up': 'delta_w', 'delta_w_mlp_down': 'delta_w', 'new_m_w_ada': 'new_m', 'new_m_b_ada': 'new_m', 'new_m_norm_mix_pre': 'new_m', 'new_m_norm_mix_post': 'new_m', 'new_m_w_in': 'new_m', 'new_m_ssm_conv_w': 'new_m', 'new_m_ssm_conv_b': 'new_m', 'new_m_ssm_dt_bias': 'new_m', 'new_m_ssm_A_log': 'new_m', 'new_m_ssm_D': 'new_m', 'new_m_ssm_norm_w': 'new_m', 'new_m_gdn_conv_w': 'new_m', 'new_m_gdn_dt_bias': 'new_m', 'new_m_gdn_A_log': 'new_m', 'new_m_gdn_norm_w': 'new_m', 'new_m_w_ssm_up': 'new_m', 'new_m_w_gdn_up': 'new_m', 'new_m_w_out': 'new_m', 'new_m_norm_mlp_pre': 'new_m', 'new_m_norm_mlp_post': 'new_m', 'new_m_w_mlp_up': 'new_m', 'new_m_w_mlp_down': 'new_m', 'new_v_w_ada': 'new_v', 'new_v_b_ada': 'new_v', 'new_v_norm_mix_pre': 'new_v', 'new_v_norm_mix_post': 'new_v', 'new_v_w_in': 'new_v', 'new_v_ssm_conv_w': 'new_v', 'new_v_ssm_conv_b': 'new_v', 'new_v_ssm_dt_bias': 'new_v', 'new_v_ssm_A_log': 'new_v', 'new_v_ssm_D': 'new_v', 'new_v_ssm_norm_w': 'new_v', 'new_v_gdn_conv_w': 'new_v', 'new_v_gdn_dt_bias': 'new_v', 'new_v_gdn_A_log': 'new_v', 'new_v_gdn_norm_w': 'new_v', 'new_v_w_ssm_up': 'new_v', 'new_v_w_gdn_up': 'new_v', 'new_v_w_out': 'new_v', 'new_v_norm_mlp_pre': 'new_v', 'new_v_norm_mlp_post': 'new_v', 'new_v_w_mlp_up': 'new_v', 'new_v_w_mlp_down': 'new_v'}


def _forward(args):
    return _fwd_reference(*[args[k] for k in FWD_PARAMS])


def _output_shape():
    out = _jax.eval_shape(lambda: _forward(_fwd_setup_inputs(0)))
    return out.shape, out.dtype

N_MICROBATCH = 1
ADAM_LR = 0.001
ADAM_B1 = 0.9
ADAM_B2 = 0.999
ADAM_EPS = 1e-08
ADAM_WD = 0.01
ADAM_STEP = 10
PER_EXAMPLE_BATCH_AXIS = {'x': 0, 'c': 0, 'loss_target': 0}
SHARED_INPUTS = []
_WEIGHT_DTYPES = {'w_ada': _jnp.float32, 'b_ada': _jnp.float32, 'norm_mix_pre': _jnp.float32, 'norm_mix_post': _jnp.float32, 'w_in': _jnp.float32, 'ssm_conv_w': _jnp.float32, 'ssm_conv_b': _jnp.float32, 'ssm_dt_bias': _jnp.float32, 'ssm_A_log': _jnp.float32, 'ssm_D': _jnp.float32, 'ssm_norm_w': _jnp.float32, 'gdn_conv_w': _jnp.float32, 'gdn_dt_bias': _jnp.float32, 'gdn_A_log': _jnp.float32, 'gdn_norm_w': _jnp.float32, 'w_ssm_up': _jnp.float32, 'w_gdn_up': _jnp.float32, 'w_out': _jnp.float32, 'norm_mlp_pre': _jnp.float32, 'norm_mlp_post': _jnp.float32, 'w_mlp_up': _jnp.float32, 'w_mlp_down': _jnp.float32}
MOMENT_SCALE = {'w_ada': 1.817565e+00, 'b_ada': 3.462583e+00, 'norm_mix_pre': 1.377878e-01, 'norm_mix_post': 3.803124e+00, 'w_in': 5.066637e-02, 'ssm_conv_w': 6.874790e-02, 'ssm_conv_b': 1.579658e-01, 'ssm_dt_bias': 1.239996e-01, 'ssm_A_log': 7.210734e-01, 'ssm_D': 2.837529e-01, 'ssm_norm_w': 1.216222e-01, 'gdn_conv_w': 4.670566e-02, 'gdn_dt_bias': 9.436967e-02, 'gdn_A_log': 9.660792e-02, 'gdn_norm_w': 3.470824e-01, 'w_ssm_up': 1.762979e-01, 'w_gdn_up': 1.127339e-01, 'w_out': 2.146557e-01, 'norm_mlp_pre': 1.158774e-01, 'norm_mlp_post': 3.936973e+00, 'w_mlp_up': 9.650828e-02, 'w_mlp_down': 4.491692e-01}


def _to_microbatches(a, axis):
    t = _jnp.moveaxis(a, axis, 0)
    t = t.reshape((N_MICROBATCH, t.shape[0] // N_MICROBATCH) + t.shape[1:])
    return _jnp.moveaxis(t, 1, axis + 1)


def setup_inputs(seed: int = 0) -> dict:
    inp = _fwd_setup_inputs(seed)
    key = _jax.random.fold_in(_jax.random.key(seed), 7919)
    shape, _ = _output_shape()
    out = dict(inp)
    out["loss_target"] = _jax.random.normal(_jax.random.fold_in(key, 0), shape, _jnp.float32)
    for i, name in enumerate(TWIN_WEIGHTS):
        w = inp[name].astype(_jnp.float32)
        if MOMENT_SCALE is None:
            s = _jnp.sqrt(_jnp.mean(_jnp.square(w)) + 1e-30)
        else:
            s = MOMENT_SCALE[name]
        km, kv = _jax.random.split(_jax.random.fold_in(key, i + 1))
        out[name] = w
        out["m_" + name] = s * _jax.random.normal(km, w.shape, _jnp.float32)
        out["v_" + name] = (s * s) * _jax.random.uniform(kv, w.shape, _jnp.float32, 0.5, 1.5)
    if N_MICROBATCH > 1:
        for name, axis in PER_EXAMPLE_BATCH_AXIS.items():
            out[name] = _to_microbatches(out[name], axis)
    return {'x': out['x'], 'c': out['c'], 'w_ada': out['w_ada'], 'b_ada': out['b_ada'], 'norm_mix_pre': out['norm_mix_pre'], 'norm_mix_post': out['norm_mix_post'], 'w_in': out['w_in'], 'ssm_conv_w': out['ssm_conv_w'], 'ssm_conv_b': out['ssm_conv_b'], 'ssm_dt_bias': out['ssm_dt_bias'], 'ssm_A_log': out['ssm_A_log'], 'ssm_D': out['ssm_D'], 'ssm_norm_w': out['ssm_norm_w'], 'gdn_conv_w': out['gdn_conv_w'], 'gdn_dt_bias': out['gdn_dt_bias'], 'gdn_A_log': out['gdn_A_log'], 'gdn_norm_w': out['gdn_norm_w'], 'w_ssm_up': out['w_ssm_up'], 'w_gdn_up': out['w_gdn_up'], 'w_out': out['w_out'], 'norm_mlp_pre': out['norm_mlp_pre'], 'norm_mlp_post': out['norm_mlp_post'], 'w_mlp_up': out['w_mlp_up'], 'w_mlp_down': out['w_mlp_down'], 'loss_target': out['loss_target'], 'm_w_ada': out['m_w_ada'], 'm_b_ada': out['m_b_ada'], 'm_norm_mix_pre': out['m_norm_mix_pre'], 'm_norm_mix_post': out['m_norm_mix_post'], 'm_w_in': out['m_w_in'], 'm_ssm_conv_w': out['m_ssm_conv_w'], 'm_ssm_conv_b': out['m_ssm_conv_b'], 'm_ssm_dt_bias': out['m_ssm_dt_bias'], 'm_ssm_A_log': out['m_ssm_A_log'], 'm_ssm_D': out['m_ssm_D'], 'm_ssm_norm_w': out['m_ssm_norm_w'], 'm_gdn_conv_w': out['m_gdn_conv_w'], 'm_gdn_dt_bias': out['m_gdn_dt_bias'], 'm_gdn_A_log': out['m_gdn_A_log'], 'm_gdn_norm_w': out['m_gdn_norm_w'], 'm_w_ssm_up': out['m_w_ssm_up'], 'm_w_gdn_up': out['m_w_gdn_up'], 'm_w_out': out['m_w_out'], 'm_norm_mlp_pre': out['m_norm_mlp_pre'], 'm_norm_mlp_post': out['m_norm_mlp_post'], 'm_w_mlp_up': out['m_w_mlp_up'], 'm_w_mlp_down': out['m_w_mlp_down'], 'v_w_ada': out['v_w_ada'], 'v_b_ada': out['v_b_ada'], 'v_norm_mix_pre': out['v_norm_mix_pre'], 'v_norm_mix_post': out['v_norm_mix_post'], 'v_w_in': out['v_w_in'], 'v_ssm_conv_w': out['v_ssm_conv_w'], 'v_ssm_conv_b': out['v_ssm_conv_b'], 'v_ssm_dt_bias': out['v_ssm_dt_bias'], 'v_ssm_A_log': out['v_ssm_A_log'], 'v_ssm_D': out['v_ssm_D'], 'v_ssm_norm_w': out['v_ssm_norm_w'], 'v_gdn_conv_w': out['v_gdn_conv_w'], 'v_gdn_dt_bias': out['v_gdn_dt_bias'], 'v_gdn_A_log': out['v_gdn_A_log'], 'v_gdn_norm_w': out['v_gdn_norm_w'], 'v_w_ssm_up': out['v_w_ssm_up'], 'v_w_gdn_up': out['v_w_gdn_up'], 'v_w_out': out['v_w_out'], 'v_norm_mlp_pre': out['v_norm_mlp_pre'], 'v_norm_mlp_post': out['v_norm_mlp_post'], 'v_w_mlp_up': out['v_w_mlp_up'], 'v_w_mlp_down': out['v_w_mlp_down']}


def _loss(weights, diff, rest, loss_target):
    with _jax.named_scope("forward"):
        args = {**rest, TWIN_DIFF_INPUT: diff, **{k: w.astype(_WEIGHT_DTYPES[k]) for k, w in weights.items()}}
        y = _forward(args)
    with _jax.named_scope("loss_head"):
        err = _jnp.square(y.astype(_jnp.float32) - loss_target)
        return 0.5 * _jnp.sum(_jnp.mean(err, axis=-1)) if err.ndim else 0.5 * err


def _adamw(w, g, m, v):
    m = ADAM_B1 * m + (1.0 - ADAM_B1) * g
    v = ADAM_B2 * v + (1.0 - ADAM_B2) * _jnp.square(g)
    m_hat = m / (1.0 - ADAM_B1 ** ADAM_STEP)
    v_hat = v / (1.0 - ADAM_B2 ** ADAM_STEP)
    delta = -ADAM_LR * (m_hat / (_jnp.sqrt(v_hat) + ADAM_EPS) + ADAM_WD * w)
    return delta, m, v


def reference(x, c, w_ada, b_ada, norm_mix_pre, norm_mix_post, w_in, ssm_conv_w, ssm_conv_b, ssm_dt_bias, ssm_A_log, ssm_D, ssm_norm_w, gdn_conv_w, gdn_dt_bias, gdn_A_log, gdn_norm_w, w_ssm_up, w_gdn_up, w_out, norm_mlp_pre, norm_mlp_post, w_mlp_up, w_mlp_down, loss_target, m_w_ada, m_b_ada, m_norm_mix_pre, m_norm_mix_post, m_w_in, m_ssm_conv_w, m_ssm_conv_b, m_ssm_dt_bias, m_ssm_A_log, m_ssm_D, m_ssm_norm_w, m_gdn_conv_w, m_gdn_dt_bias, m_gdn_A_log, m_gdn_norm_w, m_w_ssm_up, m_w_gdn_up, m_w_out, m_norm_mlp_pre, m_norm_mlp_post, m_w_mlp_up, m_w_mlp_down, v_w_ada, v_b_ada, v_norm_mix_pre, v_norm_mix_post, v_w_in, v_ssm_conv_w, v_ssm_conv_b, v_ssm_dt_bias, v_ssm_A_log, v_ssm_D, v_ssm_norm_w, v_gdn_conv_w, v_gdn_dt_bias, v_gdn_A_log, v_gdn_norm_w, v_w_ssm_up, v_w_gdn_up, v_w_out, v_norm_mlp_pre, v_norm_mlp_post, v_w_mlp_up, v_w_mlp_down):
    given = dict(x=x, c=c, w_ada=w_ada, b_ada=b_ada, norm_mix_pre=norm_mix_pre, norm_mix_post=norm_mix_post, w_in=w_in, ssm_conv_w=ssm_conv_w, ssm_conv_b=ssm_conv_b, ssm_dt_bias=ssm_dt_bias, ssm_A_log=ssm_A_log, ssm_D=ssm_D, ssm_norm_w=ssm_norm_w, gdn_conv_w=gdn_conv_w, gdn_dt_bias=gdn_dt_bias, gdn_A_log=gdn_A_log, gdn_norm_w=gdn_norm_w, w_ssm_up=w_ssm_up, w_gdn_up=w_gdn_up, w_out=w_out, norm_mlp_pre=norm_mlp_pre, norm_mlp_post=norm_mlp_post, w_mlp_up=w_mlp_up, w_mlp_down=w_mlp_down, loss_target=loss_target, m_w_ada=m_w_ada, m_b_ada=m_b_ada, m_norm_mix_pre=m_norm_mix_pre, m_norm_mix_post=m_norm_mix_post, m_w_in=m_w_in, m_ssm_conv_w=m_ssm_conv_w, m_ssm_conv_b=m_ssm_conv_b, m_ssm_dt_bias=m_ssm_dt_bias, m_ssm_A_log=m_ssm_A_log, m_ssm_D=m_ssm_D, m_ssm_norm_w=m_ssm_norm_w, m_gdn_conv_w=m_gdn_conv_w, m_gdn_dt_bias=m_gdn_dt_bias, m_gdn_A_log=m_gdn_A_log, m_gdn_norm_w=m_gdn_norm_w, m_w_ssm_up=m_w_ssm_up, m_w_gdn_up=m_w_gdn_up, m_w_out=m_w_out, m_norm_mlp_pre=m_norm_mlp_pre, m_norm_mlp_post=m_norm_mlp_post, m_w_mlp_up=m_w_mlp_up, m_w_mlp_down=m_w_mlp_down, v_w_ada=v_w_ada, v_b_ada=v_b_ada, v_norm_mix_pre=v_norm_mix_pre, v_norm_mix_post=v_norm_mix_post, v_w_in=v_w_in, v_ssm_conv_w=v_ssm_conv_w, v_ssm_conv_b=v_ssm_conv_b, v_ssm_dt_bias=v_ssm_dt_bias, v_ssm_A_log=v_ssm_A_log, v_ssm_D=v_ssm_D, v_ssm_norm_w=v_ssm_norm_w, v_gdn_conv_w=v_gdn_conv_w, v_gdn_dt_bias=v_gdn_dt_bias, v_gdn_A_log=v_gdn_A_log, v_gdn_norm_w=v_gdn_norm_w, v_w_ssm_up=v_w_ssm_up, v_w_gdn_up=v_w_gdn_up, v_w_out=v_w_out, v_norm_mlp_pre=v_norm_mlp_pre, v_norm_mlp_post=v_norm_mlp_post, v_w_mlp_up=v_w_mlp_up, v_w_mlp_down=v_w_mlp_down)
    weights = {n: given[n] for n in TWIN_WEIGHTS}
    shared = {n: given[n] for n in SHARED_INPUTS}
    per_example = {n: given[n] for n in ['x', 'c']}
    grad_fn = _jax.value_and_grad(_loss, argnums=(0, 1))

    def one_microbatch(ex, loss_target):
        ex = dict(ex)
        diff = ex.pop(TWIN_DIFF_INPUT)
        return grad_fn(weights, diff, {**shared, **ex}, loss_target)

    if N_MICROBATCH == 1:
        loss, (grad_w, grad_x) = one_microbatch(per_example, given["loss_target"])
    else:
        def body(carry, xs):
            loss_sum, grad_sum = carry
            l_k, (gw_k, gx_k) = one_microbatch(xs[0], xs[1])
            with _jax.named_scope("update"):
                return (loss_sum + l_k, _jax.tree.map(_jnp.add, grad_sum, gw_k)), gx_k

        init = (_jnp.zeros((), _jnp.float32), _jax.tree.map(_jnp.zeros_like, weights))
        (loss, grad_w), grad_x = _jax.lax.scan(body, init, (per_example, given["loss_target"]))
    with _jax.named_scope("update"):
        delta_w, new_m, new_v = {}, {}, {}
        for n in TWIN_WEIGHTS:
            delta_w[n], new_m[n], new_v[n] = _adamw(weights[n], grad_w[n], given["m_" + n], given["v_" + n])
    return (loss, grad_x, *[grad_w[n] for n in TWIN_WEIGHTS], *[delta_w[n] for n in TWIN_WEIGHTS],
            *[new_m[n] for n in TWIN_WEIGHTS], *[new_v[n] for n in TWIN_WEIGHTS])
```

```python
import functools

import jax
import jax.numpy as jnp
from jax import lax
from jax.experimental import pallas as pl
from jax.experimental.pallas import tpu as pltpu

F32 = jnp.float32
BF16 = jnp.bfloat16
N_DEV = 8
D_MODEL = 1024
EPS = 1e-6
CONV_K = 4
SSM_CHUNK = 128
SSM_HEAD_DIM = 64
SSM_D_STATE = 128
SSM_GROUPS = 8
SSM_HEADS_PER_GROUP = 4
SSM_GROUP_WIDTH = SSM_HEADS_PER_GROUP * SSM_HEAD_DIM
GDN_CHUNK = 64
GDN_HEAD = 128
GDN_QK_HEADS = 8
GDN_V_PER_QK = 2
GDN_INV_BLOCK = 16
C_ZS, C_XBC, C_QKV, C_ZG, C_GS, C_GG, N_MAIN = 0, 2048, 6144, 10240, 12288, 13312, 14336
N_SMALL = 128
ADAM_LR, ADAM_B1, ADAM_B2, ADAM_EPS, ADAM_WD, ADAM_STEP = 0.001, 0.9, 0.999, 1e-08, 0.01, 10
VMEM_LIMIT = 56 * 1024 * 1024
NEG_INF = float("-inf")

_NT = (((1,), (1,)), ((), ()))
_NN = (((1,), (0,)), ((), ()))
_TN = (((0,), (0,)), ((), ()))


def _params(*sem):
    return pltpu.CompilerParams(dimension_semantics=sem, vmem_limit_bytes=VMEM_LIMIT)


def _dot(a, b, dims=_NN):
    return lax.dot_general(a.astype(BF16), b.astype(BF16), dims, preferred_element_type=F32)


def _split(a):
    hi = a.astype(BF16)
    return hi, (a - hi.astype(F32)).astype(BF16)


def _dot3(a, b, dims=_NN):
    ah, al = _split(a)
    bh, bl = _split(b)
    d = lambda u, v: lax.dot_general(u, v, dims, preferred_element_type=F32)
    return d(ah, bh) + (d(ah, bl) + d(al, bh))


def _sigmoid(x):
    return 1.0 / (1.0 + jnp.exp(-x))


def _silu(x):
    return x * _sigmoid(x)


def _dsilu(x):
    s = _sigmoid(x)
    return s * (1.0 + x * (1.0 - s))


def _softplus(x):
    return jnp.maximum(x, 0.0) + jnp.log1p(jnp.exp(-jnp.abs(x)))


def _iota(n, m, d):
    return lax.broadcasted_iota(jnp.int32, (n, m), d)


def _rowsum(x):
    return jnp.sum(x, axis=1, keepdims=True)


def _colsum(x):
    return jnp.sum(x, axis=0, keepdims=True)


def _cumsum_forms(col, ii, jj):
    row = _colsum(jnp.where(ii == jj, col, 0.0))
    cum_col = _rowsum(jnp.where(ii >= jj, row, 0.0))
    cum_row = _colsum(jnp.where(ii <= jj, col, 0.0))
    return cum_col, cum_row


def _rev_cumsum_col(col, ii, jj):
    row = _colsum(jnp.where(ii >= jj, col, 0.0))
    return _rowsum(jnp.where(ii == jj, row, 0.0))


def _blk(dim, pref):
    return pref if dim % pref == 0 else dim


def _mm(a, b, M, N, K, *, mode, out_dtype, name, a_off=(0, 0), b_off=(0, 0), add=None, epi=None, extra=None,
        tm=1024, tn=1024, tk=1024):
    tm, tn, tk = _blk(M, tm), _blk(N, tn), _blk(K, tk)
    nk = K // tk
    if mode == "tn":
        a_spec = pl.BlockSpec((tk, tm), lambda i, j, k: (k + a_off[0] // tk, i + a_off[1] // tm))
        assert a_off[0] % tk == 0 and a_off[1] % tm == 0
    else:
        a_spec = pl.BlockSpec((tm, tk), lambda i, j, k: (i + a_off[0] // tm, k + a_off[1] // tk))
        assert a_off[0] % tm == 0 and a_off[1] % tk == 0
    if mode == "nt":
        b_spec = pl.BlockSpec((tn, tk), lambda i, j, k: (j + b_off[0] // tn, k + b_off[1] // tk))
        assert b_off[0] % tn == 0 and b_off[1] % tk == 0
    else:
        b_spec = pl.BlockSpec((tk, tn), lambda i, j, k: (k + b_off[0] // tk, j + b_off[1] // tn))
        assert b_off[0] % tk == 0 and b_off[1] % tn == 0
    dims = {"nn": _NN, "nt": _NT, "tn": _TN}[mode]
    o_spec = pl.BlockSpec((tm, tn), lambda i, j, k: (i, j))
    ins, in_specs = [a, b], [a_spec, b_spec]
    if add is not None:
        ins.append(add)
        in_specs.append(o_spec)
    if extra is not None:
        ins.append(extra)
        in_specs.append(o_spec)
    n_in = len(ins)
    if epi == "relu2":
        out_shape = (jax.ShapeDtypeStruct((M, N), F32), jax.ShapeDtypeStruct((M, N), BF16))
        out_specs = (o_spec, o_spec)
    else:
        out_shape = jax.ShapeDtypeStruct((M, N), out_dtype)
        out_specs = o_spec

    def body(*refs):
        a_ref, b_ref = refs[0], refs[1]
        acc = refs[-1]
        outs = refs[n_in:-1]
        k = pl.program_id(2)

        @pl.when(k == 0)
        def _():
            acc[...] = jnp.zeros_like(acc)

        acc[...] += _dot(a_ref[...], b_ref[...], dims)

        @pl.when(k == nk - 1)
        def _():
            r = acc[...]
            pos = 2
            if add is not None:
                r = r + refs[pos][...]
                pos += 1
            if epi == "relu2":
                outs[0][...] = r
                p = jnp.maximum(r, 0.0)
                outs[1][...] = (p * p).astype(BF16)
            elif epi == "drelu2":
                outs[0][...] = (r * (2.0 * jnp.maximum(refs[pos][...], 0.0))).astype(out_dtype)
            else:
                outs[0][...] = r.astype(out_dtype)

    return pl.pallas_call(
        body, name=name, grid=(M // tm, N // tn, nk), in_specs=in_specs, out_specs=out_specs, out_shape=out_shape,
        scratch_shapes=[pltpu.VMEM((tm, tn), F32)],
        compiler_params=_params("parallel", "parallel", "arbitrary"))(*ins)


def _row_spec(tb, d):
    return pl.BlockSpec((tb, d), lambda i: (i, 0))


def _vec_spec(d):
    return pl.BlockSpec((1, d), lambda i: (0, 0))


def _pre_fwd(x, w, sc, sh, name):
    S, Dm = x.shape
    tb = _blk(S, 512)

    def body(x_ref, w_ref, sc_ref, sh_ref, h_ref):
        xv = x_ref[...]
        r = lax.rsqrt(jnp.mean(xv * xv, axis=-1, keepdims=True) + EPS)
        h_ref[...] = ((xv * r * w_ref[...]) * (1.0 + sc_ref[...]) + sh_ref[...]).astype(BF16)

    return pl.pallas_call(
        body, name=name, grid=(S // tb,), in_specs=[_row_spec(tb, Dm)] + [_vec_spec(Dm)] * 3,
        out_specs=_row_spec(tb, Dm), out_shape=jax.ShapeDtypeStruct((S, Dm), BF16),
        compiler_params=_params("parallel"))(x, w, sc, sh)


def _post_fwd(x, y, w, g, name):
    S, Dm = x.shape
    tb = _blk(S, 512)

    def body(x_ref, y_ref, w_ref, g_ref, o_ref):
        yv = y_ref[...]
        r = lax.rsqrt(jnp.mean(yv * yv, axis=-1, keepdims=True) + EPS)
        o_ref[...] = x_ref[...] + g_ref[...] * (yv * r * w_ref[...])

    return pl.pallas_call(
        body, name=name, grid=(S // tb,), in_specs=[_row_spec(tb, Dm)] * 2 + [_vec_spec(Dm)] * 2,
        out_specs=_row_spec(tb, Dm), out_shape=jax.ShapeDtypeStruct((S, Dm), F32),
        compiler_params=_params("parallel"))(x, y, w, g)


def _final_fwd(x, y, w, g, target, name):
    S, Dm = x.shape
    tb = _blk(S, 512)
    nb = S // tb

    def body(x_ref, y_ref, w_ref, g_ref, t_ref, dx_ref, loss_ref, acc):
        i = pl.program_id(0)

        @pl.when(i == 0)
        def _():
            acc[...] = jnp.zeros_like(acc)

        yv = y_ref[...]
        r = lax.rsqrt(jnp.mean(yv * yv, axis=-1, keepdims=True) + EPS)
        e = (x_ref[...] + g_ref[...] * (yv * r * w_ref[...])) - t_ref[...]
        dx_ref[...] = e * (1.0 / Dm)
        acc[...] += _colsum(e * e)

        @pl.when(i == nb - 1)
        def _():
            loss_ref[...] = (0.5 / Dm) * _rowsum(acc[...])

    return pl.pallas_call(
        body, name=name, grid=(nb,), in_specs=[_row_spec(tb, Dm)] * 2 + [_vec_spec(Dm)] * 2 + [_row_spec(tb, Dm)],
        out_specs=(_row_spec(tb, Dm), pl.BlockSpec((1, 1), lambda i: (0, 0))),
        out_shape=(jax.ShapeDtypeStruct((S, Dm), F32), jax.ShapeDtypeStruct((1, 1), F32)),
        scratch_shapes=[pltpu.VMEM((1, Dm), F32)], compiler_params=_params("arbitrary"))(x, y, w, g, target)


def _post_bwd(dxo, y, w, g, name):
    S, Dm = y.shape
    tb = _blk(S, 512)

    def body(d_ref, y_ref, w_ref, g_ref, dy_ref, dg_ref, dw_ref):
        i = pl.program_id(0)

        @pl.when(i == 0)
        def _():
            dg_ref[...] = jnp.zeros_like(dg_ref)
            dw_ref[...] = jnp.zeros_like(dw_ref)

        yv, dv = y_ref[...], d_ref[...]
        r = lax.rsqrt(jnp.mean(yv * yv, axis=-1, keepdims=True) + EPS)
        yh = yv * r
        dg_ref[...] += _colsum(dv * (yh * w_ref[...]))
        dn = dv * g_ref[...]
        dw_ref[...] += _colsum(dn * yh)
        dyh = dn * w_ref[...]
        dy_ref[...] = (r * (dyh - yh * jnp.mean(dyh * yh, axis=-1, keepdims=True))).astype(BF16)

    return pl.pallas_call(
        body, name=name, grid=(S // tb,), in_specs=[_row_spec(tb, Dm)] * 2 + [_vec_spec(Dm)] * 2,
        out_specs=(_row_spec(tb, Dm), _vec_spec(Dm), _vec_spec(Dm)),
        out_shape=(jax.ShapeDtypeStruct((S, Dm), BF16), jax.ShapeDtypeStruct((1, Dm), F32),
                   jax.ShapeDtypeStruct((1, Dm), F32)),
        compiler_params=_params("arbitrary"))(dxo, y, w, g)


def _pre_bwd(dh, x, w, sc, dres, name):
    S, Dm = x.shape
    tb = _blk(S, 512)

    def body(dh_ref, x_ref, w_ref, sc_ref, dr_ref, dx_ref, dsh_ref, dsc_ref, dw_ref):
        i = pl.program_id(0)

        @pl.when(i == 0)
        def _():
            dsh_ref[...] = jnp.zeros_like(dsh_ref)
            dsc_ref[...] = jnp.zeros_like(dsc_ref)
            dw_ref[...] = jnp.zeros_like(dw_ref)

        xv, dv = x_ref[...], dh_ref[...]
        r = lax.rsqrt(jnp.mean(xv * xv, axis=-1, keepdims=True) + EPS)
        xh = xv * r
        one_sc = 1.0 + sc_ref[...]
        dsh_ref[...] += _colsum(dv)
        dsc_ref[...] += _colsum(dv * (xh * w_ref[...]))
        dw_ref[...] += _colsum(dv * one_sc * xh)
        dxh = dv * one_sc * w_ref[...]
        dx_ref[...] = dr_ref[...] + r * (dxh - xh * jnp.mean(dxh * xh, axis=-1, keepdims=True))

    vec = jax.ShapeDtypeStruct((1, Dm), F32)
    return pl.pallas_call(
        body, name=name, grid=(S // tb,),
        in_specs=[_row_spec(tb, Dm)] * 2 + [_vec_spec(Dm)] * 2 + [_row_spec(tb, Dm)],
        out_specs=(_row_spec(tb, Dm), _vec_spec(Dm), _vec_spec(Dm), _vec_spec(Dm)),
        out_shape=(jax.ShapeDtypeStruct((S, Dm), F32), vec, vec, vec),
        compiler_params=_params("arbitrary"))(dh, x, w, sc, dres)


def _merge_fwd(ys, yg, proj, name):
    S, Dm = ys.shape
    tb = _blk(S, 512)

    def body(ys_ref, yg_ref, gs_ref, gg_ref, o_ref):
        o_ref[...] = (_sigmoid(gs_ref[...]) * ys_ref[...] + _sigmoid(gg_ref[...]) * yg_ref[...]).astype(BF16)

    return pl.pallas_call(
        body, name=name, grid=(S // tb,),
        in_specs=[_row_spec(tb, Dm)] * 2 + [pl.BlockSpec((tb, Dm), lambda i: (i, C_GS // Dm)),
                                            pl.BlockSpec((tb, Dm), lambda i: (i, C_GG // Dm))],
        out_specs=_row_spec(tb, Dm), out_shape=jax.ShapeDtypeStruct((S, Dm), BF16),
        compiler_params=_params("parallel"))(ys, yg, proj, proj)


def _merge_bwd(dm, ys, yg, proj, name):
    S, Dm = ys.shape
    tb = _blk(S, 512)

    def body(dm_ref, ys_ref, yg_ref, gs_ref, gg_ref, dys_ref, dyg_ref, dgate_ref):
        d = dm_ref[...]
        ss, sg = _sigmoid(gs_ref[...]), _sigmoid(gg_ref[...])
        dys_ref[...] = (d * ss).astype(BF16)
        dyg_ref[...] = (d * sg).astype(BF16)
        dgate_ref[:, :Dm] = (d * ys_ref[...] * ss * (1.0 - ss)).astype(BF16)
        dgate_ref[:, Dm:] = (d * yg_ref[...] * sg * (1.0 - sg)).astype(BF16)

    return pl.pallas_call(
        body, name=name, grid=(S // tb,),
        in_specs=[_row_spec(tb, Dm)] * 3 + [pl.BlockSpec((tb, Dm), lambda i: (i, C_GS // Dm)),
                                            pl.BlockSpec((tb, Dm), lambda i: (i, C_GG // Dm))],
        out_specs=(_row_spec(tb, Dm), _row_spec(tb, Dm), _row_spec(tb, 2 * Dm)),
        out_shape=(jax.ShapeDtypeStruct((S, Dm), BF16), jax.ShapeDtypeStruct((S, Dm), BF16),
                   jax.ShapeDtypeStruct((S, 2 * Dm), BF16)),
        compiler_params=_params("parallel"))(dm, ys, yg, proj, proj)


CONV_COLS = 128


def _shift_down(x, k, rows):
    return jnp.where(rows >= k, pltpu.roll(x, k, 0), 0.0)


def _shift_up(x, k, rows, S):
    return jnp.where(rows < S - k, pltpu.roll(x, S - k, 0), 0.0)


def _conv_fwd(proj, w, b, name):
    S = proj.shape[0]
    n = w.shape[1]
    cb = CONV_COLS

    def body(x_ref, w_ref, b_ref, o_ref):
        x = x_ref[...]
        rows = _iota(S, cb, 0)
        pre = x * w_ref[CONV_K - 1:CONV_K, :] + b_ref[...]
        for k in range(1, CONV_K):
            pre = pre + _shift_down(x, k, rows) * w_ref[CONV_K - 1 - k:CONV_K - k, :]
        o_ref[...] = _silu(pre)

    return pl.pallas_call(
        body, name=name, grid=(n // cb,),
        in_specs=[pl.BlockSpec((S, cb), lambda j: (0, j + C_XBC // cb)), pl.BlockSpec((CONV_K, cb), lambda j: (0, j)),
                  pl.BlockSpec((1, cb), lambda j: (0, j))],
        out_specs=pl.BlockSpec((S, cb), lambda j: (0, j)), out_shape=jax.ShapeDtypeStruct((S, n), F32),
        compiler_params=_params("parallel"))(proj, w, b)


def _conv_bwd(dact, proj, w, b, col0, name):
    S, n = dact.shape
    cb = CONV_COLS
    o = col0 // cb

    def body(d_ref, x_ref, w_ref, b_ref, dx_ref, dw_ref, db_ref):
        x = x_ref[...]
        rows = _iota(S, cb, 0)
        xs = [x] + [_shift_down(x, k, rows) for k in range(1, CONV_K)]
        pre = xs[0] * w_ref[CONV_K - 1:CONV_K, :] + b_ref[...]
        for k in range(1, CONV_K):
            pre = pre + xs[k] * w_ref[CONV_K - 1 - k:CONV_K - k, :]
        dpre = d_ref[...] * _dsilu(pre)
        db_ref[...] = _colsum(dpre)
        dx = dpre * w_ref[CONV_K - 1:CONV_K, :]
        for k in range(CONV_K):
            dw_ref[CONV_K - 1 - k:CONV_K - k, :] = _colsum(dpre * xs[k])
            if k:
                dx = dx + _shift_up(dpre, k, rows, S) * w_ref[CONV_K - 1 - k:CONV_K - k, :]
        dx_ref[...] = dx.astype(BF16)

    return pl.pallas_call(
        body, name=name, grid=(n // cb,),
        in_specs=[pl.BlockSpec((S, cb), lambda j: (0, j)), pl.BlockSpec((S, cb), lambda j: (0, j + o + C_XBC // cb)),
                  pl.BlockSpec((CONV_K, cb), lambda j: (0, j + o)), pl.BlockSpec((1, cb), lambda j: (0, j + o))],
        out_specs=(pl.BlockSpec((S, cb), lambda j: (0, j)), pl.BlockSpec((CONV_K, cb), lambda j: (0, j)),
                   pl.BlockSpec((1, cb), lambda j: (0, j))),
        out_shape=(jax.ShapeDtypeStruct((S, n), BF16), jax.ShapeDtypeStruct((CONV_K, n), F32),
                   jax.ShapeDtypeStruct((1, n), F32)),
        compiler_params=_params("parallel"))(dact, proj, w, b)


def _ssd_specs(L, order):
    W, N = SSM_GROUP_WIDTH, SSM_D_STATE
    x_spec = pl.BlockSpec((L, W), lambda g, c: (order(c), g))
    b_spec = pl.BlockSpec((L, N), lambda g, c: (order(c), 2048 // N + g))
    c_spec = pl.BlockSpec((L, N), lambda g, c: (order(c), 3072 // N + g))
    z_spec = pl.BlockSpec((L, W), lambda g, c: (order(c), C_ZS // W + g))
    dt_spec = pl.BlockSpec((1, L, SSM_HEADS_PER_GROUP), lambda g, c: (g, order(c), 0))
    p_spec = pl.BlockSpec((1, 3, SSM_HEADS_PER_GROUP), lambda g, c: (g, 0, 0))
    nw_spec = pl.BlockSpec((1, 1, W), lambda g, c: (g, 0, 0))
    s_spec = pl.BlockSpec((1, 1, W, N), lambda g, c: (g, order(c), 0, 0))
    return x_spec, b_spec, c_spec, z_spec, dt_spec, p_spec, nw_spec, s_spec


def _ssd_fwd(conv, proj, dt_raw, pvec, nw, name):
    S = conv.shape[0]
    L, P, N, H, W = SSM_CHUNK, SSM_HEAD_DIM, SSM_D_STATE, SSM_HEADS_PER_GROUP, SSM_GROUP_WIDTH
    nc = S // L

    def body(x_ref, b_ref, c_ref, z_ref, dt_ref, p_ref, nw_ref, y_ref, yn_ref, s0_ref, state):
        c = pl.program_id(1)

        @pl.when(c == 0)
        def _():
            state[...] = jnp.zeros_like(state)

        s0_ref[0, 0] = state[...]
        x, Bm, Cm = x_ref[...], b_ref[...], c_ref[...]
        p = p_ref[0]
        dt = _softplus(dt_ref[0] + p[0:1, :])
        a = dt * (-jnp.exp(p[1:2, :]))
        ii, jj = _iota(L, L, 0), _iota(L, L, 1)
        CB = _dot(Cm, Bm, _NT)
        y1 = []
        for h in range(H):
            sl = slice(h * P, (h + 1) * P)
            acum, acum_row = _cumsum_forms(a[:, h:h + 1], ii, jj)
            decay = jnp.exp(jnp.where(ii >= jj, acum - acum_row, NEG_INF))
            xh = x[:, sl]
            xdt = xh * dt[:, h:h + 1]
            S0 = state[sl, :]
            yh = _dot(CB * decay, xdt) + _dot(Cm, S0, _NT) * jnp.exp(acum)
            a_last = acum[L - 1:L, :]
            state[sl, :] = S0 * jnp.exp(a_last) + _dot(xdt * jnp.exp(a_last - acum), Bm, _TN)
            y_ref[:, sl] = yh
            y1.append(yh + p[2:3, h:h + 1] * xh)
        y2 = jnp.concatenate(y1, axis=1) * _silu(z_ref[...])
        r = lax.rsqrt(jnp.mean(y2 * y2, axis=-1, keepdims=True) + EPS)
        yn_ref[...] = (y2 * r * nw_ref[0]).astype(BF16)

    x_spec, b_spec, c_spec, z_spec, dt_spec, p_spec, nw_spec, s_spec = _ssd_specs(L, lambda c: c)
    y_spec = pl.BlockSpec((L, W), lambda g, c: (c, g))
    return pl.pallas_call(
        body, name=name, grid=(SSM_GROUPS, nc),
        in_specs=[x_spec, b_spec, c_spec, z_spec, dt_spec, p_spec, nw_spec],
        out_specs=(y_spec, y_spec, s_spec),
        out_shape=(jax.ShapeDtypeStruct((S, SSM_GROUPS * W), F32), jax.ShapeDtypeStruct((S, SSM_GROUPS * W), BF16),
                   jax.ShapeDtypeStruct((SSM_GROUPS, nc, W, N), F32)),
        scratch_shapes=[pltpu.VMEM((W, N), F32)],
        compiler_params=_params("parallel", "arbitrary"))(conv, conv, conv, proj, dt_raw, pvec, nw)


def _ssd_bwd(dyn, conv, proj, dt_raw, pvec, nw, y_ssd, states, name):
    S = conv.shape[0]
    L, P, N, H, W = SSM_CHUNK, SSM_HEAD_DIM, SSM_D_STATE, SSM_HEADS_PER_GROUP, SSM_GROUP_WIDTH
    nc = S // L

    def body(dyn_ref, x_ref, b_ref, c_ref, z_ref, dt_ref, p_ref, nw_ref, y_ref, s0_ref,
             dx_ref, db_ref, dc_ref, dz_ref, ddt_ref, dp_ref, dnw_ref, dstate):
        c = pl.program_id(1)

        @pl.when(c == 0)
        def _():
            dstate[...] = jnp.zeros_like(dstate)
            dp_ref[...] = jnp.zeros_like(dp_ref)
            dnw_ref[...] = jnp.zeros_like(dnw_ref)

        x, Bm, Cm, z, yssd = x_ref[...], b_ref[...], c_ref[...], z_ref[...], y_ref[...]
        p = p_ref[0]
        dtr = dt_ref[0] + p[0:1, :]
        dt = _softplus(dtr)
        A = -jnp.exp(p[1:2, :])
        a = dt * A
        ii, jj = _iota(L, L, 0), _iota(L, L, 1)
        Dfull = jnp.concatenate([jnp.broadcast_to(p[2:3, h:h + 1], (1, P)) for h in range(H)], axis=1)
        y1 = yssd + Dfull * x
        sz = _silu(z)
        y2 = y1 * sz
        r = lax.rsqrt(jnp.mean(y2 * y2, axis=-1, keepdims=True) + EPS)
        y2h = y2 * r
        dyn_v = dyn_ref[...]
        dnw_ref[0] += _colsum(dyn_v * y2h)
        dy2h = dyn_v * nw_ref[0]
        dy2 = r * (dy2h - y2h * jnp.mean(dy2h * y2h, axis=-1, keepdims=True))
        dz_ref[...] = (dy2 * y1 * _dsilu(z)).astype(BF16)
        dy1 = dy2 * sz
        CB = _dot(Cm, Bm, _NT)
        dCB = jnp.zeros((L, L), F32)
        dC = jnp.zeros((L, N), F32)
        dB = jnp.zeros((L, N), F32)
        ddt_cols, d_bias, d_alog, d_skip = [], [], [], []
        last = (_iota(L, 1, 0) == L - 1)
        for h in range(H):
            sl = slice(h * P, (h + 1) * P)
            acum, acum_row = _cumsum_forms(a[:, h:h + 1], ii, jj)
            decay = jnp.exp(jnp.where(ii >= jj, acum - acum_row, NEG_INF))
            eac = jnp.exp(acum)
            a_last = acum[L - 1:L, :]
            wdec = jnp.exp(a_last - acum)
            xh, dth = x[:, sl], dt[:, h:h + 1]
            X = xh * dth
            dY = dy1[:, sl]
            S0, dS1 = s0_ref[0, 0, sl, :], dstate[sl, :]
            M = CB * decay
            dXs = _dot(Bm, dS1, _NT) * wdec
            dX = _dot(M, dY, _TN) + dXs
            G = _dot(dY, X, _NT)
            Q = G * M
            y_off = _dot(Cm, S0, _NT) * eac
            dCB = dCB + G * decay
            dC = dC + _dot(dY * eac, S0)
            dB = dB + _dot(X * wdec, dS1)
            dstate[sl, :] = jnp.exp(a_last) * dS1 + _dot(dY * eac, Cm, _TN)
            d_alast = jnp.sum(_rowsum(X * dXs), axis=0, keepdims=True) + jnp.exp(a_last) * jnp.sum(
                _rowsum(dS1 * S0), axis=0, keepdims=True)
            dacum = (_rowsum(Q) - _rowsum(jnp.where(ii == jj, _colsum(Q), 0.0)) + _rowsum(dY * y_off)
                     - _rowsum(X * dXs) + jnp.where(last, d_alast, 0.0))
            da = _rev_cumsum_col(dacum, ii, jj)
            ddt = da * A[:, h:h + 1] + _rowsum(dX * xh)
            ddt_raw = ddt * _sigmoid(dtr[:, h:h + 1])
            ddt_cols.append(ddt_raw)
            d_bias.append(_colsum(ddt_raw))
            d_alog.append(_colsum(da * dth) * A[:, h:h + 1])
            d_skip.append(jnp.sum(_rowsum(dY * xh), axis=0, keepdims=True))
            dx_ref[:, sl] = dX * dth + p[2:3, h:h + 1] * dY
        ddt_ref[0] = jnp.concatenate(ddt_cols, axis=1)
        dp_ref[0] += jnp.concatenate([jnp.concatenate(d_bias, axis=1), jnp.concatenate(d_alog, axis=1),
                                      jnp.concatenate(d_skip, axis=1)], axis=0)
        dc_ref[...] = dC + _dot(dCB, Bm)
        db_ref[...] = dB + _dot(dCB, Cm, _TN)

    rev = lambda c: nc - 1 - c
    x_spec, b_spec, c_spec, z_spec, dt_spec, p_spec, nw_spec, s_spec = _ssd_specs(L, rev)
    y_spec = pl.BlockSpec((L, W), lambda g, c: (rev(c), g))
    n_spec = pl.BlockSpec((L, N), lambda g, c: (rev(c), g))
    return pl.pallas_call(
        body, name=name, grid=(SSM_GROUPS, nc),
        in_specs=[y_spec, x_spec, b_spec, c_spec, z_spec, dt_spec, p_spec, nw_spec, y_spec, s_spec],
        out_specs=(y_spec, n_spec, n_spec, y_spec, dt_spec, p_spec, nw_spec),
        out_shape=(jax.ShapeDtypeStruct((S, SSM_GROUPS * W), F32), jax.ShapeDtypeStruct((S, SSM_GROUPS * N), F32),
                   jax.ShapeDtypeStruct((S, SSM_GROUPS * N), F32), jax.ShapeDtypeStruct((S, SSM_GROUPS * W), BF16),
                   jax.ShapeDtypeStruct((SSM_GROUPS, S, H), F32), jax.ShapeDtypeStruct((SSM_GROUPS, 3, H), F32),
                   jax.ShapeDtypeStruct((SSM_GROUPS, 1, W), F32)),
        scratch_shapes=[pltpu.VMEM((W, N), F32)],
        compiler_params=_params("parallel", "arbitrary"))(dyn, conv, conv, conv, proj, dt_raw, pvec, nw, y_ssd, states)


def _unit_lower_inverse(A, ii, jj):
    eye = (ii == jj).astype(F32)
    same = (ii // GDN_INV_BLOCK) == (jj // GDN_INV_BLOCK)
    Ad = jnp.where(same, A, 0.0)
    Ao = A - Ad
    X = eye - Ad
    Pw = Ad
    for _ in range(3):
        Pw = _dot3(Pw, Pw)
        X = _dot3(X, eye + Pw)
    Bm = _dot3(X, Ao)
    B2 = _dot3(Bm, Bm)
    return _dot3(_dot3(eye - Bm, eye + B2), X)


def _gdn_specs(L, order):
    Hd, W = GDN_HEAD, GDN_V_PER_QK * GDN_HEAD
    q_spec = pl.BlockSpec((L, Hd), lambda h, c: (order(c), (C_QKV - C_XBC) // Hd + h))
    k_spec = pl.BlockSpec((L, Hd), lambda h, c: (order(c), (C_QKV - C_XBC + 1024) // Hd + h))
    v_spec = pl.BlockSpec((L, W), lambda h, c: (order(c), (C_QKV - C_XBC + 2048) // W + h))
    z_spec = pl.BlockSpec((L, W), lambda h, c: (order(c), C_ZG // W + h))
    ba_spec = pl.BlockSpec((1, L, GDN_V_PER_QK), lambda h, c: (h, order(c), 0))
    p_spec = pl.BlockSpec((1, 2, GDN_V_PER_QK), lambda h, c: (h, 0, 0))
    nw_spec = pl.BlockSpec((1, Hd), lambda h, c: (0, 0))
    s_spec = pl.BlockSpec((1, 1, W, Hd), lambda h, c: (h, order(c), 0, 0))
    return q_spec, k_spec, v_spec, z_spec, ba_spec, p_spec, nw_spec, s_spec


def _gdn_chunk(qa, ka, b_col, a_col, p, j, ii, jj):
    L = qa.shape[0]
    rq = lax.rsqrt(_rowsum(qa * qa) + EPS)
    rk = lax.rsqrt(_rowsum(ka * ka) + EPS)
    q = qa * rq * (GDN_HEAD ** -0.5)
    k = ka * rk
    beta = _sigmoid(b_col)
    sp_in = a_col + p[0:1, j:j + 1]
    neg_ea = -jnp.exp(p[1:2, j:j + 1])
    g = neg_ea * _softplus(sp_in)
    gcum, gcum_row = _cumsum_forms(g, ii, jj)
    Dm = jnp.exp(jnp.where(ii >= jj, gcum - gcum_row, NEG_INF))
    kbeta = k * beta
    Am = jnp.where(ii > jj, _dot(kbeta, k, _NT) * Dm, 0.0)
    T = _unit_lower_inverse(Am, ii, jj)
    eg = jnp.exp(gcum)
    g_last = gcum[L - 1:L, :]
    return dict(rq=rq, rk=rk, q=q, k=k, beta=beta, sp_in=sp_in, neg_ea=neg_ea, g=g, gcum=gcum, Dm=Dm, kbeta=kbeta,
                Am=Am, T=T, eg=eg, g_last=g_last, kdec=k * jnp.exp(g_last - gcum))


def _gdn_fwd(conv, proj, b_raw, a_raw, pvec, nw, name):
    S = conv.shape[0]
    L, Hd, J = GDN_CHUNK, GDN_HEAD, GDN_V_PER_QK
    W = J * Hd
    nc = S // L

    def body(q_ref, k_ref, v_ref, z_ref, b_ref, a_ref, p_ref, nw_ref, o_ref, on_ref, s0_ref, state):
        c = pl.program_id(1)

        @pl.when(c == 0)
        def _():
            state[...] = jnp.zeros_like(state)

        s0_ref[0, 0] = state[...]
        ii, jj = _iota(L, L, 0), _iota(L, L, 1)
        qa, ka, p = q_ref[...], k_ref[...], p_ref[0]
        for j in range(J):
            sl = slice(j * Hd, (j + 1) * Hd)
            t = _gdn_chunk(qa, ka, b_ref[0][:, j:j + 1], a_ref[0][:, j:j + 1], p, j, ii, jj)
            v = v_ref[:, sl]
            S0 = state[sl, :]
            U = _dot3(t["T"], v * t["beta"])
            Wm = _dot3(t["T"], t["kbeta"] * t["eg"])
            Vn = U - _dot(Wm, S0)
            o = _dot(t["q"] * t["eg"], S0) + _dot(_dot(t["q"], t["k"], _NT) * t["Dm"], Vn)
            state[sl, :] = S0 * jnp.exp(t["g_last"]) + _dot(t["kdec"], Vn, _TN)
            o_ref[:, sl] = o
            r = lax.rsqrt(jnp.mean(o * o, axis=-1, keepdims=True) + EPS)
            on_ref[:, sl] = ((o * r * nw_ref[...]) * _silu(z_ref[:, sl])).astype(BF16)

    q_spec, k_spec, v_spec, z_spec, ba_spec, p_spec, nw_spec, s_spec = _gdn_specs(L, lambda c: c)
    o_spec = pl.BlockSpec((L, W), lambda h, c: (c, h))
    return pl.pallas_call(
        body, name=name, grid=(GDN_QK_HEADS, nc),
        in_specs=[q_spec, k_spec, v_spec, z_spec, ba_spec, ba_spec, p_spec, nw_spec],
        out_specs=(o_spec, o_spec, s_spec),
        out_shape=(jax.ShapeDtypeStruct((S, GDN_QK_HEADS * W), F32), jax.ShapeDtypeStruct((S, GDN_QK_HEADS * W), BF16),
                   jax.ShapeDtypeStruct((GDN_QK_HEADS, nc, W, Hd), F32)),
        scratch_shapes=[pltpu.VMEM((W, Hd), F32)],
        compiler_params=_params("parallel", "arbitrary"))(conv, conv, conv, proj, b_raw, a_raw, pvec, nw)


def _gdn_bwd(don, conv, proj, b_raw, a_raw, pvec, nw, o_pre, states, name):
    S = conv.shape[0]
    L, Hd, J = GDN_CHUNK, GDN_HEAD, GDN_V_PER_QK
    W = J * Hd
    nc = S // L

    def body(don_ref, q_ref, k_ref, v_ref, z_ref, b_ref, a_ref, p_ref, nw_ref, o_ref, s0_ref,
             dq_ref, dk_ref, dv_ref, dz_ref, db_ref, da_ref, dp_ref, dnw_ref, dstate):
        c = pl.program_id(1)

        @pl.when(c == 0)
        def _():
            dstate[...] = jnp.zeros_like(dstate)
            dp_ref[...] = jnp.zeros_like(dp_ref)
            dnw_ref[...] = jnp.zeros_like(dnw_ref)

        ii, jj = _iota(L, L, 0), _iota(L, L, 1)
        last = (_iota(L, 1, 0) == L - 1)
        qa, ka, p = q_ref[...], k_ref[...], p_ref[0]
        dq = jnp.zeros((L, Hd), F32)
        dk = jnp.zeros((L, Hd), F32)
        db_cols, da_cols, d_bias, d_alog = [], [], [], []
        for j in range(J):
            sl = slice(j * Hd, (j + 1) * Hd)
            t = _gdn_chunk(qa, ka, b_ref[0][:, j:j + 1], a_ref[0][:, j:j + 1], p, j, ii, jj)
            q, k, beta, eg, Dm, T, kbeta, kdec = (t[n] for n in ("q", "k", "beta", "eg", "Dm", "T", "kbeta", "kdec"))
            v, z, o = v_ref[:, sl], z_ref[:, sl], o_ref[:, sl]
            S0, dS1 = s0_ref[0, 0, sl, :], dstate[sl, :]
            sz = _silu(z)
            r = lax.rsqrt(jnp.mean(o * o, axis=-1, keepdims=True) + EPS)
            oh = o * r
            d_on = don_ref[:, sl]
            dz_ref[:, sl] = (d_on * (oh * nw_ref[...]) * _dsilu(z)).astype(BF16)
            dn = d_on * sz
            dnw_ref[0] += _colsum(dn * oh)
            doh = dn * nw_ref[...]
            dO = r * (doh - oh * jnp.mean(doh * oh, axis=-1, keepdims=True))
            Rw = kbeta * eg
            U = _dot3(T, v * beta)
            Wm = _dot3(T, Rw)
            Vn = U - _dot(Wm, S0)
            Pm = _dot(q, k, _NT) * Dm
            o_inter = _dot(q * eg, S0)
            dVn = _dot(Pm, dO, _TN) + _dot(kdec, dS1)
            dP = _dot(dO, Vn, _NT)
            dQK = dP * Dm
            dq = dq + _dot(dQK, k) + _dot(dO, S0, _NT) * eg
            dKd = _dot(Vn, dS1, _NT)
            dk = dk + _dot(dQK, q, _TN) + dKd * jnp.exp(t["g_last"] - t["gcum"])
            dstate[sl, :] = jnp.exp(t["g_last"]) * dS1 + _dot(q * eg, dO, _TN) - _dot(Wm, dVn, _TN)
            dW = -_dot(dVn, S0, _NT)
            dRu = _dot3(T, dVn, _TN)
            dRw = _dot3(T, dW, _TN)
            dA = jnp.where(ii > jj, -(_dot(dRu, U, _NT) + _dot(dRw, Wm, _NT)), 0.0)
            dKK = dA * Dm
            dkbeta = _dot(dKK, k) + dRw * eg
            dk = dk + _dot(dKK, kbeta, _TN) + dkbeta * beta
            dbeta = _rowsum(dkbeta * k) + _rowsum(dRu * v)
            dv_ref[:, sl] = dRu * beta
            Q = dA * t["Am"] + dP * Pm
            rho = _rowsum(dKd * kdec)
            d_glast = jnp.sum(rho, axis=0, keepdims=True) + jnp.exp(t["g_last"]) * jnp.sum(
                _rowsum(dS1 * S0), axis=0, keepdims=True)
            col_of_colsum = _rowsum(jnp.where(ii == jj, _colsum(Q), 0.0))
            dgam = (_rowsum(dRw * Rw) + _rowsum(Q) - col_of_colsum + _rowsum(dO * o_inter) - rho
                    + jnp.where(last, d_glast, 0.0))
            dg = _rev_cumsum_col(dgam, ii, jj)
            d_alog.append(_colsum(dg * t["g"]))
            da_raw = dg * t["neg_ea"] * _sigmoid(t["sp_in"])
            da_cols.append(da_raw)
            d_bias.append(_colsum(da_raw))
            db_cols.append(dbeta * beta * (1.0 - beta))
        qh = qa * t["rq"]
        dqh = dq * (GDN_HEAD ** -0.5)
        dq_ref[...] = t["rq"] * (dqh - qh * _rowsum(dqh * qh))
        dk_ref[...] = t["rk"] * (dk - t["k"] * _rowsum(dk * t["k"]))
        db_ref[0] = jnp.concatenate(db_cols, axis=1)
        da_ref[0] = jnp.concatenate(da_cols, axis=1)
        dp_ref[0] += jnp.concatenate([jnp.concatenate(d_bias, axis=1), jnp.concatenate(d_alog, axis=1)], axis=0)

    rev = lambda c: nc - 1 - c
    q_spec, k_spec, v_spec, z_spec, ba_spec, p_spec, nw_spec, s_spec = _gdn_specs(L, rev)
    o_spec = pl.BlockSpec((L, W), lambda h, c: (rev(c), h))
    h_spec = pl.BlockSpec((L, Hd), lambda h, c: (rev(c), h))
    dnw_spec = pl.BlockSpec((1, 1, Hd), lambda h, c: (h, 0, 0))
    return pl.pallas_call(
        body, name=name, grid=(GDN_QK_HEADS, nc),
        in_specs=[o_spec, q_spec, k_spec, v_spec, z_spec, ba_spec, ba_spec, p_spec, nw_spec, o_spec, s_spec],
        out_specs=(h_spec, h_spec, o_spec, o_spec, ba_spec, ba_spec, p_spec, dnw_spec),
        out_shape=(jax.ShapeDtypeStruct((S, GDN_QK_HEADS * Hd), F32), jax.ShapeDtypeStruct((S, GDN_QK_HEADS * Hd), F32),
                   jax.ShapeDtypeStruct((S, GDN_QK_HEADS * W), F32), jax.ShapeDtypeStruct((S, GDN_QK_HEADS * W), BF16),
                   jax.ShapeDtypeStruct((GDN_QK_HEADS, S, J), F32), jax.ShapeDtypeStruct((GDN_QK_HEADS, S, J), F32),
                   jax.ShapeDtypeStruct((GDN_QK_HEADS, 2, J), F32), jax.ShapeDtypeStruct((GDN_QK_HEADS, 1, Hd), F32)),
        scratch_shapes=[pltpu.VMEM((W, Hd), F32)],
        compiler_params=_params("parallel", "arbitrary"))(don, conv, conv, conv, proj, b_raw, a_raw, pvec, nw, o_pre,
                                                          states)


def _ada_fwd(c_all, w_loc, b_loc, name):
    n = w_loc.shape[1]

    def body(c_ref, w_ref, b_ref, o_ref):
        o_ref[...] = _dot3(_silu(c_ref[...]), w_ref[...]) + b_ref[...]

    return pl.pallas_call(body, name=name, out_shape=jax.ShapeDtypeStruct((N_DEV, n), F32),
                          compiler_params=pltpu.CompilerParams(vmem_limit_bytes=VMEM_LIMIT))(c_all, w_loc, b_loc)


def _ada_bwd(c_all_t, dmod_cols, name):
    Dm, n = c_all_t.shape[0], dmod_cols.shape[1]

    def body(c_ref, d_ref, o_ref):
        ca = _silu(c_ref[...])
        acc = ca[:, 0:1] * d_ref[0:1, :]
        for i in range(1, N_DEV):
            acc = acc + ca[:, i:i + 1] * d_ref[i:i + 1, :]
        o_ref[...] = acc

    return pl.pallas_call(body, name=name, out_shape=jax.ShapeDtypeStruct((Dm, n), F32),
                          compiler_params=pltpu.CompilerParams(vmem_limit_bytes=VMEM_LIMIT))(c_all_t, dmod_cols)


ADAM_BLOCK_BYTES = 12 * 1024 * 1024


def _adam(contrib, w, m, v, name):
    n, R, C = contrib.shape
    tr = R
    while tr % 16 == 0 and (n + 7) * tr * C * 4 > ADAM_BLOCK_BYTES:
        tr //= 2

    def body(c_ref, w_ref, m_ref, v_ref, g_ref, d_ref, nm_ref, nv_ref):
        g = c_ref[0]
        for i in range(1, n):
            g = g + c_ref[i]
        nm = ADAM_B1 * m_ref[...] + (1.0 - ADAM_B1) * g
        nv = ADAM_B2 * v_ref[...] + (1.0 - ADAM_B2) * (g * g)
        m_hat = nm / (1.0 - ADAM_B1 ** ADAM_STEP)
        v_hat = nv / (1.0 - ADAM_B2 ** ADAM_STEP)
        g_ref[...] = g
        d_ref[...] = -ADAM_LR * (m_hat / (jnp.sqrt(v_hat) + ADAM_EPS) + ADAM_WD * w_ref[...])
        nm_ref[...] = nm
        nv_ref[...] = nv

    spec = pl.BlockSpec((tr, C), lambda i: (i, 0))
    shp = jax.ShapeDtypeStruct((R, C), F32)
    return pl.pallas_call(
        body, name=name, grid=(R // tr,), in_specs=[pl.BlockSpec((n, tr, C), lambda i: (0, i, 0)), spec, spec, spec],
        out_specs=(spec,) * 4, out_shape=(shp,) * 4, compiler_params=_params("parallel"))(contrib, w, m, v)


def _exchange(arrays, modes, name):
    n = len(arrays)
    out_shape = tuple(jax.ShapeDtypeStruct((N_DEV,) + a.shape if md == "gather" else a.shape, a.dtype)
                      for a, md in zip(arrays, modes))

    def body(*refs):
        ins, outs = refs[:n], refs[n:2 * n]
        send_sems, recv_sems, loc_sems = refs[2 * n:]
        ix, iy, ic = lax.axis_index("x"), lax.axis_index("y"), lax.axis_index("c")
        me = 4 * ix + 2 * iy + ic
        peers = []
        for m in range(1, N_DEV):
            px = 1 - ix if m & 4 else ix
            py = 1 - iy if m & 2 else iy
            pc = 1 - ic if m & 1 else ic
            peers.append(((px, py, pc), 4 * px + 2 * py + pc))

        def src(k, slot):
            return ins[k] if modes[k] == "gather" else ins[k].at[slot]

        def remote(k, m, to_slot, land_slot):
            return pltpu.make_async_remote_copy(
                src_ref=src(k, to_slot), dst_ref=outs[k].at[land_slot], send_sem=send_sems.at[k, m],
                recv_sem=recv_sems.at[k, m], device_id=peers[m][0], device_id_type=pl.DeviceIdType.MESH)

        local = [pltpu.make_async_copy(src(k, me), outs[k].at[me], loc_sems.at[k]) for k in range(n)]
        for cp in local:
            cp.start()
        sends = [remote(k, m, peers[m][1], me) for m in range(N_DEV - 1) for k in range(n)]
        for cp in sends:
            cp.start()
        for m in range(N_DEV - 1):
            for k in range(n):
                remote(k, m, peers[m][1], peers[m][1]).wait_recv()
        for cp in sends:
            cp.wait_send()
        for cp in local:
            cp.wait()

    any_spec = pl.BlockSpec(memory_space=pl.ANY)
    return pl.pallas_call(
        body, name=name, in_specs=[any_spec] * n, out_specs=(any_spec,) * n, out_shape=out_shape,
        scratch_shapes=[pltpu.SemaphoreType.DMA((n, N_DEV - 1)), pltpu.SemaphoreType.DMA((n, N_DEV - 1)),
                        pltpu.SemaphoreType.DMA((n,))])(*arrays)


W_IN_SPLITS = (0, 2048, 6144, 6176, 10272, 12320, 12336, 12352, 13376, 14400)
N_REPLICATED = 16640
REPLICATED = ("b_ada", "norm_mix_pre", "norm_mix_post", "ssm_conv_b", "ssm_dt_bias", "ssm_A_log", "ssm_D",
              "ssm_norm_w", "gdn_dt_bias", "gdn_A_log", "gdn_norm_w", "norm_mlp_pre", "norm_mlp_post")
WEIGHTS = ("w_ada", "b_ada", "norm_mix_pre", "norm_mix_post", "w_in", "ssm_conv_w", "ssm_conv_b", "ssm_dt_bias",
           "ssm_A_log", "ssm_D", "ssm_norm_w", "gdn_conv_w", "gdn_dt_bias", "gdn_A_log", "gdn_norm_w", "w_ssm_up",
           "w_gdn_up", "w_out", "norm_mlp_pre", "norm_mlp_post", "w_mlp_up", "w_mlp_down")


def _by_cols(t):
    return t.transpose(1, 0, 2).reshape(t.shape[1], N_DEV * t.shape[2])


def _to_col_shards(t):
    R, C8 = t.shape
    return t.reshape(R, N_DEV, C8 // N_DEV).transpose(1, 0, 2)


def _heads_first(t, groups):
    S = t.shape[0]
    return t.reshape(S, groups, t.shape[1] // groups).transpose(1, 0, 2)


def _heads_last(t):
    return t.transpose(1, 0, 2).reshape(t.shape[1], t.shape[0] * t.shape[2])


def kernel(x, c, w_ada, b_ada, norm_mix_pre, norm_mix_post, w_in, ssm_conv_w, ssm_conv_b, ssm_dt_bias, ssm_A_log, ssm_D, ssm_norm_w, gdn_conv_w, gdn_dt_bias, gdn_A_log, gdn_norm_w, w_ssm_up, w_gdn_up, w_out, norm_mlp_pre, norm_mlp_post, w_mlp_up, w_mlp_down, loss_target, m_w_ada, m_b_ada, m_norm_mix_pre, m_norm_mix_post, m_w_in, m_ssm_conv_w, m_ssm_conv_b, m_ssm_dt_bias, m_ssm_A_log, m_ssm_D, m_ssm_norm_w, m_gdn_conv_w, m_gdn_dt_bias, m_gdn_A_log, m_gdn_norm_w, m_w_ssm_up, m_w_gdn_up, m_w_out, m_norm_mlp_pre, m_norm_mlp_post, m_w_mlp_up, m_w_mlp_down, v_w_ada, v_b_ada, v_norm_mix_pre, v_norm_mix_post, v_w_in, v_ssm_conv_w, v_ssm_conv_b, v_ssm_dt_bias, v_ssm_A_log, v_ssm_D, v_ssm_norm_w, v_gdn_conv_w, v_gdn_dt_bias, v_gdn_A_log, v_gdn_norm_w, v_w_ssm_up, v_w_gdn_up, v_w_out, v_norm_mlp_pre, v_norm_mlp_post, v_w_mlp_up, v_w_mlp_down):
    S, Dm = x.shape[1], D_MODEL
    me = 4 * lax.axis_index("x") + 2 * lax.axis_index("y") + lax.axis_index("c")
    x2, tgt = x[0], loss_target[0]
    n_ada = w_ada.shape[2]
    given = dict(
        w_ada=(w_ada, m_w_ada, v_w_ada), b_ada=(b_ada, m_b_ada, v_b_ada),
        norm_mix_pre=(norm_mix_pre, m_norm_mix_pre, v_norm_mix_pre),
        norm_mix_post=(norm_mix_post, m_norm_mix_post, v_norm_mix_post), w_in=(w_in, m_w_in, v_w_in),
        ssm_conv_w=(ssm_conv_w, m_ssm_conv_w, v_ssm_conv_w), ssm_conv_b=(ssm_conv_b, m_ssm_conv_b, v_ssm_conv_b),
        ssm_dt_bias=(ssm_dt_bias, m_ssm_dt_bias, v_ssm_dt_bias), ssm_A_log=(ssm_A_log, m_ssm_A_log, v_ssm_A_log),
        ssm_D=(ssm_D, m_ssm_D, v_ssm_D), ssm_norm_w=(ssm_norm_w, m_ssm_norm_w, v_ssm_norm_w),
        gdn_conv_w=(gdn_conv_w, m_gdn_conv_w, v_gdn_conv_w), gdn_dt_bias=(gdn_dt_bias, m_gdn_dt_bias, v_gdn_dt_bias),
        gdn_A_log=(gdn_A_log, m_gdn_A_log, v_gdn_A_log), gdn_norm_w=(gdn_norm_w, m_gdn_norm_w, v_gdn_norm_w),
        w_ssm_up=(w_ssm_up, m_w_ssm_up, v_w_ssm_up), w_gdn_up=(w_gdn_up, m_w_gdn_up, v_w_gdn_up),
        w_out=(w_out, m_w_out, v_w_out), norm_mlp_pre=(norm_mlp_pre, m_norm_mlp_pre, v_norm_mlp_pre),
        norm_mlp_post=(norm_mlp_post, m_norm_mlp_post, v_norm_mlp_post), w_mlp_up=(w_mlp_up, m_w_mlp_up, v_w_mlp_up),
        w_mlp_down=(w_mlp_down, m_w_mlp_down, v_w_mlp_down))

    (c_all, scw, gcw, g_in, g_su, g_gu, g_out, g_mu, g_md) = _exchange(
        [c, ssm_conv_w[0], gdn_conv_w[0], w_in[0].astype(BF16), w_ssm_up[0].astype(BF16), w_gdn_up[0].astype(BF16),
         w_out[0].astype(BF16), w_mlp_up[0].astype(BF16), w_mlp_down[0].astype(BF16)], ["gather"] * 9, "gather_weights")
    c_all = c_all.reshape(N_DEV, Dm)
    wf = _by_cols(g_in)
    sp = W_IN_SPLITS
    w_main = jnp.concatenate([wf[:, sp[0]:sp[2]], wf[:, sp[3]:sp[5]], wf[:, sp[7]:sp[9]]], axis=1)
    w_small = jnp.concatenate([wf[:, sp[2]:sp[3]], wf[:, sp[5]:sp[7]], jnp.zeros((Dm, N_SMALL - 64), BF16)], axis=1)
    w_su, w_gu = g_su.reshape(2 * Dm, Dm), g_gu.reshape(2 * Dm, Dm)
    w_o, w_mu, w_md = g_out.reshape(Dm, Dm), _by_cols(g_mu), g_md.reshape(4 * Dm, Dm)
    conv_w = jnp.concatenate([_by_cols(scw), _by_cols(gcw)], axis=1)
    conv_b = jnp.concatenate([ssm_conv_b, jnp.zeros_like(ssm_conv_b)], axis=1)

    b_loc = lax.dynamic_slice(b_ada, (0, me * n_ada), (1, n_ada))
    mod_part = _ada_fwd(c_all, w_ada[0], b_loc, "ada_fwd")
    (mod_rows,) = _exchange([mod_part.reshape(N_DEV, 1, n_ada)], ["a2a"], "exchange_mod")
    mod = mod_rows.reshape(1, 6 * Dm)
    sh1, sc1, g1, sh2, sc2, g2 = [mod[:, i * Dm:(i + 1) * Dm] for i in range(6)]

    h = _pre_fwd(x2, norm_mix_pre, sc1, sh1, "pre_mix")
    proj = _mm(h, w_main, S, N_MAIN, Dm, mode="nn", out_dtype=F32, name="proj_main")
    small = _mm(h, w_small, S, N_SMALL, Dm, mode="nn", out_dtype=F32, name="proj_small")
    conv = _conv_fwd(proj, conv_w, conv_b, "conv_fwd")
    dt_g, b_g, a_g = _heads_first(small[:, 0:32], 8), _heads_first(small[:, 32:48], 8), _heads_first(small[:, 48:64], 8)
    pv_ssm = jnp.stack([ssm_dt_bias.reshape(8, 4), ssm_A_log.reshape(8, 4), ssm_D.reshape(8, 4)], axis=1)
    nw_ssm = ssm_norm_w.reshape(8, 1, SSM_GROUP_WIDTH)
    pv_gdn = jnp.stack([gdn_dt_bias.reshape(8, 2), gdn_A_log.reshape(8, 2)], axis=1)
    y_ssd, ysn, st_ssm = _ssd_fwd(conv, proj, dt_g, pv_ssm, nw_ssm, "ssd_fwd")
    o_pre, ogn, st_gdn = _gdn_fwd(conv, proj, b_g, a_g, pv_gdn, gdn_norm_w, "gdn_fwd")
    ys = _mm(ysn, w_su, S, Dm, 2 * Dm, mode="nn", out_dtype=F32, name="ssm_up")
    yg = _mm(ogn, w_gu, S, Dm, 2 * Dm, mode="nn", out_dtype=F32, name="gdn_up")
    merged = _merge_fwd(ys, yg, proj, "merge_fwd")
    mo = _mm(merged, w_o, S, Dm, Dm, mode="nn", out_dtype=F32, name="mix_out")
    x1 = _post_fwd(x2, mo, norm_mix_post, g1, "post_mix")
    h2 = _pre_fwd(x1, norm_mlp_pre, sc2, sh2, "pre_mlp")
    u, act = _mm(h2, w_mu, S, 4 * Dm, Dm, mode="nn", out_dtype=F32, epi="relu2", name="mlp_up")
    y_mlp = _mm(act, w_md, S, Dm, 4 * Dm, mode="nn", out_dtype=F32, name="mlp_down")
    dx2, loss_loc = _final_fwd(x1, y_mlp, norm_mlp_post, g2, tgt, "post_mlp_loss")

    dy, dg2, dw_post2 = _post_bwd(dx2, y_mlp, norm_mlp_post, g2, "post_mlp_bwd")
    du = _mm(dy, w_md, S, 4 * Dm, Dm, mode="nt", out_dtype=BF16, epi="drelu2", extra=u, name="mlp_down_dx")
    gw_md = _mm(act, dy, 4 * Dm, Dm, S, mode="tn", out_dtype=F32, name="mlp_down_dw")
    dh2 = _mm(du, w_mu, S, Dm, 4 * Dm, mode="nt", out_dtype=F32, name="mlp_up_dx")
    gw_mu = _mm(h2, du, Dm, 4 * Dm, S, mode="tn", out_dtype=F32, name="mlp_up_dw")
    dx1, dsh2, dsc2, dw_pre2 = _pre_bwd(dh2, x1, norm_mlp_pre, sc2, dx2, "pre_mlp_bwd")
    dmo, dg1, dw_post1 = _post_bwd(dx1, mo, norm_mix_post, g1, "post_mix_bwd")
    dmerged = _mm(dmo, w_o, S, Dm, Dm, mode="nt", out_dtype=F32, name="mix_out_dx")
    gw_o = _mm(merged, dmo, Dm, Dm, S, mode="tn", out_dtype=F32, name="mix_out_dw")
    dys, dyg, dgate = _merge_bwd(dmerged, ys, yg, proj, "merge_bwd")
    dysn = _mm(dys, w_su, S, 2 * Dm, Dm, mode="nt", out_dtype=F32, name="ssm_up_dx")
    gw_su = _mm(ysn, dys, 2 * Dm, Dm, S, mode="tn", out_dtype=F32, name="ssm_up_dw")
    dogn = _mm(dyg, w_gu, S, 2 * Dm, Dm, mode="nt", out_dtype=F32, name="gdn_up_dx")
    gw_gu = _mm(ogn, dyg, 2 * Dm, Dm, S, mode="tn", out_dtype=F32, name="gdn_up_dw")
    dxs, dBm, dCm, dz_s, ddt_g, dpv_ssm, dnw_ssm = _ssd_bwd(dysn, conv, proj, dt_g, pv_ssm, nw_ssm, y_ssd, st_ssm,
                                                            "ssd_bwd")
    dq, dk, dv, dz_g, db_g, da_g, dpv_gdn, dnw_gdn = _gdn_bwd(dogn, conv, proj, b_g, a_g, pv_gdn, gdn_norm_w, o_pre,
                                                              st_gdn, "gdn_bwd")
    conv_pieces = []
    for nm, d_act, col0 in (("xs", dxs, 0), ("B", dBm, 2048), ("C", dCm, 3072), ("q", dq, 4096), ("k", dk, 5120),
                            ("v", dv, 6144)):
        conv_pieces.append(_conv_bwd(d_act, proj, conv_w, conv_b, col0, "conv_bwd_" + nm) + (col0,))
    d_small = jnp.concatenate([_heads_last(ddt_g), _heads_last(db_g), _heads_last(da_g),
                               jnp.zeros((S, N_SMALL - 64), F32)], axis=1).astype(BF16)
    pieces = [(dz_s, C_ZS)] + [(p[0], C_XBC + p[3]) for p in conv_pieces] + [(dz_g, C_ZG), (dgate, C_GS)]
    dh = _mm(d_small, w_small, S, Dm, N_SMALL, mode="nt", out_dtype=F32, name="proj_small_dx")
    gw_small = _mm(h, d_small, Dm, N_SMALL, S, mode="tn", out_dtype=F32, name="proj_small_dw")
    gw_cols = []
    for d_piece, off in pieces:
        width = d_piece.shape[1]
        dh = _mm(d_piece, w_main, S, Dm, width, mode="nt", out_dtype=F32, b_off=(0, off), add=dh,
                 name="proj_dx_%d" % off)
        gw_cols.append(_mm(h, d_piece, Dm, width, S, mode="tn", out_dtype=F32, name="proj_dw_%d" % off))
    dx, dsh1, dsc1, dw_pre1 = _pre_bwd(dh, x2, norm_mix_pre, sc1, dx1, "pre_mix_bwd")

    main_cols = jnp.concatenate(gw_cols, axis=1)
    gw_in = jnp.concatenate([main_cols[:, 0:C_QKV], gw_small[:, 0:32], main_cols[:, C_QKV:C_GS], gw_small[:, 32:64],
                             main_cols[:, C_GS:N_MAIN]], axis=1)
    (r_in, r_su, r_gu, r_o, r_mu, r_md) = _exchange(
        [_to_col_shards(gw_in), gw_su.reshape(N_DEV, -1, Dm), gw_gu.reshape(N_DEV, -1, Dm),
         gw_o.reshape(N_DEV, -1, Dm), _to_col_shards(gw_mu), gw_md.reshape(N_DEV, -1, Dm)], ["a2a"] * 6,
        "exchange_grads")

    dconv_w = jnp.concatenate([p[1] for p in conv_pieces], axis=1)
    dconv_b = jnp.concatenate([p[2] for p in conv_pieces[:3]], axis=1)
    dmod = jnp.concatenate([dsh1, dsc1, dg1, dsh2, dsc2, dg2], axis=1)
    small_vec = jnp.concatenate(
        [dmod, dw_pre1, dw_post1, dconv_b, dpv_ssm[:, 0].reshape(1, 32), dpv_ssm[:, 1].reshape(1, 32),
         dpv_ssm[:, 2].reshape(1, 32), dnw_ssm.reshape(1, 2048), dpv_gdn[:, 0].reshape(1, 16),
         dpv_gdn[:, 1].reshape(1, 16), jnp.sum(dnw_gdn, axis=0), dw_pre2, dw_post2, dconv_w.reshape(1, -1)], axis=1)
    n_vec = small_vec.shape[1]
    small_vec = jnp.pad(small_vec, ((0, 0), (0, (-n_vec) % 1024))).reshape(-1, 1024)
    (small_all,) = _exchange([small_vec], ["gather"], "gather_small_grads")
    small_all = small_all.reshape(N_DEV, -1)
    dmod_cols = lax.dynamic_slice(small_all, (0, me * n_ada), (N_DEV, n_ada))
    gw_ada = _ada_bwd(c_all.T, dmod_cols, "ada_bwd")
    conv_all = small_all[:, N_REPLICATED:n_vec].reshape(N_DEV, CONV_K, 2 * N_DEV * 512)
    conv_contrib = jnp.concatenate(
        [lax.dynamic_slice(conv_all, (0, 0, me * 512), (N_DEV, CONV_K, 512)),
         lax.dynamic_slice(conv_all, (0, 0, N_DEV * 512 + me * 512), (N_DEV, CONV_K, 512))], axis=1)
    rep_contrib = small_all[:, :N_REPLICATED].reshape(N_DEV, N_REPLICATED // 128, 128)

    results = {}

    def adam_big(nm, contrib):
        w3 = given[nm]
        res = _adam(contrib, w3[0][0], w3[1][0], w3[2][0], "adam_" + nm)
        results[nm] = tuple(r.reshape(w3[0].shape) for r in res)

    adam_big("w_ada", gw_ada[None])
    adam_big("w_in", r_in)
    adam_big("w_ssm_up", r_su)
    adam_big("w_gdn_up", r_gu)
    adam_big("w_out", r_o)
    adam_big("w_mlp_up", r_mu)
    adam_big("w_mlp_down", r_md)
    packed = [jnp.concatenate([given[nm][i] for nm in REPLICATED], axis=1).reshape(N_REPLICATED // 128, 128)
              for i in range(3)]
    rep_res = _adam(rep_contrib, packed[0], packed[1], packed[2], "adam_replicated")
    pos = 0
    for nm in REPLICATED:
        size = given[nm][0].shape[1]
        results[nm] = tuple(r.reshape(1, N_REPLICATED)[:, pos:pos + size] for r in rep_res)
        pos += size
    conv_wmv = [jnp.concatenate([given["ssm_conv_w"][i][0], given["gdn_conv_w"][i][0]], axis=0) for i in range(3)]
    conv_res = _adam(conv_contrib, conv_wmv[0], conv_wmv[1], conv_wmv[2], "adam_conv_w")
    results["ssm_conv_w"] = tuple(r[None, :CONV_K] for r in conv_res)
    results["gdn_conv_w"] = tuple(r[None, CONV_K:] for r in conv_res)

    loss = lax.psum(loss_loc[0, 0], ("x", "y", "c"))
    return (loss, dx[None]) + tuple(results[nm][i] for i in range(4) for nm in WEIGHTS)
```

```python
import jax
import jax.numpy as jnp
from jax import lax
from jax.experimental import pallas as pl
from jax.experimental.pallas import tpu as pltpu

F32 = jnp.float32
BF16 = jnp.bfloat16
N_DEV = 8
D_MODEL = 1024
EPS = 1e-6
CONV_K = 4
SSM_CHUNK = 128
SSM_HEAD_DIM = 64
SSM_D_STATE = 128
SSM_GROUPS = 8
SSM_HEADS_PER_GROUP = 4
SSM_GROUP_WIDTH = SSM_HEADS_PER_GROUP * SSM_HEAD_DIM
SSM_GROUPS_PER_STEP = 2
GDN_CHUNK = 64
GDN_HEAD = 128
GDN_QK_HEADS = 8
GDN_V_PER_QK = 2
GDN_QK_PER_STEP = 4
GDN_INV_BLOCK = 16
C_ZS, C_XBC, C_QKV, C_ZG, C_GS, C_GG, N_MAIN = 0, 2048, 6144, 10240, 12288, 13312, 14336
N_SMALL = 128
ADAM_LR, ADAM_B1, ADAM_B2, ADAM_EPS, ADAM_WD, ADAM_STEP = 0.001, 0.9, 0.999, 1e-08, 0.01, 10
VMEM_LIMIT = 56 * 1024 * 1024
NEG_INF = float("-inf")

_NT = (((1,), (1,)), ((), ()))
_NN = (((1,), (0,)), ((), ()))
_TN = (((0,), (0,)), ((), ()))


def _params(*sem):
    return pltpu.CompilerParams(dimension_semantics=sem, vmem_limit_bytes=VMEM_LIMIT)


def _dot(a, b, dims=_NN):
    return lax.dot_general(a.astype(BF16), b.astype(BF16), dims, preferred_element_type=F32)


def _split(a):
    hi = a.astype(BF16)
    return hi, (a - hi.astype(F32)).astype(BF16)


def _dot3(a, b, dims=_NN):
    ah, al = _split(a)
    bh, bl = _split(b)
    d = lambda u, v: lax.dot_general(u, v, dims, preferred_element_type=F32)
    return d(ah, bh) + (d(ah, bl) + d(al, bh))


def _sigmoid(x):
    return 1.0 / (1.0 + jnp.exp(-x))


def _silu(x):
    return x * _sigmoid(x)


def _dsilu(x):
    s = _sigmoid(x)
    return s * (1.0 + x * (1.0 - s))


def _softplus(x):
    return jnp.maximum(x, 0.0) + jnp.log1p(jnp.exp(-jnp.abs(x)))


def _iota(n, m, d):
    return lax.broadcasted_iota(jnp.int32, (n, m), d)


def _rowsum(x):
    return jnp.sum(x, axis=1, keepdims=True)


def _colsum(x):
    return jnp.sum(x, axis=0, keepdims=True)


def _total(x):
    return _colsum(_rowsum(x))


def _cumsum_forms(col, ii, jj):
    row = _colsum(jnp.where(ii == jj, col, 0.0))
    cum_col = _rowsum(jnp.where(ii >= jj, row, 0.0))
    cum_row = _colsum(jnp.where(ii <= jj, col, 0.0))
    return cum_col, cum_row


def _rev_cumsum_col(col, ii, jj):
    row = _colsum(jnp.where(ii >= jj, col, 0.0))
    return _rowsum(jnp.where(ii == jj, row, 0.0))


def _blk(dim, pref):
    return pref if dim % pref == 0 else dim


def _lockstep(gens):
    gens = list(gens)
    while gens:
        alive = []
        for g in gens:
            try:
                next(g)
                alive.append(g)
            except StopIteration:
                pass
        gens = alive


def _mm(a, b, M, N, K, *, mode, out_dtype, name, a_off=(0, 0), b_off=(0, 0), add=None, epi=None, extra=None,
        tm=1024, tn=1024, tk=1024):
    tm, tn, tk = _blk(M, tm), _blk(N, tn), _blk(K, tk)
    nk = K // tk
    if mode == "tn":
        a_spec = pl.BlockSpec((tk, tm), lambda i, j, k: (k + a_off[0] // tk, i + a_off[1] // tm))
        assert a_off[0] % tk == 0 and a_off[1] % tm == 0
    else:
        a_spec = pl.BlockSpec((tm, tk), lambda i, j, k: (i + a_off[0] // tm, k + a_off[1] // tk))
        assert a_off[0] % tm == 0 and a_off[1] % tk == 0
    if mode == "nt":
        b_spec = pl.BlockSpec((tn, tk), lambda i, j, k: (j + b_off[0] // tn, k + b_off[1] // tk))
        assert b_off[0] % tn == 0 and b_off[1] % tk == 0
    else:
        b_spec = pl.BlockSpec((tk, tn), lambda i, j, k: (k + b_off[0] // tk, j + b_off[1] // tn))
        assert b_off[0] % tk == 0 and b_off[1] % tn == 0
    dims = {"nn": _NN, "nt": _NT, "tn": _TN}[mode]
    o_spec = pl.BlockSpec((tm, tn), lambda i, j, k: (i, j))
    ins, in_specs = [a, b], [a_spec, b_spec]
    if add is not None:
        ins.append(add)
        in_specs.append(o_spec)
    if extra is not None:
        ins.append(extra)
        in_specs.append(o_spec)
    n_in = len(ins)
    if epi == "relu2":
        out_shape = (jax.ShapeDtypeStruct((M, N), F32), jax.ShapeDtypeStruct((M, N), BF16))
        out_specs = (o_spec, o_spec)
    else:
        out_shape = jax.ShapeDtypeStruct((M, N), out_dtype)
        out_specs = o_spec

    def body(*refs):
        a_ref, b_ref = refs[0], refs[1]
        acc = refs[-1]
        outs = refs[n_in:-1]
        k = pl.program_id(2)

        @pl.when(k == 0)
        def _():
            acc[...] = jnp.zeros_like(acc)

        acc[...] += _dot(a_ref[...], b_ref[...], dims)

        @pl.when(k == nk - 1)
        def _():
            r = acc[...]
            pos = 2
            if add is not None:
                r = r + refs[pos][...]
                pos += 1
            if epi == "relu2":
                outs[0][...] = r
                p = jnp.maximum(r, 0.0)
                outs[1][...] = (p * p).astype(BF16)
            elif epi == "drelu2":
                outs[0][...] = (r * (2.0 * jnp.maximum(refs[pos][...], 0.0))).astype(out_dtype)
            else:
                outs[0][...] = r.astype(out_dtype)

    return pl.pallas_call(
        body, name=name, grid=(M // tm, N // tn, nk), in_specs=in_specs, out_specs=out_specs, out_shape=out_shape,
        scratch_shapes=[pltpu.VMEM((tm, tn), F32)],
        compiler_params=_params("parallel", "parallel", "arbitrary"))(*ins)


def _row_spec(tb, d):
    return pl.BlockSpec((tb, d), lambda i: (i, 0))


def _vec_spec(d):
    return pl.BlockSpec((1, d), lambda i: (0, 0))


def _pre_fwd(x, w, sc, sh, name):
    S, Dm = x.shape
    tb = _blk(S, 512)

    def body(x_ref, w_ref, sc_ref, sh_ref, h_ref):
        xv = x_ref[...]
        r = lax.rsqrt(jnp.mean(xv * xv, axis=-1, keepdims=True) + EPS)
        h_ref[...] = ((xv * r * w_ref[...]) * (1.0 + sc_ref[...]) + sh_ref[...]).astype(BF16)

    return pl.pallas_call(
        body, name=name, grid=(S // tb,), in_specs=[_row_spec(tb, Dm)] + [_vec_spec(Dm)] * 3,
        out_specs=_row_spec(tb, Dm), out_shape=jax.ShapeDtypeStruct((S, Dm), BF16),
        compiler_params=_params("parallel"))(x, w, sc, sh)


def _post_fwd(x, y, w, g, name):
    S, Dm = x.shape
    tb = _blk(S, 512)

    def body(x_ref, y_ref, w_ref, g_ref, o_ref):
        yv = y_ref[...]
        r = lax.rsqrt(jnp.mean(yv * yv, axis=-1, keepdims=True) + EPS)
        o_ref[...] = x_ref[...] + g_ref[...] * (yv * r * w_ref[...])

    return pl.pallas_call(
        body, name=name, grid=(S // tb,), in_specs=[_row_spec(tb, Dm)] * 2 + [_vec_spec(Dm)] * 2,
        out_specs=_row_spec(tb, Dm), out_shape=jax.ShapeDtypeStruct((S, Dm), F32),
        compiler_params=_params("parallel"))(x, y, w, g)


def _final_fwd(x, y, w, g, target, name):
    S, Dm = x.shape
    tb = _blk(S, 512)
    nb = S // tb

    def body(x_ref, y_ref, w_ref, g_ref, t_ref, dx_ref, loss_ref, acc):
        i = pl.program_id(0)

        @pl.when(i == 0)
        def _():
            acc[...] = jnp.zeros_like(acc)

        yv = y_ref[...]
        r = lax.rsqrt(jnp.mean(yv * yv, axis=-1, keepdims=True) + EPS)
        e = (x_ref[...] + g_ref[...] * (yv * r * w_ref[...])) - t_ref[...]
        dx_ref[...] = e * (1.0 / Dm)
        acc[...] += _colsum(e * e)

        @pl.when(i == nb - 1)
        def _():
            loss_ref[...] = (0.5 / Dm) * _rowsum(acc[...])

    return pl.pallas_call(
        body, name=name, grid=(nb,), in_specs=[_row_spec(tb, Dm)] * 2 + [_vec_spec(Dm)] * 2 + [_row_spec(tb, Dm)],
        out_specs=(_row_spec(tb, Dm), pl.BlockSpec((1, 1), lambda i: (0, 0))),
        out_shape=(jax.ShapeDtypeStruct((S, Dm), F32), jax.ShapeDtypeStruct((1, 1), F32)),
        scratch_shapes=[pltpu.VMEM((1, Dm), F32)], compiler_params=_params("arbitrary"))(x, y, w, g, target)


def _post_bwd(dxo, y, w, g, name):
    S, Dm = y.shape
    tb = _blk(S, 512)

    def body(d_ref, y_ref, w_ref, g_ref, dy_ref, dg_ref, dw_ref):
        i = pl.program_id(0)

        @pl.when(i == 0)
        def _():
            dg_ref[...] = jnp.zeros_like(dg_ref)
            dw_ref[...] = jnp.zeros_like(dw_ref)

        yv, dv = y_ref[...], d_ref[...]
        r = lax.rsqrt(jnp.mean(yv * yv, axis=-1, keepdims=True) + EPS)
        yh = yv * r
        dg_ref[...] += _colsum(dv * (yh * w_ref[...]))
        dn = dv * g_ref[...]
        dw_ref[...] += _colsum(dn * yh)
        dyh = dn * w_ref[...]
        dy_ref[...] = (r * (dyh - yh * jnp.mean(dyh * yh, axis=-1, keepdims=True))).astype(BF16)

    return pl.pallas_call(
        body, name=name, grid=(S // tb,), in_specs=[_row_spec(tb, Dm)] * 2 + [_vec_spec(Dm)] * 2,
        out_specs=(_row_spec(tb, Dm), _vec_spec(Dm), _vec_spec(Dm)),
        out_shape=(jax.ShapeDtypeStruct((S, Dm), BF16), jax.ShapeDtypeStruct((1, Dm), F32),
                   jax.ShapeDtypeStruct((1, Dm), F32)),
        compiler_params=_params("arbitrary"))(dxo, y, w, g)


def _pre_bwd(dh, x, w, sc, dres, name):
    S, Dm = x.shape
    tb = _blk(S, 512)

    def body(dh_ref, x_ref, w_ref, sc_ref, dr_ref, dx_ref, dsh_ref, dsc_ref, dw_ref):
        i = pl.program_id(0)

        @pl.when(i == 0)
        def _():
            dsh_ref[...] = jnp.zeros_like(dsh_ref)
            dsc_ref[...] = jnp.zeros_like(dsc_ref)
            dw_ref[...] = jnp.zeros_like(dw_ref)

        xv, dv = x_ref[...], dh_ref[...]
        r = lax.rsqrt(jnp.mean(xv * xv, axis=-1, keepdims=True) + EPS)
        xh = xv * r
        one_sc = 1.0 + sc_ref[...]
        dsh_ref[...] += _colsum(dv)
        dsc_ref[...] += _colsum(dv * (xh * w_ref[...]))
        dw_ref[...] += _colsum(dv * one_sc * xh)
        dxh = dv * one_sc * w_ref[...]
        dx_ref[...] = dr_ref[...] + r * (dxh - xh * jnp.mean(dxh * xh, axis=-1, keepdims=True))

    vec = jax.ShapeDtypeStruct((1, Dm), F32)
    return pl.pallas_call(
        body, name=name, grid=(S // tb,),
        in_specs=[_row_spec(tb, Dm)] * 2 + [_vec_spec(Dm)] * 2 + [_row_spec(tb, Dm)],
        out_specs=(_row_spec(tb, Dm), _vec_spec(Dm), _vec_spec(Dm), _vec_spec(Dm)),
        out_shape=(jax.ShapeDtypeStruct((S, Dm), F32), vec, vec, vec),
        compiler_params=_params("arbitrary"))(dh, x, w, sc, dres)


def _merge_fwd(ys, yg, proj, name):
    S, Dm = ys.shape
    tb = _blk(S, 512)

    def body(ys_ref, yg_ref, gs_ref, gg_ref, o_ref):
        o_ref[...] = (_sigmoid(gs_ref[...]) * ys_ref[...] + _sigmoid(gg_ref[...]) * yg_ref[...]).astype(BF16)

    return pl.pallas_call(
        body, name=name, grid=(S // tb,),
        in_specs=[_row_spec(tb, Dm)] * 2 + [pl.BlockSpec((tb, Dm), lambda i: (i, C_GS // Dm)),
                                            pl.BlockSpec((tb, Dm), lambda i: (i, C_GG // Dm))],
        out_specs=_row_spec(tb, Dm), out_shape=jax.ShapeDtypeStruct((S, Dm), BF16),
        compiler_params=_params("parallel"))(ys, yg, proj, proj)


def _merge_bwd(dm, ys, yg, proj, name):
    S, Dm = ys.shape
    tb = _blk(S, 512)

    def body(dm_ref, ys_ref, yg_ref, gs_ref, gg_ref, dys_ref, dyg_ref, dgate_ref):
        d = dm_ref[...]
        ss, sg = _sigmoid(gs_ref[...]), _sigmoid(gg_ref[...])
        dys_ref[...] = (d * ss).astype(BF16)
        dyg_ref[...] = (d * sg).astype(BF16)
        dgate_ref[:, :Dm] = (d * ys_ref[...] * ss * (1.0 - ss)).astype(BF16)
        dgate_ref[:, Dm:] = (d * yg_ref[...] * sg * (1.0 - sg)).astype(BF16)

    return pl.pallas_call(
        body, name=name, grid=(S // tb,),
        in_specs=[_row_spec(tb, Dm)] * 3 + [pl.BlockSpec((tb, Dm), lambda i: (i, C_GS // Dm)),
                                            pl.BlockSpec((tb, Dm), lambda i: (i, C_GG // Dm))],
        out_specs=(_row_spec(tb, Dm), _row_spec(tb, Dm), _row_spec(tb, 2 * Dm)),
        out_shape=(jax.ShapeDtypeStruct((S, Dm), BF16), jax.ShapeDtypeStruct((S, Dm), BF16),
                   jax.ShapeDtypeStruct((S, 2 * Dm), BF16)),
        compiler_params=_params("parallel"))(dm, ys, yg, proj, proj)


CONV_COLS = 128


def _shift_down(x, k, rows):
    return jnp.where(rows >= k, pltpu.roll(x, k, 0), 0.0)


def _shift_up(x, k, rows, S):
    return jnp.where(rows < S - k, pltpu.roll(x, S - k, 0), 0.0)


def _conv_fwd(proj, w, b, name):
    S = proj.shape[0]
    n = w.shape[1]
    cb = CONV_COLS

    def body(x_ref, w_ref, b_ref, o_ref):
        x = x_ref[...]
        rows = _iota(S, cb, 0)
        pre = x * w_ref[CONV_K - 1:CONV_K, :] + b_ref[...]
        for k in range(1, CONV_K):
            pre = pre + _shift_down(x, k, rows) * w_ref[CONV_K - 1 - k:CONV_K - k, :]
        o_ref[...] = _silu(pre)

    return pl.pallas_call(
        body, name=name, grid=(n // cb,),
        in_specs=[pl.BlockSpec((S, cb), lambda j: (0, j + C_XBC // cb)), pl.BlockSpec((CONV_K, cb), lambda j: (0, j)),
                  pl.BlockSpec((1, cb), lambda j: (0, j))],
        out_specs=pl.BlockSpec((S, cb), lambda j: (0, j)), out_shape=jax.ShapeDtypeStruct((S, n), F32),
        compiler_params=_params("parallel"))(proj, w, b)


def _conv_bwd(dact, proj, w, b, col0, name):
    S, n = dact.shape
    cb = CONV_COLS
    o = col0 // cb

    def body(d_ref, x_ref, w_ref, b_ref, dx_ref, dw_ref, db_ref):
        x = x_ref[...]
        rows = _iota(S, cb, 0)
        xs = [x] + [_shift_down(x, k, rows) for k in range(1, CONV_K)]
        pre = xs[0] * w_ref[CONV_K - 1:CONV_K, :] + b_ref[...]
        for k in range(1, CONV_K):
            pre = pre + xs[k] * w_ref[CONV_K - 1 - k:CONV_K - k, :]
        dpre = d_ref[...] * _dsilu(pre)
        db_ref[...] = _colsum(dpre)
        dx = dpre * w_ref[CONV_K - 1:CONV_K, :]
        for k in range(CONV_K):
            dw_ref[CONV_K - 1 - k:CONV_K - k, :] = _colsum(dpre * xs[k])
            if k:
                dx = dx + _shift_up(dpre, k, rows, S) * w_ref[CONV_K - 1 - k:CONV_K - k, :]
        dx_ref[...] = dx.astype(BF16)

    return pl.pallas_call(
        body, name=name, grid=(n // cb,),
        in_specs=[pl.BlockSpec((S, cb), lambda j: (0, j)), pl.BlockSpec((S, cb), lambda j: (0, j + o + C_XBC // cb)),
                  pl.BlockSpec((CONV_K, cb), lambda j: (0, j + o)), pl.BlockSpec((1, cb), lambda j: (0, j + o))],
        out_specs=(pl.BlockSpec((S, cb), lambda j: (0, j)), pl.BlockSpec((CONV_K, cb), lambda j: (0, j)),
                   pl.BlockSpec((1, cb), lambda j: (0, j))),
        out_shape=(jax.ShapeDtypeStruct((S, n), BF16), jax.ShapeDtypeStruct((CONV_K, n), F32),
                   jax.ShapeDtypeStruct((1, n), F32)),
        compiler_params=_params("parallel"))(dact, proj, w, b)


def _ssd_specs(L, order):
    G = SSM_GROUPS_PER_STEP
    W, N = G * SSM_GROUP_WIDTH, G * SSM_D_STATE
    x_spec = pl.BlockSpec((L, W), lambda g, c: (order(c), g))
    b_spec = pl.BlockSpec((L, N), lambda g, c: (order(c), 2048 // N + g))
    c_spec = pl.BlockSpec((L, N), lambda g, c: (order(c), 3072 // N + g))
    z_spec = pl.BlockSpec((L, W), lambda g, c: (order(c), C_ZS // W + g))
    dt_spec = pl.BlockSpec((G, L, SSM_HEADS_PER_GROUP), lambda g, c: (g, order(c), 0))
    p_spec = pl.BlockSpec((G, 3, SSM_HEADS_PER_GROUP), lambda g, c: (g, 0, 0))
    nw_spec = pl.BlockSpec((G, 1, SSM_GROUP_WIDTH), lambda g, c: (g, 0, 0))
    s_spec = pl.BlockSpec((G, 1, SSM_GROUP_WIDTH, SSM_D_STATE), lambda g, c: (g, order(c), 0, 0))
    return x_spec, b_spec, c_spec, z_spec, dt_spec, p_spec, nw_spec, s_spec


def _ssd_fwd(conv, proj, dt_raw, pvec, nw, name):
    S = conv.shape[0]
    L, P, N, H, W, G = SSM_CHUNK, SSM_HEAD_DIM, SSM_D_STATE, SSM_HEADS_PER_GROUP, SSM_GROUP_WIDTH, SSM_GROUPS_PER_STEP
    nc = S // L

    def body(x_ref, b_ref, c_ref, z_ref, dt_ref, p_ref, nw_ref, y_ref, yn_ref, s0_ref, state):
        c = pl.program_id(1)

        @pl.when(c == 0)
        def _():
            state[...] = jnp.zeros_like(state)

        ii, jj = _iota(L, L, 0), _iota(L, L, 1)
        Bm = [b_ref[:, gi * N:(gi + 1) * N] for gi in range(G)]
        Cm = [c_ref[:, gi * N:(gi + 1) * N] for gi in range(G)]
        CB = [_dot(Cm[gi], Bm[gi], _NT) for gi in range(G)]
        y1 = {}
        for gi in range(G):
            s0_ref[gi, 0] = state[gi * W:(gi + 1) * W, :]

        def head(gi, h):
            sl = slice(gi * W + h * P, gi * W + (h + 1) * P)
            p = p_ref[gi]
            dt = _softplus(dt_ref[gi][:, h:h + 1] + p[0:1, h:h + 1])
            a = dt * (-jnp.exp(p[1:2, h:h + 1]))
            acum, acum_row = _cumsum_forms(a, ii, jj)
            decay = jnp.exp(jnp.where(ii >= jj, acum - acum_row, NEG_INF))
            xh = x_ref[:, sl]
            xdt = xh * dt
            S0 = state[sl, :]
            a_last = acum[L - 1:L, :]
            y_diag = _dot(CB[gi] * decay, xdt)
            y_off = _dot(Cm[gi], S0, _NT)
            s_new = _dot(xdt * jnp.exp(a_last - acum), Bm[gi], _TN)
            yield
            yh = y_diag + y_off * jnp.exp(acum)
            state[sl, :] = S0 * jnp.exp(a_last) + s_new
            y_ref[:, sl] = yh
            y1[gi, h] = yh + p[2:3, h:h + 1] * xh

        _lockstep(head(gi, h) for gi in range(G) for h in range(H))
        for gi in range(G):
            gsl = slice(gi * W, (gi + 1) * W)
            y2 = jnp.concatenate([y1[gi, h] for h in range(H)], axis=1) * _silu(z_ref[:, gsl])
            r = lax.rsqrt(jnp.mean(y2 * y2, axis=-1, keepdims=True) + EPS)
            yn_ref[:, gsl] = (y2 * r * nw_ref[gi]).astype(BF16)

    x_spec, b_spec, c_spec, z_spec, dt_spec, p_spec, nw_spec, s_spec = _ssd_specs(L, lambda c: c)
    y_spec = pl.BlockSpec((L, G * W), lambda g, c: (c, g))
    return pl.pallas_call(
        body, name=name, grid=(SSM_GROUPS // G, nc),
        in_specs=[x_spec, b_spec, c_spec, z_spec, dt_spec, p_spec, nw_spec],
        out_specs=(y_spec, y_spec, s_spec),
        out_shape=(jax.ShapeDtypeStruct((S, SSM_GROUPS * W), F32), jax.ShapeDtypeStruct((S, SSM_GROUPS * W), BF16),
                   jax.ShapeDtypeStruct((SSM_GROUPS, nc, W, N), F32)),
        scratch_shapes=[pltpu.VMEM((G * W, N), F32)],
        compiler_params=_params("parallel", "arbitrary"))(conv, conv, conv, proj, dt_raw, pvec, nw)


def _ssd_bwd(dyn, conv, proj, dt_raw, pvec, nw, y_ssd, states, name):
    S = conv.shape[0]
    L, P, N, H, W, G = SSM_CHUNK, SSM_HEAD_DIM, SSM_D_STATE, SSM_HEADS_PER_GROUP, SSM_GROUP_WIDTH, SSM_GROUPS_PER_STEP
    nc = S // L

    def body(dyn_ref, x_ref, b_ref, c_ref, z_ref, dt_ref, p_ref, nw_ref, y_ref, s0_ref,
             dx_ref, db_ref, dc_ref, dz_ref, ddt_ref, dp_ref, dnw_ref, dstate):
        c = pl.program_id(1)

        @pl.when(c == 0)
        def _():
            dstate[...] = jnp.zeros_like(dstate)
            dp_ref[...] = jnp.zeros_like(dp_ref)
            dnw_ref[...] = jnp.zeros_like(dnw_ref)

        ii, jj = _iota(L, L, 0), _iota(L, L, 1)
        last = (_iota(L, 1, 0) == L - 1)
        Bm = [b_ref[:, gi * N:(gi + 1) * N] for gi in range(G)]
        Cm = [c_ref[:, gi * N:(gi + 1) * N] for gi in range(G)]
        CB = [_dot(Cm[gi], Bm[gi], _NT) for gi in range(G)]
        dy1 = []
        for gi in range(G):
            gsl = slice(gi * W, (gi + 1) * W)
            p = p_ref[gi]
            x, z = x_ref[:, gsl], z_ref[:, gsl]
            Dfull = jnp.concatenate([jnp.broadcast_to(p[2:3, h:h + 1], (1, P)) for h in range(H)], axis=1)
            y1 = y_ref[:, gsl] + Dfull * x
            sz = _silu(z)
            y2 = y1 * sz
            r = lax.rsqrt(jnp.mean(y2 * y2, axis=-1, keepdims=True) + EPS)
            y2h = y2 * r
            dyn_v = dyn_ref[:, gsl]
            dnw_ref[gi] += _colsum(dyn_v * y2h)
            dy2h = dyn_v * nw_ref[gi]
            dy2 = r * (dy2h - y2h * jnp.mean(dy2h * y2h, axis=-1, keepdims=True))
            dz_ref[:, gsl] = (dy2 * y1 * _dsilu(z)).astype(BF16)
            dy1.append(dy2 * sz)
        res = {}

        def head(gi, h):
            sl = slice(gi * W + h * P, gi * W + (h + 1) * P)
            p = p_ref[gi]
            dtr = dt_ref[gi][:, h:h + 1] + p[0:1, h:h + 1]
            dth = _softplus(dtr)
            A = -jnp.exp(p[1:2, h:h + 1])
            acum, acum_row = _cumsum_forms(dth * A, ii, jj)
            decay = jnp.exp(jnp.where(ii >= jj, acum - acum_row, NEG_INF))
            eac = jnp.exp(acum)
            a_last = acum[L - 1:L, :]
            wdec = jnp.exp(a_last - acum)
            xh = x_ref[:, sl]
            X = xh * dth
            dY = dy1[gi][:, h * P:(h + 1) * P]
            S0, dS1 = s0_ref[gi, 0, h * P:(h + 1) * P, :], dstate[sl, :]
            M = CB[gi] * decay
            dYe = dY * eac
            dXs_raw = _dot(Bm[gi], dS1, _NT)
            dXm = _dot(M, dY, _TN)
            Gm = _dot(dY, X, _NT)
            y_off_raw = _dot(Cm[gi], S0, _NT)
            dC_s = _dot(dYe, S0)
            dB_s = _dot(X * wdec, dS1)
            dS_c = _dot(dYe, Cm[gi], _TN)
            yield
            dXs = dXs_raw * wdec
            dX = dXm + dXs
            Q = Gm * M
            dstate[sl, :] = jnp.exp(a_last) * dS1 + dS_c
            d_alast = _total(X * dXs) + jnp.exp(a_last) * _total(dS1 * S0)
            dacum = (_rowsum(Q) - _rowsum(jnp.where(ii == jj, _colsum(Q), 0.0)) + _rowsum(dY * (y_off_raw * eac))
                     - _rowsum(X * dXs) + jnp.where(last, d_alast, 0.0))
            da = _rev_cumsum_col(dacum, ii, jj)
            ddt_raw = (da * A + _rowsum(dX * xh)) * _sigmoid(dtr)
            dx_ref[:, sl] = dX * dth + p[2:3, h:h + 1] * dY
            res[gi, h] = (Gm * decay, dC_s, dB_s, ddt_raw, _colsum(ddt_raw), _colsum(da * dth) * A, _total(dY * xh))

        _lockstep(head(gi, h) for gi in range(G) for h in range(H))
        for gi in range(G):
            parts = [res[gi, h] for h in range(H)]
            dCB = sum(pt[0] for pt in parts)
            dc_ref[:, gi * N:(gi + 1) * N] = sum(pt[1] for pt in parts) + _dot(dCB, Bm[gi])
            db_ref[:, gi * N:(gi + 1) * N] = sum(pt[2] for pt in parts) + _dot(dCB, Cm[gi], _TN)
            ddt_ref[gi] = jnp.concatenate([pt[3] for pt in parts], axis=1)
            dp_ref[gi] += jnp.concatenate([jnp.concatenate([pt[k] for pt in parts], axis=1) for k in (4, 5, 6)], axis=0)

    rev = lambda c: nc - 1 - c
    x_spec, b_spec, c_spec, z_spec, dt_spec, p_spec, nw_spec, s_spec = _ssd_specs(L, rev)
    y_spec = pl.BlockSpec((L, G * W), lambda g, c: (rev(c), g))
    n_spec = pl.BlockSpec((L, G * N), lambda g, c: (rev(c), g))
    return pl.pallas_call(
        body, name=name, grid=(SSM_GROUPS // G, nc),
        in_specs=[y_spec, x_spec, b_spec, c_spec, z_spec, dt_spec, p_spec, nw_spec, y_spec, s_spec],
        out_specs=(y_spec, n_spec, n_spec, y_spec, dt_spec, p_spec, nw_spec),
        out_shape=(jax.ShapeDtypeStruct((S, SSM_GROUPS * W), F32), jax.ShapeDtypeStruct((S, SSM_GROUPS * N), F32),
                   jax.ShapeDtypeStruct((S, SSM_GROUPS * N), F32), jax.ShapeDtypeStruct((S, SSM_GROUPS * W), BF16),
                   jax.ShapeDtypeStruct((SSM_GROUPS, S, H), F32), jax.ShapeDtypeStruct((SSM_GROUPS, 3, H), F32),
                   jax.ShapeDtypeStruct((SSM_GROUPS, 1, W), F32)),
        scratch_shapes=[pltpu.VMEM((G * W, N), F32)],
        compiler_params=_params("parallel", "arbitrary"))(dyn, conv, conv, conv, proj, dt_raw, pvec, nw, y_ssd, states)


def _unit_lower_inverse(A, ii, jj):
    eye = (ii == jj).astype(F32)
    same = (ii // GDN_INV_BLOCK) == (jj // GDN_INV_BLOCK)
    Ad = jnp.where(same, A, 0.0)
    Ao = A - Ad
    P2 = _dot3(Ad, Ad)
    yield
    P4, X = _dot3(P2, P2), _dot3(eye - Ad, eye + P2)
    yield
    P8, X = _dot3(P4, P4), _dot3(X, eye + P4)
    yield
    X = _dot3(X, eye + P8)
    yield
    Bm = _dot3(X, Ao)
    yield
    B2 = _dot3(Bm, Bm)
    yield
    Y = _dot3(eye - Bm, eye + B2)
    yield
    T = _dot3(Y, X)
    yield
    return T


def _gdn_specs(L, order):
    G = GDN_QK_PER_STEP
    Hd, W = G * GDN_HEAD, G * GDN_V_PER_QK * GDN_HEAD
    q_spec = pl.BlockSpec((L, Hd), lambda h, c: (order(c), (C_QKV - C_XBC) // Hd + h))
    k_spec = pl.BlockSpec((L, Hd), lambda h, c: (order(c), (C_QKV - C_XBC + 1024) // Hd + h))
    v_spec = pl.BlockSpec((L, W), lambda h, c: (order(c), (C_QKV - C_XBC + 2048) // W + h))
    z_spec = pl.BlockSpec((L, W), lambda h, c: (order(c), C_ZG // W + h))
    ba_spec = pl.BlockSpec((G, L, GDN_V_PER_QK), lambda h, c: (h, order(c), 0))
    p_spec = pl.BlockSpec((G, 2, GDN_V_PER_QK), lambda h, c: (h, 0, 0))
    nw_spec = pl.BlockSpec((1, GDN_HEAD), lambda h, c: (0, 0))
    s_spec = pl.BlockSpec((G, 1, GDN_V_PER_QK * GDN_HEAD, GDN_HEAD), lambda h, c: (h, order(c), 0, 0))
    t_spec = pl.BlockSpec((G * GDN_V_PER_QK, 1, L, L), lambda h, c: (h, order(c), 0, 0))
    return q_spec, k_spec, v_spec, z_spec, ba_spec, p_spec, nw_spec, s_spec, t_spec


def _gdn_gates(qa, ka, b_col, a_col, p, j, ii, jj):
    L = qa.shape[0]
    rq = lax.rsqrt(_rowsum(qa * qa) + EPS)
    rk = lax.rsqrt(_rowsum(ka * ka) + EPS)
    q = qa * rq * (GDN_HEAD ** -0.5)
    k = ka * rk
    beta = _sigmoid(b_col)
    sp_in = a_col + p[0:1, j:j + 1]
    neg_ea = -jnp.exp(p[1:2, j:j + 1])
    g = neg_ea * _softplus(sp_in)
    gcum, gcum_row = _cumsum_forms(g, ii, jj)
    Dm = jnp.exp(jnp.where(ii >= jj, gcum - gcum_row, NEG_INF))
    eg = jnp.exp(gcum)
    g_last = gcum[L - 1:L, :]
    wdec = jnp.exp(g_last - gcum)
    return dict(rq=rq, rk=rk, q=q, k=k, beta=beta, sp_in=sp_in, neg_ea=neg_ea, g=g, Dm=Dm, kbeta=k * beta, eg=eg,
                g_last=g_last, wdec=wdec, kdec=k * wdec)


def _gdn_fwd(conv, proj, b_raw, a_raw, pvec, nw, name):
    S = conv.shape[0]
    L, Hd, J, G = GDN_CHUNK, GDN_HEAD, GDN_V_PER_QK, GDN_QK_PER_STEP
    W = J * Hd
    nc = S // L

    def body(q_ref, k_ref, v_ref, z_ref, b_ref, a_ref, p_ref, nw_ref, o_ref, on_ref, s0_ref, t_ref, state):
        c = pl.program_id(1)

        @pl.when(c == 0)
        def _():
            state[...] = jnp.zeros_like(state)

        ii, jj = _iota(L, L, 0), _iota(L, L, 1)
        for hq in range(G):
            s0_ref[hq, 0] = state[hq * W:(hq + 1) * W, :]

        def head(hq, j):
            hd = hq * J + j
            hsl, sl = slice(hq * Hd, (hq + 1) * Hd), slice(hd * Hd, (hd + 1) * Hd)
            t = _gdn_gates(q_ref[:, hsl], k_ref[:, hsl], b_ref[hq][:, j:j + 1], a_ref[hq][:, j:j + 1], p_ref[hq], j,
                           ii, jj)
            KK = _dot(t["kbeta"], t["k"], _NT)
            QK = _dot(t["q"], t["k"], _NT)
            yield
            T = yield from _unit_lower_inverse(jnp.where(ii > jj, KK * t["Dm"], 0.0), ii, jj)
            t_ref[hd, 0] = T
            S0 = state[sl, :]
            U = _dot3(T, v_ref[:, sl] * t["beta"])
            Wm = _dot3(T, t["kbeta"] * t["eg"])
            o_inter = _dot(t["q"] * t["eg"], S0)
            yield
            Vn = U - _dot(Wm, S0)
            yield
            o = o_inter + _dot(QK * t["Dm"], Vn)
            s_new = _dot(t["kdec"], Vn, _TN)
            yield
            state[sl, :] = S0 * jnp.exp(t["g_last"]) + s_new
            o_ref[:, sl] = o
            r = lax.rsqrt(jnp.mean(o * o, axis=-1, keepdims=True) + EPS)
            on_ref[:, sl] = ((o * r * nw_ref[...]) * _silu(z_ref[:, sl])).astype(BF16)

        _lockstep(head(hq, j) for hq in range(G) for j in range(J))

    q_spec, k_spec, v_spec, z_spec, ba_spec, p_spec, nw_spec, s_spec, t_spec = _gdn_specs(L, lambda c: c)
    o_spec = pl.BlockSpec((L, G * W), lambda h, c: (c, h))
    return pl.pallas_call(
        body, name=name, grid=(GDN_QK_HEADS // G, nc),
        in_specs=[q_spec, k_spec, v_spec, z_spec, ba_spec, ba_spec, p_spec, nw_spec],
        out_specs=(o_spec, o_spec, s_spec, t_spec),
        out_shape=(jax.ShapeDtypeStruct((S, GDN_QK_HEADS * W), F32), jax.ShapeDtypeStruct((S, GDN_QK_HEADS * W), BF16),
                   jax.ShapeDtypeStruct((GDN_QK_HEADS, nc, W, Hd), F32),
                   jax.ShapeDtypeStruct((GDN_QK_HEADS * J, nc, L, L), F32)),
        scratch_shapes=[pltpu.VMEM((G * W, Hd), F32)],
        compiler_params=_params("parallel", "arbitrary"))(conv, conv, conv, proj, b_raw, a_raw, pvec, nw)


def _gdn_bwd(don, conv, proj, b_raw, a_raw, pvec, nw, o_pre, states, t_inv, name):
    S = conv.shape[0]
    L, Hd, J, G = GDN_CHUNK, GDN_HEAD, GDN_V_PER_QK, GDN_QK_PER_STEP
    W = J * Hd
    nc = S // L

    def body(don_ref, q_ref, k_ref, v_ref, z_ref, b_ref, a_ref, p_ref, nw_ref, o_ref, s0_ref, t_ref,
             dq_ref, dk_ref, dv_ref, dz_ref, db_ref, da_ref, dp_ref, dnw_ref, dstate):
        c = pl.program_id(1)

        @pl.when(c == 0)
        def _():
            dstate[...] = jnp.zeros_like(dstate)
            dp_ref[...] = jnp.zeros_like(dp_ref)
            dnw_ref[...] = jnp.zeros_like(dnw_ref)

        ii, jj = _iota(L, L, 0), _iota(L, L, 1)
        last = (_iota(L, 1, 0) == L - 1)
        res = {}

        def head(hq, j):
            hd = hq * J + j
            hsl, sl = slice(hq * Hd, (hq + 1) * Hd), slice(hd * Hd, (hd + 1) * Hd)
            qa, ka = q_ref[:, hsl], k_ref[:, hsl]
            t = _gdn_gates(qa, ka, b_ref[hq][:, j:j + 1], a_ref[hq][:, j:j + 1], p_ref[hq], j, ii, jj)
            q, k, beta, eg, Dm, kbeta, kdec = (t[nm] for nm in ("q", "k", "beta", "eg", "Dm", "kbeta", "kdec"))
            T = t_ref[hd, 0]
            v, z, o = v_ref[:, sl], z_ref[:, sl], o_ref[:, sl]
            S0, dS1 = s0_ref[hq, 0, j * Hd:(j + 1) * Hd, :], dstate[sl, :]
            sz = _silu(z)
            r = lax.rsqrt(jnp.mean(o * o, axis=-1, keepdims=True) + EPS)
            oh = o * r
            d_on = don_ref[:, sl]
            dz_ref[:, sl] = (d_on * (oh * nw_ref[...]) * _dsilu(z)).astype(BF16)
            dn = d_on * sz
            dnw_part = _colsum(dn * oh)
            doh = dn * nw_ref[...]
            dO = r * (doh - oh * jnp.mean(doh * oh, axis=-1, keepdims=True))
            Rw = kbeta * eg
            qe = q * eg
            U = _dot3(T, v * beta)
            Wm = _dot3(T, Rw)
            KK = _dot(kbeta, k, _NT)
            QK = _dot(q, k, _NT)
            o_inter = _dot(qe, S0)
            dq_s = _dot(dO, S0, _NT)
            dS_q = _dot(qe, dO, _TN)
            yield
            Am = jnp.where(ii > jj, KK * Dm, 0.0)
            Pm = QK * Dm
            Vn = U - _dot(Wm, S0)
            dVn_s = _dot(kdec, dS1)
            yield
            dVn = _dot(Pm, dO, _TN) + dVn_s
            dP = _dot(dO, Vn, _NT)
            dKd = _dot(Vn, dS1, _NT)
            yield
            dQK = dP * Dm
            dq = _dot(dQK, k) + dq_s * eg
            dk = _dot(dQK, q, _TN) + dKd * t["wdec"]
            dstate[sl, :] = jnp.exp(t["g_last"]) * dS1 + dS_q - _dot(Wm, dVn, _TN)
            dW = -_dot(dVn, S0, _NT)
            dRu = _dot3(T, dVn, _TN)
            yield
            dRw = _dot3(T, dW, _TN)
            dA_u = _dot(dRu, U, _NT)
            yield
            dA = jnp.where(ii > jj, -(dA_u + _dot(dRw, Wm, _NT)), 0.0)
            yield
            dKK = dA * Dm
            dkbeta = _dot(dKK, k) + dRw * eg
            dk = dk + _dot(dKK, kbeta, _TN)
            yield
            dk = dk + dkbeta * beta
            dbeta = _rowsum(dkbeta * k) + _rowsum(dRu * v)
            dv_ref[:, sl] = dRu * beta
            Q = dA * Am + dP * Pm
            rho = _rowsum(dKd * kdec)
            d_glast = _colsum(rho) + jnp.exp(t["g_last"]) * _total(dS1 * S0)
            col_of_colsum = _rowsum(jnp.where(ii == jj, _colsum(Q), 0.0))
            dgam = (_rowsum(dRw * Rw) + _rowsum(Q) - col_of_colsum + _rowsum(dO * o_inter) - rho
                    + jnp.where(last, d_glast, 0.0))
            dg = _rev_cumsum_col(dgam, ii, jj)
            da_raw = dg * t["neg_ea"] * _sigmoid(t["sp_in"])
            res[hq, j] = dict(dq=dq, dk=dk, db=dbeta * beta * (1.0 - beta), da=da_raw, d_bias=_colsum(da_raw),
                              d_alog=_colsum(dg * t["g"]), dnw=dnw_part, rq=t["rq"], rk=t["rk"], k=k, qh=qa * t["rq"])

        _lockstep(head(hq, j) for hq in range(G) for j in range(J))
        for hq in range(G):
            parts = [res[hq, j] for j in range(J)]
            hsl = slice(hq * Hd, (hq + 1) * Hd)
            p0 = parts[0]
            dqh = sum(pt["dq"] for pt in parts) * (GDN_HEAD ** -0.5)
            dkn = sum(pt["dk"] for pt in parts)
            dq_ref[:, hsl] = p0["rq"] * (dqh - p0["qh"] * _rowsum(dqh * p0["qh"]))
            dk_ref[:, hsl] = p0["rk"] * (dkn - p0["k"] * _rowsum(dkn * p0["k"]))
            db_ref[hq] = jnp.concatenate([pt["db"] for pt in parts], axis=1)
            da_ref[hq] = jnp.concatenate([pt["da"] for pt in parts], axis=1)
            dp_ref[hq] += jnp.concatenate([jnp.concatenate([pt["d_bias"] for pt in parts], axis=1),
                                           jnp.concatenate([pt["d_alog"] for pt in parts], axis=1)], axis=0)
            dnw_ref[hq] += sum(pt["dnw"] for pt in parts)

    rev = lambda c: nc - 1 - c
    q_spec, k_spec, v_spec, z_spec, ba_spec, p_spec, nw_spec, s_spec, t_spec = _gdn_specs(L, rev)
    o_spec = pl.BlockSpec((L, G * W), lambda h, c: (rev(c), h))
    h_spec = pl.BlockSpec((L, G * Hd), lambda h, c: (rev(c), h))
    dnw_spec = pl.BlockSpec((G, 1, Hd), lambda h, c: (h, 0, 0))
    return pl.pallas_call(
        body, name=name, grid=(GDN_QK_HEADS // G, nc),
        in_specs=[o_spec, q_spec, k_spec, v_spec, z_spec, ba_spec, ba_spec, p_spec, nw_spec, o_spec, s_spec, t_spec],
        out_specs=(h_spec, h_spec, o_spec, o_spec, ba_spec, ba_spec, p_spec, dnw_spec),
        out_shape=(jax.ShapeDtypeStruct((S, GDN_QK_HEADS * Hd), F32), jax.ShapeDtypeStruct((S, GDN_QK_HEADS * Hd), F32),
                   jax.ShapeDtypeStruct((S, GDN_QK_HEADS * W), F32), jax.ShapeDtypeStruct((S, GDN_QK_HEADS * W), BF16),
                   jax.ShapeDtypeStruct((GDN_QK_HEADS, S, J), F32), jax.ShapeDtypeStruct((GDN_QK_HEADS, S, J), F32),
                   jax.ShapeDtypeStruct((GDN_QK_HEADS, 2, J), F32), jax.ShapeDtypeStruct((GDN_QK_HEADS, 1, Hd), F32)),
        scratch_shapes=[pltpu.VMEM((G * W, Hd), F32)],
        compiler_params=_params("parallel", "arbitrary"))(don, conv, conv, conv, proj, b_raw, a_raw, pvec, nw, o_pre,
                                                          states, t_inv)


def _ada_fwd(c_all, w_loc, b_loc, name):
    n = w_loc.shape[1]

    def body(c_ref, w_ref, b_ref, o_ref):
        o_ref[...] = _dot3(_silu(c_ref[...]), w_ref[...]) + b_ref[...]

    return pl.pallas_call(body, name=name, out_shape=jax.ShapeDtypeStruct((N_DEV, n), F32),
                          compiler_params=pltpu.CompilerParams(vmem_limit_bytes=VMEM_LIMIT))(c_all, w_loc, b_loc)


def _ada_bwd(c_all_t, dmod_cols, name):
    Dm, n = c_all_t.shape[0], dmod_cols.shape[1]

    def body(c_ref, d_ref, o_ref):
        ca = _silu(c_ref[...])
        acc = ca[:, 0:1] * d_ref[0:1, :]
        for i in range(1, N_DEV):
            acc = acc + ca[:, i:i + 1] * d_ref[i:i + 1, :]
        o_ref[...] = acc

    return pl.pallas_call(body, name=name, out_shape=jax.ShapeDtypeStruct((Dm, n), F32),
                          compiler_params=pltpu.CompilerParams(vmem_limit_bytes=VMEM_LIMIT))(c_all_t, dmod_cols)


ADAM_BLOCK_BYTES = 12 * 1024 * 1024


def _adam(contrib, w, m, v, name):
    n, R, C = contrib.shape
    tr = R
    while tr % 16 == 0 and (n + 7) * tr * C * 4 > ADAM_BLOCK_BYTES:
        tr //= 2

    def body(c_ref, w_ref, m_ref, v_ref, g_ref, d_ref, nm_ref, nv_ref):
        g = c_ref[0].astype(F32)
        for i in range(1, n):
            g = g + c_ref[i].astype(F32)
        nm = ADAM_B1 * m_ref[...] + (1.0 - ADAM_B1) * g
        nv = ADAM_B2 * v_ref[...] + (1.0 - ADAM_B2) * (g * g)
        m_hat = nm / (1.0 - ADAM_B1 ** ADAM_STEP)
        v_hat = nv / (1.0 - ADAM_B2 ** ADAM_STEP)
        g_ref[...] = g
        d_ref[...] = -ADAM_LR * (m_hat / (jnp.sqrt(v_hat) + ADAM_EPS) + ADAM_WD * w_ref[...])
        nm_ref[...] = nm
        nv_ref[...] = nv

    spec = pl.BlockSpec((tr, C), lambda i: (i, 0))
    shp = jax.ShapeDtypeStruct((R, C), F32)
    return pl.pallas_call(
        body, name=name, grid=(R // tr,), in_specs=[pl.BlockSpec((n, tr, C), lambda i: (0, i, 0)), spec, spec, spec],
        out_specs=(spec,) * 4, out_shape=(shp,) * 4, compiler_params=_params("parallel"))(contrib, w, m, v)


def _exchange(arrays, modes, name):
    n = len(arrays)
    out_shape = tuple(jax.ShapeDtypeStruct((N_DEV,) + a.shape if md == "gather" else a.shape, a.dtype)
                      for a, md in zip(arrays, modes))

    def body(*refs):
        ins, outs = refs[:n], refs[n:2 * n]
        send_sems, recv_sems, loc_sems = refs[2 * n:]
        ix, iy, ic = lax.axis_index("x"), lax.axis_index("y"), lax.axis_index("c")
        me = 4 * ix + 2 * iy + ic
        peers = []
        for m in range(1, N_DEV):
            px = 1 - ix if m & 4 else ix
            py = 1 - iy if m & 2 else iy
            pc = 1 - ic if m & 1 else ic
            peers.append(((px, py, pc), 4 * px + 2 * py + pc))

        def src(k, slot):
            return ins[k] if modes[k] == "gather" else ins[k].at[slot]

        def remote(k, m, to_slot, land_slot):
            return pltpu.make_async_remote_copy(
                src_ref=src(k, to_slot), dst_ref=outs[k].at[land_slot], send_sem=send_sems.at[k, m],
                recv_sem=recv_sems.at[k, m], device_id=peers[m][0], device_id_type=pl.DeviceIdType.MESH)

        local = [pltpu.make_async_copy(src(k, me), outs[k].at[me], loc_sems.at[k]) for k in range(n)]
        for cp in local:
            cp.start()
        sends = [remote(k, m, peers[m][1], me) for m in range(N_DEV - 1) for k in range(n)]
        for cp in sends:
            cp.start()
        for m in range(N_DEV - 1):
            for k in range(n):
                remote(k, m, peers[m][1], peers[m][1]).wait_recv()
        for cp in sends:
            cp.wait_send()
        for cp in local:
            cp.wait()

    any_spec = pl.BlockSpec(memory_space=pl.ANY)
    return pl.pallas_call(
        body, name=name, in_specs=[any_spec] * n, out_specs=(any_spec,) * n, out_shape=out_shape,
        scratch_shapes=[pltpu.SemaphoreType.DMA((n, N_DEV - 1)), pltpu.SemaphoreType.DMA((n, N_DEV - 1)),
                        pltpu.SemaphoreType.DMA((n,))])(*arrays)


W_IN_SPLITS = (0, 2048, 6144, 6176, 10272, 12320, 12336, 12352, 13376, 14400)
N_REPLICATED = 16640
REPLICATED = ("b_ada", "norm_mix_pre", "norm_mix_post", "ssm_conv_b", "ssm_dt_bias", "ssm_A_log", "ssm_D",
              "ssm_norm_w", "gdn_dt_bias", "gdn_A_log", "gdn_norm_w", "norm_mlp_pre", "norm_mlp_post")
WEIGHTS = ("w_ada", "b_ada", "norm_mix_pre", "norm_mix_post", "w_in", "ssm_conv_w", "ssm_conv_b", "ssm_dt_bias",
           "ssm_A_log", "ssm_D", "ssm_norm_w", "gdn_conv_w", "gdn_dt_bias", "gdn_A_log", "gdn_norm_w", "w_ssm_up",
           "w_gdn_up", "w_out", "norm_mlp_pre", "norm_mlp_post", "w_mlp_up", "w_mlp_down")


def _by_cols(t):
    return t.transpose(1, 0, 2).reshape(t.shape[1], N_DEV * t.shape[2])


def _to_col_shards(t):
    R, C8 = t.shape
    return t.reshape(R, N_DEV, C8 // N_DEV).transpose(1, 0, 2)


def _heads_first(t, groups):
    S = t.shape[0]
    return t.reshape(S, groups, t.shape[1] // groups).transpose(1, 0, 2)


def _heads_last(t):
    return t.transpose(1, 0, 2).reshape(t.shape[1], t.shape[0] * t.shape[2])


def kernel(x, c, w_ada, b_ada, norm_mix_pre, norm_mix_post, w_in, ssm_conv_w, ssm_conv_b, ssm_dt_bias, ssm_A_log, ssm_D, ssm_norm_w, gdn_conv_w, gdn_dt_bias, gdn_A_log, gdn_norm_w, w_ssm_up, w_gdn_up, w_out, norm_mlp_pre, norm_mlp_post, w_mlp_up, w_mlp_down, loss_target, m_w_ada, m_b_ada, m_norm_mix_pre, m_norm_mix_post, m_w_in, m_ssm_conv_w, m_ssm_conv_b, m_ssm_dt_bias, m_ssm_A_log, m_ssm_D, m_ssm_norm_w, m_gdn_conv_w, m_gdn_dt_bias, m_gdn_A_log, m_gdn_norm_w, m_w_ssm_up, m_w_gdn_up, m_w_out, m_norm_mlp_pre, m_norm_mlp_post, m_w_mlp_up, m_w_mlp_down, v_w_ada, v_b_ada, v_norm_mix_pre, v_norm_mix_post, v_w_in, v_ssm_conv_w, v_ssm_conv_b, v_ssm_dt_bias, v_ssm_A_log, v_ssm_D, v_ssm_norm_w, v_gdn_conv_w, v_gdn_dt_bias, v_gdn_A_log, v_gdn_norm_w, v_w_ssm_up, v_w_gdn_up, v_w_out, v_norm_mlp_pre, v_norm_mlp_post, v_w_mlp_up, v_w_mlp_down):
    S, Dm = x.shape[1], D_MODEL
    me = 4 * lax.axis_index("x") + 2 * lax.axis_index("y") + lax.axis_index("c")
    x2, tgt = x[0], loss_target[0]
    n_ada = w_ada.shape[2]
    given = dict(
        w_ada=(w_ada, m_w_ada, v_w_ada), b_ada=(b_ada, m_b_ada, v_b_ada),
        norm_mix_pre=(norm_mix_pre, m_norm_mix_pre, v_norm_mix_pre),
        norm_mix_post=(norm_mix_post, m_norm_mix_post, v_norm_mix_post), w_in=(w_in, m_w_in, v_w_in),
        ssm_conv_w=(ssm_conv_w, m_ssm_conv_w, v_ssm_conv_w), ssm_conv_b=(ssm_conv_b, m_ssm_conv_b, v_ssm_conv_b),
        ssm_dt_bias=(ssm_dt_bias, m_ssm_dt_bias, v_ssm_dt_bias), ssm_A_log=(ssm_A_log, m_ssm_A_log, v_ssm_A_log),
        ssm_D=(ssm_D, m_ssm_D, v_ssm_D), ssm_norm_w=(ssm_norm_w, m_ssm_norm_w, v_ssm_norm_w),
        gdn_conv_w=(gdn_conv_w, m_gdn_conv_w, v_gdn_conv_w), gdn_dt_bias=(gdn_dt_bias, m_gdn_dt_bias, v_gdn_dt_bias),
        gdn_A_log=(gdn_A_log, m_gdn_A_log, v_gdn_A_log), gdn_norm_w=(gdn_norm_w, m_gdn_norm_w, v_gdn_norm_w),
        w_ssm_up=(w_ssm_up, m_w_ssm_up, v_w_ssm_up), w_gdn_up=(w_gdn_up, m_w_gdn_up, v_w_gdn_up),
        w_out=(w_out, m_w_out, v_w_out), norm_mlp_pre=(norm_mlp_pre, m_norm_mlp_pre, v_norm_mlp_pre),
        norm_mlp_post=(norm_mlp_post, m_norm_mlp_post, v_norm_mlp_post), w_mlp_up=(w_mlp_up, m_w_mlp_up, v_w_mlp_up),
        w_mlp_down=(w_mlp_down, m_w_mlp_down, v_w_mlp_down))

    (c_all, scw, gcw, g_in, g_su, g_gu, g_out, g_mu, g_md) = _exchange(
        [c, ssm_conv_w[0], gdn_conv_w[0], w_in[0].astype(BF16), w_ssm_up[0].astype(BF16), w_gdn_up[0].astype(BF16),
         w_out[0].astype(BF16), w_mlp_up[0].astype(BF16), w_mlp_down[0].astype(BF16)], ["gather"] * 9, "gather_weights")
    c_all = c_all.reshape(N_DEV, Dm)
    wf = _by_cols(g_in)
    sp = W_IN_SPLITS
    w_main = jnp.concatenate([wf[:, sp[0]:sp[2]], wf[:, sp[3]:sp[5]], wf[:, sp[7]:sp[9]]], axis=1)
    w_small = jnp.concatenate([wf[:, sp[2]:sp[3]], wf[:, sp[5]:sp[7]], jnp.zeros((Dm, N_SMALL - 64), BF16)], axis=1)
    w_su, w_gu = g_su.reshape(2 * Dm, Dm), g_gu.reshape(2 * Dm, Dm)
    w_o, w_mu, w_md = g_out.reshape(Dm, Dm), _by_cols(g_mu), g_md.reshape(4 * Dm, Dm)
    conv_w = jnp.concatenate([_by_cols(scw), _by_cols(gcw)], axis=1)
    conv_b = jnp.concatenate([ssm_conv_b, jnp.zeros_like(ssm_conv_b)], axis=1)

    b_loc = lax.dynamic_slice(b_ada, (0, me * n_ada), (1, n_ada))
    mod_part = _ada_fwd(c_all, w_ada[0], b_loc, "ada_fwd")
    (mod_rows,) = _exchange([mod_part.reshape(N_DEV, 1, n_ada)], ["a2a"], "exchange_mod")
    mod = mod_rows.reshape(1, 6 * Dm)
    sh1, sc1, g1, sh2, sc2, g2 = [mod[:, i * Dm:(i + 1) * Dm] for i in range(6)]

    h = _pre_fwd(x2, norm_mix_pre, sc1, sh1, "pre_mix")
    proj = _mm(h, w_main, S, N_MAIN, Dm, mode="nn", out_dtype=F32, name="proj_main")
    small = _mm(h, w_small, S, N_SMALL, Dm, mode="nn", out_dtype=F32, name="proj_small")
    conv = _conv_fwd(proj, conv_w, conv_b, "conv_fwd")
    dt_g, b_g, a_g = _heads_first(small[:, 0:32], 8), _heads_first(small[:, 32:48], 8), _heads_first(small[:, 48:64], 8)
    pv_ssm = jnp.stack([ssm_dt_bias.reshape(8, 4), ssm_A_log.reshape(8, 4), ssm_D.reshape(8, 4)], axis=1)
    nw_ssm = ssm_norm_w.reshape(8, 1, SSM_GROUP_WIDTH)
    pv_gdn = jnp.stack([gdn_dt_bias.reshape(8, 2), gdn_A_log.reshape(8, 2)], axis=1)
    y_ssd, ysn, st_ssm = _ssd_fwd(conv, proj, dt_g, pv_ssm, nw_ssm, "ssd_fwd")
    o_pre, ogn, st_gdn, t_inv = _gdn_fwd(conv, proj, b_g, a_g, pv_gdn, gdn_norm_w, "gdn_fwd")
    ys = _mm(ysn, w_su, S, Dm, 2 * Dm, mode="nn", out_dtype=F32, name="ssm_up")
    yg = _mm(ogn, w_gu, S, Dm, 2 * Dm, mode="nn", out_dtype=F32, name="gdn_up")
    merged = _merge_fwd(ys, yg, proj, "merge_fwd")
    mo = _mm(merged, w_o, S, Dm, Dm, mode="nn", out_dtype=F32, name="mix_out")
    x1 = _post_fwd(x2, mo, norm_mix_post, g1, "post_mix")
    h2 = _pre_fwd(x1, norm_mlp_pre, sc2, sh2, "pre_mlp")
    u, act = _mm(h2, w_mu, S, 4 * Dm, Dm, mode="nn", out_dtype=F32, epi="relu2", name="mlp_up")
    y_mlp = _mm(act, w_md, S, Dm, 4 * Dm, mode="nn", out_dtype=F32, name="mlp_down")
    dx2, loss_loc = _final_fwd(x1, y_mlp, norm_mlp_post, g2, tgt, "post_mlp_loss")

    dy, dg2, dw_post2 = _post_bwd(dx2, y_mlp, norm_mlp_post, g2, "post_mlp_bwd")
    du = _mm(dy, w_md, S, 4 * Dm, Dm, mode="nt", out_dtype=BF16, epi="drelu2", extra=u, name="mlp_down_dx")
    gw_md = _mm(act, dy, 4 * Dm, Dm, S, mode="tn", out_dtype=BF16, name="mlp_down_dw")
    dh2 = _mm(du, w_mu, S, Dm, 4 * Dm, mode="nt", out_dtype=F32, name="mlp_up_dx")
    gw_mu = _mm(h2, du, Dm, 4 * Dm, S, mode="tn", out_dtype=BF16, name="mlp_up_dw")
    dx1, dsh2, dsc2, dw_pre2 = _pre_bwd(dh2, x1, norm_mlp_pre, sc2, dx2, "pre_mlp_bwd")
    dmo, dg1, dw_post1 = _post_bwd(dx1, mo, norm_mix_post, g1, "post_mix_bwd")
    dmerged = _mm(dmo, w_o, S, Dm, Dm, mode="nt", out_dtype=F32, name="mix_out_dx")
    gw_o = _mm(merged, dmo, Dm, Dm, S, mode="tn", out_dtype=BF16, name="mix_out_dw")
    dys, dyg, dgate = _merge_bwd(dmerged, ys, yg, proj, "merge_bwd")
    dysn = _mm(dys, w_su, S, 2 * Dm, Dm, mode="nt", out_dtype=F32, name="ssm_up_dx")
    gw_su = _mm(ysn, dys, 2 * Dm, Dm, S, mode="tn", out_dtype=BF16, name="ssm_up_dw")
    dogn = _mm(dyg, w_gu, S, 2 * Dm, Dm, mode="nt", out_dtype=F32, name="gdn_up_dx")
    gw_gu = _mm(ogn, dyg, 2 * Dm, Dm, S, mode="tn", out_dtype=BF16, name="gdn_up_dw")
    dxs, dBm, dCm, dz_s, ddt_g, dpv_ssm, dnw_ssm = _ssd_bwd(dysn, conv, proj, dt_g, pv_ssm, nw_ssm, y_ssd, st_ssm,
                                                            "ssd_bwd")
    dq, dk, dv, dz_g, db_g, da_g, dpv_gdn, dnw_gdn = _gdn_bwd(dogn, conv, proj, b_g, a_g, pv_gdn, gdn_norm_w, o_pre,
                                                              st_gdn, t_inv, "gdn_bwd")
    conv_pieces = []
    for nm, d_act, col0 in (("xs", dxs, 0), ("B", dBm, 2048), ("C", dCm, 3072), ("q", dq, 4096), ("k", dk, 5120),
                            ("v", dv, 6144)):
        conv_pieces.append(_conv_bwd(d_act, proj, conv_w, conv_b, col0, "conv_bwd_" + nm) + (col0,))
    d_small = jnp.concatenate([_heads_last(ddt_g), _heads_last(db_g), _heads_last(da_g),
                               jnp.zeros((S, N_SMALL - 64), F32)], axis=1).astype(BF16)
    pieces = [(dz_s, C_ZS)] + [(p[0], C_XBC + p[3]) for p in conv_pieces] + [(dz_g, C_ZG), (dgate, C_GS)]
    dh = _mm(d_small, w_small, S, Dm, N_SMALL, mode="nt", out_dtype=F32, name="proj_small_dx")
    gw_small = _mm(h, d_small, Dm, N_SMALL, S, mode="tn", out_dtype=BF16, name="proj_small_dw")
    gw_cols = []
    for d_piece, off in pieces:
        width = d_piece.shape[1]
        dh = _mm(d_piece, w_main, S, Dm, width, mode="nt", out_dtype=F32, b_off=(0, off), add=dh,
                 name="proj_dx_%d" % off)
        gw_cols.append(_mm(h, d_piece, Dm, width, S, mode="tn", out_dtype=BF16, name="proj_dw_%d" % off))
    dx, dsh1, dsc1, dw_pre1 = _pre_bwd(dh, x2, norm_mix_pre, sc1, dx1, "pre_mix_bwd")

    main_cols = jnp.concatenate(gw_cols, axis=1)
    gw_in = jnp.concatenate([main_cols[:, 0:C_QKV], gw_small[:, 0:32], main_cols[:, C_QKV:C_GS], gw_small[:, 32:64],
                             main_cols[:, C_GS:N_MAIN]], axis=1)
    (r_in, r_su, r_gu, r_o, r_mu, r_md) = _exchange(
        [_to_col_shards(gw_in), gw_su.reshape(N_DEV, -1, Dm), gw_gu.reshape(N_DEV, -1, Dm),
         gw_o.reshape(N_DEV, -1, Dm), _to_col_shards(gw_mu), gw_md.reshape(N_DEV, -1, Dm)], ["a2a"] * 6,
        "exchange_grads")

    dconv_w = jnp.concatenate([p[1] for p in conv_pieces], axis=1)
    dconv_b = jnp.concatenate([p[2] for p in conv_pieces[:3]], axis=1)
    dmod = jnp.concatenate([dsh1, dsc1, dg1, dsh2, dsc2, dg2], axis=1)
    small_vec = jnp.concatenate(
        [dmod, dw_pre1, dw_post1, dconv_b, dpv_ssm[:, 0].reshape(1, 32), dpv_ssm[:, 1].reshape(1, 32),
         dpv_ssm[:, 2].reshape(1, 32), dnw_ssm.reshape(1, 2048), dpv_gdn[:, 0].reshape(1, 16),
         dpv_gdn[:, 1].reshape(1, 16), jnp.sum(dnw_gdn, axis=0), dw_pre2, dw_post2, dconv_w.reshape(1, -1)], axis=1)
    n_vec = small_vec.shape[1]
    small_vec = jnp.pad(small_vec, ((0, 0), (0, (-n_vec) % 1024))).reshape(-1, 1024)
    (small_all,) = _exchange([small_vec], ["gather"], "gather_small_grads")
    small_all = small_all.reshape(N_DEV, -1)
    dmod_cols = lax.dynamic_slice(small_all, (0, me * n_ada), (N_DEV, n_ada))
    gw_ada = _ada_bwd(c_all.T, dmod_cols, "ada_bwd")
    conv_all = small_all[:, N_REPLICATED:n_vec].reshape(N_DEV, CONV_K, 2 * N_DEV * 512)
    conv_contrib = jnp.concatenate(
        [lax.dynamic_slice(conv_all, (0, 0, me * 512), (N_DEV, CONV_K, 512)),
         lax.dynamic_slice(conv_all, (0, 0, N_DEV * 512 + me * 512), (N_DEV, CONV_K, 512))], axis=1)
    rep_contrib = small_all[:, :N_REPLICATED].reshape(N_DEV, N_REPLICATED // 128, 128)

    results = {}

    def adam_big(nm, contrib):
        w3 = given[nm]
        res = _adam(contrib, w3[0][0], w3[1][0], w3[2][0], "adam_" + nm)
        results[nm] = tuple(r.reshape(w3[0].shape) for r in res)

    adam_big("w_ada", gw_ada[None])
    adam_big("w_in", r_in)
    adam_big("w_ssm_up", r_su)
    adam_big("w_gdn_up", r_gu)
    adam_big("w_out", r_o)
    adam_big("w_mlp_up", r_mu)
    adam_big("w_mlp_down", r_md)
    packed = [jnp.concatenate([given[nm][i] for nm in REPLICATED], axis=1).reshape(N_REPLICATED // 128, 128)
              for i in range(3)]
    rep_res = _adam(rep_contrib, packed[0], packed[1], packed[2], "adam_replicated")
    pos = 0
    for nm in REPLICATED:
        size = given[nm][0].shape[1]
        results[nm] = tuple(r.reshape(1, N_REPLICATED)[:, pos:pos + size] for r in rep_res)
        pos += size
    conv_wmv = [jnp.concatenate([given["ssm_conv_w"][i][0], given["gdn_conv_w"][i][0]], axis=0) for i in range(3)]
    conv_res = _adam(conv_contrib, conv_wmv[0], conv_wmv[1], conv_wmv[2], "adam_conv_w")
    results["ssm_conv_w"] = tuple(r[None, :CONV_K] for r in conv_res)
    results["gdn_conv_w"] = tuple(r[None, CONV_K:] for r in conv_res)

    loss = lax.psum(loss_loc[0, 0], ("x", "y", "c"))
    return (loss, dx[None]) + tuple(results[nm][i] for i in range(4) for nm in WEIGHTS)
```

```python
import jax
import jax.numpy as jnp
from jax import lax
from jax.experimental import pallas as pl
from jax.experimental.pallas import tpu as pltpu

F32 = jnp.float32
BF16 = jnp.bfloat16
N_DEV = 8
D_MODEL = 1024
EPS = 1e-6
CONV_K = 4
SSM_CHUNK = 128
SSM_HEAD_DIM = 64
SSM_D_STATE = 128
SSM_GROUPS = 8
SSM_HEADS_PER_GROUP = 4
SSM_GROUP_WIDTH = SSM_HEADS_PER_GROUP * SSM_HEAD_DIM
SSM_GROUPS_PER_STEP = 2
GDN_CHUNK = 64
GDN_HEAD = 128
GDN_QK_HEADS = 8
GDN_V_PER_QK = 2
GDN_QK_PER_STEP = 4
GDN_INV_BLOCK = 16
C_ZS, C_XBC, C_QKV, C_ZG, C_GS, C_GG, N_MAIN = 0, 2048, 6144, 10240, 12288, 13312, 14336
N_SMALL = 128
ADAM_LR, ADAM_B1, ADAM_B2, ADAM_EPS, ADAM_WD, ADAM_STEP = 0.001, 0.9, 0.999, 1e-08, 0.01, 10
VMEM_LIMIT = 56 * 1024 * 1024
NEG_INF = float("-inf")

_NT = (((1,), (1,)), ((), ()))
_NN = (((1,), (0,)), ((), ()))
_TN = (((0,), (0,)), ((), ()))


def _params(*sem):
    return pltpu.CompilerParams(dimension_semantics=sem, vmem_limit_bytes=VMEM_LIMIT)


def _dot(a, b, dims=_NN):
    return lax.dot_general(a.astype(BF16), b.astype(BF16), dims, preferred_element_type=F32)


def _split(a):
    hi = a.astype(BF16)
    return hi, (a - hi.astype(F32)).astype(BF16)


def _dot3(a, b, dims=_NN):
    ah, al = _split(a)
    bh, bl = _split(b)
    d = lambda u, v: lax.dot_general(u, v, dims, preferred_element_type=F32)
    return d(ah, bh) + (d(ah, bl) + d(al, bh))


def _sigmoid(x):
    return 1.0 / (1.0 + jnp.exp(-x))


def _silu(x):
    return x * _sigmoid(x)


def _dsilu(x):
    s = _sigmoid(x)
    return s * (1.0 + x * (1.0 - s))


def _softplus(x):
    return jnp.maximum(x, 0.0) + jnp.log1p(jnp.exp(-jnp.abs(x)))


def _iota(n, m, d):
    return lax.broadcasted_iota(jnp.int32, (n, m), d)


def _rowsum(x):
    return jnp.sum(x, axis=1, keepdims=True)


def _colsum(x):
    return jnp.sum(x, axis=0, keepdims=True)


def _total(x):
    return _colsum(_rowsum(x))


def _cumsum_forms(col, ii, jj):
    row = _colsum(jnp.where(ii == jj, col, 0.0))
    cum_col = _rowsum(jnp.where(ii >= jj, row, 0.0))
    cum_row = _colsum(jnp.where(ii <= jj, col, 0.0))
    return cum_col, cum_row


def _rev_cumsum_col(col, ii, jj):
    row = _colsum(jnp.where(ii >= jj, col, 0.0))
    return _rowsum(jnp.where(ii == jj, row, 0.0))


def _blk(dim, pref):
    return pref if dim % pref == 0 else dim


def _lockstep(gens):
    gens = list(gens)
    while gens:
        alive = []
        for g in gens:
            try:
                next(g)
                alive.append(g)
            except StopIteration:
                pass
        gens = alive


def _mm(a, b, M, N, K, *, mode, out_dtype, name, a_off=(0, 0), b_off=(0, 0), add=None, epi=None, extra=None,
        tm=1024, tn=1024, tk=1024):
    tm, tn, tk = _blk(M, tm), _blk(N, tn), _blk(K, tk)
    nk = K // tk
    if mode == "tn":
        a_spec = pl.BlockSpec((tk, tm), lambda i, j, k: (k + a_off[0] // tk, i + a_off[1] // tm))
        assert a_off[0] % tk == 0 and a_off[1] % tm == 0
    else:
        a_spec = pl.BlockSpec((tm, tk), lambda i, j, k: (i + a_off[0] // tm, k + a_off[1] // tk))
        assert a_off[0] % tm == 0 and a_off[1] % tk == 0
    if mode == "nt":
        b_spec = pl.BlockSpec((tn, tk), lambda i, j, k: (j + b_off[0] // tn, k + b_off[1] // tk))
        assert b_off[0] % tn == 0 and b_off[1] % tk == 0
    else:
        b_spec = pl.BlockSpec((tk, tn), lambda i, j, k: (k + b_off[0] // tk, j + b_off[1] // tn))
        assert b_off[0] % tk == 0 and b_off[1] % tn == 0
    dims = {"nn": _NN, "nt": _NT, "tn": _TN}[mode]
    o_spec = pl.BlockSpec((tm, tn), lambda i, j, k: (i, j))
    ins, in_specs = [a, b], [a_spec, b_spec]
    if add is not None:
        ins.append(add)
        in_specs.append(o_spec)
    if extra is not None:
        ins.append(extra)
        in_specs.append(o_spec)
    n_in = len(ins)
    if epi == "relu2":
        out_shape = (jax.ShapeDtypeStruct((M, N), F32), jax.ShapeDtypeStruct((M, N), BF16))
        out_specs = (o_spec, o_spec)
    else:
        out_shape = jax.ShapeDtypeStruct((M, N), out_dtype)
        out_specs = o_spec

    def body(*refs):
        a_ref, b_ref = refs[0], refs[1]
        acc = refs[-1]
        outs = refs[n_in:-1]
        k = pl.program_id(2)

        @pl.when(k == 0)
        def _():
            acc[...] = jnp.zeros_like(acc)

        acc[...] += _dot(a_ref[...], b_ref[...], dims)

        @pl.when(k == nk - 1)
        def _():
            r = acc[...]
            pos = 2
            if add is not None:
                r = r + refs[pos][...]
                pos += 1
            if epi == "relu2":
                outs[0][...] = r
                p = jnp.maximum(r, 0.0)
                outs[1][...] = (p * p).astype(BF16)
            elif epi == "drelu2":
                outs[0][...] = (r * (2.0 * jnp.maximum(refs[pos][...], 0.0))).astype(out_dtype)
            else:
                outs[0][...] = r.astype(out_dtype)

    return pl.pallas_call(
        body, name=name, grid=(M // tm, N // tn, nk), in_specs=in_specs, out_specs=out_specs, out_shape=out_shape,
        scratch_shapes=[pltpu.VMEM((tm, tn), F32)],
        compiler_params=_params("parallel", "parallel", "arbitrary"))(*ins)


def _row_spec(tb, d):
    return pl.BlockSpec((tb, d), lambda i: (i, 0))


def _vec_spec(d):
    return pl.BlockSpec((1, d), lambda i: (0, 0))


def _pre_fwd(x, w, sc, sh, name):
    S, Dm = x.shape
    tb = _blk(S, 512)

    def body(x_ref, w_ref, sc_ref, sh_ref, h_ref):
        xv = x_ref[...]
        r = lax.rsqrt(jnp.mean(xv * xv, axis=-1, keepdims=True) + EPS)
        h_ref[...] = ((xv * r * w_ref[...]) * (1.0 + sc_ref[...]) + sh_ref[...]).astype(BF16)

    return pl.pallas_call(
        body, name=name, grid=(S // tb,), in_specs=[_row_spec(tb, Dm)] + [_vec_spec(Dm)] * 3,
        out_specs=_row_spec(tb, Dm), out_shape=jax.ShapeDtypeStruct((S, Dm), BF16),
        compiler_params=_params("parallel"))(x, w, sc, sh)


def _post_fwd(x, y, w, g, name):
    S, Dm = x.shape
    tb = _blk(S, 512)

    def body(x_ref, y_ref, w_ref, g_ref, o_ref):
        yv = y_ref[...]
        r = lax.rsqrt(jnp.mean(yv * yv, axis=-1, keepdims=True) + EPS)
        o_ref[...] = x_ref[...] + g_ref[...] * (yv * r * w_ref[...])

    return pl.pallas_call(
        body, name=name, grid=(S // tb,), in_specs=[_row_spec(tb, Dm)] * 2 + [_vec_spec(Dm)] * 2,
        out_specs=_row_spec(tb, Dm), out_shape=jax.ShapeDtypeStruct((S, Dm), F32),
        compiler_params=_params("parallel"))(x, y, w, g)


def _final_fwd(x, y, w, g, target, name):
    S, Dm = x.shape
    tb = _blk(S, 512)
    nb = S // tb

    def body(x_ref, y_ref, w_ref, g_ref, t_ref, dx_ref, loss_ref, acc):
        i = pl.program_id(0)

        @pl.when(i == 0)
        def _():
            acc[...] = jnp.zeros_like(acc)

        yv = y_ref[...]
        r = lax.rsqrt(jnp.mean(yv * yv, axis=-1, keepdims=True) + EPS)
        e = (x_ref[...] + g_ref[...] * (yv * r * w_ref[...])) - t_ref[...]
        dx_ref[...] = e * (1.0 / Dm)
        acc[...] += _colsum(e * e)

        @pl.when(i == nb - 1)
        def _():
            loss_ref[...] = (0.5 / Dm) * _rowsum(acc[...])

    return pl.pallas_call(
        body, name=name, grid=(nb,), in_specs=[_row_spec(tb, Dm)] * 2 + [_vec_spec(Dm)] * 2 + [_row_spec(tb, Dm)],
        out_specs=(_row_spec(tb, Dm), pl.BlockSpec((1, 1), lambda i: (0, 0))),
        out_shape=(jax.ShapeDtypeStruct((S, Dm), F32), jax.ShapeDtypeStruct((1, 1), F32)),
        scratch_shapes=[pltpu.VMEM((1, Dm), F32)], compiler_params=_params("arbitrary"))(x, y, w, g, target)


def _post_bwd(dxo, y, w, g, name):
    S, Dm = y.shape
    tb = _blk(S, 512)

    def body(d_ref, y_ref, w_ref, g_ref, dy_ref, dg_ref, dw_ref):
        i = pl.program_id(0)

        @pl.when(i == 0)
        def _():
            dg_ref[...] = jnp.zeros_like(dg_ref)
            dw_ref[...] = jnp.zeros_like(dw_ref)

        yv, dv = y_ref[...], d_ref[...]
        r = lax.rsqrt(jnp.mean(yv * yv, axis=-1, keepdims=True) + EPS)
        yh = yv * r
        dg_ref[...] += _colsum(dv * (yh * w_ref[...]))
        dn = dv * g_ref[...]
        dw_ref[...] += _colsum(dn * yh)
        dyh = dn * w_ref[...]
        dy_ref[...] = (r * (dyh - yh * jnp.mean(dyh * yh, axis=-1, keepdims=True))).astype(BF16)

    return pl.pallas_call(
        body, name=name, grid=(S // tb,), in_specs=[_row_spec(tb, Dm)] * 2 + [_vec_spec(Dm)] * 2,
        out_specs=(_row_spec(tb, Dm), _vec_spec(Dm), _vec_spec(Dm)),
        out_shape=(jax.ShapeDtypeStruct((S, Dm), BF16), jax.ShapeDtypeStruct((1, Dm), F32),
                   jax.ShapeDtypeStruct((1, Dm), F32)),
        compiler_params=_params("arbitrary"))(dxo, y, w, g)


def _pre_bwd(dh, x, w, sc, dres, name):
    S, Dm = x.shape
    tb = _blk(S, 512)

    def body(dh_ref, x_ref, w_ref, sc_ref, dr_ref, dx_ref, dsh_ref, dsc_ref, dw_ref):
        i = pl.program_id(0)

        @pl.when(i == 0)
        def _():
            dsh_ref[...] = jnp.zeros_like(dsh_ref)
            dsc_ref[...] = jnp.zeros_like(dsc_ref)
            dw_ref[...] = jnp.zeros_like(dw_ref)

        xv, dv = x_ref[...], dh_ref[...]
        r = lax.rsqrt(jnp.mean(xv * xv, axis=-1, keepdims=True) + EPS)
        xh = xv * r
        one_sc = 1.0 + sc_ref[...]
        dsh_ref[...] += _colsum(dv)
        dsc_ref[...] += _colsum(dv * (xh * w_ref[...]))
        dw_ref[...] += _colsum(dv * one_sc * xh)
        dxh = dv * one_sc * w_ref[...]
        dx_ref[...] = dr_ref[...] + r * (dxh - xh * jnp.mean(dxh * xh, axis=-1, keepdims=True))

    vec = jax.ShapeDtypeStruct((1, Dm), F32)
    return pl.pallas_call(
        body, name=name, grid=(S // tb,),
        in_specs=[_row_spec(tb, Dm)] * 2 + [_vec_spec(Dm)] * 2 + [_row_spec(tb, Dm)],
        out_specs=(_row_spec(tb, Dm), _vec_spec(Dm), _vec_spec(Dm), _vec_spec(Dm)),
        out_shape=(jax.ShapeDtypeStruct((S, Dm), F32), vec, vec, vec),
        compiler_params=_params("arbitrary"))(dh, x, w, sc, dres)


def _merge_fwd(ys, yg, proj, name):
    S, Dm = ys.shape
    tb = _blk(S, 512)

    def body(ys_ref, yg_ref, gs_ref, gg_ref, o_ref):
        o_ref[...] = (_sigmoid(gs_ref[...]) * ys_ref[...] + _sigmoid(gg_ref[...]) * yg_ref[...]).astype(BF16)

    return pl.pallas_call(
        body, name=name, grid=(S // tb,),
        in_specs=[_row_spec(tb, Dm)] * 2 + [pl.BlockSpec((tb, Dm), lambda i: (i, C_GS // Dm)),
                                            pl.BlockSpec((tb, Dm), lambda i: (i, C_GG // Dm))],
        out_specs=_row_spec(tb, Dm), out_shape=jax.ShapeDtypeStruct((S, Dm), BF16),
        compiler_params=_params("parallel"))(ys, yg, proj, proj)


def _merge_bwd(dm, ys, yg, proj, name):
    S, Dm = ys.shape
    tb = _blk(S, 512)

    def body(dm_ref, ys_ref, yg_ref, gs_ref, gg_ref, dys_ref, dyg_ref, dgate_ref):
        d = dm_ref[...]
        ss, sg = _sigmoid(gs_ref[...]), _sigmoid(gg_ref[...])
        dys_ref[...] = (d * ss).astype(BF16)
        dyg_ref[...] = (d * sg).astype(BF16)
        dgate_ref[:, :Dm] = (d * ys_ref[...] * ss * (1.0 - ss)).astype(BF16)
        dgate_ref[:, Dm:] = (d * yg_ref[...] * sg * (1.0 - sg)).astype(BF16)

    return pl.pallas_call(
        body, name=name, grid=(S // tb,),
        in_specs=[_row_spec(tb, Dm)] * 3 + [pl.BlockSpec((tb, Dm), lambda i: (i, C_GS // Dm)),
                                            pl.BlockSpec((tb, Dm), lambda i: (i, C_GG // Dm))],
        out_specs=(_row_spec(tb, Dm), _row_spec(tb, Dm), _row_spec(tb, 2 * Dm)),
        out_shape=(jax.ShapeDtypeStruct((S, Dm), BF16), jax.ShapeDtypeStruct((S, Dm), BF16),
                   jax.ShapeDtypeStruct((S, 2 * Dm), BF16)),
        compiler_params=_params("parallel"))(dm, ys, yg, proj, proj)


CONV_COLS = 128


def _shift_down(x, k, rows):
    return jnp.where(rows >= k, pltpu.roll(x, k, 0), 0.0)


def _shift_up(x, k, rows, S):
    return jnp.where(rows < S - k, pltpu.roll(x, S - k, 0), 0.0)


def _conv_fwd(proj, w, b, name):
    S = proj.shape[0]
    n = w.shape[1]
    cb = CONV_COLS

    def body(x_ref, w_ref, b_ref, o_ref):
        x = x_ref[...]
        rows = _iota(S, cb, 0)
        pre = x * w_ref[CONV_K - 1:CONV_K, :] + b_ref[...]
        for k in range(1, CONV_K):
            pre = pre + _shift_down(x, k, rows) * w_ref[CONV_K - 1 - k:CONV_K - k, :]
        o_ref[...] = _silu(pre)

    return pl.pallas_call(
        body, name=name, grid=(n // cb,),
        in_specs=[pl.BlockSpec((S, cb), lambda j: (0, j + C_XBC // cb)), pl.BlockSpec((CONV_K, cb), lambda j: (0, j)),
                  pl.BlockSpec((1, cb), lambda j: (0, j))],
        out_specs=pl.BlockSpec((S, cb), lambda j: (0, j)), out_shape=jax.ShapeDtypeStruct((S, n), F32),
        compiler_params=_params("parallel"))(proj, w, b)


def _conv_bwd(dact, proj, w, b, col0, name):
    S, n = dact.shape
    cb = CONV_COLS
    o = col0 // cb

    def body(d_ref, x_ref, w_ref, b_ref, dx_ref, dw_ref, db_ref):
        x = x_ref[...]
        rows = _iota(S, cb, 0)
        xs = [x] + [_shift_down(x, k, rows) for k in range(1, CONV_K)]
        pre = xs[0] * w_ref[CONV_K - 1:CONV_K, :] + b_ref[...]
        for k in range(1, CONV_K):
            pre = pre + xs[k] * w_ref[CONV_K - 1 - k:CONV_K - k, :]
        dpre = d_ref[...] * _dsilu(pre)
        db_ref[...] = _colsum(dpre)
        dx = dpre * w_ref[CONV_K - 1:CONV_K, :]
        for k in range(CONV_K):
            dw_ref[CONV_K - 1 - k:CONV_K - k, :] = _colsum(dpre * xs[k])
            if k:
                dx = dx + _shift_up(dpre, k, rows, S) * w_ref[CONV_K - 1 - k:CONV_K - k, :]
        dx_ref[...] = dx.astype(BF16)

    return pl.pallas_call(
        body, name=name, grid=(n // cb,),
        in_specs=[pl.BlockSpec((S, cb), lambda j: (0, j)), pl.BlockSpec((S, cb), lambda j: (0, j + o + C_XBC // cb)),
                  pl.BlockSpec((CONV_K, cb), lambda j: (0, j + o)), pl.BlockSpec((1, cb), lambda j: (0, j + o))],
        out_specs=(pl.BlockSpec((S, cb), lambda j: (0, j)), pl.BlockSpec((CONV_K, cb), lambda j: (0, j)),
                   pl.BlockSpec((1, cb), lambda j: (0, j))),
        out_shape=(jax.ShapeDtypeStruct((S, n), BF16), jax.ShapeDtypeStruct((CONV_K, n), F32),
                   jax.ShapeDtypeStruct((1, n), F32)),
        compiler_params=_params("parallel"))(dact, proj, w, b)


def _ssd_specs(L, order):
    G = SSM_GROUPS_PER_STEP
    W, N = G * SSM_GROUP_WIDTH, G * SSM_D_STATE
    x_spec = pl.BlockSpec((L, W), lambda g, c: (order(c), g))
    b_spec = pl.BlockSpec((L, N), lambda g, c: (order(c), 2048 // N + g))
    c_spec = pl.BlockSpec((L, N), lambda g, c: (order(c), 3072 // N + g))
    z_spec = pl.BlockSpec((L, W), lambda g, c: (order(c), C_ZS // W + g))
    dt_spec = pl.BlockSpec((G, L, SSM_HEADS_PER_GROUP), lambda g, c: (g, order(c), 0))
    p_spec = pl.BlockSpec((G, 3, SSM_HEADS_PER_GROUP), lambda g, c: (g, 0, 0))
    nw_spec = pl.BlockSpec((G, 1, SSM_GROUP_WIDTH), lambda g, c: (g, 0, 0))
    s_spec = pl.BlockSpec((G, 1, SSM_GROUP_WIDTH, SSM_D_STATE), lambda g, c: (g, order(c), 0, 0))
    return x_spec, b_spec, c_spec, z_spec, dt_spec, p_spec, nw_spec, s_spec


def _ssd_fwd(conv, proj, dt_raw, pvec, nw, name):
    S = conv.shape[0]
    L, P, N, H, W, G = SSM_CHUNK, SSM_HEAD_DIM, SSM_D_STATE, SSM_HEADS_PER_GROUP, SSM_GROUP_WIDTH, SSM_GROUPS_PER_STEP
    nc = S // L

    def body(x_ref, b_ref, c_ref, z_ref, dt_ref, p_ref, nw_ref, y_ref, yn_ref, s0_ref, state):
        c = pl.program_id(1)

        @pl.when(c == 0)
        def _():
            state[...] = jnp.zeros_like(state)

        ii, jj = _iota(L, L, 0), _iota(L, L, 1)
        Bm = [b_ref[:, gi * N:(gi + 1) * N] for gi in range(G)]
        Cm = [c_ref[:, gi * N:(gi + 1) * N] for gi in range(G)]
        CB = [_dot(Cm[gi], Bm[gi], _NT) for gi in range(G)]
        y1 = {}
        for gi in range(G):
            s0_ref[gi, 0] = state[gi * W:(gi + 1) * W, :]

        def head(gi, h):
            sl = slice(gi * W + h * P, gi * W + (h + 1) * P)
            p = p_ref[gi]
            dt = _softplus(dt_ref[gi][:, h:h + 1] + p[0:1, h:h + 1])
            a = dt * (-jnp.exp(p[1:2, h:h + 1]))
            acum, acum_row = _cumsum_forms(a, ii, jj)
            decay = jnp.exp(jnp.where(ii >= jj, acum - acum_row, NEG_INF))
            xh = x_ref[:, sl]
            xdt = xh * dt
            S0 = state[sl, :]
            a_last = acum[L - 1:L, :]
            y_diag = _dot(CB[gi] * decay, xdt)
            y_off = _dot(Cm[gi], S0, _NT)
            s_new = _dot(xdt * jnp.exp(a_last - acum), Bm[gi], _TN)
            yield
            yh = y_diag + y_off * jnp.exp(acum)
            state[sl, :] = S0 * jnp.exp(a_last) + s_new
            y_ref[:, sl] = yh
            y1[gi, h] = yh + p[2:3, h:h + 1] * xh

        _lockstep(head(gi, h) for gi in range(G) for h in range(H))
        for gi in range(G):
            gsl = slice(gi * W, (gi + 1) * W)
            y2 = jnp.concatenate([y1[gi, h] for h in range(H)], axis=1) * _silu(z_ref[:, gsl])
            r = lax.rsqrt(jnp.mean(y2 * y2, axis=-1, keepdims=True) + EPS)
            yn_ref[:, gsl] = (y2 * r * nw_ref[gi]).astype(BF16)

    x_spec, b_spec, c_spec, z_spec, dt_spec, p_spec, nw_spec, s_spec = _ssd_specs(L, lambda c: c)
    y_spec = pl.BlockSpec((L, G * W), lambda g, c: (c, g))
    return pl.pallas_call(
        body, name=name, grid=(SSM_GROUPS // G, nc),
        in_specs=[x_spec, b_spec, c_spec, z_spec, dt_spec, p_spec, nw_spec],
        out_specs=(y_spec, y_spec, s_spec),
        out_shape=(jax.ShapeDtypeStruct((S, SSM_GROUPS * W), F32), jax.ShapeDtypeStruct((S, SSM_GROUPS * W), BF16),
                   jax.ShapeDtypeStruct((SSM_GROUPS, nc, W, N), F32)),
        scratch_shapes=[pltpu.VMEM((G * W, N), F32)],
        compiler_params=_params("parallel", "arbitrary"))(conv, conv, conv, proj, dt_raw, pvec, nw)


def _ssd_bwd(dyn, conv, proj, dt_raw, pvec, nw, y_ssd, states, name):
    S = conv.shape[0]
    L, P, N, H, W, G = SSM_CHUNK, SSM_HEAD_DIM, SSM_D_STATE, SSM_HEADS_PER_GROUP, SSM_GROUP_WIDTH, SSM_GROUPS_PER_STEP
    nc = S // L

    def body(dyn_ref, x_ref, b_ref, c_ref, z_ref, dt_ref, p_ref, nw_ref, y_ref, s0_ref,
             dx_ref, db_ref, dc_ref, dz_ref, ddt_ref, dp_ref, dnw_ref, dstate):
        c = pl.program_id(1)

        @pl.when(c == 0)
        def _():
            dstate[...] = jnp.zeros_like(dstate)
            dp_ref[...] = jnp.zeros_like(dp_ref)
            dnw_ref[...] = jnp.zeros_like(dnw_ref)

        ii, jj = _iota(L, L, 0), _iota(L, L, 1)
        last = (_iota(L, 1, 0) == L - 1)
        Bm = [b_ref[:, gi * N:(gi + 1) * N] for gi in range(G)]
        Cm = [c_ref[:, gi * N:(gi + 1) * N] for gi in range(G)]
        CB = [_dot(Cm[gi], Bm[gi], _NT) for gi in range(G)]
        dy1 = []
        for gi in range(G):
            gsl = slice(gi * W, (gi + 1) * W)
            p = p_ref[gi]
            x, z = x_ref[:, gsl], z_ref[:, gsl]
            Dfull = jnp.concatenate([jnp.broadcast_to(p[2:3, h:h + 1], (1, P)) for h in range(H)], axis=1)
            y1 = y_ref[:, gsl] + Dfull * x
            sz = _silu(z)
            y2 = y1 * sz
            r = lax.rsqrt(jnp.mean(y2 * y2, axis=-1, keepdims=True) + EPS)
            y2h = y2 * r
            dyn_v = dyn_ref[:, gsl]
            dnw_ref[gi] += _colsum(dyn_v * y2h)
            dy2h = dyn_v * nw_ref[gi]
            dy2 = r * (dy2h - y2h * jnp.mean(dy2h * y2h, axis=-1, keepdims=True))
            dz_ref[:, gsl] = (dy2 * y1 * _dsilu(z)).astype(BF16)
            dy1.append(dy2 * sz)
        res = {}

        def head(gi, h):
            sl = slice(gi * W + h * P, gi * W + (h + 1) * P)
            p = p_ref[gi]
            dtr = dt_ref[gi][:, h:h + 1] + p[0:1, h:h + 1]
            dth = _softplus(dtr)
            A = -jnp.exp(p[1:2, h:h + 1])
            acum, acum_row = _cumsum_forms(dth * A, ii, jj)
            decay = jnp.exp(jnp.where(ii >= jj, acum - acum_row, NEG_INF))
            eac = jnp.exp(acum)
            a_last = acum[L - 1:L, :]
            wdec = jnp.exp(a_last - acum)
            xh = x_ref[:, sl]
            X = xh * dth
            dY = dy1[gi][:, h * P:(h + 1) * P]
            S0, dS1 = s0_ref[gi, 0, h * P:(h + 1) * P, :], dstate[sl, :]
            M = CB[gi] * decay
            dYe = dY * eac
            dXs_raw = _dot(Bm[gi], dS1, _NT)
            dXm = _dot(M, dY, _TN)
            Gm = _dot(dY, X, _NT)
            y_off_raw = _dot(Cm[gi], S0, _NT)
            dC_s = _dot(dYe, S0)
            dB_s = _dot(X * wdec, dS1)
            dS_c = _dot(dYe, Cm[gi], _TN)
            yield
            dXs = dXs_raw * wdec
            dX = dXm + dXs
            Q = Gm * M
            dstate[sl, :] = jnp.exp(a_last) * dS1 + dS_c
            d_alast = _total(X * dXs) + jnp.exp(a_last) * _total(dS1 * S0)
            dacum = (_rowsum(Q) - _rowsum(jnp.where(ii == jj, _colsum(Q), 0.0)) + _rowsum(dY * (y_off_raw * eac))
                     - _rowsum(X * dXs) + jnp.where(last, d_alast, 0.0))
            da = _rev_cumsum_col(dacum, ii, jj)
            ddt_raw = (da * A + _rowsum(dX * xh)) * _sigmoid(dtr)
            dx_ref[:, sl] = dX * dth + p[2:3, h:h + 1] * dY
            res[gi, h] = (Gm * decay, dC_s, dB_s, ddt_raw, _colsum(ddt_raw), _colsum(da * dth) * A, _total(dY * xh))

        _lockstep(head(gi, h) for gi in range(G) for h in range(H))
        for gi in range(G):
            parts = [res[gi, h] for h in range(H)]
            dCB = sum(pt[0] for pt in parts)
            dc_ref[:, gi * N:(gi + 1) * N] = sum(pt[1] for pt in parts) + _dot(dCB, Bm[gi])
            db_ref[:, gi * N:(gi + 1) * N] = sum(pt[2] for pt in parts) + _dot(dCB, Cm[gi], _TN)
            ddt_ref[gi] = jnp.concatenate([pt[3] for pt in parts], axis=1)
            dp_ref[gi] += jnp.concatenate([jnp.concatenate([pt[k] for pt in parts], axis=1) for k in (4, 5, 6)], axis=0)

    rev = lambda c: nc - 1 - c
    x_spec, b_spec, c_spec, z_spec, dt_spec, p_spec, nw_spec, s_spec = _ssd_specs(L, rev)
    y_spec = pl.BlockSpec((L, G * W), lambda g, c: (rev(c), g))
    n_spec = pl.BlockSpec((L, G * N), lambda g, c: (rev(c), g))
    return pl.pallas_call(
        body, name=name, grid=(SSM_GROUPS // G, nc),
        in_specs=[y_spec, x_spec, b_spec, c_spec, z_spec, dt_spec, p_spec, nw_spec, y_spec, s_spec],
        out_specs=(y_spec, n_spec, n_spec, y_spec, dt_spec, p_spec, nw_spec),
        out_shape=(jax.ShapeDtypeStruct((S, SSM_GROUPS * W), F32), jax.ShapeDtypeStruct((S, SSM_GROUPS * N), F32),
                   jax.ShapeDtypeStruct((S, SSM_GROUPS * N), F32), jax.ShapeDtypeStruct((S, SSM_GROUPS * W), BF16),
                   jax.ShapeDtypeStruct((SSM_GROUPS, S, H), F32), jax.ShapeDtypeStruct((SSM_GROUPS, 3, H), F32),
                   jax.ShapeDtypeStruct((SSM_GROUPS, 1, W), F32)),
        scratch_shapes=[pltpu.VMEM((G * W, N), F32)],
        compiler_params=_params("parallel", "arbitrary"))(dyn, conv, conv, conv, proj, dt_raw, pvec, nw, y_ssd, states)


def _unit_lower_inverse(A, ii, jj):
    eye = (ii == jj).astype(F32)
    same = (ii // GDN_INV_BLOCK) == (jj // GDN_INV_BLOCK)
    Ad = jnp.where(same, A, 0.0)
    Ao = A - Ad
    P2 = _dot3(Ad, Ad)
    yield
    P4, X = _dot3(P2, P2), _dot3(eye - Ad, eye + P2)
    yield
    P8, X = _dot3(P4, P4), _dot3(X, eye + P4)
    yield
    X = _dot3(X, eye + P8)
    yield
    Bm = _dot3(X, Ao)
    yield
    B2 = _dot3(Bm, Bm)
    yield
    Y = _dot3(eye - Bm, eye + B2)
    yield
    T = _dot3(Y, X)
    yield
    return T


def _gdn_specs(L, order):
    G = GDN_QK_PER_STEP
    Hd, W = G * GDN_HEAD, G * GDN_V_PER_QK * GDN_HEAD
    q_spec = pl.BlockSpec((L, Hd), lambda h, c: (order(c), (C_QKV - C_XBC) // Hd + h))
    k_spec = pl.BlockSpec((L, Hd), lambda h, c: (order(c), (C_QKV - C_XBC + 1024) // Hd + h))
    v_spec = pl.BlockSpec((L, W), lambda h, c: (order(c), (C_QKV - C_XBC + 2048) // W + h))
    z_spec = pl.BlockSpec((L, W), lambda h, c: (order(c), C_ZG // W + h))
    ba_spec = pl.BlockSpec((G, L, GDN_V_PER_QK), lambda h, c: (h, order(c), 0))
    p_spec = pl.BlockSpec((G, 2, GDN_V_PER_QK), lambda h, c: (h, 0, 0))
    nw_spec = pl.BlockSpec((1, GDN_HEAD), lambda h, c: (0, 0))
    s_spec = pl.BlockSpec((G, 1, GDN_V_PER_QK * GDN_HEAD, GDN_HEAD), lambda h, c: (h, order(c), 0, 0))
    t_spec = pl.BlockSpec((G * GDN_V_PER_QK, 1, L, L), lambda h, c: (h, order(c), 0, 0))
    return q_spec, k_spec, v_spec, z_spec, ba_spec, p_spec, nw_spec, s_spec, t_spec


def _gdn_gates(qa, ka, b_col, a_col, p, j, ii, jj):
    L = qa.shape[0]
    rq = lax.rsqrt(_rowsum(qa * qa) + EPS)
    rk = lax.rsqrt(_rowsum(ka * ka) + EPS)
    q = qa * rq * (GDN_HEAD ** -0.5)
    k = ka * rk
    beta = _sigmoid(b_col)
    sp_in = a_col + p[0:1, j:j + 1]
    neg_ea = -jnp.exp(p[1:2, j:j + 1])
    g = neg_ea * _softplus(sp_in)
    gcum, gcum_row = _cumsum_forms(g, ii, jj)
    Dm = jnp.exp(jnp.where(ii >= jj, gcum - gcum_row, NEG_INF))
    eg = jnp.exp(gcum)
    g_last = gcum[L - 1:L, :]
    wdec = jnp.exp(g_last - gcum)
    return dict(rq=rq, rk=rk, q=q, k=k, beta=beta, sp_in=sp_in, neg_ea=neg_ea, g=g, Dm=Dm, kbeta=k * beta, eg=eg,
                g_last=g_last, wdec=wdec, kdec=k * wdec)


def _gdn_fwd(conv, proj, b_raw, a_raw, pvec, nw, name):
    S = conv.shape[0]
    L, Hd, J, G = GDN_CHUNK, GDN_HEAD, GDN_V_PER_QK, GDN_QK_PER_STEP
    W = J * Hd
    nc = S // L

    def body(q_ref, k_ref, v_ref, z_ref, b_ref, a_ref, p_ref, nw_ref, o_ref, on_ref, s0_ref, t_ref, state):
        c = pl.program_id(1)

        @pl.when(c == 0)
        def _():
            state[...] = jnp.zeros_like(state)

        ii, jj = _iota(L, L, 0), _iota(L, L, 1)
        for hq in range(G):
            s0_ref[hq, 0] = state[hq * W:(hq + 1) * W, :]

        def head(hq, j):
            hd = hq * J + j
            hsl, sl = slice(hq * Hd, (hq + 1) * Hd), slice(hd * Hd, (hd + 1) * Hd)
            t = _gdn_gates(q_ref[:, hsl], k_ref[:, hsl], b_ref[hq][:, j:j + 1], a_ref[hq][:, j:j + 1], p_ref[hq], j,
                           ii, jj)
            KK = _dot(t["kbeta"], t["k"], _NT)
            QK = _dot(t["q"], t["k"], _NT)
            yield
            T = yield from _unit_lower_inverse(jnp.where(ii > jj, KK * t["Dm"], 0.0), ii, jj)
            t_ref[hd, 0] = T
            S0 = state[sl, :]
            U = _dot3(T, v_ref[:, sl] * t["beta"])
            Wm = _dot3(T, t["kbeta"] * t["eg"])
            o_inter = _dot(t["q"] * t["eg"], S0)
            yield
            Vn = U - _dot(Wm, S0)
            yield
            o = o_inter + _dot(QK * t["Dm"], Vn)
            s_new = _dot(t["kdec"], Vn, _TN)
            yield
            state[sl, :] = S0 * jnp.exp(t["g_last"]) + s_new
            o_ref[:, sl] = o
            r = lax.rsqrt(jnp.mean(o * o, axis=-1, keepdims=True) + EPS)
            on_ref[:, sl] = ((o * r * nw_ref[...]) * _silu(z_ref[:, sl])).astype(BF16)

        _lockstep(head(hq, j) for hq in range(G) for j in range(J))

    q_spec, k_spec, v_spec, z_spec, ba_spec, p_spec, nw_spec, s_spec, t_spec = _gdn_specs(L, lambda c: c)
    o_spec = pl.BlockSpec((L, G * W), lambda h, c: (c, h))
    return pl.pallas_call(
        body, name=name, grid=(GDN_QK_HEADS // G, nc),
        in_specs=[q_spec, k_spec, v_spec, z_spec, ba_spec, ba_spec, p_spec, nw_spec],
        out_specs=(o_spec, o_spec, s_spec, t_spec),
        out_shape=(jax.ShapeDtypeStruct((S, GDN_QK_HEADS * W), F32), jax.ShapeDtypeStruct((S, GDN_QK_HEADS * W), BF16),
                   jax.ShapeDtypeStruct((GDN_QK_HEADS, nc, W, Hd), F32),
                   jax.ShapeDtypeStruct((GDN_QK_HEADS * J, nc, L, L), F32)),
        scratch_shapes=[pltpu.VMEM((G * W, Hd), F32)],
        compiler_params=_params("parallel", "arbitrary"))(conv, conv, conv, proj, b_raw, a_raw, pvec, nw)


def _gdn_bwd(don, conv, proj, b_raw, a_raw, pvec, nw, o_pre, states, t_inv, name):
    S = conv.shape[0]
    L, Hd, J, G = GDN_CHUNK, GDN_HEAD, GDN_V_PER_QK, GDN_QK_PER_STEP
    W = J * Hd
    nc = S // L

    def body(don_ref, q_ref, k_ref, v_ref, z_ref, b_ref, a_ref, p_ref, nw_ref, o_ref, s0_ref, t_ref,
             dq_ref, dk_ref, dv_ref, dz_ref, db_ref, da_ref, dp_ref, dnw_ref, dstate):
        c = pl.program_id(1)

        @pl.when(c == 0)
        def _():
            dstate[...] = jnp.zeros_like(dstate)
            dp_ref[...] = jnp.zeros_like(dp_ref)
            dnw_ref[...] = jnp.zeros_like(dnw_ref)

        ii, jj = _iota(L, L, 0), _iota(L, L, 1)
        last = (_iota(L, 1, 0) == L - 1)
        res = {}

        def head(hq, j):
            hd = hq * J + j
            hsl, sl = slice(hq * Hd, (hq + 1) * Hd), slice(hd * Hd, (hd + 1) * Hd)
            qa, ka = q_ref[:, hsl], k_ref[:, hsl]
            t = _gdn_gates(qa, ka, b_ref[hq][:, j:j + 1], a_ref[hq][:, j:j + 1], p_ref[hq], j, ii, jj)
            q, k, beta, eg, Dm, kbeta, kdec = (t[nm] for nm in ("q", "k", "beta", "eg", "Dm", "kbeta", "kdec"))
            T = t_ref[hd, 0]
            v, z, o = v_ref[:, sl], z_ref[:, sl], o_ref[:, sl]
            S0, dS1 = s0_ref[hq, 0, j * Hd:(j + 1) * Hd, :], dstate[sl, :]
            sz = _silu(z)
            r = lax.rsqrt(jnp.mean(o * o, axis=-1, keepdims=True) + EPS)
            oh = o * r
            d_on = don_ref[:, sl]
            dz_ref[:, sl] = (d_on * (oh * nw_ref[...]) * _dsilu(z)).astype(BF16)
            dn = d_on * sz
            dnw_part = _colsum(dn * oh)
            doh = dn * nw_ref[...]
            dO = r * (doh - oh * jnp.mean(doh * oh, axis=-1, keepdims=True))
            Rw = kbeta * eg
            qe = q * eg
            U = _dot3(T, v * beta)
            Wm = _dot3(T, Rw)
            KK = _dot(kbeta, k, _NT)
            QK = _dot(q, k, _NT)
            o_inter = _dot(qe, S0)
            dq_s = _dot(dO, S0, _NT)
            dS_q = _dot(qe, dO, _TN)
            yield
            Am = jnp.where(ii > jj, KK * Dm, 0.0)
            Pm = QK * Dm
            Vn = U - _dot(Wm, S0)
            dVn_s = _dot(kdec, dS1)
            yield
            dVn = _dot(Pm, dO, _TN) + dVn_s
            dP = _dot(dO, Vn, _NT)
            dKd = _dot(Vn, dS1, _NT)
            yield
            dQK = dP * Dm
            dq = _dot(dQK, k) + dq_s * eg
            dk = _dot(dQK, q, _TN) + dKd * t["wdec"]
            dstate[sl, :] = jnp.exp(t["g_last"]) * dS1 + dS_q - _dot(Wm, dVn, _TN)
            dW = -_dot(dVn, S0, _NT)
            dRu = _dot3(T, dVn, _TN)
            yield
            dRw = _dot3(T, dW, _TN)
            dA_u = _dot(dRu, U, _NT)
            yield
            dA = jnp.where(ii > jj, -(dA_u + _dot(dRw, Wm, _NT)), 0.0)
            yield
            dKK = dA * Dm
            dkbeta = _dot(dKK, k) + dRw * eg
            dk = dk + _dot(dKK, kbeta, _TN)
            yield
            dk = dk + dkbeta * beta
            dbeta = _rowsum(dkbeta * k) + _rowsum(dRu * v)
            dv_ref[:, sl] = dRu * beta
            Q = dA * Am + dP * Pm
            rho = _rowsum(dKd * kdec)
            d_glast = _colsum(rho) + jnp.exp(t["g_last"]) * _total(dS1 * S0)
            col_of_colsum = _rowsum(jnp.where(ii == jj, _colsum(Q), 0.0))
            dgam = (_rowsum(dRw * Rw) + _rowsum(Q) - col_of_colsum + _rowsum(dO * o_inter) - rho
                    + jnp.where(last, d_glast, 0.0))
            dg = _rev_cumsum_col(dgam, ii, jj)
            da_raw = dg * t["neg_ea"] * _sigmoid(t["sp_in"])
            res[hq, j] = dict(dq=dq, dk=dk, db=dbeta * beta * (1.0 - beta), da=da_raw, d_bias=_colsum(da_raw),
                              d_alog=_colsum(dg * t["g"]), dnw=dnw_part, rq=t["rq"], rk=t["rk"], k=k, qh=qa * t["rq"])

        _lockstep(head(hq, j) for hq in range(G) for j in range(J))
        for hq in range(G):
            parts = [res[hq, j] for j in range(J)]
            hsl = slice(hq * Hd, (hq + 1) * Hd)
            p0 = parts[0]
            dqh = sum(pt["dq"] for pt in parts) * (GDN_HEAD ** -0.5)
            dkn = sum(pt["dk"] for pt in parts)
            dq_ref[:, hsl] = p0["rq"] * (dqh - p0["qh"] * _rowsum(dqh * p0["qh"]))
            dk_ref[:, hsl] = p0["rk"] * (dkn - p0["k"] * _rowsum(dkn * p0["k"]))
            db_ref[hq] = jnp.concatenate([pt["db"] for pt in parts], axis=1)
            da_ref[hq] = jnp.concatenate([pt["da"] for pt in parts], axis=1)
            dp_ref[hq] += jnp.concatenate([jnp.concatenate([pt["d_bias"] for pt in parts], axis=1),
                                           jnp.concatenate([pt["d_alog"] for pt in parts], axis=1)], axis=0)
            dnw_ref[hq] += sum(pt["dnw"] for pt in parts)

    rev = lambda c: nc - 1 - c
    q_spec, k_spec, v_spec, z_spec, ba_spec, p_spec, nw_spec, s_spec, t_spec = _gdn_specs(L, rev)
    o_spec = pl.BlockSpec((L, G * W), lambda h, c: (rev(c), h))
    h_spec = pl.BlockSpec((L, G * Hd), lambda h, c: (rev(c), h))
    dnw_spec = pl.BlockSpec((G, 1, Hd), lambda h, c: (h, 0, 0))
    return pl.pallas_call(
        body, name=name, grid=(GDN_QK_HEADS // G, nc),
        in_specs=[o_spec, q_spec, k_spec, v_spec, z_spec, ba_spec, ba_spec, p_spec, nw_spec, o_spec, s_spec, t_spec],
        out_specs=(h_spec, h_spec, o_spec, o_spec, ba_spec, ba_spec, p_spec, dnw_spec),
        out_shape=(jax.ShapeDtypeStruct((S, GDN_QK_HEADS * Hd), F32), jax.ShapeDtypeStruct((S, GDN_QK_HEADS * Hd), F32),
                   jax.ShapeDtypeStruct((S, GDN_QK_HEADS * W), F32), jax.ShapeDtypeStruct((S, GDN_QK_HEADS * W), BF16),
                   jax.ShapeDtypeStruct((GDN_QK_HEADS, S, J), F32), jax.ShapeDtypeStruct((GDN_QK_HEADS, S, J), F32),
                   jax.ShapeDtypeStruct((GDN_QK_HEADS, 2, J), F32), jax.ShapeDtypeStruct((GDN_QK_HEADS, 1, Hd), F32)),
        scratch_shapes=[pltpu.VMEM((G * W, Hd), F32)],
        compiler_params=_params("parallel", "arbitrary"))(don, conv, conv, conv, proj, b_raw, a_raw, pvec, nw, o_pre,
                                                          states, t_inv)


def _ada_fwd(c_all, w_loc, b_loc, name):
    n = w_loc.shape[1]

    def body(c_ref, w_ref, b_ref, o_ref):
        o_ref[...] = _dot3(_silu(c_ref[...]), w_ref[...]) + b_ref[...]

    return pl.pallas_call(body, name=name, out_shape=jax.ShapeDtypeStruct((N_DEV, n), F32),
                          compiler_params=pltpu.CompilerParams(vmem_limit_bytes=VMEM_LIMIT))(c_all, w_loc, b_loc)


def _ada_bwd(c_all_t, dmod_cols, name):
    Dm, n = c_all_t.shape[0], dmod_cols.shape[1]

    def body(c_ref, d_ref, o_ref):
        ca = _silu(c_ref[...])
        acc = ca[:, 0:1] * d_ref[0:1, :]
        for i in range(1, N_DEV):
            acc = acc + ca[:, i:i + 1] * d_ref[i:i + 1, :]
        o_ref[...] = acc

    return pl.pallas_call(body, name=name, out_shape=jax.ShapeDtypeStruct((Dm, n), F32),
                          compiler_params=pltpu.CompilerParams(vmem_limit_bytes=VMEM_LIMIT))(c_all_t, dmod_cols)


ADAM_BLOCK_BYTES = 12 * 1024 * 1024


def _adam(contrib, w, m, v, name):
    n, R, C = contrib.shape
    tr = R
    while tr % 16 == 0 and (n + 7) * tr * C * 4 > ADAM_BLOCK_BYTES:
        tr //= 2

    def body(c_ref, w_ref, m_ref, v_ref, g_ref, d_ref, nm_ref, nv_ref):
        g = c_ref[0].astype(F32)
        for i in range(1, n):
            g = g + c_ref[i].astype(F32)
        nm = ADAM_B1 * m_ref[...] + (1.0 - ADAM_B1) * g
        nv = ADAM_B2 * v_ref[...] + (1.0 - ADAM_B2) * (g * g)
        m_hat = nm / (1.0 - ADAM_B1 ** ADAM_STEP)
        v_hat = nv / (1.0 - ADAM_B2 ** ADAM_STEP)
        g_ref[...] = g
        d_ref[...] = -ADAM_LR * (m_hat / (jnp.sqrt(v_hat) + ADAM_EPS) + ADAM_WD * w_ref[...])
        nm_ref[...] = nm
        nv_ref[...] = nv

    spec = pl.BlockSpec((tr, C), lambda i: (i, 0))
    shp = jax.ShapeDtypeStruct((R, C), F32)
    return pl.pallas_call(
        body, name=name, grid=(R // tr,), in_specs=[pl.BlockSpec((n, tr, C), lambda i: (0, i, 0)), spec, spec, spec],
        out_specs=(spec,) * 4, out_shape=(shp,) * 4, compiler_params=_params("parallel"))(contrib, w, m, v)


def _exchange(arrays, modes, name):
    n = len(arrays)
    out_shape = tuple(jax.ShapeDtypeStruct((N_DEV,) + a.shape if md == "gather" else a.shape, a.dtype)
                      for a, md in zip(arrays, modes))

    def body(*refs):
        ins, outs = refs[:n], refs[n:2 * n]
        send_sems, recv_sems, loc_sems = refs[2 * n:]
        ix, iy, ic = lax.axis_index("x"), lax.axis_index("y"), lax.axis_index("c")
        me = 4 * ix + 2 * iy + ic
        peers = []
        for m in range(1, N_DEV):
            px = 1 - ix if m & 4 else ix
            py = 1 - iy if m & 2 else iy
            pc = 1 - ic if m & 1 else ic
            peers.append(((px, py, pc), 4 * px + 2 * py + pc))

        def src(k, slot):
            return ins[k] if modes[k] == "gather" else ins[k].at[slot]

        def remote(k, m, to_slot, land_slot):
            return pltpu.make_async_remote_copy(
                src_ref=src(k, to_slot), dst_ref=outs[k].at[land_slot], send_sem=send_sems.at[k, m],
                recv_sem=recv_sems.at[k, m], device_id=peers[m][0], device_id_type=pl.DeviceIdType.MESH)

        local = [pltpu.make_async_copy(src(k, me), outs[k].at[me], loc_sems.at[k]) for k in range(n)]
        for cp in local:
            cp.start()
        sends = [remote(k, m, peers[m][1], me) for m in range(N_DEV - 1) for k in range(n)]
        for cp in sends:
            cp.start()
        for m in range(N_DEV - 1):
            for k in range(n):
                remote(k, m, peers[m][1], peers[m][1]).wait_recv()
        for cp in sends:
            cp.wait_send()
        for cp in local:
            cp.wait()

    any_spec = pl.BlockSpec(memory_space=pl.ANY)
    return pl.pallas_call(
        body, name=name, in_specs=[any_spec] * n, out_specs=(any_spec,) * n, out_shape=out_shape,
        scratch_shapes=[pltpu.SemaphoreType.DMA((n, N_DEV - 1)), pltpu.SemaphoreType.DMA((n, N_DEV - 1)),
                        pltpu.SemaphoreType.DMA((n,))])(*arrays)


def _peer_table():
    ix, iy, ic = lax.axis_index("x"), lax.axis_index("y"), lax.axis_index("c")
    peers = []
    for m in range(1, N_DEV):
        px = 1 - ix if m & 4 else ix
        py = 1 - iy if m & 2 else iy
        pc = 1 - ic if m & 1 else ic
        peers.append(((px, py, pc), 4 * px + 2 * py + pc))
    return 4 * ix + 2 * iy + ic, peers


def _exchange_start(arrays, modes, after, name):
    n = len(arrays)
    land_shapes = [(N_DEV,) + a.shape if md == "gather" else a.shape for a, md in zip(arrays, modes)]

    def body(*refs):
        ins, lands = refs[:n], refs[n:2 * n]
        send_sems, recv_sems = refs[2 * n + 1], refs[2 * n + 2]
        token, loc_sems = refs[-2], refs[-1]
        me, peers = _peer_table()

        def src(k, slot):
            return ins[k] if modes[k] == "gather" else ins[k].at[slot]

        local = [pltpu.make_async_copy(src(k, me), lands[k].at[me], loc_sems.at[k]) for k in range(n)]
        for cp in local:
            cp.start()
        for m in range(N_DEV - 1):
            for k in range(n):
                pltpu.make_async_remote_copy(
                    src_ref=src(k, peers[m][1]), dst_ref=lands[k].at[me], send_sem=send_sems, recv_sem=recv_sems,
                    device_id=peers[m][0], device_id_type=pl.DeviceIdType.MESH).start()
        for cp in local:
            cp.wait()
        token[...] = jnp.zeros_like(token)

    hbm = pl.BlockSpec(memory_space=pltpu.HBM)
    sem = pl.BlockSpec(memory_space=pltpu.SEMAPHORE)
    sem_shape = pltpu.SemaphoreType.DMA(())
    operands = [pltpu.with_memory_space_constraint(a, pltpu.HBM) for a in arrays]
    operands += [pltpu.with_memory_space_constraint(lax.empty(s, a.dtype), pltpu.HBM)
                 for s, a in zip(land_shapes, arrays)]
    out = pl.pallas_call(
        body, name=name,
        out_shape=(sem_shape, sem_shape) + tuple(pltpu.HBM(a.shape, a.dtype) for a in arrays)
        + tuple(pltpu.HBM(s, a.dtype) for s, a in zip(land_shapes, arrays)) + (jax.ShapeDtypeStruct((8, 128), F32),),
        in_specs=[hbm] * (2 * n) + [pl.BlockSpec(memory_space=pl.ANY)],
        out_specs=(sem, sem) + (hbm,) * (2 * n) + (pl.BlockSpec(memory_space=pltpu.VMEM),),
        input_output_aliases={i: 2 + i for i in range(2 * n)},
        scratch_shapes=[pltpu.SemaphoreType.DMA((n,))],
        compiler_params=pltpu.CompilerParams(has_side_effects=pltpu.SideEffectType.DATAFLOW_SIDE_EFFECTING))(
            *operands, after)
    return out[0], out[1], out[2:2 + n], out[2 + n:2 + 2 * n], out[-1]


def _exchange_wait(started, modes, after, name):
    send_sems, recv_sems, sent, lands, _ = started
    n = len(sent)

    def body(*refs):
        ins, zones = refs[:n], refs[n:2 * n]
        send_ref, recv_ref = refs[2 * n], refs[2 * n + 1]
        me, peers = _peer_table()

        def src(k, slot):
            return ins[k] if modes[k] == "gather" else ins[k].at[slot]

        for m in range(N_DEV - 1):
            for k in range(n):
                cp = pltpu.make_async_remote_copy(
                    src_ref=src(k, peers[m][1]), dst_ref=zones[k].at[peers[m][1]], send_sem=send_ref,
                    recv_sem=recv_ref, device_id=peers[m][0], device_id_type=pl.DeviceIdType.MESH)
                cp.wait_send()
                cp.wait_recv()

    hbm = pl.BlockSpec(memory_space=pltpu.HBM)
    sem = pl.BlockSpec(memory_space=pltpu.SEMAPHORE)
    out = pl.pallas_call(
        body, name=name,
        out_shape=tuple(pltpu.HBM(a.shape, a.dtype) for a in sent) + tuple(pltpu.HBM(a.shape, a.dtype) for a in lands),
        in_specs=[hbm] * (2 * n) + [sem, sem, pl.BlockSpec(memory_space=pl.ANY)], out_specs=(hbm,) * (2 * n),
        input_output_aliases={i: i for i in range(2 * n)},
        compiler_params=pltpu.CompilerParams(has_side_effects=pltpu.SideEffectType.DATAFLOW_SIDE_EFFECTING))(
            *sent, *lands, send_sems, recv_sems, after)
    return out[n:]


W_IN_SPLITS = (0, 2048, 6144, 6176, 10272, 12320, 12336, 12352, 13376, 14400)
N_REPLICATED = 16640
REPLICATED = ("b_ada", "norm_mix_pre", "norm_mix_post", "ssm_conv_b", "ssm_dt_bias", "ssm_A_log", "ssm_D",
              "ssm_norm_w", "gdn_dt_bias", "gdn_A_log", "gdn_norm_w", "norm_mlp_pre", "norm_mlp_post")
WEIGHTS = ("w_ada", "b_ada", "norm_mix_pre", "norm_mix_post", "w_in", "ssm_conv_w", "ssm_conv_b", "ssm_dt_bias",
           "ssm_A_log", "ssm_D", "ssm_norm_w", "gdn_conv_w", "gdn_dt_bias", "gdn_A_log", "gdn_norm_w", "w_ssm_up",
           "w_gdn_up", "w_out", "norm_mlp_pre", "norm_mlp_post", "w_mlp_up", "w_mlp_down")


def _by_cols(t):
    return t.transpose(1, 0, 2).reshape(t.shape[1], N_DEV * t.shape[2])


def _to_col_shards(t):
    R, C8 = t.shape
    return t.reshape(R, N_DEV, C8 // N_DEV).transpose(1, 0, 2)


def _heads_first(t, groups):
    S = t.shape[0]
    return t.reshape(S, groups, t.shape[1] // groups).transpose(1, 0, 2)


def _heads_last(t):
    return t.transpose(1, 0, 2).reshape(t.shape[1], t.shape[0] * t.shape[2])


def kernel(x, c, w_ada, b_ada, norm_mix_pre, norm_mix_post, w_in, ssm_conv_w, ssm_conv_b, ssm_dt_bias, ssm_A_log, ssm_D, ssm_norm_w, gdn_conv_w, gdn_dt_bias, gdn_A_log, gdn_norm_w, w_ssm_up, w_gdn_up, w_out, norm_mlp_pre, norm_mlp_post, w_mlp_up, w_mlp_down, loss_target, m_w_ada, m_b_ada, m_norm_mix_pre, m_norm_mix_post, m_w_in, m_ssm_conv_w, m_ssm_conv_b, m_ssm_dt_bias, m_ssm_A_log, m_ssm_D, m_ssm_norm_w, m_gdn_conv_w, m_gdn_dt_bias, m_gdn_A_log, m_gdn_norm_w, m_w_ssm_up, m_w_gdn_up, m_w_out, m_norm_mlp_pre, m_norm_mlp_post, m_w_mlp_up, m_w_mlp_down, v_w_ada, v_b_ada, v_norm_mix_pre, v_norm_mix_post, v_w_in, v_ssm_conv_w, v_ssm_conv_b, v_ssm_dt_bias, v_ssm_A_log, v_ssm_D, v_ssm_norm_w, v_gdn_conv_w, v_gdn_dt_bias, v_gdn_A_log, v_gdn_norm_w, v_w_ssm_up, v_w_gdn_up, v_w_out, v_norm_mlp_pre, v_norm_mlp_post, v_w_mlp_up, v_w_mlp_down):
    S, Dm = x.shape[1], D_MODEL
    me = 4 * lax.axis_index("x") + 2 * lax.axis_index("y") + lax.axis_index("c")
    x2, tgt = x[0], loss_target[0]
    n_ada = w_ada.shape[2]
    given = dict(
        w_ada=(w_ada, m_w_ada, v_w_ada), b_ada=(b_ada, m_b_ada, v_b_ada),
        norm_mix_pre=(norm_mix_pre, m_norm_mix_pre, v_norm_mix_pre),
        norm_mix_post=(norm_mix_post, m_norm_mix_post, v_norm_mix_post), w_in=(w_in, m_w_in, v_w_in),
        ssm_conv_w=(ssm_conv_w, m_ssm_conv_w, v_ssm_conv_w), ssm_conv_b=(ssm_conv_b, m_ssm_conv_b, v_ssm_conv_b),
        ssm_dt_bias=(ssm_dt_bias, m_ssm_dt_bias, v_ssm_dt_bias), ssm_A_log=(ssm_A_log, m_ssm_A_log, v_ssm_A_log),
        ssm_D=(ssm_D, m_ssm_D, v_ssm_D), ssm_norm_w=(ssm_norm_w, m_ssm_norm_w, v_ssm_norm_w),
        gdn_conv_w=(gdn_conv_w, m_gdn_conv_w, v_gdn_conv_w), gdn_dt_bias=(gdn_dt_bias, m_gdn_dt_bias, v_gdn_dt_bias),
        gdn_A_log=(gdn_A_log, m_gdn_A_log, v_gdn_A_log), gdn_norm_w=(gdn_norm_w, m_gdn_norm_w, v_gdn_norm_w),
        w_ssm_up=(w_ssm_up, m_w_ssm_up, v_w_ssm_up), w_gdn_up=(w_gdn_up, m_w_gdn_up, v_w_gdn_up),
        w_out=(w_out, m_w_out, v_w_out), norm_mlp_pre=(norm_mlp_pre, m_norm_mlp_pre, v_norm_mlp_pre),
        norm_mlp_post=(norm_mlp_post, m_norm_mlp_post, v_norm_mlp_post), w_mlp_up=(w_mlp_up, m_w_mlp_up, v_w_mlp_up),
        w_mlp_down=(w_mlp_down, m_w_mlp_down, v_w_mlp_down))

    (c_all, scw, gcw, g_in) = _exchange([c, ssm_conv_w[0], gdn_conv_w[0], w_in[0].astype(BF16)], ["gather"] * 4,
                                        "gather_w_in")
    c_all = c_all.reshape(N_DEV, Dm)
    rest = _exchange_start([w_ssm_up[0].astype(BF16), w_gdn_up[0].astype(BF16), w_out[0].astype(BF16),
                            w_mlp_up[0].astype(BF16), w_mlp_down[0].astype(BF16)], ["gather"] * 5, c_all,
                           "gather_rest_start")
    wf = _by_cols(g_in)
    sp = W_IN_SPLITS
    w_main = jnp.concatenate([wf[:, sp[0]:sp[2]], wf[:, sp[3]:sp[5]], wf[:, sp[7]:sp[9]]], axis=1)
    w_small = jnp.concatenate([wf[:, sp[2]:sp[3]], wf[:, sp[5]:sp[7]], jnp.zeros((Dm, N_SMALL - 64), BF16)], axis=1)
    conv_w = jnp.concatenate([_by_cols(scw), _by_cols(gcw)], axis=1)
    conv_b = jnp.concatenate([ssm_conv_b, jnp.zeros_like(ssm_conv_b)], axis=1)

    b_loc = lax.dynamic_slice(b_ada, (0, me * n_ada), (1, n_ada)) + rest[4][0:1, 0:1]
    mod_part = _ada_fwd(c_all, w_ada[0], b_loc, "ada_fwd")
    (mod_rows,) = _exchange([mod_part.reshape(N_DEV, 1, n_ada)], ["a2a"], "exchange_mod")
    mod = mod_rows.reshape(1, 6 * Dm)
    sh1, sc1, g1, sh2, sc2, g2 = [mod[:, i * Dm:(i + 1) * Dm] for i in range(6)]

    h = _pre_fwd(x2, norm_mix_pre, sc1, sh1, "pre_mix")
    proj = _mm(h, w_main, S, N_MAIN, Dm, mode="nn", out_dtype=F32, name="proj_main")
    small = _mm(h, w_small, S, N_SMALL, Dm, mode="nn", out_dtype=F32, name="proj_small")
    conv = _conv_fwd(proj, conv_w, conv_b, "conv_fwd")
    dt_g, b_g, a_g = _heads_first(small[:, 0:32], 8), _heads_first(small[:, 32:48], 8), _heads_first(small[:, 48:64], 8)
    pv_ssm = jnp.stack([ssm_dt_bias.reshape(8, 4), ssm_A_log.reshape(8, 4), ssm_D.reshape(8, 4)], axis=1)
    nw_ssm = ssm_norm_w.reshape(8, 1, SSM_GROUP_WIDTH)
    pv_gdn = jnp.stack([gdn_dt_bias.reshape(8, 2), gdn_A_log.reshape(8, 2)], axis=1)
    y_ssd, ysn, st_ssm = _ssd_fwd(conv, proj, dt_g, pv_ssm, nw_ssm, "ssd_fwd")
    o_pre, ogn, st_gdn, t_inv = _gdn_fwd(conv, proj, b_g, a_g, pv_gdn, gdn_norm_w, "gdn_fwd")
    g_su, g_gu, g_out, g_mu, g_md = _exchange_wait(rest, ["gather"] * 5, ogn, "gather_rest_wait")
    w_su, w_gu = g_su.reshape(2 * Dm, Dm), g_gu.reshape(2 * Dm, Dm)
    w_o, w_mu, w_md = g_out.reshape(Dm, Dm), _by_cols(g_mu), g_md.reshape(4 * Dm, Dm)
    ys = _mm(ysn, w_su, S, Dm, 2 * Dm, mode="nn", out_dtype=F32, name="ssm_up")
    yg = _mm(ogn, w_gu, S, Dm, 2 * Dm, mode="nn", out_dtype=F32, name="gdn_up")
    merged = _merge_fwd(ys, yg, proj, "merge_fwd")
    mo = _mm(merged, w_o, S, Dm, Dm, mode="nn", out_dtype=F32, name="mix_out")
    x1 = _post_fwd(x2, mo, norm_mix_post, g1, "post_mix")
    h2 = _pre_fwd(x1, norm_mlp_pre, sc2, sh2, "pre_mlp")
    u, act = _mm(h2, w_mu, S, 4 * Dm, Dm, mode="nn", out_dtype=F32, epi="relu2", name="mlp_up")
    y_mlp = _mm(act, w_md, S, Dm, 4 * Dm, mode="nn", out_dtype=F32, name="mlp_down")
    dx2, loss_loc = _final_fwd(x1, y_mlp, norm_mlp_post, g2, tgt, "post_mlp_loss")

    dy, dg2, dw_post2 = _post_bwd(dx2, y_mlp, norm_mlp_post, g2, "post_mlp_bwd")
    du = _mm(dy, w_md, S, 4 * Dm, Dm, mode="nt", out_dtype=BF16, epi="drelu2", extra=u, name="mlp_down_dx")
    gw_md = _mm(act, dy, 4 * Dm, Dm, S, mode="tn", out_dtype=BF16, name="mlp_down_dw")
    dh2 = _mm(du, w_mu, S, Dm, 4 * Dm, mode="nt", out_dtype=F32, name="mlp_up_dx")
    gw_mu = _mm(h2, du, Dm, 4 * Dm, S, mode="tn", out_dtype=BF16, name="mlp_up_dw")
    mlp_x = _exchange_start([_to_col_shards(gw_mu), gw_md.reshape(N_DEV, -1, Dm)], ["a2a"] * 2, gw_md,
                            "grads_mlp_start")
    dx1, dsh2, dsc2, dw_pre2 = _pre_bwd(dh2, x1, norm_mlp_pre, sc2 + mlp_x[4][0:1, 0:1], dx2, "pre_mlp_bwd")
    dmo, dg1, dw_post1 = _post_bwd(dx1, mo, norm_mix_post, g1, "post_mix_bwd")
    dmerged = _mm(dmo, w_o, S, Dm, Dm, mode="nt", out_dtype=F32, name="mix_out_dx")
    gw_o = _mm(merged, dmo, Dm, Dm, S, mode="tn", out_dtype=BF16, name="mix_out_dw")
    dys, dyg, dgate = _merge_bwd(dmerged, ys, yg, proj, "merge_bwd")
    dysn = _mm(dys, w_su, S, 2 * Dm, Dm, mode="nt", out_dtype=F32, name="ssm_up_dx")
    gw_su = _mm(ysn, dys, 2 * Dm, Dm, S, mode="tn", out_dtype=BF16, name="ssm_up_dw")
    dogn = _mm(dyg, w_gu, S, 2 * Dm, Dm, mode="nt", out_dtype=F32, name="gdn_up_dx")
    gw_gu = _mm(ogn, dyg, 2 * Dm, Dm, S, mode="tn", out_dtype=BF16, name="gdn_up_dw")
    mix_x = _exchange_start([gw_su.reshape(N_DEV, -1, Dm), gw_gu.reshape(N_DEV, -1, Dm), gw_o.reshape(N_DEV, -1, Dm)],
                            ["a2a"] * 3, gw_gu, "grads_mix_start")
    dxs, dBm, dCm, dz_s, ddt_g, dpv_ssm, dnw_ssm = _ssd_bwd(dysn, conv, proj, dt_g, pv_ssm + mix_x[4][0, 0], nw_ssm,
                                                            y_ssd, st_ssm, "ssd_bwd")
    dq, dk, dv, dz_g, db_g, da_g, dpv_gdn, dnw_gdn = _gdn_bwd(dogn, conv, proj, b_g, a_g, pv_gdn, gdn_norm_w, o_pre,
                                                              st_gdn, t_inv, "gdn_bwd")
    conv_pieces = []
    for nm, d_act, col0 in (("xs", dxs, 0), ("B", dBm, 2048), ("C", dCm, 3072), ("q", dq, 4096), ("k", dk, 5120),
                            ("v", dv, 6144)):
        conv_pieces.append(_conv_bwd(d_act, proj, conv_w, conv_b, col0, "conv_bwd_" + nm) + (col0,))
    d_small = jnp.concatenate([_heads_last(ddt_g), _heads_last(db_g), _heads_last(da_g),
                               jnp.zeros((S, N_SMALL - 64), F32)], axis=1).astype(BF16)
    pieces = [(dz_s, C_ZS)] + [(p[0], C_XBC + p[3]) for p in conv_pieces] + [(dz_g, C_ZG), (dgate, C_GS)]
    gw_small = _mm(h, d_small, Dm, N_SMALL, S, mode="tn", out_dtype=BF16, name="proj_small_dw")
    gw_cols = [_mm(h, d_piece, Dm, d_piece.shape[1], S, mode="tn", out_dtype=BF16, name="proj_dw_%d" % off)
               for d_piece, off in pieces]
    main_cols = jnp.concatenate(gw_cols, axis=1)
    gw_in = jnp.concatenate([main_cols[:, 0:C_QKV], gw_small[:, 0:32], main_cols[:, C_QKV:C_GS], gw_small[:, 32:64],
                             main_cols[:, C_GS:N_MAIN]], axis=1)
    in_x = _exchange_start([_to_col_shards(gw_in)], ["a2a"], gw_small, "grads_w_in_start")
    dh = _mm(d_small, w_small + in_x[4][0:1, 0:1].astype(BF16), S, Dm, N_SMALL, mode="nt", out_dtype=F32,
             name="proj_small_dx")
    for d_piece, off in pieces:
        dh = _mm(d_piece, w_main, S, Dm, d_piece.shape[1], mode="nt", out_dtype=F32, b_off=(0, off), add=dh,
                 name="proj_dx_%d" % off)
    dx, dsh1, dsc1, dw_pre1 = _pre_bwd(dh, x2, norm_mix_pre, sc1, dx1, "pre_mix_bwd")
    r_mu, r_md = _exchange_wait(mlp_x, ["a2a"] * 2, dx, "grads_mlp_wait")
    r_su, r_gu, r_o = _exchange_wait(mix_x, ["a2a"] * 3, dx, "grads_mix_wait")

    dconv_w = jnp.concatenate([p[1] for p in conv_pieces], axis=1)
    dconv_b = jnp.concatenate([p[2] for p in conv_pieces[:3]], axis=1)
    dmod = jnp.concatenate([dsh1, dsc1, dg1, dsh2, dsc2, dg2], axis=1)
    small_vec = jnp.concatenate(
        [dmod, dw_pre1, dw_post1, dconv_b, dpv_ssm[:, 0].reshape(1, 32), dpv_ssm[:, 1].reshape(1, 32),
         dpv_ssm[:, 2].reshape(1, 32), dnw_ssm.reshape(1, 2048), dpv_gdn[:, 0].reshape(1, 16),
         dpv_gdn[:, 1].reshape(1, 16), jnp.sum(dnw_gdn, axis=0), dw_pre2, dw_post2, dconv_w.reshape(1, -1)], axis=1)
    n_vec = small_vec.shape[1]
    small_vec = jnp.pad(small_vec, ((0, 0), (0, (-n_vec) % 1024))).reshape(-1, 1024)
    (small_all,) = _exchange([small_vec], ["gather"], "gather_small_grads")
    small_all = small_all.reshape(N_DEV, -1)
    dmod_cols = lax.dynamic_slice(small_all, (0, me * n_ada), (N_DEV, n_ada))
    gw_ada = _ada_bwd(c_all.T, dmod_cols, "ada_bwd")
    conv_all = small_all[:, N_REPLICATED:n_vec].reshape(N_DEV, CONV_K, 2 * N_DEV * 512)
    conv_contrib = jnp.concatenate(
        [lax.dynamic_slice(conv_all, (0, 0, me * 512), (N_DEV, CONV_K, 512)),
         lax.dynamic_slice(conv_all, (0, 0, N_DEV * 512 + me * 512), (N_DEV, CONV_K, 512))], axis=1)
    rep_contrib = small_all[:, :N_REPLICATED].reshape(N_DEV, N_REPLICATED // 128, 128)

    results = {}

    def adam_big(nm, contrib):
        w3 = given[nm]
        res = _adam(contrib, w3[0][0], w3[1][0], w3[2][0], "adam_" + nm)
        results[nm] = tuple(r.reshape(w3[0].shape) for r in res)

    adam_big("w_ada", gw_ada[None])
    adam_big("w_ssm_up", r_su)
    adam_big("w_gdn_up", r_gu)
    adam_big("w_out", r_o)
    adam_big("w_mlp_up", r_mu)
    adam_big("w_mlp_down", r_md)
    (r_in,) = _exchange_wait(in_x, ["a2a"], results["w_mlp_down"][0], "grads_w_in_wait")
    adam_big("w_in", r_in)
    packed = [jnp.concatenate([given[nm][i] for nm in REPLICATED], axis=1).reshape(N_REPLICATED // 128, 128)
              for i in range(3)]
    rep_res = _adam(rep_contrib, packed[0], packed[1], packed[2], "adam_replicated")
    pos = 0
    for nm in REPLICATED:
        size = given[nm][0].shape[1]
        results[nm] = tuple(r.reshape(1, N_REPLICATED)[:, pos:pos + size] for r in rep_res)
        pos += size
    conv_wmv = [jnp.concatenate([given["ssm_conv_w"][i][0], given["gdn_conv_w"][i][0]], axis=0) for i in range(3)]
    conv_res = _adam(conv_contrib, conv_wmv[0], conv_wmv[1], conv_wmv[2], "adam_conv_w")
    results["ssm_conv_w"] = tuple(r[None, :CONV_K] for r in conv_res)
    results["gdn_conv_w"] = tuple(r[None, CONV_K:] for r in conv_res)

    loss = lax.psum(loss_loc[0, 0], ("x", "y", "c"))
    return (loss, dx[None]) + tuple(results[nm][i] for i in range(4) for nm in WEIGHTS)
```

```python
import jax
import jax.numpy as jnp
from jax import lax
from jax.experimental import pallas as pl
from jax.experimental.pallas import tpu as pltpu

F32 = jnp.float32
BF16 = jnp.bfloat16
N_DEV = 8
D_MODEL = 1024
EPS = 1e-6
CONV_K = 4
SSM_CHUNK = 128
SSM_HEAD_DIM = 64
SSM_D_STATE = 128
SSM_GROUPS = 8
SSM_HEADS_PER_GROUP = 4
SSM_GROUP_WIDTH = SSM_HEADS_PER_GROUP * SSM_HEAD_DIM
SSM_GROUPS_PER_STEP = 2
GDN_CHUNK = 64
GDN_HEAD = 128
GDN_QK_HEADS = 8
GDN_V_PER_QK = 2
GDN_QK_PER_STEP = 4
GDN_INV_BLOCK = 16
C_ZS, C_XBC, C_QKV, C_ZG, C_GS, C_GG, N_MAIN = 0, 2048, 6144, 10240, 12288, 13312, 14336
N_SMALL = 128
ADAM_LR, ADAM_B1, ADAM_B2, ADAM_EPS, ADAM_WD, ADAM_STEP = 0.001, 0.9, 0.999, 1e-08, 0.01, 10
VMEM_LIMIT = 56 * 1024 * 1024
NEG_INF = float("-inf")

_NT = (((1,), (1,)), ((), ()))
_NN = (((1,), (0,)), ((), ()))
_TN = (((0,), (0,)), ((), ()))


def _params(*sem):
    return pltpu.CompilerParams(dimension_semantics=sem, vmem_limit_bytes=VMEM_LIMIT)


def _dot(a, b, dims=_NN):
    return lax.dot_general(a.astype(BF16), b.astype(BF16), dims, preferred_element_type=F32)


def _split(a):
    hi = a.astype(BF16)
    return hi, (a - hi.astype(F32)).astype(BF16)


def _dot3(a, b, dims=_NN):
    ah, al = _split(a)
    bh, bl = _split(b)
    d = lambda u, v: lax.dot_general(u, v, dims, preferred_element_type=F32)
    return d(ah, bh) + (d(ah, bl) + d(al, bh))


def _sigmoid(x):
    return 1.0 / (1.0 + jnp.exp(-x))


def _silu(x):
    return x * _sigmoid(x)


def _dsilu(x):
    s = _sigmoid(x)
    return s * (1.0 + x * (1.0 - s))


def _softplus(x):
    return jnp.maximum(x, 0.0) + jnp.log1p(jnp.exp(-jnp.abs(x)))


def _iota(n, m, d):
    return lax.broadcasted_iota(jnp.int32, (n, m), d)


def _rowsum(x):
    return jnp.sum(x, axis=1, keepdims=True)


def _colsum(x):
    return jnp.sum(x, axis=0, keepdims=True)


def _total(x):
    return _rowsum(_colsum(x))


MXU_LANES = 128


def _parts(x, n):
    out = []
    for _ in range(n):
        p = x.astype(BF16)
        out.append(p)
        x = x - p.astype(F32)
    return out


def _sum_by(m01, x, dims=_NN, n=3):
    return sum(lax.dot_general(m01, p, dims, preferred_element_type=F32) for p in _parts(x, n))


def _row_col_sums(q):
    ones = jnp.ones((q.shape[0], MXU_LANES), BF16)
    acc = 0.0
    for p in _parts(q, 2):
        acc = acc + (lax.dot_general(p, ones, _NN, preferred_element_type=F32)
                     - lax.dot_general(p, ones, _TN, preferred_element_type=F32))
    return acc[:, 0:1]


def _cumsum_forms(col, ii, jj):
    lower = jnp.where(ii >= jj, 1.0, 0.0).astype(BF16)
    cum_col = _sum_by(lower, jnp.broadcast_to(col, (col.shape[0], MXU_LANES)))[:, 0:1]
    cum_row = _colsum(jnp.where(ii <= jj, col, 0.0))
    return cum_col, cum_row


def _rev_cumsum_col(col, ii, jj):
    upper = jnp.where(ii <= jj, 1.0, 0.0).astype(BF16)
    return _sum_by(upper, jnp.broadcast_to(col, (col.shape[0], MXU_LANES)))[:, 0:1]


def _blk(dim, pref):
    return pref if dim % pref == 0 else dim


def _lockstep(gens):
    gens = list(gens)
    while gens:
        alive = []
        for g in gens:
            try:
                next(g)
                alive.append(g)
            except StopIteration:
                pass
        gens = alive


def _mm(a, b, M, N, K, *, mode, out_dtype, name, a_off=(0, 0), b_off=(0, 0), add=None, epi=None, extra=None,
        tm=1024, tn=1024, tk=1024):
    tm, tn, tk = _blk(M, tm), _blk(N, tn), _blk(K, tk)
    nk = K // tk
    if mode == "tn":
        a_spec = pl.BlockSpec((tk, tm), lambda i, j, k: (k + a_off[0] // tk, i + a_off[1] // tm))
        assert a_off[0] % tk == 0 and a_off[1] % tm == 0
    else:
        a_spec = pl.BlockSpec((tm, tk), lambda i, j, k: (i + a_off[0] // tm, k + a_off[1] // tk))
        assert a_off[0] % tm == 0 and a_off[1] % tk == 0
    if mode == "nt":
        b_spec = pl.BlockSpec((tn, tk), lambda i, j, k: (j + b_off[0] // tn, k + b_off[1] // tk))
        assert b_off[0] % tn == 0 and b_off[1] % tk == 0
    else:
        b_spec = pl.BlockSpec((tk, tn), lambda i, j, k: (k + b_off[0] // tk, j + b_off[1] // tn))
        assert b_off[0] % tk == 0 and b_off[1] % tn == 0
    dims = {"nn": _NN, "nt": _NT, "tn": _TN}[mode]
    o_spec = pl.BlockSpec((tm, tn), lambda i, j, k: (i, j))
    ins, in_specs = [a, b], [a_spec, b_spec]
    if add is not None:
        ins.append(add)
        in_specs.append(o_spec)
    if extra is not None:
        ins.append(extra)
        in_specs.append(o_spec)
    n_in = len(ins)
    if epi == "relu2":
        out_shape = (jax.ShapeDtypeStruct((M, N), F32), jax.ShapeDtypeStruct((M, N), BF16))
        out_specs = (o_spec, o_spec)
    else:
        out_shape = jax.ShapeDtypeStruct((M, N), out_dtype)
        out_specs = o_spec

    def body(*refs):
        a_ref, b_ref = refs[0], refs[1]
        acc = refs[-1]
        outs = refs[n_in:-1]
        k = pl.program_id(2)

        @pl.when(k == 0)
        def _():
            acc[...] = jnp.zeros_like(acc)

        acc[...] += _dot(a_ref[...], b_ref[...], dims)

        @pl.when(k == nk - 1)
        def _():
            r = acc[...]
            pos = 2
            if add is not None:
                r = r + refs[pos][...]
                pos += 1
            if epi == "relu2":
                outs[0][...] = r
                p = jnp.maximum(r, 0.0)
                outs[1][...] = (p * p).astype(BF16)
            elif epi == "drelu2":
                outs[0][...] = (r * (2.0 * jnp.maximum(refs[pos][...], 0.0))).astype(out_dtype)
            else:
                outs[0][...] = r.astype(out_dtype)

    return pl.pallas_call(
        body, name=name, grid=(M // tm, N // tn, nk), in_specs=in_specs, out_specs=out_specs, out_shape=out_shape,
        scratch_shapes=[pltpu.VMEM((tm, tn), F32)],
        compiler_params=_params("parallel", "parallel", "arbitrary"))(*ins)


def _row_spec(tb, d):
    return pl.BlockSpec((tb, d), lambda i: (i, 0))


def _vec_spec(d):
    return pl.BlockSpec((1, d), lambda i: (0, 0))


def _pre_fwd(x, w, sc, sh, name):
    S, Dm = x.shape
    tb = _blk(S, 512)

    def body(x_ref, w_ref, sc_ref, sh_ref, h_ref):
        xv = x_ref[...]
        r = lax.rsqrt(jnp.mean(xv * xv, axis=-1, keepdims=True) + EPS)
        h_ref[...] = ((xv * r * w_ref[...]) * (1.0 + sc_ref[...]) + sh_ref[...]).astype(BF16)

    return pl.pallas_call(
        body, name=name, grid=(S // tb,), in_specs=[_row_spec(tb, Dm)] + [_vec_spec(Dm)] * 3,
        out_specs=_row_spec(tb, Dm), out_shape=jax.ShapeDtypeStruct((S, Dm), BF16),
        compiler_params=_params("parallel"))(x, w, sc, sh)


def _post_fwd(x, y, w, g, name):
    S, Dm = x.shape
    tb = _blk(S, 512)

    def body(x_ref, y_ref, w_ref, g_ref, o_ref):
        yv = y_ref[...]
        r = lax.rsqrt(jnp.mean(yv * yv, axis=-1, keepdims=True) + EPS)
        o_ref[...] = x_ref[...] + g_ref[...] * (yv * r * w_ref[...])

    return pl.pallas_call(
        body, name=name, grid=(S // tb,), in_specs=[_row_spec(tb, Dm)] * 2 + [_vec_spec(Dm)] * 2,
        out_specs=_row_spec(tb, Dm), out_shape=jax.ShapeDtypeStruct((S, Dm), F32),
        compiler_params=_params("parallel"))(x, y, w, g)


def _final_fwd(x, y, w, g, target, name):
    S, Dm = x.shape
    tb = _blk(S, 512)
    nb = S // tb

    def body(x_ref, y_ref, w_ref, g_ref, t_ref, dx_ref, loss_ref, acc):
        i = pl.program_id(0)

        @pl.when(i == 0)
        def _():
            acc[...] = jnp.zeros_like(acc)

        yv = y_ref[...]
        r = lax.rsqrt(jnp.mean(yv * yv, axis=-1, keepdims=True) + EPS)
        e = (x_ref[...] + g_ref[...] * (yv * r * w_ref[...])) - t_ref[...]
        dx_ref[...] = e * (1.0 / Dm)
        acc[...] += _colsum(e * e)

        @pl.when(i == nb - 1)
        def _():
            loss_ref[...] = (0.5 / Dm) * _rowsum(acc[...])

    return pl.pallas_call(
        body, name=name, grid=(nb,), in_specs=[_row_spec(tb, Dm)] * 2 + [_vec_spec(Dm)] * 2 + [_row_spec(tb, Dm)],
        out_specs=(_row_spec(tb, Dm), pl.BlockSpec((1, 1), lambda i: (0, 0))),
        out_shape=(jax.ShapeDtypeStruct((S, Dm), F32), jax.ShapeDtypeStruct((1, 1), F32)),
        scratch_shapes=[pltpu.VMEM((1, Dm), F32)], compiler_params=_params("arbitrary"))(x, y, w, g, target)


def _post_bwd(dxo, y, w, g, name):
    S, Dm = y.shape
    tb = _blk(S, 512)

    def body(d_ref, y_ref, w_ref, g_ref, dy_ref, dg_ref, dw_ref):
        i = pl.program_id(0)

        @pl.when(i == 0)
        def _():
            dg_ref[...] = jnp.zeros_like(dg_ref)
            dw_ref[...] = jnp.zeros_like(dw_ref)

        yv, dv = y_ref[...], d_ref[...]
        r = lax.rsqrt(jnp.mean(yv * yv, axis=-1, keepdims=True) + EPS)
        yh = yv * r
        dg_ref[...] += _colsum(dv * (yh * w_ref[...]))
        dn = dv * g_ref[...]
        dw_ref[...] += _colsum(dn * yh)
        dyh = dn * w_ref[...]
        dy_ref[...] = (r * (dyh - yh * jnp.mean(dyh * yh, axis=-1, keepdims=True))).astype(BF16)

    return pl.pallas_call(
        body, name=name, grid=(S // tb,), in_specs=[_row_spec(tb, Dm)] * 2 + [_vec_spec(Dm)] * 2,
        out_specs=(_row_spec(tb, Dm), _vec_spec(Dm), _vec_spec(Dm)),
        out_shape=(jax.ShapeDtypeStruct((S, Dm), BF16), jax.ShapeDtypeStruct((1, Dm), F32),
                   jax.ShapeDtypeStruct((1, Dm), F32)),
        compiler_params=_params("arbitrary"))(dxo, y, w, g)


def _pre_bwd(dh, x, w, sc, dres, name):
    S, Dm = x.shape
    tb = _blk(S, 512)

    def body(dh_ref, x_ref, w_ref, sc_ref, dr_ref, dx_ref, dsh_ref, dsc_ref, dw_ref):
        i = pl.program_id(0)

        @pl.when(i == 0)
        def _():
            dsh_ref[...] = jnp.zeros_like(dsh_ref)
            dsc_ref[...] = jnp.zeros_like(dsc_ref)
            dw_ref[...] = jnp.zeros_like(dw_ref)

        xv, dv = x_ref[...], dh_ref[...]
        r = lax.rsqrt(jnp.mean(xv * xv, axis=-1, keepdims=True) + EPS)
        xh = xv * r
        one_sc = 1.0 + sc_ref[...]
        dsh_ref[...] += _colsum(dv)
        dsc_ref[...] += _colsum(dv * (xh * w_ref[...]))
        dw_ref[...] += _colsum(dv * one_sc * xh)
        dxh = dv * one_sc * w_ref[...]
        dx_ref[...] = dr_ref[...] + r * (dxh - xh * jnp.mean(dxh * xh, axis=-1, keepdims=True))

    vec = jax.ShapeDtypeStruct((1, Dm), F32)
    return pl.pallas_call(
        body, name=name, grid=(S // tb,),
        in_specs=[_row_spec(tb, Dm)] * 2 + [_vec_spec(Dm)] * 2 + [_row_spec(tb, Dm)],
        out_specs=(_row_spec(tb, Dm), _vec_spec(Dm), _vec_spec(Dm), _vec_spec(Dm)),
        out_shape=(jax.ShapeDtypeStruct((S, Dm), F32), vec, vec, vec),
        compiler_params=_params("arbitrary"))(dh, x, w, sc, dres)


def _merge_fwd(ys, yg, proj, name):
    S, Dm = ys.shape
    tb = _blk(S, 512)

    def body(ys_ref, yg_ref, gs_ref, gg_ref, o_ref):
        o_ref[...] = (_sigmoid(gs_ref[...]) * ys_ref[...] + _sigmoid(gg_ref[...]) * yg_ref[...]).astype(BF16)

    return pl.pallas_call(
        body, name=name, grid=(S // tb,),
        in_specs=[_row_spec(tb, Dm)] * 2 + [pl.BlockSpec((tb, Dm), lambda i: (i, C_GS // Dm)),
                                            pl.BlockSpec((tb, Dm), lambda i: (i, C_GG // Dm))],
        out_specs=_row_spec(tb, Dm), out_shape=jax.ShapeDtypeStruct((S, Dm), BF16),
        compiler_params=_params("parallel"))(ys, yg, proj, proj)


def _merge_bwd(dm, ys, yg, proj, name):
    S, Dm = ys.shape
    tb = _blk(S, 512)

    def body(dm_ref, ys_ref, yg_ref, gs_ref, gg_ref, dys_ref, dyg_ref, dgate_ref):
        d = dm_ref[...]
        ss, sg = _sigmoid(gs_ref[...]), _sigmoid(gg_ref[...])
        dys_ref[...] = (d * ss).astype(BF16)
        dyg_ref[...] = (d * sg).astype(BF16)
        dgate_ref[:, :Dm] = (d * ys_ref[...] * ss * (1.0 - ss)).astype(BF16)
        dgate_ref[:, Dm:] = (d * yg_ref[...] * sg * (1.0 - sg)).astype(BF16)

    return pl.pallas_call(
        body, name=name, grid=(S // tb,),
        in_specs=[_row_spec(tb, Dm)] * 3 + [pl.BlockSpec((tb, Dm), lambda i: (i, C_GS // Dm)),
                                            pl.BlockSpec((tb, Dm), lambda i: (i, C_GG // Dm))],
        out_specs=(_row_spec(tb, Dm), _row_spec(tb, Dm), _row_spec(tb, 2 * Dm)),
        out_shape=(jax.ShapeDtypeStruct((S, Dm), BF16), jax.ShapeDtypeStruct((S, Dm), BF16),
                   jax.ShapeDtypeStruct((S, 2 * Dm), BF16)),
        compiler_params=_params("parallel"))(dm, ys, yg, proj, proj)


CONV_COLS = 128


def _shift_down(x, k, rows):
    return jnp.where(rows >= k, pltpu.roll(x, k, 0), 0.0)


def _shift_up(x, k, rows, S):
    return jnp.where(rows < S - k, pltpu.roll(x, S - k, 0), 0.0)


def _conv_fwd(proj, w, b, name):
    S = proj.shape[0]
    n = w.shape[1]
    cb = CONV_COLS

    def body(x_ref, w_ref, b_ref, o_ref):
        x = x_ref[...]
        rows = _iota(S, cb, 0)
        pre = x * w_ref[CONV_K - 1:CONV_K, :] + b_ref[...]
        for k in range(1, CONV_K):
            pre = pre + _shift_down(x, k, rows) * w_ref[CONV_K - 1 - k:CONV_K - k, :]
        o_ref[...] = _silu(pre)

    return pl.pallas_call(
        body, name=name, grid=(n // cb,),
        in_specs=[pl.BlockSpec((S, cb), lambda j: (0, j + C_XBC // cb)), pl.BlockSpec((CONV_K, cb), lambda j: (0, j)),
                  pl.BlockSpec((1, cb), lambda j: (0, j))],
        out_specs=pl.BlockSpec((S, cb), lambda j: (0, j)), out_shape=jax.ShapeDtypeStruct((S, n), F32),
        compiler_params=_params("parallel"))(proj, w, b)


def _conv_bwd(dact, proj, w, b, col0, name):
    S, n = dact.shape
    cb = CONV_COLS
    o = col0 // cb

    def body(d_ref, x_ref, w_ref, b_ref, dx_ref, dw_ref, db_ref):
        x = x_ref[...]
        rows = _iota(S, cb, 0)
        xs = [x] + [_shift_down(x, k, rows) for k in range(1, CONV_K)]
        pre = xs[0] * w_ref[CONV_K - 1:CONV_K, :] + b_ref[...]
        for k in range(1, CONV_K):
            pre = pre + xs[k] * w_ref[CONV_K - 1 - k:CONV_K - k, :]
        dpre = d_ref[...] * _dsilu(pre)
        db_ref[...] = _colsum(dpre)
        dx = dpre * w_ref[CONV_K - 1:CONV_K, :]
        for k in range(CONV_K):
            dw_ref[CONV_K - 1 - k:CONV_K - k, :] = _colsum(dpre * xs[k])
            if k:
                dx = dx + _shift_up(dpre, k, rows, S) * w_ref[CONV_K - 1 - k:CONV_K - k, :]
        dx_ref[...] = dx.astype(BF16)

    return pl.pallas_call(
        body, name=name, grid=(n // cb,),
        in_specs=[pl.BlockSpec((S, cb), lambda j: (0, j)), pl.BlockSpec((S, cb), lambda j: (0, j + o + C_XBC // cb)),
                  pl.BlockSpec((CONV_K, cb), lambda j: (0, j + o)), pl.BlockSpec((1, cb), lambda j: (0, j + o))],
        out_specs=(pl.BlockSpec((S, cb), lambda j: (0, j)), pl.BlockSpec((CONV_K, cb), lambda j: (0, j)),
                   pl.BlockSpec((1, cb), lambda j: (0, j))),
        out_shape=(jax.ShapeDtypeStruct((S, n), BF16), jax.ShapeDtypeStruct((CONV_K, n), F32),
                   jax.ShapeDtypeStruct((1, n), F32)),
        compiler_params=_params("parallel"))(dact, proj, w, b)


def _ssd_specs(L, order):
    G = SSM_GROUPS_PER_STEP
    W, N = G * SSM_GROUP_WIDTH, G * SSM_D_STATE
    x_spec = pl.BlockSpec((L, W), lambda g, c: (order(c), g))
    b_spec = pl.BlockSpec((L, N), lambda g, c: (order(c), 2048 // N + g))
    c_spec = pl.BlockSpec((L, N), lambda g, c: (order(c), 3072 // N + g))
    z_spec = pl.BlockSpec((L, W), lambda g, c: (order(c), C_ZS // W + g))
    dt_spec = pl.BlockSpec((G, L, SSM_HEADS_PER_GROUP), lambda g, c: (g, order(c), 0))
    p_spec = pl.BlockSpec((G, 3, SSM_HEADS_PER_GROUP), lambda g, c: (g, 0, 0))
    nw_spec = pl.BlockSpec((G, 1, SSM_GROUP_WIDTH), lambda g, c: (g, 0, 0))
    s_spec = pl.BlockSpec((G, 1, SSM_GROUP_WIDTH, SSM_D_STATE), lambda g, c: (g, order(c), 0, 0))
    return x_spec, b_spec, c_spec, z_spec, dt_spec, p_spec, nw_spec, s_spec


def _ssd_fwd(conv, proj, dt_raw, pvec, nw, name):
    S = conv.shape[0]
    L, P, N, H, W, G = SSM_CHUNK, SSM_HEAD_DIM, SSM_D_STATE, SSM_HEADS_PER_GROUP, SSM_GROUP_WIDTH, SSM_GROUPS_PER_STEP
    nc = S // L

    def body(x_ref, b_ref, c_ref, z_ref, dt_ref, p_ref, nw_ref, y_ref, yn_ref, s0_ref, state):
        c = pl.program_id(1)

        @pl.when(c == 0)
        def _():
            state[...] = jnp.zeros_like(state)

        ii, jj = _iota(L, L, 0), _iota(L, L, 1)
        Bm = [b_ref[:, gi * N:(gi + 1) * N] for gi in range(G)]
        Cm = [c_ref[:, gi * N:(gi + 1) * N] for gi in range(G)]
        CB = [_dot(Cm[gi], Bm[gi], _NT) for gi in range(G)]
        y1 = {}
        for gi in range(G):
            s0_ref[gi, 0] = state[gi * W:(gi + 1) * W, :]

        def head(gi, h):
            sl = slice(gi * W + h * P, gi * W + (h + 1) * P)
            p = p_ref[gi]
            dt = _softplus(dt_ref[gi][:, h:h + 1] + p[0:1, h:h + 1])
            a = dt * (-jnp.exp(p[1:2, h:h + 1]))
            acum, acum_row = _cumsum_forms(a, ii, jj)
            xh = x_ref[:, sl]
            xdt = xh * dt
            yield
            decay = jnp.exp(jnp.where(ii >= jj, acum - acum_row, NEG_INF))
            S0 = state[sl, :]
            a_last = acum[L - 1:L, :]
            y_diag = _dot(CB[gi] * decay, xdt)
            y_off = _dot(Cm[gi], S0, _NT)
            s_new = _dot(xdt * jnp.exp(a_last - acum), Bm[gi], _TN)
            yield
            yh = y_diag + y_off * jnp.exp(acum)
            state[sl, :] = S0 * jnp.exp(a_last) + s_new
            y_ref[:, sl] = yh
            y1[gi, h] = yh + p[2:3, h:h + 1] * xh

        _lockstep(head(gi, h) for gi in range(G) for h in range(H))
        for gi in range(G):
            gsl = slice(gi * W, (gi + 1) * W)
            y2 = jnp.concatenate([y1[gi, h] for h in range(H)], axis=1) * _silu(z_ref[:, gsl])
            r = lax.rsqrt(jnp.mean(y2 * y2, axis=-1, keepdims=True) + EPS)
            yn_ref[:, gsl] = (y2 * r * nw_ref[gi]).astype(BF16)

    x_spec, b_spec, c_spec, z_spec, dt_spec, p_spec, nw_spec, s_spec = _ssd_specs(L, lambda c: c)
    y_spec = pl.BlockSpec((L, G * W), lambda g, c: (c, g))
    return pl.pallas_call(
        body, name=name, grid=(SSM_GROUPS // G, nc),
        in_specs=[x_spec, b_spec, c_spec, z_spec, dt_spec, p_spec, nw_spec],
        out_specs=(y_spec, y_spec, s_spec),
        out_shape=(jax.ShapeDtypeStruct((S, SSM_GROUPS * W), F32), jax.ShapeDtypeStruct((S, SSM_GROUPS * W), BF16),
                   jax.ShapeDtypeStruct((SSM_GROUPS, nc, W, N), F32)),
        scratch_shapes=[pltpu.VMEM((G * W, N), F32)],
        compiler_params=_params("parallel", "arbitrary"))(conv, conv, conv, proj, dt_raw, pvec, nw)


def _ssd_bwd(dyn, conv, proj, dt_raw, pvec, nw, y_ssd, states, name):
    S = conv.shape[0]
    L, P, N, H, W, G = SSM_CHUNK, SSM_HEAD_DIM, SSM_D_STATE, SSM_HEADS_PER_GROUP, SSM_GROUP_WIDTH, SSM_GROUPS_PER_STEP
    nc = S // L

    def body(dyn_ref, x_ref, b_ref, c_ref, z_ref, dt_ref, p_ref, nw_ref, y_ref, s0_ref,
             dx_ref, db_ref, dc_ref, dz_ref, ddt_ref, dp_ref, dnw_ref, dstate):
        c = pl.program_id(1)

        @pl.when(c == 0)
        def _():
            dstate[...] = jnp.zeros_like(dstate)
            dp_ref[...] = jnp.zeros_like(dp_ref)
            dnw_ref[...] = jnp.zeros_like(dnw_ref)

        ii, jj = _iota(L, L, 0), _iota(L, L, 1)
        last = (_iota(L, 1, 0) == L - 1)
        Bm = [b_ref[:, gi * N:(gi + 1) * N] for gi in range(G)]
        Cm = [c_ref[:, gi * N:(gi + 1) * N] for gi in range(G)]
        CB = [_dot(Cm[gi], Bm[gi], _NT) for gi in range(G)]
        dy1 = []
        for gi in range(G):
            gsl = slice(gi * W, (gi + 1) * W)
            p = p_ref[gi]
            x, z = x_ref[:, gsl], z_ref[:, gsl]
            Dfull = jnp.concatenate([jnp.broadcast_to(p[2:3, h:h + 1], (1, P)) for h in range(H)], axis=1)
            y1 = y_ref[:, gsl] + Dfull * x
            sz = _silu(z)
            y2 = y1 * sz
            r = lax.rsqrt(jnp.mean(y2 * y2, axis=-1, keepdims=True) + EPS)
            y2h = y2 * r
            dyn_v = dyn_ref[:, gsl]
            dnw_ref[gi] += _colsum(dyn_v * y2h)
            dy2h = dyn_v * nw_ref[gi]
            dy2 = r * (dy2h - y2h * jnp.mean(dy2h * y2h, axis=-1, keepdims=True))
            dz_ref[:, gsl] = (dy2 * y1 * _dsilu(z)).astype(BF16)
            dy1.append(dy2 * sz)
        res = {}

        def head(gi, h):
            sl = slice(gi * W + h * P, gi * W + (h + 1) * P)
            p = p_ref[gi]
            dtr = dt_ref[gi][:, h:h + 1] + p[0:1, h:h + 1]
            dth = _softplus(dtr)
            A = -jnp.exp(p[1:2, h:h + 1])
            acum, acum_row = _cumsum_forms(dth * A, ii, jj)
            xh = x_ref[:, sl]
            X = xh * dth
            yield
            decay = jnp.exp(jnp.where(ii >= jj, acum - acum_row, NEG_INF))
            eac = jnp.exp(acum)
            a_last = acum[L - 1:L, :]
            wdec = jnp.exp(a_last - acum)
            dY = dy1[gi][:, h * P:(h + 1) * P]
            S0, dS1 = s0_ref[gi, 0, h * P:(h + 1) * P, :], dstate[sl, :]
            M = CB[gi] * decay
            dYe = dY * eac
            dXs_raw = _dot(Bm[gi], dS1, _NT)
            dXm = _dot(M, dY, _TN)
            Gm = _dot(dY, X, _NT)
            y_off_raw = _dot(Cm[gi], S0, _NT)
            dC_s = _dot(dYe, S0)
            dB_s = _dot(X * wdec, dS1)
            dS_c = _dot(dYe, Cm[gi], _TN)
            yield
            dXs = dXs_raw * wdec
            dX = dXm + dXs
            Q = Gm * M
            dstate[sl, :] = jnp.exp(a_last) * dS1 + dS_c
            q_sums = _row_col_sums(Q)
            x_dxs = _rowsum(X * dXs)
            d_alast = _colsum(x_dxs) + jnp.exp(a_last) * _total(dS1 * S0)
            rest = _rowsum(dY * (y_off_raw * eac)) - x_dxs + jnp.where(last, d_alast, 0.0)
            dx_dt = _rowsum(dX * xh)
            yield
            da = _rev_cumsum_col(q_sums + rest, ii, jj)
            yield
            ddt_raw = (da * A + dx_dt) * _sigmoid(dtr)
            dx_ref[:, sl] = dX * dth + p[2:3, h:h + 1] * dY
            res[gi, h] = (Gm * decay, dC_s, dB_s, ddt_raw, _colsum(ddt_raw), _colsum(da * dth) * A, _total(dY * xh))

        _lockstep(head(gi, h) for gi in range(G) for h in range(H))
        for gi in range(G):
            parts = [res[gi, h] for h in range(H)]
            dCB = sum(pt[0] for pt in parts)
            dc_ref[:, gi * N:(gi + 1) * N] = sum(pt[1] for pt in parts) + _dot(dCB, Bm[gi])
            db_ref[:, gi * N:(gi + 1) * N] = sum(pt[2] for pt in parts) + _dot(dCB, Cm[gi], _TN)
            ddt_ref[gi] = jnp.concatenate([pt[3] for pt in parts], axis=1)
            dp_ref[gi] += jnp.concatenate([jnp.concatenate([pt[k] for pt in parts], axis=1) for k in (4, 5, 6)], axis=0)

    rev = lambda c: nc - 1 - c
    x_spec, b_spec, c_spec, z_spec, dt_spec, p_spec, nw_spec, s_spec = _ssd_specs(L, rev)
    y_spec = pl.BlockSpec((L, G * W), lambda g, c: (rev(c), g))
    n_spec = pl.BlockSpec((L, G * N), lambda g, c: (rev(c), g))
    return pl.pallas_call(
        body, name=name, grid=(SSM_GROUPS // G, nc),
        in_specs=[y_spec, x_spec, b_spec, c_spec, z_spec, dt_spec, p_spec, nw_spec, y_spec, s_spec],
        out_specs=(y_spec, n_spec, n_spec, y_spec, dt_spec, p_spec, nw_spec),
        out_shape=(jax.ShapeDtypeStruct((S, SSM_GROUPS * W), F32), jax.ShapeDtypeStruct((S, SSM_GROUPS * N), F32),
                   jax.ShapeDtypeStruct((S, SSM_GROUPS * N), F32), jax.ShapeDtypeStruct((S, SSM_GROUPS * W), BF16),
                   jax.ShapeDtypeStruct((SSM_GROUPS, S, H), F32), jax.ShapeDtypeStruct((SSM_GROUPS, 3, H), F32),
                   jax.ShapeDtypeStruct((SSM_GROUPS, 1, W), F32)),
        scratch_shapes=[pltpu.VMEM((G * W, N), F32)],
        compiler_params=_params("parallel", "arbitrary"))(dyn, conv, conv, conv, proj, dt_raw, pvec, nw, y_ssd, states)


def _unit_lower_inverse(A, ii, jj):
    eye = (ii == jj).astype(F32)
    same = (ii // GDN_INV_BLOCK) == (jj // GDN_INV_BLOCK)
    Ad = jnp.where(same, A, 0.0)
    Ao = A - Ad
    P2 = _dot3(Ad, Ad)
    yield
    P4, X = _dot3(P2, P2), _dot3(eye - Ad, eye + P2)
    yield
    P8, X = _dot3(P4, P4), _dot3(X, eye + P4)
    yield
    X = _dot3(X, eye + P8)
    yield
    Bm = _dot3(X, Ao)
    yield
    B2 = _dot3(Bm, Bm)
    yield
    Y = _dot3(eye - Bm, eye + B2)
    yield
    T = _dot3(Y, X)
    yield
    return T


def _gdn_specs(L, order):
    G = GDN_QK_PER_STEP
    Hd, W = G * GDN_HEAD, G * GDN_V_PER_QK * GDN_HEAD
    q_spec = pl.BlockSpec((L, Hd), lambda h, c: (order(c), (C_QKV - C_XBC) // Hd + h))
    k_spec = pl.BlockSpec((L, Hd), lambda h, c: (order(c), (C_QKV - C_XBC + 1024) // Hd + h))
    v_spec = pl.BlockSpec((L, W), lambda h, c: (order(c), (C_QKV - C_XBC + 2048) // W + h))
    z_spec = pl.BlockSpec((L, W), lambda h, c: (order(c), C_ZG // W + h))
    ba_spec = pl.BlockSpec((G, L, GDN_V_PER_QK), lambda h, c: (h, order(c), 0))
    p_spec = pl.BlockSpec((G, 2, GDN_V_PER_QK), lambda h, c: (h, 0, 0))
    nw_spec = pl.BlockSpec((1, GDN_HEAD), lambda h, c: (0, 0))
    s_spec = pl.BlockSpec((G, 1, GDN_V_PER_QK * GDN_HEAD, GDN_HEAD), lambda h, c: (h, order(c), 0, 0))
    t_spec = pl.BlockSpec((G * GDN_V_PER_QK, 1, L, L), lambda h, c: (h, order(c), 0, 0))
    return q_spec, k_spec, v_spec, z_spec, ba_spec, p_spec, nw_spec, s_spec, t_spec


def _gdn_gates(qa, ka, b_col, a_col, p, j, ii, jj):
    L = qa.shape[0]
    sp_in = a_col + p[0:1, j:j + 1]
    neg_ea = -jnp.exp(p[1:2, j:j + 1])
    g = neg_ea * _softplus(sp_in)
    gcum, gcum_row = _cumsum_forms(g, ii, jj)
    rq = lax.rsqrt(_rowsum(qa * qa) + EPS)
    rk = lax.rsqrt(_rowsum(ka * ka) + EPS)
    q = qa * rq * (GDN_HEAD ** -0.5)
    k = ka * rk
    beta = _sigmoid(b_col)
    yield
    Dm = jnp.exp(jnp.where(ii >= jj, gcum - gcum_row, NEG_INF))
    eg = jnp.exp(gcum)
    g_last = gcum[L - 1:L, :]
    wdec = jnp.exp(g_last - gcum)
    return dict(rq=rq, rk=rk, q=q, k=k, beta=beta, sp_in=sp_in, neg_ea=neg_ea, g=g, Dm=Dm, kbeta=k * beta, eg=eg,
                g_last=g_last, wdec=wdec, kdec=k * wdec)


def _gdn_fwd(conv, proj, b_raw, a_raw, pvec, nw, name):
    S = conv.shape[0]
    L, Hd, J, G = GDN_CHUNK, GDN_HEAD, GDN_V_PER_QK, GDN_QK_PER_STEP
    W = J * Hd
    nc = S // L

    def body(q_ref, k_ref, v_ref, z_ref, b_ref, a_ref, p_ref, nw_ref, o_ref, on_ref, s0_ref, t_ref, state):
        c = pl.program_id(1)

        @pl.when(c == 0)
        def _():
            state[...] = jnp.zeros_like(state)

        ii, jj = _iota(L, L, 0), _iota(L, L, 1)
        for hq in range(G):
            s0_ref[hq, 0] = state[hq * W:(hq + 1) * W, :]

        def head(hq, j):
            hd = hq * J + j
            hsl, sl = slice(hq * Hd, (hq + 1) * Hd), slice(hd * Hd, (hd + 1) * Hd)
            t = yield from _gdn_gates(q_ref[:, hsl], k_ref[:, hsl], b_ref[hq][:, j:j + 1], a_ref[hq][:, j:j + 1],
                                      p_ref[hq], j, ii, jj)
            KK = _dot(t["kbeta"], t["k"], _NT)
            QK = _dot(t["q"], t["k"], _NT)
            yield
            T = yield from _unit_lower_inverse(jnp.where(ii > jj, KK * t["Dm"], 0.0), ii, jj)
            t_ref[hd, 0] = T
            S0 = state[sl, :]
            U = _dot3(T, v_ref[:, sl] * t["beta"])
            Wm = _dot3(T, t["kbeta"] * t["eg"])
            o_inter = _dot(t["q"] * t["eg"], S0)
            yield
            Vn = U - _dot(Wm, S0)
            yield
            o = o_inter + _dot(QK * t["Dm"], Vn)
            s_new = _dot(t["kdec"], Vn, _TN)
            yield
            state[sl, :] = S0 * jnp.exp(t["g_last"]) + s_new
            o_ref[:, sl] = o
            r = lax.rsqrt(jnp.mean(o * o, axis=-1, keepdims=True) + EPS)
            on_ref[:, sl] = ((o * r * nw_ref[...]) * _silu(z_ref[:, sl])).astype(BF16)

        _lockstep(head(hq, j) for hq in range(G) for j in range(J))

    q_spec, k_spec, v_spec, z_spec, ba_spec, p_spec, nw_spec, s_spec, t_spec = _gdn_specs(L, lambda c: c)
    o_spec = pl.BlockSpec((L, G * W), lambda h, c: (c, h))
    return pl.pallas_call(
        body, name=name, grid=(GDN_QK_HEADS // G, nc),
        in_specs=[q_spec, k_spec, v_spec, z_spec, ba_spec, ba_spec, p_spec, nw_spec],
        out_specs=(o_spec, o_spec, s_spec, t_spec),
        out_shape=(jax.ShapeDtypeStruct((S, GDN_QK_HEADS * W), F32), jax.ShapeDtypeStruct((S, GDN_QK_HEADS * W), BF16),
                   jax.ShapeDtypeStruct((GDN_QK_HEADS, nc, W, Hd), F32),
                   jax.ShapeDtypeStruct((GDN_QK_HEADS * J, nc, L, L), F32)),
        scratch_shapes=[pltpu.VMEM((G * W, Hd), F32)],
        compiler_params=_params("parallel", "arbitrary"))(conv, conv, conv, proj, b_raw, a_raw, pvec, nw)


def _gdn_bwd(don, conv, proj, b_raw, a_raw, pvec, nw, o_pre, states, t_inv, name):
    S = conv.shape[0]
    L, Hd, J, G = GDN_CHUNK, GDN_HEAD, GDN_V_PER_QK, GDN_QK_PER_STEP
    W = J * Hd
    nc = S // L

    def body(don_ref, q_ref, k_ref, v_ref, z_ref, b_ref, a_ref, p_ref, nw_ref, o_ref, s0_ref, t_ref,
             dq_ref, dk_ref, dv_ref, dz_ref, db_ref, da_ref, dp_ref, dnw_ref, dstate):
        c = pl.program_id(1)

        @pl.when(c == 0)
        def _():
            dstate[...] = jnp.zeros_like(dstate)
            dp_ref[...] = jnp.zeros_like(dp_ref)
            dnw_ref[...] = jnp.zeros_like(dnw_ref)

        ii, jj = _iota(L, L, 0), _iota(L, L, 1)
        last = (_iota(L, 1, 0) == L - 1)
        res = {}

        def head(hq, j):
            hd = hq * J + j
            hsl, sl = slice(hq * Hd, (hq + 1) * Hd), slice(hd * Hd, (hd + 1) * Hd)
            qa, ka = q_ref[:, hsl], k_ref[:, hsl]
            t = yield from _gdn_gates(qa, ka, b_ref[hq][:, j:j + 1], a_ref[hq][:, j:j + 1], p_ref[hq], j, ii, jj)
            q, k, beta, eg, Dm, kbeta, kdec = (t[nm] for nm in ("q", "k", "beta", "eg", "Dm", "kbeta", "kdec"))
            T = t_ref[hd, 0]
            v, z, o = v_ref[:, sl], z_ref[:, sl], o_ref[:, sl]
            S0, dS1 = s0_ref[hq, 0, j * Hd:(j + 1) * Hd, :], dstate[sl, :]
            sz = _silu(z)
            r = lax.rsqrt(jnp.mean(o * o, axis=-1, keepdims=True) + EPS)
            oh = o * r
            d_on = don_ref[:, sl]
            dz_ref[:, sl] = (d_on * (oh * nw_ref[...]) * _dsilu(z)).astype(BF16)
            dn = d_on * sz
            dnw_part = _colsum(dn * oh)
            doh = dn * nw_ref[...]
            dO = r * (doh - oh * jnp.mean(doh * oh, axis=-1, keepdims=True))
            Rw = kbeta * eg
            qe = q * eg
            U = _dot3(T, v * beta)
            Wm = _dot3(T, Rw)
            KK = _dot(kbeta, k, _NT)
            QK = _dot(q, k, _NT)
            o_inter = _dot(qe, S0)
            dq_s = _dot(dO, S0, _NT)
            dS_q = _dot(qe, dO, _TN)
            yield
            Am = jnp.where(ii > jj, KK * Dm, 0.0)
            Pm = QK * Dm
            Vn = U - _dot(Wm, S0)
            dVn_s = _dot(kdec, dS1)
            yield
            dVn = _dot(Pm, dO, _TN) + dVn_s
            dP = _dot(dO, Vn, _NT)
            dKd = _dot(Vn, dS1, _NT)
            yield
            dQK = dP * Dm
            dq = _dot(dQK, k) + dq_s * eg
            dk = _dot(dQK, q, _TN) + dKd * t["wdec"]
            dstate[sl, :] = jnp.exp(t["g_last"]) * dS1 + dS_q - _dot(Wm, dVn, _TN)
            dW = -_dot(dVn, S0, _NT)
            dRu = _dot3(T, dVn, _TN)
            yield
            dRw = _dot3(T, dW, _TN)
            dA_u = _dot(dRu, U, _NT)
            yield
            dA = jnp.where(ii > jj, -(dA_u + _dot(dRw, Wm, _NT)), 0.0)
            yield
            dKK = dA * Dm
            dkbeta = _dot(dKK, k) + dRw * eg
            dk = dk + _dot(dKK, kbeta, _TN)
            yield
            dk = dk + dkbeta * beta
            dbeta = _rowsum(dkbeta * k) + _rowsum(dRu * v)
            dv_ref[:, sl] = dRu * beta
            Q = dA * Am + dP * Pm
            rho = _rowsum(dKd * kdec)
            d_glast = _colsum(rho) + jnp.exp(t["g_last"]) * _total(dS1 * S0)
            q_sums = _row_col_sums(Q)
            rest = _rowsum(dRw * Rw) + _rowsum(dO * o_inter) - rho + jnp.where(last, d_glast, 0.0)
            yield
            dg = _rev_cumsum_col(q_sums + rest, ii, jj)
            yield
            da_raw = dg * t["neg_ea"] * _sigmoid(t["sp_in"])
            res[hq, j] = dict(dq=dq, dk=dk, db=dbeta * beta * (1.0 - beta), da=da_raw, d_bias=_colsum(da_raw),
                              d_alog=_colsum(dg * t["g"]), dnw=dnw_part, rq=t["rq"], rk=t["rk"], k=k, qh=qa * t["rq"])

        _lockstep(head(hq, j) for hq in range(G) for j in range(J))
        for hq in range(G):
            parts = [res[hq, j] for j in range(J)]
            hsl = slice(hq * Hd, (hq + 1) * Hd)
            p0 = parts[0]
            dqh = sum(pt["dq"] for pt in parts) * (GDN_HEAD ** -0.5)
            dkn = sum(pt["dk"] for pt in parts)
            dq_ref[:, hsl] = p0["rq"] * (dqh - p0["qh"] * _rowsum(dqh * p0["qh"]))
            dk_ref[:, hsl] = p0["rk"] * (dkn - p0["k"] * _rowsum(dkn * p0["k"]))
            db_ref[hq] = jnp.concatenate([pt["db"] for pt in parts], axis=1)
            da_ref[hq] = jnp.concatenate([pt["da"] for pt in parts], axis=1)
            dp_ref[hq] += jnp.concatenate([jnp.concatenate([pt["d_bias"] for pt in parts], axis=1),
                                           jnp.concatenate([pt["d_alog"] for pt in parts], axis=1)], axis=0)
            dnw_ref[hq] += sum(pt["dnw"] for pt in parts)

    rev = lambda c: nc - 1 - c
    q_spec, k_spec, v_spec, z_spec, ba_spec, p_spec, nw_spec, s_spec, t_spec = _gdn_specs(L, rev)
    o_spec = pl.BlockSpec((L, G * W), lambda h, c: (rev(c), h))
    h_spec = pl.BlockSpec((L, G * Hd), lambda h, c: (rev(c), h))
    dnw_spec = pl.BlockSpec((G, 1, Hd), lambda h, c: (h, 0, 0))
    return pl.pallas_call(
        body, name=name, grid=(GDN_QK_HEADS // G, nc),
        in_specs=[o_spec, q_spec, k_spec, v_spec, z_spec, ba_spec, ba_spec, p_spec, nw_spec, o_spec, s_spec, t_spec],
        out_specs=(h_spec, h_spec, o_spec, o_spec, ba_spec, ba_spec, p_spec, dnw_spec),
        out_shape=(jax.ShapeDtypeStruct((S, GDN_QK_HEADS * Hd), F32), jax.ShapeDtypeStruct((S, GDN_QK_HEADS * Hd), F32),
                   jax.ShapeDtypeStruct((S, GDN_QK_HEADS * W), F32), jax.ShapeDtypeStruct((S, GDN_QK_HEADS * W), BF16),
                   jax.ShapeDtypeStruct((GDN_QK_HEADS, S, J), F32), jax.ShapeDtypeStruct((GDN_QK_HEADS, S, J), F32),
                   jax.ShapeDtypeStruct((GDN_QK_HEADS, 2, J), F32), jax.ShapeDtypeStruct((GDN_QK_HEADS, 1, Hd), F32)),
        scratch_shapes=[pltpu.VMEM((G * W, Hd), F32)],
        compiler_params=_params("parallel", "arbitrary"))(don, conv, conv, conv, proj, b_raw, a_raw, pvec, nw, o_pre,
                                                          states, t_inv)


def _ada_fwd(c_all, w_loc, b_loc, name):
    n = w_loc.shape[1]

    def body(c_ref, w_ref, b_ref, o_ref):
        o_ref[...] = _dot3(_silu(c_ref[...]), w_ref[...]) + b_ref[...]

    return pl.pallas_call(body, name=name, out_shape=jax.ShapeDtypeStruct((N_DEV, n), F32),
                          compiler_params=pltpu.CompilerParams(vmem_limit_bytes=VMEM_LIMIT))(c_all, w_loc, b_loc)


def _ada_bwd(c_all_t, dmod_cols, name):
    Dm, n = c_all_t.shape[0], dmod_cols.shape[1]

    def body(c_ref, d_ref, o_ref):
        ca = _silu(c_ref[...])
        acc = ca[:, 0:1] * d_ref[0:1, :]
        for i in range(1, N_DEV):
            acc = acc + ca[:, i:i + 1] * d_ref[i:i + 1, :]
        o_ref[...] = acc

    return pl.pallas_call(body, name=name, out_shape=jax.ShapeDtypeStruct((Dm, n), F32),
                          compiler_params=pltpu.CompilerParams(vmem_limit_bytes=VMEM_LIMIT))(c_all_t, dmod_cols)


ADAM_BLOCK_BYTES = 12 * 1024 * 1024


def _adam(contrib, w, m, v, name):
    n, R, C = contrib.shape
    tr = R
    while tr % 16 == 0 and (n + 7) * tr * C * 4 > ADAM_BLOCK_BYTES:
        tr //= 2

    def body(c_ref, w_ref, m_ref, v_ref, g_ref, d_ref, nm_ref, nv_ref):
        g = c_ref[0].astype(F32)
        for i in range(1, n):
            g = g + c_ref[i].astype(F32)
        nm = ADAM_B1 * m_ref[...] + (1.0 - ADAM_B1) * g
        nv = ADAM_B2 * v_ref[...] + (1.0 - ADAM_B2) * (g * g)
        m_hat = nm / (1.0 - ADAM_B1 ** ADAM_STEP)
        v_hat = nv / (1.0 - ADAM_B2 ** ADAM_STEP)
        g_ref[...] = g
        d_ref[...] = -ADAM_LR * (m_hat / (jnp.sqrt(v_hat) + ADAM_EPS) + ADAM_WD * w_ref[...])
        nm_ref[...] = nm
        nv_ref[...] = nv

    spec = pl.BlockSpec((tr, C), lambda i: (i, 0))
    shp = jax.ShapeDtypeStruct((R, C), F32)
    return pl.pallas_call(
        body, name=name, grid=(R // tr,), in_specs=[pl.BlockSpec((n, tr, C), lambda i: (0, i, 0)), spec, spec, spec],
        out_specs=(spec,) * 4, out_shape=(shp,) * 4, compiler_params=_params("parallel"))(contrib, w, m, v)


def _exchange(arrays, modes, name):
    n = len(arrays)
    out_shape = tuple(jax.ShapeDtypeStruct((N_DEV,) + a.shape if md == "gather" else a.shape, a.dtype)
                      for a, md in zip(arrays, modes))

    def body(*refs):
        ins, outs = refs[:n], refs[n:2 * n]
        send_sems, recv_sems, loc_sems = refs[2 * n:]
        ix, iy, ic = lax.axis_index("x"), lax.axis_index("y"), lax.axis_index("c")
        me = 4 * ix + 2 * iy + ic
        peers = []
        for m in range(1, N_DEV):
            px = 1 - ix if m & 4 else ix
            py = 1 - iy if m & 2 else iy
            pc = 1 - ic if m & 1 else ic
            peers.append(((px, py, pc), 4 * px + 2 * py + pc))

        def src(k, slot):
            return ins[k] if modes[k] == "gather" else ins[k].at[slot]

        def remote(k, m, to_slot, land_slot):
            return pltpu.make_async_remote_copy(
                src_ref=src(k, to_slot), dst_ref=outs[k].at[land_slot], send_sem=send_sems.at[k, m],
                recv_sem=recv_sems.at[k, m], device_id=peers[m][0], device_id_type=pl.DeviceIdType.MESH)

        local = [pltpu.make_async_copy(src(k, me), outs[k].at[me], loc_sems.at[k]) for k in range(n)]
        for cp in local:
            cp.start()
        sends = [remote(k, m, peers[m][1], me) for m in range(N_DEV - 1) for k in range(n)]
        for cp in sends:
            cp.start()
        for m in range(N_DEV - 1):
            for k in range(n):
                remote(k, m, peers[m][1], peers[m][1]).wait_recv()
        for cp in sends:
            cp.wait_send()
        for cp in local:
            cp.wait()

    any_spec = pl.BlockSpec(memory_space=pl.ANY)
    return pl.pallas_call(
        body, name=name, in_specs=[any_spec] * n, out_specs=(any_spec,) * n, out_shape=out_shape,
        scratch_shapes=[pltpu.SemaphoreType.DMA((n, N_DEV - 1)), pltpu.SemaphoreType.DMA((n, N_DEV - 1)),
                        pltpu.SemaphoreType.DMA((n,))])(*arrays)


def _peer_table():
    ix, iy, ic = lax.axis_index("x"), lax.axis_index("y"), lax.axis_index("c")
    peers = []
    for m in range(1, N_DEV):
        px = 1 - ix if m & 4 else ix
        py = 1 - iy if m & 2 else iy
        pc = 1 - ic if m & 1 else ic
        peers.append(((px, py, pc), 4 * px + 2 * py + pc))
    return 4 * ix + 2 * iy + ic, peers


def _exchange_start(arrays, modes, after, name):
    n = len(arrays)
    land_shapes = [(N_DEV,) + a.shape if md == "gather" else a.shape for a, md in zip(arrays, modes)]

    def body(*refs):
        ins, lands = refs[:n], refs[n:2 * n]
        send_sems, recv_sems = refs[2 * n + 1], refs[2 * n + 2]
        token, loc_sems = refs[-2], refs[-1]
        me, peers = _peer_table()

        def src(k, slot):
            return ins[k] if modes[k] == "gather" else ins[k].at[slot]

        local = [pltpu.make_async_copy(src(k, me), lands[k].at[me], loc_sems.at[k]) for k in range(n)]
        for cp in local:
            cp.start()
        for cp in local:
            cp.wait()
        for m in range(N_DEV - 1):
            for k in range(n):
                pltpu.make_async_remote_copy(
                    src_ref=src(k, peers[m][1]), dst_ref=lands[k].at[me], send_sem=send_sems, recv_sem=recv_sems,
                    device_id=peers[m][0], device_id_type=pl.DeviceIdType.MESH).start()
        token[...] = jnp.zeros_like(token)

    hbm = pl.BlockSpec(memory_space=pltpu.HBM)
    sem = pl.BlockSpec(memory_space=pltpu.SEMAPHORE)
    sem_shape = pltpu.SemaphoreType.DMA(())
    operands = [pltpu.with_memory_space_constraint(a, pltpu.HBM) for a in arrays]
    operands += [pltpu.with_memory_space_constraint(lax.empty(s, a.dtype), pltpu.HBM)
                 for s, a in zip(land_shapes, arrays)]
    out = pl.pallas_call(
        body, name=name,
        out_shape=(sem_shape, sem_shape) + tuple(pltpu.HBM(a.shape, a.dtype) for a in arrays)
        + tuple(pltpu.HBM(s, a.dtype) for s, a in zip(land_shapes, arrays)) + (jax.ShapeDtypeStruct((8, 128), F32),),
        in_specs=[hbm] * (2 * n) + [pl.BlockSpec(memory_space=pl.ANY)],
        out_specs=(sem, sem) + (hbm,) * (2 * n) + (pl.BlockSpec(memory_space=pltpu.VMEM),),
        input_output_aliases={i: 2 + i for i in range(2 * n)},
        scratch_shapes=[pltpu.SemaphoreType.DMA((n,))],
        compiler_params=pltpu.CompilerParams(has_side_effects=pltpu.SideEffectType.DATAFLOW_SIDE_EFFECTING))(
            *operands, after)
    return out[0], out[1], out[2:2 + n], out[2 + n:2 + 2 * n], out[-1]


def _exchange_wait(started, modes, after, name):
    send_sems, recv_sems, sent, lands, _ = started
    n = len(sent)

    def body(*refs):
        ins, zones = refs[:n], refs[n:2 * n]
        send_ref, recv_ref = refs[2 * n], refs[2 * n + 1]
        me, peers = _peer_table()

        def src(k, slot):
            return ins[k] if modes[k] == "gather" else ins[k].at[slot]

        for m in range(N_DEV - 1):
            for k in range(n):
                cp = pltpu.make_async_remote_copy(
                    src_ref=src(k, peers[m][1]), dst_ref=zones[k].at[peers[m][1]], send_sem=send_ref,
                    recv_sem=recv_ref, device_id=peers[m][0], device_id_type=pl.DeviceIdType.MESH)
                cp.wait_send()
                cp.wait_recv()

    hbm = pl.BlockSpec(memory_space=pltpu.HBM)
    sem = pl.BlockSpec(memory_space=pltpu.SEMAPHORE)
    out = pl.pallas_call(
        body, name=name,
        out_shape=tuple(pltpu.HBM(a.shape, a.dtype) for a in sent) + tuple(pltpu.HBM(a.shape, a.dtype) for a in lands),
        in_specs=[hbm] * (2 * n) + [sem, sem, pl.BlockSpec(memory_space=pl.ANY)], out_specs=(hbm,) * (2 * n),
        input_output_aliases={i: i for i in range(2 * n)},
        compiler_params=pltpu.CompilerParams(has_side_effects=pltpu.SideEffectType.DATAFLOW_SIDE_EFFECTING))(
            *sent, *lands, send_sems, recv_sems, after)
    return out[n:]


W_IN_SPLITS = (0, 2048, 6144, 6176, 10272, 12320, 12336, 12352, 13376, 14400)
N_REPLICATED = 16640
REPLICATED = ("b_ada", "norm_mix_pre", "norm_mix_post", "ssm_conv_b", "ssm_dt_bias", "ssm_A_log", "ssm_D",
              "ssm_norm_w", "gdn_dt_bias", "gdn_A_log", "gdn_norm_w", "norm_mlp_pre", "norm_mlp_post")
WEIGHTS = ("w_ada", "b_ada", "norm_mix_pre", "norm_mix_post", "w_in", "ssm_conv_w", "ssm_conv_b", "ssm_dt_bias",
           "ssm_A_log", "ssm_D", "ssm_norm_w", "gdn_conv_w", "gdn_dt_bias", "gdn_A_log", "gdn_norm_w", "w_ssm_up",
           "w_gdn_up", "w_out", "norm_mlp_pre", "norm_mlp_post", "w_mlp_up", "w_mlp_down")


def _by_cols(t):
    return t.transpose(1, 0, 2).reshape(t.shape[1], N_DEV * t.shape[2])


def _to_col_shards(t):
    R, C8 = t.shape
    return t.reshape(R, N_DEV, C8 // N_DEV).transpose(1, 0, 2)


def _heads_first(t, groups):
    S = t.shape[0]
    return t.reshape(S, groups, t.shape[1] // groups).transpose(1, 0, 2)


def _heads_last(t):
    return t.transpose(1, 0, 2).reshape(t.shape[1], t.shape[0] * t.shape[2])


def kernel(x, c, w_ada, b_ada, norm_mix_pre, norm_mix_post, w_in, ssm_conv_w, ssm_conv_b, ssm_dt_bias, ssm_A_log, ssm_D, ssm_norm_w, gdn_conv_w, gdn_dt_bias, gdn_A_log, gdn_norm_w, w_ssm_up, w_gdn_up, w_out, norm_mlp_pre, norm_mlp_post, w_mlp_up, w_mlp_down, loss_target, m_w_ada, m_b_ada, m_norm_mix_pre, m_norm_mix_post, m_w_in, m_ssm_conv_w, m_ssm_conv_b, m_ssm_dt_bias, m_ssm_A_log, m_ssm_D, m_ssm_norm_w, m_gdn_conv_w, m_gdn_dt_bias, m_gdn_A_log, m_gdn_norm_w, m_w_ssm_up, m_w_gdn_up, m_w_out, m_norm_mlp_pre, m_norm_mlp_post, m_w_mlp_up, m_w_mlp_down, v_w_ada, v_b_ada, v_norm_mix_pre, v_norm_mix_post, v_w_in, v_ssm_conv_w, v_ssm_conv_b, v_ssm_dt_bias, v_ssm_A_log, v_ssm_D, v_ssm_norm_w, v_gdn_conv_w, v_gdn_dt_bias, v_gdn_A_log, v_gdn_norm_w, v_w_ssm_up, v_w_gdn_up, v_w_out, v_norm_mlp_pre, v_norm_mlp_post, v_w_mlp_up, v_w_mlp_down):
    S, Dm = x.shape[1], D_MODEL
    me = 4 * lax.axis_index("x") + 2 * lax.axis_index("y") + lax.axis_index("c")
    x2, tgt = x[0], loss_target[0]
    n_ada = w_ada.shape[2]
    given = dict(
        w_ada=(w_ada, m_w_ada, v_w_ada), b_ada=(b_ada, m_b_ada, v_b_ada),
        norm_mix_pre=(norm_mix_pre, m_norm_mix_pre, v_norm_mix_pre),
        norm_mix_post=(norm_mix_post, m_norm_mix_post, v_norm_mix_post), w_in=(w_in, m_w_in, v_w_in),
        ssm_conv_w=(ssm_conv_w, m_ssm_conv_w, v_ssm_conv_w), ssm_conv_b=(ssm_conv_b, m_ssm_conv_b, v_ssm_conv_b),
        ssm_dt_bias=(ssm_dt_bias, m_ssm_dt_bias, v_ssm_dt_bias), ssm_A_log=(ssm_A_log, m_ssm_A_log, v_ssm_A_log),
        ssm_D=(ssm_D, m_ssm_D, v_ssm_D), ssm_norm_w=(ssm_norm_w, m_ssm_norm_w, v_ssm_norm_w),
        gdn_conv_w=(gdn_conv_w, m_gdn_conv_w, v_gdn_conv_w), gdn_dt_bias=(gdn_dt_bias, m_gdn_dt_bias, v_gdn_dt_bias),
        gdn_A_log=(gdn_A_log, m_gdn_A_log, v_gdn_A_log), gdn_norm_w=(gdn_norm_w, m_gdn_norm_w, v_gdn_norm_w),
        w_ssm_up=(w_ssm_up, m_w_ssm_up, v_w_ssm_up), w_gdn_up=(w_gdn_up, m_w_gdn_up, v_w_gdn_up),
        w_out=(w_out, m_w_out, v_w_out), norm_mlp_pre=(norm_mlp_pre, m_norm_mlp_pre, v_norm_mlp_pre),
        norm_mlp_post=(norm_mlp_post, m_norm_mlp_post, v_norm_mlp_post), w_mlp_up=(w_mlp_up, m_w_mlp_up, v_w_mlp_up),
        w_mlp_down=(w_mlp_down, m_w_mlp_down, v_w_mlp_down))

    (c_all, scw, gcw, g_in) = _exchange([c, ssm_conv_w[0], gdn_conv_w[0], w_in[0].astype(BF16)], ["gather"] * 4,
                                        "gather_w_in")
    c_all = c_all.reshape(N_DEV, Dm)
    wf = _by_cols(g_in)
    sp = W_IN_SPLITS
    w_main = jnp.concatenate([wf[:, sp[0]:sp[2]], wf[:, sp[3]:sp[5]], wf[:, sp[7]:sp[9]]], axis=1)
    w_small = jnp.concatenate([wf[:, sp[2]:sp[3]], wf[:, sp[5]:sp[7]], jnp.zeros((Dm, N_SMALL - 64), BF16)], axis=1)
    conv_w = jnp.concatenate([_by_cols(scw), _by_cols(gcw)], axis=1)
    conv_b = jnp.concatenate([ssm_conv_b, jnp.zeros_like(ssm_conv_b)], axis=1)

    b_loc = lax.dynamic_slice(b_ada, (0, me * n_ada), (1, n_ada))
    mod_part = _ada_fwd(c_all, w_ada[0], b_loc, "ada_fwd")
    (mod_rows,) = _exchange([mod_part.reshape(N_DEV, 1, n_ada)], ["a2a"], "exchange_mod")
    rest = _exchange_start([w_ssm_up[0].astype(BF16), w_gdn_up[0].astype(BF16), w_out[0].astype(BF16),
                            w_mlp_up[0].astype(BF16), w_mlp_down[0].astype(BF16)], ["gather"] * 5, mod_rows,
                           "gather_rest_start")
    mod = mod_rows.reshape(1, 6 * Dm) + rest[4][0:1, 0:1]
    sh1, sc1, g1, sh2, sc2, g2 = [mod[:, i * Dm:(i + 1) * Dm] for i in range(6)]

    h = _pre_fwd(x2, norm_mix_pre, sc1, sh1, "pre_mix")
    proj = _mm(h, w_main, S, N_MAIN, Dm, mode="nn", out_dtype=F32, name="proj_main")
    small = _mm(h, w_small, S, N_SMALL, Dm, mode="nn", out_dtype=F32, name="proj_small")
    conv = _conv_fwd(proj, conv_w, conv_b, "conv_fwd")
    dt_g, b_g, a_g = _heads_first(small[:, 0:32], 8), _heads_first(small[:, 32:48], 8), _heads_first(small[:, 48:64], 8)
    pv_ssm = jnp.stack([ssm_dt_bias.reshape(8, 4), ssm_A_log.reshape(8, 4), ssm_D.reshape(8, 4)], axis=1)
    nw_ssm = ssm_norm_w.reshape(8, 1, SSM_GROUP_WIDTH)
    pv_gdn = jnp.stack([gdn_dt_bias.reshape(8, 2), gdn_A_log.reshape(8, 2)], axis=1)
    y_ssd, ysn, st_ssm = _ssd_fwd(conv, proj, dt_g, pv_ssm, nw_ssm, "ssd_fwd")
    o_pre, ogn, st_gdn, t_inv = _gdn_fwd(conv, proj, b_g, a_g, pv_gdn, gdn_norm_w, "gdn_fwd")
    g_su, g_gu, g_out, g_mu, g_md = _exchange_wait(rest, ["gather"] * 5, ogn, "gather_rest_wait")
    w_su, w_gu = g_su.reshape(2 * Dm, Dm), g_gu.reshape(2 * Dm, Dm)
    w_o, w_mu, w_md = g_out.reshape(Dm, Dm), _by_cols(g_mu), g_md.reshape(4 * Dm, Dm)
    ys = _mm(ysn, w_su, S, Dm, 2 * Dm, mode="nn", out_dtype=F32, name="ssm_up")
    yg = _mm(ogn, w_gu, S, Dm, 2 * Dm, mode="nn", out_dtype=F32, name="gdn_up")
    merged = _merge_fwd(ys, yg, proj, "merge_fwd")
    mo = _mm(merged, w_o, S, Dm, Dm, mode="nn", out_dtype=F32, name="mix_out")
    x1 = _post_fwd(x2, mo, norm_mix_post, g1, "post_mix")
    h2 = _pre_fwd(x1, norm_mlp_pre, sc2, sh2, "pre_mlp")
    u, act = _mm(h2, w_mu, S, 4 * Dm, Dm, mode="nn", out_dtype=F32, epi="relu2", name="mlp_up")
    y_mlp = _mm(act, w_md, S, Dm, 4 * Dm, mode="nn", out_dtype=F32, name="mlp_down")
    dx2, loss_loc = _final_fwd(x1, y_mlp, norm_mlp_post, g2, tgt, "post_mlp_loss")

    dy, dg2, dw_post2 = _post_bwd(dx2, y_mlp, norm_mlp_post, g2, "post_mlp_bwd")
    du = _mm(dy, w_md, S, 4 * Dm, Dm, mode="nt", out_dtype=BF16, epi="drelu2", extra=u, name="mlp_down_dx")
    gw_md = _mm(act, dy, 4 * Dm, Dm, S, mode="tn", out_dtype=BF16, name="mlp_down_dw")
    dh2 = _mm(du, w_mu, S, Dm, 4 * Dm, mode="nt", out_dtype=F32, name="mlp_up_dx")
    gw_mu = _mm(h2, du, Dm, 4 * Dm, S, mode="tn", out_dtype=BF16, name="mlp_up_dw")
    mlp_x = _exchange_start([_to_col_shards(gw_mu), gw_md.reshape(N_DEV, -1, Dm)], ["a2a"] * 2, gw_md,
                            "grads_mlp_start")
    dx1, dsh2, dsc2, dw_pre2 = _pre_bwd(dh2, x1, norm_mlp_pre, sc2 + mlp_x[4][0:1, 0:1], dx2, "pre_mlp_bwd")
    dmo, dg1, dw_post1 = _post_bwd(dx1, mo, norm_mix_post, g1, "post_mix_bwd")
    dmerged = _mm(dmo, w_o, S, Dm, Dm, mode="nt", out_dtype=F32, name="mix_out_dx")
    gw_o = _mm(merged, dmo, Dm, Dm, S, mode="tn", out_dtype=BF16, name="mix_out_dw")
    dys, dyg, dgate = _merge_bwd(dmerged, ys, yg, proj, "merge_bwd")
    dysn = _mm(dys, w_su, S, 2 * Dm, Dm, mode="nt", out_dtype=F32, name="ssm_up_dx")
    gw_su = _mm(ysn, dys, 2 * Dm, Dm, S, mode="tn", out_dtype=BF16, name="ssm_up_dw")
    dogn = _mm(dyg, w_gu, S, 2 * Dm, Dm, mode="nt", out_dtype=F32, name="gdn_up_dx")
    gw_gu = _mm(ogn, dyg, 2 * Dm, Dm, S, mode="tn", out_dtype=BF16, name="gdn_up_dw")
    mix_x = _exchange_start([gw_su.reshape(N_DEV, -1, Dm), gw_gu.reshape(N_DEV, -1, Dm), gw_o.reshape(N_DEV, -1, Dm)],
                            ["a2a"] * 3, gw_gu, "grads_mix_start")
    dxs, dBm, dCm, dz_s, ddt_g, dpv_ssm, dnw_ssm = _ssd_bwd(dysn, conv, proj, dt_g, pv_ssm + mix_x[4][0, 0], nw_ssm,
                                                            y_ssd, st_ssm, "ssd_bwd")
    dq, dk, dv, dz_g, db_g, da_g, dpv_gdn, dnw_gdn = _gdn_bwd(dogn, conv, proj, b_g, a_g, pv_gdn, gdn_norm_w, o_pre,
                                                              st_gdn, t_inv, "gdn_bwd")
    conv_pieces = []
    for nm, d_act, col0 in (("xs", dxs, 0), ("B", dBm, 2048), ("C", dCm, 3072), ("q", dq, 4096), ("k", dk, 5120),
                            ("v", dv, 6144)):
        conv_pieces.append(_conv_bwd(d_act, proj, conv_w, conv_b, col0, "conv_bwd_" + nm) + (col0,))
    d_small = jnp.concatenate([_heads_last(ddt_g), _heads_last(db_g), _heads_last(da_g),
                               jnp.zeros((S, N_SMALL - 64), F32)], axis=1).astype(BF16)
    pieces = [(dz_s, C_ZS)] + [(p[0], C_XBC + p[3]) for p in conv_pieces] + [(dz_g, C_ZG), (dgate, C_GS)]
    gw_small = _mm(h, d_small, Dm, N_SMALL, S, mode="tn", out_dtype=BF16, name="proj_small_dw")
    gw_cols = [_mm(h, d_piece, Dm, d_piece.shape[1], S, mode="tn", out_dtype=BF16, name="proj_dw_%d" % off)
               for d_piece, off in pieces]
    main_cols = jnp.concatenate(gw_cols, axis=1)
    gw_in = jnp.concatenate([main_cols[:, 0:C_QKV], gw_small[:, 0:32], main_cols[:, C_QKV:C_GS], gw_small[:, 32:64],
                             main_cols[:, C_GS:N_MAIN]], axis=1)
    in_x = _exchange_start([_to_col_shards(gw_in)], ["a2a"], gw_small, "grads_w_in_start")
    dh = _mm(d_small, w_small + in_x[4][0:1, 0:1].astype(BF16), S, Dm, N_SMALL, mode="nt", out_dtype=F32,
             name="proj_small_dx")
    for d_piece, off in pieces:
        dh = _mm(d_piece, w_main, S, Dm, d_piece.shape[1], mode="nt", out_dtype=F32, b_off=(0, off), add=dh,
                 name="proj_dx_%d" % off)
    dx, dsh1, dsc1, dw_pre1 = _pre_bwd(dh, x2, norm_mix_pre, sc1, dx1, "pre_mix_bwd")
    r_mu, r_md = _exchange_wait(mlp_x, ["a2a"] * 2, dx, "grads_mlp_wait")
    r_su, r_gu, r_o = _exchange_wait(mix_x, ["a2a"] * 3, dx, "grads_mix_wait")

    dconv_w = jnp.concatenate([p[1] for p in conv_pieces], axis=1)
    dconv_b = jnp.concatenate([p[2] for p in conv_pieces[:3]], axis=1)
    dmod = jnp.concatenate([dsh1, dsc1, dg1, dsh2, dsc2, dg2], axis=1)
    small_vec = jnp.concatenate(
        [dmod, dw_pre1, dw_post1, dconv_b, dpv_ssm[:, 0].reshape(1, 32), dpv_ssm[:, 1].reshape(1, 32),
         dpv_ssm[:, 2].reshape(1, 32), dnw_ssm.reshape(1, 2048), dpv_gdn[:, 0].reshape(1, 16),
         dpv_gdn[:, 1].reshape(1, 16), jnp.sum(dnw_gdn, axis=0), dw_pre2, dw_post2, dconv_w.reshape(1, -1)], axis=1)
    n_vec = small_vec.shape[1]
    small_vec = jnp.pad(small_vec, ((0, 0), (0, (-n_vec) % 1024))).reshape(-1, 1024)
    (small_all,) = _exchange([small_vec], ["gather"], "gather_small_grads")
    small_all = small_all.reshape(N_DEV, -1)
    dmod_cols = lax.dynamic_slice(small_all, (0, me * n_ada), (N_DEV, n_ada))
    gw_ada = _ada_bwd(c_all.T, dmod_cols, "ada_bwd")
    conv_all = small_all[:, N_REPLICATED:n_vec].reshape(N_DEV, CONV_K, 2 * N_DEV * 512)
    conv_contrib = jnp.concatenate(
        [lax.dynamic_slice(conv_all, (0, 0, me * 512), (N_DEV, CONV_K, 512)),
         lax.dynamic_slice(conv_all, (0, 0, N_DEV * 512 + me * 512), (N_DEV, CONV_K, 512))], axis=1)
    rep_contrib = small_all[:, :N_REPLICATED].reshape(N_DEV, N_REPLICATED // 128, 128)

    results = {}

    def adam_big(nm, contrib):
        w3 = given[nm]
        res = _adam(contrib, w3[0][0], w3[1][0], w3[2][0], "adam_" + nm)
        results[nm] = tuple(r.reshape(w3[0].shape) for r in res)

    adam_big("w_ada", gw_ada[None])
    adam_big("w_ssm_up", r_su)
    adam_big("w_gdn_up", r_gu)
    adam_big("w_out", r_o)
    adam_big("w_mlp_up", r_mu)
    adam_big("w_mlp_down", r_md)
    (r_in,) = _exchange_wait(in_x, ["a2a"], results["w_mlp_down"][0], "grads_w_in_wait")
    adam_big("w_in", r_in)
    packed = [jnp.concatenate([given[nm][i] for nm in REPLICATED], axis=1).reshape(N_REPLICATED // 128, 128)
              for i in range(3)]
    rep_res = _adam(rep_contrib, packed[0], packed[1], packed[2], "adam_replicated")
    pos = 0
    for nm in REPLICATED:
        size = given[nm][0].shape[1]
        results[nm] = tuple(r.reshape(1, N_REPLICATED)[:, pos:pos + size] for r in rep_res)
        pos += size
    conv_wmv = [jnp.concatenate([given["ssm_conv_w"][i][0], given["gdn_conv_w"][i][0]], axis=0) for i in range(3)]
    conv_res = _adam(conv_contrib, conv_wmv[0], conv_wmv[1], conv_wmv[2], "adam_conv_w")
    results["ssm_conv_w"] = tuple(r[None, :CONV_K] for r in conv_res)
    results["gdn_conv_w"] = tuple(r[None, CONV_K:] for r in conv_res)

    loss = lax.psum(loss_loc[0, 0], ("x", "y", "c"))
    return (loss, dx[None]) + tuple(results[nm][i] for i in range(4) for nm in WEIGHTS)
```

```python
import jax
import jax.numpy as jnp
from jax import lax
from jax.experimental import pallas as pl
from jax.experimental.pallas import tpu as pltpu

F32 = jnp.float32
BF16 = jnp.bfloat16
N_DEV = 8
D_MODEL = 1024
EPS = 1e-6
CONV_K = 4
SSM_CHUNK = 128
SSM_HEAD_DIM = 64
SSM_D_STATE = 128
SSM_GROUPS = 8
SSM_HEADS_PER_GROUP = 4
SSM_GROUP_WIDTH = SSM_HEADS_PER_GROUP * SSM_HEAD_DIM
SSM_GROUPS_PER_STEP = 2
GDN_CHUNK = 64
GDN_HEAD = 128
GDN_QK_HEADS = 8
GDN_V_PER_QK = 2
GDN_QK_PER_STEP = 4
GDN_INV_BLOCK = 16
C_ZS, C_XBC, C_QKV, C_ZG, C_GS, C_GG, N_MAIN = 0, 2048, 6144, 10240, 12288, 13312, 14336
N_SMALL = 128
ADAM_LR, ADAM_B1, ADAM_B2, ADAM_EPS, ADAM_WD, ADAM_STEP = 0.001, 0.9, 0.999, 1e-08, 0.01, 10
VMEM_LIMIT = 56 * 1024 * 1024
NEG_INF = float("-inf")

_NT = (((1,), (1,)), ((), ()))
_NN = (((1,), (0,)), ((), ()))
_TN = (((0,), (0,)), ((), ()))


def _params(*sem):
    return pltpu.CompilerParams(dimension_semantics=sem, vmem_limit_bytes=VMEM_LIMIT)


def _dot(a, b, dims=_NN):
    return lax.dot_general(a.astype(BF16), b.astype(BF16), dims, preferred_element_type=F32)


def _split(a):
    hi = a.astype(BF16)
    return hi, (a - hi.astype(F32)).astype(BF16)


def _dot3(a, b, dims=_NN):
    ah, al = _split(a)
    bh, bl = _split(b)
    d = lambda u, v: lax.dot_general(u, v, dims, preferred_element_type=F32)
    return d(ah, bh) + (d(ah, bl) + d(al, bh))


def _sigmoid(x):
    return 1.0 / (1.0 + jnp.exp(-x))


def _silu(x):
    return x * _sigmoid(x)


def _dsilu(x):
    s = _sigmoid(x)
    return s * (1.0 + x * (1.0 - s))


def _softplus(x):
    return jnp.maximum(x, 0.0) + jnp.log1p(jnp.exp(-jnp.abs(x)))


def _iota(n, m, d):
    return lax.broadcasted_iota(jnp.int32, (n, m), d)


def _rowsum(x):
    return jnp.sum(x, axis=1, keepdims=True)


def _colsum(x):
    return jnp.sum(x, axis=0, keepdims=True)


def _total(x):
    return _rowsum(_colsum(x))


MXU_LANES = 128


def _parts(x, n):
    out = []
    for _ in range(n):
        p = x.astype(BF16)
        out.append(p)
        x = x - p.astype(F32)
    return out


def _sum_by(m01, x, dims=_NN, n=3):
    return sum(lax.dot_general(m01, p, dims, preferred_element_type=F32) for p in _parts(x, n))


def _row_col_sums(q):
    ones = jnp.ones((q.shape[0], MXU_LANES), BF16)
    acc = 0.0
    for p in _parts(q, 2):
        acc = acc + (lax.dot_general(p, ones, _NN, preferred_element_type=F32)
                     - lax.dot_general(p, ones, _TN, preferred_element_type=F32))
    return acc[:, 0:1]


def _cumsum_forms(col, ii, jj):
    lower = jnp.where(ii >= jj, 1.0, 0.0).astype(BF16)
    cum_col = _sum_by(lower, jnp.broadcast_to(col, (col.shape[0], MXU_LANES)))[:, 0:1]
    cum_row = _colsum(jnp.where(ii <= jj, col, 0.0))
    return cum_col, cum_row


def _rev_cumsum_col(col, ii, jj):
    upper = jnp.where(ii <= jj, 1.0, 0.0).astype(BF16)
    return _sum_by(upper, jnp.broadcast_to(col, (col.shape[0], MXU_LANES)))[:, 0:1]


def _blk(dim, pref):
    return pref if dim % pref == 0 else dim


def _lockstep(gens):
    gens = list(gens)
    while gens:
        alive = []
        for g in gens:
            try:
                next(g)
                alive.append(g)
            except StopIteration:
                pass
        gens = alive


def _mm(a, b, M, N, K, *, mode, out_dtype, name, a_off=(0, 0), b_off=(0, 0), add=None, epi=None, extra=None,
        tm=1024, tn=1024, tk=1024):
    tm, tn, tk = _blk(M, tm), _blk(N, tn), _blk(K, tk)
    nk = K // tk
    if mode == "tn":
        a_spec = pl.BlockSpec((tk, tm), lambda i, j, k: (k + a_off[0] // tk, i + a_off[1] // tm))
        assert a_off[0] % tk == 0 and a_off[1] % tm == 0
    else:
        a_spec = pl.BlockSpec((tm, tk), lambda i, j, k: (i + a_off[0] // tm, k + a_off[1] // tk))
        assert a_off[0] % tm == 0 and a_off[1] % tk == 0
    if mode == "nt":
        b_spec = pl.BlockSpec((tn, tk), lambda i, j, k: (j + b_off[0] // tn, k + b_off[1] // tk))
        assert b_off[0] % tn == 0 and b_off[1] % tk == 0
    else:
        b_spec = pl.BlockSpec((tk, tn), lambda i, j, k: (k + b_off[0] // tk, j + b_off[1] // tn))
        assert b_off[0] % tk == 0 and b_off[1] % tn == 0
    dims = {"nn": _NN, "nt": _NT, "tn": _TN}[mode]
    o_spec = pl.BlockSpec((tm, tn), lambda i, j, k: (i, j))
    ins, in_specs = [a, b], [a_spec, b_spec]
    if add is not None:
        ins.append(add)
        in_specs.append(o_spec)
    if extra is not None:
        ins.append(extra)
        in_specs.append(o_spec)
    n_in = len(ins)
    if epi == "relu2":
        out_shape = (jax.ShapeDtypeStruct((M, N), F32), jax.ShapeDtypeStruct((M, N), BF16))
        out_specs = (o_spec, o_spec)
    else:
        out_shape = jax.ShapeDtypeStruct((M, N), out_dtype)
        out_specs = o_spec

    def body(*refs):
        a_ref, b_ref = refs[0], refs[1]
        acc = refs[-1]
        outs = refs[n_in:-1]
        k = pl.program_id(2)

        @pl.when(k == 0)
        def _():
            acc[...] = jnp.zeros_like(acc)

        acc[...] += _dot(a_ref[...], b_ref[...], dims)

        @pl.when(k == nk - 1)
        def _():
            r = acc[...]
            pos = 2
            if add is not None:
                r = r + refs[pos][...]
                pos += 1
            if epi == "relu2":
                outs[0][...] = r
                p = jnp.maximum(r, 0.0)
                outs[1][...] = (p * p).astype(BF16)
            elif epi == "drelu2":
                outs[0][...] = (r * (2.0 * jnp.maximum(refs[pos][...], 0.0))).astype(out_dtype)
            else:
                outs[0][...] = r.astype(out_dtype)

    return pl.pallas_call(
        body, name=name, grid=(M // tm, N // tn, nk), in_specs=in_specs, out_specs=out_specs, out_shape=out_shape,
        scratch_shapes=[pltpu.VMEM((tm, tn), F32)],
        compiler_params=_params("parallel", "parallel", "arbitrary"))(*ins)


def _row_spec(tb, d):
    return pl.BlockSpec((tb, d), lambda i: (i, 0))


def _vec_spec(d):
    return pl.BlockSpec((1, d), lambda i: (0, 0))


def _pre_fwd(x, w, sc, sh, name):
    S, Dm = x.shape
    tb = _blk(S, 512)

    def body(x_ref, w_ref, sc_ref, sh_ref, h_ref):
        xv = x_ref[...]
        r = lax.rsqrt(jnp.mean(xv * xv, axis=-1, keepdims=True) + EPS)
        h_ref[...] = ((xv * r * w_ref[...]) * (1.0 + sc_ref[...]) + sh_ref[...]).astype(BF16)

    return pl.pallas_call(
        body, name=name, grid=(S // tb,), in_specs=[_row_spec(tb, Dm)] + [_vec_spec(Dm)] * 3,
        out_specs=_row_spec(tb, Dm), out_shape=jax.ShapeDtypeStruct((S, Dm), BF16),
        compiler_params=_params("parallel"))(x, w, sc, sh)


def _post_fwd(x, y, w, g, name):
    S, Dm = x.shape
    tb = _blk(S, 512)

    def body(x_ref, y_ref, w_ref, g_ref, o_ref):
        yv = y_ref[...]
        r = lax.rsqrt(jnp.mean(yv * yv, axis=-1, keepdims=True) + EPS)
        o_ref[...] = x_ref[...] + g_ref[...] * (yv * r * w_ref[...])

    return pl.pallas_call(
        body, name=name, grid=(S // tb,), in_specs=[_row_spec(tb, Dm)] * 2 + [_vec_spec(Dm)] * 2,
        out_specs=_row_spec(tb, Dm), out_shape=jax.ShapeDtypeStruct((S, Dm), F32),
        compiler_params=_params("parallel"))(x, y, w, g)


def _final_fwd(x, y, w, g, target, name):
    S, Dm = x.shape
    tb = _blk(S, 512)
    nb = S // tb

    def body(x_ref, y_ref, w_ref, g_ref, t_ref, dx_ref, loss_ref, acc):
        i = pl.program_id(0)

        @pl.when(i == 0)
        def _():
            acc[...] = jnp.zeros_like(acc)

        yv = y_ref[...]
        r = lax.rsqrt(jnp.mean(yv * yv, axis=-1, keepdims=True) + EPS)
        e = (x_ref[...] + g_ref[...] * (yv * r * w_ref[...])) - t_ref[...]
        dx_ref[...] = e * (1.0 / Dm)
        acc[...] += _colsum(e * e)

        @pl.when(i == nb - 1)
        def _():
            loss_ref[...] = (0.5 / Dm) * _rowsum(acc[...])

    return pl.pallas_call(
        body, name=name, grid=(nb,), in_specs=[_row_spec(tb, Dm)] * 2 + [_vec_spec(Dm)] * 2 + [_row_spec(tb, Dm)],
        out_specs=(_row_spec(tb, Dm), pl.BlockSpec((1, 1), lambda i: (0, 0))),
        out_shape=(jax.ShapeDtypeStruct((S, Dm), F32), jax.ShapeDtypeStruct((1, 1), F32)),
        scratch_shapes=[pltpu.VMEM((1, Dm), F32)], compiler_params=_params("arbitrary"))(x, y, w, g, target)


def _post_bwd(dxo, y, w, g, name):
    S, Dm = y.shape
    tb = _blk(S, 512)

    def body(d_ref, y_ref, w_ref, g_ref, dy_ref, dg_ref, dw_ref):
        i = pl.program_id(0)

        @pl.when(i == 0)
        def _():
            dg_ref[...] = jnp.zeros_like(dg_ref)
            dw_ref[...] = jnp.zeros_like(dw_ref)

        yv, dv = y_ref[...], d_ref[...]
        r = lax.rsqrt(jnp.mean(yv * yv, axis=-1, keepdims=True) + EPS)
        yh = yv * r
        dg_ref[...] += _colsum(dv * (yh * w_ref[...]))
        dn = dv * g_ref[...]
        dw_ref[...] += _colsum(dn * yh)
        dyh = dn * w_ref[...]
        dy_ref[...] = (r * (dyh - yh * jnp.mean(dyh * yh, axis=-1, keepdims=True))).astype(BF16)

    return pl.pallas_call(
        body, name=name, grid=(S // tb,), in_specs=[_row_spec(tb, Dm)] * 2 + [_vec_spec(Dm)] * 2,
        out_specs=(_row_spec(tb, Dm), _vec_spec(Dm), _vec_spec(Dm)),
        out_shape=(jax.ShapeDtypeStruct((S, Dm), BF16), jax.ShapeDtypeStruct((1, Dm), F32),
                   jax.ShapeDtypeStruct((1, Dm), F32)),
        compiler_params=_params("arbitrary"))(dxo, y, w, g)


def _pre_bwd(dh, x, w, sc, dres, name):
    S, Dm = x.shape
    tb = _blk(S, 512)

    def body(dh_ref, x_ref, w_ref, sc_ref, dr_ref, dx_ref, dsh_ref, dsc_ref, dw_ref):
        i = pl.program_id(0)

        @pl.when(i == 0)
        def _():
            dsh_ref[...] = jnp.zeros_like(dsh_ref)
            dsc_ref[...] = jnp.zeros_like(dsc_ref)
            dw_ref[...] = jnp.zeros_like(dw_ref)

        xv, dv = x_ref[...], dh_ref[...]
        r = lax.rsqrt(jnp.mean(xv * xv, axis=-1, keepdims=True) + EPS)
        xh = xv * r
        one_sc = 1.0 + sc_ref[...]
        dsh_ref[...] += _colsum(dv)
        dsc_ref[...] += _colsum(dv * (xh * w_ref[...]))
        dw_ref[...] += _colsum(dv * one_sc * xh)
        dxh = dv * one_sc * w_ref[...]
        dx_ref[...] = dr_ref[...] + r * (dxh - xh * jnp.mean(dxh * xh, axis=-1, keepdims=True))

    vec = jax.ShapeDtypeStruct((1, Dm), F32)
    return pl.pallas_call(
        body, name=name, grid=(S // tb,),
        in_specs=[_row_spec(tb, Dm)] * 2 + [_vec_spec(Dm)] * 2 + [_row_spec(tb, Dm)],
        out_specs=(_row_spec(tb, Dm), _vec_spec(Dm), _vec_spec(Dm), _vec_spec(Dm)),
        out_shape=(jax.ShapeDtypeStruct((S, Dm), F32), vec, vec, vec),
        compiler_params=_params("arbitrary"))(dh, x, w, sc, dres)


def _merge_fwd(ys, yg, proj, name):
    S, Dm = ys.shape
    tb = _blk(S, 512)

    def body(ys_ref, yg_ref, gs_ref, gg_ref, o_ref):
        o_ref[...] = (_sigmoid(gs_ref[...]) * ys_ref[...] + _sigmoid(gg_ref[...]) * yg_ref[...]).astype(BF16)

    return pl.pallas_call(
        body, name=name, grid=(S // tb,),
        in_specs=[_row_spec(tb, Dm)] * 2 + [pl.BlockSpec((tb, Dm), lambda i: (i, C_GS // Dm)),
                                            pl.BlockSpec((tb, Dm), lambda i: (i, C_GG // Dm))],
        out_specs=_row_spec(tb, Dm), out_shape=jax.ShapeDtypeStruct((S, Dm), BF16),
        compiler_params=_params("parallel"))(ys, yg, proj, proj)


def _merge_bwd(dm, ys, yg, proj, name):
    S, Dm = ys.shape
    tb = _blk(S, 512)

    def body(dm_ref, ys_ref, yg_ref, gs_ref, gg_ref, dys_ref, dyg_ref, dgate_ref):
        d = dm_ref[...]
        ss, sg = _sigmoid(gs_ref[...]), _sigmoid(gg_ref[...])
        dys_ref[...] = (d * ss).astype(BF16)
        dyg_ref[...] = (d * sg).astype(BF16)
        dgate_ref[:, :Dm] = (d * ys_ref[...] * ss * (1.0 - ss)).astype(BF16)
        dgate_ref[:, Dm:] = (d * yg_ref[...] * sg * (1.0 - sg)).astype(BF16)

    return pl.pallas_call(
        body, name=name, grid=(S // tb,),
        in_specs=[_row_spec(tb, Dm)] * 3 + [pl.BlockSpec((tb, Dm), lambda i: (i, C_GS // Dm)),
                                            pl.BlockSpec((tb, Dm), lambda i: (i, C_GG // Dm))],
        out_specs=(_row_spec(tb, Dm), _row_spec(tb, Dm), _row_spec(tb, 2 * Dm)),
        out_shape=(jax.ShapeDtypeStruct((S, Dm), BF16), jax.ShapeDtypeStruct((S, Dm), BF16),
                   jax.ShapeDtypeStruct((S, 2 * Dm), BF16)),
        compiler_params=_params("parallel"))(dm, ys, yg, proj, proj)


CONV_COLS = 128


def _shift_down(x, k, rows):
    return jnp.where(rows >= k, pltpu.roll(x, k, 0), 0.0)


def _shift_up(x, k, rows, S):
    return jnp.where(rows < S - k, pltpu.roll(x, S - k, 0), 0.0)


def _conv_fwd(proj, w, b, name):
    S = proj.shape[0]
    n = w.shape[1]
    cb = CONV_COLS

    def body(x_ref, w_ref, b_ref, o_ref):
        x = x_ref[...]
        rows = _iota(S, cb, 0)
        pre = x * w_ref[CONV_K - 1:CONV_K, :] + b_ref[...]
        for k in range(1, CONV_K):
            pre = pre + _shift_down(x, k, rows) * w_ref[CONV_K - 1 - k:CONV_K - k, :]
        o_ref[...] = _silu(pre)

    return pl.pallas_call(
        body, name=name, grid=(n // cb,),
        in_specs=[pl.BlockSpec((S, cb), lambda j: (0, j + C_XBC // cb)), pl.BlockSpec((CONV_K, cb), lambda j: (0, j)),
                  pl.BlockSpec((1, cb), lambda j: (0, j))],
        out_specs=pl.BlockSpec((S, cb), lambda j: (0, j)), out_shape=jax.ShapeDtypeStruct((S, n), F32),
        compiler_params=_params("parallel"))(proj, w, b)


def _conv_bwd(dact, proj, w, b, col0, name):
    S, n = dact.shape
    cb = CONV_COLS
    o = col0 // cb

    def body(d_ref, x_ref, w_ref, b_ref, dx_ref, dw_ref, db_ref):
        x = x_ref[...]
        rows = _iota(S, cb, 0)
        xs = [x] + [_shift_down(x, k, rows) for k in range(1, CONV_K)]
        pre = xs[0] * w_ref[CONV_K - 1:CONV_K, :] + b_ref[...]
        for k in range(1, CONV_K):
            pre = pre + xs[k] * w_ref[CONV_K - 1 - k:CONV_K - k, :]
        dpre = d_ref[...] * _dsilu(pre)
        db_ref[...] = _colsum(dpre)
        dx = dpre * w_ref[CONV_K - 1:CONV_K, :]
        for k in range(CONV_K):
            dw_ref[CONV_K - 1 - k:CONV_K - k, :] = _colsum(dpre * xs[k])
            if k:
                dx = dx + _shift_up(dpre, k, rows, S) * w_ref[CONV_K - 1 - k:CONV_K - k, :]
        dx_ref[...] = dx.astype(BF16)

    return pl.pallas_call(
        body, name=name, grid=(n // cb,),
        in_specs=[pl.BlockSpec((S, cb), lambda j: (0, j)), pl.BlockSpec((S, cb), lambda j: (0, j + o + C_XBC // cb)),
                  pl.BlockSpec((CONV_K, cb), lambda j: (0, j + o)), pl.BlockSpec((1, cb), lambda j: (0, j + o))],
        out_specs=(pl.BlockSpec((S, cb), lambda j: (0, j)), pl.BlockSpec((CONV_K, cb), lambda j: (0, j)),
                   pl.BlockSpec((1, cb), lambda j: (0, j))),
        out_shape=(jax.ShapeDtypeStruct((S, n), BF16), jax.ShapeDtypeStruct((CONV_K, n), F32),
                   jax.ShapeDtypeStruct((1, n), F32)),
        compiler_params=_params("parallel"))(dact, proj, w, b)


def _ssd_specs(L, order):
    G = SSM_GROUPS_PER_STEP
    W, N = G * SSM_GROUP_WIDTH, G * SSM_D_STATE
    x_spec = pl.BlockSpec((L, W), lambda g, c: (order(c), g))
    b_spec = pl.BlockSpec((L, N), lambda g, c: (order(c), 2048 // N + g))
    c_spec = pl.BlockSpec((L, N), lambda g, c: (order(c), 3072 // N + g))
    z_spec = pl.BlockSpec((L, W), lambda g, c: (order(c), C_ZS // W + g))
    dt_spec = pl.BlockSpec((G, L, SSM_HEADS_PER_GROUP), lambda g, c: (g, order(c), 0))
    p_spec = pl.BlockSpec((G, 3, SSM_HEADS_PER_GROUP), lambda g, c: (g, 0, 0))
    nw_spec = pl.BlockSpec((G, 1, SSM_GROUP_WIDTH), lambda g, c: (g, 0, 0))
    s_spec = pl.BlockSpec((G, 1, SSM_GROUP_WIDTH, SSM_D_STATE), lambda g, c: (g, order(c), 0, 0))
    return x_spec, b_spec, c_spec, z_spec, dt_spec, p_spec, nw_spec, s_spec


class _SsdGroup:
    def __init__(self, L):
        P, H, W = SSM_HEAD_DIM, SSM_HEADS_PER_GROUP, SSM_GROUP_WIDTH
        self.L = L
        self.ii, self.jj = _iota(L, L, 0), _iota(L, L, 1)
        self.lower = jnp.where(self.ii >= self.jj, 1.0, 0.0).astype(BF16)
        self.upper = jnp.where(self.ii <= self.jj, 1.0, 0.0).astype(BF16)
        self.lo = _iota(L, 2 * P, 1) < P
        self.lo_row = _iota(1, 2 * P, 1) < P
        bi, bj = _iota(W, W, 0), _iota(W, W, 1)
        self.block = jnp.where(bi // P == bj // P, 1.0, 0.0).astype(BF16)
        si, sj = _iota(2 * P, W, 0), _iota(2 * P, W, 1)
        self.pick = jnp.where(sj == si * P, 1.0, 0.0).astype(BF16)
        self.ones = jnp.ones((L, 2 * P), BF16)

    def spread(self, v4):
        R = v4.shape[0]
        lo = self.lo if R == self.L else self.lo_row
        b = lambda h: jnp.broadcast_to(v4[:, h:h + 1], (R, 2 * SSM_HEAD_DIM))
        return jnp.concatenate([jnp.where(lo, b(0), b(1)), jnp.where(lo, b(2), b(3))], axis=1)

    def gather4(self, v):
        return jnp.concatenate([v[:, h * SSM_HEAD_DIM:h * SSM_HEAD_DIM + 1] for h in range(SSM_HEADS_PER_GROUP)],
                               axis=1)

    def head_sums(self, z):
        return sum(lax.dot_general(p, self.block, _NN, preferred_element_type=F32) for p in _parts(z, 2))

    def pair_cols(self, full, pair):
        ps = full[:, pair * 128:(pair + 1) * 128]
        sw = pltpu.roll(ps, SSM_HEAD_DIM, 1)
        return jnp.where(self.lo, ps, sw), jnp.where(self.lo, sw, ps)

    def gates(self, dt4_raw, p):
        L = self.L
        dtr = self.spread(dt4_raw + p[0:1, :])
        dt = _softplus(dtr)
        A = self.spread(-jnp.exp(p[1:2, :]))
        acum = _sum_by(self.lower, dt * A)
        yield
        rows = _sum_by(self.pick, acum, _NT)
        yield
        a_last = acum[L - 1:L, :]
        cols = self.pair_cols(acum, 0) + self.pair_cols(acum, 1)
        decay, decay_t = [], []
        for h in range(SSM_HEADS_PER_GROUP):
            seg = cols[h] - rows[h:h + 1, :]
            decay.append(jnp.exp(jnp.where(self.ii >= self.jj, seg, NEG_INF)))
            decay_t.append(jnp.exp(jnp.where(self.jj >= self.ii, -seg, NEG_INF)))
        return dict(dtr=dtr, dt=dt, A=A, D=self.spread(p[2:3, :]), acum=acum, eac=jnp.exp(acum), a_last=a_last,
                    wdec=jnp.exp(a_last - acum), decay=decay, decay_t=decay_t,
                    ea_last=[jnp.exp(rows[h:h + 1, L - 1:L]) for h in range(SSM_HEADS_PER_GROUP)])


def _ssd_fwd(conv, proj, dt_raw, pvec, nw, name):
    S = conv.shape[0]
    L, P, N, H, W, G = SSM_CHUNK, SSM_HEAD_DIM, SSM_D_STATE, SSM_HEADS_PER_GROUP, SSM_GROUP_WIDTH, SSM_GROUPS_PER_STEP
    nc = S // L

    def body(x_ref, b_ref, c_ref, z_ref, dt_ref, p_ref, nw_ref, y_ref, yn_ref, s0_ref, state):
        c = pl.program_id(1)

        @pl.when(c == 0)
        def _():
            state[...] = jnp.zeros_like(state)

        k = _SsdGroup(L)

        def group(gi):
            gsl = slice(gi * W, (gi + 1) * W)
            Bm, Cm = b_ref[:, gi * N:(gi + 1) * N], c_ref[:, gi * N:(gi + 1) * N]
            x = x_ref[:, gsl]
            S0 = state[gsl, :]
            s0_ref[gi, 0] = S0
            CB = _dot(Cm, Bm, _NT)
            y_off = _dot(Cm, S0, _NT)
            t = yield from k.gates(dt_ref[gi], p_ref[gi])
            xdt = x * t["dt"]
            s_new = _dot(xdt * t["wdec"], Bm, _TN)
            y_diag = []
            for pair in range(H // 2):
                xp = xdt[:, pair * 128:(pair + 1) * 128]
                y_diag.append(jnp.where(k.lo, _dot(CB * t["decay"][2 * pair], xp),
                                        _dot(CB * t["decay"][2 * pair + 1], xp)))
            yield
            y = jnp.concatenate(y_diag, axis=1) + y_off * t["eac"]
            for h in range(H):
                hsl = slice(gi * W + h * P, gi * W + (h + 1) * P)
                state[hsl, :] = S0[h * P:(h + 1) * P, :] * t["ea_last"][h] + s_new[h * P:(h + 1) * P, :]
            y_ref[:, gsl] = y
            y2 = (y + t["D"] * x) * _silu(z_ref[:, gsl])
            r = lax.rsqrt(jnp.mean(y2 * y2, axis=-1, keepdims=True) + EPS)
            yn_ref[:, gsl] = (y2 * r * nw_ref[gi]).astype(BF16)

        _lockstep(group(gi) for gi in range(G))

    x_spec, b_spec, c_spec, z_spec, dt_spec, p_spec, nw_spec, s_spec = _ssd_specs(L, lambda c: c)
    y_spec = pl.BlockSpec((L, G * W), lambda g, c: (c, g))
    return pl.pallas_call(
        body, name=name, grid=(SSM_GROUPS // G, nc),
        in_specs=[x_spec, b_spec, c_spec, z_spec, dt_spec, p_spec, nw_spec],
        out_specs=(y_spec, y_spec, s_spec),
        out_shape=(jax.ShapeDtypeStruct((S, SSM_GROUPS * W), F32), jax.ShapeDtypeStruct((S, SSM_GROUPS * W), BF16),
                   jax.ShapeDtypeStruct((SSM_GROUPS, nc, W, N), F32)),
        scratch_shapes=[pltpu.VMEM((G * W, N), F32)],
        compiler_params=_params("parallel", "arbitrary"))(conv, conv, conv, proj, dt_raw, pvec, nw)


def _ssd_bwd(dyn, conv, proj, dt_raw, pvec, nw, y_ssd, states, name):
    S = conv.shape[0]
    L, P, N, H, W, G = SSM_CHUNK, SSM_HEAD_DIM, SSM_D_STATE, SSM_HEADS_PER_GROUP, SSM_GROUP_WIDTH, SSM_GROUPS_PER_STEP
    nc = S // L

    def body(dyn_ref, x_ref, b_ref, c_ref, z_ref, dt_ref, p_ref, nw_ref, y_ref, s0_ref,
             dx_ref, db_ref, dc_ref, dz_ref, ddt_ref, dp_ref, dnw_ref, dstate):
        c = pl.program_id(1)

        @pl.when(c == 0)
        def _():
            dstate[...] = jnp.zeros_like(dstate)
            dp_ref[...] = jnp.zeros_like(dp_ref)
            dnw_ref[...] = jnp.zeros_like(dnw_ref)

        k = _SsdGroup(L)
        last = (_iota(L, 1, 0) == L - 1)

        def group(gi):
            gsl = slice(gi * W, (gi + 1) * W)
            Bm, Cm = b_ref[:, gi * N:(gi + 1) * N], c_ref[:, gi * N:(gi + 1) * N]
            x, z = x_ref[:, gsl], z_ref[:, gsl]
            S0, dS1 = s0_ref[gi, 0], dstate[gsl, :]
            CB = _dot(Cm, Bm, _NT)
            CBt = _dot(Bm, Cm, _NT)
            y_off_raw = _dot(Cm, S0, _NT)
            dXs_raw = _dot(Bm, dS1, _NT)
            t = yield from k.gates(dt_ref[gi], p_ref[gi])
            y1 = y_ref[:, gsl] + t["D"] * x
            sz = _silu(z)
            y2 = y1 * sz
            r = lax.rsqrt(jnp.mean(y2 * y2, axis=-1, keepdims=True) + EPS)
            y2h = y2 * r
            dyn_v = dyn_ref[:, gsl]
            dnw_ref[gi] += _colsum(dyn_v * y2h)
            dy2h = dyn_v * nw_ref[gi]
            dy2 = r * (dy2h - y2h * jnp.mean(dy2h * y2h, axis=-1, keepdims=True))
            dz_ref[:, gsl] = (dy2 * y1 * _dsilu(z)).astype(BF16)
            dY = dy2 * sz
            X = x * t["dt"]
            dYe = dY * t["eac"]
            dC_s = _dot(dYe, S0)
            dB_s = _dot(X * t["wdec"], dS1)
            dS_c = _dot(dYe, Cm, _TN)
            dXm, Gs, Gts = [], [], []
            for pair in range(H // 2):
                dYp, Xp = dY[:, pair * 128:(pair + 1) * 128], X[:, pair * 128:(pair + 1) * 128]
                dXm.append(jnp.where(k.lo, _dot(CBt * t["decay_t"][2 * pair], dYp),
                                     _dot(CBt * t["decay_t"][2 * pair + 1], dYp)))
                for mask in (k.lo, ~k.lo):
                    Gs.append(_dot(jnp.where(mask, dYp, 0.0), Xp, _NT))
                    Gts.append(_dot(jnp.where(mask, Xp, 0.0), dYp, _NT))
            yield
            dXs = dXs_raw * t["wdec"]
            dX = jnp.concatenate(dXm, axis=1) + dXs
            dCB, dCBt, q_sums = 0.0, 0.0, []
            for h in range(H):
                M, Mt = CB * t["decay"][h], CBt * t["decay_t"][h]
                dCB = dCB + Gs[h] * t["decay"][h]
                dCBt = dCBt + Gts[h] * t["decay_t"][h]
                d = Gs[h] * M - Gts[h] * Mt
                q_sums.append(sum(lax.dot_general(pt, k.ones, _NN, preferred_element_type=F32)
                                  for pt in _parts(d, 2)))
            q_f = jnp.concatenate([jnp.where(k.lo, q_sums[0], q_sums[1]), jnp.where(k.lo, q_sums[2], q_sums[3])],
                                  axis=1)
            x_dxs = k.head_sums(X * dXs)
            tot = [_total(dS1[h * P:(h + 1) * P, :] * S0[h * P:(h + 1) * P, :]) * t["ea_last"][h] for h in range(H)]
            tot_f = k.spread(jnp.concatenate(tot, axis=1))
            d_alast = _colsum(x_dxs) + tot_f
            dacum = q_f + k.head_sums(dY * (y_off_raw * t["eac"])) - x_dxs + jnp.where(last, d_alast, 0.0)
            dx_dt = k.head_sums(dX * x)
            d_skip = _colsum(k.head_sums(dY * x))
            for h in range(H):
                hsl = slice(gi * W + h * P, gi * W + (h + 1) * P)
                dstate[hsl, :] = t["ea_last"][h] * dS1[h * P:(h + 1) * P, :] + dS_c[h * P:(h + 1) * P, :]
            dc_s2 = _dot(dCB, Bm)
            db_s2 = _dot(dCBt, Cm)
            yield
            da = _sum_by(k.upper, dacum)
            yield
            ddt_raw = (da * t["A"] + dx_dt) * _sigmoid(t["dtr"])
            dx_ref[:, gsl] = dX * t["dt"] + t["D"] * dY
            dc_ref[:, gi * N:(gi + 1) * N] = dC_s + dc_s2
            db_ref[:, gi * N:(gi + 1) * N] = dB_s + db_s2
            ddt_ref[gi] = k.gather4(ddt_raw)
            dp_ref[gi] += k.gather4(jnp.concatenate([_colsum(ddt_raw), _colsum(da * t["dt"]) * t["A"], d_skip],
                                                    axis=0))

        _lockstep(group(gi) for gi in range(G))

    rev = lambda c: nc - 1 - c
    x_spec, b_spec, c_spec, z_spec, dt_spec, p_spec, nw_spec, s_spec = _ssd_specs(L, rev)
    y_spec = pl.BlockSpec((L, G * W), lambda g, c: (rev(c), g))
    n_spec = pl.BlockSpec((L, G * N), lambda g, c: (rev(c), g))
    return pl.pallas_call(
        body, name=name, grid=(SSM_GROUPS // G, nc),
        in_specs=[y_spec, x_spec, b_spec, c_spec, z_spec, dt_spec, p_spec, nw_spec, y_spec, s_spec],
        out_specs=(y_spec, n_spec, n_spec, y_spec, dt_spec, p_spec, nw_spec),
        out_shape=(jax.ShapeDtypeStruct((S, SSM_GROUPS * W), F32), jax.ShapeDtypeStruct((S, SSM_GROUPS * N), F32),
                   jax.ShapeDtypeStruct((S, SSM_GROUPS * N), F32), jax.ShapeDtypeStruct((S, SSM_GROUPS * W), BF16),
                   jax.ShapeDtypeStruct((SSM_GROUPS, S, H), F32), jax.ShapeDtypeStruct((SSM_GROUPS, 3, H), F32),
                   jax.ShapeDtypeStruct((SSM_GROUPS, 1, W), F32)),
        scratch_shapes=[pltpu.VMEM((G * W, N), F32)],
        compiler_params=_params("parallel", "arbitrary"))(dyn, conv, conv, conv, proj, dt_raw, pvec, nw, y_ssd, states)


def _unit_lower_inverse(A, ii, jj):
    eye = (ii == jj).astype(F32)
    same = (ii // GDN_INV_BLOCK) == (jj // GDN_INV_BLOCK)
    Ad = jnp.where(same, A, 0.0)
    Ao = A - Ad
    P2 = _dot3(Ad, Ad)
    yield
    P4, X = _dot3(P2, P2), _dot3(eye - Ad, eye + P2)
    yield
    P8, X = _dot3(P4, P4), _dot3(X, eye + P4)
    yield
    X = _dot3(X, eye + P8)
    yield
    Bm = _dot3(X, Ao)
    yield
    B2 = _dot3(Bm, Bm)
    yield
    Y = _dot3(eye - Bm, eye + B2)
    yield
    T = _dot3(Y, X)
    yield
    return T


def _gdn_specs(L, order):
    G = GDN_QK_PER_STEP
    Hd, W = G * GDN_HEAD, G * GDN_V_PER_QK * GDN_HEAD
    q_spec = pl.BlockSpec((L, Hd), lambda h, c: (order(c), (C_QKV - C_XBC) // Hd + h))
    k_spec = pl.BlockSpec((L, Hd), lambda h, c: (order(c), (C_QKV - C_XBC + 1024) // Hd + h))
    v_spec = pl.BlockSpec((L, W), lambda h, c: (order(c), (C_QKV - C_XBC + 2048) // W + h))
    z_spec = pl.BlockSpec((L, W), lambda h, c: (order(c), C_ZG // W + h))
    ba_spec = pl.BlockSpec((G, L, GDN_V_PER_QK), lambda h, c: (h, order(c), 0))
    p_spec = pl.BlockSpec((G, 2, GDN_V_PER_QK), lambda h, c: (h, 0, 0))
    nw_spec = pl.BlockSpec((1, GDN_HEAD), lambda h, c: (0, 0))
    s_spec = pl.BlockSpec((G, 1, GDN_V_PER_QK * GDN_HEAD, GDN_HEAD), lambda h, c: (h, order(c), 0, 0))
    t_spec = pl.BlockSpec((G * GDN_V_PER_QK, 1, L, L), lambda h, c: (h, order(c), 0, 0))
    return q_spec, k_spec, v_spec, z_spec, ba_spec, p_spec, nw_spec, s_spec, t_spec


def _gdn_gates(qa, ka, b_col, a_col, p, j, ii, jj):
    L = qa.shape[0]
    sp_in = a_col + p[0:1, j:j + 1]
    neg_ea = -jnp.exp(p[1:2, j:j + 1])
    g = neg_ea * _softplus(sp_in)
    gcum, gcum_row = _cumsum_forms(g, ii, jj)
    rq = lax.rsqrt(_rowsum(qa * qa) + EPS)
    rk = lax.rsqrt(_rowsum(ka * ka) + EPS)
    q = qa * rq * (GDN_HEAD ** -0.5)
    k = ka * rk
    beta = _sigmoid(b_col)
    yield
    Dm = jnp.exp(jnp.where(ii >= jj, gcum - gcum_row, NEG_INF))
    eg = jnp.exp(gcum)
    g_last = gcum[L - 1:L, :]
    wdec = jnp.exp(g_last - gcum)
    return dict(rq=rq, rk=rk, q=q, k=k, beta=beta, sp_in=sp_in, neg_ea=neg_ea, g=g, Dm=Dm, kbeta=k * beta, eg=eg,
                g_last=g_last, wdec=wdec, kdec=k * wdec)


def _gdn_fwd(conv, proj, b_raw, a_raw, pvec, nw, name):
    S = conv.shape[0]
    L, Hd, J, G = GDN_CHUNK, GDN_HEAD, GDN_V_PER_QK, GDN_QK_PER_STEP
    W = J * Hd
    nc = S // L

    def body(q_ref, k_ref, v_ref, z_ref, b_ref, a_ref, p_ref, nw_ref, o_ref, on_ref, s0_ref, t_ref, state):
        c = pl.program_id(1)

        @pl.when(c == 0)
        def _():
            state[...] = jnp.zeros_like(state)

        ii, jj = _iota(L, L, 0), _iota(L, L, 1)
        for hq in range(G):
            s0_ref[hq, 0] = state[hq * W:(hq + 1) * W, :]

        def head(hq, j):
            hd = hq * J + j
            hsl, sl = slice(hq * Hd, (hq + 1) * Hd), slice(hd * Hd, (hd + 1) * Hd)
            t = yield from _gdn_gates(q_ref[:, hsl], k_ref[:, hsl], b_ref[hq][:, j:j + 1], a_ref[hq][:, j:j + 1],
                                      p_ref[hq], j, ii, jj)
            KK = _dot(t["kbeta"], t["k"], _NT)
            QK = _dot(t["q"], t["k"], _NT)
            yield
            T = yield from _unit_lower_inverse(jnp.where(ii > jj, KK * t["Dm"], 0.0), ii, jj)
            t_ref[hd, 0] = T
            S0 = state[sl, :]
            U = _dot3(T, v_ref[:, sl] * t["beta"])
            Wm = _dot3(T, t["kbeta"] * t["eg"])
            o_inter = _dot(t["q"] * t["eg"], S0)
            yield
            Vn = U - _dot(Wm, S0)
            yield
            o = o_inter + _dot(QK * t["Dm"], Vn)
            s_new = _dot(t["kdec"], Vn, _TN)
            yield
            state[sl, :] = S0 * jnp.exp(t["g_last"]) + s_new
            o_ref[:, sl] = o
            r = lax.rsqrt(jnp.mean(o * o, axis=-1, keepdims=True) + EPS)
            on_ref[:, sl] = ((o * r * nw_ref[...]) * _silu(z_ref[:, sl])).astype(BF16)

        _lockstep(head(hq, j) for hq in range(G) for j in range(J))

    q_spec, k_spec, v_spec, z_spec, ba_spec, p_spec, nw_spec, s_spec, t_spec = _gdn_specs(L, lambda c: c)
    o_spec = pl.BlockSpec((L, G * W), lambda h, c: (c, h))
    return pl.pallas_call(
        body, name=name, grid=(GDN_QK_HEADS // G, nc),
        in_specs=[q_spec, k_spec, v_spec, z_spec, ba_spec, ba_spec, p_spec, nw_spec],
        out_specs=(o_spec, o_spec, s_spec, t_spec),
        out_shape=(jax.ShapeDtypeStruct((S, GDN_QK_HEADS * W), F32), jax.ShapeDtypeStruct((S, GDN_QK_HEADS * W), BF16),
                   jax.ShapeDtypeStruct((GDN_QK_HEADS, nc, W, Hd), F32),
                   jax.ShapeDtypeStruct((GDN_QK_HEADS * J, nc, L, L), F32)),
        scratch_shapes=[pltpu.VMEM((G * W, Hd), F32)],
        compiler_params=_params("parallel", "arbitrary"))(conv, conv, conv, proj, b_raw, a_raw, pvec, nw)


def _gdn_bwd(don, conv, proj, b_raw, a_raw, pvec, nw, o_pre, states, t_inv, name):
    S = conv.shape[0]
    L, Hd, J, G = GDN_CHUNK, GDN_HEAD, GDN_V_PER_QK, GDN_QK_PER_STEP
    W = J * Hd
    nc = S // L

    def body(don_ref, q_ref, k_ref, v_ref, z_ref, b_ref, a_ref, p_ref, nw_ref, o_ref, s0_ref, t_ref,
             dq_ref, dk_ref, dv_ref, dz_ref, db_ref, da_ref, dp_ref, dnw_ref, dstate):
        c = pl.program_id(1)

        @pl.when(c == 0)
        def _():
            dstate[...] = jnp.zeros_like(dstate)
            dp_ref[...] = jnp.zeros_like(dp_ref)
            dnw_ref[...] = jnp.zeros_like(dnw_ref)

        ii, jj = _iota(L, L, 0), _iota(L, L, 1)
        last = (_iota(L, 1, 0) == L - 1)
        res = {}

        def head(hq, j):
            hd = hq * J + j
            hsl, sl = slice(hq * Hd, (hq + 1) * Hd), slice(hd * Hd, (hd + 1) * Hd)
            qa, ka = q_ref[:, hsl], k_ref[:, hsl]
            t = yield from _gdn_gates(qa, ka, b_ref[hq][:, j:j + 1], a_ref[hq][:, j:j + 1], p_ref[hq], j, ii, jj)
            q, k, beta, eg, Dm, kbeta, kdec = (t[nm] for nm in ("q", "k", "beta", "eg", "Dm", "kbeta", "kdec"))
            T = t_ref[hd, 0]
            v, z, o = v_ref[:, sl], z_ref[:, sl], o_ref[:, sl]
            S0, dS1 = s0_ref[hq, 0, j * Hd:(j + 1) * Hd, :], dstate[sl, :]
            sz = _silu(z)
            r = lax.rsqrt(jnp.mean(o * o, axis=-1, keepdims=True) + EPS)
            oh = o * r
            d_on = don_ref[:, sl]
            dz_ref[:, sl] = (d_on * (oh * nw_ref[...]) * _dsilu(z)).astype(BF16)
            dn = d_on * sz
            dnw_part = _colsum(dn * oh)
            doh = dn * nw_ref[...]
            dO = r * (doh - oh * jnp.mean(doh * oh, axis=-1, keepdims=True))
            Rw = kbeta * eg
            qe = q * eg
            U = _dot3(T, v * beta)
            Wm = _dot3(T, Rw)
            KK = _dot(kbeta, k, _NT)
            QK = _dot(q, k, _NT)
            o_inter = _dot(qe, S0)
            dq_s = _dot(dO, S0, _NT)
            dS_q = _dot(qe, dO, _TN)
            yield
            Am = jnp.where(ii > jj, KK * Dm, 0.0)
            Pm = QK * Dm
            Vn = U - _dot(Wm, S0)
            dVn_s = _dot(kdec, dS1)
            yield
            dVn = _dot(Pm, dO, _TN) + dVn_s
            dP = _dot(dO, Vn, _NT)
            dKd = _dot(Vn, dS1, _NT)
            yield
            dQK = dP * Dm
            dq = _dot(dQK, k) + dq_s * eg
            dk = _dot(dQK, q, _TN) + dKd * t["wdec"]
            dstate[sl, :] = jnp.exp(t["g_last"]) * dS1 + dS_q - _dot(Wm, dVn, _TN)
            dW = -_dot(dVn, S0, _NT)
            dRu = _dot3(T, dVn, _TN)
            yield
            dRw = _dot3(T, dW, _TN)
            dA_u = _dot(dRu, U, _NT)
            yield
            dA = jnp.where(ii > jj, -(dA_u + _dot(dRw, Wm, _NT)), 0.0)
            yield
            dKK = dA * Dm
            dkbeta = _dot(dKK, k) + dRw * eg
            dk = dk + _dot(dKK, kbeta, _TN)
            yield
            dk = dk + dkbeta * beta
            dbeta = _rowsum(dkbeta * k) + _rowsum(dRu * v)
            dv_ref[:, sl] = dRu * beta
            Q = dA * Am + dP * Pm
            rho = _rowsum(dKd * kdec)
            d_glast = _colsum(rho) + jnp.exp(t["g_last"]) * _total(dS1 * S0)
            q_sums = _row_col_sums(Q)
            rest = _rowsum(dRw * Rw) + _rowsum(dO * o_inter) - rho + jnp.where(last, d_glast, 0.0)
            yield
            dg = _rev_cumsum_col(q_sums + rest, ii, jj)
            yield
            da_raw = dg * t["neg_ea"] * _sigmoid(t["sp_in"])
            res[hq, j] = dict(dq=dq, dk=dk, db=dbeta * beta * (1.0 - beta), da=da_raw, d_bias=_colsum(da_raw),
                              d_alog=_colsum(dg * t["g"]), dnw=dnw_part, rq=t["rq"], rk=t["rk"], k=k, qh=qa * t["rq"])

        _lockstep(head(hq, j) for hq in range(G) for j in range(J))
        for hq in range(G):
            parts = [res[hq, j] for j in range(J)]
            hsl = slice(hq * Hd, (hq + 1) * Hd)
            p0 = parts[0]
            dqh = sum(pt["dq"] for pt in parts) * (GDN_HEAD ** -0.5)
            dkn = sum(pt["dk"] for pt in parts)
            dq_ref[:, hsl] = p0["rq"] * (dqh - p0["qh"] * _rowsum(dqh * p0["qh"]))
            dk_ref[:, hsl] = p0["rk"] * (dkn - p0["k"] * _rowsum(dkn * p0["k"]))
            db_ref[hq] = jnp.concatenate([pt["db"] for pt in parts], axis=1)
            da_ref[hq] = jnp.concatenate([pt["da"] for pt in parts], axis=1)
            dp_ref[hq] += jnp.concatenate([jnp.concatenate([pt["d_bias"] for pt in parts], axis=1),
                                           jnp.concatenate([pt["d_alog"] for pt in parts], axis=1)], axis=0)
            dnw_ref[hq] += sum(pt["dnw"] for pt in parts)

    rev = lambda c: nc - 1 - c
    q_spec, k_spec, v_spec, z_spec, ba_spec, p_spec, nw_spec, s_spec, t_spec = _gdn_specs(L, rev)
    o_spec = pl.BlockSpec((L, G * W), lambda h, c: (rev(c), h))
    h_spec = pl.BlockSpec((L, G * Hd), lambda h, c: (rev(c), h))
    dnw_spec = pl.BlockSpec((G, 1, Hd), lambda h, c: (h, 0, 0))
    return pl.pallas_call(
        body, name=name, grid=(GDN_QK_HEADS // G, nc),
        in_specs=[o_spec, q_spec, k_spec, v_spec, z_spec, ba_spec, ba_spec, p_spec, nw_spec, o_spec, s_spec, t_spec],
        out_specs=(h_spec, h_spec, o_spec, o_spec, ba_spec, ba_spec, p_spec, dnw_spec),
        out_shape=(jax.ShapeDtypeStruct((S, GDN_QK_HEADS * Hd), F32), jax.ShapeDtypeStruct((S, GDN_QK_HEADS * Hd), F32),
                   jax.ShapeDtypeStruct((S, GDN_QK_HEADS * W), F32), jax.ShapeDtypeStruct((S, GDN_QK_HEADS * W), BF16),
                   jax.ShapeDtypeStruct((GDN_QK_HEADS, S, J), F32), jax.ShapeDtypeStruct((GDN_QK_HEADS, S, J), F32),
                   jax.ShapeDtypeStruct((GDN_QK_HEADS, 2, J), F32), jax.ShapeDtypeStruct((GDN_QK_HEADS, 1, Hd), F32)),
        scratch_shapes=[pltpu.VMEM((G * W, Hd), F32)],
        compiler_params=_params("parallel", "arbitrary"))(don, conv, conv, conv, proj, b_raw, a_raw, pvec, nw, o_pre,
                                                          states, t_inv)


def _ada_fwd(c_all, w_loc, b_loc, name):
    n = w_loc.shape[1]

    def body(c_ref, w_ref, b_ref, o_ref):
        o_ref[...] = _dot3(_silu(c_ref[...]), w_ref[...]) + b_ref[...]

    return pl.pallas_call(body, name=name, out_shape=jax.ShapeDtypeStruct((N_DEV, n), F32),
                          compiler_params=pltpu.CompilerParams(vmem_limit_bytes=VMEM_LIMIT))(c_all, w_loc, b_loc)


def _ada_bwd(c_all_t, dmod_cols, name):
    Dm, n = c_all_t.shape[0], dmod_cols.shape[1]

    def body(c_ref, d_ref, o_ref):
        ca = _silu(c_ref[...])
        acc = ca[:, 0:1] * d_ref[0:1, :]
        for i in range(1, N_DEV):
            acc = acc + ca[:, i:i + 1] * d_ref[i:i + 1, :]
        o_ref[...] = acc

    return pl.pallas_call(body, name=name, out_shape=jax.ShapeDtypeStruct((Dm, n), F32),
                          compiler_params=pltpu.CompilerParams(vmem_limit_bytes=VMEM_LIMIT))(c_all_t, dmod_cols)


ADAM_BLOCK_BYTES = 12 * 1024 * 1024


def _adam(contrib, w, m, v, name):
    n, R, C = contrib.shape
    tr = R
    while tr % 16 == 0 and (n + 7) * tr * C * 4 > ADAM_BLOCK_BYTES:
        tr //= 2

    def body(c_ref, w_ref, m_ref, v_ref, g_ref, d_ref, nm_ref, nv_ref):
        g = c_ref[0].astype(F32)
        for i in range(1, n):
            g = g + c_ref[i].astype(F32)
        nm = ADAM_B1 * m_ref[...] + (1.0 - ADAM_B1) * g
        nv = ADAM_B2 * v_ref[...] + (1.0 - ADAM_B2) * (g * g)
        m_hat = nm / (1.0 - ADAM_B1 ** ADAM_STEP)
        v_hat = nv / (1.0 - ADAM_B2 ** ADAM_STEP)
        g_ref[...] = g
        d_ref[...] = -ADAM_LR * (m_hat / (jnp.sqrt(v_hat) + ADAM_EPS) + ADAM_WD * w_ref[...])
        nm_ref[...] = nm
        nv_ref[...] = nv

    spec = pl.BlockSpec((tr, C), lambda i: (i, 0))
    shp = jax.ShapeDtypeStruct((R, C), F32)
    return pl.pallas_call(
        body, name=name, grid=(R // tr,), in_specs=[pl.BlockSpec((n, tr, C), lambda i: (0, i, 0)), spec, spec, spec],
        out_specs=(spec,) * 4, out_shape=(shp,) * 4, compiler_params=_params("parallel"))(contrib, w, m, v)


def _exchange(arrays, modes, name):
    n = len(arrays)
    out_shape = tuple(jax.ShapeDtypeStruct((N_DEV,) + a.shape if md == "gather" else a.shape, a.dtype)
                      for a, md in zip(arrays, modes))

    def body(*refs):
        ins, outs = refs[:n], refs[n:2 * n]
        send_sems, recv_sems, loc_sems = refs[2 * n:]
        ix, iy, ic = lax.axis_index("x"), lax.axis_index("y"), lax.axis_index("c")
        me = 4 * ix + 2 * iy + ic
        peers = []
        for m in range(1, N_DEV):
            px = 1 - ix if m & 4 else ix
            py = 1 - iy if m & 2 else iy
            pc = 1 - ic if m & 1 else ic
            peers.append(((px, py, pc), 4 * px + 2 * py + pc))

        def src(k, slot):
            return ins[k] if modes[k] == "gather" else ins[k].at[slot]

        def remote(k, m, to_slot, land_slot):
            return pltpu.make_async_remote_copy(
                src_ref=src(k, to_slot), dst_ref=outs[k].at[land_slot], send_sem=send_sems.at[k, m],
                recv_sem=recv_sems.at[k, m], device_id=peers[m][0], device_id_type=pl.DeviceIdType.MESH)

        local = [pltpu.make_async_copy(src(k, me), outs[k].at[me], loc_sems.at[k]) for k in range(n)]
        for cp in local:
            cp.start()
        sends = [remote(k, m, peers[m][1], me) for m in range(N_DEV - 1) for k in range(n)]
        for cp in sends:
            cp.start()
        for m in range(N_DEV - 1):
            for k in range(n):
                remote(k, m, peers[m][1], peers[m][1]).wait_recv()
        for cp in sends:
            cp.wait_send()
        for cp in local:
            cp.wait()

    any_spec = pl.BlockSpec(memory_space=pl.ANY)
    return pl.pallas_call(
        body, name=name, in_specs=[any_spec] * n, out_specs=(any_spec,) * n, out_shape=out_shape,
        scratch_shapes=[pltpu.SemaphoreType.DMA((n, N_DEV - 1)), pltpu.SemaphoreType.DMA((n, N_DEV - 1)),
                        pltpu.SemaphoreType.DMA((n,))])(*arrays)


def _peer_table():
    ix, iy, ic = lax.axis_index("x"), lax.axis_index("y"), lax.axis_index("c")
    peers = []
    for m in range(1, N_DEV):
        px = 1 - ix if m & 4 else ix
        py = 1 - iy if m & 2 else iy
        pc = 1 - ic if m & 1 else ic
        peers.append(((px, py, pc), 4 * px + 2 * py + pc))
    return 4 * ix + 2 * iy + ic, peers


def _exchange_start(arrays, modes, after, name):
    n = len(arrays)
    land_shapes = [(N_DEV,) + a.shape if md == "gather" else a.shape for a, md in zip(arrays, modes)]

    def body(*refs):
        ins, lands = refs[:n], refs[n:2 * n]
        send_sems, recv_sems = refs[2 * n + 1], refs[2 * n + 2]
        token, loc_sems = refs[-2], refs[-1]
        me, peers = _peer_table()

        def src(k, slot):
            return ins[k] if modes[k] == "gather" else ins[k].at[slot]

        local = [pltpu.make_async_copy(src(k, me), lands[k].at[me], loc_sems.at[k]) for k in range(n)]
        for cp in local:
            cp.start()
        for cp in local:
            cp.wait()
        for m in range(N_DEV - 1):
            for k in range(n):
                pltpu.make_async_remote_copy(
                    src_ref=src(k, peers[m][1]), dst_ref=lands[k].at[me], send_sem=send_sems, recv_sem=recv_sems,
                    device_id=peers[m][0], device_id_type=pl.DeviceIdType.MESH).start()
        token[...] = jnp.zeros_like(token)

    hbm = pl.BlockSpec(memory_space=pltpu.HBM)
    sem = pl.BlockSpec(memory_space=pltpu.SEMAPHORE)
    sem_shape = pltpu.SemaphoreType.DMA(())
    operands = [pltpu.with_memory_space_constraint(a, pltpu.HBM) for a in arrays]
    operands += [pltpu.with_memory_space_constraint(lax.empty(s, a.dtype), pltpu.HBM)
                 for s, a in zip(land_shapes, arrays)]
    out = pl.pallas_call(
        body, name=name,
        out_shape=(sem_shape, sem_shape) + tuple(pltpu.HBM(a.shape, a.dtype) for a in arrays)
        + tuple(pltpu.HBM(s, a.dtype) for s, a in zip(land_shapes, arrays)) + (jax.ShapeDtypeStruct((8, 128), F32),),
        in_specs=[hbm] * (2 * n) + [pl.BlockSpec(memory_space=pl.ANY)],
        out_specs=(sem, sem) + (hbm,) * (2 * n) + (pl.BlockSpec(memory_space=pltpu.VMEM),),
        input_output_aliases={i: 2 + i for i in range(2 * n)},
        scratch_shapes=[pltpu.SemaphoreType.DMA((n,))],
        compiler_params=pltpu.CompilerParams(has_side_effects=pltpu.SideEffectType.DATAFLOW_SIDE_EFFECTING))(
            *operands, after)
    return out[0], out[1], out[2:2 + n], out[2 + n:2 + 2 * n], out[-1]


def _exchange_wait(started, modes, after, name):
    send_sems, recv_sems, sent, lands, _ = started
    n = len(sent)

    def body(*refs):
        ins, zones = refs[:n], refs[n:2 * n]
        send_ref, recv_ref = refs[2 * n], refs[2 * n + 1]
        me, peers = _peer_table()

        def src(k, slot):
            return ins[k] if modes[k] == "gather" else ins[k].at[slot]

        for m in range(N_DEV - 1):
            for k in range(n):
                cp = pltpu.make_async_remote_copy(
                    src_ref=src(k, peers[m][1]), dst_ref=zones[k].at[peers[m][1]], send_sem=send_ref,
                    recv_sem=recv_ref, device_id=peers[m][0], device_id_type=pl.DeviceIdType.MESH)
                cp.wait_send()
                cp.wait_recv()

    hbm = pl.BlockSpec(memory_space=pltpu.HBM)
    sem = pl.BlockSpec(memory_space=pltpu.SEMAPHORE)
    out = pl.pallas_call(
        body, name=name,
        out_shape=tuple(pltpu.HBM(a.shape, a.dtype) for a in sent) + tuple(pltpu.HBM(a.shape, a.dtype) for a in lands),
        in_specs=[hbm] * (2 * n) + [sem, sem, pl.BlockSpec(memory_space=pl.ANY)], out_specs=(hbm,) * (2 * n),
        input_output_aliases={i: i for i in range(2 * n)},
        compiler_params=pltpu.CompilerParams(has_side_effects=pltpu.SideEffectType.DATAFLOW_SIDE_EFFECTING))(
            *sent, *lands, send_sems, recv_sems, after)
    return out[n:]


W_IN_SPLITS = (0, 2048, 6144, 6176, 10272, 12320, 12336, 12352, 13376, 14400)
N_REPLICATED = 16640
REPLICATED = ("b_ada", "norm_mix_pre", "norm_mix_post", "ssm_conv_b", "ssm_dt_bias", "ssm_A_log", "ssm_D",
              "ssm_norm_w", "gdn_dt_bias", "gdn_A_log", "gdn_norm_w", "norm_mlp_pre", "norm_mlp_post")
WEIGHTS = ("w_ada", "b_ada", "norm_mix_pre", "norm_mix_post", "w_in", "ssm_conv_w", "ssm_conv_b", "ssm_dt_bias",
           "ssm_A_log", "ssm_D", "ssm_norm_w", "gdn_conv_w", "gdn_dt_bias", "gdn_A_log", "gdn_norm_w", "w_ssm_up",
           "w_gdn_up", "w_out", "norm_mlp_pre", "norm_mlp_post", "w_mlp_up", "w_mlp_down")


def _by_cols(t):
    return t.transpose(1, 0, 2).reshape(t.shape[1], N_DEV * t.shape[2])


def _to_col_shards(t):
    R, C8 = t.shape
    return t.reshape(R, N_DEV, C8 // N_DEV).transpose(1, 0, 2)


def _heads_first(t, groups):
    S = t.shape[0]
    return t.reshape(S, groups, t.shape[1] // groups).transpose(1, 0, 2)


def _heads_last(t):
    return t.transpose(1, 0, 2).reshape(t.shape[1], t.shape[0] * t.shape[2])


def kernel(x, c, w_ada, b_ada, norm_mix_pre, norm_mix_post, w_in, ssm_conv_w, ssm_conv_b, ssm_dt_bias, ssm_A_log, ssm_D, ssm_norm_w, gdn_conv_w, gdn_dt_bias, gdn_A_log, gdn_norm_w, w_ssm_up, w_gdn_up, w_out, norm_mlp_pre, norm_mlp_post, w_mlp_up, w_mlp_down, loss_target, m_w_ada, m_b_ada, m_norm_mix_pre, m_norm_mix_post, m_w_in, m_ssm_conv_w, m_ssm_conv_b, m_ssm_dt_bias, m_ssm_A_log, m_ssm_D, m_ssm_norm_w, m_gdn_conv_w, m_gdn_dt_bias, m_gdn_A_log, m_gdn_norm_w, m_w_ssm_up, m_w_gdn_up, m_w_out, m_norm_mlp_pre, m_norm_mlp_post, m_w_mlp_up, m_w_mlp_down, v_w_ada, v_b_ada, v_norm_mix_pre, v_norm_mix_post, v_w_in, v_ssm_conv_w, v_ssm_conv_b, v_ssm_dt_bias, v_ssm_A_log, v_ssm_D, v_ssm_norm_w, v_gdn_conv_w, v_gdn_dt_bias, v_gdn_A_log, v_gdn_norm_w, v_w_ssm_up, v_w_gdn_up, v_w_out, v_norm_mlp_pre, v_norm_mlp_post, v_w_mlp_up, v_w_mlp_down):
    S, Dm = x.shape[1], D_MODEL
    me = 4 * lax.axis_index("x") + 2 * lax.axis_index("y") + lax.axis_index("c")
    x2, tgt = x[0], loss_target[0]
    n_ada = w_ada.shape[2]
    given = dict(
        w_ada=(w_ada, m_w_ada, v_w_ada), b_ada=(b_ada, m_b_ada, v_b_ada),
        norm_mix_pre=(norm_mix_pre, m_norm_mix_pre, v_norm_mix_pre),
        norm_mix_post=(norm_mix_post, m_norm_mix_post, v_norm_mix_post), w_in=(w_in, m_w_in, v_w_in),
        ssm_conv_w=(ssm_conv_w, m_ssm_conv_w, v_ssm_conv_w), ssm_conv_b=(ssm_conv_b, m_ssm_conv_b, v_ssm_conv_b),
        ssm_dt_bias=(ssm_dt_bias, m_ssm_dt_bias, v_ssm_dt_bias), ssm_A_log=(ssm_A_log, m_ssm_A_log, v_ssm_A_log),
        ssm_D=(ssm_D, m_ssm_D, v_ssm_D), ssm_norm_w=(ssm_norm_w, m_ssm_norm_w, v_ssm_norm_w),
        gdn_conv_w=(gdn_conv_w, m_gdn_conv_w, v_gdn_conv_w), gdn_dt_bias=(gdn_dt_bias, m_gdn_dt_bias, v_gdn_dt_bias),
        gdn_A_log=(gdn_A_log, m_gdn_A_log, v_gdn_A_log), gdn_norm_w=(gdn_norm_w, m_gdn_norm_w, v_gdn_norm_w),
        w_ssm_up=(w_ssm_up, m_w_ssm_up, v_w_ssm_up), w_gdn_up=(w_gdn_up, m_w_gdn_up, v_w_gdn_up),
        w_out=(w_out, m_w_out, v_w_out), norm_mlp_pre=(norm_mlp_pre, m_norm_mlp_pre, v_norm_mlp_pre),
        norm_mlp_post=(norm_mlp_post, m_norm_mlp_post, v_norm_mlp_post), w_mlp_up=(w_mlp_up, m_w_mlp_up, v_w_mlp_up),
        w_mlp_down=(w_mlp_down, m_w_mlp_down, v_w_mlp_down))

    (c_all, scw, gcw, g_in) = _exchange([c, ssm_conv_w[0], gdn_conv_w[0], w_in[0].astype(BF16)], ["gather"] * 4,
                                        "gather_w_in")
    c_all = c_all.reshape(N_DEV, Dm)
    wf = _by_cols(g_in)
    sp = W_IN_SPLITS
    w_main = jnp.concatenate([wf[:, sp[0]:sp[2]], wf[:, sp[3]:sp[5]], wf[:, sp[7]:sp[9]]], axis=1)
    w_small = jnp.concatenate([wf[:, sp[2]:sp[3]], wf[:, sp[5]:sp[7]], jnp.zeros((Dm, N_SMALL - 64), BF16)], axis=1)
    conv_w = jnp.concatenate([_by_cols(scw), _by_cols(gcw)], axis=1)
    conv_b = jnp.concatenate([ssm_conv_b, jnp.zeros_like(ssm_conv_b)], axis=1)

    b_loc = lax.dynamic_slice(b_ada, (0, me * n_ada), (1, n_ada))
    mod_part = _ada_fwd(c_all, w_ada[0], b_loc, "ada_fwd")
    (mod_rows,) = _exchange([mod_part.reshape(N_DEV, 1, n_ada)], ["a2a"], "exchange_mod")
    rest = _exchange_start([w_ssm_up[0].astype(BF16), w_gdn_up[0].astype(BF16), w_out[0].astype(BF16),
                            w_mlp_up[0].astype(BF16), w_mlp_down[0].astype(BF16)], ["gather"] * 5, mod_rows,
                           "gather_rest_start")
    mod = mod_rows.reshape(1, 6 * Dm) + rest[4][0:1, 0:1]
    sh1, sc1, g1, sh2, sc2, g2 = [mod[:, i * Dm:(i + 1) * Dm] for i in range(6)]

    h = _pre_fwd(x2, norm_mix_pre, sc1, sh1, "pre_mix")
    proj = _mm(h, w_main, S, N_MAIN, Dm, mode="nn", out_dtype=F32, name="proj_main")
    small = _mm(h, w_small, S, N_SMALL, Dm, mode="nn", out_dtype=F32, name="proj_small")
    conv = _conv_fwd(proj, conv_w, conv_b, "conv_fwd")
    dt_g, b_g, a_g = _heads_first(small[:, 0:32], 8), _heads_first(small[:, 32:48], 8), _heads_first(small[:, 48:64], 8)
    pv_ssm = jnp.stack([ssm_dt_bias.reshape(8, 4), ssm_A_log.reshape(8, 4), ssm_D.reshape(8, 4)], axis=1)
    nw_ssm = ssm_norm_w.reshape(8, 1, SSM_GROUP_WIDTH)
    pv_gdn = jnp.stack([gdn_dt_bias.reshape(8, 2), gdn_A_log.reshape(8, 2)], axis=1)
    y_ssd, ysn, st_ssm = _ssd_fwd(conv, proj, dt_g, pv_ssm, nw_ssm, "ssd_fwd")
    o_pre, ogn, st_gdn, t_inv = _gdn_fwd(conv, proj, b_g, a_g, pv_gdn, gdn_norm_w, "gdn_fwd")
    g_su, g_gu, g_out, g_mu, g_md = _exchange_wait(rest, ["gather"] * 5, ogn, "gather_rest_wait")
    w_su, w_gu = g_su.reshape(2 * Dm, Dm), g_gu.reshape(2 * Dm, Dm)
    w_o, w_mu, w_md = g_out.reshape(Dm, Dm), _by_cols(g_mu), g_md.reshape(4 * Dm, Dm)
    ys = _mm(ysn, w_su, S, Dm, 2 * Dm, mode="nn", out_dtype=F32, name="ssm_up")
    yg = _mm(ogn, w_gu, S, Dm, 2 * Dm, mode="nn", out_dtype=F32, name="gdn_up")
    merged = _merge_fwd(ys, yg, proj, "merge_fwd")
    mo = _mm(merged, w_o, S, Dm, Dm, mode="nn", out_dtype=F32, name="mix_out")
    x1 = _post_fwd(x2, mo, norm_mix_post, g1, "post_mix")
    h2 = _pre_fwd(x1, norm_mlp_pre, sc2, sh2, "pre_mlp")
    u, act = _mm(h2, w_mu, S, 4 * Dm, Dm, mode="nn", out_dtype=F32, epi="relu2", name="mlp_up")
    y_mlp = _mm(act, w_md, S, Dm, 4 * Dm, mode="nn", out_dtype=F32, name="mlp_down")
    dx2, loss_loc = _final_fwd(x1, y_mlp, norm_mlp_post, g2, tgt, "post_mlp_loss")

    dy, dg2, dw_post2 = _post_bwd(dx2, y_mlp, norm_mlp_post, g2, "post_mlp_bwd")
    du = _mm(dy, w_md, S, 4 * Dm, Dm, mode="nt", out_dtype=BF16, epi="drelu2", extra=u, name="mlp_down_dx")
    gw_md = _mm(act, dy, 4 * Dm, Dm, S, mode="tn", out_dtype=BF16, name="mlp_down_dw")
    dh2 = _mm(du, w_mu, S, Dm, 4 * Dm, mode="nt", out_dtype=F32, name="mlp_up_dx")
    gw_mu = _mm(h2, du, Dm, 4 * Dm, S, mode="tn", out_dtype=BF16, name="mlp_up_dw")
    mlp_x = _exchange_start([_to_col_shards(gw_mu), gw_md.reshape(N_DEV, -1, Dm)], ["a2a"] * 2, gw_md,
                            "grads_mlp_start")
    dx1, dsh2, dsc2, dw_pre2 = _pre_bwd(dh2, x1, norm_mlp_pre, sc2 + mlp_x[4][0:1, 0:1], dx2, "pre_mlp_bwd")
    dmo, dg1, dw_post1 = _post_bwd(dx1, mo, norm_mix_post, g1, "post_mix_bwd")
    dmerged = _mm(dmo, w_o, S, Dm, Dm, mode="nt", out_dtype=F32, name="mix_out_dx")
    gw_o = _mm(merged, dmo, Dm, Dm, S, mode="tn", out_dtype=BF16, name="mix_out_dw")
    dys, dyg, dgate = _merge_bwd(dmerged, ys, yg, proj, "merge_bwd")
    dysn = _mm(dys, w_su, S, 2 * Dm, Dm, mode="nt", out_dtype=F32, name="ssm_up_dx")
    gw_su = _mm(ysn, dys, 2 * Dm, Dm, S, mode="tn", out_dtype=BF16, name="ssm_up_dw")
    dogn = _mm(dyg, w_gu, S, 2 * Dm, Dm, mode="nt", out_dtype=F32, name="gdn_up_dx")
    gw_gu = _mm(ogn, dyg, 2 * Dm, Dm, S, mode="tn", out_dtype=BF16, name="gdn_up_dw")
    mix_x = _exchange_start([gw_su.reshape(N_DEV, -1, Dm), gw_gu.reshape(N_DEV, -1, Dm), gw_o.reshape(N_DEV, -1, Dm)],
                            ["a2a"] * 3, gw_gu, "grads_mix_start")
    dxs, dBm, dCm, dz_s, ddt_g, dpv_ssm, dnw_ssm = _ssd_bwd(dysn, conv, proj, dt_g, pv_ssm + mix_x[4][0, 0], nw_ssm,
                                                            y_ssd, st_ssm, "ssd_bwd")
    dq, dk, dv, dz_g, db_g, da_g, dpv_gdn, dnw_gdn = _gdn_bwd(dogn, conv, proj, b_g, a_g, pv_gdn, gdn_norm_w, o_pre,
                                                              st_gdn, t_inv, "gdn_bwd")
    conv_pieces = []
    for nm, d_act, col0 in (("xs", dxs, 0), ("B", dBm, 2048), ("C", dCm, 3072), ("q", dq, 4096), ("k", dk, 5120),
                            ("v", dv, 6144)):
        conv_pieces.append(_conv_bwd(d_act, proj, conv_w, conv_b, col0, "conv_bwd_" + nm) + (col0,))
    d_small = jnp.concatenate([_heads_last(ddt_g), _heads_last(db_g), _heads_last(da_g),
                               jnp.zeros((S, N_SMALL - 64), F32)], axis=1).astype(BF16)
    pieces = [(dz_s, C_ZS)] + [(p[0], C_XBC + p[3]) for p in conv_pieces] + [(dz_g, C_ZG), (dgate, C_GS)]
    gw_small = _mm(h, d_small, Dm, N_SMALL, S, mode="tn", out_dtype=BF16, name="proj_small_dw")
    gw_cols = [_mm(h, d_piece, Dm, d_piece.shape[1], S, mode="tn", out_dtype=BF16, name="proj_dw_%d" % off)
               for d_piece, off in pieces]
    main_cols = jnp.concatenate(gw_cols, axis=1)
    gw_in = jnp.concatenate([main_cols[:, 0:C_QKV], gw_small[:, 0:32], main_cols[:, C_QKV:C_GS], gw_small[:, 32:64],
                             main_cols[:, C_GS:N_MAIN]], axis=1)
    in_x = _exchange_start([_to_col_shards(gw_in)], ["a2a"], gw_small, "grads_w_in_start")
    dh = _mm(d_small, w_small + in_x[4][0:1, 0:1].astype(BF16), S, Dm, N_SMALL, mode="nt", out_dtype=F32,
             name="proj_small_dx")
    for d_piece, off in pieces:
        dh = _mm(d_piece, w_main, S, Dm, d_piece.shape[1], mode="nt", out_dtype=F32, b_off=(0, off), add=dh,
                 name="proj_dx_%d" % off)
    dx, dsh1, dsc1, dw_pre1 = _pre_bwd(dh, x2, norm_mix_pre, sc1, dx1, "pre_mix_bwd")
    r_mu, r_md = _exchange_wait(mlp_x, ["a2a"] * 2, dx, "grads_mlp_wait")
    r_su, r_gu, r_o = _exchange_wait(mix_x, ["a2a"] * 3, dx, "grads_mix_wait")

    dconv_w = jnp.concatenate([p[1] for p in conv_pieces], axis=1)
    dconv_b = jnp.concatenate([p[2] for p in conv_pieces[:3]], axis=1)
    dmod = jnp.concatenate([dsh1, dsc1, dg1, dsh2, dsc2, dg2], axis=1)
    small_vec = jnp.concatenate(
        [dmod, dw_pre1, dw_post1, dconv_b, dpv_ssm[:, 0].reshape(1, 32), dpv_ssm[:, 1].reshape(1, 32),
         dpv_ssm[:, 2].reshape(1, 32), dnw_ssm.reshape(1, 2048), dpv_gdn[:, 0].reshape(1, 16),
         dpv_gdn[:, 1].reshape(1, 16), jnp.sum(dnw_gdn, axis=0), dw_pre2, dw_post2, dconv_w.reshape(1, -1)], axis=1)
    n_vec = small_vec.shape[1]
    small_vec = jnp.pad(small_vec, ((0, 0), (0, (-n_vec) % 1024))).reshape(-1, 1024)
    (small_all,) = _exchange([small_vec], ["gather"], "gather_small_grads")
    small_all = small_all.reshape(N_DEV, -1)
    dmod_cols = lax.dynamic_slice(small_all, (0, me * n_ada), (N_DEV, n_ada))
    gw_ada = _ada_bwd(c_all.T, dmod_cols, "ada_bwd")
    conv_all = small_all[:, N_REPLICATED:n_vec].reshape(N_DEV, CONV_K, 2 * N_DEV * 512)
    conv_contrib = jnp.concatenate(
        [lax.dynamic_slice(conv_all, (0, 0, me * 512), (N_DEV, CONV_K, 512)),
         lax.dynamic_slice(conv_all, (0, 0, N_DEV * 512 + me * 512), (N_DEV, CONV_K, 512))], axis=1)
    rep_contrib = small_all[:, :N_REPLICATED].reshape(N_DEV, N_REPLICATED // 128, 128)

    results = {}

    def adam_big(nm, contrib):
        w3 = given[nm]
        res = _adam(contrib, w3[0][0], w3[1][0], w3[2][0], "adam_" + nm)
        results[nm] = tuple(r.reshape(w3[0].shape) for r in res)

    adam_big("w_ada", gw_ada[None])
    adam_big("w_ssm_up", r_su)
    adam_big("w_gdn_up", r_gu)
    adam_big("w_out", r_o)
    adam_big("w_mlp_up", r_mu)
    adam_big("w_mlp_down", r_md)
    (r_in,) = _exchange_wait(in_x, ["a2a"], results["w_mlp_down"][0], "grads_w_in_wait")
    adam_big("w_in", r_in)
    packed = [jnp.concatenate([given[nm][i] for nm in REPLICATED], axis=1).reshape(N_REPLICATED // 128, 128)
              for i in range(3)]
    rep_res = _adam(rep_contrib, packed[0], packed[1], packed[2], "adam_replicated")
    pos = 0
    for nm in REPLICATED:
        size = given[nm][0].shape[1]
        results[nm] = tuple(r.reshape(1, N_REPLICATED)[:, pos:pos + size] for r in rep_res)
        pos += size
    conv_wmv = [jnp.concatenate([given["ssm_conv_w"][i][0], given["gdn_conv_w"][i][0]], axis=0) for i in range(3)]
    conv_res = _adam(conv_contrib, conv_wmv[0], conv_wmv[1], conv_wmv[2], "adam_conv_w")
    results["ssm_conv_w"] = tuple(r[None, :CONV_K] for r in conv_res)
    results["gdn_conv_w"] = tuple(r[None, CONV_K:] for r in conv_res)

    loss = lax.psum(loss_loc[0, 0], ("x", "y", "c"))
    return (loss, dx[None]) + tuple(results[nm][i] for i in range(4) for nm in WEIGHTS)
```

```python
import jax
import jax.numpy as jnp
from jax import lax
from jax.experimental import pallas as pl
from jax.experimental.pallas import tpu as pltpu

F32 = jnp.float32
BF16 = jnp.bfloat16
N_DEV = 8
D_MODEL = 1024
EPS = 1e-6
CONV_K = 4
SSM_CHUNK = 128
SSM_HEAD_DIM = 64
SSM_D_STATE = 128
SSM_GROUPS = 8
SSM_HEADS_PER_GROUP = 4
SSM_GROUP_WIDTH = SSM_HEADS_PER_GROUP * SSM_HEAD_DIM
SSM_GROUPS_PER_STEP = 2
GDN_CHUNK = 64
GDN_HEAD = 128
GDN_QK_HEADS = 8
GDN_V_PER_QK = 2
GDN_QK_PER_STEP = 4
GDN_INV_BLOCK = 16
C_ZS, C_XBC, C_QKV, C_ZG, C_GS, C_GG, N_MAIN = 0, 2048, 6144, 10240, 12288, 13312, 14336
N_SMALL = 128
ADAM_LR, ADAM_B1, ADAM_B2, ADAM_EPS, ADAM_WD, ADAM_STEP = 0.001, 0.9, 0.999, 1e-08, 0.01, 10
VMEM_LIMIT = 56 * 1024 * 1024
NEG_INF = float("-inf")

_NT = (((1,), (1,)), ((), ()))
_NN = (((1,), (0,)), ((), ()))
_TN = (((0,), (0,)), ((), ()))


def _params(*sem):
    return pltpu.CompilerParams(dimension_semantics=sem, vmem_limit_bytes=VMEM_LIMIT)


def _dot(a, b, dims=_NN):
    return lax.dot_general(a.astype(BF16), b.astype(BF16), dims, preferred_element_type=F32)


def _split(a):
    hi = a.astype(BF16)
    return hi, (a - hi.astype(F32)).astype(BF16)


def _dot3(a, b, dims=_NN):
    ah, al = _split(a)
    bh, bl = _split(b)
    d = lambda u, v: lax.dot_general(u, v, dims, preferred_element_type=F32)
    return d(ah, bh) + (d(ah, bl) + d(al, bh))


def _sigmoid(x):
    return 1.0 / (1.0 + jnp.exp(-x))


def _silu(x):
    return x * _sigmoid(x)


def _dsilu(x):
    s = _sigmoid(x)
    return s * (1.0 + x * (1.0 - s))


def _softplus(x):
    return jnp.maximum(x, 0.0) + jnp.log1p(jnp.exp(-jnp.abs(x)))


def _iota(n, m, d):
    return lax.broadcasted_iota(jnp.int32, (n, m), d)


def _rowsum(x):
    return jnp.sum(x, axis=1, keepdims=True)


def _colsum(x):
    return jnp.sum(x, axis=0, keepdims=True)


def _total(x):
    return _rowsum(_colsum(x))


MXU_LANES = 128


def _parts(x, n):
    out = []
    for _ in range(n):
        p = x.astype(BF16)
        out.append(p)
        x = x - p.astype(F32)
    return out


def _sum_by(m01, x, dims=_NN, n=3):
    return sum(lax.dot_general(m01, p, dims, preferred_element_type=F32) for p in _parts(x, n))


def _row_col_sums(q):
    ones = jnp.ones((q.shape[0], MXU_LANES), BF16)
    acc = 0.0
    for p in _parts(q, 2):
        acc = acc + (lax.dot_general(p, ones, _NN, preferred_element_type=F32)
                     - lax.dot_general(p, ones, _TN, preferred_element_type=F32))
    return acc[:, 0:1]


def _cumsum_forms(col, ii, jj):
    lower = jnp.where(ii >= jj, 1.0, 0.0).astype(BF16)
    cum_col = _sum_by(lower, jnp.broadcast_to(col, (col.shape[0], MXU_LANES)))[:, 0:1]
    cum_row = _colsum(jnp.where(ii <= jj, col, 0.0))
    return cum_col, cum_row


def _rev_cumsum_col(col, ii, jj):
    upper = jnp.where(ii <= jj, 1.0, 0.0).astype(BF16)
    return _sum_by(upper, jnp.broadcast_to(col, (col.shape[0], MXU_LANES)))[:, 0:1]


def _blk(dim, pref):
    return pref if dim % pref == 0 else dim


def _lockstep(gens):
    gens = list(gens)
    while gens:
        alive = []
        for g in gens:
            try:
                next(g)
                alive.append(g)
            except StopIteration:
                pass
        gens = alive


def _mm(a, b, M, N, K, *, mode, out_dtype, name, a_off=(0, 0), b_off=(0, 0), add=None, epi=None, extra=None,
        tm=1024, tn=1024, tk=1024):
    tm, tn, tk = _blk(M, tm), _blk(N, tn), _blk(K, tk)
    nk = K // tk
    if mode == "tn":
        a_spec = pl.BlockSpec((tk, tm), lambda i, j, k: (k + a_off[0] // tk, i + a_off[1] // tm))
        assert a_off[0] % tk == 0 and a_off[1] % tm == 0
    else:
        a_spec = pl.BlockSpec((tm, tk), lambda i, j, k: (i + a_off[0] // tm, k + a_off[1] // tk))
        assert a_off[0] % tm == 0 and a_off[1] % tk == 0
    if mode == "nt":
        b_spec = pl.BlockSpec((tn, tk), lambda i, j, k: (j + b_off[0] // tn, k + b_off[1] // tk))
        assert b_off[0] % tn == 0 and b_off[1] % tk == 0
    else:
        b_spec = pl.BlockSpec((tk, tn), lambda i, j, k: (k + b_off[0] // tk, j + b_off[1] // tn))
        assert b_off[0] % tk == 0 and b_off[1] % tn == 0
    dims = {"nn": _NN, "nt": _NT, "tn": _TN}[mode]
    o_spec = pl.BlockSpec((tm, tn), lambda i, j, k: (i, j))
    ins, in_specs = [a, b], [a_spec, b_spec]
    if add is not None:
        ins.append(add)
        in_specs.append(o_spec)
    if extra is not None:
        ins.append(extra)
        in_specs.append(o_spec)
    n_in = len(ins)
    if epi == "relu2":
        out_shape = (jax.ShapeDtypeStruct((M, N), F32), jax.ShapeDtypeStruct((M, N), BF16))
        out_specs = (o_spec, o_spec)
    else:
        out_shape = jax.ShapeDtypeStruct((M, N), out_dtype)
        out_specs = o_spec

    def body(*refs):
        a_ref, b_ref = refs[0], refs[1]
        acc = refs[-1]
        outs = refs[n_in:-1]
        k = pl.program_id(2)

        @pl.when(k == 0)
        def _():
            acc[...] = jnp.zeros_like(acc)

        acc[...] += _dot(a_ref[...], b_ref[...], dims)

        @pl.when(k == nk - 1)
        def _():
            r = acc[...]
            pos = 2
            if add is not None:
                r = r + refs[pos][...]
                pos += 1
            if epi == "relu2":
                outs[0][...] = r
                p = jnp.maximum(r, 0.0)
                outs[1][...] = (p * p).astype(BF16)
            elif epi == "drelu2":
                outs[0][...] = (r * (2.0 * jnp.maximum(refs[pos][...], 0.0))).astype(out_dtype)
            else:
                outs[0][...] = r.astype(out_dtype)

    return pl.pallas_call(
        body, name=name, grid=(M // tm, N // tn, nk), in_specs=in_specs, out_specs=out_specs, out_shape=out_shape,
        scratch_shapes=[pltpu.VMEM((tm, tn), F32)],
        compiler_params=_params("parallel", "parallel", "arbitrary"))(*ins)


def _row_spec(tb, d):
    return pl.BlockSpec((tb, d), lambda i: (i, 0))


def _vec_spec(d):
    return pl.BlockSpec((1, d), lambda i: (0, 0))


def _pre_fwd(x, w, sc, sh, name):
    S, Dm = x.shape
    tb = _blk(S, 512)

    def body(x_ref, w_ref, sc_ref, sh_ref, h_ref):
        xv = x_ref[...]
        r = lax.rsqrt(jnp.mean(xv * xv, axis=-1, keepdims=True) + EPS)
        h_ref[...] = ((xv * r * w_ref[...]) * (1.0 + sc_ref[...]) + sh_ref[...]).astype(BF16)

    return pl.pallas_call(
        body, name=name, grid=(S // tb,), in_specs=[_row_spec(tb, Dm)] + [_vec_spec(Dm)] * 3,
        out_specs=_row_spec(tb, Dm), out_shape=jax.ShapeDtypeStruct((S, Dm), BF16),
        compiler_params=_params("parallel"))(x, w, sc, sh)


def _post_fwd(x, y, w, g, name):
    S, Dm = x.shape
    tb = _blk(S, 512)

    def body(x_ref, y_ref, w_ref, g_ref, o_ref):
        yv = y_ref[...]
        r = lax.rsqrt(jnp.mean(yv * yv, axis=-1, keepdims=True) + EPS)
        o_ref[...] = x_ref[...] + g_ref[...] * (yv * r * w_ref[...])

    return pl.pallas_call(
        body, name=name, grid=(S // tb,), in_specs=[_row_spec(tb, Dm)] * 2 + [_vec_spec(Dm)] * 2,
        out_specs=_row_spec(tb, Dm), out_shape=jax.ShapeDtypeStruct((S, Dm), F32),
        compiler_params=_params("parallel"))(x, y, w, g)


def _final_fwd(x, y, w, g, target, name):
    S, Dm = x.shape
    tb = _blk(S, 512)
    nb = S // tb

    def body(x_ref, y_ref, w_ref, g_ref, t_ref, dx_ref, loss_ref, acc):
        i = pl.program_id(0)

        @pl.when(i == 0)
        def _():
            acc[...] = jnp.zeros_like(acc)

        yv = y_ref[...]
        r = lax.rsqrt(jnp.mean(yv * yv, axis=-1, keepdims=True) + EPS)
        e = (x_ref[...] + g_ref[...] * (yv * r * w_ref[...])) - t_ref[...]
        dx_ref[...] = e * (1.0 / Dm)
        acc[...] += _colsum(e * e)

        @pl.when(i == nb - 1)
        def _():
            loss_ref[...] = (0.5 / Dm) * _rowsum(acc[...])

    return pl.pallas_call(
        body, name=name, grid=(nb,), in_specs=[_row_spec(tb, Dm)] * 2 + [_vec_spec(Dm)] * 2 + [_row_spec(tb, Dm)],
        out_specs=(_row_spec(tb, Dm), pl.BlockSpec((1, 1), lambda i: (0, 0))),
        out_shape=(jax.ShapeDtypeStruct((S, Dm), F32), jax.ShapeDtypeStruct((1, 1), F32)),
        scratch_shapes=[pltpu.VMEM((1, Dm), F32)], compiler_params=_params("arbitrary"))(x, y, w, g, target)


def _post_bwd(dxo, y, w, g, name):
    S, Dm = y.shape
    tb = _blk(S, 512)

    def body(d_ref, y_ref, w_ref, g_ref, dy_ref, dg_ref, dw_ref):
        i = pl.program_id(0)

        @pl.when(i == 0)
        def _():
            dg_ref[...] = jnp.zeros_like(dg_ref)
            dw_ref[...] = jnp.zeros_like(dw_ref)

        yv, dv = y_ref[...], d_ref[...]
        r = lax.rsqrt(jnp.mean(yv * yv, axis=-1, keepdims=True) + EPS)
        yh = yv * r
        dg_ref[...] += _colsum(dv * (yh * w_ref[...]))
        dn = dv * g_ref[...]
        dw_ref[...] += _colsum(dn * yh)
        dyh = dn * w_ref[...]
        dy_ref[...] = (r * (dyh - yh * jnp.mean(dyh * yh, axis=-1, keepdims=True))).astype(BF16)

    return pl.pallas_call(
        body, name=name, grid=(S // tb,), in_specs=[_row_spec(tb, Dm)] * 2 + [_vec_spec(Dm)] * 2,
        out_specs=(_row_spec(tb, Dm), _vec_spec(Dm), _vec_spec(Dm)),
        out_shape=(jax.ShapeDtypeStruct((S, Dm), BF16), jax.ShapeDtypeStruct((1, Dm), F32),
                   jax.ShapeDtypeStruct((1, Dm), F32)),
        compiler_params=_params("arbitrary"))(dxo, y, w, g)


def _pre_bwd(dh, x, w, sc, dres, name):
    S, Dm = x.shape
    tb = _blk(S, 512)

    def body(dh_ref, x_ref, w_ref, sc_ref, dr_ref, dx_ref, dsh_ref, dsc_ref, dw_ref):
        i = pl.program_id(0)

        @pl.when(i == 0)
        def _():
            dsh_ref[...] = jnp.zeros_like(dsh_ref)
            dsc_ref[...] = jnp.zeros_like(dsc_ref)
            dw_ref[...] = jnp.zeros_like(dw_ref)

        xv, dv = x_ref[...], dh_ref[...]
        r = lax.rsqrt(jnp.mean(xv * xv, axis=-1, keepdims=True) + EPS)
        xh = xv * r
        one_sc = 1.0 + sc_ref[...]
        dsh_ref[...] += _colsum(dv)
        dsc_ref[...] += _colsum(dv * (xh * w_ref[...]))
        dw_ref[...] += _colsum(dv * one_sc * xh)
        dxh = dv * one_sc * w_ref[...]
        dx_ref[...] = dr_ref[...] + r * (dxh - xh * jnp.mean(dxh * xh, axis=-1, keepdims=True))

    vec = jax.ShapeDtypeStruct((1, Dm), F32)
    return pl.pallas_call(
        body, name=name, grid=(S // tb,),
        in_specs=[_row_spec(tb, Dm)] * 2 + [_vec_spec(Dm)] * 2 + [_row_spec(tb, Dm)],
        out_specs=(_row_spec(tb, Dm), _vec_spec(Dm), _vec_spec(Dm), _vec_spec(Dm)),
        out_shape=(jax.ShapeDtypeStruct((S, Dm), F32), vec, vec, vec),
        compiler_params=_params("arbitrary"))(dh, x, w, sc, dres)


def _merge_fwd(ys, yg, proj, name):
    S, Dm = ys.shape
    tb = _blk(S, 512)

    def body(ys_ref, yg_ref, gs_ref, gg_ref, o_ref):
        o_ref[...] = (_sigmoid(gs_ref[...]) * ys_ref[...] + _sigmoid(gg_ref[...]) * yg_ref[...]).astype(BF16)

    return pl.pallas_call(
        body, name=name, grid=(S // tb,),
        in_specs=[_row_spec(tb, Dm)] * 2 + [pl.BlockSpec((tb, Dm), lambda i: (i, C_GS // Dm)),
                                            pl.BlockSpec((tb, Dm), lambda i: (i, C_GG // Dm))],
        out_specs=_row_spec(tb, Dm), out_shape=jax.ShapeDtypeStruct((S, Dm), BF16),
        compiler_params=_params("parallel"))(ys, yg, proj, proj)


def _merge_bwd(dm, ys, yg, proj, name):
    S, Dm = ys.shape
    tb = _blk(S, 512)

    def body(dm_ref, ys_ref, yg_ref, gs_ref, gg_ref, dys_ref, dyg_ref, dgate_ref):
        d = dm_ref[...]
        ss, sg = _sigmoid(gs_ref[...]), _sigmoid(gg_ref[...])
        dys_ref[...] = (d * ss).astype(BF16)
        dyg_ref[...] = (d * sg).astype(BF16)
        dgate_ref[:, :Dm] = (d * ys_ref[...] * ss * (1.0 - ss)).astype(BF16)
        dgate_ref[:, Dm:] = (d * yg_ref[...] * sg * (1.0 - sg)).astype(BF16)

    return pl.pallas_call(
        body, name=name, grid=(S // tb,),
        in_specs=[_row_spec(tb, Dm)] * 3 + [pl.BlockSpec((tb, Dm), lambda i: (i, C_GS // Dm)),
                                            pl.BlockSpec((tb, Dm), lambda i: (i, C_GG // Dm))],
        out_specs=(_row_spec(tb, Dm), _row_spec(tb, Dm), _row_spec(tb, 2 * Dm)),
        out_shape=(jax.ShapeDtypeStruct((S, Dm), BF16), jax.ShapeDtypeStruct((S, Dm), BF16),
                   jax.ShapeDtypeStruct((S, 2 * Dm), BF16)),
        compiler_params=_params("parallel"))(dm, ys, yg, proj, proj)


CONV_COLS = 128


def _shift_down(x, k, rows):
    return jnp.where(rows >= k, pltpu.roll(x, k, 0), 0.0)


def _shift_up(x, k, rows, S):
    return jnp.where(rows < S - k, pltpu.roll(x, S - k, 0), 0.0)


def _conv_fwd(proj, w, b, name):
    S = proj.shape[0]
    n = w.shape[1]
    cb = CONV_COLS

    def body(x_ref, w_ref, b_ref, o_ref):
        x = x_ref[...]
        rows = _iota(S, cb, 0)
        pre = x * w_ref[CONV_K - 1:CONV_K, :] + b_ref[...]
        for k in range(1, CONV_K):
            pre = pre + _shift_down(x, k, rows) * w_ref[CONV_K - 1 - k:CONV_K - k, :]
        o_ref[...] = _silu(pre)

    return pl.pallas_call(
        body, name=name, grid=(n // cb,),
        in_specs=[pl.BlockSpec((S, cb), lambda j: (0, j + C_XBC // cb)), pl.BlockSpec((CONV_K, cb), lambda j: (0, j)),
                  pl.BlockSpec((1, cb), lambda j: (0, j))],
        out_specs=pl.BlockSpec((S, cb), lambda j: (0, j)), out_shape=jax.ShapeDtypeStruct((S, n), F32),
        compiler_params=_params("parallel"))(proj, w, b)


def _conv_bwd(dact, proj, w, b, col0, name):
    S, n = dact.shape
    cb = CONV_COLS
    o = col0 // cb

    def body(d_ref, x_ref, w_ref, b_ref, dx_ref, dw_ref, db_ref):
        x = x_ref[...]
        rows = _iota(S, cb, 0)
        xs = [x] + [_shift_down(x, k, rows) for k in range(1, CONV_K)]
        pre = xs[0] * w_ref[CONV_K - 1:CONV_K, :] + b_ref[...]
        for k in range(1, CONV_K):
            pre = pre + xs[k] * w_ref[CONV_K - 1 - k:CONV_K - k, :]
        dpre = d_ref[...] * _dsilu(pre)
        db_ref[...] = _colsum(dpre)
        dx = dpre * w_ref[CONV_K - 1:CONV_K, :]
        for k in range(CONV_K):
            dw_ref[CONV_K - 1 - k:CONV_K - k, :] = _colsum(dpre * xs[k])
            if k:
                dx = dx + _shift_up(dpre, k, rows, S) * w_ref[CONV_K - 1 - k:CONV_K - k, :]
        dx_ref[...] = dx.astype(BF16)

    return pl.pallas_call(
        body, name=name, grid=(n // cb,),
        in_specs=[pl.BlockSpec((S, cb), lambda j: (0, j)), pl.BlockSpec((S, cb), lambda j: (0, j + o + C_XBC // cb)),
                  pl.BlockSpec((CONV_K, cb), lambda j: (0, j + o)), pl.BlockSpec((1, cb), lambda j: (0, j + o))],
        out_specs=(pl.BlockSpec((S, cb), lambda j: (0, j)), pl.BlockSpec((CONV_K, cb), lambda j: (0, j)),
                   pl.BlockSpec((1, cb), lambda j: (0, j))),
        out_shape=(jax.ShapeDtypeStruct((S, n), BF16), jax.ShapeDtypeStruct((CONV_K, n), F32),
                   jax.ShapeDtypeStruct((1, n), F32)),
        compiler_params=_params("parallel"))(dact, proj, w, b)


def _ssd_specs(L, order):
    G = SSM_GROUPS_PER_STEP
    W, N = G * SSM_GROUP_WIDTH, G * SSM_D_STATE
    x_spec = pl.BlockSpec((L, W), lambda g, c: (order(c), g))
    b_spec = pl.BlockSpec((L, N), lambda g, c: (order(c), 2048 // N + g))
    c_spec = pl.BlockSpec((L, N), lambda g, c: (order(c), 3072 // N + g))
    z_spec = pl.BlockSpec((L, W), lambda g, c: (order(c), C_ZS // W + g))
    dt_spec = pl.BlockSpec((G, L, SSM_HEADS_PER_GROUP), lambda g, c: (g, order(c), 0))
    p_spec = pl.BlockSpec((G, 3, SSM_HEADS_PER_GROUP), lambda g, c: (g, 0, 0))
    nw_spec = pl.BlockSpec((G, 1, SSM_GROUP_WIDTH), lambda g, c: (g, 0, 0))
    s_spec = pl.BlockSpec((G, 1, SSM_GROUP_WIDTH, SSM_D_STATE), lambda g, c: (g, order(c), 0, 0))
    return x_spec, b_spec, c_spec, z_spec, dt_spec, p_spec, nw_spec, s_spec


class _SsdGroup:
    def __init__(self, L):
        P, H, W = SSM_HEAD_DIM, SSM_HEADS_PER_GROUP, SSM_GROUP_WIDTH
        self.L = L
        self.ii, self.jj = _iota(L, L, 0), _iota(L, L, 1)
        self.lower = jnp.where(self.ii >= self.jj, 1.0, 0.0).astype(BF16)
        self.upper = jnp.where(self.ii <= self.jj, 1.0, 0.0).astype(BF16)
        self.lo = _iota(L, 2 * P, 1) < P
        self.lo_row = _iota(1, 2 * P, 1) < P
        bi, bj = _iota(W, W, 0), _iota(W, W, 1)
        self.block = jnp.where(bi // P == bj // P, 1.0, 0.0).astype(BF16)
        si, sj = _iota(2 * P, W, 0), _iota(2 * P, W, 1)
        self.pick = jnp.where(sj == si * P, 1.0, 0.0).astype(BF16)
        self.ones = jnp.ones((L, 2 * P), BF16)

    def spread(self, v4):
        R = v4.shape[0]
        lo = self.lo if R == self.L else self.lo_row
        b = lambda h: jnp.broadcast_to(v4[:, h:h + 1], (R, 2 * SSM_HEAD_DIM))
        return jnp.concatenate([jnp.where(lo, b(0), b(1)), jnp.where(lo, b(2), b(3))], axis=1)

    def gather4(self, v):
        return jnp.concatenate([v[:, h * SSM_HEAD_DIM:h * SSM_HEAD_DIM + 1] for h in range(SSM_HEADS_PER_GROUP)],
                               axis=1)

    def head_sums(self, z):
        return sum(lax.dot_general(p, self.block, _NN, preferred_element_type=F32) for p in _parts(z, 2))

    def pair_cols(self, full, pair):
        ps = full[:, pair * 128:(pair + 1) * 128]
        sw = pltpu.roll(ps, SSM_HEAD_DIM, 1)
        return jnp.where(self.lo, ps, sw), jnp.where(self.lo, sw, ps)

    def gates(self, dt4_raw, p):
        L = self.L
        dtr = self.spread(dt4_raw + p[0:1, :])
        dt = _softplus(dtr)
        A = self.spread(-jnp.exp(p[1:2, :]))
        acum = _sum_by(self.lower, dt * A)
        yield
        rows = _sum_by(self.pick, acum, _NT)
        yield
        a_last = acum[L - 1:L, :]
        cols = self.pair_cols(acum, 0) + self.pair_cols(acum, 1)
        decay, decay_t = [], []
        for h in range(SSM_HEADS_PER_GROUP):
            seg = cols[h] - rows[h:h + 1, :]
            decay.append(jnp.exp(jnp.where(self.ii >= self.jj, seg, NEG_INF)))
            decay_t.append(jnp.exp(jnp.where(self.jj >= self.ii, -seg, NEG_INF)))
        return dict(dtr=dtr, dt=dt, A=A, D=self.spread(p[2:3, :]), acum=acum, eac=jnp.exp(acum), a_last=a_last,
                    wdec=jnp.exp(a_last - acum), decay=decay, decay_t=decay_t,
                    ea_last=[jnp.exp(rows[h:h + 1, L - 1:L]) for h in range(SSM_HEADS_PER_GROUP)])


def _ssd_fwd(conv, proj, dt_raw, pvec, nw, name):
    S = conv.shape[0]
    L, P, N, H, W, G = SSM_CHUNK, SSM_HEAD_DIM, SSM_D_STATE, SSM_HEADS_PER_GROUP, SSM_GROUP_WIDTH, SSM_GROUPS_PER_STEP
    nc = S // L

    def body(x_ref, b_ref, c_ref, z_ref, dt_ref, p_ref, nw_ref, y_ref, yn_ref, s0_ref, state):
        c = pl.program_id(1)

        @pl.when(c == 0)
        def _():
            state[...] = jnp.zeros_like(state)

        k = _SsdGroup(L)

        def group(gi):
            gsl = slice(gi * W, (gi + 1) * W)
            Bm, Cm = b_ref[:, gi * N:(gi + 1) * N], c_ref[:, gi * N:(gi + 1) * N]
            x = x_ref[:, gsl]
            S0 = state[gsl, :]
            s0_ref[gi, 0] = S0
            CB = _dot(Cm, Bm, _NT)
            y_off = _dot(Cm, S0, _NT)
            t = yield from k.gates(dt_ref[gi], p_ref[gi])
            xdt = x * t["dt"]
            s_new = _dot(xdt * t["wdec"], Bm, _TN)
            y_diag = []
            for pair in range(H // 2):
                xp = xdt[:, pair * 128:(pair + 1) * 128]
                y_diag.append(jnp.where(k.lo, _dot(CB * t["decay"][2 * pair], xp),
                                        _dot(CB * t["decay"][2 * pair + 1], xp)))
            yield
            y = jnp.concatenate(y_diag, axis=1) + y_off * t["eac"]
            for h in range(H):
                hsl = slice(gi * W + h * P, gi * W + (h + 1) * P)
                state[hsl, :] = S0[h * P:(h + 1) * P, :] * t["ea_last"][h] + s_new[h * P:(h + 1) * P, :]
            y_ref[:, gsl] = y
            y2 = (y + t["D"] * x) * _silu(z_ref[:, gsl])
            r = lax.rsqrt(jnp.mean(y2 * y2, axis=-1, keepdims=True) + EPS)
            yn_ref[:, gsl] = (y2 * r * nw_ref[gi]).astype(BF16)

        _lockstep(group(gi) for gi in range(G))

    x_spec, b_spec, c_spec, z_spec, dt_spec, p_spec, nw_spec, s_spec = _ssd_specs(L, lambda c: c)
    y_spec = pl.BlockSpec((L, G * W), lambda g, c: (c, g))
    return pl.pallas_call(
        body, name=name, grid=(SSM_GROUPS // G, nc),
        in_specs=[x_spec, b_spec, c_spec, z_spec, dt_spec, p_spec, nw_spec],
        out_specs=(y_spec, y_spec, s_spec),
        out_shape=(jax.ShapeDtypeStruct((S, SSM_GROUPS * W), F32), jax.ShapeDtypeStruct((S, SSM_GROUPS * W), BF16),
                   jax.ShapeDtypeStruct((SSM_GROUPS, nc, W, N), F32)),
        scratch_shapes=[pltpu.VMEM((G * W, N), F32)],
        compiler_params=_params("parallel", "arbitrary"))(conv, conv, conv, proj, dt_raw, pvec, nw)


def _ssd_bwd(dyn, conv, proj, dt_raw, pvec, nw, y_ssd, states, name):
    S = conv.shape[0]
    L, P, N, H, W, G = SSM_CHUNK, SSM_HEAD_DIM, SSM_D_STATE, SSM_HEADS_PER_GROUP, SSM_GROUP_WIDTH, SSM_GROUPS_PER_STEP
    nc = S // L

    def body(dyn_ref, x_ref, b_ref, c_ref, z_ref, dt_ref, p_ref, nw_ref, y_ref, s0_ref,
             dx_ref, db_ref, dc_ref, dz_ref, ddt_ref, dp_ref, dnw_ref, dstate):
        c = pl.program_id(1)

        @pl.when(c == 0)
        def _():
            dstate[...] = jnp.zeros_like(dstate)
            dp_ref[...] = jnp.zeros_like(dp_ref)
            dnw_ref[...] = jnp.zeros_like(dnw_ref)

        k = _SsdGroup(L)
        last = (_iota(L, 1, 0) == L - 1)

        def group(gi):
            gsl = slice(gi * W, (gi + 1) * W)
            Bm, Cm = b_ref[:, gi * N:(gi + 1) * N], c_ref[:, gi * N:(gi + 1) * N]
            x, z = x_ref[:, gsl], z_ref[:, gsl]
            S0, dS1 = s0_ref[gi, 0], dstate[gsl, :]
            CB = _dot(Cm, Bm, _NT)
            CBt = _dot(Bm, Cm, _NT)
            y_off_raw = _dot(Cm, S0, _NT)
            dXs_raw = _dot(Bm, dS1, _NT)
            t = yield from k.gates(dt_ref[gi], p_ref[gi])
            y1 = y_ref[:, gsl] + t["D"] * x
            sz = _silu(z)
            y2 = y1 * sz
            r = lax.rsqrt(jnp.mean(y2 * y2, axis=-1, keepdims=True) + EPS)
            y2h = y2 * r
            dyn_v = dyn_ref[:, gsl]
            dnw_ref[gi] += _colsum(dyn_v * y2h)
            dy2h = dyn_v * nw_ref[gi]
            dy2 = r * (dy2h - y2h * jnp.mean(dy2h * y2h, axis=-1, keepdims=True))
            dz_ref[:, gsl] = (dy2 * y1 * _dsilu(z)).astype(BF16)
            dY = dy2 * sz
            X = x * t["dt"]
            dYe = dY * t["eac"]
            dC_s = _dot(dYe, S0)
            dB_s = _dot(X * t["wdec"], dS1)
            dS_c = _dot(dYe, Cm, _TN)
            dXm, Gs, Gts = [], [], []
            for pair in range(H // 2):
                dYp, Xp = dY[:, pair * 128:(pair + 1) * 128], X[:, pair * 128:(pair + 1) * 128]
                dXm.append(jnp.where(k.lo, _dot(CBt * t["decay_t"][2 * pair], dYp),
                                     _dot(CBt * t["decay_t"][2 * pair + 1], dYp)))
                for mask in (k.lo, ~k.lo):
                    Gs.append(_dot(jnp.where(mask, dYp, 0.0), Xp, _NT))
                    Gts.append(_dot(jnp.where(mask, Xp, 0.0), dYp, _NT))
            yield
            dXs = dXs_raw * t["wdec"]
            dX = jnp.concatenate(dXm, axis=1) + dXs
            dCB, dCBt, q_sums = 0.0, 0.0, []
            for h in range(H):
                M, Mt = CB * t["decay"][h], CBt * t["decay_t"][h]
                dCB = dCB + Gs[h] * t["decay"][h]
                dCBt = dCBt + Gts[h] * t["decay_t"][h]
                d = Gs[h] * M - Gts[h] * Mt
                q_sums.append(sum(lax.dot_general(pt, k.ones, _NN, preferred_element_type=F32)
                                  for pt in _parts(d, 2)))
            q_f = jnp.concatenate([jnp.where(k.lo, q_sums[0], q_sums[1]), jnp.where(k.lo, q_sums[2], q_sums[3])],
                                  axis=1)
            x_dxs = k.head_sums(X * dXs)
            tot = [_total(dS1[h * P:(h + 1) * P, :] * S0[h * P:(h + 1) * P, :]) * t["ea_last"][h] for h in range(H)]
            tot_f = k.spread(jnp.concatenate(tot, axis=1))
            d_alast = _colsum(x_dxs) + tot_f
            dacum = q_f + k.head_sums(dY * (y_off_raw * t["eac"])) - x_dxs + jnp.where(last, d_alast, 0.0)
            dx_dt = k.head_sums(dX * x)
            d_skip = _colsum(k.head_sums(dY * x))
            for h in range(H):
                hsl = slice(gi * W + h * P, gi * W + (h + 1) * P)
                dstate[hsl, :] = t["ea_last"][h] * dS1[h * P:(h + 1) * P, :] + dS_c[h * P:(h + 1) * P, :]
            dc_s2 = _dot(dCB, Bm)
            db_s2 = _dot(dCBt, Cm)
            yield
            da = _sum_by(k.upper, dacum)
            yield
            ddt_raw = (da * t["A"] + dx_dt) * _sigmoid(t["dtr"])
            dx_ref[:, gsl] = dX * t["dt"] + t["D"] * dY
            dc_ref[:, gi * N:(gi + 1) * N] = dC_s + dc_s2
            db_ref[:, gi * N:(gi + 1) * N] = dB_s + db_s2
            ddt_ref[gi] = k.gather4(ddt_raw)
            dp_ref[gi] += k.gather4(jnp.concatenate([_colsum(ddt_raw), _colsum(da * t["dt"]) * t["A"], d_skip],
                                                    axis=0))

        _lockstep(group(gi) for gi in range(G))

    rev = lambda c: nc - 1 - c
    x_spec, b_spec, c_spec, z_spec, dt_spec, p_spec, nw_spec, s_spec = _ssd_specs(L, rev)
    y_spec = pl.BlockSpec((L, G * W), lambda g, c: (rev(c), g))
    n_spec = pl.BlockSpec((L, G * N), lambda g, c: (rev(c), g))
    return pl.pallas_call(
        body, name=name, grid=(SSM_GROUPS // G, nc),
        in_specs=[y_spec, x_spec, b_spec, c_spec, z_spec, dt_spec, p_spec, nw_spec, y_spec, s_spec],
        out_specs=(y_spec, n_spec, n_spec, y_spec, dt_spec, p_spec, nw_spec),
        out_shape=(jax.ShapeDtypeStruct((S, SSM_GROUPS * W), F32), jax.ShapeDtypeStruct((S, SSM_GROUPS * N), F32),
                   jax.ShapeDtypeStruct((S, SSM_GROUPS * N), F32), jax.ShapeDtypeStruct((S, SSM_GROUPS * W), BF16),
                   jax.ShapeDtypeStruct((SSM_GROUPS, S, H), F32), jax.ShapeDtypeStruct((SSM_GROUPS, 3, H), F32),
                   jax.ShapeDtypeStruct((SSM_GROUPS, 1, W), F32)),
        scratch_shapes=[pltpu.VMEM((G * W, N), F32)],
        compiler_params=_params("parallel", "arbitrary"))(dyn, conv, conv, conv, proj, dt_raw, pvec, nw, y_ssd, states)


def _unit_lower_inverse(A, ii, jj):
    eye = (ii == jj).astype(F32)
    same = (ii // GDN_INV_BLOCK) == (jj // GDN_INV_BLOCK)
    Ad = jnp.where(same, A, 0.0)
    Ao = A - Ad
    P2 = _dot3(Ad, Ad)
    yield
    P4, X = _dot3(P2, P2), _dot3(eye - Ad, eye + P2)
    yield
    P8, X = _dot3(P4, P4), _dot3(X, eye + P4)
    yield
    X = _dot3(X, eye + P8)
    yield
    Bm = _dot3(X, Ao)
    yield
    B2 = _dot3(Bm, Bm)
    yield
    Y = _dot3(eye - Bm, eye + B2)
    yield
    T = _dot3(Y, X)
    yield
    return T


def _gdn_specs(L, order):
    G = GDN_QK_PER_STEP
    Hd, W = G * GDN_HEAD, G * GDN_V_PER_QK * GDN_HEAD
    q_spec = pl.BlockSpec((L, Hd), lambda h, c: (order(c), (C_QKV - C_XBC) // Hd + h))
    k_spec = pl.BlockSpec((L, Hd), lambda h, c: (order(c), (C_QKV - C_XBC + 1024) // Hd + h))
    v_spec = pl.BlockSpec((L, W), lambda h, c: (order(c), (C_QKV - C_XBC + 2048) // W + h))
    z_spec = pl.BlockSpec((L, W), lambda h, c: (order(c), C_ZG // W + h))
    ba_spec = pl.BlockSpec((G, L, GDN_V_PER_QK), lambda h, c: (h, order(c), 0))
    p_spec = pl.BlockSpec((G, 2, GDN_V_PER_QK), lambda h, c: (h, 0, 0))
    nw_spec = pl.BlockSpec((1, GDN_HEAD), lambda h, c: (0, 0))
    s_spec = pl.BlockSpec((G, 1, GDN_V_PER_QK * GDN_HEAD, GDN_HEAD), lambda h, c: (h, order(c), 0, 0))
    t_spec = pl.BlockSpec((G * GDN_V_PER_QK, 1, L, L), lambda h, c: (h, order(c), 0, 0))
    return q_spec, k_spec, v_spec, z_spec, ba_spec, p_spec, nw_spec, s_spec, t_spec


def _gdn_gates(qa, ka, b_col, a_col, p, j, ii, jj):
    L = qa.shape[0]
    sp_in = a_col + p[0:1, j:j + 1]
    neg_ea = -jnp.exp(p[1:2, j:j + 1])
    g = neg_ea * _softplus(sp_in)
    gcum, gcum_row = _cumsum_forms(g, ii, jj)
    rq = lax.rsqrt(_rowsum(qa * qa) + EPS)
    rk = lax.rsqrt(_rowsum(ka * ka) + EPS)
    q = qa * rq * (GDN_HEAD ** -0.5)
    k = ka * rk
    beta = _sigmoid(b_col)
    yield
    Dm = jnp.exp(jnp.where(ii >= jj, gcum - gcum_row, NEG_INF))
    eg = jnp.exp(gcum)
    g_last = gcum[L - 1:L, :]
    wdec = jnp.exp(g_last - gcum)
    return dict(rq=rq, rk=rk, q=q, k=k, beta=beta, sp_in=sp_in, neg_ea=neg_ea, g=g, Dm=Dm, kbeta=k * beta, eg=eg,
                g_last=g_last, wdec=wdec, kdec=k * wdec)


def _gdn_fwd(conv, proj, b_raw, a_raw, pvec, nw, name):
    S = conv.shape[0]
    L, Hd, J, G = GDN_CHUNK, GDN_HEAD, GDN_V_PER_QK, GDN_QK_PER_STEP
    W = J * Hd
    nc = S // L

    def body(q_ref, k_ref, v_ref, z_ref, b_ref, a_ref, p_ref, nw_ref, o_ref, on_ref, s0_ref, t_ref, state):
        c = pl.program_id(1)

        @pl.when(c == 0)
        def _():
            state[...] = jnp.zeros_like(state)

        ii, jj = _iota(L, L, 0), _iota(L, L, 1)
        for hq in range(G):
            s0_ref[hq, 0] = state[hq * W:(hq + 1) * W, :]

        def head(hq, j):
            hd = hq * J + j
            hsl, sl = slice(hq * Hd, (hq + 1) * Hd), slice(hd * Hd, (hd + 1) * Hd)
            t = yield from _gdn_gates(q_ref[:, hsl], k_ref[:, hsl], b_ref[hq][:, j:j + 1], a_ref[hq][:, j:j + 1],
                                      p_ref[hq], j, ii, jj)
            KK = _dot(t["kbeta"], t["k"], _NT)
            QK = _dot(t["q"], t["k"], _NT)
            yield
            T = yield from _unit_lower_inverse(jnp.where(ii > jj, KK * t["Dm"], 0.0), ii, jj)
            t_ref[hd, 0] = T
            S0 = state[sl, :]
            U = _dot3(T, v_ref[:, sl] * t["beta"])
            Wm = _dot3(T, t["kbeta"] * t["eg"])
            o_inter = _dot(t["q"] * t["eg"], S0)
            yield
            Vn = U - _dot(Wm, S0)
            yield
            o = o_inter + _dot(QK * t["Dm"], Vn)
            s_new = _dot(t["kdec"], Vn, _TN)
            yield
            state[sl, :] = S0 * jnp.exp(t["g_last"]) + s_new
            o_ref[:, sl] = o
            r = lax.rsqrt(jnp.mean(o * o, axis=-1, keepdims=True) + EPS)
            on_ref[:, sl] = ((o * r * nw_ref[...]) * _silu(z_ref[:, sl])).astype(BF16)

        _lockstep(head(hq, j) for hq in range(G) for j in range(J))

    q_spec, k_spec, v_spec, z_spec, ba_spec, p_spec, nw_spec, s_spec, t_spec = _gdn_specs(L, lambda c: c)
    o_spec = pl.BlockSpec((L, G * W), lambda h, c: (c, h))
    return pl.pallas_call(
        body, name=name, grid=(GDN_QK_HEADS // G, nc),
        in_specs=[q_spec, k_spec, v_spec, z_spec, ba_spec, ba_spec, p_spec, nw_spec],
        out_specs=(o_spec, o_spec, s_spec, t_spec),
        out_shape=(jax.ShapeDtypeStruct((S, GDN_QK_HEADS * W), F32), jax.ShapeDtypeStruct((S, GDN_QK_HEADS * W), BF16),
                   jax.ShapeDtypeStruct((GDN_QK_HEADS, nc, W, Hd), F32),
                   jax.ShapeDtypeStruct((GDN_QK_HEADS * J, nc, L, L), F32)),
        scratch_shapes=[pltpu.VMEM((G * W, Hd), F32)],
        compiler_params=_params("parallel", "arbitrary"))(conv, conv, conv, proj, b_raw, a_raw, pvec, nw)


def _gdn_bwd(don, conv, proj, b_raw, a_raw, pvec, nw, o_pre, states, t_inv, name):
    S = conv.shape[0]
    L, Hd, J, G = GDN_CHUNK, GDN_HEAD, GDN_V_PER_QK, GDN_QK_PER_STEP
    W = J * Hd
    nc = S // L

    def body(don_ref, q_ref, k_ref, v_ref, z_ref, b_ref, a_ref, p_ref, nw_ref, o_ref, s0_ref, t_ref,
             dq_ref, dk_ref, dv_ref, dz_ref, db_ref, da_ref, dp_ref, dnw_ref, dstate):
        c = pl.program_id(1)

        @pl.when(c == 0)
        def _():
            dstate[...] = jnp.zeros_like(dstate)
            dp_ref[...] = jnp.zeros_like(dp_ref)
            dnw_ref[...] = jnp.zeros_like(dnw_ref)

        ii, jj = _iota(L, L, 0), _iota(L, L, 1)
        last = (_iota(L, 1, 0) == L - 1)
        res = {}

        def head(hq, j):
            hd = hq * J + j
            hsl, sl = slice(hq * Hd, (hq + 1) * Hd), slice(hd * Hd, (hd + 1) * Hd)
            qa, ka = q_ref[:, hsl], k_ref[:, hsl]
            t = yield from _gdn_gates(qa, ka, b_ref[hq][:, j:j + 1], a_ref[hq][:, j:j + 1], p_ref[hq], j, ii, jj)
            q, k, beta, eg, Dm, kbeta, kdec = (t[nm] for nm in ("q", "k", "beta", "eg", "Dm", "kbeta", "kdec"))
            T = t_ref[hd, 0]
            v, z, o = v_ref[:, sl], z_ref[:, sl], o_ref[:, sl]
            S0, dS1 = s0_ref[hq, 0, j * Hd:(j + 1) * Hd, :], dstate[sl, :]
            sz = _silu(z)
            r = lax.rsqrt(jnp.mean(o * o, axis=-1, keepdims=True) + EPS)
            oh = o * r
            d_on = don_ref[:, sl]
            dz_ref[:, sl] = (d_on * (oh * nw_ref[...]) * _dsilu(z)).astype(BF16)
            dn = d_on * sz
            dnw_part = _colsum(dn * oh)
            doh = dn * nw_ref[...]
            dO = r * (doh - oh * jnp.mean(doh * oh, axis=-1, keepdims=True))
            Rw = kbeta * eg
            qe = q * eg
            U = _dot3(T, v * beta)
            Wm = _dot3(T, Rw)
            KK = _dot(kbeta, k, _NT)
            QK = _dot(q, k, _NT)
            o_inter = _dot(qe, S0)
            dq_s = _dot(dO, S0, _NT)
            dS_q = _dot(qe, dO, _TN)
            yield
            Am = jnp.where(ii > jj, KK * Dm, 0.0)
            Pm = QK * Dm
            Vn = U - _dot(Wm, S0)
            dVn_s = _dot(kdec, dS1)
            yield
            dVn = _dot(Pm, dO, _TN) + dVn_s
            dP = _dot(dO, Vn, _NT)
            dKd = _dot(Vn, dS1, _NT)
            yield
            dQK = dP * Dm
            dq = _dot(dQK, k) + dq_s * eg
            dk = _dot(dQK, q, _TN) + dKd * t["wdec"]
            dstate[sl, :] = jnp.exp(t["g_last"]) * dS1 + dS_q - _dot(Wm, dVn, _TN)
            dW = -_dot(dVn, S0, _NT)
            dRu = _dot3(T, dVn, _TN)
            yield
            dRw = _dot3(T, dW, _TN)
            dA_u = _dot(dRu, U, _NT)
            yield
            dA = jnp.where(ii > jj, -(dA_u + _dot(dRw, Wm, _NT)), 0.0)
            yield
            dKK = dA * Dm
            dkbeta = _dot(dKK, k) + dRw * eg
            dk = dk + _dot(dKK, kbeta, _TN)
            yield
            dk = dk + dkbeta * beta
            dbeta = _rowsum(dkbeta * k) + _rowsum(dRu * v)
            dv_ref[:, sl] = dRu * beta
            Q = dA * Am + dP * Pm
            rho = _rowsum(dKd * kdec)
            d_glast = _colsum(rho) + jnp.exp(t["g_last"]) * _total(dS1 * S0)
            q_sums = _row_col_sums(Q)
            rest = _rowsum(dRw * Rw) + _rowsum(dO * o_inter) - rho + jnp.where(last, d_glast, 0.0)
            yield
            dg = _rev_cumsum_col(q_sums + rest, ii, jj)
            yield
            da_raw = dg * t["neg_ea"] * _sigmoid(t["sp_in"])
            res[hq, j] = dict(dq=dq, dk=dk, db=dbeta * beta * (1.0 - beta), da=da_raw, d_bias=_colsum(da_raw),
                              d_alog=_colsum(dg * t["g"]), dnw=dnw_part, rq=t["rq"], rk=t["rk"], k=k, qh=qa * t["rq"])

        _lockstep(head(hq, j) for hq in range(G) for j in range(J))
        for hq in range(G):
            parts = [res[hq, j] for j in range(J)]
            hsl = slice(hq * Hd, (hq + 1) * Hd)
            p0 = parts[0]
            dqh = sum(pt["dq"] for pt in parts) * (GDN_HEAD ** -0.5)
            dkn = sum(pt["dk"] for pt in parts)
            dq_ref[:, hsl] = p0["rq"] * (dqh - p0["qh"] * _rowsum(dqh * p0["qh"]))
            dk_ref[:, hsl] = p0["rk"] * (dkn - p0["k"] * _rowsum(dkn * p0["k"]))
            db_ref[hq] = jnp.concatenate([pt["db"] for pt in parts], axis=1)
            da_ref[hq] = jnp.concatenate([pt["da"] for pt in parts], axis=1)
            dp_ref[hq] += jnp.concatenate([jnp.concatenate([pt["d_bias"] for pt in parts], axis=1),
                                           jnp.concatenate([pt["d_alog"] for pt in parts], axis=1)], axis=0)
            dnw_ref[hq] += sum(pt["dnw"] for pt in parts)

    rev = lambda c: nc - 1 - c
    q_spec, k_spec, v_spec, z_spec, ba_spec, p_spec, nw_spec, s_spec, t_spec = _gdn_specs(L, rev)
    o_spec = pl.BlockSpec((L, G * W), lambda h, c: (rev(c), h))
    h_spec = pl.BlockSpec((L, G * Hd), lambda h, c: (rev(c), h))
    dnw_spec = pl.BlockSpec((G, 1, Hd), lambda h, c: (h, 0, 0))
    return pl.pallas_call(
        body, name=name, grid=(GDN_QK_HEADS // G, nc),
        in_specs=[o_spec, q_spec, k_spec, v_spec, z_spec, ba_spec, ba_spec, p_spec, nw_spec, o_spec, s_spec, t_spec],
        out_specs=(h_spec, h_spec, o_spec, o_spec, ba_spec, ba_spec, p_spec, dnw_spec),
        out_shape=(jax.ShapeDtypeStruct((S, GDN_QK_HEADS * Hd), F32), jax.ShapeDtypeStruct((S, GDN_QK_HEADS * Hd), F32),
                   jax.ShapeDtypeStruct((S, GDN_QK_HEADS * W), F32), jax.ShapeDtypeStruct((S, GDN_QK_HEADS * W), BF16),
                   jax.ShapeDtypeStruct((GDN_QK_HEADS, S, J), F32), jax.ShapeDtypeStruct((GDN_QK_HEADS, S, J), F32),
                   jax.ShapeDtypeStruct((GDN_QK_HEADS, 2, J), F32), jax.ShapeDtypeStruct((GDN_QK_HEADS, 1, Hd), F32)),
        scratch_shapes=[pltpu.VMEM((G * W, Hd), F32)],
        compiler_params=_params("parallel", "arbitrary"))(don, conv, conv, conv, proj, b_raw, a_raw, pvec, nw, o_pre,
                                                          states, t_inv)


def _ada_fwd(c_all, w_loc, b_loc, name):
    n = w_loc.shape[1]

    def body(c_ref, w_ref, b_ref, o_ref):
        o_ref[...] = _dot3(_silu(c_ref[...]), w_ref[...]) + b_ref[...]

    return pl.pallas_call(body, name=name, out_shape=jax.ShapeDtypeStruct((N_DEV, n), F32),
                          compiler_params=pltpu.CompilerParams(vmem_limit_bytes=VMEM_LIMIT))(c_all, w_loc, b_loc)


def _ada_bwd(c_all_t, dmod_cols, name):
    Dm, n = c_all_t.shape[0], dmod_cols.shape[1]

    def body(c_ref, d_ref, o_ref):
        ca = _silu(c_ref[...])
        acc = ca[:, 0:1] * d_ref[0:1, :]
        for i in range(1, N_DEV):
            acc = acc + ca[:, i:i + 1] * d_ref[i:i + 1, :]
        o_ref[...] = acc

    return pl.pallas_call(body, name=name, out_shape=jax.ShapeDtypeStruct((Dm, n), F32),
                          compiler_params=pltpu.CompilerParams(vmem_limit_bytes=VMEM_LIMIT))(c_all_t, dmod_cols)


ADAM_BLOCK_BYTES = 12 * 1024 * 1024


def _adam(contrib, w, m, v, name):
    n, R, C = contrib.shape
    tr = R
    while tr % 16 == 0 and (n + 7) * tr * C * 4 > ADAM_BLOCK_BYTES:
        tr //= 2

    def body(c_ref, w_ref, m_ref, v_ref, g_ref, d_ref, nm_ref, nv_ref):
        g = c_ref[0].astype(F32)
        for i in range(1, n):
            g = g + c_ref[i].astype(F32)
        nm = ADAM_B1 * m_ref[...] + (1.0 - ADAM_B1) * g
        nv = ADAM_B2 * v_ref[...] + (1.0 - ADAM_B2) * (g * g)
        m_hat = nm / (1.0 - ADAM_B1 ** ADAM_STEP)
        v_hat = nv / (1.0 - ADAM_B2 ** ADAM_STEP)
        g_ref[...] = g
        d_ref[...] = -ADAM_LR * (m_hat / (jnp.sqrt(v_hat) + ADAM_EPS) + ADAM_WD * w_ref[...])
        nm_ref[...] = nm
        nv_ref[...] = nv

    spec = pl.BlockSpec((tr, C), lambda i: (i, 0))
    shp = jax.ShapeDtypeStruct((R, C), F32)
    return pl.pallas_call(
        body, name=name, grid=(R // tr,), in_specs=[pl.BlockSpec((n, tr, C), lambda i: (0, i, 0)), spec, spec, spec],
        out_specs=(spec,) * 4, out_shape=(shp,) * 4, compiler_params=_params("parallel"))(contrib, w, m, v)


def _exchange(arrays, modes, name):
    n = len(arrays)
    out_shape = tuple(jax.ShapeDtypeStruct((N_DEV,) + a.shape if md == "gather" else a.shape, a.dtype)
                      for a, md in zip(arrays, modes))

    def body(*refs):
        ins, outs = refs[:n], refs[n:2 * n]
        send_sems, recv_sems, loc_sems = refs[2 * n:]
        ix, iy, ic = lax.axis_index("x"), lax.axis_index("y"), lax.axis_index("c")
        me = 4 * ix + 2 * iy + ic
        peers = []
        for m in range(1, N_DEV):
            px = 1 - ix if m & 4 else ix
            py = 1 - iy if m & 2 else iy
            pc = 1 - ic if m & 1 else ic
            peers.append(((px, py, pc), 4 * px + 2 * py + pc))

        def src(k, slot):
            return ins[k] if modes[k] == "gather" else ins[k].at[slot]

        def remote(k, m, to_slot, land_slot):
            return pltpu.make_async_remote_copy(
                src_ref=src(k, to_slot), dst_ref=outs[k].at[land_slot], send_sem=send_sems.at[k, m],
                recv_sem=recv_sems.at[k, m], device_id=peers[m][0], device_id_type=pl.DeviceIdType.MESH)

        local = [pltpu.make_async_copy(src(k, me), outs[k].at[me], loc_sems.at[k]) for k in range(n)]
        for cp in local:
            cp.start()
        sends = [remote(k, m, peers[m][1], me) for m in range(N_DEV - 1) for k in range(n)]
        for cp in sends:
            cp.start()
        for m in range(N_DEV - 1):
            for k in range(n):
                remote(k, m, peers[m][1], peers[m][1]).wait_recv()
        for cp in sends:
            cp.wait_send()
        for cp in local:
            cp.wait()

    any_spec = pl.BlockSpec(memory_space=pl.ANY)
    return pl.pallas_call(
        body, name=name, in_specs=[any_spec] * n, out_specs=(any_spec,) * n, out_shape=out_shape,
        scratch_shapes=[pltpu.SemaphoreType.DMA((n, N_DEV - 1)), pltpu.SemaphoreType.DMA((n, N_DEV - 1)),
                        pltpu.SemaphoreType.DMA((n,))])(*arrays)


def _gather_two_level(arrays, name):
    n = len(arrays)
    out_shape = tuple(jax.ShapeDtypeStruct((N_DEV,) + a.shape, a.dtype) for a in arrays)

    def body(*refs):
        ins, outs = refs[:n], refs[n:2 * n]
        send_sems, recv_sems, loc_sems = refs[2 * n:]
        ix, iy, ic = lax.axis_index("x"), lax.axis_index("y"), lax.axis_index("c")
        lin = lambda px, py, pc: 4 * px + 2 * py + pc
        me, sib = lin(ix, iy, ic), (ix, iy, 1 - ic)
        chips = [(1 - ix, iy), (ix, 1 - iy), (1 - ix, 1 - iy)]

        def copy(k, s, block, to, src=None):
            return pltpu.make_async_remote_copy(
                src_ref=outs[k].at[block] if src is None else src, dst_ref=outs[k].at[block],
                send_sem=send_sems.at[k, s], recv_sem=recv_sems.at[k, s], device_id=to,
                device_id_type=pl.DeviceIdType.MESH)

        local = [pltpu.make_async_copy(ins[k], outs[k].at[me], loc_sems.at[k]) for k in range(n)]
        for cp in local:
            cp.start()
        first = [copy(k, 1 + j, me, (cx, cy, ic), src=ins[k]) for j, (cx, cy) in enumerate(chips) for k in range(n)]
        first += [copy(k, 0, me, sib, src=ins[k]) for k in range(n)]
        for cp in first:
            cp.start()
        passed = []
        for j, (cx, cy) in enumerate(chips):
            for k in range(n):
                copy(k, 1 + j, lin(cx, cy, ic), sib).wait_recv()
                passed.append(copy(k, 4 + j, lin(cx, cy, ic), sib))
                passed[-1].start()
        for k in range(n):
            copy(k, 0, lin(*sib), sib).wait_recv()
            for j, (cx, cy) in enumerate(chips):
                copy(k, 4 + j, lin(cx, cy, 1 - ic), sib).wait_recv()
        for cp in first + passed:
            cp.wait_send()
        for cp in local:
            cp.wait()

    any_spec = pl.BlockSpec(memory_space=pl.ANY)
    return pl.pallas_call(
        body, name=name, in_specs=[any_spec] * n, out_specs=(any_spec,) * n, out_shape=out_shape,
        scratch_shapes=[pltpu.SemaphoreType.DMA((n, N_DEV - 1)), pltpu.SemaphoreType.DMA((n, N_DEV - 1)),
                        pltpu.SemaphoreType.DMA((n,))])(*arrays)


def _peer_table():
    ix, iy, ic = lax.axis_index("x"), lax.axis_index("y"), lax.axis_index("c")
    peers = []
    for m in range(1, N_DEV):
        px = 1 - ix if m & 4 else ix
        py = 1 - iy if m & 2 else iy
        pc = 1 - ic if m & 1 else ic
        peers.append(((px, py, pc), 4 * px + 2 * py + pc))
    return 4 * ix + 2 * iy + ic, peers


def _exchange_start(arrays, modes, after, name):
    n = len(arrays)
    land_shapes = [(N_DEV,) + a.shape if md == "gather" else a.shape for a, md in zip(arrays, modes)]

    def body(*refs):
        ins, lands = refs[:n], refs[n:2 * n]
        send_sems, recv_sems = refs[2 * n + 1], refs[2 * n + 2]
        token, loc_sems = refs[-2], refs[-1]
        me, peers = _peer_table()

        def src(k, slot):
            return ins[k] if modes[k] == "gather" else ins[k].at[slot]

        local = [pltpu.make_async_copy(src(k, me), lands[k].at[me], loc_sems.at[k]) for k in range(n)]
        for cp in local:
            cp.start()
        for cp in local:
            cp.wait()
        for m in range(N_DEV - 1):
            for k in range(n):
                pltpu.make_async_remote_copy(
                    src_ref=src(k, peers[m][1]), dst_ref=lands[k].at[me], send_sem=send_sems, recv_sem=recv_sems,
                    device_id=peers[m][0], device_id_type=pl.DeviceIdType.MESH).start()
        token[...] = jnp.zeros_like(token)

    hbm = pl.BlockSpec(memory_space=pltpu.HBM)
    sem = pl.BlockSpec(memory_space=pltpu.SEMAPHORE)
    sem_shape = pltpu.SemaphoreType.DMA(())
    operands = [pltpu.with_memory_space_constraint(a, pltpu.HBM) for a in arrays]
    operands += [pltpu.with_memory_space_constraint(lax.empty(s, a.dtype), pltpu.HBM)
                 for s, a in zip(land_shapes, arrays)]
    out = pl.pallas_call(
        body, name=name,
        out_shape=(sem_shape, sem_shape) + tuple(pltpu.HBM(a.shape, a.dtype) for a in arrays)
        + tuple(pltpu.HBM(s, a.dtype) for s, a in zip(land_shapes, arrays)) + (jax.ShapeDtypeStruct((8, 128), F32),),
        in_specs=[hbm] * (2 * n) + [pl.BlockSpec(memory_space=pl.ANY)],
        out_specs=(sem, sem) + (hbm,) * (2 * n) + (pl.BlockSpec(memory_space=pltpu.VMEM),),
        input_output_aliases={i: 2 + i for i in range(2 * n)},
        scratch_shapes=[pltpu.SemaphoreType.DMA((n,))],
        compiler_params=pltpu.CompilerParams(has_side_effects=pltpu.SideEffectType.DATAFLOW_SIDE_EFFECTING))(
            *operands, after)
    return out[0], out[1], out[2:2 + n], out[2 + n:2 + 2 * n], out[-1]


def _exchange_wait(started, modes, after, name):
    send_sems, recv_sems, sent, lands, _ = started
    n = len(sent)

    def body(*refs):
        ins, zones = refs[:n], refs[n:2 * n]
        send_ref, recv_ref = refs[2 * n], refs[2 * n + 1]
        me, peers = _peer_table()

        def src(k, slot):
            return ins[k] if modes[k] == "gather" else ins[k].at[slot]

        for m in range(N_DEV - 1):
            for k in range(n):
                cp = pltpu.make_async_remote_copy(
                    src_ref=src(k, peers[m][1]), dst_ref=zones[k].at[peers[m][1]], send_sem=send_ref,
                    recv_sem=recv_ref, device_id=peers[m][0], device_id_type=pl.DeviceIdType.MESH)
                cp.wait_send()
                cp.wait_recv()

    hbm = pl.BlockSpec(memory_space=pltpu.HBM)
    sem = pl.BlockSpec(memory_space=pltpu.SEMAPHORE)
    out = pl.pallas_call(
        body, name=name,
        out_shape=tuple(pltpu.HBM(a.shape, a.dtype) for a in sent) + tuple(pltpu.HBM(a.shape, a.dtype) for a in lands),
        in_specs=[hbm] * (2 * n) + [sem, sem, pl.BlockSpec(memory_space=pl.ANY)], out_specs=(hbm,) * (2 * n),
        input_output_aliases={i: i for i in range(2 * n)},
        compiler_params=pltpu.CompilerParams(has_side_effects=pltpu.SideEffectType.DATAFLOW_SIDE_EFFECTING))(
            *sent, *lands, send_sems, recv_sems, after)
    return out[n:]


W_IN_SPLITS = (0, 2048, 6144, 6176, 10272, 12320, 12336, 12352, 13376, 14400)
N_REPLICATED = 16640
REPLICATED = ("b_ada", "norm_mix_pre", "norm_mix_post", "ssm_conv_b", "ssm_dt_bias", "ssm_A_log", "ssm_D",
              "ssm_norm_w", "gdn_dt_bias", "gdn_A_log", "gdn_norm_w", "norm_mlp_pre", "norm_mlp_post")
WEIGHTS = ("w_ada", "b_ada", "norm_mix_pre", "norm_mix_post", "w_in", "ssm_conv_w", "ssm_conv_b", "ssm_dt_bias",
           "ssm_A_log", "ssm_D", "ssm_norm_w", "gdn_conv_w", "gdn_dt_bias", "gdn_A_log", "gdn_norm_w", "w_ssm_up",
           "w_gdn_up", "w_out", "norm_mlp_pre", "norm_mlp_post", "w_mlp_up", "w_mlp_down")


def _by_cols(t):
    return t.transpose(1, 0, 2).reshape(t.shape[1], N_DEV * t.shape[2])


def _to_col_shards(t):
    R, C8 = t.shape
    return t.reshape(R, N_DEV, C8 // N_DEV).transpose(1, 0, 2)


def _heads_first(t, groups):
    S = t.shape[0]
    return t.reshape(S, groups, t.shape[1] // groups).transpose(1, 0, 2)


def _heads_last(t):
    return t.transpose(1, 0, 2).reshape(t.shape[1], t.shape[0] * t.shape[2])


def kernel(x, c, w_ada, b_ada, norm_mix_pre, norm_mix_post, w_in, ssm_conv_w, ssm_conv_b, ssm_dt_bias, ssm_A_log, ssm_D, ssm_norm_w, gdn_conv_w, gdn_dt_bias, gdn_A_log, gdn_norm_w, w_ssm_up, w_gdn_up, w_out, norm_mlp_pre, norm_mlp_post, w_mlp_up, w_mlp_down, loss_target, m_w_ada, m_b_ada, m_norm_mix_pre, m_norm_mix_post, m_w_in, m_ssm_conv_w, m_ssm_conv_b, m_ssm_dt_bias, m_ssm_A_log, m_ssm_D, m_ssm_norm_w, m_gdn_conv_w, m_gdn_dt_bias, m_gdn_A_log, m_gdn_norm_w, m_w_ssm_up, m_w_gdn_up, m_w_out, m_norm_mlp_pre, m_norm_mlp_post, m_w_mlp_up, m_w_mlp_down, v_w_ada, v_b_ada, v_norm_mix_pre, v_norm_mix_post, v_w_in, v_ssm_conv_w, v_ssm_conv_b, v_ssm_dt_bias, v_ssm_A_log, v_ssm_D, v_ssm_norm_w, v_gdn_conv_w, v_gdn_dt_bias, v_gdn_A_log, v_gdn_norm_w, v_w_ssm_up, v_w_gdn_up, v_w_out, v_norm_mlp_pre, v_norm_mlp_post, v_w_mlp_up, v_w_mlp_down):
    S, Dm = x.shape[1], D_MODEL
    me = 4 * lax.axis_index("x") + 2 * lax.axis_index("y") + lax.axis_index("c")
    x2, tgt = x[0], loss_target[0]
    n_ada = w_ada.shape[2]
    given = dict(
        w_ada=(w_ada, m_w_ada, v_w_ada), b_ada=(b_ada, m_b_ada, v_b_ada),
        norm_mix_pre=(norm_mix_pre, m_norm_mix_pre, v_norm_mix_pre),
        norm_mix_post=(norm_mix_post, m_norm_mix_post, v_norm_mix_post), w_in=(w_in, m_w_in, v_w_in),
        ssm_conv_w=(ssm_conv_w, m_ssm_conv_w, v_ssm_conv_w), ssm_conv_b=(ssm_conv_b, m_ssm_conv_b, v_ssm_conv_b),
        ssm_dt_bias=(ssm_dt_bias, m_ssm_dt_bias, v_ssm_dt_bias), ssm_A_log=(ssm_A_log, m_ssm_A_log, v_ssm_A_log),
        ssm_D=(ssm_D, m_ssm_D, v_ssm_D), ssm_norm_w=(ssm_norm_w, m_ssm_norm_w, v_ssm_norm_w),
        gdn_conv_w=(gdn_conv_w, m_gdn_conv_w, v_gdn_conv_w), gdn_dt_bias=(gdn_dt_bias, m_gdn_dt_bias, v_gdn_dt_bias),
        gdn_A_log=(gdn_A_log, m_gdn_A_log, v_gdn_A_log), gdn_norm_w=(gdn_norm_w, m_gdn_norm_w, v_gdn_norm_w),
        w_ssm_up=(w_ssm_up, m_w_ssm_up, v_w_ssm_up), w_gdn_up=(w_gdn_up, m_w_gdn_up, v_w_gdn_up),
        w_out=(w_out, m_w_out, v_w_out), norm_mlp_pre=(norm_mlp_pre, m_norm_mlp_pre, v_norm_mlp_pre),
        norm_mlp_post=(norm_mlp_post, m_norm_mlp_post, v_norm_mlp_post), w_mlp_up=(w_mlp_up, m_w_mlp_up, v_w_mlp_up),
        w_mlp_down=(w_mlp_down, m_w_mlp_down, v_w_mlp_down))

    (c_all, scw, gcw, g_in) = _gather_two_level([c, ssm_conv_w[0], gdn_conv_w[0], w_in[0].astype(BF16)], "gather_w_in")
    c_all = c_all.reshape(N_DEV, Dm)
    wf = _by_cols(g_in)
    sp = W_IN_SPLITS
    w_main = jnp.concatenate([wf[:, sp[0]:sp[2]], wf[:, sp[3]:sp[5]], wf[:, sp[7]:sp[9]]], axis=1)
    w_small = jnp.concatenate([wf[:, sp[2]:sp[3]], wf[:, sp[5]:sp[7]], jnp.zeros((Dm, N_SMALL - 64), BF16)], axis=1)
    conv_w = jnp.concatenate([_by_cols(scw), _by_cols(gcw)], axis=1)
    conv_b = jnp.concatenate([ssm_conv_b, jnp.zeros_like(ssm_conv_b)], axis=1)

    b_loc = lax.dynamic_slice(b_ada, (0, me * n_ada), (1, n_ada))
    mod_part = _ada_fwd(c_all, w_ada[0], b_loc, "ada_fwd")
    (mod_rows,) = _exchange([mod_part.reshape(N_DEV, 1, n_ada)], ["a2a"], "exchange_mod")
    rest = _exchange_start([w_ssm_up[0].astype(BF16), w_gdn_up[0].astype(BF16), w_out[0].astype(BF16),
                            w_mlp_up[0].astype(BF16), w_mlp_down[0].astype(BF16)], ["gather"] * 5, mod_rows,
                           "gather_rest_start")
    mod = mod_rows.reshape(1, 6 * Dm) + rest[4][0:1, 0:1]
    sh1, sc1, g1, sh2, sc2, g2 = [mod[:, i * Dm:(i + 1) * Dm] for i in range(6)]

    h = _pre_fwd(x2, norm_mix_pre, sc1, sh1, "pre_mix")
    proj = _mm(h, w_main, S, N_MAIN, Dm, mode="nn", out_dtype=F32, name="proj_main")
    small = _mm(h, w_small, S, N_SMALL, Dm, mode="nn", out_dtype=F32, name="proj_small")
    conv = _conv_fwd(proj, conv_w, conv_b, "conv_fwd")
    dt_g, b_g, a_g = _heads_first(small[:, 0:32], 8), _heads_first(small[:, 32:48], 8), _heads_first(small[:, 48:64], 8)
    pv_ssm = jnp.stack([ssm_dt_bias.reshape(8, 4), ssm_A_log.reshape(8, 4), ssm_D.reshape(8, 4)], axis=1)
    nw_ssm = ssm_norm_w.reshape(8, 1, SSM_GROUP_WIDTH)
    pv_gdn = jnp.stack([gdn_dt_bias.reshape(8, 2), gdn_A_log.reshape(8, 2)], axis=1)
    y_ssd, ysn, st_ssm = _ssd_fwd(conv, proj, dt_g, pv_ssm, nw_ssm, "ssd_fwd")
    o_pre, ogn, st_gdn, t_inv = _gdn_fwd(conv, proj, b_g, a_g, pv_gdn, gdn_norm_w, "gdn_fwd")
    g_su, g_gu, g_out, g_mu, g_md = _exchange_wait(rest, ["gather"] * 5, ogn, "gather_rest_wait")
    w_su, w_gu = g_su.reshape(2 * Dm, Dm), g_gu.reshape(2 * Dm, Dm)
    w_o, w_mu, w_md = g_out.reshape(Dm, Dm), _by_cols(g_mu), g_md.reshape(4 * Dm, Dm)
    ys = _mm(ysn, w_su, S, Dm, 2 * Dm, mode="nn", out_dtype=F32, name="ssm_up")
    yg = _mm(ogn, w_gu, S, Dm, 2 * Dm, mode="nn", out_dtype=F32, name="gdn_up")
    merged = _merge_fwd(ys, yg, proj, "merge_fwd")
    mo = _mm(merged, w_o, S, Dm, Dm, mode="nn", out_dtype=F32, name="mix_out")
    x1 = _post_fwd(x2, mo, norm_mix_post, g1, "post_mix")
    h2 = _pre_fwd(x1, norm_mlp_pre, sc2, sh2, "pre_mlp")
    u, act = _mm(h2, w_mu, S, 4 * Dm, Dm, mode="nn", out_dtype=F32, epi="relu2", name="mlp_up")
    y_mlp = _mm(act, w_md, S, Dm, 4 * Dm, mode="nn", out_dtype=F32, name="mlp_down")
    dx2, loss_loc = _final_fwd(x1, y_mlp, norm_mlp_post, g2, tgt, "post_mlp_loss")

    dy, dg2, dw_post2 = _post_bwd(dx2, y_mlp, norm_mlp_post, g2, "post_mlp_bwd")
    du = _mm(dy, w_md, S, 4 * Dm, Dm, mode="nt", out_dtype=BF16, epi="drelu2", extra=u, name="mlp_down_dx")
    gw_md = _mm(act, dy, 4 * Dm, Dm, S, mode="tn", out_dtype=BF16, name="mlp_down_dw")
    dh2 = _mm(du, w_mu, S, Dm, 4 * Dm, mode="nt", out_dtype=F32, name="mlp_up_dx")
    gw_mu = _mm(h2, du, Dm, 4 * Dm, S, mode="tn", out_dtype=BF16, name="mlp_up_dw")
    mlp_x = _exchange_start([_to_col_shards(gw_mu), gw_md.reshape(N_DEV, -1, Dm)], ["a2a"] * 2, gw_md,
                            "grads_mlp_start")
    dx1, dsh2, dsc2, dw_pre2 = _pre_bwd(dh2, x1, norm_mlp_pre, sc2 + mlp_x[4][0:1, 0:1], dx2, "pre_mlp_bwd")
    dmo, dg1, dw_post1 = _post_bwd(dx1, mo, norm_mix_post, g1, "post_mix_bwd")
    dmerged = _mm(dmo, w_o, S, Dm, Dm, mode="nt", out_dtype=F32, name="mix_out_dx")
    gw_o = _mm(merged, dmo, Dm, Dm, S, mode="tn", out_dtype=BF16, name="mix_out_dw")
    dys, dyg, dgate = _merge_bwd(dmerged, ys, yg, proj, "merge_bwd")
    dysn = _mm(dys, w_su, S, 2 * Dm, Dm, mode="nt", out_dtype=F32, name="ssm_up_dx")
    gw_su = _mm(ysn, dys, 2 * Dm, Dm, S, mode="tn", out_dtype=BF16, name="ssm_up_dw")
    dogn = _mm(dyg, w_gu, S, 2 * Dm, Dm, mode="nt", out_dtype=F32, name="gdn_up_dx")
    gw_gu = _mm(ogn, dyg, 2 * Dm, Dm, S, mode="tn", out_dtype=BF16, name="gdn_up_dw")
    mix_x = _exchange_start([gw_su.reshape(N_DEV, -1, Dm), gw_gu.reshape(N_DEV, -1, Dm), gw_o.reshape(N_DEV, -1, Dm)],
                            ["a2a"] * 3, gw_gu, "grads_mix_start")
    dxs, dBm, dCm, dz_s, ddt_g, dpv_ssm, dnw_ssm = _ssd_bwd(dysn, conv, proj, dt_g, pv_ssm + mix_x[4][0, 0], nw_ssm,
                                                            y_ssd, st_ssm, "ssd_bwd")
    dq, dk, dv, dz_g, db_g, da_g, dpv_gdn, dnw_gdn = _gdn_bwd(dogn, conv, proj, b_g, a_g, pv_gdn, gdn_norm_w, o_pre,
                                                              st_gdn, t_inv, "gdn_bwd")
    conv_pieces = []
    for nm, d_act, col0 in (("xs", dxs, 0), ("B", dBm, 2048), ("C", dCm, 3072), ("q", dq, 4096), ("k", dk, 5120),
                            ("v", dv, 6144)):
        conv_pieces.append(_conv_bwd(d_act, proj, conv_w, conv_b, col0, "conv_bwd_" + nm) + (col0,))
    d_small = jnp.concatenate([_heads_last(ddt_g), _heads_last(db_g), _heads_last(da_g),
                               jnp.zeros((S, N_SMALL - 64), F32)], axis=1).astype(BF16)
    d_main = jnp.concatenate([dz_s] + [p[0] for p in conv_pieces] + [dz_g, dgate], axis=1)
    gw_small = _mm(h, d_small, Dm, N_SMALL, S, mode="tn", out_dtype=BF16, name="proj_small_dw")
    main_cols = _mm(h, d_main, Dm, N_MAIN, S, mode="tn", out_dtype=BF16, name="proj_main_dw")
    gw_in = jnp.concatenate([main_cols[:, 0:C_QKV], gw_small[:, 0:32], main_cols[:, C_QKV:C_GS], gw_small[:, 32:64],
                             main_cols[:, C_GS:N_MAIN]], axis=1)
    in_x = _exchange_start([_to_col_shards(gw_in)], ["a2a"], gw_small, "grads_w_in_start")
    dh = _mm(d_small, w_small + in_x[4][0:1, 0:1].astype(BF16), S, Dm, N_SMALL, mode="nt", out_dtype=F32,
             name="proj_small_dx")
    dh = _mm(d_main, w_main, S, Dm, N_MAIN, mode="nt", out_dtype=F32, add=dh, name="proj_main_dx")
    dx, dsh1, dsc1, dw_pre1 = _pre_bwd(dh, x2, norm_mix_pre, sc1, dx1, "pre_mix_bwd")
    r_mu, r_md = _exchange_wait(mlp_x, ["a2a"] * 2, dx, "grads_mlp_wait")
    r_su, r_gu, r_o = _exchange_wait(mix_x, ["a2a"] * 3, dx, "grads_mix_wait")

    dconv_w = jnp.concatenate([p[1] for p in conv_pieces], axis=1)
    dconv_b = jnp.concatenate([p[2] for p in conv_pieces[:3]], axis=1)
    dmod = jnp.concatenate([dsh1, dsc1, dg1, dsh2, dsc2, dg2], axis=1)
    small_vec = jnp.concatenate(
        [dmod, dw_pre1, dw_post1, dconv_b, dpv_ssm[:, 0].reshape(1, 32), dpv_ssm[:, 1].reshape(1, 32),
         dpv_ssm[:, 2].reshape(1, 32), dnw_ssm.reshape(1, 2048), dpv_gdn[:, 0].reshape(1, 16),
         dpv_gdn[:, 1].reshape(1, 16), jnp.sum(dnw_gdn, axis=0), dw_pre2, dw_post2, dconv_w.reshape(1, -1)], axis=1)
    n_vec = small_vec.shape[1]
    small_vec = jnp.pad(small_vec, ((0, 0), (0, (-n_vec) % 1024))).reshape(-1, 1024)
    (small_all,) = _exchange([small_vec], ["gather"], "gather_small_grads")
    small_all = small_all.reshape(N_DEV, -1)
    dmod_cols = lax.dynamic_slice(small_all, (0, me * n_ada), (N_DEV, n_ada))
    gw_ada = _ada_bwd(c_all.T, dmod_cols, "ada_bwd")
    conv_all = small_all[:, N_REPLICATED:n_vec].reshape(N_DEV, CONV_K, 2 * N_DEV * 512)
    conv_contrib = jnp.concatenate(
        [lax.dynamic_slice(conv_all, (0, 0, me * 512), (N_DEV, CONV_K, 512)),
         lax.dynamic_slice(conv_all, (0, 0, N_DEV * 512 + me * 512), (N_DEV, CONV_K, 512))], axis=1)
    rep_contrib = small_all[:, :N_REPLICATED].reshape(N_DEV, N_REPLICATED // 128, 128)

    results = {}

    def adam_big(nm, contrib):
        w3 = given[nm]
        res = _adam(contrib, w3[0][0], w3[1][0], w3[2][0], "adam_" + nm)
        results[nm] = tuple(r.reshape(w3[0].shape) for r in res)

    adam_big("w_ada", gw_ada[None])
    adam_big("w_ssm_up", r_su)
    adam_big("w_gdn_up", r_gu)
    adam_big("w_out", r_o)
    adam_big("w_mlp_up", r_mu)
    adam_big("w_mlp_down", r_md)
    (r_in,) = _exchange_wait(in_x, ["a2a"], results["w_mlp_down"][0], "grads_w_in_wait")
    adam_big("w_in", r_in)
    packed = [jnp.concatenate([given[nm][i] for nm in REPLICATED], axis=1).reshape(N_REPLICATED // 128, 128)
              for i in range(3)]
    rep_res = _adam(rep_contrib, packed[0], packed[1], packed[2], "adam_replicated")
    pos = 0
    for nm in REPLICATED:
        size = given[nm][0].shape[1]
        results[nm] = tuple(r.reshape(1, N_REPLICATED)[:, pos:pos + size] for r in rep_res)
        pos += size
    conv_wmv = [jnp.concatenate([given["ssm_conv_w"][i][0], given["gdn_conv_w"][i][0]], axis=0) for i in range(3)]
    conv_res = _adam(conv_contrib, conv_wmv[0], conv_wmv[1], conv_wmv[2], "adam_conv_w")
    results["ssm_conv_w"] = tuple(r[None, :CONV_K] for r in conv_res)
    results["gdn_conv_w"] = tuple(r[None, CONV_K:] for r in conv_res)

    loss = lax.psum(loss_loc[0, 0], ("x", "y", "c"))
    return (loss, dx[None]) + tuple(results[nm][i] for i in range(4) for nm in WEIGHTS)
```

```python
import jax
import jax.numpy as jnp
from jax import lax
from jax.experimental import pallas as pl
from jax.experimental.pallas import tpu as pltpu

F32 = jnp.float32
BF16 = jnp.bfloat16
N_DEV = 8
D_MODEL = 1024
EPS = 1e-6
CONV_K = 4
SSM_CHUNK = 128
SSM_HEAD_DIM = 64
SSM_D_STATE = 128
SSM_GROUPS = 8
SSM_HEADS_PER_GROUP = 4
SSM_GROUP_WIDTH = SSM_HEADS_PER_GROUP * SSM_HEAD_DIM
SSM_GROUPS_PER_STEP = 2
GDN_CHUNK = 64
GDN_HEAD = 128
GDN_QK_HEADS = 8
GDN_V_PER_QK = 2
GDN_QK_PER_STEP = 4
GDN_INV_BLOCK = 16
C_ZS, C_XBC, C_QKV, C_ZG, C_GS, C_GG, N_MAIN = 0, 2048, 6144, 10240, 12288, 13312, 14336
N_SMALL = 128
ADAM_LR, ADAM_B1, ADAM_B2, ADAM_EPS, ADAM_WD, ADAM_STEP = 0.001, 0.9, 0.999, 1e-08, 0.01, 10
VMEM_LIMIT = 56 * 1024 * 1024
NEG_INF = float("-inf")

_NT = (((1,), (1,)), ((), ()))
_NN = (((1,), (0,)), ((), ()))
_TN = (((0,), (0,)), ((), ()))


def _params(*sem):
    return pltpu.CompilerParams(dimension_semantics=sem, vmem_limit_bytes=VMEM_LIMIT)


def _dot(a, b, dims=_NN):
    return lax.dot_general(a.astype(BF16), b.astype(BF16), dims, preferred_element_type=F32)


def _split(a):
    hi = a.astype(BF16)
    return hi, (a - hi.astype(F32)).astype(BF16)


def _dot3(a, b, dims=_NN):
    ah, al = _split(a)
    bh, bl = _split(b)
    d = lambda u, v: lax.dot_general(u, v, dims, preferred_element_type=F32)
    return d(ah, bh) + (d(ah, bl) + d(al, bh))


def _sigmoid(x):
    return 1.0 / (1.0 + jnp.exp(-x))


def _silu(x):
    return x * _sigmoid(x)


def _dsilu(x):
    s = _sigmoid(x)
    return s * (1.0 + x * (1.0 - s))


def _softplus(x):
    return jnp.maximum(x, 0.0) + jnp.log1p(jnp.exp(-jnp.abs(x)))


def _iota(n, m, d):
    return lax.broadcasted_iota(jnp.int32, (n, m), d)


def _rowsum(x):
    return jnp.sum(x, axis=1, keepdims=True)


def _colsum(x):
    return jnp.sum(x, axis=0, keepdims=True)


def _total(x):
    return _rowsum(_colsum(x))


MXU_LANES = 128


def _parts(x, n):
    out = []
    for _ in range(n):
        p = x.astype(BF16)
        out.append(p)
        x = x - p.astype(F32)
    return out


def _sum_by(m01, x, dims=_NN, n=3):
    return sum(lax.dot_general(m01, p, dims, preferred_element_type=F32) for p in _parts(x, n))


def _row_col_sums(q):
    ones = jnp.ones((q.shape[0], MXU_LANES), BF16)
    acc = 0.0
    for p in _parts(q, 2):
        acc = acc + (lax.dot_general(p, ones, _NN, preferred_element_type=F32)
                     - lax.dot_general(p, ones, _TN, preferred_element_type=F32))
    return acc[:, 0:1]


def _cumsum_forms(col, ii, jj):
    lower = jnp.where(ii >= jj, 1.0, 0.0).astype(BF16)
    cum_col = _sum_by(lower, jnp.broadcast_to(col, (col.shape[0], MXU_LANES)))[:, 0:1]
    cum_row = _colsum(jnp.where(ii <= jj, col, 0.0))
    return cum_col, cum_row


def _rev_cumsum_col(col, ii, jj):
    upper = jnp.where(ii <= jj, 1.0, 0.0).astype(BF16)
    return _sum_by(upper, jnp.broadcast_to(col, (col.shape[0], MXU_LANES)))[:, 0:1]


def _blk(dim, pref):
    return pref if dim % pref == 0 else dim


def _lockstep(gens):
    gens = list(gens)
    while gens:
        alive = []
        for g in gens:
            try:
                next(g)
                alive.append(g)
            except StopIteration:
                pass
        gens = alive


def _mm(a, b, M, N, K, *, mode, out_dtype, name, a_off=(0, 0), b_off=(0, 0), add=None, epi=None, extra=None,
        tm=1024, tn=1024, tk=1024):
    tm, tn, tk = _blk(M, tm), _blk(N, tn), _blk(K, tk)
    nk = K // tk
    if mode == "tn":
        a_spec = pl.BlockSpec((tk, tm), lambda i, j, k: (k + a_off[0] // tk, i + a_off[1] // tm))
        assert a_off[0] % tk == 0 and a_off[1] % tm == 0
    else:
        a_spec = pl.BlockSpec((tm, tk), lambda i, j, k: (i + a_off[0] // tm, k + a_off[1] // tk))
        assert a_off[0] % tm == 0 and a_off[1] % tk == 0
    if mode == "nt":
        b_spec = pl.BlockSpec((tn, tk), lambda i, j, k: (j + b_off[0] // tn, k + b_off[1] // tk))
        assert b_off[0] % tn == 0 and b_off[1] % tk == 0
    else:
        b_spec = pl.BlockSpec((tk, tn), lambda i, j, k: (k + b_off[0] // tk, j + b_off[1] // tn))
        assert b_off[0] % tk == 0 and b_off[1] % tn == 0
    dims = {"nn": _NN, "nt": _NT, "tn": _TN}[mode]
    o_spec = pl.BlockSpec((tm, tn), lambda i, j, k: (i, j))
    ins, in_specs = [a, b], [a_spec, b_spec]
    if add is not None:
        ins.append(add)
        in_specs.append(o_spec)
    if extra is not None:
        ins.append(extra)
        in_specs.append(o_spec)
    n_in = len(ins)
    if epi == "relu2":
        out_shape = (jax.ShapeDtypeStruct((M, N), F32), jax.ShapeDtypeStruct((M, N), BF16))
        out_specs = (o_spec, o_spec)
    else:
        out_shape = jax.ShapeDtypeStruct((M, N), out_dtype)
        out_specs = o_spec

    def body(*refs):
        a_ref, b_ref = refs[0], refs[1]
        acc = refs[-1]
        outs = refs[n_in:-1]
        k = pl.program_id(2)

        @pl.when(k == 0)
        def _():
            acc[...] = jnp.zeros_like(acc)

        acc[...] += _dot(a_ref[...], b_ref[...], dims)

        @pl.when(k == nk - 1)
        def _():
            r = acc[...]
            pos = 2
            if add is not None:
                r = r + refs[pos][...]
                pos += 1
            if epi == "relu2":
                outs[0][...] = r
                p = jnp.maximum(r, 0.0)
                outs[1][...] = (p * p).astype(BF16)
            elif epi == "drelu2":
                outs[0][...] = (r * (2.0 * jnp.maximum(refs[pos][...], 0.0))).astype(out_dtype)
            else:
                outs[0][...] = r.astype(out_dtype)

    return pl.pallas_call(
        body, name=name, grid=(M // tm, N // tn, nk), in_specs=in_specs, out_specs=out_specs, out_shape=out_shape,
        scratch_shapes=[pltpu.VMEM((tm, tn), F32)],
        compiler_params=_params("parallel", "parallel", "arbitrary"))(*ins)


def _mm_pieces(pieces, other, M, N, K, *, mode, out_dtype, name, add=None, t=1024):
    tm, tn, tk = _blk(M, t), _blk(N, t), _blk(K, t // 2)
    nk = K // tk
    unit = tk if mode == "nt" else tn
    windows, pos = [], 0
    for p in pieces:
        assert p.shape[1] % unit == 0
        windows.append((pos, p.shape[1] // unit))
        pos += p.shape[1] // unit
    assert pos * unit == (K if mode == "nt" else N)

    def piece_spec(s0, cnt):
        if mode == "nt":
            def index(i, j, k):
                on = (k >= s0) & (k < s0 + cnt)
                return jnp.where(on, i, 0), jnp.clip(k - s0, 0, cnt - 1)
            return pl.BlockSpec((tm, tk), index)

        def index(i, j, k):
            on = (j >= s0) & (j < s0 + cnt)
            return jnp.where(on, k, 0), jnp.clip(j - s0, 0, cnt - 1)
        return pl.BlockSpec((tk, tn), index)

    if mode == "nt":
        other_spec = pl.BlockSpec((tn, tk), lambda i, j, k: (j, k))
    else:
        other_spec = pl.BlockSpec((tk, tm), lambda i, j, k: (k, i))
    o_spec = pl.BlockSpec((tm, tn), lambda i, j, k: (i, j))
    n_p = len(pieces)
    ins = list(pieces) + [other] + ([add] if add is not None else [])
    in_specs = [piece_spec(*w) for w in windows] + [other_spec] + ([o_spec] if add is not None else [])

    def body(*refs):
        piece_refs, other_ref = refs[:n_p], refs[n_p]
        o_ref, acc = refs[-2], refs[-1]
        j, k = pl.program_id(1), pl.program_id(2)
        sel = k if mode == "nt" else j

        @pl.when(k == 0)
        def _():
            acc[...] = jnp.zeros_like(acc)

        for p_ref, (s0, cnt) in zip(piece_refs, windows):
            @pl.when((sel >= s0) & (sel < s0 + cnt))
            def _(p_ref=p_ref):
                if mode == "nt":
                    acc[...] += _dot(p_ref[...], other_ref[...], _NT)
                else:
                    acc[...] += _dot(other_ref[...], p_ref[...], _TN)

        @pl.when(k == nk - 1)
        def _():
            r = acc[...]
            if add is not None:
                r = r + refs[n_p + 1][...]
            o_ref[...] = r.astype(out_dtype)

    return pl.pallas_call(
        body, name=name, grid=(M // tm, N // tn, nk), in_specs=in_specs, out_specs=o_spec,
        out_shape=jax.ShapeDtypeStruct((M, N), out_dtype), scratch_shapes=[pltpu.VMEM((tm, tn), F32)],
        compiler_params=_params("parallel", "parallel", "arbitrary"))(*ins)


def _row_spec(tb, d):
    return pl.BlockSpec((tb, d), lambda i: (i, 0))


def _vec_spec(d):
    return pl.BlockSpec((1, d), lambda i: (0, 0))


def _pre_fwd(x, w, sc, sh, name):
    S, Dm = x.shape
    tb = _blk(S, 512)

    def body(x_ref, w_ref, sc_ref, sh_ref, h_ref):
        xv = x_ref[...]
        r = lax.rsqrt(jnp.mean(xv * xv, axis=-1, keepdims=True) + EPS)
        h_ref[...] = ((xv * r * w_ref[...]) * (1.0 + sc_ref[...]) + sh_ref[...]).astype(BF16)

    return pl.pallas_call(
        body, name=name, grid=(S // tb,), in_specs=[_row_spec(tb, Dm)] + [_vec_spec(Dm)] * 3,
        out_specs=_row_spec(tb, Dm), out_shape=jax.ShapeDtypeStruct((S, Dm), BF16),
        compiler_params=_params("parallel"))(x, w, sc, sh)


def _post_fwd(x, y, w, g, name):
    S, Dm = x.shape
    tb = _blk(S, 512)

    def body(x_ref, y_ref, w_ref, g_ref, o_ref):
        yv = y_ref[...]
        r = lax.rsqrt(jnp.mean(yv * yv, axis=-1, keepdims=True) + EPS)
        o_ref[...] = x_ref[...] + g_ref[...] * (yv * r * w_ref[...])

    return pl.pallas_call(
        body, name=name, grid=(S // tb,), in_specs=[_row_spec(tb, Dm)] * 2 + [_vec_spec(Dm)] * 2,
        out_specs=_row_spec(tb, Dm), out_shape=jax.ShapeDtypeStruct((S, Dm), F32),
        compiler_params=_params("parallel"))(x, y, w, g)


def _final_fwd(x, y, w, g, target, name):
    S, Dm = x.shape
    tb = _blk(S, 512)
    nb = S // tb

    def body(x_ref, y_ref, w_ref, g_ref, t_ref, dx_ref, loss_ref, acc):
        i = pl.program_id(0)

        @pl.when(i == 0)
        def _():
            acc[...] = jnp.zeros_like(acc)

        yv = y_ref[...]
        r = lax.rsqrt(jnp.mean(yv * yv, axis=-1, keepdims=True) + EPS)
        e = (x_ref[...] + g_ref[...] * (yv * r * w_ref[...])) - t_ref[...]
        dx_ref[...] = e * (1.0 / Dm)
        acc[...] += _colsum(e * e)

        @pl.when(i == nb - 1)
        def _():
            loss_ref[...] = (0.5 / Dm) * _rowsum(acc[...])

    return pl.pallas_call(
        body, name=name, grid=(nb,), in_specs=[_row_spec(tb, Dm)] * 2 + [_vec_spec(Dm)] * 2 + [_row_spec(tb, Dm)],
        out_specs=(_row_spec(tb, Dm), pl.BlockSpec((1, 1), lambda i: (0, 0))),
        out_shape=(jax.ShapeDtypeStruct((S, Dm), F32), jax.ShapeDtypeStruct((1, 1), F32)),
        scratch_shapes=[pltpu.VMEM((1, Dm), F32)], compiler_params=_params("arbitrary"))(x, y, w, g, target)


def _post_bwd(dxo, y, w, g, name):
    S, Dm = y.shape
    tb = _blk(S, 512)

    def body(d_ref, y_ref, w_ref, g_ref, dy_ref, dg_ref, dw_ref):
        i = pl.program_id(0)

        @pl.when(i == 0)
        def _():
            dg_ref[...] = jnp.zeros_like(dg_ref)
            dw_ref[...] = jnp.zeros_like(dw_ref)

        yv, dv = y_ref[...], d_ref[...]
        r = lax.rsqrt(jnp.mean(yv * yv, axis=-1, keepdims=True) + EPS)
        yh = yv * r
        dg_ref[...] += _colsum(dv * (yh * w_ref[...]))
        dn = dv * g_ref[...]
        dw_ref[...] += _colsum(dn * yh)
        dyh = dn * w_ref[...]
        dy_ref[...] = (r * (dyh - yh * jnp.mean(dyh * yh, axis=-1, keepdims=True))).astype(BF16)

    return pl.pallas_call(
        body, name=name, grid=(S // tb,), in_specs=[_row_spec(tb, Dm)] * 2 + [_vec_spec(Dm)] * 2,
        out_specs=(_row_spec(tb, Dm), _vec_spec(Dm), _vec_spec(Dm)),
        out_shape=(jax.ShapeDtypeStruct((S, Dm), BF16), jax.ShapeDtypeStruct((1, Dm), F32),
                   jax.ShapeDtypeStruct((1, Dm), F32)),
        compiler_params=_params("arbitrary"))(dxo, y, w, g)


def _pre_bwd(dh, x, w, sc, dres, name):
    S, Dm = x.shape
    tb = _blk(S, 512)

    def body(dh_ref, x_ref, w_ref, sc_ref, dr_ref, dx_ref, dsh_ref, dsc_ref, dw_ref):
        i = pl.program_id(0)

        @pl.when(i == 0)
        def _():
            dsh_ref[...] = jnp.zeros_like(dsh_ref)
            dsc_ref[...] = jnp.zeros_like(dsc_ref)
            dw_ref[...] = jnp.zeros_like(dw_ref)

        xv, dv = x_ref[...], dh_ref[...]
        r = lax.rsqrt(jnp.mean(xv * xv, axis=-1, keepdims=True) + EPS)
        xh = xv * r
        one_sc = 1.0 + sc_ref[...]
        dsh_ref[...] += _colsum(dv)
        dsc_ref[...] += _colsum(dv * (xh * w_ref[...]))
        dw_ref[...] += _colsum(dv * one_sc * xh)
        dxh = dv * one_sc * w_ref[...]
        dx_ref[...] = dr_ref[...] + r * (dxh - xh * jnp.mean(dxh * xh, axis=-1, keepdims=True))

    vec = jax.ShapeDtypeStruct((1, Dm), F32)
    return pl.pallas_call(
        body, name=name, grid=(S // tb,),
        in_specs=[_row_spec(tb, Dm)] * 2 + [_vec_spec(Dm)] * 2 + [_row_spec(tb, Dm)],
        out_specs=(_row_spec(tb, Dm), _vec_spec(Dm), _vec_spec(Dm), _vec_spec(Dm)),
        out_shape=(jax.ShapeDtypeStruct((S, Dm), F32), vec, vec, vec),
        compiler_params=_params("arbitrary"))(dh, x, w, sc, dres)


def _merge_fwd(ys, yg, proj, name):
    S, Dm = ys.shape
    tb = _blk(S, 512)

    def body(ys_ref, yg_ref, gs_ref, gg_ref, o_ref):
        o_ref[...] = (_sigmoid(gs_ref[...]) * ys_ref[...] + _sigmoid(gg_ref[...]) * yg_ref[...]).astype(BF16)

    return pl.pallas_call(
        body, name=name, grid=(S // tb,),
        in_specs=[_row_spec(tb, Dm)] * 2 + [pl.BlockSpec((tb, Dm), lambda i: (i, C_GS // Dm)),
                                            pl.BlockSpec((tb, Dm), lambda i: (i, C_GG // Dm))],
        out_specs=_row_spec(tb, Dm), out_shape=jax.ShapeDtypeStruct((S, Dm), BF16),
        compiler_params=_params("parallel"))(ys, yg, proj, proj)


def _merge_bwd(dm, ys, yg, proj, name):
    S, Dm = ys.shape
    tb = _blk(S, 512)

    def body(dm_ref, ys_ref, yg_ref, gs_ref, gg_ref, dys_ref, dyg_ref, dgate_ref):
        d = dm_ref[...]
        ss, sg = _sigmoid(gs_ref[...]), _sigmoid(gg_ref[...])
        dys_ref[...] = (d * ss).astype(BF16)
        dyg_ref[...] = (d * sg).astype(BF16)
        dgate_ref[:, :Dm] = (d * ys_ref[...] * ss * (1.0 - ss)).astype(BF16)
        dgate_ref[:, Dm:] = (d * yg_ref[...] * sg * (1.0 - sg)).astype(BF16)

    return pl.pallas_call(
        body, name=name, grid=(S // tb,),
        in_specs=[_row_spec(tb, Dm)] * 3 + [pl.BlockSpec((tb, Dm), lambda i: (i, C_GS // Dm)),
                                            pl.BlockSpec((tb, Dm), lambda i: (i, C_GG // Dm))],
        out_specs=(_row_spec(tb, Dm), _row_spec(tb, Dm), _row_spec(tb, 2 * Dm)),
        out_shape=(jax.ShapeDtypeStruct((S, Dm), BF16), jax.ShapeDtypeStruct((S, Dm), BF16),
                   jax.ShapeDtypeStruct((S, 2 * Dm), BF16)),
        compiler_params=_params("parallel"))(dm, ys, yg, proj, proj)


CONV_COLS = 128


def _shift_down(x, k, rows):
    return jnp.where(rows >= k, pltpu.roll(x, k, 0), 0.0)


def _shift_up(x, k, rows, S):
    return jnp.where(rows < S - k, pltpu.roll(x, S - k, 0), 0.0)


def _conv_fwd(proj, w, b, name):
    S = proj.shape[0]
    n = w.shape[1]
    cb = CONV_COLS

    def body(x_ref, w_ref, b_ref, o_ref):
        x = x_ref[...]
        rows = _iota(S, cb, 0)
        pre = x * w_ref[CONV_K - 1:CONV_K, :] + b_ref[...]
        for k in range(1, CONV_K):
            pre = pre + _shift_down(x, k, rows) * w_ref[CONV_K - 1 - k:CONV_K - k, :]
        o_ref[...] = _silu(pre)

    return pl.pallas_call(
        body, name=name, grid=(n // cb,),
        in_specs=[pl.BlockSpec((S, cb), lambda j: (0, j + C_XBC // cb)), pl.BlockSpec((CONV_K, cb), lambda j: (0, j)),
                  pl.BlockSpec((1, cb), lambda j: (0, j))],
        out_specs=pl.BlockSpec((S, cb), lambda j: (0, j)), out_shape=jax.ShapeDtypeStruct((S, n), F32),
        compiler_params=_params("parallel"))(proj, w, b)


def _conv_bwd(dact, proj, w, b, col0, name):
    S, n = dact.shape
    cb = CONV_COLS
    o = col0 // cb

    def body(d_ref, x_ref, w_ref, b_ref, dx_ref, dw_ref, db_ref):
        x = x_ref[...]
        rows = _iota(S, cb, 0)
        xs = [x] + [_shift_down(x, k, rows) for k in range(1, CONV_K)]
        pre = xs[0] * w_ref[CONV_K - 1:CONV_K, :] + b_ref[...]
        for k in range(1, CONV_K):
            pre = pre + xs[k] * w_ref[CONV_K - 1 - k:CONV_K - k, :]
        dpre = d_ref[...] * _dsilu(pre)
        db_ref[...] = _colsum(dpre)
        dx = dpre * w_ref[CONV_K - 1:CONV_K, :]
        for k in range(CONV_K):
            dw_ref[CONV_K - 1 - k:CONV_K - k, :] = _colsum(dpre * xs[k])
            if k:
                dx = dx + _shift_up(dpre, k, rows, S) * w_ref[CONV_K - 1 - k:CONV_K - k, :]
        dx_ref[...] = dx.astype(BF16)

    return pl.pallas_call(
        body, name=name, grid=(n // cb,),
        in_specs=[pl.BlockSpec((S, cb), lambda j: (0, j)), pl.BlockSpec((S, cb), lambda j: (0, j + o + C_XBC // cb)),
                  pl.BlockSpec((CONV_K, cb), lambda j: (0, j + o)), pl.BlockSpec((1, cb), lambda j: (0, j + o))],
        out_specs=(pl.BlockSpec((S, cb), lambda j: (0, j)), pl.BlockSpec((CONV_K, cb), lambda j: (0, j)),
                   pl.BlockSpec((1, cb), lambda j: (0, j))),
        out_shape=(jax.ShapeDtypeStruct((S, n), BF16), jax.ShapeDtypeStruct((CONV_K, n), F32),
                   jax.ShapeDtypeStruct((1, n), F32)),
        compiler_params=_params("parallel"))(dact, proj, w, b)


def _ssd_specs(L, order):
    G = SSM_GROUPS_PER_STEP
    W, N = G * SSM_GROUP_WIDTH, G * SSM_D_STATE
    x_spec = pl.BlockSpec((L, W), lambda g, c: (order(c), g))
    b_spec = pl.BlockSpec((L, N), lambda g, c: (order(c), 2048 // N + g))
    c_spec = pl.BlockSpec((L, N), lambda g, c: (order(c), 3072 // N + g))
    z_spec = pl.BlockSpec((L, W), lambda g, c: (order(c), C_ZS // W + g))
    dt_spec = pl.BlockSpec((G, L, SSM_HEADS_PER_GROUP), lambda g, c: (g, order(c), 0))
    p_spec = pl.BlockSpec((G, 3, SSM_HEADS_PER_GROUP), lambda g, c: (g, 0, 0))
    nw_spec = pl.BlockSpec((G, 1, SSM_GROUP_WIDTH), lambda g, c: (g, 0, 0))
    s_spec = pl.BlockSpec((G, 1, SSM_GROUP_WIDTH, SSM_D_STATE), lambda g, c: (g, order(c), 0, 0))
    return x_spec, b_spec, c_spec, z_spec, dt_spec, p_spec, nw_spec, s_spec


class _SsdGroup:
    def __init__(self, L):
        P, H, W = SSM_HEAD_DIM, SSM_HEADS_PER_GROUP, SSM_GROUP_WIDTH
        self.L = L
        self.ii, self.jj = _iota(L, L, 0), _iota(L, L, 1)
        self.lower = jnp.where(self.ii >= self.jj, 1.0, 0.0).astype(BF16)
        self.upper = jnp.where(self.ii <= self.jj, 1.0, 0.0).astype(BF16)
        self.lo = _iota(L, 2 * P, 1) < P
        self.lo_row = _iota(1, 2 * P, 1) < P
        bi, bj = _iota(W, W, 0), _iota(W, W, 1)
        self.block = jnp.where(bi // P == bj // P, 1.0, 0.0).astype(BF16)
        si, sj = _iota(2 * P, W, 0), _iota(2 * P, W, 1)
        self.pick = jnp.where(sj == si * P, 1.0, 0.0).astype(BF16)
        self.ones = jnp.ones((L, 2 * P), BF16)

    def spread(self, v4):
        R = v4.shape[0]
        lo = self.lo if R == self.L else self.lo_row
        b = lambda h: jnp.broadcast_to(v4[:, h:h + 1], (R, 2 * SSM_HEAD_DIM))
        return jnp.concatenate([jnp.where(lo, b(0), b(1)), jnp.where(lo, b(2), b(3))], axis=1)

    def gather4(self, v):
        return jnp.concatenate([v[:, h * SSM_HEAD_DIM:h * SSM_HEAD_DIM + 1] for h in range(SSM_HEADS_PER_GROUP)],
                               axis=1)

    def head_sums(self, z):
        return sum(lax.dot_general(p, self.block, _NN, preferred_element_type=F32) for p in _parts(z, 2))

    def pair_cols(self, full, pair):
        ps = full[:, pair * 128:(pair + 1) * 128]
        sw = pltpu.roll(ps, SSM_HEAD_DIM, 1)
        return jnp.where(self.lo, ps, sw), jnp.where(self.lo, sw, ps)

    def gates(self, dt4_raw, p):
        L = self.L
        dtr = self.spread(dt4_raw + p[0:1, :])
        dt = _softplus(dtr)
        A = self.spread(-jnp.exp(p[1:2, :]))
        acum = _sum_by(self.lower, dt * A)
        yield
        rows = _sum_by(self.pick, acum, _NT)
        yield
        a_last = acum[L - 1:L, :]
        cols = self.pair_cols(acum, 0) + self.pair_cols(acum, 1)
        decay, decay_t = [], []
        for h in range(SSM_HEADS_PER_GROUP):
            seg = cols[h] - rows[h:h + 1, :]
            decay.append(jnp.exp(jnp.where(self.ii >= self.jj, seg, NEG_INF)))
            decay_t.append(jnp.exp(jnp.where(self.jj >= self.ii, -seg, NEG_INF)))
        return dict(dtr=dtr, dt=dt, A=A, D=self.spread(p[2:3, :]), acum=acum, eac=jnp.exp(acum), a_last=a_last,
                    wdec=jnp.exp(a_last - acum), decay=decay, decay_t=decay_t,
                    ea_last=[jnp.exp(rows[h:h + 1, L - 1:L]) for h in range(SSM_HEADS_PER_GROUP)])


def _ssd_fwd(conv, proj, dt_raw, pvec, nw, name):
    S = conv.shape[0]
    L, P, N, H, W, G = SSM_CHUNK, SSM_HEAD_DIM, SSM_D_STATE, SSM_HEADS_PER_GROUP, SSM_GROUP_WIDTH, SSM_GROUPS_PER_STEP
    nc = S // L

    def body(x_ref, b_ref, c_ref, z_ref, dt_ref, p_ref, nw_ref, y_ref, yn_ref, s0_ref, state):
        c = pl.program_id(1)

        @pl.when(c == 0)
        def _():
            state[...] = jnp.zeros_like(state)

        k = _SsdGroup(L)

        def group(gi):
            gsl = slice(gi * W, (gi + 1) * W)
            Bm, Cm = b_ref[:, gi * N:(gi + 1) * N], c_ref[:, gi * N:(gi + 1) * N]
            x = x_ref[:, gsl]
            S0 = state[gsl, :]
            s0_ref[gi, 0] = S0
            CB = _dot(Cm, Bm, _NT)
            y_off = _dot(Cm, S0, _NT)
            t = yield from k.gates(dt_ref[gi], p_ref[gi])
            xdt = x * t["dt"]
            s_new = _dot(xdt * t["wdec"], Bm, _TN)
            y_diag = []
            for pair in range(H // 2):
                xp = xdt[:, pair * 128:(pair + 1) * 128]
                y_diag.append(jnp.where(k.lo, _dot(CB * t["decay"][2 * pair], xp),
                                        _dot(CB * t["decay"][2 * pair + 1], xp)))
            yield
            y = jnp.concatenate(y_diag, axis=1) + y_off * t["eac"]
            for h in range(H):
                hsl = slice(gi * W + h * P, gi * W + (h + 1) * P)
                state[hsl, :] = S0[h * P:(h + 1) * P, :] * t["ea_last"][h] + s_new[h * P:(h + 1) * P, :]
            y_ref[:, gsl] = y
            y2 = (y + t["D"] * x) * _silu(z_ref[:, gsl])
            r = lax.rsqrt(jnp.mean(y2 * y2, axis=-1, keepdims=True) + EPS)
            yn_ref[:, gsl] = (y2 * r * nw_ref[gi]).astype(BF16)

        _lockstep(group(gi) for gi in range(G))

    x_spec, b_spec, c_spec, z_spec, dt_spec, p_spec, nw_spec, s_spec = _ssd_specs(L, lambda c: c)
    y_spec = pl.BlockSpec((L, G * W), lambda g, c: (c, g))
    return pl.pallas_call(
        body, name=name, grid=(SSM_GROUPS // G, nc),
        in_specs=[x_spec, b_spec, c_spec, z_spec, dt_spec, p_spec, nw_spec],
        out_specs=(y_spec, y_spec, s_spec),
        out_shape=(jax.ShapeDtypeStruct((S, SSM_GROUPS * W), F32), jax.ShapeDtypeStruct((S, SSM_GROUPS * W), BF16),
                   jax.ShapeDtypeStruct((SSM_GROUPS, nc, W, N), F32)),
        scratch_shapes=[pltpu.VMEM((G * W, N), F32)],
        compiler_params=_params("parallel", "arbitrary"))(conv, conv, conv, proj, dt_raw, pvec, nw)


def _ssd_bwd(dyn, conv, proj, dt_raw, pvec, nw, y_ssd, states, name):
    S = conv.shape[0]
    L, P, N, H, W, G = SSM_CHUNK, SSM_HEAD_DIM, SSM_D_STATE, SSM_HEADS_PER_GROUP, SSM_GROUP_WIDTH, SSM_GROUPS_PER_STEP
    nc = S // L

    def body(dyn_ref, x_ref, b_ref, c_ref, z_ref, dt_ref, p_ref, nw_ref, y_ref, s0_ref,
             dx_ref, db_ref, dc_ref, dz_ref, ddt_ref, dp_ref, dnw_ref, dstate):
        c = pl.program_id(1)

        @pl.when(c == 0)
        def _():
            dstate[...] = jnp.zeros_like(dstate)
            dp_ref[...] = jnp.zeros_like(dp_ref)
            dnw_ref[...] = jnp.zeros_like(dnw_ref)

        k = _SsdGroup(L)
        last = (_iota(L, 1, 0) == L - 1)

        def group(gi):
            gsl = slice(gi * W, (gi + 1) * W)
            Bm, Cm = b_ref[:, gi * N:(gi + 1) * N], c_ref[:, gi * N:(gi + 1) * N]
            x, z = x_ref[:, gsl], z_ref[:, gsl]
            S0, dS1 = s0_ref[gi, 0], dstate[gsl, :]
            CB = _dot(Cm, Bm, _NT)
            CBt = _dot(Bm, Cm, _NT)
            y_off_raw = _dot(Cm, S0, _NT)
            dXs_raw = _dot(Bm, dS1, _NT)
            t = yield from k.gates(dt_ref[gi], p_ref[gi])
            y1 = y_ref[:, gsl] + t["D"] * x
            sz = _silu(z)
            y2 = y1 * sz
            r = lax.rsqrt(jnp.mean(y2 * y2, axis=-1, keepdims=True) + EPS)
            y2h = y2 * r
            dyn_v = dyn_ref[:, gsl]
            dnw_ref[gi] += _colsum(dyn_v * y2h)
            dy2h = dyn_v * nw_ref[gi]
            dy2 = r * (dy2h - y2h * jnp.mean(dy2h * y2h, axis=-1, keepdims=True))
            dz_ref[:, gsl] = (dy2 * y1 * _dsilu(z)).astype(BF16)
            dY = dy2 * sz
            X = x * t["dt"]
            dYe = dY * t["eac"]
            dC_s = _dot(dYe, S0)
            dB_s = _dot(X * t["wdec"], dS1)
            dS_c = _dot(dYe, Cm, _TN)
            dXm, Gs, Gts = [], [], []
            for pair in range(H // 2):
                dYp, Xp = dY[:, pair * 128:(pair + 1) * 128], X[:, pair * 128:(pair + 1) * 128]
                dXm.append(jnp.where(k.lo, _dot(CBt * t["decay_t"][2 * pair], dYp),
                                     _dot(CBt * t["decay_t"][2 * pair + 1], dYp)))
                for mask in (k.lo, ~k.lo):
                    Gs.append(_dot(jnp.where(mask, dYp, 0.0), Xp, _NT))
                    Gts.append(_dot(jnp.where(mask, Xp, 0.0), dYp, _NT))
            yield
            dXs = dXs_raw * t["wdec"]
            dX = jnp.concatenate(dXm, axis=1) + dXs
            dCB, dCBt, q_sums = 0.0, 0.0, []
            for h in range(H):
                M, Mt = CB * t["decay"][h], CBt * t["decay_t"][h]
                dCB = dCB + Gs[h] * t["decay"][h]
                dCBt = dCBt + Gts[h] * t["decay_t"][h]
                d = Gs[h] * M - Gts[h] * Mt
                q_sums.append(sum(lax.dot_general(pt, k.ones, _NN, preferred_element_type=F32)
                                  for pt in _parts(d, 2)))
            q_f = jnp.concatenate([jnp.where(k.lo, q_sums[0], q_sums[1]), jnp.where(k.lo, q_sums[2], q_sums[3])],
                                  axis=1)
            x_dxs = k.head_sums(X * dXs)
            tot = [_total(dS1[h * P:(h + 1) * P, :] * S0[h * P:(h + 1) * P, :]) * t["ea_last"][h] for h in range(H)]
            tot_f = k.spread(jnp.concatenate(tot, axis=1))
            d_alast = _colsum(x_dxs) + tot_f
            dacum = q_f + k.head_sums(dY * (y_off_raw * t["eac"])) - x_dxs + jnp.where(last, d_alast, 0.0)
            dx_dt = k.head_sums(dX * x)
            d_skip = _colsum(k.head_sums(dY * x))
            for h in range(H):
                hsl = slice(gi * W + h * P, gi * W + (h + 1) * P)
                dstate[hsl, :] = t["ea_last"][h] * dS1[h * P:(h + 1) * P, :] + dS_c[h * P:(h + 1) * P, :]
            dc_s2 = _dot(dCB, Bm)
            db_s2 = _dot(dCBt, Cm)
            yield
            da = _sum_by(k.upper, dacum)
            yield
            ddt_raw = (da * t["A"] + dx_dt) * _sigmoid(t["dtr"])
            dx_ref[:, gsl] = dX * t["dt"] + t["D"] * dY
            dc_ref[:, gi * N:(gi + 1) * N] = dC_s + dc_s2
            db_ref[:, gi * N:(gi + 1) * N] = dB_s + db_s2
            ddt_ref[gi] = k.gather4(ddt_raw)
            dp_ref[gi] += k.gather4(jnp.concatenate([_colsum(ddt_raw), _colsum(da * t["dt"]) * t["A"], d_skip],
                                                    axis=0))

        _lockstep(group(gi) for gi in range(G))

    rev = lambda c: nc - 1 - c
    x_spec, b_spec, c_spec, z_spec, dt_spec, p_spec, nw_spec, s_spec = _ssd_specs(L, rev)
    y_spec = pl.BlockSpec((L, G * W), lambda g, c: (rev(c), g))
    n_spec = pl.BlockSpec((L, G * N), lambda g, c: (rev(c), g))
    return pl.pallas_call(
        body, name=name, grid=(SSM_GROUPS // G, nc),
        in_specs=[y_spec, x_spec, b_spec, c_spec, z_spec, dt_spec, p_spec, nw_spec, y_spec, s_spec],
        out_specs=(y_spec, n_spec, n_spec, y_spec, dt_spec, p_spec, nw_spec),
        out_shape=(jax.ShapeDtypeStruct((S, SSM_GROUPS * W), F32), jax.ShapeDtypeStruct((S, SSM_GROUPS * N), F32),
                   jax.ShapeDtypeStruct((S, SSM_GROUPS * N), F32), jax.ShapeDtypeStruct((S, SSM_GROUPS * W), BF16),
                   jax.ShapeDtypeStruct((SSM_GROUPS, S, H), F32), jax.ShapeDtypeStruct((SSM_GROUPS, 3, H), F32),
                   jax.ShapeDtypeStruct((SSM_GROUPS, 1, W), F32)),
        scratch_shapes=[pltpu.VMEM((G * W, N), F32)],
        compiler_params=_params("parallel", "arbitrary"))(dyn, conv, conv, conv, proj, dt_raw, pvec, nw, y_ssd, states)


def _unit_lower_inverse(A, ii, jj):
    eye = (ii == jj).astype(F32)
    same = (ii // GDN_INV_BLOCK) == (jj // GDN_INV_BLOCK)
    Ad = jnp.where(same, A, 0.0)
    Ao = A - Ad
    P2 = _dot3(Ad, Ad)
    yield
    P4, X = _dot3(P2, P2), _dot3(eye - Ad, eye + P2)
    yield
    P8, X = _dot3(P4, P4), _dot3(X, eye + P4)
    yield
    X = _dot3(X, eye + P8)
    yield
    Bm = _dot3(X, Ao)
    yield
    B2 = _dot3(Bm, Bm)
    yield
    Y = _dot3(eye - Bm, eye + B2)
    yield
    T = _dot3(Y, X)
    yield
    return T


def _gdn_specs(L, order):
    G = GDN_QK_PER_STEP
    Hd, W = G * GDN_HEAD, G * GDN_V_PER_QK * GDN_HEAD
    q_spec = pl.BlockSpec((L, Hd), lambda h, c: (order(c), (C_QKV - C_XBC) // Hd + h))
    k_spec = pl.BlockSpec((L, Hd), lambda h, c: (order(c), (C_QKV - C_XBC + 1024) // Hd + h))
    v_spec = pl.BlockSpec((L, W), lambda h, c: (order(c), (C_QKV - C_XBC + 2048) // W + h))
    z_spec = pl.BlockSpec((L, W), lambda h, c: (order(c), C_ZG // W + h))
    ba_spec = pl.BlockSpec((G, L, GDN_V_PER_QK), lambda h, c: (h, order(c), 0))
    p_spec = pl.BlockSpec((G, 2, GDN_V_PER_QK), lambda h, c: (h, 0, 0))
    nw_spec = pl.BlockSpec((1, GDN_HEAD), lambda h, c: (0, 0))
    s_spec = pl.BlockSpec((G, 1, GDN_V_PER_QK * GDN_HEAD, GDN_HEAD), lambda h, c: (h, order(c), 0, 0))
    t_spec = pl.BlockSpec((G * GDN_V_PER_QK, 1, L, L), lambda h, c: (h, order(c), 0, 0))
    return q_spec, k_spec, v_spec, z_spec, ba_spec, p_spec, nw_spec, s_spec, t_spec


def _gdn_gates(qa, ka, b_col, a_col, p, j, ii, jj):
    L = qa.shape[0]
    sp_in = a_col + p[0:1, j:j + 1]
    neg_ea = -jnp.exp(p[1:2, j:j + 1])
    g = neg_ea * _softplus(sp_in)
    gcum, gcum_row = _cumsum_forms(g, ii, jj)
    rq = lax.rsqrt(_rowsum(qa * qa) + EPS)
    rk = lax.rsqrt(_rowsum(ka * ka) + EPS)
    q = qa * rq * (GDN_HEAD ** -0.5)
    k = ka * rk
    beta = _sigmoid(b_col)
    yield
    Dm = jnp.exp(jnp.where(ii >= jj, gcum - gcum_row, NEG_INF))
    eg = jnp.exp(gcum)
    g_last = gcum[L - 1:L, :]
    wdec = jnp.exp(g_last - gcum)
    return dict(rq=rq, rk=rk, q=q, k=k, beta=beta, sp_in=sp_in, neg_ea=neg_ea, g=g, Dm=Dm, kbeta=k * beta, eg=eg,
                g_last=g_last, wdec=wdec, kdec=k * wdec)


def _gdn_fwd(conv, proj, b_raw, a_raw, pvec, nw, name):
    S = conv.shape[0]
    L, Hd, J, G = GDN_CHUNK, GDN_HEAD, GDN_V_PER_QK, GDN_QK_PER_STEP
    W = J * Hd
    nc = S // L

    def body(q_ref, k_ref, v_ref, z_ref, b_ref, a_ref, p_ref, nw_ref, o_ref, on_ref, s0_ref, t_ref, state):
        c = pl.program_id(1)

        @pl.when(c == 0)
        def _():
            state[...] = jnp.zeros_like(state)

        ii, jj = _iota(L, L, 0), _iota(L, L, 1)
        for hq in range(G):
            s0_ref[hq, 0] = state[hq * W:(hq + 1) * W, :]

        def head(hq, j):
            hd = hq * J + j
            hsl, sl = slice(hq * Hd, (hq + 1) * Hd), slice(hd * Hd, (hd + 1) * Hd)
            t = yield from _gdn_gates(q_ref[:, hsl], k_ref[:, hsl], b_ref[hq][:, j:j + 1], a_ref[hq][:, j:j + 1],
                                      p_ref[hq], j, ii, jj)
            KK = _dot(t["kbeta"], t["k"], _NT)
            QK = _dot(t["q"], t["k"], _NT)
            yield
            T = yield from _unit_lower_inverse(jnp.where(ii > jj, KK * t["Dm"], 0.0), ii, jj)
            t_ref[hd, 0] = T
            S0 = state[sl, :]
            U = _dot3(T, v_ref[:, sl] * t["beta"])
            Wm = _dot3(T, t["kbeta"] * t["eg"])
            o_inter = _dot(t["q"] * t["eg"], S0)
            yield
            Vn = U - _dot(Wm, S0)
            yield
            o = o_inter + _dot(QK * t["Dm"], Vn)
            s_new = _dot(t["kdec"], Vn, _TN)
            yield
            state[sl, :] = S0 * jnp.exp(t["g_last"]) + s_new
            o_ref[:, sl] = o
            r = lax.rsqrt(jnp.mean(o * o, axis=-1, keepdims=True) + EPS)
            on_ref[:, sl] = ((o * r * nw_ref[...]) * _silu(z_ref[:, sl])).astype(BF16)

        _lockstep(head(hq, j) for hq in range(G) for j in range(J))

    q_spec, k_spec, v_spec, z_spec, ba_spec, p_spec, nw_spec, s_spec, t_spec = _gdn_specs(L, lambda c: c)
    o_spec = pl.BlockSpec((L, G * W), lambda h, c: (c, h))
    return pl.pallas_call(
        body, name=name, grid=(GDN_QK_HEADS // G, nc),
        in_specs=[q_spec, k_spec, v_spec, z_spec, ba_spec, ba_spec, p_spec, nw_spec],
        out_specs=(o_spec, o_spec, s_spec, t_spec),
        out_shape=(jax.ShapeDtypeStruct((S, GDN_QK_HEADS * W), F32), jax.ShapeDtypeStruct((S, GDN_QK_HEADS * W), BF16),
                   jax.ShapeDtypeStruct((GDN_QK_HEADS, nc, W, Hd), F32),
                   jax.ShapeDtypeStruct((GDN_QK_HEADS * J, nc, L, L), F32)),
        scratch_shapes=[pltpu.VMEM((G * W, Hd), F32)],
        compiler_params=_params("parallel", "arbitrary"))(conv, conv, conv, proj, b_raw, a_raw, pvec, nw)


def _gdn_bwd(don, conv, proj, b_raw, a_raw, pvec, nw, o_pre, states, t_inv, name):
    S = conv.shape[0]
    L, Hd, J, G = GDN_CHUNK, GDN_HEAD, GDN_V_PER_QK, GDN_QK_PER_STEP
    W = J * Hd
    nc = S // L

    def body(don_ref, q_ref, k_ref, v_ref, z_ref, b_ref, a_ref, p_ref, nw_ref, o_ref, s0_ref, t_ref,
             dq_ref, dk_ref, dv_ref, dz_ref, db_ref, da_ref, dp_ref, dnw_ref, dstate):
        c = pl.program_id(1)

        @pl.when(c == 0)
        def _():
            dstate[...] = jnp.zeros_like(dstate)
            dp_ref[...] = jnp.zeros_like(dp_ref)
            dnw_ref[...] = jnp.zeros_like(dnw_ref)

        ii, jj = _iota(L, L, 0), _iota(L, L, 1)
        last = (_iota(L, 1, 0) == L - 1)
        res = {}

        def head(hq, j):
            hd = hq * J + j
            hsl, sl = slice(hq * Hd, (hq + 1) * Hd), slice(hd * Hd, (hd + 1) * Hd)
            qa, ka = q_ref[:, hsl], k_ref[:, hsl]
            t = yield from _gdn_gates(qa, ka, b_ref[hq][:, j:j + 1], a_ref[hq][:, j:j + 1], p_ref[hq], j, ii, jj)
            q, k, beta, eg, Dm, kbeta, kdec = (t[nm] for nm in ("q", "k", "beta", "eg", "Dm", "kbeta", "kdec"))
            T = t_ref[hd, 0]
            v, z, o = v_ref[:, sl], z_ref[:, sl], o_ref[:, sl]
            S0, dS1 = s0_ref[hq, 0, j * Hd:(j + 1) * Hd, :], dstate[sl, :]
            sz = _silu(z)
            r = lax.rsqrt(jnp.mean(o * o, axis=-1, keepdims=True) + EPS)
            oh = o * r
            d_on = don_ref[:, sl]
            dz_ref[:, sl] = (d_on * (oh * nw_ref[...]) * _dsilu(z)).astype(BF16)
            dn = d_on * sz
            dnw_part = _colsum(dn * oh)
            doh = dn * nw_ref[...]
            dO = r * (doh - oh * jnp.mean(doh * oh, axis=-1, keepdims=True))
            Rw = kbeta * eg
            qe = q * eg
            U = _dot3(T, v * beta)
            Wm = _dot3(T, Rw)
            KK = _dot(kbeta, k, _NT)
            QK = _dot(q, k, _NT)
            o_inter = _dot(qe, S0)
            dq_s = _dot(dO, S0, _NT)
            dS_q = _dot(qe, dO, _TN)
            yield
            Am = jnp.where(ii > jj, KK * Dm, 0.0)
            Pm = QK * Dm
            Vn = U - _dot(Wm, S0)
            dVn_s = _dot(kdec, dS1)
            yield
            dVn = _dot(Pm, dO, _TN) + dVn_s
            dP = _dot(dO, Vn, _NT)
            dKd = _dot(Vn, dS1, _NT)
            yield
            dQK = dP * Dm
            dq = _dot(dQK, k) + dq_s * eg
            dk = _dot(dQK, q, _TN) + dKd * t["wdec"]
            dstate[sl, :] = jnp.exp(t["g_last"]) * dS1 + dS_q - _dot(Wm, dVn, _TN)
            dW = -_dot(dVn, S0, _NT)
            dRu = _dot3(T, dVn, _TN)
            yield
            dRw = _dot3(T, dW, _TN)
            dA_u = _dot(dRu, U, _NT)
            yield
            dA = jnp.where(ii > jj, -(dA_u + _dot(dRw, Wm, _NT)), 0.0)
            yield
            dKK = dA * Dm
            dkbeta = _dot(dKK, k) + dRw * eg
            dk = dk + _dot(dKK, kbeta, _TN)
            yield
            dk = dk + dkbeta * beta
            dbeta = _rowsum(dkbeta * k) + _rowsum(dRu * v)
            dv_ref[:, sl] = dRu * beta
            Q = dA * Am + dP * Pm
            rho = _rowsum(dKd * kdec)
            d_glast = _colsum(rho) + jnp.exp(t["g_last"]) * _total(dS1 * S0)
            q_sums = _row_col_sums(Q)
            rest = _rowsum(dRw * Rw) + _rowsum(dO * o_inter) - rho + jnp.where(last, d_glast, 0.0)
            yield
            dg = _rev_cumsum_col(q_sums + rest, ii, jj)
            yield
            da_raw = dg * t["neg_ea"] * _sigmoid(t["sp_in"])
            res[hq, j] = dict(dq=dq, dk=dk, db=dbeta * beta * (1.0 - beta), da=da_raw, d_bias=_colsum(da_raw),
                              d_alog=_colsum(dg * t["g"]), dnw=dnw_part, rq=t["rq"], rk=t["rk"], k=k, qh=qa * t["rq"])

        _lockstep(head(hq, j) for hq in range(G) for j in range(J))
        for hq in range(G):
            parts = [res[hq, j] for j in range(J)]
            hsl = slice(hq * Hd, (hq + 1) * Hd)
            p0 = parts[0]
            dqh = sum(pt["dq"] for pt in parts) * (GDN_HEAD ** -0.5)
            dkn = sum(pt["dk"] for pt in parts)
            dq_ref[:, hsl] = p0["rq"] * (dqh - p0["qh"] * _rowsum(dqh * p0["qh"]))
            dk_ref[:, hsl] = p0["rk"] * (dkn - p0["k"] * _rowsum(dkn * p0["k"]))
            db_ref[hq] = jnp.concatenate([pt["db"] for pt in parts], axis=1)
            da_ref[hq] = jnp.concatenate([pt["da"] for pt in parts], axis=1)
            dp_ref[hq] += jnp.concatenate([jnp.concatenate([pt["d_bias"] for pt in parts], axis=1),
                                           jnp.concatenate([pt["d_alog"] for pt in parts], axis=1)], axis=0)
            dnw_ref[hq] += sum(pt["dnw"] for pt in parts)

    rev = lambda c: nc - 1 - c
    q_spec, k_spec, v_spec, z_spec, ba_spec, p_spec, nw_spec, s_spec, t_spec = _gdn_specs(L, rev)
    o_spec = pl.BlockSpec((L, G * W), lambda h, c: (rev(c), h))
    h_spec = pl.BlockSpec((L, G * Hd), lambda h, c: (rev(c), h))
    dnw_spec = pl.BlockSpec((G, 1, Hd), lambda h, c: (h, 0, 0))
    return pl.pallas_call(
        body, name=name, grid=(GDN_QK_HEADS // G, nc),
        in_specs=[o_spec, q_spec, k_spec, v_spec, z_spec, ba_spec, ba_spec, p_spec, nw_spec, o_spec, s_spec, t_spec],
        out_specs=(h_spec, h_spec, o_spec, o_spec, ba_spec, ba_spec, p_spec, dnw_spec),
        out_shape=(jax.ShapeDtypeStruct((S, GDN_QK_HEADS * Hd), F32), jax.ShapeDtypeStruct((S, GDN_QK_HEADS * Hd), F32),
                   jax.ShapeDtypeStruct((S, GDN_QK_HEADS * W), F32), jax.ShapeDtypeStruct((S, GDN_QK_HEADS * W), BF16),
                   jax.ShapeDtypeStruct((GDN_QK_HEADS, S, J), F32), jax.ShapeDtypeStruct((GDN_QK_HEADS, S, J), F32),
                   jax.ShapeDtypeStruct((GDN_QK_HEADS, 2, J), F32), jax.ShapeDtypeStruct((GDN_QK_HEADS, 1, Hd), F32)),
        scratch_shapes=[pltpu.VMEM((G * W, Hd), F32)],
        compiler_params=_params("parallel", "arbitrary"))(don, conv, conv, conv, proj, b_raw, a_raw, pvec, nw, o_pre,
                                                          states, t_inv)


def _ada_fwd(c_all, w_loc, b_loc, name):
    n = w_loc.shape[1]

    def body(c_ref, w_ref, b_ref, o_ref):
        o_ref[...] = _dot3(_silu(c_ref[...]), w_ref[...]) + b_ref[...]

    return pl.pallas_call(body, name=name, out_shape=jax.ShapeDtypeStruct((N_DEV, n), F32),
                          compiler_params=pltpu.CompilerParams(vmem_limit_bytes=VMEM_LIMIT))(c_all, w_loc, b_loc)


def _ada_bwd(c_all_t, dmod_cols, name):
    Dm, n = c_all_t.shape[0], dmod_cols.shape[1]

    def body(c_ref, d_ref, o_ref):
        ca = _silu(c_ref[...])
        acc = ca[:, 0:1] * d_ref[0:1, :]
        for i in range(1, N_DEV):
            acc = acc + ca[:, i:i + 1] * d_ref[i:i + 1, :]
        o_ref[...] = acc

    return pl.pallas_call(body, name=name, out_shape=jax.ShapeDtypeStruct((Dm, n), F32),
                          compiler_params=pltpu.CompilerParams(vmem_limit_bytes=VMEM_LIMIT))(c_all_t, dmod_cols)


ADAM_BLOCK_BYTES = 12 * 1024 * 1024


def _adam(contrib, w, m, v, name):
    n, R, C = contrib.shape
    tr = R
    while tr % 16 == 0 and (n + 7) * tr * C * 4 > ADAM_BLOCK_BYTES:
        tr //= 2

    def body(c_ref, w_ref, m_ref, v_ref, g_ref, d_ref, nm_ref, nv_ref):
        g = c_ref[0].astype(F32)
        for i in range(1, n):
            g = g + c_ref[i].astype(F32)
        nm = ADAM_B1 * m_ref[...] + (1.0 - ADAM_B1) * g
        nv = ADAM_B2 * v_ref[...] + (1.0 - ADAM_B2) * (g * g)
        m_hat = nm / (1.0 - ADAM_B1 ** ADAM_STEP)
        v_hat = nv / (1.0 - ADAM_B2 ** ADAM_STEP)
        g_ref[...] = g
        d_ref[...] = -ADAM_LR * (m_hat / (jnp.sqrt(v_hat) + ADAM_EPS) + ADAM_WD * w_ref[...])
        nm_ref[...] = nm
        nv_ref[...] = nv

    spec = pl.BlockSpec((tr, C), lambda i: (i, 0))
    shp = jax.ShapeDtypeStruct((R, C), F32)
    return pl.pallas_call(
        body, name=name, grid=(R // tr,), in_specs=[pl.BlockSpec((n, tr, C), lambda i: (0, i, 0)), spec, spec, spec],
        out_specs=(spec,) * 4, out_shape=(shp,) * 4, compiler_params=_params("parallel"))(contrib, w, m, v)


def _exchange(arrays, modes, name, chips=False):
    n = len(arrays)
    out_shape = tuple(jax.ShapeDtypeStruct((N_DEV,) + a.shape if md == "gather" else a.shape, a.dtype)
                      for a, md in zip(arrays, modes))

    def body(*refs):
        ins, outs = refs[:n], refs[n:2 * n]
        send_sems, recv_sems, loc_sems = refs[2 * n:]
        me, peers = _peer_table(chips)

        def src(k, slot):
            return ins[k] if modes[k] == "gather" else ins[k].at[slot]

        def remote(k, m, to_slot, land_slot):
            return pltpu.make_async_remote_copy(
                src_ref=src(k, to_slot), dst_ref=outs[k].at[land_slot], send_sem=send_sems.at[k, m],
                recv_sem=recv_sems.at[k, m], device_id=peers[m][0], device_id_type=pl.DeviceIdType.MESH)

        local = [pltpu.make_async_copy(src(k, me), outs[k].at[me], loc_sems.at[k]) for k in range(n)]
        for cp in local:
            cp.start()
        sends = [remote(k, m, peers[m][1], me) for m in range(len(peers)) for k in range(n)]
        for cp in sends:
            cp.start()
        for m in range(len(peers)):
            for k in range(n):
                remote(k, m, peers[m][1], peers[m][1]).wait_recv()
        for cp in sends:
            cp.wait_send()
        for cp in local:
            cp.wait()

    any_spec = pl.BlockSpec(memory_space=pl.ANY)
    return pl.pallas_call(
        body, name=name, in_specs=[any_spec] * n, out_specs=(any_spec,) * n, out_shape=out_shape,
        scratch_shapes=[pltpu.SemaphoreType.DMA((n, N_DEV - 1)), pltpu.SemaphoreType.DMA((n, N_DEV - 1)),
                        pltpu.SemaphoreType.DMA((n,))])(*arrays)


def _gather_two_level(arrays, name):
    n = len(arrays)
    out_shape = tuple(jax.ShapeDtypeStruct((N_DEV,) + a.shape, a.dtype) for a in arrays)

    def body(*refs):
        ins, outs = refs[:n], refs[n:2 * n]
        send_sems, recv_sems, loc_sems = refs[2 * n:]
        ix, iy, ic = lax.axis_index("x"), lax.axis_index("y"), lax.axis_index("c")
        lin = lambda px, py, pc: 4 * px + 2 * py + pc
        me, sib = lin(ix, iy, ic), (ix, iy, 1 - ic)
        chips = [(1 - ix, iy), (ix, 1 - iy), (1 - ix, 1 - iy)]

        def copy(k, s, block, to, src=None):
            return pltpu.make_async_remote_copy(
                src_ref=outs[k].at[block] if src is None else src, dst_ref=outs[k].at[block],
                send_sem=send_sems.at[k, s], recv_sem=recv_sems.at[k, s], device_id=to,
                device_id_type=pl.DeviceIdType.MESH)

        local = [pltpu.make_async_copy(ins[k], outs[k].at[me], loc_sems.at[k]) for k in range(n)]
        for cp in local:
            cp.start()
        first = [copy(k, 1 + j, me, (cx, cy, ic), src=ins[k]) for j, (cx, cy) in enumerate(chips) for k in range(n)]
        first += [copy(k, 0, me, sib, src=ins[k]) for k in range(n)]
        for cp in first:
            cp.start()
        passed = []
        for j, (cx, cy) in enumerate(chips):
            for k in range(n):
                copy(k, 1 + j, lin(cx, cy, ic), sib).wait_recv()
                passed.append(copy(k, 4 + j, lin(cx, cy, ic), sib))
                passed[-1].start()
        for k in range(n):
            copy(k, 0, lin(*sib), sib).wait_recv()
            for j, (cx, cy) in enumerate(chips):
                copy(k, 4 + j, lin(cx, cy, 1 - ic), sib).wait_recv()
        for cp in first + passed:
            cp.wait_send()
        for cp in local:
            cp.wait()

    any_spec = pl.BlockSpec(memory_space=pl.ANY)
    return pl.pallas_call(
        body, name=name, in_specs=[any_spec] * n, out_specs=(any_spec,) * n, out_shape=out_shape,
        scratch_shapes=[pltpu.SemaphoreType.DMA((n, N_DEV - 1)), pltpu.SemaphoreType.DMA((n, N_DEV - 1)),
                        pltpu.SemaphoreType.DMA((n,))])(*arrays)


def _peer_table(chips=False):
    ix, iy, ic = lax.axis_index("x"), lax.axis_index("y"), lax.axis_index("c")
    peers = []
    for m in ((2, 4, 6) if chips else range(1, N_DEV)):
        px = 1 - ix if m & 4 else ix
        py = 1 - iy if m & 2 else iy
        pc = 1 - ic if m & 1 else ic
        peers.append(((px, py, pc), 2 * px + py if chips else 4 * px + 2 * py + pc))
    return (2 * ix + iy if chips else 4 * ix + 2 * iy + ic), peers


def _exchange_start(arrays, modes, after, name, chips=False):
    n = len(arrays)
    land_shapes = [(N_DEV,) + a.shape if md == "gather" else a.shape for a, md in zip(arrays, modes)]

    def body(*refs):
        ins, lands = refs[:n], refs[n:2 * n]
        send_sems, recv_sems = refs[2 * n + 1], refs[2 * n + 2]
        token, loc_sems = refs[-2], refs[-1]
        me, peers = _peer_table(chips)

        def src(k, slot):
            return ins[k] if modes[k] == "gather" else ins[k].at[slot]

        local = [pltpu.make_async_copy(src(k, me), lands[k].at[me], loc_sems.at[k]) for k in range(n)]
        for cp in local:
            cp.start()
        for cp in local:
            cp.wait()
        for peer, slot in peers:
            for k in range(n):
                pltpu.make_async_remote_copy(
                    src_ref=src(k, slot), dst_ref=lands[k].at[me], send_sem=send_sems, recv_sem=recv_sems,
                    device_id=peer, device_id_type=pl.DeviceIdType.MESH).start()
        token[...] = jnp.zeros_like(token)

    hbm = pl.BlockSpec(memory_space=pltpu.HBM)
    sem = pl.BlockSpec(memory_space=pltpu.SEMAPHORE)
    sem_shape = pltpu.SemaphoreType.DMA(())
    operands = [pltpu.with_memory_space_constraint(a, pltpu.HBM) for a in arrays]
    operands += [pltpu.with_memory_space_constraint(lax.empty(s, a.dtype), pltpu.HBM)
                 for s, a in zip(land_shapes, arrays)]
    out = pl.pallas_call(
        body, name=name,
        out_shape=(sem_shape, sem_shape) + tuple(pltpu.HBM(a.shape, a.dtype) for a in arrays)
        + tuple(pltpu.HBM(s, a.dtype) for s, a in zip(land_shapes, arrays)) + (jax.ShapeDtypeStruct((8, 128), F32),),
        in_specs=[hbm] * (2 * n) + [pl.BlockSpec(memory_space=pl.ANY)],
        out_specs=(sem, sem) + (hbm,) * (2 * n) + (pl.BlockSpec(memory_space=pltpu.VMEM),),
        input_output_aliases={i: 2 + i for i in range(2 * n)},
        scratch_shapes=[pltpu.SemaphoreType.DMA((n,))],
        compiler_params=pltpu.CompilerParams(has_side_effects=pltpu.SideEffectType.DATAFLOW_SIDE_EFFECTING))(
            *operands, after)
    return out[0], out[1], out[2:2 + n], out[2 + n:2 + 2 * n], out[-1]


def _exchange_wait(started, modes, after, name, chips=False):
    send_sems, recv_sems, sent, lands, _ = started
    n = len(sent)

    def body(*refs):
        ins, zones = refs[:n], refs[n:2 * n]
        send_ref, recv_ref = refs[2 * n], refs[2 * n + 1]
        _, peers = _peer_table(chips)

        def src(k, slot):
            return ins[k] if modes[k] == "gather" else ins[k].at[slot]

        for peer, slot in peers:
            for k in range(n):
                cp = pltpu.make_async_remote_copy(
                    src_ref=src(k, slot), dst_ref=zones[k].at[slot], send_sem=send_ref, recv_sem=recv_ref,
                    device_id=peer, device_id_type=pl.DeviceIdType.MESH)
                cp.wait_send()
                cp.wait_recv()

    hbm = pl.BlockSpec(memory_space=pltpu.HBM)
    sem = pl.BlockSpec(memory_space=pltpu.SEMAPHORE)
    out = pl.pallas_call(
        body, name=name,
        out_shape=tuple(pltpu.HBM(a.shape, a.dtype) for a in sent) + tuple(pltpu.HBM(a.shape, a.dtype) for a in lands),
        in_specs=[hbm] * (2 * n) + [sem, sem, pl.BlockSpec(memory_space=pl.ANY)], out_specs=(hbm,) * (2 * n),
        input_output_aliases={i: i for i in range(2 * n)},
        compiler_params=pltpu.CompilerParams(has_side_effects=pltpu.SideEffectType.DATAFLOW_SIDE_EFFECTING))(
            *sent, *lands, send_sems, recv_sems, after)
    return out[n:]


def _swap_sibling(arr, name):
    def body(in_ref, out_ref, send_sem, recv_sem):
        ix, iy, ic = lax.axis_index("x"), lax.axis_index("y"), lax.axis_index("c")
        cp = pltpu.make_async_remote_copy(src_ref=in_ref, dst_ref=out_ref, send_sem=send_sem, recv_sem=recv_sem,
                                          device_id=(ix, iy, 1 - ic), device_id_type=pl.DeviceIdType.MESH)
        cp.start()
        cp.wait()

    any_spec = pl.BlockSpec(memory_space=pl.ANY)
    return pl.pallas_call(body, name=name, in_specs=[any_spec], out_specs=any_spec,
                          out_shape=jax.ShapeDtypeStruct(arr.shape, arr.dtype),
                          scratch_shapes=[pltpu.SemaphoreType.DMA, pltpu.SemaphoreType.DMA])(arr)


def _add_pair(a, b, name):
    n, R, C = a.shape
    tr = _blk(R, 256)

    def body(a_ref, b_ref, o_ref):
        o_ref[...] = (a_ref[...].astype(F32) + b_ref[...].astype(F32)).astype(o_ref.dtype)

    spec = pl.BlockSpec((1, tr, C), lambda i, j: (i, j, 0))
    return pl.pallas_call(body, name=name, grid=(n, R // tr), in_specs=[spec, spec], out_specs=spec,
                          out_shape=jax.ShapeDtypeStruct(a.shape, a.dtype),
                          compiler_params=_params("parallel", "parallel"))(a, b)


W_IN_SPLITS = (0, 2048, 6144, 6176, 10272, 12320, 12336, 12352, 13376, 14400)
N_REPLICATED = 16640
REPLICATED = ("b_ada", "norm_mix_pre", "norm_mix_post", "ssm_conv_b", "ssm_dt_bias", "ssm_A_log", "ssm_D",
              "ssm_norm_w", "gdn_dt_bias", "gdn_A_log", "gdn_norm_w", "norm_mlp_pre", "norm_mlp_post")
WEIGHTS = ("w_ada", "b_ada", "norm_mix_pre", "norm_mix_post", "w_in", "ssm_conv_w", "ssm_conv_b", "ssm_dt_bias",
           "ssm_A_log", "ssm_D", "ssm_norm_w", "gdn_conv_w", "gdn_dt_bias", "gdn_A_log", "gdn_norm_w", "w_ssm_up",
           "w_gdn_up", "w_out", "norm_mlp_pre", "norm_mlp_post", "w_mlp_up", "w_mlp_down")


def _by_cols(t):
    return t.transpose(1, 0, 2).reshape(t.shape[1], N_DEV * t.shape[2])


def _to_col_shards(t):
    R, C8 = t.shape
    return t.reshape(R, N_DEV, C8 // N_DEV).transpose(1, 0, 2)


def _heads_first(t, groups):
    S = t.shape[0]
    return t.reshape(S, groups, t.shape[1] // groups).transpose(1, 0, 2)


def _heads_last(t):
    return t.transpose(1, 0, 2).reshape(t.shape[1], t.shape[0] * t.shape[2])


def kernel(x, c, w_ada, b_ada, norm_mix_pre, norm_mix_post, w_in, ssm_conv_w, ssm_conv_b, ssm_dt_bias, ssm_A_log, ssm_D, ssm_norm_w, gdn_conv_w, gdn_dt_bias, gdn_A_log, gdn_norm_w, w_ssm_up, w_gdn_up, w_out, norm_mlp_pre, norm_mlp_post, w_mlp_up, w_mlp_down, loss_target, m_w_ada, m_b_ada, m_norm_mix_pre, m_norm_mix_post, m_w_in, m_ssm_conv_w, m_ssm_conv_b, m_ssm_dt_bias, m_ssm_A_log, m_ssm_D, m_ssm_norm_w, m_gdn_conv_w, m_gdn_dt_bias, m_gdn_A_log, m_gdn_norm_w, m_w_ssm_up, m_w_gdn_up, m_w_out, m_norm_mlp_pre, m_norm_mlp_post, m_w_mlp_up, m_w_mlp_down, v_w_ada, v_b_ada, v_norm_mix_pre, v_norm_mix_post, v_w_in, v_ssm_conv_w, v_ssm_conv_b, v_ssm_dt_bias, v_ssm_A_log, v_ssm_D, v_ssm_norm_w, v_gdn_conv_w, v_gdn_dt_bias, v_gdn_A_log, v_gdn_norm_w, v_w_ssm_up, v_w_gdn_up, v_w_out, v_norm_mlp_pre, v_norm_mlp_post, v_w_mlp_up, v_w_mlp_down):
    S, Dm = x.shape[1], D_MODEL
    me = 4 * lax.axis_index("x") + 2 * lax.axis_index("y") + lax.axis_index("c")
    x2, tgt = x[0], loss_target[0]
    n_ada = w_ada.shape[2]
    given = dict(
        w_ada=(w_ada, m_w_ada, v_w_ada), b_ada=(b_ada, m_b_ada, v_b_ada),
        norm_mix_pre=(norm_mix_pre, m_norm_mix_pre, v_norm_mix_pre),
        norm_mix_post=(norm_mix_post, m_norm_mix_post, v_norm_mix_post), w_in=(w_in, m_w_in, v_w_in),
        ssm_conv_w=(ssm_conv_w, m_ssm_conv_w, v_ssm_conv_w), ssm_conv_b=(ssm_conv_b, m_ssm_conv_b, v_ssm_conv_b),
        ssm_dt_bias=(ssm_dt_bias, m_ssm_dt_bias, v_ssm_dt_bias), ssm_A_log=(ssm_A_log, m_ssm_A_log, v_ssm_A_log),
        ssm_D=(ssm_D, m_ssm_D, v_ssm_D), ssm_norm_w=(ssm_norm_w, m_ssm_norm_w, v_ssm_norm_w),
        gdn_conv_w=(gdn_conv_w, m_gdn_conv_w, v_gdn_conv_w), gdn_dt_bias=(gdn_dt_bias, m_gdn_dt_bias, v_gdn_dt_bias),
        gdn_A_log=(gdn_A_log, m_gdn_A_log, v_gdn_A_log), gdn_norm_w=(gdn_norm_w, m_gdn_norm_w, v_gdn_norm_w),
        w_ssm_up=(w_ssm_up, m_w_ssm_up, v_w_ssm_up), w_gdn_up=(w_gdn_up, m_w_gdn_up, v_w_gdn_up),
        w_out=(w_out, m_w_out, v_w_out), norm_mlp_pre=(norm_mlp_pre, m_norm_mlp_pre, v_norm_mlp_pre),
        norm_mlp_post=(norm_mlp_post, m_norm_mlp_post, v_norm_mlp_post), w_mlp_up=(w_mlp_up, m_w_mlp_up, v_w_mlp_up),
        w_mlp_down=(w_mlp_down, m_w_mlp_down, v_w_mlp_down))

    (c_all, scw, gcw, g_in) = _gather_two_level([c, ssm_conv_w[0], gdn_conv_w[0], w_in[0].astype(BF16)], "gather_w_in")
    c_all = c_all.reshape(N_DEV, Dm)
    wf = _by_cols(g_in)
    sp = W_IN_SPLITS
    w_main = jnp.concatenate([wf[:, sp[0]:sp[2]], wf[:, sp[3]:sp[5]], wf[:, sp[7]:sp[9]]], axis=1)
    w_small = jnp.concatenate([wf[:, sp[2]:sp[3]], wf[:, sp[5]:sp[7]], jnp.zeros((Dm, N_SMALL - 64), BF16)], axis=1)
    conv_w = jnp.concatenate([_by_cols(scw), _by_cols(gcw)], axis=1)
    conv_b = jnp.concatenate([ssm_conv_b, jnp.zeros_like(ssm_conv_b)], axis=1)

    b_loc = lax.dynamic_slice(b_ada, (0, me * n_ada), (1, n_ada))
    mod_part = _ada_fwd(c_all, w_ada[0], b_loc, "ada_fwd")
    (mod_rows,) = _exchange([mod_part.reshape(N_DEV, 1, n_ada)], ["a2a"], "exchange_mod")
    rest = _exchange_start([w_ssm_up[0].astype(BF16), w_gdn_up[0].astype(BF16), w_out[0].astype(BF16),
                            w_mlp_up[0].astype(BF16), w_mlp_down[0].astype(BF16)], ["gather"] * 5, mod_rows,
                           "gather_rest_start")
    mod = mod_rows.reshape(1, 6 * Dm) + rest[4][0:1, 0:1]
    sh1, sc1, g1, sh2, sc2, g2 = [mod[:, i * Dm:(i + 1) * Dm] for i in range(6)]

    h = _pre_fwd(x2, norm_mix_pre, sc1, sh1, "pre_mix")
    proj = _mm(h, w_main, S, N_MAIN, Dm, mode="nn", out_dtype=F32, name="proj_main")
    small = _mm(h, w_small, S, N_SMALL, Dm, mode="nn", out_dtype=F32, name="proj_small")
    conv = _conv_fwd(proj, conv_w, conv_b, "conv_fwd")
    dt_g, b_g, a_g = _heads_first(small[:, 0:32], 8), _heads_first(small[:, 32:48], 8), _heads_first(small[:, 48:64], 8)
    pv_ssm = jnp.stack([ssm_dt_bias.reshape(8, 4), ssm_A_log.reshape(8, 4), ssm_D.reshape(8, 4)], axis=1)
    nw_ssm = ssm_norm_w.reshape(8, 1, SSM_GROUP_WIDTH)
    pv_gdn = jnp.stack([gdn_dt_bias.reshape(8, 2), gdn_A_log.reshape(8, 2)], axis=1)
    y_ssd, ysn, st_ssm = _ssd_fwd(conv, proj, dt_g, pv_ssm, nw_ssm, "ssd_fwd")
    o_pre, ogn, st_gdn, t_inv = _gdn_fwd(conv, proj, b_g, a_g, pv_gdn, gdn_norm_w, "gdn_fwd")
    g_su, g_gu, g_out, g_mu, g_md = _exchange_wait(rest, ["gather"] * 5, ogn, "gather_rest_wait")
    w_su, w_gu = g_su.reshape(2 * Dm, Dm), g_gu.reshape(2 * Dm, Dm)
    w_o, w_mu, w_md = g_out.reshape(Dm, Dm), _by_cols(g_mu), g_md.reshape(4 * Dm, Dm)
    ys = _mm(ysn, w_su, S, Dm, 2 * Dm, mode="nn", out_dtype=F32, name="ssm_up")
    yg = _mm(ogn, w_gu, S, Dm, 2 * Dm, mode="nn", out_dtype=F32, name="gdn_up")
    merged = _merge_fwd(ys, yg, proj, "merge_fwd")
    mo = _mm(merged, w_o, S, Dm, Dm, mode="nn", out_dtype=F32, name="mix_out")
    x1 = _post_fwd(x2, mo, norm_mix_post, g1, "post_mix")
    h2 = _pre_fwd(x1, norm_mlp_pre, sc2, sh2, "pre_mlp")
    u, act = _mm(h2, w_mu, S, 4 * Dm, Dm, mode="nn", out_dtype=F32, epi="relu2", name="mlp_up")
    y_mlp = _mm(act, w_md, S, Dm, 4 * Dm, mode="nn", out_dtype=F32, name="mlp_down")
    dx2, loss_loc = _final_fwd(x1, y_mlp, norm_mlp_post, g2, tgt, "post_mlp_loss")

    dy, dg2, dw_post2 = _post_bwd(dx2, y_mlp, norm_mlp_post, g2, "post_mlp_bwd")
    du = _mm(dy, w_md, S, 4 * Dm, Dm, mode="nt", out_dtype=BF16, epi="drelu2", extra=u, name="mlp_down_dx")
    gw_md = _mm(act, dy, 4 * Dm, Dm, S, mode="tn", out_dtype=BF16, name="mlp_down_dw")
    dh2 = _mm(du, w_mu, S, Dm, 4 * Dm, mode="nt", out_dtype=F32, name="mlp_up_dx")
    gw_mu = _mm(h2, du, Dm, 4 * Dm, S, mode="tn", out_dtype=BF16, name="mlp_up_dw")
    mlp_x = _exchange_start([_to_col_shards(gw_mu), gw_md.reshape(N_DEV, -1, Dm)], ["a2a"] * 2, gw_md,
                            "grads_mlp_start")
    dx1, dsh2, dsc2, dw_pre2 = _pre_bwd(dh2, x1, norm_mlp_pre, sc2 + mlp_x[4][0:1, 0:1], dx2, "pre_mlp_bwd")
    dmo, dg1, dw_post1 = _post_bwd(dx1, mo, norm_mix_post, g1, "post_mix_bwd")
    dmerged = _mm(dmo, w_o, S, Dm, Dm, mode="nt", out_dtype=F32, name="mix_out_dx")
    gw_o = _mm(merged, dmo, Dm, Dm, S, mode="tn", out_dtype=BF16, name="mix_out_dw")
    dys, dyg, dgate = _merge_bwd(dmerged, ys, yg, proj, "merge_bwd")
    dysn = _mm(dys, w_su, S, 2 * Dm, Dm, mode="nt", out_dtype=F32, name="ssm_up_dx")
    gw_su = _mm(ysn, dys, 2 * Dm, Dm, S, mode="tn", out_dtype=BF16, name="ssm_up_dw")
    dogn = _mm(dyg, w_gu, S, 2 * Dm, Dm, mode="nt", out_dtype=F32, name="gdn_up_dx")
    gw_gu = _mm(ogn, dyg, 2 * Dm, Dm, S, mode="tn", out_dtype=BF16, name="gdn_up_dw")
    mix_x = _exchange_start([gw_su.reshape(N_DEV, -1, Dm), gw_gu.reshape(N_DEV, -1, Dm), gw_o.reshape(N_DEV, -1, Dm)],
                            ["a2a"] * 3, gw_gu, "grads_mix_start")
    dxs, dBm, dCm, dz_s, ddt_g, dpv_ssm, dnw_ssm = _ssd_bwd(dysn, conv, proj, dt_g, pv_ssm + mix_x[4][0, 0], nw_ssm,
                                                            y_ssd, st_ssm, "ssd_bwd")
    dq, dk, dv, dz_g, db_g, da_g, dpv_gdn, dnw_gdn = _gdn_bwd(dogn, conv, proj, b_g, a_g, pv_gdn, gdn_norm_w, o_pre,
                                                              st_gdn, t_inv, "gdn_bwd")
    conv_pieces = []
    for nm, d_act, col0 in (("xs", dxs, 0), ("B", dBm, 2048), ("C", dCm, 3072), ("q", dq, 4096), ("k", dk, 5120),
                            ("v", dv, 6144)):
        conv_pieces.append(_conv_bwd(d_act, proj, conv_w, conv_b, col0, "conv_bwd_" + nm) + (col0,))
    d_small = jnp.concatenate([_heads_last(ddt_g), _heads_last(db_g), _heads_last(da_g),
                               jnp.zeros((S, N_SMALL - 64), F32)], axis=1).astype(BF16)
    d_main = [dz_s] + [p[0] for p in conv_pieces] + [dz_g, dgate]
    gw_small = _mm(h, d_small, Dm, N_SMALL, S, mode="tn", out_dtype=BF16, name="proj_small_dw")
    main_cols = _mm_pieces(d_main, h, Dm, N_MAIN, S, mode="tn", out_dtype=BF16, name="proj_main_dw")
    gw_in = jnp.concatenate([main_cols[:, 0:C_QKV], gw_small[:, 0:32], main_cols[:, C_QKV:C_GS], gw_small[:, 32:64],
                             main_cols[:, C_GS:N_MAIN]], axis=1)
    by_dest = _to_col_shards(gw_in).reshape(N_DEV // 2, 2, Dm, -1)
    my_c = lax.axis_index("c")
    keep = lax.dynamic_index_in_dim(by_dest, my_c, axis=1, keepdims=False)
    give = lax.dynamic_index_in_dim(by_dest, 1 - my_c, axis=1, keepdims=False)
    chip_sum = _add_pair(keep, _swap_sibling(give, "grads_w_in_pair"), "grads_w_in_pair_sum")
    in_x = _exchange_start([chip_sum], ["a2a"], gw_small, "grads_w_in_start", chips=True)
    dh = _mm(d_small, w_small + in_x[4][0:1, 0:1].astype(BF16), S, Dm, N_SMALL, mode="nt", out_dtype=F32,
             name="proj_small_dx")
    dh = _mm_pieces(d_main, w_main, S, Dm, N_MAIN, mode="nt", out_dtype=F32, add=dh, name="proj_main_dx")
    dx, dsh1, dsc1, dw_pre1 = _pre_bwd(dh, x2, norm_mix_pre, sc1, dx1, "pre_mix_bwd")
    r_mu, r_md = _exchange_wait(mlp_x, ["a2a"] * 2, dx, "grads_mlp_wait")
    r_su, r_gu, r_o = _exchange_wait(mix_x, ["a2a"] * 3, dx, "grads_mix_wait")

    dconv_w = jnp.concatenate([p[1] for p in conv_pieces], axis=1)
    dconv_b = jnp.concatenate([p[2] for p in conv_pieces[:3]], axis=1)
    dmod = jnp.concatenate([dsh1, dsc1, dg1, dsh2, dsc2, dg2], axis=1)
    small_vec = jnp.concatenate(
        [dmod, dw_pre1, dw_post1, dconv_b, dpv_ssm[:, 0].reshape(1, 32), dpv_ssm[:, 1].reshape(1, 32),
         dpv_ssm[:, 2].reshape(1, 32), dnw_ssm.reshape(1, 2048), dpv_gdn[:, 0].reshape(1, 16),
         dpv_gdn[:, 1].reshape(1, 16), jnp.sum(dnw_gdn, axis=0), dw_pre2, dw_post2, dconv_w.reshape(1, -1)], axis=1)
    n_vec = small_vec.shape[1]
    small_vec = jnp.pad(small_vec, ((0, 0), (0, (-n_vec) % 1024))).reshape(-1, 1024)
    (small_all,) = _exchange([small_vec], ["gather"], "gather_small_grads")
    small_all = small_all.reshape(N_DEV, -1)
    dmod_cols = lax.dynamic_slice(small_all, (0, me * n_ada), (N_DEV, n_ada))
    gw_ada = _ada_bwd(c_all.T, dmod_cols, "ada_bwd")
    conv_all = small_all[:, N_REPLICATED:n_vec].reshape(N_DEV, CONV_K, 2 * N_DEV * 512)
    conv_contrib = jnp.concatenate(
        [lax.dynamic_slice(conv_all, (0, 0, me * 512), (N_DEV, CONV_K, 512)),
         lax.dynamic_slice(conv_all, (0, 0, N_DEV * 512 + me * 512), (N_DEV, CONV_K, 512))], axis=1)
    rep_contrib = small_all[:, :N_REPLICATED].reshape(N_DEV, N_REPLICATED // 128, 128)

    results = {}

    def adam_big(nm, contrib):
        w3 = given[nm]
        res = _adam(contrib, w3[0][0], w3[1][0], w3[2][0], "adam_" + nm)
        results[nm] = tuple(r.reshape(w3[0].shape) for r in res)

    adam_big("w_ada", gw_ada[None])
    adam_big("w_ssm_up", r_su)
    adam_big("w_gdn_up", r_gu)
    adam_big("w_out", r_o)
    adam_big("w_mlp_up", r_mu)
    adam_big("w_mlp_down", r_md)
    (r_in,) = _exchange_wait(in_x, ["a2a"], results["w_mlp_down"][0], "grads_w_in_wait", chips=True)
    adam_big("w_in", r_in)
    packed = [jnp.concatenate([given[nm][i] for nm in REPLICATED], axis=1).reshape(N_REPLICATED // 128, 128)
              for i in range(3)]
    rep_res = _adam(rep_contrib, packed[0], packed[1], packed[2], "adam_replicated")
    pos = 0
    for nm in REPLICATED:
        size = given[nm][0].shape[1]
        results[nm] = tuple(r.reshape(1, N_REPLICATED)[:, pos:pos + size] for r in rep_res)
        pos += size
    conv_wmv = [jnp.concatenate([given["ssm_conv_w"][i][0], given["gdn_conv_w"][i][0]], axis=0) for i in range(3)]
    conv_res = _adam(conv_contrib, conv_wmv[0], conv_wmv[1], conv_wmv[2], "adam_conv_w")
    results["ssm_conv_w"] = tuple(r[None, :CONV_K] for r in conv_res)
    results["gdn_conv_w"] = tuple(r[None, CONV_K:] for r in conv_res)

    loss = lax.psum(loss_loc[0, 0], ("x", "y", "c"))
    return (loss, dx[None]) + tuple(results[nm][i] for i in range(4) for nm in WEIGHTS)
```

```python
import jax
import jax.numpy as jnp
from jax import lax
from jax.experimental import pallas as pl
from jax.experimental.pallas import tpu as pltpu

F32 = jnp.float32
BF16 = jnp.bfloat16
N_DEV = 8
D_MODEL = 1024
EPS = 1e-6
CONV_K = 4
SSM_CHUNK = 128
SSM_HEAD_DIM = 64
SSM_D_STATE = 128
SSM_GROUPS = 8
SSM_HEADS_PER_GROUP = 4
SSM_GROUP_WIDTH = SSM_HEADS_PER_GROUP * SSM_HEAD_DIM
SSM_GROUPS_PER_STEP = 2
GDN_CHUNK = 64
GDN_HEAD = 128
GDN_QK_HEADS = 8
GDN_V_PER_QK = 2
GDN_QK_PER_STEP = 4
GDN_INV_BLOCK = 16
C_ZS, C_XBC, C_QKV, C_ZG, C_GS, C_GG, N_MAIN = 0, 2048, 6144, 10240, 12288, 13312, 14336
N_SMALL = 128
ADAM_LR, ADAM_B1, ADAM_B2, ADAM_EPS, ADAM_WD, ADAM_STEP = 0.001, 0.9, 0.999, 1e-08, 0.01, 10
VMEM_LIMIT = 56 * 1024 * 1024
NEG_INF = float("-inf")

_NT = (((1,), (1,)), ((), ()))
_NN = (((1,), (0,)), ((), ()))
_TN = (((0,), (0,)), ((), ()))


def _params(*sem):
    return pltpu.CompilerParams(dimension_semantics=sem, vmem_limit_bytes=VMEM_LIMIT)


def _dot(a, b, dims=_NN):
    return lax.dot_general(a.astype(BF16), b.astype(BF16), dims, preferred_element_type=F32)


def _split(a):
    hi = a.astype(BF16)
    return hi, (a - hi.astype(F32)).astype(BF16)


def _dot3(a, b, dims=_NN):
    ah, al = _split(a)
    bh, bl = _split(b)
    d = lambda u, v: lax.dot_general(u, v, dims, preferred_element_type=F32)
    return d(ah, bh) + (d(ah, bl) + d(al, bh))


def _sigmoid(x):
    return 1.0 / (1.0 + jnp.exp(-x))


def _silu(x):
    return x * _sigmoid(x)


def _dsilu(x):
    s = _sigmoid(x)
    return s * (1.0 + x * (1.0 - s))


def _softplus(x):
    return jnp.maximum(x, 0.0) + jnp.log1p(jnp.exp(-jnp.abs(x)))


def _iota(n, m, d):
    return lax.broadcasted_iota(jnp.int32, (n, m), d)


def _rowsum(x):
    return jnp.sum(x, axis=1, keepdims=True)


def _colsum(x):
    return jnp.sum(x, axis=0, keepdims=True)


def _total(x):
    return _rowsum(_colsum(x))


MXU_LANES = 128


def _parts(x, n):
    out = []
    for _ in range(n):
        p = x.astype(BF16)
        out.append(p)
        x = x - p.astype(F32)
    return out


def _sum_by(m01, x, dims=_NN, n=3):
    return sum(lax.dot_general(m01, p, dims, preferred_element_type=F32) for p in _parts(x, n))


def _row_col_sums(q):
    ones = jnp.ones((q.shape[0], MXU_LANES), BF16)
    acc = 0.0
    for p in _parts(q, 2):
        acc = acc + (lax.dot_general(p, ones, _NN, preferred_element_type=F32)
                     - lax.dot_general(p, ones, _TN, preferred_element_type=F32))
    return acc[:, 0:1]


def _cumsum_forms(col, ii, jj):
    lower = jnp.where(ii >= jj, 1.0, 0.0).astype(BF16)
    cum_col = _sum_by(lower, jnp.broadcast_to(col, (col.shape[0], MXU_LANES)))[:, 0:1]
    cum_row = _colsum(jnp.where(ii <= jj, col, 0.0))
    return cum_col, cum_row


def _rev_cumsum_col(col, ii, jj):
    upper = jnp.where(ii <= jj, 1.0, 0.0).astype(BF16)
    return _sum_by(upper, jnp.broadcast_to(col, (col.shape[0], MXU_LANES)))[:, 0:1]


def _blk(dim, pref):
    return pref if dim % pref == 0 else dim


def _lockstep(gens):
    gens = list(gens)
    while gens:
        alive = []
        for g in gens:
            try:
                next(g)
                alive.append(g)
            except StopIteration:
                pass
        gens = alive


def _mm(a, b, M, N, K, *, mode, out_dtype, name, a_off=(0, 0), b_off=(0, 0), add=None, epi=None, extra=None,
        tm=1024, tn=1024, tk=1024):
    tm, tn, tk = _blk(M, tm), _blk(N, tn), _blk(K, tk)
    nk = K // tk
    if mode == "tn":
        a_spec = pl.BlockSpec((tk, tm), lambda i, j, k: (k + a_off[0] // tk, i + a_off[1] // tm))
        assert a_off[0] % tk == 0 and a_off[1] % tm == 0
    else:
        a_spec = pl.BlockSpec((tm, tk), lambda i, j, k: (i + a_off[0] // tm, k + a_off[1] // tk))
        assert a_off[0] % tm == 0 and a_off[1] % tk == 0
    if mode == "nt":
        b_spec = pl.BlockSpec((tn, tk), lambda i, j, k: (j + b_off[0] // tn, k + b_off[1] // tk))
        assert b_off[0] % tn == 0 and b_off[1] % tk == 0
    else:
        b_spec = pl.BlockSpec((tk, tn), lambda i, j, k: (k + b_off[0] // tk, j + b_off[1] // tn))
        assert b_off[0] % tk == 0 and b_off[1] % tn == 0
    dims = {"nn": _NN, "nt": _NT, "tn": _TN}[mode]
    o_spec = pl.BlockSpec((tm, tn), lambda i, j, k: (i, j))
    ins, in_specs = [a, b], [a_spec, b_spec]
    if add is not None:
        ins.append(add)
        in_specs.append(o_spec)
    if extra is not None:
        ins.append(extra)
        in_specs.append(o_spec)
    n_in = len(ins)
    if epi == "relu2":
        out_shape = (jax.ShapeDtypeStruct((M, N), F32), jax.ShapeDtypeStruct((M, N), BF16))
        out_specs = (o_spec, o_spec)
    else:
        out_shape = jax.ShapeDtypeStruct((M, N), out_dtype)
        out_specs = o_spec

    def body(*refs):
        a_ref, b_ref = refs[0], refs[1]
        acc = refs[-1]
        outs = refs[n_in:-1]
        k = pl.program_id(2)

        @pl.when(k == 0)
        def _():
            acc[...] = jnp.zeros_like(acc)

        acc[...] += _dot(a_ref[...], b_ref[...], dims)

        @pl.when(k == nk - 1)
        def _():
            r = acc[...]
            pos = 2
            if add is not None:
                r = r + refs[pos][...]
                pos += 1
            if epi == "relu2":
                outs[0][...] = r
                p = jnp.maximum(r, 0.0)
                outs[1][...] = (p * p).astype(BF16)
            elif epi == "drelu2":
                outs[0][...] = (r * (2.0 * jnp.maximum(refs[pos][...], 0.0))).astype(out_dtype)
            else:
                outs[0][...] = r.astype(out_dtype)

    return pl.pallas_call(
        body, name=name, grid=(M // tm, N // tn, nk), in_specs=in_specs, out_specs=out_specs, out_shape=out_shape,
        scratch_shapes=[pltpu.VMEM((tm, tn), F32)],
        compiler_params=_params("parallel", "parallel", "arbitrary"))(*ins)


def _row_spec(tb, d):
    return pl.BlockSpec((tb, d), lambda i: (i, 0))


def _vec_spec(d):
    return pl.BlockSpec((1, d), lambda i: (0, 0))


def _pre_fwd(x, w, sc, sh, name):
    S, Dm = x.shape
    tb = _blk(S, 512)

    def body(x_ref, w_ref, sc_ref, sh_ref, h_ref):
        xv = x_ref[...]
        r = lax.rsqrt(jnp.mean(xv * xv, axis=-1, keepdims=True) + EPS)
        h_ref[...] = ((xv * r * w_ref[...]) * (1.0 + sc_ref[...]) + sh_ref[...]).astype(BF16)

    return pl.pallas_call(
        body, name=name, grid=(S // tb,), in_specs=[_row_spec(tb, Dm)] + [_vec_spec(Dm)] * 3,
        out_specs=_row_spec(tb, Dm), out_shape=jax.ShapeDtypeStruct((S, Dm), BF16),
        compiler_params=_params("parallel"))(x, w, sc, sh)


def _post_fwd(x, y, w, g, name):
    S, Dm = x.shape
    tb = _blk(S, 512)

    def body(x_ref, y_ref, w_ref, g_ref, o_ref):
        yv = y_ref[...]
        r = lax.rsqrt(jnp.mean(yv * yv, axis=-1, keepdims=True) + EPS)
        o_ref[...] = x_ref[...] + g_ref[...] * (yv * r * w_ref[...])

    return pl.pallas_call(
        body, name=name, grid=(S // tb,), in_specs=[_row_spec(tb, Dm)] * 2 + [_vec_spec(Dm)] * 2,
        out_specs=_row_spec(tb, Dm), out_shape=jax.ShapeDtypeStruct((S, Dm), F32),
        compiler_params=_params("parallel"))(x, y, w, g)


def _final_fwd(x, y, w, g, target, name):
    S, Dm = x.shape
    tb = _blk(S, 512)
    nb = S // tb

    def body(x_ref, y_ref, w_ref, g_ref, t_ref, dx_ref, loss_ref, acc):
        i = pl.program_id(0)

        @pl.when(i == 0)
        def _():
            acc[...] = jnp.zeros_like(acc)

        yv = y_ref[...]
        r = lax.rsqrt(jnp.mean(yv * yv, axis=-1, keepdims=True) + EPS)
        e = (x_ref[...] + g_ref[...] * (yv * r * w_ref[...])) - t_ref[...]
        dx_ref[...] = e * (1.0 / Dm)
        acc[...] += _colsum(e * e)

        @pl.when(i == nb - 1)
        def _():
            loss_ref[...] = (0.5 / Dm) * _rowsum(acc[...])

    return pl.pallas_call(
        body, name=name, grid=(nb,), in_specs=[_row_spec(tb, Dm)] * 2 + [_vec_spec(Dm)] * 2 + [_row_spec(tb, Dm)],
        out_specs=(_row_spec(tb, Dm), pl.BlockSpec((1, 1), lambda i: (0, 0))),
        out_shape=(jax.ShapeDtypeStruct((S, Dm), F32), jax.ShapeDtypeStruct((1, 1), F32)),
        scratch_shapes=[pltpu.VMEM((1, Dm), F32)], compiler_params=_params("arbitrary"))(x, y, w, g, target)


def _post_bwd(dxo, y, w, g, name):
    S, Dm = y.shape
    tb = _blk(S, 512)

    def body(d_ref, y_ref, w_ref, g_ref, dy_ref, dg_ref, dw_ref):
        i = pl.program_id(0)

        @pl.when(i == 0)
        def _():
            dg_ref[...] = jnp.zeros_like(dg_ref)
            dw_ref[...] = jnp.zeros_like(dw_ref)

        yv, dv = y_ref[...], d_ref[...]
        r = lax.rsqrt(jnp.mean(yv * yv, axis=-1, keepdims=True) + EPS)
        yh = yv * r
        dg_ref[...] += _colsum(dv * (yh * w_ref[...]))
        dn = dv * g_ref[...]
        dw_ref[...] += _colsum(dn * yh)
        dyh = dn * w_ref[...]
        dy_ref[...] = (r * (dyh - yh * jnp.mean(dyh * yh, axis=-1, keepdims=True))).astype(BF16)

    return pl.pallas_call(
        body, name=name, grid=(S // tb,), in_specs=[_row_spec(tb, Dm)] * 2 + [_vec_spec(Dm)] * 2,
        out_specs=(_row_spec(tb, Dm), _vec_spec(Dm), _vec_spec(Dm)),
        out_shape=(jax.ShapeDtypeStruct((S, Dm), BF16), jax.ShapeDtypeStruct((1, Dm), F32),
                   jax.ShapeDtypeStruct((1, Dm), F32)),
        compiler_params=_params("arbitrary"))(dxo, y, w, g)


def _pre_bwd(dh, x, w, sc, dres, name):
    S, Dm = x.shape
    tb = _blk(S, 512)

    def body(dh_ref, x_ref, w_ref, sc_ref, dr_ref, dx_ref, dsh_ref, dsc_ref, dw_ref):
        i = pl.program_id(0)

        @pl.when(i == 0)
        def _():
            dsh_ref[...] = jnp.zeros_like(dsh_ref)
            dsc_ref[...] = jnp.zeros_like(dsc_ref)
            dw_ref[...] = jnp.zeros_like(dw_ref)

        xv, dv = x_ref[...], dh_ref[...]
        r = lax.rsqrt(jnp.mean(xv * xv, axis=-1, keepdims=True) + EPS)
        xh = xv * r
        one_sc = 1.0 + sc_ref[...]
        dsh_ref[...] += _colsum(dv)
        dsc_ref[...] += _colsum(dv * (xh * w_ref[...]))
        dw_ref[...] += _colsum(dv * one_sc * xh)
        dxh = dv * one_sc * w_ref[...]
        dx_ref[...] = dr_ref[...] + r * (dxh - xh * jnp.mean(dxh * xh, axis=-1, keepdims=True))

    vec = jax.ShapeDtypeStruct((1, Dm), F32)
    return pl.pallas_call(
        body, name=name, grid=(S // tb,),
        in_specs=[_row_spec(tb, Dm)] * 2 + [_vec_spec(Dm)] * 2 + [_row_spec(tb, Dm)],
        out_specs=(_row_spec(tb, Dm), _vec_spec(Dm), _vec_spec(Dm), _vec_spec(Dm)),
        out_shape=(jax.ShapeDtypeStruct((S, Dm), F32), vec, vec, vec),
        compiler_params=_params("arbitrary"))(dh, x, w, sc, dres)


def _merge_fwd(ys, yg, proj, name):
    S, Dm = ys.shape
    tb = _blk(S, 512)

    def body(ys_ref, yg_ref, gs_ref, gg_ref, o_ref):
        o_ref[...] = (_sigmoid(gs_ref[...]) * ys_ref[...] + _sigmoid(gg_ref[...]) * yg_ref[...]).astype(BF16)

    return pl.pallas_call(
        body, name=name, grid=(S // tb,),
        in_specs=[_row_spec(tb, Dm)] * 2 + [pl.BlockSpec((tb, Dm), lambda i: (i, C_GS // Dm)),
                                            pl.BlockSpec((tb, Dm), lambda i: (i, C_GG // Dm))],
        out_specs=_row_spec(tb, Dm), out_shape=jax.ShapeDtypeStruct((S, Dm), BF16),
        compiler_params=_params("parallel"))(ys, yg, proj, proj)


D_PROJ_ANY = pl.BlockSpec(memory_space=pl.ANY)


def _merge_bwd(dm, ys, yg, proj, d_proj, name):
    S, Dm = ys.shape
    tb = _blk(S, 512)

    def body(dm_ref, ys_ref, yg_ref, gs_ref, gg_ref, _, dys_ref, dyg_ref, dgate_ref):
        d = dm_ref[...]
        ss, sg = _sigmoid(gs_ref[...]), _sigmoid(gg_ref[...])
        dys_ref[...] = (d * ss).astype(BF16)
        dyg_ref[...] = (d * sg).astype(BF16)
        dgate_ref[:, :Dm] = (d * ys_ref[...] * ss * (1.0 - ss)).astype(BF16)
        dgate_ref[:, Dm:] = (d * yg_ref[...] * sg * (1.0 - sg)).astype(BF16)

    return pl.pallas_call(
        body, name=name, grid=(S // tb,),
        in_specs=[_row_spec(tb, Dm)] * 3 + [pl.BlockSpec((tb, Dm), lambda i: (i, C_GS // Dm)),
                                            pl.BlockSpec((tb, Dm), lambda i: (i, C_GG // Dm)), D_PROJ_ANY],
        out_specs=(_row_spec(tb, Dm), _row_spec(tb, Dm), pl.BlockSpec((tb, 2 * Dm), lambda i: (i, C_GS // (2 * Dm)))),
        out_shape=(jax.ShapeDtypeStruct((S, Dm), BF16), jax.ShapeDtypeStruct((S, Dm), BF16),
                   jax.ShapeDtypeStruct(d_proj.shape, BF16)),
        input_output_aliases={5: 2}, compiler_params=_params("parallel"))(dm, ys, yg, proj, proj, d_proj)


CONV_COLS = 128


def _shift_down(x, k, rows):
    return jnp.where(rows >= k, pltpu.roll(x, k, 0), 0.0)


def _shift_up(x, k, rows, S):
    return jnp.where(rows < S - k, pltpu.roll(x, S - k, 0), 0.0)


def _conv_fwd(proj, w, b, name):
    S = proj.shape[0]
    n = w.shape[1]
    cb = CONV_COLS

    def body(x_ref, w_ref, b_ref, o_ref):
        x = x_ref[...]
        rows = _iota(S, cb, 0)
        pre = x * w_ref[CONV_K - 1:CONV_K, :] + b_ref[...]
        for k in range(1, CONV_K):
            pre = pre + _shift_down(x, k, rows) * w_ref[CONV_K - 1 - k:CONV_K - k, :]
        o_ref[...] = _silu(pre)

    return pl.pallas_call(
        body, name=name, grid=(n // cb,),
        in_specs=[pl.BlockSpec((S, cb), lambda j: (0, j + C_XBC // cb)), pl.BlockSpec((CONV_K, cb), lambda j: (0, j)),
                  pl.BlockSpec((1, cb), lambda j: (0, j))],
        out_specs=pl.BlockSpec((S, cb), lambda j: (0, j)), out_shape=jax.ShapeDtypeStruct((S, n), F32),
        compiler_params=_params("parallel"))(proj, w, b)


def _conv_bwd(dact, proj, w, b, col0, d_proj, name):
    S, n = dact.shape
    cb = CONV_COLS
    o = col0 // cb

    def body(d_ref, x_ref, w_ref, b_ref, _, dx_ref, dw_ref, db_ref):
        x = x_ref[...]
        rows = _iota(S, cb, 0)
        xs = [x] + [_shift_down(x, k, rows) for k in range(1, CONV_K)]
        pre = xs[0] * w_ref[CONV_K - 1:CONV_K, :] + b_ref[...]
        for k in range(1, CONV_K):
            pre = pre + xs[k] * w_ref[CONV_K - 1 - k:CONV_K - k, :]
        dpre = d_ref[...] * _dsilu(pre)
        db_ref[...] = _colsum(dpre)
        dx = dpre * w_ref[CONV_K - 1:CONV_K, :]
        for k in range(CONV_K):
            dw_ref[CONV_K - 1 - k:CONV_K - k, :] = _colsum(dpre * xs[k])
            if k:
                dx = dx + _shift_up(dpre, k, rows, S) * w_ref[CONV_K - 1 - k:CONV_K - k, :]
        dx_ref[...] = dx.astype(BF16)

    return pl.pallas_call(
        body, name=name, grid=(n // cb,),
        in_specs=[pl.BlockSpec((S, cb), lambda j: (0, j)), pl.BlockSpec((S, cb), lambda j: (0, j + o + C_XBC // cb)),
                  pl.BlockSpec((CONV_K, cb), lambda j: (0, j + o)), pl.BlockSpec((1, cb), lambda j: (0, j + o)),
                  D_PROJ_ANY],
        out_specs=(pl.BlockSpec((S, cb), lambda j: (0, j + o + C_XBC // cb)),
                   pl.BlockSpec((CONV_K, cb), lambda j: (0, j)), pl.BlockSpec((1, cb), lambda j: (0, j))),
        out_shape=(jax.ShapeDtypeStruct(d_proj.shape, BF16), jax.ShapeDtypeStruct((CONV_K, n), F32),
                   jax.ShapeDtypeStruct((1, n), F32)),
        input_output_aliases={4: 0}, compiler_params=_params("parallel"))(dact, proj, w, b, d_proj)


def _ssd_specs(L, order):
    G = SSM_GROUPS_PER_STEP
    W, N = G * SSM_GROUP_WIDTH, G * SSM_D_STATE
    x_spec = pl.BlockSpec((L, W), lambda g, c: (order(c), g))
    b_spec = pl.BlockSpec((L, N), lambda g, c: (order(c), 2048 // N + g))
    c_spec = pl.BlockSpec((L, N), lambda g, c: (order(c), 3072 // N + g))
    z_spec = pl.BlockSpec((L, W), lambda g, c: (order(c), C_ZS // W + g))
    dt_spec = pl.BlockSpec((G, L, SSM_HEADS_PER_GROUP), lambda g, c: (g, order(c), 0))
    p_spec = pl.BlockSpec((G, 3, SSM_HEADS_PER_GROUP), lambda g, c: (g, 0, 0))
    nw_spec = pl.BlockSpec((G, 1, SSM_GROUP_WIDTH), lambda g, c: (g, 0, 0))
    s_spec = pl.BlockSpec((G, 1, SSM_GROUP_WIDTH, SSM_D_STATE), lambda g, c: (g, order(c), 0, 0))
    return x_spec, b_spec, c_spec, z_spec, dt_spec, p_spec, nw_spec, s_spec


class _SsdGroup:
    def __init__(self, L):
        P, H, W = SSM_HEAD_DIM, SSM_HEADS_PER_GROUP, SSM_GROUP_WIDTH
        self.L = L
        self.ii, self.jj = _iota(L, L, 0), _iota(L, L, 1)
        self.lower = jnp.where(self.ii >= self.jj, 1.0, 0.0).astype(BF16)
        self.upper = jnp.where(self.ii <= self.jj, 1.0, 0.0).astype(BF16)
        self.lo = _iota(L, 2 * P, 1) < P
        self.lo_row = _iota(1, 2 * P, 1) < P
        bi, bj = _iota(W, W, 0), _iota(W, W, 1)
        self.block = jnp.where(bi // P == bj // P, 1.0, 0.0).astype(BF16)
        si, sj = _iota(2 * P, W, 0), _iota(2 * P, W, 1)
        self.pick = jnp.where(sj == si * P, 1.0, 0.0).astype(BF16)
        self.ones = jnp.ones((L, 2 * P), BF16)

    def spread(self, v4):
        R = v4.shape[0]
        lo = self.lo if R == self.L else self.lo_row
        b = lambda h: jnp.broadcast_to(v4[:, h:h + 1], (R, 2 * SSM_HEAD_DIM))
        return jnp.concatenate([jnp.where(lo, b(0), b(1)), jnp.where(lo, b(2), b(3))], axis=1)

    def gather4(self, v):
        return jnp.concatenate([v[:, h * SSM_HEAD_DIM:h * SSM_HEAD_DIM + 1] for h in range(SSM_HEADS_PER_GROUP)],
                               axis=1)

    def head_sums(self, z):
        return sum(lax.dot_general(p, self.block, _NN, preferred_element_type=F32) for p in _parts(z, 2))

    def pair_cols(self, full, pair):
        ps = full[:, pair * 128:(pair + 1) * 128]
        sw = pltpu.roll(ps, SSM_HEAD_DIM, 1)
        return jnp.where(self.lo, ps, sw), jnp.where(self.lo, sw, ps)

    def gates(self, dt4_raw, p):
        L = self.L
        dtr = self.spread(dt4_raw + p[0:1, :])
        dt = _softplus(dtr)
        A = self.spread(-jnp.exp(p[1:2, :]))
        acum = _sum_by(self.lower, dt * A)
        yield
        rows = _sum_by(self.pick, acum, _NT)
        yield
        a_last = acum[L - 1:L, :]
        cols = self.pair_cols(acum, 0) + self.pair_cols(acum, 1)
        decay, decay_t = [], []
        for h in range(SSM_HEADS_PER_GROUP):
            seg = cols[h] - rows[h:h + 1, :]
            decay.append(jnp.exp(jnp.where(self.ii >= self.jj, seg, NEG_INF)))
            decay_t.append(jnp.exp(jnp.where(self.jj >= self.ii, -seg, NEG_INF)))
        return dict(dtr=dtr, dt=dt, A=A, D=self.spread(p[2:3, :]), acum=acum, eac=jnp.exp(acum), a_last=a_last,
                    wdec=jnp.exp(a_last - acum), decay=decay, decay_t=decay_t,
                    ea_last=[jnp.exp(rows[h:h + 1, L - 1:L]) for h in range(SSM_HEADS_PER_GROUP)])


def _ssd_fwd(conv, proj, dt_raw, pvec, nw, name):
    S = conv.shape[0]
    L, P, N, H, W, G = SSM_CHUNK, SSM_HEAD_DIM, SSM_D_STATE, SSM_HEADS_PER_GROUP, SSM_GROUP_WIDTH, SSM_GROUPS_PER_STEP
    nc = S // L

    def body(x_ref, b_ref, c_ref, z_ref, dt_ref, p_ref, nw_ref, y_ref, yn_ref, s0_ref, state):
        c = pl.program_id(1)

        @pl.when(c == 0)
        def _():
            state[...] = jnp.zeros_like(state)

        k = _SsdGroup(L)

        def group(gi):
            gsl = slice(gi * W, (gi + 1) * W)
            Bm, Cm = b_ref[:, gi * N:(gi + 1) * N], c_ref[:, gi * N:(gi + 1) * N]
            x = x_ref[:, gsl]
            S0 = state[gsl, :]
            s0_ref[gi, 0] = S0
            CB = _dot(Cm, Bm, _NT)
            y_off = _dot(Cm, S0, _NT)
            t = yield from k.gates(dt_ref[gi], p_ref[gi])
            xdt = x * t["dt"]
            s_new = _dot(xdt * t["wdec"], Bm, _TN)
            y_diag = []
            for pair in range(H // 2):
                xp = xdt[:, pair * 128:(pair + 1) * 128]
                y_diag.append(jnp.where(k.lo, _dot(CB * t["decay"][2 * pair], xp),
                                        _dot(CB * t["decay"][2 * pair + 1], xp)))
            yield
            y = jnp.concatenate(y_diag, axis=1) + y_off * t["eac"]
            for h in range(H):
                hsl = slice(gi * W + h * P, gi * W + (h + 1) * P)
                state[hsl, :] = S0[h * P:(h + 1) * P, :] * t["ea_last"][h] + s_new[h * P:(h + 1) * P, :]
            y_ref[:, gsl] = y
            y2 = (y + t["D"] * x) * _silu(z_ref[:, gsl])
            r = lax.rsqrt(jnp.mean(y2 * y2, axis=-1, keepdims=True) + EPS)
            yn_ref[:, gsl] = (y2 * r * nw_ref[gi]).astype(BF16)

        _lockstep(group(gi) for gi in range(G))

    x_spec, b_spec, c_spec, z_spec, dt_spec, p_spec, nw_spec, s_spec = _ssd_specs(L, lambda c: c)
    y_spec = pl.BlockSpec((L, G * W), lambda g, c: (c, g))
    return pl.pallas_call(
        body, name=name, grid=(SSM_GROUPS // G, nc),
        in_specs=[x_spec, b_spec, c_spec, z_spec, dt_spec, p_spec, nw_spec],
        out_specs=(y_spec, y_spec, s_spec),
        out_shape=(jax.ShapeDtypeStruct((S, SSM_GROUPS * W), F32), jax.ShapeDtypeStruct((S, SSM_GROUPS * W), BF16),
                   jax.ShapeDtypeStruct((SSM_GROUPS, nc, W, N), F32)),
        scratch_shapes=[pltpu.VMEM((G * W, N), F32)],
        compiler_params=_params("parallel", "arbitrary"))(conv, conv, conv, proj, dt_raw, pvec, nw)


def _ssd_bwd(dyn, conv, proj, dt_raw, pvec, nw, y_ssd, states, d_proj, name):
    S = conv.shape[0]
    L, P, N, H, W, G = SSM_CHUNK, SSM_HEAD_DIM, SSM_D_STATE, SSM_HEADS_PER_GROUP, SSM_GROUP_WIDTH, SSM_GROUPS_PER_STEP
    nc = S // L

    def body(dyn_ref, x_ref, b_ref, c_ref, z_ref, dt_ref, p_ref, nw_ref, y_ref, s0_ref, _,
             dx_ref, db_ref, dc_ref, dz_ref, ddt_ref, dp_ref, dnw_ref, dstate):
        c = pl.program_id(1)

        @pl.when(c == 0)
        def _():
            dstate[...] = jnp.zeros_like(dstate)
            dp_ref[...] = jnp.zeros_like(dp_ref)
            dnw_ref[...] = jnp.zeros_like(dnw_ref)

        k = _SsdGroup(L)
        last = (_iota(L, 1, 0) == L - 1)

        def group(gi):
            gsl = slice(gi * W, (gi + 1) * W)
            Bm, Cm = b_ref[:, gi * N:(gi + 1) * N], c_ref[:, gi * N:(gi + 1) * N]
            x, z = x_ref[:, gsl], z_ref[:, gsl]
            S0, dS1 = s0_ref[gi, 0], dstate[gsl, :]
            CB = _dot(Cm, Bm, _NT)
            CBt = _dot(Bm, Cm, _NT)
            y_off_raw = _dot(Cm, S0, _NT)
            dXs_raw = _dot(Bm, dS1, _NT)
            t = yield from k.gates(dt_ref[gi], p_ref[gi])
            y1 = y_ref[:, gsl] + t["D"] * x
            sz = _silu(z)
            y2 = y1 * sz
            r = lax.rsqrt(jnp.mean(y2 * y2, axis=-1, keepdims=True) + EPS)
            y2h = y2 * r
            dyn_v = dyn_ref[:, gsl]
            dnw_ref[gi] += _colsum(dyn_v * y2h)
            dy2h = dyn_v * nw_ref[gi]
            dy2 = r * (dy2h - y2h * jnp.mean(dy2h * y2h, axis=-1, keepdims=True))
            dz_ref[:, gsl] = (dy2 * y1 * _dsilu(z)).astype(BF16)
            dY = dy2 * sz
            X = x * t["dt"]
            dYe = dY * t["eac"]
            dC_s = _dot(dYe, S0)
            dB_s = _dot(X * t["wdec"], dS1)
            dS_c = _dot(dYe, Cm, _TN)
            dXm, Gs, Gts = [], [], []
            for pair in range(H // 2):
                dYp, Xp = dY[:, pair * 128:(pair + 1) * 128], X[:, pair * 128:(pair + 1) * 128]
                dXm.append(jnp.where(k.lo, _dot(CBt * t["decay_t"][2 * pair], dYp),
                                     _dot(CBt * t["decay_t"][2 * pair + 1], dYp)))
                for mask in (k.lo, ~k.lo):
                    Gs.append(_dot(jnp.where(mask, dYp, 0.0), Xp, _NT))
                    Gts.append(_dot(jnp.where(mask, Xp, 0.0), dYp, _NT))
            yield
            dXs = dXs_raw * t["wdec"]
            dX = jnp.concatenate(dXm, axis=1) + dXs
            dCB, dCBt, q_sums = 0.0, 0.0, []
            for h in range(H):
                M, Mt = CB * t["decay"][h], CBt * t["decay_t"][h]
                dCB = dCB + Gs[h] * t["decay"][h]
                dCBt = dCBt + Gts[h] * t["decay_t"][h]
                d = Gs[h] * M - Gts[h] * Mt
                q_sums.append(sum(lax.dot_general(pt, k.ones, _NN, preferred_element_type=F32)
                                  for pt in _parts(d, 2)))
            q_f = jnp.concatenate([jnp.where(k.lo, q_sums[0], q_sums[1]), jnp.where(k.lo, q_sums[2], q_sums[3])],
                                  axis=1)
            x_dxs = k.head_sums(X * dXs)
            tot = [_total(dS1[h * P:(h + 1) * P, :] * S0[h * P:(h + 1) * P, :]) * t["ea_last"][h] for h in range(H)]
            tot_f = k.spread(jnp.concatenate(tot, axis=1))
            d_alast = _colsum(x_dxs) + tot_f
            dacum = q_f + k.head_sums(dY * (y_off_raw * t["eac"])) - x_dxs + jnp.where(last, d_alast, 0.0)
            dx_dt = k.head_sums(dX * x)
            d_skip = _colsum(k.head_sums(dY * x))
            for h in range(H):
                hsl = slice(gi * W + h * P, gi * W + (h + 1) * P)
                dstate[hsl, :] = t["ea_last"][h] * dS1[h * P:(h + 1) * P, :] + dS_c[h * P:(h + 1) * P, :]
            dc_s2 = _dot(dCB, Bm)
            db_s2 = _dot(dCBt, Cm)
            yield
            da = _sum_by(k.upper, dacum)
            yield
            ddt_raw = (da * t["A"] + dx_dt) * _sigmoid(t["dtr"])
            dx_ref[:, gsl] = dX * t["dt"] + t["D"] * dY
            dc_ref[:, gi * N:(gi + 1) * N] = dC_s + dc_s2
            db_ref[:, gi * N:(gi + 1) * N] = dB_s + db_s2
            ddt_ref[gi] = k.gather4(ddt_raw)
            dp_ref[gi] += k.gather4(jnp.concatenate([_colsum(ddt_raw), _colsum(da * t["dt"]) * t["A"], d_skip],
                                                    axis=0))

        _lockstep(group(gi) for gi in range(G))

    rev = lambda c: nc - 1 - c
    x_spec, b_spec, c_spec, z_spec, dt_spec, p_spec, nw_spec, s_spec = _ssd_specs(L, rev)
    y_spec = pl.BlockSpec((L, G * W), lambda g, c: (rev(c), g))
    n_spec = pl.BlockSpec((L, G * N), lambda g, c: (rev(c), g))
    return pl.pallas_call(
        body, name=name, grid=(SSM_GROUPS // G, nc),
        in_specs=[y_spec, x_spec, b_spec, c_spec, z_spec, dt_spec, p_spec, nw_spec, y_spec, s_spec, D_PROJ_ANY],
        out_specs=(y_spec, n_spec, n_spec, z_spec, dt_spec, p_spec, nw_spec),
        out_shape=(jax.ShapeDtypeStruct((S, SSM_GROUPS * W), F32), jax.ShapeDtypeStruct((S, SSM_GROUPS * N), F32),
                   jax.ShapeDtypeStruct((S, SSM_GROUPS * N), F32), jax.ShapeDtypeStruct(d_proj.shape, BF16),
                   jax.ShapeDtypeStruct((SSM_GROUPS, S, H), F32), jax.ShapeDtypeStruct((SSM_GROUPS, 3, H), F32),
                   jax.ShapeDtypeStruct((SSM_GROUPS, 1, W), F32)),
        scratch_shapes=[pltpu.VMEM((G * W, N), F32)], input_output_aliases={10: 3},
        compiler_params=_params("parallel", "arbitrary"))(dyn, conv, conv, conv, proj, dt_raw, pvec, nw, y_ssd, states,
                                                          d_proj)


def _unit_lower_inverse(A, ii, jj):
    eye = (ii == jj).astype(F32)
    same = (ii // GDN_INV_BLOCK) == (jj // GDN_INV_BLOCK)
    Ad = jnp.where(same, A, 0.0)
    Ao = A - Ad
    P2 = _dot3(Ad, Ad)
    yield
    P4, X = _dot3(P2, P2), _dot3(eye - Ad, eye + P2)
    yield
    P8, X = _dot3(P4, P4), _dot3(X, eye + P4)
    yield
    X = _dot3(X, eye + P8)
    yield
    Bm = _dot3(X, Ao)
    yield
    B2 = _dot3(Bm, Bm)
    yield
    Y = _dot3(eye - Bm, eye + B2)
    yield
    T = _dot3(Y, X)
    yield
    return T


def _gdn_specs(L, order):
    G = GDN_QK_PER_STEP
    Hd, W = G * GDN_HEAD, G * GDN_V_PER_QK * GDN_HEAD
    q_spec = pl.BlockSpec((L, Hd), lambda h, c: (order(c), (C_QKV - C_XBC) // Hd + h))
    k_spec = pl.BlockSpec((L, Hd), lambda h, c: (order(c), (C_QKV - C_XBC + 1024) // Hd + h))
    v_spec = pl.BlockSpec((L, W), lambda h, c: (order(c), (C_QKV - C_XBC + 2048) // W + h))
    z_spec = pl.BlockSpec((L, W), lambda h, c: (order(c), C_ZG // W + h))
    ba_spec = pl.BlockSpec((G, L, GDN_V_PER_QK), lambda h, c: (h, order(c), 0))
    p_spec = pl.BlockSpec((G, 2, GDN_V_PER_QK), lambda h, c: (h, 0, 0))
    nw_spec = pl.BlockSpec((1, GDN_HEAD), lambda h, c: (0, 0))
    s_spec = pl.BlockSpec((G, 1, GDN_V_PER_QK * GDN_HEAD, GDN_HEAD), lambda h, c: (h, order(c), 0, 0))
    t_spec = pl.BlockSpec((G * GDN_V_PER_QK, 1, L, L), lambda h, c: (h, order(c), 0, 0))
    return q_spec, k_spec, v_spec, z_spec, ba_spec, p_spec, nw_spec, s_spec, t_spec


def _gdn_gates(qa, ka, b_col, a_col, p, j, ii, jj):
    L = qa.shape[0]
    sp_in = a_col + p[0:1, j:j + 1]
    neg_ea = -jnp.exp(p[1:2, j:j + 1])
    g = neg_ea * _softplus(sp_in)
    gcum, gcum_row = _cumsum_forms(g, ii, jj)
    rq = lax.rsqrt(_rowsum(qa * qa) + EPS)
    rk = lax.rsqrt(_rowsum(ka * ka) + EPS)
    q = qa * rq * (GDN_HEAD ** -0.5)
    k = ka * rk
    beta = _sigmoid(b_col)
    yield
    Dm = jnp.exp(jnp.where(ii >= jj, gcum - gcum_row, NEG_INF))
    eg = jnp.exp(gcum)
    g_last = gcum[L - 1:L, :]
    wdec = jnp.exp(g_last - gcum)
    return dict(rq=rq, rk=rk, q=q, k=k, beta=beta, sp_in=sp_in, neg_ea=neg_ea, g=g, Dm=Dm, kbeta=k * beta, eg=eg,
                g_last=g_last, wdec=wdec, kdec=k * wdec)


def _gdn_fwd(conv, proj, b_raw, a_raw, pvec, nw, name):
    S = conv.shape[0]
    L, Hd, J, G = GDN_CHUNK, GDN_HEAD, GDN_V_PER_QK, GDN_QK_PER_STEP
    W = J * Hd
    nc = S // L

    def body(q_ref, k_ref, v_ref, z_ref, b_ref, a_ref, p_ref, nw_ref, o_ref, on_ref, s0_ref, t_ref, state):
        c = pl.program_id(1)

        @pl.when(c == 0)
        def _():
            state[...] = jnp.zeros_like(state)

        ii, jj = _iota(L, L, 0), _iota(L, L, 1)
        for hq in range(G):
            s0_ref[hq, 0] = state[hq * W:(hq + 1) * W, :]

        def head(hq, j):
            hd = hq * J + j
            hsl, sl = slice(hq * Hd, (hq + 1) * Hd), slice(hd * Hd, (hd + 1) * Hd)
            t = yield from _gdn_gates(q_ref[:, hsl], k_ref[:, hsl], b_ref[hq][:, j:j + 1], a_ref[hq][:, j:j + 1],
                                      p_ref[hq], j, ii, jj)
            KK = _dot(t["kbeta"], t["k"], _NT)
            QK = _dot(t["q"], t["k"], _NT)
            yield
            T = yield from _unit_lower_inverse(jnp.where(ii > jj, KK * t["Dm"], 0.0), ii, jj)
            t_ref[hd, 0] = T
            S0 = state[sl, :]
            U = _dot3(T, v_ref[:, sl] * t["beta"])
            Wm = _dot3(T, t["kbeta"] * t["eg"])
            o_inter = _dot(t["q"] * t["eg"], S0)
            yield
            Vn = U - _dot(Wm, S0)
            yield
            o = o_inter + _dot(QK * t["Dm"], Vn)
            s_new = _dot(t["kdec"], Vn, _TN)
            yield
            state[sl, :] = S0 * jnp.exp(t["g_last"]) + s_new
            o_ref[:, sl] = o
            r = lax.rsqrt(jnp.mean(o * o, axis=-1, keepdims=True) + EPS)
            on_ref[:, sl] = ((o * r * nw_ref[...]) * _silu(z_ref[:, sl])).astype(BF16)

        _lockstep(head(hq, j) for hq in range(G) for j in range(J))

    q_spec, k_spec, v_spec, z_spec, ba_spec, p_spec, nw_spec, s_spec, t_spec = _gdn_specs(L, lambda c: c)
    o_spec = pl.BlockSpec((L, G * W), lambda h, c: (c, h))
    return pl.pallas_call(
        body, name=name, grid=(GDN_QK_HEADS // G, nc),
        in_specs=[q_spec, k_spec, v_spec, z_spec, ba_spec, ba_spec, p_spec, nw_spec],
        out_specs=(o_spec, o_spec, s_spec, t_spec),
        out_shape=(jax.ShapeDtypeStruct((S, GDN_QK_HEADS * W), F32), jax.ShapeDtypeStruct((S, GDN_QK_HEADS * W), BF16),
                   jax.ShapeDtypeStruct((GDN_QK_HEADS, nc, W, Hd), F32),
                   jax.ShapeDtypeStruct((GDN_QK_HEADS * J, nc, L, L), F32)),
        scratch_shapes=[pltpu.VMEM((G * W, Hd), F32)],
        compiler_params=_params("parallel", "arbitrary"))(conv, conv, conv, proj, b_raw, a_raw, pvec, nw)


def _gdn_bwd(don, conv, proj, b_raw, a_raw, pvec, nw, o_pre, states, t_inv, d_proj, name):
    S = conv.shape[0]
    L, Hd, J, G = GDN_CHUNK, GDN_HEAD, GDN_V_PER_QK, GDN_QK_PER_STEP
    W = J * Hd
    nc = S // L

    def body(don_ref, q_ref, k_ref, v_ref, z_ref, b_ref, a_ref, p_ref, nw_ref, o_ref, s0_ref, t_ref, _,
             dq_ref, dk_ref, dv_ref, dz_ref, db_ref, da_ref, dp_ref, dnw_ref, dstate):
        c = pl.program_id(1)

        @pl.when(c == 0)
        def _():
            dstate[...] = jnp.zeros_like(dstate)
            dp_ref[...] = jnp.zeros_like(dp_ref)
            dnw_ref[...] = jnp.zeros_like(dnw_ref)

        ii, jj = _iota(L, L, 0), _iota(L, L, 1)
        last = (_iota(L, 1, 0) == L - 1)
        res = {}

        def head(hq, j):
            hd = hq * J + j
            hsl, sl = slice(hq * Hd, (hq + 1) * Hd), slice(hd * Hd, (hd + 1) * Hd)
            qa, ka = q_ref[:, hsl], k_ref[:, hsl]
            t = yield from _gdn_gates(qa, ka, b_ref[hq][:, j:j + 1], a_ref[hq][:, j:j + 1], p_ref[hq], j, ii, jj)
            q, k, beta, eg, Dm, kbeta, kdec = (t[nm] for nm in ("q", "k", "beta", "eg", "Dm", "kbeta", "kdec"))
            T = t_ref[hd, 0]
            v, z, o = v_ref[:, sl], z_ref[:, sl], o_ref[:, sl]
            S0, dS1 = s0_ref[hq, 0, j * Hd:(j + 1) * Hd, :], dstate[sl, :]
            sz = _silu(z)
            r = lax.rsqrt(jnp.mean(o * o, axis=-1, keepdims=True) + EPS)
            oh = o * r
            d_on = don_ref[:, sl]
            dz_ref[:, sl] = (d_on * (oh * nw_ref[...]) * _dsilu(z)).astype(BF16)
            dn = d_on * sz
            dnw_part = _colsum(dn * oh)
            doh = dn * nw_ref[...]
            dO = r * (doh - oh * jnp.mean(doh * oh, axis=-1, keepdims=True))
            Rw = kbeta * eg
            qe = q * eg
            U = _dot3(T, v * beta)
            Wm = _dot3(T, Rw)
            KK = _dot(kbeta, k, _NT)
            QK = _dot(q, k, _NT)
            o_inter = _dot(qe, S0)
            dq_s = _dot(dO, S0, _NT)
            dS_q = _dot(qe, dO, _TN)
            yield
            Am = jnp.where(ii > jj, KK * Dm, 0.0)
            Pm = QK * Dm
            Vn = U - _dot(Wm, S0)
            dVn_s = _dot(kdec, dS1)
            yield
            dVn = _dot(Pm, dO, _TN) + dVn_s
            dP = _dot(dO, Vn, _NT)
            dKd = _dot(Vn, dS1, _NT)
            yield
            dQK = dP * Dm
            dq = _dot(dQK, k) + dq_s * eg
            dk = _dot(dQK, q, _TN) + dKd * t["wdec"]
            dstate[sl, :] = jnp.exp(t["g_last"]) * dS1 + dS_q - _dot(Wm, dVn, _TN)
            dW = -_dot(dVn, S0, _NT)
            dRu = _dot3(T, dVn, _TN)
            yield
            dRw = _dot3(T, dW, _TN)
            dA_u = _dot(dRu, U, _NT)
            yield
            dA = jnp.where(ii > jj, -(dA_u + _dot(dRw, Wm, _NT)), 0.0)
            yield
            dKK = dA * Dm
            dkbeta = _dot(dKK, k) + dRw * eg
            dk = dk + _dot(dKK, kbeta, _TN)
            yield
            dk = dk + dkbeta * beta
            dbeta = _rowsum(dkbeta * k) + _rowsum(dRu * v)
            dv_ref[:, sl] = dRu * beta
            Q = dA * Am + dP * Pm
            rho = _rowsum(dKd * kdec)
            d_glast = _colsum(rho) + jnp.exp(t["g_last"]) * _total(dS1 * S0)
            q_sums = _row_col_sums(Q)
            rest = _rowsum(dRw * Rw) + _rowsum(dO * o_inter) - rho + jnp.where(last, d_glast, 0.0)
            yield
            dg = _rev_cumsum_col(q_sums + rest, ii, jj)
            yield
            da_raw = dg * t["neg_ea"] * _sigmoid(t["sp_in"])
            res[hq, j] = dict(dq=dq, dk=dk, db=dbeta * beta * (1.0 - beta), da=da_raw, d_bias=_colsum(da_raw),
                              d_alog=_colsum(dg * t["g"]), dnw=dnw_part, rq=t["rq"], rk=t["rk"], k=k, qh=qa * t["rq"])

        _lockstep(head(hq, j) for hq in range(G) for j in range(J))
        for hq in range(G):
            parts = [res[hq, j] for j in range(J)]
            hsl = slice(hq * Hd, (hq + 1) * Hd)
            p0 = parts[0]
            dqh = sum(pt["dq"] for pt in parts) * (GDN_HEAD ** -0.5)
            dkn = sum(pt["dk"] for pt in parts)
            dq_ref[:, hsl] = p0["rq"] * (dqh - p0["qh"] * _rowsum(dqh * p0["qh"]))
            dk_ref[:, hsl] = p0["rk"] * (dkn - p0["k"] * _rowsum(dkn * p0["k"]))
            db_ref[hq] = jnp.concatenate([pt["db"] for pt in parts], axis=1)
            da_ref[hq] = jnp.concatenate([pt["da"] for pt in parts], axis=1)
            dp_ref[hq] += jnp.concatenate([jnp.concatenate([pt["d_bias"] for pt in parts], axis=1),
                                           jnp.concatenate([pt["d_alog"] for pt in parts], axis=1)], axis=0)
            dnw_ref[hq] += sum(pt["dnw"] for pt in parts)

    rev = lambda c: nc - 1 - c
    q_spec, k_spec, v_spec, z_spec, ba_spec, p_spec, nw_spec, s_spec, t_spec = _gdn_specs(L, rev)
    o_spec = pl.BlockSpec((L, G * W), lambda h, c: (rev(c), h))
    h_spec = pl.BlockSpec((L, G * Hd), lambda h, c: (rev(c), h))
    dnw_spec = pl.BlockSpec((G, 1, Hd), lambda h, c: (h, 0, 0))
    return pl.pallas_call(
        body, name=name, grid=(GDN_QK_HEADS // G, nc),
        in_specs=[o_spec, q_spec, k_spec, v_spec, z_spec, ba_spec, ba_spec, p_spec, nw_spec, o_spec, s_spec, t_spec,
                  D_PROJ_ANY],
        out_specs=(h_spec, h_spec, o_spec, z_spec, ba_spec, ba_spec, p_spec, dnw_spec),
        out_shape=(jax.ShapeDtypeStruct((S, GDN_QK_HEADS * Hd), F32), jax.ShapeDtypeStruct((S, GDN_QK_HEADS * Hd), F32),
                   jax.ShapeDtypeStruct((S, GDN_QK_HEADS * W), F32), jax.ShapeDtypeStruct(d_proj.shape, BF16),
                   jax.ShapeDtypeStruct((GDN_QK_HEADS, S, J), F32), jax.ShapeDtypeStruct((GDN_QK_HEADS, S, J), F32),
                   jax.ShapeDtypeStruct((GDN_QK_HEADS, 2, J), F32), jax.ShapeDtypeStruct((GDN_QK_HEADS, 1, Hd), F32)),
        scratch_shapes=[pltpu.VMEM((G * W, Hd), F32)], input_output_aliases={12: 3},
        compiler_params=_params("parallel", "arbitrary"))(don, conv, conv, conv, proj, b_raw, a_raw, pvec, nw, o_pre,
                                                          states, t_inv, d_proj)


def _ada_fwd(c_all, w_loc, b_loc, name):
    n = w_loc.shape[1]

    def body(c_ref, w_ref, b_ref, o_ref):
        o_ref[...] = _dot3(_silu(c_ref[...]), w_ref[...]) + b_ref[...]

    return pl.pallas_call(body, name=name, out_shape=jax.ShapeDtypeStruct((N_DEV, n), F32),
                          compiler_params=pltpu.CompilerParams(vmem_limit_bytes=VMEM_LIMIT))(c_all, w_loc, b_loc)


def _ada_bwd(c_all_t, dmod_cols, name):
    Dm, n = c_all_t.shape[0], dmod_cols.shape[1]

    def body(c_ref, d_ref, o_ref):
        ca = _silu(c_ref[...])
        acc = ca[:, 0:1] * d_ref[0:1, :]
        for i in range(1, N_DEV):
            acc = acc + ca[:, i:i + 1] * d_ref[i:i + 1, :]
        o_ref[...] = acc

    return pl.pallas_call(body, name=name, out_shape=jax.ShapeDtypeStruct((Dm, n), F32),
                          compiler_params=pltpu.CompilerParams(vmem_limit_bytes=VMEM_LIMIT))(c_all_t, dmod_cols)


ADAM_BLOCK_BYTES = 12 * 1024 * 1024


def _adam(contrib, w, m, v, name):
    n, R, C = contrib.shape
    tr = R
    while tr % 16 == 0 and (n + 7) * tr * C * 4 > ADAM_BLOCK_BYTES:
        tr //= 2

    def body(c_ref, w_ref, m_ref, v_ref, g_ref, d_ref, nm_ref, nv_ref):
        g = c_ref[0].astype(F32)
        for i in range(1, n):
            g = g + c_ref[i].astype(F32)
        nm = ADAM_B1 * m_ref[...] + (1.0 - ADAM_B1) * g
        nv = ADAM_B2 * v_ref[...] + (1.0 - ADAM_B2) * (g * g)
        m_hat = nm / (1.0 - ADAM_B1 ** ADAM_STEP)
        v_hat = nv / (1.0 - ADAM_B2 ** ADAM_STEP)
        g_ref[...] = g
        d_ref[...] = -ADAM_LR * (m_hat / (jnp.sqrt(v_hat) + ADAM_EPS) + ADAM_WD * w_ref[...])
        nm_ref[...] = nm
        nv_ref[...] = nv

    spec = pl.BlockSpec((tr, C), lambda i: (i, 0))
    shp = jax.ShapeDtypeStruct((R, C), F32)
    return pl.pallas_call(
        body, name=name, grid=(R // tr,), in_specs=[pl.BlockSpec((n, tr, C), lambda i: (0, i, 0)), spec, spec, spec],
        out_specs=(spec,) * 4, out_shape=(shp,) * 4, compiler_params=_params("parallel"))(contrib, w, m, v)


def _exchange(arrays, modes, name, chips=False):
    n = len(arrays)
    out_shape = tuple(jax.ShapeDtypeStruct((N_DEV,) + a.shape if md == "gather" else a.shape, a.dtype)
                      for a, md in zip(arrays, modes))

    def body(*refs):
        ins, outs = refs[:n], refs[n:2 * n]
        send_sems, recv_sems, loc_sems = refs[2 * n:]
        me, peers = _peer_table(chips)

        def src(k, slot):
            return ins[k] if modes[k] == "gather" else ins[k].at[slot]

        def remote(k, m, to_slot, land_slot):
            return pltpu.make_async_remote_copy(
                src_ref=src(k, to_slot), dst_ref=outs[k].at[land_slot], send_sem=send_sems.at[k, m],
                recv_sem=recv_sems.at[k, m], device_id=peers[m][0], device_id_type=pl.DeviceIdType.MESH)

        local = [pltpu.make_async_copy(src(k, me), outs[k].at[me], loc_sems.at[k]) for k in range(n)]
        for cp in local:
            cp.start()
        sends = [remote(k, m, peers[m][1], me) for m in range(len(peers)) for k in range(n)]
        for cp in sends:
            cp.start()
        for m in range(len(peers)):
            for k in range(n):
                remote(k, m, peers[m][1], peers[m][1]).wait_recv()
        for cp in sends:
            cp.wait_send()
        for cp in local:
            cp.wait()

    any_spec = pl.BlockSpec(memory_space=pl.ANY)
    return pl.pallas_call(
        body, name=name, in_specs=[any_spec] * n, out_specs=(any_spec,) * n, out_shape=out_shape,
        scratch_shapes=[pltpu.SemaphoreType.DMA((n, N_DEV - 1)), pltpu.SemaphoreType.DMA((n, N_DEV - 1)),
                        pltpu.SemaphoreType.DMA((n,))])(*arrays)


def _gather_two_level(arrays, name):
    n = len(arrays)
    out_shape = tuple(jax.ShapeDtypeStruct((N_DEV,) + a.shape, a.dtype) for a in arrays)

    def body(*refs):
        ins, outs = refs[:n], refs[n:2 * n]
        send_sems, recv_sems, loc_sems = refs[2 * n:]
        ix, iy, ic = lax.axis_index("x"), lax.axis_index("y"), lax.axis_index("c")
        lin = lambda px, py, pc: 4 * px + 2 * py + pc
        me, sib = lin(ix, iy, ic), (ix, iy, 1 - ic)
        chips = [(1 - ix, iy), (ix, 1 - iy), (1 - ix, 1 - iy)]

        def copy(k, s, block, to, src=None):
            return pltpu.make_async_remote_copy(
                src_ref=outs[k].at[block] if src is None else src, dst_ref=outs[k].at[block],
                send_sem=send_sems.at[k, s], recv_sem=recv_sems.at[k, s], device_id=to,
                device_id_type=pl.DeviceIdType.MESH)

        local = [pltpu.make_async_copy(ins[k], outs[k].at[me], loc_sems.at[k]) for k in range(n)]
        for cp in local:
            cp.start()
        first = [copy(k, 1 + j, me, (cx, cy, ic), src=ins[k]) for j, (cx, cy) in enumerate(chips) for k in range(n)]
        first += [copy(k, 0, me, sib, src=ins[k]) for k in range(n)]
        for cp in first:
            cp.start()
        passed = []
        for j, (cx, cy) in enumerate(chips):
            for k in range(n):
                copy(k, 1 + j, lin(cx, cy, ic), sib).wait_recv()
                passed.append(copy(k, 4 + j, lin(cx, cy, ic), sib))
                passed[-1].start()
        for k in range(n):
            copy(k, 0, lin(*sib), sib).wait_recv()
            for j, (cx, cy) in enumerate(chips):
                copy(k, 4 + j, lin(cx, cy, 1 - ic), sib).wait_recv()
        for cp in first + passed:
            cp.wait_send()
        for cp in local:
            cp.wait()

    any_spec = pl.BlockSpec(memory_space=pl.ANY)
    return pl.pallas_call(
        body, name=name, in_specs=[any_spec] * n, out_specs=(any_spec,) * n, out_shape=out_shape,
        scratch_shapes=[pltpu.SemaphoreType.DMA((n, N_DEV - 1)), pltpu.SemaphoreType.DMA((n, N_DEV - 1)),
                        pltpu.SemaphoreType.DMA((n,))])(*arrays)


def _peer_table(chips=False):
    ix, iy, ic = lax.axis_index("x"), lax.axis_index("y"), lax.axis_index("c")
    peers = []
    for m in ((2, 4, 6) if chips else range(1, N_DEV)):
        px = 1 - ix if m & 4 else ix
        py = 1 - iy if m & 2 else iy
        pc = 1 - ic if m & 1 else ic
        peers.append(((px, py, pc), 2 * px + py if chips else 4 * px + 2 * py + pc))
    return (2 * ix + iy if chips else 4 * ix + 2 * iy + ic), peers


def _exchange_start(arrays, modes, after, name, chips=False):
    n = len(arrays)
    land_shapes = [(N_DEV,) + a.shape if md == "gather" else a.shape for a, md in zip(arrays, modes)]

    def body(*refs):
        ins, lands = refs[:n], refs[n:2 * n]
        send_sems, recv_sems = refs[2 * n + 1], refs[2 * n + 2]
        token = refs[-1]
        me, peers = _peer_table(chips)

        def src(k, slot):
            return ins[k] if modes[k] == "gather" else ins[k].at[slot]

        for peer, slot in peers:
            for k in range(n):
                pltpu.make_async_remote_copy(
                    src_ref=src(k, slot), dst_ref=lands[k].at[me], send_sem=send_sems, recv_sem=recv_sems,
                    device_id=peer, device_id_type=pl.DeviceIdType.MESH).start()
        token[...] = jnp.zeros_like(token)

    hbm = pl.BlockSpec(memory_space=pltpu.HBM)
    sem = pl.BlockSpec(memory_space=pltpu.SEMAPHORE)
    sem_shape = pltpu.SemaphoreType.DMA(())
    operands = [pltpu.with_memory_space_constraint(a, pltpu.HBM) for a in arrays]
    operands += [pltpu.with_memory_space_constraint(lax.empty(s, a.dtype), pltpu.HBM)
                 for s, a in zip(land_shapes, arrays)]
    out = pl.pallas_call(
        body, name=name,
        out_shape=(sem_shape, sem_shape) + tuple(pltpu.HBM(a.shape, a.dtype) for a in arrays)
        + tuple(pltpu.HBM(s, a.dtype) for s, a in zip(land_shapes, arrays)) + (jax.ShapeDtypeStruct((8, 128), F32),),
        in_specs=[hbm] * (2 * n) + [pl.BlockSpec(memory_space=pl.ANY)],
        out_specs=(sem, sem) + (hbm,) * (2 * n) + (pl.BlockSpec(memory_space=pltpu.VMEM),),
        input_output_aliases={i: 2 + i for i in range(2 * n)},
        compiler_params=pltpu.CompilerParams(has_side_effects=pltpu.SideEffectType.DATAFLOW_SIDE_EFFECTING))(
            *operands, after)
    return out[0], out[1], out[2:2 + n], out[2 + n:2 + 2 * n], out[-1]


def _exchange_wait(started, modes, after, name, chips=False):
    send_sems, recv_sems, sent, lands, _ = started
    n = len(sent)

    def body(*refs):
        ins, zones = refs[:n], refs[n:2 * n]
        send_ref, recv_ref = refs[2 * n], refs[2 * n + 1]
        _, peers = _peer_table(chips)

        def src(k, slot):
            return ins[k] if modes[k] == "gather" else ins[k].at[slot]

        for peer, slot in peers:
            for k in range(n):
                cp = pltpu.make_async_remote_copy(
                    src_ref=src(k, slot), dst_ref=zones[k].at[slot], send_sem=send_ref, recv_sem=recv_ref,
                    device_id=peer, device_id_type=pl.DeviceIdType.MESH)
                cp.wait_send()
                cp.wait_recv()

    hbm = pl.BlockSpec(memory_space=pltpu.HBM)
    sem = pl.BlockSpec(memory_space=pltpu.SEMAPHORE)
    out = pl.pallas_call(
        body, name=name,
        out_shape=tuple(pltpu.HBM(a.shape, a.dtype) for a in sent) + tuple(pltpu.HBM(a.shape, a.dtype) for a in lands),
        in_specs=[hbm] * (2 * n) + [sem, sem, pl.BlockSpec(memory_space=pl.ANY)], out_specs=(hbm,) * (2 * n),
        input_output_aliases={i: i for i in range(2 * n)},
        compiler_params=pltpu.CompilerParams(has_side_effects=pltpu.SideEffectType.DATAFLOW_SIDE_EFFECTING))(
            *sent, *lands, send_sems, recv_sems, after)
    ix, iy, ic = lax.axis_index("x"), lax.axis_index("y"), lax.axis_index("c")
    me = 2 * ix + iy if chips else 4 * ix + 2 * iy + ic
    filled = []
    for k in range(n):
        own = sent[k] if modes[k] == "gather" else lax.dynamic_index_in_dim(sent[k], me, axis=0, keepdims=False)
        filled.append(lax.dynamic_update_index_in_dim(out[n + k], own, me, axis=0))
    return filled


def _swap_sibling(arr, name):
    def body(in_ref, out_ref, send_sem, recv_sem):
        ix, iy, ic = lax.axis_index("x"), lax.axis_index("y"), lax.axis_index("c")
        cp = pltpu.make_async_remote_copy(src_ref=in_ref, dst_ref=out_ref, send_sem=send_sem, recv_sem=recv_sem,
                                          device_id=(ix, iy, 1 - ic), device_id_type=pl.DeviceIdType.MESH)
        cp.start()
        cp.wait()

    any_spec = pl.BlockSpec(memory_space=pl.ANY)
    return pl.pallas_call(body, name=name, in_specs=[any_spec], out_specs=any_spec,
                          out_shape=jax.ShapeDtypeStruct(arr.shape, arr.dtype),
                          scratch_shapes=[pltpu.SemaphoreType.DMA, pltpu.SemaphoreType.DMA])(arr)


def _add_pair(a, b, name):
    n, R, C = a.shape
    tr = _blk(R, 256)

    def body(a_ref, b_ref, o_ref):
        o_ref[...] = (a_ref[...].astype(F32) + b_ref[...].astype(F32)).astype(o_ref.dtype)

    spec = pl.BlockSpec((1, tr, C), lambda i, j: (i, j, 0))
    return pl.pallas_call(body, name=name, grid=(n, R // tr), in_specs=[spec, spec], out_specs=spec,
                          out_shape=jax.ShapeDtypeStruct(a.shape, a.dtype),
                          compiler_params=_params("parallel", "parallel"))(a, b)


W_IN_SPLITS = (0, 2048, 6144, 6176, 10272, 12320, 12336, 12352, 13376, 14400)
N_REPLICATED = 16640
REPLICATED = ("b_ada", "norm_mix_pre", "norm_mix_post", "ssm_conv_b", "ssm_dt_bias", "ssm_A_log", "ssm_D",
              "ssm_norm_w", "gdn_dt_bias", "gdn_A_log", "gdn_norm_w", "norm_mlp_pre", "norm_mlp_post")
WEIGHTS = ("w_ada", "b_ada", "norm_mix_pre", "norm_mix_post", "w_in", "ssm_conv_w", "ssm_conv_b", "ssm_dt_bias",
           "ssm_A_log", "ssm_D", "ssm_norm_w", "gdn_conv_w", "gdn_dt_bias", "gdn_A_log", "gdn_norm_w", "w_ssm_up",
           "w_gdn_up", "w_out", "norm_mlp_pre", "norm_mlp_post", "w_mlp_up", "w_mlp_down")


def _by_cols(t):
    return t.transpose(1, 0, 2).reshape(t.shape[1], N_DEV * t.shape[2])


def _to_col_shards(t):
    R, C8 = t.shape
    return t.reshape(R, N_DEV, C8 // N_DEV).transpose(1, 0, 2)


def _heads_first(t, groups):
    S = t.shape[0]
    return t.reshape(S, groups, t.shape[1] // groups).transpose(1, 0, 2)


def _heads_last(t):
    return t.transpose(1, 0, 2).reshape(t.shape[1], t.shape[0] * t.shape[2])


def kernel(x, c, w_ada, b_ada, norm_mix_pre, norm_mix_post, w_in, ssm_conv_w, ssm_conv_b, ssm_dt_bias, ssm_A_log, ssm_D, ssm_norm_w, gdn_conv_w, gdn_dt_bias, gdn_A_log, gdn_norm_w, w_ssm_up, w_gdn_up, w_out, norm_mlp_pre, norm_mlp_post, w_mlp_up, w_mlp_down, loss_target, m_w_ada, m_b_ada, m_norm_mix_pre, m_norm_mix_post, m_w_in, m_ssm_conv_w, m_ssm_conv_b, m_ssm_dt_bias, m_ssm_A_log, m_ssm_D, m_ssm_norm_w, m_gdn_conv_w, m_gdn_dt_bias, m_gdn_A_log, m_gdn_norm_w, m_w_ssm_up, m_w_gdn_up, m_w_out, m_norm_mlp_pre, m_norm_mlp_post, m_w_mlp_up, m_w_mlp_down, v_w_ada, v_b_ada, v_norm_mix_pre, v_norm_mix_post, v_w_in, v_ssm_conv_w, v_ssm_conv_b, v_ssm_dt_bias, v_ssm_A_log, v_ssm_D, v_ssm_norm_w, v_gdn_conv_w, v_gdn_dt_bias, v_gdn_A_log, v_gdn_norm_w, v_w_ssm_up, v_w_gdn_up, v_w_out, v_norm_mlp_pre, v_norm_mlp_post, v_w_mlp_up, v_w_mlp_down):
    S, Dm = x.shape[1], D_MODEL
    me = 4 * lax.axis_index("x") + 2 * lax.axis_index("y") + lax.axis_index("c")
    x2, tgt = x[0], loss_target[0]
    n_ada = w_ada.shape[2]
    given = dict(
        w_ada=(w_ada, m_w_ada, v_w_ada), b_ada=(b_ada, m_b_ada, v_b_ada),
        norm_mix_pre=(norm_mix_pre, m_norm_mix_pre, v_norm_mix_pre),
        norm_mix_post=(norm_mix_post, m_norm_mix_post, v_norm_mix_post), w_in=(w_in, m_w_in, v_w_in),
        ssm_conv_w=(ssm_conv_w, m_ssm_conv_w, v_ssm_conv_w), ssm_conv_b=(ssm_conv_b, m_ssm_conv_b, v_ssm_conv_b),
        ssm_dt_bias=(ssm_dt_bias, m_ssm_dt_bias, v_ssm_dt_bias), ssm_A_log=(ssm_A_log, m_ssm_A_log, v_ssm_A_log),
        ssm_D=(ssm_D, m_ssm_D, v_ssm_D), ssm_norm_w=(ssm_norm_w, m_ssm_norm_w, v_ssm_norm_w),
        gdn_conv_w=(gdn_conv_w, m_gdn_conv_w, v_gdn_conv_w), gdn_dt_bias=(gdn_dt_bias, m_gdn_dt_bias, v_gdn_dt_bias),
        gdn_A_log=(gdn_A_log, m_gdn_A_log, v_gdn_A_log), gdn_norm_w=(gdn_norm_w, m_gdn_norm_w, v_gdn_norm_w),
        w_ssm_up=(w_ssm_up, m_w_ssm_up, v_w_ssm_up), w_gdn_up=(w_gdn_up, m_w_gdn_up, v_w_gdn_up),
        w_out=(w_out, m_w_out, v_w_out), norm_mlp_pre=(norm_mlp_pre, m_norm_mlp_pre, v_norm_mlp_pre),
        norm_mlp_post=(norm_mlp_post, m_norm_mlp_post, v_norm_mlp_post), w_mlp_up=(w_mlp_up, m_w_mlp_up, v_w_mlp_up),
        w_mlp_down=(w_mlp_down, m_w_mlp_down, v_w_mlp_down))

    (c_all, scw, gcw, g_in) = _gather_two_level([c, ssm_conv_w[0], gdn_conv_w[0], w_in[0].astype(BF16)], "gather_w_in")
    c_all = c_all.reshape(N_DEV, Dm)
    wf = _by_cols(g_in)
    sp = W_IN_SPLITS
    w_main = jnp.concatenate([wf[:, sp[0]:sp[2]], wf[:, sp[3]:sp[5]], wf[:, sp[7]:sp[9]]], axis=1)
    w_small = jnp.concatenate([wf[:, sp[2]:sp[3]], wf[:, sp[5]:sp[7]], jnp.zeros((Dm, N_SMALL - 64), BF16)], axis=1)
    conv_w = jnp.concatenate([_by_cols(scw), _by_cols(gcw)], axis=1)
    conv_b = jnp.concatenate([ssm_conv_b, jnp.zeros_like(ssm_conv_b)], axis=1)

    b_loc = lax.dynamic_slice(b_ada, (0, me * n_ada), (1, n_ada))
    mod_part = _ada_fwd(c_all, w_ada[0], b_loc, "ada_fwd")
    (mod_rows,) = _exchange([mod_part.reshape(N_DEV, 1, n_ada)], ["a2a"], "exchange_mod")
    rest = _exchange_start([w_ssm_up[0].astype(BF16), w_gdn_up[0].astype(BF16), w_out[0].astype(BF16),
                            w_mlp_up[0].astype(BF16), w_mlp_down[0].astype(BF16)], ["gather"] * 5, mod_rows,
                           "gather_rest_start")
    mod = mod_rows.reshape(1, 6 * Dm) + rest[4][0:1, 0:1]
    sh1, sc1, g1, sh2, sc2, g2 = [mod[:, i * Dm:(i + 1) * Dm] for i in range(6)]

    h = _pre_fwd(x2, norm_mix_pre, sc1, sh1, "pre_mix")
    proj = _mm(h, w_main, S, N_MAIN, Dm, mode="nn", out_dtype=F32, name="proj_main")
    small = _mm(h, w_small, S, N_SMALL, Dm, mode="nn", out_dtype=F32, name="proj_small")
    conv = _conv_fwd(proj, conv_w, conv_b, "conv_fwd")
    dt_g, b_g, a_g = _heads_first(small[:, 0:32], 8), _heads_first(small[:, 32:48], 8), _heads_first(small[:, 48:64], 8)
    pv_ssm = jnp.stack([ssm_dt_bias.reshape(8, 4), ssm_A_log.reshape(8, 4), ssm_D.reshape(8, 4)], axis=1)
    nw_ssm = ssm_norm_w.reshape(8, 1, SSM_GROUP_WIDTH)
    pv_gdn = jnp.stack([gdn_dt_bias.reshape(8, 2), gdn_A_log.reshape(8, 2)], axis=1)
    y_ssd, ysn, st_ssm = _ssd_fwd(conv, proj, dt_g, pv_ssm, nw_ssm, "ssd_fwd")
    o_pre, ogn, st_gdn, t_inv = _gdn_fwd(conv, proj, b_g, a_g, pv_gdn, gdn_norm_w, "gdn_fwd")
    g_su, g_gu, g_out, g_mu, g_md = _exchange_wait(rest, ["gather"] * 5, ogn, "gather_rest_wait")
    w_su, w_gu = g_su.reshape(2 * Dm, Dm), g_gu.reshape(2 * Dm, Dm)
    w_o, w_mu, w_md = g_out.reshape(Dm, Dm), _by_cols(g_mu), g_md.reshape(4 * Dm, Dm)
    ys = _mm(ysn, w_su, S, Dm, 2 * Dm, mode="nn", out_dtype=F32, name="ssm_up")
    yg = _mm(ogn, w_gu, S, Dm, 2 * Dm, mode="nn", out_dtype=F32, name="gdn_up")
    merged = _merge_fwd(ys, yg, proj, "merge_fwd")
    mo = _mm(merged, w_o, S, Dm, Dm, mode="nn", out_dtype=F32, name="mix_out")
    x1 = _post_fwd(x2, mo, norm_mix_post, g1, "post_mix")
    h2 = _pre_fwd(x1, norm_mlp_pre, sc2, sh2, "pre_mlp")
    u, act = _mm(h2, w_mu, S, 4 * Dm, Dm, mode="nn", out_dtype=F32, epi="relu2", name="mlp_up")
    y_mlp = _mm(act, w_md, S, Dm, 4 * Dm, mode="nn", out_dtype=F32, name="mlp_down")
    dx2, loss_loc = _final_fwd(x1, y_mlp, norm_mlp_post, g2, tgt, "post_mlp_loss")

    dy, dg2, dw_post2 = _post_bwd(dx2, y_mlp, norm_mlp_post, g2, "post_mlp_bwd")
    du = _mm(dy, w_md, S, 4 * Dm, Dm, mode="nt", out_dtype=BF16, epi="drelu2", extra=u, name="mlp_down_dx")
    gw_md = _mm(act, dy, 4 * Dm, Dm, S, mode="tn", out_dtype=BF16, name="mlp_down_dw")
    dh2 = _mm(du, w_mu, S, Dm, 4 * Dm, mode="nt", out_dtype=F32, name="mlp_up_dx")
    gw_mu = _mm(h2, du, Dm, 4 * Dm, S, mode="tn", out_dtype=BF16, name="mlp_up_dw")
    mlp_x = _exchange_start([_to_col_shards(gw_mu), gw_md.reshape(N_DEV, -1, Dm)], ["a2a"] * 2, gw_md,
                            "grads_mlp_start")
    dx1, dsh2, dsc2, dw_pre2 = _pre_bwd(dh2, x1, norm_mlp_pre, sc2 + mlp_x[4][0:1, 0:1], dx2, "pre_mlp_bwd")
    dmo, dg1, dw_post1 = _post_bwd(dx1, mo, norm_mix_post, g1, "post_mix_bwd")
    dmerged = _mm(dmo, w_o, S, Dm, Dm, mode="nt", out_dtype=F32, name="mix_out_dx")
    gw_o = _mm(merged, dmo, Dm, Dm, S, mode="tn", out_dtype=BF16, name="mix_out_dw")
    dys, dyg, d_proj = _merge_bwd(dmerged, ys, yg, proj, lax.empty((S, N_MAIN), BF16), "merge_bwd")
    dysn = _mm(dys, w_su, S, 2 * Dm, Dm, mode="nt", out_dtype=F32, name="ssm_up_dx")
    gw_su = _mm(ysn, dys, 2 * Dm, Dm, S, mode="tn", out_dtype=BF16, name="ssm_up_dw")
    dogn = _mm(dyg, w_gu, S, 2 * Dm, Dm, mode="nt", out_dtype=F32, name="gdn_up_dx")
    gw_gu = _mm(ogn, dyg, 2 * Dm, Dm, S, mode="tn", out_dtype=BF16, name="gdn_up_dw")
    mix_x = _exchange_start([gw_su.reshape(N_DEV, -1, Dm), gw_gu.reshape(N_DEV, -1, Dm), gw_o.reshape(N_DEV, -1, Dm)],
                            ["a2a"] * 3, gw_gu, "grads_mix_start")
    dxs, dBm, dCm, d_proj, ddt_g, dpv_ssm, dnw_ssm = _ssd_bwd(dysn, conv, proj, dt_g, pv_ssm + mix_x[4][0, 0], nw_ssm,
                                                              y_ssd, st_ssm, d_proj, "ssd_bwd")
    dq, dk, dv, d_proj, db_g, da_g, dpv_gdn, dnw_gdn = _gdn_bwd(dogn, conv, proj, b_g, a_g, pv_gdn, gdn_norm_w, o_pre,
                                                                st_gdn, t_inv, d_proj, "gdn_bwd")
    conv_pieces = []
    for nm, d_act, col0 in (("xs", dxs, 0), ("B", dBm, 2048), ("C", dCm, 3072), ("q", dq, 4096), ("k", dk, 5120),
                            ("v", dv, 6144)):
        d_proj, dw_piece, db_piece = _conv_bwd(d_act, proj, conv_w, conv_b, col0, d_proj, "conv_bwd_" + nm)
        conv_pieces.append((dw_piece, db_piece))
    d_small = jnp.concatenate([_heads_last(ddt_g), _heads_last(db_g), _heads_last(da_g),
                               jnp.zeros((S, N_SMALL - 64), F32)], axis=1).astype(BF16)
    gw_small = _mm(h, d_small, Dm, N_SMALL, S, mode="tn", out_dtype=BF16, name="proj_small_dw")
    main_cols = _mm(h, d_proj, Dm, N_MAIN, S, mode="tn", out_dtype=BF16, name="proj_main_dw")
    gw_in = jnp.concatenate([main_cols[:, 0:C_QKV], gw_small[:, 0:32], main_cols[:, C_QKV:C_GS], gw_small[:, 32:64],
                             main_cols[:, C_GS:N_MAIN]], axis=1)
    by_dest = _to_col_shards(gw_in).reshape(N_DEV // 2, 2, Dm, -1)
    my_c = lax.axis_index("c")
    keep = lax.dynamic_index_in_dim(by_dest, my_c, axis=1, keepdims=False)
    give = lax.dynamic_index_in_dim(by_dest, 1 - my_c, axis=1, keepdims=False)
    chip_sum = _add_pair(keep, _swap_sibling(give, "grads_w_in_pair"), "grads_w_in_pair_sum")
    in_x = _exchange_start([chip_sum], ["a2a"], gw_small, "grads_w_in_start", chips=True)
    dh = _mm(d_small, w_small + in_x[4][0:1, 0:1].astype(BF16), S, Dm, N_SMALL, mode="nt", out_dtype=F32,
             name="proj_small_dx")
    dh = _mm(d_proj, w_main, S, Dm, N_MAIN, mode="nt", out_dtype=F32, add=dh, name="proj_main_dx")
    dx, dsh1, dsc1, dw_pre1 = _pre_bwd(dh, x2, norm_mix_pre, sc1, dx1, "pre_mix_bwd")
    r_mu, r_md = _exchange_wait(mlp_x, ["a2a"] * 2, dx, "grads_mlp_wait")
    r_su, r_gu, r_o = _exchange_wait(mix_x, ["a2a"] * 3, dx, "grads_mix_wait")

    dconv_w = jnp.concatenate([p[0] for p in conv_pieces], axis=1)
    dconv_b = jnp.concatenate([p[1] for p in conv_pieces[:3]], axis=1)
    dmod = jnp.concatenate([dsh1, dsc1, dg1, dsh2, dsc2, dg2], axis=1)
    small_vec = jnp.concatenate(
        [dmod, dw_pre1, dw_post1, dconv_b, dpv_ssm[:, 0].reshape(1, 32), dpv_ssm[:, 1].reshape(1, 32),
         dpv_ssm[:, 2].reshape(1, 32), dnw_ssm.reshape(1, 2048), dpv_gdn[:, 0].reshape(1, 16),
         dpv_gdn[:, 1].reshape(1, 16), jnp.sum(dnw_gdn, axis=0), dw_pre2, dw_post2, dconv_w.reshape(1, -1)], axis=1)
    n_vec = small_vec.shape[1]
    small_vec = jnp.pad(small_vec, ((0, 0), (0, (-n_vec) % 1024))).reshape(-1, 1024)
    (small_all,) = _exchange([small_vec], ["gather"], "gather_small_grads")
    small_all = small_all.reshape(N_DEV, -1)
    dmod_cols = lax.dynamic_slice(small_all, (0, me * n_ada), (N_DEV, n_ada))
    gw_ada = _ada_bwd(c_all.T, dmod_cols, "ada_bwd")
    conv_all = small_all[:, N_REPLICATED:n_vec].reshape(N_DEV, CONV_K, 2 * N_DEV * 512)
    conv_contrib = jnp.concatenate(
        [lax.dynamic_slice(conv_all, (0, 0, me * 512), (N_DEV, CONV_K, 512)),
         lax.dynamic_slice(conv_all, (0, 0, N_DEV * 512 + me * 512), (N_DEV, CONV_K, 512))], axis=1)
    rep_contrib = small_all[:, :N_REPLICATED].reshape(N_DEV, N_REPLICATED // 128, 128)

    results = {}

    def adam_big(nm, contrib):
        w3 = given[nm]
        res = _adam(contrib, w3[0][0], w3[1][0], w3[2][0], "adam_" + nm)
        results[nm] = tuple(r.reshape(w3[0].shape) for r in res)

    adam_big("w_ada", gw_ada[None])
    adam_big("w_ssm_up", r_su)
    adam_big("w_gdn_up", r_gu)
    adam_big("w_out", r_o)
    adam_big("w_mlp_up", r_mu)
    adam_big("w_mlp_down", r_md)
    (r_in,) = _exchange_wait(in_x, ["a2a"], results["w_mlp_down"][0], "grads_w_in_wait", chips=True)
    adam_big("w_in", r_in)
    packed = [jnp.concatenate([given[nm][i] for nm in REPLICATED], axis=1).reshape(N_REPLICATED // 128, 128)
              for i in range(3)]
    rep_res = _adam(rep_contrib, packed[0], packed[1], packed[2], "adam_replicated")
    pos = 0
    for nm in REPLICATED:
        size = given[nm][0].shape[1]
        results[nm] = tuple(r.reshape(1, N_REPLICATED)[:, pos:pos + size] for r in rep_res)
        pos += size
    conv_wmv = [jnp.concatenate([given["ssm_conv_w"][i][0], given["gdn_conv_w"][i][0]], axis=0) for i in range(3)]
    conv_res = _adam(conv_contrib, conv_wmv[0], conv_wmv[1], conv_wmv[2], "adam_conv_w")
    results["ssm_conv_w"] = tuple(r[None, :CONV_K] for r in conv_res)
    results["gdn_conv_w"] = tuple(r[None, CONV_K:] for r in conv_res)

    loss = lax.psum(loss_loc[0, 0], ("x", "y", "c"))
    return (loss, dx[None]) + tuple(results[nm][i] for i in range(4) for nm in WEIGHTS)
```

```python
import jax
import jax.numpy as jnp
from jax import lax
from jax.experimental import pallas as pl
from jax.experimental.pallas import tpu as pltpu

F32 = jnp.float32
BF16 = jnp.bfloat16
N_DEV = 8
D_MODEL = 1024
EPS = 1e-6
CONV_K = 4
SSM_CHUNK = 128
SSM_HEAD_DIM = 64
SSM_D_STATE = 128
SSM_GROUPS = 8
SSM_HEADS_PER_GROUP = 4
SSM_GROUP_WIDTH = SSM_HEADS_PER_GROUP * SSM_HEAD_DIM
SSM_GROUPS_PER_STEP = 2
GDN_CHUNK = 64
GDN_HEAD = 128
GDN_QK_HEADS = 8
GDN_V_PER_QK = 2
GDN_QK_PER_STEP = 4
GDN_INV_BLOCK = 16
C_ZS, C_XBC, C_QKV, C_ZG, C_GS, C_GG, N_MAIN = 0, 2048, 6144, 10240, 12288, 13312, 14336
N_SMALL = 128
ADAM_LR, ADAM_B1, ADAM_B2, ADAM_EPS, ADAM_WD, ADAM_STEP = 0.001, 0.9, 0.999, 1e-08, 0.01, 10
VMEM_LIMIT = 56 * 1024 * 1024
MM_WHOLE_K = 4096
MM_SPLIT_K = 2048
NEG_INF = float("-inf")

_NT = (((1,), (1,)), ((), ()))
_NN = (((1,), (0,)), ((), ()))
_TN = (((0,), (0,)), ((), ()))


def _params(*sem):
    return pltpu.CompilerParams(dimension_semantics=sem, vmem_limit_bytes=VMEM_LIMIT)


def _dot(a, b, dims=_NN):
    return lax.dot_general(a.astype(BF16), b.astype(BF16), dims, preferred_element_type=F32)


def _split(a):
    hi = a.astype(BF16)
    return hi, (a - hi.astype(F32)).astype(BF16)


def _dot3(a, b, dims=_NN):
    ah, al = _split(a)
    bh, bl = _split(b)
    d = lambda u, v: lax.dot_general(u, v, dims, preferred_element_type=F32)
    return d(ah, bh) + (d(ah, bl) + d(al, bh))


def _sigmoid(x):
    return 1.0 / (1.0 + jnp.exp(-x))


def _silu(x):
    return x * _sigmoid(x)


def _dsilu(x):
    s = _sigmoid(x)
    return s * (1.0 + x * (1.0 - s))


def _softplus(x):
    return jnp.maximum(x, 0.0) + jnp.log1p(jnp.exp(-jnp.abs(x)))


def _iota(n, m, d):
    return lax.broadcasted_iota(jnp.int32, (n, m), d)


def _rowsum(x):
    return jnp.sum(x, axis=1, keepdims=True)


def _colsum(x):
    return jnp.sum(x, axis=0, keepdims=True)


def _total(x):
    return _rowsum(_colsum(x))


MXU_LANES = 128


def _parts(x, n):
    out = []
    for _ in range(n):
        p = x.astype(BF16)
        out.append(p)
        x = x - p.astype(F32)
    return out


def _sum_by(m01, x, dims=_NN, n=3):
    return sum(lax.dot_general(m01, p, dims, preferred_element_type=F32) for p in _parts(x, n))


def _row_col_sums(q):
    ones = jnp.ones((q.shape[0], MXU_LANES), BF16)
    acc = 0.0
    for p in _parts(q, 2):
        acc = acc + (lax.dot_general(p, ones, _NN, preferred_element_type=F32)
                     - lax.dot_general(p, ones, _TN, preferred_element_type=F32))
    return acc[:, 0:1]


def _cumsum_forms(col, ii, jj):
    lower = jnp.where(ii >= jj, 1.0, 0.0).astype(BF16)
    cum_col = _sum_by(lower, jnp.broadcast_to(col, (col.shape[0], MXU_LANES)))[:, 0:1]
    cum_row = _colsum(jnp.where(ii <= jj, col, 0.0))
    return cum_col, cum_row


def _rev_cumsum_col(col, ii, jj):
    upper = jnp.where(ii <= jj, 1.0, 0.0).astype(BF16)
    return _sum_by(upper, jnp.broadcast_to(col, (col.shape[0], MXU_LANES)))[:, 0:1]


def _blk(dim, pref):
    return pref if dim % pref == 0 else dim


def _lockstep(gens):
    gens = list(gens)
    while gens:
        alive = []
        for g in gens:
            try:
                next(g)
                alive.append(g)
            except StopIteration:
                pass
        gens = alive


def _mm(a, b, M, N, K, *, mode, out_dtype, name, a_off=(0, 0), b_off=(0, 0), add=None, epi=None, extra=None,
        tm=1024, tn=1024):
    tm, tn = _blk(M, tm), _blk(N, tn)
    tk = K if K <= MM_WHOLE_K else _blk(K, MM_SPLIT_K)
    nk = K // tk
    if mode == "tn":
        a_spec = pl.BlockSpec((tk, tm), lambda i, j, k: (k + a_off[0] // tk, i + a_off[1] // tm))
        assert a_off[0] % tk == 0 and a_off[1] % tm == 0
    else:
        a_spec = pl.BlockSpec((tm, tk), lambda i, j, k: (i + a_off[0] // tm, k + a_off[1] // tk))
        assert a_off[0] % tm == 0 and a_off[1] % tk == 0
    if mode == "nt":
        b_spec = pl.BlockSpec((tn, tk), lambda i, j, k: (j + b_off[0] // tn, k + b_off[1] // tk))
        assert b_off[0] % tn == 0 and b_off[1] % tk == 0
    else:
        b_spec = pl.BlockSpec((tk, tn), lambda i, j, k: (k + b_off[0] // tk, j + b_off[1] // tn))
        assert b_off[0] % tk == 0 and b_off[1] % tn == 0
    dims = {"nn": _NN, "nt": _NT, "tn": _TN}[mode]
    o_spec = pl.BlockSpec((tm, tn), lambda i, j, k: (i, j))
    ins, in_specs = [a, b], [a_spec, b_spec]
    if add is not None:
        ins.append(add)
        in_specs.append(o_spec)
    if extra is not None:
        ins.append(extra)
        in_specs.append(o_spec)
    n_in = len(ins)
    if epi == "relu2":
        out_shape = (jax.ShapeDtypeStruct((M, N), F32), jax.ShapeDtypeStruct((M, N), BF16))
        out_specs = (o_spec, o_spec)
    else:
        out_shape = jax.ShapeDtypeStruct((M, N), out_dtype)
        out_specs = o_spec

    def body(*refs):
        a_ref, b_ref = refs[0], refs[1]
        outs = refs[n_in:] if nk == 1 else refs[n_in:-1]

        def finish(r):
            pos = 2
            if add is not None:
                r = r + refs[pos][...]
                pos += 1
            if epi == "relu2":
                outs[0][...] = r
                p = jnp.maximum(r, 0.0)
                outs[1][...] = (p * p).astype(BF16)
            elif epi == "drelu2":
                outs[0][...] = (r * (2.0 * jnp.maximum(refs[pos][...], 0.0))).astype(out_dtype)
            else:
                outs[0][...] = r.astype(out_dtype)

        if nk == 1:
            finish(_dot(a_ref[...], b_ref[...], dims))
            return
        acc = refs[-1]
        k = pl.program_id(2)

        @pl.when(k == 0)
        def _():
            acc[...] = jnp.zeros_like(acc)

        acc[...] += _dot(a_ref[...], b_ref[...], dims)

        @pl.when(k == nk - 1)
        def _():
            finish(acc[...])

    return pl.pallas_call(
        body, name=name, grid=(M // tm, N // tn, nk), in_specs=in_specs, out_specs=out_specs, out_shape=out_shape,
        scratch_shapes=[] if nk == 1 else [pltpu.VMEM((tm, tn), F32)],
        compiler_params=_params("parallel", "parallel", "arbitrary"))(*ins)


def _row_spec(tb, d):
    return pl.BlockSpec((tb, d), lambda i: (i, 0))


def _vec_spec(d):
    return pl.BlockSpec((1, d), lambda i: (0, 0))


def _pre_fwd(x, w, sc, sh, name):
    S, Dm = x.shape
    tb = _blk(S, 512)

    def body(x_ref, w_ref, sc_ref, sh_ref, h_ref):
        xv = x_ref[...]
        r = lax.rsqrt(jnp.mean(xv * xv, axis=-1, keepdims=True) + EPS)
        h_ref[...] = ((xv * r * w_ref[...]) * (1.0 + sc_ref[...]) + sh_ref[...]).astype(BF16)

    return pl.pallas_call(
        body, name=name, grid=(S // tb,), in_specs=[_row_spec(tb, Dm)] + [_vec_spec(Dm)] * 3,
        out_specs=_row_spec(tb, Dm), out_shape=jax.ShapeDtypeStruct((S, Dm), BF16),
        compiler_params=_params("parallel"))(x, w, sc, sh)


def _post_fwd(x, y, w, g, name):
    S, Dm = x.shape
    tb = _blk(S, 512)

    def body(x_ref, y_ref, w_ref, g_ref, o_ref):
        yv = y_ref[...]
        r = lax.rsqrt(jnp.mean(yv * yv, axis=-1, keepdims=True) + EPS)
        o_ref[...] = x_ref[...] + g_ref[...] * (yv * r * w_ref[...])

    return pl.pallas_call(
        body, name=name, grid=(S // tb,), in_specs=[_row_spec(tb, Dm)] * 2 + [_vec_spec(Dm)] * 2,
        out_specs=_row_spec(tb, Dm), out_shape=jax.ShapeDtypeStruct((S, Dm), F32),
        compiler_params=_params("parallel"))(x, y, w, g)


def _final_fwd(x, y, w, g, target, name):
    S, Dm = x.shape
    tb = _blk(S, 512)
    nb = S // tb

    def body(x_ref, y_ref, w_ref, g_ref, t_ref, dx_ref, loss_ref, acc):
        i = pl.program_id(0)

        @pl.when(i == 0)
        def _():
            acc[...] = jnp.zeros_like(acc)

        yv = y_ref[...]
        r = lax.rsqrt(jnp.mean(yv * yv, axis=-1, keepdims=True) + EPS)
        e = (x_ref[...] + g_ref[...] * (yv * r * w_ref[...])) - t_ref[...]
        dx_ref[...] = e * (1.0 / Dm)
        acc[...] += _colsum(e * e)

        @pl.when(i == nb - 1)
        def _():
            loss_ref[...] = (0.5 / Dm) * _rowsum(acc[...])

    return pl.pallas_call(
        body, name=name, grid=(nb,), in_specs=[_row_spec(tb, Dm)] * 2 + [_vec_spec(Dm)] * 2 + [_row_spec(tb, Dm)],
        out_specs=(_row_spec(tb, Dm), pl.BlockSpec((1, 1), lambda i: (0, 0))),
        out_shape=(jax.ShapeDtypeStruct((S, Dm), F32), jax.ShapeDtypeStruct((1, 1), F32)),
        scratch_shapes=[pltpu.VMEM((1, Dm), F32)], compiler_params=_params("arbitrary"))(x, y, w, g, target)


def _post_bwd(dxo, y, w, g, name):
    S, Dm = y.shape
    tb = _blk(S, 512)

    def body(d_ref, y_ref, w_ref, g_ref, dy_ref, dg_ref, dw_ref):
        i = pl.program_id(0)

        @pl.when(i == 0)
        def _():
            dg_ref[...] = jnp.zeros_like(dg_ref)
            dw_ref[...] = jnp.zeros_like(dw_ref)

        yv, dv = y_ref[...], d_ref[...]
        r = lax.rsqrt(jnp.mean(yv * yv, axis=-1, keepdims=True) + EPS)
        yh = yv * r
        dg_ref[...] += _colsum(dv * (yh * w_ref[...]))
        dn = dv * g_ref[...]
        dw_ref[...] += _colsum(dn * yh)
        dyh = dn * w_ref[...]
        dy_ref[...] = (r * (dyh - yh * jnp.mean(dyh * yh, axis=-1, keepdims=True))).astype(BF16)

    return pl.pallas_call(
        body, name=name, grid=(S // tb,), in_specs=[_row_spec(tb, Dm)] * 2 + [_vec_spec(Dm)] * 2,
        out_specs=(_row_spec(tb, Dm), _vec_spec(Dm), _vec_spec(Dm)),
        out_shape=(jax.ShapeDtypeStruct((S, Dm), BF16), jax.ShapeDtypeStruct((1, Dm), F32),
                   jax.ShapeDtypeStruct((1, Dm), F32)),
        compiler_params=_params("arbitrary"))(dxo, y, w, g)


def _pre_bwd(dh, x, w, sc, dres, name):
    S, Dm = x.shape
    tb = _blk(S, 512)

    def body(dh_ref, x_ref, w_ref, sc_ref, dr_ref, dx_ref, dsh_ref, dsc_ref, dw_ref):
        i = pl.program_id(0)

        @pl.when(i == 0)
        def _():
            dsh_ref[...] = jnp.zeros_like(dsh_ref)
            dsc_ref[...] = jnp.zeros_like(dsc_ref)
            dw_ref[...] = jnp.zeros_like(dw_ref)

        xv, dv = x_ref[...], dh_ref[...]
        r = lax.rsqrt(jnp.mean(xv * xv, axis=-1, keepdims=True) + EPS)
        xh = xv * r
        one_sc = 1.0 + sc_ref[...]
        dsh_ref[...] += _colsum(dv)
        dsc_ref[...] += _colsum(dv * (xh * w_ref[...]))
        dw_ref[...] += _colsum(dv * one_sc * xh)
        dxh = dv * one_sc * w_ref[...]
        dx_ref[...] = dr_ref[...] + r * (dxh - xh * jnp.mean(dxh * xh, axis=-1, keepdims=True))

    vec = jax.ShapeDtypeStruct((1, Dm), F32)
    return pl.pallas_call(
        body, name=name, grid=(S // tb,),
        in_specs=[_row_spec(tb, Dm)] * 2 + [_vec_spec(Dm)] * 2 + [_row_spec(tb, Dm)],
        out_specs=(_row_spec(tb, Dm), _vec_spec(Dm), _vec_spec(Dm), _vec_spec(Dm)),
        out_shape=(jax.ShapeDtypeStruct((S, Dm), F32), vec, vec, vec),
        compiler_params=_params("arbitrary"))(dh, x, w, sc, dres)


def _merge_fwd(ys, yg, proj, name):
    S, Dm = ys.shape
    tb = _blk(S, 512)

    def body(ys_ref, yg_ref, gs_ref, gg_ref, o_ref):
        o_ref[...] = (_sigmoid(gs_ref[...]) * ys_ref[...] + _sigmoid(gg_ref[...]) * yg_ref[...]).astype(BF16)

    return pl.pallas_call(
        body, name=name, grid=(S // tb,),
        in_specs=[_row_spec(tb, Dm)] * 2 + [pl.BlockSpec((tb, Dm), lambda i: (i, C_GS // Dm)),
                                            pl.BlockSpec((tb, Dm), lambda i: (i, C_GG // Dm))],
        out_specs=_row_spec(tb, Dm), out_shape=jax.ShapeDtypeStruct((S, Dm), BF16),
        compiler_params=_params("parallel"))(ys, yg, proj, proj)


D_PROJ_ANY = pl.BlockSpec(memory_space=pl.ANY)


def _merge_bwd(dm, ys, yg, proj, d_proj, name):
    S, Dm = ys.shape
    tb = _blk(S, 512)

    def body(dm_ref, ys_ref, yg_ref, gs_ref, gg_ref, _, dys_ref, dyg_ref, dgate_ref):
        d = dm_ref[...]
        ss, sg = _sigmoid(gs_ref[...]), _sigmoid(gg_ref[...])
        dys_ref[...] = (d * ss).astype(BF16)
        dyg_ref[...] = (d * sg).astype(BF16)
        dgate_ref[:, :Dm] = (d * ys_ref[...] * ss * (1.0 - ss)).astype(BF16)
        dgate_ref[:, Dm:] = (d * yg_ref[...] * sg * (1.0 - sg)).astype(BF16)

    return pl.pallas_call(
        body, name=name, grid=(S // tb,),
        in_specs=[_row_spec(tb, Dm)] * 3 + [pl.BlockSpec((tb, Dm), lambda i: (i, C_GS // Dm)),
                                            pl.BlockSpec((tb, Dm), lambda i: (i, C_GG // Dm)), D_PROJ_ANY],
        out_specs=(_row_spec(tb, Dm), _row_spec(tb, Dm), pl.BlockSpec((tb, 2 * Dm), lambda i: (i, C_GS // (2 * Dm)))),
        out_shape=(jax.ShapeDtypeStruct((S, Dm), BF16), jax.ShapeDtypeStruct((S, Dm), BF16),
                   jax.ShapeDtypeStruct(d_proj.shape, BF16)),
        input_output_aliases={5: 2}, compiler_params=_params("parallel"))(dm, ys, yg, proj, proj, d_proj)


CONV_COLS = 128


def _shift_down(x, k, rows):
    return jnp.where(rows >= k, pltpu.roll(x, k, 0), 0.0)


def _shift_up(x, k, rows, S):
    return jnp.where(rows < S - k, pltpu.roll(x, S - k, 0), 0.0)


def _conv_fwd(proj, w, b, name):
    S = proj.shape[0]
    n = w.shape[1]
    cb = CONV_COLS

    def body(x_ref, w_ref, b_ref, o_ref):
        x = x_ref[...]
        rows = _iota(S, cb, 0)
        pre = x * w_ref[CONV_K - 1:CONV_K, :] + b_ref[...]
        for k in range(1, CONV_K):
            pre = pre + _shift_down(x, k, rows) * w_ref[CONV_K - 1 - k:CONV_K - k, :]
        o_ref[...] = _silu(pre)

    return pl.pallas_call(
        body, name=name, grid=(n // cb,),
        in_specs=[pl.BlockSpec((S, cb), lambda j: (0, j + C_XBC // cb)), pl.BlockSpec((CONV_K, cb), lambda j: (0, j)),
                  pl.BlockSpec((1, cb), lambda j: (0, j))],
        out_specs=pl.BlockSpec((S, cb), lambda j: (0, j)), out_shape=jax.ShapeDtypeStruct((S, n), F32),
        compiler_params=_params("parallel"))(proj, w, b)


def _conv_bwd(dact, proj, w, b, col0, d_proj, name):
    S, n = dact.shape
    cb = CONV_COLS
    o = col0 // cb

    def body(d_ref, x_ref, w_ref, b_ref, _, dx_ref, dw_ref, db_ref):
        x = x_ref[...]
        rows = _iota(S, cb, 0)
        xs = [x] + [_shift_down(x, k, rows) for k in range(1, CONV_K)]
        pre = xs[0] * w_ref[CONV_K - 1:CONV_K, :] + b_ref[...]
        for k in range(1, CONV_K):
            pre = pre + xs[k] * w_ref[CONV_K - 1 - k:CONV_K - k, :]
        dpre = d_ref[...] * _dsilu(pre)
        db_ref[...] = _colsum(dpre)
        dx = dpre * w_ref[CONV_K - 1:CONV_K, :]
        for k in range(CONV_K):
            dw_ref[CONV_K - 1 - k:CONV_K - k, :] = _colsum(dpre * xs[k])
            if k:
                dx = dx + _shift_up(dpre, k, rows, S) * w_ref[CONV_K - 1 - k:CONV_K - k, :]
        dx_ref[...] = dx.astype(BF16)

    return pl.pallas_call(
        body, name=name, grid=(n // cb,),
        in_specs=[pl.BlockSpec((S, cb), lambda j: (0, j)), pl.BlockSpec((S, cb), lambda j: (0, j + o + C_XBC // cb)),
                  pl.BlockSpec((CONV_K, cb), lambda j: (0, j + o)), pl.BlockSpec((1, cb), lambda j: (0, j + o)),
                  D_PROJ_ANY],
        out_specs=(pl.BlockSpec((S, cb), lambda j: (0, j + o + C_XBC // cb)),
                   pl.BlockSpec((CONV_K, cb), lambda j: (0, j)), pl.BlockSpec((1, cb), lambda j: (0, j))),
        out_shape=(jax.ShapeDtypeStruct(d_proj.shape, BF16), jax.ShapeDtypeStruct((CONV_K, n), F32),
                   jax.ShapeDtypeStruct((1, n), F32)),
        input_output_aliases={4: 0}, compiler_params=_params("parallel"))(dact, proj, w, b, d_proj)


def _ssd_specs(L, order):
    G = SSM_GROUPS_PER_STEP
    W, N = G * SSM_GROUP_WIDTH, G * SSM_D_STATE
    x_spec = pl.BlockSpec((L, W), lambda g, c: (order(c), g))
    b_spec = pl.BlockSpec((L, N), lambda g, c: (order(c), 2048 // N + g))
    c_spec = pl.BlockSpec((L, N), lambda g, c: (order(c), 3072 // N + g))
    z_spec = pl.BlockSpec((L, W), lambda g, c: (order(c), C_ZS // W + g))
    dt_spec = pl.BlockSpec((G, L, SSM_HEADS_PER_GROUP), lambda g, c: (g, order(c), 0))
    p_spec = pl.BlockSpec((G, 3, SSM_HEADS_PER_GROUP), lambda g, c: (g, 0, 0))
    nw_spec = pl.BlockSpec((G, 1, SSM_GROUP_WIDTH), lambda g, c: (g, 0, 0))
    s_spec = pl.BlockSpec((G, 1, SSM_GROUP_WIDTH, SSM_D_STATE), lambda g, c: (g, order(c), 0, 0))
    return x_spec, b_spec, c_spec, z_spec, dt_spec, p_spec, nw_spec, s_spec


class _SsdGroup:
    def __init__(self, L):
        P, H, W = SSM_HEAD_DIM, SSM_HEADS_PER_GROUP, SSM_GROUP_WIDTH
        self.L = L
        self.ii, self.jj = _iota(L, L, 0), _iota(L, L, 1)
        self.lower = jnp.where(self.ii >= self.jj, 1.0, 0.0).astype(BF16)
        self.upper = jnp.where(self.ii <= self.jj, 1.0, 0.0).astype(BF16)
        self.lo = _iota(L, 2 * P, 1) < P
        self.lo_row = _iota(1, 2 * P, 1) < P
        bi, bj = _iota(W, W, 0), _iota(W, W, 1)
        self.block = jnp.where(bi // P == bj // P, 1.0, 0.0).astype(BF16)
        si, sj = _iota(2 * P, W, 0), _iota(2 * P, W, 1)
        self.pick = jnp.where(sj == si * P, 1.0, 0.0).astype(BF16)
        self.ones = jnp.ones((L, 2 * P), BF16)

    def spread(self, v4):
        R = v4.shape[0]
        lo = self.lo if R == self.L else self.lo_row
        b = lambda h: jnp.broadcast_to(v4[:, h:h + 1], (R, 2 * SSM_HEAD_DIM))
        return jnp.concatenate([jnp.where(lo, b(0), b(1)), jnp.where(lo, b(2), b(3))], axis=1)

    def gather4(self, v):
        return jnp.concatenate([v[:, h * SSM_HEAD_DIM:h * SSM_HEAD_DIM + 1] for h in range(SSM_HEADS_PER_GROUP)],
                               axis=1)

    def head_sums(self, z):
        return sum(lax.dot_general(p, self.block, _NN, preferred_element_type=F32) for p in _parts(z, 2))

    def pair_cols(self, full, pair):
        ps = full[:, pair * 128:(pair + 1) * 128]
        sw = pltpu.roll(ps, SSM_HEAD_DIM, 1)
        return jnp.where(self.lo, ps, sw), jnp.where(self.lo, sw, ps)

    def gates(self, dt4_raw, p):
        L = self.L
        dtr = self.spread(dt4_raw + p[0:1, :])
        dt = _softplus(dtr)
        A = self.spread(-jnp.exp(p[1:2, :]))
        acum = _sum_by(self.lower, dt * A)
        yield
        rows = _sum_by(self.pick, acum, _NT)
        yield
        a_last = acum[L - 1:L, :]
        cols = self.pair_cols(acum, 0) + self.pair_cols(acum, 1)
        decay, decay_t = [], []
        for h in range(SSM_HEADS_PER_GROUP):
            seg = cols[h] - rows[h:h + 1, :]
            decay.append(jnp.exp(jnp.where(self.ii >= self.jj, seg, NEG_INF)))
            decay_t.append(jnp.exp(jnp.where(self.jj >= self.ii, -seg, NEG_INF)))
        return dict(dtr=dtr, dt=dt, A=A, D=self.spread(p[2:3, :]), acum=acum, eac=jnp.exp(acum), a_last=a_last,
                    wdec=jnp.exp(a_last - acum), decay=decay, decay_t=decay_t,
                    ea_last=[jnp.exp(rows[h:h + 1, L - 1:L]) for h in range(SSM_HEADS_PER_GROUP)])


def _ssd_fwd(conv, proj, dt_raw, pvec, nw, name):
    S = conv.shape[0]
    L, P, N, H, W, G = SSM_CHUNK, SSM_HEAD_DIM, SSM_D_STATE, SSM_HEADS_PER_GROUP, SSM_GROUP_WIDTH, SSM_GROUPS_PER_STEP
    nc = S // L

    def body(x_ref, b_ref, c_ref, z_ref, dt_ref, p_ref, nw_ref, y_ref, yn_ref, s0_ref, state):
        c = pl.program_id(1)

        @pl.when(c == 0)
        def _():
            state[...] = jnp.zeros_like(state)

        k = _SsdGroup(L)

        def group(gi):
            gsl = slice(gi * W, (gi + 1) * W)
            Bm, Cm = b_ref[:, gi * N:(gi + 1) * N], c_ref[:, gi * N:(gi + 1) * N]
            x = x_ref[:, gsl]
            S0 = state[gsl, :]
            s0_ref[gi, 0] = S0
            CB = _dot(Cm, Bm, _NT)
            y_off = _dot(Cm, S0, _NT)
            t = yield from k.gates(dt_ref[gi], p_ref[gi])
            xdt = x * t["dt"]
            s_new = _dot(xdt * t["wdec"], Bm, _TN)
            y_diag = []
            for pair in range(H // 2):
                xp = xdt[:, pair * 128:(pair + 1) * 128]
                y_diag.append(jnp.where(k.lo, _dot(CB * t["decay"][2 * pair], xp),
                                        _dot(CB * t["decay"][2 * pair + 1], xp)))
            yield
            y = jnp.concatenate(y_diag, axis=1) + y_off * t["eac"]
            for h in range(H):
                hsl = slice(gi * W + h * P, gi * W + (h + 1) * P)
                state[hsl, :] = S0[h * P:(h + 1) * P, :] * t["ea_last"][h] + s_new[h * P:(h + 1) * P, :]
            y_ref[:, gsl] = y
            y2 = (y + t["D"] * x) * _silu(z_ref[:, gsl])
            r = lax.rsqrt(jnp.mean(y2 * y2, axis=-1, keepdims=True) + EPS)
            yn_ref[:, gsl] = (y2 * r * nw_ref[gi]).astype(BF16)

        _lockstep(group(gi) for gi in range(G))

    x_spec, b_spec, c_spec, z_spec, dt_spec, p_spec, nw_spec, s_spec = _ssd_specs(L, lambda c: c)
    y_spec = pl.BlockSpec((L, G * W), lambda g, c: (c, g))
    return pl.pallas_call(
        body, name=name, grid=(SSM_GROUPS // G, nc),
        in_specs=[x_spec, b_spec, c_spec, z_spec, dt_spec, p_spec, nw_spec],
        out_specs=(y_spec, y_spec, s_spec),
        out_shape=(jax.ShapeDtypeStruct((S, SSM_GROUPS * W), F32), jax.ShapeDtypeStruct((S, SSM_GROUPS * W), BF16),
                   jax.ShapeDtypeStruct((SSM_GROUPS, nc, W, N), F32)),
        scratch_shapes=[pltpu.VMEM((G * W, N), F32)],
        compiler_params=_params("parallel", "arbitrary"))(conv, conv, conv, proj, dt_raw, pvec, nw)


def _ssd_bwd(dyn, conv, proj, dt_raw, pvec, nw, y_ssd, states, d_proj, name):
    S = conv.shape[0]
    L, P, N, H, W, G = SSM_CHUNK, SSM_HEAD_DIM, SSM_D_STATE, SSM_HEADS_PER_GROUP, SSM_GROUP_WIDTH, SSM_GROUPS_PER_STEP
    nc = S // L

    def body(dyn_ref, x_ref, b_ref, c_ref, z_ref, dt_ref, p_ref, nw_ref, y_ref, s0_ref, _,
             dx_ref, db_ref, dc_ref, dz_ref, ddt_ref, dp_ref, dnw_ref, dstate):
        c = pl.program_id(1)

        @pl.when(c == 0)
        def _():
            dstate[...] = jnp.zeros_like(dstate)
            dp_ref[...] = jnp.zeros_like(dp_ref)
            dnw_ref[...] = jnp.zeros_like(dnw_ref)

        k = _SsdGroup(L)
        last = (_iota(L, 1, 0) == L - 1)

        def group(gi):
            gsl = slice(gi * W, (gi + 1) * W)
            Bm, Cm = b_ref[:, gi * N:(gi + 1) * N], c_ref[:, gi * N:(gi + 1) * N]
            x, z = x_ref[:, gsl], z_ref[:, gsl]
            S0, dS1 = s0_ref[gi, 0], dstate[gsl, :]
            CB = _dot(Cm, Bm, _NT)
            CBt = _dot(Bm, Cm, _NT)
            y_off_raw = _dot(Cm, S0, _NT)
            dXs_raw = _dot(Bm, dS1, _NT)
            t = yield from k.gates(dt_ref[gi], p_ref[gi])
            y1 = y_ref[:, gsl] + t["D"] * x
            sz = _silu(z)
            y2 = y1 * sz
            r = lax.rsqrt(jnp.mean(y2 * y2, axis=-1, keepdims=True) + EPS)
            y2h = y2 * r
            dyn_v = dyn_ref[:, gsl]
            dnw_ref[gi] += _colsum(dyn_v * y2h)
            dy2h = dyn_v * nw_ref[gi]
            dy2 = r * (dy2h - y2h * jnp.mean(dy2h * y2h, axis=-1, keepdims=True))
            dz_ref[:, gsl] = (dy2 * y1 * _dsilu(z)).astype(BF16)
            dY = dy2 * sz
            X = x * t["dt"]
            dYe = dY * t["eac"]
            dC_s = _dot(dYe, S0)
            dB_s = _dot(X * t["wdec"], dS1)
            dS_c = _dot(dYe, Cm, _TN)
            dXm, Gs, Gts = [], [], []
            for pair in range(H // 2):
                dYp, Xp = dY[:, pair * 128:(pair + 1) * 128], X[:, pair * 128:(pair + 1) * 128]
                dXm.append(jnp.where(k.lo, _dot(CBt * t["decay_t"][2 * pair], dYp),
                                     _dot(CBt * t["decay_t"][2 * pair + 1], dYp)))
                for mask in (k.lo, ~k.lo):
                    Gs.append(_dot(jnp.where(mask, dYp, 0.0), Xp, _NT))
                    Gts.append(_dot(jnp.where(mask, Xp, 0.0), dYp, _NT))
            yield
            dXs = dXs_raw * t["wdec"]
            dX = jnp.concatenate(dXm, axis=1) + dXs
            dCB, dCBt, q_sums = 0.0, 0.0, []
            for h in range(H):
                M, Mt = CB * t["decay"][h], CBt * t["decay_t"][h]
                dCB = dCB + Gs[h] * t["decay"][h]
                dCBt = dCBt + Gts[h] * t["decay_t"][h]
                d = Gs[h] * M - Gts[h] * Mt
                q_sums.append(sum(lax.dot_general(pt, k.ones, _NN, preferred_element_type=F32)
                                  for pt in _parts(d, 2)))
            q_f = jnp.concatenate([jnp.where(k.lo, q_sums[0], q_sums[1]), jnp.where(k.lo, q_sums[2], q_sums[3])],
                                  axis=1)
            x_dxs = k.head_sums(X * dXs)
            tot = [_total(dS1[h * P:(h + 1) * P, :] * S0[h * P:(h + 1) * P, :]) * t["ea_last"][h] for h in range(H)]
            tot_f = k.spread(jnp.concatenate(tot, axis=1))
            d_alast = _colsum(x_dxs) + tot_f
            dacum = q_f + k.head_sums(dY * (y_off_raw * t["eac"])) - x_dxs + jnp.where(last, d_alast, 0.0)
            dx_dt = k.head_sums(dX * x)
            d_skip = _colsum(k.head_sums(dY * x))
            for h in range(H):
                hsl = slice(gi * W + h * P, gi * W + (h + 1) * P)
                dstate[hsl, :] = t["ea_last"][h] * dS1[h * P:(h + 1) * P, :] + dS_c[h * P:(h + 1) * P, :]
            dc_s2 = _dot(dCB, Bm)
            db_s2 = _dot(dCBt, Cm)
            yield
            da = _sum_by(k.upper, dacum)
            yield
            ddt_raw = (da * t["A"] + dx_dt) * _sigmoid(t["dtr"])
            dx_ref[:, gsl] = dX * t["dt"] + t["D"] * dY
            dc_ref[:, gi * N:(gi + 1) * N] = dC_s + dc_s2
            db_ref[:, gi * N:(gi + 1) * N] = dB_s + db_s2
            ddt_ref[gi] = k.gather4(ddt_raw)
            dp_ref[gi] += k.gather4(jnp.concatenate([_colsum(ddt_raw), _colsum(da * t["dt"]) * t["A"], d_skip],
                                                    axis=0))

        _lockstep(group(gi) for gi in range(G))

    rev = lambda c: nc - 1 - c
    x_spec, b_spec, c_spec, z_spec, dt_spec, p_spec, nw_spec, s_spec = _ssd_specs(L, rev)
    y_spec = pl.BlockSpec((L, G * W), lambda g, c: (rev(c), g))
    n_spec = pl.BlockSpec((L, G * N), lambda g, c: (rev(c), g))
    return pl.pallas_call(
        body, name=name, grid=(SSM_GROUPS // G, nc),
        in_specs=[y_spec, x_spec, b_spec, c_spec, z_spec, dt_spec, p_spec, nw_spec, y_spec, s_spec, D_PROJ_ANY],
        out_specs=(y_spec, n_spec, n_spec, z_spec, dt_spec, p_spec, nw_spec),
        out_shape=(jax.ShapeDtypeStruct((S, SSM_GROUPS * W), F32), jax.ShapeDtypeStruct((S, SSM_GROUPS * N), F32),
                   jax.ShapeDtypeStruct((S, SSM_GROUPS * N), F32), jax.ShapeDtypeStruct(d_proj.shape, BF16),
                   jax.ShapeDtypeStruct((SSM_GROUPS, S, H), F32), jax.ShapeDtypeStruct((SSM_GROUPS, 3, H), F32),
                   jax.ShapeDtypeStruct((SSM_GROUPS, 1, W), F32)),
        scratch_shapes=[pltpu.VMEM((G * W, N), F32)], input_output_aliases={10: 3},
        compiler_params=_params("parallel", "arbitrary"))(dyn, conv, conv, conv, proj, dt_raw, pvec, nw, y_ssd, states,
                                                          d_proj)


def _unit_lower_inverse(A, ii, jj):
    eye = (ii == jj).astype(F32)
    same = (ii // GDN_INV_BLOCK) == (jj // GDN_INV_BLOCK)
    Ad = jnp.where(same, A, 0.0)
    Ao = A - Ad
    P2 = _dot3(Ad, Ad)
    yield
    P4, X = _dot3(P2, P2), _dot3(eye - Ad, eye + P2)
    yield
    P8, X = _dot3(P4, P4), _dot3(X, eye + P4)
    yield
    X = _dot3(X, eye + P8)
    yield
    Bm = _dot3(X, Ao)
    yield
    B2 = _dot3(Bm, Bm)
    yield
    Y = _dot3(eye - Bm, eye + B2)
    yield
    T = _dot3(Y, X)
    yield
    return T


def _gdn_specs(L, order):
    G = GDN_QK_PER_STEP
    Hd, W = G * GDN_HEAD, G * GDN_V_PER_QK * GDN_HEAD
    q_spec = pl.BlockSpec((L, Hd), lambda h, c: (order(c), (C_QKV - C_XBC) // Hd + h))
    k_spec = pl.BlockSpec((L, Hd), lambda h, c: (order(c), (C_QKV - C_XBC + 1024) // Hd + h))
    v_spec = pl.BlockSpec((L, W), lambda h, c: (order(c), (C_QKV - C_XBC + 2048) // W + h))
    z_spec = pl.BlockSpec((L, W), lambda h, c: (order(c), C_ZG // W + h))
    ba_spec = pl.BlockSpec((G, L, GDN_V_PER_QK), lambda h, c: (h, order(c), 0))
    p_spec = pl.BlockSpec((G, 2, GDN_V_PER_QK), lambda h, c: (h, 0, 0))
    nw_spec = pl.BlockSpec((1, GDN_HEAD), lambda h, c: (0, 0))
    s_spec = pl.BlockSpec((G, 1, GDN_V_PER_QK * GDN_HEAD, GDN_HEAD), lambda h, c: (h, order(c), 0, 0))
    t_spec = pl.BlockSpec((G * GDN_V_PER_QK, 1, L, L), lambda h, c: (h, order(c), 0, 0))
    return q_spec, k_spec, v_spec, z_spec, ba_spec, p_spec, nw_spec, s_spec, t_spec


def _gdn_gates(qa, ka, b_col, a_col, p, j, ii, jj):
    L = qa.shape[0]
    sp_in = a_col + p[0:1, j:j + 1]
    neg_ea = -jnp.exp(p[1:2, j:j + 1])
    g = neg_ea * _softplus(sp_in)
    gcum, gcum_row = _cumsum_forms(g, ii, jj)
    rq = lax.rsqrt(_rowsum(qa * qa) + EPS)
    rk = lax.rsqrt(_rowsum(ka * ka) + EPS)
    q = qa * rq * (GDN_HEAD ** -0.5)
    k = ka * rk
    beta = _sigmoid(b_col)
    yield
    Dm = jnp.exp(jnp.where(ii >= jj, gcum - gcum_row, NEG_INF))
    eg = jnp.exp(gcum)
    g_last = gcum[L - 1:L, :]
    wdec = jnp.exp(g_last - gcum)
    return dict(rq=rq, rk=rk, q=q, k=k, beta=beta, sp_in=sp_in, neg_ea=neg_ea, g=g, Dm=Dm, kbeta=k * beta, eg=eg,
                g_last=g_last, wdec=wdec, kdec=k * wdec)


def _gdn_fwd(conv, proj, b_raw, a_raw, pvec, nw, name):
    S = conv.shape[0]
    L, Hd, J, G = GDN_CHUNK, GDN_HEAD, GDN_V_PER_QK, GDN_QK_PER_STEP
    W = J * Hd
    nc = S // L

    def body(q_ref, k_ref, v_ref, z_ref, b_ref, a_ref, p_ref, nw_ref, o_ref, on_ref, s0_ref, t_ref, state):
        c = pl.program_id(1)

        @pl.when(c == 0)
        def _():
            state[...] = jnp.zeros_like(state)

        ii, jj = _iota(L, L, 0), _iota(L, L, 1)
        for hq in range(G):
            s0_ref[hq, 0] = state[hq * W:(hq + 1) * W, :]

        def head(hq, j):
            hd = hq * J + j
            hsl, sl = slice(hq * Hd, (hq + 1) * Hd), slice(hd * Hd, (hd + 1) * Hd)
            t = yield from _gdn_gates(q_ref[:, hsl], k_ref[:, hsl], b_ref[hq][:, j:j + 1], a_ref[hq][:, j:j + 1],
                                      p_ref[hq], j, ii, jj)
            KK = _dot(t["kbeta"], t["k"], _NT)
            QK = _dot(t["q"], t["k"], _NT)
            yield
            T = yield from _unit_lower_inverse(jnp.where(ii > jj, KK * t["Dm"], 0.0), ii, jj)
            t_ref[hd, 0] = T
            S0 = state[sl, :]
            U = _dot3(T, v_ref[:, sl] * t["beta"])
            Wm = _dot3(T, t["kbeta"] * t["eg"])
            o_inter = _dot(t["q"] * t["eg"], S0)
            yield
            Vn = U - _dot(Wm, S0)
            yield
            o = o_inter + _dot(QK * t["Dm"], Vn)
            s_new = _dot(t["kdec"], Vn, _TN)
            yield
            state[sl, :] = S0 * jnp.exp(t["g_last"]) + s_new
            o_ref[:, sl] = o
            r = lax.rsqrt(jnp.mean(o * o, axis=-1, keepdims=True) + EPS)
            on_ref[:, sl] = ((o * r * nw_ref[...]) * _silu(z_ref[:, sl])).astype(BF16)

        _lockstep(head(hq, j) for hq in range(G) for j in range(J))

    q_spec, k_spec, v_spec, z_spec, ba_spec, p_spec, nw_spec, s_spec, t_spec = _gdn_specs(L, lambda c: c)
    o_spec = pl.BlockSpec((L, G * W), lambda h, c: (c, h))
    return pl.pallas_call(
        body, name=name, grid=(GDN_QK_HEADS // G, nc),
        in_specs=[q_spec, k_spec, v_spec, z_spec, ba_spec, ba_spec, p_spec, nw_spec],
        out_specs=(o_spec, o_spec, s_spec, t_spec),
        out_shape=(jax.ShapeDtypeStruct((S, GDN_QK_HEADS * W), F32), jax.ShapeDtypeStruct((S, GDN_QK_HEADS * W), BF16),
                   jax.ShapeDtypeStruct((GDN_QK_HEADS, nc, W, Hd), F32),
                   jax.ShapeDtypeStruct((GDN_QK_HEADS * J, nc, L, L), F32)),
        scratch_shapes=[pltpu.VMEM((G * W, Hd), F32)],
        compiler_params=_params("parallel", "arbitrary"))(conv, conv, conv, proj, b_raw, a_raw, pvec, nw)


def _gdn_bwd(don, conv, proj, b_raw, a_raw, pvec, nw, o_pre, states, t_inv, d_proj, name):
    S = conv.shape[0]
    L, Hd, J, G = GDN_CHUNK, GDN_HEAD, GDN_V_PER_QK, GDN_QK_PER_STEP
    W = J * Hd
    nc = S // L

    def body(don_ref, q_ref, k_ref, v_ref, z_ref, b_ref, a_ref, p_ref, nw_ref, o_ref, s0_ref, t_ref, _,
             dq_ref, dk_ref, dv_ref, dz_ref, db_ref, da_ref, dp_ref, dnw_ref, dstate):
        c = pl.program_id(1)

        @pl.when(c == 0)
        def _():
            dstate[...] = jnp.zeros_like(dstate)
            dp_ref[...] = jnp.zeros_like(dp_ref)
            dnw_ref[...] = jnp.zeros_like(dnw_ref)

        ii, jj = _iota(L, L, 0), _iota(L, L, 1)
        last = (_iota(L, 1, 0) == L - 1)
        res = {}

        def head(hq, j):
            hd = hq * J + j
            hsl, sl = slice(hq * Hd, (hq + 1) * Hd), slice(hd * Hd, (hd + 1) * Hd)
            qa, ka = q_ref[:, hsl], k_ref[:, hsl]
            t = yield from _gdn_gates(qa, ka, b_ref[hq][:, j:j + 1], a_ref[hq][:, j:j + 1], p_ref[hq], j, ii, jj)
            q, k, beta, eg, Dm, kbeta, kdec = (t[nm] for nm in ("q", "k", "beta", "eg", "Dm", "kbeta", "kdec"))
            T = t_ref[hd, 0]
            v, z, o = v_ref[:, sl], z_ref[:, sl], o_ref[:, sl]
            S0, dS1 = s0_ref[hq, 0, j * Hd:(j + 1) * Hd, :], dstate[sl, :]
            sz = _silu(z)
            r = lax.rsqrt(jnp.mean(o * o, axis=-1, keepdims=True) + EPS)
            oh = o * r
            d_on = don_ref[:, sl]
            dz_ref[:, sl] = (d_on * (oh * nw_ref[...]) * _dsilu(z)).astype(BF16)
            dn = d_on * sz
            dnw_part = _colsum(dn * oh)
            doh = dn * nw_ref[...]
            dO = r * (doh - oh * jnp.mean(doh * oh, axis=-1, keepdims=True))
            Rw = kbeta * eg
            qe = q * eg
            U = _dot3(T, v * beta)
            Wm = _dot3(T, Rw)
            KK = _dot(kbeta, k, _NT)
            QK = _dot(q, k, _NT)
            o_inter = _dot(qe, S0)
            dq_s = _dot(dO, S0, _NT)
            dS_q = _dot(qe, dO, _TN)
            yield
            Am = jnp.where(ii > jj, KK * Dm, 0.0)
            Pm = QK * Dm
            Vn = U - _dot(Wm, S0)
            dVn_s = _dot(kdec, dS1)
            yield
            dVn = _dot(Pm, dO, _TN) + dVn_s
            dP = _dot(dO, Vn, _NT)
            dKd = _dot(Vn, dS1, _NT)
            yield
            dQK = dP * Dm
            dq = _dot(dQK, k) + dq_s * eg
            dk = _dot(dQK, q, _TN) + dKd * t["wdec"]
            dstate[sl, :] = jnp.exp(t["g_last"]) * dS1 + dS_q - _dot(Wm, dVn, _TN)
            dW = -_dot(dVn, S0, _NT)
            dRu = _dot3(T, dVn, _TN)
            yield
            dRw = _dot3(T, dW, _TN)
            dA_u = _dot(dRu, U, _NT)
            yield
            dA = jnp.where(ii > jj, -(dA_u + _dot(dRw, Wm, _NT)), 0.0)
            yield
            dKK = dA * Dm
            dkbeta = _dot(dKK, k) + dRw * eg
            dk = dk + _dot(dKK, kbeta, _TN)
            yield
            dk = dk + dkbeta * beta
            dbeta = _rowsum(dkbeta * k) + _rowsum(dRu * v)
            dv_ref[:, sl] = dRu * beta
            Q = dA * Am + dP * Pm
            rho = _rowsum(dKd * kdec)
            d_glast = _colsum(rho) + jnp.exp(t["g_last"]) * _total(dS1 * S0)
            q_sums = _row_col_sums(Q)
            rest = _rowsum(dRw * Rw) + _rowsum(dO * o_inter) - rho + jnp.where(last, d_glast, 0.0)
            yield
            dg = _rev_cumsum_col(q_sums + rest, ii, jj)
            yield
            da_raw = dg * t["neg_ea"] * _sigmoid(t["sp_in"])
            res[hq, j] = dict(dq=dq, dk=dk, db=dbeta * beta * (1.0 - beta), da=da_raw, d_bias=_colsum(da_raw),
                              d_alog=_colsum(dg * t["g"]), dnw=dnw_part, rq=t["rq"], rk=t["rk"], k=k, qh=qa * t["rq"])

        _lockstep(head(hq, j) for hq in range(G) for j in range(J))
        for hq in range(G):
            parts = [res[hq, j] for j in range(J)]
            hsl = slice(hq * Hd, (hq + 1) * Hd)
            p0 = parts[0]
            dqh = sum(pt["dq"] for pt in parts) * (GDN_HEAD ** -0.5)
            dkn = sum(pt["dk"] for pt in parts)
            dq_ref[:, hsl] = p0["rq"] * (dqh - p0["qh"] * _rowsum(dqh * p0["qh"]))
            dk_ref[:, hsl] = p0["rk"] * (dkn - p0["k"] * _rowsum(dkn * p0["k"]))
            db_ref[hq] = jnp.concatenate([pt["db"] for pt in parts], axis=1)
            da_ref[hq] = jnp.concatenate([pt["da"] for pt in parts], axis=1)
            dp_ref[hq] += jnp.concatenate([jnp.concatenate([pt["d_bias"] for pt in parts], axis=1),
                                           jnp.concatenate([pt["d_alog"] for pt in parts], axis=1)], axis=0)
            dnw_ref[hq] += sum(pt["dnw"] for pt in parts)

    rev = lambda c: nc - 1 - c
    q_spec, k_spec, v_spec, z_spec, ba_spec, p_spec, nw_spec, s_spec, t_spec = _gdn_specs(L, rev)
    o_spec = pl.BlockSpec((L, G * W), lambda h, c: (rev(c), h))
    h_spec = pl.BlockSpec((L, G * Hd), lambda h, c: (rev(c), h))
    dnw_spec = pl.BlockSpec((G, 1, Hd), lambda h, c: (h, 0, 0))
    return pl.pallas_call(
        body, name=name, grid=(GDN_QK_HEADS // G, nc),
        in_specs=[o_spec, q_spec, k_spec, v_spec, z_spec, ba_spec, ba_spec, p_spec, nw_spec, o_spec, s_spec, t_spec,
                  D_PROJ_ANY],
        out_specs=(h_spec, h_spec, o_spec, z_spec, ba_spec, ba_spec, p_spec, dnw_spec),
        out_shape=(jax.ShapeDtypeStruct((S, GDN_QK_HEADS * Hd), F32), jax.ShapeDtypeStruct((S, GDN_QK_HEADS * Hd), F32),
                   jax.ShapeDtypeStruct((S, GDN_QK_HEADS * W), F32), jax.ShapeDtypeStruct(d_proj.shape, BF16),
                   jax.ShapeDtypeStruct((GDN_QK_HEADS, S, J), F32), jax.ShapeDtypeStruct((GDN_QK_HEADS, S, J), F32),
                   jax.ShapeDtypeStruct((GDN_QK_HEADS, 2, J), F32), jax.ShapeDtypeStruct((GDN_QK_HEADS, 1, Hd), F32)),
        scratch_shapes=[pltpu.VMEM((G * W, Hd), F32)], input_output_aliases={12: 3},
        compiler_params=_params("parallel", "arbitrary"))(don, conv, conv, conv, proj, b_raw, a_raw, pvec, nw, o_pre,
                                                          states, t_inv, d_proj)


def _ada_fwd(c_all, w_loc, b_loc, name):
    n = w_loc.shape[1]

    def body(c_ref, w_ref, b_ref, o_ref):
        o_ref[...] = _dot3(_silu(c_ref[...]), w_ref[...]) + b_ref[...]

    return pl.pallas_call(body, name=name, out_shape=jax.ShapeDtypeStruct((N_DEV, n), F32),
                          compiler_params=pltpu.CompilerParams(vmem_limit_bytes=VMEM_LIMIT))(c_all, w_loc, b_loc)


def _ada_bwd(c_all_t, dmod_cols, name):
    Dm, n = c_all_t.shape[0], dmod_cols.shape[1]

    def body(c_ref, d_ref, o_ref):
        ca = _silu(c_ref[...])
        acc = ca[:, 0:1] * d_ref[0:1, :]
        for i in range(1, N_DEV):
            acc = acc + ca[:, i:i + 1] * d_ref[i:i + 1, :]
        o_ref[...] = acc

    return pl.pallas_call(body, name=name, out_shape=jax.ShapeDtypeStruct((Dm, n), F32),
                          compiler_params=pltpu.CompilerParams(vmem_limit_bytes=VMEM_LIMIT))(c_all_t, dmod_cols)


ADAM_BLOCK_BYTES = 12 * 1024 * 1024


def _adam(contrib, w, m, v, name):
    n, R, C = contrib.shape
    tr = R
    while tr % 16 == 0 and (n + 7) * tr * C * 4 > ADAM_BLOCK_BYTES:
        tr //= 2

    def body(c_ref, w_ref, m_ref, v_ref, g_ref, d_ref, nm_ref, nv_ref):
        g = c_ref[0].astype(F32)
        for i in range(1, n):
            g = g + c_ref[i].astype(F32)
        nm = ADAM_B1 * m_ref[...] + (1.0 - ADAM_B1) * g
        nv = ADAM_B2 * v_ref[...] + (1.0 - ADAM_B2) * (g * g)
        m_hat = nm / (1.0 - ADAM_B1 ** ADAM_STEP)
        v_hat = nv / (1.0 - ADAM_B2 ** ADAM_STEP)
        g_ref[...] = g
        d_ref[...] = -ADAM_LR * (m_hat / (jnp.sqrt(v_hat) + ADAM_EPS) + ADAM_WD * w_ref[...])
        nm_ref[...] = nm
        nv_ref[...] = nv

    spec = pl.BlockSpec((tr, C), lambda i: (i, 0))
    shp = jax.ShapeDtypeStruct((R, C), F32)
    return pl.pallas_call(
        body, name=name, grid=(R // tr,), in_specs=[pl.BlockSpec((n, tr, C), lambda i: (0, i, 0)), spec, spec, spec],
        out_specs=(spec,) * 4, out_shape=(shp,) * 4, compiler_params=_params("parallel"))(contrib, w, m, v)


def _exchange(arrays, modes, name, chips=False):
    n = len(arrays)
    out_shape = tuple(jax.ShapeDtypeStruct((N_DEV,) + a.shape if md == "gather" else a.shape, a.dtype)
                      for a, md in zip(arrays, modes))

    def body(*refs):
        ins, outs = refs[:n], refs[n:2 * n]
        send_sems, recv_sems, loc_sems = refs[2 * n:]
        me, peers = _peer_table(chips)

        def src(k, slot):
            return ins[k] if modes[k] == "gather" else ins[k].at[slot]

        def remote(k, m, to_slot, land_slot):
            return pltpu.make_async_remote_copy(
                src_ref=src(k, to_slot), dst_ref=outs[k].at[land_slot], send_sem=send_sems.at[k, m],
                recv_sem=recv_sems.at[k, m], device_id=peers[m][0], device_id_type=pl.DeviceIdType.MESH)

        local = [pltpu.make_async_copy(src(k, me), outs[k].at[me], loc_sems.at[k]) for k in range(n)]
        for cp in local:
            cp.start()
        sends = [remote(k, m, peers[m][1], me) for m in range(len(peers)) for k in range(n)]
        for cp in sends:
            cp.start()
        for m in range(len(peers)):
            for k in range(n):
                remote(k, m, peers[m][1], peers[m][1]).wait_recv()
        for cp in sends:
            cp.wait_send()
        for cp in local:
            cp.wait()

    any_spec = pl.BlockSpec(memory_space=pl.ANY)
    return pl.pallas_call(
        body, name=name, in_specs=[any_spec] * n, out_specs=(any_spec,) * n, out_shape=out_shape,
        scratch_shapes=[pltpu.SemaphoreType.DMA((n, N_DEV - 1)), pltpu.SemaphoreType.DMA((n, N_DEV - 1)),
                        pltpu.SemaphoreType.DMA((n,))])(*arrays)


def _gather_two_level(arrays, name):
    n = len(arrays)
    out_shape = tuple(jax.ShapeDtypeStruct((N_DEV,) + a.shape, a.dtype) for a in arrays)

    def body(*refs):
        ins, outs = refs[:n], refs[n:2 * n]
        send_sems, recv_sems, loc_sems = refs[2 * n:]
        ix, iy, ic = lax.axis_index("x"), lax.axis_index("y"), lax.axis_index("c")
        lin = lambda px, py, pc: 4 * px + 2 * py + pc
        me, sib = lin(ix, iy, ic), (ix, iy, 1 - ic)
        chips = [(1 - ix, iy), (ix, 1 - iy), (1 - ix, 1 - iy)]

        def copy(k, s, block, to, src=None):
            return pltpu.make_async_remote_copy(
                src_ref=outs[k].at[block] if src is None else src, dst_ref=outs[k].at[block],
                send_sem=send_sems.at[k, s], recv_sem=recv_sems.at[k, s], device_id=to,
                device_id_type=pl.DeviceIdType.MESH)

        local = [pltpu.make_async_copy(ins[k], outs[k].at[me], loc_sems.at[k]) for k in range(n)]
        for cp in local:
            cp.start()
        first = [copy(k, 1 + j, me, (cx, cy, ic), src=ins[k]) for j, (cx, cy) in enumerate(chips) for k in range(n)]
        first += [copy(k, 0, me, sib, src=ins[k]) for k in range(n)]
        for cp in first:
            cp.start()
        passed = []
        for j, (cx, cy) in enumerate(chips):
            for k in range(n):
                copy(k, 1 + j, lin(cx, cy, ic), sib).wait_recv()
                passed.append(copy(k, 4 + j, lin(cx, cy, ic), sib))
                passed[-1].start()
        for k in range(n):
            copy(k, 0, lin(*sib), sib).wait_recv()
            for j, (cx, cy) in enumerate(chips):
                copy(k, 4 + j, lin(cx, cy, 1 - ic), sib).wait_recv()
        for cp in first + passed:
            cp.wait_send()
        for cp in local:
            cp.wait()

    any_spec = pl.BlockSpec(memory_space=pl.ANY)
    return pl.pallas_call(
        body, name=name, in_specs=[any_spec] * n, out_specs=(any_spec,) * n, out_shape=out_shape,
        scratch_shapes=[pltpu.SemaphoreType.DMA((n, N_DEV - 1)), pltpu.SemaphoreType.DMA((n, N_DEV - 1)),
                        pltpu.SemaphoreType.DMA((n,))])(*arrays)


def _peer_table(chips=False):
    ix, iy, ic = lax.axis_index("x"), lax.axis_index("y"), lax.axis_index("c")
    peers = []
    for m in ((2, 4, 6) if chips else range(1, N_DEV)):
        px = 1 - ix if m & 4 else ix
        py = 1 - iy if m & 2 else iy
        pc = 1 - ic if m & 1 else ic
        peers.append(((px, py, pc), 2 * px + py if chips else 4 * px + 2 * py + pc))
    return (2 * ix + iy if chips else 4 * ix + 2 * iy + ic), peers


def _exchange_start(arrays, modes, after, name, chips=False):
    n = len(arrays)
    land_shapes = [(N_DEV,) + a.shape if md == "gather" else a.shape for a, md in zip(arrays, modes)]

    def body(*refs):
        ins, lands = refs[:n], refs[n:2 * n]
        send_sems, recv_sems = refs[2 * n + 1], refs[2 * n + 2]
        token = refs[-1]
        me, peers = _peer_table(chips)

        def src(k, slot):
            return ins[k] if modes[k] == "gather" else ins[k].at[slot]

        for peer, slot in peers:
            for k in range(n):
                pltpu.make_async_remote_copy(
                    src_ref=src(k, slot), dst_ref=lands[k].at[me], send_sem=send_sems, recv_sem=recv_sems,
                    device_id=peer, device_id_type=pl.DeviceIdType.MESH).start()
        token[...] = jnp.zeros_like(token)

    hbm = pl.BlockSpec(memory_space=pltpu.HBM)
    sem = pl.BlockSpec(memory_space=pltpu.SEMAPHORE)
    sem_shape = pltpu.SemaphoreType.DMA(())
    operands = [pltpu.with_memory_space_constraint(a, pltpu.HBM) for a in arrays]
    operands += [pltpu.with_memory_space_constraint(lax.empty(s, a.dtype), pltpu.HBM)
                 for s, a in zip(land_shapes, arrays)]
    out = pl.pallas_call(
        body, name=name,
        out_shape=(sem_shape, sem_shape) + tuple(pltpu.HBM(a.shape, a.dtype) for a in arrays)
        + tuple(pltpu.HBM(s, a.dtype) for s, a in zip(land_shapes, arrays)) + (jax.ShapeDtypeStruct((8, 128), F32),),
        in_specs=[hbm] * (2 * n) + [pl.BlockSpec(memory_space=pl.ANY)],
        out_specs=(sem, sem) + (hbm,) * (2 * n) + (pl.BlockSpec(memory_space=pltpu.VMEM),),
        input_output_aliases={i: 2 + i for i in range(2 * n)},
        compiler_params=pltpu.CompilerParams(has_side_effects=pltpu.SideEffectType.DATAFLOW_SIDE_EFFECTING))(
            *operands, after)
    return out[0], out[1], out[2:2 + n], out[2 + n:2 + 2 * n], out[-1]


def _exchange_wait(started, modes, after, name, chips=False):
    send_sems, recv_sems, sent, lands, _ = started
    n = len(sent)

    def body(*refs):
        ins, zones = refs[:n], refs[n:2 * n]
        send_ref, recv_ref = refs[2 * n], refs[2 * n + 1]
        _, peers = _peer_table(chips)

        def src(k, slot):
            return ins[k] if modes[k] == "gather" else ins[k].at[slot]

        for peer, slot in peers:
            for k in range(n):
                cp = pltpu.make_async_remote_copy(
                    src_ref=src(k, slot), dst_ref=zones[k].at[slot], send_sem=send_ref, recv_sem=recv_ref,
                    device_id=peer, device_id_type=pl.DeviceIdType.MESH)
                cp.wait_send()
                cp.wait_recv()

    hbm = pl.BlockSpec(memory_space=pltpu.HBM)
    sem = pl.BlockSpec(memory_space=pltpu.SEMAPHORE)
    out = pl.pallas_call(
        body, name=name,
        out_shape=tuple(pltpu.HBM(a.shape, a.dtype) for a in sent) + tuple(pltpu.HBM(a.shape, a.dtype) for a in lands),
        in_specs=[hbm] * (2 * n) + [sem, sem, pl.BlockSpec(memory_space=pl.ANY)], out_specs=(hbm,) * (2 * n),
        input_output_aliases={i: i for i in range(2 * n)},
        compiler_params=pltpu.CompilerParams(has_side_effects=pltpu.SideEffectType.DATAFLOW_SIDE_EFFECTING))(
            *sent, *lands, send_sems, recv_sems, after)
    ix, iy, ic = lax.axis_index("x"), lax.axis_index("y"), lax.axis_index("c")
    me = 2 * ix + iy if chips else 4 * ix + 2 * iy + ic
    filled = []
    for k in range(n):
        own = sent[k] if modes[k] == "gather" else lax.dynamic_index_in_dim(sent[k], me, axis=0, keepdims=False)
        filled.append(lax.dynamic_update_index_in_dim(out[n + k], own, me, axis=0))
    return filled


def _swap_sibling(arr, name):
    def body(in_ref, out_ref, send_sem, recv_sem):
        ix, iy, ic = lax.axis_index("x"), lax.axis_index("y"), lax.axis_index("c")
        cp = pltpu.make_async_remote_copy(src_ref=in_ref, dst_ref=out_ref, send_sem=send_sem, recv_sem=recv_sem,
                                          device_id=(ix, iy, 1 - ic), device_id_type=pl.DeviceIdType.MESH)
        cp.start()
        cp.wait()

    any_spec = pl.BlockSpec(memory_space=pl.ANY)
    return pl.pallas_call(body, name=name, in_specs=[any_spec], out_specs=any_spec,
                          out_shape=jax.ShapeDtypeStruct(arr.shape, arr.dtype),
                          scratch_shapes=[pltpu.SemaphoreType.DMA, pltpu.SemaphoreType.DMA])(arr)


def _add_pair(a, b, name):
    n, R, C = a.shape
    tr = _blk(R, 256)

    def body(a_ref, b_ref, o_ref):
        o_ref[...] = (a_ref[...].astype(F32) + b_ref[...].astype(F32)).astype(o_ref.dtype)

    spec = pl.BlockSpec((1, tr, C), lambda i, j: (i, j, 0))
    return pl.pallas_call(body, name=name, grid=(n, R // tr), in_specs=[spec, spec], out_specs=spec,
                          out_shape=jax.ShapeDtypeStruct(a.shape, a.dtype),
                          compiler_params=_params("parallel", "parallel"))(a, b)


W_IN_SPLITS = (0, 2048, 6144, 6176, 10272, 12320, 12336, 12352, 13376, 14400)
N_REPLICATED = 16640
REPLICATED = ("b_ada", "norm_mix_pre", "norm_mix_post", "ssm_conv_b", "ssm_dt_bias", "ssm_A_log", "ssm_D",
              "ssm_norm_w", "gdn_dt_bias", "gdn_A_log", "gdn_norm_w", "norm_mlp_pre", "norm_mlp_post")
WEIGHTS = ("w_ada", "b_ada", "norm_mix_pre", "norm_mix_post", "w_in", "ssm_conv_w", "ssm_conv_b", "ssm_dt_bias",
           "ssm_A_log", "ssm_D", "ssm_norm_w", "gdn_conv_w", "gdn_dt_bias", "gdn_A_log", "gdn_norm_w", "w_ssm_up",
           "w_gdn_up", "w_out", "norm_mlp_pre", "norm_mlp_post", "w_mlp_up", "w_mlp_down")


def _by_cols(t):
    return t.transpose(1, 0, 2).reshape(t.shape[1], N_DEV * t.shape[2])


def _to_col_shards(t):
    R, C8 = t.shape
    return t.reshape(R, N_DEV, C8 // N_DEV).transpose(1, 0, 2)


def _heads_first(t, groups):
    S = t.shape[0]
    return t.reshape(S, groups, t.shape[1] // groups).transpose(1, 0, 2)


def _heads_last(t):
    return t.transpose(1, 0, 2).reshape(t.shape[1], t.shape[0] * t.shape[2])


def kernel(x, c, w_ada, b_ada, norm_mix_pre, norm_mix_post, w_in, ssm_conv_w, ssm_conv_b, ssm_dt_bias, ssm_A_log, ssm_D, ssm_norm_w, gdn_conv_w, gdn_dt_bias, gdn_A_log, gdn_norm_w, w_ssm_up, w_gdn_up, w_out, norm_mlp_pre, norm_mlp_post, w_mlp_up, w_mlp_down, loss_target, m_w_ada, m_b_ada, m_norm_mix_pre, m_norm_mix_post, m_w_in, m_ssm_conv_w, m_ssm_conv_b, m_ssm_dt_bias, m_ssm_A_log, m_ssm_D, m_ssm_norm_w, m_gdn_conv_w, m_gdn_dt_bias, m_gdn_A_log, m_gdn_norm_w, m_w_ssm_up, m_w_gdn_up, m_w_out, m_norm_mlp_pre, m_norm_mlp_post, m_w_mlp_up, m_w_mlp_down, v_w_ada, v_b_ada, v_norm_mix_pre, v_norm_mix_post, v_w_in, v_ssm_conv_w, v_ssm_conv_b, v_ssm_dt_bias, v_ssm_A_log, v_ssm_D, v_ssm_norm_w, v_gdn_conv_w, v_gdn_dt_bias, v_gdn_A_log, v_gdn_norm_w, v_w_ssm_up, v_w_gdn_up, v_w_out, v_norm_mlp_pre, v_norm_mlp_post, v_w_mlp_up, v_w_mlp_down):
    S, Dm = x.shape[1], D_MODEL
    me = 4 * lax.axis_index("x") + 2 * lax.axis_index("y") + lax.axis_index("c")
    x2, tgt = x[0], loss_target[0]
    n_ada = w_ada.shape[2]
    given = dict(
        w_ada=(w_ada, m_w_ada, v_w_ada), b_ada=(b_ada, m_b_ada, v_b_ada),
        norm_mix_pre=(norm_mix_pre, m_norm_mix_pre, v_norm_mix_pre),
        norm_mix_post=(norm_mix_post, m_norm_mix_post, v_norm_mix_post), w_in=(w_in, m_w_in, v_w_in),
        ssm_conv_w=(ssm_conv_w, m_ssm_conv_w, v_ssm_conv_w), ssm_conv_b=(ssm_conv_b, m_ssm_conv_b, v_ssm_conv_b),
        ssm_dt_bias=(ssm_dt_bias, m_ssm_dt_bias, v_ssm_dt_bias), ssm_A_log=(ssm_A_log, m_ssm_A_log, v_ssm_A_log),
        ssm_D=(ssm_D, m_ssm_D, v_ssm_D), ssm_norm_w=(ssm_norm_w, m_ssm_norm_w, v_ssm_norm_w),
        gdn_conv_w=(gdn_conv_w, m_gdn_conv_w, v_gdn_conv_w), gdn_dt_bias=(gdn_dt_bias, m_gdn_dt_bias, v_gdn_dt_bias),
        gdn_A_log=(gdn_A_log, m_gdn_A_log, v_gdn_A_log), gdn_norm_w=(gdn_norm_w, m_gdn_norm_w, v_gdn_norm_w),
        w_ssm_up=(w_ssm_up, m_w_ssm_up, v_w_ssm_up), w_gdn_up=(w_gdn_up, m_w_gdn_up, v_w_gdn_up),
        w_out=(w_out, m_w_out, v_w_out), norm_mlp_pre=(norm_mlp_pre, m_norm_mlp_pre, v_norm_mlp_pre),
        norm_mlp_post=(norm_mlp_post, m_norm_mlp_post, v_norm_mlp_post), w_mlp_up=(w_mlp_up, m_w_mlp_up, v_w_mlp_up),
        w_mlp_down=(w_mlp_down, m_w_mlp_down, v_w_mlp_down))

    (c_all, scw, gcw, g_in) = _gather_two_level([c, ssm_conv_w[0], gdn_conv_w[0], w_in[0].astype(BF16)], "gather_w_in")
    c_all = c_all.reshape(N_DEV, Dm)
    wf = _by_cols(g_in)
    sp = W_IN_SPLITS
    w_main = jnp.concatenate([wf[:, sp[0]:sp[2]], wf[:, sp[3]:sp[5]], wf[:, sp[7]:sp[9]]], axis=1)
    w_small = jnp.concatenate([wf[:, sp[2]:sp[3]], wf[:, sp[5]:sp[7]], jnp.zeros((Dm, N_SMALL - 64), BF16)], axis=1)
    conv_w = jnp.concatenate([_by_cols(scw), _by_cols(gcw)], axis=1)
    conv_b = jnp.concatenate([ssm_conv_b, jnp.zeros_like(ssm_conv_b)], axis=1)

    b_loc = lax.dynamic_slice(b_ada, (0, me * n_ada), (1, n_ada))
    mod_part = _ada_fwd(c_all, w_ada[0], b_loc, "ada_fwd")
    (mod_rows,) = _exchange([mod_part.reshape(N_DEV, 1, n_ada)], ["a2a"], "exchange_mod")
    rest = _exchange_start([w_ssm_up[0].astype(BF16), w_gdn_up[0].astype(BF16), w_out[0].astype(BF16),
                            w_mlp_up[0].astype(BF16), w_mlp_down[0].astype(BF16)], ["gather"] * 5, mod_rows,
                           "gather_rest_start")
    mod = mod_rows.reshape(1, 6 * Dm) + rest[4][0:1, 0:1]
    sh1, sc1, g1, sh2, sc2, g2 = [mod[:, i * Dm:(i + 1) * Dm] for i in range(6)]

    h = _pre_fwd(x2, norm_mix_pre, sc1, sh1, "pre_mix")
    proj = _mm(h, w_main, S, N_MAIN, Dm, mode="nn", out_dtype=F32, name="proj_main")
    small = _mm(h, w_small, S, N_SMALL, Dm, mode="nn", out_dtype=F32, name="proj_small")
    conv = _conv_fwd(proj, conv_w, conv_b, "conv_fwd")
    dt_g, b_g, a_g = _heads_first(small[:, 0:32], 8), _heads_first(small[:, 32:48], 8), _heads_first(small[:, 48:64], 8)
    pv_ssm = jnp.stack([ssm_dt_bias.reshape(8, 4), ssm_A_log.reshape(8, 4), ssm_D.reshape(8, 4)], axis=1)
    nw_ssm = ssm_norm_w.reshape(8, 1, SSM_GROUP_WIDTH)
    pv_gdn = jnp.stack([gdn_dt_bias.reshape(8, 2), gdn_A_log.reshape(8, 2)], axis=1)
    y_ssd, ysn, st_ssm = _ssd_fwd(conv, proj, dt_g, pv_ssm, nw_ssm, "ssd_fwd")
    o_pre, ogn, st_gdn, t_inv = _gdn_fwd(conv, proj, b_g, a_g, pv_gdn, gdn_norm_w, "gdn_fwd")
    g_su, g_gu, g_out, g_mu, g_md = _exchange_wait(rest, ["gather"] * 5, ogn, "gather_rest_wait")
    w_su, w_gu = g_su.reshape(2 * Dm, Dm), g_gu.reshape(2 * Dm, Dm)
    w_o, w_mu, w_md = g_out.reshape(Dm, Dm), _by_cols(g_mu), g_md.reshape(4 * Dm, Dm)
    ys = _mm(ysn, w_su, S, Dm, 2 * Dm, mode="nn", out_dtype=F32, name="ssm_up")
    yg = _mm(ogn, w_gu, S, Dm, 2 * Dm, mode="nn", out_dtype=F32, name="gdn_up")
    merged = _merge_fwd(ys, yg, proj, "merge_fwd")
    mo = _mm(merged, w_o, S, Dm, Dm, mode="nn", out_dtype=F32, name="mix_out")
    x1 = _post_fwd(x2, mo, norm_mix_post, g1, "post_mix")
    h2 = _pre_fwd(x1, norm_mlp_pre, sc2, sh2, "pre_mlp")
    u, act = _mm(h2, w_mu, S, 4 * Dm, Dm, mode="nn", out_dtype=F32, epi="relu2", name="mlp_up")
    y_mlp = _mm(act, w_md, S, Dm, 4 * Dm, mode="nn", out_dtype=F32, name="mlp_down")
    dx2, loss_loc = _final_fwd(x1, y_mlp, norm_mlp_post, g2, tgt, "post_mlp_loss")

    dy, dg2, dw_post2 = _post_bwd(dx2, y_mlp, norm_mlp_post, g2, "post_mlp_bwd")
    du = _mm(dy, w_md, S, 4 * Dm, Dm, mode="nt", out_dtype=BF16, epi="drelu2", extra=u, name="mlp_down_dx")
    gw_md = _mm(act, dy, 4 * Dm, Dm, S, mode="tn", out_dtype=BF16, name="mlp_down_dw")
    dh2 = _mm(du, w_mu, S, Dm, 4 * Dm, mode="nt", out_dtype=F32, name="mlp_up_dx")
    gw_mu = _mm(h2, du, Dm, 4 * Dm, S, mode="tn", out_dtype=BF16, name="mlp_up_dw")
    mlp_x = _exchange_start([_to_col_shards(gw_mu), gw_md.reshape(N_DEV, -1, Dm)], ["a2a"] * 2, gw_md,
                            "grads_mlp_start")
    dx1, dsh2, dsc2, dw_pre2 = _pre_bwd(dh2, x1, norm_mlp_pre, sc2 + mlp_x[4][0:1, 0:1], dx2, "pre_mlp_bwd")
    dmo, dg1, dw_post1 = _post_bwd(dx1, mo, norm_mix_post, g1, "post_mix_bwd")
    dmerged = _mm(dmo, w_o, S, Dm, Dm, mode="nt", out_dtype=F32, name="mix_out_dx")
    gw_o = _mm(merged, dmo, Dm, Dm, S, mode="tn", out_dtype=BF16, name="mix_out_dw")
    dys, dyg, d_proj = _merge_bwd(dmerged, ys, yg, proj, lax.empty((S, N_MAIN), BF16), "merge_bwd")
    dysn = _mm(dys, w_su, S, 2 * Dm, Dm, mode="nt", out_dtype=F32, name="ssm_up_dx")
    gw_su = _mm(ysn, dys, 2 * Dm, Dm, S, mode="tn", out_dtype=BF16, name="ssm_up_dw")
    dogn = _mm(dyg, w_gu, S, 2 * Dm, Dm, mode="nt", out_dtype=F32, name="gdn_up_dx")
    gw_gu = _mm(ogn, dyg, 2 * Dm, Dm, S, mode="tn", out_dtype=BF16, name="gdn_up_dw")
    mix_x = _exchange_start([gw_su.reshape(N_DEV, -1, Dm), gw_gu.reshape(N_DEV, -1, Dm), gw_o.reshape(N_DEV, -1, Dm)],
                            ["a2a"] * 3, gw_gu, "grads_mix_start")
    dxs, dBm, dCm, d_proj, ddt_g, dpv_ssm, dnw_ssm = _ssd_bwd(dysn, conv, proj, dt_g, pv_ssm + mix_x[4][0, 0], nw_ssm,
                                                              y_ssd, st_ssm, d_proj, "ssd_bwd")
    dq, dk, dv, d_proj, db_g, da_g, dpv_gdn, dnw_gdn = _gdn_bwd(dogn, conv, proj, b_g, a_g, pv_gdn, gdn_norm_w, o_pre,
                                                                st_gdn, t_inv, d_proj, "gdn_bwd")
    conv_pieces = []
    for nm, d_act, col0 in (("xs", dxs, 0), ("B", dBm, 2048), ("C", dCm, 3072), ("q", dq, 4096), ("k", dk, 5120),
                            ("v", dv, 6144)):
        d_proj, dw_piece, db_piece = _conv_bwd(d_act, proj, conv_w, conv_b, col0, d_proj, "conv_bwd_" + nm)
        conv_pieces.append((dw_piece, db_piece))
    d_small = jnp.concatenate([_heads_last(ddt_g), _heads_last(db_g), _heads_last(da_g),
                               jnp.zeros((S, N_SMALL - 64), F32)], axis=1).astype(BF16)
    gw_small = _mm(h, d_small, Dm, N_SMALL, S, mode="tn", out_dtype=BF16, name="proj_small_dw")
    main_cols = _mm(h, d_proj, Dm, N_MAIN, S, mode="tn", out_dtype=BF16, name="proj_main_dw")
    gw_in = jnp.concatenate([main_cols[:, 0:C_QKV], gw_small[:, 0:32], main_cols[:, C_QKV:C_GS], gw_small[:, 32:64],
                             main_cols[:, C_GS:N_MAIN]], axis=1)
    by_dest = _to_col_shards(gw_in).reshape(N_DEV // 2, 2, Dm, -1)
    my_c = lax.axis_index("c")
    keep = lax.dynamic_index_in_dim(by_dest, my_c, axis=1, keepdims=False)
    give = lax.dynamic_index_in_dim(by_dest, 1 - my_c, axis=1, keepdims=False)
    chip_sum = _add_pair(keep, _swap_sibling(give, "grads_w_in_pair"), "grads_w_in_pair_sum")
    in_x = _exchange_start([chip_sum], ["a2a"], gw_small, "grads_w_in_start", chips=True)
    dh = _mm(d_small, w_small + in_x[4][0:1, 0:1].astype(BF16), S, Dm, N_SMALL, mode="nt", out_dtype=F32,
             name="proj_small_dx")
    dh = _mm(d_proj, w_main, S, Dm, N_MAIN, mode="nt", out_dtype=F32, add=dh, name="proj_main_dx")
    dx, dsh1, dsc1, dw_pre1 = _pre_bwd(dh, x2, norm_mix_pre, sc1, dx1, "pre_mix_bwd")
    r_mu, r_md = _exchange_wait(mlp_x, ["a2a"] * 2, dx, "grads_mlp_wait")
    r_su, r_gu, r_o = _exchange_wait(mix_x, ["a2a"] * 3, dx, "grads_mix_wait")

    dconv_w = jnp.concatenate([p[0] for p in conv_pieces], axis=1)
    dconv_b = jnp.concatenate([p[1] for p in conv_pieces[:3]], axis=1)
    dmod = jnp.concatenate([dsh1, dsc1, dg1, dsh2, dsc2, dg2], axis=1)
    small_vec = jnp.concatenate(
        [dmod, dw_pre1, dw_post1, dconv_b, dpv_ssm[:, 0].reshape(1, 32), dpv_ssm[:, 1].reshape(1, 32),
         dpv_ssm[:, 2].reshape(1, 32), dnw_ssm.reshape(1, 2048), dpv_gdn[:, 0].reshape(1, 16),
         dpv_gdn[:, 1].reshape(1, 16), jnp.sum(dnw_gdn, axis=0), dw_pre2, dw_post2, dconv_w.reshape(1, -1)], axis=1)
    n_vec = small_vec.shape[1]
    small_vec = jnp.pad(small_vec, ((0, 0), (0, (-n_vec) % 1024))).reshape(-1, 1024)
    (small_all,) = _exchange([small_vec], ["gather"], "gather_small_grads")
    small_all = small_all.reshape(N_DEV, -1)
    dmod_cols = lax.dynamic_slice(small_all, (0, me * n_ada), (N_DEV, n_ada))
    gw_ada = _ada_bwd(c_all.T, dmod_cols, "ada_bwd")
    conv_all = small_all[:, N_REPLICATED:n_vec].reshape(N_DEV, CONV_K, 2 * N_DEV * 512)
    conv_contrib = jnp.concatenate(
        [lax.dynamic_slice(conv_all, (0, 0, me * 512), (N_DEV, CONV_K, 512)),
         lax.dynamic_slice(conv_all, (0, 0, N_DEV * 512 + me * 512), (N_DEV, CONV_K, 512))], axis=1)
    rep_contrib = small_all[:, :N_REPLICATED].reshape(N_DEV, N_REPLICATED // 128, 128)

    results = {}

    def adam_big(nm, contrib):
        w3 = given[nm]
        res = _adam(contrib, w3[0][0], w3[1][0], w3[2][0], "adam_" + nm)
        results[nm] = tuple(r.reshape(w3[0].shape) for r in res)

    adam_big("w_ada", gw_ada[None])
    adam_big("w_ssm_up", r_su)
    adam_big("w_gdn_up", r_gu)
    adam_big("w_out", r_o)
    adam_big("w_mlp_up", r_mu)
    adam_big("w_mlp_down", r_md)
    (r_in,) = _exchange_wait(in_x, ["a2a"], results["w_mlp_down"][0], "grads_w_in_wait", chips=True)
    adam_big("w_in", r_in)
    packed = [jnp.concatenate([given[nm][i] for nm in REPLICATED], axis=1).reshape(N_REPLICATED // 128, 128)
              for i in range(3)]
    rep_res = _adam(rep_contrib, packed[0], packed[1], packed[2], "adam_replicated")
    pos = 0
    for nm in REPLICATED:
        size = given[nm][0].shape[1]
        results[nm] = tuple(r.reshape(1, N_REPLICATED)[:, pos:pos + size] for r in rep_res)
        pos += size
    conv_wmv = [jnp.concatenate([given["ssm_conv_w"][i][0], given["gdn_conv_w"][i][0]], axis=0) for i in range(3)]
    conv_res = _adam(conv_contrib, conv_wmv[0], conv_wmv[1], conv_wmv[2], "adam_conv_w")
    results["ssm_conv_w"] = tuple(r[None, :CONV_K] for r in conv_res)
    results["gdn_conv_w"] = tuple(r[None, CONV_K:] for r in conv_res)

    loss = lax.psum(loss_loc[0, 0], ("x", "y", "c"))
    return (loss, dx[None]) + tuple(results[nm][i] for i in range(4) for nm in WEIGHTS)
```

```python
import jax
import jax.numpy as jnp
from jax import lax
from jax.experimental import pallas as pl
from jax.experimental.pallas import tpu as pltpu

F32 = jnp.float32
BF16 = jnp.bfloat16
N_DEV = 8
D_MODEL = 1024
EPS = 1e-6
CONV_K = 4
SSM_CHUNK = 128
SSM_HEAD_DIM = 64
SSM_D_STATE = 128
SSM_GROUPS = 8
SSM_HEADS_PER_GROUP = 4
SSM_GROUP_WIDTH = SSM_HEADS_PER_GROUP * SSM_HEAD_DIM
SSM_GROUPS_PER_STEP = 2
GDN_CHUNK = 64
GDN_HEAD = 128
GDN_QK_HEADS = 8
GDN_V_PER_QK = 2
GDN_QK_PER_STEP = 4
GDN_INV_BLOCK = 16
C_ZS, C_XBC, C_QKV, C_ZG, C_GS, C_GG, N_MAIN = 0, 2048, 6144, 10240, 12288, 13312, 14336
N_SMALL = 128
ADAM_LR, ADAM_B1, ADAM_B2, ADAM_EPS, ADAM_WD, ADAM_STEP = 0.001, 0.9, 0.999, 1e-08, 0.01, 10
VMEM_LIMIT = 56 * 1024 * 1024
MM_WHOLE_K = 4096
MM_SPLIT_K = 2048
NEG_INF = float("-inf")

_NT = (((1,), (1,)), ((), ()))
_NN = (((1,), (0,)), ((), ()))
_TN = (((0,), (0,)), ((), ()))


def _params(*sem):
    return pltpu.CompilerParams(dimension_semantics=sem, vmem_limit_bytes=VMEM_LIMIT)


def _dot(a, b, dims=_NN):
    return lax.dot_general(a.astype(BF16), b.astype(BF16), dims, preferred_element_type=F32)


def _split(a):
    hi = a.astype(BF16)
    return hi, (a - hi.astype(F32)).astype(BF16)


def _dot3(a, b, dims=_NN):
    ah, al = _split(a)
    bh, bl = _split(b)
    d = lambda u, v: lax.dot_general(u, v, dims, preferred_element_type=F32)
    return d(ah, bh) + (d(ah, bl) + d(al, bh))


def _dot2(a, b, dims=_NN):
    ah, al = _split(a)
    bb = b.astype(BF16)
    d = lambda u: lax.dot_general(u, bb, dims, preferred_element_type=F32)
    return d(ah) + d(al)


def _sigmoid(x):
    return 1.0 / (1.0 + jnp.exp(-x))


def _silu(x):
    return x * _sigmoid(x)


def _dsilu(x):
    s = _sigmoid(x)
    return s * (1.0 + x * (1.0 - s))


def _softplus(x):
    return jnp.maximum(x, 0.0) + jnp.log1p(jnp.exp(-jnp.abs(x)))


def _iota(n, m, d):
    return lax.broadcasted_iota(jnp.int32, (n, m), d)


def _rowsum(x):
    return jnp.sum(x, axis=1, keepdims=True)


def _colsum(x):
    return jnp.sum(x, axis=0, keepdims=True)


def _total(x):
    return _rowsum(_colsum(x))


MXU_LANES = 128


def _parts(x, n):
    out = []
    for _ in range(n):
        p = x.astype(BF16)
        out.append(p)
        x = x - p.astype(F32)
    return out


def _sum_by(m01, x, dims=_NN, n=3):
    return sum(lax.dot_general(m01, p, dims, preferred_element_type=F32) for p in _parts(x, n))


def _row_col_sums(q):
    ones = jnp.ones((q.shape[0], MXU_LANES), BF16)
    acc = 0.0
    for p in _parts(q, 2):
        acc = acc + (lax.dot_general(p, ones, _NN, preferred_element_type=F32)
                     - lax.dot_general(p, ones, _TN, preferred_element_type=F32))
    return acc[:, 0:1]


def _cumsum_forms(col, ii, jj):
    lower = jnp.where(ii >= jj, 1.0, 0.0).astype(BF16)
    cum_col = _sum_by(lower, jnp.broadcast_to(col, (col.shape[0], MXU_LANES)))[:, 0:1]
    cum_row = _colsum(jnp.where(ii <= jj, col, 0.0))
    return cum_col, cum_row


def _rev_cumsum_col(col, ii, jj):
    upper = jnp.where(ii <= jj, 1.0, 0.0).astype(BF16)
    return _sum_by(upper, jnp.broadcast_to(col, (col.shape[0], MXU_LANES)))[:, 0:1]


def _blk(dim, pref):
    return pref if dim % pref == 0 else dim


def _lockstep(gens):
    gens = list(gens)
    while gens:
        alive = []
        for g in gens:
            try:
                next(g)
                alive.append(g)
            except StopIteration:
                pass
        gens = alive


def _mm(a, b, M, N, K, *, mode, out_dtype, name, a_off=(0, 0), b_off=(0, 0), add=None, epi=None, extra=None,
        tm=1024, tn=1024):
    tm, tn = _blk(M, tm), _blk(N, tn)
    tk = K if K <= MM_WHOLE_K else _blk(K, MM_SPLIT_K)
    nk = K // tk
    if mode == "tn":
        a_spec = pl.BlockSpec((tk, tm), lambda i, j, k: (k + a_off[0] // tk, i + a_off[1] // tm))
        assert a_off[0] % tk == 0 and a_off[1] % tm == 0
    else:
        a_spec = pl.BlockSpec((tm, tk), lambda i, j, k: (i + a_off[0] // tm, k + a_off[1] // tk))
        assert a_off[0] % tm == 0 and a_off[1] % tk == 0
    if mode == "nt":
        b_spec = pl.BlockSpec((tn, tk), lambda i, j, k: (j + b_off[0] // tn, k + b_off[1] // tk))
        assert b_off[0] % tn == 0 and b_off[1] % tk == 0
    else:
        b_spec = pl.BlockSpec((tk, tn), lambda i, j, k: (k + b_off[0] // tk, j + b_off[1] // tn))
        assert b_off[0] % tk == 0 and b_off[1] % tn == 0
    dims = {"nn": _NN, "nt": _NT, "tn": _TN}[mode]
    o_spec = pl.BlockSpec((tm, tn), lambda i, j, k: (i, j))
    ins, in_specs = [a, b], [a_spec, b_spec]
    if add is not None:
        ins.append(add)
        in_specs.append(o_spec)
    if extra is not None:
        ins.append(extra)
        in_specs.append(o_spec)
    n_in = len(ins)
    if epi == "relu2":
        out_shape = (jax.ShapeDtypeStruct((M, N), F32), jax.ShapeDtypeStruct((M, N), BF16))
        out_specs = (o_spec, o_spec)
    else:
        out_shape = jax.ShapeDtypeStruct((M, N), out_dtype)
        out_specs = o_spec

    def body(*refs):
        a_ref, b_ref = refs[0], refs[1]
        outs = refs[n_in:] if nk == 1 else refs[n_in:-1]

        def finish(r):
            pos = 2
            if add is not None:
                r = r + refs[pos][...]
                pos += 1
            if epi == "relu2":
                outs[0][...] = r
                p = jnp.maximum(r, 0.0)
                outs[1][...] = (p * p).astype(BF16)
            elif epi == "drelu2":
                outs[0][...] = (r * (2.0 * jnp.maximum(refs[pos][...], 0.0))).astype(out_dtype)
            else:
                outs[0][...] = r.astype(out_dtype)

        if nk == 1:
            finish(_dot(a_ref[...], b_ref[...], dims))
            return
        acc = refs[-1]
        k = pl.program_id(2)

        @pl.when(k == 0)
        def _():
            acc[...] = jnp.zeros_like(acc)

        acc[...] += _dot(a_ref[...], b_ref[...], dims)

        @pl.when(k == nk - 1)
        def _():
            finish(acc[...])

    return pl.pallas_call(
        body, name=name, grid=(M // tm, N // tn, nk), in_specs=in_specs, out_specs=out_specs, out_shape=out_shape,
        scratch_shapes=[] if nk == 1 else [pltpu.VMEM((tm, tn), F32)],
        compiler_params=_params("parallel", "parallel", "arbitrary"))(*ins)


def _mm_rows(a, b, M, N, K, *, mode, name, extras, out_shapes, out_specs, epilogue, aliases=None, tm=512):
    tm = _blk(M, tm)
    a_spec = pl.BlockSpec((tm, K), lambda i: (i, 0))
    b_spec = pl.BlockSpec((K, N) if mode == "nn" else (N, K), lambda i: (0, 0))
    dims = _NN if mode == "nn" else _NT
    n_ex = len(extras)

    def body(a_ref, b_ref, *refs):
        epilogue(_dot(a_ref[...], b_ref[...], dims), refs[:n_ex], refs[n_ex:])

    return pl.pallas_call(
        body, name=name, grid=(M // tm,), in_specs=[a_spec, b_spec] + [sp for _, sp in extras],
        out_specs=tuple(out_specs), out_shape=tuple(out_shapes), input_output_aliases=aliases or {},
        compiler_params=_params("parallel"))(a, b, *[x for x, _ in extras])


def _row_spec(tb, d):
    return pl.BlockSpec((tb, d), lambda i: (i, 0))


def _vec_spec(d):
    return pl.BlockSpec((1, d), lambda i: (0, 0))


def _pre_fwd(x, w, sc, sh, name):
    S, Dm = x.shape
    tb = _blk(S, 512)

    def body(x_ref, w_ref, sc_ref, sh_ref, h_ref):
        xv = x_ref[...]
        r = lax.rsqrt(jnp.mean(xv * xv, axis=-1, keepdims=True) + EPS)
        h_ref[...] = ((xv * r * w_ref[...]) * (1.0 + sc_ref[...]) + sh_ref[...]).astype(BF16)

    return pl.pallas_call(
        body, name=name, grid=(S // tb,), in_specs=[_row_spec(tb, Dm)] + [_vec_spec(Dm)] * 3,
        out_specs=_row_spec(tb, Dm), out_shape=jax.ShapeDtypeStruct((S, Dm), BF16),
        compiler_params=_params("parallel"))(x, w, sc, sh)


def _final_fwd_bwd(x, y, w, g, target, name):
    S, Dm = x.shape
    tb = _blk(S, 512)
    nb = S // tb

    def body(x_ref, y_ref, w_ref, g_ref, t_ref, dx_ref, loss_ref, dy_ref, dg_ref, dw_ref, acc):
        i = pl.program_id(0)

        @pl.when(i == 0)
        def _():
            acc[...] = jnp.zeros_like(acc)
            dg_ref[...] = jnp.zeros_like(dg_ref)
            dw_ref[...] = jnp.zeros_like(dw_ref)

        yv = y_ref[...]
        r = lax.rsqrt(jnp.mean(yv * yv, axis=-1, keepdims=True) + EPS)
        yh = yv * r
        n = yh * w_ref[...]
        e = (x_ref[...] + g_ref[...] * n) - t_ref[...]
        dv = e * (1.0 / Dm)
        dx_ref[...] = dv
        acc[...] += _colsum(e * e)
        dg_ref[...] += _colsum(dv * n)
        dn = dv * g_ref[...]
        dw_ref[...] += _colsum(dn * yh)
        dyh = dn * w_ref[...]
        dy_ref[...] = (r * (dyh - yh * jnp.mean(dyh * yh, axis=-1, keepdims=True))).astype(BF16)

        @pl.when(i == nb - 1)
        def _():
            loss_ref[...] = (0.5 / Dm) * _rowsum(acc[...])

    row, vec = _row_spec(tb, Dm), _vec_spec(Dm)
    vec_shape = jax.ShapeDtypeStruct((1, Dm), F32)
    return pl.pallas_call(
        body, name=name, grid=(nb,), in_specs=[row, row, vec, vec, row],
        out_specs=(row, pl.BlockSpec((1, 1), lambda i: (0, 0)), row, vec, vec),
        out_shape=(jax.ShapeDtypeStruct((S, Dm), F32), jax.ShapeDtypeStruct((1, 1), F32),
                   jax.ShapeDtypeStruct((S, Dm), BF16), vec_shape, vec_shape),
        scratch_shapes=[pltpu.VMEM((1, Dm), F32)], compiler_params=_params("arbitrary"))(x, y, w, g, target)


def _post_bwd(dxo, y, w, g, name):
    S, Dm = y.shape
    tb = _blk(S, 512)

    def body(d_ref, y_ref, w_ref, g_ref, dy_ref, dg_ref, dw_ref):
        i = pl.program_id(0)

        @pl.when(i == 0)
        def _():
            dg_ref[...] = jnp.zeros_like(dg_ref)
            dw_ref[...] = jnp.zeros_like(dw_ref)

        yv, dv = y_ref[...], d_ref[...]
        r = lax.rsqrt(jnp.mean(yv * yv, axis=-1, keepdims=True) + EPS)
        yh = yv * r
        dg_ref[...] += _colsum(dv * (yh * w_ref[...]))
        dn = dv * g_ref[...]
        dw_ref[...] += _colsum(dn * yh)
        dyh = dn * w_ref[...]
        dy_ref[...] = (r * (dyh - yh * jnp.mean(dyh * yh, axis=-1, keepdims=True))).astype(BF16)

    return pl.pallas_call(
        body, name=name, grid=(S // tb,), in_specs=[_row_spec(tb, Dm)] * 2 + [_vec_spec(Dm)] * 2,
        out_specs=(_row_spec(tb, Dm), _vec_spec(Dm), _vec_spec(Dm)),
        out_shape=(jax.ShapeDtypeStruct((S, Dm), BF16), jax.ShapeDtypeStruct((1, Dm), F32),
                   jax.ShapeDtypeStruct((1, Dm), F32)),
        compiler_params=_params("arbitrary"))(dxo, y, w, g)


def _pre_bwd(dh, x, w, sc, dres, name):
    S, Dm = x.shape
    tb = _blk(S, 512)

    def body(dh_ref, x_ref, w_ref, sc_ref, dr_ref, dx_ref, dsh_ref, dsc_ref, dw_ref):
        i = pl.program_id(0)

        @pl.when(i == 0)
        def _():
            dsh_ref[...] = jnp.zeros_like(dsh_ref)
            dsc_ref[...] = jnp.zeros_like(dsc_ref)
            dw_ref[...] = jnp.zeros_like(dw_ref)

        xv, dv = x_ref[...], dh_ref[...]
        r = lax.rsqrt(jnp.mean(xv * xv, axis=-1, keepdims=True) + EPS)
        xh = xv * r
        one_sc = 1.0 + sc_ref[...]
        dsh_ref[...] += _colsum(dv)
        dsc_ref[...] += _colsum(dv * (xh * w_ref[...]))
        dw_ref[...] += _colsum(dv * one_sc * xh)
        dxh = dv * one_sc * w_ref[...]
        dx_ref[...] = dr_ref[...] + r * (dxh - xh * jnp.mean(dxh * xh, axis=-1, keepdims=True))

    vec = jax.ShapeDtypeStruct((1, Dm), F32)
    return pl.pallas_call(
        body, name=name, grid=(S // tb,),
        in_specs=[_row_spec(tb, Dm)] * 2 + [_vec_spec(Dm)] * 2 + [_row_spec(tb, Dm)],
        out_specs=(_row_spec(tb, Dm), _vec_spec(Dm), _vec_spec(Dm), _vec_spec(Dm)),
        out_shape=(jax.ShapeDtypeStruct((S, Dm), F32), vec, vec, vec),
        compiler_params=_params("arbitrary"))(dh, x, w, sc, dres)


D_PROJ_ANY = pl.BlockSpec(memory_space=pl.ANY)


def _gate_specs(tm, Dm):
    return (pl.BlockSpec((tm, Dm), lambda i: (i, C_GS // Dm)), pl.BlockSpec((tm, Dm), lambda i: (i, C_GG // Dm)))


def _gdn_up_merge(ogn, w_gu, ys, proj, name):
    S, K = ogn.shape
    Dm = ys.shape[1]
    tm = _blk(S, 512)
    row = _row_spec(tm, Dm)

    def epilogue(r, ex, out):
        ys_ref, gs_ref, gg_ref = ex
        out[0][...] = r
        out[1][...] = (_sigmoid(gs_ref[...]) * ys_ref[...] + _sigmoid(gg_ref[...]) * r).astype(BF16)

    gs_spec, gg_spec = _gate_specs(tm, Dm)
    return _mm_rows(ogn, w_gu, S, Dm, K, mode="nn", name=name, tm=tm,
                    extras=[(ys, row), (proj, gs_spec), (proj, gg_spec)],
                    out_shapes=[jax.ShapeDtypeStruct((S, Dm), F32), jax.ShapeDtypeStruct((S, Dm), BF16)],
                    out_specs=[row, row], epilogue=epilogue)


def _mix_out_post_pre(merged, w_o, x, w_post, g, w_pre, sc, sh, name):
    S, Dm = x.shape
    tm = _blk(S, 512)
    row, vec = _row_spec(tm, Dm), _vec_spec(Dm)

    def epilogue(r, ex, out):
        x_ref, wpost_ref, g_ref, wpre_ref, sc_ref, sh_ref = ex
        out[0][...] = r
        rr = lax.rsqrt(jnp.mean(r * r, axis=-1, keepdims=True) + EPS)
        x1 = x_ref[...] + g_ref[...] * (r * rr * wpost_ref[...])
        out[1][...] = x1
        r1 = lax.rsqrt(jnp.mean(x1 * x1, axis=-1, keepdims=True) + EPS)
        out[2][...] = ((x1 * r1 * wpre_ref[...]) * (1.0 + sc_ref[...]) + sh_ref[...]).astype(BF16)

    return _mm_rows(merged, w_o, S, Dm, Dm, mode="nn", name=name, tm=tm,
                    extras=[(x, row), (w_post, vec), (g, vec), (w_pre, vec), (sc, vec), (sh, vec)],
                    out_shapes=[jax.ShapeDtypeStruct((S, Dm), F32), jax.ShapeDtypeStruct((S, Dm), F32),
                                jax.ShapeDtypeStruct((S, Dm), BF16)], out_specs=[row, row, row], epilogue=epilogue)


def _mix_out_dx_merge_bwd(dmo, w_o, ys, yg, proj, d_proj, name):
    S, Dm = ys.shape
    tm = _blk(S, 512)
    row = _row_spec(tm, Dm)

    def epilogue(d, ex, out):
        ys_ref, yg_ref, gs_ref, gg_ref, _ = ex
        ss, sg = _sigmoid(gs_ref[...]), _sigmoid(gg_ref[...])
        out[0][...] = (d * ss).astype(BF16)
        out[1][...] = (d * sg).astype(BF16)
        out[2][:, :Dm] = (d * ys_ref[...] * ss * (1.0 - ss)).astype(BF16)
        out[2][:, Dm:] = (d * yg_ref[...] * sg * (1.0 - sg)).astype(BF16)

    gs_spec, gg_spec = _gate_specs(tm, Dm)
    return _mm_rows(dmo, w_o, S, Dm, Dm, mode="nt", name=name, tm=tm,
                    extras=[(ys, row), (yg, row), (proj, gs_spec), (proj, gg_spec), (d_proj, D_PROJ_ANY)],
                    out_shapes=[jax.ShapeDtypeStruct((S, Dm), BF16), jax.ShapeDtypeStruct((S, Dm), BF16),
                                jax.ShapeDtypeStruct(d_proj.shape, BF16)],
                    out_specs=[row, row, pl.BlockSpec((tm, 2 * Dm), lambda i: (i, C_GS // (2 * Dm)))],
                    epilogue=epilogue, aliases={6: 2})


CONV_COLS = 128


def _shift_down(x, k, rows):
    return jnp.where(rows >= k, pltpu.roll(x, k, 0), 0.0)


def _shift_up(x, k, rows, S):
    return jnp.where(rows < S - k, pltpu.roll(x, S - k, 0), 0.0)


def _conv_fwd(proj, w, b, name):
    S = proj.shape[0]
    n = w.shape[1]
    cb = CONV_COLS

    def body(x_ref, w_ref, b_ref, o_ref):
        x = x_ref[...]
        rows = _iota(S, cb, 0)
        pre = x * w_ref[CONV_K - 1:CONV_K, :] + b_ref[...]
        for k in range(1, CONV_K):
            pre = pre + _shift_down(x, k, rows) * w_ref[CONV_K - 1 - k:CONV_K - k, :]
        o_ref[...] = _silu(pre)

    return pl.pallas_call(
        body, name=name, grid=(n // cb,),
        in_specs=[pl.BlockSpec((S, cb), lambda j: (0, j + C_XBC // cb)), pl.BlockSpec((CONV_K, cb), lambda j: (0, j)),
                  pl.BlockSpec((1, cb), lambda j: (0, j))],
        out_specs=pl.BlockSpec((S, cb), lambda j: (0, j)), out_shape=jax.ShapeDtypeStruct((S, n), F32),
        compiler_params=_params("parallel"))(proj, w, b)


def _conv_bwd(dact, proj, w, b, col0, d_proj, name):
    S, n = dact.shape
    cb = CONV_COLS
    o = col0 // cb

    def body(d_ref, x_ref, w_ref, b_ref, _, dx_ref, dw_ref, db_ref):
        x = x_ref[...]
        rows = _iota(S, cb, 0)
        xs = [x] + [_shift_down(x, k, rows) for k in range(1, CONV_K)]
        pre = xs[0] * w_ref[CONV_K - 1:CONV_K, :] + b_ref[...]
        for k in range(1, CONV_K):
            pre = pre + xs[k] * w_ref[CONV_K - 1 - k:CONV_K - k, :]
        dpre = d_ref[...] * _dsilu(pre)
        db_ref[...] = _colsum(dpre)
        dx = dpre * w_ref[CONV_K - 1:CONV_K, :]
        for k in range(CONV_K):
            dw_ref[CONV_K - 1 - k:CONV_K - k, :] = _colsum(dpre * xs[k])
            if k:
                dx = dx + _shift_up(dpre, k, rows, S) * w_ref[CONV_K - 1 - k:CONV_K - k, :]
        dx_ref[...] = dx.astype(BF16)

    return pl.pallas_call(
        body, name=name, grid=(n // cb,),
        in_specs=[pl.BlockSpec((S, cb), lambda j: (0, j)), pl.BlockSpec((S, cb), lambda j: (0, j + o + C_XBC // cb)),
                  pl.BlockSpec((CONV_K, cb), lambda j: (0, j + o)), pl.BlockSpec((1, cb), lambda j: (0, j + o)),
                  D_PROJ_ANY],
        out_specs=(pl.BlockSpec((S, cb), lambda j: (0, j + o + C_XBC // cb)),
                   pl.BlockSpec((CONV_K, cb), lambda j: (0, j)), pl.BlockSpec((1, cb), lambda j: (0, j))),
        out_shape=(jax.ShapeDtypeStruct(d_proj.shape, BF16), jax.ShapeDtypeStruct((CONV_K, n), F32),
                   jax.ShapeDtypeStruct((1, n), F32)),
        input_output_aliases={4: 0}, compiler_params=_params("parallel"))(dact, proj, w, b, d_proj)


def _ssd_specs(L, order):
    G = SSM_GROUPS_PER_STEP
    W, N = G * SSM_GROUP_WIDTH, G * SSM_D_STATE
    x_spec = pl.BlockSpec((L, W), lambda g, c: (order(c), g))
    b_spec = pl.BlockSpec((L, N), lambda g, c: (order(c), 2048 // N + g))
    c_spec = pl.BlockSpec((L, N), lambda g, c: (order(c), 3072 // N + g))
    z_spec = pl.BlockSpec((L, W), lambda g, c: (order(c), C_ZS // W + g))
    dt_spec = pl.BlockSpec((G, L, SSM_HEADS_PER_GROUP), lambda g, c: (g, order(c), 0))
    p_spec = pl.BlockSpec((G, 3, SSM_HEADS_PER_GROUP), lambda g, c: (g, 0, 0))
    nw_spec = pl.BlockSpec((G, 1, SSM_GROUP_WIDTH), lambda g, c: (g, 0, 0))
    s_spec = pl.BlockSpec((G, 1, SSM_GROUP_WIDTH, SSM_D_STATE), lambda g, c: (g, order(c), 0, 0))
    return x_spec, b_spec, c_spec, z_spec, dt_spec, p_spec, nw_spec, s_spec


class _SsdGroup:
    def __init__(self, L):
        P, H, W = SSM_HEAD_DIM, SSM_HEADS_PER_GROUP, SSM_GROUP_WIDTH
        self.L = L
        self.ii, self.jj = _iota(L, L, 0), _iota(L, L, 1)
        self.lower = jnp.where(self.ii >= self.jj, 1.0, 0.0).astype(BF16)
        self.upper = jnp.where(self.ii <= self.jj, 1.0, 0.0).astype(BF16)
        self.lo = _iota(L, 2 * P, 1) < P
        self.lo_row = _iota(1, 2 * P, 1) < P
        bi, bj = _iota(W, W, 0), _iota(W, W, 1)
        self.block = jnp.where(bi // P == bj // P, 1.0, 0.0).astype(BF16)
        si, sj = _iota(2 * P, W, 0), _iota(2 * P, W, 1)
        self.pick = jnp.where(sj == si * P, 1.0, 0.0).astype(BF16)
        self.ones = jnp.ones((L, 2 * P), BF16)

    def spread(self, v4):
        R = v4.shape[0]
        lo = self.lo if R == self.L else self.lo_row
        b = lambda h: jnp.broadcast_to(v4[:, h:h + 1], (R, 2 * SSM_HEAD_DIM))
        return jnp.concatenate([jnp.where(lo, b(0), b(1)), jnp.where(lo, b(2), b(3))], axis=1)

    def gather4(self, v):
        return jnp.concatenate([v[:, h * SSM_HEAD_DIM:h * SSM_HEAD_DIM + 1] for h in range(SSM_HEADS_PER_GROUP)],
                               axis=1)

    def head_sums(self, z):
        return sum(lax.dot_general(p, self.block, _NN, preferred_element_type=F32) for p in _parts(z, 2))

    def pair_cols(self, full, pair):
        ps = full[:, pair * 128:(pair + 1) * 128]
        sw = pltpu.roll(ps, SSM_HEAD_DIM, 1)
        return jnp.where(self.lo, ps, sw), jnp.where(self.lo, sw, ps)

    def gates(self, dt4_raw, p):
        L = self.L
        dtr = self.spread(dt4_raw + p[0:1, :])
        dt = _softplus(dtr)
        A = self.spread(-jnp.exp(p[1:2, :]))
        acum = _sum_by(self.lower, dt * A)
        yield
        rows = _sum_by(self.pick, acum, _NT)
        yield
        a_last = acum[L - 1:L, :]
        cols = self.pair_cols(acum, 0) + self.pair_cols(acum, 1)
        decay, decay_t = [], []
        for h in range(SSM_HEADS_PER_GROUP):
            seg = cols[h] - rows[h:h + 1, :]
            decay.append(jnp.exp(jnp.where(self.ii >= self.jj, seg, NEG_INF)))
            decay_t.append(jnp.exp(jnp.where(self.jj >= self.ii, -seg, NEG_INF)))
        return dict(dtr=dtr, dt=dt, A=A, D=self.spread(p[2:3, :]), acum=acum, eac=jnp.exp(acum), a_last=a_last,
                    wdec=jnp.exp(a_last - acum), decay=decay, decay_t=decay_t,
                    ea_last=[jnp.exp(rows[h:h + 1, L - 1:L]) for h in range(SSM_HEADS_PER_GROUP)])


def _ssd_fwd(conv, proj, dt_raw, pvec, nw, name):
    S = conv.shape[0]
    L, P, N, H, W, G = SSM_CHUNK, SSM_HEAD_DIM, SSM_D_STATE, SSM_HEADS_PER_GROUP, SSM_GROUP_WIDTH, SSM_GROUPS_PER_STEP
    nc = S // L

    def body(x_ref, b_ref, c_ref, z_ref, dt_ref, p_ref, nw_ref, y_ref, yn_ref, s0_ref, state):
        c = pl.program_id(1)

        @pl.when(c == 0)
        def _():
            state[...] = jnp.zeros_like(state)

        k = _SsdGroup(L)

        def group(gi):
            gsl = slice(gi * W, (gi + 1) * W)
            Bm, Cm = b_ref[:, gi * N:(gi + 1) * N], c_ref[:, gi * N:(gi + 1) * N]
            x = x_ref[:, gsl]
            S0 = state[gsl, :]
            s0_ref[gi, 0] = S0
            CB = _dot(Cm, Bm, _NT)
            y_off = _dot(Cm, S0, _NT)
            t = yield from k.gates(dt_ref[gi], p_ref[gi])
            xdt = x * t["dt"]
            s_new = _dot(xdt * t["wdec"], Bm, _TN)
            y_diag = []
            for pair in range(H // 2):
                xp = xdt[:, pair * 128:(pair + 1) * 128]
                y_diag.append(jnp.where(k.lo, _dot(CB * t["decay"][2 * pair], xp),
                                        _dot(CB * t["decay"][2 * pair + 1], xp)))
            yield
            y = jnp.concatenate(y_diag, axis=1) + y_off * t["eac"]
            for h in range(H):
                hsl = slice(gi * W + h * P, gi * W + (h + 1) * P)
                state[hsl, :] = S0[h * P:(h + 1) * P, :] * t["ea_last"][h] + s_new[h * P:(h + 1) * P, :]
            y_ref[:, gsl] = y
            y2 = (y + t["D"] * x) * _silu(z_ref[:, gsl])
            r = lax.rsqrt(jnp.mean(y2 * y2, axis=-1, keepdims=True) + EPS)
            yn_ref[:, gsl] = (y2 * r * nw_ref[gi]).astype(BF16)

        _lockstep(group(gi) for gi in range(G))

    x_spec, b_spec, c_spec, z_spec, dt_spec, p_spec, nw_spec, s_spec = _ssd_specs(L, lambda c: c)
    y_spec = pl.BlockSpec((L, G * W), lambda g, c: (c, g))
    return pl.pallas_call(
        body, name=name, grid=(SSM_GROUPS // G, nc),
        in_specs=[x_spec, b_spec, c_spec, z_spec, dt_spec, p_spec, nw_spec],
        out_specs=(y_spec, y_spec, s_spec),
        out_shape=(jax.ShapeDtypeStruct((S, SSM_GROUPS * W), F32), jax.ShapeDtypeStruct((S, SSM_GROUPS * W), BF16),
                   jax.ShapeDtypeStruct((SSM_GROUPS, nc, W, N), F32)),
        scratch_shapes=[pltpu.VMEM((G * W, N), F32)],
        compiler_params=_params("parallel", "arbitrary"))(conv, conv, conv, proj, dt_raw, pvec, nw)


def _ssd_bwd(dyn, conv, proj, dt_raw, pvec, nw, y_ssd, states, d_proj, name):
    S = conv.shape[0]
    L, P, N, H, W, G = SSM_CHUNK, SSM_HEAD_DIM, SSM_D_STATE, SSM_HEADS_PER_GROUP, SSM_GROUP_WIDTH, SSM_GROUPS_PER_STEP
    nc = S // L

    def body(dyn_ref, x_ref, b_ref, c_ref, z_ref, dt_ref, p_ref, nw_ref, y_ref, s0_ref, _,
             dx_ref, db_ref, dc_ref, dz_ref, ddt_ref, dp_ref, dnw_ref, dstate):
        c = pl.program_id(1)

        @pl.when(c == 0)
        def _():
            dstate[...] = jnp.zeros_like(dstate)
            dp_ref[...] = jnp.zeros_like(dp_ref)
            dnw_ref[...] = jnp.zeros_like(dnw_ref)

        k = _SsdGroup(L)
        last = (_iota(L, 1, 0) == L - 1)

        def group(gi):
            gsl = slice(gi * W, (gi + 1) * W)
            Bm, Cm = b_ref[:, gi * N:(gi + 1) * N], c_ref[:, gi * N:(gi + 1) * N]
            x, z = x_ref[:, gsl], z_ref[:, gsl]
            S0, dS1 = s0_ref[gi, 0], dstate[gsl, :]
            CB = _dot(Cm, Bm, _NT)
            CBt = _dot(Bm, Cm, _NT)
            y_off_raw = _dot(Cm, S0, _NT)
            dXs_raw = _dot(Bm, dS1, _NT)
            t = yield from k.gates(dt_ref[gi], p_ref[gi])
            y1 = y_ref[:, gsl] + t["D"] * x
            sz = _silu(z)
            y2 = y1 * sz
            r = lax.rsqrt(jnp.mean(y2 * y2, axis=-1, keepdims=True) + EPS)
            y2h = y2 * r
            dyn_v = dyn_ref[:, gsl]
            dnw_ref[gi] += _colsum(dyn_v * y2h)
            dy2h = dyn_v * nw_ref[gi]
            dy2 = r * (dy2h - y2h * jnp.mean(dy2h * y2h, axis=-1, keepdims=True))
            dz_ref[:, gsl] = (dy2 * y1 * _dsilu(z)).astype(BF16)
            dY = dy2 * sz
            X = x * t["dt"]
            dYe = dY * t["eac"]
            dC_s = _dot(dYe, S0)
            dB_s = _dot(X * t["wdec"], dS1)
            dS_c = _dot(dYe, Cm, _TN)
            dXm, Gs, Gts = [], [], []
            for pair in range(H // 2):
                dYp, Xp = dY[:, pair * 128:(pair + 1) * 128], X[:, pair * 128:(pair + 1) * 128]
                dXm.append(jnp.where(k.lo, _dot(CBt * t["decay_t"][2 * pair], dYp),
                                     _dot(CBt * t["decay_t"][2 * pair + 1], dYp)))
                for mask in (k.lo, ~k.lo):
                    Gs.append(_dot(jnp.where(mask, dYp, 0.0), Xp, _NT))
                    Gts.append(_dot(jnp.where(mask, Xp, 0.0), dYp, _NT))
            yield
            dXs = dXs_raw * t["wdec"]
            dX = jnp.concatenate(dXm, axis=1) + dXs
            dCB, dCBt, q_sums = 0.0, 0.0, []
            for h in range(H):
                M, Mt = CB * t["decay"][h], CBt * t["decay_t"][h]
                dCB = dCB + Gs[h] * t["decay"][h]
                dCBt = dCBt + Gts[h] * t["decay_t"][h]
                d = Gs[h] * M - Gts[h] * Mt
                q_sums.append(sum(lax.dot_general(pt, k.ones, _NN, preferred_element_type=F32)
                                  for pt in _parts(d, 2)))
            q_f = jnp.concatenate([jnp.where(k.lo, q_sums[0], q_sums[1]), jnp.where(k.lo, q_sums[2], q_sums[3])],
                                  axis=1)
            x_dxs = k.head_sums(X * dXs)
            tot = [_total(dS1[h * P:(h + 1) * P, :] * S0[h * P:(h + 1) * P, :]) * t["ea_last"][h] for h in range(H)]
            tot_f = k.spread(jnp.concatenate(tot, axis=1))
            d_alast = _colsum(x_dxs) + tot_f
            dacum = q_f + k.head_sums(dY * (y_off_raw * t["eac"])) - x_dxs + jnp.where(last, d_alast, 0.0)
            dx_dt = k.head_sums(dX * x)
            d_skip = _colsum(k.head_sums(dY * x))
            for h in range(H):
                hsl = slice(gi * W + h * P, gi * W + (h + 1) * P)
                dstate[hsl, :] = t["ea_last"][h] * dS1[h * P:(h + 1) * P, :] + dS_c[h * P:(h + 1) * P, :]
            dc_s2 = _dot(dCB, Bm)
            db_s2 = _dot(dCBt, Cm)
            yield
            da = _sum_by(k.upper, dacum)
            yield
            ddt_raw = (da * t["A"] + dx_dt) * _sigmoid(t["dtr"])
            dx_ref[:, gsl] = dX * t["dt"] + t["D"] * dY
            dc_ref[:, gi * N:(gi + 1) * N] = dC_s + dc_s2
            db_ref[:, gi * N:(gi + 1) * N] = dB_s + db_s2
            ddt_ref[gi] = k.gather4(ddt_raw)
            dp_ref[gi] += k.gather4(jnp.concatenate([_colsum(ddt_raw), _colsum(da * t["dt"]) * t["A"], d_skip],
                                                    axis=0))

        _lockstep(group(gi) for gi in range(G))

    rev = lambda c: nc - 1 - c
    x_spec, b_spec, c_spec, z_spec, dt_spec, p_spec, nw_spec, s_spec = _ssd_specs(L, rev)
    y_spec = pl.BlockSpec((L, G * W), lambda g, c: (rev(c), g))
    n_spec = pl.BlockSpec((L, G * N), lambda g, c: (rev(c), g))
    return pl.pallas_call(
        body, name=name, grid=(SSM_GROUPS // G, nc),
        in_specs=[y_spec, x_spec, b_spec, c_spec, z_spec, dt_spec, p_spec, nw_spec, y_spec, s_spec, D_PROJ_ANY],
        out_specs=(y_spec, n_spec, n_spec, z_spec, dt_spec, p_spec, nw_spec),
        out_shape=(jax.ShapeDtypeStruct((S, SSM_GROUPS * W), F32), jax.ShapeDtypeStruct((S, SSM_GROUPS * N), F32),
                   jax.ShapeDtypeStruct((S, SSM_GROUPS * N), F32), jax.ShapeDtypeStruct(d_proj.shape, BF16),
                   jax.ShapeDtypeStruct((SSM_GROUPS, S, H), F32), jax.ShapeDtypeStruct((SSM_GROUPS, 3, H), F32),
                   jax.ShapeDtypeStruct((SSM_GROUPS, 1, W), F32)),
        scratch_shapes=[pltpu.VMEM((G * W, N), F32)], input_output_aliases={10: 3},
        compiler_params=_params("parallel", "arbitrary"))(dyn, conv, conv, conv, proj, dt_raw, pvec, nw, y_ssd, states,
                                                          d_proj)


def _unit_lower_inverse(A, ii, jj):
    eye = (ii == jj).astype(F32)
    same = (ii // GDN_INV_BLOCK) == (jj // GDN_INV_BLOCK)
    Ad = jnp.where(same, A, 0.0)
    Ao = A - Ad
    P2 = _dot3(Ad, Ad)
    yield
    P4, X = _dot(P2, P2), _dot3(eye - Ad, eye + P2)
    yield
    P8, X = _dot(P4, P4), X + _dot2(X, P4)
    yield
    X = X + _dot2(X, P8)
    yield
    Bm = _dot3(X, Ao)
    yield
    B2 = _dot3(Bm, Bm)
    yield
    Y = (eye - Bm) + B2 - _dot2(Bm, B2)
    yield
    T = _dot3(Y, X)
    yield
    return T


def _gdn_specs(L, order):
    G = GDN_QK_PER_STEP
    Hd, W = G * GDN_HEAD, G * GDN_V_PER_QK * GDN_HEAD
    q_spec = pl.BlockSpec((L, Hd), lambda h, c: (order(c), (C_QKV - C_XBC) // Hd + h))
    k_spec = pl.BlockSpec((L, Hd), lambda h, c: (order(c), (C_QKV - C_XBC + 1024) // Hd + h))
    v_spec = pl.BlockSpec((L, W), lambda h, c: (order(c), (C_QKV - C_XBC + 2048) // W + h))
    z_spec = pl.BlockSpec((L, W), lambda h, c: (order(c), C_ZG // W + h))
    ba_spec = pl.BlockSpec((G, L, GDN_V_PER_QK), lambda h, c: (h, order(c), 0))
    p_spec = pl.BlockSpec((G, 2, GDN_V_PER_QK), lambda h, c: (h, 0, 0))
    nw_spec = pl.BlockSpec((1, GDN_HEAD), lambda h, c: (0, 0))
    s_spec = pl.BlockSpec((G, 1, GDN_V_PER_QK * GDN_HEAD, GDN_HEAD), lambda h, c: (h, order(c), 0, 0))
    t_spec = pl.BlockSpec((G * GDN_V_PER_QK, 1, L, L), lambda h, c: (h, order(c), 0, 0))
    return q_spec, k_spec, v_spec, z_spec, ba_spec, p_spec, nw_spec, s_spec, t_spec


def _gdn_gates(qa, ka, b_col, a_col, p, j, ii, jj):
    L = qa.shape[0]
    sp_in = a_col + p[0:1, j:j + 1]
    neg_ea = -jnp.exp(p[1:2, j:j + 1])
    g = neg_ea * _softplus(sp_in)
    gcum, gcum_row = _cumsum_forms(g, ii, jj)
    rq = lax.rsqrt(_rowsum(qa * qa) + EPS)
    rk = lax.rsqrt(_rowsum(ka * ka) + EPS)
    q = qa * rq * (GDN_HEAD ** -0.5)
    k = ka * rk
    beta = _sigmoid(b_col)
    yield
    Dm = jnp.exp(jnp.where(ii >= jj, gcum - gcum_row, NEG_INF))
    eg = jnp.exp(gcum)
    g_last = gcum[L - 1:L, :]
    wdec = jnp.exp(g_last - gcum)
    return dict(rq=rq, rk=rk, q=q, k=k, beta=beta, sp_in=sp_in, neg_ea=neg_ea, g=g, Dm=Dm, kbeta=k * beta, eg=eg,
                g_last=g_last, wdec=wdec, kdec=k * wdec)


def _gdn_fwd(conv, proj, b_raw, a_raw, pvec, nw, name):
    S = conv.shape[0]
    L, Hd, J, G = GDN_CHUNK, GDN_HEAD, GDN_V_PER_QK, GDN_QK_PER_STEP
    W = J * Hd
    nc = S // L

    def body(q_ref, k_ref, v_ref, z_ref, b_ref, a_ref, p_ref, nw_ref, o_ref, on_ref, s0_ref, t_ref, state):
        c = pl.program_id(1)

        @pl.when(c == 0)
        def _():
            state[...] = jnp.zeros_like(state)

        ii, jj = _iota(L, L, 0), _iota(L, L, 1)
        for hq in range(G):
            s0_ref[hq, 0] = state[hq * W:(hq + 1) * W, :]

        def head(hq, j):
            hd = hq * J + j
            hsl, sl = slice(hq * Hd, (hq + 1) * Hd), slice(hd * Hd, (hd + 1) * Hd)
            t = yield from _gdn_gates(q_ref[:, hsl], k_ref[:, hsl], b_ref[hq][:, j:j + 1], a_ref[hq][:, j:j + 1],
                                      p_ref[hq], j, ii, jj)
            KK = _dot(t["kbeta"], t["k"], _NT)
            QK = _dot(t["q"], t["k"], _NT)
            yield
            T = yield from _unit_lower_inverse(jnp.where(ii > jj, KK * t["Dm"], 0.0), ii, jj)
            t_ref[hd, 0] = T
            S0 = state[sl, :]
            U = _dot2(T, v_ref[:, sl] * t["beta"])
            Wm = _dot2(T, t["kbeta"] * t["eg"])
            o_inter = _dot(t["q"] * t["eg"], S0)
            yield
            Vn = U - _dot(Wm, S0)
            yield
            o = o_inter + _dot(QK * t["Dm"], Vn)
            s_new = _dot(t["kdec"], Vn, _TN)
            yield
            state[sl, :] = S0 * jnp.exp(t["g_last"]) + s_new
            o_ref[:, sl] = o
            r = lax.rsqrt(jnp.mean(o * o, axis=-1, keepdims=True) + EPS)
            on_ref[:, sl] = ((o * r * nw_ref[...]) * _silu(z_ref[:, sl])).astype(BF16)

        _lockstep(head(hq, j) for hq in range(G) for j in range(J))

    q_spec, k_spec, v_spec, z_spec, ba_spec, p_spec, nw_spec, s_spec, t_spec = _gdn_specs(L, lambda c: c)
    o_spec = pl.BlockSpec((L, G * W), lambda h, c: (c, h))
    return pl.pallas_call(
        body, name=name, grid=(GDN_QK_HEADS // G, nc),
        in_specs=[q_spec, k_spec, v_spec, z_spec, ba_spec, ba_spec, p_spec, nw_spec],
        out_specs=(o_spec, o_spec, s_spec, t_spec),
        out_shape=(jax.ShapeDtypeStruct((S, GDN_QK_HEADS * W), F32), jax.ShapeDtypeStruct((S, GDN_QK_HEADS * W), BF16),
                   jax.ShapeDtypeStruct((GDN_QK_HEADS, nc, W, Hd), F32),
                   jax.ShapeDtypeStruct((GDN_QK_HEADS * J, nc, L, L), F32)),
        scratch_shapes=[pltpu.VMEM((G * W, Hd), F32)],
        compiler_params=_params("parallel", "arbitrary"))(conv, conv, conv, proj, b_raw, a_raw, pvec, nw)


def _gdn_bwd(don, conv, proj, b_raw, a_raw, pvec, nw, o_pre, states, t_inv, d_proj, name):
    S = conv.shape[0]
    L, Hd, J, G = GDN_CHUNK, GDN_HEAD, GDN_V_PER_QK, GDN_QK_PER_STEP
    W = J * Hd
    nc = S // L

    def body(don_ref, q_ref, k_ref, v_ref, z_ref, b_ref, a_ref, p_ref, nw_ref, o_ref, s0_ref, t_ref, _,
             dq_ref, dk_ref, dv_ref, dz_ref, db_ref, da_ref, dp_ref, dnw_ref, dstate):
        c = pl.program_id(1)

        @pl.when(c == 0)
        def _():
            dstate[...] = jnp.zeros_like(dstate)
            dp_ref[...] = jnp.zeros_like(dp_ref)
            dnw_ref[...] = jnp.zeros_like(dnw_ref)

        ii, jj = _iota(L, L, 0), _iota(L, L, 1)
        last = (_iota(L, 1, 0) == L - 1)
        res = {}

        def head(hq, j):
            hd = hq * J + j
            hsl, sl = slice(hq * Hd, (hq + 1) * Hd), slice(hd * Hd, (hd + 1) * Hd)
            qa, ka = q_ref[:, hsl], k_ref[:, hsl]
            t = yield from _gdn_gates(qa, ka, b_ref[hq][:, j:j + 1], a_ref[hq][:, j:j + 1], p_ref[hq], j, ii, jj)
            q, k, beta, eg, Dm, kbeta, kdec = (t[nm] for nm in ("q", "k", "beta", "eg", "Dm", "kbeta", "kdec"))
            T = t_ref[hd, 0]
            v, z, o = v_ref[:, sl], z_ref[:, sl], o_ref[:, sl]
            S0, dS1 = s0_ref[hq, 0, j * Hd:(j + 1) * Hd, :], dstate[sl, :]
            sz = _silu(z)
            r = lax.rsqrt(jnp.mean(o * o, axis=-1, keepdims=True) + EPS)
            oh = o * r
            d_on = don_ref[:, sl]
            dz_ref[:, sl] = (d_on * (oh * nw_ref[...]) * _dsilu(z)).astype(BF16)
            dn = d_on * sz
            dnw_part = _colsum(dn * oh)
            doh = dn * nw_ref[...]
            dO = r * (doh - oh * jnp.mean(doh * oh, axis=-1, keepdims=True))
            Rw = kbeta * eg
            qe = q * eg
            U = _dot2(T, v * beta)
            Wm = _dot2(T, Rw)
            KK = _dot(kbeta, k, _NT)
            QK = _dot(q, k, _NT)
            o_inter = _dot(qe, S0)
            dq_s = _dot(dO, S0, _NT)
            dS_q = _dot(qe, dO, _TN)
            yield
            Am = jnp.where(ii > jj, KK * Dm, 0.0)
            Pm = QK * Dm
            Vn = U - _dot(Wm, S0)
            dVn_s = _dot(kdec, dS1)
            yield
            dVn = _dot(Pm, dO, _TN) + dVn_s
            dP = _dot(dO, Vn, _NT)
            dKd = _dot(Vn, dS1, _NT)
            yield
            dQK = dP * Dm
            dq = _dot(dQK, k) + dq_s * eg
            dk = _dot(dQK, q, _TN) + dKd * t["wdec"]
            dstate[sl, :] = jnp.exp(t["g_last"]) * dS1 + dS_q - _dot(Wm, dVn, _TN)
            dW = -_dot(dVn, S0, _NT)
            dRu = _dot2(T, dVn, _TN)
            yield
            dRw = _dot2(T, dW, _TN)
            dA_u = _dot(dRu, U, _NT)
            yield
            dA = jnp.where(ii > jj, -(dA_u + _dot(dRw, Wm, _NT)), 0.0)
            yield
            dKK = dA * Dm
            dkbeta = _dot(dKK, k) + dRw * eg
            dk = dk + _dot(dKK, kbeta, _TN)
            yield
            dk = dk + dkbeta * beta
            dbeta = _rowsum(dkbeta * k) + _rowsum(dRu * v)
            dv_ref[:, sl] = dRu * beta
            Q = dA * Am + dP * Pm
            rho = _rowsum(dKd * kdec)
            d_glast = _colsum(rho) + jnp.exp(t["g_last"]) * _total(dS1 * S0)
            q_sums = _row_col_sums(Q)
            rest = _rowsum(dRw * Rw) + _rowsum(dO * o_inter) - rho + jnp.where(last, d_glast, 0.0)
            yield
            dg = _rev_cumsum_col(q_sums + rest, ii, jj)
            yield
            da_raw = dg * t["neg_ea"] * _sigmoid(t["sp_in"])
            res[hq, j] = dict(dq=dq, dk=dk, db=dbeta * beta * (1.0 - beta), da=da_raw, d_bias=_colsum(da_raw),
                              d_alog=_colsum(dg * t["g"]), dnw=dnw_part, rq=t["rq"], rk=t["rk"], k=k, qh=qa * t["rq"])

        _lockstep(head(hq, j) for hq in range(G) for j in range(J))
        for hq in range(G):
            parts = [res[hq, j] for j in range(J)]
            hsl = slice(hq * Hd, (hq + 1) * Hd)
            p0 = parts[0]
            dqh = sum(pt["dq"] for pt in parts) * (GDN_HEAD ** -0.5)
            dkn = sum(pt["dk"] for pt in parts)
            dq_ref[:, hsl] = p0["rq"] * (dqh - p0["qh"] * _rowsum(dqh * p0["qh"]))
            dk_ref[:, hsl] = p0["rk"] * (dkn - p0["k"] * _rowsum(dkn * p0["k"]))
            db_ref[hq] = jnp.concatenate([pt["db"] for pt in parts], axis=1)
            da_ref[hq] = jnp.concatenate([pt["da"] for pt in parts], axis=1)
            dp_ref[hq] += jnp.concatenate([jnp.concatenate([pt["d_bias"] for pt in parts], axis=1),
                                           jnp.concatenate([pt["d_alog"] for pt in parts], axis=1)], axis=0)
            dnw_ref[hq] += sum(pt["dnw"] for pt in parts)

    rev = lambda c: nc - 1 - c
    q_spec, k_spec, v_spec, z_spec, ba_spec, p_spec, nw_spec, s_spec, t_spec = _gdn_specs(L, rev)
    o_spec = pl.BlockSpec((L, G * W), lambda h, c: (rev(c), h))
    h_spec = pl.BlockSpec((L, G * Hd), lambda h, c: (rev(c), h))
    dnw_spec = pl.BlockSpec((G, 1, Hd), lambda h, c: (h, 0, 0))
    return pl.pallas_call(
        body, name=name, grid=(GDN_QK_HEADS // G, nc),
        in_specs=[o_spec, q_spec, k_spec, v_spec, z_spec, ba_spec, ba_spec, p_spec, nw_spec, o_spec, s_spec, t_spec,
                  D_PROJ_ANY],
        out_specs=(h_spec, h_spec, o_spec, z_spec, ba_spec, ba_spec, p_spec, dnw_spec),
        out_shape=(jax.ShapeDtypeStruct((S, GDN_QK_HEADS * Hd), F32), jax.ShapeDtypeStruct((S, GDN_QK_HEADS * Hd), F32),
                   jax.ShapeDtypeStruct((S, GDN_QK_HEADS * W), F32), jax.ShapeDtypeStruct(d_proj.shape, BF16),
                   jax.ShapeDtypeStruct((GDN_QK_HEADS, S, J), F32), jax.ShapeDtypeStruct((GDN_QK_HEADS, S, J), F32),
                   jax.ShapeDtypeStruct((GDN_QK_HEADS, 2, J), F32), jax.ShapeDtypeStruct((GDN_QK_HEADS, 1, Hd), F32)),
        scratch_shapes=[pltpu.VMEM((G * W, Hd), F32)], input_output_aliases={12: 3},
        compiler_params=_params("parallel", "arbitrary"))(don, conv, conv, conv, proj, b_raw, a_raw, pvec, nw, o_pre,
                                                          states, t_inv, d_proj)


def _ada_fwd(c_all, w_loc, b_loc, name):
    n = w_loc.shape[1]

    def body(c_ref, w_ref, b_ref, o_ref):
        o_ref[...] = _dot3(_silu(c_ref[...]), w_ref[...]) + b_ref[...]

    return pl.pallas_call(body, name=name, out_shape=jax.ShapeDtypeStruct((N_DEV, n), F32),
                          compiler_params=pltpu.CompilerParams(vmem_limit_bytes=VMEM_LIMIT))(c_all, w_loc, b_loc)


def _ada_bwd(c_all_t, dmod_cols, name):
    Dm, n = c_all_t.shape[0], dmod_cols.shape[1]

    def body(c_ref, d_ref, o_ref):
        ca = _silu(c_ref[...])
        acc = ca[:, 0:1] * d_ref[0:1, :]
        for i in range(1, N_DEV):
            acc = acc + ca[:, i:i + 1] * d_ref[i:i + 1, :]
        o_ref[...] = acc

    return pl.pallas_call(body, name=name, out_shape=jax.ShapeDtypeStruct((Dm, n), F32),
                          compiler_params=pltpu.CompilerParams(vmem_limit_bytes=VMEM_LIMIT))(c_all_t, dmod_cols)


ADAM_BLOCK_BYTES = 12 * 1024 * 1024


def _adam(contrib, w, m, v, name):
    n, R, C = contrib.shape
    tr = R
    while tr % 16 == 0 and (n + 7) * tr * C * 4 > ADAM_BLOCK_BYTES:
        tr //= 2

    def body(c_ref, w_ref, m_ref, v_ref, g_ref, d_ref, nm_ref, nv_ref):
        g = c_ref[0].astype(F32)
        for i in range(1, n):
            g = g + c_ref[i].astype(F32)
        nm = ADAM_B1 * m_ref[...] + (1.0 - ADAM_B1) * g
        nv = ADAM_B2 * v_ref[...] + (1.0 - ADAM_B2) * (g * g)
        m_hat = nm / (1.0 - ADAM_B1 ** ADAM_STEP)
        v_hat = nv / (1.0 - ADAM_B2 ** ADAM_STEP)
        g_ref[...] = g
        d_ref[...] = -ADAM_LR * (m_hat / (jnp.sqrt(v_hat) + ADAM_EPS) + ADAM_WD * w_ref[...])
        nm_ref[...] = nm
        nv_ref[...] = nv

    spec = pl.BlockSpec((tr, C), lambda i: (i, 0))
    shp = jax.ShapeDtypeStruct((R, C), F32)
    return pl.pallas_call(
        body, name=name, grid=(R // tr,), in_specs=[pl.BlockSpec((n, tr, C), lambda i: (0, i, 0)), spec, spec, spec],
        out_specs=(spec,) * 4, out_shape=(shp,) * 4, compiler_params=_params("parallel"))(contrib, w, m, v)


def _exchange(arrays, modes, name, chips=False):
    n = len(arrays)
    out_shape = tuple(jax.ShapeDtypeStruct((N_DEV,) + a.shape if md == "gather" else a.shape, a.dtype)
                      for a, md in zip(arrays, modes))

    def body(*refs):
        ins, outs = refs[:n], refs[n:2 * n]
        send_sems, recv_sems, loc_sems = refs[2 * n:]
        me, peers = _peer_table(chips)

        def src(k, slot):
            return ins[k] if modes[k] == "gather" else ins[k].at[slot]

        def remote(k, m, to_slot, land_slot):
            return pltpu.make_async_remote_copy(
                src_ref=src(k, to_slot), dst_ref=outs[k].at[land_slot], send_sem=send_sems.at[k, m],
                recv_sem=recv_sems.at[k, m], device_id=peers[m][0], device_id_type=pl.DeviceIdType.MESH)

        local = [pltpu.make_async_copy(src(k, me), outs[k].at[me], loc_sems.at[k]) for k in range(n)]
        for cp in local:
            cp.start()
        sends = [remote(k, m, peers[m][1], me) for m in range(len(peers)) for k in range(n)]
        for cp in sends:
            cp.start()
        for m in range(len(peers)):
            for k in range(n):
                remote(k, m, peers[m][1], peers[m][1]).wait_recv()
        for cp in sends:
            cp.wait_send()
        for cp in local:
            cp.wait()

    any_spec = pl.BlockSpec(memory_space=pl.ANY)
    return pl.pallas_call(
        body, name=name, in_specs=[any_spec] * n, out_specs=(any_spec,) * n, out_shape=out_shape,
        scratch_shapes=[pltpu.SemaphoreType.DMA((n, N_DEV - 1)), pltpu.SemaphoreType.DMA((n, N_DEV - 1)),
                        pltpu.SemaphoreType.DMA((n,))])(*arrays)


def _gather_two_level(arrays, name):
    n = len(arrays)
    out_shape = tuple(jax.ShapeDtypeStruct((N_DEV,) + a.shape, a.dtype) for a in arrays)

    def body(*refs):
        ins, outs = refs[:n], refs[n:2 * n]
        send_sems, recv_sems, loc_sems = refs[2 * n:]
        ix, iy, ic = lax.axis_index("x"), lax.axis_index("y"), lax.axis_index("c")
        lin = lambda px, py, pc: 4 * px + 2 * py + pc
        me, sib = lin(ix, iy, ic), (ix, iy, 1 - ic)
        chips = [(1 - ix, iy), (ix, 1 - iy), (1 - ix, 1 - iy)]

        def copy(k, s, block, to, src=None):
            return pltpu.make_async_remote_copy(
                src_ref=outs[k].at[block] if src is None else src, dst_ref=outs[k].at[block],
                send_sem=send_sems.at[k, s], recv_sem=recv_sems.at[k, s], device_id=to,
                device_id_type=pl.DeviceIdType.MESH)

        local = [pltpu.make_async_copy(ins[k], outs[k].at[me], loc_sems.at[k]) for k in range(n)]
        for cp in local:
            cp.start()
        first = [copy(k, 1 + j, me, (cx, cy, ic), src=ins[k]) for j, (cx, cy) in enumerate(chips) for k in range(n)]
        first += [copy(k, 0, me, sib, src=ins[k]) for k in range(n)]
        for cp in first:
            cp.start()
        passed = []
        for j, (cx, cy) in enumerate(chips):
            for k in range(n):
                copy(k, 1 + j, lin(cx, cy, ic), sib).wait_recv()
                passed.append(copy(k, 4 + j, lin(cx, cy, ic), sib))
                passed[-1].start()
        for k in range(n):
            copy(k, 0, lin(*sib), sib).wait_recv()
            for j, (cx, cy) in enumerate(chips):
                copy(k, 4 + j, lin(cx, cy, 1 - ic), sib).wait_recv()
        for cp in first + passed:
            cp.wait_send()
        for cp in local:
            cp.wait()

    any_spec = pl.BlockSpec(memory_space=pl.ANY)
    return pl.pallas_call(
        body, name=name, in_specs=[any_spec] * n, out_specs=(any_spec,) * n, out_shape=out_shape,
        scratch_shapes=[pltpu.SemaphoreType.DMA((n, N_DEV - 1)), pltpu.SemaphoreType.DMA((n, N_DEV - 1)),
                        pltpu.SemaphoreType.DMA((n,))])(*arrays)


def _peer_table(chips=False):
    ix, iy, ic = lax.axis_index("x"), lax.axis_index("y"), lax.axis_index("c")
    peers = []
    for m in ((2, 4, 6) if chips else range(1, N_DEV)):
        px = 1 - ix if m & 4 else ix
        py = 1 - iy if m & 2 else iy
        pc = 1 - ic if m & 1 else ic
        peers.append(((px, py, pc), 2 * px + py if chips else 4 * px + 2 * py + pc))
    return (2 * ix + iy if chips else 4 * ix + 2 * iy + ic), peers


def _exchange_start(arrays, modes, after, name, chips=False):
    n = len(arrays)
    land_shapes = [(N_DEV,) + a.shape if md == "gather" else a.shape for a, md in zip(arrays, modes)]

    def body(*refs):
        ins, lands = refs[:n], refs[n:2 * n]
        send_sems, recv_sems = refs[2 * n + 1], refs[2 * n + 2]
        token = refs[-1]
        me, peers = _peer_table(chips)

        def src(k, slot):
            return ins[k] if modes[k] == "gather" else ins[k].at[slot]

        for peer, slot in peers:
            for k in range(n):
                pltpu.make_async_remote_copy(
                    src_ref=src(k, slot), dst_ref=lands[k].at[me], send_sem=send_sems, recv_sem=recv_sems,
                    device_id=peer, device_id_type=pl.DeviceIdType.MESH).start()
        token[...] = jnp.zeros_like(token)

    hbm = pl.BlockSpec(memory_space=pltpu.HBM)
    sem = pl.BlockSpec(memory_space=pltpu.SEMAPHORE)
    sem_shape = pltpu.SemaphoreType.DMA(())
    operands = [pltpu.with_memory_space_constraint(a, pltpu.HBM) for a in arrays]
    operands += [pltpu.with_memory_space_constraint(lax.empty(s, a.dtype), pltpu.HBM)
                 for s, a in zip(land_shapes, arrays)]
    out = pl.pallas_call(
        body, name=name,
        out_shape=(sem_shape, sem_shape) + tuple(pltpu.HBM(a.shape, a.dtype) for a in arrays)
        + tuple(pltpu.HBM(s, a.dtype) for s, a in zip(land_shapes, arrays)) + (jax.ShapeDtypeStruct((8, 128), F32),),
        in_specs=[hbm] * (2 * n) + [pl.BlockSpec(memory_space=pl.ANY)],
        out_specs=(sem, sem) + (hbm,) * (2 * n) + (pl.BlockSpec(memory_space=pltpu.VMEM),),
        input_output_aliases={i: 2 + i for i in range(2 * n)},
        compiler_params=pltpu.CompilerParams(has_side_effects=pltpu.SideEffectType.DATAFLOW_SIDE_EFFECTING))(
            *operands, after)
    return out[0], out[1], out[2:2 + n], out[2 + n:2 + 2 * n], out[-1]


def _exchange_wait(started, modes, after, name, chips=False):
    send_sems, recv_sems, sent, lands, _ = started
    n = len(sent)

    def body(*refs):
        ins, zones = refs[:n], refs[n:2 * n]
        send_ref, recv_ref = refs[2 * n], refs[2 * n + 1]
        _, peers = _peer_table(chips)

        def src(k, slot):
            return ins[k] if modes[k] == "gather" else ins[k].at[slot]

        for peer, slot in peers:
            for k in range(n):
                cp = pltpu.make_async_remote_copy(
                    src_ref=src(k, slot), dst_ref=zones[k].at[slot], send_sem=send_ref, recv_sem=recv_ref,
                    device_id=peer, device_id_type=pl.DeviceIdType.MESH)
                cp.wait_send()
                cp.wait_recv()

    hbm = pl.BlockSpec(memory_space=pltpu.HBM)
    sem = pl.BlockSpec(memory_space=pltpu.SEMAPHORE)
    out = pl.pallas_call(
        body, name=name,
        out_shape=tuple(pltpu.HBM(a.shape, a.dtype) for a in sent) + tuple(pltpu.HBM(a.shape, a.dtype) for a in lands),
        in_specs=[hbm] * (2 * n) + [sem, sem, pl.BlockSpec(memory_space=pl.ANY)], out_specs=(hbm,) * (2 * n),
        input_output_aliases={i: i for i in range(2 * n)},
        compiler_params=pltpu.CompilerParams(has_side_effects=pltpu.SideEffectType.DATAFLOW_SIDE_EFFECTING))(
            *sent, *lands, send_sems, recv_sems, after)
    ix, iy, ic = lax.axis_index("x"), lax.axis_index("y"), lax.axis_index("c")
    me = 2 * ix + iy if chips else 4 * ix + 2 * iy + ic
    filled = []
    for k in range(n):
        own = sent[k] if modes[k] == "gather" else lax.dynamic_index_in_dim(sent[k], me, axis=0, keepdims=False)
        filled.append(lax.dynamic_update_index_in_dim(out[n + k], own, me, axis=0))
    return filled


def _swap_sibling(arr, name):
    def body(in_ref, out_ref, send_sem, recv_sem):
        ix, iy, ic = lax.axis_index("x"), lax.axis_index("y"), lax.axis_index("c")
        cp = pltpu.make_async_remote_copy(src_ref=in_ref, dst_ref=out_ref, send_sem=send_sem, recv_sem=recv_sem,
                                          device_id=(ix, iy, 1 - ic), device_id_type=pl.DeviceIdType.MESH)
        cp.start()
        cp.wait()

    any_spec = pl.BlockSpec(memory_space=pl.ANY)
    return pl.pallas_call(body, name=name, in_specs=[any_spec], out_specs=any_spec,
                          out_shape=jax.ShapeDtypeStruct(arr.shape, arr.dtype),
                          scratch_shapes=[pltpu.SemaphoreType.DMA, pltpu.SemaphoreType.DMA])(arr)


def _add_pair(a, b, name):
    n, R, C = a.shape
    tr = _blk(R, 256)

    def body(a_ref, b_ref, o_ref):
        o_ref[...] = (a_ref[...].astype(F32) + b_ref[...].astype(F32)).astype(o_ref.dtype)

    spec = pl.BlockSpec((1, tr, C), lambda i, j: (i, j, 0))
    return pl.pallas_call(body, name=name, grid=(n, R // tr), in_specs=[spec, spec], out_specs=spec,
                          out_shape=jax.ShapeDtypeStruct(a.shape, a.dtype),
                          compiler_params=_params("parallel", "parallel"))(a, b)


W_IN_SPLITS = (0, 2048, 6144, 6176, 10272, 12320, 12336, 12352, 13376, 14400)
N_REPLICATED = 16640
REPLICATED = ("b_ada", "norm_mix_pre", "norm_mix_post", "ssm_conv_b", "ssm_dt_bias", "ssm_A_log", "ssm_D",
              "ssm_norm_w", "gdn_dt_bias", "gdn_A_log", "gdn_norm_w", "norm_mlp_pre", "norm_mlp_post")
WEIGHTS = ("w_ada", "b_ada", "norm_mix_pre", "norm_mix_post", "w_in", "ssm_conv_w", "ssm_conv_b", "ssm_dt_bias",
           "ssm_A_log", "ssm_D", "ssm_norm_w", "gdn_conv_w", "gdn_dt_bias", "gdn_A_log", "gdn_norm_w", "w_ssm_up",
           "w_gdn_up", "w_out", "norm_mlp_pre", "norm_mlp_post", "w_mlp_up", "w_mlp_down")


def _by_cols(t):
    return t.transpose(1, 0, 2).reshape(t.shape[1], N_DEV * t.shape[2])


def _to_col_shards(t):
    R, C8 = t.shape
    return t.reshape(R, N_DEV, C8 // N_DEV).transpose(1, 0, 2)


def _heads_first(t, groups):
    S = t.shape[0]
    return t.reshape(S, groups, t.shape[1] // groups).transpose(1, 0, 2)


def _heads_last(t):
    return t.transpose(1, 0, 2).reshape(t.shape[1], t.shape[0] * t.shape[2])


def kernel(x, c, w_ada, b_ada, norm_mix_pre, norm_mix_post, w_in, ssm_conv_w, ssm_conv_b, ssm_dt_bias, ssm_A_log, ssm_D, ssm_norm_w, gdn_conv_w, gdn_dt_bias, gdn_A_log, gdn_norm_w, w_ssm_up, w_gdn_up, w_out, norm_mlp_pre, norm_mlp_post, w_mlp_up, w_mlp_down, loss_target, m_w_ada, m_b_ada, m_norm_mix_pre, m_norm_mix_post, m_w_in, m_ssm_conv_w, m_ssm_conv_b, m_ssm_dt_bias, m_ssm_A_log, m_ssm_D, m_ssm_norm_w, m_gdn_conv_w, m_gdn_dt_bias, m_gdn_A_log, m_gdn_norm_w, m_w_ssm_up, m_w_gdn_up, m_w_out, m_norm_mlp_pre, m_norm_mlp_post, m_w_mlp_up, m_w_mlp_down, v_w_ada, v_b_ada, v_norm_mix_pre, v_norm_mix_post, v_w_in, v_ssm_conv_w, v_ssm_conv_b, v_ssm_dt_bias, v_ssm_A_log, v_ssm_D, v_ssm_norm_w, v_gdn_conv_w, v_gdn_dt_bias, v_gdn_A_log, v_gdn_norm_w, v_w_ssm_up, v_w_gdn_up, v_w_out, v_norm_mlp_pre, v_norm_mlp_post, v_w_mlp_up, v_w_mlp_down):
    S, Dm = x.shape[1], D_MODEL
    me = 4 * lax.axis_index("x") + 2 * lax.axis_index("y") + lax.axis_index("c")
    x2, tgt = x[0], loss_target[0]
    n_ada = w_ada.shape[2]
    given = dict(
        w_ada=(w_ada, m_w_ada, v_w_ada), b_ada=(b_ada, m_b_ada, v_b_ada),
        norm_mix_pre=(norm_mix_pre, m_norm_mix_pre, v_norm_mix_pre),
        norm_mix_post=(norm_mix_post, m_norm_mix_post, v_norm_mix_post), w_in=(w_in, m_w_in, v_w_in),
        ssm_conv_w=(ssm_conv_w, m_ssm_conv_w, v_ssm_conv_w), ssm_conv_b=(ssm_conv_b, m_ssm_conv_b, v_ssm_conv_b),
        ssm_dt_bias=(ssm_dt_bias, m_ssm_dt_bias, v_ssm_dt_bias), ssm_A_log=(ssm_A_log, m_ssm_A_log, v_ssm_A_log),
        ssm_D=(ssm_D, m_ssm_D, v_ssm_D), ssm_norm_w=(ssm_norm_w, m_ssm_norm_w, v_ssm_norm_w),
        gdn_conv_w=(gdn_conv_w, m_gdn_conv_w, v_gdn_conv_w), gdn_dt_bias=(gdn_dt_bias, m_gdn_dt_bias, v_gdn_dt_bias),
        gdn_A_log=(gdn_A_log, m_gdn_A_log, v_gdn_A_log), gdn_norm_w=(gdn_norm_w, m_gdn_norm_w, v_gdn_norm_w),
        w_ssm_up=(w_ssm_up, m_w_ssm_up, v_w_ssm_up), w_gdn_up=(w_gdn_up, m_w_gdn_up, v_w_gdn_up),
        w_out=(w_out, m_w_out, v_w_out), norm_mlp_pre=(norm_mlp_pre, m_norm_mlp_pre, v_norm_mlp_pre),
        norm_mlp_post=(norm_mlp_post, m_norm_mlp_post, v_norm_mlp_post), w_mlp_up=(w_mlp_up, m_w_mlp_up, v_w_mlp_up),
        w_mlp_down=(w_mlp_down, m_w_mlp_down, v_w_mlp_down))

    (c_all, scw, gcw, g_in) = _gather_two_level([c, ssm_conv_w[0], gdn_conv_w[0], w_in[0].astype(BF16)], "gather_w_in")
    c_all = c_all.reshape(N_DEV, Dm)
    wf = _by_cols(g_in)
    sp = W_IN_SPLITS
    w_main = jnp.concatenate([wf[:, sp[0]:sp[2]], wf[:, sp[3]:sp[5]], wf[:, sp[7]:sp[9]]], axis=1)
    w_small = jnp.concatenate([wf[:, sp[2]:sp[3]], wf[:, sp[5]:sp[7]], jnp.zeros((Dm, N_SMALL - 64), BF16)], axis=1)
    conv_w = jnp.concatenate([_by_cols(scw), _by_cols(gcw)], axis=1)
    conv_b = jnp.concatenate([ssm_conv_b, jnp.zeros_like(ssm_conv_b)], axis=1)

    b_loc = lax.dynamic_slice(b_ada, (0, me * n_ada), (1, n_ada))
    mod_part = _ada_fwd(c_all, w_ada[0], b_loc, "ada_fwd")
    (mod_rows,) = _exchange([mod_part.reshape(N_DEV, 1, n_ada)], ["a2a"], "exchange_mod")
    rest = _exchange_start([w_ssm_up[0].astype(BF16), w_gdn_up[0].astype(BF16), w_out[0].astype(BF16),
                            w_mlp_up[0].astype(BF16), w_mlp_down[0].astype(BF16)], ["gather"] * 5, mod_rows,
                           "gather_rest_start")
    mod = mod_rows.reshape(1, 6 * Dm) + rest[4][0:1, 0:1]
    sh1, sc1, g1, sh2, sc2, g2 = [mod[:, i * Dm:(i + 1) * Dm] for i in range(6)]

    h = _pre_fwd(x2, norm_mix_pre, sc1, sh1, "pre_mix")
    proj = _mm(h, w_main, S, N_MAIN, Dm, mode="nn", out_dtype=F32, name="proj_main")
    small = _mm(h, w_small, S, N_SMALL, Dm, mode="nn", out_dtype=F32, name="proj_small")
    conv = _conv_fwd(proj, conv_w, conv_b, "conv_fwd")
    dt_g, b_g, a_g = _heads_first(small[:, 0:32], 8), _heads_first(small[:, 32:48], 8), _heads_first(small[:, 48:64], 8)
    pv_ssm = jnp.stack([ssm_dt_bias.reshape(8, 4), ssm_A_log.reshape(8, 4), ssm_D.reshape(8, 4)], axis=1)
    nw_ssm = ssm_norm_w.reshape(8, 1, SSM_GROUP_WIDTH)
    pv_gdn = jnp.stack([gdn_dt_bias.reshape(8, 2), gdn_A_log.reshape(8, 2)], axis=1)
    y_ssd, ysn, st_ssm = _ssd_fwd(conv, proj, dt_g, pv_ssm, nw_ssm, "ssd_fwd")
    o_pre, ogn, st_gdn, t_inv = _gdn_fwd(conv, proj, b_g, a_g, pv_gdn, gdn_norm_w, "gdn_fwd")
    g_su, g_gu, g_out, g_mu, g_md = _exchange_wait(rest, ["gather"] * 5, ogn, "gather_rest_wait")
    w_su, w_gu = g_su.reshape(2 * Dm, Dm), g_gu.reshape(2 * Dm, Dm)
    w_o, w_mu, w_md = g_out.reshape(Dm, Dm), _by_cols(g_mu), g_md.reshape(4 * Dm, Dm)
    ys = _mm(ysn, w_su, S, Dm, 2 * Dm, mode="nn", out_dtype=F32, name="ssm_up")
    yg, merged = _gdn_up_merge(ogn, w_gu, ys, proj, "gdn_up_merge")
    mo, x1, h2 = _mix_out_post_pre(merged, w_o, x2, norm_mix_post, g1, norm_mlp_pre, sc2, sh2, "mix_out_post_pre")
    u, act = _mm(h2, w_mu, S, 4 * Dm, Dm, mode="nn", out_dtype=F32, epi="relu2", name="mlp_up")
    y_mlp = _mm(act, w_md, S, Dm, 4 * Dm, mode="nn", out_dtype=F32, name="mlp_down")
    dx2, loss_loc, dy, dg2, dw_post2 = _final_fwd_bwd(x1, y_mlp, norm_mlp_post, g2, tgt, "post_mlp_loss_bwd")

    du = _mm(dy, w_md, S, 4 * Dm, Dm, mode="nt", out_dtype=BF16, epi="drelu2", extra=u, name="mlp_down_dx")
    gw_md = _mm(act, dy, 4 * Dm, Dm, S, mode="tn", out_dtype=BF16, name="mlp_down_dw")
    dh2 = _mm(du, w_mu, S, Dm, 4 * Dm, mode="nt", out_dtype=F32, name="mlp_up_dx")
    gw_mu = _mm(h2, du, Dm, 4 * Dm, S, mode="tn", out_dtype=BF16, name="mlp_up_dw")
    mlp_x = _exchange_start([_to_col_shards(gw_mu), gw_md.reshape(N_DEV, -1, Dm)], ["a2a"] * 2, gw_md,
                            "grads_mlp_start")
    dx1, dsh2, dsc2, dw_pre2 = _pre_bwd(dh2, x1, norm_mlp_pre, sc2 + mlp_x[4][0:1, 0:1], dx2, "pre_mlp_bwd")
    dmo, dg1, dw_post1 = _post_bwd(dx1, mo, norm_mix_post, g1, "post_mix_bwd")
    gw_o = _mm(merged, dmo, Dm, Dm, S, mode="tn", out_dtype=BF16, name="mix_out_dw")
    dys, dyg, d_proj = _mix_out_dx_merge_bwd(dmo, w_o, ys, yg, proj, lax.empty((S, N_MAIN), BF16), "mix_out_dx_merge")
    dysn = _mm(dys, w_su, S, 2 * Dm, Dm, mode="nt", out_dtype=F32, name="ssm_up_dx")
    gw_su = _mm(ysn, dys, 2 * Dm, Dm, S, mode="tn", out_dtype=BF16, name="ssm_up_dw")
    dogn = _mm(dyg, w_gu, S, 2 * Dm, Dm, mode="nt", out_dtype=F32, name="gdn_up_dx")
    gw_gu = _mm(ogn, dyg, 2 * Dm, Dm, S, mode="tn", out_dtype=BF16, name="gdn_up_dw")
    mix_x = _exchange_start([gw_su.reshape(N_DEV, -1, Dm), gw_gu.reshape(N_DEV, -1, Dm), gw_o.reshape(N_DEV, -1, Dm)],
                            ["a2a"] * 3, gw_gu, "grads_mix_start")
    dxs, dBm, dCm, d_proj, ddt_g, dpv_ssm, dnw_ssm = _ssd_bwd(dysn, conv, proj, dt_g, pv_ssm + mix_x[4][0, 0], nw_ssm,
                                                              y_ssd, st_ssm, d_proj, "ssd_bwd")
    dq, dk, dv, d_proj, db_g, da_g, dpv_gdn, dnw_gdn = _gdn_bwd(dogn, conv, proj, b_g, a_g, pv_gdn, gdn_norm_w, o_pre,
                                                                st_gdn, t_inv, d_proj, "gdn_bwd")
    conv_pieces = []
    for nm, d_act, col0 in (("xs", dxs, 0), ("B", dBm, 2048), ("C", dCm, 3072), ("q", dq, 4096), ("k", dk, 5120),
                            ("v", dv, 6144)):
        d_proj, dw_piece, db_piece = _conv_bwd(d_act, proj, conv_w, conv_b, col0, d_proj, "conv_bwd_" + nm)
        conv_pieces.append((dw_piece, db_piece))
    d_small = jnp.concatenate([_heads_last(ddt_g), _heads_last(db_g), _heads_last(da_g),
                               jnp.zeros((S, N_SMALL - 64), F32)], axis=1).astype(BF16)
    gw_small = _mm(h, d_small, Dm, N_SMALL, S, mode="tn", out_dtype=BF16, name="proj_small_dw")
    main_cols = _mm(h, d_proj, Dm, N_MAIN, S, mode="tn", out_dtype=BF16, name="proj_main_dw")
    gw_in = jnp.concatenate([main_cols[:, 0:C_QKV], gw_small[:, 0:32], main_cols[:, C_QKV:C_GS], gw_small[:, 32:64],
                             main_cols[:, C_GS:N_MAIN]], axis=1)
    by_dest = _to_col_shards(gw_in).reshape(N_DEV // 2, 2, Dm, -1)
    my_c = lax.axis_index("c")
    keep = lax.dynamic_index_in_dim(by_dest, my_c, axis=1, keepdims=False)
    give = lax.dynamic_index_in_dim(by_dest, 1 - my_c, axis=1, keepdims=False)
    chip_sum = _add_pair(keep, _swap_sibling(give, "grads_w_in_pair"), "grads_w_in_pair_sum")
    in_x = _exchange_start([chip_sum], ["a2a"], gw_small, "grads_w_in_start", chips=True)
    dh = _mm(d_small, w_small + in_x[4][0:1, 0:1].astype(BF16), S, Dm, N_SMALL, mode="nt", out_dtype=F32,
             name="proj_small_dx")
    dh = _mm(d_proj, w_main, S, Dm, N_MAIN, mode="nt", out_dtype=F32, add=dh, name="proj_main_dx")
    dx, dsh1, dsc1, dw_pre1 = _pre_bwd(dh, x2, norm_mix_pre, sc1, dx1, "pre_mix_bwd")
    r_mu, r_md = _exchange_wait(mlp_x, ["a2a"] * 2, dx, "grads_mlp_wait")
    r_su, r_gu, r_o = _exchange_wait(mix_x, ["a2a"] * 3, dx, "grads_mix_wait")

    dconv_w = jnp.concatenate([p[0] for p in conv_pieces], axis=1)
    dconv_b = jnp.concatenate([p[1] for p in conv_pieces[:3]], axis=1)
    dmod = jnp.concatenate([dsh1, dsc1, dg1, dsh2, dsc2, dg2], axis=1)
    small_vec = jnp.concatenate(
        [dmod, dw_pre1, dw_post1, dconv_b, dpv_ssm[:, 0].reshape(1, 32), dpv_ssm[:, 1].reshape(1, 32),
         dpv_ssm[:, 2].reshape(1, 32), dnw_ssm.reshape(1, 2048), dpv_gdn[:, 0].reshape(1, 16),
         dpv_gdn[:, 1].reshape(1, 16), jnp.sum(dnw_gdn, axis=0), dw_pre2, dw_post2, dconv_w.reshape(1, -1)], axis=1)
    n_vec = small_vec.shape[1]
    small_vec = jnp.pad(small_vec, ((0, 0), (0, (-n_vec) % 1024))).reshape(-1, 1024)
    (small_all,) = _exchange([small_vec], ["gather"], "gather_small_grads")
    small_all = small_all.reshape(N_DEV, -1)
    dmod_cols = lax.dynamic_slice(small_all, (0, me * n_ada), (N_DEV, n_ada))
    gw_ada = _ada_bwd(c_all.T, dmod_cols, "ada_bwd")
    conv_all = small_all[:, N_REPLICATED:n_vec].reshape(N_DEV, CONV_K, 2 * N_DEV * 512)
    conv_contrib = jnp.concatenate(
        [lax.dynamic_slice(conv_all, (0, 0, me * 512), (N_DEV, CONV_K, 512)),
         lax.dynamic_slice(conv_all, (0, 0, N_DEV * 512 + me * 512), (N_DEV, CONV_K, 512))], axis=1)
    rep_contrib = small_all[:, :N_REPLICATED].reshape(N_DEV, N_REPLICATED // 128, 128)

    results = {}

    def adam_big(nm, contrib):
        w3 = given[nm]
        res = _adam(contrib, w3[0][0], w3[1][0], w3[2][0], "adam_" + nm)
        results[nm] = tuple(r.reshape(w3[0].shape) for r in res)

    adam_big("w_ada", gw_ada[None])
    adam_big("w_ssm_up", r_su)
    adam_big("w_gdn_up", r_gu)
    adam_big("w_out", r_o)
    adam_big("w_mlp_up", r_mu)
    adam_big("w_mlp_down", r_md)
    (r_in,) = _exchange_wait(in_x, ["a2a"], results["w_mlp_down"][0], "grads_w_in_wait", chips=True)
    adam_big("w_in", r_in)
    packed = [jnp.concatenate([given[nm][i] for nm in REPLICATED], axis=1).reshape(N_REPLICATED // 128, 128)
              for i in range(3)]
    rep_res = _adam(rep_contrib, packed[0], packed[1], packed[2], "adam_replicated")
    pos = 0
    for nm in REPLICATED:
        size = given[nm][0].shape[1]
        results[nm] = tuple(r.reshape(1, N_REPLICATED)[:, pos:pos + size] for r in rep_res)
        pos += size
    conv_wmv = [jnp.concatenate([given["ssm_conv_w"][i][0], given["gdn_conv_w"][i][0]], axis=0) for i in range(3)]
    conv_res = _adam(conv_contrib, conv_wmv[0], conv_wmv[1], conv_wmv[2], "adam_conv_w")
    results["ssm_conv_w"] = tuple(r[None, :CONV_K] for r in conv_res)
    results["gdn_conv_w"] = tuple(r[None, CONV_K:] for r in conv_res)

    loss = lax.psum(loss_loc[0, 0], ("x", "y", "c"))
    return (loss, dx[None]) + tuple(results[nm][i] for i in range(4) for nm in WEIGHTS)
```

```python
import jax
import jax.numpy as jnp
from jax import lax
from jax.experimental import pallas as pl
from jax.experimental.pallas import tpu as pltpu

F32 = jnp.float32
BF16 = jnp.bfloat16
N_DEV = 8
D_MODEL = 1024
EPS = 1e-6
CONV_K = 4
SSM_CHUNK = 128
SSM_HEAD_DIM = 64
SSM_D_STATE = 128
SSM_GROUPS = 8
SSM_HEADS_PER_GROUP = 4
SSM_GROUP_WIDTH = SSM_HEADS_PER_GROUP * SSM_HEAD_DIM
SSM_GROUPS_PER_STEP = 2
GDN_CHUNK = 64
GDN_HEAD = 128
GDN_QK_HEADS = 8
GDN_V_PER_QK = 2
GDN_QK_PER_STEP = 4
GDN_INV_BLOCK = 16
C_ZS, C_XBC, C_QKV, C_ZG, C_GS, C_GG, N_MAIN = 0, 2048, 6144, 10240, 12288, 13312, 14336
N_SMALL = 128
ADAM_LR, ADAM_B1, ADAM_B2, ADAM_EPS, ADAM_WD, ADAM_STEP = 0.001, 0.9, 0.999, 1e-08, 0.01, 10
VMEM_LIMIT = 56 * 1024 * 1024
MM_WHOLE_K = 4096
MM_SPLIT_K = 2048
NEG_INF = float("-inf")

_NT = (((1,), (1,)), ((), ()))
_NN = (((1,), (0,)), ((), ()))
_TN = (((0,), (0,)), ((), ()))


def _params(*sem):
    return pltpu.CompilerParams(dimension_semantics=sem, vmem_limit_bytes=VMEM_LIMIT)


def _dot(a, b, dims=_NN):
    return lax.dot_general(a.astype(BF16), b.astype(BF16), dims, preferred_element_type=F32)


def _split(a):
    hi = a.astype(BF16)
    return hi, (a - hi.astype(F32)).astype(BF16)


def _dot3(a, b, dims=_NN):
    ah, al = _split(a)
    bh, bl = _split(b)
    d = lambda u, v: lax.dot_general(u, v, dims, preferred_element_type=F32)
    return d(ah, bh) + (d(ah, bl) + d(al, bh))


def _dot2(a, b, dims=_NN):
    ah, al = _split(a)
    bb = b.astype(BF16)
    d = lambda u: lax.dot_general(u, bb, dims, preferred_element_type=F32)
    return d(ah) + d(al)


def _sigmoid(x):
    return 1.0 / (1.0 + jnp.exp(-x))


def _silu(x):
    return x * _sigmoid(x)


def _dsilu(x):
    s = _sigmoid(x)
    return s * (1.0 + x * (1.0 - s))


def _softplus(x):
    return jnp.maximum(x, 0.0) + jnp.log1p(jnp.exp(-jnp.abs(x)))


def _iota(n, m, d):
    return lax.broadcasted_iota(jnp.int32, (n, m), d)


def _rowsum(x):
    return jnp.sum(x, axis=1, keepdims=True)


def _colsum(x):
    return jnp.sum(x, axis=0, keepdims=True)


def _total(x):
    return _rowsum(_colsum(x))


MXU_LANES = 128


def _parts(x, n):
    out = []
    for _ in range(n):
        p = x.astype(BF16)
        out.append(p)
        x = x - p.astype(F32)
    return out


def _sum_by(m01, x, dims=_NN, n=3):
    return sum(lax.dot_general(m01, p, dims, preferred_element_type=F32) for p in _parts(x, n))


def _row_col_sums(q):
    ones = jnp.ones((q.shape[0], MXU_LANES), BF16)
    acc = 0.0
    for p in _parts(q, 2):
        acc = acc + (lax.dot_general(p, ones, _NN, preferred_element_type=F32)
                     - lax.dot_general(p, ones, _TN, preferred_element_type=F32))
    return acc[:, 0:1]


def _cumsum_forms(col, ii, jj):
    lower = jnp.where(ii >= jj, 1.0, 0.0).astype(BF16)
    cum_col = _sum_by(lower, jnp.broadcast_to(col, (col.shape[0], MXU_LANES)))[:, 0:1]
    cum_row = _colsum(jnp.where(ii <= jj, col, 0.0))
    return cum_col, cum_row


def _rev_cumsum_col(col, ii, jj):
    upper = jnp.where(ii <= jj, 1.0, 0.0).astype(BF16)
    return _sum_by(upper, jnp.broadcast_to(col, (col.shape[0], MXU_LANES)))[:, 0:1]


def _blk(dim, pref):
    return pref if dim % pref == 0 else dim


def _lockstep(gens):
    gens = list(gens)
    while gens:
        alive = []
        for g in gens:
            try:
                next(g)
                alive.append(g)
            except StopIteration:
                pass
        gens = alive


def _mm(a, b, M, N, K, *, mode, out_dtype, name, a_off=(0, 0), b_off=(0, 0), add=None, epi=None, extra=None,
        tm=1024, tn=1024):
    tm, tn = _blk(M, tm), _blk(N, tn)
    tk = K if K <= MM_WHOLE_K else _blk(K, MM_SPLIT_K)
    nk = K // tk
    if mode == "tn":
        a_spec = pl.BlockSpec((tk, tm), lambda i, j, k: (k + a_off[0] // tk, i + a_off[1] // tm))
        assert a_off[0] % tk == 0 and a_off[1] % tm == 0
    else:
        a_spec = pl.BlockSpec((tm, tk), lambda i, j, k: (i + a_off[0] // tm, k + a_off[1] // tk))
        assert a_off[0] % tm == 0 and a_off[1] % tk == 0
    if mode == "nt":
        b_spec = pl.BlockSpec((tn, tk), lambda i, j, k: (j + b_off[0] // tn, k + b_off[1] // tk))
        assert b_off[0] % tn == 0 and b_off[1] % tk == 0
    else:
        b_spec = pl.BlockSpec((tk, tn), lambda i, j, k: (k + b_off[0] // tk, j + b_off[1] // tn))
        assert b_off[0] % tk == 0 and b_off[1] % tn == 0
    dims = {"nn": _NN, "nt": _NT, "tn": _TN}[mode]
    o_spec = pl.BlockSpec((tm, tn), lambda i, j, k: (i, j))
    ins, in_specs = [a, b], [a_spec, b_spec]
    if add is not None:
        ins.append(add)
        in_specs.append(o_spec)
    if extra is not None:
        ins.append(extra)
        in_specs.append(o_spec)
    n_in = len(ins)
    if epi == "relu2":
        out_shape = (jax.ShapeDtypeStruct((M, N), F32), jax.ShapeDtypeStruct((M, N), BF16))
        out_specs = (o_spec, o_spec)
    else:
        out_shape = jax.ShapeDtypeStruct((M, N), out_dtype)
        out_specs = o_spec

    def body(*refs):
        a_ref, b_ref = refs[0], refs[1]
        outs = refs[n_in:] if nk == 1 else refs[n_in:-1]

        def finish(r):
            pos = 2
            if add is not None:
                r = r + refs[pos][...]
                pos += 1
            if epi == "relu2":
                outs[0][...] = r
                p = jnp.maximum(r, 0.0)
                outs[1][...] = (p * p).astype(BF16)
            elif epi == "drelu2":
                outs[0][...] = (r * (2.0 * jnp.maximum(refs[pos][...], 0.0))).astype(out_dtype)
            else:
                outs[0][...] = r.astype(out_dtype)

        if nk == 1:
            finish(_dot(a_ref[...], b_ref[...], dims))
            return
        acc = refs[-1]
        k = pl.program_id(2)

        @pl.when(k == 0)
        def _():
            acc[...] = jnp.zeros_like(acc)

        acc[...] += _dot(a_ref[...], b_ref[...], dims)

        @pl.when(k == nk - 1)
        def _():
            finish(acc[...])

    return pl.pallas_call(
        body, name=name, grid=(M // tm, N // tn, nk), in_specs=in_specs, out_specs=out_specs, out_shape=out_shape,
        scratch_shapes=[] if nk == 1 else [pltpu.VMEM((tm, tn), F32)],
        compiler_params=_params("parallel", "parallel", "arbitrary"))(*ins)


def _mm_rows(a, b, M, N, K, *, mode, name, extras, out_shapes, out_specs, epilogue, aliases=None, tm=512):
    tm = _blk(M, tm)
    a_spec = pl.BlockSpec((tm, K), lambda i: (i, 0))
    b_spec = pl.BlockSpec((K, N) if mode == "nn" else (N, K), lambda i: (0, 0))
    dims = _NN if mode == "nn" else _NT
    n_ex = len(extras)

    def body(a_ref, b_ref, *refs):
        epilogue(_dot(a_ref[...], b_ref[...], dims), refs[:n_ex], refs[n_ex:])

    return pl.pallas_call(
        body, name=name, grid=(M // tm,), in_specs=[a_spec, b_spec] + [sp for _, sp in extras],
        out_specs=tuple(out_specs), out_shape=tuple(out_shapes), input_output_aliases=aliases or {},
        compiler_params=_params("parallel"))(a, b, *[x for x, _ in extras])


def _row_spec(tb, d):
    return pl.BlockSpec((tb, d), lambda i: (i, 0))


def _vec_spec(d):
    return pl.BlockSpec((1, d), lambda i: (0, 0))


def _pre_fwd(x, w, sc, sh, name):
    S, Dm = x.shape
    tb = _blk(S, 512)

    def body(x_ref, w_ref, sc_ref, sh_ref, h_ref):
        xv = x_ref[...]
        r = lax.rsqrt(jnp.mean(xv * xv, axis=-1, keepdims=True) + EPS)
        h_ref[...] = ((xv * r * w_ref[...]) * (1.0 + sc_ref[...]) + sh_ref[...]).astype(BF16)

    return pl.pallas_call(
        body, name=name, grid=(S // tb,), in_specs=[_row_spec(tb, Dm)] + [_vec_spec(Dm)] * 3,
        out_specs=_row_spec(tb, Dm), out_shape=jax.ShapeDtypeStruct((S, Dm), BF16),
        compiler_params=_params("parallel"))(x, w, sc, sh)


def _final_fwd_bwd(x, y, w, g, target, name):
    S, Dm = x.shape
    tb = _blk(S, 512)
    nb = S // tb

    def body(x_ref, y_ref, w_ref, g_ref, t_ref, dx_ref, loss_ref, dy_ref, dg_ref, dw_ref, acc):
        i = pl.program_id(0)

        @pl.when(i == 0)
        def _():
            acc[...] = jnp.zeros_like(acc)
            dg_ref[...] = jnp.zeros_like(dg_ref)
            dw_ref[...] = jnp.zeros_like(dw_ref)

        yv = y_ref[...]
        r = lax.rsqrt(jnp.mean(yv * yv, axis=-1, keepdims=True) + EPS)
        yh = yv * r
        n = yh * w_ref[...]
        e = (x_ref[...] + g_ref[...] * n) - t_ref[...]
        dv = e * (1.0 / Dm)
        dx_ref[...] = dv
        acc[...] += _colsum(e * e)
        dg_ref[...] += _colsum(dv * n)
        dn = dv * g_ref[...]
        dw_ref[...] += _colsum(dn * yh)
        dyh = dn * w_ref[...]
        dy_ref[...] = (r * (dyh - yh * jnp.mean(dyh * yh, axis=-1, keepdims=True))).astype(BF16)

        @pl.when(i == nb - 1)
        def _():
            loss_ref[...] = (0.5 / Dm) * _rowsum(acc[...])

    row, vec = _row_spec(tb, Dm), _vec_spec(Dm)
    vec_shape = jax.ShapeDtypeStruct((1, Dm), F32)
    return pl.pallas_call(
        body, name=name, grid=(nb,), in_specs=[row, row, vec, vec, row],
        out_specs=(row, pl.BlockSpec((1, 1), lambda i: (0, 0)), row, vec, vec),
        out_shape=(jax.ShapeDtypeStruct((S, Dm), F32), jax.ShapeDtypeStruct((1, 1), F32),
                   jax.ShapeDtypeStruct((S, Dm), BF16), vec_shape, vec_shape),
        scratch_shapes=[pltpu.VMEM((1, Dm), F32)], compiler_params=_params("arbitrary"))(x, y, w, g, target)


def _post_bwd(dxo, y, w, g, name):
    S, Dm = y.shape
    tb = _blk(S, 512)

    def body(d_ref, y_ref, w_ref, g_ref, dy_ref, dg_ref, dw_ref):
        i = pl.program_id(0)

        @pl.when(i == 0)
        def _():
            dg_ref[...] = jnp.zeros_like(dg_ref)
            dw_ref[...] = jnp.zeros_like(dw_ref)

        yv, dv = y_ref[...], d_ref[...]
        r = lax.rsqrt(jnp.mean(yv * yv, axis=-1, keepdims=True) + EPS)
        yh = yv * r
        dg_ref[...] += _colsum(dv * (yh * w_ref[...]))
        dn = dv * g_ref[...]
        dw_ref[...] += _colsum(dn * yh)
        dyh = dn * w_ref[...]
        dy_ref[...] = (r * (dyh - yh * jnp.mean(dyh * yh, axis=-1, keepdims=True))).astype(BF16)

    return pl.pallas_call(
        body, name=name, grid=(S // tb,), in_specs=[_row_spec(tb, Dm)] * 2 + [_vec_spec(Dm)] * 2,
        out_specs=(_row_spec(tb, Dm), _vec_spec(Dm), _vec_spec(Dm)),
        out_shape=(jax.ShapeDtypeStruct((S, Dm), BF16), jax.ShapeDtypeStruct((1, Dm), F32),
                   jax.ShapeDtypeStruct((1, Dm), F32)),
        compiler_params=_params("arbitrary"))(dxo, y, w, g)


def _pre_bwd(dh, x, w, sc, dres, name):
    S, Dm = x.shape
    tb = _blk(S, 512)

    def body(dh_ref, x_ref, w_ref, sc_ref, dr_ref, dx_ref, dsh_ref, dsc_ref, dw_ref):
        i = pl.program_id(0)

        @pl.when(i == 0)
        def _():
            dsh_ref[...] = jnp.zeros_like(dsh_ref)
            dsc_ref[...] = jnp.zeros_like(dsc_ref)
            dw_ref[...] = jnp.zeros_like(dw_ref)

        xv, dv = x_ref[...], dh_ref[...]
        r = lax.rsqrt(jnp.mean(xv * xv, axis=-1, keepdims=True) + EPS)
        xh = xv * r
        one_sc = 1.0 + sc_ref[...]
        dsh_ref[...] += _colsum(dv)
        dsc_ref[...] += _colsum(dv * (xh * w_ref[...]))
        dw_ref[...] += _colsum(dv * one_sc * xh)
        dxh = dv * one_sc * w_ref[...]
        dx_ref[...] = dr_ref[...] + r * (dxh - xh * jnp.mean(dxh * xh, axis=-1, keepdims=True))

    vec = jax.ShapeDtypeStruct((1, Dm), F32)
    return pl.pallas_call(
        body, name=name, grid=(S // tb,),
        in_specs=[_row_spec(tb, Dm)] * 2 + [_vec_spec(Dm)] * 2 + [_row_spec(tb, Dm)],
        out_specs=(_row_spec(tb, Dm), _vec_spec(Dm), _vec_spec(Dm), _vec_spec(Dm)),
        out_shape=(jax.ShapeDtypeStruct((S, Dm), F32), vec, vec, vec),
        compiler_params=_params("arbitrary"))(dh, x, w, sc, dres)


D_PROJ_ANY = pl.BlockSpec(memory_space=pl.ANY)


def _gate_specs(tm, Dm):
    return (pl.BlockSpec((tm, Dm), lambda i: (i, C_GS // Dm)), pl.BlockSpec((tm, Dm), lambda i: (i, C_GG // Dm)))


def _gdn_up_merge(ogn, w_gu, ys, proj, name):
    S, K = ogn.shape
    Dm = ys.shape[1]
    tm = _blk(S, 512)
    row = _row_spec(tm, Dm)

    def epilogue(r, ex, out):
        ys_ref, gs_ref, gg_ref = ex
        out[0][...] = r
        out[1][...] = (_sigmoid(gs_ref[...]) * ys_ref[...] + _sigmoid(gg_ref[...]) * r).astype(BF16)

    gs_spec, gg_spec = _gate_specs(tm, Dm)
    return _mm_rows(ogn, w_gu, S, Dm, K, mode="nn", name=name, tm=tm,
                    extras=[(ys, row), (proj, gs_spec), (proj, gg_spec)],
                    out_shapes=[jax.ShapeDtypeStruct((S, Dm), F32), jax.ShapeDtypeStruct((S, Dm), BF16)],
                    out_specs=[row, row], epilogue=epilogue)


def _mix_out_post_pre(merged, w_o, x, w_post, g, w_pre, sc, sh, name):
    S, Dm = x.shape
    tm = _blk(S, 512)
    row, vec = _row_spec(tm, Dm), _vec_spec(Dm)

    def epilogue(r, ex, out):
        x_ref, wpost_ref, g_ref, wpre_ref, sc_ref, sh_ref = ex
        out[0][...] = r
        rr = lax.rsqrt(jnp.mean(r * r, axis=-1, keepdims=True) + EPS)
        x1 = x_ref[...] + g_ref[...] * (r * rr * wpost_ref[...])
        out[1][...] = x1
        r1 = lax.rsqrt(jnp.mean(x1 * x1, axis=-1, keepdims=True) + EPS)
        out[2][...] = ((x1 * r1 * wpre_ref[...]) * (1.0 + sc_ref[...]) + sh_ref[...]).astype(BF16)

    return _mm_rows(merged, w_o, S, Dm, Dm, mode="nn", name=name, tm=tm,
                    extras=[(x, row), (w_post, vec), (g, vec), (w_pre, vec), (sc, vec), (sh, vec)],
                    out_shapes=[jax.ShapeDtypeStruct((S, Dm), F32), jax.ShapeDtypeStruct((S, Dm), F32),
                                jax.ShapeDtypeStruct((S, Dm), BF16)], out_specs=[row, row, row], epilogue=epilogue)


def _mix_out_dx_merge_bwd(dmo, w_o, ys, yg, proj, d_proj, name):
    S, Dm = ys.shape
    tm = _blk(S, 512)
    row = _row_spec(tm, Dm)

    def epilogue(d, ex, out):
        ys_ref, yg_ref, gs_ref, gg_ref, _ = ex
        ss, sg = _sigmoid(gs_ref[...]), _sigmoid(gg_ref[...])
        out[0][...] = (d * ss).astype(BF16)
        out[1][...] = (d * sg).astype(BF16)
        out[2][:, :Dm] = (d * ys_ref[...] * ss * (1.0 - ss)).astype(BF16)
        out[2][:, Dm:] = (d * yg_ref[...] * sg * (1.0 - sg)).astype(BF16)

    gs_spec, gg_spec = _gate_specs(tm, Dm)
    return _mm_rows(dmo, w_o, S, Dm, Dm, mode="nt", name=name, tm=tm,
                    extras=[(ys, row), (yg, row), (proj, gs_spec), (proj, gg_spec), (d_proj, D_PROJ_ANY)],
                    out_shapes=[jax.ShapeDtypeStruct((S, Dm), BF16), jax.ShapeDtypeStruct((S, Dm), BF16),
                                jax.ShapeDtypeStruct(d_proj.shape, BF16)],
                    out_specs=[row, row, pl.BlockSpec((tm, 2 * Dm), lambda i: (i, C_GS // (2 * Dm)))],
                    epilogue=epilogue, aliases={6: 2})


CONV_COLS = 128


def _shift_down(x, k, rows):
    return jnp.where(rows >= k, pltpu.roll(x, k, 0), 0.0)


def _shift_up(x, k, rows, S):
    return jnp.where(rows < S - k, pltpu.roll(x, S - k, 0), 0.0)


def _conv_fwd(proj, w, b, name):
    S = proj.shape[0]
    n = w.shape[1]
    cb = CONV_COLS

    def body(x_ref, w_ref, b_ref, o_ref):
        x = x_ref[...]
        rows = _iota(S, cb, 0)
        pre = x * w_ref[CONV_K - 1:CONV_K, :] + b_ref[...]
        for k in range(1, CONV_K):
            pre = pre + _shift_down(x, k, rows) * w_ref[CONV_K - 1 - k:CONV_K - k, :]
        o_ref[...] = _silu(pre)

    return pl.pallas_call(
        body, name=name, grid=(n // cb,),
        in_specs=[pl.BlockSpec((S, cb), lambda j: (0, j + C_XBC // cb)), pl.BlockSpec((CONV_K, cb), lambda j: (0, j)),
                  pl.BlockSpec((1, cb), lambda j: (0, j))],
        out_specs=pl.BlockSpec((S, cb), lambda j: (0, j)), out_shape=jax.ShapeDtypeStruct((S, n), F32),
        compiler_params=_params("parallel"))(proj, w, b)


def _conv_bwd(dact, proj, w, b, col0, d_proj, name):
    S, n = dact.shape
    cb = CONV_COLS
    o = col0 // cb

    def body(d_ref, x_ref, w_ref, b_ref, _, dx_ref, dw_ref, db_ref):
        x = x_ref[...]
        rows = _iota(S, cb, 0)
        xs = [x] + [_shift_down(x, k, rows) for k in range(1, CONV_K)]
        pre = xs[0] * w_ref[CONV_K - 1:CONV_K, :] + b_ref[...]
        for k in range(1, CONV_K):
            pre = pre + xs[k] * w_ref[CONV_K - 1 - k:CONV_K - k, :]
        dpre = d_ref[...] * _dsilu(pre)
        db_ref[...] = _colsum(dpre)
        dx = dpre * w_ref[CONV_K - 1:CONV_K, :]
        for k in range(CONV_K):
            dw_ref[CONV_K - 1 - k:CONV_K - k, :] = _colsum(dpre * xs[k])
            if k:
                dx = dx + _shift_up(dpre, k, rows, S) * w_ref[CONV_K - 1 - k:CONV_K - k, :]
        dx_ref[...] = dx.astype(BF16)

    return pl.pallas_call(
        body, name=name, grid=(n // cb,),
        in_specs=[pl.BlockSpec((S, cb), lambda j: (0, j)), pl.BlockSpec((S, cb), lambda j: (0, j + o + C_XBC // cb)),
                  pl.BlockSpec((CONV_K, cb), lambda j: (0, j + o)), pl.BlockSpec((1, cb), lambda j: (0, j + o)),
                  D_PROJ_ANY],
        out_specs=(pl.BlockSpec((S, cb), lambda j: (0, j + o + C_XBC // cb)),
                   pl.BlockSpec((CONV_K, cb), lambda j: (0, j)), pl.BlockSpec((1, cb), lambda j: (0, j))),
        out_shape=(jax.ShapeDtypeStruct(d_proj.shape, BF16), jax.ShapeDtypeStruct((CONV_K, n), F32),
                   jax.ShapeDtypeStruct((1, n), F32)),
        input_output_aliases={4: 0}, compiler_params=_params("parallel"))(dact, proj, w, b, d_proj)


def _ssd_specs(L, order):
    G = SSM_GROUPS_PER_STEP
    W, N = G * SSM_GROUP_WIDTH, G * SSM_D_STATE
    x_spec = pl.BlockSpec((L, W), lambda g, c: (order(c), g))
    b_spec = pl.BlockSpec((L, N), lambda g, c: (order(c), 2048 // N + g))
    c_spec = pl.BlockSpec((L, N), lambda g, c: (order(c), 3072 // N + g))
    z_spec = pl.BlockSpec((L, W), lambda g, c: (order(c), C_ZS // W + g))
    dt_spec = pl.BlockSpec((G, L, SSM_HEADS_PER_GROUP), lambda g, c: (g, order(c), 0))
    p_spec = pl.BlockSpec((G, 3, SSM_HEADS_PER_GROUP), lambda g, c: (g, 0, 0))
    nw_spec = pl.BlockSpec((G, 1, SSM_GROUP_WIDTH), lambda g, c: (g, 0, 0))
    s_spec = pl.BlockSpec((G, 1, SSM_GROUP_WIDTH, SSM_D_STATE), lambda g, c: (g, order(c), 0, 0))
    return x_spec, b_spec, c_spec, z_spec, dt_spec, p_spec, nw_spec, s_spec


class _SsdGroup:
    def __init__(self, L):
        P, H, W = SSM_HEAD_DIM, SSM_HEADS_PER_GROUP, SSM_GROUP_WIDTH
        self.L = L
        self.ii, self.jj = _iota(L, L, 0), _iota(L, L, 1)
        self.lower = jnp.where(self.ii >= self.jj, 1.0, 0.0).astype(BF16)
        self.upper = jnp.where(self.ii <= self.jj, 1.0, 0.0).astype(BF16)
        self.lo = _iota(L, 2 * P, 1) < P
        self.lo_row = _iota(1, 2 * P, 1) < P
        bi, bj = _iota(W, W, 0), _iota(W, W, 1)
        self.block = jnp.where(bi // P == bj // P, 1.0, 0.0).astype(BF16)
        si, sj = _iota(2 * P, W, 0), _iota(2 * P, W, 1)
        self.pick = jnp.where(sj == si * P, 1.0, 0.0).astype(BF16)
        self.ones = jnp.ones((L, 2 * P), BF16)

    def spread(self, v4):
        R = v4.shape[0]
        lo = self.lo if R == self.L else self.lo_row
        b = lambda h: jnp.broadcast_to(v4[:, h:h + 1], (R, 2 * SSM_HEAD_DIM))
        return jnp.concatenate([jnp.where(lo, b(0), b(1)), jnp.where(lo, b(2), b(3))], axis=1)

    def gather4(self, v):
        return jnp.concatenate([v[:, h * SSM_HEAD_DIM:h * SSM_HEAD_DIM + 1] for h in range(SSM_HEADS_PER_GROUP)],
                               axis=1)

    def head_sums(self, z):
        return sum(lax.dot_general(p, self.block, _NN, preferred_element_type=F32) for p in _parts(z, 2))

    def pair_cols(self, full, pair):
        ps = full[:, pair * 128:(pair + 1) * 128]
        sw = pltpu.roll(ps, SSM_HEAD_DIM, 1)
        return jnp.where(self.lo, ps, sw), jnp.where(self.lo, sw, ps)

    def gates(self, dt4_raw, p):
        L = self.L
        dtr = self.spread(dt4_raw + p[0:1, :])
        dt = _softplus(dtr)
        A = self.spread(-jnp.exp(p[1:2, :]))
        acum = _sum_by(self.lower, dt * A)
        yield
        rows = _sum_by(self.pick, acum, _NT)
        yield
        a_last = acum[L - 1:L, :]
        cols = self.pair_cols(acum, 0) + self.pair_cols(acum, 1)
        decay, decay_t = [], []
        for h in range(SSM_HEADS_PER_GROUP):
            seg = cols[h] - rows[h:h + 1, :]
            decay.append(jnp.exp(jnp.where(self.ii >= self.jj, seg, NEG_INF)))
            decay_t.append(jnp.exp(jnp.where(self.jj >= self.ii, -seg, NEG_INF)))
        return dict(dtr=dtr, dt=dt, A=A, D=self.spread(p[2:3, :]), acum=acum, eac=jnp.exp(acum), a_last=a_last,
                    wdec=jnp.exp(a_last - acum), decay=decay, decay_t=decay_t,
                    ea_last=[jnp.exp(rows[h:h + 1, L - 1:L]) for h in range(SSM_HEADS_PER_GROUP)])


def _ssd_fwd(conv, proj, dt_raw, pvec, nw, name):
    S = conv.shape[0]
    L, P, N, H, W, G = SSM_CHUNK, SSM_HEAD_DIM, SSM_D_STATE, SSM_HEADS_PER_GROUP, SSM_GROUP_WIDTH, SSM_GROUPS_PER_STEP
    nc = S // L

    def body(x_ref, b_ref, c_ref, z_ref, dt_ref, p_ref, nw_ref, y_ref, yn_ref, s0_ref, state):
        c = pl.program_id(1)

        @pl.when(c == 0)
        def _():
            state[...] = jnp.zeros_like(state)

        k = _SsdGroup(L)

        def group(gi):
            gsl = slice(gi * W, (gi + 1) * W)
            Bm, Cm = b_ref[:, gi * N:(gi + 1) * N], c_ref[:, gi * N:(gi + 1) * N]
            x = x_ref[:, gsl]
            S0 = state[gsl, :]
            s0_ref[gi, 0] = S0
            CB = _dot(Cm, Bm, _NT)
            y_off = _dot(Cm, S0, _NT)
            t = yield from k.gates(dt_ref[gi], p_ref[gi])
            xdt = x * t["dt"]
            s_new = _dot(xdt * t["wdec"], Bm, _TN)
            y_diag = []
            for pair in range(H // 2):
                xp = xdt[:, pair * 128:(pair + 1) * 128]
                y_diag.append(jnp.where(k.lo, _dot(CB * t["decay"][2 * pair], xp),
                                        _dot(CB * t["decay"][2 * pair + 1], xp)))
            yield
            y = jnp.concatenate(y_diag, axis=1) + y_off * t["eac"]
            for h in range(H):
                hsl = slice(gi * W + h * P, gi * W + (h + 1) * P)
                state[hsl, :] = S0[h * P:(h + 1) * P, :] * t["ea_last"][h] + s_new[h * P:(h + 1) * P, :]
            y_ref[:, gsl] = y
            y2 = (y + t["D"] * x) * _silu(z_ref[:, gsl])
            r = lax.rsqrt(jnp.mean(y2 * y2, axis=-1, keepdims=True) + EPS)
            yn_ref[:, gsl] = (y2 * r * nw_ref[gi]).astype(BF16)

        _lockstep(group(gi) for gi in range(G))

    x_spec, b_spec, c_spec, z_spec, dt_spec, p_spec, nw_spec, s_spec = _ssd_specs(L, lambda c: c)
    y_spec = pl.BlockSpec((L, G * W), lambda g, c: (c, g))
    return pl.pallas_call(
        body, name=name, grid=(SSM_GROUPS // G, nc),
        in_specs=[x_spec, b_spec, c_spec, z_spec, dt_spec, p_spec, nw_spec],
        out_specs=(y_spec, y_spec, s_spec),
        out_shape=(jax.ShapeDtypeStruct((S, SSM_GROUPS * W), F32), jax.ShapeDtypeStruct((S, SSM_GROUPS * W), BF16),
                   jax.ShapeDtypeStruct((SSM_GROUPS, nc, W, N), F32)),
        scratch_shapes=[pltpu.VMEM((G * W, N), F32)],
        compiler_params=_params("parallel", "arbitrary"))(conv, conv, conv, proj, dt_raw, pvec, nw)


def _ssd_bwd(dyn, conv, proj, dt_raw, pvec, nw, y_ssd, states, d_proj, name):
    S = conv.shape[0]
    L, P, N, H, W, G = SSM_CHUNK, SSM_HEAD_DIM, SSM_D_STATE, SSM_HEADS_PER_GROUP, SSM_GROUP_WIDTH, SSM_GROUPS_PER_STEP
    nc = S // L

    def body(dyn_ref, x_ref, b_ref, c_ref, z_ref, dt_ref, p_ref, nw_ref, y_ref, s0_ref, _,
             dx_ref, db_ref, dc_ref, dz_ref, ddt_ref, dp_ref, dnw_ref, dstate):
        c = pl.program_id(1)

        @pl.when(c == 0)
        def _():
            dstate[...] = jnp.zeros_like(dstate)
            dp_ref[...] = jnp.zeros_like(dp_ref)
            dnw_ref[...] = jnp.zeros_like(dnw_ref)

        k = _SsdGroup(L)
        last = (_iota(L, 1, 0) == L - 1)

        def group(gi):
            gsl = slice(gi * W, (gi + 1) * W)
            Bm, Cm = b_ref[:, gi * N:(gi + 1) * N], c_ref[:, gi * N:(gi + 1) * N]
            x, z = x_ref[:, gsl], z_ref[:, gsl]
            S0, dS1 = s0_ref[gi, 0], dstate[gsl, :]
            CB = _dot(Cm, Bm, _NT)
            CBt = _dot(Bm, Cm, _NT)
            y_off_raw = _dot(Cm, S0, _NT)
            dXs_raw = _dot(Bm, dS1, _NT)
            t = yield from k.gates(dt_ref[gi], p_ref[gi])
            y1 = y_ref[:, gsl] + t["D"] * x
            sz = _silu(z)
            y2 = y1 * sz
            r = lax.rsqrt(jnp.mean(y2 * y2, axis=-1, keepdims=True) + EPS)
            y2h = y2 * r
            dyn_v = dyn_ref[:, gsl]
            dnw_ref[gi] += _colsum(dyn_v * y2h)
            dy2h = dyn_v * nw_ref[gi]
            dy2 = r * (dy2h - y2h * jnp.mean(dy2h * y2h, axis=-1, keepdims=True))
            dz_ref[:, gsl] = (dy2 * y1 * _dsilu(z)).astype(BF16)
            dY = dy2 * sz
            X = x * t["dt"]
            dYe = dY * t["eac"]
            dC_s = _dot(dYe, S0)
            dB_s = _dot(X * t["wdec"], dS1)
            dS_c = _dot(dYe, Cm, _TN)
            dXm, Gs, Gts = [], [], []
            for pair in range(H // 2):
                dYp, Xp = dY[:, pair * 128:(pair + 1) * 128], X[:, pair * 128:(pair + 1) * 128]
                dXm.append(jnp.where(k.lo, _dot(CBt * t["decay_t"][2 * pair], dYp),
                                     _dot(CBt * t["decay_t"][2 * pair + 1], dYp)))
                for mask in (k.lo, ~k.lo):
                    Gs.append(_dot(jnp.where(mask, dYp, 0.0), Xp, _NT))
                    Gts.append(_dot(jnp.where(mask, Xp, 0.0), dYp, _NT))
            yield
            dXs = dXs_raw * t["wdec"]
            dX = jnp.concatenate(dXm, axis=1) + dXs
            dCB, dCBt, q_sums = 0.0, 0.0, []
            for h in range(H):
                M, Mt = CB * t["decay"][h], CBt * t["decay_t"][h]
                dCB = dCB + Gs[h] * t["decay"][h]
                dCBt = dCBt + Gts[h] * t["decay_t"][h]
                d = Gs[h] * M - Gts[h] * Mt
                q_sums.append(sum(lax.dot_general(pt, k.ones, _NN, preferred_element_type=F32)
                                  for pt in _parts(d, 2)))
            q_f = jnp.concatenate([jnp.where(k.lo, q_sums[0], q_sums[1]), jnp.where(k.lo, q_sums[2], q_sums[3])],
                                  axis=1)
            x_dxs = k.head_sums(X * dXs)
            tot = [_total(dS1[h * P:(h + 1) * P, :] * S0[h * P:(h + 1) * P, :]) * t["ea_last"][h] for h in range(H)]
            tot_f = k.spread(jnp.concatenate(tot, axis=1))
            d_alast = _colsum(x_dxs) + tot_f
            dacum = q_f + k.head_sums(dY * (y_off_raw * t["eac"])) - x_dxs + jnp.where(last, d_alast, 0.0)
            dx_dt = k.head_sums(dX * x)
            d_skip = _colsum(k.head_sums(dY * x))
            for h in range(H):
                hsl = slice(gi * W + h * P, gi * W + (h + 1) * P)
                dstate[hsl, :] = t["ea_last"][h] * dS1[h * P:(h + 1) * P, :] + dS_c[h * P:(h + 1) * P, :]
            dc_s2 = _dot(dCB, Bm)
            db_s2 = _dot(dCBt, Cm)
            yield
            da = _sum_by(k.upper, dacum)
            yield
            ddt_raw = (da * t["A"] + dx_dt) * _sigmoid(t["dtr"])
            dx_ref[:, gsl] = dX * t["dt"] + t["D"] * dY
            dc_ref[:, gi * N:(gi + 1) * N] = dC_s + dc_s2
            db_ref[:, gi * N:(gi + 1) * N] = dB_s + db_s2
            ddt_ref[gi] = k.gather4(ddt_raw)
            dp_ref[gi] += k.gather4(jnp.concatenate([_colsum(ddt_raw), _colsum(da * t["dt"]) * t["A"], d_skip],
                                                    axis=0))

        _lockstep(group(gi) for gi in range(G))

    rev = lambda c: nc - 1 - c
    x_spec, b_spec, c_spec, z_spec, dt_spec, p_spec, nw_spec, s_spec = _ssd_specs(L, rev)
    y_spec = pl.BlockSpec((L, G * W), lambda g, c: (rev(c), g))
    n_spec = pl.BlockSpec((L, G * N), lambda g, c: (rev(c), g))
    return pl.pallas_call(
        body, name=name, grid=(SSM_GROUPS // G, nc),
        in_specs=[y_spec, x_spec, b_spec, c_spec, z_spec, dt_spec, p_spec, nw_spec, y_spec, s_spec, D_PROJ_ANY],
        out_specs=(y_spec, n_spec, n_spec, z_spec, dt_spec, p_spec, nw_spec),
        out_shape=(jax.ShapeDtypeStruct((S, SSM_GROUPS * W), F32), jax.ShapeDtypeStruct((S, SSM_GROUPS * N), F32),
                   jax.ShapeDtypeStruct((S, SSM_GROUPS * N), F32), jax.ShapeDtypeStruct(d_proj.shape, BF16),
                   jax.ShapeDtypeStruct((SSM_GROUPS, S, H), F32), jax.ShapeDtypeStruct((SSM_GROUPS, 3, H), F32),
                   jax.ShapeDtypeStruct((SSM_GROUPS, 1, W), F32)),
        scratch_shapes=[pltpu.VMEM((G * W, N), F32)], input_output_aliases={10: 3},
        compiler_params=_params("parallel", "arbitrary"))(dyn, conv, conv, conv, proj, dt_raw, pvec, nw, y_ssd, states,
                                                          d_proj)


def _unit_lower_inverse(A, ii, jj):
    eye = (ii == jj).astype(F32)
    same = (ii // GDN_INV_BLOCK) == (jj // GDN_INV_BLOCK)
    Ad = jnp.where(same, A, 0.0)
    Ao = A - Ad
    P2 = _dot3(Ad, Ad)
    yield
    P4, X = _dot(P2, P2), _dot3(eye - Ad, eye + P2)
    yield
    P8, X = _dot(P4, P4), X + _dot2(X, P4)
    yield
    X = X + _dot2(X, P8)
    yield
    Bm = _dot3(X, Ao)
    yield
    B2 = _dot3(Bm, Bm)
    yield
    Y = (eye - Bm) + B2 - _dot2(Bm, B2)
    yield
    T = _dot3(Y, X)
    yield
    return T


def _gdn_specs(L, order):
    G = GDN_QK_PER_STEP
    Hd, W = G * GDN_HEAD, G * GDN_V_PER_QK * GDN_HEAD
    q_spec = pl.BlockSpec((L, Hd), lambda h, c: (order(c), (C_QKV - C_XBC) // Hd + h))
    k_spec = pl.BlockSpec((L, Hd), lambda h, c: (order(c), (C_QKV - C_XBC + 1024) // Hd + h))
    v_spec = pl.BlockSpec((L, W), lambda h, c: (order(c), (C_QKV - C_XBC + 2048) // W + h))
    z_spec = pl.BlockSpec((L, W), lambda h, c: (order(c), C_ZG // W + h))
    ba_spec = pl.BlockSpec((G, L, GDN_V_PER_QK), lambda h, c: (h, order(c), 0))
    p_spec = pl.BlockSpec((G, 2, GDN_V_PER_QK), lambda h, c: (h, 0, 0))
    nw_spec = pl.BlockSpec((1, GDN_HEAD), lambda h, c: (0, 0))
    s_spec = pl.BlockSpec((G, 1, GDN_V_PER_QK * GDN_HEAD, GDN_HEAD), lambda h, c: (h, order(c), 0, 0))
    t_spec = pl.BlockSpec((G * GDN_V_PER_QK, 1, L, L), lambda h, c: (h, order(c), 0, 0))
    return q_spec, k_spec, v_spec, z_spec, ba_spec, p_spec, nw_spec, s_spec, t_spec


def _gdn_gates(qa, ka, b_col, a_col, p, j, ii, jj):
    L = qa.shape[0]
    sp_in = a_col + p[0:1, j:j + 1]
    neg_ea = -jnp.exp(p[1:2, j:j + 1])
    g = neg_ea * _softplus(sp_in)
    gcum, gcum_row = _cumsum_forms(g, ii, jj)
    rq = lax.rsqrt(_rowsum(qa * qa) + EPS)
    rk = lax.rsqrt(_rowsum(ka * ka) + EPS)
    q = qa * rq * (GDN_HEAD ** -0.5)
    k = ka * rk
    beta = _sigmoid(b_col)
    yield
    Dm = jnp.exp(jnp.where(ii >= jj, gcum - gcum_row, NEG_INF))
    eg = jnp.exp(gcum)
    g_last = gcum[L - 1:L, :]
    wdec = jnp.exp(g_last - gcum)
    return dict(rq=rq, rk=rk, q=q, k=k, beta=beta, sp_in=sp_in, neg_ea=neg_ea, g=g, Dm=Dm, kbeta=k * beta, eg=eg,
                g_last=g_last, wdec=wdec, kdec=k * wdec)


def _gdn_fwd(conv, proj, b_raw, a_raw, pvec, nw, name):
    S = conv.shape[0]
    L, Hd, J, G = GDN_CHUNK, GDN_HEAD, GDN_V_PER_QK, GDN_QK_PER_STEP
    W = J * Hd
    nc = S // L

    def body(q_ref, k_ref, v_ref, z_ref, b_ref, a_ref, p_ref, nw_ref, o_ref, on_ref, s0_ref, t_ref, state):
        c = pl.program_id(1)

        @pl.when(c == 0)
        def _():
            state[...] = jnp.zeros_like(state)

        ii, jj = _iota(L, L, 0), _iota(L, L, 1)
        for hq in range(G):
            s0_ref[hq, 0] = state[hq * W:(hq + 1) * W, :]

        def head(hq, j):
            hd = hq * J + j
            hsl, sl = slice(hq * Hd, (hq + 1) * Hd), slice(hd * Hd, (hd + 1) * Hd)
            t = yield from _gdn_gates(q_ref[:, hsl], k_ref[:, hsl], b_ref[hq][:, j:j + 1], a_ref[hq][:, j:j + 1],
                                      p_ref[hq], j, ii, jj)
            KK = _dot(t["kbeta"], t["k"], _NT)
            QK = _dot(t["q"], t["k"], _NT)
            yield
            T = yield from _unit_lower_inverse(jnp.where(ii > jj, KK * t["Dm"], 0.0), ii, jj)
            t_ref[hd, 0] = T
            S0 = state[sl, :]
            U = _dot2(T, v_ref[:, sl] * t["beta"])
            Wm = _dot2(T, t["kbeta"] * t["eg"])
            o_inter = _dot(t["q"] * t["eg"], S0)
            yield
            Vn = U - _dot(Wm, S0)
            yield
            o = o_inter + _dot(QK * t["Dm"], Vn)
            s_new = _dot(t["kdec"], Vn, _TN)
            yield
            state[sl, :] = S0 * jnp.exp(t["g_last"]) + s_new
            o_ref[:, sl] = o
            r = lax.rsqrt(jnp.mean(o * o, axis=-1, keepdims=True) + EPS)
            on_ref[:, sl] = ((o * r * nw_ref[...]) * _silu(z_ref[:, sl])).astype(BF16)

        _lockstep(head(hq, j) for hq in range(G) for j in range(J))

    q_spec, k_spec, v_spec, z_spec, ba_spec, p_spec, nw_spec, s_spec, t_spec = _gdn_specs(L, lambda c: c)
    o_spec = pl.BlockSpec((L, G * W), lambda h, c: (c, h))
    return pl.pallas_call(
        body, name=name, grid=(GDN_QK_HEADS // G, nc),
        in_specs=[q_spec, k_spec, v_spec, z_spec, ba_spec, ba_spec, p_spec, nw_spec],
        out_specs=(o_spec, o_spec, s_spec, t_spec),
        out_shape=(jax.ShapeDtypeStruct((S, GDN_QK_HEADS * W), F32), jax.ShapeDtypeStruct((S, GDN_QK_HEADS * W), BF16),
                   jax.ShapeDtypeStruct((GDN_QK_HEADS, nc, W, Hd), F32),
                   jax.ShapeDtypeStruct((GDN_QK_HEADS * J, nc, L, L), F32)),
        scratch_shapes=[pltpu.VMEM((G * W, Hd), F32)],
        compiler_params=_params("parallel", "arbitrary"))(conv, conv, conv, proj, b_raw, a_raw, pvec, nw)


def _gdn_bwd(don, conv, proj, b_raw, a_raw, pvec, nw, o_pre, states, t_inv, d_proj, name):
    S = conv.shape[0]
    L, Hd, J, G = GDN_CHUNK, GDN_HEAD, GDN_V_PER_QK, GDN_QK_PER_STEP
    W = J * Hd
    nc = S // L

    def body(don_ref, q_ref, k_ref, v_ref, z_ref, b_ref, a_ref, p_ref, nw_ref, o_ref, s0_ref, t_ref, _,
             dq_ref, dk_ref, dv_ref, dz_ref, db_ref, da_ref, dp_ref, dnw_ref, dstate):
        c = pl.program_id(1)

        @pl.when(c == 0)
        def _():
            dstate[...] = jnp.zeros_like(dstate)
            dp_ref[...] = jnp.zeros_like(dp_ref)
            dnw_ref[...] = jnp.zeros_like(dnw_ref)

        ii, jj = _iota(L, L, 0), _iota(L, L, 1)
        last = (_iota(L, 1, 0) == L - 1)
        res = {}

        def head(hq, j):
            hd = hq * J + j
            hsl, sl = slice(hq * Hd, (hq + 1) * Hd), slice(hd * Hd, (hd + 1) * Hd)
            qa, ka = q_ref[:, hsl], k_ref[:, hsl]
            t = yield from _gdn_gates(qa, ka, b_ref[hq][:, j:j + 1], a_ref[hq][:, j:j + 1], p_ref[hq], j, ii, jj)
            q, k, beta, eg, Dm, kbeta, kdec = (t[nm] for nm in ("q", "k", "beta", "eg", "Dm", "kbeta", "kdec"))
            T = t_ref[hd, 0]
            v, z, o = v_ref[:, sl], z_ref[:, sl], o_ref[:, sl]
            S0, dS1 = s0_ref[hq, 0, j * Hd:(j + 1) * Hd, :], dstate[sl, :]
            sz = _silu(z)
            r = lax.rsqrt(jnp.mean(o * o, axis=-1, keepdims=True) + EPS)
            oh = o * r
            d_on = don_ref[:, sl]
            dz_ref[:, sl] = (d_on * (oh * nw_ref[...]) * _dsilu(z)).astype(BF16)
            dn = d_on * sz
            dnw_part = _colsum(dn * oh)
            doh = dn * nw_ref[...]
            dO = r * (doh - oh * jnp.mean(doh * oh, axis=-1, keepdims=True))
            Rw = kbeta * eg
            qe = q * eg
            U = _dot2(T, v * beta)
            Wm = _dot2(T, Rw)
            KK = _dot(kbeta, k, _NT)
            QK = _dot(q, k, _NT)
            o_inter = _dot(qe, S0)
            dq_s = _dot(dO, S0, _NT)
            dS_q = _dot(qe, dO, _TN)
            yield
            Am = jnp.where(ii > jj, KK * Dm, 0.0)
            Pm = QK * Dm
            Vn = U - _dot(Wm, S0)
            dVn_s = _dot(kdec, dS1)
            yield
            dVn = _dot(Pm, dO, _TN) + dVn_s
            dP = _dot(dO, Vn, _NT)
            dKd = _dot(Vn, dS1, _NT)
            yield
            dQK = dP * Dm
            dq = _dot(dQK, k) + dq_s * eg
            dk = _dot(dQK, q, _TN) + dKd * t["wdec"]
            dstate[sl, :] = jnp.exp(t["g_last"]) * dS1 + dS_q - _dot(Wm, dVn, _TN)
            dW = -_dot(dVn, S0, _NT)
            dRu = _dot2(T, dVn, _TN)
            yield
            dRw = _dot2(T, dW, _TN)
            dA_u = _dot(dRu, U, _NT)
            yield
            dA = jnp.where(ii > jj, -(dA_u + _dot(dRw, Wm, _NT)), 0.0)
            yield
            dKK = dA * Dm
            dkbeta = _dot(dKK, k) + dRw * eg
            dk = dk + _dot(dKK, kbeta, _TN)
            yield
            dk = dk + dkbeta * beta
            dbeta = _rowsum(dkbeta * k) + _rowsum(dRu * v)
            dv_ref[:, sl] = dRu * beta
            Q = dA * Am + dP * Pm
            rho = _rowsum(dKd * kdec)
            d_glast = _colsum(rho) + jnp.exp(t["g_last"]) * _total(dS1 * S0)
            q_sums = _row_col_sums(Q)
            rest = _rowsum(dRw * Rw) + _rowsum(dO * o_inter) - rho + jnp.where(last, d_glast, 0.0)
            yield
            dg = _rev_cumsum_col(q_sums + rest, ii, jj)
            yield
            da_raw = dg * t["neg_ea"] * _sigmoid(t["sp_in"])
            res[hq, j] = dict(dq=dq, dk=dk, db=dbeta * beta * (1.0 - beta), da=da_raw, d_bias=_colsum(da_raw),
                              d_alog=_colsum(dg * t["g"]), dnw=dnw_part, rq=t["rq"], rk=t["rk"], k=k, qh=qa * t["rq"])

        _lockstep(head(hq, j) for hq in range(G) for j in range(J))
        for hq in range(G):
            parts = [res[hq, j] for j in range(J)]
            hsl = slice(hq * Hd, (hq + 1) * Hd)
            p0 = parts[0]
            dqh = sum(pt["dq"] for pt in parts) * (GDN_HEAD ** -0.5)
            dkn = sum(pt["dk"] for pt in parts)
            dq_ref[:, hsl] = p0["rq"] * (dqh - p0["qh"] * _rowsum(dqh * p0["qh"]))
            dk_ref[:, hsl] = p0["rk"] * (dkn - p0["k"] * _rowsum(dkn * p0["k"]))
            db_ref[hq] = jnp.concatenate([pt["db"] for pt in parts], axis=1)
            da_ref[hq] = jnp.concatenate([pt["da"] for pt in parts], axis=1)
            dp_ref[hq] += jnp.concatenate([jnp.concatenate([pt["d_bias"] for pt in parts], axis=1),
                                           jnp.concatenate([pt["d_alog"] for pt in parts], axis=1)], axis=0)
            dnw_ref[hq] += sum(pt["dnw"] for pt in parts)

    rev = lambda c: nc - 1 - c
    q_spec, k_spec, v_spec, z_spec, ba_spec, p_spec, nw_spec, s_spec, t_spec = _gdn_specs(L, rev)
    o_spec = pl.BlockSpec((L, G * W), lambda h, c: (rev(c), h))
    h_spec = pl.BlockSpec((L, G * Hd), lambda h, c: (rev(c), h))
    dnw_spec = pl.BlockSpec((G, 1, Hd), lambda h, c: (h, 0, 0))
    return pl.pallas_call(
        body, name=name, grid=(GDN_QK_HEADS // G, nc),
        in_specs=[o_spec, q_spec, k_spec, v_spec, z_spec, ba_spec, ba_spec, p_spec, nw_spec, o_spec, s_spec, t_spec,
                  D_PROJ_ANY],
        out_specs=(h_spec, h_spec, o_spec, z_spec, ba_spec, ba_spec, p_spec, dnw_spec),
        out_shape=(jax.ShapeDtypeStruct((S, GDN_QK_HEADS * Hd), F32), jax.ShapeDtypeStruct((S, GDN_QK_HEADS * Hd), F32),
                   jax.ShapeDtypeStruct((S, GDN_QK_HEADS * W), F32), jax.ShapeDtypeStruct(d_proj.shape, BF16),
                   jax.ShapeDtypeStruct((GDN_QK_HEADS, S, J), F32), jax.ShapeDtypeStruct((GDN_QK_HEADS, S, J), F32),
                   jax.ShapeDtypeStruct((GDN_QK_HEADS, 2, J), F32), jax.ShapeDtypeStruct((GDN_QK_HEADS, 1, Hd), F32)),
        scratch_shapes=[pltpu.VMEM((G * W, Hd), F32)], input_output_aliases={12: 3},
        compiler_params=_params("parallel", "arbitrary"))(don, conv, conv, conv, proj, b_raw, a_raw, pvec, nw, o_pre,
                                                          states, t_inv, d_proj)


def _ada_fwd(c_all, w_loc, b_loc, name):
    n = w_loc.shape[1]

    def body(c_ref, w_ref, b_ref, o_ref):
        o_ref[...] = _dot3(_silu(c_ref[...]), w_ref[...]) + b_ref[...]

    return pl.pallas_call(body, name=name, out_shape=jax.ShapeDtypeStruct((N_DEV, n), F32),
                          compiler_params=pltpu.CompilerParams(vmem_limit_bytes=VMEM_LIMIT))(c_all, w_loc, b_loc)


def _ada_bwd(c_all_t, dmod_cols, name):
    Dm, n = c_all_t.shape[0], dmod_cols.shape[1]

    def body(c_ref, d_ref, o_ref):
        ca = _silu(c_ref[...])
        acc = ca[:, 0:1] * d_ref[0:1, :]
        for i in range(1, N_DEV):
            acc = acc + ca[:, i:i + 1] * d_ref[i:i + 1, :]
        o_ref[...] = acc

    return pl.pallas_call(body, name=name, out_shape=jax.ShapeDtypeStruct((Dm, n), F32),
                          compiler_params=pltpu.CompilerParams(vmem_limit_bytes=VMEM_LIMIT))(c_all_t, dmod_cols)


ADAM_BLOCK_BYTES = 12 * 1024 * 1024


def _adam(contrib, w, m, v, name):
    n, R, C = contrib.shape
    tr = R
    while tr % 16 == 0 and (n + 7) * tr * C * 4 > ADAM_BLOCK_BYTES:
        tr //= 2

    def body(c_ref, w_ref, m_ref, v_ref, g_ref, d_ref, nm_ref, nv_ref):
        g = c_ref[0].astype(F32)
        for i in range(1, n):
            g = g + c_ref[i].astype(F32)
        nm = ADAM_B1 * m_ref[...] + (1.0 - ADAM_B1) * g
        nv = ADAM_B2 * v_ref[...] + (1.0 - ADAM_B2) * (g * g)
        m_hat = nm / (1.0 - ADAM_B1 ** ADAM_STEP)
        v_hat = nv / (1.0 - ADAM_B2 ** ADAM_STEP)
        g_ref[...] = g
        d_ref[...] = -ADAM_LR * (m_hat / (jnp.sqrt(v_hat) + ADAM_EPS) + ADAM_WD * w_ref[...])
        nm_ref[...] = nm
        nv_ref[...] = nv

    spec = pl.BlockSpec((tr, C), lambda i: (i, 0))
    shp = jax.ShapeDtypeStruct((R, C), F32)
    return pl.pallas_call(
        body, name=name, grid=(R // tr,), in_specs=[pl.BlockSpec((n, tr, C), lambda i: (0, i, 0)), spec, spec, spec],
        out_specs=(spec,) * 4, out_shape=(shp,) * 4, compiler_params=_params("parallel"))(contrib, w, m, v)


def _exchange(arrays, modes, name, chips=False):
    n = len(arrays)
    out_shape = tuple(jax.ShapeDtypeStruct((N_DEV,) + a.shape if md == "gather" else a.shape, a.dtype)
                      for a, md in zip(arrays, modes))

    def body(*refs):
        ins, outs = refs[:n], refs[n:2 * n]
        send_sems, recv_sems, loc_sems = refs[2 * n:]
        me, peers = _peer_table(chips)

        def src(k, slot):
            return ins[k] if modes[k] == "gather" else ins[k].at[slot]

        def remote(k, m, to_slot, land_slot):
            return pltpu.make_async_remote_copy(
                src_ref=src(k, to_slot), dst_ref=outs[k].at[land_slot], send_sem=send_sems.at[k, m],
                recv_sem=recv_sems.at[k, m], device_id=peers[m][0], device_id_type=pl.DeviceIdType.MESH)

        local = [pltpu.make_async_copy(src(k, me), outs[k].at[me], loc_sems.at[k]) for k in range(n)]
        for cp in local:
            cp.start()
        sends = [remote(k, m, peers[m][1], me) for m in range(len(peers)) for k in range(n)]
        for cp in sends:
            cp.start()
        for m in range(len(peers)):
            for k in range(n):
                remote(k, m, peers[m][1], peers[m][1]).wait_recv()
        for cp in sends:
            cp.wait_send()
        for cp in local:
            cp.wait()

    any_spec = pl.BlockSpec(memory_space=pl.ANY)
    return pl.pallas_call(
        body, name=name, in_specs=[any_spec] * n, out_specs=(any_spec,) * n, out_shape=out_shape,
        scratch_shapes=[pltpu.SemaphoreType.DMA((n, N_DEV - 1)), pltpu.SemaphoreType.DMA((n, N_DEV - 1)),
                        pltpu.SemaphoreType.DMA((n,))])(*arrays)


def _gather_two_level(arrays, name):
    n = len(arrays)
    out_shape = tuple(jax.ShapeDtypeStruct((N_DEV,) + a.shape, a.dtype) for a in arrays)

    def body(*refs):
        ins, outs = refs[:n], refs[n:2 * n]
        send_sems, recv_sems, loc_sems = refs[2 * n:]
        ix, iy, ic = lax.axis_index("x"), lax.axis_index("y"), lax.axis_index("c")
        lin = lambda px, py, pc: 4 * px + 2 * py + pc
        me, sib = lin(ix, iy, ic), (ix, iy, 1 - ic)
        chips = [(1 - ix, iy), (ix, 1 - iy), (1 - ix, 1 - iy)]

        def copy(k, s, block, to, src=None):
            return pltpu.make_async_remote_copy(
                src_ref=outs[k].at[block] if src is None else src, dst_ref=outs[k].at[block],
                send_sem=send_sems.at[k, s], recv_sem=recv_sems.at[k, s], device_id=to,
                device_id_type=pl.DeviceIdType.MESH)

        local = [pltpu.make_async_copy(ins[k], outs[k].at[me], loc_sems.at[k]) for k in range(n)]
        for cp in local:
            cp.start()
        first = [copy(k, 1 + j, me, (cx, cy, ic), src=ins[k]) for j, (cx, cy) in enumerate(chips) for k in range(n)]
        first += [copy(k, 0, me, sib, src=ins[k]) for k in range(n)]
        for cp in first:
            cp.start()
        passed = []
        for j, (cx, cy) in enumerate(chips):
            for k in range(n):
                copy(k, 1 + j, lin(cx, cy, ic), sib).wait_recv()
                passed.append(copy(k, 4 + j, lin(cx, cy, ic), sib))
                passed[-1].start()
        for k in range(n):
            copy(k, 0, lin(*sib), sib).wait_recv()
            for j, (cx, cy) in enumerate(chips):
                copy(k, 4 + j, lin(cx, cy, 1 - ic), sib).wait_recv()
        for cp in first + passed:
            cp.wait_send()
        for cp in local:
            cp.wait()

    any_spec = pl.BlockSpec(memory_space=pl.ANY)
    return pl.pallas_call(
        body, name=name, in_specs=[any_spec] * n, out_specs=(any_spec,) * n, out_shape=out_shape,
        scratch_shapes=[pltpu.SemaphoreType.DMA((n, N_DEV - 1)), pltpu.SemaphoreType.DMA((n, N_DEV - 1)),
                        pltpu.SemaphoreType.DMA((n,))])(*arrays)


def _peer_table(chips=False):
    ix, iy, ic = lax.axis_index("x"), lax.axis_index("y"), lax.axis_index("c")
    peers = []
    for m in ((2, 4, 6) if chips else range(1, N_DEV)):
        px = 1 - ix if m & 4 else ix
        py = 1 - iy if m & 2 else iy
        pc = 1 - ic if m & 1 else ic
        peers.append(((px, py, pc), 2 * px + py if chips else 4 * px + 2 * py + pc))
    return (2 * ix + iy if chips else 4 * ix + 2 * iy + ic), peers


def _exchange_start(arrays, modes, after, name, chips=False):
    n = len(arrays)
    land_shapes = [(N_DEV,) + a.shape if md == "gather" else a.shape for a, md in zip(arrays, modes)]

    def body(*refs):
        ins, lands = refs[:n], refs[n:2 * n]
        send_sems, recv_sems = refs[2 * n + 1], refs[2 * n + 2]
        token = refs[-1]
        me, peers = _peer_table(chips)

        def src(k, slot):
            return ins[k] if modes[k] == "gather" else ins[k].at[slot]

        for peer, slot in peers:
            for k in range(n):
                pltpu.make_async_remote_copy(
                    src_ref=src(k, slot), dst_ref=lands[k].at[me], send_sem=send_sems, recv_sem=recv_sems,
                    device_id=peer, device_id_type=pl.DeviceIdType.MESH).start()
        token[...] = jnp.zeros_like(token)

    hbm = pl.BlockSpec(memory_space=pltpu.HBM)
    sem = pl.BlockSpec(memory_space=pltpu.SEMAPHORE)
    sem_shape = pltpu.SemaphoreType.DMA(())
    operands = [pltpu.with_memory_space_constraint(a, pltpu.HBM) for a in arrays]
    operands += [pltpu.with_memory_space_constraint(lax.empty(s, a.dtype), pltpu.HBM)
                 for s, a in zip(land_shapes, arrays)]
    out = pl.pallas_call(
        body, name=name,
        out_shape=(sem_shape, sem_shape) + tuple(pltpu.HBM(a.shape, a.dtype) for a in arrays)
        + tuple(pltpu.HBM(s, a.dtype) for s, a in zip(land_shapes, arrays)) + (jax.ShapeDtypeStruct((8, 128), F32),),
        in_specs=[hbm] * (2 * n) + [pl.BlockSpec(memory_space=pl.ANY)],
        out_specs=(sem, sem) + (hbm,) * (2 * n) + (pl.BlockSpec(memory_space=pltpu.VMEM),),
        input_output_aliases={i: 2 + i for i in range(2 * n)},
        compiler_params=pltpu.CompilerParams(has_side_effects=pltpu.SideEffectType.DATAFLOW_SIDE_EFFECTING))(
            *operands, after)
    return out[0], out[1], out[2:2 + n], out[2 + n:2 + 2 * n], out[-1]


def _exchange_wait(started, modes, after, name, chips=False):
    send_sems, recv_sems, sent, lands, _ = started
    n = len(sent)

    def body(*refs):
        ins, zones = refs[:n], refs[n:2 * n]
        send_ref, recv_ref = refs[2 * n], refs[2 * n + 1]
        _, peers = _peer_table(chips)

        def src(k, slot):
            return ins[k] if modes[k] == "gather" else ins[k].at[slot]

        for peer, slot in peers:
            for k in range(n):
                cp = pltpu.make_async_remote_copy(
                    src_ref=src(k, slot), dst_ref=zones[k].at[slot], send_sem=send_ref, recv_sem=recv_ref,
                    device_id=peer, device_id_type=pl.DeviceIdType.MESH)
                cp.wait_send()
                cp.wait_recv()

    hbm = pl.BlockSpec(memory_space=pltpu.HBM)
    sem = pl.BlockSpec(memory_space=pltpu.SEMAPHORE)
    out = pl.pallas_call(
        body, name=name,
        out_shape=tuple(pltpu.HBM(a.shape, a.dtype) for a in sent) + tuple(pltpu.HBM(a.shape, a.dtype) for a in lands),
        in_specs=[hbm] * (2 * n) + [sem, sem, pl.BlockSpec(memory_space=pl.ANY)], out_specs=(hbm,) * (2 * n),
        input_output_aliases={i: i for i in range(2 * n)},
        compiler_params=pltpu.CompilerParams(has_side_effects=pltpu.SideEffectType.DATAFLOW_SIDE_EFFECTING))(
            *sent, *lands, send_sems, recv_sems, after)
    ix, iy, ic = lax.axis_index("x"), lax.axis_index("y"), lax.axis_index("c")
    me = 2 * ix + iy if chips else 4 * ix + 2 * iy + ic
    filled = []
    for k in range(n):
        own = sent[k] if modes[k] == "gather" else lax.dynamic_index_in_dim(sent[k], me, axis=0, keepdims=False)
        filled.append(lax.dynamic_update_index_in_dim(out[n + k], own, me, axis=0))
    return filled


def _swap_sibling(to_c0, to_c1, name):
    def body(c0_ref, c1_ref, out_ref, send_sem, recv_sem):
        ix, iy, ic = lax.axis_index("x"), lax.axis_index("y"), lax.axis_index("c")

        def copy(src):
            return pltpu.make_async_remote_copy(src_ref=src, dst_ref=out_ref, send_sem=send_sem, recv_sem=recv_sem,
                                                device_id=(ix, iy, 1 - ic), device_id_type=pl.DeviceIdType.MESH)

        @pl.when(ic == 0)
        def _():
            copy(c1_ref).start()

        @pl.when(ic == 1)
        def _():
            copy(c0_ref).start()

        copy(c0_ref).wait()

    any_spec = pl.BlockSpec(memory_space=pl.ANY)
    return pl.pallas_call(body, name=name, in_specs=[any_spec, any_spec], out_specs=any_spec,
                          out_shape=jax.ShapeDtypeStruct(to_c0.shape, to_c0.dtype),
                          scratch_shapes=[pltpu.SemaphoreType.DMA, pltpu.SemaphoreType.DMA])(to_c0, to_c1)


def _add_pair(to_c0, to_c1, got, name):
    n, R, C = got.shape
    tr = _blk(R, 256)

    def body(c0_ref, c1_ref, got_ref, o_ref):
        ic = lax.axis_index("c")

        @pl.when(ic == 0)
        def _():
            o_ref[...] = (c0_ref[...].astype(F32) + got_ref[...].astype(F32)).astype(o_ref.dtype)

        @pl.when(ic == 1)
        def _():
            o_ref[...] = (c1_ref[...].astype(F32) + got_ref[...].astype(F32)).astype(o_ref.dtype)

    spec = pl.BlockSpec((1, tr, C), lambda i, j: (i, j, 0))
    return pl.pallas_call(body, name=name, grid=(n, R // tr), in_specs=[spec, spec, spec], out_specs=spec,
                          out_shape=jax.ShapeDtypeStruct(got.shape, got.dtype),
                          compiler_params=_params("parallel", "parallel"))(to_c0, to_c1, got)


W_IN_SPLITS = (0, 2048, 6144, 6176, 10272, 12320, 12336, 12352, 13376, 14400)
N_REPLICATED = 16640
REPLICATED = ("b_ada", "norm_mix_pre", "norm_mix_post", "ssm_conv_b", "ssm_dt_bias", "ssm_A_log", "ssm_D",
              "ssm_norm_w", "gdn_dt_bias", "gdn_A_log", "gdn_norm_w", "norm_mlp_pre", "norm_mlp_post")
WEIGHTS = ("w_ada", "b_ada", "norm_mix_pre", "norm_mix_post", "w_in", "ssm_conv_w", "ssm_conv_b", "ssm_dt_bias",
           "ssm_A_log", "ssm_D", "ssm_norm_w", "gdn_conv_w", "gdn_dt_bias", "gdn_A_log", "gdn_norm_w", "w_ssm_up",
           "w_gdn_up", "w_out", "norm_mlp_pre", "norm_mlp_post", "w_mlp_up", "w_mlp_down")


def _cols_of_shards(g, a, b):
    width, pieces = g.shape[2], []
    while a < b:
        i = a // width
        hi = min(b, (i + 1) * width)
        pieces.append(g[i][:, a - i * width:hi - i * width])
        a = hi
    return pieces


ORIG_SEGMENTS = ((0, 6144, "main", 0), (6144, 6176, "small", 0), (6176, 12320, "main", 6144),
                 (12320, 12352, "small", 32), (12352, 14400, "main", 12288))


def _orig_cols(main_cols, small_cols, a, b):
    pieces = []
    for s0, s1, which, off in ORIG_SEGMENTS:
        lo, hi = max(a, s0), min(b, s1)
        if lo < hi:
            pieces.append((main_cols if which == "main" else small_cols)[:, off + lo - s0:off + hi - s0])
    return jnp.concatenate(pieces, axis=1)


def _by_cols(t):
    return t.transpose(1, 0, 2).reshape(t.shape[1], N_DEV * t.shape[2])


def _to_col_shards(t):
    R, C8 = t.shape
    return t.reshape(R, N_DEV, C8 // N_DEV).transpose(1, 0, 2)


def _heads_first(t, groups):
    S = t.shape[0]
    return t.reshape(S, groups, t.shape[1] // groups).transpose(1, 0, 2)


def _heads_last(t):
    return t.transpose(1, 0, 2).reshape(t.shape[1], t.shape[0] * t.shape[2])


def kernel(x, c, w_ada, b_ada, norm_mix_pre, norm_mix_post, w_in, ssm_conv_w, ssm_conv_b, ssm_dt_bias, ssm_A_log, ssm_D, ssm_norm_w, gdn_conv_w, gdn_dt_bias, gdn_A_log, gdn_norm_w, w_ssm_up, w_gdn_up, w_out, norm_mlp_pre, norm_mlp_post, w_mlp_up, w_mlp_down, loss_target, m_w_ada, m_b_ada, m_norm_mix_pre, m_norm_mix_post, m_w_in, m_ssm_conv_w, m_ssm_conv_b, m_ssm_dt_bias, m_ssm_A_log, m_ssm_D, m_ssm_norm_w, m_gdn_conv_w, m_gdn_dt_bias, m_gdn_A_log, m_gdn_norm_w, m_w_ssm_up, m_w_gdn_up, m_w_out, m_norm_mlp_pre, m_norm_mlp_post, m_w_mlp_up, m_w_mlp_down, v_w_ada, v_b_ada, v_norm_mix_pre, v_norm_mix_post, v_w_in, v_ssm_conv_w, v_ssm_conv_b, v_ssm_dt_bias, v_ssm_A_log, v_ssm_D, v_ssm_norm_w, v_gdn_conv_w, v_gdn_dt_bias, v_gdn_A_log, v_gdn_norm_w, v_w_ssm_up, v_w_gdn_up, v_w_out, v_norm_mlp_pre, v_norm_mlp_post, v_w_mlp_up, v_w_mlp_down):
    S, Dm = x.shape[1], D_MODEL
    me = 4 * lax.axis_index("x") + 2 * lax.axis_index("y") + lax.axis_index("c")
    x2, tgt = x[0], loss_target[0]
    n_ada = w_ada.shape[2]
    given = dict(
        w_ada=(w_ada, m_w_ada, v_w_ada), b_ada=(b_ada, m_b_ada, v_b_ada),
        norm_mix_pre=(norm_mix_pre, m_norm_mix_pre, v_norm_mix_pre),
        norm_mix_post=(norm_mix_post, m_norm_mix_post, v_norm_mix_post), w_in=(w_in, m_w_in, v_w_in),
        ssm_conv_w=(ssm_conv_w, m_ssm_conv_w, v_ssm_conv_w), ssm_conv_b=(ssm_conv_b, m_ssm_conv_b, v_ssm_conv_b),
        ssm_dt_bias=(ssm_dt_bias, m_ssm_dt_bias, v_ssm_dt_bias), ssm_A_log=(ssm_A_log, m_ssm_A_log, v_ssm_A_log),
        ssm_D=(ssm_D, m_ssm_D, v_ssm_D), ssm_norm_w=(ssm_norm_w, m_ssm_norm_w, v_ssm_norm_w),
        gdn_conv_w=(gdn_conv_w, m_gdn_conv_w, v_gdn_conv_w), gdn_dt_bias=(gdn_dt_bias, m_gdn_dt_bias, v_gdn_dt_bias),
        gdn_A_log=(gdn_A_log, m_gdn_A_log, v_gdn_A_log), gdn_norm_w=(gdn_norm_w, m_gdn_norm_w, v_gdn_norm_w),
        w_ssm_up=(w_ssm_up, m_w_ssm_up, v_w_ssm_up), w_gdn_up=(w_gdn_up, m_w_gdn_up, v_w_gdn_up),
        w_out=(w_out, m_w_out, v_w_out), norm_mlp_pre=(norm_mlp_pre, m_norm_mlp_pre, v_norm_mlp_pre),
        norm_mlp_post=(norm_mlp_post, m_norm_mlp_post, v_norm_mlp_post), w_mlp_up=(w_mlp_up, m_w_mlp_up, v_w_mlp_up),
        w_mlp_down=(w_mlp_down, m_w_mlp_down, v_w_mlp_down))

    (c_all, scw, gcw, g_in) = _gather_two_level([c, ssm_conv_w[0], gdn_conv_w[0], w_in[0].astype(BF16)], "gather_w_in")
    c_all = c_all.reshape(N_DEV, Dm)
    sp = W_IN_SPLITS
    w_main = jnp.concatenate(_cols_of_shards(g_in, sp[0], sp[2]) + _cols_of_shards(g_in, sp[3], sp[5])
                             + _cols_of_shards(g_in, sp[7], sp[9]), axis=1)
    w_small = jnp.concatenate(_cols_of_shards(g_in, sp[2], sp[3]) + _cols_of_shards(g_in, sp[5], sp[7])
                              + [jnp.zeros((Dm, N_SMALL - 64), BF16)], axis=1)
    conv_w = jnp.concatenate([_by_cols(scw), _by_cols(gcw)], axis=1)
    conv_b = jnp.concatenate([ssm_conv_b, jnp.zeros_like(ssm_conv_b)], axis=1)

    b_loc = lax.dynamic_slice(b_ada, (0, me * n_ada), (1, n_ada))
    mod_part = _ada_fwd(c_all, w_ada[0], b_loc, "ada_fwd")
    (mod_rows,) = _exchange([mod_part.reshape(N_DEV, 1, n_ada)], ["a2a"], "exchange_mod")
    rest = _exchange_start([w_ssm_up[0].astype(BF16), w_gdn_up[0].astype(BF16), w_out[0].astype(BF16),
                            w_mlp_up[0].astype(BF16), w_mlp_down[0].astype(BF16)], ["gather"] * 5, mod_rows,
                           "gather_rest_start")
    mod = mod_rows.reshape(1, 6 * Dm) + rest[4][0:1, 0:1]
    sh1, sc1, g1, sh2, sc2, g2 = [mod[:, i * Dm:(i + 1) * Dm] for i in range(6)]

    h = _pre_fwd(x2, norm_mix_pre, sc1, sh1, "pre_mix")
    proj = _mm(h, w_main, S, N_MAIN, Dm, mode="nn", out_dtype=F32, name="proj_main")
    small = _mm(h, w_small, S, N_SMALL, Dm, mode="nn", out_dtype=F32, name="proj_small")
    conv = _conv_fwd(proj, conv_w, conv_b, "conv_fwd")
    dt_g, b_g, a_g = _heads_first(small[:, 0:32], 8), _heads_first(small[:, 32:48], 8), _heads_first(small[:, 48:64], 8)
    pv_ssm = jnp.stack([ssm_dt_bias.reshape(8, 4), ssm_A_log.reshape(8, 4), ssm_D.reshape(8, 4)], axis=1)
    nw_ssm = ssm_norm_w.reshape(8, 1, SSM_GROUP_WIDTH)
    pv_gdn = jnp.stack([gdn_dt_bias.reshape(8, 2), gdn_A_log.reshape(8, 2)], axis=1)
    y_ssd, ysn, st_ssm = _ssd_fwd(conv, proj, dt_g, pv_ssm, nw_ssm, "ssd_fwd")
    o_pre, ogn, st_gdn, t_inv = _gdn_fwd(conv, proj, b_g, a_g, pv_gdn, gdn_norm_w, "gdn_fwd")
    g_su, g_gu, g_out, g_mu, g_md = _exchange_wait(rest, ["gather"] * 5, ogn, "gather_rest_wait")
    w_su, w_gu = g_su.reshape(2 * Dm, Dm), g_gu.reshape(2 * Dm, Dm)
    w_o, w_mu, w_md = g_out.reshape(Dm, Dm), _by_cols(g_mu), g_md.reshape(4 * Dm, Dm)
    ys = _mm(ysn, w_su, S, Dm, 2 * Dm, mode="nn", out_dtype=F32, name="ssm_up")
    yg, merged = _gdn_up_merge(ogn, w_gu, ys, proj, "gdn_up_merge")
    mo, x1, h2 = _mix_out_post_pre(merged, w_o, x2, norm_mix_post, g1, norm_mlp_pre, sc2, sh2, "mix_out_post_pre")
    u, act = _mm(h2, w_mu, S, 4 * Dm, Dm, mode="nn", out_dtype=F32, epi="relu2", name="mlp_up")
    y_mlp = _mm(act, w_md, S, Dm, 4 * Dm, mode="nn", out_dtype=F32, name="mlp_down")
    dx2, loss_loc, dy, dg2, dw_post2 = _final_fwd_bwd(x1, y_mlp, norm_mlp_post, g2, tgt, "post_mlp_loss_bwd")

    du = _mm(dy, w_md, S, 4 * Dm, Dm, mode="nt", out_dtype=BF16, epi="drelu2", extra=u, name="mlp_down_dx")
    gw_md = _mm(act, dy, 4 * Dm, Dm, S, mode="tn", out_dtype=BF16, name="mlp_down_dw")
    dh2 = _mm(du, w_mu, S, Dm, 4 * Dm, mode="nt", out_dtype=F32, name="mlp_up_dx")
    gw_mu = _mm(h2, du, Dm, 4 * Dm, S, mode="tn", out_dtype=BF16, name="mlp_up_dw")
    mlp_x = _exchange_start([_to_col_shards(gw_mu), gw_md.reshape(N_DEV, -1, Dm)], ["a2a"] * 2, gw_md,
                            "grads_mlp_start")
    dx1, dsh2, dsc2, dw_pre2 = _pre_bwd(dh2, x1, norm_mlp_pre, sc2 + mlp_x[4][0:1, 0:1], dx2, "pre_mlp_bwd")
    dmo, dg1, dw_post1 = _post_bwd(dx1, mo, norm_mix_post, g1, "post_mix_bwd")
    gw_o = _mm(merged, dmo, Dm, Dm, S, mode="tn", out_dtype=BF16, name="mix_out_dw")
    dys, dyg, d_proj = _mix_out_dx_merge_bwd(dmo, w_o, ys, yg, proj, lax.empty((S, N_MAIN), BF16), "mix_out_dx_merge")
    dysn = _mm(dys, w_su, S, 2 * Dm, Dm, mode="nt", out_dtype=F32, name="ssm_up_dx")
    gw_su = _mm(ysn, dys, 2 * Dm, Dm, S, mode="tn", out_dtype=BF16, name="ssm_up_dw")
    dogn = _mm(dyg, w_gu, S, 2 * Dm, Dm, mode="nt", out_dtype=F32, name="gdn_up_dx")
    gw_gu = _mm(ogn, dyg, 2 * Dm, Dm, S, mode="tn", out_dtype=BF16, name="gdn_up_dw")
    mix_x = _exchange_start([gw_su.reshape(N_DEV, -1, Dm), gw_gu.reshape(N_DEV, -1, Dm), gw_o.reshape(N_DEV, -1, Dm)],
                            ["a2a"] * 3, gw_gu, "grads_mix_start")
    dxs, dBm, dCm, d_proj, ddt_g, dpv_ssm, dnw_ssm = _ssd_bwd(dysn, conv, proj, dt_g, pv_ssm + mix_x[4][0, 0], nw_ssm,
                                                              y_ssd, st_ssm, d_proj, "ssd_bwd")
    dq, dk, dv, d_proj, db_g, da_g, dpv_gdn, dnw_gdn = _gdn_bwd(dogn, conv, proj, b_g, a_g, pv_gdn, gdn_norm_w, o_pre,
                                                                st_gdn, t_inv, d_proj, "gdn_bwd")
    conv_pieces = []
    for nm, d_act, col0 in (("xs", dxs, 0), ("B", dBm, 2048), ("C", dCm, 3072), ("q", dq, 4096), ("k", dk, 5120),
                            ("v", dv, 6144)):
        d_proj, dw_piece, db_piece = _conv_bwd(d_act, proj, conv_w, conv_b, col0, d_proj, "conv_bwd_" + nm)
        conv_pieces.append((dw_piece, db_piece))
    d_small = jnp.concatenate([_heads_last(ddt_g), _heads_last(db_g), _heads_last(da_g),
                               jnp.zeros((S, N_SMALL - 64), F32)], axis=1).astype(BF16)
    gw_small = _mm(h, d_small, Dm, N_SMALL, S, mode="tn", out_dtype=BF16, name="proj_small_dw")
    main_cols = _mm(h, d_proj, Dm, N_MAIN, S, mode="tn", out_dtype=BF16, name="proj_main_dw")
    n_shard = w_in.shape[2]
    slabs = [_orig_cols(main_cols, gw_small, i * n_shard, (i + 1) * n_shard) for i in range(N_DEV)]
    to_c0, to_c1 = jnp.stack(slabs[0::2]), jnp.stack(slabs[1::2])
    chip_sum = _add_pair(to_c0, to_c1, _swap_sibling(to_c0, to_c1, "grads_w_in_pair"), "grads_w_in_pair_sum")
    in_x = _exchange_start([chip_sum], ["a2a"], gw_small, "grads_w_in_start", chips=True)
    dh = _mm(d_small, w_small + in_x[4][0:1, 0:1].astype(BF16), S, Dm, N_SMALL, mode="nt", out_dtype=F32,
             name="proj_small_dx")
    dh = _mm(d_proj, w_main, S, Dm, N_MAIN, mode="nt", out_dtype=F32, add=dh, name="proj_main_dx")
    dx, dsh1, dsc1, dw_pre1 = _pre_bwd(dh, x2, norm_mix_pre, sc1, dx1, "pre_mix_bwd")
    r_mu, r_md = _exchange_wait(mlp_x, ["a2a"] * 2, dx, "grads_mlp_wait")
    r_su, r_gu, r_o = _exchange_wait(mix_x, ["a2a"] * 3, dx, "grads_mix_wait")

    dconv_w = jnp.concatenate([p[0] for p in conv_pieces], axis=1)
    dconv_b = jnp.concatenate([p[1] for p in conv_pieces[:3]], axis=1)
    dmod = jnp.concatenate([dsh1, dsc1, dg1, dsh2, dsc2, dg2], axis=1)
    small_vec = jnp.concatenate(
        [dmod, dw_pre1, dw_post1, dconv_b, dpv_ssm[:, 0].reshape(1, 32), dpv_ssm[:, 1].reshape(1, 32),
         dpv_ssm[:, 2].reshape(1, 32), dnw_ssm.reshape(1, 2048), dpv_gdn[:, 0].reshape(1, 16),
         dpv_gdn[:, 1].reshape(1, 16), jnp.sum(dnw_gdn, axis=0), dw_pre2, dw_post2, dconv_w.reshape(1, -1)], axis=1)
    n_vec = small_vec.shape[1]
    small_vec = jnp.pad(small_vec, ((0, 0), (0, (-n_vec) % 1024))).reshape(-1, 1024)
    (small_all,) = _exchange([small_vec], ["gather"], "gather_small_grads")
    small_all = small_all.reshape(N_DEV, -1)
    dmod_cols = lax.dynamic_slice(small_all, (0, me * n_ada), (N_DEV, n_ada))
    gw_ada = _ada_bwd(c_all.T, dmod_cols, "ada_bwd")
    conv_all = small_all[:, N_REPLICATED:n_vec].reshape(N_DEV, CONV_K, 2 * N_DEV * 512)
    conv_contrib = jnp.concatenate(
        [lax.dynamic_slice(conv_all, (0, 0, me * 512), (N_DEV, CONV_K, 512)),
         lax.dynamic_slice(conv_all, (0, 0, N_DEV * 512 + me * 512), (N_DEV, CONV_K, 512))], axis=1)
    rep_contrib = small_all[:, :N_REPLICATED].reshape(N_DEV, N_REPLICATED // 128, 128)

    results = {}

    def adam_big(nm, contrib):
        w3 = given[nm]
        res = _adam(contrib, w3[0][0], w3[1][0], w3[2][0], "adam_" + nm)
        results[nm] = tuple(r.reshape(w3[0].shape) for r in res)

    adam_big("w_ada", gw_ada[None])
    adam_big("w_ssm_up", r_su)
    adam_big("w_gdn_up", r_gu)
    adam_big("w_out", r_o)
    adam_big("w_mlp_up", r_mu)
    adam_big("w_mlp_down", r_md)
    (r_in,) = _exchange_wait(in_x, ["a2a"], results["w_mlp_down"][0], "grads_w_in_wait", chips=True)
    adam_big("w_in", r_in)
    packed = [jnp.concatenate([given[nm][i] for nm in REPLICATED], axis=1).reshape(N_REPLICATED // 128, 128)
              for i in range(3)]
    rep_res = _adam(rep_contrib, packed[0], packed[1], packed[2], "adam_replicated")
    pos = 0
    for nm in REPLICATED:
        size = given[nm][0].shape[1]
        results[nm] = tuple(r.reshape(1, N_REPLICATED)[:, pos:pos + size] for r in rep_res)
        pos += size
    conv_wmv = [jnp.concatenate([given["ssm_conv_w"][i][0], given["gdn_conv_w"][i][0]], axis=0) for i in range(3)]
    conv_res = _adam(conv_contrib, conv_wmv[0], conv_wmv[1], conv_wmv[2], "adam_conv_w")
    results["ssm_conv_w"] = tuple(r[None, :CONV_K] for r in conv_res)
    results["gdn_conv_w"] = tuple(r[None, CONV_K:] for r in conv_res)

    loss = lax.psum(loss_loc[0, 0], ("x", "y", "c"))
    return (loss, dx[None]) + tuple(results[nm][i] for i in range(4) for nm in WEIGHTS)
```

```python
import jax
import jax.numpy as jnp
from jax import lax
from jax.experimental import pallas as pl
from jax.experimental.pallas import tpu as pltpu

F32 = jnp.float32
BF16 = jnp.bfloat16
N_DEV = 8
D_MODEL = 1024
EPS = 1e-6
CONV_K = 4
SSM_CHUNK = 128
SSM_HEAD_DIM = 64
SSM_D_STATE = 128
SSM_GROUPS = 8
SSM_HEADS_PER_GROUP = 4
SSM_GROUP_WIDTH = SSM_HEADS_PER_GROUP * SSM_HEAD_DIM
SSM_GROUPS_PER_STEP = 2
GDN_CHUNK = 64
GDN_HEAD = 128
GDN_QK_HEADS = 8
GDN_V_PER_QK = 2
GDN_QK_PER_STEP = 4
GDN_INV_BLOCK = 16
C_ZS, C_XBC, C_QKV, C_ZG, C_GS, C_GG, N_MAIN = 0, 2048, 6144, 10240, 12288, 13312, 14336
N_SMALL = 128
ADAM_LR, ADAM_B1, ADAM_B2, ADAM_EPS, ADAM_WD, ADAM_STEP = 0.001, 0.9, 0.999, 1e-08, 0.01, 10
VMEM_LIMIT = 56 * 1024 * 1024
MM_WHOLE_K = 4096
MM_SPLIT_K = 2048
NEG_INF = float("-inf")

_NT = (((1,), (1,)), ((), ()))
_NN = (((1,), (0,)), ((), ()))
_TN = (((0,), (0,)), ((), ()))


def _params(*sem):
    return pltpu.CompilerParams(dimension_semantics=sem, vmem_limit_bytes=VMEM_LIMIT)


def _dot(a, b, dims=_NN):
    return lax.dot_general(a.astype(BF16), b.astype(BF16), dims, preferred_element_type=F32)


def _split(a):
    hi = a.astype(BF16)
    return hi, (a - hi.astype(F32)).astype(BF16)


def _dot3(a, b, dims=_NN):
    ah, al = _split(a)
    bh, bl = _split(b)
    d = lambda u, v: lax.dot_general(u, v, dims, preferred_element_type=F32)
    return d(ah, bh) + (d(ah, bl) + d(al, bh))


def _dot2(a, b, dims=_NN):
    ah, al = _split(a)
    bb = b.astype(BF16)
    d = lambda u: lax.dot_general(u, bb, dims, preferred_element_type=F32)
    return d(ah) + d(al)


def _sigmoid(x):
    return 0.5 * jnp.tanh(0.5 * x) + 0.5


def _silu(x):
    return x * _sigmoid(x)


def _dsilu(x):
    s = _sigmoid(x)
    return s * (1.0 + x * (1.0 - s))


def _softplus(x):
    return jnp.maximum(x, 0.0) + jnp.log1p(jnp.exp(-jnp.abs(x)))


def _iota(n, m, d):
    return lax.broadcasted_iota(jnp.int32, (n, m), d)


def _rowsum(x):
    return jnp.sum(x, axis=1, keepdims=True)


def _colsum(x):
    return jnp.sum(x, axis=0, keepdims=True)


def _total(x):
    return _rowsum(_colsum(x))


MXU_LANES = 128


def _parts(x, n):
    out = []
    for _ in range(n):
        p = x.astype(BF16)
        out.append(p)
        x = x - p.astype(F32)
    return out


def _sum_by(m01, x, dims=_NN, n=3):
    return sum(lax.dot_general(m01, p, dims, preferred_element_type=F32) for p in _parts(x, n))


def _row_col_sums(q):
    ones = jnp.ones((q.shape[0], MXU_LANES), BF16)
    acc = 0.0
    for p in _parts(q, 2):
        acc = acc + (lax.dot_general(p, ones, _NN, preferred_element_type=F32)
                     - lax.dot_general(p, ones, _TN, preferred_element_type=F32))
    return acc[:, 0:1]


def _cumsum_forms(col, ii, jj):
    lower = jnp.where(ii >= jj, 1.0, 0.0).astype(BF16)
    cum_col = _sum_by(lower, jnp.broadcast_to(col, (col.shape[0], MXU_LANES)))[:, 0:1]
    cum_row = _colsum(jnp.where(ii <= jj, col, 0.0))
    return cum_col, cum_row


def _rev_cumsum_col(col, ii, jj):
    upper = jnp.where(ii <= jj, 1.0, 0.0).astype(BF16)
    return _sum_by(upper, jnp.broadcast_to(col, (col.shape[0], MXU_LANES)))[:, 0:1]


def _blk(dim, pref):
    return pref if dim % pref == 0 else dim


def _lockstep(gens):
    gens = list(gens)
    while gens:
        alive = []
        for g in gens:
            try:
                next(g)
                alive.append(g)
            except StopIteration:
                pass
        gens = alive


def _mm(a, b, M, N, K, *, mode, out_dtype, name, a_off=(0, 0), b_off=(0, 0), add=None, epi=None, extra=None,
        tm=1024, tn=1024):
    tm, tn = _blk(M, tm), _blk(N, tn)
    tk = K if K <= MM_WHOLE_K else _blk(K, MM_SPLIT_K)
    nk = K // tk
    if mode == "tn":
        a_spec = pl.BlockSpec((tk, tm), lambda i, j, k: (k + a_off[0] // tk, i + a_off[1] // tm))
        assert a_off[0] % tk == 0 and a_off[1] % tm == 0
    else:
        a_spec = pl.BlockSpec((tm, tk), lambda i, j, k: (i + a_off[0] // tm, k + a_off[1] // tk))
        assert a_off[0] % tm == 0 and a_off[1] % tk == 0
    if mode == "nt":
        b_spec = pl.BlockSpec((tn, tk), lambda i, j, k: (j + b_off[0] // tn, k + b_off[1] // tk))
        assert b_off[0] % tn == 0 and b_off[1] % tk == 0
    else:
        b_spec = pl.BlockSpec((tk, tn), lambda i, j, k: (k + b_off[0] // tk, j + b_off[1] // tn))
        assert b_off[0] % tk == 0 and b_off[1] % tn == 0
    dims = {"nn": _NN, "nt": _NT, "tn": _TN}[mode]
    o_spec = pl.BlockSpec((tm, tn), lambda i, j, k: (i, j))
    ins, in_specs = [a, b], [a_spec, b_spec]
    if add is not None:
        ins.append(add)
        in_specs.append(o_spec)
    if extra is not None:
        ins.append(extra)
        in_specs.append(o_spec)
    n_in = len(ins)
    if epi == "relu2":
        out_shape = (jax.ShapeDtypeStruct((M, N), F32), jax.ShapeDtypeStruct((M, N), BF16))
        out_specs = (o_spec, o_spec)
    else:
        out_shape = jax.ShapeDtypeStruct((M, N), out_dtype)
        out_specs = o_spec

    def body(*refs):
        a_ref, b_ref = refs[0], refs[1]
        outs = refs[n_in:] if nk == 1 else refs[n_in:-1]

        def finish(r):
            pos = 2
            if add is not None:
                r = r + refs[pos][...]
                pos += 1
            if epi == "relu2":
                outs[0][...] = r
                p = jnp.maximum(r, 0.0)
                outs[1][...] = (p * p).astype(BF16)
            elif epi == "drelu2":
                outs[0][...] = (r * (2.0 * jnp.maximum(refs[pos][...], 0.0))).astype(out_dtype)
            else:
                outs[0][...] = r.astype(out_dtype)

        if nk == 1:
            finish(_dot(a_ref[...], b_ref[...], dims))
            return
        acc = refs[-1]
        k = pl.program_id(2)

        @pl.when(k == 0)
        def _():
            acc[...] = jnp.zeros_like(acc)

        acc[...] += _dot(a_ref[...], b_ref[...], dims)

        @pl.when(k == nk - 1)
        def _():
            finish(acc[...])

    return pl.pallas_call(
        body, name=name, grid=(M // tm, N // tn, nk), in_specs=in_specs, out_specs=out_specs, out_shape=out_shape,
        scratch_shapes=[] if nk == 1 else [pltpu.VMEM((tm, tn), F32)],
        compiler_params=_params("parallel", "parallel", "arbitrary"))(*ins)


def _mm_rows(a, b, M, N, K, *, mode, name, extras, out_shapes, out_specs, epilogue, aliases=None, tm=512):
    tm = _blk(M, tm)
    a_spec = pl.BlockSpec((tm, K), lambda i: (i, 0))
    b_spec = pl.BlockSpec((K, N) if mode == "nn" else (N, K), lambda i: (0, 0))
    dims = _NN if mode == "nn" else _NT
    n_ex = len(extras)

    def body(a_ref, b_ref, *refs):
        epilogue(_dot(a_ref[...], b_ref[...], dims), refs[:n_ex], refs[n_ex:])

    return pl.pallas_call(
        body, name=name, grid=(M // tm,), in_specs=[a_spec, b_spec] + [sp for _, sp in extras],
        out_specs=tuple(out_specs), out_shape=tuple(out_shapes), input_output_aliases=aliases or {},
        compiler_params=_params("parallel"))(a, b, *[x for x, _ in extras])


def _row_spec(tb, d):
    return pl.BlockSpec((tb, d), lambda i: (i, 0))


def _vec_spec(d):
    return pl.BlockSpec((1, d), lambda i: (0, 0))


def _pre_fwd(x, w, sc, sh, name):
    S, Dm = x.shape
    tb = _blk(S, 512)

    def body(x_ref, w_ref, sc_ref, sh_ref, h_ref):
        xv = x_ref[...]
        r = lax.rsqrt(jnp.mean(xv * xv, axis=-1, keepdims=True) + EPS)
        h_ref[...] = ((xv * r * w_ref[...]) * (1.0 + sc_ref[...]) + sh_ref[...]).astype(BF16)

    return pl.pallas_call(
        body, name=name, grid=(S // tb,), in_specs=[_row_spec(tb, Dm)] + [_vec_spec(Dm)] * 3,
        out_specs=_row_spec(tb, Dm), out_shape=jax.ShapeDtypeStruct((S, Dm), BF16),
        compiler_params=_params("parallel"))(x, w, sc, sh)


def _final_fwd_bwd(x, y, w, g, target, name):
    S, Dm = x.shape
    tb = _blk(S, 512)
    nb = S // tb

    def body(x_ref, y_ref, w_ref, g_ref, t_ref, dx_ref, loss_ref, dy_ref, dg_ref, dw_ref, acc):
        i = pl.program_id(0)

        @pl.when(i == 0)
        def _():
            acc[...] = jnp.zeros_like(acc)
            dg_ref[...] = jnp.zeros_like(dg_ref)
            dw_ref[...] = jnp.zeros_like(dw_ref)

        yv = y_ref[...]
        r = lax.rsqrt(jnp.mean(yv * yv, axis=-1, keepdims=True) + EPS)
        yh = yv * r
        n = yh * w_ref[...]
        e = (x_ref[...] + g_ref[...] * n) - t_ref[...]
        dv = e * (1.0 / Dm)
        dx_ref[...] = dv
        acc[...] += _colsum(e * e)
        dg_ref[...] += _colsum(dv * n)
        dn = dv * g_ref[...]
        dw_ref[...] += _colsum(dn * yh)
        dyh = dn * w_ref[...]
        dy_ref[...] = (r * (dyh - yh * jnp.mean(dyh * yh, axis=-1, keepdims=True))).astype(BF16)

        @pl.when(i == nb - 1)
        def _():
            loss_ref[...] = (0.5 / Dm) * _rowsum(acc[...])

    row, vec = _row_spec(tb, Dm), _vec_spec(Dm)
    vec_shape = jax.ShapeDtypeStruct((1, Dm), F32)
    return pl.pallas_call(
        body, name=name, grid=(nb,), in_specs=[row, row, vec, vec, row],
        out_specs=(row, pl.BlockSpec((1, 1), lambda i: (0, 0)), row, vec, vec),
        out_shape=(jax.ShapeDtypeStruct((S, Dm), F32), jax.ShapeDtypeStruct((1, 1), F32),
                   jax.ShapeDtypeStruct((S, Dm), BF16), vec_shape, vec_shape),
        scratch_shapes=[pltpu.VMEM((1, Dm), F32)], compiler_params=_params("arbitrary"))(x, y, w, g, target)


def _post_bwd(dxo, y, w, g, name):
    S, Dm = y.shape
    tb = _blk(S, 512)

    def body(d_ref, y_ref, w_ref, g_ref, dy_ref, dg_ref, dw_ref):
        i = pl.program_id(0)

        @pl.when(i == 0)
        def _():
            dg_ref[...] = jnp.zeros_like(dg_ref)
            dw_ref[...] = jnp.zeros_like(dw_ref)

        yv, dv = y_ref[...], d_ref[...]
        r = lax.rsqrt(jnp.mean(yv * yv, axis=-1, keepdims=True) + EPS)
        yh = yv * r
        dg_ref[...] += _colsum(dv * (yh * w_ref[...]))
        dn = dv * g_ref[...]
        dw_ref[...] += _colsum(dn * yh)
        dyh = dn * w_ref[...]
        dy_ref[...] = (r * (dyh - yh * jnp.mean(dyh * yh, axis=-1, keepdims=True))).astype(BF16)

    return pl.pallas_call(
        body, name=name, grid=(S // tb,), in_specs=[_row_spec(tb, Dm)] * 2 + [_vec_spec(Dm)] * 2,
        out_specs=(_row_spec(tb, Dm), _vec_spec(Dm), _vec_spec(Dm)),
        out_shape=(jax.ShapeDtypeStruct((S, Dm), BF16), jax.ShapeDtypeStruct((1, Dm), F32),
                   jax.ShapeDtypeStruct((1, Dm), F32)),
        compiler_params=_params("arbitrary"))(dxo, y, w, g)


def _pre_bwd(dh, x, w, sc, dres, name):
    S, Dm = x.shape
    tb = _blk(S, 512)

    def body(dh_ref, x_ref, w_ref, sc_ref, dr_ref, dx_ref, dsh_ref, dsc_ref, dw_ref):
        i = pl.program_id(0)

        @pl.when(i == 0)
        def _():
            dsh_ref[...] = jnp.zeros_like(dsh_ref)
            dsc_ref[...] = jnp.zeros_like(dsc_ref)
            dw_ref[...] = jnp.zeros_like(dw_ref)

        xv, dv = x_ref[...], dh_ref[...]
        r = lax.rsqrt(jnp.mean(xv * xv, axis=-1, keepdims=True) + EPS)
        xh = xv * r
        one_sc = 1.0 + sc_ref[...]
        dsh_ref[...] += _colsum(dv)
        dsc_ref[...] += _colsum(dv * (xh * w_ref[...]))
        dw_ref[...] += _colsum(dv * one_sc * xh)
        dxh = dv * one_sc * w_ref[...]
        dx_ref[...] = dr_ref[...] + r * (dxh - xh * jnp.mean(dxh * xh, axis=-1, keepdims=True))

    vec = jax.ShapeDtypeStruct((1, Dm), F32)
    return pl.pallas_call(
        body, name=name, grid=(S // tb,),
        in_specs=[_row_spec(tb, Dm)] * 2 + [_vec_spec(Dm)] * 2 + [_row_spec(tb, Dm)],
        out_specs=(_row_spec(tb, Dm), _vec_spec(Dm), _vec_spec(Dm), _vec_spec(Dm)),
        out_shape=(jax.ShapeDtypeStruct((S, Dm), F32), vec, vec, vec),
        compiler_params=_params("arbitrary"))(dh, x, w, sc, dres)


D_PROJ_ANY = pl.BlockSpec(memory_space=pl.ANY)


def _gate_specs(tm, Dm):
    return (pl.BlockSpec((tm, Dm), lambda i: (i, C_GS // Dm)), pl.BlockSpec((tm, Dm), lambda i: (i, C_GG // Dm)))


def _gdn_up_merge(ogn, w_gu, ys, proj, name):
    S, K = ogn.shape
    Dm = ys.shape[1]
    tm = _blk(S, 512)
    row = _row_spec(tm, Dm)

    def epilogue(r, ex, out):
        ys_ref, gs_ref, gg_ref = ex
        out[0][...] = r
        out[1][...] = (_sigmoid(gs_ref[...]) * ys_ref[...] + _sigmoid(gg_ref[...]) * r).astype(BF16)

    gs_spec, gg_spec = _gate_specs(tm, Dm)
    return _mm_rows(ogn, w_gu, S, Dm, K, mode="nn", name=name, tm=tm,
                    extras=[(ys, row), (proj, gs_spec), (proj, gg_spec)],
                    out_shapes=[jax.ShapeDtypeStruct((S, Dm), F32), jax.ShapeDtypeStruct((S, Dm), BF16)],
                    out_specs=[row, row], epilogue=epilogue)


def _mix_out_post_pre(merged, w_o, x, w_post, g, w_pre, sc, sh, name):
    S, Dm = x.shape
    tm = _blk(S, 512)
    row, vec = _row_spec(tm, Dm), _vec_spec(Dm)

    def epilogue(r, ex, out):
        x_ref, wpost_ref, g_ref, wpre_ref, sc_ref, sh_ref = ex
        out[0][...] = r
        rr = lax.rsqrt(jnp.mean(r * r, axis=-1, keepdims=True) + EPS)
        x1 = x_ref[...] + g_ref[...] * (r * rr * wpost_ref[...])
        out[1][...] = x1
        r1 = lax.rsqrt(jnp.mean(x1 * x1, axis=-1, keepdims=True) + EPS)
        out[2][...] = ((x1 * r1 * wpre_ref[...]) * (1.0 + sc_ref[...]) + sh_ref[...]).astype(BF16)

    return _mm_rows(merged, w_o, S, Dm, Dm, mode="nn", name=name, tm=tm,
                    extras=[(x, row), (w_post, vec), (g, vec), (w_pre, vec), (sc, vec), (sh, vec)],
                    out_shapes=[jax.ShapeDtypeStruct((S, Dm), F32), jax.ShapeDtypeStruct((S, Dm), F32),
                                jax.ShapeDtypeStruct((S, Dm), BF16)], out_specs=[row, row, row], epilogue=epilogue)


def _mix_out_dx_merge_bwd(dmo, w_o, ys, yg, proj, d_proj, name):
    S, Dm = ys.shape
    tm = _blk(S, 512)
    row = _row_spec(tm, Dm)

    def epilogue(d, ex, out):
        ys_ref, yg_ref, gs_ref, gg_ref, _ = ex
        ss, sg = _sigmoid(gs_ref[...]), _sigmoid(gg_ref[...])
        out[0][...] = (d * ss).astype(BF16)
        out[1][...] = (d * sg).astype(BF16)
        out[2][:, :Dm] = (d * ys_ref[...] * ss * (1.0 - ss)).astype(BF16)
        out[2][:, Dm:] = (d * yg_ref[...] * sg * (1.0 - sg)).astype(BF16)

    gs_spec, gg_spec = _gate_specs(tm, Dm)
    return _mm_rows(dmo, w_o, S, Dm, Dm, mode="nt", name=name, tm=tm,
                    extras=[(ys, row), (yg, row), (proj, gs_spec), (proj, gg_spec), (d_proj, D_PROJ_ANY)],
                    out_shapes=[jax.ShapeDtypeStruct((S, Dm), BF16), jax.ShapeDtypeStruct((S, Dm), BF16),
                                jax.ShapeDtypeStruct(d_proj.shape, BF16)],
                    out_specs=[row, row, pl.BlockSpec((tm, 2 * Dm), lambda i: (i, C_GS // (2 * Dm)))],
                    epilogue=epilogue, aliases={6: 2})


CONV_COLS = 128
CONV_BWD_ROWS = 128


def _taps_down(x):
    rows = _iota(x.shape[0], x.shape[1], 0)
    return [x] + [jnp.where(rows >= k, pltpu.roll(x, k, 0), 0.0) for k in range(1, CONV_K)]


def _conv_pre(taps, w_ref, b_ref):
    pre = taps[0] * w_ref[CONV_K - 1:CONV_K, :] + b_ref[...]
    for k in range(1, CONV_K):
        pre = pre + taps[k] * w_ref[CONV_K - 1 - k:CONV_K - k, :]
    return pre


def _conv_dx(dpre, w_ref):
    n = dpre.shape[0]
    rows = _iota(n, dpre.shape[1], 0)
    dx = dpre * w_ref[CONV_K - 1:CONV_K, :]
    for k in range(1, CONV_K):
        dx = dx + jnp.where(rows < n - k, pltpu.roll(dpre, n - k, 0), 0.0) * w_ref[CONV_K - 1 - k:CONV_K - k, :]
    return dx


def _conv_fwd(proj, w, b, name):
    S = proj.shape[0]
    n = w.shape[1]
    cb = CONV_COLS

    def body(x_ref, w_ref, b_ref, o_ref):
        o_ref[...] = _silu(_conv_pre(_taps_down(x_ref[...]), w_ref, b_ref))

    return pl.pallas_call(
        body, name=name, grid=(n // cb,),
        in_specs=[pl.BlockSpec((S, cb), lambda j: (0, j + C_XBC // cb)), pl.BlockSpec((CONV_K, cb), lambda j: (0, j)),
                  pl.BlockSpec((1, cb), lambda j: (0, j))],
        out_specs=pl.BlockSpec((S, cb), lambda j: (0, j)), out_shape=jax.ShapeDtypeStruct((S, n), F32),
        compiler_params=_params("parallel"))(proj, w, b)


def _conv_bwd(dact, proj, w, b, col0, d_proj, name):
    S, n = dact.shape
    cb = CONV_COLS
    o = col0 // cb

    R, HALO = _blk(S, CONV_BWD_ROWS), 8
    n_chunks = S // R

    def body(d_ref, x_ref, w_ref, b_ref, _, dx_ref, dw_ref, db_ref):
        def chunk(r0, first, last, sums):
            lo, hi = (0 if first else HALO), (0 if last else HALO)
            start = r0 - lo if isinstance(r0, int) else pl.multiple_of(r0 - lo, HALO)
            xe = x_ref[pl.ds(start, lo + R + hi), :]
            rows = _iota(lo + R + hi, cb, 0)
            taps = [xe[lo:, :]]
            for k in range(1, CONV_K):
                t = pltpu.roll(xe, k, 0)
                taps.append((jnp.where(rows >= k, t, 0.0) if first else t)[lo:, :])
            dpre_e = d_ref[pl.ds(r0, R + hi), :] * _dsilu(_conv_pre(taps, w_ref, b_ref))
            dpre = dpre_e[0:R, :]
            db, dw = sums
            db = db + _colsum(dpre)
            dw = [dw[k] + _colsum(dpre * taps[k][0:R, :]) for k in range(CONV_K)]
            rows_e = _iota(R + hi, cb, 0)
            dx = dpre * w_ref[CONV_K - 1:CONV_K, :]
            for k in range(1, CONV_K):
                t = pltpu.roll(dpre_e, R + hi - k, 0)
                t = jnp.where(rows_e < R - k, t, 0.0) if last else t
                dx = dx + t[0:R, :] * w_ref[CONV_K - 1 - k:CONV_K - k, :]
            dx_ref[pl.ds(r0, R), :] = dx.astype(BF16)
            return db, dw

        zero = jnp.zeros((1, cb), F32)
        sums = chunk(0, True, n_chunks == 1, (zero, [zero] * CONV_K))
        if n_chunks > 2:
            def step(i, carry):
                db, dw = chunk(pl.multiple_of(i * R, R), False, False, (carry[0], list(carry[1:])))
                return (db,) + tuple(dw)
            carry = lax.fori_loop(1, n_chunks - 1, step, (sums[0],) + tuple(sums[1]))
            sums = (carry[0], list(carry[1:]))
        if n_chunks > 1:
            sums = chunk((n_chunks - 1) * R, False, True, sums)
        db_ref[...] = sums[0]
        for k in range(CONV_K):
            dw_ref[CONV_K - 1 - k:CONV_K - k, :] = sums[1][k]

    return pl.pallas_call(
        body, name=name, grid=(n // cb,),
        in_specs=[pl.BlockSpec((S, cb), lambda j: (0, j)), pl.BlockSpec((S, cb), lambda j: (0, j + o + C_XBC // cb)),
                  pl.BlockSpec((CONV_K, cb), lambda j: (0, j + o)), pl.BlockSpec((1, cb), lambda j: (0, j + o)),
                  D_PROJ_ANY],
        out_specs=(pl.BlockSpec((S, cb), lambda j: (0, j + o + C_XBC // cb)),
                   pl.BlockSpec((CONV_K, cb), lambda j: (0, j)), pl.BlockSpec((1, cb), lambda j: (0, j))),
        out_shape=(jax.ShapeDtypeStruct(d_proj.shape, BF16), jax.ShapeDtypeStruct((CONV_K, n), F32),
                   jax.ShapeDtypeStruct((1, n), F32)),
        input_output_aliases={4: 0}, compiler_params=_params("parallel"))(dact, proj, w, b, d_proj)


def _ssd_specs(L, order):
    G = SSM_GROUPS_PER_STEP
    W, N = G * SSM_GROUP_WIDTH, G * SSM_D_STATE
    x_spec = pl.BlockSpec((L, W), lambda g, c: (order(c), g))
    b_spec = pl.BlockSpec((L, N), lambda g, c: (order(c), 2048 // N + g))
    c_spec = pl.BlockSpec((L, N), lambda g, c: (order(c), 3072 // N + g))
    z_spec = pl.BlockSpec((L, W), lambda g, c: (order(c), C_ZS // W + g))
    dt_spec = pl.BlockSpec((G, L, SSM_HEADS_PER_GROUP), lambda g, c: (g, order(c), 0))
    p_spec = pl.BlockSpec((G, 3, SSM_HEADS_PER_GROUP), lambda g, c: (g, 0, 0))
    nw_spec = pl.BlockSpec((G, 1, SSM_GROUP_WIDTH), lambda g, c: (g, 0, 0))
    s_spec = pl.BlockSpec((G, 1, SSM_GROUP_WIDTH, SSM_D_STATE), lambda g, c: (g, order(c), 0, 0))
    return x_spec, b_spec, c_spec, z_spec, dt_spec, p_spec, nw_spec, s_spec


class _SsdGroup:
    def __init__(self, L):
        P, H, W = SSM_HEAD_DIM, SSM_HEADS_PER_GROUP, SSM_GROUP_WIDTH
        self.L = L
        self.ii, self.jj = _iota(L, L, 0), _iota(L, L, 1)
        self.lower = jnp.where(self.ii >= self.jj, 1.0, 0.0).astype(BF16)
        self.upper = jnp.where(self.ii <= self.jj, 1.0, 0.0).astype(BF16)
        self.lo = _iota(L, 2 * P, 1) < P
        self.lo_row = _iota(1, 2 * P, 1) < P
        bi, bj = _iota(W, W, 0), _iota(W, W, 1)
        self.block = jnp.where(bi // P == bj // P, 1.0, 0.0).astype(BF16)
        si, sj = _iota(2 * P, W, 0), _iota(2 * P, W, 1)
        self.pick = jnp.where(sj == si * P, 1.0, 0.0).astype(BF16)
        self.ones = jnp.ones((L, 2 * P), BF16)

    def spread(self, v4):
        R = v4.shape[0]
        lo = self.lo if R == self.L else self.lo_row
        b = lambda h: jnp.broadcast_to(v4[:, h:h + 1], (R, 2 * SSM_HEAD_DIM))
        return jnp.concatenate([jnp.where(lo, b(0), b(1)), jnp.where(lo, b(2), b(3))], axis=1)

    def gather4(self, v):
        return jnp.concatenate([v[:, h * SSM_HEAD_DIM:h * SSM_HEAD_DIM + 1] for h in range(SSM_HEADS_PER_GROUP)],
                               axis=1)

    def head_sums(self, z):
        return sum(lax.dot_general(p, self.block, _NN, preferred_element_type=F32) for p in _parts(z, 2))

    def pair_cols(self, full, pair):
        ps = full[:, pair * 128:(pair + 1) * 128]
        sw = pltpu.roll(ps, SSM_HEAD_DIM, 1)
        return jnp.where(self.lo, ps, sw), jnp.where(self.lo, sw, ps)

    def gates(self, dt4_raw, p):
        L = self.L
        dtr = self.spread(dt4_raw + p[0:1, :])
        dt = _softplus(dtr)
        A = self.spread(-jnp.exp(p[1:2, :]))
        acum = _sum_by(self.lower, dt * A)
        yield
        rows = _sum_by(self.pick, acum, _NT)
        yield
        a_last = acum[L - 1:L, :]
        cols = self.pair_cols(acum, 0) + self.pair_cols(acum, 1)
        decay, decay_t = [], []
        for h in range(SSM_HEADS_PER_GROUP):
            seg = cols[h] - rows[h:h + 1, :]
            decay.append(jnp.exp(jnp.where(self.ii >= self.jj, seg, NEG_INF)))
            decay_t.append(jnp.exp(jnp.where(self.jj >= self.ii, -seg, NEG_INF)))
        return dict(dtr=dtr, dt=dt, A=A, D=self.spread(p[2:3, :]), acum=acum, eac=jnp.exp(acum), a_last=a_last,
                    wdec=jnp.exp(a_last - acum), decay=decay, decay_t=decay_t,
                    ea_last=[jnp.exp(rows[h:h + 1, L - 1:L]) for h in range(SSM_HEADS_PER_GROUP)])


def _ssd_fwd(conv, proj, dt_raw, pvec, nw, name):
    S = conv.shape[0]
    L, P, N, H, W, G = SSM_CHUNK, SSM_HEAD_DIM, SSM_D_STATE, SSM_HEADS_PER_GROUP, SSM_GROUP_WIDTH, SSM_GROUPS_PER_STEP
    nc = S // L

    def body(x_ref, b_ref, c_ref, z_ref, dt_ref, p_ref, nw_ref, y_ref, yn_ref, s0_ref, state):
        c = pl.program_id(1)

        @pl.when(c == 0)
        def _():
            state[...] = jnp.zeros_like(state)

        k = _SsdGroup(L)

        def group(gi):
            gsl = slice(gi * W, (gi + 1) * W)
            Bm, Cm = b_ref[:, gi * N:(gi + 1) * N], c_ref[:, gi * N:(gi + 1) * N]
            x = x_ref[:, gsl]
            S0 = state[gsl, :]
            s0_ref[gi, 0] = S0
            CB = _dot(Cm, Bm, _NT)
            y_off = _dot(Cm, S0, _NT)
            t = yield from k.gates(dt_ref[gi], p_ref[gi])
            xdt = x * t["dt"]
            s_new = _dot(xdt * t["wdec"], Bm, _TN)
            y_diag = []
            for pair in range(H // 2):
                xp = xdt[:, pair * 128:(pair + 1) * 128]
                y_diag.append(jnp.where(k.lo, _dot(CB * t["decay"][2 * pair], xp),
                                        _dot(CB * t["decay"][2 * pair + 1], xp)))
            yield
            y = jnp.concatenate(y_diag, axis=1) + y_off * t["eac"]
            for h in range(H):
                hsl = slice(gi * W + h * P, gi * W + (h + 1) * P)
                state[hsl, :] = S0[h * P:(h + 1) * P, :] * t["ea_last"][h] + s_new[h * P:(h + 1) * P, :]
            y_ref[:, gsl] = y
            y2 = (y + t["D"] * x) * _silu(z_ref[:, gsl])
            r = lax.rsqrt(jnp.mean(y2 * y2, axis=-1, keepdims=True) + EPS)
            yn_ref[:, gsl] = (y2 * r * nw_ref[gi]).astype(BF16)

        _lockstep(group(gi) for gi in range(G))

    x_spec, b_spec, c_spec, z_spec, dt_spec, p_spec, nw_spec, s_spec = _ssd_specs(L, lambda c: c)
    y_spec = pl.BlockSpec((L, G * W), lambda g, c: (c, g))
    return pl.pallas_call(
        body, name=name, grid=(SSM_GROUPS // G, nc),
        in_specs=[x_spec, b_spec, c_spec, z_spec, dt_spec, p_spec, nw_spec],
        out_specs=(y_spec, y_spec, s_spec),
        out_shape=(jax.ShapeDtypeStruct((S, SSM_GROUPS * W), F32), jax.ShapeDtypeStruct((S, SSM_GROUPS * W), BF16),
                   jax.ShapeDtypeStruct((SSM_GROUPS, nc, W, N), F32)),
        scratch_shapes=[pltpu.VMEM((G * W, N), F32)],
        compiler_params=_params("parallel", "arbitrary"))(conv, conv, conv, proj, dt_raw, pvec, nw)


def _ssd_bwd(dyn, conv, proj, dt_raw, pvec, nw, y_ssd, states, d_proj, name):
    S = conv.shape[0]
    L, P, N, H, W, G = SSM_CHUNK, SSM_HEAD_DIM, SSM_D_STATE, SSM_HEADS_PER_GROUP, SSM_GROUP_WIDTH, SSM_GROUPS_PER_STEP
    nc = S // L

    def body(dyn_ref, x_ref, b_ref, c_ref, z_ref, dt_ref, p_ref, nw_ref, y_ref, s0_ref, _,
             dx_ref, db_ref, dc_ref, dz_ref, ddt_ref, dp_ref, dnw_ref, dstate):
        c = pl.program_id(1)

        @pl.when(c == 0)
        def _():
            dstate[...] = jnp.zeros_like(dstate)
            dp_ref[...] = jnp.zeros_like(dp_ref)
            dnw_ref[...] = jnp.zeros_like(dnw_ref)

        k = _SsdGroup(L)
        last = (_iota(L, 1, 0) == L - 1)

        def group(gi):
            gsl = slice(gi * W, (gi + 1) * W)
            Bm, Cm = b_ref[:, gi * N:(gi + 1) * N], c_ref[:, gi * N:(gi + 1) * N]
            x, z = x_ref[:, gsl], z_ref[:, gsl]
            S0, dS1 = s0_ref[gi, 0], dstate[gsl, :]
            CB = _dot(Cm, Bm, _NT)
            CBt = _dot(Bm, Cm, _NT)
            y_off_raw = _dot(Cm, S0, _NT)
            dXs_raw = _dot(Bm, dS1, _NT)
            t = yield from k.gates(dt_ref[gi], p_ref[gi])
            y1 = y_ref[:, gsl] + t["D"] * x
            sz = _silu(z)
            y2 = y1 * sz
            r = lax.rsqrt(jnp.mean(y2 * y2, axis=-1, keepdims=True) + EPS)
            y2h = y2 * r
            dyn_v = dyn_ref[:, gsl]
            dnw_ref[gi] += _colsum(dyn_v * y2h)
            dy2h = dyn_v * nw_ref[gi]
            dy2 = r * (dy2h - y2h * jnp.mean(dy2h * y2h, axis=-1, keepdims=True))
            dz_ref[:, gsl] = (dy2 * y1 * _dsilu(z)).astype(BF16)
            dY = dy2 * sz
            X = x * t["dt"]
            dYe = dY * t["eac"]
            dC_s = _dot(dYe, S0)
            dB_s = _dot(X * t["wdec"], dS1)
            dS_c = _dot(dYe, Cm, _TN)
            dXm, Gs, Gts = [], [], []
            for pair in range(H // 2):
                dYp, Xp = dY[:, pair * 128:(pair + 1) * 128], X[:, pair * 128:(pair + 1) * 128]
                dXm.append(jnp.where(k.lo, _dot(CBt * t["decay_t"][2 * pair], dYp),
                                     _dot(CBt * t["decay_t"][2 * pair + 1], dYp)))
                for mask in (k.lo, ~k.lo):
                    Gs.append(_dot(jnp.where(mask, dYp, 0.0), Xp, _NT))
                    Gts.append(_dot(jnp.where(mask, Xp, 0.0), dYp, _NT))
            yield
            dXs = dXs_raw * t["wdec"]
            dX = jnp.concatenate(dXm, axis=1) + dXs
            dCB, dCBt, q_sums = 0.0, 0.0, []
            for h in range(H):
                M, Mt = CB * t["decay"][h], CBt * t["decay_t"][h]
                dCB = dCB + Gs[h] * t["decay"][h]
                dCBt = dCBt + Gts[h] * t["decay_t"][h]
                d = Gs[h] * M - Gts[h] * Mt
                q_sums.append(sum(lax.dot_general(pt, k.ones, _NN, preferred_element_type=F32)
                                  for pt in _parts(d, 2)))
            q_f = jnp.concatenate([jnp.where(k.lo, q_sums[0], q_sums[1]), jnp.where(k.lo, q_sums[2], q_sums[3])],
                                  axis=1)
            x_dxs = k.head_sums(X * dXs)
            tot = [_total(dS1[h * P:(h + 1) * P, :] * S0[h * P:(h + 1) * P, :]) * t["ea_last"][h] for h in range(H)]
            tot_f = k.spread(jnp.concatenate(tot, axis=1))
            d_alast = _colsum(x_dxs) + tot_f
            dacum = q_f + k.head_sums(dY * (y_off_raw * t["eac"])) - x_dxs + jnp.where(last, d_alast, 0.0)
            dx_dt = k.head_sums(dX * x)
            d_skip = _colsum(k.head_sums(dY * x))
            for h in range(H):
                hsl = slice(gi * W + h * P, gi * W + (h + 1) * P)
                dstate[hsl, :] = t["ea_last"][h] * dS1[h * P:(h + 1) * P, :] + dS_c[h * P:(h + 1) * P, :]
            dc_s2 = _dot(dCB, Bm)
            db_s2 = _dot(dCBt, Cm)
            yield
            da = _sum_by(k.upper, dacum)
            yield
            ddt_raw = (da * t["A"] + dx_dt) * _sigmoid(t["dtr"])
            dx_ref[:, gsl] = dX * t["dt"] + t["D"] * dY
            dc_ref[:, gi * N:(gi + 1) * N] = dC_s + dc_s2
            db_ref[:, gi * N:(gi + 1) * N] = dB_s + db_s2
            ddt_ref[gi] = k.gather4(ddt_raw)
            dp_ref[gi] += k.gather4(jnp.concatenate([_colsum(ddt_raw), _colsum(da * t["dt"]) * t["A"], d_skip],
                                                    axis=0))

        _lockstep(group(gi) for gi in range(G))

    rev = lambda c: nc - 1 - c
    x_spec, b_spec, c_spec, z_spec, dt_spec, p_spec, nw_spec, s_spec = _ssd_specs(L, rev)
    y_spec = pl.BlockSpec((L, G * W), lambda g, c: (rev(c), g))
    n_spec = pl.BlockSpec((L, G * N), lambda g, c: (rev(c), g))
    return pl.pallas_call(
        body, name=name, grid=(SSM_GROUPS // G, nc),
        in_specs=[y_spec, x_spec, b_spec, c_spec, z_spec, dt_spec, p_spec, nw_spec, y_spec, s_spec, D_PROJ_ANY],
        out_specs=(y_spec, n_spec, n_spec, z_spec, dt_spec, p_spec, nw_spec),
        out_shape=(jax.ShapeDtypeStruct((S, SSM_GROUPS * W), F32), jax.ShapeDtypeStruct((S, SSM_GROUPS * N), F32),
                   jax.ShapeDtypeStruct((S, SSM_GROUPS * N), F32), jax.ShapeDtypeStruct(d_proj.shape, BF16),
                   jax.ShapeDtypeStruct((SSM_GROUPS, S, H), F32), jax.ShapeDtypeStruct((SSM_GROUPS, 3, H), F32),
                   jax.ShapeDtypeStruct((SSM_GROUPS, 1, W), F32)),
        scratch_shapes=[pltpu.VMEM((G * W, N), F32)], input_output_aliases={10: 3},
        compiler_params=_params("parallel", "arbitrary"))(dyn, conv, conv, conv, proj, dt_raw, pvec, nw, y_ssd, states,
                                                          d_proj)


def _unit_lower_inverse(A, ii, jj):
    eye = (ii == jj).astype(F32)
    same = (ii // GDN_INV_BLOCK) == (jj // GDN_INV_BLOCK)
    Ad = jnp.where(same, A, 0.0)
    Ao = A - Ad
    P2 = _dot3(Ad, Ad)
    yield
    P4, X = _dot(P2, P2), _dot3(eye - Ad, eye + P2)
    yield
    P8, X = _dot(P4, P4), X + _dot2(X, P4)
    yield
    X = X + _dot2(X, P8)
    yield
    Bm = _dot3(X, Ao)
    yield
    B2 = _dot3(Bm, Bm)
    yield
    Y = (eye - Bm) + B2 - _dot2(Bm, B2)
    yield
    T = _dot3(Y, X)
    yield
    return T


def _gdn_specs(L, order):
    G = GDN_QK_PER_STEP
    Hd, W = G * GDN_HEAD, G * GDN_V_PER_QK * GDN_HEAD
    q_spec = pl.BlockSpec((L, Hd), lambda h, c: (order(c), (C_QKV - C_XBC) // Hd + h))
    k_spec = pl.BlockSpec((L, Hd), lambda h, c: (order(c), (C_QKV - C_XBC + 1024) // Hd + h))
    v_spec = pl.BlockSpec((L, W), lambda h, c: (order(c), (C_QKV - C_XBC + 2048) // W + h))
    z_spec = pl.BlockSpec((L, W), lambda h, c: (order(c), C_ZG // W + h))
    ba_spec = pl.BlockSpec((G, L, GDN_V_PER_QK), lambda h, c: (h, order(c), 0))
    p_spec = pl.BlockSpec((G, 2, GDN_V_PER_QK), lambda h, c: (h, 0, 0))
    nw_spec = pl.BlockSpec((1, GDN_HEAD), lambda h, c: (0, 0))
    s_spec = pl.BlockSpec((G, 1, GDN_V_PER_QK * GDN_HEAD, GDN_HEAD), lambda h, c: (h, order(c), 0, 0))
    t_spec = pl.BlockSpec((G * GDN_V_PER_QK, 1, L, L), lambda h, c: (h, order(c), 0, 0))
    return q_spec, k_spec, v_spec, z_spec, ba_spec, p_spec, nw_spec, s_spec, t_spec


def _gdn_gates(qa, ka, b_col, a_col, p, j, ii, jj):
    L = qa.shape[0]
    sp_in = a_col + p[0:1, j:j + 1]
    neg_ea = -jnp.exp(p[1:2, j:j + 1])
    g = neg_ea * _softplus(sp_in)
    gcum, gcum_row = _cumsum_forms(g, ii, jj)
    rq = lax.rsqrt(_rowsum(qa * qa) + EPS)
    rk = lax.rsqrt(_rowsum(ka * ka) + EPS)
    q = qa * rq * (GDN_HEAD ** -0.5)
    k = ka * rk
    beta = _sigmoid(b_col)
    yield
    Dm = jnp.exp(jnp.where(ii >= jj, gcum - gcum_row, NEG_INF))
    eg = jnp.exp(gcum)
    g_last = gcum[L - 1:L, :]
    wdec = jnp.exp(g_last - gcum)
    return dict(rq=rq, rk=rk, q=q, k=k, beta=beta, sp_in=sp_in, neg_ea=neg_ea, g=g, Dm=Dm, kbeta=k * beta, eg=eg,
                g_last=g_last, wdec=wdec, kdec=k * wdec)


def _gdn_fwd(conv, proj, b_raw, a_raw, pvec, nw, name):
    S = conv.shape[0]
    L, Hd, J, G = GDN_CHUNK, GDN_HEAD, GDN_V_PER_QK, GDN_QK_PER_STEP
    W = J * Hd
    nc = S // L

    def body(q_ref, k_ref, v_ref, z_ref, b_ref, a_ref, p_ref, nw_ref, o_ref, on_ref, s0_ref, t_ref, state):
        c = pl.program_id(1)

        @pl.when(c == 0)
        def _():
            state[...] = jnp.zeros_like(state)

        ii, jj = _iota(L, L, 0), _iota(L, L, 1)
        for hq in range(G):
            s0_ref[hq, 0] = state[hq * W:(hq + 1) * W, :]

        def head(hq, j):
            hd = hq * J + j
            hsl, sl = slice(hq * Hd, (hq + 1) * Hd), slice(hd * Hd, (hd + 1) * Hd)
            t = yield from _gdn_gates(q_ref[:, hsl], k_ref[:, hsl], b_ref[hq][:, j:j + 1], a_ref[hq][:, j:j + 1],
                                      p_ref[hq], j, ii, jj)
            KK = _dot(t["kbeta"], t["k"], _NT)
            QK = _dot(t["q"], t["k"], _NT)
            yield
            T = yield from _unit_lower_inverse(jnp.where(ii > jj, KK * t["Dm"], 0.0), ii, jj)
            t_ref[hd, 0] = T
            S0 = state[sl, :]
            U = _dot2(T, v_ref[:, sl] * t["beta"])
            Wm = _dot2(T, t["kbeta"] * t["eg"])
            o_inter = _dot(t["q"] * t["eg"], S0)
            yield
            Vn = U - _dot(Wm, S0)
            yield
            o = o_inter + _dot(QK * t["Dm"], Vn)
            s_new = _dot(t["kdec"], Vn, _TN)
            yield
            state[sl, :] = S0 * jnp.exp(t["g_last"]) + s_new
            o_ref[:, sl] = o
            r = lax.rsqrt(jnp.mean(o * o, axis=-1, keepdims=True) + EPS)
            on_ref[:, sl] = ((o * r * nw_ref[...]) * _silu(z_ref[:, sl])).astype(BF16)

        _lockstep(head(hq, j) for hq in range(G) for j in range(J))

    q_spec, k_spec, v_spec, z_spec, ba_spec, p_spec, nw_spec, s_spec, t_spec = _gdn_specs(L, lambda c: c)
    o_spec = pl.BlockSpec((L, G * W), lambda h, c: (c, h))
    return pl.pallas_call(
        body, name=name, grid=(GDN_QK_HEADS // G, nc),
        in_specs=[q_spec, k_spec, v_spec, z_spec, ba_spec, ba_spec, p_spec, nw_spec],
        out_specs=(o_spec, o_spec, s_spec, t_spec),
        out_shape=(jax.ShapeDtypeStruct((S, GDN_QK_HEADS * W), F32), jax.ShapeDtypeStruct((S, GDN_QK_HEADS * W), BF16),
                   jax.ShapeDtypeStruct((GDN_QK_HEADS, nc, W, Hd), F32),
                   jax.ShapeDtypeStruct((GDN_QK_HEADS * J, nc, L, L), F32)),
        scratch_shapes=[pltpu.VMEM((G * W, Hd), F32)],
        compiler_params=_params("parallel", "arbitrary"))(conv, conv, conv, proj, b_raw, a_raw, pvec, nw)


def _gdn_bwd(don, conv, proj, b_raw, a_raw, pvec, nw, o_pre, states, t_inv, d_proj, name):
    S = conv.shape[0]
    L, Hd, J, G = GDN_CHUNK, GDN_HEAD, GDN_V_PER_QK, GDN_QK_PER_STEP
    W = J * Hd
    nc = S // L

    def body(don_ref, q_ref, k_ref, v_ref, z_ref, b_ref, a_ref, p_ref, nw_ref, o_ref, s0_ref, t_ref, _,
             dq_ref, dk_ref, dv_ref, dz_ref, db_ref, da_ref, dp_ref, dnw_ref, dstate):
        c = pl.program_id(1)

        @pl.when(c == 0)
        def _():
            dstate[...] = jnp.zeros_like(dstate)
            dp_ref[...] = jnp.zeros_like(dp_ref)
            dnw_ref[...] = jnp.zeros_like(dnw_ref)

        ii, jj = _iota(L, L, 0), _iota(L, L, 1)
        last = (_iota(L, 1, 0) == L - 1)
        res = {}

        def head(hq, j):
            hd = hq * J + j
            hsl, sl = slice(hq * Hd, (hq + 1) * Hd), slice(hd * Hd, (hd + 1) * Hd)
            qa, ka = q_ref[:, hsl], k_ref[:, hsl]
            t = yield from _gdn_gates(qa, ka, b_ref[hq][:, j:j + 1], a_ref[hq][:, j:j + 1], p_ref[hq], j, ii, jj)
            q, k, beta, eg, Dm, kbeta, kdec = (t[nm] for nm in ("q", "k", "beta", "eg", "Dm", "kbeta", "kdec"))
            T = t_ref[hd, 0]
            v, z, o = v_ref[:, sl], z_ref[:, sl], o_ref[:, sl]
            S0, dS1 = s0_ref[hq, 0, j * Hd:(j + 1) * Hd, :], dstate[sl, :]
            sz = _silu(z)
            r = lax.rsqrt(jnp.mean(o * o, axis=-1, keepdims=True) + EPS)
            oh = o * r
            d_on = don_ref[:, sl]
            dz_ref[:, sl] = (d_on * (oh * nw_ref[...]) * _dsilu(z)).astype(BF16)
            dn = d_on * sz
            dnw_part = _colsum(dn * oh)
            doh = dn * nw_ref[...]
            dO = r * (doh - oh * jnp.mean(doh * oh, axis=-1, keepdims=True))
            Rw = kbeta * eg
            qe = q * eg
            U = _dot2(T, v * beta)
            Wm = _dot2(T, Rw)
            KK = _dot(kbeta, k, _NT)
            QK = _dot(q, k, _NT)
            o_inter = _dot(qe, S0)
            dq_s = _dot(dO, S0, _NT)
            dS_q = _dot(qe, dO, _TN)
            yield
            Am = jnp.where(ii > jj, KK * Dm, 0.0)
            Pm = QK * Dm
            Vn = U - _dot(Wm, S0)
            dVn_s = _dot(kdec, dS1)
            yield
            dVn = _dot(Pm, dO, _TN) + dVn_s
            dP = _dot(dO, Vn, _NT)
            dKd = _dot(Vn, dS1, _NT)
            yield
            dQK = dP * Dm
            dq = _dot(dQK, k) + dq_s * eg
            dk = _dot(dQK, q, _TN) + dKd * t["wdec"]
            dstate[sl, :] = jnp.exp(t["g_last"]) * dS1 + dS_q - _dot(Wm, dVn, _TN)
            dW = -_dot(dVn, S0, _NT)
            dRu = _dot2(T, dVn, _TN)
            yield
            dRw = _dot2(T, dW, _TN)
            dA_u = _dot(dRu, U, _NT)
            yield
            dA = jnp.where(ii > jj, -(dA_u + _dot(dRw, Wm, _NT)), 0.0)
            yield
            dKK = dA * Dm
            dkbeta = _dot(dKK, k) + dRw * eg
            dk = dk + _dot(dKK, kbeta, _TN)
            yield
            dk = dk + dkbeta * beta
            dbeta = _rowsum(dkbeta * k) + _rowsum(dRu * v)
            dv_ref[:, sl] = dRu * beta
            Q = dA * Am + dP * Pm
            rho = _rowsum(dKd * kdec)
            d_glast = _colsum(rho) + jnp.exp(t["g_last"]) * _total(dS1 * S0)
            q_sums = _row_col_sums(Q)
            rest = _rowsum(dRw * Rw) + _rowsum(dO * o_inter) - rho + jnp.where(last, d_glast, 0.0)
            yield
            dg = _rev_cumsum_col(q_sums + rest, ii, jj)
            yield
            da_raw = dg * t["neg_ea"] * _sigmoid(t["sp_in"])
            res[hq, j] = dict(dq=dq, dk=dk, db=dbeta * beta * (1.0 - beta), da=da_raw, d_bias=_colsum(da_raw),
                              d_alog=_colsum(dg * t["g"]), dnw=dnw_part, rq=t["rq"], rk=t["rk"], k=k, qh=qa * t["rq"])

        _lockstep(head(hq, j) for hq in range(G) for j in range(J))
        for hq in range(G):
            parts = [res[hq, j] for j in range(J)]
            hsl = slice(hq * Hd, (hq + 1) * Hd)
            p0 = parts[0]
            dqh = sum(pt["dq"] for pt in parts) * (GDN_HEAD ** -0.5)
            dkn = sum(pt["dk"] for pt in parts)
            dq_ref[:, hsl] = p0["rq"] * (dqh - p0["qh"] * _rowsum(dqh * p0["qh"]))
            dk_ref[:, hsl] = p0["rk"] * (dkn - p0["k"] * _rowsum(dkn * p0["k"]))
            db_ref[hq] = jnp.concatenate([pt["db"] for pt in parts], axis=1)
            da_ref[hq] = jnp.concatenate([pt["da"] for pt in parts], axis=1)
            dp_ref[hq] += jnp.concatenate([jnp.concatenate([pt["d_bias"] for pt in parts], axis=1),
                                           jnp.concatenate([pt["d_alog"] for pt in parts], axis=1)], axis=0)
            dnw_ref[hq] += sum(pt["dnw"] for pt in parts)

    rev = lambda c: nc - 1 - c
    q_spec, k_spec, v_spec, z_spec, ba_spec, p_spec, nw_spec, s_spec, t_spec = _gdn_specs(L, rev)
    o_spec = pl.BlockSpec((L, G * W), lambda h, c: (rev(c), h))
    h_spec = pl.BlockSpec((L, G * Hd), lambda h, c: (rev(c), h))
    dnw_spec = pl.BlockSpec((G, 1, Hd), lambda h, c: (h, 0, 0))
    return pl.pallas_call(
        body, name=name, grid=(GDN_QK_HEADS // G, nc),
        in_specs=[o_spec, q_spec, k_spec, v_spec, z_spec, ba_spec, ba_spec, p_spec, nw_spec, o_spec, s_spec, t_spec,
                  D_PROJ_ANY],
        out_specs=(h_spec, h_spec, o_spec, z_spec, ba_spec, ba_spec, p_spec, dnw_spec),
        out_shape=(jax.ShapeDtypeStruct((S, GDN_QK_HEADS * Hd), F32), jax.ShapeDtypeStruct((S, GDN_QK_HEADS * Hd), F32),
                   jax.ShapeDtypeStruct((S, GDN_QK_HEADS * W), F32), jax.ShapeDtypeStruct(d_proj.shape, BF16),
                   jax.ShapeDtypeStruct((GDN_QK_HEADS, S, J), F32), jax.ShapeDtypeStruct((GDN_QK_HEADS, S, J), F32),
                   jax.ShapeDtypeStruct((GDN_QK_HEADS, 2, J), F32), jax.ShapeDtypeStruct((GDN_QK_HEADS, 1, Hd), F32)),
        scratch_shapes=[pltpu.VMEM((G * W, Hd), F32)], input_output_aliases={12: 3},
        compiler_params=_params("parallel", "arbitrary"))(don, conv, conv, conv, proj, b_raw, a_raw, pvec, nw, o_pre,
                                                          states, t_inv, d_proj)


def _ada_fwd(c_all, w_loc, b_loc, name):
    n = w_loc.shape[1]

    def body(c_ref, w_ref, b_ref, o_ref):
        o_ref[...] = _dot3(_silu(c_ref[...]), w_ref[...]) + b_ref[...]

    return pl.pallas_call(body, name=name, out_shape=jax.ShapeDtypeStruct((N_DEV, n), F32),
                          compiler_params=pltpu.CompilerParams(vmem_limit_bytes=VMEM_LIMIT))(c_all, w_loc, b_loc)


def _ada_bwd(c_all_t, dmod_cols, name):
    Dm, n = c_all_t.shape[0], dmod_cols.shape[1]

    def body(c_ref, d_ref, o_ref):
        ca = _silu(c_ref[...])
        acc = ca[:, 0:1] * d_ref[0:1, :]
        for i in range(1, N_DEV):
            acc = acc + ca[:, i:i + 1] * d_ref[i:i + 1, :]
        o_ref[...] = acc

    return pl.pallas_call(body, name=name, out_shape=jax.ShapeDtypeStruct((Dm, n), F32),
                          compiler_params=pltpu.CompilerParams(vmem_limit_bytes=VMEM_LIMIT))(c_all_t, dmod_cols)


ADAM_BLOCK_BYTES = 12 * 1024 * 1024


def _adam(contrib, w, m, v, name):
    n, R, C = contrib.shape
    tr = R
    while tr % 16 == 0 and (n + 7) * tr * C * 4 > ADAM_BLOCK_BYTES:
        tr //= 2

    def body(c_ref, w_ref, m_ref, v_ref, g_ref, d_ref, nm_ref, nv_ref):
        g = c_ref[0].astype(F32)
        for i in range(1, n):
            g = g + c_ref[i].astype(F32)
        nm = ADAM_B1 * m_ref[...] + (1.0 - ADAM_B1) * g
        nv = ADAM_B2 * v_ref[...] + (1.0 - ADAM_B2) * (g * g)
        m_hat = nm / (1.0 - ADAM_B1 ** ADAM_STEP)
        v_hat = nv / (1.0 - ADAM_B2 ** ADAM_STEP)
        g_ref[...] = g
        d_ref[...] = -ADAM_LR * (m_hat / (jnp.sqrt(v_hat) + ADAM_EPS) + ADAM_WD * w_ref[...])
        nm_ref[...] = nm
        nv_ref[...] = nv

    spec = pl.BlockSpec((tr, C), lambda i: (i, 0))
    shp = jax.ShapeDtypeStruct((R, C), F32)
    return pl.pallas_call(
        body, name=name, grid=(R // tr,), in_specs=[pl.BlockSpec((n, tr, C), lambda i: (0, i, 0)), spec, spec, spec],
        out_specs=(spec,) * 4, out_shape=(shp,) * 4, compiler_params=_params("parallel"))(contrib, w, m, v)


def _exchange(arrays, modes, name, chips=False):
    n = len(arrays)
    out_shape = tuple(jax.ShapeDtypeStruct((N_DEV,) + a.shape if md == "gather" else a.shape, a.dtype)
                      for a, md in zip(arrays, modes))

    def body(*refs):
        ins, outs = refs[:n], refs[n:2 * n]
        send_sems, recv_sems, loc_sems = refs[2 * n:]
        me, peers = _peer_table(chips)

        def src(k, slot):
            return ins[k] if modes[k] == "gather" else ins[k].at[slot]

        def remote(k, m, to_slot, land_slot):
            return pltpu.make_async_remote_copy(
                src_ref=src(k, to_slot), dst_ref=outs[k].at[land_slot], send_sem=send_sems.at[k, m],
                recv_sem=recv_sems.at[k, m], device_id=peers[m][0], device_id_type=pl.DeviceIdType.MESH)

        local = [pltpu.make_async_copy(src(k, me), outs[k].at[me], loc_sems.at[k]) for k in range(n)]
        for cp in local:
            cp.start()
        sends = [remote(k, m, peers[m][1], me) for m in range(len(peers)) for k in range(n)]
        for cp in sends:
            cp.start()
        for m in range(len(peers)):
            for k in range(n):
                remote(k, m, peers[m][1], peers[m][1]).wait_recv()
        for cp in sends:
            cp.wait_send()
        for cp in local:
            cp.wait()

    any_spec = pl.BlockSpec(memory_space=pl.ANY)
    return pl.pallas_call(
        body, name=name, in_specs=[any_spec] * n, out_specs=(any_spec,) * n, out_shape=out_shape,
        scratch_shapes=[pltpu.SemaphoreType.DMA((n, N_DEV - 1)), pltpu.SemaphoreType.DMA((n, N_DEV - 1)),
                        pltpu.SemaphoreType.DMA((n,))])(*arrays)


def _gather_two_level(arrays, name):
    n = len(arrays)
    out_shape = tuple(jax.ShapeDtypeStruct((N_DEV,) + a.shape, a.dtype) for a in arrays)

    def body(*refs):
        ins, outs = refs[:n], refs[n:2 * n]
        send_sems, recv_sems, loc_sems = refs[2 * n:]
        ix, iy, ic = lax.axis_index("x"), lax.axis_index("y"), lax.axis_index("c")
        lin = lambda px, py, pc: 4 * px + 2 * py + pc
        me, sib = lin(ix, iy, ic), (ix, iy, 1 - ic)
        chips = [(1 - ix, iy), (ix, 1 - iy), (1 - ix, 1 - iy)]

        def copy(k, s, block, to, src=None):
            return pltpu.make_async_remote_copy(
                src_ref=outs[k].at[block] if src is None else src, dst_ref=outs[k].at[block],
                send_sem=send_sems.at[k, s], recv_sem=recv_sems.at[k, s], device_id=to,
                device_id_type=pl.DeviceIdType.MESH)

        local = [pltpu.make_async_copy(ins[k], outs[k].at[me], loc_sems.at[k]) for k in range(n)]
        for cp in local:
            cp.start()
        first = [copy(k, 1 + j, me, (cx, cy, ic), src=ins[k]) for j, (cx, cy) in enumerate(chips) for k in range(n)]
        first += [copy(k, 0, me, sib, src=ins[k]) for k in range(n)]
        for cp in first:
            cp.start()
        passed = []
        for j, (cx, cy) in enumerate(chips):
            for k in range(n):
                copy(k, 1 + j, lin(cx, cy, ic), sib).wait_recv()
                passed.append(copy(k, 4 + j, lin(cx, cy, ic), sib))
                passed[-1].start()
        for k in range(n):
            copy(k, 0, lin(*sib), sib).wait_recv()
            for j, (cx, cy) in enumerate(chips):
                copy(k, 4 + j, lin(cx, cy, 1 - ic), sib).wait_recv()
        for cp in first + passed:
            cp.wait_send()
        for cp in local:
            cp.wait()

    any_spec = pl.BlockSpec(memory_space=pl.ANY)
    return pl.pallas_call(
        body, name=name, in_specs=[any_spec] * n, out_specs=(any_spec,) * n, out_shape=out_shape,
        scratch_shapes=[pltpu.SemaphoreType.DMA((n, N_DEV - 1)), pltpu.SemaphoreType.DMA((n, N_DEV - 1)),
                        pltpu.SemaphoreType.DMA((n,))])(*arrays)


def _peer_table(chips=False):
    ix, iy, ic = lax.axis_index("x"), lax.axis_index("y"), lax.axis_index("c")
    peers = []
    for m in ((2, 4, 6) if chips else range(1, N_DEV)):
        px = 1 - ix if m & 4 else ix
        py = 1 - iy if m & 2 else iy
        pc = 1 - ic if m & 1 else ic
        peers.append(((px, py, pc), 2 * px + py if chips else 4 * px + 2 * py + pc))
    return (2 * ix + iy if chips else 4 * ix + 2 * iy + ic), peers


def _exchange_start(arrays, modes, after, name, chips=False):
    n = len(arrays)
    land_shapes = [(N_DEV,) + a.shape if md == "gather" else a.shape for a, md in zip(arrays, modes)]

    def body(*refs):
        ins, lands = refs[:n], refs[n:2 * n]
        send_sems, recv_sems = refs[2 * n + 1], refs[2 * n + 2]
        token = refs[-1]
        me, peers = _peer_table(chips)

        def src(k, slot):
            return ins[k] if modes[k] == "gather" else ins[k].at[slot]

        for peer, slot in peers:
            for k in range(n):
                pltpu.make_async_remote_copy(
                    src_ref=src(k, slot), dst_ref=lands[k].at[me], send_sem=send_sems, recv_sem=recv_sems,
                    device_id=peer, device_id_type=pl.DeviceIdType.MESH).start()
        token[...] = jnp.zeros_like(token)

    hbm = pl.BlockSpec(memory_space=pltpu.HBM)
    sem = pl.BlockSpec(memory_space=pltpu.SEMAPHORE)
    sem_shape = pltpu.SemaphoreType.DMA(())
    operands = [pltpu.with_memory_space_constraint(a, pltpu.HBM) for a in arrays]
    operands += [pltpu.with_memory_space_constraint(lax.empty(s, a.dtype), pltpu.HBM)
                 for s, a in zip(land_shapes, arrays)]
    out = pl.pallas_call(
        body, name=name,
        out_shape=(sem_shape, sem_shape) + tuple(pltpu.HBM(a.shape, a.dtype) for a in arrays)
        + tuple(pltpu.HBM(s, a.dtype) for s, a in zip(land_shapes, arrays)) + (jax.ShapeDtypeStruct((8, 128), F32),),
        in_specs=[hbm] * (2 * n) + [pl.BlockSpec(memory_space=pl.ANY)],
        out_specs=(sem, sem) + (hbm,) * (2 * n) + (pl.BlockSpec(memory_space=pltpu.VMEM),),
        input_output_aliases={i: 2 + i for i in range(2 * n)},
        compiler_params=pltpu.CompilerParams(has_side_effects=pltpu.SideEffectType.DATAFLOW_SIDE_EFFECTING))(
            *operands, after)
    return out[0], out[1], out[2:2 + n], out[2 + n:2 + 2 * n], out[-1]


def _exchange_wait(started, modes, after, name, chips=False):
    send_sems, recv_sems, sent, lands, _ = started
    n = len(sent)

    def body(*refs):
        ins, zones = refs[:n], refs[n:2 * n]
        send_ref, recv_ref = refs[2 * n], refs[2 * n + 1]
        _, peers = _peer_table(chips)

        def src(k, slot):
            return ins[k] if modes[k] == "gather" else ins[k].at[slot]

        for peer, slot in peers:
            for k in range(n):
                cp = pltpu.make_async_remote_copy(
                    src_ref=src(k, slot), dst_ref=zones[k].at[slot], send_sem=send_ref, recv_sem=recv_ref,
                    device_id=peer, device_id_type=pl.DeviceIdType.MESH)
                cp.wait_send()
                cp.wait_recv()

    hbm = pl.BlockSpec(memory_space=pltpu.HBM)
    sem = pl.BlockSpec(memory_space=pltpu.SEMAPHORE)
    out = pl.pallas_call(
        body, name=name,
        out_shape=tuple(pltpu.HBM(a.shape, a.dtype) for a in sent) + tuple(pltpu.HBM(a.shape, a.dtype) for a in lands),
        in_specs=[hbm] * (2 * n) + [sem, sem, pl.BlockSpec(memory_space=pl.ANY)], out_specs=(hbm,) * (2 * n),
        input_output_aliases={i: i for i in range(2 * n)},
        compiler_params=pltpu.CompilerParams(has_side_effects=pltpu.SideEffectType.DATAFLOW_SIDE_EFFECTING))(
            *sent, *lands, send_sems, recv_sems, after)
    ix, iy, ic = lax.axis_index("x"), lax.axis_index("y"), lax.axis_index("c")
    me = 2 * ix + iy if chips else 4 * ix + 2 * iy + ic
    filled = []
    for k in range(n):
        own = sent[k] if modes[k] == "gather" else lax.dynamic_index_in_dim(sent[k], me, axis=0, keepdims=False)
        filled.append(lax.dynamic_update_index_in_dim(out[n + k], own, me, axis=0))
    return filled


def _swap_sibling(to_c0, to_c1, name):
    def body(c0_ref, c1_ref, out_ref, send_sem, recv_sem):
        ix, iy, ic = lax.axis_index("x"), lax.axis_index("y"), lax.axis_index("c")

        def copy(src):
            return pltpu.make_async_remote_copy(src_ref=src, dst_ref=out_ref, send_sem=send_sem, recv_sem=recv_sem,
                                                device_id=(ix, iy, 1 - ic), device_id_type=pl.DeviceIdType.MESH)

        @pl.when(ic == 0)
        def _():
            copy(c1_ref).start()

        @pl.when(ic == 1)
        def _():
            copy(c0_ref).start()

        copy(c0_ref).wait()

    any_spec = pl.BlockSpec(memory_space=pl.ANY)
    return pl.pallas_call(body, name=name, in_specs=[any_spec, any_spec], out_specs=any_spec,
                          out_shape=jax.ShapeDtypeStruct(to_c0.shape, to_c0.dtype),
                          scratch_shapes=[pltpu.SemaphoreType.DMA, pltpu.SemaphoreType.DMA])(to_c0, to_c1)


def _add_pair(to_c0, to_c1, got, name):
    n, R, C = got.shape
    tr = _blk(R, 256)

    def body(c0_ref, c1_ref, got_ref, o_ref):
        ic = lax.axis_index("c")

        @pl.when(ic == 0)
        def _():
            o_ref[...] = (c0_ref[...].astype(F32) + got_ref[...].astype(F32)).astype(o_ref.dtype)

        @pl.when(ic == 1)
        def _():
            o_ref[...] = (c1_ref[...].astype(F32) + got_ref[...].astype(F32)).astype(o_ref.dtype)

    spec = pl.BlockSpec((1, tr, C), lambda i, j: (i, j, 0))
    return pl.pallas_call(body, name=name, grid=(n, R // tr), in_specs=[spec, spec, spec], out_specs=spec,
                          out_shape=jax.ShapeDtypeStruct(got.shape, got.dtype),
                          compiler_params=_params("parallel", "parallel"))(to_c0, to_c1, got)


W_IN_SPLITS = (0, 2048, 6144, 6176, 10272, 12320, 12336, 12352, 13376, 14400)
N_REPLICATED = 16640
REPLICATED = ("b_ada", "norm_mix_pre", "norm_mix_post", "ssm_conv_b", "ssm_dt_bias", "ssm_A_log", "ssm_D",
              "ssm_norm_w", "gdn_dt_bias", "gdn_A_log", "gdn_norm_w", "norm_mlp_pre", "norm_mlp_post")
WEIGHTS = ("w_ada", "b_ada", "norm_mix_pre", "norm_mix_post", "w_in", "ssm_conv_w", "ssm_conv_b", "ssm_dt_bias",
           "ssm_A_log", "ssm_D", "ssm_norm_w", "gdn_conv_w", "gdn_dt_bias", "gdn_A_log", "gdn_norm_w", "w_ssm_up",
           "w_gdn_up", "w_out", "norm_mlp_pre", "norm_mlp_post", "w_mlp_up", "w_mlp_down")


def _cols_of_shards(g, a, b):
    width, pieces = g.shape[2], []
    while a < b:
        i = a // width
        hi = min(b, (i + 1) * width)
        pieces.append(g[i][:, a - i * width:hi - i * width])
        a = hi
    return pieces


ORIG_SEGMENTS = ((0, 6144, "main", 0), (6144, 6176, "small", 0), (6176, 12320, "main", 6144),
                 (12320, 12352, "small", 32), (12352, 14400, "main", 12288))


def _orig_cols(main_cols, small_cols, a, b):
    pieces = []
    for s0, s1, which, off in ORIG_SEGMENTS:
        lo, hi = max(a, s0), min(b, s1)
        if lo < hi:
            pieces.append((main_cols if which == "main" else small_cols)[:, off + lo - s0:off + hi - s0])
    return jnp.concatenate(pieces, axis=1)


def _by_cols(t):
    return t.transpose(1, 0, 2).reshape(t.shape[1], N_DEV * t.shape[2])


def _to_col_shards(t):
    R, C8 = t.shape
    return t.reshape(R, N_DEV, C8 // N_DEV).transpose(1, 0, 2)


def _heads_first(t, groups):
    S = t.shape[0]
    return t.reshape(S, groups, t.shape[1] // groups).transpose(1, 0, 2)


def _heads_last(t):
    return t.transpose(1, 0, 2).reshape(t.shape[1], t.shape[0] * t.shape[2])


def kernel(x, c, w_ada, b_ada, norm_mix_pre, norm_mix_post, w_in, ssm_conv_w, ssm_conv_b, ssm_dt_bias, ssm_A_log, ssm_D, ssm_norm_w, gdn_conv_w, gdn_dt_bias, gdn_A_log, gdn_norm_w, w_ssm_up, w_gdn_up, w_out, norm_mlp_pre, norm_mlp_post, w_mlp_up, w_mlp_down, loss_target, m_w_ada, m_b_ada, m_norm_mix_pre, m_norm_mix_post, m_w_in, m_ssm_conv_w, m_ssm_conv_b, m_ssm_dt_bias, m_ssm_A_log, m_ssm_D, m_ssm_norm_w, m_gdn_conv_w, m_gdn_dt_bias, m_gdn_A_log, m_gdn_norm_w, m_w_ssm_up, m_w_gdn_up, m_w_out, m_norm_mlp_pre, m_norm_mlp_post, m_w_mlp_up, m_w_mlp_down, v_w_ada, v_b_ada, v_norm_mix_pre, v_norm_mix_post, v_w_in, v_ssm_conv_w, v_ssm_conv_b, v_ssm_dt_bias, v_ssm_A_log, v_ssm_D, v_ssm_norm_w, v_gdn_conv_w, v_gdn_dt_bias, v_gdn_A_log, v_gdn_norm_w, v_w_ssm_up, v_w_gdn_up, v_w_out, v_norm_mlp_pre, v_norm_mlp_post, v_w_mlp_up, v_w_mlp_down):
    S, Dm = x.shape[1], D_MODEL
    me = 4 * lax.axis_index("x") + 2 * lax.axis_index("y") + lax.axis_index("c")
    x2, tgt = x[0], loss_target[0]
    n_ada = w_ada.shape[2]
    given = dict(
        w_ada=(w_ada, m_w_ada, v_w_ada), b_ada=(b_ada, m_b_ada, v_b_ada),
        norm_mix_pre=(norm_mix_pre, m_norm_mix_pre, v_norm_mix_pre),
        norm_mix_post=(norm_mix_post, m_norm_mix_post, v_norm_mix_post), w_in=(w_in, m_w_in, v_w_in),
        ssm_conv_w=(ssm_conv_w, m_ssm_conv_w, v_ssm_conv_w), ssm_conv_b=(ssm_conv_b, m_ssm_conv_b, v_ssm_conv_b),
        ssm_dt_bias=(ssm_dt_bias, m_ssm_dt_bias, v_ssm_dt_bias), ssm_A_log=(ssm_A_log, m_ssm_A_log, v_ssm_A_log),
        ssm_D=(ssm_D, m_ssm_D, v_ssm_D), ssm_norm_w=(ssm_norm_w, m_ssm_norm_w, v_ssm_norm_w),
        gdn_conv_w=(gdn_conv_w, m_gdn_conv_w, v_gdn_conv_w), gdn_dt_bias=(gdn_dt_bias, m_gdn_dt_bias, v_gdn_dt_bias),
        gdn_A_log=(gdn_A_log, m_gdn_A_log, v_gdn_A_log), gdn_norm_w=(gdn_norm_w, m_gdn_norm_w, v_gdn_norm_w),
        w_ssm_up=(w_ssm_up, m_w_ssm_up, v_w_ssm_up), w_gdn_up=(w_gdn_up, m_w_gdn_up, v_w_gdn_up),
        w_out=(w_out, m_w_out, v_w_out), norm_mlp_pre=(norm_mlp_pre, m_norm_mlp_pre, v_norm_mlp_pre),
        norm_mlp_post=(norm_mlp_post, m_norm_mlp_post, v_norm_mlp_post), w_mlp_up=(w_mlp_up, m_w_mlp_up, v_w_mlp_up),
        w_mlp_down=(w_mlp_down, m_w_mlp_down, v_w_mlp_down))

    (c_all, scw, gcw, g_in) = _gather_two_level([c, ssm_conv_w[0], gdn_conv_w[0], w_in[0].astype(BF16)], "gather_w_in")
    c_all = c_all.reshape(N_DEV, Dm)
    sp = W_IN_SPLITS
    w_main = jnp.concatenate(_cols_of_shards(g_in, sp[0], sp[2]) + _cols_of_shards(g_in, sp[3], sp[5])
                             + _cols_of_shards(g_in, sp[7], sp[9]), axis=1)
    w_small = jnp.concatenate(_cols_of_shards(g_in, sp[2], sp[3]) + _cols_of_shards(g_in, sp[5], sp[7])
                              + [jnp.zeros((Dm, N_SMALL - 64), BF16)], axis=1)
    conv_w = jnp.concatenate([_by_cols(scw), _by_cols(gcw)], axis=1)
    conv_b = jnp.concatenate([ssm_conv_b, jnp.zeros_like(ssm_conv_b)], axis=1)

    b_loc = lax.dynamic_slice(b_ada, (0, me * n_ada), (1, n_ada))
    mod_part = _ada_fwd(c_all, w_ada[0], b_loc, "ada_fwd")
    (mod_rows,) = _exchange([mod_part.reshape(N_DEV, 1, n_ada)], ["a2a"], "exchange_mod")
    rest = _exchange_start([w_ssm_up[0].astype(BF16), w_gdn_up[0].astype(BF16), w_out[0].astype(BF16),
                            w_mlp_up[0].astype(BF16), w_mlp_down[0].astype(BF16)], ["gather"] * 5, mod_rows,
                           "gather_rest_start")
    mod = mod_rows.reshape(1, 6 * Dm) + rest[4][0:1, 0:1]
    sh1, sc1, g1, sh2, sc2, g2 = [mod[:, i * Dm:(i + 1) * Dm] for i in range(6)]

    h = _pre_fwd(x2, norm_mix_pre, sc1, sh1, "pre_mix")
    proj = _mm(h, w_main, S, N_MAIN, Dm, mode="nn", out_dtype=F32, name="proj_main")
    small = _mm(h, w_small, S, N_SMALL, Dm, mode="nn", out_dtype=F32, name="proj_small")
    conv = _conv_fwd(proj, conv_w, conv_b, "conv_fwd")
    dt_g, b_g, a_g = _heads_first(small[:, 0:32], 8), _heads_first(small[:, 32:48], 8), _heads_first(small[:, 48:64], 8)
    pv_ssm = jnp.stack([ssm_dt_bias.reshape(8, 4), ssm_A_log.reshape(8, 4), ssm_D.reshape(8, 4)], axis=1)
    nw_ssm = ssm_norm_w.reshape(8, 1, SSM_GROUP_WIDTH)
    pv_gdn = jnp.stack([gdn_dt_bias.reshape(8, 2), gdn_A_log.reshape(8, 2)], axis=1)
    y_ssd, ysn, st_ssm = _ssd_fwd(conv, proj, dt_g, pv_ssm, nw_ssm, "ssd_fwd")
    o_pre, ogn, st_gdn, t_inv = _gdn_fwd(conv, proj, b_g, a_g, pv_gdn, gdn_norm_w, "gdn_fwd")
    g_su, g_gu, g_out, g_mu, g_md = _exchange_wait(rest, ["gather"] * 5, ogn, "gather_rest_wait")
    w_su, w_gu = g_su.reshape(2 * Dm, Dm), g_gu.reshape(2 * Dm, Dm)
    w_o, w_mu, w_md = g_out.reshape(Dm, Dm), _by_cols(g_mu), g_md.reshape(4 * Dm, Dm)
    ys = _mm(ysn, w_su, S, Dm, 2 * Dm, mode="nn", out_dtype=F32, name="ssm_up")
    yg, merged = _gdn_up_merge(ogn, w_gu, ys, proj, "gdn_up_merge")
    mo, x1, h2 = _mix_out_post_pre(merged, w_o, x2, norm_mix_post, g1, norm_mlp_pre, sc2, sh2, "mix_out_post_pre")
    u, act = _mm(h2, w_mu, S, 4 * Dm, Dm, mode="nn", out_dtype=F32, epi="relu2", name="mlp_up")
    y_mlp = _mm(act, w_md, S, Dm, 4 * Dm, mode="nn", out_dtype=F32, name="mlp_down")
    dx2, loss_loc, dy, dg2, dw_post2 = _final_fwd_bwd(x1, y_mlp, norm_mlp_post, g2, tgt, "post_mlp_loss_bwd")

    du = _mm(dy, w_md, S, 4 * Dm, Dm, mode="nt", out_dtype=BF16, epi="drelu2", extra=u, name="mlp_down_dx")
    gw_md = _mm(act, dy, 4 * Dm, Dm, S, mode="tn", out_dtype=BF16, name="mlp_down_dw")
    dh2 = _mm(du, w_mu, S, Dm, 4 * Dm, mode="nt", out_dtype=F32, name="mlp_up_dx")
    gw_mu = _mm(h2, du, Dm, 4 * Dm, S, mode="tn", out_dtype=BF16, name="mlp_up_dw")
    mlp_x = _exchange_start([_to_col_shards(gw_mu), gw_md.reshape(N_DEV, -1, Dm)], ["a2a"] * 2, gw_md,
                            "grads_mlp_start")
    dx1, dsh2, dsc2, dw_pre2 = _pre_bwd(dh2, x1, norm_mlp_pre, sc2 + mlp_x[4][0:1, 0:1], dx2, "pre_mlp_bwd")
    dmo, dg1, dw_post1 = _post_bwd(dx1, mo, norm_mix_post, g1, "post_mix_bwd")
    gw_o = _mm(merged, dmo, Dm, Dm, S, mode="tn", out_dtype=BF16, name="mix_out_dw")
    dys, dyg, d_proj = _mix_out_dx_merge_bwd(dmo, w_o, ys, yg, proj, lax.empty((S, N_MAIN), BF16), "mix_out_dx_merge")
    dysn = _mm(dys, w_su, S, 2 * Dm, Dm, mode="nt", out_dtype=F32, name="ssm_up_dx")
    gw_su = _mm(ysn, dys, 2 * Dm, Dm, S, mode="tn", out_dtype=BF16, name="ssm_up_dw")
    dogn = _mm(dyg, w_gu, S, 2 * Dm, Dm, mode="nt", out_dtype=F32, name="gdn_up_dx")
    gw_gu = _mm(ogn, dyg, 2 * Dm, Dm, S, mode="tn", out_dtype=BF16, name="gdn_up_dw")
    mix_x = _exchange_start([gw_su.reshape(N_DEV, -1, Dm), gw_gu.reshape(N_DEV, -1, Dm), gw_o.reshape(N_DEV, -1, Dm)],
                            ["a2a"] * 3, gw_gu, "grads_mix_start")
    dxs, dBm, dCm, d_proj, ddt_g, dpv_ssm, dnw_ssm = _ssd_bwd(dysn, conv, proj, dt_g, pv_ssm + mix_x[4][0, 0], nw_ssm,
                                                              y_ssd, st_ssm, d_proj, "ssd_bwd")
    dq, dk, dv, d_proj, db_g, da_g, dpv_gdn, dnw_gdn = _gdn_bwd(dogn, conv, proj, b_g, a_g, pv_gdn, gdn_norm_w, o_pre,
                                                                st_gdn, t_inv, d_proj, "gdn_bwd")
    conv_pieces = []
    for nm, d_act, col0 in (("xs", dxs, 0), ("B", dBm, 2048), ("C", dCm, 3072), ("q", dq, 4096), ("k", dk, 5120),
                            ("v", dv, 6144)):
        d_proj, dw_piece, db_piece = _conv_bwd(d_act, proj, conv_w, conv_b, col0, d_proj, "conv_bwd_" + nm)
        conv_pieces.append((dw_piece, db_piece))
    d_small = jnp.concatenate([_heads_last(ddt_g), _heads_last(db_g), _heads_last(da_g),
                               jnp.zeros((S, N_SMALL - 64), F32)], axis=1).astype(BF16)
    gw_small = _mm(h, d_small, Dm, N_SMALL, S, mode="tn", out_dtype=BF16, name="proj_small_dw")
    main_cols = _mm(h, d_proj, Dm, N_MAIN, S, mode="tn", out_dtype=BF16, name="proj_main_dw")
    n_shard = w_in.shape[2]
    slabs = [_orig_cols(main_cols, gw_small, i * n_shard, (i + 1) * n_shard) for i in range(N_DEV)]
    to_c0, to_c1 = jnp.stack(slabs[0::2]), jnp.stack(slabs[1::2])
    chip_sum = _add_pair(to_c0, to_c1, _swap_sibling(to_c0, to_c1, "grads_w_in_pair"), "grads_w_in_pair_sum")
    in_x = _exchange_start([chip_sum], ["a2a"], gw_small, "grads_w_in_start", chips=True)
    dh = _mm(d_small, w_small + in_x[4][0:1, 0:1].astype(BF16), S, Dm, N_SMALL, mode="nt", out_dtype=F32,
             name="proj_small_dx")
    dh = _mm(d_proj, w_main, S, Dm, N_MAIN, mode="nt", out_dtype=F32, add=dh, name="proj_main_dx")
    dx, dsh1, dsc1, dw_pre1 = _pre_bwd(dh, x2, norm_mix_pre, sc1, dx1, "pre_mix_bwd")
    r_mu, r_md = _exchange_wait(mlp_x, ["a2a"] * 2, dx, "grads_mlp_wait")
    r_su, r_gu, r_o = _exchange_wait(mix_x, ["a2a"] * 3, dx, "grads_mix_wait")

    dconv_w = jnp.concatenate([p[0] for p in conv_pieces], axis=1)
    dconv_b = jnp.concatenate([p[1] for p in conv_pieces[:3]], axis=1)
    dmod = jnp.concatenate([dsh1, dsc1, dg1, dsh2, dsc2, dg2], axis=1)
    small_vec = jnp.concatenate(
        [dmod, dw_pre1, dw_post1, dconv_b, dpv_ssm[:, 0].reshape(1, 32), dpv_ssm[:, 1].reshape(1, 32),
         dpv_ssm[:, 2].reshape(1, 32), dnw_ssm.reshape(1, 2048), dpv_gdn[:, 0].reshape(1, 16),
         dpv_gdn[:, 1].reshape(1, 16), jnp.sum(dnw_gdn, axis=0), dw_pre2, dw_post2, dconv_w.reshape(1, -1)], axis=1)
    n_vec = small_vec.shape[1]
    small_vec = jnp.pad(small_vec, ((0, 0), (0, (-n_vec) % 1024))).reshape(-1, 1024)
    (small_all,) = _exchange([small_vec], ["gather"], "gather_small_grads")
    small_all = small_all.reshape(N_DEV, -1)
    dmod_cols = lax.dynamic_slice(small_all, (0, me * n_ada), (N_DEV, n_ada))
    gw_ada = _ada_bwd(c_all.T, dmod_cols, "ada_bwd")
    conv_all = small_all[:, N_REPLICATED:n_vec].reshape(N_DEV, CONV_K, 2 * N_DEV * 512)
    conv_contrib = jnp.concatenate(
        [lax.dynamic_slice(conv_all, (0, 0, me * 512), (N_DEV, CONV_K, 512)),
         lax.dynamic_slice(conv_all, (0, 0, N_DEV * 512 + me * 512), (N_DEV, CONV_K, 512))], axis=1)
    rep_contrib = small_all[:, :N_REPLICATED].reshape(N_DEV, N_REPLICATED // 128, 128)

    results = {}

    def adam_big(nm, contrib):
        w3 = given[nm]
        res = _adam(contrib, w3[0][0], w3[1][0], w3[2][0], "adam_" + nm)
        results[nm] = tuple(r.reshape(w3[0].shape) for r in res)

    adam_big("w_ada", gw_ada[None])
    adam_big("w_ssm_up", r_su)
    adam_big("w_gdn_up", r_gu)
    adam_big("w_out", r_o)
    adam_big("w_mlp_up", r_mu)
    adam_big("w_mlp_down", r_md)
    (r_in,) = _exchange_wait(in_x, ["a2a"], results["w_mlp_down"][0], "grads_w_in_wait", chips=True)
    adam_big("w_in", r_in)
    packed = [jnp.concatenate([given[nm][i] for nm in REPLICATED], axis=1).reshape(N_REPLICATED // 128, 128)
              for i in range(3)]
    rep_res = _adam(rep_contrib, packed[0], packed[1], packed[2], "adam_replicated")
    pos = 0
    for nm in REPLICATED:
        size = given[nm][0].shape[1]
        results[nm] = tuple(r.reshape(1, N_REPLICATED)[:, pos:pos + size] for r in rep_res)
        pos += size
    conv_wmv = [jnp.concatenate([given["ssm_conv_w"][i][0], given["gdn_conv_w"][i][0]], axis=0) for i in range(3)]
    conv_res = _adam(conv_contrib, conv_wmv[0], conv_wmv[1], conv_wmv[2], "adam_conv_w")
    results["ssm_conv_w"] = tuple(r[None, :CONV_K] for r in conv_res)
    results["gdn_conv_w"] = tuple(r[None, CONV_K:] for r in conv_res)

    loss = lax.psum(loss_loc[0, 0], ("x", "y", "c"))
    return (loss, dx[None]) + tuple(results[nm][i] for i in range(4) for nm in WEIGHTS)
```

```python
import jax
import jax.numpy as jnp
from jax import lax
from jax.experimental import pallas as pl
from jax.experimental.pallas import tpu as pltpu

F32 = jnp.float32
BF16 = jnp.bfloat16
N_DEV = 8
D_MODEL = 1024
EPS = 1e-6
CONV_K = 4
SSM_CHUNK = 128
SSM_HEAD_DIM = 64
SSM_D_STATE = 128
SSM_GROUPS = 8
SSM_HEADS_PER_GROUP = 4
SSM_GROUP_WIDTH = SSM_HEADS_PER_GROUP * SSM_HEAD_DIM
SSM_GROUPS_PER_STEP = 2
GDN_CHUNK = 64
GDN_HEAD = 128
GDN_QK_HEADS = 8
GDN_V_PER_QK = 2
GDN_QK_PER_STEP = 4
GDN_INV_BLOCK = 16
C_ZS, C_XBC, C_QKV, C_ZG, C_GS, C_GG, N_MAIN = 0, 2048, 6144, 10240, 12288, 13312, 14336
N_SMALL = 128
ADAM_LR, ADAM_B1, ADAM_B2, ADAM_EPS, ADAM_WD, ADAM_STEP = 0.001, 0.9, 0.999, 1e-08, 0.01, 10
VMEM_LIMIT = 56 * 1024 * 1024
MM_WHOLE_K = 4096
MM_SPLIT_K = 2048
NEG_INF = float("-inf")

_NT = (((1,), (1,)), ((), ()))
_NN = (((1,), (0,)), ((), ()))
_TN = (((0,), (0,)), ((), ()))


def _params(*sem):
    return pltpu.CompilerParams(dimension_semantics=sem, vmem_limit_bytes=VMEM_LIMIT)


def _dot(a, b, dims=_NN):
    return lax.dot_general(a.astype(BF16), b.astype(BF16), dims, preferred_element_type=F32)


def _split(a):
    hi = a.astype(BF16)
    return hi, (a - hi.astype(F32)).astype(BF16)


def _dot3(a, b, dims=_NN):
    ah, al = _split(a)
    bh, bl = _split(b)
    d = lambda u, v: lax.dot_general(u, v, dims, preferred_element_type=F32)
    return d(ah, bh) + (d(ah, bl) + d(al, bh))


def _dot2(a, b, dims=_NN):
    ah, al = _split(a)
    bb = b.astype(BF16)
    d = lambda u: lax.dot_general(u, bb, dims, preferred_element_type=F32)
    return d(ah) + d(al)


def _sigmoid(x):
    return 0.5 * jnp.tanh(0.5 * x) + 0.5


def _silu(x):
    return x * _sigmoid(x)


def _dsilu(x):
    s = _sigmoid(x)
    return s * (1.0 + x * (1.0 - s))


def _softplus(x):
    return jnp.maximum(x, 0.0) + jnp.log1p(jnp.exp(-jnp.abs(x)))


def _iota(n, m, d):
    return lax.broadcasted_iota(jnp.int32, (n, m), d)


def _rowsum(x):
    return jnp.sum(x, axis=1, keepdims=True)


def _colsum(x):
    return jnp.sum(x, axis=0, keepdims=True)


def _total(x):
    return _rowsum(_colsum(x))


MXU_LANES = 128


def _parts(x, n):
    out = []
    for _ in range(n):
        p = x.astype(BF16)
        out.append(p)
        x = x - p.astype(F32)
    return out


def _sum_by(m01, x, dims=_NN, n=3):
    return sum(lax.dot_general(m01, p, dims, preferred_element_type=F32) for p in _parts(x, n))


def _row_col_sums(q):
    ones = jnp.ones((q.shape[0], MXU_LANES), BF16)
    acc = 0.0
    for p in _parts(q, 2):
        acc = acc + (lax.dot_general(p, ones, _NN, preferred_element_type=F32)
                     - lax.dot_general(p, ones, _TN, preferred_element_type=F32))
    return acc[:, 0:1]


def _cumsum_forms(col, ii, jj):
    lower = jnp.where(ii >= jj, 1.0, 0.0).astype(BF16)
    cum_col = _sum_by(lower, jnp.broadcast_to(col, (col.shape[0], MXU_LANES)))[:, 0:1]
    cum_row = _colsum(jnp.where(ii <= jj, col, 0.0))
    return cum_col, cum_row


def _rev_cumsum_col(col, ii, jj):
    upper = jnp.where(ii <= jj, 1.0, 0.0).astype(BF16)
    return _sum_by(upper, jnp.broadcast_to(col, (col.shape[0], MXU_LANES)))[:, 0:1]


def _blk(dim, pref):
    return pref if dim % pref == 0 else dim


def _lockstep(gens):
    gens = list(gens)
    while gens:
        alive = []
        for g in gens:
            try:
                next(g)
                alive.append(g)
            except StopIteration:
                pass
        gens = alive


def _mm(a, b, M, N, K, *, mode, out_dtype, name, a_off=(0, 0), b_off=(0, 0), add=None, epi=None, extra=None,
        tm=1024, tn=1024):
    tm, tn = _blk(M, tm), _blk(N, tn)
    tk = K if K <= MM_WHOLE_K else _blk(K, MM_SPLIT_K)
    nk = K // tk
    if mode == "tn":
        a_spec = pl.BlockSpec((tk, tm), lambda i, j, k: (k + a_off[0] // tk, i + a_off[1] // tm))
        assert a_off[0] % tk == 0 and a_off[1] % tm == 0
    else:
        a_spec = pl.BlockSpec((tm, tk), lambda i, j, k: (i + a_off[0] // tm, k + a_off[1] // tk))
        assert a_off[0] % tm == 0 and a_off[1] % tk == 0
    if mode == "nt":
        b_spec = pl.BlockSpec((tn, tk), lambda i, j, k: (j + b_off[0] // tn, k + b_off[1] // tk))
        assert b_off[0] % tn == 0 and b_off[1] % tk == 0
    else:
        b_spec = pl.BlockSpec((tk, tn), lambda i, j, k: (k + b_off[0] // tk, j + b_off[1] // tn))
        assert b_off[0] % tk == 0 and b_off[1] % tn == 0
    dims = {"nn": _NN, "nt": _NT, "tn": _TN}[mode]
    o_spec = pl.BlockSpec((tm, tn), lambda i, j, k: (i, j))
    ins, in_specs = [a, b], [a_spec, b_spec]
    if add is not None:
        ins.append(add)
        in_specs.append(o_spec)
    if extra is not None:
        ins.append(extra)
        in_specs.append(o_spec)
    n_in = len(ins)
    if epi == "relu2":
        out_shape = (jax.ShapeDtypeStruct((M, N), BF16), jax.ShapeDtypeStruct((M, N), BF16))
        out_specs = (o_spec, o_spec)
    else:
        out_shape = jax.ShapeDtypeStruct((M, N), out_dtype)
        out_specs = o_spec

    def body(*refs):
        a_ref, b_ref = refs[0], refs[1]
        outs = refs[n_in:] if nk == 1 else refs[n_in:-1]

        def finish(r):
            pos = 2
            if add is not None:
                r = r + refs[pos][...]
                pos += 1
            if epi == "relu2":
                p = jnp.maximum(r, 0.0)
                outs[0][...] = p.astype(BF16)
                outs[1][...] = (p * p).astype(BF16)
            elif epi == "drelu2":
                outs[0][...] = (r * (2.0 * refs[pos][...].astype(F32))).astype(out_dtype)
            else:
                outs[0][...] = r.astype(out_dtype)

        if nk == 1:
            finish(_dot(a_ref[...], b_ref[...], dims))
            return
        acc = refs[-1]
        k = pl.program_id(2)

        @pl.when(k == 0)
        def _():
            acc[...] = jnp.zeros_like(acc)

        acc[...] += _dot(a_ref[...], b_ref[...], dims)

        @pl.when(k == nk - 1)
        def _():
            finish(acc[...])

    return pl.pallas_call(
        body, name=name, grid=(M // tm, N // tn, nk), in_specs=in_specs, out_specs=out_specs, out_shape=out_shape,
        scratch_shapes=[] if nk == 1 else [pltpu.VMEM((tm, tn), F32)],
        compiler_params=_params("parallel", "parallel", "arbitrary"))(*ins)


def _mm_rows(a, b, M, N, K, *, mode, name, extras, out_shapes, out_specs, epilogue, aliases=None, tm=512):
    tm = _blk(M, tm)
    a_spec = pl.BlockSpec((tm, K), lambda i: (i, 0))
    b_spec = pl.BlockSpec((K, N) if mode == "nn" else (N, K), lambda i: (0, 0))
    dims = _NN if mode == "nn" else _NT
    n_ex = len(extras)

    def body(a_ref, b_ref, *refs):
        epilogue(_dot(a_ref[...], b_ref[...], dims), refs[:n_ex], refs[n_ex:])

    return pl.pallas_call(
        body, name=name, grid=(M // tm,), in_specs=[a_spec, b_spec] + [sp for _, sp in extras],
        out_specs=tuple(out_specs), out_shape=tuple(out_shapes), input_output_aliases=aliases or {},
        compiler_params=_params("parallel"))(a, b, *[x for x, _ in extras])


def _row_spec(tb, d):
    return pl.BlockSpec((tb, d), lambda i: (i, 0))


def _vec_spec(d):
    return pl.BlockSpec((1, d), lambda i: (0, 0))


def _pre_fwd(x, w, sc, sh, name):
    S, Dm = x.shape
    tb = _blk(S, 512)

    def body(x_ref, w_ref, sc_ref, sh_ref, h_ref):
        xv = x_ref[...]
        r = lax.rsqrt(jnp.mean(xv * xv, axis=-1, keepdims=True) + EPS)
        h_ref[...] = ((xv * r * w_ref[...]) * (1.0 + sc_ref[...]) + sh_ref[...]).astype(BF16)

    return pl.pallas_call(
        body, name=name, grid=(S // tb,), in_specs=[_row_spec(tb, Dm)] + [_vec_spec(Dm)] * 3,
        out_specs=_row_spec(tb, Dm), out_shape=jax.ShapeDtypeStruct((S, Dm), BF16),
        compiler_params=_params("parallel"))(x, w, sc, sh)


def _final_fwd_bwd(x, y, w, g, target, name):
    S, Dm = x.shape
    tb = _blk(S, 512)
    nb = S // tb

    def body(x_ref, y_ref, w_ref, g_ref, t_ref, dx_ref, loss_ref, dy_ref, dg_ref, dw_ref, acc):
        i = pl.program_id(0)

        @pl.when(i == 0)
        def _():
            acc[...] = jnp.zeros_like(acc)
            dg_ref[...] = jnp.zeros_like(dg_ref)
            dw_ref[...] = jnp.zeros_like(dw_ref)

        yv = y_ref[...]
        r = lax.rsqrt(jnp.mean(yv * yv, axis=-1, keepdims=True) + EPS)
        yh = yv * r
        n = yh * w_ref[...]
        e = (x_ref[...] + g_ref[...] * n) - t_ref[...]
        dv = e * (1.0 / Dm)
        dx_ref[...] = dv
        acc[...] += _colsum(e * e)
        dg_ref[...] += _colsum(dv * n)
        dn = dv * g_ref[...]
        dw_ref[...] += _colsum(dn * yh)
        dyh = dn * w_ref[...]
        dy_ref[...] = (r * (dyh - yh * jnp.mean(dyh * yh, axis=-1, keepdims=True))).astype(BF16)

        @pl.when(i == nb - 1)
        def _():
            loss_ref[...] = (0.5 / Dm) * _rowsum(acc[...])

    row, vec = _row_spec(tb, Dm), _vec_spec(Dm)
    vec_shape = jax.ShapeDtypeStruct((1, Dm), F32)
    return pl.pallas_call(
        body, name=name, grid=(nb,), in_specs=[row, row, vec, vec, row],
        out_specs=(row, pl.BlockSpec((1, 1), lambda i: (0, 0)), row, vec, vec),
        out_shape=(jax.ShapeDtypeStruct((S, Dm), F32), jax.ShapeDtypeStruct((1, 1), F32),
                   jax.ShapeDtypeStruct((S, Dm), BF16), vec_shape, vec_shape),
        scratch_shapes=[pltpu.VMEM((1, Dm), F32)], compiler_params=_params("arbitrary"))(x, y, w, g, target)


def _post_bwd(dxo, y, w, g, name):
    S, Dm = y.shape
    tb = _blk(S, 512)

    def body(d_ref, y_ref, w_ref, g_ref, dy_ref, dg_ref, dw_ref):
        i = pl.program_id(0)

        @pl.when(i == 0)
        def _():
            dg_ref[...] = jnp.zeros_like(dg_ref)
            dw_ref[...] = jnp.zeros_like(dw_ref)

        yv, dv = y_ref[...], d_ref[...]
        r = lax.rsqrt(jnp.mean(yv * yv, axis=-1, keepdims=True) + EPS)
        yh = yv * r
        dg_ref[...] += _colsum(dv * (yh * w_ref[...]))
        dn = dv * g_ref[...]
        dw_ref[...] += _colsum(dn * yh)
        dyh = dn * w_ref[...]
        dy_ref[...] = (r * (dyh - yh * jnp.mean(dyh * yh, axis=-1, keepdims=True))).astype(BF16)

    return pl.pallas_call(
        body, name=name, grid=(S // tb,), in_specs=[_row_spec(tb, Dm)] * 2 + [_vec_spec(Dm)] * 2,
        out_specs=(_row_spec(tb, Dm), _vec_spec(Dm), _vec_spec(Dm)),
        out_shape=(jax.ShapeDtypeStruct((S, Dm), BF16), jax.ShapeDtypeStruct((1, Dm), F32),
                   jax.ShapeDtypeStruct((1, Dm), F32)),
        compiler_params=_params("arbitrary"))(dxo, y, w, g)


def _pre_bwd(dh, x, w, sc, dres, name):
    S, Dm = x.shape
    tb = _blk(S, 512)

    def body(dh_ref, x_ref, w_ref, sc_ref, dr_ref, dx_ref, dsh_ref, dsc_ref, dw_ref):
        i = pl.program_id(0)

        @pl.when(i == 0)
        def _():
            dsh_ref[...] = jnp.zeros_like(dsh_ref)
            dsc_ref[...] = jnp.zeros_like(dsc_ref)
            dw_ref[...] = jnp.zeros_like(dw_ref)

        xv, dv = x_ref[...], dh_ref[...]
        r = lax.rsqrt(jnp.mean(xv * xv, axis=-1, keepdims=True) + EPS)
        xh = xv * r
        one_sc = 1.0 + sc_ref[...]
        dsh_ref[...] += _colsum(dv)
        dsc_ref[...] += _colsum(dv * (xh * w_ref[...]))
        dw_ref[...] += _colsum(dv * one_sc * xh)
        dxh = dv * one_sc * w_ref[...]
        dx_ref[...] = dr_ref[...] + r * (dxh - xh * jnp.mean(dxh * xh, axis=-1, keepdims=True))

    vec = jax.ShapeDtypeStruct((1, Dm), F32)
    return pl.pallas_call(
        body, name=name, grid=(S // tb,),
        in_specs=[_row_spec(tb, Dm)] * 2 + [_vec_spec(Dm)] * 2 + [_row_spec(tb, Dm)],
        out_specs=(_row_spec(tb, Dm), _vec_spec(Dm), _vec_spec(Dm), _vec_spec(Dm)),
        out_shape=(jax.ShapeDtypeStruct((S, Dm), F32), vec, vec, vec),
        compiler_params=_params("arbitrary"))(dh, x, w, sc, dres)


D_PROJ_ANY = pl.BlockSpec(memory_space=pl.ANY)


def _gate_specs(tm, Dm):
    return (pl.BlockSpec((tm, Dm), lambda i: (i, C_GS // Dm)), pl.BlockSpec((tm, Dm), lambda i: (i, C_GG // Dm)))


def _gdn_up_merge(ogn, w_gu, ys, proj, name):
    S, K = ogn.shape
    Dm = ys.shape[1]
    tm = _blk(S, 512)
    row = _row_spec(tm, Dm)

    def epilogue(r, ex, out):
        ys_ref, gs_ref, gg_ref = ex
        out[0][...] = r
        out[1][...] = (_sigmoid(gs_ref[...]) * ys_ref[...] + _sigmoid(gg_ref[...]) * r).astype(BF16)

    gs_spec, gg_spec = _gate_specs(tm, Dm)
    return _mm_rows(ogn, w_gu, S, Dm, K, mode="nn", name=name, tm=tm,
                    extras=[(ys, row), (proj, gs_spec), (proj, gg_spec)],
                    out_shapes=[jax.ShapeDtypeStruct((S, Dm), F32), jax.ShapeDtypeStruct((S, Dm), BF16)],
                    out_specs=[row, row], epilogue=epilogue)


def _mix_out_post_pre(merged, w_o, x, w_post, g, w_pre, sc, sh, name):
    S, Dm = x.shape
    tm = _blk(S, 512)
    row, vec = _row_spec(tm, Dm), _vec_spec(Dm)

    def epilogue(r, ex, out):
        x_ref, wpost_ref, g_ref, wpre_ref, sc_ref, sh_ref = ex
        out[0][...] = r
        rr = lax.rsqrt(jnp.mean(r * r, axis=-1, keepdims=True) + EPS)
        x1 = x_ref[...] + g_ref[...] * (r * rr * wpost_ref[...])
        out[1][...] = x1
        r1 = lax.rsqrt(jnp.mean(x1 * x1, axis=-1, keepdims=True) + EPS)
        out[2][...] = ((x1 * r1 * wpre_ref[...]) * (1.0 + sc_ref[...]) + sh_ref[...]).astype(BF16)

    return _mm_rows(merged, w_o, S, Dm, Dm, mode="nn", name=name, tm=tm,
                    extras=[(x, row), (w_post, vec), (g, vec), (w_pre, vec), (sc, vec), (sh, vec)],
                    out_shapes=[jax.ShapeDtypeStruct((S, Dm), F32), jax.ShapeDtypeStruct((S, Dm), F32),
                                jax.ShapeDtypeStruct((S, Dm), BF16)], out_specs=[row, row, row], epilogue=epilogue)


def _mix_out_dx_merge_bwd(dmo, w_o, ys, yg, proj, d_proj, name):
    S, Dm = ys.shape
    tm = _blk(S, 512)
    row = _row_spec(tm, Dm)

    def epilogue(d, ex, out):
        ys_ref, yg_ref, gs_ref, gg_ref, _ = ex
        ss, sg = _sigmoid(gs_ref[...]), _sigmoid(gg_ref[...])
        out[0][...] = (d * ss).astype(BF16)
        out[1][...] = (d * sg).astype(BF16)
        out[2][:, :Dm] = (d * ys_ref[...] * ss * (1.0 - ss)).astype(BF16)
        out[2][:, Dm:] = (d * yg_ref[...] * sg * (1.0 - sg)).astype(BF16)

    gs_spec, gg_spec = _gate_specs(tm, Dm)
    return _mm_rows(dmo, w_o, S, Dm, Dm, mode="nt", name=name, tm=tm,
                    extras=[(ys, row), (yg, row), (proj, gs_spec), (proj, gg_spec), (d_proj, D_PROJ_ANY)],
                    out_shapes=[jax.ShapeDtypeStruct((S, Dm), BF16), jax.ShapeDtypeStruct((S, Dm), BF16),
                                jax.ShapeDtypeStruct(d_proj.shape, BF16)],
                    out_specs=[row, row, pl.BlockSpec((tm, 2 * Dm), lambda i: (i, C_GS // (2 * Dm)))],
                    epilogue=epilogue, aliases={6: 2})


CONV_COLS = 128
CONV_BWD_ROWS = 128


def _taps_down(x):
    rows = _iota(x.shape[0], x.shape[1], 0)
    return [x] + [jnp.where(rows >= k, pltpu.roll(x, k, 0), 0.0) for k in range(1, CONV_K)]


def _conv_pre(taps, w_ref, b_ref):
    pre = taps[0] * w_ref[CONV_K - 1:CONV_K, :] + b_ref[...]
    for k in range(1, CONV_K):
        pre = pre + taps[k] * w_ref[CONV_K - 1 - k:CONV_K - k, :]
    return pre


def _conv_dx(dpre, w_ref):
    n = dpre.shape[0]
    rows = _iota(n, dpre.shape[1], 0)
    dx = dpre * w_ref[CONV_K - 1:CONV_K, :]
    for k in range(1, CONV_K):
        dx = dx + jnp.where(rows < n - k, pltpu.roll(dpre, n - k, 0), 0.0) * w_ref[CONV_K - 1 - k:CONV_K - k, :]
    return dx


def _conv_fwd(proj, w, b, name):
    S = proj.shape[0]
    n = w.shape[1]
    cb = CONV_COLS

    def body(x_ref, w_ref, b_ref, o_ref):
        o_ref[...] = _silu(_conv_pre(_taps_down(x_ref[...]), w_ref, b_ref)).astype(BF16)

    return pl.pallas_call(
        body, name=name, grid=(n // cb,),
        in_specs=[pl.BlockSpec((S, cb), lambda j: (0, j + C_XBC // cb)), pl.BlockSpec((CONV_K, cb), lambda j: (0, j)),
                  pl.BlockSpec((1, cb), lambda j: (0, j))],
        out_specs=pl.BlockSpec((S, cb), lambda j: (0, j)), out_shape=jax.ShapeDtypeStruct((S, n), BF16),
        compiler_params=_params("parallel"))(proj, w, b)


def _conv_bwd(dact, proj, w, b, col0, d_proj, name):
    S, n = dact.shape
    cb = CONV_COLS
    o = col0 // cb

    R, HALO = _blk(S, CONV_BWD_ROWS), 8
    n_chunks = S // R

    def body(d_ref, x_ref, w_ref, b_ref, _, dx_ref, dw_ref, db_ref):
        def chunk(r0, first, last, sums):
            lo, hi = (0 if first else HALO), (0 if last else HALO)
            start = r0 - lo if isinstance(r0, int) else pl.multiple_of(r0 - lo, HALO)
            xe = x_ref[pl.ds(start, lo + R + hi), :]
            rows = _iota(lo + R + hi, cb, 0)
            taps = [xe[lo:, :]]
            for k in range(1, CONV_K):
                t = pltpu.roll(xe, k, 0)
                taps.append((jnp.where(rows >= k, t, 0.0) if first else t)[lo:, :])
            dpre_e = d_ref[pl.ds(r0, R + hi), :] * _dsilu(_conv_pre(taps, w_ref, b_ref))
            dpre = dpre_e[0:R, :]
            db, dw = sums
            db = db + _colsum(dpre)
            dw = [dw[k] + _colsum(dpre * taps[k][0:R, :]) for k in range(CONV_K)]
            rows_e = _iota(R + hi, cb, 0)
            dx = dpre * w_ref[CONV_K - 1:CONV_K, :]
            for k in range(1, CONV_K):
                t = pltpu.roll(dpre_e, R + hi - k, 0)
                t = jnp.where(rows_e < R - k, t, 0.0) if last else t
                dx = dx + t[0:R, :] * w_ref[CONV_K - 1 - k:CONV_K - k, :]
            dx_ref[pl.ds(r0, R), :] = dx.astype(BF16)
            return db, dw

        zero = jnp.zeros((1, cb), F32)
        sums = chunk(0, True, n_chunks == 1, (zero, [zero] * CONV_K))
        if n_chunks > 2:
            def step(i, carry):
                db, dw = chunk(pl.multiple_of(i * R, R), False, False, (carry[0], list(carry[1:])))
                return (db,) + tuple(dw)
            carry = lax.fori_loop(1, n_chunks - 1, step, (sums[0],) + tuple(sums[1]))
            sums = (carry[0], list(carry[1:]))
        if n_chunks > 1:
            sums = chunk((n_chunks - 1) * R, False, True, sums)
        db_ref[...] = sums[0]
        for k in range(CONV_K):
            dw_ref[CONV_K - 1 - k:CONV_K - k, :] = sums[1][k]

    return pl.pallas_call(
        body, name=name, grid=(n // cb,),
        in_specs=[pl.BlockSpec((S, cb), lambda j: (0, j)), pl.BlockSpec((S, cb), lambda j: (0, j + o + C_XBC // cb)),
                  pl.BlockSpec((CONV_K, cb), lambda j: (0, j + o)), pl.BlockSpec((1, cb), lambda j: (0, j + o)),
                  D_PROJ_ANY],
        out_specs=(pl.BlockSpec((S, cb), lambda j: (0, j + o + C_XBC // cb)),
                   pl.BlockSpec((CONV_K, cb), lambda j: (0, j)), pl.BlockSpec((1, cb), lambda j: (0, j))),
        out_shape=(jax.ShapeDtypeStruct(d_proj.shape, BF16), jax.ShapeDtypeStruct((CONV_K, n), F32),
                   jax.ShapeDtypeStruct((1, n), F32)),
        input_output_aliases={4: 0}, compiler_params=_params("parallel"))(dact, proj, w, b, d_proj)


def _ssd_specs(L, order):
    G = SSM_GROUPS_PER_STEP
    W, N = G * SSM_GROUP_WIDTH, G * SSM_D_STATE
    x_spec = pl.BlockSpec((L, W), lambda g, c: (order(c), g))
    b_spec = pl.BlockSpec((L, N), lambda g, c: (order(c), 2048 // N + g))
    c_spec = pl.BlockSpec((L, N), lambda g, c: (order(c), 3072 // N + g))
    z_spec = pl.BlockSpec((L, W), lambda g, c: (order(c), C_ZS // W + g))
    dt_spec = pl.BlockSpec((G, L, SSM_HEADS_PER_GROUP), lambda g, c: (g, order(c), 0))
    p_spec = pl.BlockSpec((G, 3, SSM_HEADS_PER_GROUP), lambda g, c: (g, 0, 0))
    nw_spec = pl.BlockSpec((G, 1, SSM_GROUP_WIDTH), lambda g, c: (g, 0, 0))
    s_spec = pl.BlockSpec((G, 1, SSM_GROUP_WIDTH, SSM_D_STATE), lambda g, c: (g, order(c), 0, 0))
    return x_spec, b_spec, c_spec, z_spec, dt_spec, p_spec, nw_spec, s_spec


class _SsdGroup:
    def __init__(self, L):
        P, H, W = SSM_HEAD_DIM, SSM_HEADS_PER_GROUP, SSM_GROUP_WIDTH
        self.L = L
        self.ii, self.jj = _iota(L, L, 0), _iota(L, L, 1)
        self.lower = jnp.where(self.ii >= self.jj, 1.0, 0.0).astype(BF16)
        self.upper = jnp.where(self.ii <= self.jj, 1.0, 0.0).astype(BF16)
        self.lo = _iota(L, 2 * P, 1) < P
        self.lo_row = _iota(1, 2 * P, 1) < P
        bi, bj = _iota(W, W, 0), _iota(W, W, 1)
        self.block = jnp.where(bi // P == bj // P, 1.0, 0.0).astype(BF16)
        si, sj = _iota(2 * P, W, 0), _iota(2 * P, W, 1)
        self.pick = jnp.where(sj == si * P, 1.0, 0.0).astype(BF16)
        self.ones = jnp.ones((L, 2 * P), BF16)

    def spread(self, v4):
        R = v4.shape[0]
        lo = self.lo if R == self.L else self.lo_row
        b = lambda h: jnp.broadcast_to(v4[:, h:h + 1], (R, 2 * SSM_HEAD_DIM))
        return jnp.concatenate([jnp.where(lo, b(0), b(1)), jnp.where(lo, b(2), b(3))], axis=1)

    def gather4(self, v):
        return jnp.concatenate([v[:, h * SSM_HEAD_DIM:h * SSM_HEAD_DIM + 1] for h in range(SSM_HEADS_PER_GROUP)],
                               axis=1)

    def head_sums(self, z):
        return sum(lax.dot_general(p, self.block, _NN, preferred_element_type=F32) for p in _parts(z, 2))

    def pair_cols(self, full, pair):
        ps = full[:, pair * 128:(pair + 1) * 128]
        sw = pltpu.roll(ps, SSM_HEAD_DIM, 1)
        return jnp.where(self.lo, ps, sw), jnp.where(self.lo, sw, ps)

    def gates(self, dt4_raw, p):
        L = self.L
        dtr = self.spread(dt4_raw + p[0:1, :])
        dt = _softplus(dtr)
        A = self.spread(-jnp.exp(p[1:2, :]))
        acum = _sum_by(self.lower, dt * A)
        yield
        rows = _sum_by(self.pick, acum, _NT)
        yield
        a_last = acum[L - 1:L, :]
        cols = self.pair_cols(acum, 0) + self.pair_cols(acum, 1)
        decay, decay_t = [], []
        for h in range(SSM_HEADS_PER_GROUP):
            seg = cols[h] - rows[h:h + 1, :]
            decay.append(jnp.exp(jnp.where(self.ii >= self.jj, seg, NEG_INF)))
            decay_t.append(jnp.exp(jnp.where(self.jj >= self.ii, -seg, NEG_INF)))
        return dict(dtr=dtr, dt=dt, A=A, D=self.spread(p[2:3, :]), acum=acum, eac=jnp.exp(acum), a_last=a_last,
                    wdec=jnp.exp(a_last - acum), decay=decay, decay_t=decay_t,
                    ea_last=[jnp.exp(rows[h:h + 1, L - 1:L]) for h in range(SSM_HEADS_PER_GROUP)])


def _ssd_fwd(conv, proj, dt_raw, pvec, nw, name):
    S = conv.shape[0]
    L, P, N, H, W, G = SSM_CHUNK, SSM_HEAD_DIM, SSM_D_STATE, SSM_HEADS_PER_GROUP, SSM_GROUP_WIDTH, SSM_GROUPS_PER_STEP
    nc = S // L

    def body(x_ref, b_ref, c_ref, z_ref, dt_ref, p_ref, nw_ref, y_ref, yn_ref, s0_ref, state):
        c = pl.program_id(1)

        @pl.when(c == 0)
        def _():
            state[...] = jnp.zeros_like(state)

        k = _SsdGroup(L)

        def group(gi):
            gsl = slice(gi * W, (gi + 1) * W)
            Bm, Cm = b_ref[:, gi * N:(gi + 1) * N], c_ref[:, gi * N:(gi + 1) * N]
            x = x_ref[:, gsl].astype(F32)
            S0 = state[gsl, :]
            s0_ref[gi, 0] = S0
            CB = _dot(Cm, Bm, _NT)
            y_off = _dot(Cm, S0, _NT)
            t = yield from k.gates(dt_ref[gi], p_ref[gi])
            xdt = x * t["dt"]
            s_new = _dot(xdt * t["wdec"], Bm, _TN)
            y_diag = []
            for pair in range(H // 2):
                xp = xdt[:, pair * 128:(pair + 1) * 128]
                y_diag.append(jnp.where(k.lo, _dot(CB * t["decay"][2 * pair], xp),
                                        _dot(CB * t["decay"][2 * pair + 1], xp)))
            yield
            y = jnp.concatenate(y_diag, axis=1) + y_off * t["eac"]
            for h in range(H):
                hsl = slice(gi * W + h * P, gi * W + (h + 1) * P)
                state[hsl, :] = S0[h * P:(h + 1) * P, :] * t["ea_last"][h] + s_new[h * P:(h + 1) * P, :]
            y_ref[:, gsl] = y
            y2 = (y + t["D"] * x) * _silu(z_ref[:, gsl])
            r = lax.rsqrt(jnp.mean(y2 * y2, axis=-1, keepdims=True) + EPS)
            yn_ref[:, gsl] = (y2 * r * nw_ref[gi]).astype(BF16)

        _lockstep(group(gi) for gi in range(G))

    x_spec, b_spec, c_spec, z_spec, dt_spec, p_spec, nw_spec, s_spec = _ssd_specs(L, lambda c: c)
    y_spec = pl.BlockSpec((L, G * W), lambda g, c: (c, g))
    return pl.pallas_call(
        body, name=name, grid=(SSM_GROUPS // G, nc),
        in_specs=[x_spec, b_spec, c_spec, z_spec, dt_spec, p_spec, nw_spec],
        out_specs=(y_spec, y_spec, s_spec),
        out_shape=(jax.ShapeDtypeStruct((S, SSM_GROUPS * W), F32), jax.ShapeDtypeStruct((S, SSM_GROUPS * W), BF16),
                   jax.ShapeDtypeStruct((SSM_GROUPS, nc, W, N), F32)),
        scratch_shapes=[pltpu.VMEM((G * W, N), F32)],
        compiler_params=_params("parallel", "arbitrary"))(conv, conv, conv, proj, dt_raw, pvec, nw)


def _ssd_bwd(dyn, conv, proj, dt_raw, pvec, nw, y_ssd, states, d_proj, name):
    S = conv.shape[0]
    L, P, N, H, W, G = SSM_CHUNK, SSM_HEAD_DIM, SSM_D_STATE, SSM_HEADS_PER_GROUP, SSM_GROUP_WIDTH, SSM_GROUPS_PER_STEP
    nc = S // L

    def body(dyn_ref, x_ref, b_ref, c_ref, z_ref, dt_ref, p_ref, nw_ref, y_ref, s0_ref, _,
             dx_ref, db_ref, dc_ref, dz_ref, ddt_ref, dp_ref, dnw_ref, dstate):
        c = pl.program_id(1)

        @pl.when(c == 0)
        def _():
            dstate[...] = jnp.zeros_like(dstate)
            dp_ref[...] = jnp.zeros_like(dp_ref)
            dnw_ref[...] = jnp.zeros_like(dnw_ref)

        k = _SsdGroup(L)
        last = (_iota(L, 1, 0) == L - 1)

        def group(gi):
            gsl = slice(gi * W, (gi + 1) * W)
            Bm, Cm = b_ref[:, gi * N:(gi + 1) * N], c_ref[:, gi * N:(gi + 1) * N]
            x, z = x_ref[:, gsl].astype(F32), z_ref[:, gsl]
            S0, dS1 = s0_ref[gi, 0], dstate[gsl, :]
            CB = _dot(Cm, Bm, _NT)
            CBt = _dot(Bm, Cm, _NT)
            y_off_raw = _dot(Cm, S0, _NT)
            dXs_raw = _dot(Bm, dS1, _NT)
            t = yield from k.gates(dt_ref[gi], p_ref[gi])
            y1 = y_ref[:, gsl] + t["D"] * x
            sz = _silu(z)
            y2 = y1 * sz
            r = lax.rsqrt(jnp.mean(y2 * y2, axis=-1, keepdims=True) + EPS)
            y2h = y2 * r
            dyn_v = dyn_ref[:, gsl]
            dnw_ref[gi] += _colsum(dyn_v * y2h)
            dy2h = dyn_v * nw_ref[gi]
            dy2 = r * (dy2h - y2h * jnp.mean(dy2h * y2h, axis=-1, keepdims=True))
            dz_ref[:, gsl] = (dy2 * y1 * _dsilu(z)).astype(BF16)
            dY = dy2 * sz
            X = x * t["dt"]
            dYe = dY * t["eac"]
            dC_s = _dot(dYe, S0)
            dB_s = _dot(X * t["wdec"], dS1)
            dS_c = _dot(dYe, Cm, _TN)
            dXm, Gs, Gts = [], [], []
            for pair in range(H // 2):
                dYp, Xp = dY[:, pair * 128:(pair + 1) * 128], X[:, pair * 128:(pair + 1) * 128]
                dXm.append(jnp.where(k.lo, _dot(CBt * t["decay_t"][2 * pair], dYp),
                                     _dot(CBt * t["decay_t"][2 * pair + 1], dYp)))
                for mask in (k.lo, ~k.lo):
                    Gs.append(_dot(jnp.where(mask, dYp, 0.0), Xp, _NT))
                    Gts.append(_dot(jnp.where(mask, Xp, 0.0), dYp, _NT))
            yield
            dXs = dXs_raw * t["wdec"]
            dX = jnp.concatenate(dXm, axis=1) + dXs
            dCB, dCBt, q_sums = 0.0, 0.0, []
            for h in range(H):
                M, Mt = CB * t["decay"][h], CBt * t["decay_t"][h]
                dCB = dCB + Gs[h] * t["decay"][h]
                dCBt = dCBt + Gts[h] * t["decay_t"][h]
                d = Gs[h] * M - Gts[h] * Mt
                q_sums.append(sum(lax.dot_general(pt, k.ones, _NN, preferred_element_type=F32)
                                  for pt in _parts(d, 2)))
            q_f = jnp.concatenate([jnp.where(k.lo, q_sums[0], q_sums[1]), jnp.where(k.lo, q_sums[2], q_sums[3])],
                                  axis=1)
            x_dxs = k.head_sums(X * dXs)
            tot = [_total(dS1[h * P:(h + 1) * P, :] * S0[h * P:(h + 1) * P, :]) * t["ea_last"][h] for h in range(H)]
            tot_f = k.spread(jnp.concatenate(tot, axis=1))
            d_alast = _colsum(x_dxs) + tot_f
            dacum = q_f + k.head_sums(dY * (y_off_raw * t["eac"])) - x_dxs + jnp.where(last, d_alast, 0.0)
            dx_dt = k.head_sums(dX * x)
            d_skip = _colsum(k.head_sums(dY * x))
            for h in range(H):
                hsl = slice(gi * W + h * P, gi * W + (h + 1) * P)
                dstate[hsl, :] = t["ea_last"][h] * dS1[h * P:(h + 1) * P, :] + dS_c[h * P:(h + 1) * P, :]
            dc_s2 = _dot(dCB, Bm)
            db_s2 = _dot(dCBt, Cm)
            yield
            da = _sum_by(k.upper, dacum)
            yield
            ddt_raw = (da * t["A"] + dx_dt) * _sigmoid(t["dtr"])
            dx_ref[:, gsl] = dX * t["dt"] + t["D"] * dY
            dc_ref[:, gi * N:(gi + 1) * N] = dC_s + dc_s2
            db_ref[:, gi * N:(gi + 1) * N] = dB_s + db_s2
            ddt_ref[gi] = k.gather4(ddt_raw)
            dp_ref[gi] += k.gather4(jnp.concatenate([_colsum(ddt_raw), _colsum(da * t["dt"]) * t["A"], d_skip],
                                                    axis=0))

        _lockstep(group(gi) for gi in range(G))

    rev = lambda c: nc - 1 - c
    x_spec, b_spec, c_spec, z_spec, dt_spec, p_spec, nw_spec, s_spec = _ssd_specs(L, rev)
    y_spec = pl.BlockSpec((L, G * W), lambda g, c: (rev(c), g))
    n_spec = pl.BlockSpec((L, G * N), lambda g, c: (rev(c), g))
    return pl.pallas_call(
        body, name=name, grid=(SSM_GROUPS // G, nc),
        in_specs=[y_spec, x_spec, b_spec, c_spec, z_spec, dt_spec, p_spec, nw_spec, y_spec, s_spec, D_PROJ_ANY],
        out_specs=(y_spec, n_spec, n_spec, z_spec, dt_spec, p_spec, nw_spec),
        out_shape=(jax.ShapeDtypeStruct((S, SSM_GROUPS * W), F32), jax.ShapeDtypeStruct((S, SSM_GROUPS * N), F32),
                   jax.ShapeDtypeStruct((S, SSM_GROUPS * N), F32), jax.ShapeDtypeStruct(d_proj.shape, BF16),
                   jax.ShapeDtypeStruct((SSM_GROUPS, S, H), F32), jax.ShapeDtypeStruct((SSM_GROUPS, 3, H), F32),
                   jax.ShapeDtypeStruct((SSM_GROUPS, 1, W), F32)),
        scratch_shapes=[pltpu.VMEM((G * W, N), F32)], input_output_aliases={10: 3},
        compiler_params=_params("parallel", "arbitrary"))(dyn, conv, conv, conv, proj, dt_raw, pvec, nw, y_ssd, states,
                                                          d_proj)


def _unit_lower_inverse(A, ii, jj):
    eye = (ii == jj).astype(F32)
    same = (ii // GDN_INV_BLOCK) == (jj // GDN_INV_BLOCK)
    Ad = jnp.where(same, A, 0.0)
    Ao = A - Ad
    P2 = _dot3(Ad, Ad)
    yield
    P4, X = _dot(P2, P2), _dot3(eye - Ad, eye + P2)
    yield
    P8, X = _dot(P4, P4), X + _dot2(X, P4)
    yield
    X = X + _dot2(X, P8)
    yield
    Bm = _dot3(X, Ao)
    yield
    B2 = _dot3(Bm, Bm)
    yield
    Y = (eye - Bm) + B2 - _dot2(Bm, B2)
    yield
    T = _dot3(Y, X)
    yield
    return T


def _gdn_specs(L, order):
    G = GDN_QK_PER_STEP
    Hd, W = G * GDN_HEAD, G * GDN_V_PER_QK * GDN_HEAD
    q_spec = pl.BlockSpec((L, Hd), lambda h, c: (order(c), (C_QKV - C_XBC) // Hd + h))
    k_spec = pl.BlockSpec((L, Hd), lambda h, c: (order(c), (C_QKV - C_XBC + 1024) // Hd + h))
    v_spec = pl.BlockSpec((L, W), lambda h, c: (order(c), (C_QKV - C_XBC + 2048) // W + h))
    z_spec = pl.BlockSpec((L, W), lambda h, c: (order(c), C_ZG // W + h))
    ba_spec = pl.BlockSpec((G, L, GDN_V_PER_QK), lambda h, c: (h, order(c), 0))
    p_spec = pl.BlockSpec((G, 2, GDN_V_PER_QK), lambda h, c: (h, 0, 0))
    nw_spec = pl.BlockSpec((1, GDN_HEAD), lambda h, c: (0, 0))
    s_spec = pl.BlockSpec((G, 1, GDN_V_PER_QK * GDN_HEAD, GDN_HEAD), lambda h, c: (h, order(c), 0, 0))
    t_spec = pl.BlockSpec((G * GDN_V_PER_QK, 1, L, L), lambda h, c: (h, order(c), 0, 0))
    return q_spec, k_spec, v_spec, z_spec, ba_spec, p_spec, nw_spec, s_spec, t_spec


def _gdn_gates(qa, ka, b_col, a_col, p, j, ii, jj):
    L = qa.shape[0]
    sp_in = a_col + p[0:1, j:j + 1]
    neg_ea = -jnp.exp(p[1:2, j:j + 1])
    g = neg_ea * _softplus(sp_in)
    gcum, gcum_row = _cumsum_forms(g, ii, jj)
    rq = lax.rsqrt(_rowsum(qa * qa) + EPS)
    rk = lax.rsqrt(_rowsum(ka * ka) + EPS)
    q = qa * rq * (GDN_HEAD ** -0.5)
    k = ka * rk
    beta = _sigmoid(b_col)
    yield
    Dm = jnp.exp(jnp.where(ii >= jj, gcum - gcum_row, NEG_INF))
    eg = jnp.exp(gcum)
    g_last = gcum[L - 1:L, :]
    wdec = jnp.exp(g_last - gcum)
    return dict(rq=rq, rk=rk, q=q, k=k, beta=beta, sp_in=sp_in, neg_ea=neg_ea, g=g, Dm=Dm, kbeta=k * beta, eg=eg,
                g_last=g_last, wdec=wdec, kdec=k * wdec)


def _gdn_fwd(conv, proj, b_raw, a_raw, pvec, nw, name):
    S = conv.shape[0]
    L, Hd, J, G = GDN_CHUNK, GDN_HEAD, GDN_V_PER_QK, GDN_QK_PER_STEP
    W = J * Hd
    nc = S // L

    def body(q_ref, k_ref, v_ref, z_ref, b_ref, a_ref, p_ref, nw_ref, o_ref, on_ref, s0_ref, t_ref, state):
        c = pl.program_id(1)

        @pl.when(c == 0)
        def _():
            state[...] = jnp.zeros_like(state)

        ii, jj = _iota(L, L, 0), _iota(L, L, 1)
        for hq in range(G):
            s0_ref[hq, 0] = state[hq * W:(hq + 1) * W, :]

        def head(hq, j):
            hd = hq * J + j
            hsl, sl = slice(hq * Hd, (hq + 1) * Hd), slice(hd * Hd, (hd + 1) * Hd)
            t = yield from _gdn_gates(q_ref[:, hsl].astype(F32), k_ref[:, hsl].astype(F32), b_ref[hq][:, j:j + 1],
                                      a_ref[hq][:, j:j + 1],
                                      p_ref[hq], j, ii, jj)
            KK = _dot(t["kbeta"], t["k"], _NT)
            QK = _dot(t["q"], t["k"], _NT)
            yield
            T = yield from _unit_lower_inverse(jnp.where(ii > jj, KK * t["Dm"], 0.0), ii, jj)
            t_ref[hd, 0] = T
            S0 = state[sl, :]
            U = _dot2(T, v_ref[:, sl].astype(F32) * t["beta"])
            Wm = _dot2(T, t["kbeta"] * t["eg"])
            o_inter = _dot(t["q"] * t["eg"], S0)
            yield
            Vn = U - _dot(Wm, S0)
            yield
            o = o_inter + _dot(QK * t["Dm"], Vn)
            s_new = _dot(t["kdec"], Vn, _TN)
            yield
            state[sl, :] = S0 * jnp.exp(t["g_last"]) + s_new
            o_ref[:, sl] = o
            r = lax.rsqrt(jnp.mean(o * o, axis=-1, keepdims=True) + EPS)
            on_ref[:, sl] = ((o * r * nw_ref[...]) * _silu(z_ref[:, sl])).astype(BF16)

        _lockstep(head(hq, j) for hq in range(G) for j in range(J))

    q_spec, k_spec, v_spec, z_spec, ba_spec, p_spec, nw_spec, s_spec, t_spec = _gdn_specs(L, lambda c: c)
    o_spec = pl.BlockSpec((L, G * W), lambda h, c: (c, h))
    return pl.pallas_call(
        body, name=name, grid=(GDN_QK_HEADS // G, nc),
        in_specs=[q_spec, k_spec, v_spec, z_spec, ba_spec, ba_spec, p_spec, nw_spec],
        out_specs=(o_spec, o_spec, s_spec, t_spec),
        out_shape=(jax.ShapeDtypeStruct((S, GDN_QK_HEADS * W), F32), jax.ShapeDtypeStruct((S, GDN_QK_HEADS * W), BF16),
                   jax.ShapeDtypeStruct((GDN_QK_HEADS, nc, W, Hd), F32),
                   jax.ShapeDtypeStruct((GDN_QK_HEADS * J, nc, L, L), F32)),
        scratch_shapes=[pltpu.VMEM((G * W, Hd), F32)],
        compiler_params=_params("parallel", "arbitrary"))(conv, conv, conv, proj, b_raw, a_raw, pvec, nw)


def _gdn_bwd(don, conv, proj, b_raw, a_raw, pvec, nw, o_pre, states, t_inv, d_proj, name):
    S = conv.shape[0]
    L, Hd, J, G = GDN_CHUNK, GDN_HEAD, GDN_V_PER_QK, GDN_QK_PER_STEP
    W = J * Hd
    nc = S // L

    def body(don_ref, q_ref, k_ref, v_ref, z_ref, b_ref, a_ref, p_ref, nw_ref, o_ref, s0_ref, t_ref, _,
             dq_ref, dk_ref, dv_ref, dz_ref, db_ref, da_ref, dp_ref, dnw_ref, dstate):
        c = pl.program_id(1)

        @pl.when(c == 0)
        def _():
            dstate[...] = jnp.zeros_like(dstate)
            dp_ref[...] = jnp.zeros_like(dp_ref)
            dnw_ref[...] = jnp.zeros_like(dnw_ref)

        ii, jj = _iota(L, L, 0), _iota(L, L, 1)
        last = (_iota(L, 1, 0) == L - 1)
        res = {}

        def head(hq, j):
            hd = hq * J + j
            hsl, sl = slice(hq * Hd, (hq + 1) * Hd), slice(hd * Hd, (hd + 1) * Hd)
            qa, ka = q_ref[:, hsl].astype(F32), k_ref[:, hsl].astype(F32)
            t = yield from _gdn_gates(qa, ka, b_ref[hq][:, j:j + 1], a_ref[hq][:, j:j + 1], p_ref[hq], j, ii, jj)
            q, k, beta, eg, Dm, kbeta, kdec = (t[nm] for nm in ("q", "k", "beta", "eg", "Dm", "kbeta", "kdec"))
            T = t_ref[hd, 0]
            v, z, o = v_ref[:, sl].astype(F32), z_ref[:, sl], o_ref[:, sl]
            S0, dS1 = s0_ref[hq, 0, j * Hd:(j + 1) * Hd, :], dstate[sl, :]
            sz = _silu(z)
            r = lax.rsqrt(jnp.mean(o * o, axis=-1, keepdims=True) + EPS)
            oh = o * r
            d_on = don_ref[:, sl]
            dz_ref[:, sl] = (d_on * (oh * nw_ref[...]) * _dsilu(z)).astype(BF16)
            dn = d_on * sz
            dnw_part = _colsum(dn * oh)
            doh = dn * nw_ref[...]
            dO = r * (doh - oh * jnp.mean(doh * oh, axis=-1, keepdims=True))
            Rw = kbeta * eg
            qe = q * eg
            U = _dot2(T, v * beta)
            Wm = _dot2(T, Rw)
            KK = _dot(kbeta, k, _NT)
            QK = _dot(q, k, _NT)
            o_inter = _dot(qe, S0)
            dq_s = _dot(dO, S0, _NT)
            dS_q = _dot(qe, dO, _TN)
            yield
            Am = jnp.where(ii > jj, KK * Dm, 0.0)
            Pm = QK * Dm
            Vn = U - _dot(Wm, S0)
            dVn_s = _dot(kdec, dS1)
            yield
            dVn = _dot(Pm, dO, _TN) + dVn_s
            dP = _dot(dO, Vn, _NT)
            dKd = _dot(Vn, dS1, _NT)
            yield
            dQK = dP * Dm
            dq = _dot(dQK, k) + dq_s * eg
            dk = _dot(dQK, q, _TN) + dKd * t["wdec"]
            dstate[sl, :] = jnp.exp(t["g_last"]) * dS1 + dS_q - _dot(Wm, dVn, _TN)
            dW = -_dot(dVn, S0, _NT)
            dRu = _dot2(T, dVn, _TN)
            yield
            dRw = _dot2(T, dW, _TN)
            dA_u = _dot(dRu, U, _NT)
            yield
            dA = jnp.where(ii > jj, -(dA_u + _dot(dRw, Wm, _NT)), 0.0)
            yield
            dKK = dA * Dm
            dkbeta = _dot(dKK, k) + dRw * eg
            dk = dk + _dot(dKK, kbeta, _TN)
            yield
            dk = dk + dkbeta * beta
            dbeta = _rowsum(dkbeta * k) + _rowsum(dRu * v)
            dv_ref[:, sl] = dRu * beta
            Q = dA * Am + dP * Pm
            rho = _rowsum(dKd * kdec)
            d_glast = _colsum(rho) + jnp.exp(t["g_last"]) * _total(dS1 * S0)
            q_sums = _row_col_sums(Q)
            rest = _rowsum(dRw * Rw) + _rowsum(dO * o_inter) - rho + jnp.where(last, d_glast, 0.0)
            yield
            dg = _rev_cumsum_col(q_sums + rest, ii, jj)
            yield
            da_raw = dg * t["neg_ea"] * _sigmoid(t["sp_in"])
            res[hq, j] = dict(dq=dq, dk=dk, db=dbeta * beta * (1.0 - beta), da=da_raw, d_bias=_colsum(da_raw),
                              d_alog=_colsum(dg * t["g"]), dnw=dnw_part, rq=t["rq"], rk=t["rk"], k=k, qh=qa * t["rq"])

        _lockstep(head(hq, j) for hq in range(G) for j in range(J))
        for hq in range(G):
            parts = [res[hq, j] for j in range(J)]
            hsl = slice(hq * Hd, (hq + 1) * Hd)
            p0 = parts[0]
            dqh = sum(pt["dq"] for pt in parts) * (GDN_HEAD ** -0.5)
            dkn = sum(pt["dk"] for pt in parts)
            dq_ref[:, hsl] = p0["rq"] * (dqh - p0["qh"] * _rowsum(dqh * p0["qh"]))
            dk_ref[:, hsl] = p0["rk"] * (dkn - p0["k"] * _rowsum(dkn * p0["k"]))
            db_ref[hq] = jnp.concatenate([pt["db"] for pt in parts], axis=1)
            da_ref[hq] = jnp.concatenate([pt["da"] for pt in parts], axis=1)
            dp_ref[hq] += jnp.concatenate([jnp.concatenate([pt["d_bias"] for pt in parts], axis=1),
                                           jnp.concatenate([pt["d_alog"] for pt in parts], axis=1)], axis=0)
            dnw_ref[hq] += sum(pt["dnw"] for pt in parts)

    rev = lambda c: nc - 1 - c
    q_spec, k_spec, v_spec, z_spec, ba_spec, p_spec, nw_spec, s_spec, t_spec = _gdn_specs(L, rev)
    o_spec = pl.BlockSpec((L, G * W), lambda h, c: (rev(c), h))
    h_spec = pl.BlockSpec((L, G * Hd), lambda h, c: (rev(c), h))
    dnw_spec = pl.BlockSpec((G, 1, Hd), lambda h, c: (h, 0, 0))
    return pl.pallas_call(
        body, name=name, grid=(GDN_QK_HEADS // G, nc),
        in_specs=[o_spec, q_spec, k_spec, v_spec, z_spec, ba_spec, ba_spec, p_spec, nw_spec, o_spec, s_spec, t_spec,
                  D_PROJ_ANY],
        out_specs=(h_spec, h_spec, o_spec, z_spec, ba_spec, ba_spec, p_spec, dnw_spec),
        out_shape=(jax.ShapeDtypeStruct((S, GDN_QK_HEADS * Hd), F32), jax.ShapeDtypeStruct((S, GDN_QK_HEADS * Hd), F32),
                   jax.ShapeDtypeStruct((S, GDN_QK_HEADS * W), F32), jax.ShapeDtypeStruct(d_proj.shape, BF16),
                   jax.ShapeDtypeStruct((GDN_QK_HEADS, S, J), F32), jax.ShapeDtypeStruct((GDN_QK_HEADS, S, J), F32),
                   jax.ShapeDtypeStruct((GDN_QK_HEADS, 2, J), F32), jax.ShapeDtypeStruct((GDN_QK_HEADS, 1, Hd), F32)),
        scratch_shapes=[pltpu.VMEM((G * W, Hd), F32)], input_output_aliases={12: 3},
        compiler_params=_params("parallel", "arbitrary"))(don, conv, conv, conv, proj, b_raw, a_raw, pvec, nw, o_pre,
                                                          states, t_inv, d_proj)


def _ada_fwd(c_all, w_loc, b_loc, name):
    n = w_loc.shape[1]

    def body(c_ref, w_ref, b_ref, o_ref):
        o_ref[...] = _dot3(_silu(c_ref[...]), w_ref[...]) + b_ref[...]

    return pl.pallas_call(body, name=name, out_shape=jax.ShapeDtypeStruct((N_DEV, n), F32),
                          compiler_params=pltpu.CompilerParams(vmem_limit_bytes=VMEM_LIMIT))(c_all, w_loc, b_loc)


def _ada_bwd(c_all_t, dmod_cols, name):
    Dm, n = c_all_t.shape[0], dmod_cols.shape[1]

    def body(c_ref, d_ref, o_ref):
        ca = _silu(c_ref[...])
        acc = ca[:, 0:1] * d_ref[0:1, :]
        for i in range(1, N_DEV):
            acc = acc + ca[:, i:i + 1] * d_ref[i:i + 1, :]
        o_ref[...] = acc

    return pl.pallas_call(body, name=name, out_shape=jax.ShapeDtypeStruct((Dm, n), F32),
                          compiler_params=pltpu.CompilerParams(vmem_limit_bytes=VMEM_LIMIT))(c_all_t, dmod_cols)


ADAM_BLOCK_BYTES = 12 * 1024 * 1024


def _adam(contrib, w, m, v, name):
    n, R, C = contrib.shape
    tr = R
    while tr % 16 == 0 and (n + 7) * tr * C * 4 > ADAM_BLOCK_BYTES:
        tr //= 2

    def body(c_ref, w_ref, m_ref, v_ref, g_ref, d_ref, nm_ref, nv_ref):
        g = c_ref[0].astype(F32)
        for i in range(1, n):
            g = g + c_ref[i].astype(F32)
        nm = ADAM_B1 * m_ref[...] + (1.0 - ADAM_B1) * g
        nv = ADAM_B2 * v_ref[...] + (1.0 - ADAM_B2) * (g * g)
        m_hat = nm / (1.0 - ADAM_B1 ** ADAM_STEP)
        v_hat = nv / (1.0 - ADAM_B2 ** ADAM_STEP)
        g_ref[...] = g
        d_ref[...] = -ADAM_LR * (m_hat / (jnp.sqrt(v_hat) + ADAM_EPS) + ADAM_WD * w_ref[...])
        nm_ref[...] = nm
        nv_ref[...] = nv

    spec = pl.BlockSpec((tr, C), lambda i: (i, 0))
    shp = jax.ShapeDtypeStruct((R, C), F32)
    return pl.pallas_call(
        body, name=name, grid=(R // tr,), in_specs=[pl.BlockSpec((n, tr, C), lambda i: (0, i, 0)), spec, spec, spec],
        out_specs=(spec,) * 4, out_shape=(shp,) * 4, compiler_params=_params("parallel"))(contrib, w, m, v)


def _exchange(arrays, modes, name, chips=False):
    n = len(arrays)
    out_shape = tuple(jax.ShapeDtypeStruct((N_DEV,) + a.shape if md == "gather" else a.shape, a.dtype)
                      for a, md in zip(arrays, modes))

    def body(*refs):
        ins, outs = refs[:n], refs[n:2 * n]
        send_sems, recv_sems, loc_sems = refs[2 * n:]
        me, peers = _peer_table(chips)

        def src(k, slot):
            return ins[k] if modes[k] == "gather" else ins[k].at[slot]

        def remote(k, m, to_slot, land_slot):
            return pltpu.make_async_remote_copy(
                src_ref=src(k, to_slot), dst_ref=outs[k].at[land_slot], send_sem=send_sems.at[k, m],
                recv_sem=recv_sems.at[k, m], device_id=peers[m][0], device_id_type=pl.DeviceIdType.MESH)

        local = [pltpu.make_async_copy(src(k, me), outs[k].at[me], loc_sems.at[k]) for k in range(n)]
        for cp in local:
            cp.start()
        sends = [remote(k, m, peers[m][1], me) for m in range(len(peers)) for k in range(n)]
        for cp in sends:
            cp.start()
        for m in range(len(peers)):
            for k in range(n):
                remote(k, m, peers[m][1], peers[m][1]).wait_recv()
        for cp in sends:
            cp.wait_send()
        for cp in local:
            cp.wait()

    any_spec = pl.BlockSpec(memory_space=pl.ANY)
    return pl.pallas_call(
        body, name=name, in_specs=[any_spec] * n, out_specs=(any_spec,) * n, out_shape=out_shape,
        scratch_shapes=[pltpu.SemaphoreType.DMA((n, N_DEV - 1)), pltpu.SemaphoreType.DMA((n, N_DEV - 1)),
                        pltpu.SemaphoreType.DMA((n,))])(*arrays)


def _gather_two_level(arrays, name):
    n = len(arrays)
    out_shape = tuple(jax.ShapeDtypeStruct((N_DEV,) + a.shape, a.dtype) for a in arrays)

    def body(*refs):
        ins, outs = refs[:n], refs[n:2 * n]
        send_sems, recv_sems, loc_sems = refs[2 * n:]
        ix, iy, ic = lax.axis_index("x"), lax.axis_index("y"), lax.axis_index("c")
        lin = lambda px, py, pc: 4 * px + 2 * py + pc
        me, sib = lin(ix, iy, ic), (ix, iy, 1 - ic)
        chips = [(1 - ix, iy), (ix, 1 - iy), (1 - ix, 1 - iy)]

        def copy(k, s, block, to, src=None):
            return pltpu.make_async_remote_copy(
                src_ref=outs[k].at[block] if src is None else src, dst_ref=outs[k].at[block],
                send_sem=send_sems.at[k, s], recv_sem=recv_sems.at[k, s], device_id=to,
                device_id_type=pl.DeviceIdType.MESH)

        local = [pltpu.make_async_copy(ins[k], outs[k].at[me], loc_sems.at[k]) for k in range(n)]
        for cp in local:
            cp.start()
        first = [copy(k, 1 + j, me, (cx, cy, ic), src=ins[k]) for j, (cx, cy) in enumerate(chips) for k in range(n)]
        first += [copy(k, 0, me, sib, src=ins[k]) for k in range(n)]
        for cp in first:
            cp.start()
        passed = []
        for j, (cx, cy) in enumerate(chips):
            for k in range(n):
                copy(k, 1 + j, lin(cx, cy, ic), sib).wait_recv()
                passed.append(copy(k, 4 + j, lin(cx, cy, ic), sib))
                passed[-1].start()
        for k in range(n):
            copy(k, 0, lin(*sib), sib).wait_recv()
            for j, (cx, cy) in enumerate(chips):
                copy(k, 4 + j, lin(cx, cy, 1 - ic), sib).wait_recv()
        for cp in first + passed:
            cp.wait_send()
        for cp in local:
            cp.wait()

    any_spec = pl.BlockSpec(memory_space=pl.ANY)
    return pl.pallas_call(
        body, name=name, in_specs=[any_spec] * n, out_specs=(any_spec,) * n, out_shape=out_shape,
        scratch_shapes=[pltpu.SemaphoreType.DMA((n, N_DEV - 1)), pltpu.SemaphoreType.DMA((n, N_DEV - 1)),
                        pltpu.SemaphoreType.DMA((n,))])(*arrays)


def _peer_table(chips=False):
    ix, iy, ic = lax.axis_index("x"), lax.axis_index("y"), lax.axis_index("c")
    peers = []
    for m in ((2, 4, 6) if chips else range(1, N_DEV)):
        px = 1 - ix if m & 4 else ix
        py = 1 - iy if m & 2 else iy
        pc = 1 - ic if m & 1 else ic
        peers.append(((px, py, pc), 2 * px + py if chips else 4 * px + 2 * py + pc))
    return (2 * ix + iy if chips else 4 * ix + 2 * iy + ic), peers


def _exchange_start(arrays, modes, after, name, chips=False):
    n = len(arrays)
    land_shapes = [(N_DEV,) + a.shape if md == "gather" else a.shape for a, md in zip(arrays, modes)]

    def body(*refs):
        ins, lands = refs[:n], refs[n:2 * n]
        send_sems, recv_sems = refs[2 * n + 1], refs[2 * n + 2]
        token = refs[-1]
        me, peers = _peer_table(chips)

        def src(k, slot):
            return ins[k] if modes[k] == "gather" else ins[k].at[slot]

        for peer, slot in peers:
            for k in range(n):
                pltpu.make_async_remote_copy(
                    src_ref=src(k, slot), dst_ref=lands[k].at[me], send_sem=send_sems, recv_sem=recv_sems,
                    device_id=peer, device_id_type=pl.DeviceIdType.MESH).start()
        token[...] = jnp.zeros_like(token)

    hbm = pl.BlockSpec(memory_space=pltpu.HBM)
    sem = pl.BlockSpec(memory_space=pltpu.SEMAPHORE)
    sem_shape = pltpu.SemaphoreType.DMA(())
    operands = [pltpu.with_memory_space_constraint(a, pltpu.HBM) for a in arrays]
    operands += [pltpu.with_memory_space_constraint(lax.empty(s, a.dtype), pltpu.HBM)
                 for s, a in zip(land_shapes, arrays)]
    out = pl.pallas_call(
        body, name=name,
        out_shape=(sem_shape, sem_shape) + tuple(pltpu.HBM(a.shape, a.dtype) for a in arrays)
        + tuple(pltpu.HBM(s, a.dtype) for s, a in zip(land_shapes, arrays)) + (jax.ShapeDtypeStruct((8, 128), F32),),
        in_specs=[hbm] * (2 * n) + [pl.BlockSpec(memory_space=pl.ANY)],
        out_specs=(sem, sem) + (hbm,) * (2 * n) + (pl.BlockSpec(memory_space=pltpu.VMEM),),
        input_output_aliases={i: 2 + i for i in range(2 * n)},
        compiler_params=pltpu.CompilerParams(has_side_effects=pltpu.SideEffectType.DATAFLOW_SIDE_EFFECTING))(
            *operands, after)
    return out[0], out[1], out[2:2 + n], out[2 + n:2 + 2 * n], out[-1]


def _exchange_wait(started, modes, after, name, chips=False):
    send_sems, recv_sems, sent, lands, _ = started
    n = len(sent)

    def body(*refs):
        ins, zones = refs[:n], refs[n:2 * n]
        send_ref, recv_ref = refs[2 * n], refs[2 * n + 1]
        _, peers = _peer_table(chips)

        def src(k, slot):
            return ins[k] if modes[k] == "gather" else ins[k].at[slot]

        for peer, slot in peers:
            for k in range(n):
                cp = pltpu.make_async_remote_copy(
                    src_ref=src(k, slot), dst_ref=zones[k].at[slot], send_sem=send_ref, recv_sem=recv_ref,
                    device_id=peer, device_id_type=pl.DeviceIdType.MESH)
                cp.wait_send()
                cp.wait_recv()

    hbm = pl.BlockSpec(memory_space=pltpu.HBM)
    sem = pl.BlockSpec(memory_space=pltpu.SEMAPHORE)
    out = pl.pallas_call(
        body, name=name,
        out_shape=tuple(pltpu.HBM(a.shape, a.dtype) for a in sent) + tuple(pltpu.HBM(a.shape, a.dtype) for a in lands),
        in_specs=[hbm] * (2 * n) + [sem, sem, pl.BlockSpec(memory_space=pl.ANY)], out_specs=(hbm,) * (2 * n),
        input_output_aliases={i: i for i in range(2 * n)},
        compiler_params=pltpu.CompilerParams(has_side_effects=pltpu.SideEffectType.DATAFLOW_SIDE_EFFECTING))(
            *sent, *lands, send_sems, recv_sems, after)
    ix, iy, ic = lax.axis_index("x"), lax.axis_index("y"), lax.axis_index("c")
    me = 2 * ix + iy if chips else 4 * ix + 2 * iy + ic
    filled = []
    for k in range(n):
        own = sent[k] if modes[k] == "gather" else lax.dynamic_index_in_dim(sent[k], me, axis=0, keepdims=False)
        filled.append(lax.dynamic_update_index_in_dim(out[n + k], own, me, axis=0))
    return filled


def _swap_sibling(to_c0, to_c1, name):
    def body(c0_ref, c1_ref, out_ref, send_sem, recv_sem):
        ix, iy, ic = lax.axis_index("x"), lax.axis_index("y"), lax.axis_index("c")

        def copy(src):
            return pltpu.make_async_remote_copy(src_ref=src, dst_ref=out_ref, send_sem=send_sem, recv_sem=recv_sem,
                                                device_id=(ix, iy, 1 - ic), device_id_type=pl.DeviceIdType.MESH)

        @pl.when(ic == 0)
        def _():
            copy(c1_ref).start()

        @pl.when(ic == 1)
        def _():
            copy(c0_ref).start()

        copy(c0_ref).wait()

    any_spec = pl.BlockSpec(memory_space=pl.ANY)
    return pl.pallas_call(body, name=name, in_specs=[any_spec, any_spec], out_specs=any_spec,
                          out_shape=jax.ShapeDtypeStruct(to_c0.shape, to_c0.dtype),
                          scratch_shapes=[pltpu.SemaphoreType.DMA, pltpu.SemaphoreType.DMA])(to_c0, to_c1)


def _add_pair(to_c0, to_c1, got, name):
    n, R, C = got.shape
    tr = _blk(R, 256)

    def body(c0_ref, c1_ref, got_ref, o_ref):
        ic = lax.axis_index("c")

        @pl.when(ic == 0)
        def _():
            o_ref[...] = (c0_ref[...].astype(F32) + got_ref[...].astype(F32)).astype(o_ref.dtype)

        @pl.when(ic == 1)
        def _():
            o_ref[...] = (c1_ref[...].astype(F32) + got_ref[...].astype(F32)).astype(o_ref.dtype)

    spec = pl.BlockSpec((1, tr, C), lambda i, j: (i, j, 0))
    return pl.pallas_call(body, name=name, grid=(n, R // tr), in_specs=[spec, spec, spec], out_specs=spec,
                          out_shape=jax.ShapeDtypeStruct(got.shape, got.dtype),
                          compiler_params=_params("parallel", "parallel"))(to_c0, to_c1, got)


W_IN_SPLITS = (0, 2048, 6144, 6176, 10272, 12320, 12336, 12352, 13376, 14400)
N_REPLICATED = 16640
REPLICATED = ("b_ada", "norm_mix_pre", "norm_mix_post", "ssm_conv_b", "ssm_dt_bias", "ssm_A_log", "ssm_D",
              "ssm_norm_w", "gdn_dt_bias", "gdn_A_log", "gdn_norm_w", "norm_mlp_pre", "norm_mlp_post")
WEIGHTS = ("w_ada", "b_ada", "norm_mix_pre", "norm_mix_post", "w_in", "ssm_conv_w", "ssm_conv_b", "ssm_dt_bias",
           "ssm_A_log", "ssm_D", "ssm_norm_w", "gdn_conv_w", "gdn_dt_bias", "gdn_A_log", "gdn_norm_w", "w_ssm_up",
           "w_gdn_up", "w_out", "norm_mlp_pre", "norm_mlp_post", "w_mlp_up", "w_mlp_down")


def _cols_of_shards(g, a, b):
    width, pieces = g.shape[2], []
    while a < b:
        i = a // width
        hi = min(b, (i + 1) * width)
        pieces.append(g[i][:, a - i * width:hi - i * width])
        a = hi
    return pieces


ORIG_SEGMENTS = ((0, 6144, "main", 0), (6144, 6176, "small", 0), (6176, 12320, "main", 6144),
                 (12320, 12352, "small", 32), (12352, 14400, "main", 12288))


def _orig_cols(main_cols, small_cols, a, b):
    pieces = []
    for s0, s1, which, off in ORIG_SEGMENTS:
        lo, hi = max(a, s0), min(b, s1)
        if lo < hi:
            pieces.append((main_cols if which == "main" else small_cols)[:, off + lo - s0:off + hi - s0])
    return jnp.concatenate(pieces, axis=1)


def _by_cols(t):
    return t.transpose(1, 0, 2).reshape(t.shape[1], N_DEV * t.shape[2])


def _to_col_shards(t):
    R, C8 = t.shape
    return t.reshape(R, N_DEV, C8 // N_DEV).transpose(1, 0, 2)


def _heads_first(t, groups):
    S = t.shape[0]
    return t.reshape(S, groups, t.shape[1] // groups).transpose(1, 0, 2)


def _heads_last(t):
    return t.transpose(1, 0, 2).reshape(t.shape[1], t.shape[0] * t.shape[2])


def kernel(x, c, w_ada, b_ada, norm_mix_pre, norm_mix_post, w_in, ssm_conv_w, ssm_conv_b, ssm_dt_bias, ssm_A_log, ssm_D, ssm_norm_w, gdn_conv_w, gdn_dt_bias, gdn_A_log, gdn_norm_w, w_ssm_up, w_gdn_up, w_out, norm_mlp_pre, norm_mlp_post, w_mlp_up, w_mlp_down, loss_target, m_w_ada, m_b_ada, m_norm_mix_pre, m_norm_mix_post, m_w_in, m_ssm_conv_w, m_ssm_conv_b, m_ssm_dt_bias, m_ssm_A_log, m_ssm_D, m_ssm_norm_w, m_gdn_conv_w, m_gdn_dt_bias, m_gdn_A_log, m_gdn_norm_w, m_w_ssm_up, m_w_gdn_up, m_w_out, m_norm_mlp_pre, m_norm_mlp_post, m_w_mlp_up, m_w_mlp_down, v_w_ada, v_b_ada, v_norm_mix_pre, v_norm_mix_post, v_w_in, v_ssm_conv_w, v_ssm_conv_b, v_ssm_dt_bias, v_ssm_A_log, v_ssm_D, v_ssm_norm_w, v_gdn_conv_w, v_gdn_dt_bias, v_gdn_A_log, v_gdn_norm_w, v_w_ssm_up, v_w_gdn_up, v_w_out, v_norm_mlp_pre, v_norm_mlp_post, v_w_mlp_up, v_w_mlp_down):
    S, Dm = x.shape[1], D_MODEL
    me = 4 * lax.axis_index("x") + 2 * lax.axis_index("y") + lax.axis_index("c")
    x2, tgt = x[0], loss_target[0]
    n_ada = w_ada.shape[2]
    given = dict(
        w_ada=(w_ada, m_w_ada, v_w_ada), b_ada=(b_ada, m_b_ada, v_b_ada),
        norm_mix_pre=(norm_mix_pre, m_norm_mix_pre, v_norm_mix_pre),
        norm_mix_post=(norm_mix_post, m_norm_mix_post, v_norm_mix_post), w_in=(w_in, m_w_in, v_w_in),
        ssm_conv_w=(ssm_conv_w, m_ssm_conv_w, v_ssm_conv_w), ssm_conv_b=(ssm_conv_b, m_ssm_conv_b, v_ssm_conv_b),
        ssm_dt_bias=(ssm_dt_bias, m_ssm_dt_bias, v_ssm_dt_bias), ssm_A_log=(ssm_A_log, m_ssm_A_log, v_ssm_A_log),
        ssm_D=(ssm_D, m_ssm_D, v_ssm_D), ssm_norm_w=(ssm_norm_w, m_ssm_norm_w, v_ssm_norm_w),
        gdn_conv_w=(gdn_conv_w, m_gdn_conv_w, v_gdn_conv_w), gdn_dt_bias=(gdn_dt_bias, m_gdn_dt_bias, v_gdn_dt_bias),
        gdn_A_log=(gdn_A_log, m_gdn_A_log, v_gdn_A_log), gdn_norm_w=(gdn_norm_w, m_gdn_norm_w, v_gdn_norm_w),
        w_ssm_up=(w_ssm_up, m_w_ssm_up, v_w_ssm_up), w_gdn_up=(w_gdn_up, m_w_gdn_up, v_w_gdn_up),
        w_out=(w_out, m_w_out, v_w_out), norm_mlp_pre=(norm_mlp_pre, m_norm_mlp_pre, v_norm_mlp_pre),
        norm_mlp_post=(norm_mlp_post, m_norm_mlp_post, v_norm_mlp_post), w_mlp_up=(w_mlp_up, m_w_mlp_up, v_w_mlp_up),
        w_mlp_down=(w_mlp_down, m_w_mlp_down, v_w_mlp_down))

    (c_all, scw, gcw, g_in) = _gather_two_level([c, ssm_conv_w[0], gdn_conv_w[0], w_in[0].astype(BF16)], "gather_w_in")
    c_all = c_all.reshape(N_DEV, Dm)
    sp = W_IN_SPLITS
    w_main = jnp.concatenate(_cols_of_shards(g_in, sp[0], sp[2]) + _cols_of_shards(g_in, sp[3], sp[5])
                             + _cols_of_shards(g_in, sp[7], sp[9]), axis=1)
    w_small = jnp.concatenate(_cols_of_shards(g_in, sp[2], sp[3]) + _cols_of_shards(g_in, sp[5], sp[7])
                              + [jnp.zeros((Dm, N_SMALL - 64), BF16)], axis=1)
    conv_w = jnp.concatenate([_by_cols(scw), _by_cols(gcw)], axis=1)
    conv_b = jnp.concatenate([ssm_conv_b, jnp.zeros_like(ssm_conv_b)], axis=1)

    b_loc = lax.dynamic_slice(b_ada, (0, me * n_ada), (1, n_ada))
    mod_part = _ada_fwd(c_all, w_ada[0], b_loc, "ada_fwd")
    (mod_rows,) = _exchange([mod_part.reshape(N_DEV, 1, n_ada)], ["a2a"], "exchange_mod")
    rest = _exchange_start([w_ssm_up[0].astype(BF16), w_gdn_up[0].astype(BF16), w_out[0].astype(BF16),
                            w_mlp_up[0].astype(BF16), w_mlp_down[0].astype(BF16)], ["gather"] * 5, mod_rows,
                           "gather_rest_start")
    mod = mod_rows.reshape(1, 6 * Dm) + rest[4][0:1, 0:1]
    sh1, sc1, g1, sh2, sc2, g2 = [mod[:, i * Dm:(i + 1) * Dm] for i in range(6)]

    h = _pre_fwd(x2, norm_mix_pre, sc1, sh1, "pre_mix")
    proj = _mm(h, w_main, S, N_MAIN, Dm, mode="nn", out_dtype=F32, name="proj_main")
    small = _mm(h, w_small, S, N_SMALL, Dm, mode="nn", out_dtype=F32, name="proj_small")
    conv = _conv_fwd(proj, conv_w, conv_b, "conv_fwd")
    dt_g, b_g, a_g = _heads_first(small[:, 0:32], 8), _heads_first(small[:, 32:48], 8), _heads_first(small[:, 48:64], 8)
    pv_ssm = jnp.stack([ssm_dt_bias.reshape(8, 4), ssm_A_log.reshape(8, 4), ssm_D.reshape(8, 4)], axis=1)
    nw_ssm = ssm_norm_w.reshape(8, 1, SSM_GROUP_WIDTH)
    pv_gdn = jnp.stack([gdn_dt_bias.reshape(8, 2), gdn_A_log.reshape(8, 2)], axis=1)
    y_ssd, ysn, st_ssm = _ssd_fwd(conv, proj, dt_g, pv_ssm, nw_ssm, "ssd_fwd")
    o_pre, ogn, st_gdn, t_inv = _gdn_fwd(conv, proj, b_g, a_g, pv_gdn, gdn_norm_w, "gdn_fwd")
    g_su, g_gu, g_out, g_mu, g_md = _exchange_wait(rest, ["gather"] * 5, ogn, "gather_rest_wait")
    w_su, w_gu = g_su.reshape(2 * Dm, Dm), g_gu.reshape(2 * Dm, Dm)
    w_o, w_mu, w_md = g_out.reshape(Dm, Dm), _by_cols(g_mu), g_md.reshape(4 * Dm, Dm)
    ys = _mm(ysn, w_su, S, Dm, 2 * Dm, mode="nn", out_dtype=F32, name="ssm_up")
    yg, merged = _gdn_up_merge(ogn, w_gu, ys, proj, "gdn_up_merge")
    mo, x1, h2 = _mix_out_post_pre(merged, w_o, x2, norm_mix_post, g1, norm_mlp_pre, sc2, sh2, "mix_out_post_pre")
    u, act = _mm(h2, w_mu, S, 4 * Dm, Dm, mode="nn", out_dtype=F32, epi="relu2", name="mlp_up")
    y_mlp = _mm(act, w_md, S, Dm, 4 * Dm, mode="nn", out_dtype=F32, name="mlp_down")
    dx2, loss_loc, dy, dg2, dw_post2 = _final_fwd_bwd(x1, y_mlp, norm_mlp_post, g2, tgt, "post_mlp_loss_bwd")

    du = _mm(dy, w_md, S, 4 * Dm, Dm, mode="nt", out_dtype=BF16, epi="drelu2", extra=u, name="mlp_down_dx")
    gw_md = _mm(act, dy, 4 * Dm, Dm, S, mode="tn", out_dtype=BF16, name="mlp_down_dw")
    dh2 = _mm(du, w_mu, S, Dm, 4 * Dm, mode="nt", out_dtype=F32, name="mlp_up_dx")
    gw_mu = _mm(h2, du, Dm, 4 * Dm, S, mode="tn", out_dtype=BF16, name="mlp_up_dw")
    mlp_x = _exchange_start([_to_col_shards(gw_mu), gw_md.reshape(N_DEV, -1, Dm)], ["a2a"] * 2, gw_md,
                            "grads_mlp_start")
    dx1, dsh2, dsc2, dw_pre2 = _pre_bwd(dh2, x1, norm_mlp_pre, sc2 + mlp_x[4][0:1, 0:1], dx2, "pre_mlp_bwd")
    dmo, dg1, dw_post1 = _post_bwd(dx1, mo, norm_mix_post, g1, "post_mix_bwd")
    gw_o = _mm(merged, dmo, Dm, Dm, S, mode="tn", out_dtype=BF16, name="mix_out_dw")
    dys, dyg, d_proj = _mix_out_dx_merge_bwd(dmo, w_o, ys, yg, proj, lax.empty((S, N_MAIN), BF16), "mix_out_dx_merge")
    dysn = _mm(dys, w_su, S, 2 * Dm, Dm, mode="nt", out_dtype=F32, name="ssm_up_dx")
    gw_su = _mm(ysn, dys, 2 * Dm, Dm, S, mode="tn", out_dtype=BF16, name="ssm_up_dw")
    dogn = _mm(dyg, w_gu, S, 2 * Dm, Dm, mode="nt", out_dtype=F32, name="gdn_up_dx")
    gw_gu = _mm(ogn, dyg, 2 * Dm, Dm, S, mode="tn", out_dtype=BF16, name="gdn_up_dw")
    mix_x = _exchange_start([gw_su.reshape(N_DEV, -1, Dm), gw_gu.reshape(N_DEV, -1, Dm), gw_o.reshape(N_DEV, -1, Dm)],
                            ["a2a"] * 3, gw_gu, "grads_mix_start")
    dxs, dBm, dCm, d_proj, ddt_g, dpv_ssm, dnw_ssm = _ssd_bwd(dysn, conv, proj, dt_g, pv_ssm + mix_x[4][0, 0], nw_ssm,
                                                              y_ssd, st_ssm, d_proj, "ssd_bwd")
    dq, dk, dv, d_proj, db_g, da_g, dpv_gdn, dnw_gdn = _gdn_bwd(dogn, conv, proj, b_g, a_g, pv_gdn, gdn_norm_w, o_pre,
                                                                st_gdn, t_inv, d_proj, "gdn_bwd")
    conv_pieces = []
    for nm, d_act, col0 in (("xs", dxs, 0), ("B", dBm, 2048), ("C", dCm, 3072), ("q", dq, 4096), ("k", dk, 5120),
                            ("v", dv, 6144)):
        d_proj, dw_piece, db_piece = _conv_bwd(d_act, proj, conv_w, conv_b, col0, d_proj, "conv_bwd_" + nm)
        conv_pieces.append((dw_piece, db_piece))
    d_small = jnp.concatenate([_heads_last(ddt_g), _heads_last(db_g), _heads_last(da_g),
                               jnp.zeros((S, N_SMALL - 64), F32)], axis=1).astype(BF16)
    gw_small = _mm(h, d_small, Dm, N_SMALL, S, mode="tn", out_dtype=BF16, name="proj_small_dw")
    main_cols = _mm(h, d_proj, Dm, N_MAIN, S, mode="tn", out_dtype=BF16, name="proj_main_dw")
    n_shard = w_in.shape[2]
    slabs = [_orig_cols(main_cols, gw_small, i * n_shard, (i + 1) * n_shard) for i in range(N_DEV)]
    to_c0, to_c1 = jnp.stack(slabs[0::2]), jnp.stack(slabs[1::2])
    chip_sum = _add_pair(to_c0, to_c1, _swap_sibling(to_c0, to_c1, "grads_w_in_pair"), "grads_w_in_pair_sum")
    in_x = _exchange_start([chip_sum], ["a2a"], gw_small, "grads_w_in_start", chips=True)
    dh = _mm(d_small, w_small + in_x[4][0:1, 0:1].astype(BF16), S, Dm, N_SMALL, mode="nt", out_dtype=F32,
             name="proj_small_dx")
    dh = _mm(d_proj, w_main, S, Dm, N_MAIN, mode="nt", out_dtype=F32, add=dh, name="proj_main_dx")
    dx, dsh1, dsc1, dw_pre1 = _pre_bwd(dh, x2, norm_mix_pre, sc1, dx1, "pre_mix_bwd")
    r_mu, r_md = _exchange_wait(mlp_x, ["a2a"] * 2, dx, "grads_mlp_wait")
    r_su, r_gu, r_o = _exchange_wait(mix_x, ["a2a"] * 3, dx, "grads_mix_wait")

    dconv_w = jnp.concatenate([p[0] for p in conv_pieces], axis=1)
    dconv_b = jnp.concatenate([p[1] for p in conv_pieces[:3]], axis=1)
    dmod = jnp.concatenate([dsh1, dsc1, dg1, dsh2, dsc2, dg2], axis=1)
    small_vec = jnp.concatenate(
        [dmod, dw_pre1, dw_post1, dconv_b, dpv_ssm[:, 0].reshape(1, 32), dpv_ssm[:, 1].reshape(1, 32),
         dpv_ssm[:, 2].reshape(1, 32), dnw_ssm.reshape(1, 2048), dpv_gdn[:, 0].reshape(1, 16),
         dpv_gdn[:, 1].reshape(1, 16), jnp.sum(dnw_gdn, axis=0), dw_pre2, dw_post2, dconv_w.reshape(1, -1)], axis=1)
    n_vec = small_vec.shape[1]
    small_vec = jnp.pad(small_vec, ((0, 0), (0, (-n_vec) % 1024))).reshape(-1, 1024)
    (small_all,) = _exchange([small_vec], ["gather"], "gather_small_grads")
    small_all = small_all.reshape(N_DEV, -1)
    dmod_cols = lax.dynamic_slice(small_all, (0, me * n_ada), (N_DEV, n_ada))
    gw_ada = _ada_bwd(c_all.T, dmod_cols, "ada_bwd")
    conv_all = small_all[:, N_REPLICATED:n_vec].reshape(N_DEV, CONV_K, 2 * N_DEV * 512)
    conv_contrib = jnp.concatenate(
        [lax.dynamic_slice(conv_all, (0, 0, me * 512), (N_DEV, CONV_K, 512)),
         lax.dynamic_slice(conv_all, (0, 0, N_DEV * 512 + me * 512), (N_DEV, CONV_K, 512))], axis=1)
    rep_contrib = small_all[:, :N_REPLICATED].reshape(N_DEV, N_REPLICATED // 128, 128)

    results = {}

    def adam_big(nm, contrib):
        w3 = given[nm]
        res = _adam(contrib, w3[0][0], w3[1][0], w3[2][0], "adam_" + nm)
        results[nm] = tuple(r.reshape(w3[0].shape) for r in res)

    adam_big("w_ada", gw_ada[None])
    adam_big("w_ssm_up", r_su)
    adam_big("w_gdn_up", r_gu)
    adam_big("w_out", r_o)
    adam_big("w_mlp_up", r_mu)
    adam_big("w_mlp_down", r_md)
    (r_in,) = _exchange_wait(in_x, ["a2a"], results["w_mlp_down"][0], "grads_w_in_wait", chips=True)
    adam_big("w_in", r_in)
    packed = [jnp.concatenate([given[nm][i] for nm in REPLICATED], axis=1).reshape(N_REPLICATED // 128, 128)
              for i in range(3)]
    rep_res = _adam(rep_contrib, packed[0], packed[1], packed[2], "adam_replicated")
    pos = 0
    for nm in REPLICATED:
        size = given[nm][0].shape[1]
        results[nm] = tuple(r.reshape(1, N_REPLICATED)[:, pos:pos + size] for r in rep_res)
        pos += size
    conv_wmv = [jnp.concatenate([given["ssm_conv_w"][i][0], given["gdn_conv_w"][i][0]], axis=0) for i in range(3)]
    conv_res = _adam(conv_contrib, conv_wmv[0], conv_wmv[1], conv_wmv[2], "adam_conv_w")
    results["ssm_conv_w"] = tuple(r[None, :CONV_K] for r in conv_res)
    results["gdn_conv_w"] = tuple(r[None, CONV_K:] for r in conv_res)

    loss = lax.psum(loss_loc[0, 0], ("x", "y", "c"))
    return (loss, dx[None]) + tuple(results[nm][i] for i in range(4) for nm in WEIGHTS)
```

```python
import jax
import jax.numpy as jnp
from jax import lax
from jax.experimental import pallas as pl
from jax.experimental.pallas import tpu as pltpu

F32 = jnp.float32
BF16 = jnp.bfloat16
N_DEV = 8
D_MODEL = 1024
EPS = 1e-6
CONV_K = 4
SSM_CHUNK = 128
SSM_HEAD_DIM = 64
SSM_D_STATE = 128
SSM_GROUPS = 8
SSM_HEADS_PER_GROUP = 4
SSM_GROUP_WIDTH = SSM_HEADS_PER_GROUP * SSM_HEAD_DIM
SSM_GROUPS_PER_STEP = 2
GDN_CHUNK = 64
GDN_HEAD = 128
GDN_QK_HEADS = 8
GDN_V_PER_QK = 2
GDN_QK_PER_STEP = 4
GDN_INV_BLOCK = 16
C_ZS, C_XBC, C_QKV, C_ZG, C_GS, C_GG, N_MAIN = 0, 2048, 6144, 10240, 12288, 13312, 14336
N_SMALL = 128
ADAM_LR, ADAM_B1, ADAM_B2, ADAM_EPS, ADAM_WD, ADAM_STEP = 0.001, 0.9, 0.999, 1e-08, 0.01, 10
VMEM_LIMIT = 56 * 1024 * 1024
MM_WHOLE_K = 4096
MM_SPLIT_K = 2048
NEG_INF = float("-inf")

_NT = (((1,), (1,)), ((), ()))
_NN = (((1,), (0,)), ((), ()))
_TN = (((0,), (0,)), ((), ()))


def _params(*sem):
    return pltpu.CompilerParams(dimension_semantics=sem, vmem_limit_bytes=VMEM_LIMIT)


def _dot(a, b, dims=_NN):
    return lax.dot_general(a.astype(BF16), b.astype(BF16), dims, preferred_element_type=F32)


def _split(a):
    hi = a.astype(BF16)
    return hi, (a - hi.astype(F32)).astype(BF16)


def _dot3(a, b, dims=_NN):
    ah, al = _split(a)
    bh, bl = _split(b)
    d = lambda u, v: lax.dot_general(u, v, dims, preferred_element_type=F32)
    return d(ah, bh) + (d(ah, bl) + d(al, bh))


def _dot2(a, b, dims=_NN):
    ah, al = _split(a)
    bb = b.astype(BF16)
    d = lambda u: lax.dot_general(u, bb, dims, preferred_element_type=F32)
    return d(ah) + d(al)


def _sigmoid(x):
    return 0.5 * jnp.tanh(0.5 * x) + 0.5


def _silu(x):
    return x * _sigmoid(x)


def _dsilu(x):
    s = _sigmoid(x)
    return s * (1.0 + x * (1.0 - s))


def _softplus(x):
    return jnp.maximum(x, 0.0) + jnp.log1p(jnp.exp(-jnp.abs(x)))


def _iota(n, m, d):
    return lax.broadcasted_iota(jnp.int32, (n, m), d)


def _rowsum(x):
    return jnp.sum(x, axis=1, keepdims=True)


def _colsum(x):
    return jnp.sum(x, axis=0, keepdims=True)


def _total(x):
    return _rowsum(_colsum(x))


MXU_LANES = 128


def _parts(x, n):
    out = []
    for _ in range(n):
        p = x.astype(BF16)
        out.append(p)
        x = x - p.astype(F32)
    return out


def _sum_by(m01, x, dims=_NN, n=3):
    return sum(lax.dot_general(m01, p, dims, preferred_element_type=F32) for p in _parts(x, n))


def _row_col_sums(q):
    ones = jnp.ones((q.shape[0], MXU_LANES), BF16)
    acc = 0.0
    for p in _parts(q, 2):
        acc = acc + (lax.dot_general(p, ones, _NN, preferred_element_type=F32)
                     - lax.dot_general(p, ones, _TN, preferred_element_type=F32))
    return acc[:, 0:1]


def _cumsum_forms(col, ii, jj):
    lower = jnp.where(ii >= jj, 1.0, 0.0).astype(BF16)
    cum_col = _sum_by(lower, jnp.broadcast_to(col, (col.shape[0], MXU_LANES)))[:, 0:1]
    cum_row = _colsum(jnp.where(ii <= jj, col, 0.0))
    return cum_col, cum_row


def _rev_cumsum_col(col, ii, jj):
    upper = jnp.where(ii <= jj, 1.0, 0.0).astype(BF16)
    return _sum_by(upper, jnp.broadcast_to(col, (col.shape[0], MXU_LANES)))[:, 0:1]


def _blk(dim, pref):
    return pref if dim % pref == 0 else dim


def _lockstep(gens):
    gens = list(gens)
    while gens:
        alive = []
        for g in gens:
            try:
                next(g)
                alive.append(g)
            except StopIteration:
                pass
        gens = alive


def _mm(a, b, M, N, K, *, mode, out_dtype, name, a_off=(0, 0), b_off=(0, 0), add=None, epi=None, extra=None,
        tm=1024, tn=1024):
    tm, tn = _blk(M, tm), _blk(N, tn)
    tk = K if K <= MM_WHOLE_K else _blk(K, MM_SPLIT_K)
    nk = K // tk
    if mode == "tn":
        a_spec = pl.BlockSpec((tk, tm), lambda i, j, k: (k + a_off[0] // tk, i + a_off[1] // tm))
        assert a_off[0] % tk == 0 and a_off[1] % tm == 0
    else:
        a_spec = pl.BlockSpec((tm, tk), lambda i, j, k: (i + a_off[0] // tm, k + a_off[1] // tk))
        assert a_off[0] % tm == 0 and a_off[1] % tk == 0
    if mode == "nt":
        b_spec = pl.BlockSpec((tn, tk), lambda i, j, k: (j + b_off[0] // tn, k + b_off[1] // tk))
        assert b_off[0] % tn == 0 and b_off[1] % tk == 0
    else:
        b_spec = pl.BlockSpec((tk, tn), lambda i, j, k: (k + b_off[0] // tk, j + b_off[1] // tn))
        assert b_off[0] % tk == 0 and b_off[1] % tn == 0
    dims = {"nn": _NN, "nt": _NT, "tn": _TN}[mode]
    o_spec = pl.BlockSpec((tm, tn), lambda i, j, k: (i, j))
    ins, in_specs = [a, b], [a_spec, b_spec]
    if add is not None:
        ins.append(add)
        in_specs.append(o_spec)
    if extra is not None:
        ins.append(extra)
        in_specs.append(o_spec)
    n_in = len(ins)
    if epi == "relu2":
        out_shape = (jax.ShapeDtypeStruct((M, N), BF16), jax.ShapeDtypeStruct((M, N), BF16))
        out_specs = (o_spec, o_spec)
    else:
        out_shape = jax.ShapeDtypeStruct((M, N), out_dtype)
        out_specs = o_spec

    def body(*refs):
        a_ref, b_ref = refs[0], refs[1]
        outs = refs[n_in:] if nk == 1 else refs[n_in:-1]

        def finish(r):
            pos = 2
            if add is not None:
                r = r + refs[pos][...]
                pos += 1
            if epi == "relu2":
                p = jnp.maximum(r, 0.0)
                outs[0][...] = p.astype(BF16)
                outs[1][...] = (p * p).astype(BF16)
            elif epi == "drelu2":
                outs[0][...] = (r * (2.0 * refs[pos][...].astype(F32))).astype(out_dtype)
            else:
                outs[0][...] = r.astype(out_dtype)

        if nk == 1:
            finish(_dot(a_ref[...], b_ref[...], dims))
            return
        acc = refs[-1]
        k = pl.program_id(2)

        @pl.when(k == 0)
        def _():
            acc[...] = jnp.zeros_like(acc)

        acc[...] += _dot(a_ref[...], b_ref[...], dims)

        @pl.when(k == nk - 1)
        def _():
            finish(acc[...])

    return pl.pallas_call(
        body, name=name, grid=(M // tm, N // tn, nk), in_specs=in_specs, out_specs=out_specs, out_shape=out_shape,
        scratch_shapes=[] if nk == 1 else [pltpu.VMEM((tm, tn), F32)],
        compiler_params=_params("parallel", "parallel", "arbitrary"))(*ins)


def _mm_rows(a, b, M, N, K, *, mode, name, extras, out_shapes, out_specs, epilogue, aliases=None, tm=512):
    tm = _blk(M, tm)
    a_spec = pl.BlockSpec((tm, K), lambda i: (i, 0))
    b_spec = pl.BlockSpec((K, N) if mode == "nn" else (N, K), lambda i: (0, 0))
    dims = _NN if mode == "nn" else _NT
    n_ex = len(extras)

    def body(a_ref, b_ref, *refs):
        epilogue(_dot(a_ref[...], b_ref[...], dims), refs[:n_ex], refs[n_ex:])

    return pl.pallas_call(
        body, name=name, grid=(M // tm,), in_specs=[a_spec, b_spec] + [sp for _, sp in extras],
        out_specs=tuple(out_specs), out_shape=tuple(out_shapes), input_output_aliases=aliases or {},
        compiler_params=_params("parallel"))(a, b, *[x for x, _ in extras])


def _row_spec(tb, d):
    return pl.BlockSpec((tb, d), lambda i: (i, 0))


def _vec_spec(d):
    return pl.BlockSpec((1, d), lambda i: (0, 0))


def _pre_fwd(x, w, sc, sh, name):
    S, Dm = x.shape
    tb = _blk(S, 512)

    def body(x_ref, w_ref, sc_ref, sh_ref, h_ref):
        xv = x_ref[...]
        r = lax.rsqrt(jnp.mean(xv * xv, axis=-1, keepdims=True) + EPS)
        h_ref[...] = ((xv * r * w_ref[...]) * (1.0 + sc_ref[...]) + sh_ref[...]).astype(BF16)

    return pl.pallas_call(
        body, name=name, grid=(S // tb,), in_specs=[_row_spec(tb, Dm)] + [_vec_spec(Dm)] * 3,
        out_specs=_row_spec(tb, Dm), out_shape=jax.ShapeDtypeStruct((S, Dm), BF16),
        compiler_params=_params("parallel"))(x, w, sc, sh)


def _final_fwd_bwd(x, y, w, g, target, name):
    S, Dm = x.shape
    tb = _blk(S, 512)
    nb = S // tb

    def body(x_ref, y_ref, w_ref, g_ref, t_ref, dx_ref, loss_ref, dy_ref, dg_ref, dw_ref, acc):
        i = pl.program_id(0)

        @pl.when(i == 0)
        def _():
            acc[...] = jnp.zeros_like(acc)
            dg_ref[...] = jnp.zeros_like(dg_ref)
            dw_ref[...] = jnp.zeros_like(dw_ref)

        yv = y_ref[...]
        r = lax.rsqrt(jnp.mean(yv * yv, axis=-1, keepdims=True) + EPS)
        yh = yv * r
        n = yh * w_ref[...]
        e = (x_ref[...] + g_ref[...] * n) - t_ref[...]
        dv = e * (1.0 / Dm)
        dx_ref[...] = dv
        acc[...] += _colsum(e * e)
        dg_ref[...] += _colsum(dv * n)
        dn = dv * g_ref[...]
        dw_ref[...] += _colsum(dn * yh)
        dyh = dn * w_ref[...]
        dy_ref[...] = (r * (dyh - yh * jnp.mean(dyh * yh, axis=-1, keepdims=True))).astype(BF16)

        @pl.when(i == nb - 1)
        def _():
            loss_ref[...] = (0.5 / Dm) * _rowsum(acc[...])

    row, vec = _row_spec(tb, Dm), _vec_spec(Dm)
    vec_shape = jax.ShapeDtypeStruct((1, Dm), F32)
    return pl.pallas_call(
        body, name=name, grid=(nb,), in_specs=[row, row, vec, vec, row],
        out_specs=(row, pl.BlockSpec((1, 1), lambda i: (0, 0)), row, vec, vec),
        out_shape=(jax.ShapeDtypeStruct((S, Dm), F32), jax.ShapeDtypeStruct((1, 1), F32),
                   jax.ShapeDtypeStruct((S, Dm), BF16), vec_shape, vec_shape),
        scratch_shapes=[pltpu.VMEM((1, Dm), F32)], compiler_params=_params("arbitrary"))(x, y, w, g, target)


def _post_bwd(dxo, y, w, g, name):
    S, Dm = y.shape
    tb = _blk(S, 512)

    def body(d_ref, y_ref, w_ref, g_ref, dy_ref, dg_ref, dw_ref):
        i = pl.program_id(0)

        @pl.when(i == 0)
        def _():
            dg_ref[...] = jnp.zeros_like(dg_ref)
            dw_ref[...] = jnp.zeros_like(dw_ref)

        yv, dv = y_ref[...], d_ref[...]
        r = lax.rsqrt(jnp.mean(yv * yv, axis=-1, keepdims=True) + EPS)
        yh = yv * r
        dg_ref[...] += _colsum(dv * (yh * w_ref[...]))
        dn = dv * g_ref[...]
        dw_ref[...] += _colsum(dn * yh)
        dyh = dn * w_ref[...]
        dy_ref[...] = (r * (dyh - yh * jnp.mean(dyh * yh, axis=-1, keepdims=True))).astype(BF16)

    return pl.pallas_call(
        body, name=name, grid=(S // tb,), in_specs=[_row_spec(tb, Dm)] * 2 + [_vec_spec(Dm)] * 2,
        out_specs=(_row_spec(tb, Dm), _vec_spec(Dm), _vec_spec(Dm)),
        out_shape=(jax.ShapeDtypeStruct((S, Dm), BF16), jax.ShapeDtypeStruct((1, Dm), F32),
                   jax.ShapeDtypeStruct((1, Dm), F32)),
        compiler_params=_params("arbitrary"))(dxo, y, w, g)


def _pre_bwd(dh, x, w, sc, dres, name):
    S, Dm = x.shape
    tb = _blk(S, 512)

    def body(dh_ref, x_ref, w_ref, sc_ref, dr_ref, dx_ref, dsh_ref, dsc_ref, dw_ref):
        i = pl.program_id(0)

        @pl.when(i == 0)
        def _():
            dsh_ref[...] = jnp.zeros_like(dsh_ref)
            dsc_ref[...] = jnp.zeros_like(dsc_ref)
            dw_ref[...] = jnp.zeros_like(dw_ref)

        xv, dv = x_ref[...], dh_ref[...]
        r = lax.rsqrt(jnp.mean(xv * xv, axis=-1, keepdims=True) + EPS)
        xh = xv * r
        one_sc = 1.0 + sc_ref[...]
        dsh_ref[...] += _colsum(dv)
        dsc_ref[...] += _colsum(dv * (xh * w_ref[...]))
        dw_ref[...] += _colsum(dv * one_sc * xh)
        dxh = dv * one_sc * w_ref[...]
        dx_ref[...] = dr_ref[...] + r * (dxh - xh * jnp.mean(dxh * xh, axis=-1, keepdims=True))

    vec = jax.ShapeDtypeStruct((1, Dm), F32)
    return pl.pallas_call(
        body, name=name, grid=(S // tb,),
        in_specs=[_row_spec(tb, Dm)] * 2 + [_vec_spec(Dm)] * 2 + [_row_spec(tb, Dm)],
        out_specs=(_row_spec(tb, Dm), _vec_spec(Dm), _vec_spec(Dm), _vec_spec(Dm)),
        out_shape=(jax.ShapeDtypeStruct((S, Dm), F32), vec, vec, vec),
        compiler_params=_params("arbitrary"))(dh, x, w, sc, dres)


D_PROJ_ANY = pl.BlockSpec(memory_space=pl.ANY)


def _gate_specs(tm, Dm):
    return (pl.BlockSpec((tm, Dm), lambda i: (i, C_GS // Dm)), pl.BlockSpec((tm, Dm), lambda i: (i, C_GG // Dm)))


def _gdn_up_merge(ogn, w_gu, ys, proj, name):
    S, K = ogn.shape
    Dm = ys.shape[1]
    tm = _blk(S, 512)
    row = _row_spec(tm, Dm)

    def epilogue(r, ex, out):
        ys_ref, gs_ref, gg_ref = ex
        out[0][...] = r
        out[1][...] = (_sigmoid(gs_ref[...].astype(F32)) * ys_ref[...]
                       + _sigmoid(gg_ref[...].astype(F32)) * r).astype(BF16)

    gs_spec, gg_spec = _gate_specs(tm, Dm)
    return _mm_rows(ogn, w_gu, S, Dm, K, mode="nn", name=name, tm=tm,
                    extras=[(ys, row), (proj, gs_spec), (proj, gg_spec)],
                    out_shapes=[jax.ShapeDtypeStruct((S, Dm), F32), jax.ShapeDtypeStruct((S, Dm), BF16)],
                    out_specs=[row, row], epilogue=epilogue)


def _mix_out_post_pre(merged, w_o, x, w_post, g, w_pre, sc, sh, name):
    S, Dm = x.shape
    tm = _blk(S, 512)
    row, vec = _row_spec(tm, Dm), _vec_spec(Dm)

    def epilogue(r, ex, out):
        x_ref, wpost_ref, g_ref, wpre_ref, sc_ref, sh_ref = ex
        out[0][...] = r
        rr = lax.rsqrt(jnp.mean(r * r, axis=-1, keepdims=True) + EPS)
        x1 = x_ref[...] + g_ref[...] * (r * rr * wpost_ref[...])
        out[1][...] = x1
        r1 = lax.rsqrt(jnp.mean(x1 * x1, axis=-1, keepdims=True) + EPS)
        out[2][...] = ((x1 * r1 * wpre_ref[...]) * (1.0 + sc_ref[...]) + sh_ref[...]).astype(BF16)

    return _mm_rows(merged, w_o, S, Dm, Dm, mode="nn", name=name, tm=tm,
                    extras=[(x, row), (w_post, vec), (g, vec), (w_pre, vec), (sc, vec), (sh, vec)],
                    out_shapes=[jax.ShapeDtypeStruct((S, Dm), F32), jax.ShapeDtypeStruct((S, Dm), F32),
                                jax.ShapeDtypeStruct((S, Dm), BF16)], out_specs=[row, row, row], epilogue=epilogue)


def _mix_out_dx_merge_bwd(dmo, w_o, ys, yg, proj, d_proj, name):
    S, Dm = ys.shape
    tm = _blk(S, 512)
    row = _row_spec(tm, Dm)

    def epilogue(d, ex, out):
        ys_ref, yg_ref, gs_ref, gg_ref, _ = ex
        ss, sg = _sigmoid(gs_ref[...].astype(F32)), _sigmoid(gg_ref[...].astype(F32))
        out[0][...] = (d * ss).astype(BF16)
        out[1][...] = (d * sg).astype(BF16)
        out[2][:, :Dm] = (d * ys_ref[...] * ss * (1.0 - ss)).astype(BF16)
        out[2][:, Dm:] = (d * yg_ref[...] * sg * (1.0 - sg)).astype(BF16)

    gs_spec, gg_spec = _gate_specs(tm, Dm)
    return _mm_rows(dmo, w_o, S, Dm, Dm, mode="nt", name=name, tm=tm,
                    extras=[(ys, row), (yg, row), (proj, gs_spec), (proj, gg_spec), (d_proj, D_PROJ_ANY)],
                    out_shapes=[jax.ShapeDtypeStruct((S, Dm), BF16), jax.ShapeDtypeStruct((S, Dm), BF16),
                                jax.ShapeDtypeStruct(d_proj.shape, BF16)],
                    out_specs=[row, row, pl.BlockSpec((tm, 2 * Dm), lambda i: (i, C_GS // (2 * Dm)))],
                    epilogue=epilogue, aliases={6: 2})


CONV_COLS = 128
CONV_BWD_ROWS = 128


def _taps_down(x):
    rows = _iota(x.shape[0], x.shape[1], 0)
    return [x] + [jnp.where(rows >= k, pltpu.roll(x, k, 0), 0.0) for k in range(1, CONV_K)]


def _conv_pre(taps, w_ref, b_ref):
    pre = taps[0] * w_ref[CONV_K - 1:CONV_K, :] + b_ref[...]
    for k in range(1, CONV_K):
        pre = pre + taps[k] * w_ref[CONV_K - 1 - k:CONV_K - k, :]
    return pre


def _conv_dx(dpre, w_ref):
    n = dpre.shape[0]
    rows = _iota(n, dpre.shape[1], 0)
    dx = dpre * w_ref[CONV_K - 1:CONV_K, :]
    for k in range(1, CONV_K):
        dx = dx + jnp.where(rows < n - k, pltpu.roll(dpre, n - k, 0), 0.0) * w_ref[CONV_K - 1 - k:CONV_K - k, :]
    return dx


def _conv_fwd(proj, w, b, name):
    S = proj.shape[0]
    n = w.shape[1]
    cb = CONV_COLS

    def body(x_ref, w_ref, b_ref, o_ref):
        o_ref[...] = _silu(_conv_pre(_taps_down(x_ref[...].astype(F32)), w_ref, b_ref)).astype(BF16)

    return pl.pallas_call(
        body, name=name, grid=(n // cb,),
        in_specs=[pl.BlockSpec((S, cb), lambda j: (0, j + C_XBC // cb)), pl.BlockSpec((CONV_K, cb), lambda j: (0, j)),
                  pl.BlockSpec((1, cb), lambda j: (0, j))],
        out_specs=pl.BlockSpec((S, cb), lambda j: (0, j)), out_shape=jax.ShapeDtypeStruct((S, n), BF16),
        compiler_params=_params("parallel"))(proj, w, b)


def _conv_bwd(dact, proj, w, b, col0, d_proj, name):
    S, n = dact.shape
    cb = CONV_COLS
    o = col0 // cb

    R, HALO = _blk(S, CONV_BWD_ROWS), 16
    n_chunks = S // R

    def body(d_ref, x_ref, w_ref, b_ref, _, dx_ref, dw_ref, db_ref):
        def chunk(r0, first, last, sums):
            lo, hi = (0 if first else HALO), (0 if last else HALO)
            start = r0 - lo if isinstance(r0, int) else pl.multiple_of(r0 - lo, HALO)
            xe = x_ref[pl.ds(start, lo + R + hi), :].astype(F32)
            rows = _iota(lo + R + hi, cb, 0)
            taps = [xe[lo:, :]]
            for k in range(1, CONV_K):
                t = pltpu.roll(xe, k, 0)
                taps.append((jnp.where(rows >= k, t, 0.0) if first else t)[lo:, :])
            dpre_e = d_ref[pl.ds(r0, R + hi), :].astype(F32) * _dsilu(_conv_pre(taps, w_ref, b_ref))
            dpre = dpre_e[0:R, :]
            db, dw = sums
            db = db + _colsum(dpre)
            dw = [dw[k] + _colsum(dpre * taps[k][0:R, :]) for k in range(CONV_K)]
            rows_e = _iota(R + hi, cb, 0)
            dx = dpre * w_ref[CONV_K - 1:CONV_K, :]
            for k in range(1, CONV_K):
                t = pltpu.roll(dpre_e, R + hi - k, 0)
                t = jnp.where(rows_e < R - k, t, 0.0) if last else t
                dx = dx + t[0:R, :] * w_ref[CONV_K - 1 - k:CONV_K - k, :]
            dx_ref[pl.ds(r0, R), :] = dx.astype(BF16)
            return db, dw

        zero = jnp.zeros((1, cb), F32)
        sums = chunk(0, True, n_chunks == 1, (zero, [zero] * CONV_K))
        if n_chunks > 2:
            def step(i, carry):
                db, dw = chunk(pl.multiple_of(i * R, R), False, False, (carry[0], list(carry[1:])))
                return (db,) + tuple(dw)
            carry = lax.fori_loop(1, n_chunks - 1, step, (sums[0],) + tuple(sums[1]))
            sums = (carry[0], list(carry[1:]))
        if n_chunks > 1:
            sums = chunk((n_chunks - 1) * R, False, True, sums)
        db_ref[...] = sums[0]
        for k in range(CONV_K):
            dw_ref[CONV_K - 1 - k:CONV_K - k, :] = sums[1][k]

    return pl.pallas_call(
        body, name=name, grid=(n // cb,),
        in_specs=[pl.BlockSpec((S, cb), lambda j: (0, j)), pl.BlockSpec((S, cb), lambda j: (0, j + o + C_XBC // cb)),
                  pl.BlockSpec((CONV_K, cb), lambda j: (0, j + o)), pl.BlockSpec((1, cb), lambda j: (0, j + o)),
                  D_PROJ_ANY],
        out_specs=(pl.BlockSpec((S, cb), lambda j: (0, j + o + C_XBC // cb)),
                   pl.BlockSpec((CONV_K, cb), lambda j: (0, j)), pl.BlockSpec((1, cb), lambda j: (0, j))),
        out_shape=(jax.ShapeDtypeStruct(d_proj.shape, BF16), jax.ShapeDtypeStruct((CONV_K, n), F32),
                   jax.ShapeDtypeStruct((1, n), F32)),
        input_output_aliases={4: 0}, compiler_params=_params("parallel"))(dact, proj, w, b, d_proj)


def _ssd_specs(L, order):
    G = SSM_GROUPS_PER_STEP
    W, N = G * SSM_GROUP_WIDTH, G * SSM_D_STATE
    x_spec = pl.BlockSpec((L, W), lambda g, c: (order(c), g))
    b_spec = pl.BlockSpec((L, N), lambda g, c: (order(c), 2048 // N + g))
    c_spec = pl.BlockSpec((L, N), lambda g, c: (order(c), 3072 // N + g))
    z_spec = pl.BlockSpec((L, W), lambda g, c: (order(c), C_ZS // W + g))
    dt_spec = pl.BlockSpec((G, L, SSM_HEADS_PER_GROUP), lambda g, c: (g, order(c), 0))
    p_spec = pl.BlockSpec((G, 3, SSM_HEADS_PER_GROUP), lambda g, c: (g, 0, 0))
    nw_spec = pl.BlockSpec((G, 1, SSM_GROUP_WIDTH), lambda g, c: (g, 0, 0))
    s_spec = pl.BlockSpec((G, 1, SSM_GROUP_WIDTH, SSM_D_STATE), lambda g, c: (g, order(c), 0, 0))
    return x_spec, b_spec, c_spec, z_spec, dt_spec, p_spec, nw_spec, s_spec


class _SsdGroup:
    def __init__(self, L):
        P, H, W = SSM_HEAD_DIM, SSM_HEADS_PER_GROUP, SSM_GROUP_WIDTH
        self.L = L
        self.ii, self.jj = _iota(L, L, 0), _iota(L, L, 1)
        self.lower = jnp.where(self.ii >= self.jj, 1.0, 0.0).astype(BF16)
        self.upper = jnp.where(self.ii <= self.jj, 1.0, 0.0).astype(BF16)
        self.lo = _iota(L, 2 * P, 1) < P
        self.lo_row = _iota(1, 2 * P, 1) < P
        bi, bj = _iota(W, W, 0), _iota(W, W, 1)
        self.block = jnp.where(bi // P == bj // P, 1.0, 0.0).astype(BF16)
        si, sj = _iota(2 * P, W, 0), _iota(2 * P, W, 1)
        self.pick = jnp.where(sj == si * P, 1.0, 0.0).astype(BF16)
        self.ones = jnp.ones((L, 2 * P), BF16)

    def spread(self, v4):
        R = v4.shape[0]
        lo = self.lo if R == self.L else self.lo_row
        b = lambda h: jnp.broadcast_to(v4[:, h:h + 1], (R, 2 * SSM_HEAD_DIM))
        return jnp.concatenate([jnp.where(lo, b(0), b(1)), jnp.where(lo, b(2), b(3))], axis=1)

    def gather4(self, v):
        return jnp.concatenate([v[:, h * SSM_HEAD_DIM:h * SSM_HEAD_DIM + 1] for h in range(SSM_HEADS_PER_GROUP)],
                               axis=1)

    def head_sums(self, z):
        return sum(lax.dot_general(p, self.block, _NN, preferred_element_type=F32) for p in _parts(z, 2))

    def pair_cols(self, full, pair):
        ps = full[:, pair * 128:(pair + 1) * 128]
        sw = pltpu.roll(ps, SSM_HEAD_DIM, 1)
        return jnp.where(self.lo, ps, sw), jnp.where(self.lo, sw, ps)

    def gates(self, dt4_raw, p):
        L = self.L
        dtr = self.spread(dt4_raw + p[0:1, :])
        dt = _softplus(dtr)
        A = self.spread(-jnp.exp(p[1:2, :]))
        acum = _sum_by(self.lower, dt * A)
        yield
        rows = _sum_by(self.pick, acum, _NT)
        yield
        a_last = acum[L - 1:L, :]
        cols = self.pair_cols(acum, 0) + self.pair_cols(acum, 1)
        decay, decay_t = [], []
        for h in range(SSM_HEADS_PER_GROUP):
            seg = cols[h] - rows[h:h + 1, :]
            decay.append(jnp.exp(jnp.where(self.ii >= self.jj, seg, NEG_INF)))
            decay_t.append(jnp.exp(jnp.where(self.jj >= self.ii, -seg, NEG_INF)))
        return dict(dtr=dtr, dt=dt, A=A, D=self.spread(p[2:3, :]), acum=acum, eac=jnp.exp(acum), a_last=a_last,
                    wdec=jnp.exp(a_last - acum), decay=decay, decay_t=decay_t,
                    ea_last=[jnp.exp(rows[h:h + 1, L - 1:L]) for h in range(SSM_HEADS_PER_GROUP)])


def _ssd_fwd(conv, proj, dt_raw, pvec, nw, name):
    S = conv.shape[0]
    L, P, N, H, W, G = SSM_CHUNK, SSM_HEAD_DIM, SSM_D_STATE, SSM_HEADS_PER_GROUP, SSM_GROUP_WIDTH, SSM_GROUPS_PER_STEP
    nc = S // L

    def body(x_ref, b_ref, c_ref, z_ref, dt_ref, p_ref, nw_ref, y_ref, yn_ref, s0_ref, state):
        c = pl.program_id(1)

        @pl.when(c == 0)
        def _():
            state[...] = jnp.zeros_like(state)

        k = _SsdGroup(L)

        def group(gi):
            gsl = slice(gi * W, (gi + 1) * W)
            Bm, Cm = b_ref[:, gi * N:(gi + 1) * N], c_ref[:, gi * N:(gi + 1) * N]
            x = x_ref[:, gsl].astype(F32)
            S0 = state[gsl, :]
            s0_ref[gi, 0] = S0
            CB = _dot(Cm, Bm, _NT)
            y_off = _dot(Cm, S0, _NT)
            t = yield from k.gates(dt_ref[gi], p_ref[gi])
            xdt = x * t["dt"]
            s_new = _dot(xdt * t["wdec"], Bm, _TN)
            y_diag = []
            for pair in range(H // 2):
                xp = xdt[:, pair * 128:(pair + 1) * 128]
                y_diag.append(jnp.where(k.lo, _dot(CB * t["decay"][2 * pair], xp),
                                        _dot(CB * t["decay"][2 * pair + 1], xp)))
            yield
            y = jnp.concatenate(y_diag, axis=1) + y_off * t["eac"]
            for h in range(H):
                hsl = slice(gi * W + h * P, gi * W + (h + 1) * P)
                state[hsl, :] = S0[h * P:(h + 1) * P, :] * t["ea_last"][h] + s_new[h * P:(h + 1) * P, :]
            y_ref[:, gsl] = y
            y2 = (y + t["D"] * x) * _silu(z_ref[:, gsl].astype(F32))
            r = lax.rsqrt(jnp.mean(y2 * y2, axis=-1, keepdims=True) + EPS)
            yn_ref[:, gsl] = (y2 * r * nw_ref[gi]).astype(BF16)

        _lockstep(group(gi) for gi in range(G))

    x_spec, b_spec, c_spec, z_spec, dt_spec, p_spec, nw_spec, s_spec = _ssd_specs(L, lambda c: c)
    y_spec = pl.BlockSpec((L, G * W), lambda g, c: (c, g))
    return pl.pallas_call(
        body, name=name, grid=(SSM_GROUPS // G, nc),
        in_specs=[x_spec, b_spec, c_spec, z_spec, dt_spec, p_spec, nw_spec],
        out_specs=(y_spec, y_spec, s_spec),
        out_shape=(jax.ShapeDtypeStruct((S, SSM_GROUPS * W), F32), jax.ShapeDtypeStruct((S, SSM_GROUPS * W), BF16),
                   jax.ShapeDtypeStruct((SSM_GROUPS, nc, W, N), F32)),
        scratch_shapes=[pltpu.VMEM((G * W, N), F32)],
        compiler_params=_params("parallel", "arbitrary"))(conv, conv, conv, proj, dt_raw, pvec, nw)


def _ssd_bwd(dyn, conv, proj, dt_raw, pvec, nw, y_ssd, states, d_proj, name):
    S = conv.shape[0]
    L, P, N, H, W, G = SSM_CHUNK, SSM_HEAD_DIM, SSM_D_STATE, SSM_HEADS_PER_GROUP, SSM_GROUP_WIDTH, SSM_GROUPS_PER_STEP
    nc = S // L

    def body(dyn_ref, x_ref, b_ref, c_ref, z_ref, dt_ref, p_ref, nw_ref, y_ref, s0_ref, _,
             dx_ref, db_ref, dc_ref, dz_ref, ddt_ref, dp_ref, dnw_ref, dstate):
        c = pl.program_id(1)

        @pl.when(c == 0)
        def _():
            dstate[...] = jnp.zeros_like(dstate)
            dp_ref[...] = jnp.zeros_like(dp_ref)
            dnw_ref[...] = jnp.zeros_like(dnw_ref)

        k = _SsdGroup(L)
        last = (_iota(L, 1, 0) == L - 1)

        def group(gi):
            gsl = slice(gi * W, (gi + 1) * W)
            Bm, Cm = b_ref[:, gi * N:(gi + 1) * N], c_ref[:, gi * N:(gi + 1) * N]
            x, z = x_ref[:, gsl].astype(F32), z_ref[:, gsl].astype(F32)
            S0, dS1 = s0_ref[gi, 0], dstate[gsl, :]
            CB = _dot(Cm, Bm, _NT)
            CBt = _dot(Bm, Cm, _NT)
            y_off_raw = _dot(Cm, S0, _NT)
            dXs_raw = _dot(Bm, dS1, _NT)
            t = yield from k.gates(dt_ref[gi], p_ref[gi])
            y1 = y_ref[:, gsl] + t["D"] * x
            sz = _silu(z)
            y2 = y1 * sz
            r = lax.rsqrt(jnp.mean(y2 * y2, axis=-1, keepdims=True) + EPS)
            y2h = y2 * r
            dyn_v = dyn_ref[:, gsl]
            dnw_ref[gi] += _colsum(dyn_v * y2h)
            dy2h = dyn_v * nw_ref[gi]
            dy2 = r * (dy2h - y2h * jnp.mean(dy2h * y2h, axis=-1, keepdims=True))
            dz_ref[:, gsl] = (dy2 * y1 * _dsilu(z)).astype(BF16)
            dY = dy2 * sz
            X = x * t["dt"]
            dYe = dY * t["eac"]
            dC_s = _dot(dYe, S0)
            dB_s = _dot(X * t["wdec"], dS1)
            dS_c = _dot(dYe, Cm, _TN)
            dXm, Gs, Gts = [], [], []
            for pair in range(H // 2):
                dYp, Xp = dY[:, pair * 128:(pair + 1) * 128], X[:, pair * 128:(pair + 1) * 128]
                dXm.append(jnp.where(k.lo, _dot(CBt * t["decay_t"][2 * pair], dYp),
                                     _dot(CBt * t["decay_t"][2 * pair + 1], dYp)))
                for mask in (k.lo, ~k.lo):
                    Gs.append(_dot(jnp.where(mask, dYp, 0.0), Xp, _NT))
                    Gts.append(_dot(jnp.where(mask, Xp, 0.0), dYp, _NT))
            yield
            dXs = dXs_raw * t["wdec"]
            dX = jnp.concatenate(dXm, axis=1) + dXs
            dCB, dCBt, q_sums = 0.0, 0.0, []
            for h in range(H):
                M, Mt = CB * t["decay"][h], CBt * t["decay_t"][h]
                dCB = dCB + Gs[h] * t["decay"][h]
                dCBt = dCBt + Gts[h] * t["decay_t"][h]
                d = Gs[h] * M - Gts[h] * Mt
                q_sums.append(sum(lax.dot_general(pt, k.ones, _NN, preferred_element_type=F32)
                                  for pt in _parts(d, 2)))
            q_f = jnp.concatenate([jnp.where(k.lo, q_sums[0], q_sums[1]), jnp.where(k.lo, q_sums[2], q_sums[3])],
                                  axis=1)
            x_dxs = k.head_sums(X * dXs)
            tot = [_total(dS1[h * P:(h + 1) * P, :] * S0[h * P:(h + 1) * P, :]) * t["ea_last"][h] for h in range(H)]
            tot_f = k.spread(jnp.concatenate(tot, axis=1))
            d_alast = _colsum(x_dxs) + tot_f
            dacum = q_f + k.head_sums(dY * (y_off_raw * t["eac"])) - x_dxs + jnp.where(last, d_alast, 0.0)
            dx_dt = k.head_sums(dX * x)
            d_skip = _colsum(k.head_sums(dY * x))
            for h in range(H):
                hsl = slice(gi * W + h * P, gi * W + (h + 1) * P)
                dstate[hsl, :] = t["ea_last"][h] * dS1[h * P:(h + 1) * P, :] + dS_c[h * P:(h + 1) * P, :]
            dc_s2 = _dot(dCB, Bm)
            db_s2 = _dot(dCBt, Cm)
            yield
            da = _sum_by(k.upper, dacum)
            yield
            ddt_raw = (da * t["A"] + dx_dt) * _sigmoid(t["dtr"])
            dx_ref[:, gsl] = (dX * t["dt"] + t["D"] * dY).astype(BF16)
            dc_ref[:, gi * N:(gi + 1) * N] = (dC_s + dc_s2).astype(BF16)
            db_ref[:, gi * N:(gi + 1) * N] = (dB_s + db_s2).astype(BF16)
            ddt_ref[gi] = k.gather4(ddt_raw)
            dp_ref[gi] += k.gather4(jnp.concatenate([_colsum(ddt_raw), _colsum(da * t["dt"]) * t["A"], d_skip],
                                                    axis=0))

        _lockstep(group(gi) for gi in range(G))

    rev = lambda c: nc - 1 - c
    x_spec, b_spec, c_spec, z_spec, dt_spec, p_spec, nw_spec, s_spec = _ssd_specs(L, rev)
    y_spec = pl.BlockSpec((L, G * W), lambda g, c: (rev(c), g))
    n_spec = pl.BlockSpec((L, G * N), lambda g, c: (rev(c), g))
    return pl.pallas_call(
        body, name=name, grid=(SSM_GROUPS // G, nc),
        in_specs=[y_spec, x_spec, b_spec, c_spec, z_spec, dt_spec, p_spec, nw_spec, y_spec, s_spec, D_PROJ_ANY],
        out_specs=(y_spec, n_spec, n_spec, z_spec, dt_spec, p_spec, nw_spec),
        out_shape=(jax.ShapeDtypeStruct((S, SSM_GROUPS * W), BF16), jax.ShapeDtypeStruct((S, SSM_GROUPS * N), BF16),
                   jax.ShapeDtypeStruct((S, SSM_GROUPS * N), BF16), jax.ShapeDtypeStruct(d_proj.shape, BF16),
                   jax.ShapeDtypeStruct((SSM_GROUPS, S, H), F32), jax.ShapeDtypeStruct((SSM_GROUPS, 3, H), F32),
                   jax.ShapeDtypeStruct((SSM_GROUPS, 1, W), F32)),
        scratch_shapes=[pltpu.VMEM((G * W, N), F32)], input_output_aliases={10: 3},
        compiler_params=_params("parallel", "arbitrary"))(dyn, conv, conv, conv, proj, dt_raw, pvec, nw, y_ssd, states,
                                                          d_proj)


def _unit_lower_inverse(A, ii, jj):
    eye = (ii == jj).astype(F32)
    same = (ii // GDN_INV_BLOCK) == (jj // GDN_INV_BLOCK)
    Ad = jnp.where(same, A, 0.0)
    Ao = A - Ad
    P2 = _dot3(Ad, Ad)
    yield
    P4, X = _dot(P2, P2), _dot3(eye - Ad, eye + P2)
    yield
    P8, X = _dot(P4, P4), X + _dot2(X, P4)
    yield
    X = X + _dot2(X, P8)
    yield
    Bm = _dot3(X, Ao)
    yield
    B2 = _dot3(Bm, Bm)
    yield
    Y = (eye - Bm) + B2 - _dot2(Bm, B2)
    yield
    T = _dot3(Y, X)
    yield
    return T


def _gdn_specs(L, order):
    G = GDN_QK_PER_STEP
    Hd, W = G * GDN_HEAD, G * GDN_V_PER_QK * GDN_HEAD
    q_spec = pl.BlockSpec((L, Hd), lambda h, c: (order(c), (C_QKV - C_XBC) // Hd + h))
    k_spec = pl.BlockSpec((L, Hd), lambda h, c: (order(c), (C_QKV - C_XBC + 1024) // Hd + h))
    v_spec = pl.BlockSpec((L, W), lambda h, c: (order(c), (C_QKV - C_XBC + 2048) // W + h))
    z_spec = pl.BlockSpec((L, W), lambda h, c: (order(c), C_ZG // W + h))
    ba_spec = pl.BlockSpec((G, L, GDN_V_PER_QK), lambda h, c: (h, order(c), 0))
    p_spec = pl.BlockSpec((G, 2, GDN_V_PER_QK), lambda h, c: (h, 0, 0))
    nw_spec = pl.BlockSpec((1, GDN_HEAD), lambda h, c: (0, 0))
    s_spec = pl.BlockSpec((G, 1, GDN_V_PER_QK * GDN_HEAD, GDN_HEAD), lambda h, c: (h, order(c), 0, 0))
    t_spec = pl.BlockSpec((G * GDN_V_PER_QK, 1, L, L), lambda h, c: (h, order(c), 0, 0))
    return q_spec, k_spec, v_spec, z_spec, ba_spec, p_spec, nw_spec, s_spec, t_spec


def _gdn_gates(qa, ka, b_col, a_col, p, j, ii, jj):
    L = qa.shape[0]
    sp_in = a_col + p[0:1, j:j + 1]
    neg_ea = -jnp.exp(p[1:2, j:j + 1])
    g = neg_ea * _softplus(sp_in)
    gcum, gcum_row = _cumsum_forms(g, ii, jj)
    rq = lax.rsqrt(_rowsum(qa * qa) + EPS)
    rk = lax.rsqrt(_rowsum(ka * ka) + EPS)
    q = qa * rq * (GDN_HEAD ** -0.5)
    k = ka * rk
    beta = _sigmoid(b_col)
    yield
    Dm = jnp.exp(jnp.where(ii >= jj, gcum - gcum_row, NEG_INF))
    eg = jnp.exp(gcum)
    g_last = gcum[L - 1:L, :]
    wdec = jnp.exp(g_last - gcum)
    return dict(rq=rq, rk=rk, q=q, k=k, beta=beta, sp_in=sp_in, neg_ea=neg_ea, g=g, Dm=Dm, kbeta=k * beta, eg=eg,
                g_last=g_last, wdec=wdec, kdec=k * wdec)


def _gdn_fwd(conv, proj, b_raw, a_raw, pvec, nw, name):
    S = conv.shape[0]
    L, Hd, J, G = GDN_CHUNK, GDN_HEAD, GDN_V_PER_QK, GDN_QK_PER_STEP
    W = J * Hd
    nc = S // L

    def body(q_ref, k_ref, v_ref, z_ref, b_ref, a_ref, p_ref, nw_ref, o_ref, on_ref, s0_ref, t_ref, state):
        c = pl.program_id(1)

        @pl.when(c == 0)
        def _():
            state[...] = jnp.zeros_like(state)

        ii, jj = _iota(L, L, 0), _iota(L, L, 1)
        for hq in range(G):
            s0_ref[hq, 0] = state[hq * W:(hq + 1) * W, :]

        def head(hq, j):
            hd = hq * J + j
            hsl, sl = slice(hq * Hd, (hq + 1) * Hd), slice(hd * Hd, (hd + 1) * Hd)
            t = yield from _gdn_gates(q_ref[:, hsl].astype(F32), k_ref[:, hsl].astype(F32), b_ref[hq][:, j:j + 1],
                                      a_ref[hq][:, j:j + 1],
                                      p_ref[hq], j, ii, jj)
            KK = _dot(t["kbeta"], t["k"], _NT)
            QK = _dot(t["q"], t["k"], _NT)
            yield
            T = yield from _unit_lower_inverse(jnp.where(ii > jj, KK * t["Dm"], 0.0), ii, jj)
            t_ref[hd, 0] = T
            S0 = state[sl, :]
            U = _dot2(T, v_ref[:, sl].astype(F32) * t["beta"])
            Wm = _dot2(T, t["kbeta"] * t["eg"])
            o_inter = _dot(t["q"] * t["eg"], S0)
            yield
            Vn = U - _dot(Wm, S0)
            yield
            o = o_inter + _dot(QK * t["Dm"], Vn)
            s_new = _dot(t["kdec"], Vn, _TN)
            yield
            state[sl, :] = S0 * jnp.exp(t["g_last"]) + s_new
            o_ref[:, sl] = o
            r = lax.rsqrt(jnp.mean(o * o, axis=-1, keepdims=True) + EPS)
            on_ref[:, sl] = ((o * r * nw_ref[...]) * _silu(z_ref[:, sl].astype(F32))).astype(BF16)

        _lockstep(head(hq, j) for hq in range(G) for j in range(J))

    q_spec, k_spec, v_spec, z_spec, ba_spec, p_spec, nw_spec, s_spec, t_spec = _gdn_specs(L, lambda c: c)
    o_spec = pl.BlockSpec((L, G * W), lambda h, c: (c, h))
    return pl.pallas_call(
        body, name=name, grid=(GDN_QK_HEADS // G, nc),
        in_specs=[q_spec, k_spec, v_spec, z_spec, ba_spec, ba_spec, p_spec, nw_spec],
        out_specs=(o_spec, o_spec, s_spec, t_spec),
        out_shape=(jax.ShapeDtypeStruct((S, GDN_QK_HEADS * W), F32), jax.ShapeDtypeStruct((S, GDN_QK_HEADS * W), BF16),
                   jax.ShapeDtypeStruct((GDN_QK_HEADS, nc, W, Hd), F32),
                   jax.ShapeDtypeStruct((GDN_QK_HEADS * J, nc, L, L), F32)),
        scratch_shapes=[pltpu.VMEM((G * W, Hd), F32)],
        compiler_params=_params("parallel", "arbitrary"))(conv, conv, conv, proj, b_raw, a_raw, pvec, nw)


def _gdn_bwd(don, conv, proj, b_raw, a_raw, pvec, nw, o_pre, states, t_inv, d_proj, name):
    S = conv.shape[0]
    L, Hd, J, G = GDN_CHUNK, GDN_HEAD, GDN_V_PER_QK, GDN_QK_PER_STEP
    W = J * Hd
    nc = S // L

    def body(don_ref, q_ref, k_ref, v_ref, z_ref, b_ref, a_ref, p_ref, nw_ref, o_ref, s0_ref, t_ref, _,
             dq_ref, dk_ref, dv_ref, dz_ref, db_ref, da_ref, dp_ref, dnw_ref, dstate):
        c = pl.program_id(1)

        @pl.when(c == 0)
        def _():
            dstate[...] = jnp.zeros_like(dstate)
            dp_ref[...] = jnp.zeros_like(dp_ref)
            dnw_ref[...] = jnp.zeros_like(dnw_ref)

        ii, jj = _iota(L, L, 0), _iota(L, L, 1)
        last = (_iota(L, 1, 0) == L - 1)
        res = {}

        def head(hq, j):
            hd = hq * J + j
            hsl, sl = slice(hq * Hd, (hq + 1) * Hd), slice(hd * Hd, (hd + 1) * Hd)
            qa, ka = q_ref[:, hsl].astype(F32), k_ref[:, hsl].astype(F32)
            t = yield from _gdn_gates(qa, ka, b_ref[hq][:, j:j + 1], a_ref[hq][:, j:j + 1], p_ref[hq], j, ii, jj)
            q, k, beta, eg, Dm, kbeta, kdec = (t[nm] for nm in ("q", "k", "beta", "eg", "Dm", "kbeta", "kdec"))
            T = t_ref[hd, 0]
            v, z, o = v_ref[:, sl].astype(F32), z_ref[:, sl].astype(F32), o_ref[:, sl]
            S0, dS1 = s0_ref[hq, 0, j * Hd:(j + 1) * Hd, :], dstate[sl, :]
            sz = _silu(z)
            r = lax.rsqrt(jnp.mean(o * o, axis=-1, keepdims=True) + EPS)
            oh = o * r
            d_on = don_ref[:, sl]
            dz_ref[:, sl] = (d_on * (oh * nw_ref[...]) * _dsilu(z)).astype(BF16)
            dn = d_on * sz
            dnw_part = _colsum(dn * oh)
            doh = dn * nw_ref[...]
            dO = r * (doh - oh * jnp.mean(doh * oh, axis=-1, keepdims=True))
            Rw = kbeta * eg
            qe = q * eg
            U = _dot2(T, v * beta)
            Wm = _dot2(T, Rw)
            KK = _dot(kbeta, k, _NT)
            QK = _dot(q, k, _NT)
            o_inter = _dot(qe, S0)
            dq_s = _dot(dO, S0, _NT)
            dS_q = _dot(qe, dO, _TN)
            yield
            Am = jnp.where(ii > jj, KK * Dm, 0.0)
            Pm = QK * Dm
            Vn = U - _dot(Wm, S0)
            dVn_s = _dot(kdec, dS1)
            yield
            dVn = _dot(Pm, dO, _TN) + dVn_s
            dP = _dot(dO, Vn, _NT)
            dKd = _dot(Vn, dS1, _NT)
            yield
            dQK = dP * Dm
            dq = _dot(dQK, k) + dq_s * eg
            dk = _dot(dQK, q, _TN) + dKd * t["wdec"]
            dstate[sl, :] = jnp.exp(t["g_last"]) * dS1 + dS_q - _dot(Wm, dVn, _TN)
            dW = -_dot(dVn, S0, _NT)
            dRu = _dot2(T, dVn, _TN)
            yield
            dRw = _dot2(T, dW, _TN)
            dA_u = _dot(dRu, U, _NT)
            yield
            dA = jnp.where(ii > jj, -(dA_u + _dot(dRw, Wm, _NT)), 0.0)
            yield
            dKK = dA * Dm
            dkbeta = _dot(dKK, k) + dRw * eg
            dk = dk + _dot(dKK, kbeta, _TN)
            yield
            dk = dk + dkbeta * beta
            dbeta = _rowsum(dkbeta * k) + _rowsum(dRu * v)
            dv_ref[:, sl] = (dRu * beta).astype(BF16)
            Q = dA * Am + dP * Pm
            rho = _rowsum(dKd * kdec)
            d_glast = _colsum(rho) + jnp.exp(t["g_last"]) * _total(dS1 * S0)
            q_sums = _row_col_sums(Q)
            rest = _rowsum(dRw * Rw) + _rowsum(dO * o_inter) - rho + jnp.where(last, d_glast, 0.0)
            yield
            dg = _rev_cumsum_col(q_sums + rest, ii, jj)
            yield
            da_raw = dg * t["neg_ea"] * _sigmoid(t["sp_in"])
            res[hq, j] = dict(dq=dq, dk=dk, db=dbeta * beta * (1.0 - beta), da=da_raw, d_bias=_colsum(da_raw),
                              d_alog=_colsum(dg * t["g"]), dnw=dnw_part, rq=t["rq"], rk=t["rk"], k=k, qh=qa * t["rq"])

        _lockstep(head(hq, j) for hq in range(G) for j in range(J))
        for hq in range(G):
            parts = [res[hq, j] for j in range(J)]
            hsl = slice(hq * Hd, (hq + 1) * Hd)
            p0 = parts[0]
            dqh = sum(pt["dq"] for pt in parts) * (GDN_HEAD ** -0.5)
            dkn = sum(pt["dk"] for pt in parts)
            dq_ref[:, hsl] = (p0["rq"] * (dqh - p0["qh"] * _rowsum(dqh * p0["qh"]))).astype(BF16)
            dk_ref[:, hsl] = (p0["rk"] * (dkn - p0["k"] * _rowsum(dkn * p0["k"]))).astype(BF16)
            db_ref[hq] = jnp.concatenate([pt["db"] for pt in parts], axis=1)
            da_ref[hq] = jnp.concatenate([pt["da"] for pt in parts], axis=1)
            dp_ref[hq] += jnp.concatenate([jnp.concatenate([pt["d_bias"] for pt in parts], axis=1),
                                           jnp.concatenate([pt["d_alog"] for pt in parts], axis=1)], axis=0)
            dnw_ref[hq] += sum(pt["dnw"] for pt in parts)

    rev = lambda c: nc - 1 - c
    q_spec, k_spec, v_spec, z_spec, ba_spec, p_spec, nw_spec, s_spec, t_spec = _gdn_specs(L, rev)
    o_spec = pl.BlockSpec((L, G * W), lambda h, c: (rev(c), h))
    h_spec = pl.BlockSpec((L, G * Hd), lambda h, c: (rev(c), h))
    dnw_spec = pl.BlockSpec((G, 1, Hd), lambda h, c: (h, 0, 0))
    return pl.pallas_call(
        body, name=name, grid=(GDN_QK_HEADS // G, nc),
        in_specs=[o_spec, q_spec, k_spec, v_spec, z_spec, ba_spec, ba_spec, p_spec, nw_spec, o_spec, s_spec, t_spec,
                  D_PROJ_ANY],
        out_specs=(h_spec, h_spec, o_spec, z_spec, ba_spec, ba_spec, p_spec, dnw_spec),
        out_shape=(jax.ShapeDtypeStruct((S, GDN_QK_HEADS * Hd), BF16), jax.ShapeDtypeStruct((S, GDN_QK_HEADS * Hd), BF16),
                   jax.ShapeDtypeStruct((S, GDN_QK_HEADS * W), BF16), jax.ShapeDtypeStruct(d_proj.shape, BF16),
                   jax.ShapeDtypeStruct((GDN_QK_HEADS, S, J), F32), jax.ShapeDtypeStruct((GDN_QK_HEADS, S, J), F32),
                   jax.ShapeDtypeStruct((GDN_QK_HEADS, 2, J), F32), jax.ShapeDtypeStruct((GDN_QK_HEADS, 1, Hd), F32)),
        scratch_shapes=[pltpu.VMEM((G * W, Hd), F32)], input_output_aliases={12: 3},
        compiler_params=_params("parallel", "arbitrary"))(don, conv, conv, conv, proj, b_raw, a_raw, pvec, nw, o_pre,
                                                          states, t_inv, d_proj)


def _ada_fwd(c_all, w_loc, b_loc, name):
    n = w_loc.shape[1]

    def body(c_ref, w_ref, b_ref, o_ref):
        o_ref[...] = _dot3(_silu(c_ref[...]), w_ref[...]) + b_ref[...]

    return pl.pallas_call(body, name=name, out_shape=jax.ShapeDtypeStruct((N_DEV, n), F32),
                          compiler_params=pltpu.CompilerParams(vmem_limit_bytes=VMEM_LIMIT))(c_all, w_loc, b_loc)


def _ada_bwd(c_all_t, dmod_cols, name):
    Dm, n = c_all_t.shape[0], dmod_cols.shape[1]

    def body(c_ref, d_ref, o_ref):
        ca = _silu(c_ref[...])
        acc = ca[:, 0:1] * d_ref[0:1, :]
        for i in range(1, N_DEV):
            acc = acc + ca[:, i:i + 1] * d_ref[i:i + 1, :]
        o_ref[...] = acc

    return pl.pallas_call(body, name=name, out_shape=jax.ShapeDtypeStruct((Dm, n), F32),
                          compiler_params=pltpu.CompilerParams(vmem_limit_bytes=VMEM_LIMIT))(c_all_t, dmod_cols)


ADAM_BLOCK_BYTES = 12 * 1024 * 1024


def _adam(contrib, w, m, v, name):
    n, R, C = contrib.shape
    tr = R
    while tr % 16 == 0 and (n + 7) * tr * C * 4 > ADAM_BLOCK_BYTES:
        tr //= 2

    def body(c_ref, w_ref, m_ref, v_ref, g_ref, d_ref, nm_ref, nv_ref):
        g = c_ref[0].astype(F32)
        for i in range(1, n):
            g = g + c_ref[i].astype(F32)
        nm = ADAM_B1 * m_ref[...] + (1.0 - ADAM_B1) * g
        nv = ADAM_B2 * v_ref[...] + (1.0 - ADAM_B2) * (g * g)
        m_hat = nm / (1.0 - ADAM_B1 ** ADAM_STEP)
        v_hat = nv / (1.0 - ADAM_B2 ** ADAM_STEP)
        g_ref[...] = g
        d_ref[...] = -ADAM_LR * (m_hat / (jnp.sqrt(v_hat) + ADAM_EPS) + ADAM_WD * w_ref[...])
        nm_ref[...] = nm
        nv_ref[...] = nv

    spec = pl.BlockSpec((tr, C), lambda i: (i, 0))
    shp = jax.ShapeDtypeStruct((R, C), F32)
    return pl.pallas_call(
        body, name=name, grid=(R // tr,), in_specs=[pl.BlockSpec((n, tr, C), lambda i: (0, i, 0)), spec, spec, spec],
        out_specs=(spec,) * 4, out_shape=(shp,) * 4, compiler_params=_params("parallel"))(contrib, w, m, v)


def _exchange(arrays, modes, name, chips=False):
    n = len(arrays)
    out_shape = tuple(jax.ShapeDtypeStruct((N_DEV,) + a.shape if md == "gather" else a.shape, a.dtype)
                      for a, md in zip(arrays, modes))

    def body(*refs):
        ins, outs = refs[:n], refs[n:2 * n]
        send_sems, recv_sems, loc_sems = refs[2 * n:]
        me, peers = _peer_table(chips)

        def src(k, slot):
            return ins[k] if modes[k] == "gather" else ins[k].at[slot]

        def remote(k, m, to_slot, land_slot):
            return pltpu.make_async_remote_copy(
                src_ref=src(k, to_slot), dst_ref=outs[k].at[land_slot], send_sem=send_sems.at[k, m],
                recv_sem=recv_sems.at[k, m], device_id=peers[m][0], device_id_type=pl.DeviceIdType.MESH)

        local = [pltpu.make_async_copy(src(k, me), outs[k].at[me], loc_sems.at[k]) for k in range(n)]
        for cp in local:
            cp.start()
        sends = [remote(k, m, peers[m][1], me) for m in range(len(peers)) for k in range(n)]
        for cp in sends:
            cp.start()
        for m in range(len(peers)):
            for k in range(n):
                remote(k, m, peers[m][1], peers[m][1]).wait_recv()
        for cp in sends:
            cp.wait_send()
        for cp in local:
            cp.wait()

    any_spec = pl.BlockSpec(memory_space=pl.ANY)
    return pl.pallas_call(
        body, name=name, in_specs=[any_spec] * n, out_specs=(any_spec,) * n, out_shape=out_shape,
        scratch_shapes=[pltpu.SemaphoreType.DMA((n, N_DEV - 1)), pltpu.SemaphoreType.DMA((n, N_DEV - 1)),
                        pltpu.SemaphoreType.DMA((n,))])(*arrays)


def _gather_two_level(arrays, name):
    n = len(arrays)
    out_shape = tuple(jax.ShapeDtypeStruct((N_DEV,) + a.shape, a.dtype) for a in arrays)

    def body(*refs):
        ins, outs = refs[:n], refs[n:2 * n]
        send_sems, recv_sems, loc_sems = refs[2 * n:]
        ix, iy, ic = lax.axis_index("x"), lax.axis_index("y"), lax.axis_index("c")
        lin = lambda px, py, pc: 4 * px + 2 * py + pc
        me, sib = lin(ix, iy, ic), (ix, iy, 1 - ic)
        chips = [(1 - ix, iy), (ix, 1 - iy), (1 - ix, 1 - iy)]

        def copy(k, s, block, to, src=None):
            return pltpu.make_async_remote_copy(
                src_ref=outs[k].at[block] if src is None else src, dst_ref=outs[k].at[block],
                send_sem=send_sems.at[k, s], recv_sem=recv_sems.at[k, s], device_id=to,
                device_id_type=pl.DeviceIdType.MESH)

        local = [pltpu.make_async_copy(ins[k], outs[k].at[me], loc_sems.at[k]) for k in range(n)]
        for cp in local:
            cp.start()
        first = [copy(k, 1 + j, me, (cx, cy, ic), src=ins[k]) for j, (cx, cy) in enumerate(chips) for k in range(n)]
        first += [copy(k, 0, me, sib, src=ins[k]) for k in range(n)]
        for cp in first:
            cp.start()
        passed = []
        for j, (cx, cy) in enumerate(chips):
            for k in range(n):
                copy(k, 1 + j, lin(cx, cy, ic), sib).wait_recv()
                passed.append(copy(k, 4 + j, lin(cx, cy, ic), sib))
                passed[-1].start()
        for k in range(n):
            copy(k, 0, lin(*sib), sib).wait_recv()
            for j, (cx, cy) in enumerate(chips):
                copy(k, 4 + j, lin(cx, cy, 1 - ic), sib).wait_recv()
        for cp in first + passed:
            cp.wait_send()
        for cp in local:
            cp.wait()

    any_spec = pl.BlockSpec(memory_space=pl.ANY)
    return pl.pallas_call(
        body, name=name, in_specs=[any_spec] * n, out_specs=(any_spec,) * n, out_shape=out_shape,
        scratch_shapes=[pltpu.SemaphoreType.DMA((n, N_DEV - 1)), pltpu.SemaphoreType.DMA((n, N_DEV - 1)),
                        pltpu.SemaphoreType.DMA((n,))])(*arrays)


def _peer_table(chips=False):
    ix, iy, ic = lax.axis_index("x"), lax.axis_index("y"), lax.axis_index("c")
    peers = []
    for m in ((2, 4, 6) if chips else range(1, N_DEV)):
        px = 1 - ix if m & 4 else ix
        py = 1 - iy if m & 2 else iy
        pc = 1 - ic if m & 1 else ic
        peers.append(((px, py, pc), 2 * px + py if chips else 4 * px + 2 * py + pc))
    return (2 * ix + iy if chips else 4 * ix + 2 * iy + ic), peers


def _exchange_start(arrays, modes, after, name, chips=False):
    n = len(arrays)
    land_shapes = [(N_DEV,) + a.shape if md == "gather" else a.shape for a, md in zip(arrays, modes)]

    def body(*refs):
        ins, lands = refs[:n], refs[n:2 * n]
        send_sems, recv_sems = refs[2 * n + 1], refs[2 * n + 2]
        token = refs[-1]
        me, peers = _peer_table(chips)

        def src(k, slot):
            return ins[k] if modes[k] == "gather" else ins[k].at[slot]

        for peer, slot in peers:
            for k in range(n):
                pltpu.make_async_remote_copy(
                    src_ref=src(k, slot), dst_ref=lands[k].at[me], send_sem=send_sems, recv_sem=recv_sems,
                    device_id=peer, device_id_type=pl.DeviceIdType.MESH).start()
        token[...] = jnp.zeros_like(token)

    hbm = pl.BlockSpec(memory_space=pltpu.HBM)
    sem = pl.BlockSpec(memory_space=pltpu.SEMAPHORE)
    sem_shape = pltpu.SemaphoreType.DMA(())
    operands = [pltpu.with_memory_space_constraint(a, pltpu.HBM) for a in arrays]
    operands += [pltpu.with_memory_space_constraint(lax.empty(s, a.dtype), pltpu.HBM)
                 for s, a in zip(land_shapes, arrays)]
    out = pl.pallas_call(
        body, name=name,
        out_shape=(sem_shape, sem_shape) + tuple(pltpu.HBM(a.shape, a.dtype) for a in arrays)
        + tuple(pltpu.HBM(s, a.dtype) for s, a in zip(land_shapes, arrays)) + (jax.ShapeDtypeStruct((8, 128), F32),),
        in_specs=[hbm] * (2 * n) + [pl.BlockSpec(memory_space=pl.ANY)],
        out_specs=(sem, sem) + (hbm,) * (2 * n) + (pl.BlockSpec(memory_space=pltpu.VMEM),),
        input_output_aliases={i: 2 + i for i in range(2 * n)},
        compiler_params=pltpu.CompilerParams(has_side_effects=pltpu.SideEffectType.DATAFLOW_SIDE_EFFECTING))(
            *operands, after)
    return out[0], out[1], out[2:2 + n], out[2 + n:2 + 2 * n], out[-1]


def _exchange_wait(started, modes, after, name, chips=False):
    send_sems, recv_sems, sent, lands, _ = started
    n = len(sent)

    def body(*refs):
        ins, zones = refs[:n], refs[n:2 * n]
        send_ref, recv_ref = refs[2 * n], refs[2 * n + 1]
        _, peers = _peer_table(chips)

        def src(k, slot):
            return ins[k] if modes[k] == "gather" else ins[k].at[slot]

        for peer, slot in peers:
            for k in range(n):
                cp = pltpu.make_async_remote_copy(
                    src_ref=src(k, slot), dst_ref=zones[k].at[slot], send_sem=send_ref, recv_sem=recv_ref,
                    device_id=peer, device_id_type=pl.DeviceIdType.MESH)
                cp.wait_send()
                cp.wait_recv()

    hbm = pl.BlockSpec(memory_space=pltpu.HBM)
    sem = pl.BlockSpec(memory_space=pltpu.SEMAPHORE)
    out = pl.pallas_call(
        body, name=name,
        out_shape=tuple(pltpu.HBM(a.shape, a.dtype) for a in sent) + tuple(pltpu.HBM(a.shape, a.dtype) for a in lands),
        in_specs=[hbm] * (2 * n) + [sem, sem, pl.BlockSpec(memory_space=pl.ANY)], out_specs=(hbm,) * (2 * n),
        input_output_aliases={i: i for i in range(2 * n)},
        compiler_params=pltpu.CompilerParams(has_side_effects=pltpu.SideEffectType.DATAFLOW_SIDE_EFFECTING))(
            *sent, *lands, send_sems, recv_sems, after)
    ix, iy, ic = lax.axis_index("x"), lax.axis_index("y"), lax.axis_index("c")
    me = 2 * ix + iy if chips else 4 * ix + 2 * iy + ic
    filled = []
    for k in range(n):
        own = sent[k] if modes[k] == "gather" else lax.dynamic_index_in_dim(sent[k], me, axis=0, keepdims=False)
        filled.append(lax.dynamic_update_index_in_dim(out[n + k], own, me, axis=0))
    return filled


def _swap_sibling(to_c0, to_c1, name):
    def body(c0_ref, c1_ref, out_ref, send_sem, recv_sem):
        ix, iy, ic = lax.axis_index("x"), lax.axis_index("y"), lax.axis_index("c")

        def copy(src):
            return pltpu.make_async_remote_copy(src_ref=src, dst_ref=out_ref, send_sem=send_sem, recv_sem=recv_sem,
                                                device_id=(ix, iy, 1 - ic), device_id_type=pl.DeviceIdType.MESH)

        @pl.when(ic == 0)
        def _():
            copy(c1_ref).start()

        @pl.when(ic == 1)
        def _():
            copy(c0_ref).start()

        copy(c0_ref).wait()

    any_spec = pl.BlockSpec(memory_space=pl.ANY)
    return pl.pallas_call(body, name=name, in_specs=[any_spec, any_spec], out_specs=any_spec,
                          out_shape=jax.ShapeDtypeStruct(to_c0.shape, to_c0.dtype),
                          scratch_shapes=[pltpu.SemaphoreType.DMA, pltpu.SemaphoreType.DMA])(to_c0, to_c1)


def _add_pair(to_c0, to_c1, got, name):
    n, R, C = got.shape
    tr = _blk(R, 256)

    def body(c0_ref, c1_ref, got_ref, o_ref):
        ic = lax.axis_index("c")

        @pl.when(ic == 0)
        def _():
            o_ref[...] = (c0_ref[...].astype(F32) + got_ref[...].astype(F32)).astype(o_ref.dtype)

        @pl.when(ic == 1)
        def _():
            o_ref[...] = (c1_ref[...].astype(F32) + got_ref[...].astype(F32)).astype(o_ref.dtype)

    spec = pl.BlockSpec((1, tr, C), lambda i, j: (i, j, 0))
    return pl.pallas_call(body, name=name, grid=(n, R // tr), in_specs=[spec, spec, spec], out_specs=spec,
                          out_shape=jax.ShapeDtypeStruct(got.shape, got.dtype),
                          compiler_params=_params("parallel", "parallel"))(to_c0, to_c1, got)


W_IN_SPLITS = (0, 2048, 6144, 6176, 10272, 12320, 12336, 12352, 13376, 14400)
N_REPLICATED = 16640
REPLICATED = ("b_ada", "norm_mix_pre", "norm_mix_post", "ssm_conv_b", "ssm_dt_bias", "ssm_A_log", "ssm_D",
              "ssm_norm_w", "gdn_dt_bias", "gdn_A_log", "gdn_norm_w", "norm_mlp_pre", "norm_mlp_post")
WEIGHTS = ("w_ada", "b_ada", "norm_mix_pre", "norm_mix_post", "w_in", "ssm_conv_w", "ssm_conv_b", "ssm_dt_bias",
           "ssm_A_log", "ssm_D", "ssm_norm_w", "gdn_conv_w", "gdn_dt_bias", "gdn_A_log", "gdn_norm_w", "w_ssm_up",
           "w_gdn_up", "w_out", "norm_mlp_pre", "norm_mlp_post", "w_mlp_up", "w_mlp_down")


def _cols_of_shards(g, a, b):
    width, pieces = g.shape[2], []
    while a < b:
        i = a // width
        hi = min(b, (i + 1) * width)
        pieces.append(g[i][:, a - i * width:hi - i * width])
        a = hi
    return pieces


ORIG_SEGMENTS = ((0, 6144, "main", 0), (6144, 6176, "small", 0), (6176, 12320, "main", 6144),
                 (12320, 12352, "small", 32), (12352, 14400, "main", 12288))


def _orig_cols(main_cols, small_cols, a, b):
    pieces = []
    for s0, s1, which, off in ORIG_SEGMENTS:
        lo, hi = max(a, s0), min(b, s1)
        if lo < hi:
            pieces.append((main_cols if which == "main" else small_cols)[:, off + lo - s0:off + hi - s0])
    return jnp.concatenate(pieces, axis=1)


def _by_cols(t):
    return t.transpose(1, 0, 2).reshape(t.shape[1], N_DEV * t.shape[2])


def _to_col_shards(t):
    R, C8 = t.shape
    return t.reshape(R, N_DEV, C8 // N_DEV).transpose(1, 0, 2)


def _heads_first(t, groups):
    S = t.shape[0]
    return t.reshape(S, groups, t.shape[1] // groups).transpose(1, 0, 2)


def _heads_last(t):
    return t.transpose(1, 0, 2).reshape(t.shape[1], t.shape[0] * t.shape[2])


def kernel(x, c, w_ada, b_ada, norm_mix_pre, norm_mix_post, w_in, ssm_conv_w, ssm_conv_b, ssm_dt_bias, ssm_A_log, ssm_D, ssm_norm_w, gdn_conv_w, gdn_dt_bias, gdn_A_log, gdn_norm_w, w_ssm_up, w_gdn_up, w_out, norm_mlp_pre, norm_mlp_post, w_mlp_up, w_mlp_down, loss_target, m_w_ada, m_b_ada, m_norm_mix_pre, m_norm_mix_post, m_w_in, m_ssm_conv_w, m_ssm_conv_b, m_ssm_dt_bias, m_ssm_A_log, m_ssm_D, m_ssm_norm_w, m_gdn_conv_w, m_gdn_dt_bias, m_gdn_A_log, m_gdn_norm_w, m_w_ssm_up, m_w_gdn_up, m_w_out, m_norm_mlp_pre, m_norm_mlp_post, m_w_mlp_up, m_w_mlp_down, v_w_ada, v_b_ada, v_norm_mix_pre, v_norm_mix_post, v_w_in, v_ssm_conv_w, v_ssm_conv_b, v_ssm_dt_bias, v_ssm_A_log, v_ssm_D, v_ssm_norm_w, v_gdn_conv_w, v_gdn_dt_bias, v_gdn_A_log, v_gdn_norm_w, v_w_ssm_up, v_w_gdn_up, v_w_out, v_norm_mlp_pre, v_norm_mlp_post, v_w_mlp_up, v_w_mlp_down):
    S, Dm = x.shape[1], D_MODEL
    me = 4 * lax.axis_index("x") + 2 * lax.axis_index("y") + lax.axis_index("c")
    x2, tgt = x[0], loss_target[0]
    n_ada = w_ada.shape[2]
    given = dict(
        w_ada=(w_ada, m_w_ada, v_w_ada), b_ada=(b_ada, m_b_ada, v_b_ada),
        norm_mix_pre=(norm_mix_pre, m_norm_mix_pre, v_norm_mix_pre),
        norm_mix_post=(norm_mix_post, m_norm_mix_post, v_norm_mix_post), w_in=(w_in, m_w_in, v_w_in),
        ssm_conv_w=(ssm_conv_w, m_ssm_conv_w, v_ssm_conv_w), ssm_conv_b=(ssm_conv_b, m_ssm_conv_b, v_ssm_conv_b),
        ssm_dt_bias=(ssm_dt_bias, m_ssm_dt_bias, v_ssm_dt_bias), ssm_A_log=(ssm_A_log, m_ssm_A_log, v_ssm_A_log),
        ssm_D=(ssm_D, m_ssm_D, v_ssm_D), ssm_norm_w=(ssm_norm_w, m_ssm_norm_w, v_ssm_norm_w),
        gdn_conv_w=(gdn_conv_w, m_gdn_conv_w, v_gdn_conv_w), gdn_dt_bias=(gdn_dt_bias, m_gdn_dt_bias, v_gdn_dt_bias),
        gdn_A_log=(gdn_A_log, m_gdn_A_log, v_gdn_A_log), gdn_norm_w=(gdn_norm_w, m_gdn_norm_w, v_gdn_norm_w),
        w_ssm_up=(w_ssm_up, m_w_ssm_up, v_w_ssm_up), w_gdn_up=(w_gdn_up, m_w_gdn_up, v_w_gdn_up),
        w_out=(w_out, m_w_out, v_w_out), norm_mlp_pre=(norm_mlp_pre, m_norm_mlp_pre, v_norm_mlp_pre),
        norm_mlp_post=(norm_mlp_post, m_norm_mlp_post, v_norm_mlp_post), w_mlp_up=(w_mlp_up, m_w_mlp_up, v_w_mlp_up),
        w_mlp_down=(w_mlp_down, m_w_mlp_down, v_w_mlp_down))

    (c_all, scw, gcw, g_in) = _gather_two_level([c, ssm_conv_w[0], gdn_conv_w[0], w_in[0].astype(BF16)], "gather_w_in")
    c_all = c_all.reshape(N_DEV, Dm)
    sp = W_IN_SPLITS
    w_main = jnp.concatenate(_cols_of_shards(g_in, sp[0], sp[2]) + _cols_of_shards(g_in, sp[3], sp[5])
                             + _cols_of_shards(g_in, sp[7], sp[9]), axis=1)
    w_small = jnp.concatenate(_cols_of_shards(g_in, sp[2], sp[3]) + _cols_of_shards(g_in, sp[5], sp[7])
                              + [jnp.zeros((Dm, N_SMALL - 64), BF16)], axis=1)
    conv_w = jnp.concatenate([_by_cols(scw), _by_cols(gcw)], axis=1)
    conv_b = jnp.concatenate([ssm_conv_b, jnp.zeros_like(ssm_conv_b)], axis=1)

    b_loc = lax.dynamic_slice(b_ada, (0, me * n_ada), (1, n_ada))
    mod_part = _ada_fwd(c_all, w_ada[0], b_loc, "ada_fwd")
    (mod_rows,) = _exchange([mod_part.reshape(N_DEV, 1, n_ada)], ["a2a"], "exchange_mod")
    rest = _exchange_start([w_ssm_up[0].astype(BF16), w_gdn_up[0].astype(BF16), w_out[0].astype(BF16),
                            w_mlp_up[0].astype(BF16), w_mlp_down[0].astype(BF16)], ["gather"] * 5, mod_rows,
                           "gather_rest_start")
    mod = mod_rows.reshape(1, 6 * Dm) + rest[4][0:1, 0:1]
    sh1, sc1, g1, sh2, sc2, g2 = [mod[:, i * Dm:(i + 1) * Dm] for i in range(6)]

    h = _pre_fwd(x2, norm_mix_pre, sc1, sh1, "pre_mix")
    proj = _mm(h, w_main, S, N_MAIN, Dm, mode="nn", out_dtype=BF16, name="proj_main")
    small = _mm(h, w_small, S, N_SMALL, Dm, mode="nn", out_dtype=F32, name="proj_small")
    conv = _conv_fwd(proj, conv_w, conv_b, "conv_fwd")
    dt_g, b_g, a_g = _heads_first(small[:, 0:32], 8), _heads_first(small[:, 32:48], 8), _heads_first(small[:, 48:64], 8)
    pv_ssm = jnp.stack([ssm_dt_bias.reshape(8, 4), ssm_A_log.reshape(8, 4), ssm_D.reshape(8, 4)], axis=1)
    nw_ssm = ssm_norm_w.reshape(8, 1, SSM_GROUP_WIDTH)
    pv_gdn = jnp.stack([gdn_dt_bias.reshape(8, 2), gdn_A_log.reshape(8, 2)], axis=1)
    y_ssd, ysn, st_ssm = _ssd_fwd(conv, proj, dt_g, pv_ssm, nw_ssm, "ssd_fwd")
    o_pre, ogn, st_gdn, t_inv = _gdn_fwd(conv, proj, b_g, a_g, pv_gdn, gdn_norm_w, "gdn_fwd")
    g_su, g_gu, g_out, g_mu, g_md = _exchange_wait(rest, ["gather"] * 5, ogn, "gather_rest_wait")
    w_su, w_gu = g_su.reshape(2 * Dm, Dm), g_gu.reshape(2 * Dm, Dm)
    w_o, w_mu, w_md = g_out.reshape(Dm, Dm), _by_cols(g_mu), g_md.reshape(4 * Dm, Dm)
    ys = _mm(ysn, w_su, S, Dm, 2 * Dm, mode="nn", out_dtype=F32, name="ssm_up")
    yg, merged = _gdn_up_merge(ogn, w_gu, ys, proj, "gdn_up_merge")
    mo, x1, h2 = _mix_out_post_pre(merged, w_o, x2, norm_mix_post, g1, norm_mlp_pre, sc2, sh2, "mix_out_post_pre")
    u, act = _mm(h2, w_mu, S, 4 * Dm, Dm, mode="nn", out_dtype=F32, epi="relu2", name="mlp_up")
    y_mlp = _mm(act, w_md, S, Dm, 4 * Dm, mode="nn", out_dtype=F32, name="mlp_down")
    dx2, loss_loc, dy, dg2, dw_post2 = _final_fwd_bwd(x1, y_mlp, norm_mlp_post, g2, tgt, "post_mlp_loss_bwd")

    du = _mm(dy, w_md, S, 4 * Dm, Dm, mode="nt", out_dtype=BF16, epi="drelu2", extra=u, name="mlp_down_dx")
    gw_md = _mm(act, dy, 4 * Dm, Dm, S, mode="tn", out_dtype=BF16, name="mlp_down_dw")
    dh2 = _mm(du, w_mu, S, Dm, 4 * Dm, mode="nt", out_dtype=F32, name="mlp_up_dx")
    gw_mu = _mm(h2, du, Dm, 4 * Dm, S, mode="tn", out_dtype=BF16, name="mlp_up_dw")
    mlp_x = _exchange_start([_to_col_shards(gw_mu), gw_md.reshape(N_DEV, -1, Dm)], ["a2a"] * 2, gw_md,
                            "grads_mlp_start")
    dx1, dsh2, dsc2, dw_pre2 = _pre_bwd(dh2, x1, norm_mlp_pre, sc2 + mlp_x[4][0:1, 0:1], dx2, "pre_mlp_bwd")
    dmo, dg1, dw_post1 = _post_bwd(dx1, mo, norm_mix_post, g1, "post_mix_bwd")
    gw_o = _mm(merged, dmo, Dm, Dm, S, mode="tn", out_dtype=BF16, name="mix_out_dw")
    dys, dyg, d_proj = _mix_out_dx_merge_bwd(dmo, w_o, ys, yg, proj, lax.empty((S, N_MAIN), BF16), "mix_out_dx_merge")
    dysn = _mm(dys, w_su, S, 2 * Dm, Dm, mode="nt", out_dtype=F32, name="ssm_up_dx")
    gw_su = _mm(ysn, dys, 2 * Dm, Dm, S, mode="tn", out_dtype=BF16, name="ssm_up_dw")
    dogn = _mm(dyg, w_gu, S, 2 * Dm, Dm, mode="nt", out_dtype=F32, name="gdn_up_dx")
    gw_gu = _mm(ogn, dyg, 2 * Dm, Dm, S, mode="tn", out_dtype=BF16, name="gdn_up_dw")
    mix_x = _exchange_start([gw_su.reshape(N_DEV, -1, Dm), gw_gu.reshape(N_DEV, -1, Dm), gw_o.reshape(N_DEV, -1, Dm)],
                            ["a2a"] * 3, gw_gu, "grads_mix_start")
    dxs, dBm, dCm, d_proj, ddt_g, dpv_ssm, dnw_ssm = _ssd_bwd(dysn, conv, proj, dt_g, pv_ssm + mix_x[4][0, 0], nw_ssm,
                                                              y_ssd, st_ssm, d_proj, "ssd_bwd")
    dq, dk, dv, d_proj, db_g, da_g, dpv_gdn, dnw_gdn = _gdn_bwd(dogn, conv, proj, b_g, a_g, pv_gdn, gdn_norm_w, o_pre,
                                                                st_gdn, t_inv, d_proj, "gdn_bwd")
    conv_pieces = []
    for nm, d_act, col0 in (("xs", dxs, 0), ("B", dBm, 2048), ("C", dCm, 3072), ("q", dq, 4096), ("k", dk, 5120),
                            ("v", dv, 6144)):
        d_proj, dw_piece, db_piece = _conv_bwd(d_act, proj, conv_w, conv_b, col0, d_proj, "conv_bwd_" + nm)
        conv_pieces.append((dw_piece, db_piece))
    d_small = jnp.concatenate([_heads_last(ddt_g), _heads_last(db_g), _heads_last(da_g),
                               jnp.zeros((S, N_SMALL - 64), F32)], axis=1).astype(BF16)
    gw_small = _mm(h, d_small, Dm, N_SMALL, S, mode="tn", out_dtype=BF16, name="proj_small_dw")
    main_cols = _mm(h, d_proj, Dm, N_MAIN, S, mode="tn", out_dtype=BF16, name="proj_main_dw")
    n_shard = w_in.shape[2]
    slabs = [_orig_cols(main_cols, gw_small, i * n_shard, (i + 1) * n_shard) for i in range(N_DEV)]
    to_c0, to_c1 = jnp.stack(slabs[0::2]), jnp.stack(slabs[1::2])
    chip_sum = _add_pair(to_c0, to_c1, _swap_sibling(to_c0, to_c1, "grads_w_in_pair"), "grads_w_in_pair_sum")
    in_x = _exchange_start([chip_sum], ["a2a"], gw_small, "grads_w_in_start", chips=True)
    dh = _mm(d_small, w_small + in_x[4][0:1, 0:1].astype(BF16), S, Dm, N_SMALL, mode="nt", out_dtype=F32,
             name="proj_small_dx")
    dh = _mm(d_proj, w_main, S, Dm, N_MAIN, mode="nt", out_dtype=F32, add=dh, name="proj_main_dx")
    dx, dsh1, dsc1, dw_pre1 = _pre_bwd(dh, x2, norm_mix_pre, sc1, dx1, "pre_mix_bwd")
    r_mu, r_md = _exchange_wait(mlp_x, ["a2a"] * 2, dx, "grads_mlp_wait")
    r_su, r_gu, r_o = _exchange_wait(mix_x, ["a2a"] * 3, dx, "grads_mix_wait")

    dconv_w = jnp.concatenate([p[0] for p in conv_pieces], axis=1)
    dconv_b = jnp.concatenate([p[1] for p in conv_pieces[:3]], axis=1)
    dmod = jnp.concatenate([dsh1, dsc1, dg1, dsh2, dsc2, dg2], axis=1)
    small_vec = jnp.concatenate(
        [dmod, dw_pre1, dw_post1, dconv_b, dpv_ssm[:, 0].reshape(1, 32), dpv_ssm[:, 1].reshape(1, 32),
         dpv_ssm[:, 2].reshape(1, 32), dnw_ssm.reshape(1, 2048), dpv_gdn[:, 0].reshape(1, 16),
         dpv_gdn[:, 1].reshape(1, 16), jnp.sum(dnw_gdn, axis=0), dw_pre2, dw_post2, dconv_w.reshape(1, -1)], axis=1)
    n_vec = small_vec.shape[1]
    small_vec = jnp.pad(small_vec, ((0, 0), (0, (-n_vec) % 1024))).reshape(-1, 1024)
    (small_all,) = _exchange([small_vec], ["gather"], "gather_small_grads")
    small_all = small_all.reshape(N_DEV, -1)
    dmod_cols = lax.dynamic_slice(small_all, (0, me * n_ada), (N_DEV, n_ada))
    gw_ada = _ada_bwd(c_all.T, dmod_cols, "ada_bwd")
    conv_all = small_all[:, N_REPLICATED:n_vec].reshape(N_DEV, CONV_K, 2 * N_DEV * 512)
    conv_contrib = jnp.concatenate(
        [lax.dynamic_slice(conv_all, (0, 0, me * 512), (N_DEV, CONV_K, 512)),
         lax.dynamic_slice(conv_all, (0, 0, N_DEV * 512 + me * 512), (N_DEV, CONV_K, 512))], axis=1)
    rep_contrib = small_all[:, :N_REPLICATED].reshape(N_DEV, N_REPLICATED // 128, 128)

    results = {}

    def adam_big(nm, contrib):
        w3 = given[nm]
        res = _adam(contrib, w3[0][0], w3[1][0], w3[2][0], "adam_" + nm)
        results[nm] = tuple(r.reshape(w3[0].shape) for r in res)

    adam_big("w_ada", gw_ada[None])
    adam_big("w_ssm_up", r_su)
    adam_big("w_gdn_up", r_gu)
    adam_big("w_out", r_o)
    adam_big("w_mlp_up", r_mu)
    adam_big("w_mlp_down", r_md)
    (r_in,) = _exchange_wait(in_x, ["a2a"], results["w_mlp_down"][0], "grads_w_in_wait", chips=True)
    adam_big("w_in", r_in)
    packed = [jnp.concatenate([given[nm][i] for nm in REPLICATED], axis=1).reshape(N_REPLICATED // 128, 128)
              for i in range(3)]
    rep_res = _adam(rep_contrib, packed[0], packed[1], packed[2], "adam_replicated")
    pos = 0
    for nm in REPLICATED:
        size = given[nm][0].shape[1]
        results[nm] = tuple(r.reshape(1, N_REPLICATED)[:, pos:pos + size] for r in rep_res)
        pos += size
    conv_wmv = [jnp.concatenate([given["ssm_conv_w"][i][0], given["gdn_conv_w"][i][0]], axis=0) for i in range(3)]
    conv_res = _adam(conv_contrib, conv_wmv[0], conv_wmv[1], conv_wmv[2], "adam_conv_w")
    results["ssm_conv_w"] = tuple(r[None, :CONV_K] for r in conv_res)
    results["gdn_conv_w"] = tuple(r[None, CONV_K:] for r in conv_res)

    loss = lax.psum(loss_loc[0, 0], ("x", "y", "c"))
    return (loss, dx[None]) + tuple(results[nm][i] for i in range(4) for nm in WEIGHTS)
```

```python
import jax
import jax.numpy as jnp
from jax import lax
from jax.experimental import pallas as pl
from jax.experimental.pallas import tpu as pltpu

F32 = jnp.float32
BF16 = jnp.bfloat16
N_DEV = 8
D_MODEL = 1024
EPS = 1e-6
CONV_K = 4
SSM_CHUNK = 128
SSM_HEAD_DIM = 64
SSM_D_STATE = 128
SSM_GROUPS = 8
SSM_HEADS_PER_GROUP = 4
SSM_GROUP_WIDTH = SSM_HEADS_PER_GROUP * SSM_HEAD_DIM
SSM_GROUPS_PER_STEP = 4
GDN_CHUNK = 64
GDN_HEAD = 128
GDN_QK_HEADS = 8
GDN_V_PER_QK = 2
GDN_QK_PER_STEP = 8
GDN_INV_BLOCK = 16
C_ZS, C_XBC, C_QKV, C_ZG, C_GS, C_GG, N_MAIN = 0, 2048, 6144, 10240, 12288, 13312, 14336
N_SMALL = 128
ADAM_LR, ADAM_B1, ADAM_B2, ADAM_EPS, ADAM_WD, ADAM_STEP = 0.001, 0.9, 0.999, 1e-08, 0.01, 10
VMEM_LIMIT = 56 * 1024 * 1024
MM_WHOLE_K = 4096
MM_SPLIT_K = 2048
NEG_INF = float("-inf")

_NT = (((1,), (1,)), ((), ()))
_NN = (((1,), (0,)), ((), ()))
_TN = (((0,), (0,)), ((), ()))


def _params(*sem):
    return pltpu.CompilerParams(dimension_semantics=sem, vmem_limit_bytes=VMEM_LIMIT)


def _dot(a, b, dims=_NN):
    return lax.dot_general(a.astype(BF16), b.astype(BF16), dims, preferred_element_type=F32)


def _split(a):
    hi = a.astype(BF16)
    return hi, (a - hi.astype(F32)).astype(BF16)


def _dot3(a, b, dims=_NN):
    ah, al = _split(a)
    bh, bl = _split(b)
    d = lambda u, v: lax.dot_general(u, v, dims, preferred_element_type=F32)
    return d(ah, bh) + (d(ah, bl) + d(al, bh))


def _dot2(a, b, dims=_NN):
    ah, al = _split(a)
    bb = b.astype(BF16)
    d = lambda u: lax.dot_general(u, bb, dims, preferred_element_type=F32)
    return d(ah) + d(al)


def _sigmoid(x):
    return 0.5 * jnp.tanh(0.5 * x) + 0.5


def _silu(x):
    return x * _sigmoid(x)


def _dsilu(x):
    s = _sigmoid(x)
    return s * (1.0 + x * (1.0 - s))


def _softplus(x):
    return jnp.maximum(x, 0.0) + jnp.log1p(jnp.exp(-jnp.abs(x)))


def _iota(n, m, d):
    return lax.broadcasted_iota(jnp.int32, (n, m), d)


def _rowsum(x):
    return jnp.sum(x, axis=1, keepdims=True)


def _colsum(x):
    return jnp.sum(x, axis=0, keepdims=True)


def _total(x):
    return _rowsum(_colsum(x))


MXU_LANES = 128


def _parts(x, n):
    out = []
    for _ in range(n):
        p = x.astype(BF16)
        out.append(p)
        x = x - p.astype(F32)
    return out


def _sum_by(m01, x, dims=_NN, n=3):
    return sum(lax.dot_general(m01, p, dims, preferred_element_type=F32) for p in _parts(x, n))


def _row_col_sums(q):
    ones = jnp.ones((q.shape[0], MXU_LANES), BF16)
    acc = 0.0
    for p in _parts(q, 2):
        acc = acc + (lax.dot_general(p, ones, _NN, preferred_element_type=F32)
                     - lax.dot_general(p, ones, _TN, preferred_element_type=F32))
    return acc[:, 0:1]


def _cumsum_forms(col, ii, jj):
    lower = jnp.where(ii >= jj, 1.0, 0.0).astype(BF16)
    cum_col = _sum_by(lower, jnp.broadcast_to(col, (col.shape[0], MXU_LANES)))[:, 0:1]
    cum_row = _colsum(jnp.where(ii <= jj, col, 0.0))
    return cum_col, cum_row


def _rev_cumsum_col(col, ii, jj):
    upper = jnp.where(ii <= jj, 1.0, 0.0).astype(BF16)
    return _sum_by(upper, jnp.broadcast_to(col, (col.shape[0], MXU_LANES)))[:, 0:1]


def _blk(dim, pref):
    return pref if dim % pref == 0 else dim


def _lockstep(gens):
    gens = list(gens)
    while gens:
        alive = []
        for g in gens:
            try:
                next(g)
                alive.append(g)
            except StopIteration:
                pass
        gens = alive


def _mm(a, b, M, N, K, *, mode, out_dtype, name, a_off=(0, 0), b_off=(0, 0), add=None, epi=None, extra=None,
        tm=1024, tn=1024):
    tm, tn = _blk(M, tm), _blk(N, tn)
    tk = K if K <= MM_WHOLE_K else _blk(K, MM_SPLIT_K)
    nk = K // tk
    if mode == "tn":
        a_spec = pl.BlockSpec((tk, tm), lambda i, j, k: (k + a_off[0] // tk, i + a_off[1] // tm))
        assert a_off[0] % tk == 0 and a_off[1] % tm == 0
    else:
        a_spec = pl.BlockSpec((tm, tk), lambda i, j, k: (i + a_off[0] // tm, k + a_off[1] // tk))
        assert a_off[0] % tm == 0 and a_off[1] % tk == 0
    if mode == "nt":
        b_spec = pl.BlockSpec((tn, tk), lambda i, j, k: (j + b_off[0] // tn, k + b_off[1] // tk))
        assert b_off[0] % tn == 0 and b_off[1] % tk == 0
    else:
        b_spec = pl.BlockSpec((tk, tn), lambda i, j, k: (k + b_off[0] // tk, j + b_off[1] // tn))
        assert b_off[0] % tk == 0 and b_off[1] % tn == 0
    dims = {"nn": _NN, "nt": _NT, "tn": _TN}[mode]
    o_spec = pl.BlockSpec((tm, tn), lambda i, j, k: (i, j))
    ins, in_specs = [a, b], [a_spec, b_spec]
    if add is not None:
        ins.append(add)
        in_specs.append(o_spec)
    if extra is not None:
        ins.append(extra)
        in_specs.append(o_spec)
    n_in = len(ins)
    if epi == "relu2":
        out_shape = (jax.ShapeDtypeStruct((M, N), BF16), jax.ShapeDtypeStruct((M, N), BF16))
        out_specs = (o_spec, o_spec)
    else:
        out_shape = jax.ShapeDtypeStruct((M, N), out_dtype)
        out_specs = o_spec

    def body(*refs):
        a_ref, b_ref = refs[0], refs[1]
        outs = refs[n_in:] if nk == 1 else refs[n_in:-1]

        def finish(r):
            pos = 2
            if add is not None:
                r = r + refs[pos][...]
                pos += 1
            if epi == "relu2":
                p = jnp.maximum(r, 0.0)
                outs[0][...] = p.astype(BF16)
                outs[1][...] = (p * p).astype(BF16)
            elif epi == "drelu2":
                outs[0][...] = (r * (2.0 * refs[pos][...].astype(F32))).astype(out_dtype)
            else:
                outs[0][...] = r.astype(out_dtype)

        if nk == 1:
            finish(_dot(a_ref[...], b_ref[...], dims))
            return
        acc = refs[-1]
        k = pl.program_id(2)

        @pl.when(k == 0)
        def _():
            acc[...] = jnp.zeros_like(acc)

        acc[...] += _dot(a_ref[...], b_ref[...], dims)

        @pl.when(k == nk - 1)
        def _():
            finish(acc[...])

    return pl.pallas_call(
        body, name=name, grid=(M // tm, N // tn, nk), in_specs=in_specs, out_specs=out_specs, out_shape=out_shape,
        scratch_shapes=[] if nk == 1 else [pltpu.VMEM((tm, tn), F32)],
        compiler_params=_params("parallel", "parallel", "arbitrary"))(*ins)


def _mm_rows(a, b, M, N, K, *, mode, name, extras, out_shapes, out_specs, epilogue, aliases=None, tm=512):
    tm = _blk(M, tm)
    a_spec = pl.BlockSpec((tm, K), lambda i: (i, 0))
    b_spec = pl.BlockSpec((K, N) if mode == "nn" else (N, K), lambda i: (0, 0))
    dims = _NN if mode == "nn" else _NT
    n_ex = len(extras)

    def body(a_ref, b_ref, *refs):
        epilogue(_dot(a_ref[...], b_ref[...], dims), refs[:n_ex], refs[n_ex:])

    return pl.pallas_call(
        body, name=name, grid=(M // tm,), in_specs=[a_spec, b_spec] + [sp for _, sp in extras],
        out_specs=tuple(out_specs), out_shape=tuple(out_shapes), input_output_aliases=aliases or {},
        compiler_params=_params("parallel"))(a, b, *[x for x, _ in extras])


def _row_spec(tb, d):
    return pl.BlockSpec((tb, d), lambda i: (i, 0))


def _vec_spec(d):
    return pl.BlockSpec((1, d), lambda i: (0, 0))


def _pre_fwd(x, w, sc, sh, name):
    S, Dm = x.shape
    tb = _blk(S, 512)

    def body(x_ref, w_ref, sc_ref, sh_ref, h_ref):
        xv = x_ref[...]
        r = lax.rsqrt(jnp.mean(xv * xv, axis=-1, keepdims=True) + EPS)
        h_ref[...] = ((xv * r * w_ref[...]) * (1.0 + sc_ref[...]) + sh_ref[...]).astype(BF16)

    return pl.pallas_call(
        body, name=name, grid=(S // tb,), in_specs=[_row_spec(tb, Dm)] + [_vec_spec(Dm)] * 3,
        out_specs=_row_spec(tb, Dm), out_shape=jax.ShapeDtypeStruct((S, Dm), BF16),
        compiler_params=_params("parallel"))(x, w, sc, sh)


def _final_fwd_bwd(x, y, w, g, target, name):
    S, Dm = x.shape
    tb = _blk(S, 512)
    nb = S // tb

    def body(x_ref, y_ref, w_ref, g_ref, t_ref, dx_ref, loss_ref, dy_ref, dg_ref, dw_ref, acc):
        i = pl.program_id(0)

        @pl.when(i == 0)
        def _():
            acc[...] = jnp.zeros_like(acc)
            dg_ref[...] = jnp.zeros_like(dg_ref)
            dw_ref[...] = jnp.zeros_like(dw_ref)

        yv = y_ref[...]
        r = lax.rsqrt(jnp.mean(yv * yv, axis=-1, keepdims=True) + EPS)
        yh = yv * r
        n = yh * w_ref[...]
        e = (x_ref[...] + g_ref[...] * n) - t_ref[...]
        dv = e * (1.0 / Dm)
        dx_ref[...] = dv
        acc[...] += _colsum(e * e)
        dg_ref[...] += _colsum(dv * n)
        dn = dv * g_ref[...]
        dw_ref[...] += _colsum(dn * yh)
        dyh = dn * w_ref[...]
        dy_ref[...] = (r * (dyh - yh * jnp.mean(dyh * yh, axis=-1, keepdims=True))).astype(BF16)

        @pl.when(i == nb - 1)
        def _():
            loss_ref[...] = (0.5 / Dm) * _rowsum(acc[...])

    row, vec = _row_spec(tb, Dm), _vec_spec(Dm)
    vec_shape = jax.ShapeDtypeStruct((1, Dm), F32)
    return pl.pallas_call(
        body, name=name, grid=(nb,), in_specs=[row, row, vec, vec, row],
        out_specs=(row, pl.BlockSpec((1, 1), lambda i: (0, 0)), row, vec, vec),
        out_shape=(jax.ShapeDtypeStruct((S, Dm), F32), jax.ShapeDtypeStruct((1, 1), F32),
                   jax.ShapeDtypeStruct((S, Dm), BF16), vec_shape, vec_shape),
        scratch_shapes=[pltpu.VMEM((1, Dm), F32)], compiler_params=_params("arbitrary"))(x, y, w, g, target)


def _post_bwd(dxo, y, w, g, name):
    S, Dm = y.shape
    tb = _blk(S, 512)

    def body(d_ref, y_ref, w_ref, g_ref, dy_ref, dg_ref, dw_ref):
        i = pl.program_id(0)

        @pl.when(i == 0)
        def _():
            dg_ref[...] = jnp.zeros_like(dg_ref)
            dw_ref[...] = jnp.zeros_like(dw_ref)

        yv, dv = y_ref[...], d_ref[...]
        r = lax.rsqrt(jnp.mean(yv * yv, axis=-1, keepdims=True) + EPS)
        yh = yv * r
        dg_ref[...] += _colsum(dv * (yh * w_ref[...]))
        dn = dv * g_ref[...]
        dw_ref[...] += _colsum(dn * yh)
        dyh = dn * w_ref[...]
        dy_ref[...] = (r * (dyh - yh * jnp.mean(dyh * yh, axis=-1, keepdims=True))).astype(BF16)

    return pl.pallas_call(
        body, name=name, grid=(S // tb,), in_specs=[_row_spec(tb, Dm)] * 2 + [_vec_spec(Dm)] * 2,
        out_specs=(_row_spec(tb, Dm), _vec_spec(Dm), _vec_spec(Dm)),
        out_shape=(jax.ShapeDtypeStruct((S, Dm), BF16), jax.ShapeDtypeStruct((1, Dm), F32),
                   jax.ShapeDtypeStruct((1, Dm), F32)),
        compiler_params=_params("arbitrary"))(dxo, y, w, g)


def _pre_bwd(dh, x, w, sc, dres, name):
    S, Dm = x.shape
    tb = _blk(S, 512)

    def body(dh_ref, x_ref, w_ref, sc_ref, dr_ref, dx_ref, dsh_ref, dsc_ref, dw_ref):
        i = pl.program_id(0)

        @pl.when(i == 0)
        def _():
            dsh_ref[...] = jnp.zeros_like(dsh_ref)
            dsc_ref[...] = jnp.zeros_like(dsc_ref)
            dw_ref[...] = jnp.zeros_like(dw_ref)

        xv, dv = x_ref[...], dh_ref[...]
        r = lax.rsqrt(jnp.mean(xv * xv, axis=-1, keepdims=True) + EPS)
        xh = xv * r
        one_sc = 1.0 + sc_ref[...]
        dsh_ref[...] += _colsum(dv)
        dsc_ref[...] += _colsum(dv * (xh * w_ref[...]))
        dw_ref[...] += _colsum(dv * one_sc * xh)
        dxh = dv * one_sc * w_ref[...]
        dx_ref[...] = dr_ref[...] + r * (dxh - xh * jnp.mean(dxh * xh, axis=-1, keepdims=True))

    vec = jax.ShapeDtypeStruct((1, Dm), F32)
    return pl.pallas_call(
        body, name=name, grid=(S // tb,),
        in_specs=[_row_spec(tb, Dm)] * 2 + [_vec_spec(Dm)] * 2 + [_row_spec(tb, Dm)],
        out_specs=(_row_spec(tb, Dm), _vec_spec(Dm), _vec_spec(Dm), _vec_spec(Dm)),
        out_shape=(jax.ShapeDtypeStruct((S, Dm), F32), vec, vec, vec),
        compiler_params=_params("arbitrary"))(dh, x, w, sc, dres)


D_PROJ_ANY = pl.BlockSpec(memory_space=pl.ANY)


def _gate_specs(tm, Dm):
    return (pl.BlockSpec((tm, Dm), lambda i: (i, C_GS // Dm)), pl.BlockSpec((tm, Dm), lambda i: (i, C_GG // Dm)))


def _gdn_up_merge(ogn, w_gu, ys, proj, name):
    S, K = ogn.shape
    Dm = ys.shape[1]
    tm = _blk(S, 512)
    row = _row_spec(tm, Dm)

    def epilogue(r, ex, out):
        ys_ref, gs_ref, gg_ref = ex
        out[0][...] = r
        out[1][...] = (_sigmoid(gs_ref[...].astype(F32)) * ys_ref[...]
                       + _sigmoid(gg_ref[...].astype(F32)) * r).astype(BF16)

    gs_spec, gg_spec = _gate_specs(tm, Dm)
    return _mm_rows(ogn, w_gu, S, Dm, K, mode="nn", name=name, tm=tm,
                    extras=[(ys, row), (proj, gs_spec), (proj, gg_spec)],
                    out_shapes=[jax.ShapeDtypeStruct((S, Dm), F32), jax.ShapeDtypeStruct((S, Dm), BF16)],
                    out_specs=[row, row], epilogue=epilogue)


def _mix_out_post_pre(merged, w_o, x, w_post, g, w_pre, sc, sh, name):
    S, Dm = x.shape
    tm = _blk(S, 512)
    row, vec = _row_spec(tm, Dm), _vec_spec(Dm)

    def epilogue(r, ex, out):
        x_ref, wpost_ref, g_ref, wpre_ref, sc_ref, sh_ref = ex
        out[0][...] = r
        rr = lax.rsqrt(jnp.mean(r * r, axis=-1, keepdims=True) + EPS)
        x1 = x_ref[...] + g_ref[...] * (r * rr * wpost_ref[...])
        out[1][...] = x1
        r1 = lax.rsqrt(jnp.mean(x1 * x1, axis=-1, keepdims=True) + EPS)
        out[2][...] = ((x1 * r1 * wpre_ref[...]) * (1.0 + sc_ref[...]) + sh_ref[...]).astype(BF16)

    return _mm_rows(merged, w_o, S, Dm, Dm, mode="nn", name=name, tm=tm,
                    extras=[(x, row), (w_post, vec), (g, vec), (w_pre, vec), (sc, vec), (sh, vec)],
                    out_shapes=[jax.ShapeDtypeStruct((S, Dm), F32), jax.ShapeDtypeStruct((S, Dm), F32),
                                jax.ShapeDtypeStruct((S, Dm), BF16)], out_specs=[row, row, row], epilogue=epilogue)


def _mix_out_dx_merge_bwd(dmo, w_o, ys, yg, proj, d_proj, name):
    S, Dm = ys.shape
    tm = _blk(S, 512)
    row = _row_spec(tm, Dm)

    def epilogue(d, ex, out):
        ys_ref, yg_ref, gs_ref, gg_ref, _ = ex
        ss, sg = _sigmoid(gs_ref[...].astype(F32)), _sigmoid(gg_ref[...].astype(F32))
        out[0][...] = (d * ss).astype(BF16)
        out[1][...] = (d * sg).astype(BF16)
        out[2][:, :Dm] = (d * ys_ref[...] * ss * (1.0 - ss)).astype(BF16)
        out[2][:, Dm:] = (d * yg_ref[...] * sg * (1.0 - sg)).astype(BF16)

    gs_spec, gg_spec = _gate_specs(tm, Dm)
    return _mm_rows(dmo, w_o, S, Dm, Dm, mode="nt", name=name, tm=tm,
                    extras=[(ys, row), (yg, row), (proj, gs_spec), (proj, gg_spec), (d_proj, D_PROJ_ANY)],
                    out_shapes=[jax.ShapeDtypeStruct((S, Dm), BF16), jax.ShapeDtypeStruct((S, Dm), BF16),
                                jax.ShapeDtypeStruct(d_proj.shape, BF16)],
                    out_specs=[row, row, pl.BlockSpec((tm, 2 * Dm), lambda i: (i, C_GS // (2 * Dm)))],
                    epilogue=epilogue, aliases={6: 2})


CONV_COLS = 128
CONV_BWD_ROWS = 128


def _taps_down(x):
    rows = _iota(x.shape[0], x.shape[1], 0)
    return [x] + [jnp.where(rows >= k, pltpu.roll(x, k, 0), 0.0) for k in range(1, CONV_K)]


def _conv_pre(taps, w_ref, b_ref):
    pre = taps[0] * w_ref[CONV_K - 1:CONV_K, :] + b_ref[...]
    for k in range(1, CONV_K):
        pre = pre + taps[k] * w_ref[CONV_K - 1 - k:CONV_K - k, :]
    return pre


def _conv_dx(dpre, w_ref):
    n = dpre.shape[0]
    rows = _iota(n, dpre.shape[1], 0)
    dx = dpre * w_ref[CONV_K - 1:CONV_K, :]
    for k in range(1, CONV_K):
        dx = dx + jnp.where(rows < n - k, pltpu.roll(dpre, n - k, 0), 0.0) * w_ref[CONV_K - 1 - k:CONV_K - k, :]
    return dx


def _conv_fwd(proj, w, b, name):
    S = proj.shape[0]
    n = w.shape[1]
    cb = CONV_COLS

    def body(x_ref, w_ref, b_ref, o_ref):
        o_ref[...] = _silu(_conv_pre(_taps_down(x_ref[...].astype(F32)), w_ref, b_ref)).astype(BF16)

    return pl.pallas_call(
        body, name=name, grid=(n // cb,),
        in_specs=[pl.BlockSpec((S, cb), lambda j: (0, j + C_XBC // cb)), pl.BlockSpec((CONV_K, cb), lambda j: (0, j)),
                  pl.BlockSpec((1, cb), lambda j: (0, j))],
        out_specs=pl.BlockSpec((S, cb), lambda j: (0, j)), out_shape=jax.ShapeDtypeStruct((S, n), BF16),
        compiler_params=_params("parallel"))(proj, w, b)


def _conv_bwd(dact, proj, w, b, col0, d_proj, name):
    S, n = dact.shape
    cb = CONV_COLS
    o = col0 // cb

    R, HALO = _blk(S, CONV_BWD_ROWS), 16
    n_chunks = S // R

    def body(d_ref, x_ref, w_ref, b_ref, _, dx_ref, dw_ref, db_ref):
        def chunk(r0, first, last, sums):
            lo, hi = (0 if first else HALO), (0 if last else HALO)
            start = r0 - lo if isinstance(r0, int) else pl.multiple_of(r0 - lo, HALO)
            xe = x_ref[pl.ds(start, lo + R + hi), :].astype(F32)
            rows = _iota(lo + R + hi, cb, 0)
            taps = [xe[lo:, :]]
            for k in range(1, CONV_K):
                t = pltpu.roll(xe, k, 0)
                taps.append((jnp.where(rows >= k, t, 0.0) if first else t)[lo:, :])
            dpre_e = d_ref[pl.ds(r0, R + hi), :].astype(F32) * _dsilu(_conv_pre(taps, w_ref, b_ref))
            dpre = dpre_e[0:R, :]
            db, dw = sums
            db = db + _colsum(dpre)
            dw = [dw[k] + _colsum(dpre * taps[k][0:R, :]) for k in range(CONV_K)]
            rows_e = _iota(R + hi, cb, 0)
            dx = dpre * w_ref[CONV_K - 1:CONV_K, :]
            for k in range(1, CONV_K):
                t = pltpu.roll(dpre_e, R + hi - k, 0)
                t = jnp.where(rows_e < R - k, t, 0.0) if last else t
                dx = dx + t[0:R, :] * w_ref[CONV_K - 1 - k:CONV_K - k, :]
            dx_ref[pl.ds(r0, R), :] = dx.astype(BF16)
            return db, dw

        zero = jnp.zeros((1, cb), F32)
        sums = chunk(0, True, n_chunks == 1, (zero, [zero] * CONV_K))
        if n_chunks > 2:
            def step(i, carry):
                db, dw = chunk(pl.multiple_of(i * R, R), False, False, (carry[0], list(carry[1:])))
                return (db,) + tuple(dw)
            carry = lax.fori_loop(1, n_chunks - 1, step, (sums[0],) + tuple(sums[1]))
            sums = (carry[0], list(carry[1:]))
        if n_chunks > 1:
            sums = chunk((n_chunks - 1) * R, False, True, sums)
        db_ref[...] = sums[0]
        for k in range(CONV_K):
            dw_ref[CONV_K - 1 - k:CONV_K - k, :] = sums[1][k]

    return pl.pallas_call(
        body, name=name, grid=(n // cb,),
        in_specs=[pl.BlockSpec((S, cb), lambda j: (0, j)), pl.BlockSpec((S, cb), lambda j: (0, j + o + C_XBC // cb)),
                  pl.BlockSpec((CONV_K, cb), lambda j: (0, j + o)), pl.BlockSpec((1, cb), lambda j: (0, j + o)),
                  D_PROJ_ANY],
        out_specs=(pl.BlockSpec((S, cb), lambda j: (0, j + o + C_XBC // cb)),
                   pl.BlockSpec((CONV_K, cb), lambda j: (0, j)), pl.BlockSpec((1, cb), lambda j: (0, j))),
        out_shape=(jax.ShapeDtypeStruct(d_proj.shape, BF16), jax.ShapeDtypeStruct((CONV_K, n), F32),
                   jax.ShapeDtypeStruct((1, n), F32)),
        input_output_aliases={4: 0}, compiler_params=_params("parallel"))(dact, proj, w, b, d_proj)


def _ssd_specs(L, order):
    G = SSM_GROUPS_PER_STEP
    W, N = G * SSM_GROUP_WIDTH, G * SSM_D_STATE
    x_spec = pl.BlockSpec((L, W), lambda g, c: (order(c), g))
    b_spec = pl.BlockSpec((L, N), lambda g, c: (order(c), 2048 // N + g))
    c_spec = pl.BlockSpec((L, N), lambda g, c: (order(c), 3072 // N + g))
    z_spec = pl.BlockSpec((L, W), lambda g, c: (order(c), C_ZS // W + g))
    dt_spec = pl.BlockSpec((G, L, SSM_HEADS_PER_GROUP), lambda g, c: (g, order(c), 0))
    p_spec = pl.BlockSpec((G, 3, SSM_HEADS_PER_GROUP), lambda g, c: (g, 0, 0))
    nw_spec = pl.BlockSpec((G, 1, SSM_GROUP_WIDTH), lambda g, c: (g, 0, 0))
    s_spec = pl.BlockSpec((G, 1, SSM_GROUP_WIDTH, SSM_D_STATE), lambda g, c: (g, order(c), 0, 0))
    return x_spec, b_spec, c_spec, z_spec, dt_spec, p_spec, nw_spec, s_spec


class _SsdGroup:
    def __init__(self, L):
        P, H, W = SSM_HEAD_DIM, SSM_HEADS_PER_GROUP, SSM_GROUP_WIDTH
        self.L = L
        self.ii, self.jj = _iota(L, L, 0), _iota(L, L, 1)
        self.lower = jnp.where(self.ii >= self.jj, 1.0, 0.0).astype(BF16)
        self.upper = jnp.where(self.ii <= self.jj, 1.0, 0.0).astype(BF16)
        self.lo = _iota(L, 2 * P, 1) < P
        self.lo_row = _iota(1, 2 * P, 1) < P
        bi, bj = _iota(W, W, 0), _iota(W, W, 1)
        self.block = jnp.where(bi // P == bj // P, 1.0, 0.0).astype(BF16)
        si, sj = _iota(2 * P, W, 0), _iota(2 * P, W, 1)
        self.pick = jnp.where(sj == si * P, 1.0, 0.0).astype(BF16)
        self.ones = jnp.ones((L, 2 * P), BF16)

    def spread(self, v4):
        R = v4.shape[0]
        lo = self.lo if R == self.L else self.lo_row
        b = lambda h: jnp.broadcast_to(v4[:, h:h + 1], (R, 2 * SSM_HEAD_DIM))
        return jnp.concatenate([jnp.where(lo, b(0), b(1)), jnp.where(lo, b(2), b(3))], axis=1)

    def gather4(self, v):
        return jnp.concatenate([v[:, h * SSM_HEAD_DIM:h * SSM_HEAD_DIM + 1] for h in range(SSM_HEADS_PER_GROUP)],
                               axis=1)

    def head_sums(self, z):
        return sum(lax.dot_general(p, self.block, _NN, preferred_element_type=F32) for p in _parts(z, 2))

    def pair_cols(self, full, pair):
        ps = full[:, pair * 128:(pair + 1) * 128]
        sw = pltpu.roll(ps, SSM_HEAD_DIM, 1)
        return jnp.where(self.lo, ps, sw), jnp.where(self.lo, sw, ps)

    def gates(self, dt4_raw, p):
        L = self.L
        dtr = self.spread(dt4_raw + p[0:1, :])
        dt = _softplus(dtr)
        A = self.spread(-jnp.exp(p[1:2, :]))
        acum = _sum_by(self.lower, dt * A)
        yield
        rows = _sum_by(self.pick, acum, _NT)
        yield
        a_last = acum[L - 1:L, :]
        cols = self.pair_cols(acum, 0) + self.pair_cols(acum, 1)
        decay, decay_t = [], []
        for h in range(SSM_HEADS_PER_GROUP):
            seg = cols[h] - rows[h:h + 1, :]
            decay.append(jnp.exp(jnp.where(self.ii >= self.jj, seg, NEG_INF)))
            decay_t.append(jnp.exp(jnp.where(self.jj >= self.ii, -seg, NEG_INF)))
        return dict(dtr=dtr, dt=dt, A=A, D=self.spread(p[2:3, :]), acum=acum, eac=jnp.exp(acum), a_last=a_last,
                    wdec=jnp.exp(a_last - acum), decay=decay, decay_t=decay_t,
                    ea_last=[jnp.exp(rows[h:h + 1, L - 1:L]) for h in range(SSM_HEADS_PER_GROUP)])


def _ssd_fwd(conv, proj, dt_raw, pvec, nw, name):
    S = conv.shape[0]
    L, P, N, H, W, G = SSM_CHUNK, SSM_HEAD_DIM, SSM_D_STATE, SSM_HEADS_PER_GROUP, SSM_GROUP_WIDTH, SSM_GROUPS_PER_STEP
    nc = S // L

    def body(x_ref, b_ref, c_ref, z_ref, dt_ref, p_ref, nw_ref, y_ref, yn_ref, s0_ref, state):
        c = pl.program_id(1)

        @pl.when(c == 0)
        def _():
            state[...] = jnp.zeros_like(state)

        k = _SsdGroup(L)

        def group(gi):
            gsl = slice(gi * W, (gi + 1) * W)
            Bm, Cm = b_ref[:, gi * N:(gi + 1) * N], c_ref[:, gi * N:(gi + 1) * N]
            x = x_ref[:, gsl].astype(F32)
            S0 = state[gsl, :]
            s0_ref[gi, 0] = S0
            CB = _dot(Cm, Bm, _NT)
            y_off = _dot(Cm, S0, _NT)
            t = yield from k.gates(dt_ref[gi], p_ref[gi])
            xdt = x * t["dt"]
            s_new = _dot(xdt * t["wdec"], Bm, _TN)
            y_diag = []
            for pair in range(H // 2):
                xp = xdt[:, pair * 128:(pair + 1) * 128]
                y_diag.append(jnp.where(k.lo, _dot(CB * t["decay"][2 * pair], xp),
                                        _dot(CB * t["decay"][2 * pair + 1], xp)))
            yield
            y = jnp.concatenate(y_diag, axis=1) + y_off * t["eac"]
            for h in range(H):
                hsl = slice(gi * W + h * P, gi * W + (h + 1) * P)
                state[hsl, :] = S0[h * P:(h + 1) * P, :] * t["ea_last"][h] + s_new[h * P:(h + 1) * P, :]
            y_ref[:, gsl] = y
            y2 = (y + t["D"] * x) * _silu(z_ref[:, gsl].astype(F32))
            r = lax.rsqrt(jnp.mean(y2 * y2, axis=-1, keepdims=True) + EPS)
            yn_ref[:, gsl] = (y2 * r * nw_ref[gi]).astype(BF16)

        _lockstep(group(gi) for gi in range(G))

    x_spec, b_spec, c_spec, z_spec, dt_spec, p_spec, nw_spec, s_spec = _ssd_specs(L, lambda c: c)
    y_spec = pl.BlockSpec((L, G * W), lambda g, c: (c, g))
    return pl.pallas_call(
        body, name=name, grid=(SSM_GROUPS // G, nc),
        in_specs=[x_spec, b_spec, c_spec, z_spec, dt_spec, p_spec, nw_spec],
        out_specs=(y_spec, y_spec, s_spec),
        out_shape=(jax.ShapeDtypeStruct((S, SSM_GROUPS * W), F32), jax.ShapeDtypeStruct((S, SSM_GROUPS * W), BF16),
                   jax.ShapeDtypeStruct((SSM_GROUPS, nc, W, N), F32)),
        scratch_shapes=[pltpu.VMEM((G * W, N), F32)],
        compiler_params=_params("parallel", "arbitrary"))(conv, conv, conv, proj, dt_raw, pvec, nw)


def _ssd_bwd(dyn, conv, proj, dt_raw, pvec, nw, y_ssd, states, d_proj, name):
    S = conv.shape[0]
    L, P, N, H, W, G = SSM_CHUNK, SSM_HEAD_DIM, SSM_D_STATE, SSM_HEADS_PER_GROUP, SSM_GROUP_WIDTH, SSM_GROUPS_PER_STEP
    nc = S // L

    def body(dyn_ref, x_ref, b_ref, c_ref, z_ref, dt_ref, p_ref, nw_ref, y_ref, s0_ref, _,
             dx_ref, db_ref, dc_ref, dz_ref, ddt_ref, dp_ref, dnw_ref, dstate):
        c = pl.program_id(1)

        @pl.when(c == 0)
        def _():
            dstate[...] = jnp.zeros_like(dstate)
            dp_ref[...] = jnp.zeros_like(dp_ref)
            dnw_ref[...] = jnp.zeros_like(dnw_ref)

        k = _SsdGroup(L)
        last = (_iota(L, 1, 0) == L - 1)

        def group(gi):
            gsl = slice(gi * W, (gi + 1) * W)
            Bm, Cm = b_ref[:, gi * N:(gi + 1) * N], c_ref[:, gi * N:(gi + 1) * N]
            x, z = x_ref[:, gsl].astype(F32), z_ref[:, gsl].astype(F32)
            S0, dS1 = s0_ref[gi, 0], dstate[gsl, :]
            CB = _dot(Cm, Bm, _NT)
            CBt = _dot(Bm, Cm, _NT)
            y_off_raw = _dot(Cm, S0, _NT)
            dXs_raw = _dot(Bm, dS1, _NT)
            t = yield from k.gates(dt_ref[gi], p_ref[gi])
            y1 = y_ref[:, gsl] + t["D"] * x
            sz = _silu(z)
            y2 = y1 * sz
            r = lax.rsqrt(jnp.mean(y2 * y2, axis=-1, keepdims=True) + EPS)
            y2h = y2 * r
            dyn_v = dyn_ref[:, gsl]
            dnw_ref[gi] += _colsum(dyn_v * y2h)
            dy2h = dyn_v * nw_ref[gi]
            dy2 = r * (dy2h - y2h * jnp.mean(dy2h * y2h, axis=-1, keepdims=True))
            dz_ref[:, gsl] = (dy2 * y1 * _dsilu(z)).astype(BF16)
            dY = dy2 * sz
            X = x * t["dt"]
            dYe = dY * t["eac"]
            dC_s = _dot(dYe, S0)
            dB_s = _dot(X * t["wdec"], dS1)
            dS_c = _dot(dYe, Cm, _TN)
            dXm, Gs, Gts = [], [], []
            for pair in range(H // 2):
                dYp, Xp = dY[:, pair * 128:(pair + 1) * 128], X[:, pair * 128:(pair + 1) * 128]
                dXm.append(jnp.where(k.lo, _dot(CBt * t["decay_t"][2 * pair], dYp),
                                     _dot(CBt * t["decay_t"][2 * pair + 1], dYp)))
                for mask in (k.lo, ~k.lo):
                    Gs.append(_dot(jnp.where(mask, dYp, 0.0), Xp, _NT))
                    Gts.append(_dot(jnp.where(mask, Xp, 0.0), dYp, _NT))
            yield
            dXs = dXs_raw * t["wdec"]
            dX = jnp.concatenate(dXm, axis=1) + dXs
            dCB, dCBt, q_sums = 0.0, 0.0, []
            for h in range(H):
                M, Mt = CB * t["decay"][h], CBt * t["decay_t"][h]
                dCB = dCB + Gs[h] * t["decay"][h]
                dCBt = dCBt + Gts[h] * t["decay_t"][h]
                d = Gs[h] * M - Gts[h] * Mt
                q_sums.append(sum(lax.dot_general(pt, k.ones, _NN, preferred_element_type=F32)
                                  for pt in _parts(d, 2)))
            q_f = jnp.concatenate([jnp.where(k.lo, q_sums[0], q_sums[1]), jnp.where(k.lo, q_sums[2], q_sums[3])],
                                  axis=1)
            x_dxs = k.head_sums(X * dXs)
            tot = [_total(dS1[h * P:(h + 1) * P, :] * S0[h * P:(h + 1) * P, :]) * t["ea_last"][h] for h in range(H)]
            tot_f = k.spread(jnp.concatenate(tot, axis=1))
            d_alast = _colsum(x_dxs) + tot_f
            dacum = q_f + k.head_sums(dY * (y_off_raw * t["eac"])) - x_dxs + jnp.where(last, d_alast, 0.0)
            dx_dt = k.head_sums(dX * x)
            d_skip = _colsum(k.head_sums(dY * x))
            for h in range(H):
                hsl = slice(gi * W + h * P, gi * W + (h + 1) * P)
                dstate[hsl, :] = t["ea_last"][h] * dS1[h * P:(h + 1) * P, :] + dS_c[h * P:(h + 1) * P, :]
            dc_s2 = _dot(dCB, Bm)
            db_s2 = _dot(dCBt, Cm)
            yield
            da = _sum_by(k.upper, dacum)
            yield
            ddt_raw = (da * t["A"] + dx_dt) * _sigmoid(t["dtr"])
            dx_ref[:, gsl] = (dX * t["dt"] + t["D"] * dY).astype(BF16)
            dc_ref[:, gi * N:(gi + 1) * N] = (dC_s + dc_s2).astype(BF16)
            db_ref[:, gi * N:(gi + 1) * N] = (dB_s + db_s2).astype(BF16)
            ddt_ref[gi] = k.gather4(ddt_raw)
            dp_ref[gi] += k.gather4(jnp.concatenate([_colsum(ddt_raw), _colsum(da * t["dt"]) * t["A"], d_skip],
                                                    axis=0))

        _lockstep(group(gi) for gi in range(G))

    rev = lambda c: nc - 1 - c
    x_spec, b_spec, c_spec, z_spec, dt_spec, p_spec, nw_spec, s_spec = _ssd_specs(L, rev)
    y_spec = pl.BlockSpec((L, G * W), lambda g, c: (rev(c), g))
    n_spec = pl.BlockSpec((L, G * N), lambda g, c: (rev(c), g))
    return pl.pallas_call(
        body, name=name, grid=(SSM_GROUPS // G, nc),
        in_specs=[y_spec, x_spec, b_spec, c_spec, z_spec, dt_spec, p_spec, nw_spec, y_spec, s_spec, D_PROJ_ANY],
        out_specs=(y_spec, n_spec, n_spec, z_spec, dt_spec, p_spec, nw_spec),
        out_shape=(jax.ShapeDtypeStruct((S, SSM_GROUPS * W), BF16), jax.ShapeDtypeStruct((S, SSM_GROUPS * N), BF16),
                   jax.ShapeDtypeStruct((S, SSM_GROUPS * N), BF16), jax.ShapeDtypeStruct(d_proj.shape, BF16),
                   jax.ShapeDtypeStruct((SSM_GROUPS, S, H), F32), jax.ShapeDtypeStruct((SSM_GROUPS, 3, H), F32),
                   jax.ShapeDtypeStruct((SSM_GROUPS, 1, W), F32)),
        scratch_shapes=[pltpu.VMEM((G * W, N), F32)], input_output_aliases={10: 3},
        compiler_params=_params("parallel", "arbitrary"))(dyn, conv, conv, conv, proj, dt_raw, pvec, nw, y_ssd, states,
                                                          d_proj)


def _unit_lower_inverse(A, ii, jj):
    eye = (ii == jj).astype(F32)
    same = (ii // GDN_INV_BLOCK) == (jj // GDN_INV_BLOCK)
    Ad = jnp.where(same, A, 0.0)
    Ao = A - Ad
    P2 = _dot3(Ad, Ad)
    yield
    P4, X = _dot(P2, P2), _dot3(eye - Ad, eye + P2)
    yield
    P8, X = _dot(P4, P4), X + _dot2(X, P4)
    yield
    X = X + _dot2(X, P8)
    yield
    Bm = _dot3(X, Ao)
    yield
    B2 = _dot3(Bm, Bm)
    yield
    Y = (eye - Bm) + B2 - _dot2(Bm, B2)
    yield
    T = _dot3(Y, X)
    yield
    return T


def _gdn_specs(L, order):
    G = GDN_QK_PER_STEP
    Hd, W = G * GDN_HEAD, G * GDN_V_PER_QK * GDN_HEAD
    q_spec = pl.BlockSpec((L, Hd), lambda h, c: (order(c), (C_QKV - C_XBC) // Hd + h))
    k_spec = pl.BlockSpec((L, Hd), lambda h, c: (order(c), (C_QKV - C_XBC + 1024) // Hd + h))
    v_spec = pl.BlockSpec((L, W), lambda h, c: (order(c), (C_QKV - C_XBC + 2048) // W + h))
    z_spec = pl.BlockSpec((L, W), lambda h, c: (order(c), C_ZG // W + h))
    ba_spec = pl.BlockSpec((G, L, GDN_V_PER_QK), lambda h, c: (h, order(c), 0))
    p_spec = pl.BlockSpec((G, 2, GDN_V_PER_QK), lambda h, c: (h, 0, 0))
    nw_spec = pl.BlockSpec((1, GDN_HEAD), lambda h, c: (0, 0))
    s_spec = pl.BlockSpec((G, 1, GDN_V_PER_QK * GDN_HEAD, GDN_HEAD), lambda h, c: (h, order(c), 0, 0))
    t_spec = pl.BlockSpec((G * GDN_V_PER_QK, 1, L, L), lambda h, c: (h, order(c), 0, 0))
    return q_spec, k_spec, v_spec, z_spec, ba_spec, p_spec, nw_spec, s_spec, t_spec


def _gdn_gates(qa, ka, b_col, a_col, p, j, ii, jj):
    L = qa.shape[0]
    sp_in = a_col + p[0:1, j:j + 1]
    neg_ea = -jnp.exp(p[1:2, j:j + 1])
    g = neg_ea * _softplus(sp_in)
    gcum, gcum_row = _cumsum_forms(g, ii, jj)
    rq = lax.rsqrt(_rowsum(qa * qa) + EPS)
    rk = lax.rsqrt(_rowsum(ka * ka) + EPS)
    q = qa * rq * (GDN_HEAD ** -0.5)
    k = ka * rk
    beta = _sigmoid(b_col)
    yield
    Dm = jnp.exp(jnp.where(ii >= jj, gcum - gcum_row, NEG_INF))
    eg = jnp.exp(gcum)
    g_last = gcum[L - 1:L, :]
    wdec = jnp.exp(g_last - gcum)
    return dict(rq=rq, rk=rk, q=q, k=k, beta=beta, sp_in=sp_in, neg_ea=neg_ea, g=g, Dm=Dm, kbeta=k * beta, eg=eg,
                g_last=g_last, wdec=wdec, kdec=k * wdec)


def _gdn_fwd(conv, proj, b_raw, a_raw, pvec, nw, name):
    S = conv.shape[0]
    L, Hd, J, G = GDN_CHUNK, GDN_HEAD, GDN_V_PER_QK, GDN_QK_PER_STEP
    W = J * Hd
    nc = S // L

    def body(q_ref, k_ref, v_ref, z_ref, b_ref, a_ref, p_ref, nw_ref, o_ref, on_ref, s0_ref, t_ref, state):
        c = pl.program_id(1)

        @pl.when(c == 0)
        def _():
            state[...] = jnp.zeros_like(state)

        ii, jj = _iota(L, L, 0), _iota(L, L, 1)
        for hq in range(G):
            s0_ref[hq, 0] = state[hq * W:(hq + 1) * W, :]

        def head(hq, j):
            hd = hq * J + j
            hsl, sl = slice(hq * Hd, (hq + 1) * Hd), slice(hd * Hd, (hd + 1) * Hd)
            t = yield from _gdn_gates(q_ref[:, hsl].astype(F32), k_ref[:, hsl].astype(F32), b_ref[hq][:, j:j + 1],
                                      a_ref[hq][:, j:j + 1],
                                      p_ref[hq], j, ii, jj)
            KK = _dot(t["kbeta"], t["k"], _NT)
            QK = _dot(t["q"], t["k"], _NT)
            yield
            T = yield from _unit_lower_inverse(jnp.where(ii > jj, KK * t["Dm"], 0.0), ii, jj)
            t_ref[hd, 0] = T
            S0 = state[sl, :]
            U = _dot2(T, v_ref[:, sl].astype(F32) * t["beta"])
            Wm = _dot2(T, t["kbeta"] * t["eg"])
            o_inter = _dot(t["q"] * t["eg"], S0)
            yield
            Vn = U - _dot(Wm, S0)
            yield
            o = o_inter + _dot(QK * t["Dm"], Vn)
            s_new = _dot(t["kdec"], Vn, _TN)
            yield
            state[sl, :] = S0 * jnp.exp(t["g_last"]) + s_new
            o_ref[:, sl] = o
            r = lax.rsqrt(jnp.mean(o * o, axis=-1, keepdims=True) + EPS)
            on_ref[:, sl] = ((o * r * nw_ref[...]) * _silu(z_ref[:, sl].astype(F32))).astype(BF16)

        _lockstep(head(hq, j) for hq in range(G) for j in range(J))

    q_spec, k_spec, v_spec, z_spec, ba_spec, p_spec, nw_spec, s_spec, t_spec = _gdn_specs(L, lambda c: c)
    o_spec = pl.BlockSpec((L, G * W), lambda h, c: (c, h))
    return pl.pallas_call(
        body, name=name, grid=(GDN_QK_HEADS // G, nc),
        in_specs=[q_spec, k_spec, v_spec, z_spec, ba_spec, ba_spec, p_spec, nw_spec],
        out_specs=(o_spec, o_spec, s_spec, t_spec),
        out_shape=(jax.ShapeDtypeStruct((S, GDN_QK_HEADS * W), F32), jax.ShapeDtypeStruct((S, GDN_QK_HEADS * W), BF16),
                   jax.ShapeDtypeStruct((GDN_QK_HEADS, nc, W, Hd), F32),
                   jax.ShapeDtypeStruct((GDN_QK_HEADS * J, nc, L, L), F32)),
        scratch_shapes=[pltpu.VMEM((G * W, Hd), F32)],
        compiler_params=_params("parallel", "arbitrary"))(conv, conv, conv, proj, b_raw, a_raw, pvec, nw)


def _gdn_bwd(don, conv, proj, b_raw, a_raw, pvec, nw, o_pre, states, t_inv, d_proj, name):
    S = conv.shape[0]
    L, Hd, J, G = GDN_CHUNK, GDN_HEAD, GDN_V_PER_QK, GDN_QK_PER_STEP
    W = J * Hd
    nc = S // L

    def body(don_ref, q_ref, k_ref, v_ref, z_ref, b_ref, a_ref, p_ref, nw_ref, o_ref, s0_ref, t_ref, _,
             dq_ref, dk_ref, dv_ref, dz_ref, db_ref, da_ref, dp_ref, dnw_ref, dstate):
        c = pl.program_id(1)

        @pl.when(c == 0)
        def _():
            dstate[...] = jnp.zeros_like(dstate)
            dp_ref[...] = jnp.zeros_like(dp_ref)
            dnw_ref[...] = jnp.zeros_like(dnw_ref)

        ii, jj = _iota(L, L, 0), _iota(L, L, 1)
        last = (_iota(L, 1, 0) == L - 1)
        res = {}

        def head(hq, j):
            hd = hq * J + j
            hsl, sl = slice(hq * Hd, (hq + 1) * Hd), slice(hd * Hd, (hd + 1) * Hd)
            qa, ka = q_ref[:, hsl].astype(F32), k_ref[:, hsl].astype(F32)
            t = yield from _gdn_gates(qa, ka, b_ref[hq][:, j:j + 1], a_ref[hq][:, j:j + 1], p_ref[hq], j, ii, jj)
            q, k, beta, eg, Dm, kbeta, kdec = (t[nm] for nm in ("q", "k", "beta", "eg", "Dm", "kbeta", "kdec"))
            T = t_ref[hd, 0]
            v, z, o = v_ref[:, sl].astype(F32), z_ref[:, sl].astype(F32), o_ref[:, sl]
            S0, dS1 = s0_ref[hq, 0, j * Hd:(j + 1) * Hd, :], dstate[sl, :]
            sz = _silu(z)
            r = lax.rsqrt(jnp.mean(o * o, axis=-1, keepdims=True) + EPS)
            oh = o * r
            d_on = don_ref[:, sl]
            dz_ref[:, sl] = (d_on * (oh * nw_ref[...]) * _dsilu(z)).astype(BF16)
            dn = d_on * sz
            dnw_part = _colsum(dn * oh)
            doh = dn * nw_ref[...]
            dO = r * (doh - oh * jnp.mean(doh * oh, axis=-1, keepdims=True))
            Rw = kbeta * eg
            qe = q * eg
            U = _dot2(T, v * beta)
            Wm = _dot2(T, Rw)
            KK = _dot(kbeta, k, _NT)
            QK = _dot(q, k, _NT)
            o_inter = _dot(qe, S0)
            dq_s = _dot(dO, S0, _NT)
            dS_q = _dot(qe, dO, _TN)
            yield
            Am = jnp.where(ii > jj, KK * Dm, 0.0)
            Pm = QK * Dm
            Vn = U - _dot(Wm, S0)
            dVn_s = _dot(kdec, dS1)
            yield
            dVn = _dot(Pm, dO, _TN) + dVn_s
            dP = _dot(dO, Vn, _NT)
            dKd = _dot(Vn, dS1, _NT)
            yield
            dQK = dP * Dm
            dq = _dot(dQK, k) + dq_s * eg
            dk = _dot(dQK, q, _TN) + dKd * t["wdec"]
            dstate[sl, :] = jnp.exp(t["g_last"]) * dS1 + dS_q - _dot(Wm, dVn, _TN)
            dW = -_dot(dVn, S0, _NT)
            dRu = _dot2(T, dVn, _TN)
            yield
            dRw = _dot2(T, dW, _TN)
            dA_u = _dot(dRu, U, _NT)
            yield
            dA = jnp.where(ii > jj, -(dA_u + _dot(dRw, Wm, _NT)), 0.0)
            yield
            dKK = dA * Dm
            dkbeta = _dot(dKK, k) + dRw * eg
            dk = dk + _dot(dKK, kbeta, _TN)
            yield
            dk = dk + dkbeta * beta
            dbeta = _rowsum(dkbeta * k) + _rowsum(dRu * v)
            dv_ref[:, sl] = (dRu * beta).astype(BF16)
            Q = dA * Am + dP * Pm
            rho = _rowsum(dKd * kdec)
            d_glast = _colsum(rho) + jnp.exp(t["g_last"]) * _total(dS1 * S0)
            q_sums = _row_col_sums(Q)
            rest = _rowsum(dRw * Rw) + _rowsum(dO * o_inter) - rho + jnp.where(last, d_glast, 0.0)
            yield
            dg = _rev_cumsum_col(q_sums + rest, ii, jj)
            yield
            da_raw = dg * t["neg_ea"] * _sigmoid(t["sp_in"])
            res[hq, j] = dict(dq=dq, dk=dk, db=dbeta * beta * (1.0 - beta), da=da_raw, d_bias=_colsum(da_raw),
                              d_alog=_colsum(dg * t["g"]), dnw=dnw_part, rq=t["rq"], rk=t["rk"], k=k, qh=qa * t["rq"])

        _lockstep(head(hq, j) for hq in range(G) for j in range(J))
        for hq in range(G):
            parts = [res[hq, j] for j in range(J)]
            hsl = slice(hq * Hd, (hq + 1) * Hd)
            p0 = parts[0]
            dqh = sum(pt["dq"] for pt in parts) * (GDN_HEAD ** -0.5)
            dkn = sum(pt["dk"] for pt in parts)
            dq_ref[:, hsl] = (p0["rq"] * (dqh - p0["qh"] * _rowsum(dqh * p0["qh"]))).astype(BF16)
            dk_ref[:, hsl] = (p0["rk"] * (dkn - p0["k"] * _rowsum(dkn * p0["k"]))).astype(BF16)
            db_ref[hq] = jnp.concatenate([pt["db"] for pt in parts], axis=1)
            da_ref[hq] = jnp.concatenate([pt["da"] for pt in parts], axis=1)
            dp_ref[hq] += jnp.concatenate([jnp.concatenate([pt["d_bias"] for pt in parts], axis=1),
                                           jnp.concatenate([pt["d_alog"] for pt in parts], axis=1)], axis=0)
            dnw_ref[hq] += sum(pt["dnw"] for pt in parts)

    rev = lambda c: nc - 1 - c
    q_spec, k_spec, v_spec, z_spec, ba_spec, p_spec, nw_spec, s_spec, t_spec = _gdn_specs(L, rev)
    o_spec = pl.BlockSpec((L, G * W), lambda h, c: (rev(c), h))
    h_spec = pl.BlockSpec((L, G * Hd), lambda h, c: (rev(c), h))
    dnw_spec = pl.BlockSpec((G, 1, Hd), lambda h, c: (h, 0, 0))
    return pl.pallas_call(
        body, name=name, grid=(GDN_QK_HEADS // G, nc),
        in_specs=[o_spec, q_spec, k_spec, v_spec, z_spec, ba_spec, ba_spec, p_spec, nw_spec, o_spec, s_spec, t_spec,
                  D_PROJ_ANY],
        out_specs=(h_spec, h_spec, o_spec, z_spec, ba_spec, ba_spec, p_spec, dnw_spec),
        out_shape=(jax.ShapeDtypeStruct((S, GDN_QK_HEADS * Hd), BF16), jax.ShapeDtypeStruct((S, GDN_QK_HEADS * Hd), BF16),
                   jax.ShapeDtypeStruct((S, GDN_QK_HEADS * W), BF16), jax.ShapeDtypeStruct(d_proj.shape, BF16),
                   jax.ShapeDtypeStruct((GDN_QK_HEADS, S, J), F32), jax.ShapeDtypeStruct((GDN_QK_HEADS, S, J), F32),
                   jax.ShapeDtypeStruct((GDN_QK_HEADS, 2, J), F32), jax.ShapeDtypeStruct((GDN_QK_HEADS, 1, Hd), F32)),
        scratch_shapes=[pltpu.VMEM((G * W, Hd), F32)], input_output_aliases={12: 3},
        compiler_params=_params("parallel", "arbitrary"))(don, conv, conv, conv, proj, b_raw, a_raw, pvec, nw, o_pre,
                                                          states, t_inv, d_proj)


def _ada_fwd(c_all, w_loc, b_loc, name):
    n = w_loc.shape[1]

    def body(c_ref, w_ref, b_ref, o_ref):
        o_ref[...] = _dot3(_silu(c_ref[...]), w_ref[...]) + b_ref[...]

    return pl.pallas_call(body, name=name, out_shape=jax.ShapeDtypeStruct((N_DEV, n), F32),
                          compiler_params=pltpu.CompilerParams(vmem_limit_bytes=VMEM_LIMIT))(c_all, w_loc, b_loc)


def _ada_bwd(c_all_t, dmod_cols, name):
    Dm, n = c_all_t.shape[0], dmod_cols.shape[1]

    def body(c_ref, d_ref, o_ref):
        ca = _silu(c_ref[...])
        acc = ca[:, 0:1] * d_ref[0:1, :]
        for i in range(1, N_DEV):
            acc = acc + ca[:, i:i + 1] * d_ref[i:i + 1, :]
        o_ref[...] = acc

    return pl.pallas_call(body, name=name, out_shape=jax.ShapeDtypeStruct((Dm, n), F32),
                          compiler_params=pltpu.CompilerParams(vmem_limit_bytes=VMEM_LIMIT))(c_all_t, dmod_cols)


ADAM_BLOCK_BYTES = 12 * 1024 * 1024


def _adam(contrib, w, m, v, name):
    n, R, C = contrib.shape
    tr = R
    while tr % 16 == 0 and (n + 7) * tr * C * 4 > ADAM_BLOCK_BYTES:
        tr //= 2

    def body(c_ref, w_ref, m_ref, v_ref, g_ref, d_ref, nm_ref, nv_ref):
        g = c_ref[0].astype(F32)
        for i in range(1, n):
            g = g + c_ref[i].astype(F32)
        nm = ADAM_B1 * m_ref[...] + (1.0 - ADAM_B1) * g
        nv = ADAM_B2 * v_ref[...] + (1.0 - ADAM_B2) * (g * g)
        m_hat = nm / (1.0 - ADAM_B1 ** ADAM_STEP)
        v_hat = nv / (1.0 - ADAM_B2 ** ADAM_STEP)
        g_ref[...] = g
        d_ref[...] = -ADAM_LR * (m_hat / (jnp.sqrt(v_hat) + ADAM_EPS) + ADAM_WD * w_ref[...])
        nm_ref[...] = nm
        nv_ref[...] = nv

    spec = pl.BlockSpec((tr, C), lambda i: (i, 0))
    shp = jax.ShapeDtypeStruct((R, C), F32)
    return pl.pallas_call(
        body, name=name, grid=(R // tr,), in_specs=[pl.BlockSpec((n, tr, C), lambda i: (0, i, 0)), spec, spec, spec],
        out_specs=(spec,) * 4, out_shape=(shp,) * 4, compiler_params=_params("parallel"))(contrib, w, m, v)


def _exchange(arrays, modes, name, chips=False):
    n = len(arrays)
    out_shape = tuple(jax.ShapeDtypeStruct((N_DEV,) + a.shape if md == "gather" else a.shape, a.dtype)
                      for a, md in zip(arrays, modes))

    def body(*refs):
        ins, outs = refs[:n], refs[n:2 * n]
        send_sems, recv_sems, loc_sems = refs[2 * n:]
        me, peers = _peer_table(chips)

        def src(k, slot):
            return ins[k] if modes[k] == "gather" else ins[k].at[slot]

        def remote(k, m, to_slot, land_slot):
            return pltpu.make_async_remote_copy(
                src_ref=src(k, to_slot), dst_ref=outs[k].at[land_slot], send_sem=send_sems.at[k, m],
                recv_sem=recv_sems.at[k, m], device_id=peers[m][0], device_id_type=pl.DeviceIdType.MESH)

        local = [pltpu.make_async_copy(src(k, me), outs[k].at[me], loc_sems.at[k]) for k in range(n)]
        for cp in local:
            cp.start()
        sends = [remote(k, m, peers[m][1], me) for m in range(len(peers)) for k in range(n)]
        for cp in sends:
            cp.start()
        for m in range(len(peers)):
            for k in range(n):
                remote(k, m, peers[m][1], peers[m][1]).wait_recv()
        for cp in sends:
            cp.wait_send()
        for cp in local:
            cp.wait()

    any_spec = pl.BlockSpec(memory_space=pl.ANY)
    return pl.pallas_call(
        body, name=name, in_specs=[any_spec] * n, out_specs=(any_spec,) * n, out_shape=out_shape,
        scratch_shapes=[pltpu.SemaphoreType.DMA((n, N_DEV - 1)), pltpu.SemaphoreType.DMA((n, N_DEV - 1)),
                        pltpu.SemaphoreType.DMA((n,))])(*arrays)


def _gather_two_level(arrays, name):
    n = len(arrays)
    out_shape = tuple(jax.ShapeDtypeStruct((N_DEV,) + a.shape, a.dtype) for a in arrays)

    def body(*refs):
        ins, outs = refs[:n], refs[n:2 * n]
        send_sems, recv_sems, loc_sems = refs[2 * n:]
        ix, iy, ic = lax.axis_index("x"), lax.axis_index("y"), lax.axis_index("c")
        lin = lambda px, py, pc: 4 * px + 2 * py + pc
        me, sib = lin(ix, iy, ic), (ix, iy, 1 - ic)
        chips = [(1 - ix, iy), (ix, 1 - iy), (1 - ix, 1 - iy)]

        def copy(k, s, block, to, src=None):
            return pltpu.make_async_remote_copy(
                src_ref=outs[k].at[block] if src is None else src, dst_ref=outs[k].at[block],
                send_sem=send_sems.at[k, s], recv_sem=recv_sems.at[k, s], device_id=to,
                device_id_type=pl.DeviceIdType.MESH)

        local = [pltpu.make_async_copy(ins[k], outs[k].at[me], loc_sems.at[k]) for k in range(n)]
        for cp in local:
            cp.start()
        first = [copy(k, 1 + j, me, (cx, cy, ic), src=ins[k]) for j, (cx, cy) in enumerate(chips) for k in range(n)]
        first += [copy(k, 0, me, sib, src=ins[k]) for k in range(n)]
        for cp in first:
            cp.start()
        passed = []
        for j, (cx, cy) in enumerate(chips):
            for k in range(n):
                copy(k, 1 + j, lin(cx, cy, ic), sib).wait_recv()
                passed.append(copy(k, 4 + j, lin(cx, cy, ic), sib))
                passed[-1].start()
        for k in range(n):
            copy(k, 0, lin(*sib), sib).wait_recv()
            for j, (cx, cy) in enumerate(chips):
                copy(k, 4 + j, lin(cx, cy, 1 - ic), sib).wait_recv()
        for cp in first + passed:
            cp.wait_send()
        for cp in local:
            cp.wait()

    any_spec = pl.BlockSpec(memory_space=pl.ANY)
    return pl.pallas_call(
        body, name=name, in_specs=[any_spec] * n, out_specs=(any_spec,) * n, out_shape=out_shape,
        scratch_shapes=[pltpu.SemaphoreType.DMA((n, N_DEV - 1)), pltpu.SemaphoreType.DMA((n, N_DEV - 1)),
                        pltpu.SemaphoreType.DMA((n,))])(*arrays)


def _peer_table(chips=False):
    ix, iy, ic = lax.axis_index("x"), lax.axis_index("y"), lax.axis_index("c")
    peers = []
    for m in ((2, 4, 6) if chips else range(1, N_DEV)):
        px = 1 - ix if m & 4 else ix
        py = 1 - iy if m & 2 else iy
        pc = 1 - ic if m & 1 else ic
        peers.append(((px, py, pc), 2 * px + py if chips else 4 * px + 2 * py + pc))
    return (2 * ix + iy if chips else 4 * ix + 2 * iy + ic), peers


def _exchange_start(arrays, modes, after, name, chips=False):
    n = len(arrays)
    land_shapes = [(N_DEV,) + a.shape if md == "gather" else a.shape for a, md in zip(arrays, modes)]

    def body(*refs):
        ins, lands = refs[:n], refs[n:2 * n]
        send_sems, recv_sems = refs[2 * n + 1], refs[2 * n + 2]
        token = refs[-1]
        me, peers = _peer_table(chips)

        def src(k, slot):
            return ins[k] if modes[k] == "gather" else ins[k].at[slot]

        for peer, slot in peers:
            for k in range(n):
                pltpu.make_async_remote_copy(
                    src_ref=src(k, slot), dst_ref=lands[k].at[me], send_sem=send_sems, recv_sem=recv_sems,
                    device_id=peer, device_id_type=pl.DeviceIdType.MESH).start()
        token[...] = jnp.zeros_like(token)

    hbm = pl.BlockSpec(memory_space=pltpu.HBM)
    sem = pl.BlockSpec(memory_space=pltpu.SEMAPHORE)
    sem_shape = pltpu.SemaphoreType.DMA(())
    operands = [pltpu.with_memory_space_constraint(a, pltpu.HBM) for a in arrays]
    operands += [pltpu.with_memory_space_constraint(lax.empty(s, a.dtype), pltpu.HBM)
                 for s, a in zip(land_shapes, arrays)]
    out = pl.pallas_call(
        body, name=name,
        out_shape=(sem_shape, sem_shape) + tuple(pltpu.HBM(a.shape, a.dtype) for a in arrays)
        + tuple(pltpu.HBM(s, a.dtype) for s, a in zip(land_shapes, arrays)) + (jax.ShapeDtypeStruct((8, 128), F32),),
        in_specs=[hbm] * (2 * n) + [pl.BlockSpec(memory_space=pl.ANY)],
        out_specs=(sem, sem) + (hbm,) * (2 * n) + (pl.BlockSpec(memory_space=pltpu.VMEM),),
        input_output_aliases={i: 2 + i for i in range(2 * n)},
        compiler_params=pltpu.CompilerParams(has_side_effects=pltpu.SideEffectType.DATAFLOW_SIDE_EFFECTING))(
            *operands, after)
    return out[0], out[1], out[2:2 + n], out[2 + n:2 + 2 * n], out[-1]


def _exchange_wait(started, modes, after, name, chips=False):
    send_sems, recv_sems, sent, lands, _ = started
    n = len(sent)

    def body(*refs):
        ins, zones = refs[:n], refs[n:2 * n]
        send_ref, recv_ref = refs[2 * n], refs[2 * n + 1]
        _, peers = _peer_table(chips)

        def src(k, slot):
            return ins[k] if modes[k] == "gather" else ins[k].at[slot]

        for peer, slot in peers:
            for k in range(n):
                cp = pltpu.make_async_remote_copy(
                    src_ref=src(k, slot), dst_ref=zones[k].at[slot], send_sem=send_ref, recv_sem=recv_ref,
                    device_id=peer, device_id_type=pl.DeviceIdType.MESH)
                cp.wait_send()
                cp.wait_recv()

    hbm = pl.BlockSpec(memory_space=pltpu.HBM)
    sem = pl.BlockSpec(memory_space=pltpu.SEMAPHORE)
    out = pl.pallas_call(
        body, name=name,
        out_shape=tuple(pltpu.HBM(a.shape, a.dtype) for a in sent) + tuple(pltpu.HBM(a.shape, a.dtype) for a in lands),
        in_specs=[hbm] * (2 * n) + [sem, sem, pl.BlockSpec(memory_space=pl.ANY)], out_specs=(hbm,) * (2 * n),
        input_output_aliases={i: i for i in range(2 * n)},
        compiler_params=pltpu.CompilerParams(has_side_effects=pltpu.SideEffectType.DATAFLOW_SIDE_EFFECTING))(
            *sent, *lands, send_sems, recv_sems, after)
    ix, iy, ic = lax.axis_index("x"), lax.axis_index("y"), lax.axis_index("c")
    me = 2 * ix + iy if chips else 4 * ix + 2 * iy + ic
    filled = []
    for k in range(n):
        own = sent[k] if modes[k] == "gather" else lax.dynamic_index_in_dim(sent[k], me, axis=0, keepdims=False)
        filled.append(lax.dynamic_update_index_in_dim(out[n + k], own, me, axis=0))
    return filled


def _swap_sibling(to_c0, to_c1, name):
    def body(c0_ref, c1_ref, out_ref, send_sem, recv_sem):
        ix, iy, ic = lax.axis_index("x"), lax.axis_index("y"), lax.axis_index("c")

        def copy(src):
            return pltpu.make_async_remote_copy(src_ref=src, dst_ref=out_ref, send_sem=send_sem, recv_sem=recv_sem,
                                                device_id=(ix, iy, 1 - ic), device_id_type=pl.DeviceIdType.MESH)

        @pl.when(ic == 0)
        def _():
            copy(c1_ref).start()

        @pl.when(ic == 1)
        def _():
            copy(c0_ref).start()

        copy(c0_ref).wait()

    any_spec = pl.BlockSpec(memory_space=pl.ANY)
    return pl.pallas_call(body, name=name, in_specs=[any_spec, any_spec], out_specs=any_spec,
                          out_shape=jax.ShapeDtypeStruct(to_c0.shape, to_c0.dtype),
                          scratch_shapes=[pltpu.SemaphoreType.DMA, pltpu.SemaphoreType.DMA])(to_c0, to_c1)


def _add_pair(to_c0, to_c1, got, name):
    n, R, C = got.shape
    tr = _blk(R, 256)

    def body(c0_ref, c1_ref, got_ref, o_ref):
        ic = lax.axis_index("c")

        @pl.when(ic == 0)
        def _():
            o_ref[...] = (c0_ref[...].astype(F32) + got_ref[...].astype(F32)).astype(o_ref.dtype)

        @pl.when(ic == 1)
        def _():
            o_ref[...] = (c1_ref[...].astype(F32) + got_ref[...].astype(F32)).astype(o_ref.dtype)

    spec = pl.BlockSpec((1, tr, C), lambda i, j: (i, j, 0))
    return pl.pallas_call(body, name=name, grid=(n, R // tr), in_specs=[spec, spec, spec], out_specs=spec,
                          out_shape=jax.ShapeDtypeStruct(got.shape, got.dtype),
                          compiler_params=_params("parallel", "parallel"))(to_c0, to_c1, got)


W_IN_SPLITS = (0, 2048, 6144, 6176, 10272, 12320, 12336, 12352, 13376, 14400)
N_REPLICATED = 16640
REPLICATED = ("b_ada", "norm_mix_pre", "norm_mix_post", "ssm_conv_b", "ssm_dt_bias", "ssm_A_log", "ssm_D",
              "ssm_norm_w", "gdn_dt_bias", "gdn_A_log", "gdn_norm_w", "norm_mlp_pre", "norm_mlp_post")
WEIGHTS = ("w_ada", "b_ada", "norm_mix_pre", "norm_mix_post", "w_in", "ssm_conv_w", "ssm_conv_b", "ssm_dt_bias",
           "ssm_A_log", "ssm_D", "ssm_norm_w", "gdn_conv_w", "gdn_dt_bias", "gdn_A_log", "gdn_norm_w", "w_ssm_up",
           "w_gdn_up", "w_out", "norm_mlp_pre", "norm_mlp_post", "w_mlp_up", "w_mlp_down")


def _cols_of_shards(g, a, b):
    width, pieces = g.shape[2], []
    while a < b:
        i = a // width
        hi = min(b, (i + 1) * width)
        pieces.append(g[i][:, a - i * width:hi - i * width])
        a = hi
    return pieces


ORIG_SEGMENTS = ((0, 6144, "main", 0), (6144, 6176, "small", 0), (6176, 12320, "main", 6144),
                 (12320, 12352, "small", 32), (12352, 14400, "main", 12288))


def _orig_cols(main_cols, small_cols, a, b):
    pieces = []
    for s0, s1, which, off in ORIG_SEGMENTS:
        lo, hi = max(a, s0), min(b, s1)
        if lo < hi:
            pieces.append((main_cols if which == "main" else small_cols)[:, off + lo - s0:off + hi - s0])
    return jnp.concatenate(pieces, axis=1)


def _by_cols(t):
    return t.transpose(1, 0, 2).reshape(t.shape[1], N_DEV * t.shape[2])


def _to_col_shards(t):
    R, C8 = t.shape
    return t.reshape(R, N_DEV, C8 // N_DEV).transpose(1, 0, 2)


def _heads_first(t, groups):
    S = t.shape[0]
    return t.reshape(S, groups, t.shape[1] // groups).transpose(1, 0, 2)


def _heads_last(t):
    return t.transpose(1, 0, 2).reshape(t.shape[1], t.shape[0] * t.shape[2])


def kernel(x, c, w_ada, b_ada, norm_mix_pre, norm_mix_post, w_in, ssm_conv_w, ssm_conv_b, ssm_dt_bias, ssm_A_log, ssm_D, ssm_norm_w, gdn_conv_w, gdn_dt_bias, gdn_A_log, gdn_norm_w, w_ssm_up, w_gdn_up, w_out, norm_mlp_pre, norm_mlp_post, w_mlp_up, w_mlp_down, loss_target, m_w_ada, m_b_ada, m_norm_mix_pre, m_norm_mix_post, m_w_in, m_ssm_conv_w, m_ssm_conv_b, m_ssm_dt_bias, m_ssm_A_log, m_ssm_D, m_ssm_norm_w, m_gdn_conv_w, m_gdn_dt_bias, m_gdn_A_log, m_gdn_norm_w, m_w_ssm_up, m_w_gdn_up, m_w_out, m_norm_mlp_pre, m_norm_mlp_post, m_w_mlp_up, m_w_mlp_down, v_w_ada, v_b_ada, v_norm_mix_pre, v_norm_mix_post, v_w_in, v_ssm_conv_w, v_ssm_conv_b, v_ssm_dt_bias, v_ssm_A_log, v_ssm_D, v_ssm_norm_w, v_gdn_conv_w, v_gdn_dt_bias, v_gdn_A_log, v_gdn_norm_w, v_w_ssm_up, v_w_gdn_up, v_w_out, v_norm_mlp_pre, v_norm_mlp_post, v_w_mlp_up, v_w_mlp_down):
    S, Dm = x.shape[1], D_MODEL
    me = 4 * lax.axis_index("x") + 2 * lax.axis_index("y") + lax.axis_index("c")
    x2, tgt = x[0], loss_target[0]
    n_ada = w_ada.shape[2]
    given = dict(
        w_ada=(w_ada, m_w_ada, v_w_ada), b_ada=(b_ada, m_b_ada, v_b_ada),
        norm_mix_pre=(norm_mix_pre, m_norm_mix_pre, v_norm_mix_pre),
        norm_mix_post=(norm_mix_post, m_norm_mix_post, v_norm_mix_post), w_in=(w_in, m_w_in, v_w_in),
        ssm_conv_w=(ssm_conv_w, m_ssm_conv_w, v_ssm_conv_w), ssm_conv_b=(ssm_conv_b, m_ssm_conv_b, v_ssm_conv_b),
        ssm_dt_bias=(ssm_dt_bias, m_ssm_dt_bias, v_ssm_dt_bias), ssm_A_log=(ssm_A_log, m_ssm_A_log, v_ssm_A_log),
        ssm_D=(ssm_D, m_ssm_D, v_ssm_D), ssm_norm_w=(ssm_norm_w, m_ssm_norm_w, v_ssm_norm_w),
        gdn_conv_w=(gdn_conv_w, m_gdn_conv_w, v_gdn_conv_w), gdn_dt_bias=(gdn_dt_bias, m_gdn_dt_bias, v_gdn_dt_bias),
        gdn_A_log=(gdn_A_log, m_gdn_A_log, v_gdn_A_log), gdn_norm_w=(gdn_norm_w, m_gdn_norm_w, v_gdn_norm_w),
        w_ssm_up=(w_ssm_up, m_w_ssm_up, v_w_ssm_up), w_gdn_up=(w_gdn_up, m_w_gdn_up, v_w_gdn_up),
        w_out=(w_out, m_w_out, v_w_out), norm_mlp_pre=(norm_mlp_pre, m_norm_mlp_pre, v_norm_mlp_pre),
        norm_mlp_post=(norm_mlp_post, m_norm_mlp_post, v_norm_mlp_post), w_mlp_up=(w_mlp_up, m_w_mlp_up, v_w_mlp_up),
        w_mlp_down=(w_mlp_down, m_w_mlp_down, v_w_mlp_down))

    (c_all, scw, gcw, g_in) = _gather_two_level([c, ssm_conv_w[0], gdn_conv_w[0], w_in[0].astype(BF16)], "gather_w_in")
    c_all = c_all.reshape(N_DEV, Dm)
    sp = W_IN_SPLITS
    w_main = jnp.concatenate(_cols_of_shards(g_in, sp[0], sp[2]) + _cols_of_shards(g_in, sp[3], sp[5])
                             + _cols_of_shards(g_in, sp[7], sp[9]), axis=1)
    w_small = jnp.concatenate(_cols_of_shards(g_in, sp[2], sp[3]) + _cols_of_shards(g_in, sp[5], sp[7])
                              + [jnp.zeros((Dm, N_SMALL - 64), BF16)], axis=1)
    conv_w = jnp.concatenate([_by_cols(scw), _by_cols(gcw)], axis=1)
    conv_b = jnp.concatenate([ssm_conv_b, jnp.zeros_like(ssm_conv_b)], axis=1)

    b_loc = lax.dynamic_slice(b_ada, (0, me * n_ada), (1, n_ada))
    mod_part = _ada_fwd(c_all, w_ada[0], b_loc, "ada_fwd")
    (mod_rows,) = _exchange([mod_part.reshape(N_DEV, 1, n_ada)], ["a2a"], "exchange_mod")
    rest = _exchange_start([w_ssm_up[0].astype(BF16), w_gdn_up[0].astype(BF16), w_out[0].astype(BF16),
                            w_mlp_up[0].astype(BF16), w_mlp_down[0].astype(BF16)], ["gather"] * 5, mod_rows,
                           "gather_rest_start")
    mod = mod_rows.reshape(1, 6 * Dm) + rest[4][0:1, 0:1]
    sh1, sc1, g1, sh2, sc2, g2 = [mod[:, i * Dm:(i + 1) * Dm] for i in range(6)]

    h = _pre_fwd(x2, norm_mix_pre, sc1, sh1, "pre_mix")
    proj = _mm(h, w_main, S, N_MAIN, Dm, mode="nn", out_dtype=BF16, name="proj_main")
    small = _mm(h, w_small, S, N_SMALL, Dm, mode="nn", out_dtype=F32, name="proj_small")
    conv = _conv_fwd(proj, conv_w, conv_b, "conv_fwd")
    dt_g, b_g, a_g = _heads_first(small[:, 0:32], 8), _heads_first(small[:, 32:48], 8), _heads_first(small[:, 48:64], 8)
    pv_ssm = jnp.stack([ssm_dt_bias.reshape(8, 4), ssm_A_log.reshape(8, 4), ssm_D.reshape(8, 4)], axis=1)
    nw_ssm = ssm_norm_w.reshape(8, 1, SSM_GROUP_WIDTH)
    pv_gdn = jnp.stack([gdn_dt_bias.reshape(8, 2), gdn_A_log.reshape(8, 2)], axis=1)
    y_ssd, ysn, st_ssm = _ssd_fwd(conv, proj, dt_g, pv_ssm, nw_ssm, "ssd_fwd")
    o_pre, ogn, st_gdn, t_inv = _gdn_fwd(conv, proj, b_g, a_g, pv_gdn, gdn_norm_w, "gdn_fwd")
    g_su, g_gu, g_out, g_mu, g_md = _exchange_wait(rest, ["gather"] * 5, ogn, "gather_rest_wait")
    w_su, w_gu = g_su.reshape(2 * Dm, Dm), g_gu.reshape(2 * Dm, Dm)
    w_o, w_mu, w_md = g_out.reshape(Dm, Dm), _by_cols(g_mu), g_md.reshape(4 * Dm, Dm)
    ys = _mm(ysn, w_su, S, Dm, 2 * Dm, mode="nn", out_dtype=F32, name="ssm_up")
    yg, merged = _gdn_up_merge(ogn, w_gu, ys, proj, "gdn_up_merge")
    mo, x1, h2 = _mix_out_post_pre(merged, w_o, x2, norm_mix_post, g1, norm_mlp_pre, sc2, sh2, "mix_out_post_pre")
    u, act = _mm(h2, w_mu, S, 4 * Dm, Dm, mode="nn", out_dtype=F32, epi="relu2", name="mlp_up")
    y_mlp = _mm(act, w_md, S, Dm, 4 * Dm, mode="nn", out_dtype=F32, name="mlp_down")
    dx2, loss_loc, dy, dg2, dw_post2 = _final_fwd_bwd(x1, y_mlp, norm_mlp_post, g2, tgt, "post_mlp_loss_bwd")

    du = _mm(dy, w_md, S, 4 * Dm, Dm, mode="nt", out_dtype=BF16, epi="drelu2", extra=u, name="mlp_down_dx")
    gw_md = _mm(act, dy, 4 * Dm, Dm, S, mode="tn", out_dtype=BF16, name="mlp_down_dw")
    dh2 = _mm(du, w_mu, S, Dm, 4 * Dm, mode="nt", out_dtype=F32, name="mlp_up_dx")
    gw_mu = _mm(h2, du, Dm, 4 * Dm, S, mode="tn", out_dtype=BF16, name="mlp_up_dw")
    mlp_x = _exchange_start([_to_col_shards(gw_mu), gw_md.reshape(N_DEV, -1, Dm)], ["a2a"] * 2, gw_md,
                            "grads_mlp_start")
    dx1, dsh2, dsc2, dw_pre2 = _pre_bwd(dh2, x1, norm_mlp_pre, sc2 + mlp_x[4][0:1, 0:1], dx2, "pre_mlp_bwd")
    dmo, dg1, dw_post1 = _post_bwd(dx1, mo, norm_mix_post, g1, "post_mix_bwd")
    gw_o = _mm(merged, dmo, Dm, Dm, S, mode="tn", out_dtype=BF16, name="mix_out_dw")
    dys, dyg, d_proj = _mix_out_dx_merge_bwd(dmo, w_o, ys, yg, proj, lax.empty((S, N_MAIN), BF16), "mix_out_dx_merge")
    dysn = _mm(dys, w_su, S, 2 * Dm, Dm, mode="nt", out_dtype=F32, name="ssm_up_dx")
    gw_su = _mm(ysn, dys, 2 * Dm, Dm, S, mode="tn", out_dtype=BF16, name="ssm_up_dw")
    dogn = _mm(dyg, w_gu, S, 2 * Dm, Dm, mode="nt", out_dtype=F32, name="gdn_up_dx")
    gw_gu = _mm(ogn, dyg, 2 * Dm, Dm, S, mode="tn", out_dtype=BF16, name="gdn_up_dw")
    mix_x = _exchange_start([gw_su.reshape(N_DEV, -1, Dm), gw_gu.reshape(N_DEV, -1, Dm), gw_o.reshape(N_DEV, -1, Dm)],
                            ["a2a"] * 3, gw_gu, "grads_mix_start")
    dxs, dBm, dCm, d_proj, ddt_g, dpv_ssm, dnw_ssm = _ssd_bwd(dysn, conv, proj, dt_g, pv_ssm + mix_x[4][0, 0], nw_ssm,
                                                              y_ssd, st_ssm, d_proj, "ssd_bwd")
    dq, dk, dv, d_proj, db_g, da_g, dpv_gdn, dnw_gdn = _gdn_bwd(dogn, conv, proj, b_g, a_g, pv_gdn, gdn_norm_w, o_pre,
                                                                st_gdn, t_inv, d_proj, "gdn_bwd")
    conv_pieces = []
    for nm, d_act, col0 in (("xs", dxs, 0), ("B", dBm, 2048), ("C", dCm, 3072), ("q", dq, 4096), ("k", dk, 5120),
                            ("v", dv, 6144)):
        d_proj, dw_piece, db_piece = _conv_bwd(d_act, proj, conv_w, conv_b, col0, d_proj, "conv_bwd_" + nm)
        conv_pieces.append((dw_piece, db_piece))
    d_small = jnp.concatenate([_heads_last(ddt_g), _heads_last(db_g), _heads_last(da_g),
                               jnp.zeros((S, N_SMALL - 64), F32)], axis=1).astype(BF16)
    gw_small = _mm(h, d_small, Dm, N_SMALL, S, mode="tn", out_dtype=BF16, name="proj_small_dw")
    main_cols = _mm(h, d_proj, Dm, N_MAIN, S, mode="tn", out_dtype=BF16, name="proj_main_dw")
    n_shard = w_in.shape[2]
    slabs = [_orig_cols(main_cols, gw_small, i * n_shard, (i + 1) * n_shard) for i in range(N_DEV)]
    to_c0, to_c1 = jnp.stack(slabs[0::2]), jnp.stack(slabs[1::2])
    chip_sum = _add_pair(to_c0, to_c1, _swap_sibling(to_c0, to_c1, "grads_w_in_pair"), "grads_w_in_pair_sum")
    in_x = _exchange_start([chip_sum], ["a2a"], gw_small, "grads_w_in_start", chips=True)
    dh = _mm(d_small, w_small + in_x[4][0:1, 0:1].astype(BF16), S, Dm, N_SMALL, mode="nt", out_dtype=F32,
             name="proj_small_dx")
    dh = _mm(d_proj, w_main, S, Dm, N_MAIN, mode="nt", out_dtype=F32, add=dh, name="proj_main_dx")
    dx, dsh1, dsc1, dw_pre1 = _pre_bwd(dh, x2, norm_mix_pre, sc1, dx1, "pre_mix_bwd")
    r_mu, r_md = _exchange_wait(mlp_x, ["a2a"] * 2, dx, "grads_mlp_wait")
    r_su, r_gu, r_o = _exchange_wait(mix_x, ["a2a"] * 3, dx, "grads_mix_wait")

    dconv_w = jnp.concatenate([p[0] for p in conv_pieces], axis=1)
    dconv_b = jnp.concatenate([p[1] for p in conv_pieces[:3]], axis=1)
    dmod = jnp.concatenate([dsh1, dsc1, dg1, dsh2, dsc2, dg2], axis=1)
    small_vec = jnp.concatenate(
        [dmod, dw_pre1, dw_post1, dconv_b, dpv_ssm[:, 0].reshape(1, 32), dpv_ssm[:, 1].reshape(1, 32),
         dpv_ssm[:, 2].reshape(1, 32), dnw_ssm.reshape(1, 2048), dpv_gdn[:, 0].reshape(1, 16),
         dpv_gdn[:, 1].reshape(1, 16), jnp.sum(dnw_gdn, axis=0), dw_pre2, dw_post2, dconv_w.reshape(1, -1)], axis=1)
    n_vec = small_vec.shape[1]
    small_vec = jnp.pad(small_vec, ((0, 0), (0, (-n_vec) % 1024))).reshape(-1, 1024)
    (small_all,) = _exchange([small_vec], ["gather"], "gather_small_grads")
    small_all = small_all.reshape(N_DEV, -1)
    dmod_cols = lax.dynamic_slice(small_all, (0, me * n_ada), (N_DEV, n_ada))
    gw_ada = _ada_bwd(c_all.T, dmod_cols, "ada_bwd")
    conv_all = small_all[:, N_REPLICATED:n_vec].reshape(N_DEV, CONV_K, 2 * N_DEV * 512)
    conv_contrib = jnp.concatenate(
        [lax.dynamic_slice(conv_all, (0, 0, me * 512), (N_DEV, CONV_K, 512)),
         lax.dynamic_slice(conv_all, (0, 0, N_DEV * 512 + me * 512), (N_DEV, CONV_K, 512))], axis=1)
    rep_contrib = small_all[:, :N_REPLICATED].reshape(N_DEV, N_REPLICATED // 128, 128)

    results = {}

    def adam_big(nm, contrib):
        w3 = given[nm]
        res = _adam(contrib, w3[0][0], w3[1][0], w3[2][0], "adam_" + nm)
        results[nm] = tuple(r.reshape(w3[0].shape) for r in res)

    adam_big("w_ada", gw_ada[None])
    adam_big("w_ssm_up", r_su)
    adam_big("w_gdn_up", r_gu)
    adam_big("w_out", r_o)
    adam_big("w_mlp_up", r_mu)
    adam_big("w_mlp_down", r_md)
    (r_in,) = _exchange_wait(in_x, ["a2a"], results["w_mlp_down"][0], "grads_w_in_wait", chips=True)
    adam_big("w_in", r_in)
    packed = [jnp.concatenate([given[nm][i] for nm in REPLICATED], axis=1).reshape(N_REPLICATED // 128, 128)
              for i in range(3)]
    rep_res = _adam(rep_contrib, packed[0], packed[1], packed[2], "adam_replicated")
    pos = 0
    for nm in REPLICATED:
        size = given[nm][0].shape[1]
        results[nm] = tuple(r.reshape(1, N_REPLICATED)[:, pos:pos + size] for r in rep_res)
        pos += size
    conv_wmv = [jnp.concatenate([given["ssm_conv_w"][i][0], given["gdn_conv_w"][i][0]], axis=0) for i in range(3)]
    conv_res = _adam(conv_contrib, conv_wmv[0], conv_wmv[1], conv_wmv[2], "adam_conv_w")
    results["ssm_conv_w"] = tuple(r[None, :CONV_K] for r in conv_res)
    results["gdn_conv_w"] = tuple(r[None, CONV_K:] for r in conv_res)

    loss = lax.psum(loss_loc[0, 0], ("x", "y", "c"))
    return (loss, dx[None]) + tuple(results[nm][i] for i in range(4) for nm in WEIGHTS)
```

```python
import jax
import jax.numpy as jnp
from jax import lax
from jax.experimental import pallas as pl
from jax.experimental.pallas import tpu as pltpu

F32 = jnp.float32
BF16 = jnp.bfloat16
N_DEV = 8
D_MODEL = 1024
EPS = 1e-6
CONV_K = 4
SSM_CHUNK = 128
SSM_HEAD_DIM = 64
SSM_D_STATE = 128
SSM_GROUPS = 8
SSM_HEADS_PER_GROUP = 4
SSM_GROUP_WIDTH = SSM_HEADS_PER_GROUP * SSM_HEAD_DIM
SSM_GROUPS_PER_STEP = 8
GDN_CHUNK = 64
GDN_HEAD = 128
GDN_QK_HEADS = 8
GDN_V_PER_QK = 2
GDN_QK_PER_STEP = 8
GDN_INV_BLOCK = 16
C_ZS, C_XBC, C_QKV, C_ZG, C_GS, C_GG, N_MAIN = 0, 2048, 6144, 10240, 12288, 13312, 14336
N_SMALL = 128
ADAM_LR, ADAM_B1, ADAM_B2, ADAM_EPS, ADAM_WD, ADAM_STEP = 0.001, 0.9, 0.999, 1e-08, 0.01, 10
VMEM_LIMIT = 56 * 1024 * 1024
MM_WHOLE_K = 4096
MM_SPLIT_K = 2048
NEG_INF = float("-inf")

_NT = (((1,), (1,)), ((), ()))
_NN = (((1,), (0,)), ((), ()))
_TN = (((0,), (0,)), ((), ()))


def _params(*sem):
    return pltpu.CompilerParams(dimension_semantics=sem, vmem_limit_bytes=VMEM_LIMIT)


def _dot(a, b, dims=_NN):
    return lax.dot_general(a.astype(BF16), b.astype(BF16), dims, preferred_element_type=F32)


def _split(a):
    hi = a.astype(BF16)
    return hi, (a - hi.astype(F32)).astype(BF16)


def _dot3(a, b, dims=_NN):
    ah, al = _split(a)
    bh, bl = _split(b)
    d = lambda u, v: lax.dot_general(u, v, dims, preferred_element_type=F32)
    return d(ah, bh) + (d(ah, bl) + d(al, bh))


def _dot2(a, b, dims=_NN):
    ah, al = _split(a)
    bb = b.astype(BF16)
    d = lambda u: lax.dot_general(u, bb, dims, preferred_element_type=F32)
    return d(ah) + d(al)


def _sigmoid(x):
    return 0.5 * jnp.tanh(0.5 * x) + 0.5


def _silu(x):
    return x * _sigmoid(x)


def _dsilu(x):
    s = _sigmoid(x)
    return s * (1.0 + x * (1.0 - s))


def _softplus(x):
    return jnp.maximum(x, 0.0) + jnp.log1p(jnp.exp(-jnp.abs(x)))


def _iota(n, m, d):
    return lax.broadcasted_iota(jnp.int32, (n, m), d)


def _rowsum(x):
    return jnp.sum(x, axis=1, keepdims=True)


def _colsum(x):
    return jnp.sum(x, axis=0, keepdims=True)


def _total(x):
    return _rowsum(_colsum(x))


MXU_LANES = 128


def _parts(x, n):
    out = []
    for _ in range(n):
        p = x.astype(BF16)
        out.append(p)
        x = x - p.astype(F32)
    return out


def _sum_by(m01, x, dims=_NN, n=3):
    return sum(lax.dot_general(m01, p, dims, preferred_element_type=F32) for p in _parts(x, n))


def _row_col_sums(q):
    ones = jnp.ones((q.shape[0], MXU_LANES), BF16)
    acc = 0.0
    for p in _parts(q, 2):
        acc = acc + (lax.dot_general(p, ones, _NN, preferred_element_type=F32)
                     - lax.dot_general(p, ones, _TN, preferred_element_type=F32))
    return acc[:, 0:1]


def _cumsum_forms(col, ii, jj):
    lower = jnp.where(ii >= jj, 1.0, 0.0).astype(BF16)
    cum_col = _sum_by(lower, jnp.broadcast_to(col, (col.shape[0], MXU_LANES)))[:, 0:1]
    cum_row = _colsum(jnp.where(ii <= jj, col, 0.0))
    return cum_col, cum_row


def _rev_cumsum_col(col, ii, jj):
    upper = jnp.where(ii <= jj, 1.0, 0.0).astype(BF16)
    return _sum_by(upper, jnp.broadcast_to(col, (col.shape[0], MXU_LANES)))[:, 0:1]


def _blk(dim, pref):
    return pref if dim % pref == 0 else dim


def _lockstep(gens):
    gens = list(gens)
    while gens:
        alive = []
        for g in gens:
            try:
                next(g)
                alive.append(g)
            except StopIteration:
                pass
        gens = alive


def _mm(a, b, M, N, K, *, mode, out_dtype, name, a_off=(0, 0), b_off=(0, 0), add=None, epi=None, extra=None,
        tm=1024, tn=1024):
    tm, tn = _blk(M, tm), _blk(N, tn)
    tk = K if K <= MM_WHOLE_K else _blk(K, MM_SPLIT_K)
    nk = K // tk
    if mode == "tn":
        a_spec = pl.BlockSpec((tk, tm), lambda i, j, k: (k + a_off[0] // tk, i + a_off[1] // tm))
        assert a_off[0] % tk == 0 and a_off[1] % tm == 0
    else:
        a_spec = pl.BlockSpec((tm, tk), lambda i, j, k: (i + a_off[0] // tm, k + a_off[1] // tk))
        assert a_off[0] % tm == 0 and a_off[1] % tk == 0
    if mode == "nt":
        b_spec = pl.BlockSpec((tn, tk), lambda i, j, k: (j + b_off[0] // tn, k + b_off[1] // tk))
        assert b_off[0] % tn == 0 and b_off[1] % tk == 0
    else:
        b_spec = pl.BlockSpec((tk, tn), lambda i, j, k: (k + b_off[0] // tk, j + b_off[1] // tn))
        assert b_off[0] % tk == 0 and b_off[1] % tn == 0
    dims = {"nn": _NN, "nt": _NT, "tn": _TN}[mode]
    o_spec = pl.BlockSpec((tm, tn), lambda i, j, k: (i, j))
    ins, in_specs = [a, b], [a_spec, b_spec]
    if add is not None:
        ins.append(add)
        in_specs.append(o_spec)
    if extra is not None:
        ins.append(extra)
        in_specs.append(o_spec)
    n_in = len(ins)
    if epi == "relu2":
        out_shape = (jax.ShapeDtypeStruct((M, N), BF16), jax.ShapeDtypeStruct((M, N), BF16))
        out_specs = (o_spec, o_spec)
    else:
        out_shape = jax.ShapeDtypeStruct((M, N), out_dtype)
        out_specs = o_spec

    def body(*refs):
        a_ref, b_ref = refs[0], refs[1]
        outs = refs[n_in:] if nk == 1 else refs[n_in:-1]

        def finish(r):
            pos = 2
            if add is not None:
                r = r + refs[pos][...]
                pos += 1
            if epi == "relu2":
                p = jnp.maximum(r, 0.0)
                outs[0][...] = p.astype(BF16)
                outs[1][...] = (p * p).astype(BF16)
            elif epi == "drelu2":
                outs[0][...] = (r * (2.0 * refs[pos][...].astype(F32))).astype(out_dtype)
            else:
                outs[0][...] = r.astype(out_dtype)

        if nk == 1:
            finish(_dot(a_ref[...], b_ref[...], dims))
            return
        acc = refs[-1]
        k = pl.program_id(2)

        @pl.when(k == 0)
        def _():
            acc[...] = jnp.zeros_like(acc)

        acc[...] += _dot(a_ref[...], b_ref[...], dims)

        @pl.when(k == nk - 1)
        def _():
            finish(acc[...])

    return pl.pallas_call(
        body, name=name, grid=(M // tm, N // tn, nk), in_specs=in_specs, out_specs=out_specs, out_shape=out_shape,
        scratch_shapes=[] if nk == 1 else [pltpu.VMEM((tm, tn), F32)],
        compiler_params=_params("parallel", "parallel", "arbitrary"))(*ins)


def _mm_rows(a, b, M, N, K, *, mode, name, extras, out_shapes, out_specs, epilogue, aliases=None, tm=512):
    tm = _blk(M, tm)
    a_spec = pl.BlockSpec((tm, K), lambda i: (i, 0))
    b_spec = pl.BlockSpec((K, N) if mode == "nn" else (N, K), lambda i: (0, 0))
    dims = _NN if mode == "nn" else _NT
    n_ex = len(extras)

    def body(a_ref, b_ref, *refs):
        epilogue(_dot(a_ref[...], b_ref[...], dims), refs[:n_ex], refs[n_ex:])

    return pl.pallas_call(
        body, name=name, grid=(M // tm,), in_specs=[a_spec, b_spec] + [sp for _, sp in extras],
        out_specs=tuple(out_specs), out_shape=tuple(out_shapes), input_output_aliases=aliases or {},
        compiler_params=_params("parallel"))(a, b, *[x for x, _ in extras])


def _row_spec(tb, d):
    return pl.BlockSpec((tb, d), lambda i: (i, 0))


def _vec_spec(d):
    return pl.BlockSpec((1, d), lambda i: (0, 0))


def _pre_fwd(x, w, sc, sh, name):
    S, Dm = x.shape
    tb = _blk(S, 512)

    def body(x_ref, w_ref, sc_ref, sh_ref, h_ref):
        xv = x_ref[...]
        r = lax.rsqrt(jnp.mean(xv * xv, axis=-1, keepdims=True) + EPS)
        h_ref[...] = ((xv * r * w_ref[...]) * (1.0 + sc_ref[...]) + sh_ref[...]).astype(BF16)

    return pl.pallas_call(
        body, name=name, grid=(S // tb,), in_specs=[_row_spec(tb, Dm)] + [_vec_spec(Dm)] * 3,
        out_specs=_row_spec(tb, Dm), out_shape=jax.ShapeDtypeStruct((S, Dm), BF16),
        compiler_params=_params("parallel"))(x, w, sc, sh)


def _final_fwd_bwd(x, y, w, g, target, name):
    S, Dm = x.shape
    tb = _blk(S, 512)
    nb = S // tb

    def body(x_ref, y_ref, w_ref, g_ref, t_ref, dx_ref, loss_ref, dy_ref, dg_ref, dw_ref, acc):
        i = pl.program_id(0)

        @pl.when(i == 0)
        def _():
            acc[...] = jnp.zeros_like(acc)
            dg_ref[...] = jnp.zeros_like(dg_ref)
            dw_ref[...] = jnp.zeros_like(dw_ref)

        yv = y_ref[...]
        r = lax.rsqrt(jnp.mean(yv * yv, axis=-1, keepdims=True) + EPS)
        yh = yv * r
        n = yh * w_ref[...]
        e = (x_ref[...] + g_ref[...] * n) - t_ref[...]
        dv = e * (1.0 / Dm)
        dx_ref[...] = dv
        acc[...] += _colsum(e * e)
        dg_ref[...] += _colsum(dv * n)
        dn = dv * g_ref[...]
        dw_ref[...] += _colsum(dn * yh)
        dyh = dn * w_ref[...]
        dy_ref[...] = (r * (dyh - yh * jnp.mean(dyh * yh, axis=-1, keepdims=True))).astype(BF16)

        @pl.when(i == nb - 1)
        def _():
            loss_ref[...] = (0.5 / Dm) * _rowsum(acc[...])

    row, vec = _row_spec(tb, Dm), _vec_spec(Dm)
    vec_shape = jax.ShapeDtypeStruct((1, Dm), F32)
    return pl.pallas_call(
        body, name=name, grid=(nb,), in_specs=[row, row, vec, vec, row],
        out_specs=(row, pl.BlockSpec((1, 1), lambda i: (0, 0)), row, vec, vec),
        out_shape=(jax.ShapeDtypeStruct((S, Dm), F32), jax.ShapeDtypeStruct((1, 1), F32),
                   jax.ShapeDtypeStruct((S, Dm), BF16), vec_shape, vec_shape),
        scratch_shapes=[pltpu.VMEM((1, Dm), F32)], compiler_params=_params("arbitrary"))(x, y, w, g, target)


def _post_bwd(dxo, y, w, g, name):
    S, Dm = y.shape
    tb = _blk(S, 512)

    def body(d_ref, y_ref, w_ref, g_ref, dy_ref, dg_ref, dw_ref):
        i = pl.program_id(0)

        @pl.when(i == 0)
        def _():
            dg_ref[...] = jnp.zeros_like(dg_ref)
            dw_ref[...] = jnp.zeros_like(dw_ref)

        yv, dv = y_ref[...], d_ref[...]
        r = lax.rsqrt(jnp.mean(yv * yv, axis=-1, keepdims=True) + EPS)
        yh = yv * r
        dg_ref[...] += _colsum(dv * (yh * w_ref[...]))
        dn = dv * g_ref[...]
        dw_ref[...] += _colsum(dn * yh)
        dyh = dn * w_ref[...]
        dy_ref[...] = (r * (dyh - yh * jnp.mean(dyh * yh, axis=-1, keepdims=True))).astype(BF16)

    return pl.pallas_call(
        body, name=name, grid=(S // tb,), in_specs=[_row_spec(tb, Dm)] * 2 + [_vec_spec(Dm)] * 2,
        out_specs=(_row_spec(tb, Dm), _vec_spec(Dm), _vec_spec(Dm)),
        out_shape=(jax.ShapeDtypeStruct((S, Dm), BF16), jax.ShapeDtypeStruct((1, Dm), F32),
                   jax.ShapeDtypeStruct((1, Dm), F32)),
        compiler_params=_params("arbitrary"))(dxo, y, w, g)


def _pre_bwd(dh, x, w, sc, dres, name):
    S, Dm = x.shape
    tb = _blk(S, 512)

    def body(dh_ref, x_ref, w_ref, sc_ref, dr_ref, dx_ref, dsh_ref, dsc_ref, dw_ref):
        i = pl.program_id(0)

        @pl.when(i == 0)
        def _():
            dsh_ref[...] = jnp.zeros_like(dsh_ref)
            dsc_ref[...] = jnp.zeros_like(dsc_ref)
            dw_ref[...] = jnp.zeros_like(dw_ref)

        xv, dv = x_ref[...], dh_ref[...]
        r = lax.rsqrt(jnp.mean(xv * xv, axis=-1, keepdims=True) + EPS)
        xh = xv * r
        one_sc = 1.0 + sc_ref[...]
        dsh_ref[...] += _colsum(dv)
        dsc_ref[...] += _colsum(dv * (xh * w_ref[...]))
        dw_ref[...] += _colsum(dv * one_sc * xh)
        dxh = dv * one_sc * w_ref[...]
        dx_ref[...] = dr_ref[...] + r * (dxh - xh * jnp.mean(dxh * xh, axis=-1, keepdims=True))

    vec = jax.ShapeDtypeStruct((1, Dm), F32)
    return pl.pallas_call(
        body, name=name, grid=(S // tb,),
        in_specs=[_row_spec(tb, Dm)] * 2 + [_vec_spec(Dm)] * 2 + [_row_spec(tb, Dm)],
        out_specs=(_row_spec(tb, Dm), _vec_spec(Dm), _vec_spec(Dm), _vec_spec(Dm)),
        out_shape=(jax.ShapeDtypeStruct((S, Dm), F32), vec, vec, vec),
        compiler_params=_params("arbitrary"))(dh, x, w, sc, dres)


D_PROJ_ANY = pl.BlockSpec(memory_space=pl.ANY)


def _gate_specs(tm, Dm):
    return (pl.BlockSpec((tm, Dm), lambda i: (i, C_GS // Dm)), pl.BlockSpec((tm, Dm), lambda i: (i, C_GG // Dm)))


def _gdn_up_merge(ogn, w_gu, ys, proj, name):
    S, K = ogn.shape
    Dm = ys.shape[1]
    tm = _blk(S, 512)
    row = _row_spec(tm, Dm)

    def epilogue(r, ex, out):
        ys_ref, gs_ref, gg_ref = ex
        out[0][...] = r
        out[1][...] = (_sigmoid(gs_ref[...].astype(F32)) * ys_ref[...]
                       + _sigmoid(gg_ref[...].astype(F32)) * r).astype(BF16)

    gs_spec, gg_spec = _gate_specs(tm, Dm)
    return _mm_rows(ogn, w_gu, S, Dm, K, mode="nn", name=name, tm=tm,
                    extras=[(ys, row), (proj, gs_spec), (proj, gg_spec)],
                    out_shapes=[jax.ShapeDtypeStruct((S, Dm), F32), jax.ShapeDtypeStruct((S, Dm), BF16)],
                    out_specs=[row, row], epilogue=epilogue)


def _mix_out_post_pre(merged, w_o, x, w_post, g, w_pre, sc, sh, name):
    S, Dm = x.shape
    tm = _blk(S, 512)
    row, vec = _row_spec(tm, Dm), _vec_spec(Dm)

    def epilogue(r, ex, out):
        x_ref, wpost_ref, g_ref, wpre_ref, sc_ref, sh_ref = ex
        out[0][...] = r
        rr = lax.rsqrt(jnp.mean(r * r, axis=-1, keepdims=True) + EPS)
        x1 = x_ref[...] + g_ref[...] * (r * rr * wpost_ref[...])
        out[1][...] = x1
        r1 = lax.rsqrt(jnp.mean(x1 * x1, axis=-1, keepdims=True) + EPS)
        out[2][...] = ((x1 * r1 * wpre_ref[...]) * (1.0 + sc_ref[...]) + sh_ref[...]).astype(BF16)

    return _mm_rows(merged, w_o, S, Dm, Dm, mode="nn", name=name, tm=tm,
                    extras=[(x, row), (w_post, vec), (g, vec), (w_pre, vec), (sc, vec), (sh, vec)],
                    out_shapes=[jax.ShapeDtypeStruct((S, Dm), F32), jax.ShapeDtypeStruct((S, Dm), F32),
                                jax.ShapeDtypeStruct((S, Dm), BF16)], out_specs=[row, row, row], epilogue=epilogue)


def _mix_out_dx_merge_bwd(dmo, w_o, ys, yg, proj, d_proj, name):
    S, Dm = ys.shape
    tm = _blk(S, 512)
    row = _row_spec(tm, Dm)

    def epilogue(d, ex, out):
        ys_ref, yg_ref, gs_ref, gg_ref, _ = ex
        ss, sg = _sigmoid(gs_ref[...].astype(F32)), _sigmoid(gg_ref[...].astype(F32))
        out[0][...] = (d * ss).astype(BF16)
        out[1][...] = (d * sg).astype(BF16)
        out[2][:, :Dm] = (d * ys_ref[...] * ss * (1.0 - ss)).astype(BF16)
        out[2][:, Dm:] = (d * yg_ref[...] * sg * (1.0 - sg)).astype(BF16)

    gs_spec, gg_spec = _gate_specs(tm, Dm)
    return _mm_rows(dmo, w_o, S, Dm, Dm, mode="nt", name=name, tm=tm,
                    extras=[(ys, row), (yg, row), (proj, gs_spec), (proj, gg_spec), (d_proj, D_PROJ_ANY)],
                    out_shapes=[jax.ShapeDtypeStruct((S, Dm), BF16), jax.ShapeDtypeStruct((S, Dm), BF16),
                                jax.ShapeDtypeStruct(d_proj.shape, BF16)],
                    out_specs=[row, row, pl.BlockSpec((tm, 2 * Dm), lambda i: (i, C_GS // (2 * Dm)))],
                    epilogue=epilogue, aliases={6: 2})


CONV_COLS = 128
CONV_BWD_ROWS = 256


def _taps_down(x):
    rows = _iota(x.shape[0], x.shape[1], 0)
    return [x] + [jnp.where(rows >= k, pltpu.roll(x, k, 0), 0.0) for k in range(1, CONV_K)]


def _conv_pre(taps, w_ref, b_ref):
    pre = taps[0] * w_ref[CONV_K - 1:CONV_K, :] + b_ref[...]
    for k in range(1, CONV_K):
        pre = pre + taps[k] * w_ref[CONV_K - 1 - k:CONV_K - k, :]
    return pre


def _conv_dx(dpre, w_ref):
    n = dpre.shape[0]
    rows = _iota(n, dpre.shape[1], 0)
    dx = dpre * w_ref[CONV_K - 1:CONV_K, :]
    for k in range(1, CONV_K):
        dx = dx + jnp.where(rows < n - k, pltpu.roll(dpre, n - k, 0), 0.0) * w_ref[CONV_K - 1 - k:CONV_K - k, :]
    return dx


def _conv_fwd(proj, w, b, name):
    S = proj.shape[0]
    n = w.shape[1]
    cb = CONV_COLS

    def body(x_ref, w_ref, b_ref, o_ref):
        o_ref[...] = _silu(_conv_pre(_taps_down(x_ref[...].astype(F32)), w_ref, b_ref)).astype(BF16)

    return pl.pallas_call(
        body, name=name, grid=(n // cb,),
        in_specs=[pl.BlockSpec((S, cb), lambda j: (0, j + C_XBC // cb)), pl.BlockSpec((CONV_K, cb), lambda j: (0, j)),
                  pl.BlockSpec((1, cb), lambda j: (0, j))],
        out_specs=pl.BlockSpec((S, cb), lambda j: (0, j)), out_shape=jax.ShapeDtypeStruct((S, n), BF16),
        compiler_params=_params("parallel"))(proj, w, b)


def _conv_bwd(dact, proj, w, b, col0, d_proj, name):
    S, n = dact.shape
    cb = CONV_COLS
    o = col0 // cb

    R, HALO = _blk(S, CONV_BWD_ROWS), 16
    n_chunks = S // R

    def body(d_ref, x_ref, w_ref, b_ref, _, dx_ref, dw_ref, db_ref):
        def chunk(r0, first, last, sums):
            lo, hi = (0 if first else HALO), (0 if last else HALO)
            start = r0 - lo if isinstance(r0, int) else pl.multiple_of(r0 - lo, HALO)
            xe = x_ref[pl.ds(start, lo + R + hi), :].astype(F32)
            rows = _iota(lo + R + hi, cb, 0)
            taps = [xe[lo:, :]]
            for k in range(1, CONV_K):
                t = pltpu.roll(xe, k, 0)
                taps.append((jnp.where(rows >= k, t, 0.0) if first else t)[lo:, :])
            dpre_e = d_ref[pl.ds(r0, R + hi), :].astype(F32) * _dsilu(_conv_pre(taps, w_ref, b_ref))
            dpre = dpre_e[0:R, :]
            db, dw = sums
            db = db + _colsum(dpre)
            dw = [dw[k] + _colsum(dpre * taps[k][0:R, :]) for k in range(CONV_K)]
            rows_e = _iota(R + hi, cb, 0)
            dx = dpre * w_ref[CONV_K - 1:CONV_K, :]
            for k in range(1, CONV_K):
                t = pltpu.roll(dpre_e, R + hi - k, 0)
                t = jnp.where(rows_e < R - k, t, 0.0) if last else t
                dx = dx + t[0:R, :] * w_ref[CONV_K - 1 - k:CONV_K - k, :]
            dx_ref[pl.ds(r0, R), :] = dx.astype(BF16)
            return db, dw

        zero = jnp.zeros((1, cb), F32)
        sums = chunk(0, True, n_chunks == 1, (zero, [zero] * CONV_K))
        if n_chunks > 2:
            def step(i, carry):
                db, dw = chunk(pl.multiple_of(i * R, R), False, False, (carry[0], list(carry[1:])))
                return (db,) + tuple(dw)
            carry = lax.fori_loop(1, n_chunks - 1, step, (sums[0],) + tuple(sums[1]))
            sums = (carry[0], list(carry[1:]))
        if n_chunks > 1:
            sums = chunk((n_chunks - 1) * R, False, True, sums)
        db_ref[...] = sums[0]
        for k in range(CONV_K):
            dw_ref[CONV_K - 1 - k:CONV_K - k, :] = sums[1][k]

    return pl.pallas_call(
        body, name=name, grid=(n // cb,),
        in_specs=[pl.BlockSpec((S, cb), lambda j: (0, j)), pl.BlockSpec((S, cb), lambda j: (0, j + o + C_XBC // cb)),
                  pl.BlockSpec((CONV_K, cb), lambda j: (0, j + o)), pl.BlockSpec((1, cb), lambda j: (0, j + o)),
                  D_PROJ_ANY],
        out_specs=(pl.BlockSpec((S, cb), lambda j: (0, j + o + C_XBC // cb)),
                   pl.BlockSpec((CONV_K, cb), lambda j: (0, j)), pl.BlockSpec((1, cb), lambda j: (0, j))),
        out_shape=(jax.ShapeDtypeStruct(d_proj.shape, BF16), jax.ShapeDtypeStruct((CONV_K, n), F32),
                   jax.ShapeDtypeStruct((1, n), F32)),
        input_output_aliases={4: 0}, compiler_params=_params("parallel"))(dact, proj, w, b, d_proj)


def _ssd_specs(L, order):
    G = SSM_GROUPS_PER_STEP
    W, N = G * SSM_GROUP_WIDTH, G * SSM_D_STATE
    x_spec = pl.BlockSpec((L, W), lambda g, c: (order(c), g))
    b_spec = pl.BlockSpec((L, N), lambda g, c: (order(c), 2048 // N + g))
    c_spec = pl.BlockSpec((L, N), lambda g, c: (order(c), 3072 // N + g))
    z_spec = pl.BlockSpec((L, W), lambda g, c: (order(c), C_ZS // W + g))
    dt_spec = pl.BlockSpec((G, L, SSM_HEADS_PER_GROUP), lambda g, c: (g, order(c), 0))
    p_spec = pl.BlockSpec((G, 3, SSM_HEADS_PER_GROUP), lambda g, c: (g, 0, 0))
    nw_spec = pl.BlockSpec((G, 1, SSM_GROUP_WIDTH), lambda g, c: (g, 0, 0))
    s_spec = pl.BlockSpec((G, 1, SSM_GROUP_WIDTH, SSM_D_STATE), lambda g, c: (g, order(c), 0, 0))
    return x_spec, b_spec, c_spec, z_spec, dt_spec, p_spec, nw_spec, s_spec


class _SsdGroup:
    def __init__(self, L):
        P, H, W = SSM_HEAD_DIM, SSM_HEADS_PER_GROUP, SSM_GROUP_WIDTH
        self.L = L
        self.ii, self.jj = _iota(L, L, 0), _iota(L, L, 1)
        self.lower = jnp.where(self.ii >= self.jj, 1.0, 0.0).astype(BF16)
        self.upper = jnp.where(self.ii <= self.jj, 1.0, 0.0).astype(BF16)
        self.lo = _iota(L, 2 * P, 1) < P
        self.lo_row = _iota(1, 2 * P, 1) < P
        bi, bj = _iota(W, W, 0), _iota(W, W, 1)
        self.block = jnp.where(bi // P == bj // P, 1.0, 0.0).astype(BF16)
        si, sj = _iota(2 * P, W, 0), _iota(2 * P, W, 1)
        self.pick = jnp.where(sj == si * P, 1.0, 0.0).astype(BF16)
        self.ones = jnp.ones((L, 2 * P), BF16)

    def spread(self, v4):
        R = v4.shape[0]
        lo = self.lo if R == self.L else self.lo_row
        b = lambda h: jnp.broadcast_to(v4[:, h:h + 1], (R, 2 * SSM_HEAD_DIM))
        return jnp.concatenate([jnp.where(lo, b(0), b(1)), jnp.where(lo, b(2), b(3))], axis=1)

    def gather4(self, v):
        return jnp.concatenate([v[:, h * SSM_HEAD_DIM:h * SSM_HEAD_DIM + 1] for h in range(SSM_HEADS_PER_GROUP)],
                               axis=1)

    def head_sums(self, z):
        return sum(lax.dot_general(p, self.block, _NN, preferred_element_type=F32) for p in _parts(z, 2))

    def pair_cols(self, full, pair):
        ps = full[:, pair * 128:(pair + 1) * 128]
        sw = pltpu.roll(ps, SSM_HEAD_DIM, 1)
        return jnp.where(self.lo, ps, sw), jnp.where(self.lo, sw, ps)

    def gates(self, dt4_raw, p):
        L = self.L
        dtr = self.spread(dt4_raw + p[0:1, :])
        dt = _softplus(dtr)
        A = self.spread(-jnp.exp(p[1:2, :]))
        acum = _sum_by(self.lower, dt * A)
        yield
        rows = _sum_by(self.pick, acum, _NT)
        yield
        a_last = acum[L - 1:L, :]
        cols = self.pair_cols(acum, 0) + self.pair_cols(acum, 1)
        decay, decay_t = [], []
        for h in range(SSM_HEADS_PER_GROUP):
            seg = cols[h] - rows[h:h + 1, :]
            decay.append(jnp.exp(jnp.where(self.ii >= self.jj, seg, NEG_INF)))
            decay_t.append(jnp.exp(jnp.where(self.jj >= self.ii, -seg, NEG_INF)))
        return dict(dtr=dtr, dt=dt, A=A, D=self.spread(p[2:3, :]), acum=acum, eac=jnp.exp(acum), a_last=a_last,
                    wdec=jnp.exp(a_last - acum), decay=decay, decay_t=decay_t,
                    ea_last=[jnp.exp(rows[h:h + 1, L - 1:L]) for h in range(SSM_HEADS_PER_GROUP)])


def _ssd_fwd(conv, proj, dt_raw, pvec, nw, name):
    S = conv.shape[0]
    L, P, N, H, W, G = SSM_CHUNK, SSM_HEAD_DIM, SSM_D_STATE, SSM_HEADS_PER_GROUP, SSM_GROUP_WIDTH, SSM_GROUPS_PER_STEP
    nc = S // L

    def body(x_ref, b_ref, c_ref, z_ref, dt_ref, p_ref, nw_ref, y_ref, yn_ref, s0_ref, state):
        c = pl.program_id(1)

        @pl.when(c == 0)
        def _():
            state[...] = jnp.zeros_like(state)

        k = _SsdGroup(L)

        def group(gi):
            gsl = slice(gi * W, (gi + 1) * W)
            Bm, Cm = b_ref[:, gi * N:(gi + 1) * N], c_ref[:, gi * N:(gi + 1) * N]
            x = x_ref[:, gsl].astype(F32)
            S0 = state[gsl, :]
            s0_ref[gi, 0] = S0
            CB = _dot(Cm, Bm, _NT)
            y_off = _dot(Cm, S0, _NT)
            t = yield from k.gates(dt_ref[gi], p_ref[gi])
            xdt = x * t["dt"]
            s_new = _dot(xdt * t["wdec"], Bm, _TN)
            y_diag = []
            for pair in range(H // 2):
                xp = xdt[:, pair * 128:(pair + 1) * 128]
                y_diag.append(jnp.where(k.lo, _dot(CB * t["decay"][2 * pair], xp),
                                        _dot(CB * t["decay"][2 * pair + 1], xp)))
            yield
            y = jnp.concatenate(y_diag, axis=1) + y_off * t["eac"]
            for h in range(H):
                hsl = slice(gi * W + h * P, gi * W + (h + 1) * P)
                state[hsl, :] = S0[h * P:(h + 1) * P, :] * t["ea_last"][h] + s_new[h * P:(h + 1) * P, :]
            y_ref[:, gsl] = y
            y2 = (y + t["D"] * x) * _silu(z_ref[:, gsl].astype(F32))
            r = lax.rsqrt(jnp.mean(y2 * y2, axis=-1, keepdims=True) + EPS)
            yn_ref[:, gsl] = (y2 * r * nw_ref[gi]).astype(BF16)

        _lockstep(group(gi) for gi in range(G))

    x_spec, b_spec, c_spec, z_spec, dt_spec, p_spec, nw_spec, s_spec = _ssd_specs(L, lambda c: c)
    y_spec = pl.BlockSpec((L, G * W), lambda g, c: (c, g))
    return pl.pallas_call(
        body, name=name, grid=(SSM_GROUPS // G, nc),
        in_specs=[x_spec, b_spec, c_spec, z_spec, dt_spec, p_spec, nw_spec],
        out_specs=(y_spec, y_spec, s_spec),
        out_shape=(jax.ShapeDtypeStruct((S, SSM_GROUPS * W), F32), jax.ShapeDtypeStruct((S, SSM_GROUPS * W), BF16),
                   jax.ShapeDtypeStruct((SSM_GROUPS, nc, W, N), F32)),
        scratch_shapes=[pltpu.VMEM((G * W, N), F32)],
        compiler_params=_params("parallel", "arbitrary"))(conv, conv, conv, proj, dt_raw, pvec, nw)


def _ssd_bwd(dyn, conv, proj, dt_raw, pvec, nw, y_ssd, states, d_proj, name):
    S = conv.shape[0]
    L, P, N, H, W, G = SSM_CHUNK, SSM_HEAD_DIM, SSM_D_STATE, SSM_HEADS_PER_GROUP, SSM_GROUP_WIDTH, SSM_GROUPS_PER_STEP
    nc = S // L

    def body(dyn_ref, x_ref, b_ref, c_ref, z_ref, dt_ref, p_ref, nw_ref, y_ref, s0_ref, _,
             dx_ref, db_ref, dc_ref, dz_ref, ddt_ref, dp_ref, dnw_ref, dstate):
        c = pl.program_id(1)

        @pl.when(c == 0)
        def _():
            dstate[...] = jnp.zeros_like(dstate)
            dp_ref[...] = jnp.zeros_like(dp_ref)
            dnw_ref[...] = jnp.zeros_like(dnw_ref)

        k = _SsdGroup(L)
        last = (_iota(L, 1, 0) == L - 1)

        def group(gi):
            gsl = slice(gi * W, (gi + 1) * W)
            Bm, Cm = b_ref[:, gi * N:(gi + 1) * N], c_ref[:, gi * N:(gi + 1) * N]
            x, z = x_ref[:, gsl].astype(F32), z_ref[:, gsl].astype(F32)
            S0, dS1 = s0_ref[gi, 0], dstate[gsl, :]
            CB = _dot(Cm, Bm, _NT)
            CBt = _dot(Bm, Cm, _NT)
            y_off_raw = _dot(Cm, S0, _NT)
            dXs_raw = _dot(Bm, dS1, _NT)
            t = yield from k.gates(dt_ref[gi], p_ref[gi])
            y1 = y_ref[:, gsl] + t["D"] * x
            sz = _silu(z)
            y2 = y1 * sz
            r = lax.rsqrt(jnp.mean(y2 * y2, axis=-1, keepdims=True) + EPS)
            y2h = y2 * r
            dyn_v = dyn_ref[:, gsl]
            dnw_ref[gi] += _colsum(dyn_v * y2h)
            dy2h = dyn_v * nw_ref[gi]
            dy2 = r * (dy2h - y2h * jnp.mean(dy2h * y2h, axis=-1, keepdims=True))
            dz_ref[:, gsl] = (dy2 * y1 * _dsilu(z)).astype(BF16)
            dY = dy2 * sz
            X = x * t["dt"]
            dYe = dY * t["eac"]
            dC_s = _dot(dYe, S0)
            dB_s = _dot(X * t["wdec"], dS1)
            dS_c = _dot(dYe, Cm, _TN)
            dXm, Gs, Gts = [], [], []
            for pair in range(H // 2):
                dYp, Xp = dY[:, pair * 128:(pair + 1) * 128], X[:, pair * 128:(pair + 1) * 128]
                dXm.append(jnp.where(k.lo, _dot(CBt * t["decay_t"][2 * pair], dYp),
                                     _dot(CBt * t["decay_t"][2 * pair + 1], dYp)))
                for mask in (k.lo, ~k.lo):
                    Gs.append(_dot(jnp.where(mask, dYp, 0.0), Xp, _NT))
                    Gts.append(_dot(jnp.where(mask, Xp, 0.0), dYp, _NT))
            yield
            dXs = dXs_raw * t["wdec"]
            dX = jnp.concatenate(dXm, axis=1) + dXs
            dCB, dCBt, q_sums = 0.0, 0.0, []
            for h in range(H):
                M, Mt = CB * t["decay"][h], CBt * t["decay_t"][h]
                dCB = dCB + Gs[h] * t["decay"][h]
                dCBt = dCBt + Gts[h] * t["decay_t"][h]
                d = Gs[h] * M - Gts[h] * Mt
                q_sums.append(sum(lax.dot_general(pt, k.ones, _NN, preferred_element_type=F32)
                                  for pt in _parts(d, 2)))
            q_f = jnp.concatenate([jnp.where(k.lo, q_sums[0], q_sums[1]), jnp.where(k.lo, q_sums[2], q_sums[3])],
                                  axis=1)
            x_dxs = k.head_sums(X * dXs)
            tot = [_total(dS1[h * P:(h + 1) * P, :] * S0[h * P:(h + 1) * P, :]) * t["ea_last"][h] for h in range(H)]
            tot_f = k.spread(jnp.concatenate(tot, axis=1))
            d_alast = _colsum(x_dxs) + tot_f
            dacum = q_f + k.head_sums(dY * (y_off_raw * t["eac"])) - x_dxs + jnp.where(last, d_alast, 0.0)
            dx_dt = k.head_sums(dX * x)
            d_skip = _colsum(k.head_sums(dY * x))
            for h in range(H):
                hsl = slice(gi * W + h * P, gi * W + (h + 1) * P)
                dstate[hsl, :] = t["ea_last"][h] * dS1[h * P:(h + 1) * P, :] + dS_c[h * P:(h + 1) * P, :]
            dc_s2 = _dot(dCB, Bm)
            db_s2 = _dot(dCBt, Cm)
            yield
            da = _sum_by(k.upper, dacum)
            yield
            ddt_raw = (da * t["A"] + dx_dt) * _sigmoid(t["dtr"])
            dx_ref[:, gsl] = (dX * t["dt"] + t["D"] * dY).astype(BF16)
            dc_ref[:, gi * N:(gi + 1) * N] = (dC_s + dc_s2).astype(BF16)
            db_ref[:, gi * N:(gi + 1) * N] = (dB_s + db_s2).astype(BF16)
            ddt_ref[gi] = k.gather4(ddt_raw)
            dp_ref[gi] += k.gather4(jnp.concatenate([_colsum(ddt_raw), _colsum(da * t["dt"]) * t["A"], d_skip],
                                                    axis=0))

        _lockstep(group(gi) for gi in range(G))

    rev = lambda c: nc - 1 - c
    x_spec, b_spec, c_spec, z_spec, dt_spec, p_spec, nw_spec, s_spec = _ssd_specs(L, rev)
    y_spec = pl.BlockSpec((L, G * W), lambda g, c: (rev(c), g))
    n_spec = pl.BlockSpec((L, G * N), lambda g, c: (rev(c), g))
    return pl.pallas_call(
        body, name=name, grid=(SSM_GROUPS // G, nc),
        in_specs=[y_spec, x_spec, b_spec, c_spec, z_spec, dt_spec, p_spec, nw_spec, y_spec, s_spec, D_PROJ_ANY],
        out_specs=(y_spec, n_spec, n_spec, z_spec, dt_spec, p_spec, nw_spec),
        out_shape=(jax.ShapeDtypeStruct((S, SSM_GROUPS * W), BF16), jax.ShapeDtypeStruct((S, SSM_GROUPS * N), BF16),
                   jax.ShapeDtypeStruct((S, SSM_GROUPS * N), BF16), jax.ShapeDtypeStruct(d_proj.shape, BF16),
                   jax.ShapeDtypeStruct((SSM_GROUPS, S, H), F32), jax.ShapeDtypeStruct((SSM_GROUPS, 3, H), F32),
                   jax.ShapeDtypeStruct((SSM_GROUPS, 1, W), F32)),
        scratch_shapes=[pltpu.VMEM((G * W, N), F32)], input_output_aliases={10: 3},
        compiler_params=_params("parallel", "arbitrary"))(dyn, conv, conv, conv, proj, dt_raw, pvec, nw, y_ssd, states,
                                                          d_proj)


def _unit_lower_inverse(A, ii, jj):
    eye = (ii == jj).astype(F32)
    same = (ii // GDN_INV_BLOCK) == (jj // GDN_INV_BLOCK)
    Ad = jnp.where(same, A, 0.0)
    Ao = A - Ad
    P2 = _dot3(Ad, Ad)
    yield
    P4, X = _dot(P2, P2), _dot3(eye - Ad, eye + P2)
    yield
    P8, X = _dot(P4, P4), X + _dot2(X, P4)
    yield
    X = X + _dot2(X, P8)
    yield
    Bm = _dot3(X, Ao)
    yield
    B2 = _dot3(Bm, Bm)
    yield
    Y = (eye - Bm) + B2 - _dot2(Bm, B2)
    yield
    T = _dot3(Y, X)
    yield
    return T


def _gdn_specs(L, order):
    G = GDN_QK_PER_STEP
    Hd, W = G * GDN_HEAD, G * GDN_V_PER_QK * GDN_HEAD
    q_spec = pl.BlockSpec((L, Hd), lambda h, c: (order(c), (C_QKV - C_XBC) // Hd + h))
    k_spec = pl.BlockSpec((L, Hd), lambda h, c: (order(c), (C_QKV - C_XBC + 1024) // Hd + h))
    v_spec = pl.BlockSpec((L, W), lambda h, c: (order(c), (C_QKV - C_XBC + 2048) // W + h))
    z_spec = pl.BlockSpec((L, W), lambda h, c: (order(c), C_ZG // W + h))
    ba_spec = pl.BlockSpec((G, L, GDN_V_PER_QK), lambda h, c: (h, order(c), 0))
    p_spec = pl.BlockSpec((G, 2, GDN_V_PER_QK), lambda h, c: (h, 0, 0))
    nw_spec = pl.BlockSpec((1, GDN_HEAD), lambda h, c: (0, 0))
    s_spec = pl.BlockSpec((G, 1, GDN_V_PER_QK * GDN_HEAD, GDN_HEAD), lambda h, c: (h, order(c), 0, 0))
    t_spec = pl.BlockSpec((G * GDN_V_PER_QK, 1, L, L), lambda h, c: (h, order(c), 0, 0))
    return q_spec, k_spec, v_spec, z_spec, ba_spec, p_spec, nw_spec, s_spec, t_spec


def _gdn_gates(qa, ka, b_col, a_col, p, j, ii, jj):
    L = qa.shape[0]
    sp_in = a_col + p[0:1, j:j + 1]
    neg_ea = -jnp.exp(p[1:2, j:j + 1])
    g = neg_ea * _softplus(sp_in)
    gcum, gcum_row = _cumsum_forms(g, ii, jj)
    rq = lax.rsqrt(_rowsum(qa * qa) + EPS)
    rk = lax.rsqrt(_rowsum(ka * ka) + EPS)
    q = qa * rq * (GDN_HEAD ** -0.5)
    k = ka * rk
    beta = _sigmoid(b_col)
    yield
    Dm = jnp.exp(jnp.where(ii >= jj, gcum - gcum_row, NEG_INF))
    eg = jnp.exp(gcum)
    g_last = gcum[L - 1:L, :]
    wdec = jnp.exp(g_last - gcum)
    return dict(rq=rq, rk=rk, q=q, k=k, beta=beta, sp_in=sp_in, neg_ea=neg_ea, g=g, Dm=Dm, kbeta=k * beta, eg=eg,
                g_last=g_last, wdec=wdec, kdec=k * wdec)


def _gdn_fwd(conv, proj, b_raw, a_raw, pvec, nw, name):
    S = conv.shape[0]
    L, Hd, J, G = GDN_CHUNK, GDN_HEAD, GDN_V_PER_QK, GDN_QK_PER_STEP
    W = J * Hd
    nc = S // L

    def body(q_ref, k_ref, v_ref, z_ref, b_ref, a_ref, p_ref, nw_ref, o_ref, on_ref, s0_ref, t_ref, state):
        c = pl.program_id(1)

        @pl.when(c == 0)
        def _():
            state[...] = jnp.zeros_like(state)

        ii, jj = _iota(L, L, 0), _iota(L, L, 1)
        for hq in range(G):
            s0_ref[hq, 0] = state[hq * W:(hq + 1) * W, :]

        def head(hq, j):
            hd = hq * J + j
            hsl, sl = slice(hq * Hd, (hq + 1) * Hd), slice(hd * Hd, (hd + 1) * Hd)
            t = yield from _gdn_gates(q_ref[:, hsl].astype(F32), k_ref[:, hsl].astype(F32), b_ref[hq][:, j:j + 1],
                                      a_ref[hq][:, j:j + 1],
                                      p_ref[hq], j, ii, jj)
            KK = _dot(t["kbeta"], t["k"], _NT)
            QK = _dot(t["q"], t["k"], _NT)
            yield
            T = yield from _unit_lower_inverse(jnp.where(ii > jj, KK * t["Dm"], 0.0), ii, jj)
            t_ref[hd, 0] = T
            S0 = state[sl, :]
            U = _dot2(T, v_ref[:, sl].astype(F32) * t["beta"])
            Wm = _dot2(T, t["kbeta"] * t["eg"])
            o_inter = _dot(t["q"] * t["eg"], S0)
            yield
            Vn = U - _dot(Wm, S0)
            yield
            o = o_inter + _dot(QK * t["Dm"], Vn)
            s_new = _dot(t["kdec"], Vn, _TN)
            yield
            state[sl, :] = S0 * jnp.exp(t["g_last"]) + s_new
            o_ref[:, sl] = o
            r = lax.rsqrt(jnp.mean(o * o, axis=-1, keepdims=True) + EPS)
            on_ref[:, sl] = ((o * r * nw_ref[...]) * _silu(z_ref[:, sl].astype(F32))).astype(BF16)

        _lockstep(head(hq, j) for hq in range(G) for j in range(J))

    q_spec, k_spec, v_spec, z_spec, ba_spec, p_spec, nw_spec, s_spec, t_spec = _gdn_specs(L, lambda c: c)
    o_spec = pl.BlockSpec((L, G * W), lambda h, c: (c, h))
    return pl.pallas_call(
        body, name=name, grid=(GDN_QK_HEADS // G, nc),
        in_specs=[q_spec, k_spec, v_spec, z_spec, ba_spec, ba_spec, p_spec, nw_spec],
        out_specs=(o_spec, o_spec, s_spec, t_spec),
        out_shape=(jax.ShapeDtypeStruct((S, GDN_QK_HEADS * W), F32), jax.ShapeDtypeStruct((S, GDN_QK_HEADS * W), BF16),
                   jax.ShapeDtypeStruct((GDN_QK_HEADS, nc, W, Hd), F32),
                   jax.ShapeDtypeStruct((GDN_QK_HEADS * J, nc, L, L), F32)),
        scratch_shapes=[pltpu.VMEM((G * W, Hd), F32)],
        compiler_params=_params("parallel", "arbitrary"))(conv, conv, conv, proj, b_raw, a_raw, pvec, nw)


def _gdn_bwd(don, conv, proj, b_raw, a_raw, pvec, nw, o_pre, states, t_inv, d_proj, name):
    S = conv.shape[0]
    L, Hd, J, G = GDN_CHUNK, GDN_HEAD, GDN_V_PER_QK, GDN_QK_PER_STEP
    W = J * Hd
    nc = S // L

    def body(don_ref, q_ref, k_ref, v_ref, z_ref, b_ref, a_ref, p_ref, nw_ref, o_ref, s0_ref, t_ref, _,
             dq_ref, dk_ref, dv_ref, dz_ref, db_ref, da_ref, dp_ref, dnw_ref, dstate):
        c = pl.program_id(1)

        @pl.when(c == 0)
        def _():
            dstate[...] = jnp.zeros_like(dstate)
            dp_ref[...] = jnp.zeros_like(dp_ref)
            dnw_ref[...] = jnp.zeros_like(dnw_ref)

        ii, jj = _iota(L, L, 0), _iota(L, L, 1)
        last = (_iota(L, 1, 0) == L - 1)
        res = {}

        def head(hq, j):
            hd = hq * J + j
            hsl, sl = slice(hq * Hd, (hq + 1) * Hd), slice(hd * Hd, (hd + 1) * Hd)
            qa, ka = q_ref[:, hsl].astype(F32), k_ref[:, hsl].astype(F32)
            t = yield from _gdn_gates(qa, ka, b_ref[hq][:, j:j + 1], a_ref[hq][:, j:j + 1], p_ref[hq], j, ii, jj)
            q, k, beta, eg, Dm, kbeta, kdec = (t[nm] for nm in ("q", "k", "beta", "eg", "Dm", "kbeta", "kdec"))
            T = t_ref[hd, 0]
            v, z, o = v_ref[:, sl].astype(F32), z_ref[:, sl].astype(F32), o_ref[:, sl]
            S0, dS1 = s0_ref[hq, 0, j * Hd:(j + 1) * Hd, :], dstate[sl, :]
            sz = _silu(z)
            r = lax.rsqrt(jnp.mean(o * o, axis=-1, keepdims=True) + EPS)
            oh = o * r
            d_on = don_ref[:, sl]
            dz_ref[:, sl] = (d_on * (oh * nw_ref[...]) * _dsilu(z)).astype(BF16)
            dn = d_on * sz
            dnw_part = _colsum(dn * oh)
            doh = dn * nw_ref[...]
            dO = r * (doh - oh * jnp.mean(doh * oh, axis=-1, keepdims=True))
            Rw = kbeta * eg
            qe = q * eg
            U = _dot2(T, v * beta)
            Wm = _dot2(T, Rw)
            KK = _dot(kbeta, k, _NT)
            QK = _dot(q, k, _NT)
            o_inter = _dot(qe, S0)
            dq_s = _dot(dO, S0, _NT)
            dS_q = _dot(qe, dO, _TN)
            yield
            Am = jnp.where(ii > jj, KK * Dm, 0.0)
            Pm = QK * Dm
            Vn = U - _dot(Wm, S0)
            dVn_s = _dot(kdec, dS1)
            yield
            dVn = _dot(Pm, dO, _TN) + dVn_s
            dP = _dot(dO, Vn, _NT)
            dKd = _dot(Vn, dS1, _NT)
            yield
            dQK = dP * Dm
            dq = _dot(dQK, k) + dq_s * eg
            dk = _dot(dQK, q, _TN) + dKd * t["wdec"]
            dstate[sl, :] = jnp.exp(t["g_last"]) * dS1 + dS_q - _dot(Wm, dVn, _TN)
            dW = -_dot(dVn, S0, _NT)
            dRu = _dot2(T, dVn, _TN)
            yield
            dRw = _dot2(T, dW, _TN)
            dA_u = _dot(dRu, U, _NT)
            yield
            dA = jnp.where(ii > jj, -(dA_u + _dot(dRw, Wm, _NT)), 0.0)
            yield
            dKK = dA * Dm
            dkbeta = _dot(dKK, k) + dRw * eg
            dk = dk + _dot(dKK, kbeta, _TN)
            yield
            dk = dk + dkbeta * beta
            dbeta = _rowsum(dkbeta * k) + _rowsum(dRu * v)
            dv_ref[:, sl] = (dRu * beta).astype(BF16)
            Q = dA * Am + dP * Pm
            rho = _rowsum(dKd * kdec)
            d_glast = _colsum(rho) + jnp.exp(t["g_last"]) * _total(dS1 * S0)
            q_sums = _row_col_sums(Q)
            rest = _rowsum(dRw * Rw) + _rowsum(dO * o_inter) - rho + jnp.where(last, d_glast, 0.0)
            yield
            dg = _rev_cumsum_col(q_sums + rest, ii, jj)
            yield
            da_raw = dg * t["neg_ea"] * _sigmoid(t["sp_in"])
            res[hq, j] = dict(dq=dq, dk=dk, db=dbeta * beta * (1.0 - beta), da=da_raw, d_bias=_colsum(da_raw),
                              d_alog=_colsum(dg * t["g"]), dnw=dnw_part, rq=t["rq"], rk=t["rk"], k=k, qh=qa * t["rq"])

        _lockstep(head(hq, j) for hq in range(G) for j in range(J))
        for hq in range(G):
            parts = [res[hq, j] for j in range(J)]
            hsl = slice(hq * Hd, (hq + 1) * Hd)
            p0 = parts[0]
            dqh = sum(pt["dq"] for pt in parts) * (GDN_HEAD ** -0.5)
            dkn = sum(pt["dk"] for pt in parts)
            dq_ref[:, hsl] = (p0["rq"] * (dqh - p0["qh"] * _rowsum(dqh * p0["qh"]))).astype(BF16)
            dk_ref[:, hsl] = (p0["rk"] * (dkn - p0["k"] * _rowsum(dkn * p0["k"]))).astype(BF16)
            db_ref[hq] = jnp.concatenate([pt["db"] for pt in parts], axis=1)
            da_ref[hq] = jnp.concatenate([pt["da"] for pt in parts], axis=1)
            dp_ref[hq] += jnp.concatenate([jnp.concatenate([pt["d_bias"] for pt in parts], axis=1),
                                           jnp.concatenate([pt["d_alog"] for pt in parts], axis=1)], axis=0)
            dnw_ref[hq] += sum(pt["dnw"] for pt in parts)

    rev = lambda c: nc - 1 - c
    q_spec, k_spec, v_spec, z_spec, ba_spec, p_spec, nw_spec, s_spec, t_spec = _gdn_specs(L, rev)
    o_spec = pl.BlockSpec((L, G * W), lambda h, c: (rev(c), h))
    h_spec = pl.BlockSpec((L, G * Hd), lambda h, c: (rev(c), h))
    dnw_spec = pl.BlockSpec((G, 1, Hd), lambda h, c: (h, 0, 0))
    return pl.pallas_call(
        body, name=name, grid=(GDN_QK_HEADS // G, nc),
        in_specs=[o_spec, q_spec, k_spec, v_spec, z_spec, ba_spec, ba_spec, p_spec, nw_spec, o_spec, s_spec, t_spec,
                  D_PROJ_ANY],
        out_specs=(h_spec, h_spec, o_spec, z_spec, ba_spec, ba_spec, p_spec, dnw_spec),
        out_shape=(jax.ShapeDtypeStruct((S, GDN_QK_HEADS * Hd), BF16), jax.ShapeDtypeStruct((S, GDN_QK_HEADS * Hd), BF16),
                   jax.ShapeDtypeStruct((S, GDN_QK_HEADS * W), BF16), jax.ShapeDtypeStruct(d_proj.shape, BF16),
                   jax.ShapeDtypeStruct((GDN_QK_HEADS, S, J), F32), jax.ShapeDtypeStruct((GDN_QK_HEADS, S, J), F32),
                   jax.ShapeDtypeStruct((GDN_QK_HEADS, 2, J), F32), jax.ShapeDtypeStruct((GDN_QK_HEADS, 1, Hd), F32)),
        scratch_shapes=[pltpu.VMEM((G * W, Hd), F32)], input_output_aliases={12: 3},
        compiler_params=_params("parallel", "arbitrary"))(don, conv, conv, conv, proj, b_raw, a_raw, pvec, nw, o_pre,
                                                          states, t_inv, d_proj)


def _ada_fwd(c_all, w_loc, b_loc, name):
    n = w_loc.shape[1]

    def body(c_ref, w_ref, b_ref, o_ref):
        o_ref[...] = _dot3(_silu(c_ref[...]), w_ref[...]) + b_ref[...]

    return pl.pallas_call(body, name=name, out_shape=jax.ShapeDtypeStruct((N_DEV, n), F32),
                          compiler_params=pltpu.CompilerParams(vmem_limit_bytes=VMEM_LIMIT))(c_all, w_loc, b_loc)


def _ada_bwd(c_all_t, dmod_cols, name):
    Dm, n = c_all_t.shape[0], dmod_cols.shape[1]

    def body(c_ref, d_ref, o_ref):
        ca = _silu(c_ref[...])
        acc = ca[:, 0:1] * d_ref[0:1, :]
        for i in range(1, N_DEV):
            acc = acc + ca[:, i:i + 1] * d_ref[i:i + 1, :]
        o_ref[...] = acc

    return pl.pallas_call(body, name=name, out_shape=jax.ShapeDtypeStruct((Dm, n), F32),
                          compiler_params=pltpu.CompilerParams(vmem_limit_bytes=VMEM_LIMIT))(c_all_t, dmod_cols)


ADAM_BLOCK_BYTES = 12 * 1024 * 1024


def _adam(contrib, w, m, v, name):
    n, R, C = contrib.shape
    tr = R
    while tr % 16 == 0 and (n + 7) * tr * C * 4 > ADAM_BLOCK_BYTES:
        tr //= 2

    def body(c_ref, w_ref, m_ref, v_ref, g_ref, d_ref, nm_ref, nv_ref):
        g = c_ref[0].astype(F32)
        for i in range(1, n):
            g = g + c_ref[i].astype(F32)
        nm = ADAM_B1 * m_ref[...] + (1.0 - ADAM_B1) * g
        nv = ADAM_B2 * v_ref[...] + (1.0 - ADAM_B2) * (g * g)
        m_hat = nm / (1.0 - ADAM_B1 ** ADAM_STEP)
        v_hat = nv / (1.0 - ADAM_B2 ** ADAM_STEP)
        g_ref[...] = g
        d_ref[...] = -ADAM_LR * (m_hat / (jnp.sqrt(v_hat) + ADAM_EPS) + ADAM_WD * w_ref[...])
        nm_ref[...] = nm
        nv_ref[...] = nv

    spec = pl.BlockSpec((tr, C), lambda i: (i, 0))
    shp = jax.ShapeDtypeStruct((R, C), F32)
    return pl.pallas_call(
        body, name=name, grid=(R // tr,), in_specs=[pl.BlockSpec((n, tr, C), lambda i: (0, i, 0)), spec, spec, spec],
        out_specs=(spec,) * 4, out_shape=(shp,) * 4, compiler_params=_params("parallel"))(contrib, w, m, v)


def _exchange(arrays, modes, name, chips=False):
    n = len(arrays)
    out_shape = tuple(jax.ShapeDtypeStruct((N_DEV,) + a.shape if md == "gather" else a.shape, a.dtype)
                      for a, md in zip(arrays, modes))

    def body(*refs):
        ins, outs = refs[:n], refs[n:2 * n]
        send_sems, recv_sems, loc_sems = refs[2 * n:]
        me, peers = _peer_table(chips)

        def src(k, slot):
            return ins[k] if modes[k] == "gather" else ins[k].at[slot]

        def remote(k, m, to_slot, land_slot):
            return pltpu.make_async_remote_copy(
                src_ref=src(k, to_slot), dst_ref=outs[k].at[land_slot], send_sem=send_sems.at[k, m],
                recv_sem=recv_sems.at[k, m], device_id=peers[m][0], device_id_type=pl.DeviceIdType.MESH)

        local = [pltpu.make_async_copy(src(k, me), outs[k].at[me], loc_sems.at[k]) for k in range(n)]
        for cp in local:
            cp.start()
        sends = [remote(k, m, peers[m][1], me) for m in range(len(peers)) for k in range(n)]
        for cp in sends:
            cp.start()
        for m in range(len(peers)):
            for k in range(n):
                remote(k, m, peers[m][1], peers[m][1]).wait_recv()
        for cp in sends:
            cp.wait_send()
        for cp in local:
            cp.wait()

    any_spec = pl.BlockSpec(memory_space=pl.ANY)
    return pl.pallas_call(
        body, name=name, in_specs=[any_spec] * n, out_specs=(any_spec,) * n, out_shape=out_shape,
        scratch_shapes=[pltpu.SemaphoreType.DMA((n, N_DEV - 1)), pltpu.SemaphoreType.DMA((n, N_DEV - 1)),
                        pltpu.SemaphoreType.DMA((n,))])(*arrays)


def _gather_two_level(arrays, name):
    n = len(arrays)
    out_shape = tuple(jax.ShapeDtypeStruct((N_DEV,) + a.shape, a.dtype) for a in arrays)

    def body(*refs):
        ins, outs = refs[:n], refs[n:2 * n]
        send_sems, recv_sems, loc_sems = refs[2 * n:]
        ix, iy, ic = lax.axis_index("x"), lax.axis_index("y"), lax.axis_index("c")
        lin = lambda px, py, pc: 4 * px + 2 * py + pc
        me, sib = lin(ix, iy, ic), (ix, iy, 1 - ic)
        chips = [(1 - ix, iy), (ix, 1 - iy), (1 - ix, 1 - iy)]

        def copy(k, s, block, to, src=None):
            return pltpu.make_async_remote_copy(
                src_ref=outs[k].at[block] if src is None else src, dst_ref=outs[k].at[block],
                send_sem=send_sems.at[k, s], recv_sem=recv_sems.at[k, s], device_id=to,
                device_id_type=pl.DeviceIdType.MESH)

        local = [pltpu.make_async_copy(ins[k], outs[k].at[me], loc_sems.at[k]) for k in range(n)]
        for cp in local:
            cp.start()
        first = [copy(k, 1 + j, me, (cx, cy, ic), src=ins[k]) for j, (cx, cy) in enumerate(chips) for k in range(n)]
        first += [copy(k, 0, me, sib, src=ins[k]) for k in range(n)]
        for cp in first:
            cp.start()
        passed = []
        for j, (cx, cy) in enumerate(chips):
            for k in range(n):
                copy(k, 1 + j, lin(cx, cy, ic), sib).wait_recv()
                passed.append(copy(k, 4 + j, lin(cx, cy, ic), sib))
                passed[-1].start()
        for k in range(n):
            copy(k, 0, lin(*sib), sib).wait_recv()
            for j, (cx, cy) in enumerate(chips):
                copy(k, 4 + j, lin(cx, cy, 1 - ic), sib).wait_recv()
        for cp in first + passed:
            cp.wait_send()
        for cp in local:
            cp.wait()

    any_spec = pl.BlockSpec(memory_space=pl.ANY)
    return pl.pallas_call(
        body, name=name, in_specs=[any_spec] * n, out_specs=(any_spec,) * n, out_shape=out_shape,
        scratch_shapes=[pltpu.SemaphoreType.DMA((n, N_DEV - 1)), pltpu.SemaphoreType.DMA((n, N_DEV - 1)),
                        pltpu.SemaphoreType.DMA((n,))])(*arrays)


def _peer_table(chips=False):
    ix, iy, ic = lax.axis_index("x"), lax.axis_index("y"), lax.axis_index("c")
    peers = []
    for m in ((2, 4, 6) if chips else range(1, N_DEV)):
        px = 1 - ix if m & 4 else ix
        py = 1 - iy if m & 2 else iy
        pc = 1 - ic if m & 1 else ic
        peers.append(((px, py, pc), 2 * px + py if chips else 4 * px + 2 * py + pc))
    return (2 * ix + iy if chips else 4 * ix + 2 * iy + ic), peers


def _exchange_start(arrays, modes, after, name, chips=False):
    n = len(arrays)
    land_shapes = [(N_DEV,) + a.shape if md == "gather" else a.shape for a, md in zip(arrays, modes)]

    def body(*refs):
        ins, lands = refs[:n], refs[n:2 * n]
        send_sems, recv_sems = refs[2 * n + 1], refs[2 * n + 2]
        token = refs[-1]
        me, peers = _peer_table(chips)

        def src(k, slot):
            return ins[k] if modes[k] == "gather" else ins[k].at[slot]

        for peer, slot in peers:
            for k in range(n):
                pltpu.make_async_remote_copy(
                    src_ref=src(k, slot), dst_ref=lands[k].at[me], send_sem=send_sems, recv_sem=recv_sems,
                    device_id=peer, device_id_type=pl.DeviceIdType.MESH).start()
        token[...] = jnp.zeros_like(token)

    hbm = pl.BlockSpec(memory_space=pltpu.HBM)
    sem = pl.BlockSpec(memory_space=pltpu.SEMAPHORE)
    sem_shape = pltpu.SemaphoreType.DMA(())
    operands = [pltpu.with_memory_space_constraint(a, pltpu.HBM) for a in arrays]
    operands += [pltpu.with_memory_space_constraint(lax.empty(s, a.dtype), pltpu.HBM)
                 for s, a in zip(land_shapes, arrays)]
    out = pl.pallas_call(
        body, name=name,
        out_shape=(sem_shape, sem_shape) + tuple(pltpu.HBM(a.shape, a.dtype) for a in arrays)
        + tuple(pltpu.HBM(s, a.dtype) for s, a in zip(land_shapes, arrays)) + (jax.ShapeDtypeStruct((8, 128), F32),),
        in_specs=[hbm] * (2 * n) + [pl.BlockSpec(memory_space=pl.ANY)],
        out_specs=(sem, sem) + (hbm,) * (2 * n) + (pl.BlockSpec(memory_space=pltpu.VMEM),),
        input_output_aliases={i: 2 + i for i in range(2 * n)},
        compiler_params=pltpu.CompilerParams(has_side_effects=pltpu.SideEffectType.DATAFLOW_SIDE_EFFECTING))(
            *operands, after)
    return out[0], out[1], out[2:2 + n], out[2 + n:2 + 2 * n], out[-1]


def _exchange_wait(started, modes, after, name, chips=False):
    send_sems, recv_sems, sent, lands, _ = started
    n = len(sent)

    def body(*refs):
        ins, zones = refs[:n], refs[n:2 * n]
        send_ref, recv_ref = refs[2 * n], refs[2 * n + 1]
        _, peers = _peer_table(chips)

        def src(k, slot):
            return ins[k] if modes[k] == "gather" else ins[k].at[slot]

        for peer, slot in peers:
            for k in range(n):
                cp = pltpu.make_async_remote_copy(
                    src_ref=src(k, slot), dst_ref=zones[k].at[slot], send_sem=send_ref, recv_sem=recv_ref,
                    device_id=peer, device_id_type=pl.DeviceIdType.MESH)
                cp.wait_send()
                cp.wait_recv()

    hbm = pl.BlockSpec(memory_space=pltpu.HBM)
    sem = pl.BlockSpec(memory_space=pltpu.SEMAPHORE)
    out = pl.pallas_call(
        body, name=name,
        out_shape=tuple(pltpu.HBM(a.shape, a.dtype) for a in sent) + tuple(pltpu.HBM(a.shape, a.dtype) for a in lands),
        in_specs=[hbm] * (2 * n) + [sem, sem, pl.BlockSpec(memory_space=pl.ANY)], out_specs=(hbm,) * (2 * n),
        input_output_aliases={i: i for i in range(2 * n)},
        compiler_params=pltpu.CompilerParams(has_side_effects=pltpu.SideEffectType.DATAFLOW_SIDE_EFFECTING))(
            *sent, *lands, send_sems, recv_sems, after)
    ix, iy, ic = lax.axis_index("x"), lax.axis_index("y"), lax.axis_index("c")
    me = 2 * ix + iy if chips else 4 * ix + 2 * iy + ic
    filled = []
    for k in range(n):
        own = sent[k] if modes[k] == "gather" else lax.dynamic_index_in_dim(sent[k], me, axis=0, keepdims=False)
        filled.append(lax.dynamic_update_index_in_dim(out[n + k], own, me, axis=0))
    return filled


def _swap_sibling(to_c0, to_c1, name):
    def body(c0_ref, c1_ref, out_ref, send_sem, recv_sem):
        ix, iy, ic = lax.axis_index("x"), lax.axis_index("y"), lax.axis_index("c")

        def copy(src):
            return pltpu.make_async_remote_copy(src_ref=src, dst_ref=out_ref, send_sem=send_sem, recv_sem=recv_sem,
                                                device_id=(ix, iy, 1 - ic), device_id_type=pl.DeviceIdType.MESH)

        @pl.when(ic == 0)
        def _():
            copy(c1_ref).start()

        @pl.when(ic == 1)
        def _():
            copy(c0_ref).start()

        copy(c0_ref).wait()

    any_spec = pl.BlockSpec(memory_space=pl.ANY)
    return pl.pallas_call(body, name=name, in_specs=[any_spec, any_spec], out_specs=any_spec,
                          out_shape=jax.ShapeDtypeStruct(to_c0.shape, to_c0.dtype),
                          scratch_shapes=[pltpu.SemaphoreType.DMA, pltpu.SemaphoreType.DMA])(to_c0, to_c1)


def _add_pair(to_c0, to_c1, got, name):
    n, R, C = got.shape
    tr = _blk(R, 256)

    def body(c0_ref, c1_ref, got_ref, o_ref):
        ic = lax.axis_index("c")

        @pl.when(ic == 0)
        def _():
            o_ref[...] = (c0_ref[...].astype(F32) + got_ref[...].astype(F32)).astype(o_ref.dtype)

        @pl.when(ic == 1)
        def _():
            o_ref[...] = (c1_ref[...].astype(F32) + got_ref[...].astype(F32)).astype(o_ref.dtype)

    spec = pl.BlockSpec((1, tr, C), lambda i, j: (i, j, 0))
    return pl.pallas_call(body, name=name, grid=(n, R // tr), in_specs=[spec, spec, spec], out_specs=spec,
                          out_shape=jax.ShapeDtypeStruct(got.shape, got.dtype),
                          compiler_params=_params("parallel", "parallel"))(to_c0, to_c1, got)


W_IN_SPLITS = (0, 2048, 6144, 6176, 10272, 12320, 12336, 12352, 13376, 14400)
N_REPLICATED = 16640
REPLICATED = ("b_ada", "norm_mix_pre", "norm_mix_post", "ssm_conv_b", "ssm_dt_bias", "ssm_A_log", "ssm_D",
              "ssm_norm_w", "gdn_dt_bias", "gdn_A_log", "gdn_norm_w", "norm_mlp_pre", "norm_mlp_post")
WEIGHTS = ("w_ada", "b_ada", "norm_mix_pre", "norm_mix_post", "w_in", "ssm_conv_w", "ssm_conv_b", "ssm_dt_bias",
           "ssm_A_log", "ssm_D", "ssm_norm_w", "gdn_conv_w", "gdn_dt_bias", "gdn_A_log", "gdn_norm_w", "w_ssm_up",
           "w_gdn_up", "w_out", "norm_mlp_pre", "norm_mlp_post", "w_mlp_up", "w_mlp_down")


def _cols_of_shards(g, a, b):
    width, pieces = g.shape[2], []
    while a < b:
        i = a // width
        hi = min(b, (i + 1) * width)
        pieces.append(g[i][:, a - i * width:hi - i * width])
        a = hi
    return pieces


ORIG_SEGMENTS = ((0, 6144, "main", 0), (6144, 6176, "small", 0), (6176, 12320, "main", 6144),
                 (12320, 12352, "small", 32), (12352, 14400, "main", 12288))


def _orig_cols(main_cols, small_cols, a, b):
    pieces = []
    for s0, s1, which, off in ORIG_SEGMENTS:
        lo, hi = max(a, s0), min(b, s1)
        if lo < hi:
            pieces.append((main_cols if which == "main" else small_cols)[:, off + lo - s0:off + hi - s0])
    return jnp.concatenate(pieces, axis=1)


def _by_cols(t):
    return t.transpose(1, 0, 2).reshape(t.shape[1], N_DEV * t.shape[2])


def _to_col_shards(t):
    R, C8 = t.shape
    return t.reshape(R, N_DEV, C8 // N_DEV).transpose(1, 0, 2)


def _heads_first(t, groups):
    S = t.shape[0]
    return t.reshape(S, groups, t.shape[1] // groups).transpose(1, 0, 2)


def _heads_last(t):
    return t.transpose(1, 0, 2).reshape(t.shape[1], t.shape[0] * t.shape[2])


def kernel(x, c, w_ada, b_ada, norm_mix_pre, norm_mix_post, w_in, ssm_conv_w, ssm_conv_b, ssm_dt_bias, ssm_A_log, ssm_D, ssm_norm_w, gdn_conv_w, gdn_dt_bias, gdn_A_log, gdn_norm_w, w_ssm_up, w_gdn_up, w_out, norm_mlp_pre, norm_mlp_post, w_mlp_up, w_mlp_down, loss_target, m_w_ada, m_b_ada, m_norm_mix_pre, m_norm_mix_post, m_w_in, m_ssm_conv_w, m_ssm_conv_b, m_ssm_dt_bias, m_ssm_A_log, m_ssm_D, m_ssm_norm_w, m_gdn_conv_w, m_gdn_dt_bias, m_gdn_A_log, m_gdn_norm_w, m_w_ssm_up, m_w_gdn_up, m_w_out, m_norm_mlp_pre, m_norm_mlp_post, m_w_mlp_up, m_w_mlp_down, v_w_ada, v_b_ada, v_norm_mix_pre, v_norm_mix_post, v_w_in, v_ssm_conv_w, v_ssm_conv_b, v_ssm_dt_bias, v_ssm_A_log, v_ssm_D, v_ssm_norm_w, v_gdn_conv_w, v_gdn_dt_bias, v_gdn_A_log, v_gdn_norm_w, v_w_ssm_up, v_w_gdn_up, v_w_out, v_norm_mlp_pre, v_norm_mlp_post, v_w_mlp_up, v_w_mlp_down):
    S, Dm = x.shape[1], D_MODEL
    me = 4 * lax.axis_index("x") + 2 * lax.axis_index("y") + lax.axis_index("c")
    x2, tgt = x[0], loss_target[0]
    n_ada = w_ada.shape[2]
    given = dict(
        w_ada=(w_ada, m_w_ada, v_w_ada), b_ada=(b_ada, m_b_ada, v_b_ada),
        norm_mix_pre=(norm_mix_pre, m_norm_mix_pre, v_norm_mix_pre),
        norm_mix_post=(norm_mix_post, m_norm_mix_post, v_norm_mix_post), w_in=(w_in, m_w_in, v_w_in),
        ssm_conv_w=(ssm_conv_w, m_ssm_conv_w, v_ssm_conv_w), ssm_conv_b=(ssm_conv_b, m_ssm_conv_b, v_ssm_conv_b),
        ssm_dt_bias=(ssm_dt_bias, m_ssm_dt_bias, v_ssm_dt_bias), ssm_A_log=(ssm_A_log, m_ssm_A_log, v_ssm_A_log),
        ssm_D=(ssm_D, m_ssm_D, v_ssm_D), ssm_norm_w=(ssm_norm_w, m_ssm_norm_w, v_ssm_norm_w),
        gdn_conv_w=(gdn_conv_w, m_gdn_conv_w, v_gdn_conv_w), gdn_dt_bias=(gdn_dt_bias, m_gdn_dt_bias, v_gdn_dt_bias),
        gdn_A_log=(gdn_A_log, m_gdn_A_log, v_gdn_A_log), gdn_norm_w=(gdn_norm_w, m_gdn_norm_w, v_gdn_norm_w),
        w_ssm_up=(w_ssm_up, m_w_ssm_up, v_w_ssm_up), w_gdn_up=(w_gdn_up, m_w_gdn_up, v_w_gdn_up),
        w_out=(w_out, m_w_out, v_w_out), norm_mlp_pre=(norm_mlp_pre, m_norm_mlp_pre, v_norm_mlp_pre),
        norm_mlp_post=(norm_mlp_post, m_norm_mlp_post, v_norm_mlp_post), w_mlp_up=(w_mlp_up, m_w_mlp_up, v_w_mlp_up),
        w_mlp_down=(w_mlp_down, m_w_mlp_down, v_w_mlp_down))

    (c_all, scw, gcw, g_in) = _gather_two_level([c, ssm_conv_w[0], gdn_conv_w[0], w_in[0].astype(BF16)], "gather_w_in")
    c_all = c_all.reshape(N_DEV, Dm)
    sp = W_IN_SPLITS
    w_main = jnp.concatenate(_cols_of_shards(g_in, sp[0], sp[2]) + _cols_of_shards(g_in, sp[3], sp[5])
                             + _cols_of_shards(g_in, sp[7], sp[9]), axis=1)
    w_small = jnp.concatenate(_cols_of_shards(g_in, sp[2], sp[3]) + _cols_of_shards(g_in, sp[5], sp[7])
                              + [jnp.zeros((Dm, N_SMALL - 64), BF16)], axis=1)
    conv_w = jnp.concatenate([_by_cols(scw), _by_cols(gcw)], axis=1)
    conv_b = jnp.concatenate([ssm_conv_b, jnp.zeros_like(ssm_conv_b)], axis=1)

    b_loc = lax.dynamic_slice(b_ada, (0, me * n_ada), (1, n_ada))
    mod_part = _ada_fwd(c_all, w_ada[0], b_loc, "ada_fwd")
    (mod_rows,) = _exchange([mod_part.reshape(N_DEV, 1, n_ada)], ["a2a"], "exchange_mod")
    rest = _exchange_start([w_ssm_up[0].astype(BF16), w_gdn_up[0].astype(BF16), w_out[0].astype(BF16),
                            w_mlp_up[0].astype(BF16), w_mlp_down[0].astype(BF16)], ["gather"] * 5, mod_rows,
                           "gather_rest_start")
    mod = mod_rows.reshape(1, 6 * Dm) + rest[4][0:1, 0:1]
    sh1, sc1, g1, sh2, sc2, g2 = [mod[:, i * Dm:(i + 1) * Dm] for i in range(6)]

    h = _pre_fwd(x2, norm_mix_pre, sc1, sh1, "pre_mix")
    proj = _mm(h, w_main, S, N_MAIN, Dm, mode="nn", out_dtype=BF16, name="proj_main")
    small = _mm(h, w_small, S, N_SMALL, Dm, mode="nn", out_dtype=F32, name="proj_small")
    conv = _conv_fwd(proj, conv_w, conv_b, "conv_fwd")
    dt_g, b_g, a_g = _heads_first(small[:, 0:32], 8), _heads_first(small[:, 32:48], 8), _heads_first(small[:, 48:64], 8)
    pv_ssm = jnp.stack([ssm_dt_bias.reshape(8, 4), ssm_A_log.reshape(8, 4), ssm_D.reshape(8, 4)], axis=1)
    nw_ssm = ssm_norm_w.reshape(8, 1, SSM_GROUP_WIDTH)
    pv_gdn = jnp.stack([gdn_dt_bias.reshape(8, 2), gdn_A_log.reshape(8, 2)], axis=1)
    y_ssd, ysn, st_ssm = _ssd_fwd(conv, proj, dt_g, pv_ssm, nw_ssm, "ssd_fwd")
    o_pre, ogn, st_gdn, t_inv = _gdn_fwd(conv, proj, b_g, a_g, pv_gdn, gdn_norm_w, "gdn_fwd")
    g_su, g_gu, g_out, g_mu, g_md = _exchange_wait(rest, ["gather"] * 5, ogn, "gather_rest_wait")
    w_su, w_gu = g_su.reshape(2 * Dm, Dm), g_gu.reshape(2 * Dm, Dm)
    w_o, w_mu, w_md = g_out.reshape(Dm, Dm), _by_cols(g_mu), g_md.reshape(4 * Dm, Dm)
    ys = _mm(ysn, w_su, S, Dm, 2 * Dm, mode="nn", out_dtype=F32, name="ssm_up")
    yg, merged = _gdn_up_merge(ogn, w_gu, ys, proj, "gdn_up_merge")
    mo, x1, h2 = _mix_out_post_pre(merged, w_o, x2, norm_mix_post, g1, norm_mlp_pre, sc2, sh2, "mix_out_post_pre")
    u, act = _mm(h2, w_mu, S, 4 * Dm, Dm, mode="nn", out_dtype=F32, epi="relu2", name="mlp_up")
    y_mlp = _mm(act, w_md, S, Dm, 4 * Dm, mode="nn", out_dtype=F32, name="mlp_down")
    dx2, loss_loc, dy, dg2, dw_post2 = _final_fwd_bwd(x1, y_mlp, norm_mlp_post, g2, tgt, "post_mlp_loss_bwd")

    du = _mm(dy, w_md, S, 4 * Dm, Dm, mode="nt", out_dtype=BF16, epi="drelu2", extra=u, name="mlp_down_dx")
    gw_md = _mm(act, dy, 4 * Dm, Dm, S, mode="tn", out_dtype=BF16, name="mlp_down_dw")
    dh2 = _mm(du, w_mu, S, Dm, 4 * Dm, mode="nt", out_dtype=F32, name="mlp_up_dx")
    gw_mu = _mm(h2, du, Dm, 4 * Dm, S, mode="tn", out_dtype=BF16, name="mlp_up_dw")
    mlp_x = _exchange_start([_to_col_shards(gw_mu), gw_md.reshape(N_DEV, -1, Dm)], ["a2a"] * 2, gw_md,
                            "grads_mlp_start")
    dx1, dsh2, dsc2, dw_pre2 = _pre_bwd(dh2, x1, norm_mlp_pre, sc2 + mlp_x[4][0:1, 0:1], dx2, "pre_mlp_bwd")
    dmo, dg1, dw_post1 = _post_bwd(dx1, mo, norm_mix_post, g1, "post_mix_bwd")
    gw_o = _mm(merged, dmo, Dm, Dm, S, mode="tn", out_dtype=BF16, name="mix_out_dw")
    dys, dyg, d_proj = _mix_out_dx_merge_bwd(dmo, w_o, ys, yg, proj, lax.empty((S, N_MAIN), BF16), "mix_out_dx_merge")
    dysn = _mm(dys, w_su, S, 2 * Dm, Dm, mode="nt", out_dtype=F32, name="ssm_up_dx")
    gw_su = _mm(ysn, dys, 2 * Dm, Dm, S, mode="tn", out_dtype=BF16, name="ssm_up_dw")
    dogn = _mm(dyg, w_gu, S, 2 * Dm, Dm, mode="nt", out_dtype=F32, name="gdn_up_dx")
    gw_gu = _mm(ogn, dyg, 2 * Dm, Dm, S, mode="tn", out_dtype=BF16, name="gdn_up_dw")
    mix_x = _exchange_start([gw_su.reshape(N_DEV, -1, Dm), gw_gu.reshape(N_DEV, -1, Dm), gw_o.reshape(N_DEV, -1, Dm)],
                            ["a2a"] * 3, gw_gu, "grads_mix_start")
    dxs, dBm, dCm, d_proj, ddt_g, dpv_ssm, dnw_ssm = _ssd_bwd(dysn, conv, proj, dt_g, pv_ssm + mix_x[4][0, 0], nw_ssm,
                                                              y_ssd, st_ssm, d_proj, "ssd_bwd")
    dq, dk, dv, d_proj, db_g, da_g, dpv_gdn, dnw_gdn = _gdn_bwd(dogn, conv, proj, b_g, a_g, pv_gdn, gdn_norm_w, o_pre,
                                                                st_gdn, t_inv, d_proj, "gdn_bwd")
    conv_pieces = []
    for nm, d_act, col0 in (("xs", dxs, 0), ("B", dBm, 2048), ("C", dCm, 3072), ("q", dq, 4096), ("k", dk, 5120),
                            ("v", dv, 6144)):
        d_proj, dw_piece, db_piece = _conv_bwd(d_act, proj, conv_w, conv_b, col0, d_proj, "conv_bwd_" + nm)
        conv_pieces.append((dw_piece, db_piece))
    d_small = jnp.concatenate([_heads_last(ddt_g), _heads_last(db_g), _heads_last(da_g),
                               jnp.zeros((S, N_SMALL - 64), F32)], axis=1).astype(BF16)
    gw_small = _mm(h, d_small, Dm, N_SMALL, S, mode="tn", out_dtype=BF16, name="proj_small_dw")
    main_cols = _mm(h, d_proj, Dm, N_MAIN, S, mode="tn", out_dtype=BF16, name="proj_main_dw")
    n_shard = w_in.shape[2]
    slabs = [_orig_cols(main_cols, gw_small, i * n_shard, (i + 1) * n_shard) for i in range(N_DEV)]
    to_c0, to_c1 = jnp.stack(slabs[0::2]), jnp.stack(slabs[1::2])
    chip_sum = _add_pair(to_c0, to_c1, _swap_sibling(to_c0, to_c1, "grads_w_in_pair"), "grads_w_in_pair_sum")
    in_x = _exchange_start([chip_sum], ["a2a"], gw_small, "grads_w_in_start", chips=True)
    dh = _mm(d_small, w_small + in_x[4][0:1, 0:1].astype(BF16), S, Dm, N_SMALL, mode="nt", out_dtype=F32,
             name="proj_small_dx")
    dh = _mm(d_proj, w_main, S, Dm, N_MAIN, mode="nt", out_dtype=F32, add=dh, name="proj_main_dx")
    dx, dsh1, dsc1, dw_pre1 = _pre_bwd(dh, x2, norm_mix_pre, sc1, dx1, "pre_mix_bwd")
    r_mu, r_md = _exchange_wait(mlp_x, ["a2a"] * 2, dx, "grads_mlp_wait")
    r_su, r_gu, r_o = _exchange_wait(mix_x, ["a2a"] * 3, dx, "grads_mix_wait")

    dconv_w = jnp.concatenate([p[0] for p in conv_pieces], axis=1)
    dconv_b = jnp.concatenate([p[1] for p in conv_pieces[:3]], axis=1)
    dmod = jnp.concatenate([dsh1, dsc1, dg1, dsh2, dsc2, dg2], axis=1)
    small_vec = jnp.concatenate(
        [dmod, dw_pre1, dw_post1, dconv_b, dpv_ssm[:, 0].reshape(1, 32), dpv_ssm[:, 1].reshape(1, 32),
         dpv_ssm[:, 2].reshape(1, 32), dnw_ssm.reshape(1, 2048), dpv_gdn[:, 0].reshape(1, 16),
         dpv_gdn[:, 1].reshape(1, 16), jnp.sum(dnw_gdn, axis=0), dw_pre2, dw_post2, dconv_w.reshape(1, -1)], axis=1)
    n_vec = small_vec.shape[1]
    small_vec = jnp.pad(small_vec, ((0, 0), (0, (-n_vec) % 1024))).reshape(-1, 1024)
    (small_all,) = _exchange([small_vec], ["gather"], "gather_small_grads")
    small_all = small_all.reshape(N_DEV, -1)
    dmod_cols = lax.dynamic_slice(small_all, (0, me * n_ada), (N_DEV, n_ada))
    gw_ada = _ada_bwd(c_all.T, dmod_cols, "ada_bwd")
    conv_all = small_all[:, N_REPLICATED:n_vec].reshape(N_DEV, CONV_K, 2 * N_DEV * 512)
    conv_contrib = jnp.concatenate(
        [lax.dynamic_slice(conv_all, (0, 0, me * 512), (N_DEV, CONV_K, 512)),
         lax.dynamic_slice(conv_all, (0, 0, N_DEV * 512 + me * 512), (N_DEV, CONV_K, 512))], axis=1)
    rep_contrib = small_all[:, :N_REPLICATED].reshape(N_DEV, N_REPLICATED // 128, 128)

    results = {}

    def adam_big(nm, contrib):
        w3 = given[nm]
        res = _adam(contrib, w3[0][0], w3[1][0], w3[2][0], "adam_" + nm)
        results[nm] = tuple(r.reshape(w3[0].shape) for r in res)

    adam_big("w_ada", gw_ada[None])
    adam_big("w_ssm_up", r_su)
    adam_big("w_gdn_up", r_gu)
    adam_big("w_out", r_o)
    adam_big("w_mlp_up", r_mu)
    adam_big("w_mlp_down", r_md)
    (r_in,) = _exchange_wait(in_x, ["a2a"], results["w_mlp_down"][0], "grads_w_in_wait", chips=True)
    adam_big("w_in", r_in)
    packed = [jnp.concatenate([given[nm][i] for nm in REPLICATED], axis=1).reshape(N_REPLICATED // 128, 128)
              for i in range(3)]
    rep_res = _adam(rep_contrib, packed[0], packed[1], packed[2], "adam_replicated")
    pos = 0
    for nm in REPLICATED:
        size = given[nm][0].shape[1]
        results[nm] = tuple(r.reshape(1, N_REPLICATED)[:, pos:pos + size] for r in rep_res)
        pos += size
    conv_wmv = [jnp.concatenate([given["ssm_conv_w"][i][0], given["gdn_conv_w"][i][0]], axis=0) for i in range(3)]
    conv_res = _adam(conv_contrib, conv_wmv[0], conv_wmv[1], conv_wmv[2], "adam_conv_w")
    results["ssm_conv_w"] = tuple(r[None, :CONV_K] for r in conv_res)
    results["gdn_conv_w"] = tuple(r[None, CONV_K:] for r in conv_res)

    loss = lax.psum(loss_loc[0, 0], ("x", "y", "c"))
    return (loss, dx[None]) + tuple(results[nm][i] for i in range(4) for nm in WEIGHTS)
```

```python
import jax
import jax.numpy as jnp
from jax import lax
from jax.experimental import pallas as pl
from jax.experimental.pallas import tpu as pltpu

F32 = jnp.float32
BF16 = jnp.bfloat16
N_DEV = 8
D_MODEL = 1024
EPS = 1e-6
CONV_K = 4
SSM_CHUNK = 128
SSM_HEAD_DIM = 64
SSM_D_STATE = 128
SSM_GROUPS = 8
SSM_HEADS_PER_GROUP = 4
SSM_GROUP_WIDTH = SSM_HEADS_PER_GROUP * SSM_HEAD_DIM
SSM_GROUPS_PER_STEP = 8
GDN_CHUNK = 64
GDN_HEAD = 128
GDN_QK_HEADS = 8
GDN_V_PER_QK = 2
GDN_QK_PER_STEP = 8
GDN_INV_BLOCK = 16
C_ZS, C_XBC, C_QKV, C_ZG, C_GS, C_GG, N_MAIN = 0, 2048, 6144, 10240, 12288, 13312, 14336
N_SMALL = 128
ADAM_LR, ADAM_B1, ADAM_B2, ADAM_EPS, ADAM_WD, ADAM_STEP = 0.001, 0.9, 0.999, 1e-08, 0.01, 10
VMEM_LIMIT = 56 * 1024 * 1024
MM_WHOLE_K = 4096
MM_SPLIT_K = 2048
NEG_INF = float("-inf")

_NT = (((1,), (1,)), ((), ()))
_NN = (((1,), (0,)), ((), ()))
_TN = (((0,), (0,)), ((), ()))


def _params(*sem):
    return pltpu.CompilerParams(dimension_semantics=sem, vmem_limit_bytes=VMEM_LIMIT)


def _dot(a, b, dims=_NN):
    return lax.dot_general(a.astype(BF16), b.astype(BF16), dims, preferred_element_type=F32)


def _split(a):
    hi = a.astype(BF16)
    return hi, (a - hi.astype(F32)).astype(BF16)


def _dot3(a, b, dims=_NN):
    ah, al = _split(a)
    bh, bl = _split(b)
    d = lambda u, v: lax.dot_general(u, v, dims, preferred_element_type=F32)
    return d(ah, bh) + (d(ah, bl) + d(al, bh))


def _dot2(a, b, dims=_NN):
    ah, al = _split(a)
    bb = b.astype(BF16)
    d = lambda u: lax.dot_general(u, bb, dims, preferred_element_type=F32)
    return d(ah) + d(al)


def _sigmoid(x):
    return 0.5 * jnp.tanh(0.5 * x) + 0.5


def _silu(x):
    return x * _sigmoid(x)


def _dsilu(x):
    s = _sigmoid(x)
    return s * (1.0 + x * (1.0 - s))


def _softplus(x):
    return jnp.maximum(x, 0.0) + jnp.log1p(jnp.exp(-jnp.abs(x)))


def _iota(n, m, d):
    return lax.broadcasted_iota(jnp.int32, (n, m), d)


def _rowsum(x):
    return jnp.sum(x, axis=1, keepdims=True)


def _colsum(x):
    return jnp.sum(x, axis=0, keepdims=True)


def _total(x):
    return _rowsum(_colsum(x))


MXU_LANES = 128


def _parts(x, n):
    out = []
    for _ in range(n):
        p = x.astype(BF16)
        out.append(p)
        x = x - p.astype(F32)
    return out


def _sum_by(m01, x, dims=_NN, n=3):
    return sum(lax.dot_general(m01, p, dims, preferred_element_type=F32) for p in _parts(x, n))


def _row_col_sums(q):
    ones = jnp.ones((q.shape[0], MXU_LANES), BF16)
    acc = 0.0
    for p in _parts(q, 2):
        acc = acc + (lax.dot_general(p, ones, _NN, preferred_element_type=F32)
                     - lax.dot_general(p, ones, _TN, preferred_element_type=F32))
    return acc[:, 0:1]


def _cumsum_forms(col, ii, jj):
    lower = jnp.where(ii >= jj, 1.0, 0.0).astype(BF16)
    cum_col = _sum_by(lower, jnp.broadcast_to(col, (col.shape[0], MXU_LANES)))[:, 0:1]
    cum_row = _colsum(jnp.where(ii <= jj, col, 0.0))
    return cum_col, cum_row


def _rev_cumsum_col(col, ii, jj):
    upper = jnp.where(ii <= jj, 1.0, 0.0).astype(BF16)
    return _sum_by(upper, jnp.broadcast_to(col, (col.shape[0], MXU_LANES)))[:, 0:1]


def _blk(dim, pref):
    return pref if dim % pref == 0 else dim


def _lockstep(gens):
    gens = list(gens)
    while gens:
        alive = []
        for g in gens:
            try:
                next(g)
                alive.append(g)
            except StopIteration:
                pass
        gens = alive


def _mm(a, b, M, N, K, *, mode, out_dtype, name, a_off=(0, 0), b_off=(0, 0), add=None, epi=None, extra=None,
        tm=1024, tn=1024):
    tm, tn = _blk(M, tm), _blk(N, tn)
    tk = K if K <= MM_WHOLE_K else _blk(K, MM_SPLIT_K)
    nk = K // tk
    if mode == "tn":
        a_spec = pl.BlockSpec((tk, tm), lambda i, j, k: (k + a_off[0] // tk, i + a_off[1] // tm))
        assert a_off[0] % tk == 0 and a_off[1] % tm == 0
    else:
        a_spec = pl.BlockSpec((tm, tk), lambda i, j, k: (i + a_off[0] // tm, k + a_off[1] // tk))
        assert a_off[0] % tm == 0 and a_off[1] % tk == 0
    if mode == "nt":
        b_spec = pl.BlockSpec((tn, tk), lambda i, j, k: (j + b_off[0] // tn, k + b_off[1] // tk))
        assert b_off[0] % tn == 0 and b_off[1] % tk == 0
    else:
        b_spec = pl.BlockSpec((tk, tn), lambda i, j, k: (k + b_off[0] // tk, j + b_off[1] // tn))
        assert b_off[0] % tk == 0 and b_off[1] % tn == 0
    dims = {"nn": _NN, "nt": _NT, "tn": _TN}[mode]
    o_spec = pl.BlockSpec((tm, tn), lambda i, j, k: (i, j))
    ins, in_specs = [a, b], [a_spec, b_spec]
    if add is not None:
        ins.append(add)
        in_specs.append(o_spec)
    if extra is not None:
        ins.append(extra)
        in_specs.append(o_spec)
    n_in = len(ins)
    if epi == "relu2":
        out_shape = (jax.ShapeDtypeStruct((M, N), BF16), jax.ShapeDtypeStruct((M, N), BF16))
        out_specs = (o_spec, o_spec)
    else:
        out_shape = jax.ShapeDtypeStruct((M, N), out_dtype)
        out_specs = o_spec

    def body(*refs):
        a_ref, b_ref = refs[0], refs[1]
        outs = refs[n_in:] if nk == 1 else refs[n_in:-1]

        def finish(r):
            pos = 2
            if add is not None:
                r = r + refs[pos][...]
                pos += 1
            if epi == "relu2":
                p = jnp.maximum(r, 0.0)
                outs[0][...] = p.astype(BF16)
                outs[1][...] = (p * p).astype(BF16)
            elif epi == "drelu2":
                outs[0][...] = (r * (2.0 * refs[pos][...].astype(F32))).astype(out_dtype)
            else:
                outs[0][...] = r.astype(out_dtype)

        if nk == 1:
            finish(_dot(a_ref[...], b_ref[...], dims))
            return
        acc = refs[-1]
        k = pl.program_id(2)

        @pl.when(k == 0)
        def _():
            acc[...] = jnp.zeros_like(acc)

        acc[...] += _dot(a_ref[...], b_ref[...], dims)

        @pl.when(k == nk - 1)
        def _():
            finish(acc[...])

    return pl.pallas_call(
        body, name=name, grid=(M // tm, N // tn, nk), in_specs=in_specs, out_specs=out_specs, out_shape=out_shape,
        scratch_shapes=[] if nk == 1 else [pltpu.VMEM((tm, tn), F32)],
        compiler_params=_params("parallel", "parallel", "arbitrary"))(*ins)


def _mm_rows(a, b, M, N, K, *, mode, name, extras, out_shapes, out_specs, epilogue, aliases=None, tm=512):
    tm = _blk(M, tm)
    a_spec = pl.BlockSpec((tm, K), lambda i: (i, 0))
    b_spec = pl.BlockSpec((K, N) if mode == "nn" else (N, K), lambda i: (0, 0))
    dims = _NN if mode == "nn" else _NT
    n_ex = len(extras)

    def body(a_ref, b_ref, *refs):
        epilogue(_dot(a_ref[...], b_ref[...], dims), refs[:n_ex], refs[n_ex:])

    return pl.pallas_call(
        body, name=name, grid=(M // tm,), in_specs=[a_spec, b_spec] + [sp for _, sp in extras],
        out_specs=tuple(out_specs), out_shape=tuple(out_shapes), input_output_aliases=aliases or {},
        compiler_params=_params("parallel"))(a, b, *[x for x, _ in extras])


def _row_spec(tb, d):
    return pl.BlockSpec((tb, d), lambda i: (i, 0))


def _vec_spec(d):
    return pl.BlockSpec((1, d), lambda i: (0, 0))


def _pre_fwd(x, w, sc, sh, name):
    S, Dm = x.shape
    tb = _blk(S, 512)

    def body(x_ref, w_ref, sc_ref, sh_ref, h_ref):
        xv = x_ref[...]
        r = lax.rsqrt(jnp.mean(xv * xv, axis=-1, keepdims=True) + EPS)
        h_ref[...] = ((xv * r * w_ref[...]) * (1.0 + sc_ref[...]) + sh_ref[...]).astype(BF16)

    return pl.pallas_call(
        body, name=name, grid=(S // tb,), in_specs=[_row_spec(tb, Dm)] + [_vec_spec(Dm)] * 3,
        out_specs=_row_spec(tb, Dm), out_shape=jax.ShapeDtypeStruct((S, Dm), BF16),
        compiler_params=_params("parallel"))(x, w, sc, sh)


def _final_fwd_bwd(x, y, w, g, target, name):
    S, Dm = x.shape
    tb = _blk(S, 512)
    nb = S // tb

    def body(x_ref, y_ref, w_ref, g_ref, t_ref, dx_ref, loss_ref, dy_ref, dg_ref, dw_ref, acc):
        i = pl.program_id(0)

        @pl.when(i == 0)
        def _():
            acc[...] = jnp.zeros_like(acc)
            dg_ref[...] = jnp.zeros_like(dg_ref)
            dw_ref[...] = jnp.zeros_like(dw_ref)

        yv = y_ref[...]
        r = lax.rsqrt(jnp.mean(yv * yv, axis=-1, keepdims=True) + EPS)
        yh = yv * r
        n = yh * w_ref[...]
        e = (x_ref[...] + g_ref[...] * n) - t_ref[...]
        dv = e * (1.0 / Dm)
        dx_ref[...] = dv
        acc[...] += _colsum(e * e)
        dg_ref[...] += _colsum(dv * n)
        dn = dv * g_ref[...]
        dw_ref[...] += _colsum(dn * yh)
        dyh = dn * w_ref[...]
        dy_ref[...] = (r * (dyh - yh * jnp.mean(dyh * yh, axis=-1, keepdims=True))).astype(BF16)

        @pl.when(i == nb - 1)
        def _():
            loss_ref[...] = (0.5 / Dm) * _rowsum(acc[...])

    row, vec = _row_spec(tb, Dm), _vec_spec(Dm)
    vec_shape = jax.ShapeDtypeStruct((1, Dm), F32)
    return pl.pallas_call(
        body, name=name, grid=(nb,), in_specs=[row, row, vec, vec, row],
        out_specs=(row, pl.BlockSpec((1, 1), lambda i: (0, 0)), row, vec, vec),
        out_shape=(jax.ShapeDtypeStruct((S, Dm), F32), jax.ShapeDtypeStruct((1, 1), F32),
                   jax.ShapeDtypeStruct((S, Dm), BF16), vec_shape, vec_shape),
        scratch_shapes=[pltpu.VMEM((1, Dm), F32)], compiler_params=_params("arbitrary"))(x, y, w, g, target)


def _post_bwd(dxo, y, w, g, name):
    S, Dm = y.shape
    tb = _blk(S, 512)

    def body(d_ref, y_ref, w_ref, g_ref, dy_ref, dg_ref, dw_ref):
        i = pl.program_id(0)

        @pl.when(i == 0)
        def _():
            dg_ref[...] = jnp.zeros_like(dg_ref)
            dw_ref[...] = jnp.zeros_like(dw_ref)

        yv, dv = y_ref[...], d_ref[...]
        r = lax.rsqrt(jnp.mean(yv * yv, axis=-1, keepdims=True) + EPS)
        yh = yv * r
        dg_ref[...] += _colsum(dv * (yh * w_ref[...]))
        dn = dv * g_ref[...]
        dw_ref[...] += _colsum(dn * yh)
        dyh = dn * w_ref[...]
        dy_ref[...] = (r * (dyh - yh * jnp.mean(dyh * yh, axis=-1, keepdims=True))).astype(BF16)

    return pl.pallas_call(
        body, name=name, grid=(S // tb,), in_specs=[_row_spec(tb, Dm)] * 2 + [_vec_spec(Dm)] * 2,
        out_specs=(_row_spec(tb, Dm), _vec_spec(Dm), _vec_spec(Dm)),
        out_shape=(jax.ShapeDtypeStruct((S, Dm), BF16), jax.ShapeDtypeStruct((1, Dm), F32),
                   jax.ShapeDtypeStruct((1, Dm), F32)),
        compiler_params=_params("arbitrary"))(dxo, y, w, g)


def _pre_bwd(dh, x, w, sc, dres, name):
    S, Dm = x.shape
    tb = _blk(S, 512)

    def body(dh_ref, x_ref, w_ref, sc_ref, dr_ref, dx_ref, dsh_ref, dsc_ref, dw_ref):
        i = pl.program_id(0)

        @pl.when(i == 0)
        def _():
            dsh_ref[...] = jnp.zeros_like(dsh_ref)
            dsc_ref[...] = jnp.zeros_like(dsc_ref)
            dw_ref[...] = jnp.zeros_like(dw_ref)

        xv, dv = x_ref[...], dh_ref[...]
        r = lax.rsqrt(jnp.mean(xv * xv, axis=-1, keepdims=True) + EPS)
        xh = xv * r
        one_sc = 1.0 + sc_ref[...]
        dsh_ref[...] += _colsum(dv)
        dsc_ref[...] += _colsum(dv * (xh * w_ref[...]))
        dw_ref[...] += _colsum(dv * one_sc * xh)
        dxh = dv * one_sc * w_ref[...]
        dx_ref[...] = dr_ref[...] + r * (dxh - xh * jnp.mean(dxh * xh, axis=-1, keepdims=True))

    vec = jax.ShapeDtypeStruct((1, Dm), F32)
    return pl.pallas_call(
        body, name=name, grid=(S // tb,),
        in_specs=[_row_spec(tb, Dm)] * 2 + [_vec_spec(Dm)] * 2 + [_row_spec(tb, Dm)],
        out_specs=(_row_spec(tb, Dm), _vec_spec(Dm), _vec_spec(Dm), _vec_spec(Dm)),
        out_shape=(jax.ShapeDtypeStruct((S, Dm), F32), vec, vec, vec),
        compiler_params=_params("arbitrary"))(dh, x, w, sc, dres)


D_PROJ_ANY = pl.BlockSpec(memory_space=pl.ANY)


def _gate_specs(tm, Dm):
    return (pl.BlockSpec((tm, Dm), lambda i: (i, C_GS // Dm)), pl.BlockSpec((tm, Dm), lambda i: (i, C_GG // Dm)))


def _gdn_up_merge(ogn, w_gu, ys, proj, name):
    S, K = ogn.shape
    Dm = ys.shape[1]
    tm = _blk(S, 512)
    row = _row_spec(tm, Dm)

    def epilogue(r, ex, out):
        ys_ref, gs_ref, gg_ref = ex
        out[0][...] = r
        out[1][...] = (_sigmoid(gs_ref[...].astype(F32)) * ys_ref[...]
                       + _sigmoid(gg_ref[...].astype(F32)) * r).astype(BF16)

    gs_spec, gg_spec = _gate_specs(tm, Dm)
    return _mm_rows(ogn, w_gu, S, Dm, K, mode="nn", name=name, tm=tm,
                    extras=[(ys, row), (proj, gs_spec), (proj, gg_spec)],
                    out_shapes=[jax.ShapeDtypeStruct((S, Dm), F32), jax.ShapeDtypeStruct((S, Dm), BF16)],
                    out_specs=[row, row], epilogue=epilogue)


def _mix_out_post_pre(merged, w_o, x, w_post, g, w_pre, sc, sh, name):
    S, Dm = x.shape
    tm = _blk(S, 512)
    row, vec = _row_spec(tm, Dm), _vec_spec(Dm)

    def epilogue(r, ex, out):
        x_ref, wpost_ref, g_ref, wpre_ref, sc_ref, sh_ref = ex
        out[0][...] = r
        rr = lax.rsqrt(jnp.mean(r * r, axis=-1, keepdims=True) + EPS)
        x1 = x_ref[...] + g_ref[...] * (r * rr * wpost_ref[...])
        out[1][...] = x1
        r1 = lax.rsqrt(jnp.mean(x1 * x1, axis=-1, keepdims=True) + EPS)
        out[2][...] = ((x1 * r1 * wpre_ref[...]) * (1.0 + sc_ref[...]) + sh_ref[...]).astype(BF16)

    return _mm_rows(merged, w_o, S, Dm, Dm, mode="nn", name=name, tm=tm,
                    extras=[(x, row), (w_post, vec), (g, vec), (w_pre, vec), (sc, vec), (sh, vec)],
                    out_shapes=[jax.ShapeDtypeStruct((S, Dm), F32), jax.ShapeDtypeStruct((S, Dm), F32),
                                jax.ShapeDtypeStruct((S, Dm), BF16)], out_specs=[row, row, row], epilogue=epilogue)


def _mix_out_dx_merge_bwd(dmo, w_o, ys, yg, proj, d_proj, name):
    S, Dm = ys.shape
    tm = _blk(S, 512)
    row = _row_spec(tm, Dm)

    def epilogue(d, ex, out):
        ys_ref, yg_ref, gs_ref, gg_ref, _ = ex
        ss, sg = _sigmoid(gs_ref[...].astype(F32)), _sigmoid(gg_ref[...].astype(F32))
        out[0][...] = (d * ss).astype(BF16)
        out[1][...] = (d * sg).astype(BF16)
        out[2][:, :Dm] = (d * ys_ref[...] * ss * (1.0 - ss)).astype(BF16)
        out[2][:, Dm:] = (d * yg_ref[...] * sg * (1.0 - sg)).astype(BF16)

    gs_spec, gg_spec = _gate_specs(tm, Dm)
    return _mm_rows(dmo, w_o, S, Dm, Dm, mode="nt", name=name, tm=tm,
                    extras=[(ys, row), (yg, row), (proj, gs_spec), (proj, gg_spec), (d_proj, D_PROJ_ANY)],
                    out_shapes=[jax.ShapeDtypeStruct((S, Dm), BF16), jax.ShapeDtypeStruct((S, Dm), BF16),
                                jax.ShapeDtypeStruct(d_proj.shape, BF16)],
                    out_specs=[row, row, pl.BlockSpec((tm, 2 * Dm), lambda i: (i, C_GS // (2 * Dm)))],
                    epilogue=epilogue, aliases={6: 2})


CONV_COLS = 128
CONV_BWD_ROWS = 256


def _taps_down(x):
    rows = _iota(x.shape[0], x.shape[1], 0)
    return [x] + [jnp.where(rows >= k, pltpu.roll(x, k, 0), 0.0) for k in range(1, CONV_K)]


def _conv_pre(taps, w_ref, b_ref):
    pre = taps[0] * w_ref[CONV_K - 1:CONV_K, :] + b_ref[...]
    for k in range(1, CONV_K):
        pre = pre + taps[k] * w_ref[CONV_K - 1 - k:CONV_K - k, :]
    return pre


def _conv_dx(dpre, w_ref):
    n = dpre.shape[0]
    rows = _iota(n, dpre.shape[1], 0)
    dx = dpre * w_ref[CONV_K - 1:CONV_K, :]
    for k in range(1, CONV_K):
        dx = dx + jnp.where(rows < n - k, pltpu.roll(dpre, n - k, 0), 0.0) * w_ref[CONV_K - 1 - k:CONV_K - k, :]
    return dx


def _conv_fwd(proj, w, b, name):
    S = proj.shape[0]
    n = w.shape[1]
    cb = CONV_COLS

    def body(x_ref, w_ref, b_ref, o_ref):
        o_ref[...] = _silu(_conv_pre(_taps_down(x_ref[...].astype(F32)), w_ref, b_ref)).astype(BF16)

    return pl.pallas_call(
        body, name=name, grid=(n // cb,),
        in_specs=[pl.BlockSpec((S, cb), lambda j: (0, j + C_XBC // cb)), pl.BlockSpec((CONV_K, cb), lambda j: (0, j)),
                  pl.BlockSpec((1, cb), lambda j: (0, j))],
        out_specs=pl.BlockSpec((S, cb), lambda j: (0, j)), out_shape=jax.ShapeDtypeStruct((S, n), BF16),
        compiler_params=_params("parallel"))(proj, w, b)


def _conv_bwd(dact, proj, w, b, col0, d_proj, name):
    S, n = dact.shape
    cb = CONV_COLS
    o = col0 // cb

    R, HALO = _blk(S, CONV_BWD_ROWS), 16
    n_chunks = S // R

    def body(d_ref, x_ref, w_ref, b_ref, _, dx_ref, dw_ref, db_ref):
        def chunk(r0, first, last, sums):
            lo, hi = (0 if first else HALO), (0 if last else HALO)
            start = r0 - lo if isinstance(r0, int) else pl.multiple_of(r0 - lo, HALO)
            xe = x_ref[pl.ds(start, lo + R + hi), :].astype(F32)
            rows = _iota(lo + R + hi, cb, 0)
            taps = [xe[lo:, :]]
            for k in range(1, CONV_K):
                t = pltpu.roll(xe, k, 0)
                taps.append((jnp.where(rows >= k, t, 0.0) if first else t)[lo:, :])
            dpre_e = d_ref[pl.ds(r0, R + hi), :].astype(F32) * _dsilu(_conv_pre(taps, w_ref, b_ref))
            dpre = dpre_e[0:R, :]
            db, dw = sums
            db = db + _colsum(dpre)
            dw = [dw[k] + _colsum(dpre * taps[k][0:R, :]) for k in range(CONV_K)]
            rows_e = _iota(R + hi, cb, 0)
            dx = dpre * w_ref[CONV_K - 1:CONV_K, :]
            for k in range(1, CONV_K):
                t = pltpu.roll(dpre_e, R + hi - k, 0)
                t = jnp.where(rows_e < R - k, t, 0.0) if last else t
                dx = dx + t[0:R, :] * w_ref[CONV_K - 1 - k:CONV_K - k, :]
            dx_ref[pl.ds(r0, R), :] = dx.astype(BF16)
            return db, dw

        zero = jnp.zeros((1, cb), F32)
        sums = chunk(0, True, n_chunks == 1, (zero, [zero] * CONV_K))
        if n_chunks > 2:
            def step(i, carry):
                db, dw = chunk(pl.multiple_of(i * R, R), False, False, (carry[0], list(carry[1:])))
                return (db,) + tuple(dw)
            carry = lax.fori_loop(1, n_chunks - 1, step, (sums[0],) + tuple(sums[1]))
            sums = (carry[0], list(carry[1:]))
        if n_chunks > 1:
            sums = chunk((n_chunks - 1) * R, False, True, sums)
        db_ref[...] = sums[0]
        for k in range(CONV_K):
            dw_ref[CONV_K - 1 - k:CONV_K - k, :] = sums[1][k]

    return pl.pallas_call(
        body, name=name, grid=(n // cb,),
        in_specs=[pl.BlockSpec((S, cb), lambda j: (0, j)), pl.BlockSpec((S, cb), lambda j: (0, j + o + C_XBC // cb)),
                  pl.BlockSpec((CONV_K, cb), lambda j: (0, j + o)), pl.BlockSpec((1, cb), lambda j: (0, j + o)),
                  D_PROJ_ANY],
        out_specs=(pl.BlockSpec((S, cb), lambda j: (0, j + o + C_XBC // cb)),
                   pl.BlockSpec((CONV_K, cb), lambda j: (0, j)), pl.BlockSpec((1, cb), lambda j: (0, j))),
        out_shape=(jax.ShapeDtypeStruct(d_proj.shape, BF16), jax.ShapeDtypeStruct((CONV_K, n), F32),
                   jax.ShapeDtypeStruct((1, n), F32)),
        input_output_aliases={4: 0}, compiler_params=_params("parallel"))(dact, proj, w, b, d_proj)


def _ssd_specs(L, order):
    G = SSM_GROUPS_PER_STEP
    W, N = G * SSM_GROUP_WIDTH, G * SSM_D_STATE
    x_spec = pl.BlockSpec((L, W), lambda g, c: (order(c), g))
    b_spec = pl.BlockSpec((L, N), lambda g, c: (order(c), 2048 // N + g))
    c_spec = pl.BlockSpec((L, N), lambda g, c: (order(c), 3072 // N + g))
    z_spec = pl.BlockSpec((L, W), lambda g, c: (order(c), C_ZS // W + g))
    dt_spec = pl.BlockSpec((G, L, SSM_HEADS_PER_GROUP), lambda g, c: (g, order(c), 0))
    p_spec = pl.BlockSpec((G, 3, SSM_HEADS_PER_GROUP), lambda g, c: (g, 0, 0))
    nw_spec = pl.BlockSpec((G, 1, SSM_GROUP_WIDTH), lambda g, c: (g, 0, 0))
    s_spec = pl.BlockSpec((G, 1, SSM_GROUP_WIDTH, SSM_D_STATE), lambda g, c: (g, order(c), 0, 0))
    return x_spec, b_spec, c_spec, z_spec, dt_spec, p_spec, nw_spec, s_spec


class _SsdGroup:
    def __init__(self, L):
        P, H, W = SSM_HEAD_DIM, SSM_HEADS_PER_GROUP, SSM_GROUP_WIDTH
        self.L = L
        self.ii, self.jj = _iota(L, L, 0), _iota(L, L, 1)
        self.lower = jnp.where(self.ii >= self.jj, 1.0, 0.0).astype(BF16)
        self.upper = jnp.where(self.ii <= self.jj, 1.0, 0.0).astype(BF16)
        self.lo = _iota(L, 2 * P, 1) < P
        self.lo_row = _iota(1, 2 * P, 1) < P
        bi, bj = _iota(W, W, 0), _iota(W, W, 1)
        self.block = jnp.where(bi // P == bj // P, 1.0, 0.0).astype(BF16)
        si, sj = _iota(2 * P, W, 0), _iota(2 * P, W, 1)
        self.pick = jnp.where(sj == si * P, 1.0, 0.0).astype(BF16)
        self.ones = jnp.ones((L, 2 * P), BF16)

    def spread(self, v4):
        R = v4.shape[0]
        lo = self.lo if R == self.L else self.lo_row
        b = lambda h: jnp.broadcast_to(v4[:, h:h + 1], (R, 2 * SSM_HEAD_DIM))
        return jnp.concatenate([jnp.where(lo, b(0), b(1)), jnp.where(lo, b(2), b(3))], axis=1)

    def gather4(self, v):
        return jnp.concatenate([v[:, h * SSM_HEAD_DIM:h * SSM_HEAD_DIM + 1] for h in range(SSM_HEADS_PER_GROUP)],
                               axis=1)

    def head_sums(self, z):
        return sum(lax.dot_general(p, self.block, _NN, preferred_element_type=F32) for p in _parts(z, 2))

    def pair_cols(self, full, pair):
        ps = full[:, pair * 128:(pair + 1) * 128]
        sw = pltpu.roll(ps, SSM_HEAD_DIM, 1)
        return jnp.where(self.lo, ps, sw), jnp.where(self.lo, sw, ps)

    def gates(self, dt4_raw, p):
        L = self.L
        dtr = self.spread(dt4_raw + p[0:1, :])
        dt = _softplus(dtr)
        A = self.spread(-jnp.exp(p[1:2, :]))
        acum = _sum_by(self.lower, dt * A)
        yield
        rows = _sum_by(self.pick, acum, _NT)
        yield
        a_last = acum[L - 1:L, :]
        cols = self.pair_cols(acum, 0) + self.pair_cols(acum, 1)
        decay, decay_t = [], []
        for h in range(SSM_HEADS_PER_GROUP):
            seg = cols[h] - rows[h:h + 1, :]
            decay.append(jnp.exp(jnp.where(self.ii >= self.jj, seg, NEG_INF)))
            decay_t.append(jnp.exp(jnp.where(self.jj >= self.ii, -seg, NEG_INF)))
        return dict(dtr=dtr, dt=dt, A=A, D=self.spread(p[2:3, :]), acum=acum, eac=jnp.exp(acum), a_last=a_last,
                    wdec=jnp.exp(a_last - acum), decay=decay, decay_t=decay_t,
                    ea_last=[jnp.exp(rows[h:h + 1, L - 1:L]) for h in range(SSM_HEADS_PER_GROUP)])


def _ssd_fwd(conv, proj, dt_raw, pvec, nw, name):
    S = conv.shape[0]
    L, P, N, H, W, G = SSM_CHUNK, SSM_HEAD_DIM, SSM_D_STATE, SSM_HEADS_PER_GROUP, SSM_GROUP_WIDTH, SSM_GROUPS_PER_STEP
    nc = S // L

    def body(x_ref, b_ref, c_ref, z_ref, dt_ref, p_ref, nw_ref, y_ref, yn_ref, s0_ref, state):
        c = pl.program_id(1)

        @pl.when(c == 0)
        def _():
            state[...] = jnp.zeros_like(state)

        k = _SsdGroup(L)

        def group(gi):
            gsl = slice(gi * W, (gi + 1) * W)
            Bm, Cm = b_ref[:, gi * N:(gi + 1) * N], c_ref[:, gi * N:(gi + 1) * N]
            x = x_ref[:, gsl].astype(F32)
            S0 = state[gsl, :]
            s0_ref[gi, 0] = S0
            CB = _dot(Cm, Bm, _NT)
            y_off = _dot(Cm, S0, _NT)
            t = yield from k.gates(dt_ref[gi], p_ref[gi])
            xdt = x * t["dt"]
            s_new = _dot(xdt * t["wdec"], Bm, _TN)
            y_diag = []
            for pair in range(H // 2):
                xp = xdt[:, pair * 128:(pair + 1) * 128]
                y_diag.append(jnp.where(k.lo, _dot(CB * t["decay"][2 * pair], xp),
                                        _dot(CB * t["decay"][2 * pair + 1], xp)))
            yield
            y = jnp.concatenate(y_diag, axis=1) + y_off * t["eac"]
            for h in range(H):
                hsl = slice(gi * W + h * P, gi * W + (h + 1) * P)
                state[hsl, :] = S0[h * P:(h + 1) * P, :] * t["ea_last"][h] + s_new[h * P:(h + 1) * P, :]
            y_ref[:, gsl] = y
            y2 = (y + t["D"] * x) * _silu(z_ref[:, gsl].astype(F32))
            r = lax.rsqrt(jnp.mean(y2 * y2, axis=-1, keepdims=True) + EPS)
            yn_ref[:, gsl] = (y2 * r * nw_ref[gi]).astype(BF16)

        _lockstep(group(gi) for gi in range(G))

    x_spec, b_spec, c_spec, z_spec, dt_spec, p_spec, nw_spec, s_spec = _ssd_specs(L, lambda c: c)
    y_spec = pl.BlockSpec((L, G * W), lambda g, c: (c, g))
    return pl.pallas_call(
        body, name=name, grid=(SSM_GROUPS // G, nc),
        in_specs=[x_spec, b_spec, c_spec, z_spec, dt_spec, p_spec, nw_spec],
        out_specs=(y_spec, y_spec, s_spec),
        out_shape=(jax.ShapeDtypeStruct((S, SSM_GROUPS * W), F32), jax.ShapeDtypeStruct((S, SSM_GROUPS * W), BF16),
                   jax.ShapeDtypeStruct((SSM_GROUPS, nc, W, N), F32)),
        scratch_shapes=[pltpu.VMEM((G * W, N), F32)],
        compiler_params=_params("parallel", "arbitrary"))(conv, conv, conv, proj, dt_raw, pvec, nw)


def _ssd_bwd(dyn, conv, proj, dt_raw, pvec, nw, y_ssd, states, d_proj, d_conv, name):
    S = conv.shape[0]
    L, P, N, H, W, G = SSM_CHUNK, SSM_HEAD_DIM, SSM_D_STATE, SSM_HEADS_PER_GROUP, SSM_GROUP_WIDTH, SSM_GROUPS_PER_STEP
    nc = S // L
    assert G == SSM_GROUPS
    CB0, CC0 = SSM_GROUPS * W, SSM_GROUPS * (W + N)

    def body(dyn_ref, x_ref, b_ref, c_ref, z_ref, dt_ref, p_ref, nw_ref, y_ref, s0_ref, _, _2,
             dact_ref, dz_ref, ddt_ref, dp_ref, dnw_ref, dstate):
        c = pl.program_id(1)

        @pl.when(c == 0)
        def _():
            dstate[...] = jnp.zeros_like(dstate)
            dp_ref[...] = jnp.zeros_like(dp_ref)
            dnw_ref[...] = jnp.zeros_like(dnw_ref)

        k = _SsdGroup(L)
        last = (_iota(L, 1, 0) == L - 1)

        def group(gi):
            gsl = slice(gi * W, (gi + 1) * W)
            Bm, Cm = b_ref[:, gi * N:(gi + 1) * N], c_ref[:, gi * N:(gi + 1) * N]
            x, z = x_ref[:, gsl].astype(F32), z_ref[:, gsl].astype(F32)
            S0, dS1 = s0_ref[gi, 0], dstate[gsl, :]
            CB = _dot(Cm, Bm, _NT)
            CBt = _dot(Bm, Cm, _NT)
            y_off_raw = _dot(Cm, S0, _NT)
            dXs_raw = _dot(Bm, dS1, _NT)
            t = yield from k.gates(dt_ref[gi], p_ref[gi])
            y1 = y_ref[:, gsl] + t["D"] * x
            sz = _silu(z)
            y2 = y1 * sz
            r = lax.rsqrt(jnp.mean(y2 * y2, axis=-1, keepdims=True) + EPS)
            y2h = y2 * r
            dyn_v = dyn_ref[:, gsl]
            dnw_ref[gi] += _colsum(dyn_v * y2h)
            dy2h = dyn_v * nw_ref[gi]
            dy2 = r * (dy2h - y2h * jnp.mean(dy2h * y2h, axis=-1, keepdims=True))
            dz_ref[:, gsl] = (dy2 * y1 * _dsilu(z)).astype(BF16)
            dY = dy2 * sz
            X = x * t["dt"]
            dYe = dY * t["eac"]
            dC_s = _dot(dYe, S0)
            dB_s = _dot(X * t["wdec"], dS1)
            dS_c = _dot(dYe, Cm, _TN)
            dXm, Gs, Gts = [], [], []
            for pair in range(H // 2):
                dYp, Xp = dY[:, pair * 128:(pair + 1) * 128], X[:, pair * 128:(pair + 1) * 128]
                dXm.append(jnp.where(k.lo, _dot(CBt * t["decay_t"][2 * pair], dYp),
                                     _dot(CBt * t["decay_t"][2 * pair + 1], dYp)))
                for mask in (k.lo, ~k.lo):
                    Gs.append(_dot(jnp.where(mask, dYp, 0.0), Xp, _NT))
                    Gts.append(_dot(jnp.where(mask, Xp, 0.0), dYp, _NT))
            yield
            dXs = dXs_raw * t["wdec"]
            dX = jnp.concatenate(dXm, axis=1) + dXs
            dCB, dCBt, q_sums = 0.0, 0.0, []
            for h in range(H):
                M, Mt = CB * t["decay"][h], CBt * t["decay_t"][h]
                dCB = dCB + Gs[h] * t["decay"][h]
                dCBt = dCBt + Gts[h] * t["decay_t"][h]
                d = Gs[h] * M - Gts[h] * Mt
                q_sums.append(sum(lax.dot_general(pt, k.ones, _NN, preferred_element_type=F32)
                                  for pt in _parts(d, 2)))
            q_f = jnp.concatenate([jnp.where(k.lo, q_sums[0], q_sums[1]), jnp.where(k.lo, q_sums[2], q_sums[3])],
                                  axis=1)
            x_dxs = k.head_sums(X * dXs)
            tot = [_total(dS1[h * P:(h + 1) * P, :] * S0[h * P:(h + 1) * P, :]) * t["ea_last"][h] for h in range(H)]
            tot_f = k.spread(jnp.concatenate(tot, axis=1))
            d_alast = _colsum(x_dxs) + tot_f
            dacum = q_f + k.head_sums(dY * (y_off_raw * t["eac"])) - x_dxs + jnp.where(last, d_alast, 0.0)
            dx_dt = k.head_sums(dX * x)
            d_skip = _colsum(k.head_sums(dY * x))
            for h in range(H):
                hsl = slice(gi * W + h * P, gi * W + (h + 1) * P)
                dstate[hsl, :] = t["ea_last"][h] * dS1[h * P:(h + 1) * P, :] + dS_c[h * P:(h + 1) * P, :]
            dc_s2 = _dot(dCB, Bm)
            db_s2 = _dot(dCBt, Cm)
            yield
            da = _sum_by(k.upper, dacum)
            yield
            ddt_raw = (da * t["A"] + dx_dt) * _sigmoid(t["dtr"])
            dact_ref[:, gsl] = (dX * t["dt"] + t["D"] * dY).astype(BF16)
            dact_ref[:, CC0 + gi * N:CC0 + (gi + 1) * N] = (dC_s + dc_s2).astype(BF16)
            dact_ref[:, CB0 + gi * N:CB0 + (gi + 1) * N] = (dB_s + db_s2).astype(BF16)
            ddt_ref[gi] = k.gather4(ddt_raw)
            dp_ref[gi] += k.gather4(jnp.concatenate([_colsum(ddt_raw), _colsum(da * t["dt"]) * t["A"], d_skip],
                                                    axis=0))

        _lockstep(group(gi) for gi in range(G))

    rev = lambda c: nc - 1 - c
    x_spec, b_spec, c_spec, z_spec, dt_spec, p_spec, nw_spec, s_spec = _ssd_specs(L, rev)
    y_spec = pl.BlockSpec((L, G * W), lambda g, c: (rev(c), g))
    half = d_conv.shape[1] // 2
    return pl.pallas_call(
        body, name=name, grid=(SSM_GROUPS // G, nc),
        in_specs=[y_spec, x_spec, b_spec, c_spec, z_spec, dt_spec, p_spec, nw_spec, y_spec, s_spec, D_PROJ_ANY,
                  D_PROJ_ANY],
        out_specs=(pl.BlockSpec((L, half), lambda g, c: (rev(c), 0)), z_spec, dt_spec, p_spec, nw_spec),
        out_shape=(jax.ShapeDtypeStruct(d_conv.shape, BF16), jax.ShapeDtypeStruct(d_proj.shape, BF16),
                   jax.ShapeDtypeStruct((SSM_GROUPS, S, H), F32), jax.ShapeDtypeStruct((SSM_GROUPS, 3, H), F32),
                   jax.ShapeDtypeStruct((SSM_GROUPS, 1, W), F32)),
        scratch_shapes=[pltpu.VMEM((G * W, N), F32)], input_output_aliases={10: 1, 11: 0},
        compiler_params=_params("parallel", "arbitrary"))(dyn, conv, conv, conv, proj, dt_raw, pvec, nw, y_ssd, states,
                                                          d_proj, d_conv)


def _unit_lower_inverse(A, ii, jj):
    eye = (ii == jj).astype(F32)
    same = (ii // GDN_INV_BLOCK) == (jj // GDN_INV_BLOCK)
    Ad = jnp.where(same, A, 0.0)
    Ao = A - Ad
    P2 = _dot3(Ad, Ad)
    yield
    P4, X = _dot(P2, P2), _dot3(eye - Ad, eye + P2)
    yield
    P8, X = _dot(P4, P4), X + _dot2(X, P4)
    yield
    X = X + _dot2(X, P8)
    yield
    Bm = _dot3(X, Ao)
    yield
    B2 = _dot3(Bm, Bm)
    yield
    Y = (eye - Bm) + B2 - _dot2(Bm, B2)
    yield
    T = _dot3(Y, X)
    yield
    return T


def _gdn_specs(L, order):
    G = GDN_QK_PER_STEP
    Hd, W = G * GDN_HEAD, G * GDN_V_PER_QK * GDN_HEAD
    q_spec = pl.BlockSpec((L, Hd), lambda h, c: (order(c), (C_QKV - C_XBC) // Hd + h))
    k_spec = pl.BlockSpec((L, Hd), lambda h, c: (order(c), (C_QKV - C_XBC + 1024) // Hd + h))
    v_spec = pl.BlockSpec((L, W), lambda h, c: (order(c), (C_QKV - C_XBC + 2048) // W + h))
    z_spec = pl.BlockSpec((L, W), lambda h, c: (order(c), C_ZG // W + h))
    ba_spec = pl.BlockSpec((G, L, GDN_V_PER_QK), lambda h, c: (h, order(c), 0))
    p_spec = pl.BlockSpec((G, 2, GDN_V_PER_QK), lambda h, c: (h, 0, 0))
    nw_spec = pl.BlockSpec((1, GDN_HEAD), lambda h, c: (0, 0))
    s_spec = pl.BlockSpec((G, 1, GDN_V_PER_QK * GDN_HEAD, GDN_HEAD), lambda h, c: (h, order(c), 0, 0))
    t_spec = pl.BlockSpec((G * GDN_V_PER_QK, 1, L, L), lambda h, c: (h, order(c), 0, 0))
    return q_spec, k_spec, v_spec, z_spec, ba_spec, p_spec, nw_spec, s_spec, t_spec


def _gdn_gates(qa, ka, b_col, a_col, p, j, ii, jj):
    L = qa.shape[0]
    sp_in = a_col + p[0:1, j:j + 1]
    neg_ea = -jnp.exp(p[1:2, j:j + 1])
    g = neg_ea * _softplus(sp_in)
    gcum, gcum_row = _cumsum_forms(g, ii, jj)
    rq = lax.rsqrt(_rowsum(qa * qa) + EPS)
    rk = lax.rsqrt(_rowsum(ka * ka) + EPS)
    q = qa * rq * (GDN_HEAD ** -0.5)
    k = ka * rk
    beta = _sigmoid(b_col)
    yield
    Dm = jnp.exp(jnp.where(ii >= jj, gcum - gcum_row, NEG_INF))
    eg = jnp.exp(gcum)
    g_last = gcum[L - 1:L, :]
    wdec = jnp.exp(g_last - gcum)
    return dict(rq=rq, rk=rk, q=q, k=k, beta=beta, sp_in=sp_in, neg_ea=neg_ea, g=g, Dm=Dm, kbeta=k * beta, eg=eg,
                g_last=g_last, wdec=wdec, kdec=k * wdec)


def _gdn_fwd(conv, proj, b_raw, a_raw, pvec, nw, name):
    S = conv.shape[0]
    L, Hd, J, G = GDN_CHUNK, GDN_HEAD, GDN_V_PER_QK, GDN_QK_PER_STEP
    W = J * Hd
    nc = S // L

    def body(q_ref, k_ref, v_ref, z_ref, b_ref, a_ref, p_ref, nw_ref, o_ref, on_ref, s0_ref, t_ref, state):
        c = pl.program_id(1)

        @pl.when(c == 0)
        def _():
            state[...] = jnp.zeros_like(state)

        ii, jj = _iota(L, L, 0), _iota(L, L, 1)
        for hq in range(G):
            s0_ref[hq, 0] = state[hq * W:(hq + 1) * W, :]

        def head(hq, j):
            hd = hq * J + j
            hsl, sl = slice(hq * Hd, (hq + 1) * Hd), slice(hd * Hd, (hd + 1) * Hd)
            t = yield from _gdn_gates(q_ref[:, hsl].astype(F32), k_ref[:, hsl].astype(F32), b_ref[hq][:, j:j + 1],
                                      a_ref[hq][:, j:j + 1],
                                      p_ref[hq], j, ii, jj)
            KK = _dot(t["kbeta"], t["k"], _NT)
            QK = _dot(t["q"], t["k"], _NT)
            yield
            T = yield from _unit_lower_inverse(jnp.where(ii > jj, KK * t["Dm"], 0.0), ii, jj)
            t_ref[hd, 0] = T
            S0 = state[sl, :]
            U = _dot2(T, v_ref[:, sl].astype(F32) * t["beta"])
            Wm = _dot2(T, t["kbeta"] * t["eg"])
            o_inter = _dot(t["q"] * t["eg"], S0)
            yield
            Vn = U - _dot(Wm, S0)
            yield
            o = o_inter + _dot(QK * t["Dm"], Vn)
            s_new = _dot(t["kdec"], Vn, _TN)
            yield
            state[sl, :] = S0 * jnp.exp(t["g_last"]) + s_new
            o_ref[:, sl] = o
            r = lax.rsqrt(jnp.mean(o * o, axis=-1, keepdims=True) + EPS)
            on_ref[:, sl] = ((o * r * nw_ref[...]) * _silu(z_ref[:, sl].astype(F32))).astype(BF16)

        _lockstep(head(hq, j) for hq in range(G) for j in range(J))

    q_spec, k_spec, v_spec, z_spec, ba_spec, p_spec, nw_spec, s_spec, t_spec = _gdn_specs(L, lambda c: c)
    o_spec = pl.BlockSpec((L, G * W), lambda h, c: (c, h))
    return pl.pallas_call(
        body, name=name, grid=(GDN_QK_HEADS // G, nc),
        in_specs=[q_spec, k_spec, v_spec, z_spec, ba_spec, ba_spec, p_spec, nw_spec],
        out_specs=(o_spec, o_spec, s_spec, t_spec),
        out_shape=(jax.ShapeDtypeStruct((S, GDN_QK_HEADS * W), F32), jax.ShapeDtypeStruct((S, GDN_QK_HEADS * W), BF16),
                   jax.ShapeDtypeStruct((GDN_QK_HEADS, nc, W, Hd), F32),
                   jax.ShapeDtypeStruct((GDN_QK_HEADS * J, nc, L, L), F32)),
        scratch_shapes=[pltpu.VMEM((G * W, Hd), F32)],
        compiler_params=_params("parallel", "arbitrary"))(conv, conv, conv, proj, b_raw, a_raw, pvec, nw)


def _gdn_bwd(don, conv, proj, b_raw, a_raw, pvec, nw, o_pre, states, t_inv, d_proj, d_conv, name):
    S = conv.shape[0]
    L, Hd, J, G = GDN_CHUNK, GDN_HEAD, GDN_V_PER_QK, GDN_QK_PER_STEP
    W = J * Hd
    nc = S // L
    assert G == GDN_QK_HEADS
    CK0, CV0 = GDN_QK_HEADS * Hd, 2 * GDN_QK_HEADS * Hd

    def body(don_ref, q_ref, k_ref, v_ref, z_ref, b_ref, a_ref, p_ref, nw_ref, o_ref, s0_ref, t_ref, _, _2,
             dact_ref, dz_ref, db_ref, da_ref, dp_ref, dnw_ref, dstate):
        c = pl.program_id(1)

        @pl.when(c == 0)
        def _():
            dstate[...] = jnp.zeros_like(dstate)
            dp_ref[...] = jnp.zeros_like(dp_ref)
            dnw_ref[...] = jnp.zeros_like(dnw_ref)

        ii, jj = _iota(L, L, 0), _iota(L, L, 1)
        last = (_iota(L, 1, 0) == L - 1)
        res = {}

        def head(hq, j):
            hd = hq * J + j
            hsl, sl = slice(hq * Hd, (hq + 1) * Hd), slice(hd * Hd, (hd + 1) * Hd)
            qa, ka = q_ref[:, hsl].astype(F32), k_ref[:, hsl].astype(F32)
            t = yield from _gdn_gates(qa, ka, b_ref[hq][:, j:j + 1], a_ref[hq][:, j:j + 1], p_ref[hq], j, ii, jj)
            q, k, beta, eg, Dm, kbeta, kdec = (t[nm] for nm in ("q", "k", "beta", "eg", "Dm", "kbeta", "kdec"))
            T = t_ref[hd, 0]
            v, z, o = v_ref[:, sl].astype(F32), z_ref[:, sl].astype(F32), o_ref[:, sl]
            S0, dS1 = s0_ref[hq, 0, j * Hd:(j + 1) * Hd, :], dstate[sl, :]
            sz = _silu(z)
            r = lax.rsqrt(jnp.mean(o * o, axis=-1, keepdims=True) + EPS)
            oh = o * r
            d_on = don_ref[:, sl]
            dz_ref[:, sl] = (d_on * (oh * nw_ref[...]) * _dsilu(z)).astype(BF16)
            dn = d_on * sz
            dnw_part = _colsum(dn * oh)
            doh = dn * nw_ref[...]
            dO = r * (doh - oh * jnp.mean(doh * oh, axis=-1, keepdims=True))
            Rw = kbeta * eg
            qe = q * eg
            U = _dot2(T, v * beta)
            Wm = _dot2(T, Rw)
            KK = _dot(kbeta, k, _NT)
            QK = _dot(q, k, _NT)
            o_inter = _dot(qe, S0)
            dq_s = _dot(dO, S0, _NT)
            dS_q = _dot(qe, dO, _TN)
            yield
            Am = jnp.where(ii > jj, KK * Dm, 0.0)
            Pm = QK * Dm
            Vn = U - _dot(Wm, S0)
            dVn_s = _dot(kdec, dS1)
            yield
            dVn = _dot(Pm, dO, _TN) + dVn_s
            dP = _dot(dO, Vn, _NT)
            dKd = _dot(Vn, dS1, _NT)
            yield
            dQK = dP * Dm
            dq = _dot(dQK, k) + dq_s * eg
            dk = _dot(dQK, q, _TN) + dKd * t["wdec"]
            dstate[sl, :] = jnp.exp(t["g_last"]) * dS1 + dS_q - _dot(Wm, dVn, _TN)
            dW = -_dot(dVn, S0, _NT)
            dRu = _dot2(T, dVn, _TN)
            yield
            dRw = _dot2(T, dW, _TN)
            dA_u = _dot(dRu, U, _NT)
            yield
            dA = jnp.where(ii > jj, -(dA_u + _dot(dRw, Wm, _NT)), 0.0)
            yield
            dKK = dA * Dm
            dkbeta = _dot(dKK, k) + dRw * eg
            dk = dk + _dot(dKK, kbeta, _TN)
            yield
            dk = dk + dkbeta * beta
            dbeta = _rowsum(dkbeta * k) + _rowsum(dRu * v)
            dact_ref[:, CV0 + hd * Hd:CV0 + (hd + 1) * Hd] = (dRu * beta).astype(BF16)
            Q = dA * Am + dP * Pm
            rho = _rowsum(dKd * kdec)
            d_glast = _colsum(rho) + jnp.exp(t["g_last"]) * _total(dS1 * S0)
            q_sums = _row_col_sums(Q)
            rest = _rowsum(dRw * Rw) + _rowsum(dO * o_inter) - rho + jnp.where(last, d_glast, 0.0)
            yield
            dg = _rev_cumsum_col(q_sums + rest, ii, jj)
            yield
            da_raw = dg * t["neg_ea"] * _sigmoid(t["sp_in"])
            res[hq, j] = dict(dq=dq, dk=dk, db=dbeta * beta * (1.0 - beta), da=da_raw, d_bias=_colsum(da_raw),
                              d_alog=_colsum(dg * t["g"]), dnw=dnw_part, rq=t["rq"], rk=t["rk"], k=k, qh=qa * t["rq"])

        _lockstep(head(hq, j) for hq in range(G) for j in range(J))
        for hq in range(G):
            parts = [res[hq, j] for j in range(J)]
            hsl = slice(hq * Hd, (hq + 1) * Hd)
            p0 = parts[0]
            dqh = sum(pt["dq"] for pt in parts) * (GDN_HEAD ** -0.5)
            dkn = sum(pt["dk"] for pt in parts)
            dact_ref[:, hsl] = (p0["rq"] * (dqh - p0["qh"] * _rowsum(dqh * p0["qh"]))).astype(BF16)
            dact_ref[:, CK0 + hq * Hd:CK0 + (hq + 1) * Hd] = (
                p0["rk"] * (dkn - p0["k"] * _rowsum(dkn * p0["k"]))).astype(BF16)
            db_ref[hq] = jnp.concatenate([pt["db"] for pt in parts], axis=1)
            da_ref[hq] = jnp.concatenate([pt["da"] for pt in parts], axis=1)
            dp_ref[hq] += jnp.concatenate([jnp.concatenate([pt["d_bias"] for pt in parts], axis=1),
                                           jnp.concatenate([pt["d_alog"] for pt in parts], axis=1)], axis=0)
            dnw_ref[hq] += sum(pt["dnw"] for pt in parts)

    rev = lambda c: nc - 1 - c
    q_spec, k_spec, v_spec, z_spec, ba_spec, p_spec, nw_spec, s_spec, t_spec = _gdn_specs(L, rev)
    o_spec = pl.BlockSpec((L, G * W), lambda h, c: (rev(c), h))
    dnw_spec = pl.BlockSpec((G, 1, Hd), lambda h, c: (h, 0, 0))
    half = d_conv.shape[1] // 2
    return pl.pallas_call(
        body, name=name, grid=(GDN_QK_HEADS // G, nc),
        in_specs=[o_spec, q_spec, k_spec, v_spec, z_spec, ba_spec, ba_spec, p_spec, nw_spec, o_spec, s_spec, t_spec,
                  D_PROJ_ANY, D_PROJ_ANY],
        out_specs=(pl.BlockSpec((L, half), lambda h, c: (rev(c), 1)), z_spec, ba_spec, ba_spec, p_spec, dnw_spec),
        out_shape=(jax.ShapeDtypeStruct(d_conv.shape, BF16), jax.ShapeDtypeStruct(d_proj.shape, BF16),
                   jax.ShapeDtypeStruct((GDN_QK_HEADS, S, J), F32), jax.ShapeDtypeStruct((GDN_QK_HEADS, S, J), F32),
                   jax.ShapeDtypeStruct((GDN_QK_HEADS, 2, J), F32), jax.ShapeDtypeStruct((GDN_QK_HEADS, 1, Hd), F32)),
        scratch_shapes=[pltpu.VMEM((G * W, Hd), F32)], input_output_aliases={12: 1, 13: 0},
        compiler_params=_params("parallel", "arbitrary"))(don, conv, conv, conv, proj, b_raw, a_raw, pvec, nw, o_pre,
                                                          states, t_inv, d_proj, d_conv)


def _ada_fwd(c_all, w_loc, b_loc, name):
    n = w_loc.shape[1]

    def body(c_ref, w_ref, b_ref, o_ref):
        o_ref[...] = _dot3(_silu(c_ref[...]), w_ref[...]) + b_ref[...]

    return pl.pallas_call(body, name=name, out_shape=jax.ShapeDtypeStruct((N_DEV, n), F32),
                          compiler_params=pltpu.CompilerParams(vmem_limit_bytes=VMEM_LIMIT))(c_all, w_loc, b_loc)


def _ada_bwd(c_all_t, dmod_cols, name):
    Dm, n = c_all_t.shape[0], dmod_cols.shape[1]

    def body(c_ref, d_ref, o_ref):
        ca = _silu(c_ref[...])
        acc = ca[:, 0:1] * d_ref[0:1, :]
        for i in range(1, N_DEV):
            acc = acc + ca[:, i:i + 1] * d_ref[i:i + 1, :]
        o_ref[...] = acc

    return pl.pallas_call(body, name=name, out_shape=jax.ShapeDtypeStruct((Dm, n), F32),
                          compiler_params=pltpu.CompilerParams(vmem_limit_bytes=VMEM_LIMIT))(c_all_t, dmod_cols)


ADAM_BLOCK_BYTES = 12 * 1024 * 1024


def _adam(contrib, w, m, v, name):
    n, R, C = contrib.shape
    tr = R
    while tr % 16 == 0 and (n + 7) * tr * C * 4 > ADAM_BLOCK_BYTES:
        tr //= 2

    def body(c_ref, w_ref, m_ref, v_ref, g_ref, d_ref, nm_ref, nv_ref):
        g = c_ref[0].astype(F32)
        for i in range(1, n):
            g = g + c_ref[i].astype(F32)
        nm = ADAM_B1 * m_ref[...] + (1.0 - ADAM_B1) * g
        nv = ADAM_B2 * v_ref[...] + (1.0 - ADAM_B2) * (g * g)
        m_hat = nm / (1.0 - ADAM_B1 ** ADAM_STEP)
        v_hat = nv / (1.0 - ADAM_B2 ** ADAM_STEP)
        g_ref[...] = g
        d_ref[...] = -ADAM_LR * (m_hat / (jnp.sqrt(v_hat) + ADAM_EPS) + ADAM_WD * w_ref[...])
        nm_ref[...] = nm
        nv_ref[...] = nv

    spec = pl.BlockSpec((tr, C), lambda i: (i, 0))
    shp = jax.ShapeDtypeStruct((R, C), F32)
    return pl.pallas_call(
        body, name=name, grid=(R // tr,), in_specs=[pl.BlockSpec((n, tr, C), lambda i: (0, i, 0)), spec, spec, spec],
        out_specs=(spec,) * 4, out_shape=(shp,) * 4, compiler_params=_params("parallel"))(contrib, w, m, v)


def _exchange(arrays, modes, name, chips=False):
    n = len(arrays)
    out_shape = tuple(jax.ShapeDtypeStruct((N_DEV,) + a.shape if md == "gather" else a.shape, a.dtype)
                      for a, md in zip(arrays, modes))

    def body(*refs):
        ins, outs = refs[:n], refs[n:2 * n]
        send_sems, recv_sems, loc_sems = refs[2 * n:]
        me, peers = _peer_table(chips)

        def src(k, slot):
            return ins[k] if modes[k] == "gather" else ins[k].at[slot]

        def remote(k, m, to_slot, land_slot):
            return pltpu.make_async_remote_copy(
                src_ref=src(k, to_slot), dst_ref=outs[k].at[land_slot], send_sem=send_sems.at[k, m],
                recv_sem=recv_sems.at[k, m], device_id=peers[m][0], device_id_type=pl.DeviceIdType.MESH)

        local = [pltpu.make_async_copy(src(k, me), outs[k].at[me], loc_sems.at[k]) for k in range(n)]
        for cp in local:
            cp.start()
        sends = [remote(k, m, peers[m][1], me) for m in range(len(peers)) for k in range(n)]
        for cp in sends:
            cp.start()
        for m in range(len(peers)):
            for k in range(n):
                remote(k, m, peers[m][1], peers[m][1]).wait_recv()
        for cp in sends:
            cp.wait_send()
        for cp in local:
            cp.wait()

    any_spec = pl.BlockSpec(memory_space=pl.ANY)
    return pl.pallas_call(
        body, name=name, in_specs=[any_spec] * n, out_specs=(any_spec,) * n, out_shape=out_shape,
        scratch_shapes=[pltpu.SemaphoreType.DMA((n, N_DEV - 1)), pltpu.SemaphoreType.DMA((n, N_DEV - 1)),
                        pltpu.SemaphoreType.DMA((n,))])(*arrays)


def _gather_two_level(arrays, name):
    n = len(arrays)
    out_shape = tuple(jax.ShapeDtypeStruct((N_DEV,) + a.shape, a.dtype) for a in arrays)

    def body(*refs):
        ins, outs = refs[:n], refs[n:2 * n]
        send_sems, recv_sems, loc_sems = refs[2 * n:]
        ix, iy, ic = lax.axis_index("x"), lax.axis_index("y"), lax.axis_index("c")
        lin = lambda px, py, pc: 4 * px + 2 * py + pc
        me, sib = lin(ix, iy, ic), (ix, iy, 1 - ic)
        chips = [(1 - ix, iy), (ix, 1 - iy), (1 - ix, 1 - iy)]

        def copy(k, s, block, to, src=None):
            return pltpu.make_async_remote_copy(
                src_ref=outs[k].at[block] if src is None else src, dst_ref=outs[k].at[block],
                send_sem=send_sems.at[k, s], recv_sem=recv_sems.at[k, s], device_id=to,
                device_id_type=pl.DeviceIdType.MESH)

        local = [pltpu.make_async_copy(ins[k], outs[k].at[me], loc_sems.at[k]) for k in range(n)]
        for cp in local:
            cp.start()
        first = [copy(k, 1 + j, me, (cx, cy, ic), src=ins[k]) for j, (cx, cy) in enumerate(chips) for k in range(n)]
        first += [copy(k, 0, me, sib, src=ins[k]) for k in range(n)]
        for cp in first:
            cp.start()
        passed = []
        for j, (cx, cy) in enumerate(chips):
            for k in range(n):
                copy(k, 1 + j, lin(cx, cy, ic), sib).wait_recv()
                passed.append(copy(k, 4 + j, lin(cx, cy, ic), sib))
                passed[-1].start()
        for k in range(n):
            copy(k, 0, lin(*sib), sib).wait_recv()
            for j, (cx, cy) in enumerate(chips):
                copy(k, 4 + j, lin(cx, cy, 1 - ic), sib).wait_recv()
        for cp in first + passed:
            cp.wait_send()
        for cp in local:
            cp.wait()

    any_spec = pl.BlockSpec(memory_space=pl.ANY)
    return pl.pallas_call(
        body, name=name, in_specs=[any_spec] * n, out_specs=(any_spec,) * n, out_shape=out_shape,
        scratch_shapes=[pltpu.SemaphoreType.DMA((n, N_DEV - 1)), pltpu.SemaphoreType.DMA((n, N_DEV - 1)),
                        pltpu.SemaphoreType.DMA((n,))])(*arrays)


def _peer_table(chips=False):
    ix, iy, ic = lax.axis_index("x"), lax.axis_index("y"), lax.axis_index("c")
    peers = []
    for m in ((2, 4, 6) if chips else range(1, N_DEV)):
        px = 1 - ix if m & 4 else ix
        py = 1 - iy if m & 2 else iy
        pc = 1 - ic if m & 1 else ic
        peers.append(((px, py, pc), 2 * px + py if chips else 4 * px + 2 * py + pc))
    return (2 * ix + iy if chips else 4 * ix + 2 * iy + ic), peers


def _exchange_start(arrays, modes, after, name, chips=False):
    n = len(arrays)
    land_shapes = [(N_DEV,) + a.shape if md == "gather" else a.shape for a, md in zip(arrays, modes)]

    def body(*refs):
        ins, lands = refs[:n], refs[n:2 * n]
        send_sems, recv_sems = refs[2 * n + 1], refs[2 * n + 2]
        token = refs[-1]
        me, peers = _peer_table(chips)

        def src(k, slot):
            return ins[k] if modes[k] == "gather" else ins[k].at[slot]

        for peer, slot in peers:
            for k in range(n):
                pltpu.make_async_remote_copy(
                    src_ref=src(k, slot), dst_ref=lands[k].at[me], send_sem=send_sems, recv_sem=recv_sems,
                    device_id=peer, device_id_type=pl.DeviceIdType.MESH).start()
        token[...] = jnp.zeros_like(token)

    hbm = pl.BlockSpec(memory_space=pltpu.HBM)
    sem = pl.BlockSpec(memory_space=pltpu.SEMAPHORE)
    sem_shape = pltpu.SemaphoreType.DMA(())
    operands = [pltpu.with_memory_space_constraint(a, pltpu.HBM) for a in arrays]
    operands += [pltpu.with_memory_space_constraint(lax.empty(s, a.dtype), pltpu.HBM)
                 for s, a in zip(land_shapes, arrays)]
    out = pl.pallas_call(
        body, name=name,
        out_shape=(sem_shape, sem_shape) + tuple(pltpu.HBM(a.shape, a.dtype) for a in arrays)
        + tuple(pltpu.HBM(s, a.dtype) for s, a in zip(land_shapes, arrays)) + (jax.ShapeDtypeStruct((8, 128), F32),),
        in_specs=[hbm] * (2 * n) + [pl.BlockSpec(memory_space=pl.ANY)],
        out_specs=(sem, sem) + (hbm,) * (2 * n) + (pl.BlockSpec(memory_space=pltpu.VMEM),),
        input_output_aliases={i: 2 + i for i in range(2 * n)},
        compiler_params=pltpu.CompilerParams(has_side_effects=pltpu.SideEffectType.DATAFLOW_SIDE_EFFECTING))(
            *operands, after)
    return out[0], out[1], out[2:2 + n], out[2 + n:2 + 2 * n], out[-1]


def _exchange_wait(started, modes, after, name, chips=False):
    send_sems, recv_sems, sent, lands, _ = started
    n = len(sent)

    def body(*refs):
        ins, zones = refs[:n], refs[n:2 * n]
        send_ref, recv_ref = refs[2 * n], refs[2 * n + 1]
        _, peers = _peer_table(chips)

        def src(k, slot):
            return ins[k] if modes[k] == "gather" else ins[k].at[slot]

        for peer, slot in peers:
            for k in range(n):
                cp = pltpu.make_async_remote_copy(
                    src_ref=src(k, slot), dst_ref=zones[k].at[slot], send_sem=send_ref, recv_sem=recv_ref,
                    device_id=peer, device_id_type=pl.DeviceIdType.MESH)
                cp.wait_send()
                cp.wait_recv()

    hbm = pl.BlockSpec(memory_space=pltpu.HBM)
    sem = pl.BlockSpec(memory_space=pltpu.SEMAPHORE)
    out = pl.pallas_call(
        body, name=name,
        out_shape=tuple(pltpu.HBM(a.shape, a.dtype) for a in sent) + tuple(pltpu.HBM(a.shape, a.dtype) for a in lands),
        in_specs=[hbm] * (2 * n) + [sem, sem, pl.BlockSpec(memory_space=pl.ANY)], out_specs=(hbm,) * (2 * n),
        input_output_aliases={i: i for i in range(2 * n)},
        compiler_params=pltpu.CompilerParams(has_side_effects=pltpu.SideEffectType.DATAFLOW_SIDE_EFFECTING))(
            *sent, *lands, send_sems, recv_sems, after)
    ix, iy, ic = lax.axis_index("x"), lax.axis_index("y"), lax.axis_index("c")
    me = 2 * ix + iy if chips else 4 * ix + 2 * iy + ic
    filled = []
    for k in range(n):
        own = sent[k] if modes[k] == "gather" else lax.dynamic_index_in_dim(sent[k], me, axis=0, keepdims=False)
        filled.append(lax.dynamic_update_index_in_dim(out[n + k], own, me, axis=0))
    return filled


def _swap_sibling(to_c0, to_c1, name):
    def body(c0_ref, c1_ref, out_ref, send_sem, recv_sem):
        ix, iy, ic = lax.axis_index("x"), lax.axis_index("y"), lax.axis_index("c")

        def copy(src):
            return pltpu.make_async_remote_copy(src_ref=src, dst_ref=out_ref, send_sem=send_sem, recv_sem=recv_sem,
                                                device_id=(ix, iy, 1 - ic), device_id_type=pl.DeviceIdType.MESH)

        @pl.when(ic == 0)
        def _():
            copy(c1_ref).start()

        @pl.when(ic == 1)
        def _():
            copy(c0_ref).start()

        copy(c0_ref).wait()

    any_spec = pl.BlockSpec(memory_space=pl.ANY)
    return pl.pallas_call(body, name=name, in_specs=[any_spec, any_spec], out_specs=any_spec,
                          out_shape=jax.ShapeDtypeStruct(to_c0.shape, to_c0.dtype),
                          scratch_shapes=[pltpu.SemaphoreType.DMA, pltpu.SemaphoreType.DMA])(to_c0, to_c1)


def _add_pair(to_c0, to_c1, got, name):
    n, R, C = got.shape
    tr = _blk(R, 256)

    def body(c0_ref, c1_ref, got_ref, o_ref):
        ic = lax.axis_index("c")

        @pl.when(ic == 0)
        def _():
            o_ref[...] = (c0_ref[...].astype(F32) + got_ref[...].astype(F32)).astype(o_ref.dtype)

        @pl.when(ic == 1)
        def _():
            o_ref[...] = (c1_ref[...].astype(F32) + got_ref[...].astype(F32)).astype(o_ref.dtype)

    spec = pl.BlockSpec((1, tr, C), lambda i, j: (i, j, 0))
    return pl.pallas_call(body, name=name, grid=(n, R // tr), in_specs=[spec, spec, spec], out_specs=spec,
                          out_shape=jax.ShapeDtypeStruct(got.shape, got.dtype),
                          compiler_params=_params("parallel", "parallel"))(to_c0, to_c1, got)


W_IN_SPLITS = (0, 2048, 6144, 6176, 10272, 12320, 12336, 12352, 13376, 14400)
N_REPLICATED = 16640
REPLICATED = ("b_ada", "norm_mix_pre", "norm_mix_post", "ssm_conv_b", "ssm_dt_bias", "ssm_A_log", "ssm_D",
              "ssm_norm_w", "gdn_dt_bias", "gdn_A_log", "gdn_norm_w", "norm_mlp_pre", "norm_mlp_post")
WEIGHTS = ("w_ada", "b_ada", "norm_mix_pre", "norm_mix_post", "w_in", "ssm_conv_w", "ssm_conv_b", "ssm_dt_bias",
           "ssm_A_log", "ssm_D", "ssm_norm_w", "gdn_conv_w", "gdn_dt_bias", "gdn_A_log", "gdn_norm_w", "w_ssm_up",
           "w_gdn_up", "w_out", "norm_mlp_pre", "norm_mlp_post", "w_mlp_up", "w_mlp_down")


def _cols_of_shards(g, a, b):
    width, pieces = g.shape[2], []
    while a < b:
        i = a // width
        hi = min(b, (i + 1) * width)
        pieces.append(g[i][:, a - i * width:hi - i * width])
        a = hi
    return pieces


ORIG_SEGMENTS = ((0, 6144, "main", 0), (6144, 6176, "small", 0), (6176, 12320, "main", 6144),
                 (12320, 12352, "small", 32), (12352, 14400, "main", 12288))


def _orig_cols(main_cols, small_cols, a, b):
    pieces = []
    for s0, s1, which, off in ORIG_SEGMENTS:
        lo, hi = max(a, s0), min(b, s1)
        if lo < hi:
            pieces.append((main_cols if which == "main" else small_cols)[:, off + lo - s0:off + hi - s0])
    return jnp.concatenate(pieces, axis=1)


def _by_cols(t):
    return t.transpose(1, 0, 2).reshape(t.shape[1], N_DEV * t.shape[2])


def _to_col_shards(t):
    R, C8 = t.shape
    return t.reshape(R, N_DEV, C8 // N_DEV).transpose(1, 0, 2)


def _heads_first(t, groups):
    S = t.shape[0]
    return t.reshape(S, groups, t.shape[1] // groups).transpose(1, 0, 2)


def _heads_last(t):
    return t.transpose(1, 0, 2).reshape(t.shape[1], t.shape[0] * t.shape[2])


def kernel(x, c, w_ada, b_ada, norm_mix_pre, norm_mix_post, w_in, ssm_conv_w, ssm_conv_b, ssm_dt_bias, ssm_A_log, ssm_D, ssm_norm_w, gdn_conv_w, gdn_dt_bias, gdn_A_log, gdn_norm_w, w_ssm_up, w_gdn_up, w_out, norm_mlp_pre, norm_mlp_post, w_mlp_up, w_mlp_down, loss_target, m_w_ada, m_b_ada, m_norm_mix_pre, m_norm_mix_post, m_w_in, m_ssm_conv_w, m_ssm_conv_b, m_ssm_dt_bias, m_ssm_A_log, m_ssm_D, m_ssm_norm_w, m_gdn_conv_w, m_gdn_dt_bias, m_gdn_A_log, m_gdn_norm_w, m_w_ssm_up, m_w_gdn_up, m_w_out, m_norm_mlp_pre, m_norm_mlp_post, m_w_mlp_up, m_w_mlp_down, v_w_ada, v_b_ada, v_norm_mix_pre, v_norm_mix_post, v_w_in, v_ssm_conv_w, v_ssm_conv_b, v_ssm_dt_bias, v_ssm_A_log, v_ssm_D, v_ssm_norm_w, v_gdn_conv_w, v_gdn_dt_bias, v_gdn_A_log, v_gdn_norm_w, v_w_ssm_up, v_w_gdn_up, v_w_out, v_norm_mlp_pre, v_norm_mlp_post, v_w_mlp_up, v_w_mlp_down):
    S, Dm = x.shape[1], D_MODEL
    me = 4 * lax.axis_index("x") + 2 * lax.axis_index("y") + lax.axis_index("c")
    x2, tgt = x[0], loss_target[0]
    n_ada = w_ada.shape[2]
    given = dict(
        w_ada=(w_ada, m_w_ada, v_w_ada), b_ada=(b_ada, m_b_ada, v_b_ada),
        norm_mix_pre=(norm_mix_pre, m_norm_mix_pre, v_norm_mix_pre),
        norm_mix_post=(norm_mix_post, m_norm_mix_post, v_norm_mix_post), w_in=(w_in, m_w_in, v_w_in),
        ssm_conv_w=(ssm_conv_w, m_ssm_conv_w, v_ssm_conv_w), ssm_conv_b=(ssm_conv_b, m_ssm_conv_b, v_ssm_conv_b),
        ssm_dt_bias=(ssm_dt_bias, m_ssm_dt_bias, v_ssm_dt_bias), ssm_A_log=(ssm_A_log, m_ssm_A_log, v_ssm_A_log),
        ssm_D=(ssm_D, m_ssm_D, v_ssm_D), ssm_norm_w=(ssm_norm_w, m_ssm_norm_w, v_ssm_norm_w),
        gdn_conv_w=(gdn_conv_w, m_gdn_conv_w, v_gdn_conv_w), gdn_dt_bias=(gdn_dt_bias, m_gdn_dt_bias, v_gdn_dt_bias),
        gdn_A_log=(gdn_A_log, m_gdn_A_log, v_gdn_A_log), gdn_norm_w=(gdn_norm_w, m_gdn_norm_w, v_gdn_norm_w),
        w_ssm_up=(w_ssm_up, m_w_ssm_up, v_w_ssm_up), w_gdn_up=(w_gdn_up, m_w_gdn_up, v_w_gdn_up),
        w_out=(w_out, m_w_out, v_w_out), norm_mlp_pre=(norm_mlp_pre, m_norm_mlp_pre, v_norm_mlp_pre),
        norm_mlp_post=(norm_mlp_post, m_norm_mlp_post, v_norm_mlp_post), w_mlp_up=(w_mlp_up, m_w_mlp_up, v_w_mlp_up),
        w_mlp_down=(w_mlp_down, m_w_mlp_down, v_w_mlp_down))

    (c_all, scw, gcw, g_in) = _gather_two_level([c, ssm_conv_w[0], gdn_conv_w[0], w_in[0].astype(BF16)], "gather_w_in")
    c_all = c_all.reshape(N_DEV, Dm)
    sp = W_IN_SPLITS
    w_main = jnp.concatenate(_cols_of_shards(g_in, sp[0], sp[2]) + _cols_of_shards(g_in, sp[3], sp[5])
                             + _cols_of_shards(g_in, sp[7], sp[9]), axis=1)
    w_small = jnp.concatenate(_cols_of_shards(g_in, sp[2], sp[3]) + _cols_of_shards(g_in, sp[5], sp[7])
                              + [jnp.zeros((Dm, N_SMALL - 64), BF16)], axis=1)
    conv_w = jnp.concatenate([_by_cols(scw), _by_cols(gcw)], axis=1)
    conv_b = jnp.concatenate([ssm_conv_b, jnp.zeros_like(ssm_conv_b)], axis=1)

    b_loc = lax.dynamic_slice(b_ada, (0, me * n_ada), (1, n_ada))
    mod_part = _ada_fwd(c_all, w_ada[0], b_loc, "ada_fwd")
    (mod_rows,) = _exchange([mod_part.reshape(N_DEV, 1, n_ada)], ["a2a"], "exchange_mod")
    rest = _exchange_start([w_ssm_up[0].astype(BF16), w_gdn_up[0].astype(BF16), w_out[0].astype(BF16),
                            w_mlp_up[0].astype(BF16), w_mlp_down[0].astype(BF16)], ["gather"] * 5, mod_rows,
                           "gather_rest_start")
    mod = mod_rows.reshape(1, 6 * Dm) + rest[4][0:1, 0:1]
    sh1, sc1, g1, sh2, sc2, g2 = [mod[:, i * Dm:(i + 1) * Dm] for i in range(6)]

    h = _pre_fwd(x2, norm_mix_pre, sc1, sh1, "pre_mix")
    proj = _mm(h, w_main, S, N_MAIN, Dm, mode="nn", out_dtype=BF16, name="proj_main")
    small = _mm(h, w_small, S, N_SMALL, Dm, mode="nn", out_dtype=F32, name="proj_small")
    conv = _conv_fwd(proj, conv_w, conv_b, "conv_fwd")
    dt_g, b_g, a_g = _heads_first(small[:, 0:32], 8), _heads_first(small[:, 32:48], 8), _heads_first(small[:, 48:64], 8)
    pv_ssm = jnp.stack([ssm_dt_bias.reshape(8, 4), ssm_A_log.reshape(8, 4), ssm_D.reshape(8, 4)], axis=1)
    nw_ssm = ssm_norm_w.reshape(8, 1, SSM_GROUP_WIDTH)
    pv_gdn = jnp.stack([gdn_dt_bias.reshape(8, 2), gdn_A_log.reshape(8, 2)], axis=1)
    y_ssd, ysn, st_ssm = _ssd_fwd(conv, proj, dt_g, pv_ssm, nw_ssm, "ssd_fwd")
    o_pre, ogn, st_gdn, t_inv = _gdn_fwd(conv, proj, b_g, a_g, pv_gdn, gdn_norm_w, "gdn_fwd")
    g_su, g_gu, g_out, g_mu, g_md = _exchange_wait(rest, ["gather"] * 5, ogn, "gather_rest_wait")
    w_su, w_gu = g_su.reshape(2 * Dm, Dm), g_gu.reshape(2 * Dm, Dm)
    w_o, w_mu, w_md = g_out.reshape(Dm, Dm), _by_cols(g_mu), g_md.reshape(4 * Dm, Dm)
    ys = _mm(ysn, w_su, S, Dm, 2 * Dm, mode="nn", out_dtype=F32, name="ssm_up")
    yg, merged = _gdn_up_merge(ogn, w_gu, ys, proj, "gdn_up_merge")
    mo, x1, h2 = _mix_out_post_pre(merged, w_o, x2, norm_mix_post, g1, norm_mlp_pre, sc2, sh2, "mix_out_post_pre")
    u, act = _mm(h2, w_mu, S, 4 * Dm, Dm, mode="nn", out_dtype=F32, epi="relu2", name="mlp_up")
    y_mlp = _mm(act, w_md, S, Dm, 4 * Dm, mode="nn", out_dtype=F32, name="mlp_down")
    dx2, loss_loc, dy, dg2, dw_post2 = _final_fwd_bwd(x1, y_mlp, norm_mlp_post, g2, tgt, "post_mlp_loss_bwd")

    du = _mm(dy, w_md, S, 4 * Dm, Dm, mode="nt", out_dtype=BF16, epi="drelu2", extra=u, name="mlp_down_dx")
    gw_md = _mm(act, dy, 4 * Dm, Dm, S, mode="tn", out_dtype=BF16, name="mlp_down_dw")
    dh2 = _mm(du, w_mu, S, Dm, 4 * Dm, mode="nt", out_dtype=F32, name="mlp_up_dx")
    gw_mu = _mm(h2, du, Dm, 4 * Dm, S, mode="tn", out_dtype=BF16, name="mlp_up_dw")
    mlp_x = _exchange_start([_to_col_shards(gw_mu), gw_md.reshape(N_DEV, -1, Dm)], ["a2a"] * 2, gw_md,
                            "grads_mlp_start")
    dx1, dsh2, dsc2, dw_pre2 = _pre_bwd(dh2, x1, norm_mlp_pre, sc2 + mlp_x[4][0:1, 0:1], dx2, "pre_mlp_bwd")
    dmo, dg1, dw_post1 = _post_bwd(dx1, mo, norm_mix_post, g1, "post_mix_bwd")
    gw_o = _mm(merged, dmo, Dm, Dm, S, mode="tn", out_dtype=BF16, name="mix_out_dw")
    dys, dyg, d_proj = _mix_out_dx_merge_bwd(dmo, w_o, ys, yg, proj, lax.empty((S, N_MAIN), BF16), "mix_out_dx_merge")
    dysn = _mm(dys, w_su, S, 2 * Dm, Dm, mode="nt", out_dtype=F32, name="ssm_up_dx")
    gw_su = _mm(ysn, dys, 2 * Dm, Dm, S, mode="tn", out_dtype=BF16, name="ssm_up_dw")
    dogn = _mm(dyg, w_gu, S, 2 * Dm, Dm, mode="nt", out_dtype=F32, name="gdn_up_dx")
    gw_gu = _mm(ogn, dyg, 2 * Dm, Dm, S, mode="tn", out_dtype=BF16, name="gdn_up_dw")
    mix_x = _exchange_start([gw_su.reshape(N_DEV, -1, Dm), gw_gu.reshape(N_DEV, -1, Dm), gw_o.reshape(N_DEV, -1, Dm)],
                            ["a2a"] * 3, gw_gu, "grads_mix_start")
    d_conv, d_proj, ddt_g, dpv_ssm, dnw_ssm = _ssd_bwd(dysn, conv, proj, dt_g, pv_ssm + mix_x[4][0, 0], nw_ssm, y_ssd,
                                                       st_ssm, d_proj, lax.empty(conv.shape, BF16), "ssd_bwd")
    d_conv, d_proj, db_g, da_g, dpv_gdn, dnw_gdn = _gdn_bwd(dogn, conv, proj, b_g, a_g, pv_gdn, gdn_norm_w, o_pre,
                                                            st_gdn, t_inv, d_proj, d_conv, "gdn_bwd")
    d_proj, dconv_w, dconv_b = _conv_bwd(d_conv, proj, conv_w, conv_b, 0, d_proj, "conv_bwd")
    d_small = jnp.concatenate([_heads_last(ddt_g), _heads_last(db_g), _heads_last(da_g),
                               jnp.zeros((S, N_SMALL - 64), F32)], axis=1).astype(BF16)
    gw_small = _mm(h, d_small, Dm, N_SMALL, S, mode="tn", out_dtype=BF16, name="proj_small_dw")
    main_cols = _mm(h, d_proj, Dm, N_MAIN, S, mode="tn", out_dtype=BF16, name="proj_main_dw")
    n_shard = w_in.shape[2]
    slabs = [_orig_cols(main_cols, gw_small, i * n_shard, (i + 1) * n_shard) for i in range(N_DEV)]
    to_c0, to_c1 = jnp.stack(slabs[0::2]), jnp.stack(slabs[1::2])
    chip_sum = _add_pair(to_c0, to_c1, _swap_sibling(to_c0, to_c1, "grads_w_in_pair"), "grads_w_in_pair_sum")
    in_x = _exchange_start([chip_sum], ["a2a"], gw_small, "grads_w_in_start", chips=True)
    dh = _mm(d_small, w_small + in_x[4][0:1, 0:1].astype(BF16), S, Dm, N_SMALL, mode="nt", out_dtype=F32,
             name="proj_small_dx")
    dh = _mm(d_proj, w_main, S, Dm, N_MAIN, mode="nt", out_dtype=F32, add=dh, name="proj_main_dx")
    dx, dsh1, dsc1, dw_pre1 = _pre_bwd(dh, x2, norm_mix_pre, sc1, dx1, "pre_mix_bwd")
    r_mu, r_md = _exchange_wait(mlp_x, ["a2a"] * 2, dx, "grads_mlp_wait")
    r_su, r_gu, r_o = _exchange_wait(mix_x, ["a2a"] * 3, dx, "grads_mix_wait")

    dconv_b = dconv_b[:, :ssm_conv_b.shape[1]]
    dmod = jnp.concatenate([dsh1, dsc1, dg1, dsh2, dsc2, dg2], axis=1)
    small_vec = jnp.concatenate(
        [dmod, dw_pre1, dw_post1, dconv_b, dpv_ssm[:, 0].reshape(1, 32), dpv_ssm[:, 1].reshape(1, 32),
         dpv_ssm[:, 2].reshape(1, 32), dnw_ssm.reshape(1, 2048), dpv_gdn[:, 0].reshape(1, 16),
         dpv_gdn[:, 1].reshape(1, 16), jnp.sum(dnw_gdn, axis=0), dw_pre2, dw_post2, dconv_w.reshape(1, -1)], axis=1)
    n_vec = small_vec.shape[1]
    small_vec = jnp.pad(small_vec, ((0, 0), (0, (-n_vec) % 1024))).reshape(-1, 1024)
    (small_all,) = _exchange([small_vec], ["gather"], "gather_small_grads")
    small_all = small_all.reshape(N_DEV, -1)
    dmod_cols = lax.dynamic_slice(small_all, (0, me * n_ada), (N_DEV, n_ada))
    gw_ada = _ada_bwd(c_all.T, dmod_cols, "ada_bwd")
    conv_all = small_all[:, N_REPLICATED:n_vec].reshape(N_DEV, CONV_K, 2 * N_DEV * 512)
    conv_contrib = jnp.concatenate(
        [lax.dynamic_slice(conv_all, (0, 0, me * 512), (N_DEV, CONV_K, 512)),
         lax.dynamic_slice(conv_all, (0, 0, N_DEV * 512 + me * 512), (N_DEV, CONV_K, 512))], axis=1)
    rep_contrib = small_all[:, :N_REPLICATED].reshape(N_DEV, N_REPLICATED // 128, 128)

    results = {}

    def adam_big(nm, contrib):
        w3 = given[nm]
        res = _adam(contrib, w3[0][0], w3[1][0], w3[2][0], "adam_" + nm)
        results[nm] = tuple(r.reshape(w3[0].shape) for r in res)

    adam_big("w_ada", gw_ada[None])
    adam_big("w_ssm_up", r_su)
    adam_big("w_gdn_up", r_gu)
    adam_big("w_out", r_o)
    adam_big("w_mlp_up", r_mu)
    adam_big("w_mlp_down", r_md)
    (r_in,) = _exchange_wait(in_x, ["a2a"], results["w_mlp_down"][0], "grads_w_in_wait", chips=True)
    adam_big("w_in", r_in)
    packed = [jnp.concatenate([given[nm][i] for nm in REPLICATED], axis=1).reshape(N_REPLICATED // 128, 128)
              for i in range(3)]
    rep_res = _adam(rep_contrib, packed[0], packed[1], packed[2], "adam_replicated")
    pos = 0
    for nm in REPLICATED:
        size = given[nm][0].shape[1]
        results[nm] = tuple(r.reshape(1, N_REPLICATED)[:, pos:pos + size] for r in rep_res)
        pos += size
    conv_wmv = [jnp.concatenate([given["ssm_conv_w"][i][0], given["gdn_conv_w"][i][0]], axis=0) for i in range(3)]
    conv_res = _adam(conv_contrib, conv_wmv[0], conv_wmv[1], conv_wmv[2], "adam_conv_w")
    results["ssm_conv_w"] = tuple(r[None, :CONV_K] for r in conv_res)
    results["gdn_conv_w"] = tuple(r[None, CONV_K:] for r in conv_res)

    loss = lax.psum(loss_loc[0, 0], ("x", "y", "c"))
    return (loss, dx[None]) + tuple(results[nm][i] for i in range(4) for nm in WEIGHTS)
```

```python
import jax
import jax.numpy as jnp
from jax import lax
from jax.experimental import pallas as pl
from jax.experimental.pallas import tpu as pltpu

F32 = jnp.float32
BF16 = jnp.bfloat16
N_DEV = 8
D_MODEL = 1024
EPS = 1e-6
CONV_K = 4
SSM_CHUNK = 128
SSM_HEAD_DIM = 64
SSM_D_STATE = 128
SSM_GROUPS = 8
SSM_HEADS_PER_GROUP = 4
SSM_GROUP_WIDTH = SSM_HEADS_PER_GROUP * SSM_HEAD_DIM
SSM_GROUPS_PER_STEP = 8
GDN_CHUNK = 64
GDN_HEAD = 128
GDN_QK_HEADS = 8
GDN_V_PER_QK = 2
GDN_QK_PER_STEP = 8
GDN_INV_BLOCK = 16
C_ZS, C_XBC, C_QKV, C_ZG, C_GS, C_GG, N_MAIN = 0, 2048, 6144, 10240, 12288, 13312, 14336
N_SMALL = 128
ADAM_LR, ADAM_B1, ADAM_B2, ADAM_EPS, ADAM_WD, ADAM_STEP = 0.001, 0.9, 0.999, 1e-08, 0.01, 10
VMEM_LIMIT = 56 * 1024 * 1024
MM_WHOLE_K = 4096
MM_SPLIT_K = 2048
MM_WIDE_N = 2048
NEG_INF = float("-inf")

_NT = (((1,), (1,)), ((), ()))
_NN = (((1,), (0,)), ((), ()))
_TN = (((0,), (0,)), ((), ()))


def _params(*sem):
    return pltpu.CompilerParams(dimension_semantics=sem, vmem_limit_bytes=VMEM_LIMIT)


def _dot(a, b, dims=_NN):
    return lax.dot_general(a.astype(BF16), b.astype(BF16), dims, preferred_element_type=F32)


def _split(a):
    hi = a.astype(BF16)
    return hi, (a - hi.astype(F32)).astype(BF16)


def _dot3(a, b, dims=_NN):
    ah, al = _split(a)
    bh, bl = _split(b)
    d = lambda u, v: lax.dot_general(u, v, dims, preferred_element_type=F32)
    return d(ah, bh) + (d(ah, bl) + d(al, bh))


def _dot2(a, b, dims=_NN):
    ah, al = _split(a)
    bb = b.astype(BF16)
    d = lambda u: lax.dot_general(u, bb, dims, preferred_element_type=F32)
    return d(ah) + d(al)


def _sigmoid(x):
    return 0.5 * jnp.tanh(0.5 * x) + 0.5


def _silu(x):
    return x * _sigmoid(x)


def _dsilu(x):
    s = _sigmoid(x)
    return s * (1.0 + x * (1.0 - s))


def _softplus(x):
    return jnp.maximum(x, 0.0) + jnp.log1p(jnp.exp(-jnp.abs(x)))


def _iota(n, m, d):
    return lax.broadcasted_iota(jnp.int32, (n, m), d)


def _rowsum(x):
    return jnp.sum(x, axis=1, keepdims=True)


def _colsum(x):
    return jnp.sum(x, axis=0, keepdims=True)


def _total(x):
    return _rowsum(_colsum(x))


MXU_LANES = 128


def _parts(x, n):
    out = []
    for _ in range(n):
        p = x.astype(BF16)
        out.append(p)
        x = x - p.astype(F32)
    return out


def _sum_by(m01, x, dims=_NN, n=3):
    return sum(lax.dot_general(m01, p, dims, preferred_element_type=F32) for p in _parts(x, n))


def _row_col_sums(q):
    ones = jnp.ones((q.shape[0], MXU_LANES), BF16)
    acc = 0.0
    for p in _parts(q, 2):
        acc = acc + (lax.dot_general(p, ones, _NN, preferred_element_type=F32)
                     - lax.dot_general(p, ones, _TN, preferred_element_type=F32))
    return acc[:, 0:1]


def _cumsum_forms(col, ii, jj):
    lower = jnp.where(ii >= jj, 1.0, 0.0).astype(BF16)
    cum_col = _sum_by(lower, jnp.broadcast_to(col, (col.shape[0], MXU_LANES)))[:, 0:1]
    cum_row = _colsum(jnp.where(ii <= jj, col, 0.0))
    return cum_col, cum_row


def _rev_cumsum_col(col, ii, jj):
    upper = jnp.where(ii <= jj, 1.0, 0.0).astype(BF16)
    return _sum_by(upper, jnp.broadcast_to(col, (col.shape[0], MXU_LANES)))[:, 0:1]


def _blk(dim, pref):
    return pref if dim % pref == 0 else dim


def _lockstep(gens):
    gens = list(gens)
    while gens:
        alive = []
        for g in gens:
            try:
                next(g)
                alive.append(g)
            except StopIteration:
                pass
        gens = alive


def _mm(a, b, M, N, K, *, mode, out_dtype, name, a_off=(0, 0), b_off=(0, 0), add=None, epi=None, extra=None,
        tm=1024, tn=1024):
    tm, tn = _blk(M, tm), _blk(N, tn)
    tk = K if K <= MM_WHOLE_K else _blk(K, MM_SPLIT_K)
    nk = K // tk
    if mode == "tn":
        a_spec = pl.BlockSpec((tk, tm), lambda i, j, k: (k + a_off[0] // tk, i + a_off[1] // tm))
        assert a_off[0] % tk == 0 and a_off[1] % tm == 0
    else:
        a_spec = pl.BlockSpec((tm, tk), lambda i, j, k: (i + a_off[0] // tm, k + a_off[1] // tk))
        assert a_off[0] % tm == 0 and a_off[1] % tk == 0
    if mode == "nt":
        b_spec = pl.BlockSpec((tn, tk), lambda i, j, k: (j + b_off[0] // tn, k + b_off[1] // tk))
        assert b_off[0] % tn == 0 and b_off[1] % tk == 0
    else:
        b_spec = pl.BlockSpec((tk, tn), lambda i, j, k: (k + b_off[0] // tk, j + b_off[1] // tn))
        assert b_off[0] % tk == 0 and b_off[1] % tn == 0
    dims = {"nn": _NN, "nt": _NT, "tn": _TN}[mode]
    o_spec = pl.BlockSpec((tm, tn), lambda i, j, k: (i, j))
    ins, in_specs = [a, b], [a_spec, b_spec]
    if add is not None:
        ins.append(add)
        in_specs.append(o_spec)
    if extra is not None:
        ins.append(extra)
        in_specs.append(o_spec)
    n_in = len(ins)
    if epi == "relu2":
        out_shape = (jax.ShapeDtypeStruct((M, N), BF16), jax.ShapeDtypeStruct((M, N), BF16))
        out_specs = (o_spec, o_spec)
    else:
        out_shape = jax.ShapeDtypeStruct((M, N), out_dtype)
        out_specs = o_spec

    def body(*refs):
        a_ref, b_ref = refs[0], refs[1]
        outs = refs[n_in:] if nk == 1 else refs[n_in:-1]

        def finish(r):
            pos = 2
            if add is not None:
                r = r + refs[pos][...]
                pos += 1
            if epi == "relu2":
                p = jnp.maximum(r, 0.0)
                outs[0][...] = p.astype(BF16)
                outs[1][...] = (p * p).astype(BF16)
            elif epi == "drelu2":
                outs[0][...] = (r * (2.0 * refs[pos][...].astype(F32))).astype(out_dtype)
            else:
                outs[0][...] = r.astype(out_dtype)

        if nk == 1:
            finish(_dot(a_ref[...], b_ref[...], dims))
            return
        acc = refs[-1]
        k = pl.program_id(2)

        @pl.when(k == 0)
        def _():
            acc[...] = jnp.zeros_like(acc)

        acc[...] += _dot(a_ref[...], b_ref[...], dims)

        @pl.when(k == nk - 1)
        def _():
            finish(acc[...])

    return pl.pallas_call(
        body, name=name, grid=(M // tm, N // tn, nk), in_specs=in_specs, out_specs=out_specs, out_shape=out_shape,
        scratch_shapes=[] if nk == 1 else [pltpu.VMEM((tm, tn), F32)],
        compiler_params=_params("parallel", "parallel", "arbitrary"))(*ins)


def _mm_rows(a, b, M, N, K, *, mode, name, extras, out_shapes, out_specs, epilogue, aliases=None, tm=512):
    tm = _blk(M, tm)
    a_spec = pl.BlockSpec((tm, K), lambda i: (i, 0))
    b_spec = pl.BlockSpec((K, N) if mode == "nn" else (N, K), lambda i: (0, 0))
    dims = _NN if mode == "nn" else _NT
    n_ex = len(extras)

    def body(a_ref, b_ref, *refs):
        epilogue(_dot(a_ref[...], b_ref[...], dims), refs[:n_ex], refs[n_ex:])

    return pl.pallas_call(
        body, name=name, grid=(M // tm,), in_specs=[a_spec, b_spec] + [sp for _, sp in extras],
        out_specs=tuple(out_specs), out_shape=tuple(out_shapes), input_output_aliases=aliases or {},
        compiler_params=_params("parallel"))(a, b, *[x for x, _ in extras])


def _row_spec(tb, d):
    return pl.BlockSpec((tb, d), lambda i: (i, 0))


def _vec_spec(d):
    return pl.BlockSpec((1, d), lambda i: (0, 0))


def _pre_fwd(x, w, sc, sh, name):
    S, Dm = x.shape
    tb = _blk(S, 512)

    def body(x_ref, w_ref, sc_ref, sh_ref, h_ref):
        xv = x_ref[...]
        r = lax.rsqrt(jnp.mean(xv * xv, axis=-1, keepdims=True) + EPS)
        h_ref[...] = ((xv * r * w_ref[...]) * (1.0 + sc_ref[...]) + sh_ref[...]).astype(BF16)

    return pl.pallas_call(
        body, name=name, grid=(S // tb,), in_specs=[_row_spec(tb, Dm)] + [_vec_spec(Dm)] * 3,
        out_specs=_row_spec(tb, Dm), out_shape=jax.ShapeDtypeStruct((S, Dm), BF16),
        compiler_params=_params("parallel"))(x, w, sc, sh)


def _final_fwd_bwd(x, y, w, g, target, name):
    S, Dm = x.shape
    tb = _blk(S, 512)
    nb = S // tb

    def body(x_ref, y_ref, w_ref, g_ref, t_ref, dx_ref, loss_ref, dy_ref, dg_ref, dw_ref, acc):
        i = pl.program_id(0)

        @pl.when(i == 0)
        def _():
            acc[...] = jnp.zeros_like(acc)
            dg_ref[...] = jnp.zeros_like(dg_ref)
            dw_ref[...] = jnp.zeros_like(dw_ref)

        yv = y_ref[...]
        r = lax.rsqrt(jnp.mean(yv * yv, axis=-1, keepdims=True) + EPS)
        yh = yv * r
        n = yh * w_ref[...]
        e = (x_ref[...] + g_ref[...] * n) - t_ref[...]
        dv = e * (1.0 / Dm)
        dx_ref[...] = dv
        acc[...] += _colsum(e * e)
        dg_ref[...] += _colsum(dv * n)
        dn = dv * g_ref[...]
        dw_ref[...] += _colsum(dn * yh)
        dyh = dn * w_ref[...]
        dy_ref[...] = (r * (dyh - yh * jnp.mean(dyh * yh, axis=-1, keepdims=True))).astype(BF16)

        @pl.when(i == nb - 1)
        def _():
            loss_ref[...] = (0.5 / Dm) * _rowsum(acc[...])

    row, vec = _row_spec(tb, Dm), _vec_spec(Dm)
    vec_shape = jax.ShapeDtypeStruct((1, Dm), F32)
    return pl.pallas_call(
        body, name=name, grid=(nb,), in_specs=[row, row, vec, vec, row],
        out_specs=(row, pl.BlockSpec((1, 1), lambda i: (0, 0)), row, vec, vec),
        out_shape=(jax.ShapeDtypeStruct((S, Dm), F32), jax.ShapeDtypeStruct((1, 1), F32),
                   jax.ShapeDtypeStruct((S, Dm), BF16), vec_shape, vec_shape),
        scratch_shapes=[pltpu.VMEM((1, Dm), F32)], compiler_params=_params("arbitrary"))(x, y, w, g, target)


def _post_bwd(dxo, y, w, g, name):
    S, Dm = y.shape
    tb = _blk(S, 512)

    def body(d_ref, y_ref, w_ref, g_ref, dy_ref, dg_ref, dw_ref):
        i = pl.program_id(0)

        @pl.when(i == 0)
        def _():
            dg_ref[...] = jnp.zeros_like(dg_ref)
            dw_ref[...] = jnp.zeros_like(dw_ref)

        yv, dv = y_ref[...], d_ref[...]
        r = lax.rsqrt(jnp.mean(yv * yv, axis=-1, keepdims=True) + EPS)
        yh = yv * r
        dg_ref[...] += _colsum(dv * (yh * w_ref[...]))
        dn = dv * g_ref[...]
        dw_ref[...] += _colsum(dn * yh)
        dyh = dn * w_ref[...]
        dy_ref[...] = (r * (dyh - yh * jnp.mean(dyh * yh, axis=-1, keepdims=True))).astype(BF16)

    return pl.pallas_call(
        body, name=name, grid=(S // tb,), in_specs=[_row_spec(tb, Dm)] * 2 + [_vec_spec(Dm)] * 2,
        out_specs=(_row_spec(tb, Dm), _vec_spec(Dm), _vec_spec(Dm)),
        out_shape=(jax.ShapeDtypeStruct((S, Dm), BF16), jax.ShapeDtypeStruct((1, Dm), F32),
                   jax.ShapeDtypeStruct((1, Dm), F32)),
        compiler_params=_params("arbitrary"))(dxo, y, w, g)


def _pre_bwd(dh, x, w, sc, dres, name):
    S, Dm = x.shape
    tb = _blk(S, 512)

    def body(dh_ref, x_ref, w_ref, sc_ref, dr_ref, dx_ref, dsh_ref, dsc_ref, dw_ref):
        i = pl.program_id(0)

        @pl.when(i == 0)
        def _():
            dsh_ref[...] = jnp.zeros_like(dsh_ref)
            dsc_ref[...] = jnp.zeros_like(dsc_ref)
            dw_ref[...] = jnp.zeros_like(dw_ref)

        xv, dv = x_ref[...], dh_ref[...]
        r = lax.rsqrt(jnp.mean(xv * xv, axis=-1, keepdims=True) + EPS)
        xh = xv * r
        one_sc = 1.0 + sc_ref[...]
        dsh_ref[...] += _colsum(dv)
        dsc_ref[...] += _colsum(dv * (xh * w_ref[...]))
        dw_ref[...] += _colsum(dv * one_sc * xh)
        dxh = dv * one_sc * w_ref[...]
        dx_ref[...] = dr_ref[...] + r * (dxh - xh * jnp.mean(dxh * xh, axis=-1, keepdims=True))

    vec = jax.ShapeDtypeStruct((1, Dm), F32)
    return pl.pallas_call(
        body, name=name, grid=(S // tb,),
        in_specs=[_row_spec(tb, Dm)] * 2 + [_vec_spec(Dm)] * 2 + [_row_spec(tb, Dm)],
        out_specs=(_row_spec(tb, Dm), _vec_spec(Dm), _vec_spec(Dm), _vec_spec(Dm)),
        out_shape=(jax.ShapeDtypeStruct((S, Dm), F32), vec, vec, vec),
        compiler_params=_params("arbitrary"))(dh, x, w, sc, dres)


D_PROJ_ANY = pl.BlockSpec(memory_space=pl.ANY)


def _gate_specs(tm, Dm):
    return (pl.BlockSpec((tm, Dm), lambda i: (i, C_GS // Dm)), pl.BlockSpec((tm, Dm), lambda i: (i, C_GG // Dm)))


def _gdn_up_merge(ogn, w_gu, ys, proj, name):
    S, K = ogn.shape
    Dm = ys.shape[1]
    tm = _blk(S, 512)
    row = _row_spec(tm, Dm)

    def epilogue(r, ex, out):
        ys_ref, gs_ref, gg_ref = ex
        out[0][...] = r
        out[1][...] = (_sigmoid(gs_ref[...].astype(F32)) * ys_ref[...]
                       + _sigmoid(gg_ref[...].astype(F32)) * r).astype(BF16)

    gs_spec, gg_spec = _gate_specs(tm, Dm)
    return _mm_rows(ogn, w_gu, S, Dm, K, mode="nn", name=name, tm=tm,
                    extras=[(ys, row), (proj, gs_spec), (proj, gg_spec)],
                    out_shapes=[jax.ShapeDtypeStruct((S, Dm), F32), jax.ShapeDtypeStruct((S, Dm), BF16)],
                    out_specs=[row, row], epilogue=epilogue)


def _mix_out_post_pre(merged, w_o, x, w_post, g, w_pre, sc, sh, name):
    S, Dm = x.shape
    tm = _blk(S, 512)
    row, vec = _row_spec(tm, Dm), _vec_spec(Dm)

    def epilogue(r, ex, out):
        x_ref, wpost_ref, g_ref, wpre_ref, sc_ref, sh_ref = ex
        out[0][...] = r
        rr = lax.rsqrt(jnp.mean(r * r, axis=-1, keepdims=True) + EPS)
        x1 = x_ref[...] + g_ref[...] * (r * rr * wpost_ref[...])
        out[1][...] = x1
        r1 = lax.rsqrt(jnp.mean(x1 * x1, axis=-1, keepdims=True) + EPS)
        out[2][...] = ((x1 * r1 * wpre_ref[...]) * (1.0 + sc_ref[...]) + sh_ref[...]).astype(BF16)

    return _mm_rows(merged, w_o, S, Dm, Dm, mode="nn", name=name, tm=tm,
                    extras=[(x, row), (w_post, vec), (g, vec), (w_pre, vec), (sc, vec), (sh, vec)],
                    out_shapes=[jax.ShapeDtypeStruct((S, Dm), F32), jax.ShapeDtypeStruct((S, Dm), F32),
                                jax.ShapeDtypeStruct((S, Dm), BF16)], out_specs=[row, row, row], epilogue=epilogue)


def _mix_out_dx_merge_bwd(dmo, w_o, ys, yg, proj, d_proj, name):
    S, Dm = ys.shape
    tm = _blk(S, 512)
    row = _row_spec(tm, Dm)

    def epilogue(d, ex, out):
        ys_ref, yg_ref, gs_ref, gg_ref, _ = ex
        ss, sg = _sigmoid(gs_ref[...].astype(F32)), _sigmoid(gg_ref[...].astype(F32))
        out[0][...] = (d * ss).astype(BF16)
        out[1][...] = (d * sg).astype(BF16)
        out[2][:, :Dm] = (d * ys_ref[...] * ss * (1.0 - ss)).astype(BF16)
        out[2][:, Dm:] = (d * yg_ref[...] * sg * (1.0 - sg)).astype(BF16)

    gs_spec, gg_spec = _gate_specs(tm, Dm)
    return _mm_rows(dmo, w_o, S, Dm, Dm, mode="nt", name=name, tm=tm,
                    extras=[(ys, row), (yg, row), (proj, gs_spec), (proj, gg_spec), (d_proj, D_PROJ_ANY)],
                    out_shapes=[jax.ShapeDtypeStruct((S, Dm), BF16), jax.ShapeDtypeStruct((S, Dm), BF16),
                                jax.ShapeDtypeStruct(d_proj.shape, BF16)],
                    out_specs=[row, row, pl.BlockSpec((tm, 2 * Dm), lambda i: (i, C_GS // (2 * Dm)))],
                    epilogue=epilogue, aliases={6: 2})


CONV_COLS = 128
CONV_BWD_ROWS = 256


def _taps_down(x):
    rows = _iota(x.shape[0], x.shape[1], 0)
    return [x] + [jnp.where(rows >= k, pltpu.roll(x, k, 0), 0.0) for k in range(1, CONV_K)]


def _conv_pre(taps, w_ref, b_ref):
    pre = taps[0] * w_ref[CONV_K - 1:CONV_K, :] + b_ref[...]
    for k in range(1, CONV_K):
        pre = pre + taps[k] * w_ref[CONV_K - 1 - k:CONV_K - k, :]
    return pre


def _conv_dx(dpre, w_ref):
    n = dpre.shape[0]
    rows = _iota(n, dpre.shape[1], 0)
    dx = dpre * w_ref[CONV_K - 1:CONV_K, :]
    for k in range(1, CONV_K):
        dx = dx + jnp.where(rows < n - k, pltpu.roll(dpre, n - k, 0), 0.0) * w_ref[CONV_K - 1 - k:CONV_K - k, :]
    return dx


def _conv_fwd(proj, w, b, name):
    S = proj.shape[0]
    n = w.shape[1]
    cb = CONV_COLS

    def body(x_ref, w_ref, b_ref, o_ref):
        o_ref[...] = _silu(_conv_pre(_taps_down(x_ref[...].astype(F32)), w_ref, b_ref)).astype(BF16)

    return pl.pallas_call(
        body, name=name, grid=(n // cb,),
        in_specs=[pl.BlockSpec((S, cb), lambda j: (0, j + C_XBC // cb)), pl.BlockSpec((CONV_K, cb), lambda j: (0, j)),
                  pl.BlockSpec((1, cb), lambda j: (0, j))],
        out_specs=pl.BlockSpec((S, cb), lambda j: (0, j)), out_shape=jax.ShapeDtypeStruct((S, n), BF16),
        compiler_params=_params("parallel"))(proj, w, b)


def _conv_bwd(dact, proj, w, b, col0, d_proj, name):
    S, n = dact.shape
    cb = CONV_COLS
    o = col0 // cb

    R, HALO = _blk(S, CONV_BWD_ROWS), 16
    n_chunks = S // R

    def body(d_ref, x_ref, w_ref, b_ref, _, dx_ref, dw_ref, db_ref):
        def chunk(r0, first, last, sums):
            lo, hi = (0 if first else HALO), (0 if last else HALO)
            start = r0 - lo if isinstance(r0, int) else pl.multiple_of(r0 - lo, HALO)
            xe = x_ref[pl.ds(start, lo + R + hi), :].astype(F32)
            rows = _iota(lo + R + hi, cb, 0)
            taps = [xe[lo:, :]]
            for k in range(1, CONV_K):
                t = pltpu.roll(xe, k, 0)
                taps.append((jnp.where(rows >= k, t, 0.0) if first else t)[lo:, :])
            dpre_e = d_ref[pl.ds(r0, R + hi), :].astype(F32) * _dsilu(_conv_pre(taps, w_ref, b_ref))
            dpre = dpre_e[0:R, :]
            db, dw = sums
            db = db + _colsum(dpre)
            dw = [dw[k] + _colsum(dpre * taps[k][0:R, :]) for k in range(CONV_K)]
            rows_e = _iota(R + hi, cb, 0)
            dx = dpre * w_ref[CONV_K - 1:CONV_K, :]
            for k in range(1, CONV_K):
                t = pltpu.roll(dpre_e, R + hi - k, 0)
                t = jnp.where(rows_e < R - k, t, 0.0) if last else t
                dx = dx + t[0:R, :] * w_ref[CONV_K - 1 - k:CONV_K - k, :]
            dx_ref[pl.ds(r0, R), :] = dx.astype(BF16)
            return db, dw

        zero = jnp.zeros((1, cb), F32)
        sums = chunk(0, True, n_chunks == 1, (zero, [zero] * CONV_K))
        if n_chunks > 2:
            def step(i, carry):
                db, dw = chunk(pl.multiple_of(i * R, R), False, False, (carry[0], list(carry[1:])))
                return (db,) + tuple(dw)
            carry = lax.fori_loop(1, n_chunks - 1, step, (sums[0],) + tuple(sums[1]))
            sums = (carry[0], list(carry[1:]))
        if n_chunks > 1:
            sums = chunk((n_chunks - 1) * R, False, True, sums)
        db_ref[...] = sums[0]
        for k in range(CONV_K):
            dw_ref[CONV_K - 1 - k:CONV_K - k, :] = sums[1][k]

    return pl.pallas_call(
        body, name=name, grid=(n // cb,),
        in_specs=[pl.BlockSpec((S, cb), lambda j: (0, j)), pl.BlockSpec((S, cb), lambda j: (0, j + o + C_XBC // cb)),
                  pl.BlockSpec((CONV_K, cb), lambda j: (0, j + o)), pl.BlockSpec((1, cb), lambda j: (0, j + o)),
                  D_PROJ_ANY],
        out_specs=(pl.BlockSpec((S, cb), lambda j: (0, j + o + C_XBC // cb)),
                   pl.BlockSpec((CONV_K, cb), lambda j: (0, j)), pl.BlockSpec((1, cb), lambda j: (0, j))),
        out_shape=(jax.ShapeDtypeStruct(d_proj.shape, BF16), jax.ShapeDtypeStruct((CONV_K, n), F32),
                   jax.ShapeDtypeStruct((1, n), F32)),
        input_output_aliases={4: 0}, compiler_params=_params("parallel"))(dact, proj, w, b, d_proj)


def _ssd_specs(L, order):
    G = SSM_GROUPS_PER_STEP
    W, N = G * SSM_GROUP_WIDTH, G * SSM_D_STATE
    x_spec = pl.BlockSpec((L, W), lambda g, c: (order(c), g))
    b_spec = pl.BlockSpec((L, N), lambda g, c: (order(c), 2048 // N + g))
    c_spec = pl.BlockSpec((L, N), lambda g, c: (order(c), 3072 // N + g))
    z_spec = pl.BlockSpec((L, W), lambda g, c: (order(c), C_ZS // W + g))
    dt_spec = pl.BlockSpec((G, L, SSM_HEADS_PER_GROUP), lambda g, c: (g, order(c), 0))
    p_spec = pl.BlockSpec((G, 3, SSM_HEADS_PER_GROUP), lambda g, c: (g, 0, 0))
    nw_spec = pl.BlockSpec((G, 1, SSM_GROUP_WIDTH), lambda g, c: (g, 0, 0))
    s_spec = pl.BlockSpec((G, 1, SSM_GROUP_WIDTH, SSM_D_STATE), lambda g, c: (g, order(c), 0, 0))
    return x_spec, b_spec, c_spec, z_spec, dt_spec, p_spec, nw_spec, s_spec


class _SsdGroup:
    def __init__(self, L):
        P, H, W = SSM_HEAD_DIM, SSM_HEADS_PER_GROUP, SSM_GROUP_WIDTH
        self.L = L
        self.ii, self.jj = _iota(L, L, 0), _iota(L, L, 1)
        self.lower = jnp.where(self.ii >= self.jj, 1.0, 0.0).astype(BF16)
        self.upper = jnp.where(self.ii <= self.jj, 1.0, 0.0).astype(BF16)
        self.lo = _iota(L, 2 * P, 1) < P
        self.lo_row = _iota(1, 2 * P, 1) < P
        bi, bj = _iota(W, W, 0), _iota(W, W, 1)
        self.block = jnp.where(bi // P == bj // P, 1.0, 0.0).astype(BF16)
        si, sj = _iota(2 * P, W, 0), _iota(2 * P, W, 1)
        self.pick = jnp.where(sj == si * P, 1.0, 0.0).astype(BF16)
        self.ones = jnp.ones((L, 2 * P), BF16)

    def spread(self, v4):
        R = v4.shape[0]
        lo = self.lo if R == self.L else self.lo_row
        b = lambda h: jnp.broadcast_to(v4[:, h:h + 1], (R, 2 * SSM_HEAD_DIM))
        return jnp.concatenate([jnp.where(lo, b(0), b(1)), jnp.where(lo, b(2), b(3))], axis=1)

    def gather4(self, v):
        return jnp.concatenate([v[:, h * SSM_HEAD_DIM:h * SSM_HEAD_DIM + 1] for h in range(SSM_HEADS_PER_GROUP)],
                               axis=1)

    def head_sums(self, z):
        return sum(lax.dot_general(p, self.block, _NN, preferred_element_type=F32) for p in _parts(z, 2))

    def pair_cols(self, full, pair):
        ps = full[:, pair * 128:(pair + 1) * 128]
        sw = pltpu.roll(ps, SSM_HEAD_DIM, 1)
        return jnp.where(self.lo, ps, sw), jnp.where(self.lo, sw, ps)

    def gates(self, dt4_raw, p):
        L = self.L
        dtr = self.spread(dt4_raw + p[0:1, :])
        dt = _softplus(dtr)
        A = self.spread(-jnp.exp(p[1:2, :]))
        acum = _sum_by(self.lower, dt * A)
        yield
        rows = _sum_by(self.pick, acum, _NT)
        yield
        a_last = acum[L - 1:L, :]
        cols = self.pair_cols(acum, 0) + self.pair_cols(acum, 1)
        decay, decay_t = [], []
        for h in range(SSM_HEADS_PER_GROUP):
            seg = cols[h] - rows[h:h + 1, :]
            decay.append(jnp.exp(jnp.where(self.ii >= self.jj, seg, NEG_INF)))
            decay_t.append(jnp.exp(jnp.where(self.jj >= self.ii, -seg, NEG_INF)))
        return dict(dtr=dtr, dt=dt, A=A, D=self.spread(p[2:3, :]), acum=acum, eac=jnp.exp(acum), a_last=a_last,
                    wdec=jnp.exp(a_last - acum), decay=decay, decay_t=decay_t,
                    ea_last=[jnp.exp(rows[h:h + 1, L - 1:L]) for h in range(SSM_HEADS_PER_GROUP)])


def _ssd_fwd(conv, proj, dt_raw, pvec, nw, name):
    S = conv.shape[0]
    L, P, N, H, W, G = SSM_CHUNK, SSM_HEAD_DIM, SSM_D_STATE, SSM_HEADS_PER_GROUP, SSM_GROUP_WIDTH, SSM_GROUPS_PER_STEP
    nc = S // L

    def body(x_ref, b_ref, c_ref, z_ref, dt_ref, p_ref, nw_ref, y_ref, yn_ref, s0_ref, state):
        c = pl.program_id(1)

        @pl.when(c == 0)
        def _():
            state[...] = jnp.zeros_like(state)

        k = _SsdGroup(L)

        def group(gi):
            gsl = slice(gi * W, (gi + 1) * W)
            Bm, Cm = b_ref[:, gi * N:(gi + 1) * N], c_ref[:, gi * N:(gi + 1) * N]
            x = x_ref[:, gsl].astype(F32)
            S0 = state[gsl, :]
            s0_ref[gi, 0] = S0
            CB = _dot(Cm, Bm, _NT)
            y_off = _dot(Cm, S0, _NT)
            t = yield from k.gates(dt_ref[gi], p_ref[gi])
            xdt = x * t["dt"]
            s_new = _dot(xdt * t["wdec"], Bm, _TN)
            y_diag = []
            for pair in range(H // 2):
                xp = xdt[:, pair * 128:(pair + 1) * 128]
                y_diag.append(jnp.where(k.lo, _dot(CB * t["decay"][2 * pair], xp),
                                        _dot(CB * t["decay"][2 * pair + 1], xp)))
            yield
            y = jnp.concatenate(y_diag, axis=1) + y_off * t["eac"]
            for h in range(H):
                hsl = slice(gi * W + h * P, gi * W + (h + 1) * P)
                state[hsl, :] = S0[h * P:(h + 1) * P, :] * t["ea_last"][h] + s_new[h * P:(h + 1) * P, :]
            y_ref[:, gsl] = y
            y2 = (y + t["D"] * x) * _silu(z_ref[:, gsl].astype(F32))
            r = lax.rsqrt(jnp.mean(y2 * y2, axis=-1, keepdims=True) + EPS)
            yn_ref[:, gsl] = (y2 * r * nw_ref[gi]).astype(BF16)

        _lockstep(group(gi) for gi in range(G))

    x_spec, b_spec, c_spec, z_spec, dt_spec, p_spec, nw_spec, s_spec = _ssd_specs(L, lambda c: c)
    y_spec = pl.BlockSpec((L, G * W), lambda g, c: (c, g))
    return pl.pallas_call(
        body, name=name, grid=(SSM_GROUPS // G, nc),
        in_specs=[x_spec, b_spec, c_spec, z_spec, dt_spec, p_spec, nw_spec],
        out_specs=(y_spec, y_spec, s_spec),
        out_shape=(jax.ShapeDtypeStruct((S, SSM_GROUPS * W), F32), jax.ShapeDtypeStruct((S, SSM_GROUPS * W), BF16),
                   jax.ShapeDtypeStruct((SSM_GROUPS, nc, W, N), F32)),
        scratch_shapes=[pltpu.VMEM((G * W, N), F32)],
        compiler_params=_params("parallel", "arbitrary"))(conv, conv, conv, proj, dt_raw, pvec, nw)


def _ssd_bwd(dyn, conv, proj, dt_raw, pvec, nw, y_ssd, states, d_proj, d_conv, name):
    S = conv.shape[0]
    L, P, N, H, W, G = SSM_CHUNK, SSM_HEAD_DIM, SSM_D_STATE, SSM_HEADS_PER_GROUP, SSM_GROUP_WIDTH, SSM_GROUPS_PER_STEP
    nc = S // L
    assert G == SSM_GROUPS
    CB0, CC0 = SSM_GROUPS * W, SSM_GROUPS * (W + N)

    def body(dyn_ref, x_ref, b_ref, c_ref, z_ref, dt_ref, p_ref, nw_ref, y_ref, s0_ref, _, _2,
             dact_ref, dz_ref, ddt_ref, dp_ref, dnw_ref, dstate):
        c = pl.program_id(1)

        @pl.when(c == 0)
        def _():
            dstate[...] = jnp.zeros_like(dstate)
            dp_ref[...] = jnp.zeros_like(dp_ref)
            dnw_ref[...] = jnp.zeros_like(dnw_ref)

        k = _SsdGroup(L)
        last = (_iota(L, 1, 0) == L - 1)

        def group(gi):
            gsl = slice(gi * W, (gi + 1) * W)
            Bm, Cm = b_ref[:, gi * N:(gi + 1) * N], c_ref[:, gi * N:(gi + 1) * N]
            x, z = x_ref[:, gsl].astype(F32), z_ref[:, gsl].astype(F32)
            S0, dS1 = s0_ref[gi, 0], dstate[gsl, :]
            CB = _dot(Cm, Bm, _NT)
            CBt = _dot(Bm, Cm, _NT)
            y_off_raw = _dot(Cm, S0, _NT)
            dXs_raw = _dot(Bm, dS1, _NT)
            t = yield from k.gates(dt_ref[gi], p_ref[gi])
            y1 = y_ref[:, gsl] + t["D"] * x
            sz = _silu(z)
            y2 = y1 * sz
            r = lax.rsqrt(jnp.mean(y2 * y2, axis=-1, keepdims=True) + EPS)
            y2h = y2 * r
            dyn_v = dyn_ref[:, gsl]
            dnw_ref[gi] += _colsum(dyn_v * y2h)
            dy2h = dyn_v * nw_ref[gi]
            dy2 = r * (dy2h - y2h * jnp.mean(dy2h * y2h, axis=-1, keepdims=True))
            dz_ref[:, gsl] = (dy2 * y1 * _dsilu(z)).astype(BF16)
            dY = dy2 * sz
            X = x * t["dt"]
            dYe = dY * t["eac"]
            dC_s = _dot(dYe, S0)
            dB_s = _dot(X * t["wdec"], dS1)
            dS_c = _dot(dYe, Cm, _TN)
            dXm, Gs, Gts = [], [], []
            for pair in range(H // 2):
                dYp, Xp = dY[:, pair * 128:(pair + 1) * 128], X[:, pair * 128:(pair + 1) * 128]
                dXm.append(jnp.where(k.lo, _dot(CBt * t["decay_t"][2 * pair], dYp),
                                     _dot(CBt * t["decay_t"][2 * pair + 1], dYp)))
                for mask in (k.lo, ~k.lo):
                    Gs.append(_dot(jnp.where(mask, dYp, 0.0), Xp, _NT))
                    Gts.append(_dot(jnp.where(mask, Xp, 0.0), dYp, _NT))
            yield
            dXs = dXs_raw * t["wdec"]
            dX = jnp.concatenate(dXm, axis=1) + dXs
            dCB, dCBt, q_sums = 0.0, 0.0, []
            for h in range(H):
                M, Mt = CB * t["decay"][h], CBt * t["decay_t"][h]
                dCB = dCB + Gs[h] * t["decay"][h]
                dCBt = dCBt + Gts[h] * t["decay_t"][h]
                d = Gs[h] * M - Gts[h] * Mt
                q_sums.append(sum(lax.dot_general(pt, k.ones, _NN, preferred_element_type=F32)
                                  for pt in _parts(d, 2)))
            q_f = jnp.concatenate([jnp.where(k.lo, q_sums[0], q_sums[1]), jnp.where(k.lo, q_sums[2], q_sums[3])],
                                  axis=1)
            x_dxs = X * dXs
            tot = [_total(dS1[h * P:(h + 1) * P, :] * S0[h * P:(h + 1) * P, :]) * t["ea_last"][h] for h in range(H)]
            tot_f = k.spread(jnp.concatenate(tot, axis=1))
            d_alast = k.head_sums(jnp.broadcast_to(_colsum(x_dxs), (8, W)))[0:1, :] + tot_f
            dacum = q_f + k.head_sums(dY * (y_off_raw * t["eac"]) - x_dxs) + jnp.where(last, d_alast, 0.0)
            dx_dt = k.head_sums(dX * x)
            d_skip = _colsum(k.head_sums(dY * x))
            for h in range(H):
                hsl = slice(gi * W + h * P, gi * W + (h + 1) * P)
                dstate[hsl, :] = t["ea_last"][h] * dS1[h * P:(h + 1) * P, :] + dS_c[h * P:(h + 1) * P, :]
            dc_s2 = _dot(dCB, Bm)
            db_s2 = _dot(dCBt, Cm)
            yield
            da = _sum_by(k.upper, dacum)
            yield
            ddt_raw = (da * t["A"] + dx_dt) * _sigmoid(t["dtr"])
            dact_ref[:, gsl] = (dX * t["dt"] + t["D"] * dY).astype(BF16)
            dact_ref[:, CC0 + gi * N:CC0 + (gi + 1) * N] = (dC_s + dc_s2).astype(BF16)
            dact_ref[:, CB0 + gi * N:CB0 + (gi + 1) * N] = (dB_s + db_s2).astype(BF16)
            ddt_ref[gi] = k.gather4(ddt_raw)
            dp_ref[gi] += k.gather4(jnp.concatenate([_colsum(ddt_raw), _colsum(da * t["dt"]) * t["A"], d_skip],
                                                    axis=0))

        _lockstep(group(gi) for gi in range(G))

    rev = lambda c: nc - 1 - c
    x_spec, b_spec, c_spec, z_spec, dt_spec, p_spec, nw_spec, s_spec = _ssd_specs(L, rev)
    y_spec = pl.BlockSpec((L, G * W), lambda g, c: (rev(c), g))
    half = d_conv.shape[1] // 2
    return pl.pallas_call(
        body, name=name, grid=(SSM_GROUPS // G, nc),
        in_specs=[y_spec, x_spec, b_spec, c_spec, z_spec, dt_spec, p_spec, nw_spec, y_spec, s_spec, D_PROJ_ANY,
                  D_PROJ_ANY],
        out_specs=(pl.BlockSpec((L, half), lambda g, c: (rev(c), 0)), z_spec, dt_spec, p_spec, nw_spec),
        out_shape=(jax.ShapeDtypeStruct(d_conv.shape, BF16), jax.ShapeDtypeStruct(d_proj.shape, BF16),
                   jax.ShapeDtypeStruct((SSM_GROUPS, S, H), F32), jax.ShapeDtypeStruct((SSM_GROUPS, 3, H), F32),
                   jax.ShapeDtypeStruct((SSM_GROUPS, 1, W), F32)),
        scratch_shapes=[pltpu.VMEM((G * W, N), F32)], input_output_aliases={10: 1, 11: 0},
        compiler_params=_params("parallel", "arbitrary"))(dyn, conv, conv, conv, proj, dt_raw, pvec, nw, y_ssd, states,
                                                          d_proj, d_conv)


def _unit_lower_inverse(A, ii, jj):
    eye = (ii == jj).astype(F32)
    same = (ii // GDN_INV_BLOCK) == (jj // GDN_INV_BLOCK)
    Ad = jnp.where(same, A, 0.0)
    Ao = A - Ad
    P2 = _dot3(Ad, Ad)
    yield
    P4, X = _dot(P2, P2), _dot3(eye - Ad, eye + P2)
    yield
    P8, X = _dot(P4, P4), X + _dot2(X, P4)
    yield
    X = X + _dot2(X, P8)
    yield
    Bm = _dot3(X, Ao)
    yield
    B2 = _dot3(Bm, Bm)
    yield
    Y = (eye - Bm) + B2 - _dot2(Bm, B2)
    yield
    T = _dot3(Y, X)
    yield
    return T


def _gdn_specs(L, order):
    G = GDN_QK_PER_STEP
    Hd, W = G * GDN_HEAD, G * GDN_V_PER_QK * GDN_HEAD
    q_spec = pl.BlockSpec((L, Hd), lambda h, c: (order(c), (C_QKV - C_XBC) // Hd + h))
    k_spec = pl.BlockSpec((L, Hd), lambda h, c: (order(c), (C_QKV - C_XBC + 1024) // Hd + h))
    v_spec = pl.BlockSpec((L, W), lambda h, c: (order(c), (C_QKV - C_XBC + 2048) // W + h))
    z_spec = pl.BlockSpec((L, W), lambda h, c: (order(c), C_ZG // W + h))
    ba_spec = pl.BlockSpec((G, L, GDN_V_PER_QK), lambda h, c: (h, order(c), 0))
    p_spec = pl.BlockSpec((G, 2, GDN_V_PER_QK), lambda h, c: (h, 0, 0))
    nw_spec = pl.BlockSpec((1, GDN_HEAD), lambda h, c: (0, 0))
    s_spec = pl.BlockSpec((G, 1, GDN_V_PER_QK * GDN_HEAD, GDN_HEAD), lambda h, c: (h, order(c), 0, 0))
    t_spec = pl.BlockSpec((G * GDN_V_PER_QK, 1, L, L), lambda h, c: (h, order(c), 0, 0))
    return q_spec, k_spec, v_spec, z_spec, ba_spec, p_spec, nw_spec, s_spec, t_spec


def _gdn_gates(qa, ka, b_col, a_col, p, j, ii, jj):
    L = qa.shape[0]
    sp_in = a_col + p[0:1, j:j + 1]
    neg_ea = -jnp.exp(p[1:2, j:j + 1])
    g = neg_ea * _softplus(sp_in)
    gcum, gcum_row = _cumsum_forms(g, ii, jj)
    rq = lax.rsqrt(_rowsum(qa * qa) + EPS)
    rk = lax.rsqrt(_rowsum(ka * ka) + EPS)
    q = qa * rq * (GDN_HEAD ** -0.5)
    k = ka * rk
    beta = _sigmoid(b_col)
    yield
    Dm = jnp.exp(jnp.where(ii >= jj, gcum - gcum_row, NEG_INF))
    eg = jnp.exp(gcum)
    g_last = gcum[L - 1:L, :]
    wdec = jnp.exp(g_last - gcum)
    return dict(rq=rq, rk=rk, q=q, k=k, beta=beta, sp_in=sp_in, neg_ea=neg_ea, g=g, Dm=Dm, kbeta=k * beta, eg=eg,
                g_last=g_last, wdec=wdec, kdec=k * wdec)


def _gdn_fwd(conv, proj, b_raw, a_raw, pvec, nw, name):
    S = conv.shape[0]
    L, Hd, J, G = GDN_CHUNK, GDN_HEAD, GDN_V_PER_QK, GDN_QK_PER_STEP
    W = J * Hd
    nc = S // L

    def body(q_ref, k_ref, v_ref, z_ref, b_ref, a_ref, p_ref, nw_ref, o_ref, on_ref, s0_ref, t_ref, state):
        c = pl.program_id(1)

        @pl.when(c == 0)
        def _():
            state[...] = jnp.zeros_like(state)

        ii, jj = _iota(L, L, 0), _iota(L, L, 1)
        for hq in range(G):
            s0_ref[hq, 0] = state[hq * W:(hq + 1) * W, :]

        def head(hq, j):
            hd = hq * J + j
            hsl, sl = slice(hq * Hd, (hq + 1) * Hd), slice(hd * Hd, (hd + 1) * Hd)
            t = yield from _gdn_gates(q_ref[:, hsl].astype(F32), k_ref[:, hsl].astype(F32), b_ref[hq][:, j:j + 1],
                                      a_ref[hq][:, j:j + 1],
                                      p_ref[hq], j, ii, jj)
            KK = _dot(t["kbeta"], t["k"], _NT)
            QK = _dot(t["q"], t["k"], _NT)
            yield
            T = yield from _unit_lower_inverse(jnp.where(ii > jj, KK * t["Dm"], 0.0), ii, jj)
            t_ref[hd, 0] = T
            S0 = state[sl, :]
            U = _dot2(T, v_ref[:, sl].astype(F32) * t["beta"])
            Wm = _dot2(T, t["kbeta"] * t["eg"])
            o_inter = _dot(t["q"] * t["eg"], S0)
            yield
            Vn = U - _dot(Wm, S0)
            yield
            o = o_inter + _dot(QK * t["Dm"], Vn)
            s_new = _dot(t["kdec"], Vn, _TN)
            yield
            state[sl, :] = S0 * jnp.exp(t["g_last"]) + s_new
            o_ref[:, sl] = o
            r = lax.rsqrt(jnp.mean(o * o, axis=-1, keepdims=True) + EPS)
            on_ref[:, sl] = ((o * r * nw_ref[...]) * _silu(z_ref[:, sl].astype(F32))).astype(BF16)

        _lockstep(head(hq, j) for hq in range(G) for j in range(J))

    q_spec, k_spec, v_spec, z_spec, ba_spec, p_spec, nw_spec, s_spec, t_spec = _gdn_specs(L, lambda c: c)
    o_spec = pl.BlockSpec((L, G * W), lambda h, c: (c, h))
    return pl.pallas_call(
        body, name=name, grid=(GDN_QK_HEADS // G, nc),
        in_specs=[q_spec, k_spec, v_spec, z_spec, ba_spec, ba_spec, p_spec, nw_spec],
        out_specs=(o_spec, o_spec, s_spec, t_spec),
        out_shape=(jax.ShapeDtypeStruct((S, GDN_QK_HEADS * W), F32), jax.ShapeDtypeStruct((S, GDN_QK_HEADS * W), BF16),
                   jax.ShapeDtypeStruct((GDN_QK_HEADS, nc, W, Hd), F32),
                   jax.ShapeDtypeStruct((GDN_QK_HEADS * J, nc, L, L), F32)),
        scratch_shapes=[pltpu.VMEM((G * W, Hd), F32)],
        compiler_params=_params("parallel", "arbitrary"))(conv, conv, conv, proj, b_raw, a_raw, pvec, nw)


def _gdn_bwd(don, conv, proj, b_raw, a_raw, pvec, nw, o_pre, states, t_inv, d_proj, d_conv, name):
    S = conv.shape[0]
    L, Hd, J, G = GDN_CHUNK, GDN_HEAD, GDN_V_PER_QK, GDN_QK_PER_STEP
    W = J * Hd
    nc = S // L
    assert G == GDN_QK_HEADS
    CK0, CV0 = GDN_QK_HEADS * Hd, 2 * GDN_QK_HEADS * Hd

    def body(don_ref, q_ref, k_ref, v_ref, z_ref, b_ref, a_ref, p_ref, nw_ref, o_ref, s0_ref, t_ref, _, _2,
             dact_ref, dz_ref, db_ref, da_ref, dp_ref, dnw_ref, dstate):
        c = pl.program_id(1)

        @pl.when(c == 0)
        def _():
            dstate[...] = jnp.zeros_like(dstate)
            dp_ref[...] = jnp.zeros_like(dp_ref)
            dnw_ref[...] = jnp.zeros_like(dnw_ref)

        ii, jj = _iota(L, L, 0), _iota(L, L, 1)
        last = (_iota(L, 1, 0) == L - 1)
        res = {}

        def head(hq, j):
            hd = hq * J + j
            hsl, sl = slice(hq * Hd, (hq + 1) * Hd), slice(hd * Hd, (hd + 1) * Hd)
            qa, ka = q_ref[:, hsl].astype(F32), k_ref[:, hsl].astype(F32)
            t = yield from _gdn_gates(qa, ka, b_ref[hq][:, j:j + 1], a_ref[hq][:, j:j + 1], p_ref[hq], j, ii, jj)
            q, k, beta, eg, Dm, kbeta, kdec = (t[nm] for nm in ("q", "k", "beta", "eg", "Dm", "kbeta", "kdec"))
            T = t_ref[hd, 0]
            v, z, o = v_ref[:, sl].astype(F32), z_ref[:, sl].astype(F32), o_ref[:, sl]
            S0, dS1 = s0_ref[hq, 0, j * Hd:(j + 1) * Hd, :], dstate[sl, :]
            sz = _silu(z)
            r = lax.rsqrt(jnp.mean(o * o, axis=-1, keepdims=True) + EPS)
            oh = o * r
            d_on = don_ref[:, sl]
            dz_ref[:, sl] = (d_on * (oh * nw_ref[...]) * _dsilu(z)).astype(BF16)
            dn = d_on * sz
            dnw_part = _colsum(dn * oh)
            doh = dn * nw_ref[...]
            dO = r * (doh - oh * jnp.mean(doh * oh, axis=-1, keepdims=True))
            Rw = kbeta * eg
            qe = q * eg
            U = _dot2(T, v * beta)
            Wm = _dot2(T, Rw)
            KK = _dot(kbeta, k, _NT)
            QK = _dot(q, k, _NT)
            o_inter = _dot(qe, S0)
            dq_s = _dot(dO, S0, _NT)
            dS_q = _dot(qe, dO, _TN)
            yield
            Am = jnp.where(ii > jj, KK * Dm, 0.0)
            Pm = QK * Dm
            Vn = U - _dot(Wm, S0)
            dVn_s = _dot(kdec, dS1)
            yield
            dVn = _dot(Pm, dO, _TN) + dVn_s
            dP = _dot(dO, Vn, _NT)
            dKd = _dot(Vn, dS1, _NT)
            yield
            dQK = dP * Dm
            dq = _dot(dQK, k) + dq_s * eg
            dk = _dot(dQK, q, _TN) + dKd * t["wdec"]
            dstate[sl, :] = jnp.exp(t["g_last"]) * dS1 + dS_q - _dot(Wm, dVn, _TN)
            dW = -_dot(dVn, S0, _NT)
            dRu = _dot2(T, dVn, _TN)
            yield
            dRw = _dot2(T, dW, _TN)
            dA_u = _dot(dRu, U, _NT)
            yield
            dA = jnp.where(ii > jj, -(dA_u + _dot(dRw, Wm, _NT)), 0.0)
            yield
            dKK = dA * Dm
            dkbeta = _dot(dKK, k) + dRw * eg
            dk = dk + _dot(dKK, kbeta, _TN)
            yield
            dk = dk + dkbeta * beta
            dbeta = _rowsum(dkbeta * k) + _rowsum(dRu * v)
            dact_ref[:, CV0 + hd * Hd:CV0 + (hd + 1) * Hd] = (dRu * beta).astype(BF16)
            Q = dA * Am + dP * Pm
            rho = _rowsum(dKd * kdec)
            d_glast = _colsum(rho) + jnp.exp(t["g_last"]) * _total(dS1 * S0)
            q_sums = _row_col_sums(Q)
            rest = _rowsum(dRw * Rw) + _rowsum(dO * o_inter) - rho + jnp.where(last, d_glast, 0.0)
            yield
            dg = _rev_cumsum_col(q_sums + rest, ii, jj)
            yield
            da_raw = dg * t["neg_ea"] * _sigmoid(t["sp_in"])
            res[hq, j] = dict(dq=dq, dk=dk, db=dbeta * beta * (1.0 - beta), da=da_raw, d_bias=_colsum(da_raw),
                              d_alog=_colsum(dg * t["g"]), dnw=dnw_part, rq=t["rq"], rk=t["rk"], k=k, qh=qa * t["rq"])

        _lockstep(head(hq, j) for hq in range(G) for j in range(J))
        for hq in range(G):
            parts = [res[hq, j] for j in range(J)]
            hsl = slice(hq * Hd, (hq + 1) * Hd)
            p0 = parts[0]
            dqh = sum(pt["dq"] for pt in parts) * (GDN_HEAD ** -0.5)
            dkn = sum(pt["dk"] for pt in parts)
            dact_ref[:, hsl] = (p0["rq"] * (dqh - p0["qh"] * _rowsum(dqh * p0["qh"]))).astype(BF16)
            dact_ref[:, CK0 + hq * Hd:CK0 + (hq + 1) * Hd] = (
                p0["rk"] * (dkn - p0["k"] * _rowsum(dkn * p0["k"]))).astype(BF16)
            db_ref[hq] = jnp.concatenate([pt["db"] for pt in parts], axis=1)
            da_ref[hq] = jnp.concatenate([pt["da"] for pt in parts], axis=1)
            dp_ref[hq] += jnp.concatenate([jnp.concatenate([pt["d_bias"] for pt in parts], axis=1),
                                           jnp.concatenate([pt["d_alog"] for pt in parts], axis=1)], axis=0)
            dnw_ref[hq] += sum(pt["dnw"] for pt in parts)

    rev = lambda c: nc - 1 - c
    q_spec, k_spec, v_spec, z_spec, ba_spec, p_spec, nw_spec, s_spec, t_spec = _gdn_specs(L, rev)
    o_spec = pl.BlockSpec((L, G * W), lambda h, c: (rev(c), h))
    dnw_spec = pl.BlockSpec((G, 1, Hd), lambda h, c: (h, 0, 0))
    half = d_conv.shape[1] // 2
    return pl.pallas_call(
        body, name=name, grid=(GDN_QK_HEADS // G, nc),
        in_specs=[o_spec, q_spec, k_spec, v_spec, z_spec, ba_spec, ba_spec, p_spec, nw_spec, o_spec, s_spec, t_spec,
                  D_PROJ_ANY, D_PROJ_ANY],
        out_specs=(pl.BlockSpec((L, half), lambda h, c: (rev(c), 1)), z_spec, ba_spec, ba_spec, p_spec, dnw_spec),
        out_shape=(jax.ShapeDtypeStruct(d_conv.shape, BF16), jax.ShapeDtypeStruct(d_proj.shape, BF16),
                   jax.ShapeDtypeStruct((GDN_QK_HEADS, S, J), F32), jax.ShapeDtypeStruct((GDN_QK_HEADS, S, J), F32),
                   jax.ShapeDtypeStruct((GDN_QK_HEADS, 2, J), F32), jax.ShapeDtypeStruct((GDN_QK_HEADS, 1, Hd), F32)),
        scratch_shapes=[pltpu.VMEM((G * W, Hd), F32)], input_output_aliases={12: 1, 13: 0},
        compiler_params=_params("parallel", "arbitrary"))(don, conv, conv, conv, proj, b_raw, a_raw, pvec, nw, o_pre,
                                                          states, t_inv, d_proj, d_conv)


def _ada_fwd(c_all, w_loc, b_loc, name):
    n = w_loc.shape[1]

    def body(c_ref, w_ref, b_ref, o_ref):
        o_ref[...] = _dot3(_silu(c_ref[...]), w_ref[...]) + b_ref[...]

    return pl.pallas_call(body, name=name, out_shape=jax.ShapeDtypeStruct((N_DEV, n), F32),
                          compiler_params=pltpu.CompilerParams(vmem_limit_bytes=VMEM_LIMIT))(c_all, w_loc, b_loc)


def _ada_bwd(c_all_t, dmod_cols, name):
    Dm, n = c_all_t.shape[0], dmod_cols.shape[1]

    def body(c_ref, d_ref, o_ref):
        ca = _silu(c_ref[...])
        acc = ca[:, 0:1] * d_ref[0:1, :]
        for i in range(1, N_DEV):
            acc = acc + ca[:, i:i + 1] * d_ref[i:i + 1, :]
        o_ref[...] = acc

    return pl.pallas_call(body, name=name, out_shape=jax.ShapeDtypeStruct((Dm, n), F32),
                          compiler_params=pltpu.CompilerParams(vmem_limit_bytes=VMEM_LIMIT))(c_all_t, dmod_cols)


ADAM_BLOCK_BYTES = 12 * 1024 * 1024


def _adam(contrib, w, m, v, name):
    n, R, C = contrib.shape
    tr = R
    while tr % 16 == 0 and (n + 7) * tr * C * 4 > ADAM_BLOCK_BYTES:
        tr //= 2

    def body(c_ref, w_ref, m_ref, v_ref, g_ref, d_ref, nm_ref, nv_ref):
        g = c_ref[0].astype(F32)
        for i in range(1, n):
            g = g + c_ref[i].astype(F32)
        nm = ADAM_B1 * m_ref[...] + (1.0 - ADAM_B1) * g
        nv = ADAM_B2 * v_ref[...] + (1.0 - ADAM_B2) * (g * g)
        m_hat = nm / (1.0 - ADAM_B1 ** ADAM_STEP)
        v_hat = nv / (1.0 - ADAM_B2 ** ADAM_STEP)
        g_ref[...] = g
        d_ref[...] = -ADAM_LR * (m_hat / (jnp.sqrt(v_hat) + ADAM_EPS) + ADAM_WD * w_ref[...])
        nm_ref[...] = nm
        nv_ref[...] = nv

    spec = pl.BlockSpec((tr, C), lambda i: (i, 0))
    shp = jax.ShapeDtypeStruct((R, C), F32)
    return pl.pallas_call(
        body, name=name, grid=(R // tr,), in_specs=[pl.BlockSpec((n, tr, C), lambda i: (0, i, 0)), spec, spec, spec],
        out_specs=(spec,) * 4, out_shape=(shp,) * 4, compiler_params=_params("parallel"))(contrib, w, m, v)


def _exchange(arrays, modes, name, chips=False):
    n = len(arrays)
    out_shape = tuple(jax.ShapeDtypeStruct((N_DEV,) + a.shape if md == "gather" else a.shape, a.dtype)
                      for a, md in zip(arrays, modes))

    def body(*refs):
        ins, outs = refs[:n], refs[n:2 * n]
        send_sems, recv_sems, loc_sems = refs[2 * n:]
        me, peers = _peer_table(chips)

        def src(k, slot):
            return ins[k] if modes[k] == "gather" else ins[k].at[slot]

        def remote(k, m, to_slot, land_slot):
            return pltpu.make_async_remote_copy(
                src_ref=src(k, to_slot), dst_ref=outs[k].at[land_slot], send_sem=send_sems.at[k, m],
                recv_sem=recv_sems.at[k, m], device_id=peers[m][0], device_id_type=pl.DeviceIdType.MESH)

        local = [pltpu.make_async_copy(src(k, me), outs[k].at[me], loc_sems.at[k]) for k in range(n)]
        for cp in local:
            cp.start()
        sends = [remote(k, m, peers[m][1], me) for m in range(len(peers)) for k in range(n)]
        for cp in sends:
            cp.start()
        for m in range(len(peers)):
            for k in range(n):
                remote(k, m, peers[m][1], peers[m][1]).wait_recv()
        for cp in sends:
            cp.wait_send()
        for cp in local:
            cp.wait()

    any_spec = pl.BlockSpec(memory_space=pl.ANY)
    return pl.pallas_call(
        body, name=name, in_specs=[any_spec] * n, out_specs=(any_spec,) * n, out_shape=out_shape,
        scratch_shapes=[pltpu.SemaphoreType.DMA((n, N_DEV - 1)), pltpu.SemaphoreType.DMA((n, N_DEV - 1)),
                        pltpu.SemaphoreType.DMA((n,))])(*arrays)


def _gather_two_level(arrays, name):
    n = len(arrays)
    out_shape = tuple(jax.ShapeDtypeStruct((N_DEV,) + a.shape, a.dtype) for a in arrays)

    def body(*refs):
        ins, outs = refs[:n], refs[n:2 * n]
        send_sems, recv_sems, loc_sems = refs[2 * n:]
        ix, iy, ic = lax.axis_index("x"), lax.axis_index("y"), lax.axis_index("c")
        lin = lambda px, py, pc: 4 * px + 2 * py + pc
        me, sib = lin(ix, iy, ic), (ix, iy, 1 - ic)
        chips = [(1 - ix, iy), (ix, 1 - iy), (1 - ix, 1 - iy)]

        def copy(k, s, block, to, src=None):
            return pltpu.make_async_remote_copy(
                src_ref=outs[k].at[block] if src is None else src, dst_ref=outs[k].at[block],
                send_sem=send_sems.at[k, s], recv_sem=recv_sems.at[k, s], device_id=to,
                device_id_type=pl.DeviceIdType.MESH)

        local = [pltpu.make_async_copy(ins[k], outs[k].at[me], loc_sems.at[k]) for k in range(n)]
        for cp in local:
            cp.start()
        first = [copy(k, 1 + j, me, (cx, cy, ic), src=ins[k]) for j, (cx, cy) in enumerate(chips) for k in range(n)]
        first += [copy(k, 0, me, sib, src=ins[k]) for k in range(n)]
        for cp in first:
            cp.start()
        passed = []
        for j, (cx, cy) in enumerate(chips):
            for k in range(n):
                copy(k, 1 + j, lin(cx, cy, ic), sib).wait_recv()
                passed.append(copy(k, 4 + j, lin(cx, cy, ic), sib))
                passed[-1].start()
        for k in range(n):
            copy(k, 0, lin(*sib), sib).wait_recv()
            for j, (cx, cy) in enumerate(chips):
                copy(k, 4 + j, lin(cx, cy, 1 - ic), sib).wait_recv()
        for cp in first + passed:
            cp.wait_send()
        for cp in local:
            cp.wait()

    any_spec = pl.BlockSpec(memory_space=pl.ANY)
    return pl.pallas_call(
        body, name=name, in_specs=[any_spec] * n, out_specs=(any_spec,) * n, out_shape=out_shape,
        scratch_shapes=[pltpu.SemaphoreType.DMA((n, N_DEV - 1)), pltpu.SemaphoreType.DMA((n, N_DEV - 1)),
                        pltpu.SemaphoreType.DMA((n,))])(*arrays)


def _peer_table(chips=False):
    ix, iy, ic = lax.axis_index("x"), lax.axis_index("y"), lax.axis_index("c")
    peers = []
    for m in ((2, 4, 6) if chips else range(1, N_DEV)):
        px = 1 - ix if m & 4 else ix
        py = 1 - iy if m & 2 else iy
        pc = 1 - ic if m & 1 else ic
        peers.append(((px, py, pc), 2 * px + py if chips else 4 * px + 2 * py + pc))
    return (2 * ix + iy if chips else 4 * ix + 2 * iy + ic), peers


def _exchange_start(arrays, modes, after, name, chips=False):
    n = len(arrays)
    land_shapes = [(N_DEV,) + a.shape if md == "gather" else a.shape for a, md in zip(arrays, modes)]

    def body(*refs):
        ins, lands = refs[:n], refs[n:2 * n]
        send_sems, recv_sems = refs[2 * n + 1], refs[2 * n + 2]
        token = refs[-1]
        me, peers = _peer_table(chips)

        def src(k, slot):
            return ins[k] if modes[k] == "gather" else ins[k].at[slot]

        for peer, slot in peers:
            for k in range(n):
                pltpu.make_async_remote_copy(
                    src_ref=src(k, slot), dst_ref=lands[k].at[me], send_sem=send_sems, recv_sem=recv_sems,
                    device_id=peer, device_id_type=pl.DeviceIdType.MESH).start()
        token[...] = jnp.zeros_like(token)

    hbm = pl.BlockSpec(memory_space=pltpu.HBM)
    sem = pl.BlockSpec(memory_space=pltpu.SEMAPHORE)
    sem_shape = pltpu.SemaphoreType.DMA(())
    operands = [pltpu.with_memory_space_constraint(a, pltpu.HBM) for a in arrays]
    operands += [pltpu.with_memory_space_constraint(lax.empty(s, a.dtype), pltpu.HBM)
                 for s, a in zip(land_shapes, arrays)]
    out = pl.pallas_call(
        body, name=name,
        out_shape=(sem_shape, sem_shape) + tuple(pltpu.HBM(a.shape, a.dtype) for a in arrays)
        + tuple(pltpu.HBM(s, a.dtype) for s, a in zip(land_shapes, arrays)) + (jax.ShapeDtypeStruct((8, 128), F32),),
        in_specs=[hbm] * (2 * n) + [pl.BlockSpec(memory_space=pl.ANY)],
        out_specs=(sem, sem) + (hbm,) * (2 * n) + (pl.BlockSpec(memory_space=pltpu.VMEM),),
        input_output_aliases={i: 2 + i for i in range(2 * n)},
        compiler_params=pltpu.CompilerParams(has_side_effects=pltpu.SideEffectType.DATAFLOW_SIDE_EFFECTING))(
            *operands, after)
    return out[0], out[1], out[2:2 + n], out[2 + n:2 + 2 * n], out[-1]


def _exchange_wait(started, modes, after, name, chips=False):
    send_sems, recv_sems, sent, lands, _ = started
    n = len(sent)

    def body(*refs):
        ins, zones = refs[:n], refs[n:2 * n]
        send_ref, recv_ref = refs[2 * n], refs[2 * n + 1]
        _, peers = _peer_table(chips)

        def src(k, slot):
            return ins[k] if modes[k] == "gather" else ins[k].at[slot]

        for peer, slot in peers:
            for k in range(n):
                cp = pltpu.make_async_remote_copy(
                    src_ref=src(k, slot), dst_ref=zones[k].at[slot], send_sem=send_ref, recv_sem=recv_ref,
                    device_id=peer, device_id_type=pl.DeviceIdType.MESH)
                cp.wait_send()
                cp.wait_recv()

    hbm = pl.BlockSpec(memory_space=pltpu.HBM)
    sem = pl.BlockSpec(memory_space=pltpu.SEMAPHORE)
    out = pl.pallas_call(
        body, name=name,
        out_shape=tuple(pltpu.HBM(a.shape, a.dtype) for a in sent) + tuple(pltpu.HBM(a.shape, a.dtype) for a in lands),
        in_specs=[hbm] * (2 * n) + [sem, sem, pl.BlockSpec(memory_space=pl.ANY)], out_specs=(hbm,) * (2 * n),
        input_output_aliases={i: i for i in range(2 * n)},
        compiler_params=pltpu.CompilerParams(has_side_effects=pltpu.SideEffectType.DATAFLOW_SIDE_EFFECTING))(
            *sent, *lands, send_sems, recv_sems, after)
    ix, iy, ic = lax.axis_index("x"), lax.axis_index("y"), lax.axis_index("c")
    me = 2 * ix + iy if chips else 4 * ix + 2 * iy + ic
    filled = []
    for k in range(n):
        own = sent[k] if modes[k] == "gather" else lax.dynamic_index_in_dim(sent[k], me, axis=0, keepdims=False)
        filled.append(lax.dynamic_update_index_in_dim(out[n + k], own, me, axis=0))
    return filled


def _swap_sibling(to_c0, to_c1, name):
    def body(c0_ref, c1_ref, out_ref, send_sem, recv_sem):
        ix, iy, ic = lax.axis_index("x"), lax.axis_index("y"), lax.axis_index("c")

        def copy(src):
            return pltpu.make_async_remote_copy(src_ref=src, dst_ref=out_ref, send_sem=send_sem, recv_sem=recv_sem,
                                                device_id=(ix, iy, 1 - ic), device_id_type=pl.DeviceIdType.MESH)

        @pl.when(ic == 0)
        def _():
            copy(c1_ref).start()

        @pl.when(ic == 1)
        def _():
            copy(c0_ref).start()

        copy(c0_ref).wait()

    any_spec = pl.BlockSpec(memory_space=pl.ANY)
    return pl.pallas_call(body, name=name, in_specs=[any_spec, any_spec], out_specs=any_spec,
                          out_shape=jax.ShapeDtypeStruct(to_c0.shape, to_c0.dtype),
                          scratch_shapes=[pltpu.SemaphoreType.DMA, pltpu.SemaphoreType.DMA])(to_c0, to_c1)


def _add_pair(to_c0, to_c1, got, name):
    n, R, C = got.shape
    tr = _blk(R, 256)

    def body(c0_ref, c1_ref, got_ref, o_ref):
        ic = lax.axis_index("c")

        @pl.when(ic == 0)
        def _():
            o_ref[...] = (c0_ref[...].astype(F32) + got_ref[...].astype(F32)).astype(o_ref.dtype)

        @pl.when(ic == 1)
        def _():
            o_ref[...] = (c1_ref[...].astype(F32) + got_ref[...].astype(F32)).astype(o_ref.dtype)

    spec = pl.BlockSpec((1, tr, C), lambda i, j: (i, j, 0))
    return pl.pallas_call(body, name=name, grid=(n, R // tr), in_specs=[spec, spec, spec], out_specs=spec,
                          out_shape=jax.ShapeDtypeStruct(got.shape, got.dtype),
                          compiler_params=_params("parallel", "parallel"))(to_c0, to_c1, got)


W_IN_SPLITS = (0, 2048, 6144, 6176, 10272, 12320, 12336, 12352, 13376, 14400)
N_REPLICATED = 16640
REPLICATED = ("b_ada", "norm_mix_pre", "norm_mix_post", "ssm_conv_b", "ssm_dt_bias", "ssm_A_log", "ssm_D",
              "ssm_norm_w", "gdn_dt_bias", "gdn_A_log", "gdn_norm_w", "norm_mlp_pre", "norm_mlp_post")
WEIGHTS = ("w_ada", "b_ada", "norm_mix_pre", "norm_mix_post", "w_in", "ssm_conv_w", "ssm_conv_b", "ssm_dt_bias",
           "ssm_A_log", "ssm_D", "ssm_norm_w", "gdn_conv_w", "gdn_dt_bias", "gdn_A_log", "gdn_norm_w", "w_ssm_up",
           "w_gdn_up", "w_out", "norm_mlp_pre", "norm_mlp_post", "w_mlp_up", "w_mlp_down")


def _cols_of_shards(g, a, b):
    width, pieces = g.shape[2], []
    while a < b:
        i = a // width
        hi = min(b, (i + 1) * width)
        pieces.append(g[i][:, a - i * width:hi - i * width])
        a = hi
    return pieces


ORIG_SEGMENTS = ((0, 6144, "main", 0), (6144, 6176, "small", 0), (6176, 12320, "main", 6144),
                 (12320, 12352, "small", 32), (12352, 14400, "main", 12288))


def _orig_cols(main_cols, small_cols, a, b):
    pieces = []
    for s0, s1, which, off in ORIG_SEGMENTS:
        lo, hi = max(a, s0), min(b, s1)
        if lo < hi:
            pieces.append((main_cols if which == "main" else small_cols)[:, off + lo - s0:off + hi - s0])
    return jnp.concatenate(pieces, axis=1)


def _by_cols(t):
    return t.transpose(1, 0, 2).reshape(t.shape[1], N_DEV * t.shape[2])


def _to_col_shards(t):
    R, C8 = t.shape
    return t.reshape(R, N_DEV, C8 // N_DEV).transpose(1, 0, 2)


def _heads_first(t, groups):
    S = t.shape[0]
    return t.reshape(S, groups, t.shape[1] // groups).transpose(1, 0, 2)


def _heads_last(t):
    return t.transpose(1, 0, 2).reshape(t.shape[1], t.shape[0] * t.shape[2])


def kernel(x, c, w_ada, b_ada, norm_mix_pre, norm_mix_post, w_in, ssm_conv_w, ssm_conv_b, ssm_dt_bias, ssm_A_log, ssm_D, ssm_norm_w, gdn_conv_w, gdn_dt_bias, gdn_A_log, gdn_norm_w, w_ssm_up, w_gdn_up, w_out, norm_mlp_pre, norm_mlp_post, w_mlp_up, w_mlp_down, loss_target, m_w_ada, m_b_ada, m_norm_mix_pre, m_norm_mix_post, m_w_in, m_ssm_conv_w, m_ssm_conv_b, m_ssm_dt_bias, m_ssm_A_log, m_ssm_D, m_ssm_norm_w, m_gdn_conv_w, m_gdn_dt_bias, m_gdn_A_log, m_gdn_norm_w, m_w_ssm_up, m_w_gdn_up, m_w_out, m_norm_mlp_pre, m_norm_mlp_post, m_w_mlp_up, m_w_mlp_down, v_w_ada, v_b_ada, v_norm_mix_pre, v_norm_mix_post, v_w_in, v_ssm_conv_w, v_ssm_conv_b, v_ssm_dt_bias, v_ssm_A_log, v_ssm_D, v_ssm_norm_w, v_gdn_conv_w, v_gdn_dt_bias, v_gdn_A_log, v_gdn_norm_w, v_w_ssm_up, v_w_gdn_up, v_w_out, v_norm_mlp_pre, v_norm_mlp_post, v_w_mlp_up, v_w_mlp_down):
    S, Dm = x.shape[1], D_MODEL
    me = 4 * lax.axis_index("x") + 2 * lax.axis_index("y") + lax.axis_index("c")
    x2, tgt = x[0], loss_target[0]
    n_ada = w_ada.shape[2]
    given = dict(
        w_ada=(w_ada, m_w_ada, v_w_ada), b_ada=(b_ada, m_b_ada, v_b_ada),
        norm_mix_pre=(norm_mix_pre, m_norm_mix_pre, v_norm_mix_pre),
        norm_mix_post=(norm_mix_post, m_norm_mix_post, v_norm_mix_post), w_in=(w_in, m_w_in, v_w_in),
        ssm_conv_w=(ssm_conv_w, m_ssm_conv_w, v_ssm_conv_w), ssm_conv_b=(ssm_conv_b, m_ssm_conv_b, v_ssm_conv_b),
        ssm_dt_bias=(ssm_dt_bias, m_ssm_dt_bias, v_ssm_dt_bias), ssm_A_log=(ssm_A_log, m_ssm_A_log, v_ssm_A_log),
        ssm_D=(ssm_D, m_ssm_D, v_ssm_D), ssm_norm_w=(ssm_norm_w, m_ssm_norm_w, v_ssm_norm_w),
        gdn_conv_w=(gdn_conv_w, m_gdn_conv_w, v_gdn_conv_w), gdn_dt_bias=(gdn_dt_bias, m_gdn_dt_bias, v_gdn_dt_bias),
        gdn_A_log=(gdn_A_log, m_gdn_A_log, v_gdn_A_log), gdn_norm_w=(gdn_norm_w, m_gdn_norm_w, v_gdn_norm_w),
        w_ssm_up=(w_ssm_up, m_w_ssm_up, v_w_ssm_up), w_gdn_up=(w_gdn_up, m_w_gdn_up, v_w_gdn_up),
        w_out=(w_out, m_w_out, v_w_out), norm_mlp_pre=(norm_mlp_pre, m_norm_mlp_pre, v_norm_mlp_pre),
        norm_mlp_post=(norm_mlp_post, m_norm_mlp_post, v_norm_mlp_post), w_mlp_up=(w_mlp_up, m_w_mlp_up, v_w_mlp_up),
        w_mlp_down=(w_mlp_down, m_w_mlp_down, v_w_mlp_down))

    (c_all, scw, gcw, g_in) = _gather_two_level([c, ssm_conv_w[0], gdn_conv_w[0], w_in[0].astype(BF16)], "gather_w_in")
    c_all = c_all.reshape(N_DEV, Dm)
    sp = W_IN_SPLITS
    w_main = jnp.concatenate(_cols_of_shards(g_in, sp[0], sp[2]) + _cols_of_shards(g_in, sp[3], sp[5])
                             + _cols_of_shards(g_in, sp[7], sp[9]), axis=1)
    w_small = jnp.concatenate(_cols_of_shards(g_in, sp[2], sp[3]) + _cols_of_shards(g_in, sp[5], sp[7])
                              + [jnp.zeros((Dm, N_SMALL - 64), BF16)], axis=1)
    conv_w = jnp.concatenate([_by_cols(scw), _by_cols(gcw)], axis=1)
    conv_b = jnp.concatenate([ssm_conv_b, jnp.zeros_like(ssm_conv_b)], axis=1)

    b_loc = lax.dynamic_slice(b_ada, (0, me * n_ada), (1, n_ada))
    mod_part = _ada_fwd(c_all, w_ada[0], b_loc, "ada_fwd")
    (mod_rows,) = _exchange([mod_part.reshape(N_DEV, 1, n_ada)], ["a2a"], "exchange_mod")
    rest = _exchange_start([w_ssm_up[0].astype(BF16), w_gdn_up[0].astype(BF16), w_out[0].astype(BF16),
                            w_mlp_up[0].astype(BF16), w_mlp_down[0].astype(BF16)], ["gather"] * 5, mod_rows,
                           "gather_rest_start")
    mod = mod_rows.reshape(1, 6 * Dm) + rest[4][0:1, 0:1]
    sh1, sc1, g1, sh2, sc2, g2 = [mod[:, i * Dm:(i + 1) * Dm] for i in range(6)]

    h = _pre_fwd(x2, norm_mix_pre, sc1, sh1, "pre_mix")
    proj = _mm(h, w_main, S, N_MAIN, Dm, mode="nn", out_dtype=BF16, tn=MM_WIDE_N, name="proj_main")
    small = _mm(h, w_small, S, N_SMALL, Dm, mode="nn", out_dtype=F32, name="proj_small")
    conv = _conv_fwd(proj, conv_w, conv_b, "conv_fwd")
    dt_g, b_g, a_g = _heads_first(small[:, 0:32], 8), _heads_first(small[:, 32:48], 8), _heads_first(small[:, 48:64], 8)
    pv_ssm = jnp.stack([ssm_dt_bias.reshape(8, 4), ssm_A_log.reshape(8, 4), ssm_D.reshape(8, 4)], axis=1)
    nw_ssm = ssm_norm_w.reshape(8, 1, SSM_GROUP_WIDTH)
    pv_gdn = jnp.stack([gdn_dt_bias.reshape(8, 2), gdn_A_log.reshape(8, 2)], axis=1)
    y_ssd, ysn, st_ssm = _ssd_fwd(conv, proj, dt_g, pv_ssm, nw_ssm, "ssd_fwd")
    o_pre, ogn, st_gdn, t_inv = _gdn_fwd(conv, proj, b_g, a_g, pv_gdn, gdn_norm_w, "gdn_fwd")
    g_su, g_gu, g_out, g_mu, g_md = _exchange_wait(rest, ["gather"] * 5, ogn, "gather_rest_wait")
    w_su, w_gu = g_su.reshape(2 * Dm, Dm), g_gu.reshape(2 * Dm, Dm)
    w_o, w_mu, w_md = g_out.reshape(Dm, Dm), _by_cols(g_mu), g_md.reshape(4 * Dm, Dm)
    ys = _mm(ysn, w_su, S, Dm, 2 * Dm, mode="nn", out_dtype=F32, name="ssm_up")
    yg, merged = _gdn_up_merge(ogn, w_gu, ys, proj, "gdn_up_merge")
    mo, x1, h2 = _mix_out_post_pre(merged, w_o, x2, norm_mix_post, g1, norm_mlp_pre, sc2, sh2, "mix_out_post_pre")
    u, act = _mm(h2, w_mu, S, 4 * Dm, Dm, mode="nn", out_dtype=F32, epi="relu2", tn=MM_WIDE_N, name="mlp_up")
    y_mlp = _mm(act, w_md, S, Dm, 4 * Dm, mode="nn", out_dtype=F32, name="mlp_down")
    dx2, loss_loc, dy, dg2, dw_post2 = _final_fwd_bwd(x1, y_mlp, norm_mlp_post, g2, tgt, "post_mlp_loss_bwd")

    du = _mm(dy, w_md, S, 4 * Dm, Dm, mode="nt", out_dtype=BF16, epi="drelu2", extra=u, tn=MM_WIDE_N, name="mlp_down_dx")
    gw_md = _mm(act, dy, 4 * Dm, Dm, S, mode="tn", out_dtype=BF16, name="mlp_down_dw")
    dh2 = _mm(du, w_mu, S, Dm, 4 * Dm, mode="nt", out_dtype=F32, name="mlp_up_dx")
    gw_mu = _mm(h2, du, Dm, 4 * Dm, S, mode="tn", out_dtype=BF16, name="mlp_up_dw")
    mlp_x = _exchange_start([_to_col_shards(gw_mu), gw_md.reshape(N_DEV, -1, Dm)], ["a2a"] * 2, gw_md,
                            "grads_mlp_start")
    dx1, dsh2, dsc2, dw_pre2 = _pre_bwd(dh2, x1, norm_mlp_pre, sc2 + mlp_x[4][0:1, 0:1], dx2, "pre_mlp_bwd")
    dmo, dg1, dw_post1 = _post_bwd(dx1, mo, norm_mix_post, g1, "post_mix_bwd")
    gw_o = _mm(merged, dmo, Dm, Dm, S, mode="tn", out_dtype=BF16, name="mix_out_dw")
    dys, dyg, d_proj = _mix_out_dx_merge_bwd(dmo, w_o, ys, yg, proj, lax.empty((S, N_MAIN), BF16), "mix_out_dx_merge")
    dysn = _mm(dys, w_su, S, 2 * Dm, Dm, mode="nt", out_dtype=F32, tn=MM_WIDE_N, name="ssm_up_dx")
    gw_su = _mm(ysn, dys, 2 * Dm, Dm, S, mode="tn", out_dtype=BF16, name="ssm_up_dw")
    dogn = _mm(dyg, w_gu, S, 2 * Dm, Dm, mode="nt", out_dtype=F32, tn=MM_WIDE_N, name="gdn_up_dx")
    gw_gu = _mm(ogn, dyg, 2 * Dm, Dm, S, mode="tn", out_dtype=BF16, name="gdn_up_dw")
    mix_x = _exchange_start([gw_su.reshape(N_DEV, -1, Dm), gw_gu.reshape(N_DEV, -1, Dm), gw_o.reshape(N_DEV, -1, Dm)],
                            ["a2a"] * 3, gw_gu, "grads_mix_start")
    d_conv, d_proj, ddt_g, dpv_ssm, dnw_ssm = _ssd_bwd(dysn, conv, proj, dt_g, pv_ssm + mix_x[4][0, 0], nw_ssm, y_ssd,
                                                       st_ssm, d_proj, lax.empty(conv.shape, BF16), "ssd_bwd")
    d_conv, d_proj, db_g, da_g, dpv_gdn, dnw_gdn = _gdn_bwd(dogn, conv, proj, b_g, a_g, pv_gdn, gdn_norm_w, o_pre,
                                                            st_gdn, t_inv, d_proj, d_conv, "gdn_bwd")
    d_proj, dconv_w, dconv_b = _conv_bwd(d_conv, proj, conv_w, conv_b, 0, d_proj, "conv_bwd")
    d_small = jnp.concatenate([_heads_last(ddt_g), _heads_last(db_g), _heads_last(da_g),
                               jnp.zeros((S, N_SMALL - 64), F32)], axis=1).astype(BF16)
    gw_small = _mm(h, d_small, Dm, N_SMALL, S, mode="tn", out_dtype=BF16, name="proj_small_dw")
    main_cols = _mm(h, d_proj, Dm, N_MAIN, S, mode="tn", out_dtype=BF16, name="proj_main_dw")
    n_shard = w_in.shape[2]
    slabs = [_orig_cols(main_cols, gw_small, i * n_shard, (i + 1) * n_shard) for i in range(N_DEV)]
    to_c0, to_c1 = jnp.stack(slabs[0::2]), jnp.stack(slabs[1::2])
    chip_sum = _add_pair(to_c0, to_c1, _swap_sibling(to_c0, to_c1, "grads_w_in_pair"), "grads_w_in_pair_sum")
    in_x = _exchange_start([chip_sum], ["a2a"], gw_small, "grads_w_in_start", chips=True)
    dh = _mm(d_small, w_small + in_x[4][0:1, 0:1].astype(BF16), S, Dm, N_SMALL, mode="nt", out_dtype=F32,
             name="proj_small_dx")
    dh = _mm(d_proj, w_main, S, Dm, N_MAIN, mode="nt", out_dtype=F32, add=dh, name="proj_main_dx")
    dx, dsh1, dsc1, dw_pre1 = _pre_bwd(dh, x2, norm_mix_pre, sc1, dx1, "pre_mix_bwd")
    r_mu, r_md = _exchange_wait(mlp_x, ["a2a"] * 2, dx, "grads_mlp_wait")
    r_su, r_gu, r_o = _exchange_wait(mix_x, ["a2a"] * 3, dx, "grads_mix_wait")

    dconv_b = dconv_b[:, :ssm_conv_b.shape[1]]
    dmod = jnp.concatenate([dsh1, dsc1, dg1, dsh2, dsc2, dg2], axis=1)
    small_vec = jnp.concatenate(
        [dmod, dw_pre1, dw_post1, dconv_b, dpv_ssm[:, 0].reshape(1, 32), dpv_ssm[:, 1].reshape(1, 32),
         dpv_ssm[:, 2].reshape(1, 32), dnw_ssm.reshape(1, 2048), dpv_gdn[:, 0].reshape(1, 16),
         dpv_gdn[:, 1].reshape(1, 16), jnp.sum(dnw_gdn, axis=0), dw_pre2, dw_post2, dconv_w.reshape(1, -1)], axis=1)
    n_vec = small_vec.shape[1]
    small_vec = jnp.pad(small_vec, ((0, 0), (0, (-n_vec) % 1024))).reshape(-1, 1024)
    (small_all,) = _exchange([small_vec], ["gather"], "gather_small_grads")
    small_all = small_all.reshape(N_DEV, -1)
    dmod_cols = lax.dynamic_slice(small_all, (0, me * n_ada), (N_DEV, n_ada))
    gw_ada = _ada_bwd(c_all.T, dmod_cols, "ada_bwd")
    conv_all = small_all[:, N_REPLICATED:n_vec].reshape(N_DEV, CONV_K, 2 * N_DEV * 512)
    conv_contrib = jnp.concatenate(
        [lax.dynamic_slice(conv_all, (0, 0, me * 512), (N_DEV, CONV_K, 512)),
         lax.dynamic_slice(conv_all, (0, 0, N_DEV * 512 + me * 512), (N_DEV, CONV_K, 512))], axis=1)
    rep_contrib = small_all[:, :N_REPLICATED].reshape(N_DEV, N_REPLICATED // 128, 128)

    results = {}

    def adam_big(nm, contrib):
        w3 = given[nm]
        res = _adam(contrib, w3[0][0], w3[1][0], w3[2][0], "adam_" + nm)
        results[nm] = tuple(r.reshape(w3[0].shape) for r in res)

    adam_big("w_ada", gw_ada[None])
    adam_big("w_ssm_up", r_su)
    adam_big("w_gdn_up", r_gu)
    adam_big("w_out", r_o)
    adam_big("w_mlp_up", r_mu)
    adam_big("w_mlp_down", r_md)
    (r_in,) = _exchange_wait(in_x, ["a2a"], results["w_mlp_down"][0], "grads_w_in_wait", chips=True)
    adam_big("w_in", r_in)
    packed = [jnp.concatenate([given[nm][i] for nm in REPLICATED], axis=1).reshape(N_REPLICATED // 128, 128)
              for i in range(3)]
    rep_res = _adam(rep_contrib, packed[0], packed[1], packed[2], "adam_replicated")
    pos = 0
    for nm in REPLICATED:
        size = given[nm][0].shape[1]
        results[nm] = tuple(r.reshape(1, N_REPLICATED)[:, pos:pos + size] for r in rep_res)
        pos += size
    conv_wmv = [jnp.concatenate([given["ssm_conv_w"][i][0], given["gdn_conv_w"][i][0]], axis=0) for i in range(3)]
    conv_res = _adam(conv_contrib, conv_wmv[0], conv_wmv[1], conv_wmv[2], "adam_conv_w")
    results["ssm_conv_w"] = tuple(r[None, :CONV_K] for r in conv_res)
    results["gdn_conv_w"] = tuple(r[None, CONV_K:] for r in conv_res)

    loss = lax.psum(loss_loc[0, 0], ("x", "y", "c"))
    return (loss, dx[None]) + tuple(results[nm][i] for i in range(4) for nm in WEIGHTS)
```

```python
import jax
import jax.numpy as jnp
from jax import lax
from jax.experimental import pallas as pl
from jax.experimental.pallas import tpu as pltpu

F32 = jnp.float32
BF16 = jnp.bfloat16
N_DEV = 8
D_MODEL = 1024
EPS = 1e-6
CONV_K = 4
SSM_CHUNK = 128
SSM_HEAD_DIM = 64
SSM_D_STATE = 128
SSM_GROUPS = 8
SSM_HEADS_PER_GROUP = 4
SSM_GROUP_WIDTH = SSM_HEADS_PER_GROUP * SSM_HEAD_DIM
SSM_GROUPS_PER_STEP = 8
GDN_CHUNK = 64
GDN_HEAD = 128
GDN_QK_HEADS = 8
GDN_V_PER_QK = 2
GDN_QK_PER_STEP = 8
GDN_INV_BLOCK = 16
C_ZS, C_XBC, C_QKV, C_ZG, C_GS, C_GG, N_MAIN = 0, 2048, 6144, 10240, 12288, 13312, 14336
N_SMALL = 128
ADAM_LR, ADAM_B1, ADAM_B2, ADAM_EPS, ADAM_WD, ADAM_STEP = 0.001, 0.9, 0.999, 1e-08, 0.01, 10
VMEM_LIMIT = 56 * 1024 * 1024
MM_WHOLE_K = 4096
MM_SPLIT_K = 2048
MM_WIDE_N = 2048
NEG_INF = float("-inf")

_NT = (((1,), (1,)), ((), ()))
_NN = (((1,), (0,)), ((), ()))
_TN = (((0,), (0,)), ((), ()))


def _params(*sem):
    return pltpu.CompilerParams(dimension_semantics=sem, vmem_limit_bytes=VMEM_LIMIT)


def _dot(a, b, dims=_NN):
    return lax.dot_general(a.astype(BF16), b.astype(BF16), dims, preferred_element_type=F32)


def _split(a):
    hi = a.astype(BF16)
    return hi, (a - hi.astype(F32)).astype(BF16)


def _dot3(a, b, dims=_NN):
    ah, al = _split(a)
    bh, bl = _split(b)
    d = lambda u, v: lax.dot_general(u, v, dims, preferred_element_type=F32)
    return d(ah, bh) + (d(ah, bl) + d(al, bh))


def _dot2(a, b, dims=_NN):
    ah, al = _split(a)
    bb = b.astype(BF16)
    d = lambda u: lax.dot_general(u, bb, dims, preferred_element_type=F32)
    return d(ah) + d(al)


def _sigmoid(x):
    return 0.5 * jnp.tanh(0.5 * x) + 0.5


def _silu(x):
    return x * _sigmoid(x)


def _dsilu(x):
    s = _sigmoid(x)
    return s * (1.0 + x * (1.0 - s))


def _softplus(x):
    return jnp.maximum(x, 0.0) + jnp.log1p(jnp.exp(-jnp.abs(x)))


def _iota(n, m, d):
    return lax.broadcasted_iota(jnp.int32, (n, m), d)


def _rowsum(x):
    return jnp.sum(x, axis=1, keepdims=True)


def _colsum(x):
    return jnp.sum(x, axis=0, keepdims=True)


def _total(x):
    return _rowsum(_colsum(x))


MXU_LANES = 128


def _parts(x, n):
    out = []
    for _ in range(n):
        p = x.astype(BF16)
        out.append(p)
        x = x - p.astype(F32)
    return out


def _sum_by(m01, x, dims=_NN, n=3):
    return sum(lax.dot_general(m01, p, dims, preferred_element_type=F32) for p in _parts(x, n))


def _row_col_sums(q):
    ones = jnp.ones((q.shape[0], MXU_LANES), BF16)
    acc = 0.0
    for p in _parts(q, 2):
        acc = acc + (lax.dot_general(p, ones, _NN, preferred_element_type=F32)
                     - lax.dot_general(p, ones, _TN, preferred_element_type=F32))
    return acc[:, 0:1]


def _cumsum_forms(col, ii, jj):
    lower = jnp.where(ii >= jj, 1.0, 0.0).astype(BF16)
    cum_col = _sum_by(lower, jnp.broadcast_to(col, (col.shape[0], MXU_LANES)))[:, 0:1]
    cum_row = _colsum(jnp.where(ii <= jj, col, 0.0))
    return cum_col, cum_row


def _rev_cumsum_col(col, ii, jj):
    upper = jnp.where(ii <= jj, 1.0, 0.0).astype(BF16)
    return _sum_by(upper, jnp.broadcast_to(col, (col.shape[0], MXU_LANES)))[:, 0:1]


def _blk(dim, pref):
    return pref if dim % pref == 0 else dim


def _lockstep(gens):
    gens = list(gens)
    while gens:
        alive = []
        for g in gens:
            try:
                next(g)
                alive.append(g)
            except StopIteration:
                pass
        gens = alive


def _mm(a, b, M, N, K, *, mode, out_dtype, name, a_off=(0, 0), b_off=(0, 0), add=None, epi=None, extra=None,
        tm=1024, tn=1024):
    tm, tn = _blk(M, tm), _blk(N, tn)
    tk = K if K <= MM_WHOLE_K else _blk(K, MM_SPLIT_K)
    nk = K // tk
    if mode == "tn":
        a_spec = pl.BlockSpec((tk, tm), lambda i, j, k: (k + a_off[0] // tk, i + a_off[1] // tm))
        assert a_off[0] % tk == 0 and a_off[1] % tm == 0
    else:
        a_spec = pl.BlockSpec((tm, tk), lambda i, j, k: (i + a_off[0] // tm, k + a_off[1] // tk))
        assert a_off[0] % tm == 0 and a_off[1] % tk == 0
    if mode == "nt":
        b_spec = pl.BlockSpec((tn, tk), lambda i, j, k: (j + b_off[0] // tn, k + b_off[1] // tk))
        assert b_off[0] % tn == 0 and b_off[1] % tk == 0
    else:
        b_spec = pl.BlockSpec((tk, tn), lambda i, j, k: (k + b_off[0] // tk, j + b_off[1] // tn))
        assert b_off[0] % tk == 0 and b_off[1] % tn == 0
    dims = {"nn": _NN, "nt": _NT, "tn": _TN}[mode]
    o_spec = pl.BlockSpec((tm, tn), lambda i, j, k: (i, j))
    ins, in_specs = [a, b], [a_spec, b_spec]
    if add is not None:
        ins.append(add)
        in_specs.append(o_spec)
    if extra is not None:
        ins.append(extra)
        in_specs.append(o_spec)
    n_in = len(ins)
    if epi == "relu2":
        out_shape = (jax.ShapeDtypeStruct((M, N), BF16), jax.ShapeDtypeStruct((M, N), BF16))
        out_specs = (o_spec, o_spec)
    else:
        out_shape = jax.ShapeDtypeStruct((M, N), out_dtype)
        out_specs = o_spec

    def body(*refs):
        a_ref, b_ref = refs[0], refs[1]
        outs = refs[n_in:] if nk == 1 else refs[n_in:-1]

        def finish(r):
            pos = 2
            if add is not None:
                r = r + refs[pos][...]
                pos += 1
            if epi == "relu2":
                p = jnp.maximum(r, 0.0)
                outs[0][...] = p.astype(BF16)
                outs[1][...] = (p * p).astype(BF16)
            elif epi == "drelu2":
                outs[0][...] = (r * (2.0 * refs[pos][...].astype(F32))).astype(out_dtype)
            else:
                outs[0][...] = r.astype(out_dtype)

        if nk == 1:
            finish(_dot(a_ref[...], b_ref[...], dims))
            return
        acc = refs[-1]
        k = pl.program_id(2)

        @pl.when(k == 0)
        def _():
            acc[...] = jnp.zeros_like(acc)

        acc[...] += _dot(a_ref[...], b_ref[...], dims)

        @pl.when(k == nk - 1)
        def _():
            finish(acc[...])

    return pl.pallas_call(
        body, name=name, grid=(M // tm, N // tn, nk), in_specs=in_specs, out_specs=out_specs, out_shape=out_shape,
        scratch_shapes=[] if nk == 1 else [pltpu.VMEM((tm, tn), F32)],
        compiler_params=_params("parallel", "parallel", "arbitrary"))(*ins)


def _mm_rows(a, b, M, N, K, *, mode, name, extras, out_shapes, out_specs, epilogue, aliases=None, tm=512):
    tm = _blk(M, tm)
    a_spec = pl.BlockSpec((tm, K), lambda i: (i, 0))
    b_spec = pl.BlockSpec((K, N) if mode == "nn" else (N, K), lambda i: (0, 0))
    dims = _NN if mode == "nn" else _NT
    n_ex = len(extras)

    def body(a_ref, b_ref, *refs):
        epilogue(_dot(a_ref[...], b_ref[...], dims), refs[:n_ex], refs[n_ex:])

    return pl.pallas_call(
        body, name=name, grid=(M // tm,), in_specs=[a_spec, b_spec] + [sp for _, sp in extras],
        out_specs=tuple(out_specs), out_shape=tuple(out_shapes), input_output_aliases=aliases or {},
        compiler_params=_params("parallel"))(a, b, *[x for x, _ in extras])


def _row_spec(tb, d):
    return pl.BlockSpec((tb, d), lambda i: (i, 0))


def _vec_spec(d):
    return pl.BlockSpec((1, d), lambda i: (0, 0))


def _pre_fwd(x, w, sc, sh, name):
    S, Dm = x.shape
    tb = _blk(S, 512)

    def body(x_ref, w_ref, sc_ref, sh_ref, h_ref):
        xv = x_ref[...]
        r = lax.rsqrt(jnp.mean(xv * xv, axis=-1, keepdims=True) + EPS)
        h_ref[...] = ((xv * r * w_ref[...]) * (1.0 + sc_ref[...]) + sh_ref[...]).astype(BF16)

    return pl.pallas_call(
        body, name=name, grid=(S // tb,), in_specs=[_row_spec(tb, Dm)] + [_vec_spec(Dm)] * 3,
        out_specs=_row_spec(tb, Dm), out_shape=jax.ShapeDtypeStruct((S, Dm), BF16),
        compiler_params=_params("parallel"))(x, w, sc, sh)


def _final_fwd_bwd(x, y, w, g, target, name):
    S, Dm = x.shape
    tb = _blk(S, 512)
    nb = S // tb

    def body(x_ref, y_ref, w_ref, g_ref, t_ref, dx_ref, loss_ref, dy_ref, dg_ref, dw_ref, acc):
        i = pl.program_id(0)

        @pl.when(i == 0)
        def _():
            acc[...] = jnp.zeros_like(acc)
            dg_ref[...] = jnp.zeros_like(dg_ref)
            dw_ref[...] = jnp.zeros_like(dw_ref)

        yv = y_ref[...]
        r = lax.rsqrt(jnp.mean(yv * yv, axis=-1, keepdims=True) + EPS)
        yh = yv * r
        n = yh * w_ref[...]
        e = (x_ref[...] + g_ref[...] * n) - t_ref[...]
        dv = e * (1.0 / Dm)
        dx_ref[...] = dv
        acc[...] += _colsum(e * e)
        dg_ref[...] += _colsum(dv * n)
        dn = dv * g_ref[...]
        dw_ref[...] += _colsum(dn * yh)
        dyh = dn * w_ref[...]
        dy_ref[...] = (r * (dyh - yh * jnp.mean(dyh * yh, axis=-1, keepdims=True))).astype(BF16)

        @pl.when(i == nb - 1)
        def _():
            loss_ref[...] = (0.5 / Dm) * _rowsum(acc[...])

    row, vec = _row_spec(tb, Dm), _vec_spec(Dm)
    vec_shape = jax.ShapeDtypeStruct((1, Dm), F32)
    return pl.pallas_call(
        body, name=name, grid=(nb,), in_specs=[row, row, vec, vec, row],
        out_specs=(row, pl.BlockSpec((1, 1), lambda i: (0, 0)), row, vec, vec),
        out_shape=(jax.ShapeDtypeStruct((S, Dm), F32), jax.ShapeDtypeStruct((1, 1), F32),
                   jax.ShapeDtypeStruct((S, Dm), BF16), vec_shape, vec_shape),
        scratch_shapes=[pltpu.VMEM((1, Dm), F32)], compiler_params=_params("arbitrary"))(x, y, w, g, target)


def _post_bwd(dxo, y, w, g, name):
    S, Dm = y.shape
    tb = _blk(S, 512)

    def body(d_ref, y_ref, w_ref, g_ref, dy_ref, dg_ref, dw_ref):
        i = pl.program_id(0)

        @pl.when(i == 0)
        def _():
            dg_ref[...] = jnp.zeros_like(dg_ref)
            dw_ref[...] = jnp.zeros_like(dw_ref)

        yv, dv = y_ref[...], d_ref[...]
        r = lax.rsqrt(jnp.mean(yv * yv, axis=-1, keepdims=True) + EPS)
        yh = yv * r
        dg_ref[...] += _colsum(dv * (yh * w_ref[...]))
        dn = dv * g_ref[...]
        dw_ref[...] += _colsum(dn * yh)
        dyh = dn * w_ref[...]
        dy_ref[...] = (r * (dyh - yh * jnp.mean(dyh * yh, axis=-1, keepdims=True))).astype(BF16)

    return pl.pallas_call(
        body, name=name, grid=(S // tb,), in_specs=[_row_spec(tb, Dm)] * 2 + [_vec_spec(Dm)] * 2,
        out_specs=(_row_spec(tb, Dm), _vec_spec(Dm), _vec_spec(Dm)),
        out_shape=(jax.ShapeDtypeStruct((S, Dm), BF16), jax.ShapeDtypeStruct((1, Dm), F32),
                   jax.ShapeDtypeStruct((1, Dm), F32)),
        compiler_params=_params("arbitrary"))(dxo, y, w, g)


def _pre_bwd(dh, x, w, sc, dres, name):
    S, Dm = x.shape
    tb = _blk(S, 512)

    def body(dh_ref, x_ref, w_ref, sc_ref, dr_ref, dx_ref, dsh_ref, dsc_ref, dw_ref):
        i = pl.program_id(0)

        @pl.when(i == 0)
        def _():
            dsh_ref[...] = jnp.zeros_like(dsh_ref)
            dsc_ref[...] = jnp.zeros_like(dsc_ref)
            dw_ref[...] = jnp.zeros_like(dw_ref)

        xv, dv = x_ref[...], dh_ref[...]
        r = lax.rsqrt(jnp.mean(xv * xv, axis=-1, keepdims=True) + EPS)
        xh = xv * r
        one_sc = 1.0 + sc_ref[...]
        dsh_ref[...] += _colsum(dv)
        dsc_ref[...] += _colsum(dv * (xh * w_ref[...]))
        dw_ref[...] += _colsum(dv * one_sc * xh)
        dxh = dv * one_sc * w_ref[...]
        dx_ref[...] = dr_ref[...] + r * (dxh - xh * jnp.mean(dxh * xh, axis=-1, keepdims=True))

    vec = jax.ShapeDtypeStruct((1, Dm), F32)
    return pl.pallas_call(
        body, name=name, grid=(S // tb,),
        in_specs=[_row_spec(tb, Dm)] * 2 + [_vec_spec(Dm)] * 2 + [_row_spec(tb, Dm)],
        out_specs=(_row_spec(tb, Dm), _vec_spec(Dm), _vec_spec(Dm), _vec_spec(Dm)),
        out_shape=(jax.ShapeDtypeStruct((S, Dm), F32), vec, vec, vec),
        compiler_params=_params("arbitrary"))(dh, x, w, sc, dres)


D_PROJ_ANY = pl.BlockSpec(memory_space=pl.ANY)


def _gate_specs(tm, Dm):
    return (pl.BlockSpec((tm, Dm), lambda i: (i, C_GS // Dm)), pl.BlockSpec((tm, Dm), lambda i: (i, C_GG // Dm)))


def _gdn_up_merge(ogn, w_gu, ys, proj, name):
    S, K = ogn.shape
    Dm = ys.shape[1]
    tm = _blk(S, 512)
    row = _row_spec(tm, Dm)

    def epilogue(r, ex, out):
        ys_ref, gs_ref, gg_ref = ex
        out[0][...] = r
        out[1][...] = (_sigmoid(gs_ref[...].astype(F32)) * ys_ref[...]
                       + _sigmoid(gg_ref[...].astype(F32)) * r).astype(BF16)

    gs_spec, gg_spec = _gate_specs(tm, Dm)
    return _mm_rows(ogn, w_gu, S, Dm, K, mode="nn", name=name, tm=tm,
                    extras=[(ys, row), (proj, gs_spec), (proj, gg_spec)],
                    out_shapes=[jax.ShapeDtypeStruct((S, Dm), F32), jax.ShapeDtypeStruct((S, Dm), BF16)],
                    out_specs=[row, row], epilogue=epilogue)


def _mix_out_post_pre(merged, w_o, x, w_post, g, w_pre, sc, sh, name):
    S, Dm = x.shape
    tm = _blk(S, 512)
    row, vec = _row_spec(tm, Dm), _vec_spec(Dm)

    def epilogue(r, ex, out):
        x_ref, wpost_ref, g_ref, wpre_ref, sc_ref, sh_ref = ex
        out[0][...] = r
        rr = lax.rsqrt(jnp.mean(r * r, axis=-1, keepdims=True) + EPS)
        x1 = x_ref[...] + g_ref[...] * (r * rr * wpost_ref[...])
        out[1][...] = x1
        r1 = lax.rsqrt(jnp.mean(x1 * x1, axis=-1, keepdims=True) + EPS)
        out[2][...] = ((x1 * r1 * wpre_ref[...]) * (1.0 + sc_ref[...]) + sh_ref[...]).astype(BF16)

    return _mm_rows(merged, w_o, S, Dm, Dm, mode="nn", name=name, tm=tm,
                    extras=[(x, row), (w_post, vec), (g, vec), (w_pre, vec), (sc, vec), (sh, vec)],
                    out_shapes=[jax.ShapeDtypeStruct((S, Dm), F32), jax.ShapeDtypeStruct((S, Dm), F32),
                                jax.ShapeDtypeStruct((S, Dm), BF16)], out_specs=[row, row, row], epilogue=epilogue)


def _mix_out_dx_merge_bwd(dmo, w_o, ys, yg, proj, d_proj, name):
    S, Dm = ys.shape
    tm = _blk(S, 512)
    row = _row_spec(tm, Dm)

    def epilogue(d, ex, out):
        ys_ref, yg_ref, gs_ref, gg_ref, _ = ex
        ss, sg = _sigmoid(gs_ref[...].astype(F32)), _sigmoid(gg_ref[...].astype(F32))
        out[0][...] = (d * ss).astype(BF16)
        out[1][...] = (d * sg).astype(BF16)
        out[2][:, :Dm] = (d * ys_ref[...] * ss * (1.0 - ss)).astype(BF16)
        out[2][:, Dm:] = (d * yg_ref[...] * sg * (1.0 - sg)).astype(BF16)

    gs_spec, gg_spec = _gate_specs(tm, Dm)
    return _mm_rows(dmo, w_o, S, Dm, Dm, mode="nt", name=name, tm=tm,
                    extras=[(ys, row), (yg, row), (proj, gs_spec), (proj, gg_spec), (d_proj, D_PROJ_ANY)],
                    out_shapes=[jax.ShapeDtypeStruct((S, Dm), BF16), jax.ShapeDtypeStruct((S, Dm), BF16),
                                jax.ShapeDtypeStruct(d_proj.shape, BF16)],
                    out_specs=[row, row, pl.BlockSpec((tm, 2 * Dm), lambda i: (i, C_GS // (2 * Dm)))],
                    epilogue=epilogue, aliases={6: 2})


CONV_COLS = 128
CONV_BWD_ROWS = 256


def _taps_down(x):
    rows = _iota(x.shape[0], x.shape[1], 0)
    return [x] + [jnp.where(rows >= k, pltpu.roll(x, k, 0), 0.0) for k in range(1, CONV_K)]


def _conv_pre(taps, w_ref, b_ref):
    pre = taps[0] * w_ref[CONV_K - 1:CONV_K, :] + b_ref[...]
    for k in range(1, CONV_K):
        pre = pre + taps[k] * w_ref[CONV_K - 1 - k:CONV_K - k, :]
    return pre


def _conv_dx(dpre, w_ref):
    n = dpre.shape[0]
    rows = _iota(n, dpre.shape[1], 0)
    dx = dpre * w_ref[CONV_K - 1:CONV_K, :]
    for k in range(1, CONV_K):
        dx = dx + jnp.where(rows < n - k, pltpu.roll(dpre, n - k, 0), 0.0) * w_ref[CONV_K - 1 - k:CONV_K - k, :]
    return dx


def _conv_fwd(proj, w, b, name):
    S = proj.shape[0]
    n = w.shape[1]
    cb = CONV_COLS

    def body(x_ref, w_ref, b_ref, o_ref):
        o_ref[...] = _silu(_conv_pre(_taps_down(x_ref[...].astype(F32)), w_ref, b_ref)).astype(BF16)

    return pl.pallas_call(
        body, name=name, grid=(n // cb,),
        in_specs=[pl.BlockSpec((S, cb), lambda j: (0, j + C_XBC // cb)), pl.BlockSpec((CONV_K, cb), lambda j: (0, j)),
                  pl.BlockSpec((1, cb), lambda j: (0, j))],
        out_specs=pl.BlockSpec((S, cb), lambda j: (0, j)), out_shape=jax.ShapeDtypeStruct((S, n), BF16),
        compiler_params=_params("parallel"))(proj, w, b)


def _conv_bwd(dact, proj, w, b, col0, d_proj, name):
    S, n = dact.shape
    cb = CONV_COLS
    o = col0 // cb

    R, HALO = _blk(S, CONV_BWD_ROWS), 16
    n_chunks = S // R

    def body(d_ref, x_ref, w_ref, b_ref, _, dx_ref, dw_ref, db_ref):
        def chunk(r0, first, last, sums):
            lo, hi = (0 if first else HALO), (0 if last else HALO)
            start = r0 - lo if isinstance(r0, int) else pl.multiple_of(r0 - lo, HALO)
            xe = x_ref[pl.ds(start, lo + R + hi), :].astype(F32)
            rows = _iota(lo + R + hi, cb, 0)
            taps = [xe[lo:, :]]
            for k in range(1, CONV_K):
                t = pltpu.roll(xe, k, 0)
                taps.append((jnp.where(rows >= k, t, 0.0) if first else t)[lo:, :])
            dpre_e = d_ref[pl.ds(r0, R + hi), :].astype(F32) * _dsilu(_conv_pre(taps, w_ref, b_ref))
            dpre = dpre_e[0:R, :]
            db, dw = sums
            db = db + _colsum(dpre)
            dw = [dw[k] + _colsum(dpre * taps[k][0:R, :]) for k in range(CONV_K)]
            rows_e = _iota(R + hi, cb, 0)
            dx = dpre * w_ref[CONV_K - 1:CONV_K, :]
            for k in range(1, CONV_K):
                t = pltpu.roll(dpre_e, R + hi - k, 0)
                t = jnp.where(rows_e < R - k, t, 0.0) if last else t
                dx = dx + t[0:R, :] * w_ref[CONV_K - 1 - k:CONV_K - k, :]
            dx_ref[pl.ds(r0, R), :] = dx.astype(BF16)
            return db, dw

        zero = jnp.zeros((1, cb), F32)
        sums = chunk(0, True, n_chunks == 1, (zero, [zero] * CONV_K))
        if n_chunks > 2:
            def step(i, carry):
                db, dw = chunk(pl.multiple_of(i * R, R), False, False, (carry[0], list(carry[1:])))
                return (db,) + tuple(dw)
            carry = lax.fori_loop(1, n_chunks - 1, step, (sums[0],) + tuple(sums[1]))
            sums = (carry[0], list(carry[1:]))
        if n_chunks > 1:
            sums = chunk((n_chunks - 1) * R, False, True, sums)
        db_ref[...] = sums[0]
        for k in range(CONV_K):
            dw_ref[CONV_K - 1 - k:CONV_K - k, :] = sums[1][k]

    return pl.pallas_call(
        body, name=name, grid=(n // cb,),
        in_specs=[pl.BlockSpec((S, cb), lambda j: (0, j)), pl.BlockSpec((S, cb), lambda j: (0, j + o + C_XBC // cb)),
                  pl.BlockSpec((CONV_K, cb), lambda j: (0, j + o)), pl.BlockSpec((1, cb), lambda j: (0, j + o)),
                  D_PROJ_ANY],
        out_specs=(pl.BlockSpec((S, cb), lambda j: (0, j + o + C_XBC // cb)),
                   pl.BlockSpec((CONV_K, cb), lambda j: (0, j)), pl.BlockSpec((1, cb), lambda j: (0, j))),
        out_shape=(jax.ShapeDtypeStruct(d_proj.shape, BF16), jax.ShapeDtypeStruct((CONV_K, n), F32),
                   jax.ShapeDtypeStruct((1, n), F32)),
        input_output_aliases={4: 0}, compiler_params=_params("parallel"))(dact, proj, w, b, d_proj)


def _ssd_specs(L, order):
    G = SSM_GROUPS_PER_STEP
    W, N = G * SSM_GROUP_WIDTH, G * SSM_D_STATE
    x_spec = pl.BlockSpec((L, W), lambda g, c: (order(c), g))
    b_spec = pl.BlockSpec((L, N), lambda g, c: (order(c), 2048 // N + g))
    c_spec = pl.BlockSpec((L, N), lambda g, c: (order(c), 3072 // N + g))
    z_spec = pl.BlockSpec((L, W), lambda g, c: (order(c), C_ZS // W + g))
    dt_spec = pl.BlockSpec((G, L, SSM_HEADS_PER_GROUP), lambda g, c: (g, order(c), 0))
    p_spec = pl.BlockSpec((G, 3, SSM_HEADS_PER_GROUP), lambda g, c: (g, 0, 0))
    nw_spec = pl.BlockSpec((G, 1, SSM_GROUP_WIDTH), lambda g, c: (g, 0, 0))
    s_spec = pl.BlockSpec((G, 1, SSM_GROUP_WIDTH, SSM_D_STATE), lambda g, c: (g, order(c), 0, 0))
    return x_spec, b_spec, c_spec, z_spec, dt_spec, p_spec, nw_spec, s_spec


class _SsdGroup:
    def __init__(self, L):
        P, H, W = SSM_HEAD_DIM, SSM_HEADS_PER_GROUP, SSM_GROUP_WIDTH
        self.L = L
        self.ii, self.jj = _iota(L, L, 0), _iota(L, L, 1)
        self.lower = jnp.where(self.ii >= self.jj, 1.0, 0.0).astype(BF16)
        self.upper = jnp.where(self.ii <= self.jj, 1.0, 0.0).astype(BF16)
        self.lo = _iota(L, 2 * P, 1) < P
        self.lo_row = _iota(1, 2 * P, 1) < P
        bi, bj = _iota(W, W, 0), _iota(W, W, 1)
        self.block = jnp.where(bi // P == bj // P, 1.0, 0.0).astype(BF16)
        si, sj = _iota(2 * P, W, 0), _iota(2 * P, W, 1)
        self.pick = jnp.where(sj == si * P, 1.0, 0.0).astype(BF16)
        self.ones = jnp.ones((L, 2 * P), BF16)

    def spread(self, v4):
        R = v4.shape[0]
        lo = self.lo if R == self.L else self.lo_row
        b = lambda h: jnp.broadcast_to(v4[:, h:h + 1], (R, 2 * SSM_HEAD_DIM))
        return jnp.concatenate([jnp.where(lo, b(0), b(1)), jnp.where(lo, b(2), b(3))], axis=1)

    def gather4(self, v):
        return jnp.concatenate([v[:, h * SSM_HEAD_DIM:h * SSM_HEAD_DIM + 1] for h in range(SSM_HEADS_PER_GROUP)],
                               axis=1)

    def head_sums(self, z):
        return sum(lax.dot_general(p, self.block, _NN, preferred_element_type=F32) for p in _parts(z, 2))

    def pair_cols(self, full, pair):
        ps = full[:, pair * 128:(pair + 1) * 128]
        sw = pltpu.roll(ps, SSM_HEAD_DIM, 1)
        return jnp.where(self.lo, ps, sw), jnp.where(self.lo, sw, ps)

    def gates(self, dt4_raw, p):
        L = self.L
        dtr = self.spread(dt4_raw + p[0:1, :])
        dt = _softplus(dtr)
        A = self.spread(-jnp.exp(p[1:2, :]))
        acum = _sum_by(self.lower, dt * A)
        yield
        rows = _sum_by(self.pick, acum, _NT)
        yield
        a_last = acum[L - 1:L, :]
        cols = self.pair_cols(acum, 0) + self.pair_cols(acum, 1)
        decay, decay_t = [], []
        for h in range(SSM_HEADS_PER_GROUP):
            seg = cols[h] - rows[h:h + 1, :]
            decay.append(jnp.exp(jnp.where(self.ii >= self.jj, seg, NEG_INF)))
            decay_t.append(jnp.exp(jnp.where(self.jj >= self.ii, -seg, NEG_INF)))
        return dict(dtr=dtr, dt=dt, A=A, D=self.spread(p[2:3, :]), acum=acum, eac=jnp.exp(acum), a_last=a_last,
                    wdec=jnp.exp(a_last - acum), decay=decay, decay_t=decay_t,
                    ea_last=[jnp.exp(rows[h:h + 1, L - 1:L]) for h in range(SSM_HEADS_PER_GROUP)])


def _ssd_fwd(conv, proj, dt_raw, pvec, nw, name):
    S = conv.shape[0]
    L, P, N, H, W, G = SSM_CHUNK, SSM_HEAD_DIM, SSM_D_STATE, SSM_HEADS_PER_GROUP, SSM_GROUP_WIDTH, SSM_GROUPS_PER_STEP
    nc = S // L

    def body(x_ref, b_ref, c_ref, z_ref, dt_ref, p_ref, nw_ref, y_ref, yn_ref, s0_ref, state):
        c = pl.program_id(1)

        @pl.when(c == 0)
        def _():
            state[...] = jnp.zeros_like(state)

        k = _SsdGroup(L)

        def group(gi):
            gsl = slice(gi * W, (gi + 1) * W)
            Bm, Cm = b_ref[:, gi * N:(gi + 1) * N], c_ref[:, gi * N:(gi + 1) * N]
            x = x_ref[:, gsl].astype(F32)
            S0 = state[gsl, :]
            s0_ref[gi, 0] = S0
            CB = _dot(Cm, Bm, _NT)
            y_off = _dot(Cm, S0, _NT)
            t = yield from k.gates(dt_ref[gi], p_ref[gi])
            xdt = x * t["dt"]
            s_new = _dot(xdt * t["wdec"], Bm, _TN)
            y_diag = []
            for pair in range(H // 2):
                xp = xdt[:, pair * 128:(pair + 1) * 128]
                y_diag.append(jnp.where(k.lo, _dot(CB * t["decay"][2 * pair], xp),
                                        _dot(CB * t["decay"][2 * pair + 1], xp)))
            yield
            y = jnp.concatenate(y_diag, axis=1) + y_off * t["eac"]
            for h in range(H):
                hsl = slice(gi * W + h * P, gi * W + (h + 1) * P)
                state[hsl, :] = S0[h * P:(h + 1) * P, :] * t["ea_last"][h] + s_new[h * P:(h + 1) * P, :]
            y_ref[:, gsl] = y
            y2 = (y + t["D"] * x) * _silu(z_ref[:, gsl].astype(F32))
            r = lax.rsqrt(jnp.mean(y2 * y2, axis=-1, keepdims=True) + EPS)
            yn_ref[:, gsl] = (y2 * r * nw_ref[gi]).astype(BF16)

        _lockstep(group(gi) for gi in range(G))

    x_spec, b_spec, c_spec, z_spec, dt_spec, p_spec, nw_spec, s_spec = _ssd_specs(L, lambda c: c)
    y_spec = pl.BlockSpec((L, G * W), lambda g, c: (c, g))
    return pl.pallas_call(
        body, name=name, grid=(SSM_GROUPS // G, nc),
        in_specs=[x_spec, b_spec, c_spec, z_spec, dt_spec, p_spec, nw_spec],
        out_specs=(y_spec, y_spec, s_spec),
        out_shape=(jax.ShapeDtypeStruct((S, SSM_GROUPS * W), F32), jax.ShapeDtypeStruct((S, SSM_GROUPS * W), BF16),
                   jax.ShapeDtypeStruct((SSM_GROUPS, nc, W, N), F32)),
        scratch_shapes=[pltpu.VMEM((G * W, N), F32)],
        compiler_params=_params("parallel", "arbitrary"))(conv, conv, conv, proj, dt_raw, pvec, nw)


def _ssd_bwd(dyn, conv, proj, dt_raw, pvec, nw, y_ssd, states, d_proj, d_conv, name):
    S = conv.shape[0]
    L, P, N, H, W, G = SSM_CHUNK, SSM_HEAD_DIM, SSM_D_STATE, SSM_HEADS_PER_GROUP, SSM_GROUP_WIDTH, SSM_GROUPS_PER_STEP
    nc = S // L
    assert G == SSM_GROUPS
    CB0, CC0 = SSM_GROUPS * W, SSM_GROUPS * (W + N)

    def body(dyn_ref, x_ref, b_ref, c_ref, z_ref, dt_ref, p_ref, nw_ref, y_ref, s0_ref, _, _2,
             dact_ref, dz_ref, ddt_ref, dp_ref, dnw_ref, dstate):
        c = pl.program_id(1)

        @pl.when(c == 0)
        def _():
            dstate[...] = jnp.zeros_like(dstate)
            dp_ref[...] = jnp.zeros_like(dp_ref)
            dnw_ref[...] = jnp.zeros_like(dnw_ref)

        k = _SsdGroup(L)
        last = (_iota(L, 1, 0) == L - 1)

        def group(gi):
            gsl = slice(gi * W, (gi + 1) * W)
            Bm, Cm = b_ref[:, gi * N:(gi + 1) * N], c_ref[:, gi * N:(gi + 1) * N]
            x, z = x_ref[:, gsl].astype(F32), z_ref[:, gsl].astype(F32)
            S0, dS1 = s0_ref[gi, 0], dstate[gsl, :]
            CB = _dot(Cm, Bm, _NT)
            CBt = _dot(Bm, Cm, _NT)
            y_off_raw = _dot(Cm, S0, _NT)
            dXs_raw = _dot(Bm, dS1, _NT)
            t = yield from k.gates(dt_ref[gi], p_ref[gi])
            y1 = y_ref[:, gsl] + t["D"] * x
            sz = _silu(z)
            y2 = y1 * sz
            r = lax.rsqrt(jnp.mean(y2 * y2, axis=-1, keepdims=True) + EPS)
            y2h = y2 * r
            dyn_v = dyn_ref[:, gsl]
            dnw_ref[gi] += _colsum(dyn_v * y2h)
            dy2h = dyn_v * nw_ref[gi]
            dy2 = r * (dy2h - y2h * jnp.mean(dy2h * y2h, axis=-1, keepdims=True))
            dz_ref[:, gsl] = (dy2 * y1 * _dsilu(z)).astype(BF16)
            dY = dy2 * sz
            X = x * t["dt"]
            dYe = dY * t["eac"]
            dC_s = _dot(dYe, S0)
            dB_s = _dot(X * t["wdec"], dS1)
            dS_c = _dot(dYe, Cm, _TN)
            dXm, Gs, Gts = [], [], []
            for pair in range(H // 2):
                dYp, Xp = dY[:, pair * 128:(pair + 1) * 128], X[:, pair * 128:(pair + 1) * 128]
                dXm.append(jnp.where(k.lo, _dot(CBt * t["decay_t"][2 * pair], dYp),
                                     _dot(CBt * t["decay_t"][2 * pair + 1], dYp)))
                for mask in (k.lo, ~k.lo):
                    Gs.append(_dot(jnp.where(mask, dYp, 0.0), Xp, _NT))
                    Gts.append(_dot(jnp.where(mask, Xp, 0.0), dYp, _NT))
            yield
            dXs = dXs_raw * t["wdec"]
            dX = jnp.concatenate(dXm, axis=1) + dXs
            dCB, dCBt, q_sums = 0.0, 0.0, []
            for h in range(H):
                M, Mt = CB * t["decay"][h], CBt * t["decay_t"][h]
                dCB = dCB + Gs[h] * t["decay"][h]
                dCBt = dCBt + Gts[h] * t["decay_t"][h]
                d = Gs[h] * M - Gts[h] * Mt
                q_sums.append(sum(lax.dot_general(pt, k.ones, _NN, preferred_element_type=F32)
                                  for pt in _parts(d, 2)))
            q_f = jnp.concatenate([jnp.where(k.lo, q_sums[0], q_sums[1]), jnp.where(k.lo, q_sums[2], q_sums[3])],
                                  axis=1)
            x_dxs = X * dXs
            tot = [_total(dS1[h * P:(h + 1) * P, :] * S0[h * P:(h + 1) * P, :]) * t["ea_last"][h] for h in range(H)]
            tot_f = k.spread(jnp.concatenate(tot, axis=1))
            d_alast = k.head_sums(jnp.broadcast_to(_colsum(x_dxs), (8, W)))[0:1, :] + tot_f
            dacum = q_f + k.head_sums(dY * (y_off_raw * t["eac"]) - x_dxs) + jnp.where(last, d_alast, 0.0)
            dx_dt = k.head_sums(dX * x)
            d_skip = _colsum(k.head_sums(dY * x))
            for h in range(H):
                hsl = slice(gi * W + h * P, gi * W + (h + 1) * P)
                dstate[hsl, :] = t["ea_last"][h] * dS1[h * P:(h + 1) * P, :] + dS_c[h * P:(h + 1) * P, :]
            dc_s2 = _dot(dCB, Bm)
            db_s2 = _dot(dCBt, Cm)
            yield
            da = _sum_by(k.upper, dacum)
            yield
            ddt_raw = (da * t["A"] + dx_dt) * _sigmoid(t["dtr"])
            dact_ref[:, gsl] = (dX * t["dt"] + t["D"] * dY).astype(BF16)
            dact_ref[:, CC0 + gi * N:CC0 + (gi + 1) * N] = (dC_s + dc_s2).astype(BF16)
            dact_ref[:, CB0 + gi * N:CB0 + (gi + 1) * N] = (dB_s + db_s2).astype(BF16)
            ddt_ref[gi] = k.gather4(ddt_raw)
            dp_ref[gi] += k.gather4(jnp.concatenate([_colsum(ddt_raw), _colsum(da * t["dt"]) * t["A"], d_skip],
                                                    axis=0))

        _lockstep(group(gi) for gi in range(G))

    rev = lambda c: nc - 1 - c
    x_spec, b_spec, c_spec, z_spec, dt_spec, p_spec, nw_spec, s_spec = _ssd_specs(L, rev)
    y_spec = pl.BlockSpec((L, G * W), lambda g, c: (rev(c), g))
    half = d_conv.shape[1] // 2
    return pl.pallas_call(
        body, name=name, grid=(SSM_GROUPS // G, nc),
        in_specs=[y_spec, x_spec, b_spec, c_spec, z_spec, dt_spec, p_spec, nw_spec, y_spec, s_spec, D_PROJ_ANY,
                  D_PROJ_ANY],
        out_specs=(pl.BlockSpec((L, half), lambda g, c: (rev(c), 0)), z_spec, dt_spec, p_spec, nw_spec),
        out_shape=(jax.ShapeDtypeStruct(d_conv.shape, BF16), jax.ShapeDtypeStruct(d_proj.shape, BF16),
                   jax.ShapeDtypeStruct((SSM_GROUPS, S, H), F32), jax.ShapeDtypeStruct((SSM_GROUPS, 3, H), F32),
                   jax.ShapeDtypeStruct((SSM_GROUPS, 1, W), F32)),
        scratch_shapes=[pltpu.VMEM((G * W, N), F32)], input_output_aliases={10: 1, 11: 0},
        compiler_params=_params("parallel", "arbitrary"))(dyn, conv, conv, conv, proj, dt_raw, pvec, nw, y_ssd, states,
                                                          d_proj, d_conv)


def _unit_lower_inverse(A, ii, jj):
    eye = (ii == jj).astype(F32)
    same = (ii // GDN_INV_BLOCK) == (jj // GDN_INV_BLOCK)
    Ad = jnp.where(same, A, 0.0)
    Ao = A - Ad
    P2 = _dot3(Ad, Ad)
    yield
    P4, X = _dot(P2, P2), _dot3(eye - Ad, eye + P2)
    yield
    P8, X = _dot(P4, P4), X + _dot2(X, P4)
    yield
    X = X + _dot2(X, P8)
    yield
    Bm = _dot3(X, Ao)
    yield
    B2 = _dot3(Bm, Bm)
    yield
    Y = (eye - Bm) + B2 - _dot2(Bm, B2)
    yield
    T = _dot3(Y, X)
    yield
    return T


def _gdn_specs(L, order):
    G = GDN_QK_PER_STEP
    Hd, W = G * GDN_HEAD, G * GDN_V_PER_QK * GDN_HEAD
    q_spec = pl.BlockSpec((L, Hd), lambda h, c: (order(c), (C_QKV - C_XBC) // Hd + h))
    k_spec = pl.BlockSpec((L, Hd), lambda h, c: (order(c), (C_QKV - C_XBC + 1024) // Hd + h))
    v_spec = pl.BlockSpec((L, W), lambda h, c: (order(c), (C_QKV - C_XBC + 2048) // W + h))
    z_spec = pl.BlockSpec((L, W), lambda h, c: (order(c), C_ZG // W + h))
    ba_spec = pl.BlockSpec((G, L, GDN_V_PER_QK), lambda h, c: (h, order(c), 0))
    p_spec = pl.BlockSpec((G, 2, GDN_V_PER_QK), lambda h, c: (h, 0, 0))
    nw_spec = pl.BlockSpec((1, GDN_HEAD), lambda h, c: (0, 0))
    s_spec = pl.BlockSpec((G, 1, GDN_V_PER_QK * GDN_HEAD, GDN_HEAD), lambda h, c: (h, order(c), 0, 0))
    t_spec = pl.BlockSpec((G * GDN_V_PER_QK, 1, L, L), lambda h, c: (h, order(c), 0, 0))
    return q_spec, k_spec, v_spec, z_spec, ba_spec, p_spec, nw_spec, s_spec, t_spec


def _gdn_qk(qa, ka):
    rq = lax.rsqrt(_rowsum(qa * qa) + EPS)
    rk = lax.rsqrt(_rowsum(ka * ka) + EPS)
    q, k = qa * rq * (GDN_HEAD ** -0.5), ka * rk
    return dict(qa=qa, rq=rq, rk=rk, q=q, k=k, QK=_dot(q, k, _NT))


def _gdn_gates(qk, b_col, a_col, p, j, ii, jj):
    L = qk["q"].shape[0]
    rq, rk, q, k = qk["rq"], qk["rk"], qk["q"], qk["k"]
    sp_in = a_col + p[0:1, j:j + 1]
    neg_ea = -jnp.exp(p[1:2, j:j + 1])
    g = neg_ea * _softplus(sp_in)
    gcum, gcum_row = _cumsum_forms(g, ii, jj)
    beta = _sigmoid(b_col)
    yield
    Dm = jnp.exp(jnp.where(ii >= jj, gcum - gcum_row, NEG_INF))
    eg = jnp.exp(gcum)
    g_last = gcum[L - 1:L, :]
    wdec = jnp.exp(g_last - gcum)
    return dict(rq=rq, rk=rk, q=q, k=k, beta=beta, sp_in=sp_in, neg_ea=neg_ea, g=g, Dm=Dm, kbeta=k * beta, eg=eg,
                g_last=g_last, wdec=wdec, kdec=k * wdec)


def _gdn_fwd(conv, proj, b_raw, a_raw, pvec, nw, name):
    S = conv.shape[0]
    L, Hd, J, G = GDN_CHUNK, GDN_HEAD, GDN_V_PER_QK, GDN_QK_PER_STEP
    W = J * Hd
    nc = S // L

    def body(q_ref, k_ref, v_ref, z_ref, b_ref, a_ref, p_ref, nw_ref, o_ref, on_ref, s0_ref, t_ref, state):
        c = pl.program_id(1)

        @pl.when(c == 0)
        def _():
            state[...] = jnp.zeros_like(state)

        ii, jj = _iota(L, L, 0), _iota(L, L, 1)
        for hq in range(G):
            s0_ref[hq, 0] = state[hq * W:(hq + 1) * W, :]
        qks = [_gdn_qk(q_ref[:, hq * Hd:(hq + 1) * Hd].astype(F32), k_ref[:, hq * Hd:(hq + 1) * Hd].astype(F32))
               for hq in range(G)]

        def head(hq, j):
            hd = hq * J + j
            sl = slice(hd * Hd, (hd + 1) * Hd)
            t = yield from _gdn_gates(qks[hq], b_ref[hq][:, j:j + 1], a_ref[hq][:, j:j + 1], p_ref[hq], j, ii, jj)
            KK = _dot(t["kbeta"], t["k"], _NT)
            QK = qks[hq]["QK"]
            yield
            T = yield from _unit_lower_inverse(jnp.where(ii > jj, KK * t["Dm"], 0.0), ii, jj)
            t_ref[hd, 0] = T
            S0 = state[sl, :]
            U = _dot2(T, v_ref[:, sl].astype(F32) * t["beta"])
            Wm = _dot2(T, t["kbeta"] * t["eg"])
            o_inter = _dot(t["q"] * t["eg"], S0)
            yield
            Vn = U - _dot(Wm, S0)
            yield
            o = o_inter + _dot(QK * t["Dm"], Vn)
            s_new = _dot(t["kdec"], Vn, _TN)
            yield
            state[sl, :] = S0 * jnp.exp(t["g_last"]) + s_new
            o_ref[:, sl] = o
            r = lax.rsqrt(jnp.mean(o * o, axis=-1, keepdims=True) + EPS)
            on_ref[:, sl] = ((o * r * nw_ref[...]) * _silu(z_ref[:, sl].astype(F32))).astype(BF16)

        _lockstep(head(hq, j) for hq in range(G) for j in range(J))

    q_spec, k_spec, v_spec, z_spec, ba_spec, p_spec, nw_spec, s_spec, t_spec = _gdn_specs(L, lambda c: c)
    o_spec = pl.BlockSpec((L, G * W), lambda h, c: (c, h))
    return pl.pallas_call(
        body, name=name, grid=(GDN_QK_HEADS // G, nc),
        in_specs=[q_spec, k_spec, v_spec, z_spec, ba_spec, ba_spec, p_spec, nw_spec],
        out_specs=(o_spec, o_spec, s_spec, t_spec),
        out_shape=(jax.ShapeDtypeStruct((S, GDN_QK_HEADS * W), F32), jax.ShapeDtypeStruct((S, GDN_QK_HEADS * W), BF16),
                   jax.ShapeDtypeStruct((GDN_QK_HEADS, nc, W, Hd), F32),
                   jax.ShapeDtypeStruct((GDN_QK_HEADS * J, nc, L, L), F32)),
        scratch_shapes=[pltpu.VMEM((G * W, Hd), F32)],
        compiler_params=_params("parallel", "arbitrary"))(conv, conv, conv, proj, b_raw, a_raw, pvec, nw)


def _gdn_bwd(don, conv, proj, b_raw, a_raw, pvec, nw, o_pre, states, t_inv, d_proj, d_conv, name):
    S = conv.shape[0]
    L, Hd, J, G = GDN_CHUNK, GDN_HEAD, GDN_V_PER_QK, GDN_QK_PER_STEP
    W = J * Hd
    nc = S // L
    assert G == GDN_QK_HEADS
    CK0, CV0 = GDN_QK_HEADS * Hd, 2 * GDN_QK_HEADS * Hd

    def body(don_ref, q_ref, k_ref, v_ref, z_ref, b_ref, a_ref, p_ref, nw_ref, o_ref, s0_ref, t_ref, _, _2,
             dact_ref, dz_ref, db_ref, da_ref, dp_ref, dnw_ref, dstate):
        c = pl.program_id(1)

        @pl.when(c == 0)
        def _():
            dstate[...] = jnp.zeros_like(dstate)
            dp_ref[...] = jnp.zeros_like(dp_ref)
            dnw_ref[...] = jnp.zeros_like(dnw_ref)

        ii, jj = _iota(L, L, 0), _iota(L, L, 1)
        last = (_iota(L, 1, 0) == L - 1)
        res = {}
        qks = [_gdn_qk(q_ref[:, hq * Hd:(hq + 1) * Hd].astype(F32), k_ref[:, hq * Hd:(hq + 1) * Hd].astype(F32))
               for hq in range(G)]

        def head(hq, j):
            hd = hq * J + j
            sl = slice(hd * Hd, (hd + 1) * Hd)
            qa = qks[hq]["qa"]
            t = yield from _gdn_gates(qks[hq], b_ref[hq][:, j:j + 1], a_ref[hq][:, j:j + 1], p_ref[hq], j, ii, jj)
            q, k, beta, eg, Dm, kbeta, kdec = (t[nm] for nm in ("q", "k", "beta", "eg", "Dm", "kbeta", "kdec"))
            T = t_ref[hd, 0]
            v, z, o = v_ref[:, sl].astype(F32), z_ref[:, sl].astype(F32), o_ref[:, sl]
            S0, dS1 = s0_ref[hq, 0, j * Hd:(j + 1) * Hd, :], dstate[sl, :]
            sz = _silu(z)
            r = lax.rsqrt(jnp.mean(o * o, axis=-1, keepdims=True) + EPS)
            oh = o * r
            d_on = don_ref[:, sl]
            dz_ref[:, sl] = (d_on * (oh * nw_ref[...]) * _dsilu(z)).astype(BF16)
            dn = d_on * sz
            dnw_part = _colsum(dn * oh)
            doh = dn * nw_ref[...]
            dO = r * (doh - oh * jnp.mean(doh * oh, axis=-1, keepdims=True))
            Rw = kbeta * eg
            qe = q * eg
            U = _dot2(T, v * beta)
            Wm = _dot2(T, Rw)
            KK = _dot(kbeta, k, _NT)
            QK = qks[hq]["QK"]
            o_inter = _dot(qe, S0)
            dq_s = _dot(dO, S0, _NT)
            dS_q = _dot(qe, dO, _TN)
            yield
            Am = jnp.where(ii > jj, KK * Dm, 0.0)
            Pm = QK * Dm
            Vn = U - _dot(Wm, S0)
            dVn_s = _dot(kdec, dS1)
            yield
            dVn = _dot(Pm, dO, _TN) + dVn_s
            dP = _dot(dO, Vn, _NT)
            dKd = _dot(Vn, dS1, _NT)
            yield
            dQK = dP * Dm
            dq = _dot(dQK, k) + dq_s * eg
            dk = _dot(dQK, q, _TN) + dKd * t["wdec"]
            dstate[sl, :] = jnp.exp(t["g_last"]) * dS1 + dS_q - _dot(Wm, dVn, _TN)
            dW = -_dot(dVn, S0, _NT)
            dRu = _dot2(T, dVn, _TN)
            yield
            dRw = _dot2(T, dW, _TN)
            dA_u = _dot(dRu, U, _NT)
            yield
            dA = jnp.where(ii > jj, -(dA_u + _dot(dRw, Wm, _NT)), 0.0)
            yield
            dKK = dA * Dm
            dkbeta = _dot(dKK, k) + dRw * eg
            dk = dk + _dot(dKK, kbeta, _TN)
            yield
            dk = dk + dkbeta * beta
            dbeta = _rowsum(dkbeta * k + dRu * v)
            dact_ref[:, CV0 + hd * Hd:CV0 + (hd + 1) * Hd] = (dRu * beta).astype(BF16)
            Q = dA * Am + dP * Pm
            kd = dKd * kdec
            d_glast = _total(kd) + jnp.exp(t["g_last"]) * _total(dS1 * S0)
            q_sums = _row_col_sums(Q)
            rest = _rowsum(dRw * Rw + dO * o_inter - kd) + jnp.where(last, d_glast, 0.0)
            yield
            dg = _rev_cumsum_col(q_sums + rest, ii, jj)
            yield
            da_raw = dg * t["neg_ea"] * _sigmoid(t["sp_in"])
            res[hq, j] = dict(dq=dq, dk=dk, db=dbeta * beta * (1.0 - beta), da=da_raw, d_bias=_colsum(da_raw),
                              d_alog=_colsum(dg * t["g"]), dnw=dnw_part, rq=t["rq"], rk=t["rk"], k=k, qh=qa * t["rq"])

        _lockstep(head(hq, j) for hq in range(G) for j in range(J))
        for hq in range(G):
            parts = [res[hq, j] for j in range(J)]
            hsl = slice(hq * Hd, (hq + 1) * Hd)
            p0 = parts[0]
            dqh = sum(pt["dq"] for pt in parts) * (GDN_HEAD ** -0.5)
            dkn = sum(pt["dk"] for pt in parts)
            dact_ref[:, hsl] = (p0["rq"] * (dqh - p0["qh"] * _rowsum(dqh * p0["qh"]))).astype(BF16)
            dact_ref[:, CK0 + hq * Hd:CK0 + (hq + 1) * Hd] = (
                p0["rk"] * (dkn - p0["k"] * _rowsum(dkn * p0["k"]))).astype(BF16)
            db_ref[hq] = jnp.concatenate([pt["db"] for pt in parts], axis=1)
            da_ref[hq] = jnp.concatenate([pt["da"] for pt in parts], axis=1)
            dp_ref[hq] += jnp.concatenate([jnp.concatenate([pt["d_bias"] for pt in parts], axis=1),
                                           jnp.concatenate([pt["d_alog"] for pt in parts], axis=1)], axis=0)
            dnw_ref[hq] += sum(pt["dnw"] for pt in parts)

    rev = lambda c: nc - 1 - c
    q_spec, k_spec, v_spec, z_spec, ba_spec, p_spec, nw_spec, s_spec, t_spec = _gdn_specs(L, rev)
    o_spec = pl.BlockSpec((L, G * W), lambda h, c: (rev(c), h))
    dnw_spec = pl.BlockSpec((G, 1, Hd), lambda h, c: (h, 0, 0))
    half = d_conv.shape[1] // 2
    return pl.pallas_call(
        body, name=name, grid=(GDN_QK_HEADS // G, nc),
        in_specs=[o_spec, q_spec, k_spec, v_spec, z_spec, ba_spec, ba_spec, p_spec, nw_spec, o_spec, s_spec, t_spec,
                  D_PROJ_ANY, D_PROJ_ANY],
        out_specs=(pl.BlockSpec((L, half), lambda h, c: (rev(c), 1)), z_spec, ba_spec, ba_spec, p_spec, dnw_spec),
        out_shape=(jax.ShapeDtypeStruct(d_conv.shape, BF16), jax.ShapeDtypeStruct(d_proj.shape, BF16),
                   jax.ShapeDtypeStruct((GDN_QK_HEADS, S, J), F32), jax.ShapeDtypeStruct((GDN_QK_HEADS, S, J), F32),
                   jax.ShapeDtypeStruct((GDN_QK_HEADS, 2, J), F32), jax.ShapeDtypeStruct((GDN_QK_HEADS, 1, Hd), F32)),
        scratch_shapes=[pltpu.VMEM((G * W, Hd), F32)], input_output_aliases={12: 1, 13: 0},
        compiler_params=_params("parallel", "arbitrary"))(don, conv, conv, conv, proj, b_raw, a_raw, pvec, nw, o_pre,
                                                          states, t_inv, d_proj, d_conv)


def _ada_fwd(c_all, w_loc, b_loc, name):
    n = w_loc.shape[1]

    def body(c_ref, w_ref, b_ref, o_ref):
        o_ref[...] = _dot3(_silu(c_ref[...]), w_ref[...]) + b_ref[...]

    return pl.pallas_call(body, name=name, out_shape=jax.ShapeDtypeStruct((N_DEV, n), F32),
                          compiler_params=pltpu.CompilerParams(vmem_limit_bytes=VMEM_LIMIT))(c_all, w_loc, b_loc)


def _ada_bwd(c_all_t, dmod_cols, name):
    Dm, n = c_all_t.shape[0], dmod_cols.shape[1]

    def body(c_ref, d_ref, o_ref):
        ca = _silu(c_ref[...])
        acc = ca[:, 0:1] * d_ref[0:1, :]
        for i in range(1, N_DEV):
            acc = acc + ca[:, i:i + 1] * d_ref[i:i + 1, :]
        o_ref[...] = acc

    return pl.pallas_call(body, name=name, out_shape=jax.ShapeDtypeStruct((Dm, n), F32),
                          compiler_params=pltpu.CompilerParams(vmem_limit_bytes=VMEM_LIMIT))(c_all_t, dmod_cols)


ADAM_BLOCK_BYTES = 12 * 1024 * 1024


def _adam(contrib, w, m, v, name):
    n, R, C = contrib.shape
    tr = R
    while tr % 16 == 0 and (n + 7) * tr * C * 4 > ADAM_BLOCK_BYTES:
        tr //= 2

    def body(c_ref, w_ref, m_ref, v_ref, g_ref, d_ref, nm_ref, nv_ref):
        g = c_ref[0].astype(F32)
        for i in range(1, n):
            g = g + c_ref[i].astype(F32)
        nm = ADAM_B1 * m_ref[...] + (1.0 - ADAM_B1) * g
        nv = ADAM_B2 * v_ref[...] + (1.0 - ADAM_B2) * (g * g)
        m_hat = nm / (1.0 - ADAM_B1 ** ADAM_STEP)
        v_hat = nv / (1.0 - ADAM_B2 ** ADAM_STEP)
        g_ref[...] = g
        d_ref[...] = -ADAM_LR * (m_hat / (jnp.sqrt(v_hat) + ADAM_EPS) + ADAM_WD * w_ref[...])
        nm_ref[...] = nm
        nv_ref[...] = nv

    spec = pl.BlockSpec((tr, C), lambda i: (i, 0))
    shp = jax.ShapeDtypeStruct((R, C), F32)
    return pl.pallas_call(
        body, name=name, grid=(R // tr,), in_specs=[pl.BlockSpec((n, tr, C), lambda i: (0, i, 0)), spec, spec, spec],
        out_specs=(spec,) * 4, out_shape=(shp,) * 4, compiler_params=_params("parallel"))(contrib, w, m, v)


def _exchange(arrays, modes, name, chips=False):
    n = len(arrays)
    out_shape = tuple(jax.ShapeDtypeStruct((N_DEV,) + a.shape if md == "gather" else a.shape, a.dtype)
                      for a, md in zip(arrays, modes))

    def body(*refs):
        ins, outs = refs[:n], refs[n:2 * n]
        send_sems, recv_sems, loc_sems = refs[2 * n:]
        me, peers = _peer_table(chips)

        def src(k, slot):
            return ins[k] if modes[k] == "gather" else ins[k].at[slot]

        def remote(k, m, to_slot, land_slot):
            return pltpu.make_async_remote_copy(
                src_ref=src(k, to_slot), dst_ref=outs[k].at[land_slot], send_sem=send_sems.at[k, m],
                recv_sem=recv_sems.at[k, m], device_id=peers[m][0], device_id_type=pl.DeviceIdType.MESH)

        local = [pltpu.make_async_copy(src(k, me), outs[k].at[me], loc_sems.at[k]) for k in range(n)]
        for cp in local:
            cp.start()
        sends = [remote(k, m, peers[m][1], me) for m in range(len(peers)) for k in range(n)]
        for cp in sends:
            cp.start()
        for m in range(len(peers)):
            for k in range(n):
                remote(k, m, peers[m][1], peers[m][1]).wait_recv()
        for cp in sends:
            cp.wait_send()
        for cp in local:
            cp.wait()

    any_spec = pl.BlockSpec(memory_space=pl.ANY)
    return pl.pallas_call(
        body, name=name, in_specs=[any_spec] * n, out_specs=(any_spec,) * n, out_shape=out_shape,
        scratch_shapes=[pltpu.SemaphoreType.DMA((n, N_DEV - 1)), pltpu.SemaphoreType.DMA((n, N_DEV - 1)),
                        pltpu.SemaphoreType.DMA((n,))])(*arrays)


def _gather_two_level(arrays, name):
    n = len(arrays)
    out_shape = tuple(jax.ShapeDtypeStruct((N_DEV,) + a.shape, a.dtype) for a in arrays)

    def body(*refs):
        ins, outs = refs[:n], refs[n:2 * n]
        send_sems, recv_sems, loc_sems = refs[2 * n:]
        ix, iy, ic = lax.axis_index("x"), lax.axis_index("y"), lax.axis_index("c")
        lin = lambda px, py, pc: 4 * px + 2 * py + pc
        me, sib = lin(ix, iy, ic), (ix, iy, 1 - ic)
        chips = [(1 - ix, iy), (ix, 1 - iy), (1 - ix, 1 - iy)]

        def copy(k, s, block, to, src=None):
            return pltpu.make_async_remote_copy(
                src_ref=outs[k].at[block] if src is None else src, dst_ref=outs[k].at[block],
                send_sem=send_sems.at[k, s], recv_sem=recv_sems.at[k, s], device_id=to,
                device_id_type=pl.DeviceIdType.MESH)

        local = [pltpu.make_async_copy(ins[k], outs[k].at[me], loc_sems.at[k]) for k in range(n)]
        for cp in local:
            cp.start()
        first = [copy(k, 1 + j, me, (cx, cy, ic), src=ins[k]) for j, (cx, cy) in enumerate(chips) for k in range(n)]
        first += [copy(k, 0, me, sib, src=ins[k]) for k in range(n)]
        for cp in first:
            cp.start()
        passed = []
        for j, (cx, cy) in enumerate(chips):
            for k in range(n):
                copy(k, 1 + j, lin(cx, cy, ic), sib).wait_recv()
                passed.append(copy(k, 4 + j, lin(cx, cy, ic), sib))
                passed[-1].start()
        for k in range(n):
            copy(k, 0, lin(*sib), sib).wait_recv()
            for j, (cx, cy) in enumerate(chips):
                copy(k, 4 + j, lin(cx, cy, 1 - ic), sib).wait_recv()
        for cp in first + passed:
            cp.wait_send()
        for cp in local:
            cp.wait()

    any_spec = pl.BlockSpec(memory_space=pl.ANY)
    return pl.pallas_call(
        body, name=name, in_specs=[any_spec] * n, out_specs=(any_spec,) * n, out_shape=out_shape,
        scratch_shapes=[pltpu.SemaphoreType.DMA((n, N_DEV - 1)), pltpu.SemaphoreType.DMA((n, N_DEV - 1)),
                        pltpu.SemaphoreType.DMA((n,))])(*arrays)


def _peer_table(chips=False):
    ix, iy, ic = lax.axis_index("x"), lax.axis_index("y"), lax.axis_index("c")
    peers = []
    for m in ((2, 4, 6) if chips else range(1, N_DEV)):
        px = 1 - ix if m & 4 else ix
        py = 1 - iy if m & 2 else iy
        pc = 1 - ic if m & 1 else ic
        peers.append(((px, py, pc), 2 * px + py if chips else 4 * px + 2 * py + pc))
    return (2 * ix + iy if chips else 4 * ix + 2 * iy + ic), peers


def _exchange_start(arrays, modes, after, name, chips=False):
    n = len(arrays)
    land_shapes = [(N_DEV,) + a.shape if md == "gather" else a.shape for a, md in zip(arrays, modes)]

    def body(*refs):
        ins, lands = refs[:n], refs[n:2 * n]
        send_sems, recv_sems = refs[2 * n + 1], refs[2 * n + 2]
        token = refs[-1]
        me, peers = _peer_table(chips)

        def src(k, slot):
            return ins[k] if modes[k] == "gather" else ins[k].at[slot]

        for peer, slot in peers:
            for k in range(n):
                pltpu.make_async_remote_copy(
                    src_ref=src(k, slot), dst_ref=lands[k].at[me], send_sem=send_sems, recv_sem=recv_sems,
                    device_id=peer, device_id_type=pl.DeviceIdType.MESH).start()
        token[...] = jnp.zeros_like(token)

    hbm = pl.BlockSpec(memory_space=pltpu.HBM)
    sem = pl.BlockSpec(memory_space=pltpu.SEMAPHORE)
    sem_shape = pltpu.SemaphoreType.DMA(())
    operands = [pltpu.with_memory_space_constraint(a, pltpu.HBM) for a in arrays]
    operands += [pltpu.with_memory_space_constraint(lax.empty(s, a.dtype), pltpu.HBM)
                 for s, a in zip(land_shapes, arrays)]
    out = pl.pallas_call(
        body, name=name,
        out_shape=(sem_shape, sem_shape) + tuple(pltpu.HBM(a.shape, a.dtype) for a in arrays)
        + tuple(pltpu.HBM(s, a.dtype) for s, a in zip(land_shapes, arrays)) + (jax.ShapeDtypeStruct((8, 128), F32),),
        in_specs=[hbm] * (2 * n) + [pl.BlockSpec(memory_space=pl.ANY)],
        out_specs=(sem, sem) + (hbm,) * (2 * n) + (pl.BlockSpec(memory_space=pltpu.VMEM),),
        input_output_aliases={i: 2 + i for i in range(2 * n)},
        compiler_params=pltpu.CompilerParams(has_side_effects=pltpu.SideEffectType.DATAFLOW_SIDE_EFFECTING))(
            *operands, after)
    return out[0], out[1], out[2:2 + n], out[2 + n:2 + 2 * n], out[-1]


def _exchange_wait(started, modes, after, name, chips=False):
    send_sems, recv_sems, sent, lands, _ = started
    n = len(sent)

    def body(*refs):
        ins, zones = refs[:n], refs[n:2 * n]
        send_ref, recv_ref = refs[2 * n], refs[2 * n + 1]
        _, peers = _peer_table(chips)

        def src(k, slot):
            return ins[k] if modes[k] == "gather" else ins[k].at[slot]

        for peer, slot in peers:
            for k in range(n):
                cp = pltpu.make_async_remote_copy(
                    src_ref=src(k, slot), dst_ref=zones[k].at[slot], send_sem=send_ref, recv_sem=recv_ref,
                    device_id=peer, device_id_type=pl.DeviceIdType.MESH)
                cp.wait_send()
                cp.wait_recv()

    hbm = pl.BlockSpec(memory_space=pltpu.HBM)
    sem = pl.BlockSpec(memory_space=pltpu.SEMAPHORE)
    out = pl.pallas_call(
        body, name=name,
        out_shape=tuple(pltpu.HBM(a.shape, a.dtype) for a in sent) + tuple(pltpu.HBM(a.shape, a.dtype) for a in lands),
        in_specs=[hbm] * (2 * n) + [sem, sem, pl.BlockSpec(memory_space=pl.ANY)], out_specs=(hbm,) * (2 * n),
        input_output_aliases={i: i for i in range(2 * n)},
        compiler_params=pltpu.CompilerParams(has_side_effects=pltpu.SideEffectType.DATAFLOW_SIDE_EFFECTING))(
            *sent, *lands, send_sems, recv_sems, after)
    ix, iy, ic = lax.axis_index("x"), lax.axis_index("y"), lax.axis_index("c")
    me = 2 * ix + iy if chips else 4 * ix + 2 * iy + ic
    filled = []
    for k in range(n):
        own = sent[k] if modes[k] == "gather" else lax.dynamic_index_in_dim(sent[k], me, axis=0, keepdims=False)
        filled.append(lax.dynamic_update_index_in_dim(out[n + k], own, me, axis=0))
    return filled


def _swap_sibling(to_c0, to_c1, name):
    def body(c0_ref, c1_ref, out_ref, send_sem, recv_sem):
        ix, iy, ic = lax.axis_index("x"), lax.axis_index("y"), lax.axis_index("c")

        def copy(src):
            return pltpu.make_async_remote_copy(src_ref=src, dst_ref=out_ref, send_sem=send_sem, recv_sem=recv_sem,
                                                device_id=(ix, iy, 1 - ic), device_id_type=pl.DeviceIdType.MESH)

        @pl.when(ic == 0)
        def _():
            copy(c1_ref).start()

        @pl.when(ic == 1)
        def _():
            copy(c0_ref).start()

        copy(c0_ref).wait()

    any_spec = pl.BlockSpec(memory_space=pl.ANY)
    return pl.pallas_call(body, name=name, in_specs=[any_spec, any_spec], out_specs=any_spec,
                          out_shape=jax.ShapeDtypeStruct(to_c0.shape, to_c0.dtype),
                          scratch_shapes=[pltpu.SemaphoreType.DMA, pltpu.SemaphoreType.DMA])(to_c0, to_c1)


def _add_pair(to_c0, to_c1, got, name):
    n, R, C = got.shape
    tr = _blk(R, 256)

    def body(c0_ref, c1_ref, got_ref, o_ref):
        ic = lax.axis_index("c")

        @pl.when(ic == 0)
        def _():
            o_ref[...] = (c0_ref[...].astype(F32) + got_ref[...].astype(F32)).astype(o_ref.dtype)

        @pl.when(ic == 1)
        def _():
            o_ref[...] = (c1_ref[...].astype(F32) + got_ref[...].astype(F32)).astype(o_ref.dtype)

    spec = pl.BlockSpec((1, tr, C), lambda i, j: (i, j, 0))
    return pl.pallas_call(body, name=name, grid=(n, R // tr), in_specs=[spec, spec, spec], out_specs=spec,
                          out_shape=jax.ShapeDtypeStruct(got.shape, got.dtype),
                          compiler_params=_params("parallel", "parallel"))(to_c0, to_c1, got)


W_IN_SPLITS = (0, 2048, 6144, 6176, 10272, 12320, 12336, 12352, 13376, 14400)
N_REPLICATED = 16640
REPLICATED = ("b_ada", "norm_mix_pre", "norm_mix_post", "ssm_conv_b", "ssm_dt_bias", "ssm_A_log", "ssm_D",
              "ssm_norm_w", "gdn_dt_bias", "gdn_A_log", "gdn_norm_w", "norm_mlp_pre", "norm_mlp_post")
WEIGHTS = ("w_ada", "b_ada", "norm_mix_pre", "norm_mix_post", "w_in", "ssm_conv_w", "ssm_conv_b", "ssm_dt_bias",
           "ssm_A_log", "ssm_D", "ssm_norm_w", "gdn_conv_w", "gdn_dt_bias", "gdn_A_log", "gdn_norm_w", "w_ssm_up",
           "w_gdn_up", "w_out", "norm_mlp_pre", "norm_mlp_post", "w_mlp_up", "w_mlp_down")


def _cols_of_shards(g, a, b):
    width, pieces = g.shape[2], []
    while a < b:
        i = a // width
        hi = min(b, (i + 1) * width)
        pieces.append(g[i][:, a - i * width:hi - i * width])
        a = hi
    return pieces


ORIG_SEGMENTS = ((0, 6144, "main", 0), (6144, 6176, "small", 0), (6176, 12320, "main", 6144),
                 (12320, 12352, "small", 32), (12352, 14400, "main", 12288))


def _orig_cols(main_cols, small_cols, a, b):
    pieces = []
    for s0, s1, which, off in ORIG_SEGMENTS:
        lo, hi = max(a, s0), min(b, s1)
        if lo < hi:
            pieces.append((main_cols if which == "main" else small_cols)[:, off + lo - s0:off + hi - s0])
    return jnp.concatenate(pieces, axis=1)


def _by_cols(t):
    return t.transpose(1, 0, 2).reshape(t.shape[1], N_DEV * t.shape[2])


def _to_col_shards(t):
    R, C8 = t.shape
    return t.reshape(R, N_DEV, C8 // N_DEV).transpose(1, 0, 2)


def _heads_first(t, groups):
    S = t.shape[0]
    return t.reshape(S, groups, t.shape[1] // groups).transpose(1, 0, 2)


def _heads_last(t):
    return t.transpose(1, 0, 2).reshape(t.shape[1], t.shape[0] * t.shape[2])


def kernel(x, c, w_ada, b_ada, norm_mix_pre, norm_mix_post, w_in, ssm_conv_w, ssm_conv_b, ssm_dt_bias, ssm_A_log, ssm_D, ssm_norm_w, gdn_conv_w, gdn_dt_bias, gdn_A_log, gdn_norm_w, w_ssm_up, w_gdn_up, w_out, norm_mlp_pre, norm_mlp_post, w_mlp_up, w_mlp_down, loss_target, m_w_ada, m_b_ada, m_norm_mix_pre, m_norm_mix_post, m_w_in, m_ssm_conv_w, m_ssm_conv_b, m_ssm_dt_bias, m_ssm_A_log, m_ssm_D, m_ssm_norm_w, m_gdn_conv_w, m_gdn_dt_bias, m_gdn_A_log, m_gdn_norm_w, m_w_ssm_up, m_w_gdn_up, m_w_out, m_norm_mlp_pre, m_norm_mlp_post, m_w_mlp_up, m_w_mlp_down, v_w_ada, v_b_ada, v_norm_mix_pre, v_norm_mix_post, v_w_in, v_ssm_conv_w, v_ssm_conv_b, v_ssm_dt_bias, v_ssm_A_log, v_ssm_D, v_ssm_norm_w, v_gdn_conv_w, v_gdn_dt_bias, v_gdn_A_log, v_gdn_norm_w, v_w_ssm_up, v_w_gdn_up, v_w_out, v_norm_mlp_pre, v_norm_mlp_post, v_w_mlp_up, v_w_mlp_down):
    S, Dm = x.shape[1], D_MODEL
    me = 4 * lax.axis_index("x") + 2 * lax.axis_index("y") + lax.axis_index("c")
    x2, tgt = x[0], loss_target[0]
    n_ada = w_ada.shape[2]
    given = dict(
        w_ada=(w_ada, m_w_ada, v_w_ada), b_ada=(b_ada, m_b_ada, v_b_ada),
        norm_mix_pre=(norm_mix_pre, m_norm_mix_pre, v_norm_mix_pre),
        norm_mix_post=(norm_mix_post, m_norm_mix_post, v_norm_mix_post), w_in=(w_in, m_w_in, v_w_in),
        ssm_conv_w=(ssm_conv_w, m_ssm_conv_w, v_ssm_conv_w), ssm_conv_b=(ssm_conv_b, m_ssm_conv_b, v_ssm_conv_b),
        ssm_dt_bias=(ssm_dt_bias, m_ssm_dt_bias, v_ssm_dt_bias), ssm_A_log=(ssm_A_log, m_ssm_A_log, v_ssm_A_log),
        ssm_D=(ssm_D, m_ssm_D, v_ssm_D), ssm_norm_w=(ssm_norm_w, m_ssm_norm_w, v_ssm_norm_w),
        gdn_conv_w=(gdn_conv_w, m_gdn_conv_w, v_gdn_conv_w), gdn_dt_bias=(gdn_dt_bias, m_gdn_dt_bias, v_gdn_dt_bias),
        gdn_A_log=(gdn_A_log, m_gdn_A_log, v_gdn_A_log), gdn_norm_w=(gdn_norm_w, m_gdn_norm_w, v_gdn_norm_w),
        w_ssm_up=(w_ssm_up, m_w_ssm_up, v_w_ssm_up), w_gdn_up=(w_gdn_up, m_w_gdn_up, v_w_gdn_up),
        w_out=(w_out, m_w_out, v_w_out), norm_mlp_pre=(norm_mlp_pre, m_norm_mlp_pre, v_norm_mlp_pre),
        norm_mlp_post=(norm_mlp_post, m_norm_mlp_post, v_norm_mlp_post), w_mlp_up=(w_mlp_up, m_w_mlp_up, v_w_mlp_up),
        w_mlp_down=(w_mlp_down, m_w_mlp_down, v_w_mlp_down))

    (c_all, scw, gcw, g_in) = _gather_two_level([c, ssm_conv_w[0], gdn_conv_w[0], w_in[0].astype(BF16)], "gather_w_in")
    c_all = c_all.reshape(N_DEV, Dm)
    sp = W_IN_SPLITS
    w_main = jnp.concatenate(_cols_of_shards(g_in, sp[0], sp[2]) + _cols_of_shards(g_in, sp[3], sp[5])
                             + _cols_of_shards(g_in, sp[7], sp[9]), axis=1)
    w_small = jnp.concatenate(_cols_of_shards(g_in, sp[2], sp[3]) + _cols_of_shards(g_in, sp[5], sp[7])
                              + [jnp.zeros((Dm, N_SMALL - 64), BF16)], axis=1)
    conv_w = jnp.concatenate([_by_cols(scw), _by_cols(gcw)], axis=1)
    conv_b = jnp.concatenate([ssm_conv_b, jnp.zeros_like(ssm_conv_b)], axis=1)

    b_loc = lax.dynamic_slice(b_ada, (0, me * n_ada), (1, n_ada))
    mod_part = _ada_fwd(c_all, w_ada[0], b_loc, "ada_fwd")
    (mod_rows,) = _exchange([mod_part.reshape(N_DEV, 1, n_ada)], ["a2a"], "exchange_mod")
    rest = _exchange_start([w_ssm_up[0].astype(BF16), w_gdn_up[0].astype(BF16), w_out[0].astype(BF16),
                            w_mlp_up[0].astype(BF16), w_mlp_down[0].astype(BF16)], ["gather"] * 5, mod_rows,
                           "gather_rest_start")
    mod = mod_rows.reshape(1, 6 * Dm) + rest[4][0:1, 0:1]
    sh1, sc1, g1, sh2, sc2, g2 = [mod[:, i * Dm:(i + 1) * Dm] for i in range(6)]

    h = _pre_fwd(x2, norm_mix_pre, sc1, sh1, "pre_mix")
    proj = _mm(h, w_main, S, N_MAIN, Dm, mode="nn", out_dtype=BF16, tn=MM_WIDE_N, name="proj_main")
    small = _mm(h, w_small, S, N_SMALL, Dm, mode="nn", out_dtype=F32, name="proj_small")
    conv = _conv_fwd(proj, conv_w, conv_b, "conv_fwd")
    dt_g, b_g, a_g = _heads_first(small[:, 0:32], 8), _heads_first(small[:, 32:48], 8), _heads_first(small[:, 48:64], 8)
    pv_ssm = jnp.stack([ssm_dt_bias.reshape(8, 4), ssm_A_log.reshape(8, 4), ssm_D.reshape(8, 4)], axis=1)
    nw_ssm = ssm_norm_w.reshape(8, 1, SSM_GROUP_WIDTH)
    pv_gdn = jnp.stack([gdn_dt_bias.reshape(8, 2), gdn_A_log.reshape(8, 2)], axis=1)
    y_ssd, ysn, st_ssm = _ssd_fwd(conv, proj, dt_g, pv_ssm, nw_ssm, "ssd_fwd")
    o_pre, ogn, st_gdn, t_inv = _gdn_fwd(conv, proj, b_g, a_g, pv_gdn, gdn_norm_w, "gdn_fwd")
    g_su, g_gu, g_out, g_mu, g_md = _exchange_wait(rest, ["gather"] * 5, ogn, "gather_rest_wait")
    w_su, w_gu = g_su.reshape(2 * Dm, Dm), g_gu.reshape(2 * Dm, Dm)
    w_o, w_mu, w_md = g_out.reshape(Dm, Dm), _by_cols(g_mu), g_md.reshape(4 * Dm, Dm)
    ys = _mm(ysn, w_su, S, Dm, 2 * Dm, mode="nn", out_dtype=F32, name="ssm_up")
    yg, merged = _gdn_up_merge(ogn, w_gu, ys, proj, "gdn_up_merge")
    mo, x1, h2 = _mix_out_post_pre(merged, w_o, x2, norm_mix_post, g1, norm_mlp_pre, sc2, sh2, "mix_out_post_pre")
    u, act = _mm(h2, w_mu, S, 4 * Dm, Dm, mode="nn", out_dtype=F32, epi="relu2", tn=MM_WIDE_N, name="mlp_up")
    y_mlp = _mm(act, w_md, S, Dm, 4 * Dm, mode="nn", out_dtype=F32, name="mlp_down")
    dx2, loss_loc, dy, dg2, dw_post2 = _final_fwd_bwd(x1, y_mlp, norm_mlp_post, g2, tgt, "post_mlp_loss_bwd")

    du = _mm(dy, w_md, S, 4 * Dm, Dm, mode="nt", out_dtype=BF16, epi="drelu2", extra=u, tn=MM_WIDE_N, name="mlp_down_dx")
    gw_md = _mm(act, dy, 4 * Dm, Dm, S, mode="tn", out_dtype=BF16, name="mlp_down_dw")
    dh2 = _mm(du, w_mu, S, Dm, 4 * Dm, mode="nt", out_dtype=F32, name="mlp_up_dx")
    gw_mu = _mm(h2, du, Dm, 4 * Dm, S, mode="tn", out_dtype=BF16, name="mlp_up_dw")
    mlp_x = _exchange_start([_to_col_shards(gw_mu), gw_md.reshape(N_DEV, -1, Dm)], ["a2a"] * 2, gw_md,
                            "grads_mlp_start")
    dx1, dsh2, dsc2, dw_pre2 = _pre_bwd(dh2, x1, norm_mlp_pre, sc2 + mlp_x[4][0:1, 0:1], dx2, "pre_mlp_bwd")
    dmo, dg1, dw_post1 = _post_bwd(dx1, mo, norm_mix_post, g1, "post_mix_bwd")
    gw_o = _mm(merged, dmo, Dm, Dm, S, mode="tn", out_dtype=BF16, name="mix_out_dw")
    dys, dyg, d_proj = _mix_out_dx_merge_bwd(dmo, w_o, ys, yg, proj, lax.empty((S, N_MAIN), BF16), "mix_out_dx_merge")
    dysn = _mm(dys, w_su, S, 2 * Dm, Dm, mode="nt", out_dtype=F32, tn=MM_WIDE_N, name="ssm_up_dx")
    gw_su = _mm(ysn, dys, 2 * Dm, Dm, S, mode="tn", out_dtype=BF16, name="ssm_up_dw")
    dogn = _mm(dyg, w_gu, S, 2 * Dm, Dm, mode="nt", out_dtype=F32, tn=MM_WIDE_N, name="gdn_up_dx")
    gw_gu = _mm(ogn, dyg, 2 * Dm, Dm, S, mode="tn", out_dtype=BF16, name="gdn_up_dw")
    mix_x = _exchange_start([gw_su.reshape(N_DEV, -1, Dm), gw_gu.reshape(N_DEV, -1, Dm), gw_o.reshape(N_DEV, -1, Dm)],
                            ["a2a"] * 3, gw_gu, "grads_mix_start")
    d_conv, d_proj, ddt_g, dpv_ssm, dnw_ssm = _ssd_bwd(dysn, conv, proj, dt_g, pv_ssm + mix_x[4][0, 0], nw_ssm, y_ssd,
                                                       st_ssm, d_proj, lax.empty(conv.shape, BF16), "ssd_bwd")
    d_conv, d_proj, db_g, da_g, dpv_gdn, dnw_gdn = _gdn_bwd(dogn, conv, proj, b_g, a_g, pv_gdn, gdn_norm_w, o_pre,
                                                            st_gdn, t_inv, d_proj, d_conv, "gdn_bwd")
    d_proj, dconv_w, dconv_b = _conv_bwd(d_conv, proj, conv_w, conv_b, 0, d_proj, "conv_bwd")
    d_small = jnp.concatenate([_heads_last(ddt_g), _heads_last(db_g), _heads_last(da_g),
                               jnp.zeros((S, N_SMALL - 64), F32)], axis=1).astype(BF16)
    gw_small = _mm(h, d_small, Dm, N_SMALL, S, mode="tn", out_dtype=BF16, name="proj_small_dw")
    main_cols = _mm(h, d_proj, Dm, N_MAIN, S, mode="tn", out_dtype=BF16, name="proj_main_dw")
    n_shard = w_in.shape[2]
    slabs = [_orig_cols(main_cols, gw_small, i * n_shard, (i + 1) * n_shard) for i in range(N_DEV)]
    to_c0, to_c1 = jnp.stack(slabs[0::2]), jnp.stack(slabs[1::2])
    chip_sum = _add_pair(to_c0, to_c1, _swap_sibling(to_c0, to_c1, "grads_w_in_pair"), "grads_w_in_pair_sum")
    in_x = _exchange_start([chip_sum], ["a2a"], gw_small, "grads_w_in_start", chips=True)
    dh = _mm(d_small, w_small + in_x[4][0:1, 0:1].astype(BF16), S, Dm, N_SMALL, mode="nt", out_dtype=F32,
             name="proj_small_dx")
    dh = _mm(d_proj, w_main, S, Dm, N_MAIN, mode="nt", out_dtype=F32, add=dh, name="proj_main_dx")
    dx, dsh1, dsc1, dw_pre1 = _pre_bwd(dh, x2, norm_mix_pre, sc1, dx1, "pre_mix_bwd")
    r_mu, r_md = _exchange_wait(mlp_x, ["a2a"] * 2, dx, "grads_mlp_wait")
    r_su, r_gu, r_o = _exchange_wait(mix_x, ["a2a"] * 3, dx, "grads_mix_wait")

    dconv_b = dconv_b[:, :ssm_conv_b.shape[1]]
    dmod = jnp.concatenate([dsh1, dsc1, dg1, dsh2, dsc2, dg2], axis=1)
    small_vec = jnp.concatenate(
        [dmod, dw_pre1, dw_post1, dconv_b, dpv_ssm[:, 0].reshape(1, 32), dpv_ssm[:, 1].reshape(1, 32),
         dpv_ssm[:, 2].reshape(1, 32), dnw_ssm.reshape(1, 2048), dpv_gdn[:, 0].reshape(1, 16),
         dpv_gdn[:, 1].reshape(1, 16), jnp.sum(dnw_gdn, axis=0), dw_pre2, dw_post2, dconv_w.reshape(1, -1)], axis=1)
    n_vec = small_vec.shape[1]
    small_vec = jnp.pad(small_vec, ((0, 0), (0, (-n_vec) % 1024))).reshape(-1, 1024)
    (small_all,) = _exchange([small_vec], ["gather"], "gather_small_grads")
    small_all = small_all.reshape(N_DEV, -1)
    dmod_cols = lax.dynamic_slice(small_all, (0, me * n_ada), (N_DEV, n_ada))
    gw_ada = _ada_bwd(c_all.T, dmod_cols, "ada_bwd")
    conv_all = small_all[:, N_REPLICATED:n_vec].reshape(N_DEV, CONV_K, 2 * N_DEV * 512)
    conv_contrib = jnp.concatenate(
        [lax.dynamic_slice(conv_all, (0, 0, me * 512), (N_DEV, CONV_K, 512)),
         lax.dynamic_slice(conv_all, (0, 0, N_DEV * 512 + me * 512), (N_DEV, CONV_K, 512))], axis=1)
    rep_contrib = small_all[:, :N_REPLICATED].reshape(N_DEV, N_REPLICATED // 128, 128)

    results = {}

    def adam_big(nm, contrib):
        w3 = given[nm]
        res = _adam(contrib, w3[0][0], w3[1][0], w3[2][0], "adam_" + nm)
        results[nm] = tuple(r.reshape(w3[0].shape) for r in res)

    adam_big("w_ada", gw_ada[None])
    adam_big("w_ssm_up", r_su)
    adam_big("w_gdn_up", r_gu)
    adam_big("w_out", r_o)
    adam_big("w_mlp_up", r_mu)
    adam_big("w_mlp_down", r_md)
    (r_in,) = _exchange_wait(in_x, ["a2a"], results["w_mlp_down"][0], "grads_w_in_wait", chips=True)
    adam_big("w_in", r_in)
    packed = [jnp.concatenate([given[nm][i] for nm in REPLICATED], axis=1).reshape(N_REPLICATED // 128, 128)
              for i in range(3)]
    rep_res = _adam(rep_contrib, packed[0], packed[1], packed[2], "adam_replicated")
    pos = 0
    for nm in REPLICATED:
        size = given[nm][0].shape[1]
        results[nm] = tuple(r.reshape(1, N_REPLICATED)[:, pos:pos + size] for r in rep_res)
        pos += size
    conv_wmv = [jnp.concatenate([given["ssm_conv_w"][i][0], given["gdn_conv_w"][i][0]], axis=0) for i in range(3)]
    conv_res = _adam(conv_contrib, conv_wmv[0], conv_wmv[1], conv_wmv[2], "adam_conv_w")
    results["ssm_conv_w"] = tuple(r[None, :CONV_K] for r in conv_res)
    results["gdn_conv_w"] = tuple(r[None, CONV_K:] for r in conv_res)

    loss = lax.psum(loss_loc[0, 0], ("x", "y", "c"))
    return (loss, dx[None]) + tuple(results[nm][i] for i in range(4) for nm in WEIGHTS)
```

```python
import jax
import jax.numpy as jnp
from jax import lax
from jax.experimental import pallas as pl
from jax.experimental.pallas import tpu as pltpu

F32 = jnp.float32
BF16 = jnp.bfloat16
N_DEV = 8
D_MODEL = 1024
EPS = 1e-6
CONV_K = 4
SSM_CHUNK = 128
SSM_HEAD_DIM = 64
SSM_D_STATE = 128
SSM_GROUPS = 8
SSM_HEADS_PER_GROUP = 4
SSM_GROUP_WIDTH = SSM_HEADS_PER_GROUP * SSM_HEAD_DIM
SSM_GROUPS_PER_STEP = 8
GDN_CHUNK = 64
GDN_HEAD = 128
GDN_QK_HEADS = 8
GDN_V_PER_QK = 2
GDN_QK_PER_STEP = 8
GDN_INV_BLOCK = 16
C_ZS, C_XBC, C_QKV, C_ZG, C_GS, C_GG, N_MAIN = 0, 2048, 6144, 10240, 12288, 13312, 14336
N_SMALL = 128
ADAM_LR, ADAM_B1, ADAM_B2, ADAM_EPS, ADAM_WD, ADAM_STEP = 0.001, 0.9, 0.999, 1e-08, 0.01, 10
VMEM_LIMIT = 56 * 1024 * 1024
MM_WHOLE_K = 4096
MM_SPLIT_K = 2048
MM_WIDE_N = 2048
NEG_INF = float("-inf")

_NT = (((1,), (1,)), ((), ()))
_NN = (((1,), (0,)), ((), ()))
_TN = (((0,), (0,)), ((), ()))


def _params(*sem):
    return pltpu.CompilerParams(dimension_semantics=sem, vmem_limit_bytes=VMEM_LIMIT)


def _dot(a, b, dims=_NN):
    return lax.dot_general(a.astype(BF16), b.astype(BF16), dims, preferred_element_type=F32)


def _split(a):
    hi = a.astype(BF16)
    return hi, (a - hi.astype(F32)).astype(BF16)


def _dot3(a, b, dims=_NN):
    ah, al = _split(a)
    bh, bl = _split(b)
    d = lambda u, v: lax.dot_general(u, v, dims, preferred_element_type=F32)
    return d(ah, bh) + (d(ah, bl) + d(al, bh))


def _dot2(a, b, dims=_NN):
    ah, al = _split(a)
    bb = b.astype(BF16)
    d = lambda u: lax.dot_general(u, bb, dims, preferred_element_type=F32)
    return d(ah) + d(al)


def _sigmoid(x):
    return 0.5 * jnp.tanh(0.5 * x) + 0.5


def _silu(x):
    return x * _sigmoid(x)


def _dsilu(x):
    s = _sigmoid(x)
    return s * (1.0 + x * (1.0 - s))


def _softplus(x):
    return jnp.maximum(x, 0.0) + jnp.log1p(jnp.exp(-jnp.abs(x)))


def _iota(n, m, d):
    return lax.broadcasted_iota(jnp.int32, (n, m), d)


def _rowsum(x):
    return jnp.sum(x, axis=1, keepdims=True)


def _colsum(x):
    return jnp.sum(x, axis=0, keepdims=True)


def _total(x):
    return _rowsum(_colsum(x))


MXU_LANES = 128


def _parts(x, n):
    out = []
    for _ in range(n):
        p = x.astype(BF16)
        out.append(p)
        x = x - p.astype(F32)
    return out


def _sum_by(m01, x, dims=_NN, n=3):
    return sum(lax.dot_general(m01, p, dims, preferred_element_type=F32) for p in _parts(x, n))


def _row_col_sums(q):
    ones = jnp.ones((q.shape[0], MXU_LANES), BF16)
    acc = 0.0
    for p in _parts(q, 2):
        acc = acc + (lax.dot_general(p, ones, _NN, preferred_element_type=F32)
                     - lax.dot_general(p, ones, _TN, preferred_element_type=F32))
    return acc[:, 0:1]


def _cumsum_forms(col, ii, jj):
    lower = jnp.where(ii >= jj, 1.0, 0.0).astype(BF16)
    cum_col = _sum_by(lower, jnp.broadcast_to(col, (col.shape[0], MXU_LANES)))[:, 0:1]
    cum_row = _colsum(jnp.where(ii <= jj, col, 0.0))
    return cum_col, cum_row


def _rev_cumsum_col(col, ii, jj):
    upper = jnp.where(ii <= jj, 1.0, 0.0).astype(BF16)
    return _sum_by(upper, jnp.broadcast_to(col, (col.shape[0], MXU_LANES)))[:, 0:1]


def _blk(dim, pref):
    return pref if dim % pref == 0 else dim


def _lockstep(gens):
    gens = list(gens)
    while gens:
        alive = []
        for g in gens:
            try:
                next(g)
                alive.append(g)
            except StopIteration:
                pass
        gens = alive


def _mm(a, b, M, N, K, *, mode, out_dtype, name, a_off=(0, 0), b_off=(0, 0), add=None, epi=None, extra=None,
        tm=1024, tn=1024):
    tm, tn = _blk(M, tm), _blk(N, tn)
    tk = K if K <= MM_WHOLE_K else _blk(K, MM_SPLIT_K)
    nk = K // tk
    if mode == "tn":
        a_spec = pl.BlockSpec((tk, tm), lambda i, j, k: (k + a_off[0] // tk, i + a_off[1] // tm))
        assert a_off[0] % tk == 0 and a_off[1] % tm == 0
    else:
        a_spec = pl.BlockSpec((tm, tk), lambda i, j, k: (i + a_off[0] // tm, k + a_off[1] // tk))
        assert a_off[0] % tm == 0 and a_off[1] % tk == 0
    if mode == "nt":
        b_spec = pl.BlockSpec((tn, tk), lambda i, j, k: (j + b_off[0] // tn, k + b_off[1] // tk))
        assert b_off[0] % tn == 0 and b_off[1] % tk == 0
    else:
        b_spec = pl.BlockSpec((tk, tn), lambda i, j, k: (k + b_off[0] // tk, j + b_off[1] // tn))
        assert b_off[0] % tk == 0 and b_off[1] % tn == 0
    dims = {"nn": _NN, "nt": _NT, "tn": _TN}[mode]
    o_spec = pl.BlockSpec((tm, tn), lambda i, j, k: (i, j))
    ins, in_specs = [a, b], [a_spec, b_spec]
    if add is not None:
        ins.append(add)
        in_specs.append(o_spec)
    if extra is not None:
        ins.append(extra)
        in_specs.append(o_spec)
    n_in = len(ins)
    if epi == "relu2":
        out_shape = (jax.ShapeDtypeStruct((M, N), BF16), jax.ShapeDtypeStruct((M, N), BF16))
        out_specs = (o_spec, o_spec)
    else:
        out_shape = jax.ShapeDtypeStruct((M, N), out_dtype)
        out_specs = o_spec

    def body(*refs):
        a_ref, b_ref = refs[0], refs[1]
        outs = refs[n_in:] if nk == 1 else refs[n_in:-1]

        def finish(r):
            pos = 2
            if add is not None:
                r = r + refs[pos][...]
                pos += 1
            if epi == "relu2":
                p = jnp.maximum(r, 0.0)
                outs[0][...] = p.astype(BF16)
                outs[1][...] = (p * p).astype(BF16)
            elif epi == "drelu2":
                outs[0][...] = (r * (2.0 * refs[pos][...].astype(F32))).astype(out_dtype)
            else:
                outs[0][...] = r.astype(out_dtype)

        if nk == 1:
            finish(_dot(a_ref[...], b_ref[...], dims))
            return
        acc = refs[-1]
        k = pl.program_id(2)

        @pl.when(k == 0)
        def _():
            acc[...] = jnp.zeros_like(acc)

        acc[...] += _dot(a_ref[...], b_ref[...], dims)

        @pl.when(k == nk - 1)
        def _():
            finish(acc[...])

    return pl.pallas_call(
        body, name=name, grid=(M // tm, N // tn, nk), in_specs=in_specs, out_specs=out_specs, out_shape=out_shape,
        scratch_shapes=[] if nk == 1 else [pltpu.VMEM((tm, tn), F32)],
        compiler_params=_params("parallel", "parallel", "arbitrary"))(*ins)


def _mm_rows(a, b, M, N, K, *, mode, name, extras, out_shapes, out_specs, epilogue, aliases=None, tm=512):
    tm = _blk(M, tm)
    a_spec = pl.BlockSpec((tm, K), lambda i: (i, 0))
    b_spec = pl.BlockSpec((K, N) if mode == "nn" else (N, K), lambda i: (0, 0))
    dims = _NN if mode == "nn" else _NT
    n_ex = len(extras)

    def body(a_ref, b_ref, *refs):
        epilogue(_dot(a_ref[...], b_ref[...], dims), refs[:n_ex], refs[n_ex:])

    return pl.pallas_call(
        body, name=name, grid=(M // tm,), in_specs=[a_spec, b_spec] + [sp for _, sp in extras],
        out_specs=tuple(out_specs), out_shape=tuple(out_shapes), input_output_aliases=aliases or {},
        compiler_params=_params("parallel"))(a, b, *[x for x, _ in extras])


def _row_spec(tb, d):
    return pl.BlockSpec((tb, d), lambda i: (i, 0))


def _vec_spec(d):
    return pl.BlockSpec((1, d), lambda i: (0, 0))


def _pre_fwd(x, w, sc, sh, name):
    S, Dm = x.shape
    tb = _blk(S, 512)

    def body(x_ref, w_ref, sc_ref, sh_ref, h_ref):
        xv = x_ref[...]
        r = lax.rsqrt(jnp.mean(xv * xv, axis=-1, keepdims=True) + EPS)
        h_ref[...] = ((xv * r * w_ref[...]) * (1.0 + sc_ref[...]) + sh_ref[...]).astype(BF16)

    return pl.pallas_call(
        body, name=name, grid=(S // tb,), in_specs=[_row_spec(tb, Dm)] + [_vec_spec(Dm)] * 3,
        out_specs=_row_spec(tb, Dm), out_shape=jax.ShapeDtypeStruct((S, Dm), BF16),
        compiler_params=_params("parallel"))(x, w, sc, sh)


def _final_fwd_bwd(x, y, w, g, target, name):
    S, Dm = x.shape
    tb = _blk(S, 512)
    nb = S // tb

    def body(x_ref, y_ref, w_ref, g_ref, t_ref, dx_ref, loss_ref, dy_ref, dg_ref, dw_ref, acc):
        i = pl.program_id(0)

        @pl.when(i == 0)
        def _():
            acc[...] = jnp.zeros_like(acc)
            dg_ref[...] = jnp.zeros_like(dg_ref)
            dw_ref[...] = jnp.zeros_like(dw_ref)

        yv = y_ref[...]
        r = lax.rsqrt(jnp.mean(yv * yv, axis=-1, keepdims=True) + EPS)
        yh = yv * r
        n = yh * w_ref[...]
        e = (x_ref[...] + g_ref[...] * n) - t_ref[...]
        dv = e * (1.0 / Dm)
        dx_ref[...] = dv
        acc[...] += _colsum(e * e)
        dg_ref[...] += _colsum(dv * n)
        dn = dv * g_ref[...]
        dw_ref[...] += _colsum(dn * yh)
        dyh = dn * w_ref[...]
        dy_ref[...] = (r * (dyh - yh * jnp.mean(dyh * yh, axis=-1, keepdims=True))).astype(BF16)

        @pl.when(i == nb - 1)
        def _():
            loss_ref[...] = (0.5 / Dm) * _rowsum(acc[...])

    row, vec = _row_spec(tb, Dm), _vec_spec(Dm)
    vec_shape = jax.ShapeDtypeStruct((1, Dm), F32)
    return pl.pallas_call(
        body, name=name, grid=(nb,), in_specs=[row, row, vec, vec, row],
        out_specs=(row, pl.BlockSpec((1, 1), lambda i: (0, 0)), row, vec, vec),
        out_shape=(jax.ShapeDtypeStruct((S, Dm), F32), jax.ShapeDtypeStruct((1, 1), F32),
                   jax.ShapeDtypeStruct((S, Dm), BF16), vec_shape, vec_shape),
        scratch_shapes=[pltpu.VMEM((1, Dm), F32)], compiler_params=_params("arbitrary"))(x, y, w, g, target)


def _post_bwd(dxo, y, w, g, name):
    S, Dm = y.shape
    tb = _blk(S, 512)

    def body(d_ref, y_ref, w_ref, g_ref, dy_ref, dg_ref, dw_ref):
        i = pl.program_id(0)

        @pl.when(i == 0)
        def _():
            dg_ref[...] = jnp.zeros_like(dg_ref)
            dw_ref[...] = jnp.zeros_like(dw_ref)

        yv, dv = y_ref[...], d_ref[...]
        r = lax.rsqrt(jnp.mean(yv * yv, axis=-1, keepdims=True) + EPS)
        yh = yv * r
        dg_ref[...] += _colsum(dv * (yh * w_ref[...]))
        dn = dv * g_ref[...]
        dw_ref[...] += _colsum(dn * yh)
        dyh = dn * w_ref[...]
        dy_ref[...] = (r * (dyh - yh * jnp.mean(dyh * yh, axis=-1, keepdims=True))).astype(BF16)

    return pl.pallas_call(
        body, name=name, grid=(S // tb,), in_specs=[_row_spec(tb, Dm)] * 2 + [_vec_spec(Dm)] * 2,
        out_specs=(_row_spec(tb, Dm), _vec_spec(Dm), _vec_spec(Dm)),
        out_shape=(jax.ShapeDtypeStruct((S, Dm), BF16), jax.ShapeDtypeStruct((1, Dm), F32),
                   jax.ShapeDtypeStruct((1, Dm), F32)),
        compiler_params=_params("arbitrary"))(dxo, y, w, g)


def _pre_bwd(dh, x, w, sc, dres, name):
    S, Dm = x.shape
    tb = _blk(S, 512)

    def body(dh_ref, x_ref, w_ref, sc_ref, dr_ref, dx_ref, dsh_ref, dsc_ref, dw_ref):
        i = pl.program_id(0)

        @pl.when(i == 0)
        def _():
            dsh_ref[...] = jnp.zeros_like(dsh_ref)
            dsc_ref[...] = jnp.zeros_like(dsc_ref)
            dw_ref[...] = jnp.zeros_like(dw_ref)

        xv, dv = x_ref[...], dh_ref[...]
        r = lax.rsqrt(jnp.mean(xv * xv, axis=-1, keepdims=True) + EPS)
        xh = xv * r
        one_sc = 1.0 + sc_ref[...]
        dsh_ref[...] += _colsum(dv)
        dsc_ref[...] += _colsum(dv * (xh * w_ref[...]))
        dw_ref[...] += _colsum(dv * one_sc * xh)
        dxh = dv * one_sc * w_ref[...]
        dx_ref[...] = dr_ref[...] + r * (dxh - xh * jnp.mean(dxh * xh, axis=-1, keepdims=True))

    vec = jax.ShapeDtypeStruct((1, Dm), F32)
    return pl.pallas_call(
        body, name=name, grid=(S // tb,),
        in_specs=[_row_spec(tb, Dm)] * 2 + [_vec_spec(Dm)] * 2 + [_row_spec(tb, Dm)],
        out_specs=(_row_spec(tb, Dm), _vec_spec(Dm), _vec_spec(Dm), _vec_spec(Dm)),
        out_shape=(jax.ShapeDtypeStruct((S, Dm), F32), vec, vec, vec),
        compiler_params=_params("arbitrary"))(dh, x, w, sc, dres)


D_PROJ_ANY = pl.BlockSpec(memory_space=pl.ANY)


def _gate_specs(tm, Dm):
    return (pl.BlockSpec((tm, Dm), lambda i: (i, C_GS // Dm)), pl.BlockSpec((tm, Dm), lambda i: (i, C_GG // Dm)))


def _gdn_up_merge(ogn, w_gu, ys, proj, name):
    S, K = ogn.shape
    Dm = ys.shape[1]
    tm = _blk(S, 512)
    row = _row_spec(tm, Dm)

    def epilogue(r, ex, out):
        ys_ref, gs_ref, gg_ref = ex
        out[0][...] = r
        out[1][...] = (_sigmoid(gs_ref[...].astype(F32)) * ys_ref[...]
                       + _sigmoid(gg_ref[...].astype(F32)) * r).astype(BF16)

    gs_spec, gg_spec = _gate_specs(tm, Dm)
    return _mm_rows(ogn, w_gu, S, Dm, K, mode="nn", name=name, tm=tm,
                    extras=[(ys, row), (proj, gs_spec), (proj, gg_spec)],
                    out_shapes=[jax.ShapeDtypeStruct((S, Dm), F32), jax.ShapeDtypeStruct((S, Dm), BF16)],
                    out_specs=[row, row], epilogue=epilogue)


def _mix_out_post_pre(merged, w_o, x, w_post, g, w_pre, sc, sh, name):
    S, Dm = x.shape
    tm = _blk(S, 512)
    row, vec = _row_spec(tm, Dm), _vec_spec(Dm)

    def epilogue(r, ex, out):
        x_ref, wpost_ref, g_ref, wpre_ref, sc_ref, sh_ref = ex
        out[0][...] = r
        rr = lax.rsqrt(jnp.mean(r * r, axis=-1, keepdims=True) + EPS)
        x1 = x_ref[...] + g_ref[...] * (r * rr * wpost_ref[...])
        out[1][...] = x1
        r1 = lax.rsqrt(jnp.mean(x1 * x1, axis=-1, keepdims=True) + EPS)
        out[2][...] = ((x1 * r1 * wpre_ref[...]) * (1.0 + sc_ref[...]) + sh_ref[...]).astype(BF16)

    return _mm_rows(merged, w_o, S, Dm, Dm, mode="nn", name=name, tm=tm,
                    extras=[(x, row), (w_post, vec), (g, vec), (w_pre, vec), (sc, vec), (sh, vec)],
                    out_shapes=[jax.ShapeDtypeStruct((S, Dm), F32), jax.ShapeDtypeStruct((S, Dm), F32),
                                jax.ShapeDtypeStruct((S, Dm), BF16)], out_specs=[row, row, row], epilogue=epilogue)


def _mix_out_dx_merge_bwd(dmo, w_o, ys, yg, proj, d_proj, name):
    S, Dm = ys.shape
    tm = _blk(S, 512)
    row = _row_spec(tm, Dm)

    def epilogue(d, ex, out):
        ys_ref, yg_ref, gs_ref, gg_ref, _ = ex
        ss, sg = _sigmoid(gs_ref[...].astype(F32)), _sigmoid(gg_ref[...].astype(F32))
        out[0][...] = (d * ss).astype(BF16)
        out[1][...] = (d * sg).astype(BF16)
        out[2][:, :Dm] = (d * ys_ref[...] * ss * (1.0 - ss)).astype(BF16)
        out[2][:, Dm:] = (d * yg_ref[...] * sg * (1.0 - sg)).astype(BF16)

    gs_spec, gg_spec = _gate_specs(tm, Dm)
    return _mm_rows(dmo, w_o, S, Dm, Dm, mode="nt", name=name, tm=tm,
                    extras=[(ys, row), (yg, row), (proj, gs_spec), (proj, gg_spec), (d_proj, D_PROJ_ANY)],
                    out_shapes=[jax.ShapeDtypeStruct((S, Dm), BF16), jax.ShapeDtypeStruct((S, Dm), BF16),
                                jax.ShapeDtypeStruct(d_proj.shape, BF16)],
                    out_specs=[row, row, pl.BlockSpec((tm, 2 * Dm), lambda i: (i, C_GS // (2 * Dm)))],
                    epilogue=epilogue, aliases={6: 2})


CONV_COLS = 128
CONV_BWD_ROWS = 256


def _taps_down(x):
    rows = _iota(x.shape[0], x.shape[1], 0)
    return [x] + [jnp.where(rows >= k, pltpu.roll(x, k, 0), 0.0) for k in range(1, CONV_K)]


def _conv_pre(taps, w_ref, b_ref):
    pre = taps[0] * w_ref[CONV_K - 1:CONV_K, :] + b_ref[...]
    for k in range(1, CONV_K):
        pre = pre + taps[k] * w_ref[CONV_K - 1 - k:CONV_K - k, :]
    return pre


def _conv_dx(dpre, w_ref):
    n = dpre.shape[0]
    rows = _iota(n, dpre.shape[1], 0)
    dx = dpre * w_ref[CONV_K - 1:CONV_K, :]
    for k in range(1, CONV_K):
        dx = dx + jnp.where(rows < n - k, pltpu.roll(dpre, n - k, 0), 0.0) * w_ref[CONV_K - 1 - k:CONV_K - k, :]
    return dx


def _conv_fwd(proj, w, b, name):
    S = proj.shape[0]
    n = w.shape[1]
    cb = CONV_COLS

    def body(x_ref, w_ref, b_ref, o_ref):
        o_ref[...] = _silu(_conv_pre(_taps_down(x_ref[...].astype(F32)), w_ref, b_ref)).astype(BF16)

    return pl.pallas_call(
        body, name=name, grid=(n // cb,),
        in_specs=[pl.BlockSpec((S, cb), lambda j: (0, j + C_XBC // cb)), pl.BlockSpec((CONV_K, cb), lambda j: (0, j)),
                  pl.BlockSpec((1, cb), lambda j: (0, j))],
        out_specs=pl.BlockSpec((S, cb), lambda j: (0, j)), out_shape=jax.ShapeDtypeStruct((S, n), BF16),
        compiler_params=_params("parallel"))(proj, w, b)


def _conv_bwd(dact, proj, w, b, col0, d_proj, name):
    S, n = dact.shape
    cb = CONV_COLS
    o = col0 // cb

    R, HALO = _blk(S, CONV_BWD_ROWS), 16
    n_chunks = S // R

    def body(d_ref, x_ref, w_ref, b_ref, _, dx_ref, dw_ref, db_ref):
        def chunk(r0, first, last, sums):
            lo, hi = (0 if first else HALO), (0 if last else HALO)
            start = r0 - lo if isinstance(r0, int) else pl.multiple_of(r0 - lo, HALO)
            xe = x_ref[pl.ds(start, lo + R + hi), :].astype(F32)
            rows = _iota(lo + R + hi, cb, 0)
            taps = [xe[lo:, :]]
            for k in range(1, CONV_K):
                t = pltpu.roll(xe, k, 0)
                taps.append((jnp.where(rows >= k, t, 0.0) if first else t)[lo:, :])
            dpre_e = d_ref[pl.ds(r0, R + hi), :].astype(F32) * _dsilu(_conv_pre(taps, w_ref, b_ref))
            dpre = dpre_e[0:R, :]
            db, dw = sums
            db = db + _colsum(dpre)
            dw = [dw[k] + _colsum(dpre * taps[k][0:R, :]) for k in range(CONV_K)]
            rows_e = _iota(R + hi, cb, 0)
            dx = dpre * w_ref[CONV_K - 1:CONV_K, :]
            for k in range(1, CONV_K):
                t = pltpu.roll(dpre_e, R + hi - k, 0)
                t = jnp.where(rows_e < R - k, t, 0.0) if last else t
                dx = dx + t[0:R, :] * w_ref[CONV_K - 1 - k:CONV_K - k, :]
            dx_ref[pl.ds(r0, R), :] = dx.astype(BF16)
            return db, dw

        zero = jnp.zeros((1, cb), F32)
        sums = chunk(0, True, n_chunks == 1, (zero, [zero] * CONV_K))
        if n_chunks > 2:
            def step(i, carry):
                db, dw = chunk(pl.multiple_of(i * R, R), False, False, (carry[0], list(carry[1:])))
                return (db,) + tuple(dw)
            carry = lax.fori_loop(1, n_chunks - 1, step, (sums[0],) + tuple(sums[1]))
            sums = (carry[0], list(carry[1:]))
        if n_chunks > 1:
            sums = chunk((n_chunks - 1) * R, False, True, sums)
        db_ref[...] = sums[0]
        for k in range(CONV_K):
            dw_ref[CONV_K - 1 - k:CONV_K - k, :] = sums[1][k]

    return pl.pallas_call(
        body, name=name, grid=(n // cb,),
        in_specs=[pl.BlockSpec((S, cb), lambda j: (0, j)), pl.BlockSpec((S, cb), lambda j: (0, j + o + C_XBC // cb)),
                  pl.BlockSpec((CONV_K, cb), lambda j: (0, j + o)), pl.BlockSpec((1, cb), lambda j: (0, j + o)),
                  D_PROJ_ANY],
        out_specs=(pl.BlockSpec((S, cb), lambda j: (0, j + o + C_XBC // cb)),
                   pl.BlockSpec((CONV_K, cb), lambda j: (0, j)), pl.BlockSpec((1, cb), lambda j: (0, j))),
        out_shape=(jax.ShapeDtypeStruct(d_proj.shape, BF16), jax.ShapeDtypeStruct((CONV_K, n), F32),
                   jax.ShapeDtypeStruct((1, n), F32)),
        input_output_aliases={4: 0}, compiler_params=_params("parallel"))(dact, proj, w, b, d_proj)


def _ssd_specs(L, order):
    G = SSM_GROUPS_PER_STEP
    W, N = G * SSM_GROUP_WIDTH, G * SSM_D_STATE
    x_spec = pl.BlockSpec((L, W), lambda g, c: (order(c), g))
    b_spec = pl.BlockSpec((L, N), lambda g, c: (order(c), 2048 // N + g))
    c_spec = pl.BlockSpec((L, N), lambda g, c: (order(c), 3072 // N + g))
    z_spec = pl.BlockSpec((L, W), lambda g, c: (order(c), C_ZS // W + g))
    dt_spec = pl.BlockSpec((G, L, SSM_HEADS_PER_GROUP), lambda g, c: (g, order(c), 0))
    p_spec = pl.BlockSpec((G, 3, SSM_HEADS_PER_GROUP), lambda g, c: (g, 0, 0))
    nw_spec = pl.BlockSpec((G, 1, SSM_GROUP_WIDTH), lambda g, c: (g, 0, 0))
    s_spec = pl.BlockSpec((G, 1, SSM_GROUP_WIDTH, SSM_D_STATE), lambda g, c: (g, order(c), 0, 0))
    return x_spec, b_spec, c_spec, z_spec, dt_spec, p_spec, nw_spec, s_spec


class _SsdGroup:
    def __init__(self, L):
        P, H, W = SSM_HEAD_DIM, SSM_HEADS_PER_GROUP, SSM_GROUP_WIDTH
        self.L = L
        self.ii, self.jj = _iota(L, L, 0), _iota(L, L, 1)
        self.lower = jnp.where(self.ii >= self.jj, 1.0, 0.0).astype(BF16)
        self.upper = jnp.where(self.ii <= self.jj, 1.0, 0.0).astype(BF16)
        self.lo = _iota(L, 2 * P, 1) < P
        self.lo_row = _iota(1, 2 * P, 1) < P
        bi, bj = _iota(W, W, 0), _iota(W, W, 1)
        self.block = jnp.where(bi // P == bj // P, 1.0, 0.0).astype(BF16)
        si, sj = _iota(2 * P, W, 0), _iota(2 * P, W, 1)
        self.pick = jnp.where(sj == si * P, 1.0, 0.0).astype(BF16)
        self.ones = jnp.ones((L, 2 * P), BF16)

    def spread(self, v4):
        R = v4.shape[0]
        lo = self.lo if R == self.L else self.lo_row
        b = lambda h: jnp.broadcast_to(v4[:, h:h + 1], (R, 2 * SSM_HEAD_DIM))
        return jnp.concatenate([jnp.where(lo, b(0), b(1)), jnp.where(lo, b(2), b(3))], axis=1)

    def gather4(self, v):
        return jnp.concatenate([v[:, h * SSM_HEAD_DIM:h * SSM_HEAD_DIM + 1] for h in range(SSM_HEADS_PER_GROUP)],
                               axis=1)

    def head_sums(self, z):
        return sum(lax.dot_general(p, self.block, _NN, preferred_element_type=F32) for p in _parts(z, 2))

    def pair_cols(self, full, pair):
        ps = full[:, pair * 128:(pair + 1) * 128]
        sw = pltpu.roll(ps, SSM_HEAD_DIM, 1)
        return jnp.where(self.lo, ps, sw), jnp.where(self.lo, sw, ps)

    def gates(self, dt4_raw, p):
        L = self.L
        dtr = self.spread(dt4_raw + p[0:1, :])
        dt = _softplus(dtr)
        A = self.spread(-jnp.exp(p[1:2, :]))
        acum = _sum_by(self.lower, dt * A)
        yield
        rows = _sum_by(self.pick, acum, _NT)
        yield
        a_last = acum[L - 1:L, :]
        cols = self.pair_cols(acum, 0) + self.pair_cols(acum, 1)
        decay, decay_t = [], []
        for h in range(SSM_HEADS_PER_GROUP):
            seg = cols[h] - rows[h:h + 1, :]
            decay.append(jnp.exp(jnp.where(self.ii >= self.jj, seg, NEG_INF)))
            decay_t.append(jnp.exp(jnp.where(self.jj >= self.ii, -seg, NEG_INF)))
        return dict(dtr=dtr, dt=dt, A=A, D=self.spread(p[2:3, :]), acum=acum, eac=jnp.exp(acum), a_last=a_last,
                    wdec=jnp.exp(a_last - acum), decay=decay, decay_t=decay_t,
                    ea_last=[jnp.exp(rows[h:h + 1, L - 1:L]) for h in range(SSM_HEADS_PER_GROUP)])


def _ssd_fwd(conv, proj, dt_raw, pvec, nw, name):
    S = conv.shape[0]
    L, P, N, H, W, G = SSM_CHUNK, SSM_HEAD_DIM, SSM_D_STATE, SSM_HEADS_PER_GROUP, SSM_GROUP_WIDTH, SSM_GROUPS_PER_STEP
    nc = S // L

    def body(x_ref, b_ref, c_ref, z_ref, dt_ref, p_ref, nw_ref, y_ref, yn_ref, s0_ref, state):
        c = pl.program_id(1)

        @pl.when(c == 0)
        def _():
            state[...] = jnp.zeros_like(state)

        k = _SsdGroup(L)

        def group(gi):
            gsl = slice(gi * W, (gi + 1) * W)
            Bm, Cm = b_ref[:, gi * N:(gi + 1) * N], c_ref[:, gi * N:(gi + 1) * N]
            x = x_ref[:, gsl].astype(F32)
            S0 = state[gsl, :]
            s0_ref[gi, 0] = S0
            CB = _dot(Cm, Bm, _NT)
            y_off = _dot(Cm, S0, _NT)
            t = yield from k.gates(dt_ref[gi], p_ref[gi])
            xdt = x * t["dt"]
            s_new = _dot(xdt * t["wdec"], Bm, _TN)
            y_diag = []
            for pair in range(H // 2):
                xp = xdt[:, pair * 128:(pair + 1) * 128]
                y_diag.append(jnp.where(k.lo, _dot(CB * t["decay"][2 * pair], xp),
                                        _dot(CB * t["decay"][2 * pair + 1], xp)))
            yield
            y = jnp.concatenate(y_diag, axis=1) + y_off * t["eac"]
            for h in range(H):
                hsl = slice(gi * W + h * P, gi * W + (h + 1) * P)
                state[hsl, :] = S0[h * P:(h + 1) * P, :] * t["ea_last"][h] + s_new[h * P:(h + 1) * P, :]
            y_ref[:, gsl] = y
            y2 = (y + t["D"] * x) * _silu(z_ref[:, gsl].astype(F32))
            r = lax.rsqrt(jnp.mean(y2 * y2, axis=-1, keepdims=True) + EPS)
            yn_ref[:, gsl] = (y2 * r * nw_ref[gi]).astype(BF16)

        _lockstep(group(gi) for gi in range(G))

    x_spec, b_spec, c_spec, z_spec, dt_spec, p_spec, nw_spec, s_spec = _ssd_specs(L, lambda c: c)
    y_spec = pl.BlockSpec((L, G * W), lambda g, c: (c, g))
    return pl.pallas_call(
        body, name=name, grid=(SSM_GROUPS // G, nc),
        in_specs=[x_spec, b_spec, c_spec, z_spec, dt_spec, p_spec, nw_spec],
        out_specs=(y_spec, y_spec, s_spec),
        out_shape=(jax.ShapeDtypeStruct((S, SSM_GROUPS * W), F32), jax.ShapeDtypeStruct((S, SSM_GROUPS * W), BF16),
                   jax.ShapeDtypeStruct((SSM_GROUPS, nc, W, N), F32)),
        scratch_shapes=[pltpu.VMEM((G * W, N), F32)],
        compiler_params=_params("parallel", "arbitrary"))(conv, conv, conv, proj, dt_raw, pvec, nw)


def _ssd_bwd(dyn, conv, proj, dt_raw, pvec, nw, y_ssd, states, d_proj, d_conv, name):
    S = conv.shape[0]
    L, P, N, H, W, G = SSM_CHUNK, SSM_HEAD_DIM, SSM_D_STATE, SSM_HEADS_PER_GROUP, SSM_GROUP_WIDTH, SSM_GROUPS_PER_STEP
    nc = S // L
    assert G == SSM_GROUPS
    CB0, CC0 = SSM_GROUPS * W, SSM_GROUPS * (W + N)

    def body(dyn_ref, x_ref, b_ref, c_ref, z_ref, dt_ref, p_ref, nw_ref, y_ref, s0_ref, _, _2,
             dact_ref, dz_ref, ddt_ref, dp_ref, dnw_ref, dstate):
        c = pl.program_id(1)

        @pl.when(c == 0)
        def _():
            dstate[...] = jnp.zeros_like(dstate)
            dp_ref[...] = jnp.zeros_like(dp_ref)
            dnw_ref[...] = jnp.zeros_like(dnw_ref)

        k = _SsdGroup(L)
        last = (_iota(L, 1, 0) == L - 1)

        def group(gi):
            gsl = slice(gi * W, (gi + 1) * W)
            Bm, Cm = b_ref[:, gi * N:(gi + 1) * N], c_ref[:, gi * N:(gi + 1) * N]
            x, z = x_ref[:, gsl].astype(F32), z_ref[:, gsl].astype(F32)
            S0, dS1 = s0_ref[gi, 0], dstate[gsl, :]
            CB = _dot(Cm, Bm, _NT)
            CBt = _dot(Bm, Cm, _NT)
            y_off_raw = _dot(Cm, S0, _NT)
            dXs_raw = _dot(Bm, dS1, _NT)
            t = yield from k.gates(dt_ref[gi], p_ref[gi])
            y1 = y_ref[:, gsl] + t["D"] * x
            sz = _silu(z)
            y2 = y1 * sz
            r = lax.rsqrt(jnp.mean(y2 * y2, axis=-1, keepdims=True) + EPS)
            y2h = y2 * r
            dyn_v = dyn_ref[:, gsl]
            dnw_ref[gi] += _colsum(dyn_v * y2h)
            dy2h = dyn_v * nw_ref[gi]
            dy2 = r * (dy2h - y2h * jnp.mean(dy2h * y2h, axis=-1, keepdims=True))
            dz_ref[:, gsl] = (dy2 * y1 * _dsilu(z)).astype(BF16)
            dY = dy2 * sz
            X = x * t["dt"]
            dYe = dY * t["eac"]
            dC_s = _dot(dYe, S0)
            dB_s = _dot(X * t["wdec"], dS1)
            dS_c = _dot(dYe, Cm, _TN)
            dXm, Gs, Gts = [], [], []
            for pair in range(H // 2):
                dYp, Xp = dY[:, pair * 128:(pair + 1) * 128], X[:, pair * 128:(pair + 1) * 128]
                dXm.append(jnp.where(k.lo, _dot(CBt * t["decay_t"][2 * pair], dYp),
                                     _dot(CBt * t["decay_t"][2 * pair + 1], dYp)))
                for mask in (k.lo, ~k.lo):
                    Gs.append(_dot(jnp.where(mask, dYp, 0.0), Xp, _NT))
                    Gts.append(_dot(jnp.where(mask, Xp, 0.0), dYp, _NT))
            yield
            dXs = dXs_raw * t["wdec"]
            dX = jnp.concatenate(dXm, axis=1) + dXs
            dCB, dCBt, q_sums = 0.0, 0.0, []
            for h in range(H):
                g_dec, gt_dec = Gs[h] * t["decay"][h], Gts[h] * t["decay_t"][h]
                dCB = dCB + g_dec
                dCBt = dCBt + gt_dec
                d = CB * g_dec - CBt * gt_dec
                q_sums.append(sum(lax.dot_general(pt, k.ones, _NN, preferred_element_type=F32)
                                  for pt in _parts(d, 2)))
            q_f = jnp.concatenate([jnp.where(k.lo, q_sums[0], q_sums[1]), jnp.where(k.lo, q_sums[2], q_sums[3])],
                                  axis=1)
            x_dxs = X * dXs
            tot = [_total(dS1[h * P:(h + 1) * P, :] * S0[h * P:(h + 1) * P, :]) * t["ea_last"][h] for h in range(H)]
            tot_f = k.spread(jnp.concatenate(tot, axis=1))
            d_alast = k.head_sums(jnp.broadcast_to(_colsum(x_dxs), (8, W)))[0:1, :] + tot_f
            dacum = q_f + k.head_sums(dY * (y_off_raw * t["eac"]) - x_dxs) + jnp.where(last, d_alast, 0.0)
            dx_dt = k.head_sums(dX * x)
            d_skip = _colsum(k.head_sums(dY * x))
            for h in range(H):
                hsl = slice(gi * W + h * P, gi * W + (h + 1) * P)
                dstate[hsl, :] = t["ea_last"][h] * dS1[h * P:(h + 1) * P, :] + dS_c[h * P:(h + 1) * P, :]
            dc_s2 = _dot(dCB, Bm)
            db_s2 = _dot(dCBt, Cm)
            yield
            da = _sum_by(k.upper, dacum)
            yield
            ddt_raw = (da * t["A"] + dx_dt) * _sigmoid(t["dtr"])
            dact_ref[:, gsl] = (dX * t["dt"] + t["D"] * dY).astype(BF16)
            dact_ref[:, CC0 + gi * N:CC0 + (gi + 1) * N] = (dC_s + dc_s2).astype(BF16)
            dact_ref[:, CB0 + gi * N:CB0 + (gi + 1) * N] = (dB_s + db_s2).astype(BF16)
            ddt_ref[gi] = k.gather4(ddt_raw)
            dp_ref[gi] += k.gather4(jnp.concatenate([_colsum(ddt_raw), _colsum(da * t["dt"]) * t["A"], d_skip],
                                                    axis=0))

        _lockstep(group(gi) for gi in range(G))

    rev = lambda c: nc - 1 - c
    x_spec, b_spec, c_spec, z_spec, dt_spec, p_spec, nw_spec, s_spec = _ssd_specs(L, rev)
    y_spec = pl.BlockSpec((L, G * W), lambda g, c: (rev(c), g))
    half = d_conv.shape[1] // 2
    return pl.pallas_call(
        body, name=name, grid=(SSM_GROUPS // G, nc),
        in_specs=[y_spec, x_spec, b_spec, c_spec, z_spec, dt_spec, p_spec, nw_spec, y_spec, s_spec, D_PROJ_ANY,
                  D_PROJ_ANY],
        out_specs=(pl.BlockSpec((L, half), lambda g, c: (rev(c), 0)), z_spec, dt_spec, p_spec, nw_spec),
        out_shape=(jax.ShapeDtypeStruct(d_conv.shape, BF16), jax.ShapeDtypeStruct(d_proj.shape, BF16),
                   jax.ShapeDtypeStruct((SSM_GROUPS, S, H), F32), jax.ShapeDtypeStruct((SSM_GROUPS, 3, H), F32),
                   jax.ShapeDtypeStruct((SSM_GROUPS, 1, W), F32)),
        scratch_shapes=[pltpu.VMEM((G * W, N), F32)], input_output_aliases={10: 1, 11: 0},
        compiler_params=_params("parallel", "arbitrary"))(dyn, conv, conv, conv, proj, dt_raw, pvec, nw, y_ssd, states,
                                                          d_proj, d_conv)


def _unit_lower_inverse(A, ii, jj):
    eye = (ii == jj).astype(F32)
    same = (ii // GDN_INV_BLOCK) == (jj // GDN_INV_BLOCK)
    Ad = jnp.where(same, A, 0.0)
    Ao = A - Ad
    P2 = _dot3(Ad, Ad)
    yield
    P4, X = _dot(P2, P2), _dot3(eye - Ad, eye + P2)
    yield
    P8, X = _dot(P4, P4), X + _dot2(X, P4)
    yield
    X = X + _dot2(X, P8)
    yield
    Bm = _dot3(X, Ao)
    yield
    B2 = _dot3(Bm, Bm)
    yield
    Y = (eye - Bm) + B2 - _dot2(Bm, B2)
    yield
    T = _dot3(Y, X)
    yield
    return T


def _gdn_specs(L, order):
    G = GDN_QK_PER_STEP
    Hd, W = G * GDN_HEAD, G * GDN_V_PER_QK * GDN_HEAD
    q_spec = pl.BlockSpec((L, Hd), lambda h, c: (order(c), (C_QKV - C_XBC) // Hd + h))
    k_spec = pl.BlockSpec((L, Hd), lambda h, c: (order(c), (C_QKV - C_XBC + 1024) // Hd + h))
    v_spec = pl.BlockSpec((L, W), lambda h, c: (order(c), (C_QKV - C_XBC + 2048) // W + h))
    z_spec = pl.BlockSpec((L, W), lambda h, c: (order(c), C_ZG // W + h))
    ba_spec = pl.BlockSpec((G, L, GDN_V_PER_QK), lambda h, c: (h, order(c), 0))
    p_spec = pl.BlockSpec((G, 2, GDN_V_PER_QK), lambda h, c: (h, 0, 0))
    nw_spec = pl.BlockSpec((1, GDN_HEAD), lambda h, c: (0, 0))
    s_spec = pl.BlockSpec((G, 1, GDN_V_PER_QK * GDN_HEAD, GDN_HEAD), lambda h, c: (h, order(c), 0, 0))
    t_spec = pl.BlockSpec((G * GDN_V_PER_QK, 1, L, L), lambda h, c: (h, order(c), 0, 0))
    return q_spec, k_spec, v_spec, z_spec, ba_spec, p_spec, nw_spec, s_spec, t_spec


def _gdn_qk(qa, ka):
    rq = lax.rsqrt(_rowsum(qa * qa) + EPS)
    rk = lax.rsqrt(_rowsum(ka * ka) + EPS)
    q, k = qa * rq * (GDN_HEAD ** -0.5), ka * rk
    return dict(qa=qa, rq=rq, rk=rk, q=q, k=k, QK=_dot(q, k, _NT))


def _gdn_gates(qk, b_col, a_col, p, j, ii, jj):
    L = qk["q"].shape[0]
    rq, rk, q, k = qk["rq"], qk["rk"], qk["q"], qk["k"]
    sp_in = a_col + p[0:1, j:j + 1]
    neg_ea = -jnp.exp(p[1:2, j:j + 1])
    g = neg_ea * _softplus(sp_in)
    gcum, gcum_row = _cumsum_forms(g, ii, jj)
    beta = _sigmoid(b_col)
    yield
    Dm = jnp.exp(jnp.where(ii >= jj, gcum - gcum_row, NEG_INF))
    eg = jnp.exp(gcum)
    g_last = gcum[L - 1:L, :]
    wdec = jnp.exp(g_last - gcum)
    return dict(rq=rq, rk=rk, q=q, k=k, beta=beta, sp_in=sp_in, neg_ea=neg_ea, g=g, Dm=Dm, kbeta=k * beta, eg=eg,
                g_last=g_last, wdec=wdec, kdec=k * wdec)


def _gdn_fwd(conv, proj, b_raw, a_raw, pvec, nw, name):
    S = conv.shape[0]
    L, Hd, J, G = GDN_CHUNK, GDN_HEAD, GDN_V_PER_QK, GDN_QK_PER_STEP
    W = J * Hd
    nc = S // L

    def body(q_ref, k_ref, v_ref, z_ref, b_ref, a_ref, p_ref, nw_ref, o_ref, on_ref, s0_ref, t_ref, state):
        c = pl.program_id(1)

        @pl.when(c == 0)
        def _():
            state[...] = jnp.zeros_like(state)

        ii, jj = _iota(L, L, 0), _iota(L, L, 1)
        for hq in range(G):
            s0_ref[hq, 0] = state[hq * W:(hq + 1) * W, :]
        qks = [_gdn_qk(q_ref[:, hq * Hd:(hq + 1) * Hd].astype(F32), k_ref[:, hq * Hd:(hq + 1) * Hd].astype(F32))
               for hq in range(G)]

        def head(hq, j):
            hd = hq * J + j
            sl = slice(hd * Hd, (hd + 1) * Hd)
            t = yield from _gdn_gates(qks[hq], b_ref[hq][:, j:j + 1], a_ref[hq][:, j:j + 1], p_ref[hq], j, ii, jj)
            KK = _dot(t["kbeta"], t["k"], _NT)
            QK = qks[hq]["QK"]
            yield
            T = yield from _unit_lower_inverse(jnp.where(ii > jj, KK * t["Dm"], 0.0), ii, jj)
            t_ref[hd, 0] = T
            S0 = state[sl, :]
            U = _dot2(T, v_ref[:, sl].astype(F32) * t["beta"])
            Wm = _dot2(T, t["kbeta"] * t["eg"])
            o_inter = _dot(t["q"] * t["eg"], S0)
            yield
            Vn = U - _dot(Wm, S0)
            yield
            o = o_inter + _dot(QK * t["Dm"], Vn)
            s_new = _dot(t["kdec"], Vn, _TN)
            yield
            state[sl, :] = S0 * jnp.exp(t["g_last"]) + s_new
            o_ref[:, sl] = o
            r = lax.rsqrt(jnp.mean(o * o, axis=-1, keepdims=True) + EPS)
            on_ref[:, sl] = ((o * r * nw_ref[...]) * _silu(z_ref[:, sl].astype(F32))).astype(BF16)

        _lockstep(head(hq, j) for hq in range(G) for j in range(J))

    q_spec, k_spec, v_spec, z_spec, ba_spec, p_spec, nw_spec, s_spec, t_spec = _gdn_specs(L, lambda c: c)
    o_spec = pl.BlockSpec((L, G * W), lambda h, c: (c, h))
    return pl.pallas_call(
        body, name=name, grid=(GDN_QK_HEADS // G, nc),
        in_specs=[q_spec, k_spec, v_spec, z_spec, ba_spec, ba_spec, p_spec, nw_spec],
        out_specs=(o_spec, o_spec, s_spec, t_spec),
        out_shape=(jax.ShapeDtypeStruct((S, GDN_QK_HEADS * W), F32), jax.ShapeDtypeStruct((S, GDN_QK_HEADS * W), BF16),
                   jax.ShapeDtypeStruct((GDN_QK_HEADS, nc, W, Hd), F32),
                   jax.ShapeDtypeStruct((GDN_QK_HEADS * J, nc, L, L), F32)),
        scratch_shapes=[pltpu.VMEM((G * W, Hd), F32)],
        compiler_params=_params("parallel", "arbitrary"))(conv, conv, conv, proj, b_raw, a_raw, pvec, nw)


def _gdn_bwd(don, conv, proj, b_raw, a_raw, pvec, nw, o_pre, states, t_inv, d_proj, d_conv, name):
    S = conv.shape[0]
    L, Hd, J, G = GDN_CHUNK, GDN_HEAD, GDN_V_PER_QK, GDN_QK_PER_STEP
    W = J * Hd
    nc = S // L
    assert G == GDN_QK_HEADS
    CK0, CV0 = GDN_QK_HEADS * Hd, 2 * GDN_QK_HEADS * Hd

    def body(don_ref, q_ref, k_ref, v_ref, z_ref, b_ref, a_ref, p_ref, nw_ref, o_ref, s0_ref, t_ref, _, _2,
             dact_ref, dz_ref, db_ref, da_ref, dp_ref, dnw_ref, dstate):
        c = pl.program_id(1)

        @pl.when(c == 0)
        def _():
            dstate[...] = jnp.zeros_like(dstate)
            dp_ref[...] = jnp.zeros_like(dp_ref)
            dnw_ref[...] = jnp.zeros_like(dnw_ref)

        ii, jj = _iota(L, L, 0), _iota(L, L, 1)
        last = (_iota(L, 1, 0) == L - 1)
        res = {}
        qks = [_gdn_qk(q_ref[:, hq * Hd:(hq + 1) * Hd].astype(F32), k_ref[:, hq * Hd:(hq + 1) * Hd].astype(F32))
               for hq in range(G)]

        def head(hq, j):
            hd = hq * J + j
            sl = slice(hd * Hd, (hd + 1) * Hd)
            qa = qks[hq]["qa"]
            t = yield from _gdn_gates(qks[hq], b_ref[hq][:, j:j + 1], a_ref[hq][:, j:j + 1], p_ref[hq], j, ii, jj)
            q, k, beta, eg, Dm, kbeta, kdec = (t[nm] for nm in ("q", "k", "beta", "eg", "Dm", "kbeta", "kdec"))
            T = t_ref[hd, 0]
            v, z, o = v_ref[:, sl].astype(F32), z_ref[:, sl].astype(F32), o_ref[:, sl]
            S0, dS1 = s0_ref[hq, 0, j * Hd:(j + 1) * Hd, :], dstate[sl, :]
            sz = _silu(z)
            r = lax.rsqrt(jnp.mean(o * o, axis=-1, keepdims=True) + EPS)
            oh = o * r
            d_on = don_ref[:, sl]
            dz_ref[:, sl] = (d_on * (oh * nw_ref[...]) * _dsilu(z)).astype(BF16)
            dn = d_on * sz
            dnw_part = _colsum(dn * oh)
            doh = dn * nw_ref[...]
            dO = r * (doh - oh * jnp.mean(doh * oh, axis=-1, keepdims=True))
            Rw = kbeta * eg
            qe = q * eg
            U = _dot2(T, v * beta)
            Wm = _dot2(T, Rw)
            KK = _dot(kbeta, k, _NT)
            QK = qks[hq]["QK"]
            o_inter = _dot(qe, S0)
            dq_s = _dot(dO, S0, _NT)
            dS_q = _dot(qe, dO, _TN)
            yield
            Am = jnp.where(ii > jj, KK * Dm, 0.0)
            Pm = QK * Dm
            Vn = U - _dot(Wm, S0)
            dVn_s = _dot(kdec, dS1)
            yield
            dVn = _dot(Pm, dO, _TN) + dVn_s
            dP = _dot(dO, Vn, _NT)
            dKd = _dot(Vn, dS1, _NT)
            yield
            dQK = dP * Dm
            dq = _dot(dQK, k) + dq_s * eg
            dk = _dot(dQK, q, _TN) + dKd * t["wdec"]
            dstate[sl, :] = jnp.exp(t["g_last"]) * dS1 + dS_q - _dot(Wm, dVn, _TN)
            dW = -_dot(dVn, S0, _NT)
            dRu = _dot2(T, dVn, _TN)
            yield
            dRw = _dot2(T, dW, _TN)
            dA_u = _dot(dRu, U, _NT)
            yield
            dA = jnp.where(ii > jj, -(dA_u + _dot(dRw, Wm, _NT)), 0.0)
            yield
            dKK = dA * Dm
            dkbeta = _dot(dKK, k) + dRw * eg
            dk = dk + _dot(dKK, kbeta, _TN)
            yield
            dk = dk + dkbeta * beta
            dbeta = _rowsum(dkbeta * k + dRu * v)
            dact_ref[:, CV0 + hd * Hd:CV0 + (hd + 1) * Hd] = (dRu * beta).astype(BF16)
            Q = dA * Am + dP * Pm
            kd = dKd * kdec
            d_glast = _total(kd) + jnp.exp(t["g_last"]) * _total(dS1 * S0)
            q_sums = _row_col_sums(Q)
            rest = _rowsum(dRw * Rw + dO * o_inter - kd) + jnp.where(last, d_glast, 0.0)
            yield
            dg = _rev_cumsum_col(q_sums + rest, ii, jj)
            yield
            da_raw = dg * t["neg_ea"] * _sigmoid(t["sp_in"])
            res[hq, j] = dict(dq=dq, dk=dk, db=dbeta * beta * (1.0 - beta), da=da_raw, d_bias=_colsum(da_raw),
                              d_alog=_colsum(dg * t["g"]), dnw=dnw_part, rq=t["rq"], rk=t["rk"], k=k, qh=qa * t["rq"])

        _lockstep(head(hq, j) for hq in range(G) for j in range(J))
        for hq in range(G):
            parts = [res[hq, j] for j in range(J)]
            hsl = slice(hq * Hd, (hq + 1) * Hd)
            p0 = parts[0]
            dqh = sum(pt["dq"] for pt in parts) * (GDN_HEAD ** -0.5)
            dkn = sum(pt["dk"] for pt in parts)
            dact_ref[:, hsl] = (p0["rq"] * (dqh - p0["qh"] * _rowsum(dqh * p0["qh"]))).astype(BF16)
            dact_ref[:, CK0 + hq * Hd:CK0 + (hq + 1) * Hd] = (
                p0["rk"] * (dkn - p0["k"] * _rowsum(dkn * p0["k"]))).astype(BF16)
            db_ref[hq] = jnp.concatenate([pt["db"] for pt in parts], axis=1)
            da_ref[hq] = jnp.concatenate([pt["da"] for pt in parts], axis=1)
            dp_ref[hq] += jnp.concatenate([jnp.concatenate([pt["d_bias"] for pt in parts], axis=1),
                                           jnp.concatenate([pt["d_alog"] for pt in parts], axis=1)], axis=0)
            dnw_ref[hq] += sum(pt["dnw"] for pt in parts)

    rev = lambda c: nc - 1 - c
    q_spec, k_spec, v_spec, z_spec, ba_spec, p_spec, nw_spec, s_spec, t_spec = _gdn_specs(L, rev)
    o_spec = pl.BlockSpec((L, G * W), lambda h, c: (rev(c), h))
    dnw_spec = pl.BlockSpec((G, 1, Hd), lambda h, c: (h, 0, 0))
    half = d_conv.shape[1] // 2
    return pl.pallas_call(
        body, name=name, grid=(GDN_QK_HEADS // G, nc),
        in_specs=[o_spec, q_spec, k_spec, v_spec, z_spec, ba_spec, ba_spec, p_spec, nw_spec, o_spec, s_spec, t_spec,
                  D_PROJ_ANY, D_PROJ_ANY],
        out_specs=(pl.BlockSpec((L, half), lambda h, c: (rev(c), 1)), z_spec, ba_spec, ba_spec, p_spec, dnw_spec),
        out_shape=(jax.ShapeDtypeStruct(d_conv.shape, BF16), jax.ShapeDtypeStruct(d_proj.shape, BF16),
                   jax.ShapeDtypeStruct((GDN_QK_HEADS, S, J), F32), jax.ShapeDtypeStruct((GDN_QK_HEADS, S, J), F32),
                   jax.ShapeDtypeStruct((GDN_QK_HEADS, 2, J), F32), jax.ShapeDtypeStruct((GDN_QK_HEADS, 1, Hd), F32)),
        scratch_shapes=[pltpu.VMEM((G * W, Hd), F32)], input_output_aliases={12: 1, 13: 0},
        compiler_params=_params("parallel", "arbitrary"))(don, conv, conv, conv, proj, b_raw, a_raw, pvec, nw, o_pre,
                                                          states, t_inv, d_proj, d_conv)


def _ada_fwd(c_all, w_loc, b_loc, name):
    n = w_loc.shape[1]

    def body(c_ref, w_ref, b_ref, o_ref):
        o_ref[...] = _dot3(_silu(c_ref[...]), w_ref[...]) + b_ref[...]

    return pl.pallas_call(body, name=name, out_shape=jax.ShapeDtypeStruct((N_DEV, n), F32),
                          compiler_params=pltpu.CompilerParams(vmem_limit_bytes=VMEM_LIMIT))(c_all, w_loc, b_loc)


def _ada_bwd(c_all_t, dmod_cols, name):
    Dm, n = c_all_t.shape[0], dmod_cols.shape[1]

    def body(c_ref, d_ref, o_ref):
        ca = _silu(c_ref[...])
        acc = ca[:, 0:1] * d_ref[0:1, :]
        for i in range(1, N_DEV):
            acc = acc + ca[:, i:i + 1] * d_ref[i:i + 1, :]
        o_ref[...] = acc

    return pl.pallas_call(body, name=name, out_shape=jax.ShapeDtypeStruct((Dm, n), F32),
                          compiler_params=pltpu.CompilerParams(vmem_limit_bytes=VMEM_LIMIT))(c_all_t, dmod_cols)


ADAM_BLOCK_BYTES = 12 * 1024 * 1024


def _adam(contrib, w, m, v, name):
    n, R, C = contrib.shape
    tr = R
    while tr % 16 == 0 and (n + 7) * tr * C * 4 > ADAM_BLOCK_BYTES:
        tr //= 2

    def body(c_ref, w_ref, m_ref, v_ref, g_ref, d_ref, nm_ref, nv_ref):
        g = c_ref[0].astype(F32)
        for i in range(1, n):
            g = g + c_ref[i].astype(F32)
        nm = ADAM_B1 * m_ref[...] + (1.0 - ADAM_B1) * g
        nv = ADAM_B2 * v_ref[...] + (1.0 - ADAM_B2) * (g * g)
        m_hat = nm / (1.0 - ADAM_B1 ** ADAM_STEP)
        v_hat = nv / (1.0 - ADAM_B2 ** ADAM_STEP)
        g_ref[...] = g
        d_ref[...] = -ADAM_LR * (m_hat / (jnp.sqrt(v_hat) + ADAM_EPS) + ADAM_WD * w_ref[...])
        nm_ref[...] = nm
        nv_ref[...] = nv

    spec = pl.BlockSpec((tr, C), lambda i: (i, 0))
    shp = jax.ShapeDtypeStruct((R, C), F32)
    return pl.pallas_call(
        body, name=name, grid=(R // tr,), in_specs=[pl.BlockSpec((n, tr, C), lambda i: (0, i, 0)), spec, spec, spec],
        out_specs=(spec,) * 4, out_shape=(shp,) * 4, compiler_params=_params("parallel"))(contrib, w, m, v)


def _exchange(arrays, modes, name, chips=False):
    n = len(arrays)
    out_shape = tuple(jax.ShapeDtypeStruct((N_DEV,) + a.shape if md == "gather" else a.shape, a.dtype)
                      for a, md in zip(arrays, modes))

    def body(*refs):
        ins, outs = refs[:n], refs[n:2 * n]
        send_sems, recv_sems, loc_sems = refs[2 * n:]
        me, peers = _peer_table(chips)

        def src(k, slot):
            return ins[k] if modes[k] == "gather" else ins[k].at[slot]

        def remote(k, m, to_slot, land_slot):
            return pltpu.make_async_remote_copy(
                src_ref=src(k, to_slot), dst_ref=outs[k].at[land_slot], send_sem=send_sems.at[k, m],
                recv_sem=recv_sems.at[k, m], device_id=peers[m][0], device_id_type=pl.DeviceIdType.MESH)

        local = [pltpu.make_async_copy(src(k, me), outs[k].at[me], loc_sems.at[k]) for k in range(n)]
        for cp in local:
            cp.start()
        sends = [remote(k, m, peers[m][1], me) for m in range(len(peers)) for k in range(n)]
        for cp in sends:
            cp.start()
        for m in range(len(peers)):
            for k in range(n):
                remote(k, m, peers[m][1], peers[m][1]).wait_recv()
        for cp in sends:
            cp.wait_send()
        for cp in local:
            cp.wait()

    any_spec = pl.BlockSpec(memory_space=pl.ANY)
    return pl.pallas_call(
        body, name=name, in_specs=[any_spec] * n, out_specs=(any_spec,) * n, out_shape=out_shape,
        scratch_shapes=[pltpu.SemaphoreType.DMA((n, N_DEV - 1)), pltpu.SemaphoreType.DMA((n, N_DEV - 1)),
                        pltpu.SemaphoreType.DMA((n,))])(*arrays)


def _gather_two_level(arrays, name):
    n = len(arrays)
    out_shape = tuple(jax.ShapeDtypeStruct((N_DEV,) + a.shape, a.dtype) for a in arrays)

    def body(*refs):
        ins, outs = refs[:n], refs[n:2 * n]
        send_sems, recv_sems, loc_sems = refs[2 * n:]
        ix, iy, ic = lax.axis_index("x"), lax.axis_index("y"), lax.axis_index("c")
        lin = lambda px, py, pc: 4 * px + 2 * py + pc
        me, sib = lin(ix, iy, ic), (ix, iy, 1 - ic)
        chips = [(1 - ix, iy), (ix, 1 - iy), (1 - ix, 1 - iy)]

        def copy(k, s, block, to, src=None):
            return pltpu.make_async_remote_copy(
                src_ref=outs[k].at[block] if src is None else src, dst_ref=outs[k].at[block],
                send_sem=send_sems.at[k, s], recv_sem=recv_sems.at[k, s], device_id=to,
                device_id_type=pl.DeviceIdType.MESH)

        local = [pltpu.make_async_copy(ins[k], outs[k].at[me], loc_sems.at[k]) for k in range(n)]
        for cp in local:
            cp.start()
        first = [copy(k, 1 + j, me, (cx, cy, ic), src=ins[k]) for j, (cx, cy) in enumerate(chips) for k in range(n)]
        first += [copy(k, 0, me, sib, src=ins[k]) for k in range(n)]
        for cp in first:
            cp.start()
        passed = []
        for j, (cx, cy) in enumerate(chips):
            for k in range(n):
                copy(k, 1 + j, lin(cx, cy, ic), sib).wait_recv()
                passed.append(copy(k, 4 + j, lin(cx, cy, ic), sib))
                passed[-1].start()
        for k in range(n):
            copy(k, 0, lin(*sib), sib).wait_recv()
            for j, (cx, cy) in enumerate(chips):
                copy(k, 4 + j, lin(cx, cy, 1 - ic), sib).wait_recv()
        for cp in first + passed:
            cp.wait_send()
        for cp in local:
            cp.wait()

    any_spec = pl.BlockSpec(memory_space=pl.ANY)
    return pl.pallas_call(
        body, name=name, in_specs=[any_spec] * n, out_specs=(any_spec,) * n, out_shape=out_shape,
        scratch_shapes=[pltpu.SemaphoreType.DMA((n, N_DEV - 1)), pltpu.SemaphoreType.DMA((n, N_DEV - 1)),
                        pltpu.SemaphoreType.DMA((n,))])(*arrays)


def _peer_table(chips=False):
    ix, iy, ic = lax.axis_index("x"), lax.axis_index("y"), lax.axis_index("c")
    peers = []
    for m in ((2, 4, 6) if chips else range(1, N_DEV)):
        px = 1 - ix if m & 4 else ix
        py = 1 - iy if m & 2 else iy
        pc = 1 - ic if m & 1 else ic
        peers.append(((px, py, pc), 2 * px + py if chips else 4 * px + 2 * py + pc))
    return (2 * ix + iy if chips else 4 * ix + 2 * iy + ic), peers


def _exchange_start(arrays, modes, after, name, chips=False):
    n = len(arrays)
    land_shapes = [(N_DEV,) + a.shape if md == "gather" else a.shape for a, md in zip(arrays, modes)]

    def body(*refs):
        ins, lands = refs[:n], refs[n:2 * n]
        send_sems, recv_sems = refs[2 * n + 1], refs[2 * n + 2]
        token = refs[-1]
        me, peers = _peer_table(chips)

        def src(k, slot):
            return ins[k] if modes[k] == "gather" else ins[k].at[slot]

        for peer, slot in peers:
            for k in range(n):
                pltpu.make_async_remote_copy(
                    src_ref=src(k, slot), dst_ref=lands[k].at[me], send_sem=send_sems, recv_sem=recv_sems,
                    device_id=peer, device_id_type=pl.DeviceIdType.MESH).start()
        token[...] = jnp.zeros_like(token)

    hbm = pl.BlockSpec(memory_space=pltpu.HBM)
    sem = pl.BlockSpec(memory_space=pltpu.SEMAPHORE)
    sem_shape = pltpu.SemaphoreType.DMA(())
    operands = [pltpu.with_memory_space_constraint(a, pltpu.HBM) for a in arrays]
    operands += [pltpu.with_memory_space_constraint(lax.empty(s, a.dtype), pltpu.HBM)
                 for s, a in zip(land_shapes, arrays)]
    out = pl.pallas_call(
        body, name=name,
        out_shape=(sem_shape, sem_shape) + tuple(pltpu.HBM(a.shape, a.dtype) for a in arrays)
        + tuple(pltpu.HBM(s, a.dtype) for s, a in zip(land_shapes, arrays)) + (jax.ShapeDtypeStruct((8, 128), F32),),
        in_specs=[hbm] * (2 * n) + [pl.BlockSpec(memory_space=pl.ANY)],
        out_specs=(sem, sem) + (hbm,) * (2 * n) + (pl.BlockSpec(memory_space=pltpu.VMEM),),
        input_output_aliases={i: 2 + i for i in range(2 * n)},
        compiler_params=pltpu.CompilerParams(has_side_effects=pltpu.SideEffectType.DATAFLOW_SIDE_EFFECTING))(
            *operands, after)
    return out[0], out[1], out[2:2 + n], out[2 + n:2 + 2 * n], out[-1]


def _exchange_wait(started, modes, after, name, chips=False):
    send_sems, recv_sems, sent, lands, _ = started
    n = len(sent)

    def body(*refs):
        ins, zones = refs[:n], refs[n:2 * n]
        send_ref, recv_ref = refs[2 * n], refs[2 * n + 1]
        _, peers = _peer_table(chips)

        def src(k, slot):
            return ins[k] if modes[k] == "gather" else ins[k].at[slot]

        for peer, slot in peers:
            for k in range(n):
                cp = pltpu.make_async_remote_copy(
                    src_ref=src(k, slot), dst_ref=zones[k].at[slot], send_sem=send_ref, recv_sem=recv_ref,
                    device_id=peer, device_id_type=pl.DeviceIdType.MESH)
                cp.wait_send()
                cp.wait_recv()

    hbm = pl.BlockSpec(memory_space=pltpu.HBM)
    sem = pl.BlockSpec(memory_space=pltpu.SEMAPHORE)
    out = pl.pallas_call(
        body, name=name,
        out_shape=tuple(pltpu.HBM(a.shape, a.dtype) for a in sent) + tuple(pltpu.HBM(a.shape, a.dtype) for a in lands),
        in_specs=[hbm] * (2 * n) + [sem, sem, pl.BlockSpec(memory_space=pl.ANY)], out_specs=(hbm,) * (2 * n),
        input_output_aliases={i: i for i in range(2 * n)},
        compiler_params=pltpu.CompilerParams(has_side_effects=pltpu.SideEffectType.DATAFLOW_SIDE_EFFECTING))(
            *sent, *lands, send_sems, recv_sems, after)
    ix, iy, ic = lax.axis_index("x"), lax.axis_index("y"), lax.axis_index("c")
    me = 2 * ix + iy if chips else 4 * ix + 2 * iy + ic
    filled = []
    for k in range(n):
        own = sent[k] if modes[k] == "gather" else lax.dynamic_index_in_dim(sent[k], me, axis=0, keepdims=False)
        filled.append(lax.dynamic_update_index_in_dim(out[n + k], own, me, axis=0))
    return filled


def _swap_sibling(to_c0, to_c1, name):
    def body(c0_ref, c1_ref, out_ref, send_sem, recv_sem):
        ix, iy, ic = lax.axis_index("x"), lax.axis_index("y"), lax.axis_index("c")

        def copy(src):
            return pltpu.make_async_remote_copy(src_ref=src, dst_ref=out_ref, send_sem=send_sem, recv_sem=recv_sem,
                                                device_id=(ix, iy, 1 - ic), device_id_type=pl.DeviceIdType.MESH)

        @pl.when(ic == 0)
        def _():
            copy(c1_ref).start()

        @pl.when(ic == 1)
        def _():
            copy(c0_ref).start()

        copy(c0_ref).wait()

    any_spec = pl.BlockSpec(memory_space=pl.ANY)
    return pl.pallas_call(body, name=name, in_specs=[any_spec, any_spec], out_specs=any_spec,
                          out_shape=jax.ShapeDtypeStruct(to_c0.shape, to_c0.dtype),
                          scratch_shapes=[pltpu.SemaphoreType.DMA, pltpu.SemaphoreType.DMA])(to_c0, to_c1)


def _add_pair(to_c0, to_c1, got, name):
    n, R, C = got.shape
    tr = _blk(R, 256)

    def body(c0_ref, c1_ref, got_ref, o_ref):
        ic = lax.axis_index("c")

        @pl.when(ic == 0)
        def _():
            o_ref[...] = (c0_ref[...].astype(F32) + got_ref[...].astype(F32)).astype(o_ref.dtype)

        @pl.when(ic == 1)
        def _():
            o_ref[...] = (c1_ref[...].astype(F32) + got_ref[...].astype(F32)).astype(o_ref.dtype)

    spec = pl.BlockSpec((1, tr, C), lambda i, j: (i, j, 0))
    return pl.pallas_call(body, name=name, grid=(n, R // tr), in_specs=[spec, spec, spec], out_specs=spec,
                          out_shape=jax.ShapeDtypeStruct(got.shape, got.dtype),
                          compiler_params=_params("parallel", "parallel"))(to_c0, to_c1, got)


W_IN_SPLITS = (0, 2048, 6144, 6176, 10272, 12320, 12336, 12352, 13376, 14400)
N_REPLICATED = 16640
REPLICATED = ("b_ada", "norm_mix_pre", "norm_mix_post", "ssm_conv_b", "ssm_dt_bias", "ssm_A_log", "ssm_D",
              "ssm_norm_w", "gdn_dt_bias", "gdn_A_log", "gdn_norm_w", "norm_mlp_pre", "norm_mlp_post")
WEIGHTS = ("w_ada", "b_ada", "norm_mix_pre", "norm_mix_post", "w_in", "ssm_conv_w", "ssm_conv_b", "ssm_dt_bias",
           "ssm_A_log", "ssm_D", "ssm_norm_w", "gdn_conv_w", "gdn_dt_bias", "gdn_A_log", "gdn_norm_w", "w_ssm_up",
           "w_gdn_up", "w_out", "norm_mlp_pre", "norm_mlp_post", "w_mlp_up", "w_mlp_down")


def _cols_of_shards(g, a, b):
    width, pieces = g.shape[2], []
    while a < b:
        i = a // width
        hi = min(b, (i + 1) * width)
        pieces.append(g[i][:, a - i * width:hi - i * width])
        a = hi
    return pieces


ORIG_SEGMENTS = ((0, 6144, "main", 0), (6144, 6176, "small", 0), (6176, 12320, "main", 6144),
                 (12320, 12352, "small", 32), (12352, 14400, "main", 12288))


def _orig_cols(main_cols, small_cols, a, b):
    pieces = []
    for s0, s1, which, off in ORIG_SEGMENTS:
        lo, hi = max(a, s0), min(b, s1)
        if lo < hi:
            pieces.append((main_cols if which == "main" else small_cols)[:, off + lo - s0:off + hi - s0])
    return jnp.concatenate(pieces, axis=1)


def _by_cols(t):
    return t.transpose(1, 0, 2).reshape(t.shape[1], N_DEV * t.shape[2])


def _to_col_shards(t):
    R, C8 = t.shape
    return t.reshape(R, N_DEV, C8 // N_DEV).transpose(1, 0, 2)


def _heads_first(t, groups):
    S = t.shape[0]
    return t.reshape(S, groups, t.shape[1] // groups).transpose(1, 0, 2)


def _heads_last(t):
    return t.transpose(1, 0, 2).reshape(t.shape[1], t.shape[0] * t.shape[2])


def kernel(x, c, w_ada, b_ada, norm_mix_pre, norm_mix_post, w_in, ssm_conv_w, ssm_conv_b, ssm_dt_bias, ssm_A_log, ssm_D, ssm_norm_w, gdn_conv_w, gdn_dt_bias, gdn_A_log, gdn_norm_w, w_ssm_up, w_gdn_up, w_out, norm_mlp_pre, norm_mlp_post, w_mlp_up, w_mlp_down, loss_target, m_w_ada, m_b_ada, m_norm_mix_pre, m_norm_mix_post, m_w_in, m_ssm_conv_w, m_ssm_conv_b, m_ssm_dt_bias, m_ssm_A_log, m_ssm_D, m_ssm_norm_w, m_gdn_conv_w, m_gdn_dt_bias, m_gdn_A_log, m_gdn_norm_w, m_w_ssm_up, m_w_gdn_up, m_w_out, m_norm_mlp_pre, m_norm_mlp_post, m_w_mlp_up, m_w_mlp_down, v_w_ada, v_b_ada, v_norm_mix_pre, v_norm_mix_post, v_w_in, v_ssm_conv_w, v_ssm_conv_b, v_ssm_dt_bias, v_ssm_A_log, v_ssm_D, v_ssm_norm_w, v_gdn_conv_w, v_gdn_dt_bias, v_gdn_A_log, v_gdn_norm_w, v_w_ssm_up, v_w_gdn_up, v_w_out, v_norm_mlp_pre, v_norm_mlp_post, v_w_mlp_up, v_w_mlp_down):
    S, Dm = x.shape[1], D_MODEL
    me = 4 * lax.axis_index("x") + 2 * lax.axis_index("y") + lax.axis_index("c")
    x2, tgt = x[0], loss_target[0]
    n_ada = w_ada.shape[2]
    given = dict(
        w_ada=(w_ada, m_w_ada, v_w_ada), b_ada=(b_ada, m_b_ada, v_b_ada),
        norm_mix_pre=(norm_mix_pre, m_norm_mix_pre, v_norm_mix_pre),
        norm_mix_post=(norm_mix_post, m_norm_mix_post, v_norm_mix_post), w_in=(w_in, m_w_in, v_w_in),
        ssm_conv_w=(ssm_conv_w, m_ssm_conv_w, v_ssm_conv_w), ssm_conv_b=(ssm_conv_b, m_ssm_conv_b, v_ssm_conv_b),
        ssm_dt_bias=(ssm_dt_bias, m_ssm_dt_bias, v_ssm_dt_bias), ssm_A_log=(ssm_A_log, m_ssm_A_log, v_ssm_A_log),
        ssm_D=(ssm_D, m_ssm_D, v_ssm_D), ssm_norm_w=(ssm_norm_w, m_ssm_norm_w, v_ssm_norm_w),
        gdn_conv_w=(gdn_conv_w, m_gdn_conv_w, v_gdn_conv_w), gdn_dt_bias=(gdn_dt_bias, m_gdn_dt_bias, v_gdn_dt_bias),
        gdn_A_log=(gdn_A_log, m_gdn_A_log, v_gdn_A_log), gdn_norm_w=(gdn_norm_w, m_gdn_norm_w, v_gdn_norm_w),
        w_ssm_up=(w_ssm_up, m_w_ssm_up, v_w_ssm_up), w_gdn_up=(w_gdn_up, m_w_gdn_up, v_w_gdn_up),
        w_out=(w_out, m_w_out, v_w_out), norm_mlp_pre=(norm_mlp_pre, m_norm_mlp_pre, v_norm_mlp_pre),
        norm_mlp_post=(norm_mlp_post, m_norm_mlp_post, v_norm_mlp_post), w_mlp_up=(w_mlp_up, m_w_mlp_up, v_w_mlp_up),
        w_mlp_down=(w_mlp_down, m_w_mlp_down, v_w_mlp_down))

    (c_all, scw, gcw, g_in) = _gather_two_level([c, ssm_conv_w[0], gdn_conv_w[0], w_in[0].astype(BF16)], "gather_w_in")
    c_all = c_all.reshape(N_DEV, Dm)
    sp = W_IN_SPLITS
    w_main = jnp.concatenate(_cols_of_shards(g_in, sp[0], sp[2]) + _cols_of_shards(g_in, sp[3], sp[5])
                             + _cols_of_shards(g_in, sp[7], sp[9]), axis=1)
    w_small = jnp.concatenate(_cols_of_shards(g_in, sp[2], sp[3]) + _cols_of_shards(g_in, sp[5], sp[7])
                              + [jnp.zeros((Dm, N_SMALL - 64), BF16)], axis=1)
    conv_w = jnp.concatenate([_by_cols(scw), _by_cols(gcw)], axis=1)
    conv_b = jnp.concatenate([ssm_conv_b, jnp.zeros_like(ssm_conv_b)], axis=1)

    b_loc = lax.dynamic_slice(b_ada, (0, me * n_ada), (1, n_ada))
    mod_part = _ada_fwd(c_all, w_ada[0], b_loc, "ada_fwd")
    (mod_rows,) = _exchange([mod_part.reshape(N_DEV, 1, n_ada)], ["a2a"], "exchange_mod")
    rest = _exchange_start([w_ssm_up[0].astype(BF16), w_gdn_up[0].astype(BF16), w_out[0].astype(BF16),
                            w_mlp_up[0].astype(BF16), w_mlp_down[0].astype(BF16)], ["gather"] * 5, mod_rows,
                           "gather_rest_start")
    mod = mod_rows.reshape(1, 6 * Dm) + rest[4][0:1, 0:1]
    sh1, sc1, g1, sh2, sc2, g2 = [mod[:, i * Dm:(i + 1) * Dm] for i in range(6)]

    h = _pre_fwd(x2, norm_mix_pre, sc1, sh1, "pre_mix")
    proj = _mm(h, w_main, S, N_MAIN, Dm, mode="nn", out_dtype=BF16, tn=MM_WIDE_N, name="proj_main")
    small = _mm(h, w_small, S, N_SMALL, Dm, mode="nn", out_dtype=F32, name="proj_small")
    conv = _conv_fwd(proj, conv_w, conv_b, "conv_fwd")
    dt_g, b_g, a_g = _heads_first(small[:, 0:32], 8), _heads_first(small[:, 32:48], 8), _heads_first(small[:, 48:64], 8)
    pv_ssm = jnp.stack([ssm_dt_bias.reshape(8, 4), ssm_A_log.reshape(8, 4), ssm_D.reshape(8, 4)], axis=1)
    nw_ssm = ssm_norm_w.reshape(8, 1, SSM_GROUP_WIDTH)
    pv_gdn = jnp.stack([gdn_dt_bias.reshape(8, 2), gdn_A_log.reshape(8, 2)], axis=1)
    y_ssd, ysn, st_ssm = _ssd_fwd(conv, proj, dt_g, pv_ssm, nw_ssm, "ssd_fwd")
    o_pre, ogn, st_gdn, t_inv = _gdn_fwd(conv, proj, b_g, a_g, pv_gdn, gdn_norm_w, "gdn_fwd")
    g_su, g_gu, g_out, g_mu, g_md = _exchange_wait(rest, ["gather"] * 5, ogn, "gather_rest_wait")
    w_su, w_gu = g_su.reshape(2 * Dm, Dm), g_gu.reshape(2 * Dm, Dm)
    w_o, w_mu, w_md = g_out.reshape(Dm, Dm), _by_cols(g_mu), g_md.reshape(4 * Dm, Dm)
    ys = _mm(ysn, w_su, S, Dm, 2 * Dm, mode="nn", out_dtype=F32, name="ssm_up")
    yg, merged = _gdn_up_merge(ogn, w_gu, ys, proj, "gdn_up_merge")
    mo, x1, h2 = _mix_out_post_pre(merged, w_o, x2, norm_mix_post, g1, norm_mlp_pre, sc2, sh2, "mix_out_post_pre")
    u, act = _mm(h2, w_mu, S, 4 * Dm, Dm, mode="nn", out_dtype=F32, epi="relu2", tn=MM_WIDE_N, name="mlp_up")
    y_mlp = _mm(act, w_md, S, Dm, 4 * Dm, mode="nn", out_dtype=F32, name="mlp_down")
    dx2, loss_loc, dy, dg2, dw_post2 = _final_fwd_bwd(x1, y_mlp, norm_mlp_post, g2, tgt, "post_mlp_loss_bwd")

    du = _mm(dy, w_md, S, 4 * Dm, Dm, mode="nt", out_dtype=BF16, epi="drelu2", extra=u, tn=MM_WIDE_N, name="mlp_down_dx")
    gw_md = _mm(act, dy, 4 * Dm, Dm, S, mode="tn", out_dtype=BF16, name="mlp_down_dw")
    dh2 = _mm(du, w_mu, S, Dm, 4 * Dm, mode="nt", out_dtype=F32, name="mlp_up_dx")
    gw_mu = _mm(h2, du, Dm, 4 * Dm, S, mode="tn", out_dtype=BF16, name="mlp_up_dw")
    mlp_x = _exchange_start([_to_col_shards(gw_mu), gw_md.reshape(N_DEV, -1, Dm)], ["a2a"] * 2, gw_md,
                            "grads_mlp_start")
    dx1, dsh2, dsc2, dw_pre2 = _pre_bwd(dh2, x1, norm_mlp_pre, sc2 + mlp_x[4][0:1, 0:1], dx2, "pre_mlp_bwd")
    dmo, dg1, dw_post1 = _post_bwd(dx1, mo, norm_mix_post, g1, "post_mix_bwd")
    gw_o = _mm(merged, dmo, Dm, Dm, S, mode="tn", out_dtype=BF16, name="mix_out_dw")
    dys, dyg, d_proj = _mix_out_dx_merge_bwd(dmo, w_o, ys, yg, proj, lax.empty((S, N_MAIN), BF16), "mix_out_dx_merge")
    dysn = _mm(dys, w_su, S, 2 * Dm, Dm, mode="nt", out_dtype=F32, tn=MM_WIDE_N, name="ssm_up_dx")
    gw_su = _mm(ysn, dys, 2 * Dm, Dm, S, mode="tn", out_dtype=BF16, name="ssm_up_dw")
    dogn = _mm(dyg, w_gu, S, 2 * Dm, Dm, mode="nt", out_dtype=F32, tn=MM_WIDE_N, name="gdn_up_dx")
    gw_gu = _mm(ogn, dyg, 2 * Dm, Dm, S, mode="tn", out_dtype=BF16, name="gdn_up_dw")
    mix_x = _exchange_start([gw_su.reshape(N_DEV, -1, Dm), gw_gu.reshape(N_DEV, -1, Dm), gw_o.reshape(N_DEV, -1, Dm)],
                            ["a2a"] * 3, gw_gu, "grads_mix_start")
    d_conv, d_proj, ddt_g, dpv_ssm, dnw_ssm = _ssd_bwd(dysn, conv, proj, dt_g, pv_ssm + mix_x[4][0, 0], nw_ssm, y_ssd,
                                                       st_ssm, d_proj, lax.empty(conv.shape, BF16), "ssd_bwd")
    d_conv, d_proj, db_g, da_g, dpv_gdn, dnw_gdn = _gdn_bwd(dogn, conv, proj, b_g, a_g, pv_gdn, gdn_norm_w, o_pre,
                                                            st_gdn, t_inv, d_proj, d_conv, "gdn_bwd")
    d_proj, dconv_w, dconv_b = _conv_bwd(d_conv, proj, conv_w, conv_b, 0, d_proj, "conv_bwd")
    d_small = jnp.concatenate([_heads_last(ddt_g), _heads_last(db_g), _heads_last(da_g),
                               jnp.zeros((S, N_SMALL - 64), F32)], axis=1).astype(BF16)
    gw_small = _mm(h, d_small, Dm, N_SMALL, S, mode="tn", out_dtype=BF16, name="proj_small_dw")
    main_cols = _mm(h, d_proj, Dm, N_MAIN, S, mode="tn", out_dtype=BF16, name="proj_main_dw")
    n_shard = w_in.shape[2]
    slabs = [_orig_cols(main_cols, gw_small, i * n_shard, (i + 1) * n_shard) for i in range(N_DEV)]
    to_c0, to_c1 = jnp.stack(slabs[0::2]), jnp.stack(slabs[1::2])
    chip_sum = _add_pair(to_c0, to_c1, _swap_sibling(to_c0, to_c1, "grads_w_in_pair"), "grads_w_in_pair_sum")
    in_x = _exchange_start([chip_sum], ["a2a"], gw_small, "grads_w_in_start", chips=True)
    dh = _mm(d_small, w_small + in_x[4][0:1, 0:1].astype(BF16), S, Dm, N_SMALL, mode="nt", out_dtype=F32,
             name="proj_small_dx")
    dh = _mm(d_proj, w_main, S, Dm, N_MAIN, mode="nt", out_dtype=F32, add=dh, name="proj_main_dx")
    dx, dsh1, dsc1, dw_pre1 = _pre_bwd(dh, x2, norm_mix_pre, sc1, dx1, "pre_mix_bwd")

    dconv_b = dconv_b[:, :ssm_conv_b.shape[1]]
    dmod = jnp.concatenate([dsh1, dsc1, dg1, dsh2, dsc2, dg2], axis=1)
    small_vec = jnp.concatenate(
        [dmod, dw_pre1, dw_post1, dconv_b, dpv_ssm[:, 0].reshape(1, 32), dpv_ssm[:, 1].reshape(1, 32),
         dpv_ssm[:, 2].reshape(1, 32), dnw_ssm.reshape(1, 2048), dpv_gdn[:, 0].reshape(1, 16),
         dpv_gdn[:, 1].reshape(1, 16), jnp.sum(dnw_gdn, axis=0), dw_pre2, dw_post2, dconv_w.reshape(1, -1)], axis=1)
    n_vec = small_vec.shape[1]
    small_vec = jnp.pad(small_vec, ((0, 0), (0, (-n_vec) % 1024))).reshape(-1, 1024)
    small_x = _exchange_start([small_vec], ["gather"], dx, "gather_small_start")
    r_mu, r_md = _exchange_wait(mlp_x, ["a2a"] * 2, small_x[4], "grads_mlp_wait")
    r_su, r_gu, r_o = _exchange_wait(mix_x, ["a2a"] * 3, small_x[4], "grads_mix_wait")
    results = {}

    def adam_big(nm, contrib):
        w3 = given[nm]
        res = _adam(contrib, w3[0][0], w3[1][0], w3[2][0], "adam_" + nm)
        results[nm] = tuple(r.reshape(w3[0].shape) for r in res)

    adam_big("w_ssm_up", r_su)
    adam_big("w_gdn_up", r_gu)
    adam_big("w_out", r_o)
    adam_big("w_mlp_up", r_mu)
    adam_big("w_mlp_down", r_md)
    (small_all,) = _exchange_wait(small_x, ["gather"], results["w_mlp_down"][0], "gather_small_wait")
    small_all = small_all.reshape(N_DEV, -1)
    dmod_cols = lax.dynamic_slice(small_all, (0, me * n_ada), (N_DEV, n_ada))
    gw_ada = _ada_bwd(c_all.T, dmod_cols, "ada_bwd")
    conv_all = small_all[:, N_REPLICATED:n_vec].reshape(N_DEV, CONV_K, 2 * N_DEV * 512)
    conv_contrib = jnp.concatenate(
        [lax.dynamic_slice(conv_all, (0, 0, me * 512), (N_DEV, CONV_K, 512)),
         lax.dynamic_slice(conv_all, (0, 0, N_DEV * 512 + me * 512), (N_DEV, CONV_K, 512))], axis=1)
    rep_contrib = small_all[:, :N_REPLICATED].reshape(N_DEV, N_REPLICATED // 128, 128)

    adam_big("w_ada", gw_ada[None])
    (r_in,) = _exchange_wait(in_x, ["a2a"], results["w_ada"][0], "grads_w_in_wait", chips=True)
    adam_big("w_in", r_in)
    packed = [jnp.concatenate([given[nm][i] for nm in REPLICATED], axis=1).reshape(N_REPLICATED // 128, 128)
              for i in range(3)]
    rep_res = _adam(rep_contrib, packed[0], packed[1], packed[2], "adam_replicated")
    pos = 0
    for nm in REPLICATED:
        size = given[nm][0].shape[1]
        results[nm] = tuple(r.reshape(1, N_REPLICATED)[:, pos:pos + size] for r in rep_res)
        pos += size
    conv_wmv = [jnp.concatenate([given["ssm_conv_w"][i][0], given["gdn_conv_w"][i][0]], axis=0) for i in range(3)]
    conv_res = _adam(conv_contrib, conv_wmv[0], conv_wmv[1], conv_wmv[2], "adam_conv_w")
    results["ssm_conv_w"] = tuple(r[None, :CONV_K] for r in conv_res)
    results["gdn_conv_w"] = tuple(r[None, CONV_K:] for r in conv_res)

    loss = lax.psum(loss_loc[0, 0], ("x", "y", "c"))
    return (loss, dx[None]) + tuple(results[nm][i] for i in range(4) for nm in WEIGHTS)
```

```python
import jax
import jax.numpy as jnp
from jax import lax
from jax.experimental import pallas as pl
from jax.experimental.pallas import tpu as pltpu

F32 = jnp.float32
BF16 = jnp.bfloat16
N_DEV = 8
D_MODEL = 1024
EPS = 1e-6
CONV_K = 4
SSM_CHUNK = 128
SSM_HEAD_DIM = 64
SSM_D_STATE = 128
SSM_GROUPS = 8
SSM_HEADS_PER_GROUP = 4
SSM_GROUP_WIDTH = SSM_HEADS_PER_GROUP * SSM_HEAD_DIM
SSM_GROUPS_PER_STEP = 8
GDN_CHUNK = 64
GDN_HEAD = 128
GDN_QK_HEADS = 8
GDN_V_PER_QK = 2
GDN_QK_PER_STEP = 8
GDN_INV_BLOCK = 16
C_ZS, C_XBC, C_QKV, C_ZG, C_GS, C_GG, N_MAIN = 0, 2048, 6144, 10240, 12288, 13312, 14336
N_SMALL = 128
ADAM_LR, ADAM_B1, ADAM_B2, ADAM_EPS, ADAM_WD, ADAM_STEP = 0.001, 0.9, 0.999, 1e-08, 0.01, 10
VMEM_LIMIT = 56 * 1024 * 1024
MM_WHOLE_K = 4096
MM_SPLIT_K = 2048
MM_WIDE_N = 2048
NEG_INF = float("-inf")

_NT = (((1,), (1,)), ((), ()))
_NN = (((1,), (0,)), ((), ()))
_TN = (((0,), (0,)), ((), ()))


def _params(*sem):
    return pltpu.CompilerParams(dimension_semantics=sem, vmem_limit_bytes=VMEM_LIMIT)


def _dot(a, b, dims=_NN):
    return lax.dot_general(a.astype(BF16), b.astype(BF16), dims, preferred_element_type=F32)


def _split(a):
    hi = a.astype(BF16)
    return hi, (a - hi.astype(F32)).astype(BF16)


def _dot3(a, b, dims=_NN):
    ah, al = _split(a)
    bh, bl = _split(b)
    d = lambda u, v: lax.dot_general(u, v, dims, preferred_element_type=F32)
    return d(ah, bh) + (d(ah, bl) + d(al, bh))


def _dot2(a, b, dims=_NN):
    ah, al = _split(a)
    bb = b.astype(BF16)
    d = lambda u: lax.dot_general(u, bb, dims, preferred_element_type=F32)
    return d(ah) + d(al)


def _sigmoid(x):
    return 0.5 * jnp.tanh(0.5 * x) + 0.5


def _silu(x):
    return x * _sigmoid(x)


def _dsilu(x):
    s = _sigmoid(x)
    return s * (1.0 + x * (1.0 - s))


def _softplus(x):
    return jnp.maximum(x, 0.0) + jnp.log1p(jnp.exp(-jnp.abs(x)))


def _iota(n, m, d):
    return lax.broadcasted_iota(jnp.int32, (n, m), d)


def _rowsum(x):
    return jnp.sum(x, axis=1, keepdims=True)


def _colsum(x):
    return jnp.sum(x, axis=0, keepdims=True)


def _total(x):
    return _rowsum(_colsum(x))


MXU_LANES = 128


def _parts(x, n):
    out = []
    for _ in range(n):
        p = x.astype(BF16)
        out.append(p)
        x = x - p.astype(F32)
    return out


def _sum_by(m01, x, dims=_NN, n=3):
    return sum(lax.dot_general(m01, p, dims, preferred_element_type=F32) for p in _parts(x, n))


def _row_col_sums(q):
    ones = jnp.ones((q.shape[0], MXU_LANES), BF16)
    p = q.astype(BF16)
    return (lax.dot_general(p, ones, _NN, preferred_element_type=F32)
            - lax.dot_general(p, ones, _TN, preferred_element_type=F32))[:, 0:1]


def _cumsum_forms(col, ii, jj):
    lower = jnp.where(ii >= jj, 1.0, 0.0).astype(BF16)
    cum_col = _sum_by(lower, jnp.broadcast_to(col, (col.shape[0], MXU_LANES)))[:, 0:1]
    cum_row = _colsum(jnp.where(ii <= jj, col, 0.0))
    return cum_col, cum_row


def _rev_cumsum_col(col, ii, jj):
    upper = jnp.where(ii <= jj, 1.0, 0.0).astype(BF16)
    return _sum_by(upper, jnp.broadcast_to(col, (col.shape[0], MXU_LANES)))[:, 0:1]


def _blk(dim, pref):
    return pref if dim % pref == 0 else dim


def _lockstep(gens):
    gens = list(gens)
    while gens:
        alive = []
        for g in gens:
            try:
                next(g)
                alive.append(g)
            except StopIteration:
                pass
        gens = alive


def _mm(a, b, M, N, K, *, mode, out_dtype, name, a_off=(0, 0), b_off=(0, 0), add=None, epi=None, extra=None,
        tm=1024, tn=1024):
    tm, tn = _blk(M, tm), _blk(N, tn)
    tk = K if K <= MM_WHOLE_K else _blk(K, MM_SPLIT_K)
    nk = K // tk
    if mode == "tn":
        a_spec = pl.BlockSpec((tk, tm), lambda i, j, k: (k + a_off[0] // tk, i + a_off[1] // tm))
        assert a_off[0] % tk == 0 and a_off[1] % tm == 0
    else:
        a_spec = pl.BlockSpec((tm, tk), lambda i, j, k: (i + a_off[0] // tm, k + a_off[1] // tk))
        assert a_off[0] % tm == 0 and a_off[1] % tk == 0
    if mode == "nt":
        b_spec = pl.BlockSpec((tn, tk), lambda i, j, k: (j + b_off[0] // tn, k + b_off[1] // tk))
        assert b_off[0] % tn == 0 and b_off[1] % tk == 0
    else:
        b_spec = pl.BlockSpec((tk, tn), lambda i, j, k: (k + b_off[0] // tk, j + b_off[1] // tn))
        assert b_off[0] % tk == 0 and b_off[1] % tn == 0
    dims = {"nn": _NN, "nt": _NT, "tn": _TN}[mode]
    o_spec = pl.BlockSpec((tm, tn), lambda i, j, k: (i, j))
    ins, in_specs = [a, b], [a_spec, b_spec]
    if add is not None:
        ins.append(add)
        in_specs.append(o_spec)
    if extra is not None:
        ins.append(extra)
        in_specs.append(o_spec)
    n_in = len(ins)
    if epi == "relu2":
        out_shape = (jax.ShapeDtypeStruct((M, N), BF16), jax.ShapeDtypeStruct((M, N), BF16))
        out_specs = (o_spec, o_spec)
    else:
        out_shape = jax.ShapeDtypeStruct((M, N), out_dtype)
        out_specs = o_spec

    def body(*refs):
        a_ref, b_ref = refs[0], refs[1]
        outs = refs[n_in:] if nk == 1 else refs[n_in:-1]

        def finish(r):
            pos = 2
            if add is not None:
                r = r + refs[pos][...]
                pos += 1
            if epi == "relu2":
                p = jnp.maximum(r, 0.0)
                outs[0][...] = p.astype(BF16)
                outs[1][...] = (p * p).astype(BF16)
            elif epi == "drelu2":
                outs[0][...] = (r * (2.0 * refs[pos][...].astype(F32))).astype(out_dtype)
            else:
                outs[0][...] = r.astype(out_dtype)

        if nk == 1:
            finish(_dot(a_ref[...], b_ref[...], dims))
            return
        acc = refs[-1]
        k = pl.program_id(2)

        @pl.when(k == 0)
        def _():
            acc[...] = jnp.zeros_like(acc)

        acc[...] += _dot(a_ref[...], b_ref[...], dims)

        @pl.when(k == nk - 1)
        def _():
            finish(acc[...])

    return pl.pallas_call(
        body, name=name, grid=(M // tm, N // tn, nk), in_specs=in_specs, out_specs=out_specs, out_shape=out_shape,
        scratch_shapes=[] if nk == 1 else [pltpu.VMEM((tm, tn), F32)],
        compiler_params=_params("parallel", "parallel", "arbitrary"))(*ins)


def _mm_rows(a, b, M, N, K, *, mode, name, extras, out_shapes, out_specs, epilogue, aliases=None, tm=512):
    tm = _blk(M, tm)
    a_spec = pl.BlockSpec((tm, K), lambda i: (i, 0))
    b_spec = pl.BlockSpec((K, N) if mode == "nn" else (N, K), lambda i: (0, 0))
    dims = _NN if mode == "nn" else _NT
    n_ex = len(extras)

    def body(a_ref, b_ref, *refs):
        epilogue(_dot(a_ref[...], b_ref[...], dims), refs[:n_ex], refs[n_ex:])

    return pl.pallas_call(
        body, name=name, grid=(M // tm,), in_specs=[a_spec, b_spec] + [sp for _, sp in extras],
        out_specs=tuple(out_specs), out_shape=tuple(out_shapes), input_output_aliases=aliases or {},
        compiler_params=_params("parallel"))(a, b, *[x for x, _ in extras])


def _row_spec(tb, d):
    return pl.BlockSpec((tb, d), lambda i: (i, 0))


def _vec_spec(d):
    return pl.BlockSpec((1, d), lambda i: (0, 0))


def _pre_fwd(x, w, sc, sh, name):
    S, Dm = x.shape
    tb = _blk(S, 512)

    def body(x_ref, w_ref, sc_ref, sh_ref, h_ref):
        xv = x_ref[...]
        r = lax.rsqrt(jnp.mean(xv * xv, axis=-1, keepdims=True) + EPS)
        h_ref[...] = ((xv * r * w_ref[...]) * (1.0 + sc_ref[...]) + sh_ref[...]).astype(BF16)

    return pl.pallas_call(
        body, name=name, grid=(S // tb,), in_specs=[_row_spec(tb, Dm)] + [_vec_spec(Dm)] * 3,
        out_specs=_row_spec(tb, Dm), out_shape=jax.ShapeDtypeStruct((S, Dm), BF16),
        compiler_params=_params("parallel"))(x, w, sc, sh)


def _final_fwd_bwd(x, y, w, g, target, name):
    S, Dm = x.shape
    tb = _blk(S, 512)
    nb = S // tb

    def body(x_ref, y_ref, w_ref, g_ref, t_ref, dx_ref, loss_ref, dy_ref, dg_ref, dw_ref, acc):
        i = pl.program_id(0)

        @pl.when(i == 0)
        def _():
            acc[...] = jnp.zeros_like(acc)
            dg_ref[...] = jnp.zeros_like(dg_ref)
            dw_ref[...] = jnp.zeros_like(dw_ref)

        yv = y_ref[...]
        r = lax.rsqrt(jnp.mean(yv * yv, axis=-1, keepdims=True) + EPS)
        yh = yv * r
        n = yh * w_ref[...]
        e = (x_ref[...] + g_ref[...] * n) - t_ref[...]
        dv = e * (1.0 / Dm)
        dx_ref[...] = dv
        acc[...] += _colsum(e * e)
        dg_ref[...] += _colsum(dv * n)
        dn = dv * g_ref[...]
        dw_ref[...] += _colsum(dn * yh)
        dyh = dn * w_ref[...]
        dy_ref[...] = (r * (dyh - yh * jnp.mean(dyh * yh, axis=-1, keepdims=True))).astype(BF16)

        @pl.when(i == nb - 1)
        def _():
            loss_ref[...] = (0.5 / Dm) * _rowsum(acc[...])

    row, vec = _row_spec(tb, Dm), _vec_spec(Dm)
    vec_shape = jax.ShapeDtypeStruct((1, Dm), F32)
    return pl.pallas_call(
        body, name=name, grid=(nb,), in_specs=[row, row, vec, vec, row],
        out_specs=(row, pl.BlockSpec((1, 1), lambda i: (0, 0)), row, vec, vec),
        out_shape=(jax.ShapeDtypeStruct((S, Dm), F32), jax.ShapeDtypeStruct((1, 1), F32),
                   jax.ShapeDtypeStruct((S, Dm), BF16), vec_shape, vec_shape),
        scratch_shapes=[pltpu.VMEM((1, Dm), F32)], compiler_params=_params("arbitrary"))(x, y, w, g, target)


def _post_bwd(dxo, y, w, g, name):
    S, Dm = y.shape
    tb = _blk(S, 512)

    def body(d_ref, y_ref, w_ref, g_ref, dy_ref, dg_ref, dw_ref):
        i = pl.program_id(0)

        @pl.when(i == 0)
        def _():
            dg_ref[...] = jnp.zeros_like(dg_ref)
            dw_ref[...] = jnp.zeros_like(dw_ref)

        yv, dv = y_ref[...], d_ref[...]
        r = lax.rsqrt(jnp.mean(yv * yv, axis=-1, keepdims=True) + EPS)
        yh = yv * r
        dg_ref[...] += _colsum(dv * (yh * w_ref[...]))
        dn = dv * g_ref[...]
        dw_ref[...] += _colsum(dn * yh)
        dyh = dn * w_ref[...]
        dy_ref[...] = (r * (dyh - yh * jnp.mean(dyh * yh, axis=-1, keepdims=True))).astype(BF16)

    return pl.pallas_call(
        body, name=name, grid=(S // tb,), in_specs=[_row_spec(tb, Dm)] * 2 + [_vec_spec(Dm)] * 2,
        out_specs=(_row_spec(tb, Dm), _vec_spec(Dm), _vec_spec(Dm)),
        out_shape=(jax.ShapeDtypeStruct((S, Dm), BF16), jax.ShapeDtypeStruct((1, Dm), F32),
                   jax.ShapeDtypeStruct((1, Dm), F32)),
        compiler_params=_params("arbitrary"))(dxo, y, w, g)


def _pre_bwd(dh, x, w, sc, dres, name):
    S, Dm = x.shape
    tb = _blk(S, 512)

    def body(dh_ref, x_ref, w_ref, sc_ref, dr_ref, dx_ref, dsh_ref, dsc_ref, dw_ref):
        i = pl.program_id(0)

        @pl.when(i == 0)
        def _():
            dsh_ref[...] = jnp.zeros_like(dsh_ref)
            dsc_ref[...] = jnp.zeros_like(dsc_ref)
            dw_ref[...] = jnp.zeros_like(dw_ref)

        xv, dv = x_ref[...], dh_ref[...]
        r = lax.rsqrt(jnp.mean(xv * xv, axis=-1, keepdims=True) + EPS)
        xh = xv * r
        one_sc = 1.0 + sc_ref[...]
        dsh_ref[...] += _colsum(dv)
        dsc_ref[...] += _colsum(dv * (xh * w_ref[...]))
        dw_ref[...] += _colsum(dv * one_sc * xh)
        dxh = dv * one_sc * w_ref[...]
        dx_ref[...] = dr_ref[...] + r * (dxh - xh * jnp.mean(dxh * xh, axis=-1, keepdims=True))

    vec = jax.ShapeDtypeStruct((1, Dm), F32)
    return pl.pallas_call(
        body, name=name, grid=(S // tb,),
        in_specs=[_row_spec(tb, Dm)] * 2 + [_vec_spec(Dm)] * 2 + [_row_spec(tb, Dm)],
        out_specs=(_row_spec(tb, Dm), _vec_spec(Dm), _vec_spec(Dm), _vec_spec(Dm)),
        out_shape=(jax.ShapeDtypeStruct((S, Dm), F32), vec, vec, vec),
        compiler_params=_params("arbitrary"))(dh, x, w, sc, dres)


D_PROJ_ANY = pl.BlockSpec(memory_space=pl.ANY)


def _gate_specs(tm, Dm):
    return (pl.BlockSpec((tm, Dm), lambda i: (i, C_GS // Dm)), pl.BlockSpec((tm, Dm), lambda i: (i, C_GG // Dm)))


def _gdn_up_merge(ogn, w_gu, ys, proj, name):
    S, K = ogn.shape
    Dm = ys.shape[1]
    tm = _blk(S, 512)
    row = _row_spec(tm, Dm)

    def epilogue(r, ex, out):
        ys_ref, gs_ref, gg_ref = ex
        out[0][...] = r
        out[1][...] = (_sigmoid(gs_ref[...].astype(F32)) * ys_ref[...]
                       + _sigmoid(gg_ref[...].astype(F32)) * r).astype(BF16)

    gs_spec, gg_spec = _gate_specs(tm, Dm)
    return _mm_rows(ogn, w_gu, S, Dm, K, mode="nn", name=name, tm=tm,
                    extras=[(ys, row), (proj, gs_spec), (proj, gg_spec)],
                    out_shapes=[jax.ShapeDtypeStruct((S, Dm), F32), jax.ShapeDtypeStruct((S, Dm), BF16)],
                    out_specs=[row, row], epilogue=epilogue)


def _mix_out_post_pre(merged, w_o, x, w_post, g, w_pre, sc, sh, name):
    S, Dm = x.shape
    tm = _blk(S, 512)
    row, vec = _row_spec(tm, Dm), _vec_spec(Dm)

    def epilogue(r, ex, out):
        x_ref, wpost_ref, g_ref, wpre_ref, sc_ref, sh_ref = ex
        out[0][...] = r
        rr = lax.rsqrt(jnp.mean(r * r, axis=-1, keepdims=True) + EPS)
        x1 = x_ref[...] + g_ref[...] * (r * rr * wpost_ref[...])
        out[1][...] = x1
        r1 = lax.rsqrt(jnp.mean(x1 * x1, axis=-1, keepdims=True) + EPS)
        out[2][...] = ((x1 * r1 * wpre_ref[...]) * (1.0 + sc_ref[...]) + sh_ref[...]).astype(BF16)

    return _mm_rows(merged, w_o, S, Dm, Dm, mode="nn", name=name, tm=tm,
                    extras=[(x, row), (w_post, vec), (g, vec), (w_pre, vec), (sc, vec), (sh, vec)],
                    out_shapes=[jax.ShapeDtypeStruct((S, Dm), F32), jax.ShapeDtypeStruct((S, Dm), F32),
                                jax.ShapeDtypeStruct((S, Dm), BF16)], out_specs=[row, row, row], epilogue=epilogue)


def _mix_out_dx_merge_bwd(dmo, w_o, ys, yg, proj, d_proj, name):
    S, Dm = ys.shape
    tm = _blk(S, 512)
    row = _row_spec(tm, Dm)

    def epilogue(d, ex, out):
        ys_ref, yg_ref, gs_ref, gg_ref, _ = ex
        ss, sg = _sigmoid(gs_ref[...].astype(F32)), _sigmoid(gg_ref[...].astype(F32))
        out[0][...] = (d * ss).astype(BF16)
        out[1][...] = (d * sg).astype(BF16)
        out[2][:, :Dm] = (d * ys_ref[...] * ss * (1.0 - ss)).astype(BF16)
        out[2][:, Dm:] = (d * yg_ref[...] * sg * (1.0 - sg)).astype(BF16)

    gs_spec, gg_spec = _gate_specs(tm, Dm)
    return _mm_rows(dmo, w_o, S, Dm, Dm, mode="nt", name=name, tm=tm,
                    extras=[(ys, row), (yg, row), (proj, gs_spec), (proj, gg_spec), (d_proj, D_PROJ_ANY)],
                    out_shapes=[jax.ShapeDtypeStruct((S, Dm), BF16), jax.ShapeDtypeStruct((S, Dm), BF16),
                                jax.ShapeDtypeStruct(d_proj.shape, BF16)],
                    out_specs=[row, row, pl.BlockSpec((tm, 2 * Dm), lambda i: (i, C_GS // (2 * Dm)))],
                    epilogue=epilogue, aliases={6: 2})


CONV_COLS = 128
CONV_BWD_ROWS = 256


def _taps_down(x):
    rows = _iota(x.shape[0], x.shape[1], 0)
    return [x] + [jnp.where(rows >= k, pltpu.roll(x, k, 0), 0.0) for k in range(1, CONV_K)]


def _conv_pre(taps, w_ref, b_ref):
    pre = taps[0] * w_ref[CONV_K - 1:CONV_K, :] + b_ref[...]
    for k in range(1, CONV_K):
        pre = pre + taps[k] * w_ref[CONV_K - 1 - k:CONV_K - k, :]
    return pre


def _conv_dx(dpre, w_ref):
    n = dpre.shape[0]
    rows = _iota(n, dpre.shape[1], 0)
    dx = dpre * w_ref[CONV_K - 1:CONV_K, :]
    for k in range(1, CONV_K):
        dx = dx + jnp.where(rows < n - k, pltpu.roll(dpre, n - k, 0), 0.0) * w_ref[CONV_K - 1 - k:CONV_K - k, :]
    return dx


def _conv_fwd(proj, w, b, name):
    S = proj.shape[0]
    n = w.shape[1]
    cb = CONV_COLS

    def body(x_ref, w_ref, b_ref, o_ref):
        o_ref[...] = _silu(_conv_pre(_taps_down(x_ref[...].astype(F32)), w_ref, b_ref)).astype(BF16)

    return pl.pallas_call(
        body, name=name, grid=(n // cb,),
        in_specs=[pl.BlockSpec((S, cb), lambda j: (0, j + C_XBC // cb)), pl.BlockSpec((CONV_K, cb), lambda j: (0, j)),
                  pl.BlockSpec((1, cb), lambda j: (0, j))],
        out_specs=pl.BlockSpec((S, cb), lambda j: (0, j)), out_shape=jax.ShapeDtypeStruct((S, n), BF16),
        compiler_params=_params("parallel"))(proj, w, b)


def _conv_bwd(dact, proj, w, b, col0, d_proj, name):
    S, n = dact.shape
    cb = CONV_COLS
    o = col0 // cb

    R, HALO = _blk(S, CONV_BWD_ROWS), 16
    n_chunks = S // R

    def body(d_ref, x_ref, w_ref, b_ref, _, dx_ref, dw_ref, db_ref):
        def chunk(r0, first, last, sums):
            lo, hi = (0 if first else HALO), (0 if last else HALO)
            start = r0 - lo if isinstance(r0, int) else pl.multiple_of(r0 - lo, HALO)
            xe = x_ref[pl.ds(start, lo + R + hi), :].astype(F32)
            rows = _iota(lo + R + hi, cb, 0)
            taps = [xe[lo:, :]]
            for k in range(1, CONV_K):
                t = pltpu.roll(xe, k, 0)
                taps.append((jnp.where(rows >= k, t, 0.0) if first else t)[lo:, :])
            dpre_e = d_ref[pl.ds(r0, R + hi), :].astype(F32) * _dsilu(_conv_pre(taps, w_ref, b_ref))
            dpre = dpre_e[0:R, :]
            db, dw = sums
            db = db + _colsum(dpre)
            dw = [dw[k] + _colsum(dpre * taps[k][0:R, :]) for k in range(CONV_K)]
            rows_e = _iota(R + hi, cb, 0)
            dx = dpre * w_ref[CONV_K - 1:CONV_K, :]
            for k in range(1, CONV_K):
                t = pltpu.roll(dpre_e, R + hi - k, 0)
                t = jnp.where(rows_e < R - k, t, 0.0) if last else t
                dx = dx + t[0:R, :] * w_ref[CONV_K - 1 - k:CONV_K - k, :]
            dx_ref[pl.ds(r0, R), :] = dx.astype(BF16)
            return db, dw

        zero = jnp.zeros((1, cb), F32)
        sums = chunk(0, True, n_chunks == 1, (zero, [zero] * CONV_K))
        if n_chunks > 2:
            def step(i, carry):
                db, dw = chunk(pl.multiple_of(i * R, R), False, False, (carry[0], list(carry[1:])))
                return (db,) + tuple(dw)
            carry = lax.fori_loop(1, n_chunks - 1, step, (sums[0],) + tuple(sums[1]))
            sums = (carry[0], list(carry[1:]))
        if n_chunks > 1:
            sums = chunk((n_chunks - 1) * R, False, True, sums)
        db_ref[...] = sums[0]
        for k in range(CONV_K):
            dw_ref[CONV_K - 1 - k:CONV_K - k, :] = sums[1][k]

    return pl.pallas_call(
        body, name=name, grid=(n // cb,),
        in_specs=[pl.BlockSpec((S, cb), lambda j: (0, j)), pl.BlockSpec((S, cb), lambda j: (0, j + o + C_XBC // cb)),
                  pl.BlockSpec((CONV_K, cb), lambda j: (0, j + o)), pl.BlockSpec((1, cb), lambda j: (0, j + o)),
                  D_PROJ_ANY],
        out_specs=(pl.BlockSpec((S, cb), lambda j: (0, j + o + C_XBC // cb)),
                   pl.BlockSpec((CONV_K, cb), lambda j: (0, j)), pl.BlockSpec((1, cb), lambda j: (0, j))),
        out_shape=(jax.ShapeDtypeStruct(d_proj.shape, BF16), jax.ShapeDtypeStruct((CONV_K, n), F32),
                   jax.ShapeDtypeStruct((1, n), F32)),
        input_output_aliases={4: 0}, compiler_params=_params("parallel"))(dact, proj, w, b, d_proj)


def _ssd_specs(L, order):
    G = SSM_GROUPS_PER_STEP
    W, N = G * SSM_GROUP_WIDTH, G * SSM_D_STATE
    x_spec = pl.BlockSpec((L, W), lambda g, c: (order(c), g))
    b_spec = pl.BlockSpec((L, N), lambda g, c: (order(c), 2048 // N + g))
    c_spec = pl.BlockSpec((L, N), lambda g, c: (order(c), 3072 // N + g))
    z_spec = pl.BlockSpec((L, W), lambda g, c: (order(c), C_ZS // W + g))
    dt_spec = pl.BlockSpec((G, L, SSM_HEADS_PER_GROUP), lambda g, c: (g, order(c), 0))
    p_spec = pl.BlockSpec((G, 3, SSM_HEADS_PER_GROUP), lambda g, c: (g, 0, 0))
    nw_spec = pl.BlockSpec((G, 1, SSM_GROUP_WIDTH), lambda g, c: (g, 0, 0))
    s_spec = pl.BlockSpec((G, 1, SSM_GROUP_WIDTH, SSM_D_STATE), lambda g, c: (g, order(c), 0, 0))
    return x_spec, b_spec, c_spec, z_spec, dt_spec, p_spec, nw_spec, s_spec


class _SsdGroup:
    def __init__(self, L):
        P, H, W = SSM_HEAD_DIM, SSM_HEADS_PER_GROUP, SSM_GROUP_WIDTH
        self.L = L
        self.ii, self.jj = _iota(L, L, 0), _iota(L, L, 1)
        self.lower = jnp.where(self.ii >= self.jj, 1.0, 0.0).astype(BF16)
        self.upper = jnp.where(self.ii <= self.jj, 1.0, 0.0).astype(BF16)
        self.lo = _iota(L, 2 * P, 1) < P
        self.lo_row = _iota(1, 2 * P, 1) < P
        bi, bj = _iota(W, W, 0), _iota(W, W, 1)
        self.block = jnp.where(bi // P == bj // P, 1.0, 0.0).astype(BF16)
        si, sj = _iota(2 * P, W, 0), _iota(2 * P, W, 1)
        self.pick = jnp.where(sj == si * P, 1.0, 0.0).astype(BF16)
        self.ones = jnp.ones((L, 2 * P), BF16)

    def spread(self, v4):
        R = v4.shape[0]
        lo = self.lo if R == self.L else self.lo_row
        b = lambda h: jnp.broadcast_to(v4[:, h:h + 1], (R, 2 * SSM_HEAD_DIM))
        return jnp.concatenate([jnp.where(lo, b(0), b(1)), jnp.where(lo, b(2), b(3))], axis=1)

    def gather4(self, v):
        return jnp.concatenate([v[:, h * SSM_HEAD_DIM:h * SSM_HEAD_DIM + 1] for h in range(SSM_HEADS_PER_GROUP)],
                               axis=1)

    def head_sums(self, z):
        return sum(lax.dot_general(p, self.block, _NN, preferred_element_type=F32) for p in _parts(z, 2))

    def pair_cols(self, full, pair):
        ps = full[:, pair * 128:(pair + 1) * 128]
        sw = pltpu.roll(ps, SSM_HEAD_DIM, 1)
        return jnp.where(self.lo, ps, sw), jnp.where(self.lo, sw, ps)

    def gates(self, dt4_raw, p):
        L = self.L
        dtr = self.spread(dt4_raw + p[0:1, :])
        dt = _softplus(dtr)
        A = self.spread(-jnp.exp(p[1:2, :]))
        acum = _sum_by(self.lower, dt * A)
        yield
        rows = _sum_by(self.pick, acum, _NT)
        yield
        a_last = acum[L - 1:L, :]
        cols = self.pair_cols(acum, 0) + self.pair_cols(acum, 1)
        decay, decay_t = [], []
        for h in range(SSM_HEADS_PER_GROUP):
            seg = cols[h] - rows[h:h + 1, :]
            decay.append(jnp.exp(jnp.where(self.ii >= self.jj, seg, NEG_INF)))
            decay_t.append(jnp.exp(jnp.where(self.jj >= self.ii, -seg, NEG_INF)))
        return dict(dtr=dtr, dt=dt, A=A, D=self.spread(p[2:3, :]), acum=acum, eac=jnp.exp(acum), a_last=a_last,
                    wdec=jnp.exp(a_last - acum), decay=decay, decay_t=decay_t,
                    ea_last=[jnp.exp(rows[h:h + 1, L - 1:L]) for h in range(SSM_HEADS_PER_GROUP)])


def _ssd_fwd(conv, proj, dt_raw, pvec, nw, name):
    S = conv.shape[0]
    L, P, N, H, W, G = SSM_CHUNK, SSM_HEAD_DIM, SSM_D_STATE, SSM_HEADS_PER_GROUP, SSM_GROUP_WIDTH, SSM_GROUPS_PER_STEP
    nc = S // L

    def body(x_ref, b_ref, c_ref, z_ref, dt_ref, p_ref, nw_ref, y_ref, yn_ref, s0_ref, state):
        c = pl.program_id(1)

        @pl.when(c == 0)
        def _():
            state[...] = jnp.zeros_like(state)

        k = _SsdGroup(L)

        def group(gi):
            gsl = slice(gi * W, (gi + 1) * W)
            Bm, Cm = b_ref[:, gi * N:(gi + 1) * N], c_ref[:, gi * N:(gi + 1) * N]
            x = x_ref[:, gsl].astype(F32)
            S0 = state[gsl, :]
            s0_ref[gi, 0] = S0
            CB = _dot(Cm, Bm, _NT)
            y_off = _dot(Cm, S0, _NT)
            t = yield from k.gates(dt_ref[gi], p_ref[gi])
            xdt = x * t["dt"]
            s_new = _dot(xdt * t["wdec"], Bm, _TN)
            y_diag = []
            for pair in range(H // 2):
                xp = xdt[:, pair * 128:(pair + 1) * 128]
                y_diag.append(jnp.where(k.lo, _dot(CB * t["decay"][2 * pair], xp),
                                        _dot(CB * t["decay"][2 * pair + 1], xp)))
            yield
            y = jnp.concatenate(y_diag, axis=1) + y_off * t["eac"]
            for h in range(H):
                hsl = slice(gi * W + h * P, gi * W + (h + 1) * P)
                state[hsl, :] = S0[h * P:(h + 1) * P, :] * t["ea_last"][h] + s_new[h * P:(h + 1) * P, :]
            y_ref[:, gsl] = y
            y2 = (y + t["D"] * x) * _silu(z_ref[:, gsl].astype(F32))
            r = lax.rsqrt(jnp.mean(y2 * y2, axis=-1, keepdims=True) + EPS)
            yn_ref[:, gsl] = (y2 * r * nw_ref[gi]).astype(BF16)

        _lockstep(group(gi) for gi in range(G))

    x_spec, b_spec, c_spec, z_spec, dt_spec, p_spec, nw_spec, s_spec = _ssd_specs(L, lambda c: c)
    y_spec = pl.BlockSpec((L, G * W), lambda g, c: (c, g))
    return pl.pallas_call(
        body, name=name, grid=(SSM_GROUPS // G, nc),
        in_specs=[x_spec, b_spec, c_spec, z_spec, dt_spec, p_spec, nw_spec],
        out_specs=(y_spec, y_spec, s_spec),
        out_shape=(jax.ShapeDtypeStruct((S, SSM_GROUPS * W), F32), jax.ShapeDtypeStruct((S, SSM_GROUPS * W), BF16),
                   jax.ShapeDtypeStruct((SSM_GROUPS, nc, W, N), F32)),
        scratch_shapes=[pltpu.VMEM((G * W, N), F32)],
        compiler_params=_params("parallel", "arbitrary"))(conv, conv, conv, proj, dt_raw, pvec, nw)


def _ssd_bwd(dyn, conv, proj, dt_raw, pvec, nw, y_ssd, states, d_proj, d_conv, name):
    S = conv.shape[0]
    L, P, N, H, W, G = SSM_CHUNK, SSM_HEAD_DIM, SSM_D_STATE, SSM_HEADS_PER_GROUP, SSM_GROUP_WIDTH, SSM_GROUPS_PER_STEP
    nc = S // L
    assert G == SSM_GROUPS
    CB0, CC0 = SSM_GROUPS * W, SSM_GROUPS * (W + N)

    def body(dyn_ref, x_ref, b_ref, c_ref, z_ref, dt_ref, p_ref, nw_ref, y_ref, s0_ref, _, _2,
             dact_ref, dz_ref, ddt_ref, dp_ref, dnw_ref, dstate):
        c = pl.program_id(1)

        @pl.when(c == 0)
        def _():
            dstate[...] = jnp.zeros_like(dstate)
            dp_ref[...] = jnp.zeros_like(dp_ref)
            dnw_ref[...] = jnp.zeros_like(dnw_ref)

        k = _SsdGroup(L)
        last = (_iota(L, 1, 0) == L - 1)

        def group(gi):
            gsl = slice(gi * W, (gi + 1) * W)
            Bm, Cm = b_ref[:, gi * N:(gi + 1) * N], c_ref[:, gi * N:(gi + 1) * N]
            x, z = x_ref[:, gsl].astype(F32), z_ref[:, gsl].astype(F32)
            S0, dS1 = s0_ref[gi, 0], dstate[gsl, :]
            CB = _dot(Cm, Bm, _NT)
            CBt = _dot(Bm, Cm, _NT)
            y_off_raw = _dot(Cm, S0, _NT)
            dXs_raw = _dot(Bm, dS1, _NT)
            t = yield from k.gates(dt_ref[gi], p_ref[gi])
            y1 = y_ref[:, gsl] + t["D"] * x
            sz = _silu(z)
            y2 = y1 * sz
            r = lax.rsqrt(jnp.mean(y2 * y2, axis=-1, keepdims=True) + EPS)
            y2h = y2 * r
            dyn_v = dyn_ref[:, gsl]
            dnw_ref[gi] += _colsum(dyn_v * y2h)
            dy2h = dyn_v * nw_ref[gi]
            dy2 = r * (dy2h - y2h * jnp.mean(dy2h * y2h, axis=-1, keepdims=True))
            dz_ref[:, gsl] = (dy2 * y1 * _dsilu(z)).astype(BF16)
            dY = dy2 * sz
            X = x * t["dt"]
            dYe = dY * t["eac"]
            dC_s = _dot(dYe, S0)
            dB_s = _dot(X * t["wdec"], dS1)
            dS_c = _dot(dYe, Cm, _TN)
            dXm, Gs, Gts = [], [], []
            for pair in range(H // 2):
                dYp, Xp = dY[:, pair * 128:(pair + 1) * 128], X[:, pair * 128:(pair + 1) * 128]
                dXm.append(jnp.where(k.lo, _dot(CBt * t["decay_t"][2 * pair], dYp),
                                     _dot(CBt * t["decay_t"][2 * pair + 1], dYp)))
                for mask in (k.lo, ~k.lo):
                    Gs.append(_dot(jnp.where(mask, dYp, 0.0), Xp, _NT))
                    Gts.append(_dot(jnp.where(mask, Xp, 0.0), dYp, _NT))
            yield
            dXs = dXs_raw * t["wdec"]
            dX = jnp.concatenate(dXm, axis=1) + dXs
            dCB, dCBt, q_sums = 0.0, 0.0, []
            for h in range(H):
                g_dec, gt_dec = Gs[h] * t["decay"][h], Gts[h] * t["decay_t"][h]
                dCB = dCB + g_dec
                dCBt = dCBt + gt_dec
                d = CB * g_dec - CBt * gt_dec
                q_sums.append(_dot(d, k.ones))
            q_f = jnp.concatenate([jnp.where(k.lo, q_sums[0], q_sums[1]), jnp.where(k.lo, q_sums[2], q_sums[3])],
                                  axis=1)
            x_dxs = X * dXs
            tot = [_total(dS1[h * P:(h + 1) * P, :] * S0[h * P:(h + 1) * P, :]) * t["ea_last"][h] for h in range(H)]
            tot_f = k.spread(jnp.concatenate(tot, axis=1))
            d_alast = k.head_sums(jnp.broadcast_to(_colsum(x_dxs), (8, W)))[0:1, :] + tot_f
            dacum = q_f + k.head_sums(dY * (y_off_raw * t["eac"]) - x_dxs) + jnp.where(last, d_alast, 0.0)
            dx_dt = k.head_sums(dX * x)
            d_skip = _colsum(k.head_sums(dY * x))
            for h in range(H):
                hsl = slice(gi * W + h * P, gi * W + (h + 1) * P)
                dstate[hsl, :] = t["ea_last"][h] * dS1[h * P:(h + 1) * P, :] + dS_c[h * P:(h + 1) * P, :]
            dc_s2 = _dot(dCB, Bm)
            db_s2 = _dot(dCBt, Cm)
            yield
            da = _sum_by(k.upper, dacum)
            yield
            ddt_raw = (da * t["A"] + dx_dt) * _sigmoid(t["dtr"])
            dact_ref[:, gsl] = (dX * t["dt"] + t["D"] * dY).astype(BF16)
            dact_ref[:, CC0 + gi * N:CC0 + (gi + 1) * N] = (dC_s + dc_s2).astype(BF16)
            dact_ref[:, CB0 + gi * N:CB0 + (gi + 1) * N] = (dB_s + db_s2).astype(BF16)
            ddt_ref[gi] = k.gather4(ddt_raw)
            dp_ref[gi] += k.gather4(jnp.concatenate([_colsum(ddt_raw), _colsum(da * t["dt"]) * t["A"], d_skip],
                                                    axis=0))

        _lockstep(group(gi) for gi in range(G))

    rev = lambda c: nc - 1 - c
    x_spec, b_spec, c_spec, z_spec, dt_spec, p_spec, nw_spec, s_spec = _ssd_specs(L, rev)
    y_spec = pl.BlockSpec((L, G * W), lambda g, c: (rev(c), g))
    half = d_conv.shape[1] // 2
    return pl.pallas_call(
        body, name=name, grid=(SSM_GROUPS // G, nc),
        in_specs=[y_spec, x_spec, b_spec, c_spec, z_spec, dt_spec, p_spec, nw_spec, y_spec, s_spec, D_PROJ_ANY,
                  D_PROJ_ANY],
        out_specs=(pl.BlockSpec((L, half), lambda g, c: (rev(c), 0)), z_spec, dt_spec, p_spec, nw_spec),
        out_shape=(jax.ShapeDtypeStruct(d_conv.shape, BF16), jax.ShapeDtypeStruct(d_proj.shape, BF16),
                   jax.ShapeDtypeStruct((SSM_GROUPS, S, H), F32), jax.ShapeDtypeStruct((SSM_GROUPS, 3, H), F32),
                   jax.ShapeDtypeStruct((SSM_GROUPS, 1, W), F32)),
        scratch_shapes=[pltpu.VMEM((G * W, N), F32)], input_output_aliases={10: 1, 11: 0},
        compiler_params=_params("parallel", "arbitrary"))(dyn, conv, conv, conv, proj, dt_raw, pvec, nw, y_ssd, states,
                                                          d_proj, d_conv)


def _unit_lower_inverse(A, ii, jj):
    eye = (ii == jj).astype(F32)
    same = (ii // GDN_INV_BLOCK) == (jj // GDN_INV_BLOCK)
    Ad = jnp.where(same, A, 0.0)
    Ao = A - Ad
    P2 = _dot3(Ad, Ad)
    yield
    P4, X = _dot(P2, P2), _dot3(eye - Ad, eye + P2)
    yield
    P8, X = _dot(P4, P4), X + _dot2(X, P4)
    yield
    X = X + _dot2(X, P8)
    yield
    Bm = _dot3(X, Ao)
    yield
    B2 = _dot3(Bm, Bm)
    yield
    Y = (eye - Bm) + B2 - _dot2(Bm, B2)
    yield
    T = _dot3(Y, X)
    yield
    return T


def _gdn_specs(L, order):
    G = GDN_QK_PER_STEP
    Hd, W = G * GDN_HEAD, G * GDN_V_PER_QK * GDN_HEAD
    q_spec = pl.BlockSpec((L, Hd), lambda h, c: (order(c), (C_QKV - C_XBC) // Hd + h))
    k_spec = pl.BlockSpec((L, Hd), lambda h, c: (order(c), (C_QKV - C_XBC + 1024) // Hd + h))
    v_spec = pl.BlockSpec((L, W), lambda h, c: (order(c), (C_QKV - C_XBC + 2048) // W + h))
    z_spec = pl.BlockSpec((L, W), lambda h, c: (order(c), C_ZG // W + h))
    ba_spec = pl.BlockSpec((G, L, GDN_V_PER_QK), lambda h, c: (h, order(c), 0))
    p_spec = pl.BlockSpec((G, 2, GDN_V_PER_QK), lambda h, c: (h, 0, 0))
    nw_spec = pl.BlockSpec((1, GDN_HEAD), lambda h, c: (0, 0))
    s_spec = pl.BlockSpec((G, 1, GDN_V_PER_QK * GDN_HEAD, GDN_HEAD), lambda h, c: (h, order(c), 0, 0))
    t_spec = pl.BlockSpec((G * GDN_V_PER_QK, 1, L, L), lambda h, c: (h, order(c), 0, 0))
    return q_spec, k_spec, v_spec, z_spec, ba_spec, p_spec, nw_spec, s_spec, t_spec


def _gdn_qk(qa, ka):
    rq = lax.rsqrt(_rowsum(qa * qa) + EPS)
    rk = lax.rsqrt(_rowsum(ka * ka) + EPS)
    q, k = qa * rq * (GDN_HEAD ** -0.5), ka * rk
    return dict(qa=qa, rq=rq, rk=rk, q=q, k=k, QK=_dot(q, k, _NT))


def _gdn_gates(qk, b_col, a_col, p, j, ii, jj):
    L = qk["q"].shape[0]
    rq, rk, q, k = qk["rq"], qk["rk"], qk["q"], qk["k"]
    sp_in = a_col + p[0:1, j:j + 1]
    neg_ea = -jnp.exp(p[1:2, j:j + 1])
    g = neg_ea * _softplus(sp_in)
    gcum, gcum_row = _cumsum_forms(g, ii, jj)
    beta = _sigmoid(b_col)
    yield
    Dm = jnp.exp(jnp.where(ii >= jj, gcum - gcum_row, NEG_INF))
    eg = jnp.exp(gcum)
    g_last = gcum[L - 1:L, :]
    wdec = jnp.exp(g_last - gcum)
    return dict(rq=rq, rk=rk, q=q, k=k, beta=beta, sp_in=sp_in, neg_ea=neg_ea, g=g, Dm=Dm, kbeta=k * beta, eg=eg,
                g_last=g_last, wdec=wdec, kdec=k * wdec)


def _gdn_fwd(conv, proj, b_raw, a_raw, pvec, nw, name):
    S = conv.shape[0]
    L, Hd, J, G = GDN_CHUNK, GDN_HEAD, GDN_V_PER_QK, GDN_QK_PER_STEP
    W = J * Hd
    nc = S // L

    def body(q_ref, k_ref, v_ref, z_ref, b_ref, a_ref, p_ref, nw_ref, o_ref, on_ref, s0_ref, t_ref, state):
        c = pl.program_id(1)

        @pl.when(c == 0)
        def _():
            state[...] = jnp.zeros_like(state)

        ii, jj = _iota(L, L, 0), _iota(L, L, 1)
        for hq in range(G):
            s0_ref[hq, 0] = state[hq * W:(hq + 1) * W, :]
        qks = [_gdn_qk(q_ref[:, hq * Hd:(hq + 1) * Hd].astype(F32), k_ref[:, hq * Hd:(hq + 1) * Hd].astype(F32))
               for hq in range(G)]

        def head(hq, j):
            hd = hq * J + j
            sl = slice(hd * Hd, (hd + 1) * Hd)
            t = yield from _gdn_gates(qks[hq], b_ref[hq][:, j:j + 1], a_ref[hq][:, j:j + 1], p_ref[hq], j, ii, jj)
            KK = _dot(t["kbeta"], t["k"], _NT)
            QK = qks[hq]["QK"]
            yield
            T = yield from _unit_lower_inverse(jnp.where(ii > jj, KK * t["Dm"], 0.0), ii, jj)
            t_ref[hd, 0] = T
            S0 = state[sl, :]
            U = _dot2(T, v_ref[:, sl].astype(F32) * t["beta"])
            Wm = _dot2(T, t["kbeta"] * t["eg"])
            o_inter = _dot(t["q"] * t["eg"], S0)
            yield
            Vn = U - _dot(Wm, S0)
            yield
            o = o_inter + _dot(QK * t["Dm"], Vn)
            s_new = _dot(t["kdec"], Vn, _TN)
            yield
            state[sl, :] = S0 * jnp.exp(t["g_last"]) + s_new
            o_ref[:, sl] = o
            r = lax.rsqrt(jnp.mean(o * o, axis=-1, keepdims=True) + EPS)
            on_ref[:, sl] = ((o * r * nw_ref[...]) * _silu(z_ref[:, sl].astype(F32))).astype(BF16)

        _lockstep(head(hq, j) for hq in range(G) for j in range(J))

    q_spec, k_spec, v_spec, z_spec, ba_spec, p_spec, nw_spec, s_spec, t_spec = _gdn_specs(L, lambda c: c)
    o_spec = pl.BlockSpec((L, G * W), lambda h, c: (c, h))
    return pl.pallas_call(
        body, name=name, grid=(GDN_QK_HEADS // G, nc),
        in_specs=[q_spec, k_spec, v_spec, z_spec, ba_spec, ba_spec, p_spec, nw_spec],
        out_specs=(o_spec, o_spec, s_spec, t_spec),
        out_shape=(jax.ShapeDtypeStruct((S, GDN_QK_HEADS * W), F32), jax.ShapeDtypeStruct((S, GDN_QK_HEADS * W), BF16),
                   jax.ShapeDtypeStruct((GDN_QK_HEADS, nc, W, Hd), F32),
                   jax.ShapeDtypeStruct((GDN_QK_HEADS * J, nc, L, L), F32)),
        scratch_shapes=[pltpu.VMEM((G * W, Hd), F32)],
        compiler_params=_params("parallel", "arbitrary"))(conv, conv, conv, proj, b_raw, a_raw, pvec, nw)


def _gdn_bwd(don, conv, proj, b_raw, a_raw, pvec, nw, o_pre, states, t_inv, d_proj, d_conv, name):
    S = conv.shape[0]
    L, Hd, J, G = GDN_CHUNK, GDN_HEAD, GDN_V_PER_QK, GDN_QK_PER_STEP
    W = J * Hd
    nc = S // L
    assert G == GDN_QK_HEADS
    CK0, CV0 = GDN_QK_HEADS * Hd, 2 * GDN_QK_HEADS * Hd

    def body(don_ref, q_ref, k_ref, v_ref, z_ref, b_ref, a_ref, p_ref, nw_ref, o_ref, s0_ref, t_ref, _, _2,
             dact_ref, dz_ref, db_ref, da_ref, dp_ref, dnw_ref, dstate):
        c = pl.program_id(1)

        @pl.when(c == 0)
        def _():
            dstate[...] = jnp.zeros_like(dstate)
            dp_ref[...] = jnp.zeros_like(dp_ref)
            dnw_ref[...] = jnp.zeros_like(dnw_ref)

        ii, jj = _iota(L, L, 0), _iota(L, L, 1)
        last = (_iota(L, 1, 0) == L - 1)
        res = {}
        qks = [_gdn_qk(q_ref[:, hq * Hd:(hq + 1) * Hd].astype(F32), k_ref[:, hq * Hd:(hq + 1) * Hd].astype(F32))
               for hq in range(G)]

        def head(hq, j):
            hd = hq * J + j
            sl = slice(hd * Hd, (hd + 1) * Hd)
            qa = qks[hq]["qa"]
            t = yield from _gdn_gates(qks[hq], b_ref[hq][:, j:j + 1], a_ref[hq][:, j:j + 1], p_ref[hq], j, ii, jj)
            q, k, beta, eg, Dm, kbeta, kdec = (t[nm] for nm in ("q", "k", "beta", "eg", "Dm", "kbeta", "kdec"))
            T = t_ref[hd, 0]
            v, z, o = v_ref[:, sl].astype(F32), z_ref[:, sl].astype(F32), o_ref[:, sl]
            S0, dS1 = s0_ref[hq, 0, j * Hd:(j + 1) * Hd, :], dstate[sl, :]
            sz = _silu(z)
            r = lax.rsqrt(jnp.mean(o * o, axis=-1, keepdims=True) + EPS)
            oh = o * r
            d_on = don_ref[:, sl]
            dz_ref[:, sl] = (d_on * (oh * nw_ref[...]) * _dsilu(z)).astype(BF16)
            dn = d_on * sz
            dnw_part = _colsum(dn * oh)
            doh = dn * nw_ref[...]
            dO = r * (doh - oh * jnp.mean(doh * oh, axis=-1, keepdims=True))
            Rw = kbeta * eg
            qe = q * eg
            U = _dot2(T, v * beta)
            Wm = _dot2(T, Rw)
            KK = _dot(kbeta, k, _NT)
            QK = qks[hq]["QK"]
            o_inter = _dot(qe, S0)
            dq_s = _dot(dO, S0, _NT)
            dS_q = _dot(qe, dO, _TN)
            yield
            Am = jnp.where(ii > jj, KK * Dm, 0.0)
            Pm = QK * Dm
            Vn = U - _dot(Wm, S0)
            dVn_s = _dot(kdec, dS1)
            yield
            dVn = _dot(Pm, dO, _TN) + dVn_s
            dP = _dot(dO, Vn, _NT)
            dKd = _dot(Vn, dS1, _NT)
            yield
            dQK = dP * Dm
            dq = _dot(dQK, k) + dq_s * eg
            dk = _dot(dQK, q, _TN) + dKd * t["wdec"]
            dstate[sl, :] = jnp.exp(t["g_last"]) * dS1 + dS_q - _dot(Wm, dVn, _TN)
            dW = -_dot(dVn, S0, _NT)
            dRu = _dot2(T, dVn, _TN)
            yield
            dRw = _dot2(T, dW, _TN)
            dA_u = _dot(dRu, U, _NT)
            yield
            dA = jnp.where(ii > jj, -(dA_u + _dot(dRw, Wm, _NT)), 0.0)
            yield
            dKK = dA * Dm
            dkbeta = _dot(dKK, k) + dRw * eg
            dk = dk + _dot(dKK, kbeta, _TN)
            yield
            dk = dk + dkbeta * beta
            dbeta = _rowsum(dkbeta * k + dRu * v)
            dact_ref[:, CV0 + hd * Hd:CV0 + (hd + 1) * Hd] = (dRu * beta).astype(BF16)
            Q = dA * Am + dP * Pm
            kd = dKd * kdec
            d_glast = _total(kd) + jnp.exp(t["g_last"]) * _total(dS1 * S0)
            q_sums = _row_col_sums(Q)
            rest = _rowsum(dRw * Rw + dO * o_inter - kd) + jnp.where(last, d_glast, 0.0)
            yield
            dg = _rev_cumsum_col(q_sums + rest, ii, jj)
            yield
            da_raw = dg * t["neg_ea"] * _sigmoid(t["sp_in"])
            res[hq, j] = dict(dq=dq, dk=dk, db=dbeta * beta * (1.0 - beta), da=da_raw, d_bias=_colsum(da_raw),
                              d_alog=_colsum(dg * t["g"]), dnw=dnw_part, rq=t["rq"], rk=t["rk"], k=k, qh=qa * t["rq"])

        _lockstep(head(hq, j) for hq in range(G) for j in range(J))
        for hq in range(G):
            parts = [res[hq, j] for j in range(J)]
            hsl = slice(hq * Hd, (hq + 1) * Hd)
            p0 = parts[0]
            dqh = sum(pt["dq"] for pt in parts) * (GDN_HEAD ** -0.5)
            dkn = sum(pt["dk"] for pt in parts)
            dact_ref[:, hsl] = (p0["rq"] * (dqh - p0["qh"] * _rowsum(dqh * p0["qh"]))).astype(BF16)
            dact_ref[:, CK0 + hq * Hd:CK0 + (hq + 1) * Hd] = (
                p0["rk"] * (dkn - p0["k"] * _rowsum(dkn * p0["k"]))).astype(BF16)
            db_ref[hq] = jnp.concatenate([pt["db"] for pt in parts], axis=1)
            da_ref[hq] = jnp.concatenate([pt["da"] for pt in parts], axis=1)
            dp_ref[hq] += jnp.concatenate([jnp.concatenate([pt["d_bias"] for pt in parts], axis=1),
                                           jnp.concatenate([pt["d_alog"] for pt in parts], axis=1)], axis=0)
            dnw_ref[hq] += sum(pt["dnw"] for pt in parts)

    rev = lambda c: nc - 1 - c
    q_spec, k_spec, v_spec, z_spec, ba_spec, p_spec, nw_spec, s_spec, t_spec = _gdn_specs(L, rev)
    o_spec = pl.BlockSpec((L, G * W), lambda h, c: (rev(c), h))
    dnw_spec = pl.BlockSpec((G, 1, Hd), lambda h, c: (h, 0, 0))
    half = d_conv.shape[1] // 2
    return pl.pallas_call(
        body, name=name, grid=(GDN_QK_HEADS // G, nc),
        in_specs=[o_spec, q_spec, k_spec, v_spec, z_spec, ba_spec, ba_spec, p_spec, nw_spec, o_spec, s_spec, t_spec,
                  D_PROJ_ANY, D_PROJ_ANY],
        out_specs=(pl.BlockSpec((L, half), lambda h, c: (rev(c), 1)), z_spec, ba_spec, ba_spec, p_spec, dnw_spec),
        out_shape=(jax.ShapeDtypeStruct(d_conv.shape, BF16), jax.ShapeDtypeStruct(d_proj.shape, BF16),
                   jax.ShapeDtypeStruct((GDN_QK_HEADS, S, J), F32), jax.ShapeDtypeStruct((GDN_QK_HEADS, S, J), F32),
                   jax.ShapeDtypeStruct((GDN_QK_HEADS, 2, J), F32), jax.ShapeDtypeStruct((GDN_QK_HEADS, 1, Hd), F32)),
        scratch_shapes=[pltpu.VMEM((G * W, Hd), F32)], input_output_aliases={12: 1, 13: 0},
        compiler_params=_params("parallel", "arbitrary"))(don, conv, conv, conv, proj, b_raw, a_raw, pvec, nw, o_pre,
                                                          states, t_inv, d_proj, d_conv)


def _ada_fwd(c_all, w_loc, b_loc, name):
    n = w_loc.shape[1]

    def body(c_ref, w_ref, b_ref, o_ref):
        o_ref[...] = _dot3(_silu(c_ref[...]), w_ref[...]) + b_ref[...]

    return pl.pallas_call(body, name=name, out_shape=jax.ShapeDtypeStruct((N_DEV, n), F32),
                          compiler_params=pltpu.CompilerParams(vmem_limit_bytes=VMEM_LIMIT))(c_all, w_loc, b_loc)


def _ada_bwd(c_all_t, dmod_cols, name):
    Dm, n = c_all_t.shape[0], dmod_cols.shape[1]

    def body(c_ref, d_ref, o_ref):
        ca = _silu(c_ref[...])
        acc = ca[:, 0:1] * d_ref[0:1, :]
        for i in range(1, N_DEV):
            acc = acc + ca[:, i:i + 1] * d_ref[i:i + 1, :]
        o_ref[...] = acc

    return pl.pallas_call(body, name=name, out_shape=jax.ShapeDtypeStruct((Dm, n), F32),
                          compiler_params=pltpu.CompilerParams(vmem_limit_bytes=VMEM_LIMIT))(c_all_t, dmod_cols)


ADAM_BLOCK_BYTES = 12 * 1024 * 1024


def _adam(contrib, w, m, v, name):
    n, R, C = contrib.shape
    tr = R
    while tr % 16 == 0 and (n + 7) * tr * C * 4 > ADAM_BLOCK_BYTES:
        tr //= 2

    def body(c_ref, w_ref, m_ref, v_ref, g_ref, d_ref, nm_ref, nv_ref):
        g = c_ref[0].astype(F32)
        for i in range(1, n):
            g = g + c_ref[i].astype(F32)
        nm = ADAM_B1 * m_ref[...] + (1.0 - ADAM_B1) * g
        nv = ADAM_B2 * v_ref[...] + (1.0 - ADAM_B2) * (g * g)
        m_hat = nm / (1.0 - ADAM_B1 ** ADAM_STEP)
        v_hat = nv / (1.0 - ADAM_B2 ** ADAM_STEP)
        g_ref[...] = g
        d_ref[...] = -ADAM_LR * (m_hat / (jnp.sqrt(v_hat) + ADAM_EPS) + ADAM_WD * w_ref[...])
        nm_ref[...] = nm
        nv_ref[...] = nv

    spec = pl.BlockSpec((tr, C), lambda i: (i, 0))
    shp = jax.ShapeDtypeStruct((R, C), F32)
    return pl.pallas_call(
        body, name=name, grid=(R // tr,), in_specs=[pl.BlockSpec((n, tr, C), lambda i: (0, i, 0)), spec, spec, spec],
        out_specs=(spec,) * 4, out_shape=(shp,) * 4, compiler_params=_params("parallel"))(contrib, w, m, v)


def _exchange(arrays, modes, name, chips=False):
    n = len(arrays)
    out_shape = tuple(jax.ShapeDtypeStruct((N_DEV,) + a.shape if md == "gather" else a.shape, a.dtype)
                      for a, md in zip(arrays, modes))

    def body(*refs):
        ins, outs = refs[:n], refs[n:2 * n]
        send_sems, recv_sems, loc_sems = refs[2 * n:]
        me, peers = _peer_table(chips)

        def src(k, slot):
            return ins[k] if modes[k] == "gather" else ins[k].at[slot]

        def remote(k, m, to_slot, land_slot):
            return pltpu.make_async_remote_copy(
                src_ref=src(k, to_slot), dst_ref=outs[k].at[land_slot], send_sem=send_sems.at[k, m],
                recv_sem=recv_sems.at[k, m], device_id=peers[m][0], device_id_type=pl.DeviceIdType.MESH)

        local = [pltpu.make_async_copy(src(k, me), outs[k].at[me], loc_sems.at[k]) for k in range(n)]
        for cp in local:
            cp.start()
        sends = [remote(k, m, peers[m][1], me) for m in range(len(peers)) for k in range(n)]
        for cp in sends:
            cp.start()
        for m in range(len(peers)):
            for k in range(n):
                remote(k, m, peers[m][1], peers[m][1]).wait_recv()
        for cp in sends:
            cp.wait_send()
        for cp in local:
            cp.wait()

    any_spec = pl.BlockSpec(memory_space=pl.ANY)
    return pl.pallas_call(
        body, name=name, in_specs=[any_spec] * n, out_specs=(any_spec,) * n, out_shape=out_shape,
        scratch_shapes=[pltpu.SemaphoreType.DMA((n, N_DEV - 1)), pltpu.SemaphoreType.DMA((n, N_DEV - 1)),
                        pltpu.SemaphoreType.DMA((n,))])(*arrays)


def _gather_two_level(arrays, name):
    n = len(arrays)
    out_shape = tuple(jax.ShapeDtypeStruct((N_DEV,) + a.shape, a.dtype) for a in arrays)

    def body(*refs):
        ins, outs = refs[:n], refs[n:2 * n]
        send_sems, recv_sems, loc_sems = refs[2 * n:]
        ix, iy, ic = lax.axis_index("x"), lax.axis_index("y"), lax.axis_index("c")
        lin = lambda px, py, pc: 4 * px + 2 * py + pc
        me, sib = lin(ix, iy, ic), (ix, iy, 1 - ic)
        chips = [(1 - ix, iy), (ix, 1 - iy), (1 - ix, 1 - iy)]

        def copy(k, s, block, to, src=None):
            return pltpu.make_async_remote_copy(
                src_ref=outs[k].at[block] if src is None else src, dst_ref=outs[k].at[block],
                send_sem=send_sems.at[k, s], recv_sem=recv_sems.at[k, s], device_id=to,
                device_id_type=pl.DeviceIdType.MESH)

        local = [pltpu.make_async_copy(ins[k], outs[k].at[me], loc_sems.at[k]) for k in range(n)]
        for cp in local:
            cp.start()
        first = [copy(k, 1 + j, me, (cx, cy, ic), src=ins[k]) for j, (cx, cy) in enumerate(chips) for k in range(n)]
        first += [copy(k, 0, me, sib, src=ins[k]) for k in range(n)]
        for cp in first:
            cp.start()
        passed = []
        for j, (cx, cy) in enumerate(chips):
            for k in range(n):
                copy(k, 1 + j, lin(cx, cy, ic), sib).wait_recv()
                passed.append(copy(k, 4 + j, lin(cx, cy, ic), sib))
                passed[-1].start()
        for k in range(n):
            copy(k, 0, lin(*sib), sib).wait_recv()
            for j, (cx, cy) in enumerate(chips):
                copy(k, 4 + j, lin(cx, cy, 1 - ic), sib).wait_recv()
        for cp in first + passed:
            cp.wait_send()
        for cp in local:
            cp.wait()

    any_spec = pl.BlockSpec(memory_space=pl.ANY)
    return pl.pallas_call(
        body, name=name, in_specs=[any_spec] * n, out_specs=(any_spec,) * n, out_shape=out_shape,
        scratch_shapes=[pltpu.SemaphoreType.DMA((n, N_DEV - 1)), pltpu.SemaphoreType.DMA((n, N_DEV - 1)),
                        pltpu.SemaphoreType.DMA((n,))])(*arrays)


def _peer_table(chips=False):
    ix, iy, ic = lax.axis_index("x"), lax.axis_index("y"), lax.axis_index("c")
    peers = []
    for m in ((2, 4, 6) if chips else range(1, N_DEV)):
        px = 1 - ix if m & 4 else ix
        py = 1 - iy if m & 2 else iy
        pc = 1 - ic if m & 1 else ic
        peers.append(((px, py, pc), 2 * px + py if chips else 4 * px + 2 * py + pc))
    return (2 * ix + iy if chips else 4 * ix + 2 * iy + ic), peers


def _exchange_start(arrays, modes, after, name, chips=False):
    n = len(arrays)
    land_shapes = [(N_DEV,) + a.shape if md == "gather" else a.shape for a, md in zip(arrays, modes)]

    def body(*refs):
        ins, lands = refs[:n], refs[n:2 * n]
        send_sems, recv_sems = refs[2 * n + 1], refs[2 * n + 2]
        token = refs[-1]
        me, peers = _peer_table(chips)

        def src(k, slot):
            return ins[k] if modes[k] == "gather" else ins[k].at[slot]

        for peer, slot in peers:
            for k in range(n):
                pltpu.make_async_remote_copy(
                    src_ref=src(k, slot), dst_ref=lands[k].at[me], send_sem=send_sems, recv_sem=recv_sems,
                    device_id=peer, device_id_type=pl.DeviceIdType.MESH).start()
        token[...] = jnp.zeros_like(token)

    hbm = pl.BlockSpec(memory_space=pltpu.HBM)
    sem = pl.BlockSpec(memory_space=pltpu.SEMAPHORE)
    sem_shape = pltpu.SemaphoreType.DMA(())
    operands = [pltpu.with_memory_space_constraint(a, pltpu.HBM) for a in arrays]
    operands += [pltpu.with_memory_space_constraint(lax.empty(s, a.dtype), pltpu.HBM)
                 for s, a in zip(land_shapes, arrays)]
    out = pl.pallas_call(
        body, name=name,
        out_shape=(sem_shape, sem_shape) + tuple(pltpu.HBM(a.shape, a.dtype) for a in arrays)
        + tuple(pltpu.HBM(s, a.dtype) for s, a in zip(land_shapes, arrays)) + (jax.ShapeDtypeStruct((8, 128), F32),),
        in_specs=[hbm] * (2 * n) + [pl.BlockSpec(memory_space=pl.ANY)],
        out_specs=(sem, sem) + (hbm,) * (2 * n) + (pl.BlockSpec(memory_space=pltpu.VMEM),),
        input_output_aliases={i: 2 + i for i in range(2 * n)},
        compiler_params=pltpu.CompilerParams(has_side_effects=pltpu.SideEffectType.DATAFLOW_SIDE_EFFECTING))(
            *operands, after)
    return out[0], out[1], out[2:2 + n], out[2 + n:2 + 2 * n], out[-1]


def _exchange_wait(started, modes, after, name, chips=False):
    send_sems, recv_sems, sent, lands, _ = started
    n = len(sent)

    def body(*refs):
        ins, zones = refs[:n], refs[n:2 * n]
        send_ref, recv_ref = refs[2 * n], refs[2 * n + 1]
        _, peers = _peer_table(chips)

        def src(k, slot):
            return ins[k] if modes[k] == "gather" else ins[k].at[slot]

        for peer, slot in peers:
            for k in range(n):
                cp = pltpu.make_async_remote_copy(
                    src_ref=src(k, slot), dst_ref=zones[k].at[slot], send_sem=send_ref, recv_sem=recv_ref,
                    device_id=peer, device_id_type=pl.DeviceIdType.MESH)
                cp.wait_send()
                cp.wait_recv()

    hbm = pl.BlockSpec(memory_space=pltpu.HBM)
    sem = pl.BlockSpec(memory_space=pltpu.SEMAPHORE)
    out = pl.pallas_call(
        body, name=name,
        out_shape=tuple(pltpu.HBM(a.shape, a.dtype) for a in sent) + tuple(pltpu.HBM(a.shape, a.dtype) for a in lands),
        in_specs=[hbm] * (2 * n) + [sem, sem, pl.BlockSpec(memory_space=pl.ANY)], out_specs=(hbm,) * (2 * n),
        input_output_aliases={i: i for i in range(2 * n)},
        compiler_params=pltpu.CompilerParams(has_side_effects=pltpu.SideEffectType.DATAFLOW_SIDE_EFFECTING))(
            *sent, *lands, send_sems, recv_sems, after)
    ix, iy, ic = lax.axis_index("x"), lax.axis_index("y"), lax.axis_index("c")
    me = 2 * ix + iy if chips else 4 * ix + 2 * iy + ic
    filled = []
    for k in range(n):
        own = sent[k] if modes[k] == "gather" else lax.dynamic_index_in_dim(sent[k], me, axis=0, keepdims=False)
        filled.append(lax.dynamic_update_index_in_dim(out[n + k], own, me, axis=0))
    return filled


def _swap_sibling(to_c0, to_c1, name):
    def body(c0_ref, c1_ref, out_ref, send_sem, recv_sem):
        ix, iy, ic = lax.axis_index("x"), lax.axis_index("y"), lax.axis_index("c")

        def copy(src):
            return pltpu.make_async_remote_copy(src_ref=src, dst_ref=out_ref, send_sem=send_sem, recv_sem=recv_sem,
                                                device_id=(ix, iy, 1 - ic), device_id_type=pl.DeviceIdType.MESH)

        @pl.when(ic == 0)
        def _():
            copy(c1_ref).start()

        @pl.when(ic == 1)
        def _():
            copy(c0_ref).start()

        copy(c0_ref).wait()

    any_spec = pl.BlockSpec(memory_space=pl.ANY)
    return pl.pallas_call(body, name=name, in_specs=[any_spec, any_spec], out_specs=any_spec,
                          out_shape=jax.ShapeDtypeStruct(to_c0.shape, to_c0.dtype),
                          scratch_shapes=[pltpu.SemaphoreType.DMA, pltpu.SemaphoreType.DMA])(to_c0, to_c1)


def _add_pair(to_c0, to_c1, got, name):
    n, R, C = got.shape
    tr = _blk(R, 256)

    def body(c0_ref, c1_ref, got_ref, o_ref):
        ic = lax.axis_index("c")

        @pl.when(ic == 0)
        def _():
            o_ref[...] = (c0_ref[...].astype(F32) + got_ref[...].astype(F32)).astype(o_ref.dtype)

        @pl.when(ic == 1)
        def _():
            o_ref[...] = (c1_ref[...].astype(F32) + got_ref[...].astype(F32)).astype(o_ref.dtype)

    spec = pl.BlockSpec((1, tr, C), lambda i, j: (i, j, 0))
    return pl.pallas_call(body, name=name, grid=(n, R // tr), in_specs=[spec, spec, spec], out_specs=spec,
                          out_shape=jax.ShapeDtypeStruct(got.shape, got.dtype),
                          compiler_params=_params("parallel", "parallel"))(to_c0, to_c1, got)


W_IN_SPLITS = (0, 2048, 6144, 6176, 10272, 12320, 12336, 12352, 13376, 14400)
N_REPLICATED = 16640
REPLICATED = ("b_ada", "norm_mix_pre", "norm_mix_post", "ssm_conv_b", "ssm_dt_bias", "ssm_A_log", "ssm_D",
              "ssm_norm_w", "gdn_dt_bias", "gdn_A_log", "gdn_norm_w", "norm_mlp_pre", "norm_mlp_post")
WEIGHTS = ("w_ada", "b_ada", "norm_mix_pre", "norm_mix_post", "w_in", "ssm_conv_w", "ssm_conv_b", "ssm_dt_bias",
           "ssm_A_log", "ssm_D", "ssm_norm_w", "gdn_conv_w", "gdn_dt_bias", "gdn_A_log", "gdn_norm_w", "w_ssm_up",
           "w_gdn_up", "w_out", "norm_mlp_pre", "norm_mlp_post", "w_mlp_up", "w_mlp_down")


def _cols_of_shards(g, a, b):
    width, pieces = g.shape[2], []
    while a < b:
        i = a // width
        hi = min(b, (i + 1) * width)
        pieces.append(g[i][:, a - i * width:hi - i * width])
        a = hi
    return pieces


ORIG_SEGMENTS = ((0, 6144, "main", 0), (6144, 6176, "small", 0), (6176, 12320, "main", 6144),
                 (12320, 12352, "small", 32), (12352, 14400, "main", 12288))


def _orig_cols(main_cols, small_cols, a, b):
    pieces = []
    for s0, s1, which, off in ORIG_SEGMENTS:
        lo, hi = max(a, s0), min(b, s1)
        if lo < hi:
            pieces.append((main_cols if which == "main" else small_cols)[:, off + lo - s0:off + hi - s0])
    return jnp.concatenate(pieces, axis=1)


def _by_cols(t):
    return t.transpose(1, 0, 2).reshape(t.shape[1], N_DEV * t.shape[2])


def _to_col_shards(t):
    R, C8 = t.shape
    return t.reshape(R, N_DEV, C8 // N_DEV).transpose(1, 0, 2)


def _heads_first(t, groups):
    S = t.shape[0]
    return t.reshape(S, groups, t.shape[1] // groups).transpose(1, 0, 2)


def _heads_last(t):
    return t.transpose(1, 0, 2).reshape(t.shape[1], t.shape[0] * t.shape[2])


def kernel(x, c, w_ada, b_ada, norm_mix_pre, norm_mix_post, w_in, ssm_conv_w, ssm_conv_b, ssm_dt_bias, ssm_A_log, ssm_D, ssm_norm_w, gdn_conv_w, gdn_dt_bias, gdn_A_log, gdn_norm_w, w_ssm_up, w_gdn_up, w_out, norm_mlp_pre, norm_mlp_post, w_mlp_up, w_mlp_down, loss_target, m_w_ada, m_b_ada, m_norm_mix_pre, m_norm_mix_post, m_w_in, m_ssm_conv_w, m_ssm_conv_b, m_ssm_dt_bias, m_ssm_A_log, m_ssm_D, m_ssm_norm_w, m_gdn_conv_w, m_gdn_dt_bias, m_gdn_A_log, m_gdn_norm_w, m_w_ssm_up, m_w_gdn_up, m_w_out, m_norm_mlp_pre, m_norm_mlp_post, m_w_mlp_up, m_w_mlp_down, v_w_ada, v_b_ada, v_norm_mix_pre, v_norm_mix_post, v_w_in, v_ssm_conv_w, v_ssm_conv_b, v_ssm_dt_bias, v_ssm_A_log, v_ssm_D, v_ssm_norm_w, v_gdn_conv_w, v_gdn_dt_bias, v_gdn_A_log, v_gdn_norm_w, v_w_ssm_up, v_w_gdn_up, v_w_out, v_norm_mlp_pre, v_norm_mlp_post, v_w_mlp_up, v_w_mlp_down):
    S, Dm = x.shape[1], D_MODEL
    me = 4 * lax.axis_index("x") + 2 * lax.axis_index("y") + lax.axis_index("c")
    x2, tgt = x[0], loss_target[0]
    n_ada = w_ada.shape[2]
    given = dict(
        w_ada=(w_ada, m_w_ada, v_w_ada), b_ada=(b_ada, m_b_ada, v_b_ada),
        norm_mix_pre=(norm_mix_pre, m_norm_mix_pre, v_norm_mix_pre),
        norm_mix_post=(norm_mix_post, m_norm_mix_post, v_norm_mix_post), w_in=(w_in, m_w_in, v_w_in),
        ssm_conv_w=(ssm_conv_w, m_ssm_conv_w, v_ssm_conv_w), ssm_conv_b=(ssm_conv_b, m_ssm_conv_b, v_ssm_conv_b),
        ssm_dt_bias=(ssm_dt_bias, m_ssm_dt_bias, v_ssm_dt_bias), ssm_A_log=(ssm_A_log, m_ssm_A_log, v_ssm_A_log),
        ssm_D=(ssm_D, m_ssm_D, v_ssm_D), ssm_norm_w=(ssm_norm_w, m_ssm_norm_w, v_ssm_norm_w),
        gdn_conv_w=(gdn_conv_w, m_gdn_conv_w, v_gdn_conv_w), gdn_dt_bias=(gdn_dt_bias, m_gdn_dt_bias, v_gdn_dt_bias),
        gdn_A_log=(gdn_A_log, m_gdn_A_log, v_gdn_A_log), gdn_norm_w=(gdn_norm_w, m_gdn_norm_w, v_gdn_norm_w),
        w_ssm_up=(w_ssm_up, m_w_ssm_up, v_w_ssm_up), w_gdn_up=(w_gdn_up, m_w_gdn_up, v_w_gdn_up),
        w_out=(w_out, m_w_out, v_w_out), norm_mlp_pre=(norm_mlp_pre, m_norm_mlp_pre, v_norm_mlp_pre),
        norm_mlp_post=(norm_mlp_post, m_norm_mlp_post, v_norm_mlp_post), w_mlp_up=(w_mlp_up, m_w_mlp_up, v_w_mlp_up),
        w_mlp_down=(w_mlp_down, m_w_mlp_down, v_w_mlp_down))

    (c_all, scw, gcw, g_in) = _gather_two_level([c, ssm_conv_w[0], gdn_conv_w[0], w_in[0].astype(BF16)], "gather_w_in")
    c_all = c_all.reshape(N_DEV, Dm)
    sp = W_IN_SPLITS
    w_main = jnp.concatenate(_cols_of_shards(g_in, sp[0], sp[2]) + _cols_of_shards(g_in, sp[3], sp[5])
                             + _cols_of_shards(g_in, sp[7], sp[9]), axis=1)
    w_small = jnp.concatenate(_cols_of_shards(g_in, sp[2], sp[3]) + _cols_of_shards(g_in, sp[5], sp[7])
                              + [jnp.zeros((Dm, N_SMALL - 64), BF16)], axis=1)
    conv_w = jnp.concatenate([_by_cols(scw), _by_cols(gcw)], axis=1)
    conv_b = jnp.concatenate([ssm_conv_b, jnp.zeros_like(ssm_conv_b)], axis=1)

    b_loc = lax.dynamic_slice(b_ada, (0, me * n_ada), (1, n_ada))
    mod_part = _ada_fwd(c_all, w_ada[0], b_loc, "ada_fwd")
    (mod_rows,) = _exchange([mod_part.reshape(N_DEV, 1, n_ada)], ["a2a"], "exchange_mod")
    rest = _exchange_start([w_ssm_up[0].astype(BF16), w_gdn_up[0].astype(BF16), w_out[0].astype(BF16),
                            w_mlp_up[0].astype(BF16), w_mlp_down[0].astype(BF16)], ["gather"] * 5, mod_rows,
                           "gather_rest_start")
    mod = mod_rows.reshape(1, 6 * Dm) + rest[4][0:1, 0:1]
    sh1, sc1, g1, sh2, sc2, g2 = [mod[:, i * Dm:(i + 1) * Dm] for i in range(6)]

    h = _pre_fwd(x2, norm_mix_pre, sc1, sh1, "pre_mix")
    proj = _mm(h, w_main, S, N_MAIN, Dm, mode="nn", out_dtype=BF16, tn=MM_WIDE_N, name="proj_main")
    small = _mm(h, w_small, S, N_SMALL, Dm, mode="nn", out_dtype=F32, name="proj_small")
    conv = _conv_fwd(proj, conv_w, conv_b, "conv_fwd")
    dt_g, b_g, a_g = _heads_first(small[:, 0:32], 8), _heads_first(small[:, 32:48], 8), _heads_first(small[:, 48:64], 8)
    pv_ssm = jnp.stack([ssm_dt_bias.reshape(8, 4), ssm_A_log.reshape(8, 4), ssm_D.reshape(8, 4)], axis=1)
    nw_ssm = ssm_norm_w.reshape(8, 1, SSM_GROUP_WIDTH)
    pv_gdn = jnp.stack([gdn_dt_bias.reshape(8, 2), gdn_A_log.reshape(8, 2)], axis=1)
    y_ssd, ysn, st_ssm = _ssd_fwd(conv, proj, dt_g, pv_ssm, nw_ssm, "ssd_fwd")
    o_pre, ogn, st_gdn, t_inv = _gdn_fwd(conv, proj, b_g, a_g, pv_gdn, gdn_norm_w, "gdn_fwd")
    g_su, g_gu, g_out, g_mu, g_md = _exchange_wait(rest, ["gather"] * 5, ogn, "gather_rest_wait")
    w_su, w_gu = g_su.reshape(2 * Dm, Dm), g_gu.reshape(2 * Dm, Dm)
    w_o, w_mu, w_md = g_out.reshape(Dm, Dm), _by_cols(g_mu), g_md.reshape(4 * Dm, Dm)
    ys = _mm(ysn, w_su, S, Dm, 2 * Dm, mode="nn", out_dtype=F32, name="ssm_up")
    yg, merged = _gdn_up_merge(ogn, w_gu, ys, proj, "gdn_up_merge")
    mo, x1, h2 = _mix_out_post_pre(merged, w_o, x2, norm_mix_post, g1, norm_mlp_pre, sc2, sh2, "mix_out_post_pre")
    u, act = _mm(h2, w_mu, S, 4 * Dm, Dm, mode="nn", out_dtype=F32, epi="relu2", tn=MM_WIDE_N, name="mlp_up")
    y_mlp = _mm(act, w_md, S, Dm, 4 * Dm, mode="nn", out_dtype=F32, name="mlp_down")
    dx2, loss_loc, dy, dg2, dw_post2 = _final_fwd_bwd(x1, y_mlp, norm_mlp_post, g2, tgt, "post_mlp_loss_bwd")

    du = _mm(dy, w_md, S, 4 * Dm, Dm, mode="nt", out_dtype=BF16, epi="drelu2", extra=u, tn=MM_WIDE_N, name="mlp_down_dx")
    gw_md = _mm(act, dy, 4 * Dm, Dm, S, mode="tn", out_dtype=BF16, name="mlp_down_dw")
    dh2 = _mm(du, w_mu, S, Dm, 4 * Dm, mode="nt", out_dtype=F32, name="mlp_up_dx")
    gw_mu = _mm(h2, du, Dm, 4 * Dm, S, mode="tn", out_dtype=BF16, name="mlp_up_dw")
    mlp_x = _exchange_start([_to_col_shards(gw_mu), gw_md.reshape(N_DEV, -1, Dm)], ["a2a"] * 2, gw_md,
                            "grads_mlp_start")
    dx1, dsh2, dsc2, dw_pre2 = _pre_bwd(dh2, x1, norm_mlp_pre, sc2 + mlp_x[4][0:1, 0:1], dx2, "pre_mlp_bwd")
    dmo, dg1, dw_post1 = _post_bwd(dx1, mo, norm_mix_post, g1, "post_mix_bwd")
    gw_o = _mm(merged, dmo, Dm, Dm, S, mode="tn", out_dtype=BF16, name="mix_out_dw")
    dys, dyg, d_proj = _mix_out_dx_merge_bwd(dmo, w_o, ys, yg, proj, lax.empty((S, N_MAIN), BF16), "mix_out_dx_merge")
    dysn = _mm(dys, w_su, S, 2 * Dm, Dm, mode="nt", out_dtype=F32, tn=MM_WIDE_N, name="ssm_up_dx")
    gw_su = _mm(ysn, dys, 2 * Dm, Dm, S, mode="tn", out_dtype=BF16, name="ssm_up_dw")
    dogn = _mm(dyg, w_gu, S, 2 * Dm, Dm, mode="nt", out_dtype=F32, tn=MM_WIDE_N, name="gdn_up_dx")
    gw_gu = _mm(ogn, dyg, 2 * Dm, Dm, S, mode="tn", out_dtype=BF16, name="gdn_up_dw")
    mix_x = _exchange_start([gw_su.reshape(N_DEV, -1, Dm), gw_gu.reshape(N_DEV, -1, Dm), gw_o.reshape(N_DEV, -1, Dm)],
                            ["a2a"] * 3, gw_gu, "grads_mix_start")
    d_conv, d_proj, ddt_g, dpv_ssm, dnw_ssm = _ssd_bwd(dysn, conv, proj, dt_g, pv_ssm + mix_x[4][0, 0], nw_ssm, y_ssd,
                                                       st_ssm, d_proj, lax.empty(conv.shape, BF16), "ssd_bwd")
    d_conv, d_proj, db_g, da_g, dpv_gdn, dnw_gdn = _gdn_bwd(dogn, conv, proj, b_g, a_g, pv_gdn, gdn_norm_w, o_pre,
                                                            st_gdn, t_inv, d_proj, d_conv, "gdn_bwd")
    d_proj, dconv_w, dconv_b = _conv_bwd(d_conv, proj, conv_w, conv_b, 0, d_proj, "conv_bwd")
    d_small = jnp.concatenate([_heads_last(ddt_g), _heads_last(db_g), _heads_last(da_g),
                               jnp.zeros((S, N_SMALL - 64), F32)], axis=1).astype(BF16)
    gw_small = _mm(h, d_small, Dm, N_SMALL, S, mode="tn", out_dtype=BF16, name="proj_small_dw")
    main_cols = _mm(h, d_proj, Dm, N_MAIN, S, mode="tn", out_dtype=BF16, name="proj_main_dw")
    n_shard = w_in.shape[2]
    slabs = [_orig_cols(main_cols, gw_small, i * n_shard, (i + 1) * n_shard) for i in range(N_DEV)]
    to_c0, to_c1 = jnp.stack(slabs[0::2]), jnp.stack(slabs[1::2])
    chip_sum = _add_pair(to_c0, to_c1, _swap_sibling(to_c0, to_c1, "grads_w_in_pair"), "grads_w_in_pair_sum")
    in_x = _exchange_start([chip_sum], ["a2a"], gw_small, "grads_w_in_start", chips=True)
    dh = _mm(d_small, w_small + in_x[4][0:1, 0:1].astype(BF16), S, Dm, N_SMALL, mode="nt", out_dtype=F32,
             name="proj_small_dx")
    dh = _mm(d_proj, w_main, S, Dm, N_MAIN, mode="nt", out_dtype=F32, add=dh, name="proj_main_dx")
    dx, dsh1, dsc1, dw_pre1 = _pre_bwd(dh, x2, norm_mix_pre, sc1, dx1, "pre_mix_bwd")

    dconv_b = dconv_b[:, :ssm_conv_b.shape[1]]
    dmod = jnp.concatenate([dsh1, dsc1, dg1, dsh2, dsc2, dg2], axis=1)
    small_vec = jnp.concatenate(
        [dmod, dw_pre1, dw_post1, dconv_b, dpv_ssm[:, 0].reshape(1, 32), dpv_ssm[:, 1].reshape(1, 32),
         dpv_ssm[:, 2].reshape(1, 32), dnw_ssm.reshape(1, 2048), dpv_gdn[:, 0].reshape(1, 16),
         dpv_gdn[:, 1].reshape(1, 16), jnp.sum(dnw_gdn, axis=0), dw_pre2, dw_post2, dconv_w.reshape(1, -1)], axis=1)
    n_vec = small_vec.shape[1]
    small_vec = jnp.pad(small_vec, ((0, 0), (0, (-n_vec) % 1024))).reshape(-1, 1024)
    small_x = _exchange_start([small_vec], ["gather"], dx, "gather_small_start")
    r_mu, r_md = _exchange_wait(mlp_x, ["a2a"] * 2, small_x[4], "grads_mlp_wait")
    r_su, r_gu, r_o = _exchange_wait(mix_x, ["a2a"] * 3, small_x[4], "grads_mix_wait")
    results = {}

    def adam_big(nm, contrib):
        w3 = given[nm]
        res = _adam(contrib, w3[0][0], w3[1][0], w3[2][0], "adam_" + nm)
        results[nm] = tuple(r.reshape(w3[0].shape) for r in res)

    adam_big("w_ssm_up", r_su)
    adam_big("w_gdn_up", r_gu)
    adam_big("w_out", r_o)
    adam_big("w_mlp_up", r_mu)
    adam_big("w_mlp_down", r_md)
    (small_all,) = _exchange_wait(small_x, ["gather"], results["w_mlp_down"][0], "gather_small_wait")
    small_all = small_all.reshape(N_DEV, -1)
    dmod_cols = lax.dynamic_slice(small_all, (0, me * n_ada), (N_DEV, n_ada))
    gw_ada = _ada_bwd(c_all.T, dmod_cols, "ada_bwd")
    conv_all = small_all[:, N_REPLICATED:n_vec].reshape(N_DEV, CONV_K, 2 * N_DEV * 512)
    conv_contrib = jnp.concatenate(
        [lax.dynamic_slice(conv_all, (0, 0, me * 512), (N_DEV, CONV_K, 512)),
         lax.dynamic_slice(conv_all, (0, 0, N_DEV * 512 + me * 512), (N_DEV, CONV_K, 512))], axis=1)
    rep_contrib = small_all[:, :N_REPLICATED].reshape(N_DEV, N_REPLICATED // 128, 128)

    adam_big("w_ada", gw_ada[None])
    (r_in,) = _exchange_wait(in_x, ["a2a"], results["w_ada"][0], "grads_w_in_wait", chips=True)
    adam_big("w_in", r_in)
    packed = [jnp.concatenate([given[nm][i] for nm in REPLICATED], axis=1).reshape(N_REPLICATED // 128, 128)
              for i in range(3)]
    rep_res = _adam(rep_contrib, packed[0], packed[1], packed[2], "adam_replicated")
    pos = 0
    for nm in REPLICATED:
        size = given[nm][0].shape[1]
        results[nm] = tuple(r.reshape(1, N_REPLICATED)[:, pos:pos + size] for r in rep_res)
        pos += size
    conv_wmv = [jnp.concatenate([given["ssm_conv_w"][i][0], given["gdn_conv_w"][i][0]], axis=0) for i in range(3)]
    conv_res = _adam(conv_contrib, conv_wmv[0], conv_wmv[1], conv_wmv[2], "adam_conv_w")
    results["ssm_conv_w"] = tuple(r[None, :CONV_K] for r in conv_res)
    results["gdn_conv_w"] = tuple(r[None, CONV_K:] for r in conv_res)

    loss = lax.psum(loss_loc[0, 0], ("x", "y", "c"))
    return (loss, dx[None]) + tuple(results[nm][i] for i in range(4) for nm in WEIGHTS)
```

```python
import jax
import jax.numpy as jnp
from jax import lax
from jax.experimental import pallas as pl
from jax.experimental.pallas import tpu as pltpu

F32 = jnp.float32
BF16 = jnp.bfloat16
N_DEV = 8
D_MODEL = 1024
EPS = 1e-6
CONV_K = 4
SSM_CHUNK = 128
SSM_HEAD_DIM = 64
SSM_D_STATE = 128
SSM_GROUPS = 8
SSM_HEADS_PER_GROUP = 4
SSM_GROUP_WIDTH = SSM_HEADS_PER_GROUP * SSM_HEAD_DIM
SSM_GROUPS_PER_STEP = 8
GDN_CHUNK = 64
GDN_HEAD = 128
GDN_QK_HEADS = 8
GDN_V_PER_QK = 2
GDN_QK_PER_STEP = 8
GDN_INV_BLOCK = 16
C_ZS, C_XBC, C_QKV, C_ZG, C_GS, C_GG, N_MAIN = 0, 2048, 6144, 10240, 12288, 13312, 14336
N_SMALL = 128
ADAM_LR, ADAM_B1, ADAM_B2, ADAM_EPS, ADAM_WD, ADAM_STEP = 0.001, 0.9, 0.999, 1e-08, 0.01, 10
VMEM_LIMIT = 56 * 1024 * 1024
MM_WHOLE_K = 4096
MM_SPLIT_K = 2048
MM_WIDE_N = 2048
NEG_INF = float("-inf")

_NT = (((1,), (1,)), ((), ()))
_NN = (((1,), (0,)), ((), ()))
_TN = (((0,), (0,)), ((), ()))


def _params(*sem):
    return pltpu.CompilerParams(dimension_semantics=sem, vmem_limit_bytes=VMEM_LIMIT)


def _dot(a, b, dims=_NN):
    return lax.dot_general(a.astype(BF16), b.astype(BF16), dims, preferred_element_type=F32)


def _split(a):
    hi = a.astype(BF16)
    return hi, (a - hi.astype(F32)).astype(BF16)


def _dot3(a, b, dims=_NN):
    ah, al = _split(a)
    bh, bl = _split(b)
    d = lambda u, v: lax.dot_general(u, v, dims, preferred_element_type=F32)
    return d(ah, bh) + (d(ah, bl) + d(al, bh))


def _dot2(a, b, dims=_NN):
    ah, al = _split(a)
    bb = b.astype(BF16)
    d = lambda u: lax.dot_general(u, bb, dims, preferred_element_type=F32)
    return d(ah) + d(al)


def _sigmoid(x):
    return 0.5 * jnp.tanh(0.5 * x) + 0.5


def _silu(x):
    return x * _sigmoid(x)


def _dsilu(x):
    s = _sigmoid(x)
    return s * (1.0 + x * (1.0 - s))


def _softplus(x):
    return jnp.maximum(x, 0.0) + jnp.log1p(jnp.exp(-jnp.abs(x)))


def _iota(n, m, d):
    return lax.broadcasted_iota(jnp.int32, (n, m), d)


def _rowsum(x):
    return jnp.sum(x, axis=1, keepdims=True)


def _colsum(x):
    return jnp.sum(x, axis=0, keepdims=True)


def _total(x):
    return _rowsum(_colsum(x))


MXU_LANES = 128


def _parts(x, n):
    out = []
    for _ in range(n):
        p = x.astype(BF16)
        out.append(p)
        x = x - p.astype(F32)
    return out


def _sum_by(m01, x, dims=_NN, n=3):
    return sum(lax.dot_general(m01, p, dims, preferred_element_type=F32) for p in _parts(x, n))


def _row_col_sums(q):
    ones = jnp.ones((q.shape[0], MXU_LANES), BF16)
    acc = 0.0
    for p in _parts(q, 2):
        acc = acc + (lax.dot_general(p, ones, _NN, preferred_element_type=F32)
                     - lax.dot_general(p, ones, _TN, preferred_element_type=F32))
    return acc[:, 0:1]


def _cumsum_forms(col, ii, jj):
    lower = jnp.where(ii >= jj, 1.0, 0.0).astype(BF16)
    cum_col = _sum_by(lower, jnp.broadcast_to(col, (col.shape[0], MXU_LANES)))[:, 0:1]
    cum_row = _colsum(jnp.where(ii <= jj, col, 0.0))
    return cum_col, cum_row


def _rev_cumsum_col(col, ii, jj):
    upper = jnp.where(ii <= jj, 1.0, 0.0).astype(BF16)
    return _sum_by(upper, jnp.broadcast_to(col, (col.shape[0], MXU_LANES)))[:, 0:1]


def _blk(dim, pref):
    return pref if dim % pref == 0 else dim


def _lockstep(gens):
    gens = list(gens)
    while gens:
        alive = []
        for g in gens:
            try:
                next(g)
                alive.append(g)
            except StopIteration:
                pass
        gens = alive


def _mm(a, b, M, N, K, *, mode, out_dtype, name, a_off=(0, 0), b_off=(0, 0), add=None, epi=None, extra=None,
        tm=1024, tn=1024):
    tm, tn = _blk(M, tm), _blk(N, tn)
    tk = K if K <= MM_WHOLE_K else _blk(K, MM_SPLIT_K)
    nk = K // tk
    if mode == "tn":
        a_spec = pl.BlockSpec((tk, tm), lambda i, j, k: (k + a_off[0] // tk, i + a_off[1] // tm))
        assert a_off[0] % tk == 0 and a_off[1] % tm == 0
    else:
        a_spec = pl.BlockSpec((tm, tk), lambda i, j, k: (i + a_off[0] // tm, k + a_off[1] // tk))
        assert a_off[0] % tm == 0 and a_off[1] % tk == 0
    if mode == "nt":
        b_spec = pl.BlockSpec((tn, tk), lambda i, j, k: (j + b_off[0] // tn, k + b_off[1] // tk))
        assert b_off[0] % tn == 0 and b_off[1] % tk == 0
    else:
        b_spec = pl.BlockSpec((tk, tn), lambda i, j, k: (k + b_off[0] // tk, j + b_off[1] // tn))
        assert b_off[0] % tk == 0 and b_off[1] % tn == 0
    dims = {"nn": _NN, "nt": _NT, "tn": _TN}[mode]
    o_spec = pl.BlockSpec((tm, tn), lambda i, j, k: (i, j))
    ins, in_specs = [a, b], [a_spec, b_spec]
    if add is not None:
        ins.append(add)
        in_specs.append(o_spec)
    if extra is not None:
        ins.append(extra)
        in_specs.append(o_spec)
    n_in = len(ins)
    if epi == "relu2":
        out_shape = (jax.ShapeDtypeStruct((M, N), BF16), jax.ShapeDtypeStruct((M, N), BF16))
        out_specs = (o_spec, o_spec)
    else:
        out_shape = jax.ShapeDtypeStruct((M, N), out_dtype)
        out_specs = o_spec

    def body(*refs):
        a_ref, b_ref = refs[0], refs[1]
        outs = refs[n_in:] if nk == 1 else refs[n_in:-1]

        def finish(r):
            pos = 2
            if add is not None:
                r = r + refs[pos][...]
                pos += 1
            if epi == "relu2":
                p = jnp.maximum(r, 0.0)
                outs[0][...] = p.astype(BF16)
                outs[1][...] = (p * p).astype(BF16)
            elif epi == "drelu2":
                outs[0][...] = (r * (2.0 * refs[pos][...].astype(F32))).astype(out_dtype)
            else:
                outs[0][...] = r.astype(out_dtype)

        if nk == 1:
            finish(_dot(a_ref[...], b_ref[...], dims))
            return
        acc = refs[-1]
        k = pl.program_id(2)

        @pl.when(k == 0)
        def _():
            acc[...] = jnp.zeros_like(acc)

        acc[...] += _dot(a_ref[...], b_ref[...], dims)

        @pl.when(k == nk - 1)
        def _():
            finish(acc[...])

    return pl.pallas_call(
        body, name=name, grid=(M // tm, N // tn, nk), in_specs=in_specs, out_specs=out_specs, out_shape=out_shape,
        scratch_shapes=[] if nk == 1 else [pltpu.VMEM((tm, tn), F32)],
        compiler_params=_params("parallel", "parallel", "arbitrary"))(*ins)


def _mm_rows(a, b, M, N, K, *, mode, name, extras, out_shapes, out_specs, epilogue, aliases=None, tm=512):
    tm = _blk(M, tm)
    a_spec = pl.BlockSpec((tm, K), lambda i: (i, 0))
    b_spec = pl.BlockSpec((K, N) if mode == "nn" else (N, K), lambda i: (0, 0))
    dims = _NN if mode == "nn" else _NT
    n_ex = len(extras)

    def body(a_ref, b_ref, *refs):
        epilogue(_dot(a_ref[...], b_ref[...], dims), refs[:n_ex], refs[n_ex:])

    return pl.pallas_call(
        body, name=name, grid=(M // tm,), in_specs=[a_spec, b_spec] + [sp for _, sp in extras],
        out_specs=tuple(out_specs), out_shape=tuple(out_shapes), input_output_aliases=aliases or {},
        compiler_params=_params("parallel"))(a, b, *[x for x, _ in extras])


def _row_spec(tb, d):
    return pl.BlockSpec((tb, d), lambda i: (i, 0))


def _vec_spec(d):
    return pl.BlockSpec((1, d), lambda i: (0, 0))


def _pre_fwd(x, w, sc, sh, name):
    S, Dm = x.shape
    tb = _blk(S, 512)

    def body(x_ref, w_ref, sc_ref, sh_ref, h_ref):
        xv = x_ref[...]
        r = lax.rsqrt(jnp.mean(xv * xv, axis=-1, keepdims=True) + EPS)
        h_ref[...] = ((xv * r * w_ref[...]) * (1.0 + sc_ref[...]) + sh_ref[...]).astype(BF16)

    return pl.pallas_call(
        body, name=name, grid=(S // tb,), in_specs=[_row_spec(tb, Dm)] + [_vec_spec(Dm)] * 3,
        out_specs=_row_spec(tb, Dm), out_shape=jax.ShapeDtypeStruct((S, Dm), BF16),
        compiler_params=_params("parallel"))(x, w, sc, sh)


def _final_fwd_bwd(x, y, w, g, target, name):
    S, Dm = x.shape
    tb = _blk(S, 512)
    nb = S // tb

    def body(x_ref, y_ref, w_ref, g_ref, t_ref, dx_ref, loss_ref, dy_ref, dg_ref, dw_ref, acc):
        i = pl.program_id(0)

        @pl.when(i == 0)
        def _():
            acc[...] = jnp.zeros_like(acc)
            dg_ref[...] = jnp.zeros_like(dg_ref)
            dw_ref[...] = jnp.zeros_like(dw_ref)

        yv = y_ref[...]
        r = lax.rsqrt(jnp.mean(yv * yv, axis=-1, keepdims=True) + EPS)
        yh = yv * r
        n = yh * w_ref[...]
        e = (x_ref[...] + g_ref[...] * n) - t_ref[...]
        dv = e * (1.0 / Dm)
        dx_ref[...] = dv
        acc[...] += _colsum(e * e)
        dg_ref[...] += _colsum(dv * n)
        dn = dv * g_ref[...]
        dw_ref[...] += _colsum(dn * yh)
        dyh = dn * w_ref[...]
        dy_ref[...] = (r * (dyh - yh * jnp.mean(dyh * yh, axis=-1, keepdims=True))).astype(BF16)

        @pl.when(i == nb - 1)
        def _():
            loss_ref[...] = (0.5 / Dm) * _rowsum(acc[...])

    row, vec = _row_spec(tb, Dm), _vec_spec(Dm)
    vec_shape = jax.ShapeDtypeStruct((1, Dm), F32)
    return pl.pallas_call(
        body, name=name, grid=(nb,), in_specs=[row, row, vec, vec, row],
        out_specs=(row, pl.BlockSpec((1, 1), lambda i: (0, 0)), row, vec, vec),
        out_shape=(jax.ShapeDtypeStruct((S, Dm), F32), jax.ShapeDtypeStruct((1, 1), F32),
                   jax.ShapeDtypeStruct((S, Dm), BF16), vec_shape, vec_shape),
        scratch_shapes=[pltpu.VMEM((1, Dm), F32)], compiler_params=_params("arbitrary"))(x, y, w, g, target)


def _post_bwd(dxo, y, w, g, name):
    S, Dm = y.shape
    tb = _blk(S, 512)

    def body(d_ref, y_ref, w_ref, g_ref, dy_ref, dg_ref, dw_ref):
        i = pl.program_id(0)

        @pl.when(i == 0)
        def _():
            dg_ref[...] = jnp.zeros_like(dg_ref)
            dw_ref[...] = jnp.zeros_like(dw_ref)

        yv, dv = y_ref[...], d_ref[...]
        r = lax.rsqrt(jnp.mean(yv * yv, axis=-1, keepdims=True) + EPS)
        yh = yv * r
        dg_ref[...] += _colsum(dv * (yh * w_ref[...]))
        dn = dv * g_ref[...]
        dw_ref[...] += _colsum(dn * yh)
        dyh = dn * w_ref[...]
        dy_ref[...] = (r * (dyh - yh * jnp.mean(dyh * yh, axis=-1, keepdims=True))).astype(BF16)

    return pl.pallas_call(
        body, name=name, grid=(S // tb,), in_specs=[_row_spec(tb, Dm)] * 2 + [_vec_spec(Dm)] * 2,
        out_specs=(_row_spec(tb, Dm), _vec_spec(Dm), _vec_spec(Dm)),
        out_shape=(jax.ShapeDtypeStruct((S, Dm), BF16), jax.ShapeDtypeStruct((1, Dm), F32),
                   jax.ShapeDtypeStruct((1, Dm), F32)),
        compiler_params=_params("arbitrary"))(dxo, y, w, g)


def _pre_bwd(dh, x, w, sc, dres, name):
    S, Dm = x.shape
    tb = _blk(S, 512)

    def body(dh_ref, x_ref, w_ref, sc_ref, dr_ref, dx_ref, dsh_ref, dsc_ref, dw_ref):
        i = pl.program_id(0)

        @pl.when(i == 0)
        def _():
            dsh_ref[...] = jnp.zeros_like(dsh_ref)
            dsc_ref[...] = jnp.zeros_like(dsc_ref)
            dw_ref[...] = jnp.zeros_like(dw_ref)

        xv, dv = x_ref[...], dh_ref[...]
        r = lax.rsqrt(jnp.mean(xv * xv, axis=-1, keepdims=True) + EPS)
        xh = xv * r
        one_sc = 1.0 + sc_ref[...]
        dsh_ref[...] += _colsum(dv)
        dsc_ref[...] += _colsum(dv * (xh * w_ref[...]))
        dw_ref[...] += _colsum(dv * one_sc * xh)
        dxh = dv * one_sc * w_ref[...]
        dx_ref[...] = dr_ref[...] + r * (dxh - xh * jnp.mean(dxh * xh, axis=-1, keepdims=True))

    vec = jax.ShapeDtypeStruct((1, Dm), F32)
    return pl.pallas_call(
        body, name=name, grid=(S // tb,),
        in_specs=[_row_spec(tb, Dm)] * 2 + [_vec_spec(Dm)] * 2 + [_row_spec(tb, Dm)],
        out_specs=(_row_spec(tb, Dm), _vec_spec(Dm), _vec_spec(Dm), _vec_spec(Dm)),
        out_shape=(jax.ShapeDtypeStruct((S, Dm), F32), vec, vec, vec),
        compiler_params=_params("arbitrary"))(dh, x, w, sc, dres)


D_PROJ_ANY = pl.BlockSpec(memory_space=pl.ANY)


def _gate_specs(tm, Dm):
    return (pl.BlockSpec((tm, Dm), lambda i: (i, C_GS // Dm)), pl.BlockSpec((tm, Dm), lambda i: (i, C_GG // Dm)))


def _gdn_up_merge(ogn, w_gu, ys, proj, name):
    S, K = ogn.shape
    Dm = ys.shape[1]
    tm = _blk(S, 512)
    row = _row_spec(tm, Dm)

    def epilogue(r, ex, out):
        ys_ref, gs_ref, gg_ref = ex
        out[0][...] = r
        out[1][...] = (_sigmoid(gs_ref[...].astype(F32)) * ys_ref[...]
                       + _sigmoid(gg_ref[...].astype(F32)) * r).astype(BF16)

    gs_spec, gg_spec = _gate_specs(tm, Dm)
    return _mm_rows(ogn, w_gu, S, Dm, K, mode="nn", name=name, tm=tm,
                    extras=[(ys, row), (proj, gs_spec), (proj, gg_spec)],
                    out_shapes=[jax.ShapeDtypeStruct((S, Dm), F32), jax.ShapeDtypeStruct((S, Dm), BF16)],
                    out_specs=[row, row], epilogue=epilogue)


def _mix_out_post_pre(merged, w_o, x, w_post, g, w_pre, sc, sh, name):
    S, Dm = x.shape
    tm = _blk(S, 512)
    row, vec = _row_spec(tm, Dm), _vec_spec(Dm)

    def epilogue(r, ex, out):
        x_ref, wpost_ref, g_ref, wpre_ref, sc_ref, sh_ref = ex
        out[0][...] = r
        rr = lax.rsqrt(jnp.mean(r * r, axis=-1, keepdims=True) + EPS)
        x1 = x_ref[...] + g_ref[...] * (r * rr * wpost_ref[...])
        out[1][...] = x1
        r1 = lax.rsqrt(jnp.mean(x1 * x1, axis=-1, keepdims=True) + EPS)
        out[2][...] = ((x1 * r1 * wpre_ref[...]) * (1.0 + sc_ref[...]) + sh_ref[...]).astype(BF16)

    return _mm_rows(merged, w_o, S, Dm, Dm, mode="nn", name=name, tm=tm,
                    extras=[(x, row), (w_post, vec), (g, vec), (w_pre, vec), (sc, vec), (sh, vec)],
                    out_shapes=[jax.ShapeDtypeStruct((S, Dm), F32), jax.ShapeDtypeStruct((S, Dm), F32),
                                jax.ShapeDtypeStruct((S, Dm), BF16)], out_specs=[row, row, row], epilogue=epilogue)


def _mix_out_dx_merge_bwd(dmo, w_o, ys, yg, proj, d_proj, name):
    S, Dm = ys.shape
    tm = _blk(S, 512)
    row = _row_spec(tm, Dm)

    def epilogue(d, ex, out):
        ys_ref, yg_ref, gs_ref, gg_ref, _ = ex
        ss, sg = _sigmoid(gs_ref[...].astype(F32)), _sigmoid(gg_ref[...].astype(F32))
        out[0][...] = (d * ss).astype(BF16)
        out[1][...] = (d * sg).astype(BF16)
        out[2][:, :Dm] = (d * ys_ref[...] * ss * (1.0 - ss)).astype(BF16)
        out[2][:, Dm:] = (d * yg_ref[...] * sg * (1.0 - sg)).astype(BF16)

    gs_spec, gg_spec = _gate_specs(tm, Dm)
    return _mm_rows(dmo, w_o, S, Dm, Dm, mode="nt", name=name, tm=tm,
                    extras=[(ys, row), (yg, row), (proj, gs_spec), (proj, gg_spec), (d_proj, D_PROJ_ANY)],
                    out_shapes=[jax.ShapeDtypeStruct((S, Dm), BF16), jax.ShapeDtypeStruct((S, Dm), BF16),
                                jax.ShapeDtypeStruct(d_proj.shape, BF16)],
                    out_specs=[row, row, pl.BlockSpec((tm, 2 * Dm), lambda i: (i, C_GS // (2 * Dm)))],
                    epilogue=epilogue, aliases={6: 2})


CONV_COLS = 128
CONV_BWD_ROWS = 256


def _taps_down(x):
    rows = _iota(x.shape[0], x.shape[1], 0)
    return [x] + [jnp.where(rows >= k, pltpu.roll(x, k, 0), 0.0) for k in range(1, CONV_K)]


def _conv_pre(taps, w_ref, b_ref):
    pre = taps[0] * w_ref[CONV_K - 1:CONV_K, :] + b_ref[...]
    for k in range(1, CONV_K):
        pre = pre + taps[k] * w_ref[CONV_K - 1 - k:CONV_K - k, :]
    return pre


def _conv_dx(dpre, w_ref):
    n = dpre.shape[0]
    rows = _iota(n, dpre.shape[1], 0)
    dx = dpre * w_ref[CONV_K - 1:CONV_K, :]
    for k in range(1, CONV_K):
        dx = dx + jnp.where(rows < n - k, pltpu.roll(dpre, n - k, 0), 0.0) * w_ref[CONV_K - 1 - k:CONV_K - k, :]
    return dx


def _conv_fwd(proj, w, b, name):
    S = proj.shape[0]
    n = w.shape[1]
    cb = CONV_COLS

    def body(x_ref, w_ref, b_ref, o_ref):
        o_ref[...] = _silu(_conv_pre(_taps_down(x_ref[...].astype(F32)), w_ref, b_ref)).astype(BF16)

    return pl.pallas_call(
        body, name=name, grid=(n // cb,),
        in_specs=[pl.BlockSpec((S, cb), lambda j: (0, j + C_XBC // cb)), pl.BlockSpec((CONV_K, cb), lambda j: (0, j)),
                  pl.BlockSpec((1, cb), lambda j: (0, j))],
        out_specs=pl.BlockSpec((S, cb), lambda j: (0, j)), out_shape=jax.ShapeDtypeStruct((S, n), BF16),
        compiler_params=_params("parallel"))(proj, w, b)


def _conv_bwd(dact, proj, w, b, col0, d_proj, name):
    S, n = dact.shape
    cb = CONV_COLS
    o = col0 // cb

    R, HALO = _blk(S, CONV_BWD_ROWS), 16
    n_chunks = S // R

    def body(d_ref, x_ref, w_ref, b_ref, _, dx_ref, dw_ref, db_ref):
        def chunk(r0, first, last, sums):
            lo, hi = (0 if first else HALO), (0 if last else HALO)
            start = r0 - lo if isinstance(r0, int) else pl.multiple_of(r0 - lo, HALO)
            xe = x_ref[pl.ds(start, lo + R + hi), :].astype(F32)
            rows = _iota(lo + R + hi, cb, 0)
            taps = [xe[lo:, :]]
            for k in range(1, CONV_K):
                t = pltpu.roll(xe, k, 0)
                taps.append((jnp.where(rows >= k, t, 0.0) if first else t)[lo:, :])
            dpre_e = d_ref[pl.ds(r0, R + hi), :].astype(F32) * _dsilu(_conv_pre(taps, w_ref, b_ref))
            dpre = dpre_e[0:R, :]
            db, dw = sums
            db = db + _colsum(dpre)
            dw = [dw[k] + _colsum(dpre * taps[k][0:R, :]) for k in range(CONV_K)]
            rows_e = _iota(R + hi, cb, 0)
            dx = dpre * w_ref[CONV_K - 1:CONV_K, :]
            for k in range(1, CONV_K):
                t = pltpu.roll(dpre_e, R + hi - k, 0)
                t = jnp.where(rows_e < R - k, t, 0.0) if last else t
                dx = dx + t[0:R, :] * w_ref[CONV_K - 1 - k:CONV_K - k, :]
            dx_ref[pl.ds(r0, R), :] = dx.astype(BF16)
            return db, dw

        zero = jnp.zeros((1, cb), F32)
        sums = chunk(0, True, n_chunks == 1, (zero, [zero] * CONV_K))
        if n_chunks > 2:
            def step(i, carry):
                db, dw = chunk(pl.multiple_of(i * R, R), False, False, (carry[0], list(carry[1:])))
                return (db,) + tuple(dw)
            carry = lax.fori_loop(1, n_chunks - 1, step, (sums[0],) + tuple(sums[1]))
            sums = (carry[0], list(carry[1:]))
        if n_chunks > 1:
            sums = chunk((n_chunks - 1) * R, False, True, sums)
        db_ref[...] = sums[0]
        for k in range(CONV_K):
            dw_ref[CONV_K - 1 - k:CONV_K - k, :] = sums[1][k]

    return pl.pallas_call(
        body, name=name, grid=(n // cb,),
        in_specs=[pl.BlockSpec((S, cb), lambda j: (0, j)), pl.BlockSpec((S, cb), lambda j: (0, j + o + C_XBC // cb)),
                  pl.BlockSpec((CONV_K, cb), lambda j: (0, j + o)), pl.BlockSpec((1, cb), lambda j: (0, j + o)),
                  D_PROJ_ANY],
        out_specs=(pl.BlockSpec((S, cb), lambda j: (0, j + o + C_XBC // cb)),
                   pl.BlockSpec((CONV_K, cb), lambda j: (0, j)), pl.BlockSpec((1, cb), lambda j: (0, j))),
        out_shape=(jax.ShapeDtypeStruct(d_proj.shape, BF16), jax.ShapeDtypeStruct((CONV_K, n), F32),
                   jax.ShapeDtypeStruct((1, n), F32)),
        input_output_aliases={4: 0}, compiler_params=_params("parallel"))(dact, proj, w, b, d_proj)


def _ssd_specs(L, order):
    G = SSM_GROUPS_PER_STEP
    W, N = G * SSM_GROUP_WIDTH, G * SSM_D_STATE
    x_spec = pl.BlockSpec((L, W), lambda g, c: (order(c), g))
    b_spec = pl.BlockSpec((L, N), lambda g, c: (order(c), 2048 // N + g))
    c_spec = pl.BlockSpec((L, N), lambda g, c: (order(c), 3072 // N + g))
    z_spec = pl.BlockSpec((L, W), lambda g, c: (order(c), C_ZS // W + g))
    dt_spec = pl.BlockSpec((G, L, SSM_HEADS_PER_GROUP), lambda g, c: (g, order(c), 0))
    p_spec = pl.BlockSpec((G, 3, SSM_HEADS_PER_GROUP), lambda g, c: (g, 0, 0))
    nw_spec = pl.BlockSpec((G, 1, SSM_GROUP_WIDTH), lambda g, c: (g, 0, 0))
    s_spec = pl.BlockSpec((G, 1, SSM_GROUP_WIDTH, SSM_D_STATE), lambda g, c: (g, order(c), 0, 0))
    return x_spec, b_spec, c_spec, z_spec, dt_spec, p_spec, nw_spec, s_spec


class _SsdGroup:
    def __init__(self, L):
        P, H, W = SSM_HEAD_DIM, SSM_HEADS_PER_GROUP, SSM_GROUP_WIDTH
        self.L = L
        self.ii, self.jj = _iota(L, L, 0), _iota(L, L, 1)
        self.lower = jnp.where(self.ii >= self.jj, 1.0, 0.0).astype(BF16)
        self.upper = jnp.where(self.ii <= self.jj, 1.0, 0.0).astype(BF16)
        self.lo = _iota(L, 2 * P, 1) < P
        self.lo_row = _iota(1, 2 * P, 1) < P
        bi, bj = _iota(W, W, 0), _iota(W, W, 1)
        self.block = jnp.where(bi // P == bj // P, 1.0, 0.0).astype(BF16)
        si, sj = _iota(2 * P, W, 0), _iota(2 * P, W, 1)
        self.pick = jnp.where(sj == si * P, 1.0, 0.0).astype(BF16)
        self.ones = jnp.ones((L, 2 * P), BF16)

    def spread(self, v4):
        R = v4.shape[0]
        lo = self.lo if R == self.L else self.lo_row
        b = lambda h: jnp.broadcast_to(v4[:, h:h + 1], (R, 2 * SSM_HEAD_DIM))
        return jnp.concatenate([jnp.where(lo, b(0), b(1)), jnp.where(lo, b(2), b(3))], axis=1)

    def gather4(self, v):
        return jnp.concatenate([v[:, h * SSM_HEAD_DIM:h * SSM_HEAD_DIM + 1] for h in range(SSM_HEADS_PER_GROUP)],
                               axis=1)

    def head_sums(self, z):
        return sum(lax.dot_general(p, self.block, _NN, preferred_element_type=F32) for p in _parts(z, 2))

    def pair_cols(self, full, pair):
        ps = full[:, pair * 128:(pair + 1) * 128]
        sw = pltpu.roll(ps, SSM_HEAD_DIM, 1)
        return jnp.where(self.lo, ps, sw), jnp.where(self.lo, sw, ps)

    def gates(self, dt4_raw, p):
        L = self.L
        dtr = self.spread(dt4_raw + p[0:1, :])
        dt = _softplus(dtr)
        A = self.spread(-jnp.exp(p[1:2, :]))
        acum = _sum_by(self.lower, dt * A)
        yield
        rows = _sum_by(self.pick, acum, _NT)
        yield
        a_last = acum[L - 1:L, :]
        cols = self.pair_cols(acum, 0) + self.pair_cols(acum, 1)
        decay, decay_t = [], []
        for h in range(SSM_HEADS_PER_GROUP):
            seg = cols[h] - rows[h:h + 1, :]
            decay.append(jnp.exp(jnp.where(self.ii >= self.jj, seg, NEG_INF)))
            decay_t.append(jnp.exp(jnp.where(self.jj >= self.ii, -seg, NEG_INF)))
        return dict(dtr=dtr, dt=dt, A=A, D=self.spread(p[2:3, :]), acum=acum, eac=jnp.exp(acum), a_last=a_last,
                    wdec=jnp.exp(a_last - acum), decay=decay, decay_t=decay_t,
                    ea_last=[jnp.exp(rows[h:h + 1, L - 1:L]) for h in range(SSM_HEADS_PER_GROUP)])


def _ssd_fwd(conv, proj, dt_raw, pvec, nw, name):
    S = conv.shape[0]
    L, P, N, H, W, G = SSM_CHUNK, SSM_HEAD_DIM, SSM_D_STATE, SSM_HEADS_PER_GROUP, SSM_GROUP_WIDTH, SSM_GROUPS_PER_STEP
    nc = S // L

    def body(x_ref, b_ref, c_ref, z_ref, dt_ref, p_ref, nw_ref, y_ref, yn_ref, s0_ref, state):
        c = pl.program_id(1)

        @pl.when(c == 0)
        def _():
            state[...] = jnp.zeros_like(state)

        k = _SsdGroup(L)

        def group(gi):
            gsl = slice(gi * W, (gi + 1) * W)
            Bm, Cm = b_ref[:, gi * N:(gi + 1) * N], c_ref[:, gi * N:(gi + 1) * N]
            x = x_ref[:, gsl].astype(F32)
            S0 = state[gsl, :]
            s0_ref[gi, 0] = S0
            CB = _dot(Cm, Bm, _NT)
            y_off = _dot(Cm, S0, _NT)
            t = yield from k.gates(dt_ref[gi], p_ref[gi])
            xdt = x * t["dt"]
            s_new = _dot(xdt * t["wdec"], Bm, _TN)
            y_diag = []
            for pair in range(H // 2):
                xp = xdt[:, pair * 128:(pair + 1) * 128]
                y_diag.append(jnp.where(k.lo, _dot(CB * t["decay"][2 * pair], xp),
                                        _dot(CB * t["decay"][2 * pair + 1], xp)))
            yield
            y = jnp.concatenate(y_diag, axis=1) + y_off * t["eac"]
            for h in range(H):
                hsl = slice(gi * W + h * P, gi * W + (h + 1) * P)
                state[hsl, :] = S0[h * P:(h + 1) * P, :] * t["ea_last"][h] + s_new[h * P:(h + 1) * P, :]
            y_ref[:, gsl] = y
            y2 = (y + t["D"] * x) * _silu(z_ref[:, gsl].astype(F32))
            r = lax.rsqrt(jnp.mean(y2 * y2, axis=-1, keepdims=True) + EPS)
            yn_ref[:, gsl] = (y2 * r * nw_ref[gi]).astype(BF16)

        _lockstep(group(gi) for gi in range(G))

    x_spec, b_spec, c_spec, z_spec, dt_spec, p_spec, nw_spec, s_spec = _ssd_specs(L, lambda c: c)
    y_spec = pl.BlockSpec((L, G * W), lambda g, c: (c, g))
    return pl.pallas_call(
        body, name=name, grid=(SSM_GROUPS // G, nc),
        in_specs=[x_spec, b_spec, c_spec, z_spec, dt_spec, p_spec, nw_spec],
        out_specs=(y_spec, y_spec, s_spec),
        out_shape=(jax.ShapeDtypeStruct((S, SSM_GROUPS * W), F32), jax.ShapeDtypeStruct((S, SSM_GROUPS * W), BF16),
                   jax.ShapeDtypeStruct((SSM_GROUPS, nc, W, N), F32)),
        scratch_shapes=[pltpu.VMEM((G * W, N), F32)],
        compiler_params=_params("parallel", "arbitrary"))(conv, conv, conv, proj, dt_raw, pvec, nw)


def _ssd_bwd(dyn, conv, proj, dt_raw, pvec, nw, y_ssd, states, d_proj, d_conv, name):
    S = conv.shape[0]
    L, P, N, H, W, G = SSM_CHUNK, SSM_HEAD_DIM, SSM_D_STATE, SSM_HEADS_PER_GROUP, SSM_GROUP_WIDTH, SSM_GROUPS_PER_STEP
    nc = S // L
    assert G == SSM_GROUPS
    CB0, CC0 = SSM_GROUPS * W, SSM_GROUPS * (W + N)

    def body(dyn_ref, x_ref, b_ref, c_ref, z_ref, dt_ref, p_ref, nw_ref, y_ref, s0_ref, _, _2,
             dact_ref, dz_ref, ddt_ref, dp_ref, dnw_ref, dstate):
        c = pl.program_id(1)

        @pl.when(c == 0)
        def _():
            dstate[...] = jnp.zeros_like(dstate)
            dp_ref[...] = jnp.zeros_like(dp_ref)
            dnw_ref[...] = jnp.zeros_like(dnw_ref)

        k = _SsdGroup(L)
        last = (_iota(L, 1, 0) == L - 1)

        def group(gi):
            gsl = slice(gi * W, (gi + 1) * W)
            Bm, Cm = b_ref[:, gi * N:(gi + 1) * N], c_ref[:, gi * N:(gi + 1) * N]
            x, z = x_ref[:, gsl].astype(F32), z_ref[:, gsl].astype(F32)
            S0, dS1 = s0_ref[gi, 0], dstate[gsl, :]
            CB = _dot(Cm, Bm, _NT)
            CBt = _dot(Bm, Cm, _NT)
            y_off_raw = _dot(Cm, S0, _NT)
            dXs_raw = _dot(Bm, dS1, _NT)
            t = yield from k.gates(dt_ref[gi], p_ref[gi])
            y1 = y_ref[:, gsl] + t["D"] * x
            sz = _silu(z)
            y2 = y1 * sz
            r = lax.rsqrt(jnp.mean(y2 * y2, axis=-1, keepdims=True) + EPS)
            y2h = y2 * r
            dyn_v = dyn_ref[:, gsl]
            dnw_ref[gi] += _colsum(dyn_v * y2h)
            dy2h = dyn_v * nw_ref[gi]
            dy2 = r * (dy2h - y2h * jnp.mean(dy2h * y2h, axis=-1, keepdims=True))
            dz_ref[:, gsl] = (dy2 * y1 * _dsilu(z)).astype(BF16)
            dY = dy2 * sz
            X = x * t["dt"]
            dYe = dY * t["eac"]
            dC_s = _dot(dYe, S0)
            dB_s = _dot(X * t["wdec"], dS1)
            dS_c = _dot(dYe, Cm, _TN)
            dXm, Gs, Gts = [], [], []
            for pair in range(H // 2):
                dYp, Xp = dY[:, pair * 128:(pair + 1) * 128], X[:, pair * 128:(pair + 1) * 128]
                dXm.append(jnp.where(k.lo, _dot(CBt * t["decay_t"][2 * pair], dYp),
                                     _dot(CBt * t["decay_t"][2 * pair + 1], dYp)))
                for mask in (k.lo, ~k.lo):
                    Gs.append(_dot(jnp.where(mask, dYp, 0.0), Xp, _NT))
                    Gts.append(_dot(jnp.where(mask, Xp, 0.0), dYp, _NT))
            yield
            dXs = dXs_raw * t["wdec"]
            dX = jnp.concatenate(dXm, axis=1) + dXs
            dCB, dCBt, q_sums = 0.0, 0.0, []
            for h in range(H):
                g_dec, gt_dec = Gs[h] * t["decay"][h], Gts[h] * t["decay_t"][h]
                dCB = dCB + g_dec
                dCBt = dCBt + gt_dec
                d = CB * g_dec - CBt * gt_dec
                q_sums.append(sum(lax.dot_general(pt, k.ones, _NN, preferred_element_type=F32)
                                  for pt in _parts(d, 2)))
            q_f = jnp.concatenate([jnp.where(k.lo, q_sums[0], q_sums[1]), jnp.where(k.lo, q_sums[2], q_sums[3])],
                                  axis=1)
            x_dxs = X * dXs
            tot = [_total(dS1[h * P:(h + 1) * P, :] * S0[h * P:(h + 1) * P, :]) * t["ea_last"][h] for h in range(H)]
            tot_f = k.spread(jnp.concatenate(tot, axis=1))
            d_alast = k.head_sums(jnp.broadcast_to(_colsum(x_dxs), (8, W)))[0:1, :] + tot_f
            dacum = q_f + k.head_sums(dY * (y_off_raw * t["eac"]) - x_dxs) + jnp.where(last, d_alast, 0.0)
            dx_dt = k.head_sums(dX * x)
            d_skip = _colsum(k.head_sums(dY * x))
            for h in range(H):
                hsl = slice(gi * W + h * P, gi * W + (h + 1) * P)
                dstate[hsl, :] = t["ea_last"][h] * dS1[h * P:(h + 1) * P, :] + dS_c[h * P:(h + 1) * P, :]
            dc_s2 = _dot(dCB, Bm)
            db_s2 = _dot(dCBt, Cm)
            yield
            da = _sum_by(k.upper, dacum)
            yield
            ddt_raw = (da * t["A"] + dx_dt) * _sigmoid(t["dtr"])
            dact_ref[:, gsl] = (dX * t["dt"] + t["D"] * dY).astype(BF16)
            dact_ref[:, CC0 + gi * N:CC0 + (gi + 1) * N] = (dC_s + dc_s2).astype(BF16)
            dact_ref[:, CB0 + gi * N:CB0 + (gi + 1) * N] = (dB_s + db_s2).astype(BF16)
            ddt_ref[gi] = k.gather4(ddt_raw)
            dp_ref[gi] += k.gather4(jnp.concatenate([_colsum(ddt_raw), _colsum(da * t["dt"]) * t["A"], d_skip],
                                                    axis=0))

        _lockstep(group(gi) for gi in range(G))

    rev = lambda c: nc - 1 - c
    x_spec, b_spec, c_spec, z_spec, dt_spec, p_spec, nw_spec, s_spec = _ssd_specs(L, rev)
    y_spec = pl.BlockSpec((L, G * W), lambda g, c: (rev(c), g))
    half = d_conv.shape[1] // 2
    return pl.pallas_call(
        body, name=name, grid=(SSM_GROUPS // G, nc),
        in_specs=[y_spec, x_spec, b_spec, c_spec, z_spec, dt_spec, p_spec, nw_spec, y_spec, s_spec, D_PROJ_ANY,
                  D_PROJ_ANY],
        out_specs=(pl.BlockSpec((L, half), lambda g, c: (rev(c), 0)), z_spec, dt_spec, p_spec, nw_spec),
        out_shape=(jax.ShapeDtypeStruct(d_conv.shape, BF16), jax.ShapeDtypeStruct(d_proj.shape, BF16),
                   jax.ShapeDtypeStruct((SSM_GROUPS, S, H), F32), jax.ShapeDtypeStruct((SSM_GROUPS, 3, H), F32),
                   jax.ShapeDtypeStruct((SSM_GROUPS, 1, W), F32)),
        scratch_shapes=[pltpu.VMEM((G * W, N), F32)], input_output_aliases={10: 1, 11: 0},
        compiler_params=_params("parallel", "arbitrary"))(dyn, conv, conv, conv, proj, dt_raw, pvec, nw, y_ssd, states,
                                                          d_proj, d_conv)


def _unit_lower_inverse(A, ii, jj):
    eye = (ii == jj).astype(F32)
    same = (ii // GDN_INV_BLOCK) == (jj // GDN_INV_BLOCK)
    Ad = jnp.where(same, A, 0.0)
    Ao = A - Ad
    P2 = _dot3(Ad, Ad)
    yield
    P4, X = _dot(P2, P2), _dot3(eye - Ad, eye + P2)
    yield
    P8, X = _dot(P4, P4), X + _dot2(X, P4)
    yield
    X = X + _dot2(X, P8)
    yield
    Bm = _dot3(X, Ao)
    yield
    B2 = _dot3(Bm, Bm)
    yield
    Y = (eye - Bm) + B2 - _dot2(Bm, B2)
    yield
    T = _dot3(Y, X)
    yield
    return T


def _gdn_specs(L, order):
    G = GDN_QK_PER_STEP
    Hd, W = G * GDN_HEAD, G * GDN_V_PER_QK * GDN_HEAD
    q_spec = pl.BlockSpec((L, Hd), lambda h, c: (order(c), (C_QKV - C_XBC) // Hd + h))
    k_spec = pl.BlockSpec((L, Hd), lambda h, c: (order(c), (C_QKV - C_XBC + 1024) // Hd + h))
    v_spec = pl.BlockSpec((L, W), lambda h, c: (order(c), (C_QKV - C_XBC + 2048) // W + h))
    z_spec = pl.BlockSpec((L, W), lambda h, c: (order(c), C_ZG // W + h))
    ba_spec = pl.BlockSpec((G, L, GDN_V_PER_QK), lambda h, c: (h, order(c), 0))
    p_spec = pl.BlockSpec((G, 2, GDN_V_PER_QK), lambda h, c: (h, 0, 0))
    nw_spec = pl.BlockSpec((1, GDN_HEAD), lambda h, c: (0, 0))
    s_spec = pl.BlockSpec((G, 1, GDN_V_PER_QK * GDN_HEAD, GDN_HEAD), lambda h, c: (h, order(c), 0, 0))
    t_spec = pl.BlockSpec((G * GDN_V_PER_QK, 1, L, L), lambda h, c: (h, order(c), 0, 0))
    return q_spec, k_spec, v_spec, z_spec, ba_spec, p_spec, nw_spec, s_spec, t_spec


def _gdn_qk(qa, ka):
    rq = lax.rsqrt(_rowsum(qa * qa) + EPS)
    rk = lax.rsqrt(_rowsum(ka * ka) + EPS)
    q, k = qa * rq * (GDN_HEAD ** -0.5), ka * rk
    return dict(qa=qa, rq=rq, rk=rk, q=q, k=k, QK=_dot(q, k, _NT))


def _gdn_gates(qk, b_col, a_col, p, j, ii, jj):
    L = qk["q"].shape[0]
    rq, rk, q, k = qk["rq"], qk["rk"], qk["q"], qk["k"]
    sp_in = a_col + p[0:1, j:j + 1]
    neg_ea = -jnp.exp(p[1:2, j:j + 1])
    g = neg_ea * _softplus(sp_in)
    gcum, gcum_row = _cumsum_forms(g, ii, jj)
    beta = _sigmoid(b_col)
    yield
    Dm = jnp.exp(jnp.where(ii >= jj, gcum - gcum_row, NEG_INF))
    eg = jnp.exp(gcum)
    g_last = gcum[L - 1:L, :]
    wdec = jnp.exp(g_last - gcum)
    return dict(rq=rq, rk=rk, q=q, k=k, beta=beta, sp_in=sp_in, neg_ea=neg_ea, g=g, Dm=Dm, kbeta=k * beta, eg=eg,
                g_last=g_last, wdec=wdec, kdec=k * wdec)


def _gdn_fwd(conv, proj, b_raw, a_raw, pvec, nw, name):
    S = conv.shape[0]
    L, Hd, J, G = GDN_CHUNK, GDN_HEAD, GDN_V_PER_QK, GDN_QK_PER_STEP
    W = J * Hd
    nc = S // L

    def body(q_ref, k_ref, v_ref, z_ref, b_ref, a_ref, p_ref, nw_ref, o_ref, on_ref, s0_ref, t_ref, state):
        c = pl.program_id(1)

        @pl.when(c == 0)
        def _():
            state[...] = jnp.zeros_like(state)

        ii, jj = _iota(L, L, 0), _iota(L, L, 1)
        for hq in range(G):
            s0_ref[hq, 0] = state[hq * W:(hq + 1) * W, :]
        qks = [_gdn_qk(q_ref[:, hq * Hd:(hq + 1) * Hd].astype(F32), k_ref[:, hq * Hd:(hq + 1) * Hd].astype(F32))
               for hq in range(G)]

        def head(hq, j):
            hd = hq * J + j
            sl = slice(hd * Hd, (hd + 1) * Hd)
            t = yield from _gdn_gates(qks[hq], b_ref[hq][:, j:j + 1], a_ref[hq][:, j:j + 1], p_ref[hq], j, ii, jj)
            KK = _dot(t["kbeta"], t["k"], _NT)
            QK = qks[hq]["QK"]
            yield
            T = yield from _unit_lower_inverse(jnp.where(ii > jj, KK * t["Dm"], 0.0), ii, jj)
            t_ref[hd, 0] = T
            S0 = state[sl, :]
            U = _dot2(T, v_ref[:, sl].astype(F32) * t["beta"])
            Wm = _dot2(T, t["kbeta"] * t["eg"])
            o_inter = _dot(t["q"] * t["eg"], S0)
            yield
            Vn = U - _dot(Wm, S0)
            yield
            o = o_inter + _dot(QK * t["Dm"], Vn)
            s_new = _dot(t["kdec"], Vn, _TN)
            yield
            state[sl, :] = S0 * jnp.exp(t["g_last"]) + s_new
            o_ref[:, sl] = o
            r = lax.rsqrt(jnp.mean(o * o, axis=-1, keepdims=True) + EPS)
            on_ref[:, sl] = ((o * r * nw_ref[...]) * _silu(z_ref[:, sl].astype(F32))).astype(BF16)

        _lockstep(head(hq, j) for hq in range(G) for j in range(J))

    q_spec, k_spec, v_spec, z_spec, ba_spec, p_spec, nw_spec, s_spec, t_spec = _gdn_specs(L, lambda c: c)
    o_spec = pl.BlockSpec((L, G * W), lambda h, c: (c, h))
    return pl.pallas_call(
        body, name=name, grid=(GDN_QK_HEADS // G, nc),
        in_specs=[q_spec, k_spec, v_spec, z_spec, ba_spec, ba_spec, p_spec, nw_spec],
        out_specs=(o_spec, o_spec, s_spec, t_spec),
        out_shape=(jax.ShapeDtypeStruct((S, GDN_QK_HEADS * W), F32), jax.ShapeDtypeStruct((S, GDN_QK_HEADS * W), BF16),
                   jax.ShapeDtypeStruct((GDN_QK_HEADS, nc, W, Hd), F32),
                   jax.ShapeDtypeStruct((GDN_QK_HEADS * J, nc, L, L), F32)),
        scratch_shapes=[pltpu.VMEM((G * W, Hd), F32)],
        compiler_params=_params("parallel", "arbitrary"))(conv, conv, conv, proj, b_raw, a_raw, pvec, nw)


def _gdn_bwd(don, conv, proj, b_raw, a_raw, pvec, nw, o_pre, states, t_inv, d_proj, d_conv, name):
    S = conv.shape[0]
    L, Hd, J, G = GDN_CHUNK, GDN_HEAD, GDN_V_PER_QK, GDN_QK_PER_STEP
    W = J * Hd
    nc = S // L
    assert G == GDN_QK_HEADS
    CK0, CV0 = GDN_QK_HEADS * Hd, 2 * GDN_QK_HEADS * Hd

    def body(don_ref, q_ref, k_ref, v_ref, z_ref, b_ref, a_ref, p_ref, nw_ref, o_ref, s0_ref, t_ref, _, _2,
             dact_ref, dz_ref, db_ref, da_ref, dp_ref, dnw_ref, dstate):
        c = pl.program_id(1)

        @pl.when(c == 0)
        def _():
            dstate[...] = jnp.zeros_like(dstate)
            dp_ref[...] = jnp.zeros_like(dp_ref)
            dnw_ref[...] = jnp.zeros_like(dnw_ref)

        ii, jj = _iota(L, L, 0), _iota(L, L, 1)
        last = (_iota(L, 1, 0) == L - 1)
        res = {}
        qks = [_gdn_qk(q_ref[:, hq * Hd:(hq + 1) * Hd].astype(F32), k_ref[:, hq * Hd:(hq + 1) * Hd].astype(F32))
               for hq in range(G)]

        def head(hq, j):
            hd = hq * J + j
            sl = slice(hd * Hd, (hd + 1) * Hd)
            qa = qks[hq]["qa"]
            t = yield from _gdn_gates(qks[hq], b_ref[hq][:, j:j + 1], a_ref[hq][:, j:j + 1], p_ref[hq], j, ii, jj)
            q, k, beta, eg, Dm, kbeta, kdec = (t[nm] for nm in ("q", "k", "beta", "eg", "Dm", "kbeta", "kdec"))
            T = t_ref[hd, 0]
            v, z, o = v_ref[:, sl].astype(F32), z_ref[:, sl].astype(F32), o_ref[:, sl]
            S0, dS1 = s0_ref[hq, 0, j * Hd:(j + 1) * Hd, :], dstate[sl, :]
            sz = _silu(z)
            r = lax.rsqrt(jnp.mean(o * o, axis=-1, keepdims=True) + EPS)
            oh = o * r
            d_on = don_ref[:, sl]
            dz_ref[:, sl] = (d_on * (oh * nw_ref[...]) * _dsilu(z)).astype(BF16)
            dn = d_on * sz
            dnw_part = _colsum(dn * oh)
            doh = dn * nw_ref[...]
            dO = r * (doh - oh * jnp.mean(doh * oh, axis=-1, keepdims=True))
            Rw = kbeta * eg
            qe = q * eg
            U = _dot2(T, v * beta)
            Wm = _dot2(T, Rw)
            KK = _dot(kbeta, k, _NT)
            QK = qks[hq]["QK"]
            o_inter = _dot(qe, S0)
            dq_s = _dot(dO, S0, _NT)
            dS_q = _dot(qe, dO, _TN)
            yield
            Am = jnp.where(ii > jj, KK * Dm, 0.0)
            Pm = QK * Dm
            Vn = U - _dot(Wm, S0)
            dVn_s = _dot(kdec, dS1)
            yield
            dVn = _dot(Pm, dO, _TN) + dVn_s
            dP = _dot(dO, Vn, _NT)
            dKd = _dot(Vn, dS1, _NT)
            yield
            dQK = dP * Dm
            dq = _dot(dQK, k) + dq_s * eg
            dk = _dot(dQK, q, _TN) + dKd * t["wdec"]
            dstate[sl, :] = jnp.exp(t["g_last"]) * dS1 + dS_q - _dot(Wm, dVn, _TN)
            dW = -_dot(dVn, S0, _NT)
            dRu = _dot2(T, dVn, _TN)
            yield
            dRw = _dot2(T, dW, _TN)
            dA_u = _dot(dRu, U, _NT)
            yield
            dA = jnp.where(ii > jj, -(dA_u + _dot(dRw, Wm, _NT)), 0.0)
            yield
            dKK = dA * Dm
            dkbeta = _dot(dKK, k) + dRw * eg
            dk = dk + _dot(dKK, kbeta, _TN)
            yield
            dk = dk + dkbeta * beta
            dbeta = _rowsum(dkbeta * k + dRu * v)
            dact_ref[:, CV0 + hd * Hd:CV0 + (hd + 1) * Hd] = (dRu * beta).astype(BF16)
            Q = dA * Am + dP * Pm
            kd = dKd * kdec
            d_glast = _total(kd) + jnp.exp(t["g_last"]) * _total(dS1 * S0)
            q_sums = _row_col_sums(Q)
            rest = _rowsum(dRw * Rw + dO * o_inter - kd) + jnp.where(last, d_glast, 0.0)
            yield
            dg = _rev_cumsum_col(q_sums + rest, ii, jj)
            yield
            da_raw = dg * t["neg_ea"] * _sigmoid(t["sp_in"])
            res[hq, j] = dict(dq=dq, dk=dk, db=dbeta * beta * (1.0 - beta), da=da_raw, d_bias=_colsum(da_raw),
                              d_alog=_colsum(dg * t["g"]), dnw=dnw_part, rq=t["rq"], rk=t["rk"], k=k, qh=qa * t["rq"])

        _lockstep(head(hq, j) for hq in range(G) for j in range(J))
        for hq in range(G):
            parts = [res[hq, j] for j in range(J)]
            hsl = slice(hq * Hd, (hq + 1) * Hd)
            p0 = parts[0]
            dqh = sum(pt["dq"] for pt in parts) * (GDN_HEAD ** -0.5)
            dkn = sum(pt["dk"] for pt in parts)
            dact_ref[:, hsl] = (p0["rq"] * (dqh - p0["qh"] * _rowsum(dqh * p0["qh"]))).astype(BF16)
            dact_ref[:, CK0 + hq * Hd:CK0 + (hq + 1) * Hd] = (
                p0["rk"] * (dkn - p0["k"] * _rowsum(dkn * p0["k"]))).astype(BF16)
            db_ref[hq] = jnp.concatenate([pt["db"] for pt in parts], axis=1)
            da_ref[hq] = jnp.concatenate([pt["da"] for pt in parts], axis=1)
            dp_ref[hq] += jnp.concatenate([jnp.concatenate([pt["d_bias"] for pt in parts], axis=1),
                                           jnp.concatenate([pt["d_alog"] for pt in parts], axis=1)], axis=0)
            dnw_ref[hq] += sum(pt["dnw"] for pt in parts)

    rev = lambda c: nc - 1 - c
    q_spec, k_spec, v_spec, z_spec, ba_spec, p_spec, nw_spec, s_spec, t_spec = _gdn_specs(L, rev)
    o_spec = pl.BlockSpec((L, G * W), lambda h, c: (rev(c), h))
    dnw_spec = pl.BlockSpec((G, 1, Hd), lambda h, c: (h, 0, 0))
    half = d_conv.shape[1] // 2
    return pl.pallas_call(
        body, name=name, grid=(GDN_QK_HEADS // G, nc),
        in_specs=[o_spec, q_spec, k_spec, v_spec, z_spec, ba_spec, ba_spec, p_spec, nw_spec, o_spec, s_spec, t_spec,
                  D_PROJ_ANY, D_PROJ_ANY],
        out_specs=(pl.BlockSpec((L, half), lambda h, c: (rev(c), 1)), z_spec, ba_spec, ba_spec, p_spec, dnw_spec),
        out_shape=(jax.ShapeDtypeStruct(d_conv.shape, BF16), jax.ShapeDtypeStruct(d_proj.shape, BF16),
                   jax.ShapeDtypeStruct((GDN_QK_HEADS, S, J), F32), jax.ShapeDtypeStruct((GDN_QK_HEADS, S, J), F32),
                   jax.ShapeDtypeStruct((GDN_QK_HEADS, 2, J), F32), jax.ShapeDtypeStruct((GDN_QK_HEADS, 1, Hd), F32)),
        scratch_shapes=[pltpu.VMEM((G * W, Hd), F32)], input_output_aliases={12: 1, 13: 0},
        compiler_params=_params("parallel", "arbitrary"))(don, conv, conv, conv, proj, b_raw, a_raw, pvec, nw, o_pre,
                                                          states, t_inv, d_proj, d_conv)


def _ada_fwd(c_all, w_loc, b_loc, name):
    n = w_loc.shape[1]

    def body(c_ref, w_ref, b_ref, o_ref):
        o_ref[...] = _dot3(_silu(c_ref[...]), w_ref[...]) + b_ref[...]

    return pl.pallas_call(body, name=name, out_shape=jax.ShapeDtypeStruct((N_DEV, n), F32),
                          compiler_params=pltpu.CompilerParams(vmem_limit_bytes=VMEM_LIMIT))(c_all, w_loc, b_loc)


def _ada_bwd(c_all_t, dmod_cols, name):
    Dm, n = c_all_t.shape[0], dmod_cols.shape[1]

    def body(c_ref, d_ref, o_ref):
        ca = _silu(c_ref[...])
        acc = ca[:, 0:1] * d_ref[0:1, :]
        for i in range(1, N_DEV):
            acc = acc + ca[:, i:i + 1] * d_ref[i:i + 1, :]
        o_ref[...] = acc

    return pl.pallas_call(body, name=name, out_shape=jax.ShapeDtypeStruct((Dm, n), F32),
                          compiler_params=pltpu.CompilerParams(vmem_limit_bytes=VMEM_LIMIT))(c_all_t, dmod_cols)


ADAM_BLOCK_BYTES = 12 * 1024 * 1024


def _adam(contrib, w, m, v, name):
    n, R, C = contrib.shape
    tr = R
    while tr % 16 == 0 and (n + 7) * tr * C * 4 > ADAM_BLOCK_BYTES:
        tr //= 2

    def body(c_ref, w_ref, m_ref, v_ref, g_ref, d_ref, nm_ref, nv_ref):
        g = c_ref[0].astype(F32)
        for i in range(1, n):
            g = g + c_ref[i].astype(F32)
        nm = ADAM_B1 * m_ref[...] + (1.0 - ADAM_B1) * g
        nv = ADAM_B2 * v_ref[...] + (1.0 - ADAM_B2) * (g * g)
        m_hat = nm / (1.0 - ADAM_B1 ** ADAM_STEP)
        v_hat = nv / (1.0 - ADAM_B2 ** ADAM_STEP)
        g_ref[...] = g
        d_ref[...] = -ADAM_LR * (m_hat / (jnp.sqrt(v_hat) + ADAM_EPS) + ADAM_WD * w_ref[...])
        nm_ref[...] = nm
        nv_ref[...] = nv

    spec = pl.BlockSpec((tr, C), lambda i: (i, 0))
    shp = jax.ShapeDtypeStruct((R, C), F32)
    return pl.pallas_call(
        body, name=name, grid=(R // tr,), in_specs=[pl.BlockSpec((n, tr, C), lambda i: (0, i, 0)), spec, spec, spec],
        out_specs=(spec,) * 4, out_shape=(shp,) * 4, compiler_params=_params("parallel"))(contrib, w, m, v)


def _exchange(arrays, modes, name, chips=False):
    n = len(arrays)
    out_shape = tuple(jax.ShapeDtypeStruct((N_DEV,) + a.shape if md == "gather" else a.shape, a.dtype)
                      for a, md in zip(arrays, modes))

    def body(*refs):
        ins, outs = refs[:n], refs[n:2 * n]
        send_sems, recv_sems, loc_sems = refs[2 * n:]
        me, peers = _peer_table(chips)

        def src(k, slot):
            return ins[k] if modes[k] == "gather" else ins[k].at[slot]

        def remote(k, m, to_slot, land_slot):
            return pltpu.make_async_remote_copy(
                src_ref=src(k, to_slot), dst_ref=outs[k].at[land_slot], send_sem=send_sems.at[k, m],
                recv_sem=recv_sems.at[k, m], device_id=peers[m][0], device_id_type=pl.DeviceIdType.MESH)

        local = [pltpu.make_async_copy(src(k, me), outs[k].at[me], loc_sems.at[k]) for k in range(n)]
        for cp in local:
            cp.start()
        sends = [remote(k, m, peers[m][1], me) for m in range(len(peers)) for k in range(n)]
        for cp in sends:
            cp.start()
        for m in range(len(peers)):
            for k in range(n):
                remote(k, m, peers[m][1], peers[m][1]).wait_recv()
        for cp in sends:
            cp.wait_send()
        for cp in local:
            cp.wait()

    any_spec = pl.BlockSpec(memory_space=pl.ANY)
    return pl.pallas_call(
        body, name=name, in_specs=[any_spec] * n, out_specs=(any_spec,) * n, out_shape=out_shape,
        scratch_shapes=[pltpu.SemaphoreType.DMA((n, N_DEV - 1)), pltpu.SemaphoreType.DMA((n, N_DEV - 1)),
                        pltpu.SemaphoreType.DMA((n,))])(*arrays)


def _gather_two_level(arrays, name):
    n = len(arrays)
    out_shape = tuple(jax.ShapeDtypeStruct((N_DEV,) + a.shape, a.dtype) for a in arrays)

    def body(*refs):
        ins, outs = refs[:n], refs[n:2 * n]
        send_sems, recv_sems = refs[2 * n:]
        ix, iy, ic = lax.axis_index("x"), lax.axis_index("y"), lax.axis_index("c")
        lin = lambda px, py, pc: 4 * px + 2 * py + pc
        me, sib = lin(ix, iy, ic), (ix, iy, 1 - ic)
        chips = [(1 - ix, iy), (ix, 1 - iy), (1 - ix, 1 - iy)]

        def copy(k, s, block, to, src=None):
            return pltpu.make_async_remote_copy(
                src_ref=outs[k].at[block] if src is None else src, dst_ref=outs[k].at[block],
                send_sem=send_sems.at[k, s], recv_sem=recv_sems.at[k, s], device_id=to,
                device_id_type=pl.DeviceIdType.MESH)

        first = [copy(k, 1 + j, me, (cx, cy, ic), src=ins[k]) for j, (cx, cy) in enumerate(chips) for k in range(n)]
        first += [copy(k, 0, me, sib, src=ins[k]) for k in range(n)]
        for cp in first:
            cp.start()
        passed = []
        for j, (cx, cy) in enumerate(chips):
            for k in range(n):
                copy(k, 1 + j, lin(cx, cy, ic), sib).wait_recv()
                passed.append(copy(k, 4 + j, lin(cx, cy, ic), sib))
                passed[-1].start()
        for k in range(n):
            copy(k, 0, lin(*sib), sib).wait_recv()
            for j, (cx, cy) in enumerate(chips):
                copy(k, 4 + j, lin(cx, cy, 1 - ic), sib).wait_recv()
        for cp in first + passed:
            cp.wait_send()

    any_spec = pl.BlockSpec(memory_space=pl.ANY)
    outs = pl.pallas_call(
        body, name=name, in_specs=[any_spec] * n, out_specs=(any_spec,) * n, out_shape=out_shape,
        scratch_shapes=[pltpu.SemaphoreType.DMA((n, N_DEV - 1)), pltpu.SemaphoreType.DMA((n, N_DEV - 1))])(*arrays)
    me = 4 * lax.axis_index("x") + 2 * lax.axis_index("y") + lax.axis_index("c")
    return [lax.dynamic_update_index_in_dim(o, a, me, axis=0) for o, a in zip(outs, arrays)]


def _peer_table(chips=False):
    ix, iy, ic = lax.axis_index("x"), lax.axis_index("y"), lax.axis_index("c")
    peers = []
    for m in ((2, 4, 6) if chips else range(1, N_DEV)):
        px = 1 - ix if m & 4 else ix
        py = 1 - iy if m & 2 else iy
        pc = 1 - ic if m & 1 else ic
        peers.append(((px, py, pc), 2 * px + py if chips else 4 * px + 2 * py + pc))
    return (2 * ix + iy if chips else 4 * ix + 2 * iy + ic), peers


def _exchange_start(arrays, modes, after, name, chips=False):
    n = len(arrays)
    land_shapes = [(N_DEV,) + a.shape if md == "gather" else a.shape for a, md in zip(arrays, modes)]

    def body(*refs):
        ins, lands = refs[:n], refs[n:2 * n]
        send_sems, recv_sems = refs[2 * n + 1], refs[2 * n + 2]
        token = refs[-1]
        me, peers = _peer_table(chips)

        def src(k, slot):
            return ins[k] if modes[k] == "gather" else ins[k].at[slot]

        for peer, slot in peers:
            for k in range(n):
                pltpu.make_async_remote_copy(
                    src_ref=src(k, slot), dst_ref=lands[k].at[me], send_sem=send_sems, recv_sem=recv_sems,
                    device_id=peer, device_id_type=pl.DeviceIdType.MESH).start()
        token[...] = jnp.zeros_like(token)

    hbm = pl.BlockSpec(memory_space=pltpu.HBM)
    sem = pl.BlockSpec(memory_space=pltpu.SEMAPHORE)
    sem_shape = pltpu.SemaphoreType.DMA(())
    operands = [pltpu.with_memory_space_constraint(a, pltpu.HBM) for a in arrays]
    operands += [pltpu.with_memory_space_constraint(lax.empty(s, a.dtype), pltpu.HBM)
                 for s, a in zip(land_shapes, arrays)]
    out = pl.pallas_call(
        body, name=name,
        out_shape=(sem_shape, sem_shape) + tuple(pltpu.HBM(a.shape, a.dtype) for a in arrays)
        + tuple(pltpu.HBM(s, a.dtype) for s, a in zip(land_shapes, arrays)) + (jax.ShapeDtypeStruct((8, 128), F32),),
        in_specs=[hbm] * (2 * n) + [pl.BlockSpec(memory_space=pl.ANY)],
        out_specs=(sem, sem) + (hbm,) * (2 * n) + (pl.BlockSpec(memory_space=pltpu.VMEM),),
        input_output_aliases={i: 2 + i for i in range(2 * n)},
        compiler_params=pltpu.CompilerParams(has_side_effects=pltpu.SideEffectType.DATAFLOW_SIDE_EFFECTING))(
            *operands, after)
    return out[0], out[1], out[2:2 + n], out[2 + n:2 + 2 * n], out[-1]


def _exchange_wait(started, modes, after, name, chips=False):
    send_sems, recv_sems, sent, lands, _ = started
    n = len(sent)

    def body(*refs):
        ins, zones = refs[:n], refs[n:2 * n]
        send_ref, recv_ref = refs[2 * n], refs[2 * n + 1]
        _, peers = _peer_table(chips)

        def src(k, slot):
            return ins[k] if modes[k] == "gather" else ins[k].at[slot]

        for peer, slot in peers:
            for k in range(n):
                cp = pltpu.make_async_remote_copy(
                    src_ref=src(k, slot), dst_ref=zones[k].at[slot], send_sem=send_ref, recv_sem=recv_ref,
                    device_id=peer, device_id_type=pl.DeviceIdType.MESH)
                cp.wait_send()
                cp.wait_recv()

    hbm = pl.BlockSpec(memory_space=pltpu.HBM)
    sem = pl.BlockSpec(memory_space=pltpu.SEMAPHORE)
    out = pl.pallas_call(
        body, name=name,
        out_shape=tuple(pltpu.HBM(a.shape, a.dtype) for a in sent) + tuple(pltpu.HBM(a.shape, a.dtype) for a in lands),
        in_specs=[hbm] * (2 * n) + [sem, sem, pl.BlockSpec(memory_space=pl.ANY)], out_specs=(hbm,) * (2 * n),
        input_output_aliases={i: i for i in range(2 * n)},
        compiler_params=pltpu.CompilerParams(has_side_effects=pltpu.SideEffectType.DATAFLOW_SIDE_EFFECTING))(
            *sent, *lands, send_sems, recv_sems, after)
    ix, iy, ic = lax.axis_index("x"), lax.axis_index("y"), lax.axis_index("c")
    me = 2 * ix + iy if chips else 4 * ix + 2 * iy + ic
    filled = []
    for k in range(n):
        own = sent[k] if modes[k] == "gather" else lax.dynamic_index_in_dim(sent[k], me, axis=0, keepdims=False)
        filled.append(lax.dynamic_update_index_in_dim(out[n + k], own, me, axis=0))
    return filled


def _swap_sibling(to_c0, to_c1, name):
    def body(c0_ref, c1_ref, out_ref, send_sem, recv_sem):
        ix, iy, ic = lax.axis_index("x"), lax.axis_index("y"), lax.axis_index("c")

        def copy(src):
            return pltpu.make_async_remote_copy(src_ref=src, dst_ref=out_ref, send_sem=send_sem, recv_sem=recv_sem,
                                                device_id=(ix, iy, 1 - ic), device_id_type=pl.DeviceIdType.MESH)

        @pl.when(ic == 0)
        def _():
            copy(c1_ref).start()

        @pl.when(ic == 1)
        def _():
            copy(c0_ref).start()

        copy(c0_ref).wait()

    any_spec = pl.BlockSpec(memory_space=pl.ANY)
    return pl.pallas_call(body, name=name, in_specs=[any_spec, any_spec], out_specs=any_spec,
                          out_shape=jax.ShapeDtypeStruct(to_c0.shape, to_c0.dtype),
                          scratch_shapes=[pltpu.SemaphoreType.DMA, pltpu.SemaphoreType.DMA])(to_c0, to_c1)


def _add_pair(to_c0, to_c1, got, name):
    n, R, C = got.shape
    tr = _blk(R, 256)

    def body(c0_ref, c1_ref, got_ref, o_ref):
        ic = lax.axis_index("c")

        @pl.when(ic == 0)
        def _():
            o_ref[...] = (c0_ref[...].astype(F32) + got_ref[...].astype(F32)).astype(o_ref.dtype)

        @pl.when(ic == 1)
        def _():
            o_ref[...] = (c1_ref[...].astype(F32) + got_ref[...].astype(F32)).astype(o_ref.dtype)

    spec = pl.BlockSpec((1, tr, C), lambda i, j: (i, j, 0))
    return pl.pallas_call(body, name=name, grid=(n, R // tr), in_specs=[spec, spec, spec], out_specs=spec,
                          out_shape=jax.ShapeDtypeStruct(got.shape, got.dtype),
                          compiler_params=_params("parallel", "parallel"))(to_c0, to_c1, got)


W_IN_SPLITS = (0, 2048, 6144, 6176, 10272, 12320, 12336, 12352, 13376, 14400)
N_REPLICATED = 16640
REPLICATED = ("b_ada", "norm_mix_pre", "norm_mix_post", "ssm_conv_b", "ssm_dt_bias", "ssm_A_log", "ssm_D",
              "ssm_norm_w", "gdn_dt_bias", "gdn_A_log", "gdn_norm_w", "norm_mlp_pre", "norm_mlp_post")
WEIGHTS = ("w_ada", "b_ada", "norm_mix_pre", "norm_mix_post", "w_in", "ssm_conv_w", "ssm_conv_b", "ssm_dt_bias",
           "ssm_A_log", "ssm_D", "ssm_norm_w", "gdn_conv_w", "gdn_dt_bias", "gdn_A_log", "gdn_norm_w", "w_ssm_up",
           "w_gdn_up", "w_out", "norm_mlp_pre", "norm_mlp_post", "w_mlp_up", "w_mlp_down")


def _cols_of_shards(g, a, b):
    width, pieces = g.shape[2], []
    while a < b:
        i = a // width
        hi = min(b, (i + 1) * width)
        pieces.append(g[i][:, a - i * width:hi - i * width])
        a = hi
    return pieces


ORIG_SEGMENTS = ((0, 6144, "main", 0), (6144, 6176, "small", 0), (6176, 12320, "main", 6144),
                 (12320, 12352, "small", 32), (12352, 14400, "main", 12288))


def _orig_cols(main_cols, small_cols, a, b):
    pieces = []
    for s0, s1, which, off in ORIG_SEGMENTS:
        lo, hi = max(a, s0), min(b, s1)
        if lo < hi:
            pieces.append((main_cols if which == "main" else small_cols)[:, off + lo - s0:off + hi - s0])
    return jnp.concatenate(pieces, axis=1)


def _by_cols(t):
    return t.transpose(1, 0, 2).reshape(t.shape[1], N_DEV * t.shape[2])


def _to_col_shards(t):
    R, C8 = t.shape
    return t.reshape(R, N_DEV, C8 // N_DEV).transpose(1, 0, 2)


def _heads_first(t, groups):
    S = t.shape[0]
    return t.reshape(S, groups, t.shape[1] // groups).transpose(1, 0, 2)


def _heads_last(t):
    return t.transpose(1, 0, 2).reshape(t.shape[1], t.shape[0] * t.shape[2])


def kernel(x, c, w_ada, b_ada, norm_mix_pre, norm_mix_post, w_in, ssm_conv_w, ssm_conv_b, ssm_dt_bias, ssm_A_log, ssm_D, ssm_norm_w, gdn_conv_w, gdn_dt_bias, gdn_A_log, gdn_norm_w, w_ssm_up, w_gdn_up, w_out, norm_mlp_pre, norm_mlp_post, w_mlp_up, w_mlp_down, loss_target, m_w_ada, m_b_ada, m_norm_mix_pre, m_norm_mix_post, m_w_in, m_ssm_conv_w, m_ssm_conv_b, m_ssm_dt_bias, m_ssm_A_log, m_ssm_D, m_ssm_norm_w, m_gdn_conv_w, m_gdn_dt_bias, m_gdn_A_log, m_gdn_norm_w, m_w_ssm_up, m_w_gdn_up, m_w_out, m_norm_mlp_pre, m_norm_mlp_post, m_w_mlp_up, m_w_mlp_down, v_w_ada, v_b_ada, v_norm_mix_pre, v_norm_mix_post, v_w_in, v_ssm_conv_w, v_ssm_conv_b, v_ssm_dt_bias, v_ssm_A_log, v_ssm_D, v_ssm_norm_w, v_gdn_conv_w, v_gdn_dt_bias, v_gdn_A_log, v_gdn_norm_w, v_w_ssm_up, v_w_gdn_up, v_w_out, v_norm_mlp_pre, v_norm_mlp_post, v_w_mlp_up, v_w_mlp_down):
    S, Dm = x.shape[1], D_MODEL
    me = 4 * lax.axis_index("x") + 2 * lax.axis_index("y") + lax.axis_index("c")
    x2, tgt = x[0], loss_target[0]
    n_ada = w_ada.shape[2]
    given = dict(
        w_ada=(w_ada, m_w_ada, v_w_ada), b_ada=(b_ada, m_b_ada, v_b_ada),
        norm_mix_pre=(norm_mix_pre, m_norm_mix_pre, v_norm_mix_pre),
        norm_mix_post=(norm_mix_post, m_norm_mix_post, v_norm_mix_post), w_in=(w_in, m_w_in, v_w_in),
        ssm_conv_w=(ssm_conv_w, m_ssm_conv_w, v_ssm_conv_w), ssm_conv_b=(ssm_conv_b, m_ssm_conv_b, v_ssm_conv_b),
        ssm_dt_bias=(ssm_dt_bias, m_ssm_dt_bias, v_ssm_dt_bias), ssm_A_log=(ssm_A_log, m_ssm_A_log, v_ssm_A_log),
        ssm_D=(ssm_D, m_ssm_D, v_ssm_D), ssm_norm_w=(ssm_norm_w, m_ssm_norm_w, v_ssm_norm_w),
        gdn_conv_w=(gdn_conv_w, m_gdn_conv_w, v_gdn_conv_w), gdn_dt_bias=(gdn_dt_bias, m_gdn_dt_bias, v_gdn_dt_bias),
        gdn_A_log=(gdn_A_log, m_gdn_A_log, v_gdn_A_log), gdn_norm_w=(gdn_norm_w, m_gdn_norm_w, v_gdn_norm_w),
        w_ssm_up=(w_ssm_up, m_w_ssm_up, v_w_ssm_up), w_gdn_up=(w_gdn_up, m_w_gdn_up, v_w_gdn_up),
        w_out=(w_out, m_w_out, v_w_out), norm_mlp_pre=(norm_mlp_pre, m_norm_mlp_pre, v_norm_mlp_pre),
        norm_mlp_post=(norm_mlp_post, m_norm_mlp_post, v_norm_mlp_post), w_mlp_up=(w_mlp_up, m_w_mlp_up, v_w_mlp_up),
        w_mlp_down=(w_mlp_down, m_w_mlp_down, v_w_mlp_down))

    (c_all, scw, gcw, g_in) = _gather_two_level([c, ssm_conv_w[0], gdn_conv_w[0], w_in[0].astype(BF16)], "gather_w_in")
    c_all = c_all.reshape(N_DEV, Dm)
    sp = W_IN_SPLITS
    w_main = jnp.concatenate(_cols_of_shards(g_in, sp[0], sp[2]) + _cols_of_shards(g_in, sp[3], sp[5])
                             + _cols_of_shards(g_in, sp[7], sp[9]), axis=1)
    w_small = jnp.concatenate(_cols_of_shards(g_in, sp[2], sp[3]) + _cols_of_shards(g_in, sp[5], sp[7])
                              + [jnp.zeros((Dm, N_SMALL - 64), BF16)], axis=1)
    conv_w = jnp.concatenate([_by_cols(scw), _by_cols(gcw)], axis=1)
    conv_b = jnp.concatenate([ssm_conv_b, jnp.zeros_like(ssm_conv_b)], axis=1)

    b_loc = lax.dynamic_slice(b_ada, (0, me * n_ada), (1, n_ada))
    mod_part = _ada_fwd(c_all, w_ada[0], b_loc, "ada_fwd")
    (mod_rows,) = _exchange([mod_part.reshape(N_DEV, 1, n_ada)], ["a2a"], "exchange_mod")
    rest = _exchange_start([w_ssm_up[0].astype(BF16), w_gdn_up[0].astype(BF16), w_out[0].astype(BF16),
                            w_mlp_up[0].astype(BF16), w_mlp_down[0].astype(BF16)], ["gather"] * 5, mod_rows,
                           "gather_rest_start")
    mod = mod_rows.reshape(1, 6 * Dm) + rest[4][0:1, 0:1]
    sh1, sc1, g1, sh2, sc2, g2 = [mod[:, i * Dm:(i + 1) * Dm] for i in range(6)]

    h = _pre_fwd(x2, norm_mix_pre, sc1, sh1, "pre_mix")
    proj = _mm(h, w_main, S, N_MAIN, Dm, mode="nn", out_dtype=BF16, tn=MM_WIDE_N, name="proj_main")
    small = _mm(h, w_small, S, N_SMALL, Dm, mode="nn", out_dtype=F32, name="proj_small")
    conv = _conv_fwd(proj, conv_w, conv_b, "conv_fwd")
    dt_g, b_g, a_g = _heads_first(small[:, 0:32], 8), _heads_first(small[:, 32:48], 8), _heads_first(small[:, 48:64], 8)
    pv_ssm = jnp.stack([ssm_dt_bias.reshape(8, 4), ssm_A_log.reshape(8, 4), ssm_D.reshape(8, 4)], axis=1)
    nw_ssm = ssm_norm_w.reshape(8, 1, SSM_GROUP_WIDTH)
    pv_gdn = jnp.stack([gdn_dt_bias.reshape(8, 2), gdn_A_log.reshape(8, 2)], axis=1)
    y_ssd, ysn, st_ssm = _ssd_fwd(conv, proj, dt_g, pv_ssm, nw_ssm, "ssd_fwd")
    o_pre, ogn, st_gdn, t_inv = _gdn_fwd(conv, proj, b_g, a_g, pv_gdn, gdn_norm_w, "gdn_fwd")
    g_su, g_gu, g_out, g_mu, g_md = _exchange_wait(rest, ["gather"] * 5, ogn, "gather_rest_wait")
    w_su, w_gu = g_su.reshape(2 * Dm, Dm), g_gu.reshape(2 * Dm, Dm)
    w_o, w_mu, w_md = g_out.reshape(Dm, Dm), _by_cols(g_mu), g_md.reshape(4 * Dm, Dm)
    ys = _mm(ysn, w_su, S, Dm, 2 * Dm, mode="nn", out_dtype=F32, name="ssm_up")
    yg, merged = _gdn_up_merge(ogn, w_gu, ys, proj, "gdn_up_merge")
    mo, x1, h2 = _mix_out_post_pre(merged, w_o, x2, norm_mix_post, g1, norm_mlp_pre, sc2, sh2, "mix_out_post_pre")
    u, act = _mm(h2, w_mu, S, 4 * Dm, Dm, mode="nn", out_dtype=F32, epi="relu2", tn=MM_WIDE_N, name="mlp_up")
    y_mlp = _mm(act, w_md, S, Dm, 4 * Dm, mode="nn", out_dtype=F32, name="mlp_down")
    dx2, loss_loc, dy, dg2, dw_post2 = _final_fwd_bwd(x1, y_mlp, norm_mlp_post, g2, tgt, "post_mlp_loss_bwd")

    du = _mm(dy, w_md, S, 4 * Dm, Dm, mode="nt", out_dtype=BF16, epi="drelu2", extra=u, tn=MM_WIDE_N, name="mlp_down_dx")
    gw_md = _mm(act, dy, 4 * Dm, Dm, S, mode="tn", out_dtype=BF16, name="mlp_down_dw")
    dh2 = _mm(du, w_mu, S, Dm, 4 * Dm, mode="nt", out_dtype=F32, name="mlp_up_dx")
    gw_mu = _mm(h2, du, Dm, 4 * Dm, S, mode="tn", out_dtype=BF16, name="mlp_up_dw")
    mlp_x = _exchange_start([_to_col_shards(gw_mu), gw_md.reshape(N_DEV, -1, Dm)], ["a2a"] * 2, gw_md,
                            "grads_mlp_start")
    dx1, dsh2, dsc2, dw_pre2 = _pre_bwd(dh2, x1, norm_mlp_pre, sc2 + mlp_x[4][0:1, 0:1], dx2, "pre_mlp_bwd")
    dmo, dg1, dw_post1 = _post_bwd(dx1, mo, norm_mix_post, g1, "post_mix_bwd")
    gw_o = _mm(merged, dmo, Dm, Dm, S, mode="tn", out_dtype=BF16, name="mix_out_dw")
    dys, dyg, d_proj = _mix_out_dx_merge_bwd(dmo, w_o, ys, yg, proj, lax.empty((S, N_MAIN), BF16), "mix_out_dx_merge")
    dysn = _mm(dys, w_su, S, 2 * Dm, Dm, mode="nt", out_dtype=F32, tn=MM_WIDE_N, name="ssm_up_dx")
    gw_su = _mm(ysn, dys, 2 * Dm, Dm, S, mode="tn", out_dtype=BF16, name="ssm_up_dw")
    dogn = _mm(dyg, w_gu, S, 2 * Dm, Dm, mode="nt", out_dtype=F32, tn=MM_WIDE_N, name="gdn_up_dx")
    gw_gu = _mm(ogn, dyg, 2 * Dm, Dm, S, mode="tn", out_dtype=BF16, name="gdn_up_dw")
    mix_x = _exchange_start([gw_su.reshape(N_DEV, -1, Dm), gw_gu.reshape(N_DEV, -1, Dm), gw_o.reshape(N_DEV, -1, Dm)],
                            ["a2a"] * 3, gw_gu, "grads_mix_start")
    d_conv, d_proj, ddt_g, dpv_ssm, dnw_ssm = _ssd_bwd(dysn, conv, proj, dt_g, pv_ssm + mix_x[4][0, 0], nw_ssm, y_ssd,
                                                       st_ssm, d_proj, lax.empty(conv.shape, BF16), "ssd_bwd")
    d_conv, d_proj, db_g, da_g, dpv_gdn, dnw_gdn = _gdn_bwd(dogn, conv, proj, b_g, a_g, pv_gdn, gdn_norm_w, o_pre,
                                                            st_gdn, t_inv, d_proj, d_conv, "gdn_bwd")
    d_proj, dconv_w, dconv_b = _conv_bwd(d_conv, proj, conv_w, conv_b, 0, d_proj, "conv_bwd")
    d_small = jnp.concatenate([_heads_last(ddt_g), _heads_last(db_g), _heads_last(da_g),
                               jnp.zeros((S, N_SMALL - 64), F32)], axis=1).astype(BF16)
    gw_small = _mm(h, d_small, Dm, N_SMALL, S, mode="tn", out_dtype=BF16, name="proj_small_dw")
    main_cols = _mm(h, d_proj, Dm, N_MAIN, S, mode="tn", out_dtype=BF16, name="proj_main_dw")
    n_shard = w_in.shape[2]
    slabs = [_orig_cols(main_cols, gw_small, i * n_shard, (i + 1) * n_shard) for i in range(N_DEV)]
    to_c0, to_c1 = jnp.stack(slabs[0::2]), jnp.stack(slabs[1::2])
    chip_sum = _add_pair(to_c0, to_c1, _swap_sibling(to_c0, to_c1, "grads_w_in_pair"), "grads_w_in_pair_sum")
    in_x = _exchange_start([chip_sum], ["a2a"], gw_small, "grads_w_in_start", chips=True)
    dh = _mm(d_small, w_small + in_x[4][0:1, 0:1].astype(BF16), S, Dm, N_SMALL, mode="nt", out_dtype=F32,
             name="proj_small_dx")
    dh = _mm(d_proj, w_main, S, Dm, N_MAIN, mode="nt", out_dtype=F32, add=dh, name="proj_main_dx")
    dx, dsh1, dsc1, dw_pre1 = _pre_bwd(dh, x2, norm_mix_pre, sc1, dx1, "pre_mix_bwd")

    dconv_b = dconv_b[:, :ssm_conv_b.shape[1]]
    dmod = jnp.concatenate([dsh1, dsc1, dg1, dsh2, dsc2, dg2], axis=1)
    small_vec = jnp.concatenate(
        [dmod, dw_pre1, dw_post1, dconv_b, dpv_ssm[:, 0].reshape(1, 32), dpv_ssm[:, 1].reshape(1, 32),
         dpv_ssm[:, 2].reshape(1, 32), dnw_ssm.reshape(1, 2048), dpv_gdn[:, 0].reshape(1, 16),
         dpv_gdn[:, 1].reshape(1, 16), jnp.sum(dnw_gdn, axis=0), dw_pre2, dw_post2, dconv_w.reshape(1, -1)], axis=1)
    n_vec = small_vec.shape[1]
    small_vec = jnp.pad(small_vec, ((0, 0), (0, (-n_vec) % 1024))).reshape(-1, 1024)
    small_x = _exchange_start([small_vec], ["gather"], dx, "gather_small_start")
    r_mu, r_md = _exchange_wait(mlp_x, ["a2a"] * 2, small_x[4], "grads_mlp_wait")
    r_su, r_gu, r_o = _exchange_wait(mix_x, ["a2a"] * 3, small_x[4], "grads_mix_wait")
    results = {}

    def adam_big(nm, contrib):
        w3 = given[nm]
        res = _adam(contrib, w3[0][0], w3[1][0], w3[2][0], "adam_" + nm)
        results[nm] = tuple(r.reshape(w3[0].shape) for r in res)

    adam_big("w_ssm_up", r_su)
    adam_big("w_gdn_up", r_gu)
    adam_big("w_out", r_o)
    adam_big("w_mlp_up", r_mu)
    adam_big("w_mlp_down", r_md)
    (small_all,) = _exchange_wait(small_x, ["gather"], results["w_mlp_down"][0], "gather_small_wait")
    small_all = small_all.reshape(N_DEV, -1)
    dmod_cols = lax.dynamic_slice(small_all, (0, me * n_ada), (N_DEV, n_ada))
    gw_ada = _ada_bwd(c_all.T, dmod_cols, "ada_bwd")
    conv_all = small_all[:, N_REPLICATED:n_vec].reshape(N_DEV, CONV_K, 2 * N_DEV * 512)
    conv_contrib = jnp.concatenate(
        [lax.dynamic_slice(conv_all, (0, 0, me * 512), (N_DEV, CONV_K, 512)),
         lax.dynamic_slice(conv_all, (0, 0, N_DEV * 512 + me * 512), (N_DEV, CONV_K, 512))], axis=1)
    rep_contrib = small_all[:, :N_REPLICATED].reshape(N_DEV, N_REPLICATED // 128, 128)

    adam_big("w_ada", gw_ada[None])
    (r_in,) = _exchange_wait(in_x, ["a2a"], results["w_ada"][0], "grads_w_in_wait", chips=True)
    adam_big("w_in", r_in)
    packed = [jnp.concatenate([given[nm][i] for nm in REPLICATED], axis=1).reshape(N_REPLICATED // 128, 128)
              for i in range(3)]
    rep_res = _adam(rep_contrib, packed[0], packed[1], packed[2], "adam_replicated")
    pos = 0
    for nm in REPLICATED:
        size = given[nm][0].shape[1]
        results[nm] = tuple(r.reshape(1, N_REPLICATED)[:, pos:pos + size] for r in rep_res)
        pos += size
    conv_wmv = [jnp.concatenate([given["ssm_conv_w"][i][0], given["gdn_conv_w"][i][0]], axis=0) for i in range(3)]
    conv_res = _adam(conv_contrib, conv_wmv[0], conv_wmv[1], conv_wmv[2], "adam_conv_w")
    results["ssm_conv_w"] = tuple(r[None, :CONV_K] for r in conv_res)
    results["gdn_conv_w"] = tuple(r[None, CONV_K:] for r in conv_res)

    loss = lax.psum(loss_loc[0, 0], ("x", "y", "c"))
    return (loss, dx[None]) + tuple(results[nm][i] for i in range(4) for nm in WEIGHTS)
```

```python
import jax
import jax.numpy as jnp
from jax import lax
from jax.experimental import pallas as pl
from jax.experimental.pallas import tpu as pltpu

F32 = jnp.float32
BF16 = jnp.bfloat16
N_DEV = 8
D_MODEL = 1024
EPS = 1e-6
CONV_K = 4
SSM_CHUNK = 128
SSM_HEAD_DIM = 64
SSM_D_STATE = 128
SSM_GROUPS = 8
SSM_HEADS_PER_GROUP = 4
SSM_GROUP_WIDTH = SSM_HEADS_PER_GROUP * SSM_HEAD_DIM
SSM_GROUPS_PER_STEP = 8
GDN_CHUNK = 64
GDN_HEAD = 128
GDN_QK_HEADS = 8
GDN_V_PER_QK = 2
GDN_QK_PER_STEP = 8
GDN_INV_BLOCK = 16
C_ZS, C_XBC, C_QKV, C_ZG, C_GS, C_GG, N_MAIN = 0, 2048, 6144, 10240, 12288, 13312, 14336
N_SMALL = 128
ADAM_LR, ADAM_B1, ADAM_B2, ADAM_EPS, ADAM_WD, ADAM_STEP = 0.001, 0.9, 0.999, 1e-08, 0.01, 10
VMEM_LIMIT = 56 * 1024 * 1024
MM_WHOLE_K = 4096
MM_SPLIT_K = 2048
MM_WIDE_N = 2048
NEG_INF = float("-inf")

_NT = (((1,), (1,)), ((), ()))
_NN = (((1,), (0,)), ((), ()))
_TN = (((0,), (0,)), ((), ()))


def _params(*sem):
    return pltpu.CompilerParams(dimension_semantics=sem, vmem_limit_bytes=VMEM_LIMIT)


def _dot(a, b, dims=_NN):
    return lax.dot_general(a.astype(BF16), b.astype(BF16), dims, preferred_element_type=F32)


def _split(a):
    hi = a.astype(BF16)
    return hi, (a - hi.astype(F32)).astype(BF16)


def _dot3(a, b, dims=_NN):
    ah, al = _split(a)
    bh, bl = _split(b)
    d = lambda u, v: lax.dot_general(u, v, dims, preferred_element_type=F32)
    return d(ah, bh) + (d(ah, bl) + d(al, bh))


def _dot2(a, b, dims=_NN):
    ah, al = _split(a)
    bb = b.astype(BF16)
    d = lambda u: lax.dot_general(u, bb, dims, preferred_element_type=F32)
    return d(ah) + d(al)


def _sigmoid(x):
    return 0.5 * jnp.tanh(0.5 * x) + 0.5


def _silu(x):
    return x * _sigmoid(x)


def _dsilu(x):
    s = _sigmoid(x)
    return s * (1.0 + x * (1.0 - s))


def _softplus(x):
    return jnp.maximum(x, 0.0) + jnp.log1p(jnp.exp(-jnp.abs(x)))


def _iota(n, m, d):
    return lax.broadcasted_iota(jnp.int32, (n, m), d)


def _rowsum(x):
    return jnp.sum(x, axis=1, keepdims=True)


def _colsum(x):
    return jnp.sum(x, axis=0, keepdims=True)


def _total(x):
    return _rowsum(_colsum(x))


MXU_LANES = 128


def _parts(x, n):
    out = []
    for _ in range(n):
        p = x.astype(BF16)
        out.append(p)
        x = x - p.astype(F32)
    return out


def _sum_by(m01, x, dims=_NN, n=3):
    return sum(lax.dot_general(m01, p, dims, preferred_element_type=F32) for p in _parts(x, n))


def _row_col_sums(q):
    ones = jnp.ones((q.shape[0], MXU_LANES), BF16)
    acc = 0.0
    for p in _parts(q, 2):
        acc = acc + (lax.dot_general(p, ones, _NN, preferred_element_type=F32)
                     - lax.dot_general(p, ones, _TN, preferred_element_type=F32))
    return acc[:, 0:1]


def _cumsum_forms(col, ii, jj):
    lower = jnp.where(ii >= jj, 1.0, 0.0).astype(BF16)
    cum_col = _sum_by(lower, jnp.broadcast_to(col, (col.shape[0], MXU_LANES)))[:, 0:1]
    cum_row = _colsum(jnp.where(ii <= jj, col, 0.0))
    return cum_col, cum_row


def _rev_cumsum_col(col, ii, jj):
    upper = jnp.where(ii <= jj, 1.0, 0.0).astype(BF16)
    return _sum_by(upper, jnp.broadcast_to(col, (col.shape[0], MXU_LANES)))[:, 0:1]


def _blk(dim, pref):
    return pref if dim % pref == 0 else dim


def _lockstep(gens):
    gens = list(gens)
    while gens:
        alive = []
        for g in gens:
            try:
                next(g)
                alive.append(g)
            except StopIteration:
                pass
        gens = alive


def _mm(a, b, M, N, K, *, mode, out_dtype, name, a_off=(0, 0), b_off=(0, 0), add=None, epi=None, extra=None,
        tm=1024, tn=1024):
    tm, tn = _blk(M, tm), _blk(N, tn)
    tk = K if K <= MM_WHOLE_K else _blk(K, MM_SPLIT_K)
    nk = K // tk
    if mode == "tn":
        a_spec = pl.BlockSpec((tk, tm), lambda i, j, k: (k + a_off[0] // tk, i + a_off[1] // tm))
        assert a_off[0] % tk == 0 and a_off[1] % tm == 0
    else:
        a_spec = pl.BlockSpec((tm, tk), lambda i, j, k: (i + a_off[0] // tm, k + a_off[1] // tk))
        assert a_off[0] % tm == 0 and a_off[1] % tk == 0
    if mode == "nt":
        b_spec = pl.BlockSpec((tn, tk), lambda i, j, k: (j + b_off[0] // tn, k + b_off[1] // tk))
        assert b_off[0] % tn == 0 and b_off[1] % tk == 0
    else:
        b_spec = pl.BlockSpec((tk, tn), lambda i, j, k: (k + b_off[0] // tk, j + b_off[1] // tn))
        assert b_off[0] % tk == 0 and b_off[1] % tn == 0
    dims = {"nn": _NN, "nt": _NT, "tn": _TN}[mode]
    o_spec = pl.BlockSpec((tm, tn), lambda i, j, k: (i, j))
    ins, in_specs = [a, b], [a_spec, b_spec]
    if add is not None:
        ins.append(add)
        in_specs.append(o_spec)
    if extra is not None:
        ins.append(extra)
        in_specs.append(o_spec)
    n_in = len(ins)
    if epi == "relu2":
        out_shape = (jax.ShapeDtypeStruct((M, N), BF16), jax.ShapeDtypeStruct((M, N), BF16))
        out_specs = (o_spec, o_spec)
    else:
        out_shape = jax.ShapeDtypeStruct((M, N), out_dtype)
        out_specs = o_spec

    def body(*refs):
        a_ref, b_ref = refs[0], refs[1]
        outs = refs[n_in:] if nk == 1 else refs[n_in:-1]

        def finish(r):
            pos = 2
            if add is not None:
                r = r + refs[pos][...]
                pos += 1
            if epi == "relu2":
                p = jnp.maximum(r, 0.0)
                outs[0][...] = p.astype(BF16)
                outs[1][...] = (p * p).astype(BF16)
            elif epi == "drelu2":
                outs[0][...] = (r * (2.0 * refs[pos][...].astype(F32))).astype(out_dtype)
            else:
                outs[0][...] = r.astype(out_dtype)

        if nk == 1:
            finish(_dot(a_ref[...], b_ref[...], dims))
            return
        acc = refs[-1]
        k = pl.program_id(2)

        @pl.when(k == 0)
        def _():
            acc[...] = jnp.zeros_like(acc)

        acc[...] += _dot(a_ref[...], b_ref[...], dims)

        @pl.when(k == nk - 1)
        def _():
            finish(acc[...])

    return pl.pallas_call(
        body, name=name, grid=(M // tm, N // tn, nk), in_specs=in_specs, out_specs=out_specs, out_shape=out_shape,
        scratch_shapes=[] if nk == 1 else [pltpu.VMEM((tm, tn), F32)],
        compiler_params=_params("parallel", "parallel", "arbitrary"))(*ins)


def _mm_rows(a, b, M, N, K, *, mode, name, extras, out_shapes, out_specs, epilogue, aliases=None, tm=512):
    tm = _blk(M, tm)
    a_spec = pl.BlockSpec((tm, K), lambda i: (i, 0))
    b_spec = pl.BlockSpec((K, N) if mode == "nn" else (N, K), lambda i: (0, 0))
    dims = _NN if mode == "nn" else _NT
    n_ex = len(extras)

    def body(a_ref, b_ref, *refs):
        epilogue(_dot(a_ref[...], b_ref[...], dims), refs[:n_ex], refs[n_ex:])

    return pl.pallas_call(
        body, name=name, grid=(M // tm,), in_specs=[a_spec, b_spec] + [sp for _, sp in extras],
        out_specs=tuple(out_specs), out_shape=tuple(out_shapes), input_output_aliases=aliases or {},
        compiler_params=_params("parallel"))(a, b, *[x for x, _ in extras])


def _row_spec(tb, d):
    return pl.BlockSpec((tb, d), lambda i: (i, 0))


def _vec_spec(d):
    return pl.BlockSpec((1, d), lambda i: (0, 0))


def _pre_fwd(x, w, sc, sh, name):
    S, Dm = x.shape
    tb = _blk(S, 512)

    def body(x_ref, w_ref, sc_ref, sh_ref, h_ref):
        xv = x_ref[...]
        r = lax.rsqrt(jnp.mean(xv * xv, axis=-1, keepdims=True) + EPS)
        h_ref[...] = ((xv * r * w_ref[...]) * (1.0 + sc_ref[...]) + sh_ref[...]).astype(BF16)

    return pl.pallas_call(
        body, name=name, grid=(S // tb,), in_specs=[_row_spec(tb, Dm)] + [_vec_spec(Dm)] * 3,
        out_specs=_row_spec(tb, Dm), out_shape=jax.ShapeDtypeStruct((S, Dm), BF16),
        compiler_params=_params("parallel"))(x, w, sc, sh)


def _final_fwd_bwd(x, y, w, g, target, name):
    S, Dm = x.shape
    tb = _blk(S, 512)
    nb = S // tb

    def body(x_ref, y_ref, w_ref, g_ref, t_ref, dx_ref, loss_ref, dy_ref, dg_ref, dw_ref, acc):
        i = pl.program_id(0)

        @pl.when(i == 0)
        def _():
            acc[...] = jnp.zeros_like(acc)
            dg_ref[...] = jnp.zeros_like(dg_ref)
            dw_ref[...] = jnp.zeros_like(dw_ref)

        yv = y_ref[...]
        r = lax.rsqrt(jnp.mean(yv * yv, axis=-1, keepdims=True) + EPS)
        yh = yv * r
        n = yh * w_ref[...]
        e = (x_ref[...] + g_ref[...] * n) - t_ref[...]
        dv = e * (1.0 / Dm)
        dx_ref[...] = dv
        acc[...] += _colsum(e * e)
        dg_ref[...] += _colsum(dv * n)
        dn = dv * g_ref[...]
        dw_ref[...] += _colsum(dn * yh)
        dyh = dn * w_ref[...]
        dy_ref[...] = (r * (dyh - yh * jnp.mean(dyh * yh, axis=-1, keepdims=True))).astype(BF16)

        @pl.when(i == nb - 1)
        def _():
            loss_ref[...] = (0.5 / Dm) * _rowsum(acc[...])

    row, vec = _row_spec(tb, Dm), _vec_spec(Dm)
    vec_shape = jax.ShapeDtypeStruct((1, Dm), F32)
    return pl.pallas_call(
        body, name=name, grid=(nb,), in_specs=[row, row, vec, vec, row],
        out_specs=(row, pl.BlockSpec((1, 1), lambda i: (0, 0)), row, vec, vec),
        out_shape=(jax.ShapeDtypeStruct((S, Dm), F32), jax.ShapeDtypeStruct((1, 1), F32),
                   jax.ShapeDtypeStruct((S, Dm), BF16), vec_shape, vec_shape),
        scratch_shapes=[pltpu.VMEM((1, Dm), F32)], compiler_params=_params("arbitrary"))(x, y, w, g, target)


def _post_bwd(dxo, y, w, g, name):
    S, Dm = y.shape
    tb = _blk(S, 512)

    def body(d_ref, y_ref, w_ref, g_ref, dy_ref, dg_ref, dw_ref):
        i = pl.program_id(0)

        @pl.when(i == 0)
        def _():
            dg_ref[...] = jnp.zeros_like(dg_ref)
            dw_ref[...] = jnp.zeros_like(dw_ref)

        yv, dv = y_ref[...], d_ref[...]
        r = lax.rsqrt(jnp.mean(yv * yv, axis=-1, keepdims=True) + EPS)
        yh = yv * r
        dg_ref[...] += _colsum(dv * (yh * w_ref[...]))
        dn = dv * g_ref[...]
        dw_ref[...] += _colsum(dn * yh)
        dyh = dn * w_ref[...]
        dy_ref[...] = (r * (dyh - yh * jnp.mean(dyh * yh, axis=-1, keepdims=True))).astype(BF16)

    return pl.pallas_call(
        body, name=name, grid=(S // tb,), in_specs=[_row_spec(tb, Dm)] * 2 + [_vec_spec(Dm)] * 2,
        out_specs=(_row_spec(tb, Dm), _vec_spec(Dm), _vec_spec(Dm)),
        out_shape=(jax.ShapeDtypeStruct((S, Dm), BF16), jax.ShapeDtypeStruct((1, Dm), F32),
                   jax.ShapeDtypeStruct((1, Dm), F32)),
        compiler_params=_params("arbitrary"))(dxo, y, w, g)


def _pre_bwd(dh, x, w, sc, dres, name):
    S, Dm = x.shape
    tb = _blk(S, 512)

    def body(dh_ref, x_ref, w_ref, sc_ref, dr_ref, dx_ref, dsh_ref, dsc_ref, dw_ref):
        i = pl.program_id(0)

        @pl.when(i == 0)
        def _():
            dsh_ref[...] = jnp.zeros_like(dsh_ref)
            dsc_ref[...] = jnp.zeros_like(dsc_ref)
            dw_ref[...] = jnp.zeros_like(dw_ref)

        xv, dv = x_ref[...], dh_ref[...]
        r = lax.rsqrt(jnp.mean(xv * xv, axis=-1, keepdims=True) + EPS)
        xh = xv * r
        one_sc = 1.0 + sc_ref[...]
        dsh_ref[...] += _colsum(dv)
        dsc_ref[...] += _colsum(dv * (xh * w_ref[...]))
        dw_ref[...] += _colsum(dv * one_sc * xh)
        dxh = dv * one_sc * w_ref[...]
        dx_ref[...] = dr_ref[...] + r * (dxh - xh * jnp.mean(dxh * xh, axis=-1, keepdims=True))

    vec = jax.ShapeDtypeStruct((1, Dm), F32)
    return pl.pallas_call(
        body, name=name, grid=(S // tb,),
        in_specs=[_row_spec(tb, Dm)] * 2 + [_vec_spec(Dm)] * 2 + [_row_spec(tb, Dm)],
        out_specs=(_row_spec(tb, Dm), _vec_spec(Dm), _vec_spec(Dm), _vec_spec(Dm)),
        out_shape=(jax.ShapeDtypeStruct((S, Dm), F32), vec, vec, vec),
        compiler_params=_params("arbitrary"))(dh, x, w, sc, dres)


D_PROJ_ANY = pl.BlockSpec(memory_space=pl.ANY)


def _gate_specs(tm, Dm):
    return (pl.BlockSpec((tm, Dm), lambda i: (i, C_GS // Dm)), pl.BlockSpec((tm, Dm), lambda i: (i, C_GG // Dm)))


def _gdn_up_merge(ogn, w_gu, ys, proj, name):
    S, K = ogn.shape
    Dm = ys.shape[1]
    tm = _blk(S, 512)
    row = _row_spec(tm, Dm)

    def epilogue(r, ex, out):
        ys_ref, gs_ref, gg_ref = ex
        out[0][...] = r
        out[1][...] = (_sigmoid(gs_ref[...].astype(F32)) * ys_ref[...]
                       + _sigmoid(gg_ref[...].astype(F32)) * r).astype(BF16)

    gs_spec, gg_spec = _gate_specs(tm, Dm)
    return _mm_rows(ogn, w_gu, S, Dm, K, mode="nn", name=name, tm=tm,
                    extras=[(ys, row), (proj, gs_spec), (proj, gg_spec)],
                    out_shapes=[jax.ShapeDtypeStruct((S, Dm), F32), jax.ShapeDtypeStruct((S, Dm), BF16)],
                    out_specs=[row, row], epilogue=epilogue)


def _mix_out_post_pre(merged, w_o, x, w_post, g, w_pre, sc, sh, name):
    S, Dm = x.shape
    tm = _blk(S, 512)
    row, vec = _row_spec(tm, Dm), _vec_spec(Dm)

    def epilogue(r, ex, out):
        x_ref, wpost_ref, g_ref, wpre_ref, sc_ref, sh_ref = ex
        out[0][...] = r
        rr = lax.rsqrt(jnp.mean(r * r, axis=-1, keepdims=True) + EPS)
        x1 = x_ref[...] + g_ref[...] * (r * rr * wpost_ref[...])
        out[1][...] = x1
        r1 = lax.rsqrt(jnp.mean(x1 * x1, axis=-1, keepdims=True) + EPS)
        out[2][...] = ((x1 * r1 * wpre_ref[...]) * (1.0 + sc_ref[...]) + sh_ref[...]).astype(BF16)

    return _mm_rows(merged, w_o, S, Dm, Dm, mode="nn", name=name, tm=tm,
                    extras=[(x, row), (w_post, vec), (g, vec), (w_pre, vec), (sc, vec), (sh, vec)],
                    out_shapes=[jax.ShapeDtypeStruct((S, Dm), F32), jax.ShapeDtypeStruct((S, Dm), F32),
                                jax.ShapeDtypeStruct((S, Dm), BF16)], out_specs=[row, row, row], epilogue=epilogue)


def _mix_out_dx_merge_bwd(dmo, w_o, ys, yg, proj, d_proj, name):
    S, Dm = ys.shape
    tm = _blk(S, 512)
    row = _row_spec(tm, Dm)

    def epilogue(d, ex, out):
        ys_ref, yg_ref, gs_ref, gg_ref, _ = ex
        ss, sg = _sigmoid(gs_ref[...].astype(F32)), _sigmoid(gg_ref[...].astype(F32))
        out[0][...] = (d * ss).astype(BF16)
        out[1][...] = (d * sg).astype(BF16)
        out[2][:, :Dm] = (d * ys_ref[...] * ss * (1.0 - ss)).astype(BF16)
        out[2][:, Dm:] = (d * yg_ref[...] * sg * (1.0 - sg)).astype(BF16)

    gs_spec, gg_spec = _gate_specs(tm, Dm)
    return _mm_rows(dmo, w_o, S, Dm, Dm, mode="nt", name=name, tm=tm,
                    extras=[(ys, row), (yg, row), (proj, gs_spec), (proj, gg_spec), (d_proj, D_PROJ_ANY)],
                    out_shapes=[jax.ShapeDtypeStruct((S, Dm), BF16), jax.ShapeDtypeStruct((S, Dm), BF16),
                                jax.ShapeDtypeStruct(d_proj.shape, BF16)],
                    out_specs=[row, row, pl.BlockSpec((tm, 2 * Dm), lambda i: (i, C_GS // (2 * Dm)))],
                    epilogue=epilogue, aliases={6: 2})


CONV_COLS = 128
CONV_BWD_ROWS = 256


def _taps_down(x):
    rows = _iota(x.shape[0], x.shape[1], 0)
    return [x] + [jnp.where(rows >= k, pltpu.roll(x, k, 0), 0.0) for k in range(1, CONV_K)]


def _conv_pre(taps, w_ref, b_ref):
    pre = taps[0] * w_ref[CONV_K - 1:CONV_K, :] + b_ref[...]
    for k in range(1, CONV_K):
        pre = pre + taps[k] * w_ref[CONV_K - 1 - k:CONV_K - k, :]
    return pre


def _conv_dx(dpre, w_ref):
    n = dpre.shape[0]
    rows = _iota(n, dpre.shape[1], 0)
    dx = dpre * w_ref[CONV_K - 1:CONV_K, :]
    for k in range(1, CONV_K):
        dx = dx + jnp.where(rows < n - k, pltpu.roll(dpre, n - k, 0), 0.0) * w_ref[CONV_K - 1 - k:CONV_K - k, :]
    return dx


def _conv_fwd(proj, w, b, name):
    S = proj.shape[0]
    n = w.shape[1]
    cb = CONV_COLS

    def body(x_ref, w_ref, b_ref, o_ref):
        o_ref[...] = _silu(_conv_pre(_taps_down(x_ref[...].astype(F32)), w_ref, b_ref)).astype(BF16)

    return pl.pallas_call(
        body, name=name, grid=(n // cb,),
        in_specs=[pl.BlockSpec((S, cb), lambda j: (0, j + C_XBC // cb)), pl.BlockSpec((CONV_K, cb), lambda j: (0, j)),
                  pl.BlockSpec((1, cb), lambda j: (0, j))],
        out_specs=pl.BlockSpec((S, cb), lambda j: (0, j)), out_shape=jax.ShapeDtypeStruct((S, n), BF16),
        compiler_params=_params("parallel"))(proj, w, b)


def _conv_bwd(dact, proj, w, b, col0, d_proj, name):
    S, n = dact.shape
    cb = CONV_COLS
    o = col0 // cb

    R, HALO = _blk(S, CONV_BWD_ROWS), 16
    n_chunks = S // R

    def body(d_ref, x_ref, w_ref, b_ref, _, dx_ref, dw_ref, db_ref):
        def chunk(r0, first, last, sums):
            lo, hi = (0 if first else HALO), (0 if last else HALO)
            start = r0 - lo if isinstance(r0, int) else pl.multiple_of(r0 - lo, HALO)
            xe = x_ref[pl.ds(start, lo + R + hi), :].astype(F32)
            rows = _iota(lo + R + hi, cb, 0)
            taps = [xe[lo:, :]]
            for k in range(1, CONV_K):
                t = pltpu.roll(xe, k, 0)
                taps.append((jnp.where(rows >= k, t, 0.0) if first else t)[lo:, :])
            dpre_e = d_ref[pl.ds(r0, R + hi), :].astype(F32) * _dsilu(_conv_pre(taps, w_ref, b_ref))
            dpre = dpre_e[0:R, :]
            db, dw = sums
            db = db + _colsum(dpre)
            dw = [dw[k] + _colsum(dpre * taps[k][0:R, :]) for k in range(CONV_K)]
            rows_e = _iota(R + hi, cb, 0)
            dx = dpre * w_ref[CONV_K - 1:CONV_K, :]
            for k in range(1, CONV_K):
                t = pltpu.roll(dpre_e, R + hi - k, 0)
                t = jnp.where(rows_e < R - k, t, 0.0) if last else t
                dx = dx + t[0:R, :] * w_ref[CONV_K - 1 - k:CONV_K - k, :]
            dx_ref[pl.ds(r0, R), :] = dx.astype(BF16)
            return db, dw

        zero = jnp.zeros((1, cb), F32)
        sums = chunk(0, True, n_chunks == 1, (zero, [zero] * CONV_K))
        if n_chunks > 2:
            def step(i, carry):
                db, dw = chunk(pl.multiple_of(i * R, R), False, False, (carry[0], list(carry[1:])))
                return (db,) + tuple(dw)
            carry = lax.fori_loop(1, n_chunks - 1, step, (sums[0],) + tuple(sums[1]))
            sums = (carry[0], list(carry[1:]))
        if n_chunks > 1:
            sums = chunk((n_chunks - 1) * R, False, True, sums)
        db_ref[...] = sums[0]
        for k in range(CONV_K):
            dw_ref[CONV_K - 1 - k:CONV_K - k, :] = sums[1][k]

    return pl.pallas_call(
        body, name=name, grid=(n // cb,),
        in_specs=[pl.BlockSpec((S, cb), lambda j: (0, j)), pl.BlockSpec((S, cb), lambda j: (0, j + o + C_XBC // cb)),
                  pl.BlockSpec((CONV_K, cb), lambda j: (0, j + o)), pl.BlockSpec((1, cb), lambda j: (0, j + o)),
                  D_PROJ_ANY],
        out_specs=(pl.BlockSpec((S, cb), lambda j: (0, j + o + C_XBC // cb)),
                   pl.BlockSpec((CONV_K, cb), lambda j: (0, j)), pl.BlockSpec((1, cb), lambda j: (0, j))),
        out_shape=(jax.ShapeDtypeStruct(d_proj.shape, BF16), jax.ShapeDtypeStruct((CONV_K, n), F32),
                   jax.ShapeDtypeStruct((1, n), F32)),
        input_output_aliases={4: 0}, compiler_params=_params("parallel"))(dact, proj, w, b, d_proj)


def _ssd_specs(L, order):
    G = SSM_GROUPS_PER_STEP
    W, N = G * SSM_GROUP_WIDTH, G * SSM_D_STATE
    x_spec = pl.BlockSpec((L, W), lambda g, c: (order(c), g))
    b_spec = pl.BlockSpec((L, N), lambda g, c: (order(c), 2048 // N + g))
    c_spec = pl.BlockSpec((L, N), lambda g, c: (order(c), 3072 // N + g))
    z_spec = pl.BlockSpec((L, W), lambda g, c: (order(c), C_ZS // W + g))
    dt_spec = pl.BlockSpec((G, L, SSM_HEADS_PER_GROUP), lambda g, c: (g, order(c), 0))
    p_spec = pl.BlockSpec((G, 3, SSM_HEADS_PER_GROUP), lambda g, c: (g, 0, 0))
    nw_spec = pl.BlockSpec((G, 1, SSM_GROUP_WIDTH), lambda g, c: (g, 0, 0))
    s_spec = pl.BlockSpec((G, 1, SSM_GROUP_WIDTH, SSM_D_STATE), lambda g, c: (g, order(c), 0, 0))
    return x_spec, b_spec, c_spec, z_spec, dt_spec, p_spec, nw_spec, s_spec


class _SsdGroup:
    def __init__(self, L):
        P, H, W = SSM_HEAD_DIM, SSM_HEADS_PER_GROUP, SSM_GROUP_WIDTH
        self.L = L
        self.ii, self.jj = _iota(L, L, 0), _iota(L, L, 1)
        self.lower = jnp.where(self.ii >= self.jj, 1.0, 0.0).astype(BF16)
        self.upper = jnp.where(self.ii <= self.jj, 1.0, 0.0).astype(BF16)
        self.lo = _iota(L, 2 * P, 1) < P
        self.lo_row = _iota(1, 2 * P, 1) < P
        bi, bj = _iota(W, W, 0), _iota(W, W, 1)
        self.block = jnp.where(bi // P == bj // P, 1.0, 0.0).astype(BF16)
        si, sj = _iota(2 * P, W, 0), _iota(2 * P, W, 1)
        self.pick = jnp.where(sj == si * P, 1.0, 0.0).astype(BF16)
        self.ones = jnp.ones((L, 2 * P), BF16)

    def spread(self, v4):
        R = v4.shape[0]
        lo = self.lo if R == self.L else self.lo_row
        b = lambda h: jnp.broadcast_to(v4[:, h:h + 1], (R, 2 * SSM_HEAD_DIM))
        return jnp.concatenate([jnp.where(lo, b(0), b(1)), jnp.where(lo, b(2), b(3))], axis=1)

    def gather4(self, v):
        return jnp.concatenate([v[:, h * SSM_HEAD_DIM:h * SSM_HEAD_DIM + 1] for h in range(SSM_HEADS_PER_GROUP)],
                               axis=1)

    def head_sums(self, z):
        return sum(lax.dot_general(p, self.block, _NN, preferred_element_type=F32) for p in _parts(z, 2))

    def pair_cols(self, full, pair):
        ps = full[:, pair * 128:(pair + 1) * 128]
        sw = pltpu.roll(ps, SSM_HEAD_DIM, 1)
        return jnp.where(self.lo, ps, sw), jnp.where(self.lo, sw, ps)

    def gates(self, dt4_raw, p):
        L = self.L
        dtr = self.spread(dt4_raw + p[0:1, :])
        dt = _softplus(dtr)
        A = self.spread(-jnp.exp(p[1:2, :]))
        acum = _sum_by(self.lower, dt * A)
        yield
        rows = _sum_by(self.pick, acum, _NT)
        yield
        a_last = acum[L - 1:L, :]
        cols = self.pair_cols(acum, 0) + self.pair_cols(acum, 1)
        decay, decay_t = [], []
        for h in range(SSM_HEADS_PER_GROUP):
            seg = cols[h] - rows[h:h + 1, :]
            decay.append(jnp.exp(jnp.where(self.ii >= self.jj, seg, NEG_INF)))
            decay_t.append(jnp.exp(jnp.where(self.jj >= self.ii, -seg, NEG_INF)))
        return dict(dtr=dtr, dt=dt, A=A, D=self.spread(p[2:3, :]), acum=acum, eac=jnp.exp(acum), a_last=a_last,
                    wdec=jnp.exp(a_last - acum), decay=decay, decay_t=decay_t,
                    ea_last=[jnp.exp(rows[h:h + 1, L - 1:L]) for h in range(SSM_HEADS_PER_GROUP)])


def _ssd_fwd(conv, proj, dt_raw, pvec, nw, name):
    S = conv.shape[0]
    L, P, N, H, W, G = SSM_CHUNK, SSM_HEAD_DIM, SSM_D_STATE, SSM_HEADS_PER_GROUP, SSM_GROUP_WIDTH, SSM_GROUPS_PER_STEP
    nc = S // L

    def body(x_ref, b_ref, c_ref, z_ref, dt_ref, p_ref, nw_ref, y_ref, yn_ref, s0_ref, state):
        c = pl.program_id(1)

        @pl.when(c == 0)
        def _():
            state[...] = jnp.zeros_like(state)

        k = _SsdGroup(L)

        def group(gi):
            gsl = slice(gi * W, (gi + 1) * W)
            Bm, Cm = b_ref[:, gi * N:(gi + 1) * N], c_ref[:, gi * N:(gi + 1) * N]
            x = x_ref[:, gsl].astype(F32)
            S0 = state[gsl, :]
            s0_ref[gi, 0] = S0
            CB = _dot(Cm, Bm, _NT)
            y_off = _dot(Cm, S0, _NT)
            t = yield from k.gates(dt_ref[gi], p_ref[gi])
            xdt = x * t["dt"]
            s_new = _dot(xdt * t["wdec"], Bm, _TN)
            y_diag = []
            for pair in range(H // 2):
                xp = xdt[:, pair * 128:(pair + 1) * 128]
                y_diag.append(jnp.where(k.lo, _dot(CB * t["decay"][2 * pair], xp),
                                        _dot(CB * t["decay"][2 * pair + 1], xp)))
            yield
            y = jnp.concatenate(y_diag, axis=1) + y_off * t["eac"]
            for h in range(H):
                hsl = slice(gi * W + h * P, gi * W + (h + 1) * P)
                state[hsl, :] = S0[h * P:(h + 1) * P, :] * t["ea_last"][h] + s_new[h * P:(h + 1) * P, :]
            y_ref[:, gsl] = y
            y2 = (y + t["D"] * x) * _silu(z_ref[:, gsl].astype(F32))
            r = lax.rsqrt(jnp.mean(y2 * y2, axis=-1, keepdims=True) + EPS)
            yn_ref[:, gsl] = (y2 * r * nw_ref[gi]).astype(BF16)

        _lockstep(group(gi) for gi in range(G))

    x_spec, b_spec, c_spec, z_spec, dt_spec, p_spec, nw_spec, s_spec = _ssd_specs(L, lambda c: c)
    y_spec = pl.BlockSpec((L, G * W), lambda g, c: (c, g))
    return pl.pallas_call(
        body, name=name, grid=(SSM_GROUPS // G, nc),
        in_specs=[x_spec, b_spec, c_spec, z_spec, dt_spec, p_spec, nw_spec],
        out_specs=(y_spec, y_spec, s_spec),
        out_shape=(jax.ShapeDtypeStruct((S, SSM_GROUPS * W), F32), jax.ShapeDtypeStruct((S, SSM_GROUPS * W), BF16),
                   jax.ShapeDtypeStruct((SSM_GROUPS, nc, W, N), F32)),
        scratch_shapes=[pltpu.VMEM((G * W, N), F32)],
        compiler_params=_params("parallel", "arbitrary"))(conv, conv, conv, proj, dt_raw, pvec, nw)


def _ssd_bwd(dyn, conv, proj, dt_raw, pvec, nw, y_ssd, states, d_proj, d_conv, name):
    S = conv.shape[0]
    L, P, N, H, W, G = SSM_CHUNK, SSM_HEAD_DIM, SSM_D_STATE, SSM_HEADS_PER_GROUP, SSM_GROUP_WIDTH, SSM_GROUPS_PER_STEP
    nc = S // L
    assert G == SSM_GROUPS
    CB0, CC0 = SSM_GROUPS * W, SSM_GROUPS * (W + N)

    def body(dyn_ref, x_ref, b_ref, c_ref, z_ref, dt_ref, p_ref, nw_ref, y_ref, s0_ref, _, _2,
             dact_ref, dz_ref, ddt_ref, dp_ref, dnw_ref, dstate):
        c = pl.program_id(1)

        @pl.when(c == 0)
        def _():
            dstate[...] = jnp.zeros_like(dstate)
            dp_ref[...] = jnp.zeros_like(dp_ref)
            dnw_ref[...] = jnp.zeros_like(dnw_ref)

        k = _SsdGroup(L)
        last = (_iota(L, 1, 0) == L - 1)

        def group(gi):
            gsl = slice(gi * W, (gi + 1) * W)
            Bm, Cm = b_ref[:, gi * N:(gi + 1) * N], c_ref[:, gi * N:(gi + 1) * N]
            x, z = x_ref[:, gsl].astype(F32), z_ref[:, gsl].astype(F32)
            S0, dS1 = s0_ref[gi, 0], dstate[gsl, :]
            CB = _dot(Cm, Bm, _NT)
            CBt = _dot(Bm, Cm, _NT)
            y_off_raw = _dot(Cm, S0, _NT)
            dXs_raw = _dot(Bm, dS1, _NT)
            t = yield from k.gates(dt_ref[gi], p_ref[gi])
            y1 = y_ref[:, gsl] + t["D"] * x
            sz = _silu(z)
            y2 = y1 * sz
            r = lax.rsqrt(jnp.mean(y2 * y2, axis=-1, keepdims=True) + EPS)
            y2h = y2 * r
            dyn_v = dyn_ref[:, gsl]
            dnw_ref[gi] += _colsum(dyn_v * y2h)
            dy2h = dyn_v * nw_ref[gi]
            dy2 = r * (dy2h - y2h * jnp.mean(dy2h * y2h, axis=-1, keepdims=True))
            dz_ref[:, gsl] = (dy2 * y1 * _dsilu(z)).astype(BF16)
            dY = dy2 * sz
            X = x * t["dt"]
            dYe = dY * t["eac"]
            dC_s = _dot(dYe, S0)
            dB_s = _dot(X * t["wdec"], dS1)
            dS_c = _dot(dYe, Cm, _TN)
            dXm, Gs, Gts = [], [], []
            for pair in range(H // 2):
                dYp, Xp = dY[:, pair * 128:(pair + 1) * 128], X[:, pair * 128:(pair + 1) * 128]
                dXm.append(jnp.where(k.lo, _dot(CBt * t["decay_t"][2 * pair], dYp),
                                     _dot(CBt * t["decay_t"][2 * pair + 1], dYp)))
                for mask in (k.lo, ~k.lo):
                    Gs.append(_dot(jnp.where(mask, dYp, 0.0), Xp, _NT))
                    Gts.append(_dot(jnp.where(mask, Xp, 0.0), dYp, _NT))
            yield
            dXs = dXs_raw * t["wdec"]
            dX = jnp.concatenate(dXm, axis=1) + dXs
            dCB, dCBt, q_sums = 0.0, 0.0, []
            for h in range(H):
                g_dec, gt_dec = Gs[h] * t["decay"][h], Gts[h] * t["decay_t"][h]
                dCB = dCB + g_dec
                dCBt = dCBt + gt_dec
                d = CB * g_dec - CBt * gt_dec
                q_sums.append(sum(lax.dot_general(pt, k.ones, _NN, preferred_element_type=F32)
                                  for pt in _parts(d, 2)))
            q_f = jnp.concatenate([jnp.where(k.lo, q_sums[0], q_sums[1]), jnp.where(k.lo, q_sums[2], q_sums[3])],
                                  axis=1)
            x_dxs = X * dXs
            tot = [_total(dS1[h * P:(h + 1) * P, :] * S0[h * P:(h + 1) * P, :]) * t["ea_last"][h] for h in range(H)]
            tot_f = k.spread(jnp.concatenate(tot, axis=1))
            d_alast = k.head_sums(jnp.broadcast_to(_colsum(x_dxs), (8, W)))[0:1, :] + tot_f
            dacum = q_f + k.head_sums(dY * (y_off_raw * t["eac"]) - x_dxs) + jnp.where(last, d_alast, 0.0)
            dx_dt = k.head_sums(dX * x)
            d_skip = _colsum(k.head_sums(dY * x))
            for h in range(H):
                hsl = slice(gi * W + h * P, gi * W + (h + 1) * P)
                dstate[hsl, :] = t["ea_last"][h] * dS1[h * P:(h + 1) * P, :] + dS_c[h * P:(h + 1) * P, :]
            dc_s2 = _dot(dCB, Bm)
            db_s2 = _dot(dCBt, Cm)
            yield
            da = _sum_by(k.upper, dacum)
            yield
            ddt_raw = (da * t["A"] + dx_dt) * _sigmoid(t["dtr"])
            dact_ref[:, gsl] = (dX * t["dt"] + t["D"] * dY).astype(BF16)
            dact_ref[:, CC0 + gi * N:CC0 + (gi + 1) * N] = (dC_s + dc_s2).astype(BF16)
            dact_ref[:, CB0 + gi * N:CB0 + (gi + 1) * N] = (dB_s + db_s2).astype(BF16)
            ddt_ref[gi] = k.gather4(ddt_raw)
            dp_ref[gi] += k.gather4(jnp.concatenate([_colsum(ddt_raw), _colsum(da * t["dt"]) * t["A"], d_skip],
                                                    axis=0))

        _lockstep(group(gi) for gi in range(G))

    rev = lambda c: nc - 1 - c
    x_spec, b_spec, c_spec, z_spec, dt_spec, p_spec, nw_spec, s_spec = _ssd_specs(L, rev)
    y_spec = pl.BlockSpec((L, G * W), lambda g, c: (rev(c), g))
    half = d_conv.shape[1] // 2
    return pl.pallas_call(
        body, name=name, grid=(SSM_GROUPS // G, nc),
        in_specs=[y_spec, x_spec, b_spec, c_spec, z_spec, dt_spec, p_spec, nw_spec, y_spec, s_spec, D_PROJ_ANY,
                  D_PROJ_ANY],
        out_specs=(pl.BlockSpec((L, half), lambda g, c: (rev(c), 0)), z_spec, dt_spec, p_spec, nw_spec),
        out_shape=(jax.ShapeDtypeStruct(d_conv.shape, BF16), jax.ShapeDtypeStruct(d_proj.shape, BF16),
                   jax.ShapeDtypeStruct((SSM_GROUPS, S, H), F32), jax.ShapeDtypeStruct((SSM_GROUPS, 3, H), F32),
                   jax.ShapeDtypeStruct((SSM_GROUPS, 1, W), F32)),
        scratch_shapes=[pltpu.VMEM((G * W, N), F32)], input_output_aliases={10: 1, 11: 0},
        compiler_params=_params("parallel", "arbitrary"))(dyn, conv, conv, conv, proj, dt_raw, pvec, nw, y_ssd, states,
                                                          d_proj, d_conv)


def _unit_lower_inverse(A, ii, jj):
    eye = (ii == jj).astype(F32)
    same = (ii // GDN_INV_BLOCK) == (jj // GDN_INV_BLOCK)
    Ad = jnp.where(same, A, 0.0)
    Ao = A - Ad
    P2 = _dot3(Ad, Ad)
    yield
    P4, X = _dot(P2, P2), _dot3(eye - Ad, eye + P2)
    yield
    P8, X = _dot(P4, P4), X + _dot2(X, P4)
    yield
    X = X + _dot2(X, P8)
    yield
    Bm = _dot3(X, Ao)
    yield
    B2 = _dot3(Bm, Bm)
    yield
    Y = (eye - Bm) + B2 - _dot2(Bm, B2)
    yield
    T = _dot3(Y, X)
    yield
    return T


def _gdn_specs(L, order):
    G = GDN_QK_PER_STEP
    Hd, W = G * GDN_HEAD, G * GDN_V_PER_QK * GDN_HEAD
    q_spec = pl.BlockSpec((L, Hd), lambda h, c: (order(c), (C_QKV - C_XBC) // Hd + h))
    k_spec = pl.BlockSpec((L, Hd), lambda h, c: (order(c), (C_QKV - C_XBC + 1024) // Hd + h))
    v_spec = pl.BlockSpec((L, W), lambda h, c: (order(c), (C_QKV - C_XBC + 2048) // W + h))
    z_spec = pl.BlockSpec((L, W), lambda h, c: (order(c), C_ZG // W + h))
    ba_spec = pl.BlockSpec((G, L, GDN_V_PER_QK), lambda h, c: (h, order(c), 0))
    p_spec = pl.BlockSpec((G, 2, GDN_V_PER_QK), lambda h, c: (h, 0, 0))
    nw_spec = pl.BlockSpec((1, GDN_HEAD), lambda h, c: (0, 0))
    s_spec = pl.BlockSpec((G, 1, GDN_V_PER_QK * GDN_HEAD, GDN_HEAD), lambda h, c: (h, order(c), 0, 0))
    t_spec = pl.BlockSpec((G * GDN_V_PER_QK, 1, L, 2 * L), lambda h, c: (h, order(c), 0, 0))
    return q_spec, k_spec, v_spec, z_spec, ba_spec, p_spec, nw_spec, s_spec, t_spec


def _gdn_qk(qa, ka):
    rq = lax.rsqrt(_rowsum(qa * qa) + EPS)
    rk = lax.rsqrt(_rowsum(ka * ka) + EPS)
    q, k = qa * rq * (GDN_HEAD ** -0.5), ka * rk
    return dict(qa=qa, rq=rq, rk=rk, q=q, k=k, QK=_dot(q, k, _NT))


def _gdn_gates(qk, b_col, a_col, p, j, ii, jj):
    L = qk["q"].shape[0]
    rq, rk, q, k = qk["rq"], qk["rk"], qk["q"], qk["k"]
    sp_in = a_col + p[0:1, j:j + 1]
    neg_ea = -jnp.exp(p[1:2, j:j + 1])
    g = neg_ea * _softplus(sp_in)
    gcum, gcum_row = _cumsum_forms(g, ii, jj)
    beta = _sigmoid(b_col)
    yield
    Dm = jnp.exp(jnp.where(ii >= jj, gcum - gcum_row, NEG_INF))
    eg = jnp.exp(gcum)
    g_last = gcum[L - 1:L, :]
    wdec = jnp.exp(g_last - gcum)
    return dict(rq=rq, rk=rk, q=q, k=k, beta=beta, sp_in=sp_in, neg_ea=neg_ea, g=g, Dm=Dm, kbeta=k * beta, eg=eg,
                g_last=g_last, wdec=wdec, kdec=k * wdec)


def _gdn_fwd(conv, proj, b_raw, a_raw, pvec, nw, name):
    S = conv.shape[0]
    L, Hd, J, G = GDN_CHUNK, GDN_HEAD, GDN_V_PER_QK, GDN_QK_PER_STEP
    W = J * Hd
    nc = S // L

    def body(q_ref, k_ref, v_ref, z_ref, b_ref, a_ref, p_ref, nw_ref, o_ref, on_ref, s0_ref, t_ref, state):
        c = pl.program_id(1)

        @pl.when(c == 0)
        def _():
            state[...] = jnp.zeros_like(state)

        ii, jj = _iota(L, L, 0), _iota(L, L, 1)
        for hq in range(G):
            s0_ref[hq, 0] = state[hq * W:(hq + 1) * W, :]
        qks = [_gdn_qk(q_ref[:, hq * Hd:(hq + 1) * Hd].astype(F32), k_ref[:, hq * Hd:(hq + 1) * Hd].astype(F32))
               for hq in range(G)]

        def head(hq, j):
            hd = hq * J + j
            sl = slice(hd * Hd, (hd + 1) * Hd)
            t = yield from _gdn_gates(qks[hq], b_ref[hq][:, j:j + 1], a_ref[hq][:, j:j + 1], p_ref[hq], j, ii, jj)
            KK = _dot(t["kbeta"], t["k"], _NT)
            QK = qks[hq]["QK"]
            yield
            T = yield from _unit_lower_inverse(jnp.where(ii > jj, KK * t["Dm"], 0.0), ii, jj)
            t_hi, t_lo = _split(T)
            eye_b = jnp.where(ii == jj, 1.0, 0.0).astype(BF16)
            t_t = (lax.dot_general(t_hi, eye_b, _TN, preferred_element_type=F32)
                   + lax.dot_general(t_lo, eye_b, _TN, preferred_element_type=F32))
            t_ref[hd, 0] = jnp.concatenate([T, t_t], axis=1)
            S0 = state[sl, :]
            U = _dot2(T, v_ref[:, sl].astype(F32) * t["beta"])
            Wm = _dot2(T, t["kbeta"] * t["eg"])
            o_inter = _dot(t["q"] * t["eg"], S0)
            yield
            Vn = U - _dot(Wm, S0)
            yield
            o = o_inter + _dot(QK * t["Dm"], Vn)
            s_new = _dot(t["kdec"], Vn, _TN)
            yield
            state[sl, :] = S0 * jnp.exp(t["g_last"]) + s_new
            o_ref[:, sl] = o
            r = lax.rsqrt(jnp.mean(o * o, axis=-1, keepdims=True) + EPS)
            on_ref[:, sl] = ((o * r * nw_ref[...]) * _silu(z_ref[:, sl].astype(F32))).astype(BF16)

        _lockstep(head(hq, j) for hq in range(G) for j in range(J))

    q_spec, k_spec, v_spec, z_spec, ba_spec, p_spec, nw_spec, s_spec, t_spec = _gdn_specs(L, lambda c: c)
    o_spec = pl.BlockSpec((L, G * W), lambda h, c: (c, h))
    return pl.pallas_call(
        body, name=name, grid=(GDN_QK_HEADS // G, nc),
        in_specs=[q_spec, k_spec, v_spec, z_spec, ba_spec, ba_spec, p_spec, nw_spec],
        out_specs=(o_spec, o_spec, s_spec, t_spec),
        out_shape=(jax.ShapeDtypeStruct((S, GDN_QK_HEADS * W), F32), jax.ShapeDtypeStruct((S, GDN_QK_HEADS * W), BF16),
                   jax.ShapeDtypeStruct((GDN_QK_HEADS, nc, W, Hd), F32),
                   jax.ShapeDtypeStruct((GDN_QK_HEADS * J, nc, L, 2 * L), F32)),
        scratch_shapes=[pltpu.VMEM((G * W, Hd), F32)],
        compiler_params=_params("parallel", "arbitrary"))(conv, conv, conv, proj, b_raw, a_raw, pvec, nw)


def _gdn_bwd(don, conv, proj, b_raw, a_raw, pvec, nw, o_pre, states, t_inv, d_proj, d_conv, name):
    S = conv.shape[0]
    L, Hd, J, G = GDN_CHUNK, GDN_HEAD, GDN_V_PER_QK, GDN_QK_PER_STEP
    W = J * Hd
    nc = S // L
    assert G == GDN_QK_HEADS
    CK0, CV0 = GDN_QK_HEADS * Hd, 2 * GDN_QK_HEADS * Hd

    def body(don_ref, q_ref, k_ref, v_ref, z_ref, b_ref, a_ref, p_ref, nw_ref, o_ref, s0_ref, t_ref, _, _2,
             dact_ref, dz_ref, db_ref, da_ref, dp_ref, dnw_ref, dstate):
        c = pl.program_id(1)

        @pl.when(c == 0)
        def _():
            dstate[...] = jnp.zeros_like(dstate)
            dp_ref[...] = jnp.zeros_like(dp_ref)
            dnw_ref[...] = jnp.zeros_like(dnw_ref)

        ii, jj = _iota(L, L, 0), _iota(L, L, 1)
        last = (_iota(L, 1, 0) == L - 1)
        res = {}
        qks = [_gdn_qk(q_ref[:, hq * Hd:(hq + 1) * Hd].astype(F32), k_ref[:, hq * Hd:(hq + 1) * Hd].astype(F32))
               for hq in range(G)]

        def head(hq, j):
            hd = hq * J + j
            sl = slice(hd * Hd, (hd + 1) * Hd)
            qa = qks[hq]["qa"]
            t = yield from _gdn_gates(qks[hq], b_ref[hq][:, j:j + 1], a_ref[hq][:, j:j + 1], p_ref[hq], j, ii, jj)
            q, k, beta, eg, Dm, kbeta, kdec = (t[nm] for nm in ("q", "k", "beta", "eg", "Dm", "kbeta", "kdec"))
            T, t_t = t_ref[hd, 0, :, 0:L], t_ref[hd, 0, :, L:2 * L]
            v, z, o = v_ref[:, sl].astype(F32), z_ref[:, sl].astype(F32), o_ref[:, sl]
            S0, dS1 = s0_ref[hq, 0, j * Hd:(j + 1) * Hd, :], dstate[sl, :]
            sz = _silu(z)
            r = lax.rsqrt(jnp.mean(o * o, axis=-1, keepdims=True) + EPS)
            oh = o * r
            d_on = don_ref[:, sl]
            dz_ref[:, sl] = (d_on * (oh * nw_ref[...]) * _dsilu(z)).astype(BF16)
            dn = d_on * sz
            dnw_part = _colsum(dn * oh)
            doh = dn * nw_ref[...]
            dO = r * (doh - oh * jnp.mean(doh * oh, axis=-1, keepdims=True))
            Rw = kbeta * eg
            qe = q * eg
            U = _dot2(T, v * beta)
            Wm = _dot2(T, Rw)
            KK = _dot(kbeta, k, _NT)
            QK = qks[hq]["QK"]
            o_inter = _dot(qe, S0)
            dq_s = _dot(dO, S0, _NT)
            dS_q = _dot(qe, dO, _TN)
            yield
            Am = jnp.where(ii > jj, KK * Dm, 0.0)
            Pm = QK * Dm
            Vn = U - _dot(Wm, S0)
            dVn_s = _dot(kdec, dS1)
            yield
            dVn = _dot(Pm, dO, _TN) + dVn_s
            dP = _dot(dO, Vn, _NT)
            dKd = _dot(Vn, dS1, _NT)
            yield
            dQK = dP * Dm
            dq = _dot(dQK, k) + dq_s * eg
            dk = _dot(dQK, q, _TN) + dKd * t["wdec"]
            dstate[sl, :] = jnp.exp(t["g_last"]) * dS1 + dS_q - _dot(Wm, dVn, _TN)
            dW = -_dot(dVn, S0, _NT)
            dRu = _dot2(t_t, dVn)
            yield
            dRw = _dot2(t_t, dW)
            dA_u = _dot(dRu, U, _NT)
            yield
            dA = jnp.where(ii > jj, -(dA_u + _dot(dRw, Wm, _NT)), 0.0)
            yield
            dKK = dA * Dm
            dkbeta = _dot(dKK, k) + dRw * eg
            dk = dk + _dot(dKK, kbeta, _TN)
            yield
            dk = dk + dkbeta * beta
            dbeta = _rowsum(dkbeta * k + dRu * v)
            dact_ref[:, CV0 + hd * Hd:CV0 + (hd + 1) * Hd] = (dRu * beta).astype(BF16)
            Q = dA * Am + dP * Pm
            kd = dKd * kdec
            d_glast = _total(kd) + jnp.exp(t["g_last"]) * _total(dS1 * S0)
            q_sums = _row_col_sums(Q)
            rest = _rowsum(dRw * Rw + dO * o_inter - kd) + jnp.where(last, d_glast, 0.0)
            yield
            dg = _rev_cumsum_col(q_sums + rest, ii, jj)
            yield
            da_raw = dg * t["neg_ea"] * _sigmoid(t["sp_in"])
            res[hq, j] = dict(dq=dq, dk=dk, db=dbeta * beta * (1.0 - beta), da=da_raw, d_bias=_colsum(da_raw),
                              d_alog=_colsum(dg * t["g"]), dnw=dnw_part, rq=t["rq"], rk=t["rk"], k=k, qh=qa * t["rq"])

        _lockstep(head(hq, j) for hq in range(G) for j in range(J))
        for hq in range(G):
            parts = [res[hq, j] for j in range(J)]
            hsl = slice(hq * Hd, (hq + 1) * Hd)
            p0 = parts[0]
            dqh = sum(pt["dq"] for pt in parts) * (GDN_HEAD ** -0.5)
            dkn = sum(pt["dk"] for pt in parts)
            dact_ref[:, hsl] = (p0["rq"] * (dqh - p0["qh"] * _rowsum(dqh * p0["qh"]))).astype(BF16)
            dact_ref[:, CK0 + hq * Hd:CK0 + (hq + 1) * Hd] = (
                p0["rk"] * (dkn - p0["k"] * _rowsum(dkn * p0["k"]))).astype(BF16)
            db_ref[hq] = jnp.concatenate([pt["db"] for pt in parts], axis=1)
            da_ref[hq] = jnp.concatenate([pt["da"] for pt in parts], axis=1)
            dp_ref[hq] += jnp.concatenate([jnp.concatenate([pt["d_bias"] for pt in parts], axis=1),
                                           jnp.concatenate([pt["d_alog"] for pt in parts], axis=1)], axis=0)
            dnw_ref[hq] += sum(pt["dnw"] for pt in parts)

    rev = lambda c: nc - 1 - c
    q_spec, k_spec, v_spec, z_spec, ba_spec, p_spec, nw_spec, s_spec, t_spec = _gdn_specs(L, rev)
    o_spec = pl.BlockSpec((L, G * W), lambda h, c: (rev(c), h))
    dnw_spec = pl.BlockSpec((G, 1, Hd), lambda h, c: (h, 0, 0))
    half = d_conv.shape[1] // 2
    return pl.pallas_call(
        body, name=name, grid=(GDN_QK_HEADS // G, nc),
        in_specs=[o_spec, q_spec, k_spec, v_spec, z_spec, ba_spec, ba_spec, p_spec, nw_spec, o_spec, s_spec, t_spec,
                  D_PROJ_ANY, D_PROJ_ANY],
        out_specs=(pl.BlockSpec((L, half), lambda h, c: (rev(c), 1)), z_spec, ba_spec, ba_spec, p_spec, dnw_spec),
        out_shape=(jax.ShapeDtypeStruct(d_conv.shape, BF16), jax.ShapeDtypeStruct(d_proj.shape, BF16),
                   jax.ShapeDtypeStruct((GDN_QK_HEADS, S, J), F32), jax.ShapeDtypeStruct((GDN_QK_HEADS, S, J), F32),
                   jax.ShapeDtypeStruct((GDN_QK_HEADS, 2, J), F32), jax.ShapeDtypeStruct((GDN_QK_HEADS, 1, Hd), F32)),
        scratch_shapes=[pltpu.VMEM((G * W, Hd), F32)], input_output_aliases={12: 1, 13: 0},
        compiler_params=_params("parallel", "arbitrary"))(don, conv, conv, conv, proj, b_raw, a_raw, pvec, nw, o_pre,
                                                          states, t_inv, d_proj, d_conv)


def _ada_fwd(c_all, w_loc, b_loc, name):
    n = w_loc.shape[1]

    def body(c_ref, w_ref, b_ref, o_ref):
        o_ref[...] = _dot3(_silu(c_ref[...]), w_ref[...]) + b_ref[...]

    return pl.pallas_call(body, name=name, out_shape=jax.ShapeDtypeStruct((N_DEV, n), F32),
                          compiler_params=pltpu.CompilerParams(vmem_limit_bytes=VMEM_LIMIT))(c_all, w_loc, b_loc)


def _ada_bwd(c_all_t, dmod_cols, name):
    Dm, n = c_all_t.shape[0], dmod_cols.shape[1]

    def body(c_ref, d_ref, o_ref):
        ca = _silu(c_ref[...])
        acc = ca[:, 0:1] * d_ref[0:1, :]
        for i in range(1, N_DEV):
            acc = acc + ca[:, i:i + 1] * d_ref[i:i + 1, :]
        o_ref[...] = acc

    return pl.pallas_call(body, name=name, out_shape=jax.ShapeDtypeStruct((Dm, n), F32),
                          compiler_params=pltpu.CompilerParams(vmem_limit_bytes=VMEM_LIMIT))(c_all_t, dmod_cols)


ADAM_BLOCK_BYTES = 12 * 1024 * 1024


def _adam(contrib, w, m, v, name):
    n, R, C = contrib.shape
    tr = R
    while tr % 16 == 0 and (n + 7) * tr * C * 4 > ADAM_BLOCK_BYTES:
        tr //= 2

    def body(c_ref, w_ref, m_ref, v_ref, g_ref, d_ref, nm_ref, nv_ref):
        g = c_ref[0].astype(F32)
        for i in range(1, n):
            g = g + c_ref[i].astype(F32)
        nm = ADAM_B1 * m_ref[...] + (1.0 - ADAM_B1) * g
        nv = ADAM_B2 * v_ref[...] + (1.0 - ADAM_B2) * (g * g)
        m_hat = nm / (1.0 - ADAM_B1 ** ADAM_STEP)
        v_hat = nv / (1.0 - ADAM_B2 ** ADAM_STEP)
        g_ref[...] = g
        d_ref[...] = -ADAM_LR * (m_hat / (jnp.sqrt(v_hat) + ADAM_EPS) + ADAM_WD * w_ref[...])
        nm_ref[...] = nm
        nv_ref[...] = nv

    spec = pl.BlockSpec((tr, C), lambda i: (i, 0))
    shp = jax.ShapeDtypeStruct((R, C), F32)
    return pl.pallas_call(
        body, name=name, grid=(R // tr,), in_specs=[pl.BlockSpec((n, tr, C), lambda i: (0, i, 0)), spec, spec, spec],
        out_specs=(spec,) * 4, out_shape=(shp,) * 4, compiler_params=_params("parallel"))(contrib, w, m, v)


def _exchange(arrays, modes, name, chips=False):
    n = len(arrays)
    out_shape = tuple(jax.ShapeDtypeStruct((N_DEV,) + a.shape if md == "gather" else a.shape, a.dtype)
                      for a, md in zip(arrays, modes))

    def body(*refs):
        ins, outs = refs[:n], refs[n:2 * n]
        send_sems, recv_sems, loc_sems = refs[2 * n:]
        me, peers = _peer_table(chips)

        def src(k, slot):
            return ins[k] if modes[k] == "gather" else ins[k].at[slot]

        def remote(k, m, to_slot, land_slot):
            return pltpu.make_async_remote_copy(
                src_ref=src(k, to_slot), dst_ref=outs[k].at[land_slot], send_sem=send_sems.at[k, m],
                recv_sem=recv_sems.at[k, m], device_id=peers[m][0], device_id_type=pl.DeviceIdType.MESH)

        local = [pltpu.make_async_copy(src(k, me), outs[k].at[me], loc_sems.at[k]) for k in range(n)]
        for cp in local:
            cp.start()
        sends = [remote(k, m, peers[m][1], me) for m in range(len(peers)) for k in range(n)]
        for cp in sends:
            cp.start()
        for m in range(len(peers)):
            for k in range(n):
                remote(k, m, peers[m][1], peers[m][1]).wait_recv()
        for cp in sends:
            cp.wait_send()
        for cp in local:
            cp.wait()

    any_spec = pl.BlockSpec(memory_space=pl.ANY)
    return pl.pallas_call(
        body, name=name, in_specs=[any_spec] * n, out_specs=(any_spec,) * n, out_shape=out_shape,
        scratch_shapes=[pltpu.SemaphoreType.DMA((n, N_DEV - 1)), pltpu.SemaphoreType.DMA((n, N_DEV - 1)),
                        pltpu.SemaphoreType.DMA((n,))])(*arrays)


def _gather_two_level(arrays, name):
    n = len(arrays)
    out_shape = tuple(jax.ShapeDtypeStruct((N_DEV,) + a.shape, a.dtype) for a in arrays)

    def body(*refs):
        ins, outs = refs[:n], refs[n:2 * n]
        send_sems, recv_sems, loc_sems = refs[2 * n:]
        ix, iy, ic = lax.axis_index("x"), lax.axis_index("y"), lax.axis_index("c")
        lin = lambda px, py, pc: 4 * px + 2 * py + pc
        me, sib = lin(ix, iy, ic), (ix, iy, 1 - ic)
        chips = [(1 - ix, iy), (ix, 1 - iy), (1 - ix, 1 - iy)]

        def copy(k, s, block, to, src=None):
            return pltpu.make_async_remote_copy(
                src_ref=outs[k].at[block] if src is None else src, dst_ref=outs[k].at[block],
                send_sem=send_sems.at[k, s], recv_sem=recv_sems.at[k, s], device_id=to,
                device_id_type=pl.DeviceIdType.MESH)

        local = [pltpu.make_async_copy(ins[k], outs[k].at[me], loc_sems.at[k]) for k in range(n)]
        for cp in local:
            cp.start()
        first = [copy(k, 1 + j, me, (cx, cy, ic), src=ins[k]) for j, (cx, cy) in enumerate(chips) for k in range(n)]
        first += [copy(k, 0, me, sib, src=ins[k]) for k in range(n)]
        for cp in first:
            cp.start()
        passed = []
        for j, (cx, cy) in enumerate(chips):
            for k in range(n):
                copy(k, 1 + j, lin(cx, cy, ic), sib).wait_recv()
                passed.append(copy(k, 4 + j, lin(cx, cy, ic), sib))
                passed[-1].start()
        for k in range(n):
            copy(k, 0, lin(*sib), sib).wait_recv()
            for j, (cx, cy) in enumerate(chips):
                copy(k, 4 + j, lin(cx, cy, 1 - ic), sib).wait_recv()
        for cp in first + passed:
            cp.wait_send()
        for cp in local:
            cp.wait()

    any_spec = pl.BlockSpec(memory_space=pl.ANY)
    return pl.pallas_call(
        body, name=name, in_specs=[any_spec] * n, out_specs=(any_spec,) * n, out_shape=out_shape,
        scratch_shapes=[pltpu.SemaphoreType.DMA((n, N_DEV - 1)), pltpu.SemaphoreType.DMA((n, N_DEV - 1)),
                        pltpu.SemaphoreType.DMA((n,))])(*arrays)


def _peer_table(chips=False):
    ix, iy, ic = lax.axis_index("x"), lax.axis_index("y"), lax.axis_index("c")
    peers = []
    for m in ((2, 4, 6) if chips else range(1, N_DEV)):
        px = 1 - ix if m & 4 else ix
        py = 1 - iy if m & 2 else iy
        pc = 1 - ic if m & 1 else ic
        peers.append(((px, py, pc), 2 * px + py if chips else 4 * px + 2 * py + pc))
    return (2 * ix + iy if chips else 4 * ix + 2 * iy + ic), peers


def _exchange_start(arrays, modes, after, name, chips=False):
    n = len(arrays)
    land_shapes = [(N_DEV,) + a.shape if md == "gather" else a.shape for a, md in zip(arrays, modes)]

    def body(*refs):
        ins, lands = refs[:n], refs[n:2 * n]
        send_sems, recv_sems = refs[2 * n + 1], refs[2 * n + 2]
        token = refs[-1]
        me, peers = _peer_table(chips)

        def src(k, slot):
            return ins[k] if modes[k] == "gather" else ins[k].at[slot]

        for peer, slot in peers:
            for k in range(n):
                pltpu.make_async_remote_copy(
                    src_ref=src(k, slot), dst_ref=lands[k].at[me], send_sem=send_sems, recv_sem=recv_sems,
                    device_id=peer, device_id_type=pl.DeviceIdType.MESH).start()
        token[...] = jnp.zeros_like(token)

    hbm = pl.BlockSpec(memory_space=pltpu.HBM)
    sem = pl.BlockSpec(memory_space=pltpu.SEMAPHORE)
    sem_shape = pltpu.SemaphoreType.DMA(())
    operands = [pltpu.with_memory_space_constraint(a, pltpu.HBM) for a in arrays]
    operands += [pltpu.with_memory_space_constraint(lax.empty(s, a.dtype), pltpu.HBM)
                 for s, a in zip(land_shapes, arrays)]
    out = pl.pallas_call(
        body, name=name,
        out_shape=(sem_shape, sem_shape) + tuple(pltpu.HBM(a.shape, a.dtype) for a in arrays)
        + tuple(pltpu.HBM(s, a.dtype) for s, a in zip(land_shapes, arrays)) + (jax.ShapeDtypeStruct((8, 128), F32),),
        in_specs=[hbm] * (2 * n) + [pl.BlockSpec(memory_space=pl.ANY)],
        out_specs=(sem, sem) + (hbm,) * (2 * n) + (pl.BlockSpec(memory_space=pltpu.VMEM),),
        input_output_aliases={i: 2 + i for i in range(2 * n)},
        compiler_params=pltpu.CompilerParams(has_side_effects=pltpu.SideEffectType.DATAFLOW_SIDE_EFFECTING))(
            *operands, after)
    return out[0], out[1], out[2:2 + n], out[2 + n:2 + 2 * n], out[-1]


def _exchange_wait(started, modes, after, name, chips=False):
    send_sems, recv_sems, sent, lands, _ = started
    n = len(sent)

    def body(*refs):
        ins, zones = refs[:n], refs[n:2 * n]
        send_ref, recv_ref = refs[2 * n], refs[2 * n + 1]
        _, peers = _peer_table(chips)

        def src(k, slot):
            return ins[k] if modes[k] == "gather" else ins[k].at[slot]

        for peer, slot in peers:
            for k in range(n):
                cp = pltpu.make_async_remote_copy(
                    src_ref=src(k, slot), dst_ref=zones[k].at[slot], send_sem=send_ref, recv_sem=recv_ref,
                    device_id=peer, device_id_type=pl.DeviceIdType.MESH)
                cp.wait_send()
                cp.wait_recv()

    hbm = pl.BlockSpec(memory_space=pltpu.HBM)
    sem = pl.BlockSpec(memory_space=pltpu.SEMAPHORE)
    out = pl.pallas_call(
        body, name=name,
        out_shape=tuple(pltpu.HBM(a.shape, a.dtype) for a in sent) + tuple(pltpu.HBM(a.shape, a.dtype) for a in lands),
        in_specs=[hbm] * (2 * n) + [sem, sem, pl.BlockSpec(memory_space=pl.ANY)], out_specs=(hbm,) * (2 * n),
        input_output_aliases={i: i for i in range(2 * n)},
        compiler_params=pltpu.CompilerParams(has_side_effects=pltpu.SideEffectType.DATAFLOW_SIDE_EFFECTING))(
            *sent, *lands, send_sems, recv_sems, after)
    ix, iy, ic = lax.axis_index("x"), lax.axis_index("y"), lax.axis_index("c")
    me = 2 * ix + iy if chips else 4 * ix + 2 * iy + ic
    filled = []
    for k in range(n):
        own = sent[k] if modes[k] == "gather" else lax.dynamic_index_in_dim(sent[k], me, axis=0, keepdims=False)
        filled.append(lax.dynamic_update_index_in_dim(out[n + k], own, me, axis=0))
    return filled


def _swap_sibling(to_c0, to_c1, name):
    def body(c0_ref, c1_ref, out_ref, send_sem, recv_sem):
        ix, iy, ic = lax.axis_index("x"), lax.axis_index("y"), lax.axis_index("c")

        def copy(src):
            return pltpu.make_async_remote_copy(src_ref=src, dst_ref=out_ref, send_sem=send_sem, recv_sem=recv_sem,
                                                device_id=(ix, iy, 1 - ic), device_id_type=pl.DeviceIdType.MESH)

        @pl.when(ic == 0)
        def _():
            copy(c1_ref).start()

        @pl.when(ic == 1)
        def _():
            copy(c0_ref).start()

        copy(c0_ref).wait()

    any_spec = pl.BlockSpec(memory_space=pl.ANY)
    return pl.pallas_call(body, name=name, in_specs=[any_spec, any_spec], out_specs=any_spec,
                          out_shape=jax.ShapeDtypeStruct(to_c0.shape, to_c0.dtype),
                          scratch_shapes=[pltpu.SemaphoreType.DMA, pltpu.SemaphoreType.DMA])(to_c0, to_c1)


def _add_pair(to_c0, to_c1, got, name):
    n, R, C = got.shape
    tr = _blk(R, 256)

    def body(c0_ref, c1_ref, got_ref, o_ref):
        ic = lax.axis_index("c")

        @pl.when(ic == 0)
        def _():
            o_ref[...] = (c0_ref[...].astype(F32) + got_ref[...].astype(F32)).astype(o_ref.dtype)

        @pl.when(ic == 1)
        def _():
            o_ref[...] = (c1_ref[...].astype(F32) + got_ref[...].astype(F32)).astype(o_ref.dtype)

    spec = pl.BlockSpec((1, tr, C), lambda i, j: (i, j, 0))
    return pl.pallas_call(body, name=name, grid=(n, R // tr), in_specs=[spec, spec, spec], out_specs=spec,
                          out_shape=jax.ShapeDtypeStruct(got.shape, got.dtype),
                          compiler_params=_params("parallel", "parallel"))(to_c0, to_c1, got)


W_IN_SPLITS = (0, 2048, 6144, 6176, 10272, 12320, 12336, 12352, 13376, 14400)
N_REPLICATED = 16640
REPLICATED = ("b_ada", "norm_mix_pre", "norm_mix_post", "ssm_conv_b", "ssm_dt_bias", "ssm_A_log", "ssm_D",
              "ssm_norm_w", "gdn_dt_bias", "gdn_A_log", "gdn_norm_w", "norm_mlp_pre", "norm_mlp_post")
WEIGHTS = ("w_ada", "b_ada", "norm_mix_pre", "norm_mix_post", "w_in", "ssm_conv_w", "ssm_conv_b", "ssm_dt_bias",
           "ssm_A_log", "ssm_D", "ssm_norm_w", "gdn_conv_w", "gdn_dt_bias", "gdn_A_log", "gdn_norm_w", "w_ssm_up",
           "w_gdn_up", "w_out", "norm_mlp_pre", "norm_mlp_post", "w_mlp_up", "w_mlp_down")


def _cols_of_shards(g, a, b):
    width, pieces = g.shape[2], []
    while a < b:
        i = a // width
        hi = min(b, (i + 1) * width)
        pieces.append(g[i][:, a - i * width:hi - i * width])
        a = hi
    return pieces


ORIG_SEGMENTS = ((0, 6144, "main", 0), (6144, 6176, "small", 0), (6176, 12320, "main", 6144),
                 (12320, 12352, "small", 32), (12352, 14400, "main", 12288))


def _orig_cols(main_cols, small_cols, a, b):
    pieces = []
    for s0, s1, which, off in ORIG_SEGMENTS:
        lo, hi = max(a, s0), min(b, s1)
        if lo < hi:
            pieces.append((main_cols if which == "main" else small_cols)[:, off + lo - s0:off + hi - s0])
    return jnp.concatenate(pieces, axis=1)


def _by_cols(t):
    return t.transpose(1, 0, 2).reshape(t.shape[1], N_DEV * t.shape[2])


def _to_col_shards(t):
    R, C8 = t.shape
    return t.reshape(R, N_DEV, C8 // N_DEV).transpose(1, 0, 2)


def _heads_first(t, groups):
    S = t.shape[0]
    return t.reshape(S, groups, t.shape[1] // groups).transpose(1, 0, 2)


def _heads_last(t):
    return t.transpose(1, 0, 2).reshape(t.shape[1], t.shape[0] * t.shape[2])


def kernel(x, c, w_ada, b_ada, norm_mix_pre, norm_mix_post, w_in, ssm_conv_w, ssm_conv_b, ssm_dt_bias, ssm_A_log, ssm_D, ssm_norm_w, gdn_conv_w, gdn_dt_bias, gdn_A_log, gdn_norm_w, w_ssm_up, w_gdn_up, w_out, norm_mlp_pre, norm_mlp_post, w_mlp_up, w_mlp_down, loss_target, m_w_ada, m_b_ada, m_norm_mix_pre, m_norm_mix_post, m_w_in, m_ssm_conv_w, m_ssm_conv_b, m_ssm_dt_bias, m_ssm_A_log, m_ssm_D, m_ssm_norm_w, m_gdn_conv_w, m_gdn_dt_bias, m_gdn_A_log, m_gdn_norm_w, m_w_ssm_up, m_w_gdn_up, m_w_out, m_norm_mlp_pre, m_norm_mlp_post, m_w_mlp_up, m_w_mlp_down, v_w_ada, v_b_ada, v_norm_mix_pre, v_norm_mix_post, v_w_in, v_ssm_conv_w, v_ssm_conv_b, v_ssm_dt_bias, v_ssm_A_log, v_ssm_D, v_ssm_norm_w, v_gdn_conv_w, v_gdn_dt_bias, v_gdn_A_log, v_gdn_norm_w, v_w_ssm_up, v_w_gdn_up, v_w_out, v_norm_mlp_pre, v_norm_mlp_post, v_w_mlp_up, v_w_mlp_down):
    S, Dm = x.shape[1], D_MODEL
    me = 4 * lax.axis_index("x") + 2 * lax.axis_index("y") + lax.axis_index("c")
    x2, tgt = x[0], loss_target[0]
    n_ada = w_ada.shape[2]
    given = dict(
        w_ada=(w_ada, m_w_ada, v_w_ada), b_ada=(b_ada, m_b_ada, v_b_ada),
        norm_mix_pre=(norm_mix_pre, m_norm_mix_pre, v_norm_mix_pre),
        norm_mix_post=(norm_mix_post, m_norm_mix_post, v_norm_mix_post), w_in=(w_in, m_w_in, v_w_in),
        ssm_conv_w=(ssm_conv_w, m_ssm_conv_w, v_ssm_conv_w), ssm_conv_b=(ssm_conv_b, m_ssm_conv_b, v_ssm_conv_b),
        ssm_dt_bias=(ssm_dt_bias, m_ssm_dt_bias, v_ssm_dt_bias), ssm_A_log=(ssm_A_log, m_ssm_A_log, v_ssm_A_log),
        ssm_D=(ssm_D, m_ssm_D, v_ssm_D), ssm_norm_w=(ssm_norm_w, m_ssm_norm_w, v_ssm_norm_w),
        gdn_conv_w=(gdn_conv_w, m_gdn_conv_w, v_gdn_conv_w), gdn_dt_bias=(gdn_dt_bias, m_gdn_dt_bias, v_gdn_dt_bias),
        gdn_A_log=(gdn_A_log, m_gdn_A_log, v_gdn_A_log), gdn_norm_w=(gdn_norm_w, m_gdn_norm_w, v_gdn_norm_w),
        w_ssm_up=(w_ssm_up, m_w_ssm_up, v_w_ssm_up), w_gdn_up=(w_gdn_up, m_w_gdn_up, v_w_gdn_up),
        w_out=(w_out, m_w_out, v_w_out), norm_mlp_pre=(norm_mlp_pre, m_norm_mlp_pre, v_norm_mlp_pre),
        norm_mlp_post=(norm_mlp_post, m_norm_mlp_post, v_norm_mlp_post), w_mlp_up=(w_mlp_up, m_w_mlp_up, v_w_mlp_up),
        w_mlp_down=(w_mlp_down, m_w_mlp_down, v_w_mlp_down))

    (c_all, scw, gcw, g_in) = _gather_two_level([c, ssm_conv_w[0], gdn_conv_w[0], w_in[0].astype(BF16)], "gather_w_in")
    c_all = c_all.reshape(N_DEV, Dm)
    sp = W_IN_SPLITS
    w_main = jnp.concatenate(_cols_of_shards(g_in, sp[0], sp[2]) + _cols_of_shards(g_in, sp[3], sp[5])
                             + _cols_of_shards(g_in, sp[7], sp[9]), axis=1)
    w_small = jnp.concatenate(_cols_of_shards(g_in, sp[2], sp[3]) + _cols_of_shards(g_in, sp[5], sp[7])
                              + [jnp.zeros((Dm, N_SMALL - 64), BF16)], axis=1)
    conv_w = jnp.concatenate([_by_cols(scw), _by_cols(gcw)], axis=1)
    conv_b = jnp.concatenate([ssm_conv_b, jnp.zeros_like(ssm_conv_b)], axis=1)

    b_loc = lax.dynamic_slice(b_ada, (0, me * n_ada), (1, n_ada))
    mod_part = _ada_fwd(c_all, w_ada[0], b_loc, "ada_fwd")
    (mod_rows,) = _exchange([mod_part.reshape(N_DEV, 1, n_ada)], ["a2a"], "exchange_mod")
    rest = _exchange_start([w_ssm_up[0].astype(BF16), w_gdn_up[0].astype(BF16), w_out[0].astype(BF16),
                            w_mlp_up[0].astype(BF16), w_mlp_down[0].astype(BF16)], ["gather"] * 5, mod_rows,
                           "gather_rest_start")
    mod = mod_rows.reshape(1, 6 * Dm) + rest[4][0:1, 0:1]
    sh1, sc1, g1, sh2, sc2, g2 = [mod[:, i * Dm:(i + 1) * Dm] for i in range(6)]

    h = _pre_fwd(x2, norm_mix_pre, sc1, sh1, "pre_mix")
    proj = _mm(h, w_main, S, N_MAIN, Dm, mode="nn", out_dtype=BF16, tn=MM_WIDE_N, name="proj_main")
    small = _mm(h, w_small, S, N_SMALL, Dm, mode="nn", out_dtype=F32, name="proj_small")
    conv = _conv_fwd(proj, conv_w, conv_b, "conv_fwd")
    dt_g, b_g, a_g = _heads_first(small[:, 0:32], 8), _heads_first(small[:, 32:48], 8), _heads_first(small[:, 48:64], 8)
    pv_ssm = jnp.stack([ssm_dt_bias.reshape(8, 4), ssm_A_log.reshape(8, 4), ssm_D.reshape(8, 4)], axis=1)
    nw_ssm = ssm_norm_w.reshape(8, 1, SSM_GROUP_WIDTH)
    pv_gdn = jnp.stack([gdn_dt_bias.reshape(8, 2), gdn_A_log.reshape(8, 2)], axis=1)
    y_ssd, ysn, st_ssm = _ssd_fwd(conv, proj, dt_g, pv_ssm, nw_ssm, "ssd_fwd")
    o_pre, ogn, st_gdn, t_inv = _gdn_fwd(conv, proj, b_g, a_g, pv_gdn, gdn_norm_w, "gdn_fwd")
    g_su, g_gu, g_out, g_mu, g_md = _exchange_wait(rest, ["gather"] * 5, ogn, "gather_rest_wait")
    w_su, w_gu = g_su.reshape(2 * Dm, Dm), g_gu.reshape(2 * Dm, Dm)
    w_o, w_mu, w_md = g_out.reshape(Dm, Dm), _by_cols(g_mu), g_md.reshape(4 * Dm, Dm)
    ys = _mm(ysn, w_su, S, Dm, 2 * Dm, mode="nn", out_dtype=F32, name="ssm_up")
    yg, merged = _gdn_up_merge(ogn, w_gu, ys, proj, "gdn_up_merge")
    mo, x1, h2 = _mix_out_post_pre(merged, w_o, x2, norm_mix_post, g1, norm_mlp_pre, sc2, sh2, "mix_out_post_pre")
    u, act = _mm(h2, w_mu, S, 4 * Dm, Dm, mode="nn", out_dtype=F32, epi="relu2", tn=MM_WIDE_N, name="mlp_up")
    y_mlp = _mm(act, w_md, S, Dm, 4 * Dm, mode="nn", out_dtype=F32, name="mlp_down")
    dx2, loss_loc, dy, dg2, dw_post2 = _final_fwd_bwd(x1, y_mlp, norm_mlp_post, g2, tgt, "post_mlp_loss_bwd")

    du = _mm(dy, w_md, S, 4 * Dm, Dm, mode="nt", out_dtype=BF16, epi="drelu2", extra=u, tn=MM_WIDE_N, name="mlp_down_dx")
    gw_md = _mm(act, dy, 4 * Dm, Dm, S, mode="tn", out_dtype=BF16, name="mlp_down_dw")
    dh2 = _mm(du, w_mu, S, Dm, 4 * Dm, mode="nt", out_dtype=F32, name="mlp_up_dx")
    gw_mu = _mm(h2, du, Dm, 4 * Dm, S, mode="tn", out_dtype=BF16, name="mlp_up_dw")
    mlp_x = _exchange_start([_to_col_shards(gw_mu), gw_md.reshape(N_DEV, -1, Dm)], ["a2a"] * 2, gw_md,
                            "grads_mlp_start")
    dx1, dsh2, dsc2, dw_pre2 = _pre_bwd(dh2, x1, norm_mlp_pre, sc2 + mlp_x[4][0:1, 0:1], dx2, "pre_mlp_bwd")
    dmo, dg1, dw_post1 = _post_bwd(dx1, mo, norm_mix_post, g1, "post_mix_bwd")
    gw_o = _mm(merged, dmo, Dm, Dm, S, mode="tn", out_dtype=BF16, name="mix_out_dw")
    dys, dyg, d_proj = _mix_out_dx_merge_bwd(dmo, w_o, ys, yg, proj, lax.empty((S, N_MAIN), BF16), "mix_out_dx_merge")
    dysn = _mm(dys, w_su, S, 2 * Dm, Dm, mode="nt", out_dtype=F32, tn=MM_WIDE_N, name="ssm_up_dx")
    gw_su = _mm(ysn, dys, 2 * Dm, Dm, S, mode="tn", out_dtype=BF16, name="ssm_up_dw")
    dogn = _mm(dyg, w_gu, S, 2 * Dm, Dm, mode="nt", out_dtype=F32, tn=MM_WIDE_N, name="gdn_up_dx")
    gw_gu = _mm(ogn, dyg, 2 * Dm, Dm, S, mode="tn", out_dtype=BF16, name="gdn_up_dw")
    mix_x = _exchange_start([gw_su.reshape(N_DEV, -1, Dm), gw_gu.reshape(N_DEV, -1, Dm), gw_o.reshape(N_DEV, -1, Dm)],
                            ["a2a"] * 3, gw_gu, "grads_mix_start")
    d_conv, d_proj, ddt_g, dpv_ssm, dnw_ssm = _ssd_bwd(dysn, conv, proj, dt_g, pv_ssm + mix_x[4][0, 0], nw_ssm, y_ssd,
                                                       st_ssm, d_proj, lax.empty(conv.shape, BF16), "ssd_bwd")
    d_conv, d_proj, db_g, da_g, dpv_gdn, dnw_gdn = _gdn_bwd(dogn, conv, proj, b_g, a_g, pv_gdn, gdn_norm_w, o_pre,
                                                            st_gdn, t_inv, d_proj, d_conv, "gdn_bwd")
    d_proj, dconv_w, dconv_b = _conv_bwd(d_conv, proj, conv_w, conv_b, 0, d_proj, "conv_bwd")
    d_small = jnp.concatenate([_heads_last(ddt_g), _heads_last(db_g), _heads_last(da_g),
                               jnp.zeros((S, N_SMALL - 64), F32)], axis=1).astype(BF16)
    gw_small = _mm(h, d_small, Dm, N_SMALL, S, mode="tn", out_dtype=BF16, name="proj_small_dw")
    main_cols = _mm(h, d_proj, Dm, N_MAIN, S, mode="tn", out_dtype=BF16, name="proj_main_dw")
    n_shard = w_in.shape[2]
    slabs = [_orig_cols(main_cols, gw_small, i * n_shard, (i + 1) * n_shard) for i in range(N_DEV)]
    to_c0, to_c1 = jnp.stack(slabs[0::2]), jnp.stack(slabs[1::2])
    chip_sum = _add_pair(to_c0, to_c1, _swap_sibling(to_c0, to_c1, "grads_w_in_pair"), "grads_w_in_pair_sum")
    in_x = _exchange_start([chip_sum], ["a2a"], gw_small, "grads_w_in_start", chips=True)
    dh = _mm(d_small, w_small + in_x[4][0:1, 0:1].astype(BF16), S, Dm, N_SMALL, mode="nt", out_dtype=F32,
             name="proj_small_dx")
    dh = _mm(d_proj, w_main, S, Dm, N_MAIN, mode="nt", out_dtype=F32, add=dh, name="proj_main_dx")
    dx, dsh1, dsc1, dw_pre1 = _pre_bwd(dh, x2, norm_mix_pre, sc1, dx1, "pre_mix_bwd")

    dconv_b = dconv_b[:, :ssm_conv_b.shape[1]]
    dmod = jnp.concatenate([dsh1, dsc1, dg1, dsh2, dsc2, dg2], axis=1)
    small_vec = jnp.concatenate(
        [dmod, dw_pre1, dw_post1, dconv_b, dpv_ssm[:, 0].reshape(1, 32), dpv_ssm[:, 1].reshape(1, 32),
         dpv_ssm[:, 2].reshape(1, 32), dnw_ssm.reshape(1, 2048), dpv_gdn[:, 0].reshape(1, 16),
         dpv_gdn[:, 1].reshape(1, 16), jnp.sum(dnw_gdn, axis=0), dw_pre2, dw_post2, dconv_w.reshape(1, -1)], axis=1)
    n_vec = small_vec.shape[1]
    small_vec = jnp.pad(small_vec, ((0, 0), (0, (-n_vec) % 1024))).reshape(-1, 1024)
    small_x = _exchange_start([small_vec], ["gather"], dx, "gather_small_start")
    r_mu, r_md = _exchange_wait(mlp_x, ["a2a"] * 2, small_x[4], "grads_mlp_wait")
    r_su, r_gu, r_o = _exchange_wait(mix_x, ["a2a"] * 3, small_x[4], "grads_mix_wait")
    results = {}

    def adam_big(nm, contrib):
        w3 = given[nm]
        res = _adam(contrib, w3[0][0], w3[1][0], w3[2][0], "adam_" + nm)
        results[nm] = tuple(r.reshape(w3[0].shape) for r in res)

    adam_big("w_ssm_up", r_su)
    adam_big("w_gdn_up", r_gu)
    adam_big("w_out", r_o)
    adam_big("w_mlp_up", r_mu)
    adam_big("w_mlp_down", r_md)
    (small_all,) = _exchange_wait(small_x, ["gather"], results["w_mlp_down"][0], "gather_small_wait")
    small_all = small_all.reshape(N_DEV, -1)
    dmod_cols = lax.dynamic_slice(small_all, (0, me * n_ada), (N_DEV, n_ada))
    gw_ada = _ada_bwd(c_all.T, dmod_cols, "ada_bwd")
    conv_all = small_all[:, N_REPLICATED:n_vec].reshape(N_DEV, CONV_K, 2 * N_DEV * 512)
    conv_contrib = jnp.concatenate(
        [lax.dynamic_slice(conv_all, (0, 0, me * 512), (N_DEV, CONV_K, 512)),
         lax.dynamic_slice(conv_all, (0, 0, N_DEV * 512 + me * 512), (N_DEV, CONV_K, 512))], axis=1)
    rep_contrib = small_all[:, :N_REPLICATED].reshape(N_DEV, N_REPLICATED // 128, 128)

    adam_big("w_ada", gw_ada[None])
    (r_in,) = _exchange_wait(in_x, ["a2a"], results["w_ada"][0], "grads_w_in_wait", chips=True)
    adam_big("w_in", r_in)
    packed = [jnp.concatenate([given[nm][i] for nm in REPLICATED], axis=1).reshape(N_REPLICATED // 128, 128)
              for i in range(3)]
    rep_res = _adam(rep_contrib, packed[0], packed[1], packed[2], "adam_replicated")
    pos = 0
    for nm in REPLICATED:
        size = given[nm][0].shape[1]
        results[nm] = tuple(r.reshape(1, N_REPLICATED)[:, pos:pos + size] for r in rep_res)
        pos += size
    conv_wmv = [jnp.concatenate([given["ssm_conv_w"][i][0], given["gdn_conv_w"][i][0]], axis=0) for i in range(3)]
    conv_res = _adam(conv_contrib, conv_wmv[0], conv_wmv[1], conv_wmv[2], "adam_conv_w")
    results["ssm_conv_w"] = tuple(r[None, :CONV_K] for r in conv_res)
    results["gdn_conv_w"] = tuple(r[None, CONV_K:] for r in conv_res)

    loss = lax.psum(loss_loc[0, 0], ("x", "y", "c"))
    return (loss, dx[None]) + tuple(results[nm][i] for i in range(4) for nm in WEIGHTS)
```
